```python
import math
import jax, jax.numpy as jnp
from jax import lax
import numpy as np

D_MODEL = 1024
BATCH = 8
SEQ = 4096
DEPTH = 4

N_MIXERS = 3
RMS_EPS = 1e-6

DN_HEADS = 8
DN_DK = 128
DN_DV = 256
DN_QK_W = DN_HEADS * DN_DK
DN_V_W = DN_HEADS * DN_DV
DN_CONV = 4
DN_CHUNK = 64
DN_CONV_W = 2 * DN_QK_W + DN_V_W
DN_IN = DN_CONV_W + DN_V_W + 2 * DN_HEADS

SB_HEADS = 16
SB_DH = 64
SB_W = SB_HEADS * SB_DH
SB_BLOCK = 128
SB_IN = 4 * SB_W

SC_W = 2 * D_MODEL
SC_CONV = 3
SC_IN = 4 * SC_W

N_DN = (DEPTH + 2) // 3
N_SB = (DEPTH + 1) // 3
N_SC = DEPTH // 3

kernel_name = "interleaved_deltanet_stickbreak_shortconv"


def rms_norm(x, g, eps=RMS_EPS):
    xf = x.astype(jnp.float32)
    y = xf * lax.rsqrt(jnp.mean(xf * xf, axis=-1, keepdims=True) + eps)
    return (y * g.astype(jnp.float32)).astype(x.dtype)


def l2_norm(x, eps=1e-6):
    xf = x.astype(jnp.float32)
    return xf * lax.rsqrt(jnp.sum(xf * xf, axis=-1, keepdims=True) + eps)


def causal_dwconv(x, w):
    K, C = w.shape
    return lax.conv_general_dilated(
        x, w[:, None, :].astype(x.dtype), window_strides=(1,), padding=[(K - 1, 0)],
        dimension_numbers=('NWC', 'WIO', 'NWC'), feature_group_count=C)


def gated_delta_rule(q, k, v, log_a, beta):
    f32 = jnp.float32
    Bn, T, H, dk = q.shape
    dv = v.shape[-1]
    C = DN_CHUNK
    N = T // C

    def chunks(t):
        t = t.astype(f32).reshape((Bn, N, C, H) + t.shape[3:])
        return jnp.moveaxis(t, 3, 1)

    q = chunks(q) * (dk ** -0.5)
    k = chunks(k)
    v = chunks(v)
    beta = chunks(beta)
    g = jnp.cumsum(chunks(log_a), axis=-1)
    causal = jnp.tril(jnp.ones((C, C), bool))
    strict = jnp.tril(jnp.ones((C, C), bool), -1)
    gdiff = g[..., :, None] - g[..., None, :]
    decay = jnp.where(causal, jnp.exp(jnp.where(causal, gdiff, 0.0)), 0.0)

    k_beta = k * beta[..., None]
    L = jnp.where(strict, jnp.einsum('bhncd,bhnsd->bhncs', k_beta, k) * decay, 0.0)
    eye = jnp.eye(C, dtype=f32)
    rhs = jnp.concatenate([v * beta[..., None], k_beta * jnp.exp(g)[..., None]], axis=-1)
    sol = lax.linalg.triangular_solve(L + eye, rhs, left_side=True, lower=True, unit_diagonal=True)
    u, w = sol[..., :dv], sol[..., dv:]

    intra = jnp.where(causal, jnp.einsum('bhncd,bhnsd->bhncs', q, k) * decay, 0.0)
    q_dec = q * jnp.exp(g)[..., None]
    g_last = g[..., -1]
    k_dec = k * jnp.exp(g_last[..., None] - g)[..., None]

    def step(S, xs):
        q_c, k_c, u_c, w_c, a_c, gl = xs
        v_new = u_c - jnp.einsum('bhcd,bhde->bhce', w_c, S)
        o = jnp.einsum('bhcd,bhde->bhce', q_c, S) + jnp.einsum('bhcs,bhse->bhce', a_c, v_new)
        S = S * jnp.exp(gl)[..., None, None] + jnp.einsum('bhcd,bhce->bhde', k_c, v_new)
        return S, o

    xs = tuple(jnp.moveaxis(t, 2, 0) for t in (q_dec, k_dec, u, w, intra, g_last))
    S0 = jnp.zeros((Bn, H, dk, dv), f32)
    _, o = lax.scan(step, S0, xs)
    return jnp.transpose(o, (1, 0, 3, 2, 4)).reshape(Bn, T, H, dv)


def deltanet_mixer(h, w_in, conv_w, a_log, dt_bias, o_norm_g, w_out):
    f32 = jnp.float32
    Bn, T, _ = h.shape
    proj = h @ w_in
    qkv, gate, a_in, b_in = jnp.split(proj, [DN_CONV_W, DN_CONV_W + DN_V_W, DN_CONV_W + DN_V_W + DN_HEADS], axis=-1)
    qkv = jax.nn.silu(causal_dwconv(qkv, conv_w))
    q, k, v = jnp.split(qkv, [DN_QK_W, 2 * DN_QK_W], axis=-1)
    q = l2_norm(q.reshape(Bn, T, DN_HEADS, DN_DK))
    k = l2_norm(k.reshape(Bn, T, DN_HEADS, DN_DK))
    v = v.reshape(Bn, T, DN_HEADS, DN_DV)
    beta = jax.nn.sigmoid(b_in.astype(f32))
    log_a = -jnp.exp(a_log.astype(f32)) * jax.nn.softplus(a_in.astype(f32) + dt_bias.astype(f32))
    o = gated_delta_rule(q, k, v, log_a, beta)
    o = rms_norm(o, o_norm_g) * jax.nn.silu(gate.astype(f32).reshape(Bn, T, DN_HEADS, DN_DV))
    return o.reshape(Bn, T, DN_V_W).astype(h.dtype) @ w_out


def stick_breaking_mixer(h, w_in, q_norm_g, k_norm_g, w_out):
    f32 = jnp.float32
    Bn, T, _ = h.shape
    q, k, v, gate = jnp.split(h @ w_in, 4, axis=-1)
    q = rms_norm(q.reshape(Bn, T, SB_HEADS, SB_DH), q_norm_g)
    k = rms_norm(k.reshape(Bn, T, SB_HEADS, SB_DH), k_norm_g)
    v = v.reshape(Bn, T, SB_HEADS, SB_DH)
    nb = T // SB_BLOCK
    qb = jnp.moveaxis(q.reshape(Bn, nb, SB_BLOCK, SB_HEADS, SB_DH), 1, 0)
    key_pos = jnp.arange(T)
    scale = SB_DH ** -0.5

    def block(args):
        i, q_blk = args
        q_pos = i * SB_BLOCK + jnp.arange(SB_BLOCK)
        z = jnp.einsum('bqhd,bshd->bhqs', q_blk, k, preferred_element_type=f32) * scale
        mask = key_pos[None, :] < q_pos[:, None]
        log1m = jnp.where(mask, -jax.nn.softplus(z), 0.0)
        after = lax.cumsum(log1m, axis=3, reverse=True) - log1m
        wts = jnp.where(mask, jnp.exp(jax.nn.log_sigmoid(z) + after), 0.0)
        return jnp.einsum('bhqs,bshd->bqhd', wts.astype(v.dtype), v)

    o = lax.map(block, (jnp.arange(nb), qb))
    o = jnp.moveaxis(o, 0, 1).reshape(Bn, T, SB_W)
    return (o * jax.nn.silu(gate)) @ w_out


def short_conv_mixer(h, w_in, conv_w, w_out):
    b_gate, c_gate, u, gate = jnp.split(h @ w_in, 4, axis=-1)
    y = b_gate * causal_dwconv(c_gate * u, conv_w)
    return (y * jax.nn.silu(gate)) @ w_out


def _fwd_setup_inputs(seed: int = 0) -> dict:
    key = jax.random.key(seed)
    ks = jax.random.split(key, 16)
    f32 = jnp.float32

    def dense(k, shape, fan_in):
        return jax.random.normal(k, shape, f32) * fan_in ** -0.5

    def gain(k, shape):
        return 1.0 + 0.05 * jax.random.normal(k, shape, f32)

    x = jax.random.normal(ks[0], (BATCH, SEQ, D_MODEL), f32)
    norm_g = gain(ks[1], (DEPTH, D_MODEL))
    dn_w_in = dense(ks[2], (N_DN, D_MODEL, DN_IN), D_MODEL)
    dn_conv_w = dense(ks[3], (N_DN, DN_CONV, DN_CONV_W), DN_CONV)
    dn_a_log = jnp.log(jax.random.uniform(ks[4], (N_DN, DN_HEADS), f32, 1.0, 16.0))
    dt = jnp.exp(jax.random.uniform(ks[5], (N_DN, DN_HEADS), f32, math.log(1e-3), math.log(1e-1)))
    dn_dt_bias = dt + jnp.log(-jnp.expm1(-dt))
    dn_o_norm_g = gain(ks[6], (N_DN, DN_DV))
    dn_w_out = dense(ks[7], (N_DN, DN_V_W, D_MODEL), DN_V_W)
    sb_w_in = dense(ks[8], (N_SB, D_MODEL, SB_IN), D_MODEL)
    sb_q_norm_g = gain(ks[9], (N_SB, SB_DH))
    sb_k_norm_g = gain(ks[10], (N_SB, SB_DH))
    sb_w_out = dense(ks[11], (N_SB, SB_W, D_MODEL), SB_W)
    sc_w_in = dense(ks[12], (N_SC, D_MODEL, SC_IN), D_MODEL)
    sc_conv_w = dense(ks[13], (N_SC, SC_CONV, SC_W), SC_CONV)
    sc_w_out = dense(ks[14], (N_SC, SC_W, D_MODEL), SC_W)
    return {"x": x, "norm_g": norm_g,
            "dn_w_in": dn_w_in, "dn_conv_w": dn_conv_w, "dn_a_log": dn_a_log, "dn_dt_bias": dn_dt_bias,
            "dn_o_norm_g": dn_o_norm_g, "dn_w_out": dn_w_out,
            "sb_w_in": sb_w_in, "sb_q_norm_g": sb_q_norm_g, "sb_k_norm_g": sb_k_norm_g, "sb_w_out": sb_w_out,
            "sc_w_in": sc_w_in, "sc_conv_w": sc_conv_w, "sc_w_out": sc_w_out}


def _fwd_reference(x, norm_g, dn_w_in, dn_conv_w, dn_a_log, dn_dt_bias, dn_o_norm_g, dn_w_out,
              sb_w_in, sb_q_norm_g, sb_k_norm_g, sb_w_out, sc_w_in, sc_conv_w, sc_w_out):
    for i in range(DEPTH):
        h = rms_norm(x, norm_g[i])
        j = i // N_MIXERS
        kind = i % N_MIXERS
        if kind == 0:
            y = deltanet_mixer(h, dn_w_in[j], dn_conv_w[j], dn_a_log[j], dn_dt_bias[j], dn_o_norm_g[j], dn_w_out[j])
        elif kind == 1:
            y = stick_breaking_mixer(h, sb_w_in[j], sb_q_norm_g[j], sb_k_norm_g[j], sb_w_out[j])
        else:
            y = short_conv_mixer(h, sc_w_in[j], sc_conv_w[j], sc_w_out[j])
        x = x + y
    return x


import jax as _jax
import jax.numpy as _jnp

TWIN_FORMAT = 'train_step'
FWD_PARAMS = ['x', 'norm_g', 'dn_w_in', 'dn_conv_w', 'dn_a_log', 'dn_dt_bias', 'dn_o_norm_g', 'dn_w_out', 'sb_w_in', 'sb_q_norm_g', 'sb_k_norm_g', 'sb_w_out', 'sc_w_in', 'sc_conv_w', 'sc_w_out']
TWIN_WEIGHTS = ['norm_g', 'dn_w_in', 'dn_conv_w', 'dn_a_log', 'dn_dt_bias', 'dn_o_norm_g', 'dn_w_out', 'sb_w_in', 'sb_q_norm_g', 'sb_k_norm_g', 'sb_w_out', 'sc_w_in', 'sc_conv_w', 'sc_w_out']
TWIN_DIFF_INPUT = 'x'
TWIN_INPUTS = ['x', 'norm_g', 'dn_w_in', 'dn_conv_w', 'dn_a_log', 'dn_dt_bias', 'dn_o_norm_g', 'dn_w_out', 'sb_w_in', 'sb_q_norm_g', 'sb_k_norm_g', 'sb_w_out', 'sc_w_in', 'sc_conv_w', 'sc_w_out', 'loss_target', 'm_norm_g', 'm_dn_w_in', 'm_dn_conv_w', 'm_dn_a_log', 'm_dn_dt_bias', 'm_dn_o_norm_g', 'm_dn_w_out', 'm_sb_w_in', 'm_sb_q_norm_g', 'm_sb_k_norm_g', 'm_sb_w_out', 'm_sc_w_in', 'm_sc_conv_w', 'm_sc_w_out', 'v_norm_g', 'v_dn_w_in', 'v_dn_conv_w', 'v_dn_a_log', 'v_dn_dt_bias', 'v_dn_o_norm_g', 'v_dn_w_out', 'v_sb_w_in', 'v_sb_q_norm_g', 'v_sb_k_norm_g', 'v_sb_w_out', 'v_sc_w_in', 'v_sc_conv_w', 'v_sc_w_out']
TWIN_OUTPUTS = ['loss', 'grad_x', 'grad_norm_g', 'grad_dn_w_in', 'grad_dn_conv_w', 'grad_dn_a_log', 'grad_dn_dt_bias', 'grad_dn_o_norm_g', 'grad_dn_w_out', 'grad_sb_w_in', 'grad_sb_q_norm_g', 'grad_sb_k_norm_g', 'grad_sb_w_out', 'grad_sc_w_in', 'grad_sc_conv_w', 'grad_sc_w_out', 'delta_norm_g', 'delta_dn_w_in', 'delta_dn_conv_w', 'delta_dn_a_log', 'delta_dn_dt_bias', 'delta_dn_o_norm_g', 'delta_dn_w_out', 'delta_sb_w_in', 'delta_sb_q_norm_g', 'delta_sb_k_norm_g', 'delta_sb_w_out', 'delta_sc_w_in', 'delta_sc_conv_w', 'delta_sc_w_out', 'new_m_norm_g', 'new_m_dn_w_in', 'new_m_dn_conv_w', 'new_m_dn_a_log', 'new_m_dn_dt_bias', 'new_m_dn_o_norm_g', 'new_m_dn_w_out', 'new_m_sb_w_in', 'new_m_sb_q_norm_g', 'new_m_sb_k_norm_g', 'new_m_sb_w_out', 'new_m_sc_w_in', 'new_m_sc_conv_w', 'new_m_sc_w_out', 'new_v_norm_g', 'new_v_dn_w_in', 'new_v_dn_conv_w', 'new_v_dn_a_log', 'new_v_dn_dt_bias', 'new_v_dn_o_norm_g', 'new_v_dn_w_out', 'new_v_sb_w_in', 'new_v_sb_q_norm_g', 'new_v_sb_k_norm_g', 'new_v_sb_w_out', 'new_v_sc_w_in', 'new_v_sc_conv_w', 'new_v_sc_w_out']
TWIN_LEAF_KINDS = {'loss': 'loss', 'grad_x': 'grad_x', 'grad_norm_g': 'grad_w', 'grad_dn_w_in': 'grad_w', 'grad_dn_conv_w': 'grad_w', 'grad_dn_a_log': 'grad_w', 'grad_dn_dt_bias': 'grad_w', 'grad_dn_o_norm_g': 'grad_w', 'grad_dn_w_out': 'grad_w', 'grad_sb_w_in': 'grad_w', 'grad_sb_q_norm_g': 'grad_w', 'grad_sb_k_norm_g': 'grad_w', 'grad_sb_w_out': 'grad_w', 'grad_sc_w_in': 'grad_w', 'grad_sc_conv_w': 'grad_w', 'grad_sc_w_out': 'grad_w', 'delta_norm_g': 'delta_w', 'delta_dn_w_in': 'delta_w', 'delta_dn_conv_w': 'delta_w', 'delta_dn_a_log': 'delta_w', 'delta_dn_dt_bias': 'delta_w', 'delta_dn_o_norm_g': 'delta_w', 'delta_dn_w_out': 'delta_w', 'delta_sb_w_in': 'delta_w', 'delta_sb_q_norm_g': 'delta_w', 'delta_sb_k_norm_g': 'delta_w', 'delta_sb_w_out': 'delta_w', 'delta_sc_w_in': 'delta_w', 'delta_sc_conv_w': 'delta_w', 'delta_sc_w_out': 'delta_w', 'new_m_norm_g': 'new_m', 'new_m_dn_w_in': 'new_m', 'new_m_dn_conv_w': 'new_m', 'new_m_dn_a_log': 'new_m', 'new_m_dn_dt_bias': 'new_m', 'new_m_dn_o_norm_g': 'new_m', 'new_m_dn_w_out': 'new_m', 'new_m_sb_w_in': 'new_m', 'new_m_sb_q_norm_g': 'new_m', 'new_m_sb_k_norm_g': 'new_m', 'new_m_sb_w_out': 'new_m', 'new_m_sc_w_in': 'new_m', 'new_m_sc_conv_w': 'new_m', 'new_m_sc_w_out': 'new_m', 'new_v_norm_g': 'new_v', 'new_v_dn_w_in': 'new_v', 'new_v_dn_conv_w': 'new_v', 'new_v_dn_a_log': 'new_v', 'new_v_dn_dt_bias': 'new_v', 'new_v_dn_o_norm_g': 'new_v', 'new_v_dn_w_out': 'new_v', 'new_v_sb_w_in': 'new_v', 'new_v_sb_q_norm_g': 'new_v', 'new_v_sb_k_norm_g': 'new_v', 'new_v_sb_w_out': 'new_v', 'new_v_sc_w_in': 'new_v', 'new_v_sc_conv_w': 'new_v', 'new_v_sc_w_out': 'new_v'}


def _forward(args):
    return _fwd_reference(*[args[k] for k in FWD_PARAMS])


def _output_shape():
    def fwd():
        inp = _fwd_setup_inputs(0)
        return _fwd_reference(*[inp[k] for k in FWD_PARAMS])
    out = _jax.eval_shape(fwd)
    return out.shape, out.dtype

N_MICROBATCH = 1
ADAM_LR = 0.001
ADAM_B1 = 0.9
ADAM_B2 = 0.999
ADAM_EPS = 1e-08
ADAM_WD = 0.01
ADAM_STEP = 10
PER_EXAMPLE_BATCH_AXIS = {'x': 0, 'loss_target': 0}
SHARED_INPUTS = []
_WEIGHT_DTYPES = {'norm_g': _jnp.float32, 'dn_w_in': _jnp.float32, 'dn_conv_w': _jnp.float32, 'dn_a_log': _jnp.float32, 'dn_dt_bias': _jnp.float32, 'dn_o_norm_g': _jnp.float32, 'dn_w_out': _jnp.float32, 'sb_w_in': _jnp.float32, 'sb_q_norm_g': _jnp.float32, 'sb_k_norm_g': _jnp.float32, 'sb_w_out': _jnp.float32, 'sc_w_in': _jnp.float32, 'sc_conv_w': _jnp.float32, 'sc_w_out': _jnp.float32}
MOMENT_SCALE = {'norm_g': 2.679403e+01, 'dn_w_in': 4.350131e-01, 'dn_conv_w': 6.121656e-01, 'dn_a_log': 4.327514e+01, 'dn_dt_bias': 4.137922e+01, 'dn_o_norm_g': 4.540380e+01, 'dn_w_out': 1.149451e+00, 'sb_w_in': 3.708603e-01, 'sb_q_norm_g': 1.100161e+01, 'sb_k_norm_g': 1.101602e+01, 'sb_w_out': 4.391632e-01, 'sc_w_in': 4.701312e-01, 'sc_conv_w': 3.364541e+00, 'sc_w_out': 3.779647e-01}


def _to_microbatches(a, axis):
    t = _jnp.moveaxis(a, axis, 0)
    t = t.reshape((N_MICROBATCH, t.shape[0] // N_MICROBATCH) + t.shape[1:])
    return _jnp.moveaxis(t, 1, axis + 1)


def setup_inputs(seed: int = 0) -> dict:
    inp = _fwd_setup_inputs(seed)
    key = _jax.random.fold_in(_jax.random.key(seed), 7919)
    shape, _ = _output_shape()
    out = dict(inp)
    out["loss_target"] = _jax.random.normal(_jax.random.fold_in(key, 0), shape, _jnp.float32)
    for i, name in enumerate(TWIN_WEIGHTS):
        w = inp[name].astype(_jnp.float32)
        if MOMENT_SCALE is None:
            s = _jnp.sqrt(_jnp.mean(_jnp.square(w)) + 1e-30)
        else:
            s = MOMENT_SCALE[name]
        km, kv = _jax.random.split(_jax.random.fold_in(key, i + 1))
        out[name] = w
        out["m_" + name] = s * _jax.random.normal(km, w.shape, _jnp.float32)
        out["v_" + name] = (s * s) * _jax.random.uniform(kv, w.shape, _jnp.float32, 0.5, 1.5)
    if N_MICROBATCH > 1:
        for name, axis in PER_EXAMPLE_BATCH_AXIS.items():
            out[name] = _to_microbatches(out[name], axis)
    return {'x': out['x'], 'norm_g': out['norm_g'], 'dn_w_in': out['dn_w_in'], 'dn_conv_w': out['dn_conv_w'], 'dn_a_log': out['dn_a_log'], 'dn_dt_bias': out['dn_dt_bias'], 'dn_o_norm_g': out['dn_o_norm_g'], 'dn_w_out': out['dn_w_out'], 'sb_w_in': out['sb_w_in'], 'sb_q_norm_g': out['sb_q_norm_g'], 'sb_k_norm_g': out['sb_k_norm_g'], 'sb_w_out': out['sb_w_out'], 'sc_w_in': out['sc_w_in'], 'sc_conv_w': out['sc_conv_w'], 'sc_w_out': out['sc_w_out'], 'loss_target': out['loss_target'], 'm_norm_g': out['m_norm_g'], 'm_dn_w_in': out['m_dn_w_in'], 'm_dn_conv_w': out['m_dn_conv_w'], 'm_dn_a_log': out['m_dn_a_log'], 'm_dn_dt_bias': out['m_dn_dt_bias'], 'm_dn_o_norm_g': out['m_dn_o_norm_g'], 'm_dn_w_out': out['m_dn_w_out'], 'm_sb_w_in': out['m_sb_w_in'], 'm_sb_q_norm_g': out['m_sb_q_norm_g'], 'm_sb_k_norm_g': out['m_sb_k_norm_g'], 'm_sb_w_out': out['m_sb_w_out'], 'm_sc_w_in': out['m_sc_w_in'], 'm_sc_conv_w': out['m_sc_conv_w'], 'm_sc_w_out': out['m_sc_w_out'], 'v_norm_g': out['v_norm_g'], 'v_dn_w_in': out['v_dn_w_in'], 'v_dn_conv_w': out['v_dn_conv_w'], 'v_dn_a_log': out['v_dn_a_log'], 'v_dn_dt_bias': out['v_dn_dt_bias'], 'v_dn_o_norm_g': out['v_dn_o_norm_g'], 'v_dn_w_out': out['v_dn_w_out'], 'v_sb_w_in': out['v_sb_w_in'], 'v_sb_q_norm_g': out['v_sb_q_norm_g'], 'v_sb_k_norm_g': out['v_sb_k_norm_g'], 'v_sb_w_out': out['v_sb_w_out'], 'v_sc_w_in': out['v_sc_w_in'], 'v_sc_conv_w': out['v_sc_conv_w'], 'v_sc_w_out': out['v_sc_w_out']}


def _loss(weights, diff, rest, loss_target):
    with _jax.named_scope("forward"):
        args = {**rest, TWIN_DIFF_INPUT: diff, **{k: w.astype(_WEIGHT_DTYPES[k]) for k, w in weights.items()}}
        y = _forward(args)
    with _jax.named_scope("loss_head"):
        err = _jnp.square(y.astype(_jnp.float32) - loss_target)
        return 0.5 * _jnp.sum(_jnp.mean(err, axis=-1)) if err.ndim else 0.5 * err


def _adamw(w, g, m, v):
    m = ADAM_B1 * m + (1.0 - ADAM_B1) * g
    v = ADAM_B2 * v + (1.0 - ADAM_B2) * _jnp.square(g)
    m_hat = m / (1.0 - ADAM_B1 ** ADAM_STEP)
    v_hat = v / (1.0 - ADAM_B2 ** ADAM_STEP)
    delta = -ADAM_LR * (m_hat / (_jnp.sqrt(v_hat) + ADAM_EPS) + ADAM_WD * w)
    return delta, m, v


def reference(x, norm_g, dn_w_in, dn_conv_w, dn_a_log, dn_dt_bias, dn_o_norm_g, dn_w_out, sb_w_in, sb_q_norm_g, sb_k_norm_g, sb_w_out, sc_w_in, sc_conv_w, sc_w_out, loss_target, m_norm_g, m_dn_w_in, m_dn_conv_w, m_dn_a_log, m_dn_dt_bias, m_dn_o_norm_g, m_dn_w_out, m_sb_w_in, m_sb_q_norm_g, m_sb_k_norm_g, m_sb_w_out, m_sc_w_in, m_sc_conv_w, m_sc_w_out, v_norm_g, v_dn_w_in, v_dn_conv_w, v_dn_a_log, v_dn_dt_bias, v_dn_o_norm_g, v_dn_w_out, v_sb_w_in, v_sb_q_norm_g, v_sb_k_norm_g, v_sb_w_out, v_sc_w_in, v_sc_conv_w, v_sc_w_out):
    given = dict(x=x, norm_g=norm_g, dn_w_in=dn_w_in, dn_conv_w=dn_conv_w, dn_a_log=dn_a_log, dn_dt_bias=dn_dt_bias, dn_o_norm_g=dn_o_norm_g, dn_w_out=dn_w_out, sb_w_in=sb_w_in, sb_q_norm_g=sb_q_norm_g, sb_k_norm_g=sb_k_norm_g, sb_w_out=sb_w_out, sc_w_in=sc_w_in, sc_conv_w=sc_conv_w, sc_w_out=sc_w_out, loss_target=loss_target, m_norm_g=m_norm_g, m_dn_w_in=m_dn_w_in, m_dn_conv_w=m_dn_conv_w, m_dn_a_log=m_dn_a_log, m_dn_dt_bias=m_dn_dt_bias, m_dn_o_norm_g=m_dn_o_norm_g, m_dn_w_out=m_dn_w_out, m_sb_w_in=m_sb_w_in, m_sb_q_norm_g=m_sb_q_norm_g, m_sb_k_norm_g=m_sb_k_norm_g, m_sb_w_out=m_sb_w_out, m_sc_w_in=m_sc_w_in, m_sc_conv_w=m_sc_conv_w, m_sc_w_out=m_sc_w_out, v_norm_g=v_norm_g, v_dn_w_in=v_dn_w_in, v_dn_conv_w=v_dn_conv_w, v_dn_a_log=v_dn_a_log, v_dn_dt_bias=v_dn_dt_bias, v_dn_o_norm_g=v_dn_o_norm_g, v_dn_w_out=v_dn_w_out, v_sb_w_in=v_sb_w_in, v_sb_q_norm_g=v_sb_q_norm_g, v_sb_k_norm_g=v_sb_k_norm_g, v_sb_w_out=v_sb_w_out, v_sc_w_in=v_sc_w_in, v_sc_conv_w=v_sc_conv_w, v_sc_w_out=v_sc_w_out)
    weights = {n: given[n] for n in TWIN_WEIGHTS}
    shared = {n: given[n] for n in SHARED_INPUTS}
    per_example = {n: given[n] for n in ['x']}
    grad_fn = _jax.value_and_grad(_loss, argnums=(0, 1))

    def one_microbatch(ex, loss_target):
        ex = dict(ex)
        diff = ex.pop(TWIN_DIFF_INPUT)
        return grad_fn(weights, diff, {**shared, **ex}, loss_target)

    if N_MICROBATCH == 1:
        loss, (grad_w, grad_x) = one_microbatch(per_example, given["loss_target"])
    else:
        def body(carry, xs):
            loss_sum, grad_sum = carry
            l_k, (gw_k, gx_k) = one_microbatch(xs[0], xs[1])
            with _jax.named_scope("update"):
                return (loss_sum + l_k, _jax.tree.map(_jnp.add, grad_sum, gw_k)), gx_k

        init = (_jnp.zeros((), _jnp.float32), _jax.tree.map(_jnp.zeros_like, weights))
        (loss, grad_w), grad_x = _jax.lax.scan(body, init, (per_example, given["loss_target"]))
    with _jax.named_scope("update"):
        delta_w, new_m, new_v = {}, {}, {}
        for n in TWIN_WEIGHTS:
            delta_w[n], new_m[n], new_v[n] = _adamw(weights[n], grad_w[n], given["m_" + n], given["v_" + n])
    return (loss, grad_x, *[grad_w[n] for n in TWIN_WEIGHTS], *[delta_w[n] for n in TWIN_WEIGHTS],
            *[new_m[n] for n in TWIN_WEIGHTS], *[new_v[n] for n in TWIN_WEIGHTS])
```

```python
import functools
import math

import jax
import jax.numpy as jnp
from jax import lax
from jax.experimental import pallas as pl
from jax.experimental.pallas import tpu as pltpu

F32 = jnp.float32
BF16 = jnp.bfloat16
MESH = pl.DeviceIdType.MESH

RMS_EPS = 1e-6
L2_EPS = 1e-6
LANES = 128
VMEM_BIG = 60 * 1024 * 1024

DN_HEADS, DN_DK, DN_DV, DN_CHUNK, DN_CONV = 8, 128, 256, 64, 4
DN_QK_W = DN_HEADS * DN_DK
DN_V_W = DN_HEADS * DN_DV
DN_CONV_W = 2 * DN_QK_W + DN_V_W
DN_IN = DN_CONV_W + DN_V_W + 2 * DN_HEADS
DN_IN_PAD = DN_CONV_W + DN_V_W + LANES
SB_DH = 64
SC_CONV = 3

ADAM_LR, ADAM_B1, ADAM_B2, ADAM_EPS, ADAM_WD, ADAM_STEP = 0.001, 0.9, 0.999, 1e-08, 0.01, 10


def _pick(n, cands):
    for c in cands:
        if n % c == 0:
            return c
    raise ValueError(f"no tile for {n} in {cands}")


def _bf(x):
    return x.astype(BF16)


def _dot(a, b):
    return jnp.dot(_bf(a), _bf(b), preferred_element_type=F32)


def _dot_nt(a, b):
    return lax.dot_general(_bf(a), _bf(b), (((1,), (1,)), ((), ())), preferred_element_type=F32)


def _dot_tn(a, b):
    return lax.dot_general(_bf(a), _bf(b), (((0,), (0,)), ((), ())), preferred_element_type=F32)


def _split3(a):
    hi = _bf(a)
    r = a - hi.astype(F32)
    mid = _bf(r)
    lo = _bf(r - mid.astype(F32))
    return hi, mid, lo


def _dot_x(a, b_exact_bf16, dims=(((1,), (0,)), ((), ()))):
    hi, mid, lo = _split3(a)
    f = lambda p: lax.dot_general(p, b_exact_bf16, dims, preferred_element_type=F32)
    return f(hi) + f(mid) + f(lo)


def _dot_hp(a, b, dims=(((1,), (0,)), ((), ()))):
    ah, am, _ = _split3(a)
    bh, bm, _ = _split3(b)
    f = lambda p, q: lax.dot_general(p, q, dims, preferred_element_type=F32)
    return f(ah, bh) + (f(ah, bm) + f(am, bh))


def _sigmoid(x):
    return 1.0 / (1.0 + jnp.exp(-x))


def _silu(x):
    return x * _sigmoid(x)


def _dsilu(x):
    s = _sigmoid(x)
    return s * (1.0 + x * (1.0 - s))


def _softplus(x):
    return jnp.maximum(x, 0.0) + jnp.log(1.0 + jnp.exp(-jnp.abs(x)))


def _shift_down(z, k):
    if k == 0:
        return z
    row = lax.broadcasted_iota(jnp.int32, z.shape, 0)
    return jnp.where(row >= k, pltpu.roll(z, k, 0), 0.0)


def _shift_up(z, k):
    if k == 0:
        return z
    n = z.shape[0]
    row = lax.broadcasted_iota(jnp.int32, z.shape, 0)
    return jnp.where(row < n - k, pltpu.roll(z, n - k, 0), 0.0)


def _matmul(a, b, *, mode, name, res=None, a_parts=1, b_parts=1, out_parts=1, out_dtype=F32):
    def dims2(x, parts):
        if parts == 1:
            return x.shape
        assert x.shape[0] == parts
        return (x.shape[1], x.shape[2] * parts)

    ash, bsh = dims2(a, a_parts), dims2(b, b_parts)
    if mode == "nn":
        (M, K), (K2, N) = ash, bsh
        dn = (((1,), (0,)), ((), ()))
    elif mode == "nt":
        (M, K), (N, K2) = ash, bsh
        dn = (((1,), (1,)), ((), ()))
    else:
        (K, M), (K2, N) = ash, bsh
        dn = (((0,), (0,)), ((), ()))
    assert K == K2, (ash, bsh, mode)
    tm = _pick(M, (512, 256, 128, 64, 32, 16, 8))
    n_unit = N // out_parts if mode == "nn" else (N // b_parts if mode == "tn" else N)
    tn = _pick(n_unit, (1024, 896, 768, 512, 384, 256, 128))
    k_unit = K // a_parts if mode in ("nn", "nt") else K
    tk = _pick(k_unit, (512, 896, 256, 128))
    nk = K // tk
    grid = (M // tm, N // tn, nk)

    def spec(parts, rows_are, cols_are, tr, tc, width):
        per = width // parts // tc
        if parts == 1:
            return pl.BlockSpec((tr, tc), lambda i, j, k: ((i, j, k)[rows_are], (i, j, k)[cols_are]))
        return pl.BlockSpec((None, tr, tc), lambda i, j, k: ((i, j, k)[cols_are] // per, (i, j, k)[rows_are],
                                                             (i, j, k)[cols_are] % per))

    if mode == "nn":
        a_spec = spec(a_parts, 0, 2, tm, tk, K)
        b_spec = spec(b_parts, 2, 1, tk, tn, N)
    elif mode == "nt":
        a_spec = spec(a_parts, 0, 2, tm, tk, K)
        b_spec = spec(b_parts, 1, 2, tn, tk, K)
    else:
        a_spec = spec(a_parts, 2, 0, tk, tm, M)
        b_spec = spec(b_parts, 2, 1, tk, tn, N)
    o_spec = spec(out_parts, 0, 1, tm, tn, N)
    in_specs = [a_spec, b_spec]
    operands = [a, b]
    if res is not None:
        in_specs.append(pl.BlockSpec((tm, tn), lambda i, j, k: (i, j)))
        operands.append(res)

    def body(*refs):
        a_ref, b_ref = refs[0], refs[1]
        o_ref, acc_ref = refs[-2], refs[-1]
        k = pl.program_id(2)

        @pl.when(k == 0)
        def _():
            acc_ref[...] = jnp.zeros_like(acc_ref)

        acc_ref[...] += lax.dot_general(_bf(a_ref[...]), _bf(b_ref[...]), dn, preferred_element_type=F32)

        @pl.when(k == nk - 1)
        def _():
            r = acc_ref[...]
            if res is not None:
                r = refs[2][...] + r
            o_ref[...] = r.astype(o_ref.dtype)

    out_shape = (M, N) if out_parts == 1 else (out_parts, M, N // out_parts)
    return pl.pallas_call(
        body, name=name, grid=grid, in_specs=in_specs, out_specs=o_spec,
        out_shape=jax.ShapeDtypeStruct(out_shape, out_dtype),
        scratch_shapes=[pltpu.VMEM((tm, tn), F32)],
        compiler_params=pltpu.CompilerParams(dimension_semantics=("parallel", "parallel", "arbitrary")),
    )(*operands)


def _rmsnorm_fwd(x, g, *, name):
    T, D = x.shape
    tm = _pick(T, (512, 256, 128, 64, 32, 16))

    def body(x_ref, g_ref, h_ref):
        xv = x_ref[...]
        r = lax.rsqrt(jnp.mean(xv * xv, axis=-1, keepdims=True) + RMS_EPS)
        h_ref[...] = ((xv * r) * g_ref[...]).astype(BF16)

    return pl.pallas_call(
        body, name=name, grid=(T // tm,),
        in_specs=[pl.BlockSpec((tm, D), lambda i: (i, 0)), pl.BlockSpec((1, D), lambda i: (0, 0))],
        out_specs=pl.BlockSpec((tm, D), lambda i: (i, 0)),
        out_shape=jax.ShapeDtypeStruct((T, D), BF16),
    )(x, g.reshape(1, D))


def _rmsnorm_bwd(x, g, dh, dx_in, *, name):
    T, D = x.shape
    tm = _pick(T, (512, 256, 128, 64, 32, 16))

    def body(x_ref, g_ref, dh_ref, dxin_ref, dx_ref, dg_ref):
        @pl.when(pl.program_id(0) == 0)
        def _():
            dg_ref[...] = jnp.zeros_like(dg_ref)

        xv = x_ref[...]
        r = lax.rsqrt(jnp.mean(xv * xv, axis=-1, keepdims=True) + RMS_EPS)
        xh = xv * r
        dh_v = dh_ref[...]
        dxh = dh_v * g_ref[...]
        dx_ref[...] = dxin_ref[...] + r * (dxh - xh * jnp.mean(dxh * xh, axis=-1, keepdims=True))
        dg_ref[...] += jnp.sum(dh_v * xh, axis=0, keepdims=True)

    row = pl.BlockSpec((tm, D), lambda i: (i, 0))
    vec = pl.BlockSpec((1, D), lambda i: (0, 0))
    return pl.pallas_call(
        body, name=name, grid=(T // tm,),
        in_specs=[row, vec, row, row], out_specs=[row, vec],
        out_shape=[jax.ShapeDtypeStruct((T, D), F32), jax.ShapeDtypeStruct((1, D), F32)],
        compiler_params=pltpu.CompilerParams(dimension_semantics=("arbitrary",)),
    )(x, g.reshape(1, D), dh, dx_in)


def _loss_head(y, target, *, name):
    T, D = y.shape
    tm = _pick(T, (512, 256, 128, 64, 32, 16))

    def body(y_ref, t_ref, dy_ref, l_ref):
        @pl.when(pl.program_id(0) == 0)
        def _():
            l_ref[...] = jnp.zeros_like(l_ref)

        err = y_ref[...] - t_ref[...]
        dy_ref[...] = err * (1.0 / D)
        l_ref[...] += 0.5 * jnp.sum(jnp.mean(err * err, axis=-1, keepdims=True), axis=0, keepdims=True)

    row = pl.BlockSpec((tm, D), lambda i: (i, 0))
    return pl.pallas_call(
        body, name=name, grid=(T // tm,),
        in_specs=[row, row], out_specs=[row, pl.BlockSpec((1, 1), lambda i: (0, 0))],
        out_shape=[jax.ShapeDtypeStruct((T, D), F32), jax.ShapeDtypeStruct((1, 1), F32)],
        compiler_params=pltpu.CompilerParams(dimension_semantics=("arbitrary",)),
    )(y, target)


def _sc_mid_fwd(p3, conv_w, *, name):
    _, T, W = p3.shape
    K = conv_w.shape[0]
    cw = LANES

    def body(p_ref, w_ref, o_ref):
        z = p_ref[1] * p_ref[2]
        cv = sum(w_ref[i:i + 1, :] * _shift_down(z, K - 1 - i) for i in range(K))
        o_ref[...] = ((p_ref[0] * cv) * _silu(p_ref[3])).astype(BF16)

    return pl.pallas_call(
        body, name=name, grid=(W // cw,),
        in_specs=[pl.BlockSpec((4, T, cw), lambda j: (0, 0, j)), pl.BlockSpec((K, cw), lambda j: (0, j))],
        out_specs=pl.BlockSpec((T, cw), lambda j: (0, j)),
        out_shape=jax.ShapeDtypeStruct((T, W), BF16),
        compiler_params=pltpu.CompilerParams(dimension_semantics=("parallel",), vmem_limit_bytes=VMEM_BIG),
    )(p3, conv_w)


def _sc_mid_bwd(p3, conv_w, do, *, name):
    _, T, W = p3.shape
    K = conv_w.shape[0]
    cw = LANES

    def body(p_ref, w_ref, do_ref, dp_ref, dw_ref):
        b, c, u, gate = p_ref[0], p_ref[1], p_ref[2], p_ref[3]
        z = c * u
        zs = [_shift_down(z, K - 1 - i) for i in range(K)]
        cv = sum(w_ref[i:i + 1, :] * zs[i] for i in range(K))
        y = b * cv
        dov = do_ref[...]
        dy = dov * _silu(gate)
        dp_ref[3] = dov * y * _dsilu(gate)
        dp_ref[0] = dy * cv
        dcv = dy * b
        dz = sum(w_ref[i:i + 1, :] * _shift_up(dcv, K - 1 - i) for i in range(K))
        dp_ref[1] = dz * u
        dp_ref[2] = dz * c
        for i in range(K):
            dw_ref[i:i + 1, :] = jnp.sum(dcv * zs[i], axis=0, keepdims=True)

    return pl.pallas_call(
        body, name=name, grid=(W // cw,),
        in_specs=[pl.BlockSpec((4, T, cw), lambda j: (0, 0, j)), pl.BlockSpec((K, cw), lambda j: (0, j)),
                  pl.BlockSpec((T, cw), lambda j: (0, j))],
        out_specs=[pl.BlockSpec((4, T, cw), lambda j: (0, 0, j)), pl.BlockSpec((K, cw), lambda j: (0, j))],
        out_shape=[jax.ShapeDtypeStruct((4, T, W), F32), jax.ShapeDtypeStruct((K, W), F32)],
        compiler_params=pltpu.CompilerParams(dimension_semantics=("parallel",), vmem_limit_bytes=VMEM_BIG),
    )(p3, conv_w, do)


def _sc_layer_fwd(x, ng, w_in, conv_w, w_out, tag):
    h = _rmsnorm_fwd(x, ng, name=f"{tag}_norm")
    p3 = _matmul(h, w_in, mode="nn", out_parts=4, name=f"{tag}_inproj")
    og = _sc_mid_fwd(p3, conv_w, name=f"{tag}_mid")
    x_new = _matmul(og, w_out, mode="nn", res=x, name=f"{tag}_outproj")
    return x_new, (h, p3, og)


def _sc_layer_bwd(dx, x, ng, w_in, conv_w, w_out, saved, tag):
    h, p3, og = saved
    d_wout = _matmul(og, dx, mode="tn", name=f"{tag}_dwout")
    dog = _matmul(dx, w_out, mode="nt", name=f"{tag}_dog")
    dp3, dconv = _sc_mid_bwd(p3, conv_w, dog, name=f"{tag}_midbwd")
    d_win = _matmul(h, dp3, mode="tn", b_parts=4, name=f"{tag}_dwin")
    dh = _matmul(dp3, w_in, mode="nt", a_parts=4, name=f"{tag}_dh")
    dx_prev, dng = _rmsnorm_bwd(x, ng, dh, dx, name=f"{tag}_normbwd")
    return dx_prev, dng, d_win, dconv, d_wout


SB_BLK = 128
SB_ROWS = 512


def _sb_half_mask():
    return lax.broadcasted_iota(jnp.int32, (1, LANES), 1) < SB_DH


def _sb_headnorm(x, g, lo):
    x2 = x * x
    s_lo = jnp.sum(jnp.where(lo, x2, 0.0), axis=-1, keepdims=True)
    s_hi = jnp.sum(jnp.where(lo, 0.0, x2), axis=-1, keepdims=True)
    r = lax.rsqrt(jnp.where(lo, s_lo, s_hi) * (1.0 / SB_DH) + RMS_EPS)
    xh = x * r
    return xh * g, xh, r


def _sb_tile(qm, kb, valid, scale):
    z = lax.dot_general(qm, kb, (((1,), (1,)), ((), ())), preferred_element_type=F32) * scale
    sp = _softplus(z)
    return z - sp, jnp.where(valid, -sp, 0.0)


def _sb_attn_fwd(p3, gq2, gk2, *, name):
    _, T, W = p3.shape
    blk = min(SB_BLK, T)
    rows = min(SB_ROWS, T)
    nq = T // blk
    scale = SB_DH ** -0.5

    def body(p_ref, gq_ref, gk_ref, og_ref, o_ref, ls_ref, qn_ref, kn_ref, v_ref):
        lo = _sb_half_mask()

        def prologue(i, c):
            r0 = pl.multiple_of(i * rows, rows)
            sl = pl.ds(r0, rows)
            qn_ref[sl, :] = _sb_headnorm(p_ref[0, sl, :], gq_ref[...], lo)[0].astype(BF16)
            kn_ref[sl, :] = _sb_headnorm(p_ref[1, sl, :], gk_ref[...], lo)[0].astype(BF16)
            v_ref[sl, :] = p_ref[2, sl, :].astype(BF16)
            return c

        lax.fori_loop(0, T // rows, prologue, 0)

        row = lax.broadcasted_iota(jnp.int32, (blk, blk), 0)
        col = lax.broadcasted_iota(jnp.int32, (blk, blk), 1)
        strict = col < row
        tri = (row > col).astype(BF16)

        def qblock(qi, c):
            q0 = pl.multiple_of(qi * blk, blk)
            qb = qn_ref[pl.ds(q0, blk), :]
            outs, tots = [], []
            for e in range(2):
                hm = lo if e == 0 else jnp.logical_not(lo)
                qm = jnp.where(hm, qb, jnp.zeros_like(qb))

                def kblock(t, carry, qm=qm):
                    o_acc, a_carry = carry
                    kj = qi - t
                    k0 = pl.multiple_of(kj * blk, blk)
                    valid = jnp.logical_or(kj < qi, strict)
                    logsig, log1m = _sb_tile(qm, kn_ref[pl.ds(k0, blk), :], valid, scale)
                    wts = jnp.where(valid, jnp.exp(logsig + (_dot_x(log1m, tri) + a_carry)), 0.0)
                    o_acc = o_acc + jnp.dot(_bf(wts), v_ref[pl.ds(k0, blk), :], preferred_element_type=F32)
                    return o_acc, a_carry + jnp.sum(log1m, axis=-1, keepdims=True)

                o_acc, tot = lax.fori_loop(0, qi + 1, kblock,
                                           (jnp.zeros((blk, LANES), F32), jnp.zeros((blk, 1), F32)))
                outs.append(o_acc)
                tots.append(tot)
            o = jnp.where(lo, outs[0], outs[1])
            o_ref[pl.ds(q0, blk), :] = o
            ls_ref[pl.ds(q0, blk), :] = jnp.where(lo, tots[0], tots[1])
            og_ref[pl.ds(q0, blk), :] = (o * _silu(p_ref[3, pl.ds(q0, blk), :])).astype(BF16)
            return c

        lax.fori_loop(0, nq, qblock, 0)

    colblk = pl.BlockSpec((T, LANES), lambda j: (0, j))
    vec = pl.BlockSpec((1, LANES), lambda j: (0, 0))
    return pl.pallas_call(
        body, name=name, grid=(W // LANES,),
        in_specs=[pl.BlockSpec((4, T, LANES), lambda j: (0, 0, j)), vec, vec],
        out_specs=[colblk, colblk, colblk],
        out_shape=[jax.ShapeDtypeStruct((T, W), BF16), jax.ShapeDtypeStruct((T, W), F32),
                   jax.ShapeDtypeStruct((T, W), F32)],
        scratch_shapes=[pltpu.VMEM((T, LANES), BF16)] * 3,
        compiler_params=pltpu.CompilerParams(dimension_semantics=("parallel",), vmem_limit_bytes=VMEM_BIG),
    )(p3, gq2, gk2)


def _sb_attn_bwd(p3, gq2, gk2, o, lsum, dog, *, name):
    _, T, W = p3.shape
    blk = min(SB_BLK, T)
    rows = min(SB_ROWS, T)
    nq = T // blk
    scale = SB_DH ** -0.5

    def body(p_ref, gq_ref, gk_ref, o_ref, ls_ref, dog_ref, dp_ref, dgq_ref, dgk_ref,
             qn_ref, kn_ref, v_ref, do_ref):
        lo = _sb_half_mask()

        def prologue(i, c):
            r0 = pl.multiple_of(i * rows, rows)
            sl = pl.ds(r0, rows)
            qn_ref[sl, :] = _sb_headnorm(p_ref[0, sl, :], gq_ref[...], lo)[0].astype(BF16)
            kn_ref[sl, :] = _sb_headnorm(p_ref[1, sl, :], gk_ref[...], lo)[0].astype(BF16)
            v_ref[sl, :] = p_ref[2, sl, :].astype(BF16)
            gate = p_ref[3, sl, :]
            dogv = dog_ref[sl, :]
            dp_ref[3, sl, :] = dogv * o_ref[sl, :] * _dsilu(gate)
            do_ref[sl, :] = (dogv * _silu(gate)).astype(BF16)
            zero = jnp.zeros((rows, LANES), F32)
            dp_ref[0, sl, :] = zero
            dp_ref[1, sl, :] = zero
            dp_ref[2, sl, :] = zero
            return c

        lax.fori_loop(0, T // rows, prologue, 0)

        row = lax.broadcasted_iota(jnp.int32, (blk, blk), 0)
        col = lax.broadcasted_iota(jnp.int32, (blk, blk), 1)
        strict = col < row
        upto = (row <= col).astype(BF16)
        before_m = (row < col).astype(BF16)

        def qblock(qi, c):
            q0 = pl.multiple_of(qi * blk, blk)
            qb = qn_ref[pl.ds(q0, blk), :]
            dob = do_ref[pl.ds(q0, blk), :]
            lsb = ls_ref[pl.ds(q0, blk), :]
            dqs = []
            for e in range(2):
                hm = lo if e == 0 else jnp.logical_not(lo)
                qm = jnp.where(hm, qb, jnp.zeros_like(qb))
                dom = jnp.where(hm, dob, jnp.zeros_like(dob))
                total = lsb[:, e * SB_DH:e * SB_DH + 1]

                def kblock(kj, carry, qm=qm, dom=dom, total=total):
                    dq_acc, a_pre, r_pre = carry
                    k0 = pl.multiple_of(kj * blk, blk)
                    ks = pl.ds(k0, blk)
                    valid = jnp.logical_or(kj < qi, strict)
                    kb = kn_ref[ks, :]
                    vb = v_ref[ks, :]
                    logsig, log1m = _sb_tile(qm, kb, valid, scale)
                    after = (total - a_pre) - _dot_x(log1m, upto)
                    wts = jnp.where(valid, jnp.exp(logsig + after), 0.0)
                    dw = lax.dot_general(dom, vb, (((1,), (1,)), ((), ())), preferred_element_type=F32)
                    ee = dw * wts
                    before = r_pre + _dot_x(ee, before_m)
                    beta = jnp.exp(logsig)
                    dz = jnp.where(valid, ee * (1.0 - beta) - beta * before, 0.0)
                    dzb = _bf(dz * scale)
                    dq_acc = dq_acc + jnp.dot(dzb, kb, preferred_element_type=F32)
                    dp_ref[1, ks, :] += lax.dot_general(dzb, qm, (((0,), (0,)), ((), ())),
                                                        preferred_element_type=F32)
                    dp_ref[2, ks, :] += lax.dot_general(_bf(wts), dom, (((0,), (0,)), ((), ())),
                                                        preferred_element_type=F32)
                    return (dq_acc, a_pre + jnp.sum(log1m, axis=-1, keepdims=True),
                            r_pre + jnp.sum(ee, axis=-1, keepdims=True))

                dq_acc, _, _ = lax.fori_loop(
                    0, qi + 1, kblock,
                    (jnp.zeros((blk, LANES), F32), jnp.zeros((blk, 1), F32), jnp.zeros((blk, 1), F32)))
                dqs.append(dq_acc)
            dp_ref[0, pl.ds(q0, blk), :] = jnp.where(lo, dqs[0], dqs[1])
            return c

        lax.fori_loop(0, nq, qblock, 0)

        dgq_ref[...] = jnp.zeros_like(dgq_ref)
        dgk_ref[...] = jnp.zeros_like(dgk_ref)

        def epilogue(i, c):
            r0 = pl.multiple_of(i * rows, rows)
            sl = pl.ds(r0, rows)
            for part, g_ref, dg_ref in ((0, gq_ref, dgq_ref), (1, gk_ref, dgk_ref)):
                _, xh, r = _sb_headnorm(p_ref[part, sl, :], g_ref[...], lo)
                dn = dp_ref[part, sl, :]
                dxh = dn * g_ref[...]
                prod = dxh * xh
                m_lo = jnp.sum(jnp.where(lo, prod, 0.0), axis=-1, keepdims=True)
                m_hi = jnp.sum(jnp.where(lo, 0.0, prod), axis=-1, keepdims=True)
                m = jnp.where(lo, m_lo, m_hi) * (1.0 / SB_DH)
                dp_ref[part, sl, :] = r * (dxh - xh * m)
                dg_ref[...] += jnp.sum(dn * xh, axis=0, keepdims=True)
            return c

        lax.fori_loop(0, T // rows, epilogue, 0)

    colblk = pl.BlockSpec((T, LANES), lambda j: (0, j))
    vec = pl.BlockSpec((1, LANES), lambda j: (0, 0))
    part = pl.BlockSpec((4, T, LANES), lambda j: (0, 0, j))
    gvec = pl.BlockSpec((None, 1, LANES), lambda j: (j, 0, 0))
    npair = W // LANES
    return pl.pallas_call(
        body, name=name, grid=(npair,),
        in_specs=[part, vec, vec, colblk, colblk, colblk],
        out_specs=[part, gvec, gvec],
        out_shape=[jax.ShapeDtypeStruct((4, T, W), F32), jax.ShapeDtypeStruct((npair, 1, LANES), F32),
                   jax.ShapeDtypeStruct((npair, 1, LANES), F32)],
        scratch_shapes=[pltpu.VMEM((T, LANES), BF16)] * 4,
        compiler_params=pltpu.CompilerParams(dimension_semantics=("parallel",), vmem_limit_bytes=VMEM_BIG),
    )(p3, gq2, gk2, o, lsum, dog)


_NN = (((1,), (0,)), ((), ()))
_NT = (((1,), (1,)), ((), ()))
_TN = (((0,), (0,)), ((), ()))
DN_TB = 512
DN_AB_COL = (DN_CONV_W + DN_V_W) // LANES


def _dn_conv(x, w_ref):
    k = w_ref.shape[0]
    return sum(w_ref[i:i + 1, :] * _shift_down(x, k - 1 - i) for i in range(k))


def _dn_prep_fwd(p, conv_w, *, name):
    T = p.shape[0]
    cw = conv_w.shape[1]
    n_qk = 2 * DN_QK_W // LANES

    def body(p_ref, w_ref, o_ref):
        s = _silu(_dn_conv(p_ref[...], w_ref))
        r = lax.rsqrt(jnp.sum(s * s, axis=-1, keepdims=True) + L2_EPS)
        o_ref[...] = jnp.where(pl.program_id(0) < n_qk, s * r, s)

    colblk = pl.BlockSpec((T, LANES), lambda j: (0, j))
    return pl.pallas_call(
        body, name=name, grid=(cw // LANES,),
        in_specs=[colblk, pl.BlockSpec((DN_CONV, LANES), lambda j: (0, j))],
        out_specs=colblk, out_shape=jax.ShapeDtypeStruct((T, cw), F32),
        compiler_params=pltpu.CompilerParams(dimension_semantics=("parallel",), vmem_limit_bytes=VMEM_BIG),
    )(p, conv_w)


def _dn_chunk_tri(rows, upper):
    r = lax.broadcasted_iota(jnp.int32, (rows, rows), 0)
    c = lax.broadcasted_iota(jnp.int32, (rows, rows), 1)
    same = (r // DN_CHUNK) == (c // DN_CHUNK)
    return jnp.logical_and(same, (c >= r) if upper else (c <= r)).astype(BF16)


def _dn_lane_rows(a_log, dt_bias):
    pad = lambda v: jnp.zeros((1, LANES), F32).at[0, :DN_HEADS].set(v)
    return pad(a_log), pad(dt_bias)


def _dn_ab_parts(blk, alog_row, dtb_row):
    lane = lax.broadcasted_iota(jnp.int32, (1, LANES), 1)
    is_a = lane < DN_HEADS
    is_b = jnp.logical_and(lane >= DN_HEADS, lane < 2 * DN_HEADS)
    a_arg = jnp.where(is_a, blk + dtb_row, 0.0)
    neg_exp = jnp.where(is_a, -jnp.exp(alog_row), 0.0)
    log_a = neg_exp * _softplus(a_arg)
    beta = jnp.where(is_b, _sigmoid(blk), 0.0)
    return is_a, is_b, a_arg, neg_exp, log_a, beta


def _dn_ab_fwd(p, alog_row, dtb_row, *, name):
    T = p.shape[0]
    rows = min(DN_TB, T)

    def body(p_ref, al_ref, dt_ref, o_ref):
        _, _, _, _, log_a, beta = _dn_ab_parts(p_ref[...], al_ref[...], dt_ref[...])
        hi, mid, lo_ = _split3(log_a)
        tri = _dn_chunk_tri(rows, upper=False)
        f = lambda q: jnp.dot(tri, q, preferred_element_type=F32)
        o_ref[...] = (f(hi) + f(mid) + f(lo_)) + beta

    blk = pl.BlockSpec((rows, LANES), lambda i: (i, DN_AB_COL))
    vec = pl.BlockSpec((1, LANES), lambda i: (0, 0))
    return pl.pallas_call(
        body, name=name, grid=(T // rows,), in_specs=[blk, vec, vec],
        out_specs=pl.BlockSpec((rows, LANES), lambda i: (i, 0)),
        out_shape=jax.ShapeDtypeStruct((T, LANES), F32),
        compiler_params=pltpu.CompilerParams(dimension_semantics=("parallel",)),
    )(p, alog_row, dtb_row)


def _hp(a, b, dims=_NN):
    return _dot_hp(a, b, dims)


def _dn_tri_inv(low):
    c = low.shape[0]
    eye = (lax.broadcasted_iota(jnp.int32, (c, c), 0) == lax.broadcasted_iota(jnp.int32, (c, c), 1)).astype(F32)
    pw = -low
    inv = eye + pw
    for _ in range(int(math.log2(c)) - 1):
        pw = _hp(pw, pw)
        inv = inv + _hp(inv, pw)
    return inv


def _dn_chunk(qs, k, v, g, beta, s_bf):
    c = qs.shape[0]
    row = lax.broadcasted_iota(jnp.int32, (c, c), 0)
    col = lax.broadcasted_iota(jnp.int32, (c, c), 1)
    eye, lower, strict = row == col, row >= col, row > col
    g_row = jnp.sum(jnp.where(eye, g, 0.0), axis=0, keepdims=True)
    dec = jnp.where(lower, jnp.exp(jnp.where(lower, g - g_row, 0.0)), 0.0)
    kb = k * beta
    kk = _dot_nt(kb, k)
    low = jnp.where(strict, kk * dec, 0.0)
    inv = _dn_tri_inv(low)
    eg = jnp.exp(g)
    rhs_k = kb * eg
    u = _hp(inv, v * beta)
    w = _hp(inv, rhs_k)
    aqk = jnp.where(lower, _dot_nt(qs, k) * dec, 0.0)
    rowid = lax.broadcasted_iota(jnp.int32, (c, 1), 0)
    g_last = jnp.sum(jnp.where(rowid == c - 1, g, 0.0), axis=0, keepdims=True)
    qd = qs * eg
    ekd = jnp.exp(g_last - g)
    kd = k * ekd
    vn = u - _dot(w, s_bf)
    return dict(eye=eye, lower=lower, strict=strict, dec=dec, kb=kb, kk=kk, low=low, inv=inv, eg=eg, rhs_k=rhs_k,
                u=u, w=w, aqk=aqk, g_last=g_last, qd=qd, ekd=ekd, kd=kd, vn=vn)


def _dn_head_cols(gb_blk, head):
    lane = lax.broadcasted_iota(jnp.int32, (1, LANES), 1)
    g = jnp.sum(jnp.where(lane == head, gb_blk, 0.0), axis=-1, keepdims=True)
    beta = jnp.sum(jnp.where(lane == head + DN_HEADS, gb_blk, 0.0), axis=-1, keepdims=True)
    return g, beta


def _dn_delta_fwd(qkv, gb, p, o_gain, *, name):
    T = qkv.shape[0]
    tb = min(DN_TB, T)
    nb, nc = T // tb, tb // DN_CHUNK
    H = DN_HEADS
    qscale = DN_DK ** -0.5

    def body(q_ref, k_ref, v_ref, gb_ref, gate_ref, gain_ref, o_ref, og_ref, st_ref, s_ref):
        head = pl.program_id(0)

        @pl.when(pl.program_id(1) == 0)
        def _():
            s_ref[...] = jnp.zeros_like(s_ref)

        def chunk(n, carry):
            sl = pl.ds(pl.multiple_of(n * DN_CHUNK, DN_CHUNK), DN_CHUNK)
            g, beta = _dn_head_cols(gb_ref[sl, :], head)
            s32 = s_ref[...]
            s_bf = _bf(s32)
            st_ref[n] = s_bf
            qs = q_ref[sl, :] * qscale
            t = _dn_chunk(qs, k_ref[sl, :], v_ref[sl, :], g, beta, s_bf)
            o = _dot(t["qd"], s_bf) + _dot(t["aqk"], t["vn"])
            s_ref[...] = s32 * jnp.exp(t["g_last"]) + _dot_tn(t["kd"], t["vn"])
            o_ref[sl, :] = o
            r = lax.rsqrt(jnp.mean(o * o, axis=-1, keepdims=True) + RMS_EPS)
            og_ref[sl, :] = (((o * r) * gain_ref[...]) * _silu(gate_ref[sl, :])).astype(BF16)
            return carry

        lax.fori_loop(0, nc, chunk, 0)

    qk = lambda off: pl.BlockSpec((tb, DN_DK), lambda h, i: (i, off + h))
    vblk = lambda off: pl.BlockSpec((tb, DN_DV), lambda h, i: (i, off + h))
    return pl.pallas_call(
        body, name=name, grid=(H, nb),
        in_specs=[qk(0), qk(H), vblk(2 * DN_QK_W // DN_DV), pl.BlockSpec((tb, LANES), lambda h, i: (i, 0)),
                  vblk(DN_CONV_W // DN_DV), pl.BlockSpec((1, DN_DV), lambda h, i: (0, 0))],
        out_specs=[vblk(0), vblk(0), pl.BlockSpec((None, nc, DN_DK, DN_DV), lambda h, i: (h, i, 0, 0))],
        out_shape=[jax.ShapeDtypeStruct((T, DN_V_W), F32), jax.ShapeDtypeStruct((T, DN_V_W), BF16),
                   jax.ShapeDtypeStruct((H, T // DN_CHUNK, DN_DK, DN_DV), BF16)],
        scratch_shapes=[pltpu.VMEM((DN_DK, DN_DV), F32)],
        compiler_params=pltpu.CompilerParams(dimension_semantics=("parallel", "arbitrary")),
    )(qkv, qkv, qkv, gb, p, o_gain)


def _dn_delta_bwd(qkv, gb, p, o_gain, o, states, dog, *, name):
    T = qkv.shape[0]
    tb = min(DN_TB, T)
    nb, nc = T // tb, tb // DN_CHUNK
    H = DN_HEADS
    qscale = DN_DK ** -0.5

    def body(q_ref, k_ref, v_ref, gb_ref, gate_ref, gain_ref, o_ref, st_ref, dog_ref,
             dq_ref, dk_ref, dv_ref, dgate_ref, dgb_ref, dgain_ref, ds_ref):
        head = pl.program_id(0)

        @pl.when(pl.program_id(1) == 0)
        def _():
            ds_ref[...] = jnp.zeros_like(ds_ref)

        @pl.when(jnp.logical_and(head == 0, pl.program_id(1) == 0))
        def _():
            dgain_ref[...] = jnp.zeros_like(dgain_ref)

        lane = lax.broadcasted_iota(jnp.int32, (1, LANES), 1)

        def chunk(i, carry):
            n = nc - 1 - i
            sl = pl.ds(pl.multiple_of(n * DN_CHUNK, DN_CHUNK), DN_CHUNK)
            g, beta = _dn_head_cols(gb_ref[sl, :], head)
            ov, gate, gain = o_ref[sl, :], gate_ref[sl, :], gain_ref[...]
            r = lax.rsqrt(jnp.mean(ov * ov, axis=-1, keepdims=True) + RMS_EPS)
            oh = ov * r
            dogv = dog_ref[sl, :]
            dnrm = dogv * _silu(gate)
            dgate_ref[sl, :] = dogv * (oh * gain) * _dsilu(gate)
            doh = dnrm * gain
            do = r * (doh - oh * jnp.mean(doh * oh, axis=-1, keepdims=True))
            dgain_ref[...] += jnp.sum(dnrm * oh, axis=0, keepdims=True)
            s_bf = st_ref[n]
            k, v = k_ref[sl, :], v_ref[sl, :]
            qs = q_ref[sl, :] * qscale
            t = _dn_chunk(qs, k, v, g, beta, s_bf)
            ds = ds_ref[...]
            lower, strict, eye = t["lower"], t["strict"], t["eye"]
            dqd = _dot_nt(do, s_bf)
            daqk = _dot_nt(do, t["vn"])
            dvn = _dot_tn(t["aqk"], do) + _dot(t["kd"], ds)
            dkd = _dot_nt(t["vn"], ds)
            egl = jnp.exp(t["g_last"])
            dgl = jnp.sum(jnp.sum(ds * s_bf.astype(F32), axis=-1, keepdims=True), axis=0, keepdims=True) * egl
            dw = -_dot_nt(dvn, s_bf)
            ds_ref[...] = ds * egl + _dot_tn(t["qd"], do) - _dot_tn(t["w"], dvn)
            dbv = _hp(t["inv"], dvn, _TN)
            dbk = _hp(t["inv"], dw, _TN)
            dlow = -(_hp(dbv, t["u"], _NT) + _hp(dbk, t["w"], _NT))
            m = jnp.where(strict, dlow * t["dec"], 0.0)
            dkb = _dot(m, k) + dbk * t["eg"]
            dk = _dot_tn(m, t["kb"])
            gl = jnp.where(strict, dlow * t["low"], 0.0)
            nmat = jnp.where(lower, daqk * t["dec"], 0.0)
            dqs = _dot(nmat, k) + dqd * t["eg"]
            dk = dk + _dot_tn(nmat, qs) + dkd * t["ekd"] + dkb * beta
            gmat = gl + daqk * t["aqk"]
            s_kd = jnp.sum(dkd * t["kd"], axis=-1, keepdims=True)
            dg = (jnp.sum(gmat, axis=-1, keepdims=True) + jnp.sum(dqd * t["qd"], axis=-1, keepdims=True) - s_kd
                  + jnp.sum(dbk * t["rhs_k"], axis=-1, keepdims=True))
            dg_row = -jnp.sum(gmat, axis=0, keepdims=True)
            dg = dg + jnp.sum(jnp.where(eye, dg_row, 0.0), axis=-1, keepdims=True)
            dgl = dgl + jnp.sum(s_kd, axis=0, keepdims=True)
            rowid = lax.broadcasted_iota(jnp.int32, (DN_CHUNK, 1), 0)
            dg = dg + jnp.where(rowid == DN_CHUNK - 1, dgl, 0.0)
            dbeta = jnp.sum(dbv * v, axis=-1, keepdims=True) + jnp.sum(dkb * k, axis=-1, keepdims=True)
            dq_ref[sl, :] = dqs * qscale
            dk_ref[sl, :] = dk
            dv_ref[sl, :] = dbv * beta
            dgb_ref[sl, :] = jnp.where(lane == head, dg, 0.0) + jnp.where(lane == head + DN_HEADS, dbeta, 0.0)
            return carry

        lax.fori_loop(0, nc, chunk, 0)

    rev = lambda i: nb - 1 - i
    qk = lambda off: pl.BlockSpec((tb, DN_DK), lambda h, i: (rev(i), off + h))
    vblk = lambda off: pl.BlockSpec((tb, DN_DV), lambda h, i: (rev(i), off + h))
    gain_spec = pl.BlockSpec((1, DN_DV), lambda h, i: (0, 0))
    return pl.pallas_call(
        body, name=name, grid=(H, nb),
        in_specs=[qk(0), qk(H), vblk(2 * DN_QK_W // DN_DV), pl.BlockSpec((tb, LANES), lambda h, i: (rev(i), 0)),
                  vblk(DN_CONV_W // DN_DV), gain_spec, vblk(0),
                  pl.BlockSpec((None, nc, DN_DK, DN_DV), lambda h, i: (h, rev(i), 0, 0)), vblk(0)],
        out_specs=[qk(0), qk(0), vblk(0), vblk(DN_CONV_W // DN_DV),
                   pl.BlockSpec((None, tb, LANES), lambda h, i: (h, rev(i), 0)), gain_spec],
        out_shape=[jax.ShapeDtypeStruct((T, DN_QK_W), F32), jax.ShapeDtypeStruct((T, DN_QK_W), F32),
                   jax.ShapeDtypeStruct((T, DN_V_W), F32), jax.ShapeDtypeStruct((T, DN_IN_PAD), F32),
                   jax.ShapeDtypeStruct((H, T, LANES), F32), jax.ShapeDtypeStruct((1, DN_DV), F32)],
        scratch_shapes=[pltpu.VMEM((DN_DK, DN_DV), F32)],
        compiler_params=pltpu.CompilerParams(dimension_semantics=("arbitrary", "arbitrary")),
    )(qkv, qkv, qkv, gb, p, o_gain, o, states, dog)


def _dn_conv_bwd(p, conv_w, dq, dk, dv, dp, *, name):
    T = p.shape[0]
    cw = conv_w.shape[1]
    n_q = DN_QK_W // LANES
    n_v = DN_V_W // LANES

    def body(p_ref, w_ref, dq_ref, dk_ref, dv_ref, dp_in, dp_ref, dw_ref):
        del dp_in
        j = pl.program_id(0)
        x = p_ref[...]
        ksz = w_ref.shape[0]
        xs = [_shift_down(x, ksz - 1 - i) for i in range(ksz)]
        xc = sum(w_ref[i:i + 1, :] * xs[i] for i in range(ksz))
        s = _silu(xc)
        r = lax.rsqrt(jnp.sum(s * s, axis=-1, keepdims=True) + L2_EPS)
        y = s * r
        dn = jnp.where(j < n_q, dq_ref[...], dk_ref[...])
        ds_qk = r * (dn - y * jnp.sum(dn * y, axis=-1, keepdims=True))
        ds = jnp.where(j < 2 * n_q, ds_qk, dv_ref[...])
        dxc = ds * _dsilu(xc)
        dp_ref[...] = sum(w_ref[i:i + 1, :] * _shift_up(dxc, ksz - 1 - i) for i in range(ksz))
        for i in range(ksz):
            dw_ref[i:i + 1, :] = jnp.sum(dxc * xs[i], axis=0, keepdims=True)

    colblk = pl.BlockSpec((T, LANES), lambda j: (0, j))
    wblk = pl.BlockSpec((DN_CONV, LANES), lambda j: (0, j))
    return pl.pallas_call(
        body, name=name, grid=(cw // LANES,),
        in_specs=[colblk, wblk,
                  pl.BlockSpec((T, LANES), lambda j: (0, jnp.minimum(j, n_q - 1))),
                  pl.BlockSpec((T, LANES), lambda j: (0, jnp.clip(j - n_q, 0, n_q - 1))),
                  pl.BlockSpec((T, LANES), lambda j: (0, jnp.clip(j - 2 * n_q, 0, n_v - 1))),
                  pl.BlockSpec(memory_space=pl.ANY)],
        out_specs=[colblk, wblk],
        out_shape=[jax.ShapeDtypeStruct(dp.shape, F32), jax.ShapeDtypeStruct((DN_CONV, cw), F32)],
        input_output_aliases={5: 0},
        compiler_params=pltpu.CompilerParams(dimension_semantics=("parallel",), vmem_limit_bytes=VMEM_BIG),
    )(p, conv_w, dq, dk, dv, dp)


def _dn_ab_bwd(p, alog_row, dtb_row, dgb, dp, *, name):
    T = p.shape[0]
    rows = min(DN_TB, T)
    H = DN_HEADS

    def body(p_ref, al_ref, dt_ref, dgb_ref, dp_in, dp_ref, dal_ref, ddt_ref):
        del dp_in

        @pl.when(pl.program_id(0) == 0)
        def _():
            dal_ref[...] = jnp.zeros_like(dal_ref)
            ddt_ref[...] = jnp.zeros_like(ddt_ref)

        blk = p_ref[...]
        is_a, is_b, a_arg, neg_exp, log_a, beta = _dn_ab_parts(blk, al_ref[...], dt_ref[...])
        d = dgb_ref[0]
        for hh in range(1, H):
            d = d + dgb_ref[hh]
        hi, mid, lo_ = _split3(jnp.where(is_a, d, 0.0))
        tri = _dn_chunk_tri(rows, upper=True)
        f = lambda q: jnp.dot(tri, q, preferred_element_type=F32)
        dlog_a = f(hi) + f(mid) + f(lo_)
        da_in = dlog_a * neg_exp * _sigmoid(a_arg)
        db_in = jnp.where(is_b, d, 0.0) * beta * (1.0 - beta)
        dp_ref[...] = jnp.where(is_a, da_in, 0.0) + db_in
        dal_ref[...] += jnp.sum(dlog_a * log_a, axis=0, keepdims=True)
        ddt_ref[...] += jnp.sum(jnp.where(is_a, da_in, 0.0), axis=0, keepdims=True)

    blk = pl.BlockSpec((rows, LANES), lambda i: (i, DN_AB_COL))
    vec = pl.BlockSpec((1, LANES), lambda i: (0, 0))
    return pl.pallas_call(
        body, name=name, grid=(T // rows,),
        in_specs=[blk, vec, vec, pl.BlockSpec((H, rows, LANES), lambda i: (0, i, 0)),
                  pl.BlockSpec(memory_space=pl.ANY)],
        out_specs=[blk, vec, vec],
        out_shape=[jax.ShapeDtypeStruct(dp.shape, F32), jax.ShapeDtypeStruct((1, LANES), F32),
                   jax.ShapeDtypeStruct((1, LANES), F32)],
        input_output_aliases={4: 0},
        compiler_params=pltpu.CompilerParams(dimension_semantics=("arbitrary",)),
    )(p, alog_row, dtb_row, dgb, dp)


def _dn_layer_fwd(x, ng, w_in, conv_w, a_log, dt_bias, o_gain, w_out, tag):
    alog_row, dtb_row = _dn_lane_rows(a_log, dt_bias)
    gain = o_gain.reshape(1, DN_DV)
    h = _rmsnorm_fwd(x, ng, name=f"{tag}_norm")
    p = _matmul(h, w_in, mode="nn", name=f"{tag}_inproj")
    qkv = _dn_prep_fwd(p, conv_w, name=f"{tag}_prep")
    gb = _dn_ab_fwd(p, alog_row, dtb_row, name=f"{tag}_ab")
    o, og, states = _dn_delta_fwd(qkv, gb, p, gain, name=f"{tag}_delta")
    x_new = _matmul(og, w_out, mode="nn", res=x, name=f"{tag}_outproj")
    return x_new, (h, p, qkv, gb, o, og, states)


def _dn_layer_bwd(dx, x, ng, w_in, conv_w, a_log, dt_bias, o_gain, w_out, saved, tag):
    h, p, qkv, gb, o, og, states = saved
    alog_row, dtb_row = _dn_lane_rows(a_log, dt_bias)
    gain = o_gain.reshape(1, DN_DV)
    d_wout = _matmul(og, dx, mode="tn", name=f"{tag}_dwout")
    dog = _matmul(dx, w_out, mode="nt", name=f"{tag}_dog")
    dq, dk, dv, dp, dgb, dgain = _dn_delta_bwd(qkv, gb, p, gain, o, states, dog, name=f"{tag}_deltabwd")
    dp, dconv = _dn_conv_bwd(p, conv_w, dq, dk, dv, dp, name=f"{tag}_convbwd")
    dp, dal, ddt = _dn_ab_bwd(p, alog_row, dtb_row, dgb, dp, name=f"{tag}_abbwd")
    d_win = _matmul(h, dp, mode="tn", name=f"{tag}_dwin")
    dh = _matmul(dp, w_in, mode="nt", name=f"{tag}_dh")
    dx_prev, dng = _rmsnorm_bwd(x, ng, dh, dx, name=f"{tag}_normbwd")
    return dx_prev, dng, d_win, dconv, dal[0, :DN_HEADS], ddt[0, :DN_HEADS], dgain[0], d_wout


def _sb_gains(g):
    return jnp.concatenate([g, g]).reshape(1, LANES)


def _sb_layer_fwd(x, ng, w_in, gq, gk, w_out, tag):
    h = _rmsnorm_fwd(x, ng, name=f"{tag}_norm")
    p3 = _matmul(h, w_in, mode="nn", out_parts=4, name=f"{tag}_inproj")
    og, o, lsum = _sb_attn_fwd(p3, _sb_gains(gq), _sb_gains(gk), name=f"{tag}_attn")
    x_new = _matmul(og, w_out, mode="nn", res=x, name=f"{tag}_outproj")
    return x_new, (h, p3, og, o, lsum)


def _sb_layer_bwd(dx, x, ng, w_in, gq, gk, w_out, saved, tag):
    h, p3, og, o, lsum = saved
    d_wout = _matmul(og, dx, mode="tn", name=f"{tag}_dwout")
    dog = _matmul(dx, w_out, mode="nt", name=f"{tag}_dog")
    dp3, dgq, dgk = _sb_attn_bwd(p3, _sb_gains(gq), _sb_gains(gk), o, lsum, dog, name=f"{tag}_attnbwd")
    fold = lambda d: jnp.sum(d.reshape(-1, SB_DH), axis=0)
    d_win = _matmul(h, dp3, mode="tn", b_parts=4, name=f"{tag}_dwin")
    dh = _matmul(dp3, w_in, mode="nt", a_parts=4, name=f"{tag}_dh")
    dx_prev, dng = _rmsnorm_bwd(x, ng, dh, dx, name=f"{tag}_normbwd")
    return dx_prev, dng, d_win, fold(dgq), fold(dgk), d_wout


N_CHIPS = 4
HBM = pl.BlockSpec(memory_space=pl.ANY)


def _mesh_pos():
    return lax.axis_index("x"), lax.axis_index("y"), lax.axis_index("c")


def _other_chips(x, y):
    return [(1 - x, y), (x, 1 - y), (1 - x, 1 - y)]


def _chip_exchange(srcs, *, send_slot_is_dest, copy_own, name):
    n = len(srcs)

    def body(*refs):
        src_refs, out_refs = refs[:n], refs[n:2 * n]
        send_sems, recv_sems, local_sems = refs[2 * n:]
        x, y, c = _mesh_pos()
        me = 2 * x + y
        chips = _other_chips(x, y)
        local = []
        for a in range(n):
            if not copy_own[a]:
                continue
            own = src_refs[a].at[me] if send_slot_is_dest else src_refs[a]
            local.append(pltpu.make_async_copy(own, out_refs[a].at[me], local_sems.at[a]))
        for cp in local:
            cp.start()

        def copy(a, k, landing_slot):
            px, py = chips[k]
            src = src_refs[a].at[2 * px + py] if send_slot_is_dest else src_refs[a]
            return pltpu.make_async_remote_copy(
                src_ref=src, dst_ref=out_refs[a].at[landing_slot],
                send_sem=send_sems.at[a * 3 + k], recv_sem=recv_sems.at[a * 3 + k],
                device_id=(px, py, c), device_id_type=MESH)

        sends = [copy(a, k, me) for a in range(n) for k in range(3)]
        for cp in sends:
            cp.start()
        for a in range(n):
            for k in range(3):
                px, py = chips[k]
                copy(a, k, 2 * px + py).wait_recv()
        for cp in sends:
            cp.wait_send()
        for cp in local:
            cp.wait()

    outs = []
    for s in srcs:
        shape = s.shape if send_slot_is_dest else (N_CHIPS,) + s.shape
        outs.append(jax.ShapeDtypeStruct(shape, s.dtype))
    return pl.pallas_call(
        body, name=name, in_specs=[HBM] * n, out_specs=[HBM] * n, out_shape=outs,
        scratch_shapes=[pltpu.SemaphoreType.DMA((3 * n,)), pltpu.SemaphoreType.DMA((3 * n,)),
                        pltpu.SemaphoreType.DMA((n,))],
    )(*srcs)


def _sibling_exchange(srcs, *, name):
    n = len(srcs)

    def body(*refs):
        src_refs, out_refs = refs[:n], refs[n:2 * n]
        send_sems, recv_sems = refs[2 * n:]
        x, y, c = _mesh_pos()
        copies = [pltpu.make_async_remote_copy(
            src_ref=src_refs[a], dst_ref=out_refs[a], send_sem=send_sems.at[a], recv_sem=recv_sems.at[a],
            device_id=(x, y, 1 - c), device_id_type=MESH) for a in range(n)]
        for cp in copies:
            cp.start()
        for cp in copies:
            cp.wait()

    return pl.pallas_call(
        body, name=name, in_specs=[HBM] * n, out_specs=[HBM] * n,
        out_shape=[jax.ShapeDtypeStruct(s.shape, s.dtype) for s in srcs],
        scratch_shapes=[pltpu.SemaphoreType.DMA((n,)), pltpu.SemaphoreType.DMA((n,))],
    )(*srcs)


def _sum_big(own4, recv4, me, *, name):
    _, R, C = own4.shape
    tr = _pick(R, (512, 256, 128, 64, 32, 16, 8))

    def body(me_ref, own_ref, r1_ref, r2_ref, r3_ref, o_ref):
        del me_ref
        o_ref[...] = ((own_ref[...] + r1_ref[...].astype(F32)) + r2_ref[...].astype(F32)) + r3_ref[...].astype(F32)

    slot = lambda d: pl.BlockSpec((None, tr, C), lambda i, me_ref: ((me_ref[0] + d) % N_CHIPS, i, 0))
    return pl.pallas_call(
        body, name=name,
        grid_spec=pltpu.PrefetchScalarGridSpec(
            num_scalar_prefetch=1, grid=(R // tr,),
            in_specs=[slot(0), slot(1), slot(2), slot(3)],
            out_specs=pl.BlockSpec((tr, C), lambda i, me_ref: (i, 0))),
        out_shape=jax.ShapeDtypeStruct((R, C), F32),
        compiler_params=pltpu.CompilerParams(dimension_semantics=("parallel",)),
    )(me, own4, recv4, recv4, recv4)


def _sum_small(recv4, *, name):
    _, R, C = recv4.shape

    def body(r_ref, o_ref):
        o_ref[...] = ((r_ref[0] + r_ref[1]) + r_ref[2]) + r_ref[3]

    return pl.pallas_call(body, name=name, out_shape=jax.ShapeDtypeStruct((R, C), F32))(recv4)


def _add(a, b, *, name):
    R, C = a.shape
    tr = _pick(R, (512, 256, 128, 64, 32, 16, 8))
    blk = pl.BlockSpec((tr, C), lambda i: (i, 0))

    def body(a_ref, b_ref, o_ref):
        o_ref[...] = a_ref[...] + b_ref[...]

    return pl.pallas_call(body, name=name, grid=(R // tr,), in_specs=[blk, blk], out_specs=blk,
                          out_shape=jax.ShapeDtypeStruct((R, C), F32),
                          compiler_params=pltpu.CompilerParams(dimension_semantics=("parallel",)))(a, b)


def _adamw(w, g, m, v, *, name):
    shape = w.shape
    C = shape[-1]
    R = w.size // C
    two = lambda a: a.reshape(R, C)
    tr = _pick(R, (256, 128, 64, 32, 16, 8)) if R % 8 == 0 and R > 8 else R
    blk = pl.BlockSpec((tr, C), lambda i: (i, 0))

    def body(w_ref, g_ref, m_ref, v_ref, d_ref, nm_ref, nv_ref):
        gv = g_ref[...]
        nm = ADAM_B1 * m_ref[...] + (1.0 - ADAM_B1) * gv
        nv = ADAM_B2 * v_ref[...] + (1.0 - ADAM_B2) * (gv * gv)
        m_hat = nm / (1.0 - ADAM_B1 ** ADAM_STEP)
        v_hat = nv / (1.0 - ADAM_B2 ** ADAM_STEP)
        d_ref[...] = -ADAM_LR * (m_hat / (jnp.sqrt(v_hat) + ADAM_EPS) + ADAM_WD * w_ref[...])
        nm_ref[...] = nm
        nv_ref[...] = nv

    out = jax.ShapeDtypeStruct((R, C), F32)
    d, nm, nv = pl.pallas_call(
        body, name=name, grid=(R // tr,), in_specs=[blk] * 4, out_specs=[blk] * 3, out_shape=[out] * 3,
        compiler_params=pltpu.CompilerParams(dimension_semantics=("parallel",)),
    )(two(w), two(g), two(m), two(v))
    return d.reshape(shape), nm.reshape(shape), nv.reshape(shape)


PACK_COLS = 1024
BIG = (("dn_w_in", (2, 1024, 1540), 2), ("dn_w_out", (2, 512, 1024), 1), ("sb_w_in", (1, 1024, 1024), 2),
       ("sb_w_out", (1, 256, 1024), 1), ("sc_w_in", (1, 1024, 2048), 2), ("sc_w_out", (1, 512, 1024), 1))
SMALL = (("dn_conv_w", (2, 4, 1024), 2), ("dn_o_norm_g", (2, 64), 1), ("sc_conv_w", (1, 3, 512), 2))
REPL = (("norm_g", (4, 1024)), ("dn_a_log", (2, 8)), ("dn_dt_bias", (2, 8)), ("sb_q_norm_g", (1, 64)),
        ("sb_k_norm_g", (1, 64)))


def _pack(arrays, cols, lead=()):
    flat = jnp.concatenate([a.reshape(lead + (-1,)) for a in arrays], axis=-1)
    n = flat.shape[-1]
    rows = -(-n // cols)
    unit = 512 if rows > 512 else 8
    rows = -(-rows // unit) * unit
    flat = jnp.pad(flat, [(0, 0)] * len(lead) + [(0, rows * cols - n)])
    return flat.reshape(lead + (rows, cols))


def _unpack(buf, table, lead=()):
    flat = buf.reshape(lead + (-1,))
    out, off = {}, 0
    for entry in table:
        name, shape = entry[0], entry[1]
        n = math.prod(shape)
        out[name] = flat[..., off:off + n].reshape(lead + shape)
        off += n
    return out


def _join(shards, axis):
    return jnp.concatenate([shards[j] for j in range(N_CHIPS)], axis=axis)


def _split(full, axis):
    return jnp.stack(jnp.split(full, N_CHIPS, axis=axis), axis=0)


def kernel(x, norm_g, dn_w_in, dn_conv_w, dn_a_log, dn_dt_bias, dn_o_norm_g, dn_w_out, sb_w_in, sb_q_norm_g, sb_k_norm_g, sb_w_out, sc_w_in, sc_conv_w, sc_w_out, loss_target, m_norm_g, m_dn_w_in, m_dn_conv_w, m_dn_a_log, m_dn_dt_bias, m_dn_o_norm_g, m_dn_w_out, m_sb_w_in, m_sb_q_norm_g, m_sb_k_norm_g, m_sb_w_out, m_sc_w_in, m_sc_conv_w, m_sc_w_out, v_norm_g, v_dn_w_in, v_dn_conv_w, v_dn_a_log, v_dn_dt_bias, v_dn_o_norm_g, v_dn_w_out, v_sb_w_in, v_sb_q_norm_g, v_sb_k_norm_g, v_sb_w_out, v_sc_w_in, v_sc_conv_w, v_sc_w_out):
    weights = dict(norm_g=norm_g, dn_w_in=dn_w_in, dn_conv_w=dn_conv_w, dn_a_log=dn_a_log, dn_dt_bias=dn_dt_bias,
                   dn_o_norm_g=dn_o_norm_g, dn_w_out=dn_w_out, sb_w_in=sb_w_in, sb_q_norm_g=sb_q_norm_g,
                   sb_k_norm_g=sb_k_norm_g, sb_w_out=sb_w_out, sc_w_in=sc_w_in, sc_conv_w=sc_conv_w, sc_w_out=sc_w_out)
    m_in = dict(norm_g=m_norm_g, dn_w_in=m_dn_w_in, dn_conv_w=m_dn_conv_w, dn_a_log=m_dn_a_log,
                dn_dt_bias=m_dn_dt_bias, dn_o_norm_g=m_dn_o_norm_g, dn_w_out=m_dn_w_out, sb_w_in=m_sb_w_in,
                sb_q_norm_g=m_sb_q_norm_g, sb_k_norm_g=m_sb_k_norm_g, sb_w_out=m_sb_w_out, sc_w_in=m_sc_w_in,
                sc_conv_w=m_sc_conv_w, sc_w_out=m_sc_w_out)
    v_in = dict(norm_g=v_norm_g, dn_w_in=v_dn_w_in, dn_conv_w=v_dn_conv_w, dn_a_log=v_dn_a_log,
                dn_dt_bias=v_dn_dt_bias, dn_o_norm_g=v_dn_o_norm_g, dn_w_out=v_dn_w_out, sb_w_in=v_sb_w_in,
                sb_q_norm_g=v_sb_q_norm_g, sb_k_norm_g=v_sb_k_norm_g, sb_w_out=v_sb_w_out, sc_w_in=v_sc_w_in,
                sc_conv_w=v_sc_conv_w, sc_w_out=v_sc_w_out)
    order = list(weights)
    xi, yi, _ = _mesh_pos()
    me = (2 * xi + yi).astype(jnp.int32).reshape(1)

    big = _pack([weights[n].astype(BF16) for n, _, _ in BIG], PACK_COLS)
    small = _pack([weights[n] for n, _, _ in SMALL], LANES)
    big4, small4 = _chip_exchange([big, small], send_slot_is_dest=False, copy_own=(True, True),
                                  name="gather_weights")
    full = {n: _join(a, ax) for (n, _, ax), a in zip(BIG, _unpack(big4, BIG, (N_CHIPS,)).values())}
    full.update({n: _join(a, ax) for (n, _, ax), a in zip(SMALL, _unpack(small4, SMALL, (N_CHIPS,)).values())})
    dn_w_in_pad = jnp.pad(full["dn_w_in"], ((0, 0), (0, 0), (0, DN_IN_PAD - DN_IN)))

    def dn_args(j):
        return (dn_w_in_pad[j], full["dn_conv_w"][j], dn_a_log[j], dn_dt_bias[j], full["dn_o_norm_g"][j],
                full["dn_w_out"][j])

    sb_args = (full["sb_w_in"][0], sb_q_norm_g[0], sb_k_norm_g[0], full["sb_w_out"][0])
    sc_args = (full["sc_w_in"][0], full["sc_conv_w"][0], full["sc_w_out"][0])

    x0 = x[0]
    x1, s0 = _dn_layer_fwd(x0, norm_g[0], *dn_args(0), "l0")
    x2, s1 = _sb_layer_fwd(x1, norm_g[1], *sb_args, "l1")
    x3, s2 = _sc_layer_fwd(x2, norm_g[2], *sc_args, "l2")
    x4, s3 = _dn_layer_fwd(x3, norm_g[3], *dn_args(1), "l3")
    dy, loss_local = _loss_head(x4, loss_target[0], name="loss_head")
    loss = lax.psum(loss_local[0, 0], ("x", "y", "c"))

    dx3, dng3, dwin3, dconv3, dal3, ddt3, dgain3, dwout3 = _dn_layer_bwd(dy, x3, norm_g[3], *dn_args(1), s3, "l3")
    dx2, dng2, dwin2, dconv2, dwout2 = _sc_layer_bwd(dx3, x2, norm_g[2], *sc_args, s2, "l2")
    dx1, dng1, dwin1, dgq, dgk, dwout1 = _sb_layer_bwd(dx2, x1, norm_g[1], *sb_args, s1, "l1")
    dx0, dng0, dwin0, dconv0, dal0, ddt0, dgain0, dwout0 = _dn_layer_bwd(dx1, x0, norm_g[0], *dn_args(0), s0, "l0")

    grads = dict(
        norm_g=jnp.concatenate([dng0, dng1, dng2, dng3], axis=0),
        dn_w_in=jnp.stack([dwin0[:, :DN_IN], dwin3[:, :DN_IN]]), dn_conv_w=jnp.stack([dconv0, dconv3]),
        dn_a_log=jnp.stack([dal0, dal3]), dn_dt_bias=jnp.stack([ddt0, ddt3]),
        dn_o_norm_g=jnp.stack([dgain0, dgain3]), dn_w_out=jnp.stack([dwout0, dwout3]),
        sb_w_in=dwin1[None], sb_q_norm_g=dgq[None], sb_k_norm_g=dgk[None], sb_w_out=dwout1[None],
        sc_w_in=dwin2[None], sc_conv_w=dconv2[None], sc_w_out=dwout2[None])

    gbig = _pack([_split(grads[n], ax) for n, _, ax in BIG], PACK_COLS, (N_CHIPS,))
    repl = [jnp.broadcast_to(grads[n][None], (N_CHIPS,) + s) for n, s in REPL]
    gsmall = _pack([_split(grads[n], ax) for n, _, ax in SMALL] + repl, LANES, (N_CHIPS,))
    rbig, rsmall = _chip_exchange([gbig.astype(BF16), gsmall], send_slot_is_dest=True, copy_own=(False, True),
                                  name="scatter_grads")
    pbig = _sum_big(gbig, rbig, me, name="sum_chips_big")
    psmall = _sum_small(rsmall, name="sum_chips_small")
    qbig, qsmall = _sibling_exchange([pbig, psmall], name="swap_cores")
    tbig = _add(pbig, qbig, name="sum_cores_big")
    tsmall = _add(psmall, qsmall, name="sum_cores_small")
    g_out = _unpack(tbig, BIG)
    g_out.update(_unpack(tsmall, SMALL + REPL))

    upd = {n: _adamw(weights[n], g_out[n], m_in[n], v_in[n], name=f"adamw_{n}") for n in order}
    return (loss, dx0[None], *[g_out[n] for n in order], *[upd[n][0] for n in order],
            *[upd[n][1] for n in order], *[upd[n][2] for n in order])
```

```python
import math

import jax
import jax.numpy as jnp
from jax import lax
from jax.experimental import pallas as pl
from jax.experimental.pallas import tpu as pltpu

F32 = jnp.float32
BF16 = jnp.bfloat16
MESH = pl.DeviceIdType.MESH

RMS_EPS = 1e-6
L2_EPS = 1e-6
LANES = 128
VMEM_BIG = 60 * 1024 * 1024

DN_HEADS, DN_DK, DN_DV, DN_CHUNK, DN_CONV = 8, 128, 256, 64, 4
DN_QK_W = DN_HEADS * DN_DK
DN_V_W = DN_HEADS * DN_DV
DN_CONV_W = 2 * DN_QK_W + DN_V_W
DN_IN = DN_CONV_W + DN_V_W + 2 * DN_HEADS
DN_IN_PAD = DN_CONV_W + DN_V_W + LANES
SB_DH = 64
SC_CONV = 3

ADAM_LR, ADAM_B1, ADAM_B2, ADAM_EPS, ADAM_WD, ADAM_STEP = 0.001, 0.9, 0.999, 1e-08, 0.01, 10


def _pick(n, cands):
    for c in cands:
        if n % c == 0:
            return c
    raise ValueError(f"no tile for {n} in {cands}")


def _bf(x):
    return x.astype(BF16)


def _dot(a, b):
    return jnp.dot(_bf(a), _bf(b), preferred_element_type=F32)


def _dot_nt(a, b):
    return lax.dot_general(_bf(a), _bf(b), (((1,), (1,)), ((), ())), preferred_element_type=F32)


def _dot_tn(a, b):
    return lax.dot_general(_bf(a), _bf(b), (((0,), (0,)), ((), ())), preferred_element_type=F32)


def _split3(a):
    hi = _bf(a)
    r = a - hi.astype(F32)
    mid = _bf(r)
    lo = _bf(r - mid.astype(F32))
    return hi, mid, lo


def _sigmoid(x):
    return 1.0 / (1.0 + jnp.exp(-x))


def _silu(x):
    return x * _sigmoid(x)


def _dsilu(x):
    s = _sigmoid(x)
    return s * (1.0 + x * (1.0 - s))


def _softplus(x):
    return jnp.maximum(x, 0.0) + jnp.log(1.0 + jnp.exp(-jnp.abs(x)))


def _shift_down(z, k):
    if k == 0:
        return z
    row = lax.broadcasted_iota(jnp.int32, z.shape, 0)
    return jnp.where(row >= k, pltpu.roll(z, k, 0), 0.0)


def _shift_up(z, k):
    if k == 0:
        return z
    n = z.shape[0]
    row = lax.broadcasted_iota(jnp.int32, z.shape, 0)
    return jnp.where(row < n - k, pltpu.roll(z, n - k, 0), 0.0)


def _matmul(a, b, *, mode, name, res=None, a_parts=1, b_parts=1, out_parts=1, out_dtype=F32):
    def dims2(x, parts):
        if parts == 1:
            return x.shape
        assert x.shape[0] == parts
        return (x.shape[1], x.shape[2] * parts)

    ash, bsh = dims2(a, a_parts), dims2(b, b_parts)
    if mode == "nn":
        (M, K), (K2, N) = ash, bsh
        dn = (((1,), (0,)), ((), ()))
    elif mode == "nt":
        (M, K), (N, K2) = ash, bsh
        dn = (((1,), (1,)), ((), ()))
    else:
        (K, M), (K2, N) = ash, bsh
        dn = (((0,), (0,)), ((), ()))
    assert K == K2, (ash, bsh, mode)
    tm = _pick(M, (512, 256, 128, 64, 32, 16, 8))
    n_unit = N // out_parts if mode == "nn" else (N // b_parts if mode == "tn" else N)
    tn = _pick(n_unit, (1024, 896, 768, 512, 384, 256, 128))
    k_unit = K // a_parts if mode in ("nn", "nt") else K
    tk = _pick(k_unit, (512, 896, 256, 128))
    nk = K // tk
    grid = (M // tm, N // tn, nk)

    def spec(parts, rows_are, cols_are, tr, tc, width):
        per = width // parts // tc
        if parts == 1:
            return pl.BlockSpec((tr, tc), lambda i, j, k: ((i, j, k)[rows_are], (i, j, k)[cols_are]))
        return pl.BlockSpec((None, tr, tc), lambda i, j, k: ((i, j, k)[cols_are] // per, (i, j, k)[rows_are],
                                                             (i, j, k)[cols_are] % per))

    if mode == "nn":
        a_spec = spec(a_parts, 0, 2, tm, tk, K)
        b_spec = spec(b_parts, 2, 1, tk, tn, N)
    elif mode == "nt":
        a_spec = spec(a_parts, 0, 2, tm, tk, K)
        b_spec = spec(b_parts, 1, 2, tn, tk, K)
    else:
        a_spec = spec(a_parts, 2, 0, tk, tm, M)
        b_spec = spec(b_parts, 2, 1, tk, tn, N)
    o_spec = spec(out_parts, 0, 1, tm, tn, N)
    in_specs = [a_spec, b_spec]
    operands = [a, b]
    if res is not None:
        in_specs.append(pl.BlockSpec((tm, tn), lambda i, j, k: (i, j)))
        operands.append(res)

    def body(*refs):
        a_ref, b_ref = refs[0], refs[1]
        o_ref, acc_ref = refs[-2], refs[-1]
        k = pl.program_id(2)

        @pl.when(k == 0)
        def _():
            acc_ref[...] = jnp.zeros_like(acc_ref)

        acc_ref[...] += lax.dot_general(_bf(a_ref[...]), _bf(b_ref[...]), dn, preferred_element_type=F32)

        @pl.when(k == nk - 1)
        def _():
            r = acc_ref[...]
            if res is not None:
                r = refs[2][...] + r
            o_ref[...] = r.astype(o_ref.dtype)

    out_shape = (M, N) if out_parts == 1 else (out_parts, M, N // out_parts)
    return pl.pallas_call(
        body, name=name, grid=grid, in_specs=in_specs, out_specs=o_spec,
        out_shape=jax.ShapeDtypeStruct(out_shape, out_dtype),
        scratch_shapes=[pltpu.VMEM((tm, tn), F32)],
        compiler_params=pltpu.CompilerParams(dimension_semantics=("parallel", "parallel", "arbitrary")),
    )(*operands)


def _rmsnorm_fwd(x, g, *, name):
    T, D = x.shape
    tm = _pick(T, (512, 256, 128, 64, 32, 16))

    def body(x_ref, g_ref, h_ref):
        xv = x_ref[...]
        r = lax.rsqrt(jnp.mean(xv * xv, axis=-1, keepdims=True) + RMS_EPS)
        h_ref[...] = ((xv * r) * g_ref[...]).astype(BF16)

    return pl.pallas_call(
        body, name=name, grid=(T // tm,),
        in_specs=[pl.BlockSpec((tm, D), lambda i: (i, 0)), pl.BlockSpec((1, D), lambda i: (0, 0))],
        out_specs=pl.BlockSpec((tm, D), lambda i: (i, 0)),
        out_shape=jax.ShapeDtypeStruct((T, D), BF16),
    )(x, g.reshape(1, D))


def _rmsnorm_bwd(x, g, dh, dx_in, *, name):
    T, D = x.shape
    tm = _pick(T, (512, 256, 128, 64, 32, 16))

    def body(x_ref, g_ref, dh_ref, dxin_ref, dx_ref, dg_ref):
        @pl.when(pl.program_id(0) == 0)
        def _():
            dg_ref[...] = jnp.zeros_like(dg_ref)

        xv = x_ref[...]
        r = lax.rsqrt(jnp.mean(xv * xv, axis=-1, keepdims=True) + RMS_EPS)
        xh = xv * r
        dh_v = dh_ref[...]
        dxh = dh_v * g_ref[...]
        dx_ref[...] = dxin_ref[...] + r * (dxh - xh * jnp.mean(dxh * xh, axis=-1, keepdims=True))
        dg_ref[...] += jnp.sum(dh_v * xh, axis=0, keepdims=True)

    row = pl.BlockSpec((tm, D), lambda i: (i, 0))
    vec = pl.BlockSpec((1, D), lambda i: (0, 0))
    return pl.pallas_call(
        body, name=name, grid=(T // tm,),
        in_specs=[row, vec, row, row], out_specs=[row, vec],
        out_shape=[jax.ShapeDtypeStruct((T, D), F32), jax.ShapeDtypeStruct((1, D), F32)],
        compiler_params=pltpu.CompilerParams(dimension_semantics=("arbitrary",)),
    )(x, g.reshape(1, D), dh, dx_in)


def _loss_head(y, target, *, name):
    T, D = y.shape
    tm = _pick(T, (512, 256, 128, 64, 32, 16))

    def body(y_ref, t_ref, dy_ref, l_ref):
        @pl.when(pl.program_id(0) == 0)
        def _():
            l_ref[...] = jnp.zeros_like(l_ref)

        err = y_ref[...] - t_ref[...]
        dy_ref[...] = err * (1.0 / D)
        l_ref[...] += 0.5 * jnp.sum(jnp.mean(err * err, axis=-1, keepdims=True), axis=0, keepdims=True)

    row = pl.BlockSpec((tm, D), lambda i: (i, 0))
    return pl.pallas_call(
        body, name=name, grid=(T // tm,),
        in_specs=[row, row], out_specs=[row, pl.BlockSpec((1, 1), lambda i: (0, 0))],
        out_shape=[jax.ShapeDtypeStruct((T, D), F32), jax.ShapeDtypeStruct((1, 1), F32)],
        compiler_params=pltpu.CompilerParams(dimension_semantics=("arbitrary",)),
    )(y, target)


def _sc_mid_fwd(p3, conv_w, *, name):
    _, T, W = p3.shape
    K = conv_w.shape[0]
    cw = LANES

    def body(p_ref, w_ref, o_ref):
        z = p_ref[1] * p_ref[2]
        cv = sum(w_ref[i:i + 1, :] * _shift_down(z, K - 1 - i) for i in range(K))
        o_ref[...] = ((p_ref[0] * cv) * _silu(p_ref[3])).astype(BF16)

    return pl.pallas_call(
        body, name=name, grid=(W // cw,),
        in_specs=[pl.BlockSpec((4, T, cw), lambda j: (0, 0, j)), pl.BlockSpec((K, cw), lambda j: (0, j))],
        out_specs=pl.BlockSpec((T, cw), lambda j: (0, j)),
        out_shape=jax.ShapeDtypeStruct((T, W), BF16),
        compiler_params=pltpu.CompilerParams(dimension_semantics=("parallel",), vmem_limit_bytes=VMEM_BIG),
    )(p3, conv_w)


def _sc_mid_bwd(p3, conv_w, do, *, name):
    _, T, W = p3.shape
    K = conv_w.shape[0]
    cw = LANES

    def body(p_ref, w_ref, do_ref, dp_ref, dw_ref):
        b, c, u, gate = p_ref[0], p_ref[1], p_ref[2], p_ref[3]
        z = c * u
        zs = [_shift_down(z, K - 1 - i) for i in range(K)]
        cv = sum(w_ref[i:i + 1, :] * zs[i] for i in range(K))
        y = b * cv
        dov = do_ref[...]
        dy = dov * _silu(gate)
        dp_ref[3] = dov * y * _dsilu(gate)
        dp_ref[0] = dy * cv
        dcv = dy * b
        dz = sum(w_ref[i:i + 1, :] * _shift_up(dcv, K - 1 - i) for i in range(K))
        dp_ref[1] = dz * u
        dp_ref[2] = dz * c
        for i in range(K):
            dw_ref[i:i + 1, :] = jnp.sum(dcv * zs[i], axis=0, keepdims=True)

    return pl.pallas_call(
        body, name=name, grid=(W // cw,),
        in_specs=[pl.BlockSpec((4, T, cw), lambda j: (0, 0, j)), pl.BlockSpec((K, cw), lambda j: (0, j)),
                  pl.BlockSpec((T, cw), lambda j: (0, j))],
        out_specs=[pl.BlockSpec((4, T, cw), lambda j: (0, 0, j)), pl.BlockSpec((K, cw), lambda j: (0, j))],
        out_shape=[jax.ShapeDtypeStruct((4, T, W), F32), jax.ShapeDtypeStruct((K, W), F32)],
        compiler_params=pltpu.CompilerParams(dimension_semantics=("parallel",), vmem_limit_bytes=VMEM_BIG),
    )(p3, conv_w, do)


def _sc_layer_fwd(x, ng, w_in, conv_w, w_out, tag):
    h = _rmsnorm_fwd(x, ng, name=f"{tag}_norm")
    p3 = _matmul(h, w_in, mode="nn", out_parts=4, name=f"{tag}_inproj")
    og = _sc_mid_fwd(p3, conv_w, name=f"{tag}_mid")
    x_new = _matmul(og, w_out, mode="nn", res=x, name=f"{tag}_outproj")
    return x_new, (h, p3, og)


def _sc_layer_bwd(dx, x, ng, w_in, conv_w, w_out, saved, tag):
    h, p3, og = saved
    d_wout = _matmul(og, dx, mode="tn", name=f"{tag}_dwout")
    dog = _matmul(dx, w_out, mode="nt", name=f"{tag}_dog")
    dp3, dconv = _sc_mid_bwd(p3, conv_w, dog, name=f"{tag}_midbwd")
    d_win = _matmul(h, dp3, mode="tn", b_parts=4, name=f"{tag}_dwin")
    dh = _matmul(dp3, w_in, mode="nt", a_parts=4, name=f"{tag}_dh")
    dx_prev, dng = _rmsnorm_bwd(x, ng, dh, dx, name=f"{tag}_normbwd")
    return dx_prev, dng, d_win, dconv, d_wout


SB_BQ = 256
SB_BK = 256
SB_ROWS = 512


def _dot_x2(a, b_exact_bf16):
    hi = _bf(a)
    mid = _bf(a - hi.astype(F32))
    return (jnp.dot(hi, b_exact_bf16, preferred_element_type=F32)
            + jnp.dot(mid, b_exact_bf16, preferred_element_type=F32))


def _sb_half_mask():
    return lax.broadcasted_iota(jnp.int32, (1, LANES), 1) < SB_DH


def _sb_headnorm(x, g, lo):
    x2 = x * x
    s_lo = jnp.sum(jnp.where(lo, x2, 0.0), axis=-1, keepdims=True)
    s_hi = jnp.sum(jnp.where(lo, 0.0, x2), axis=-1, keepdims=True)
    r = lax.rsqrt(jnp.where(lo, s_lo, s_hi) * (1.0 / SB_DH) + RMS_EPS)
    xh = x * r
    return xh * g, xh, r


def _sb_stack(xb, lo):
    zero = jnp.zeros_like(xb)
    return jnp.concatenate([jnp.where(lo, xb, zero), jnp.where(lo, zero, xb)], axis=0)


def _sb_rel(bq, bk):
    row = lax.broadcasted_iota(jnp.int32, (2 * bq, bk), 0)
    col = lax.broadcasted_iota(jnp.int32, (2 * bq, bk), 1)
    return col - jnp.where(row >= bq, row - bq, row)


def _sb_tile(qm, kb, valid, scale):
    z = lax.dot_general(qm, kb, (((1,), (1,)), ((), ())), preferred_element_type=F32) * scale
    sp = _softplus(z)
    return z - sp, jnp.where(valid, -sp, 0.0)


def _sb_attn_fwd(p3, gq2, gk2, *, name):
    _, T, W = p3.shape
    bq, bk = min(SB_BQ, T), min(SB_BK, T)
    rows = min(SB_ROWS, T)
    scale = SB_DH ** -0.5

    def body(p_ref, gq_ref, gk_ref, og_ref, o_ref, ls_ref, qn_ref, kn_ref, v_ref):
        lo = _sb_half_mask()

        def prologue(i, c):
            r0 = pl.multiple_of(i * rows, rows)
            sl = pl.ds(r0, rows)
            qn_ref[sl, :] = _sb_headnorm(p_ref[0, sl, :], gq_ref[...], lo)[0].astype(BF16)
            kn_ref[sl, :] = _sb_headnorm(p_ref[1, sl, :], gk_ref[...], lo)[0].astype(BF16)
            v_ref[sl, :] = p_ref[2, sl, :].astype(BF16)
            return c

        lax.fori_loop(0, T // rows, prologue, 0)

        rel = _sb_rel(bq, bk)
        tri = (lax.broadcasted_iota(jnp.int32, (bk, bk), 0)
               > lax.broadcasted_iota(jnp.int32, (bk, bk), 1)).astype(BF16)

        def qblock(qi, c):
            q0 = pl.multiple_of(qi * bq, bq)
            qm = _sb_stack(qn_ref[pl.ds(q0, bq), :], lo)
            nkb = (q0 + bq - 1) // bk + 1

            def kblock(t, carry):
                o_acc, a_carry = carry
                k0 = pl.multiple_of((nkb - 1 - t) * bk, bk)
                valid = rel < q0 - k0
                logsig, log1m = _sb_tile(qm, kn_ref[pl.ds(k0, bk), :], valid, scale)
                wts = jnp.where(valid, jnp.exp(logsig + (_dot_x2(log1m, tri) + a_carry)), 0.0)
                o_acc = o_acc + jnp.dot(_bf(wts), v_ref[pl.ds(k0, bk), :], preferred_element_type=F32)
                return o_acc, a_carry + jnp.sum(log1m, axis=-1, keepdims=True)

            o2, t2 = lax.fori_loop(0, nkb, kblock, (jnp.zeros((2 * bq, LANES), F32), jnp.zeros((2 * bq, 1), F32)))
            o = jnp.where(lo, o2[:bq], o2[bq:])
            o_ref[pl.ds(q0, bq), :] = o
            ls_ref[pl.ds(q0, bq), :] = jnp.where(lo, t2[:bq], t2[bq:])
            og_ref[pl.ds(q0, bq), :] = (o * _silu(p_ref[3, pl.ds(q0, bq), :])).astype(BF16)
            return c

        lax.fori_loop(0, T // bq, qblock, 0)

    colblk = pl.BlockSpec((T, LANES), lambda j: (0, j))
    vec = pl.BlockSpec((1, LANES), lambda j: (0, 0))
    return pl.pallas_call(
        body, name=name, grid=(W // LANES,),
        in_specs=[pl.BlockSpec((4, T, LANES), lambda j: (0, 0, j)), vec, vec],
        out_specs=[colblk, colblk, colblk],
        out_shape=[jax.ShapeDtypeStruct((T, W), BF16), jax.ShapeDtypeStruct((T, W), F32),
                   jax.ShapeDtypeStruct((T, W), F32)],
        scratch_shapes=[pltpu.VMEM((T, LANES), BF16)] * 3,
        compiler_params=pltpu.CompilerParams(dimension_semantics=("parallel",), vmem_limit_bytes=VMEM_BIG),
    )(p3, gq2, gk2)


def _sb_attn_bwd(p3, gq2, gk2, o, lsum, dog, *, name):
    _, T, W = p3.shape
    bq, bk = min(SB_BQ, T), min(SB_BK, T)
    rows = min(SB_ROWS, T)
    scale = SB_DH ** -0.5

    def body(p_ref, gq_ref, gk_ref, o_ref, ls_ref, dog_ref, dp_ref, dgq_ref, dgk_ref,
             qn_ref, kn_ref, v_ref, do_ref):
        lo = _sb_half_mask()

        def prologue(i, c):
            r0 = pl.multiple_of(i * rows, rows)
            sl = pl.ds(r0, rows)
            qn_ref[sl, :] = _sb_headnorm(p_ref[0, sl, :], gq_ref[...], lo)[0].astype(BF16)
            kn_ref[sl, :] = _sb_headnorm(p_ref[1, sl, :], gk_ref[...], lo)[0].astype(BF16)
            v_ref[sl, :] = p_ref[2, sl, :].astype(BF16)
            gate = p_ref[3, sl, :]
            dogv = dog_ref[sl, :]
            dp_ref[3, sl, :] = dogv * o_ref[sl, :] * _dsilu(gate)
            do_ref[sl, :] = (dogv * _silu(gate)).astype(BF16)
            zero = jnp.zeros((rows, LANES), F32)
            dp_ref[0, sl, :] = zero
            dp_ref[1, sl, :] = zero
            dp_ref[2, sl, :] = zero
            return c

        lax.fori_loop(0, T // rows, prologue, 0)

        rel = _sb_rel(bq, bk)
        rj = lax.broadcasted_iota(jnp.int32, (bk, bk), 0)
        cj = lax.broadcasted_iota(jnp.int32, (bk, bk), 1)
        upto = (rj <= cj).astype(BF16)
        before_m = (rj < cj).astype(BF16)

        def qblock(qi, c):
            q0 = pl.multiple_of(qi * bq, bq)
            qm = _sb_stack(qn_ref[pl.ds(q0, bq), :], lo)
            dom = _sb_stack(do_ref[pl.ds(q0, bq), :], lo)
            lsb = ls_ref[pl.ds(q0, bq), :]
            total = jnp.concatenate([lsb[:, 0:1], lsb[:, SB_DH:SB_DH + 1]], axis=0)
            nkb = (q0 + bq - 1) // bk + 1

            def kblock(kj, carry):
                dq_acc, a_pre, r_pre = carry
                k0 = pl.multiple_of(kj * bk, bk)
                ks = pl.ds(k0, bk)
                valid = rel < q0 - k0
                kb = kn_ref[ks, :]
                vb = v_ref[ks, :]
                logsig, log1m = _sb_tile(qm, kb, valid, scale)
                after = (total - a_pre) - _dot_x2(log1m, upto)
                wts = jnp.where(valid, jnp.exp(logsig + after), 0.0)
                dw = lax.dot_general(dom, vb, _NT, preferred_element_type=F32)
                ee = dw * wts
                before = r_pre + _dot_x2(ee, before_m)
                beta = jnp.exp(logsig)
                dz = jnp.where(valid, ee * (1.0 - beta) - beta * before, 0.0)
                dzb = _bf(dz * scale)
                dq_acc = dq_acc + jnp.dot(dzb, kb, preferred_element_type=F32)
                dp_ref[1, ks, :] += lax.dot_general(dzb, qm, _TN, preferred_element_type=F32)
                dp_ref[2, ks, :] += lax.dot_general(_bf(wts), dom, _TN, preferred_element_type=F32)
                return (dq_acc, a_pre + jnp.sum(log1m, axis=-1, keepdims=True),
                        r_pre + jnp.sum(ee, axis=-1, keepdims=True))

            dq2, _, _ = lax.fori_loop(0, nkb, kblock, (jnp.zeros((2 * bq, LANES), F32),
                                                       jnp.zeros((2 * bq, 1), F32), jnp.zeros((2 * bq, 1), F32)))
            dp_ref[0, pl.ds(q0, bq), :] = jnp.where(lo, dq2[:bq], dq2[bq:])
            return c

        lax.fori_loop(0, T // bq, qblock, 0)

        dgq_ref[...] = jnp.zeros_like(dgq_ref)
        dgk_ref[...] = jnp.zeros_like(dgk_ref)

        def epilogue(i, c):
            r0 = pl.multiple_of(i * rows, rows)
            sl = pl.ds(r0, rows)
            for part, g_ref, dg_ref in ((0, gq_ref, dgq_ref), (1, gk_ref, dgk_ref)):
                _, xh, r = _sb_headnorm(p_ref[part, sl, :], g_ref[...], lo)
                dn = dp_ref[part, sl, :]
                dxh = dn * g_ref[...]
                prod = dxh * xh
                m_lo = jnp.sum(jnp.where(lo, prod, 0.0), axis=-1, keepdims=True)
                m_hi = jnp.sum(jnp.where(lo, 0.0, prod), axis=-1, keepdims=True)
                m = jnp.where(lo, m_lo, m_hi) * (1.0 / SB_DH)
                dp_ref[part, sl, :] = r * (dxh - xh * m)
                dg_ref[...] += jnp.sum(dn * xh, axis=0, keepdims=True)
            return c

        lax.fori_loop(0, T // rows, epilogue, 0)

    colblk = pl.BlockSpec((T, LANES), lambda j: (0, j))
    vec = pl.BlockSpec((1, LANES), lambda j: (0, 0))
    part = pl.BlockSpec((4, T, LANES), lambda j: (0, 0, j))
    gvec = pl.BlockSpec((None, 1, LANES), lambda j: (j, 0, 0))
    npair = W // LANES
    return pl.pallas_call(
        body, name=name, grid=(npair,),
        in_specs=[part, vec, vec, colblk, colblk, colblk],
        out_specs=[part, gvec, gvec],
        out_shape=[jax.ShapeDtypeStruct((4, T, W), F32), jax.ShapeDtypeStruct((npair, 1, LANES), F32),
                   jax.ShapeDtypeStruct((npair, 1, LANES), F32)],
        scratch_shapes=[pltpu.VMEM((T, LANES), BF16)] * 4,
        compiler_params=pltpu.CompilerParams(dimension_semantics=("parallel",), vmem_limit_bytes=VMEM_BIG),
    )(p3, gq2, gk2, o, lsum, dog)


_NN = (((1,), (0,)), ((), ()))
_NT = (((1,), (1,)), ((), ()))
_TN = (((0,), (0,)), ((), ()))
DN_TB = 512
DN_AB_COL = (DN_CONV_W + DN_V_W) // LANES


def _dn_conv(x, w_ref):
    k = w_ref.shape[0]
    return sum(w_ref[i:i + 1, :] * _shift_down(x, k - 1 - i) for i in range(k))


def _dn_prep_fwd(p, conv_w, *, name):
    T = p.shape[0]
    cw = conv_w.shape[1]
    n_qk = 2 * DN_QK_W // LANES

    def body(p_ref, w_ref, o_ref):
        s = _silu(_dn_conv(p_ref[...], w_ref))
        r = lax.rsqrt(jnp.sum(s * s, axis=-1, keepdims=True) + L2_EPS)
        o_ref[...] = jnp.where(pl.program_id(0) < n_qk, s * r, s)

    colblk = pl.BlockSpec((T, LANES), lambda j: (0, j))
    return pl.pallas_call(
        body, name=name, grid=(cw // LANES,),
        in_specs=[colblk, pl.BlockSpec((DN_CONV, LANES), lambda j: (0, j))],
        out_specs=colblk, out_shape=jax.ShapeDtypeStruct((T, cw), F32),
        compiler_params=pltpu.CompilerParams(dimension_semantics=("parallel",), vmem_limit_bytes=VMEM_BIG),
    )(p, conv_w)


def _dn_chunk_tri(rows, upper):
    r = lax.broadcasted_iota(jnp.int32, (rows, rows), 0)
    c = lax.broadcasted_iota(jnp.int32, (rows, rows), 1)
    same = (r // DN_CHUNK) == (c // DN_CHUNK)
    return jnp.logical_and(same, (c >= r) if upper else (c <= r)).astype(BF16)


def _dn_lane_rows(a_log, dt_bias):
    pad = lambda v: jnp.zeros((1, LANES), F32).at[0, :DN_HEADS].set(v)
    return pad(a_log), pad(dt_bias)


def _dn_ab_parts(blk, alog_row, dtb_row):
    lane = lax.broadcasted_iota(jnp.int32, (1, LANES), 1)
    is_a = lane < DN_HEADS
    is_b = jnp.logical_and(lane >= DN_HEADS, lane < 2 * DN_HEADS)
    a_arg = jnp.where(is_a, blk + dtb_row, 0.0)
    neg_exp = jnp.where(is_a, -jnp.exp(alog_row), 0.0)
    log_a = neg_exp * _softplus(a_arg)
    beta = jnp.where(is_b, _sigmoid(blk), 0.0)
    return is_a, is_b, a_arg, neg_exp, log_a, beta


def _dn_ab_fwd(p, alog_row, dtb_row, *, name):
    T = p.shape[0]
    rows = min(DN_TB, T)

    def body(p_ref, al_ref, dt_ref, o_ref):
        _, _, _, _, log_a, beta = _dn_ab_parts(p_ref[...], al_ref[...], dt_ref[...])
        hi, mid, lo_ = _split3(log_a)
        tri = _dn_chunk_tri(rows, upper=False)
        f = lambda q: jnp.dot(tri, q, preferred_element_type=F32)
        o_ref[...] = (f(hi) + f(mid) + f(lo_)) + beta

    blk = pl.BlockSpec((rows, LANES), lambda i: (i, DN_AB_COL))
    vec = pl.BlockSpec((1, LANES), lambda i: (0, 0))
    return pl.pallas_call(
        body, name=name, grid=(T // rows,), in_specs=[blk, vec, vec],
        out_specs=pl.BlockSpec((rows, LANES), lambda i: (i, 0)),
        out_shape=jax.ShapeDtypeStruct((T, LANES), F32),
        compiler_params=pltpu.CompilerParams(dimension_semantics=("parallel",)),
    )(p, alog_row, dtb_row)


def _hp_l(a_l, b_l, dims=_NN):
    sa = [_split3(a)[:2] for a in a_l]
    sb = [_split3(b)[:2] for b in b_l]
    f = lambda p, q: lax.dot_general(p, q, dims, preferred_element_type=F32)
    hh = [f(x[0], y[0]) for x, y in zip(sa, sb)]
    hm = [f(x[0], y[1]) for x, y in zip(sa, sb)]
    mh = [f(x[1], y[0]) for x, y in zip(sa, sb)]
    return [a + (b + c) for a, b, c in zip(hh, hm, mh)]


def _dn_local(qs, k, v, g, beta, nc):
    c = DN_CHUNK
    cut = lambda x: [x[i * c:(i + 1) * c] for i in range(nc)]
    row = lax.broadcasted_iota(jnp.int32, (c, c), 0)
    col = lax.broadcasted_iota(jnp.int32, (c, c), 1)
    eye, lower, strict = row == col, row >= col, row > col
    rowid = lax.broadcasted_iota(jnp.int32, (c, 1), 0)
    eg = jnp.exp(g)
    kb = k * beta
    rhs_k = kb * eg
    g_l, k_l, kb_l, qs_l = cut(g), cut(k), cut(kb), cut(qs)
    g_row_l = [jnp.sum(jnp.where(eye, x, 0.0), axis=0, keepdims=True) for x in g_l]
    dec_l = [jnp.where(lower, jnp.exp(jnp.where(lower, x - y, 0.0)), 0.0) for x, y in zip(g_l, g_row_l)]
    kk_l = [_dot_nt(a, b) for a, b in zip(kb_l, k_l)]
    qk_l = [_dot_nt(a, b) for a, b in zip(qs_l, k_l)]
    low_l = [jnp.where(strict, a * d, 0.0) for a, d in zip(kk_l, dec_l)]
    eye_f = eye.astype(F32)
    pw_l = [-x for x in low_l]
    inv_l = [eye_f + x for x in pw_l]
    for _ in range(int(math.log2(c)) - 1):
        pw_l = _hp_l(pw_l, pw_l)
        inv_l = [a + b for a, b in zip(inv_l, _hp_l(inv_l, pw_l))]
    u_l = _hp_l(inv_l, cut(v * beta))
    w_l = _hp_l(inv_l, cut(rhs_k))
    aqk_l = [jnp.where(lower, a * d, 0.0) for a, d in zip(qk_l, dec_l)]
    g_last_l = [jnp.sum(jnp.where(rowid == c - 1, x, 0.0), axis=0, keepdims=True) for x in g_l]
    ekd_l = [jnp.exp(a - b) for a, b in zip(g_last_l, g_l)]
    kd_l = [a * b for a, b in zip(k_l, ekd_l)]
    return dict(eye=eye, lower=lower, strict=strict, dec=dec_l, k=k_l, kb=kb_l, qs=qs_l, low=low_l, inv=inv_l,
                eg=cut(eg), rhs_k=cut(rhs_k), u=u_l, w=w_l, aqk=aqk_l, g_last=g_last_l, qd=cut(qs * eg),
                ekd=ekd_l, kd=kd_l)


def _dn_head_cols(gb_blk, head):
    lane = lax.broadcasted_iota(jnp.int32, (1, LANES), 1)
    g = jnp.sum(jnp.where(lane == head, gb_blk, 0.0), axis=-1, keepdims=True)
    beta = jnp.sum(jnp.where(lane == head + DN_HEADS, gb_blk, 0.0), axis=-1, keepdims=True)
    return g, beta


def _dn_delta_fwd(qkv, gb, p, o_gain, *, name):
    T = qkv.shape[0]
    tb = min(DN_TB, T)
    nb, nc = T // tb, tb // DN_CHUNK
    H = DN_HEADS
    qscale = DN_DK ** -0.5

    def body(q_ref, k_ref, v_ref, gb_ref, gate_ref, gain_ref, o_ref, og_ref, st_ref, s_ref):
        head = pl.program_id(0)

        @pl.when(pl.program_id(1) == 0)
        def _():
            s_ref[...] = jnp.zeros_like(s_ref)

        g, beta = _dn_head_cols(gb_ref[...], head)
        t = _dn_local(q_ref[...] * qscale, k_ref[...], v_ref[...], g, beta, nc)
        s32 = s_ref[...]
        outs = []
        for i in range(nc):
            s_bf = _bf(s32)
            st_ref[i] = s_bf
            ws = _dot(t["w"][i], s_bf)
            qds = _dot(t["qd"][i], s_bf)
            vn = t["u"][i] - ws
            outs.append(qds + _dot(t["aqk"][i], vn))
            s32 = s32 * jnp.exp(t["g_last"][i]) + _dot_tn(t["kd"][i], vn)
        s_ref[...] = s32
        o = jnp.concatenate(outs, axis=0)
        o_ref[...] = o
        r = lax.rsqrt(jnp.mean(o * o, axis=-1, keepdims=True) + RMS_EPS)
        og_ref[...] = (((o * r) * gain_ref[...]) * _silu(gate_ref[...])).astype(BF16)

    qk = lambda off: pl.BlockSpec((tb, DN_DK), lambda h, i: (i, off + h))
    vblk = lambda off: pl.BlockSpec((tb, DN_DV), lambda h, i: (i, off + h))
    return pl.pallas_call(
        body, name=name, grid=(H, nb),
        in_specs=[qk(0), qk(H), vblk(2 * DN_QK_W // DN_DV), pl.BlockSpec((tb, LANES), lambda h, i: (i, 0)),
                  vblk(DN_CONV_W // DN_DV), pl.BlockSpec((1, DN_DV), lambda h, i: (0, 0))],
        out_specs=[vblk(0), vblk(0), pl.BlockSpec((None, nc, DN_DK, DN_DV), lambda h, i: (h, i, 0, 0))],
        out_shape=[jax.ShapeDtypeStruct((T, DN_V_W), F32), jax.ShapeDtypeStruct((T, DN_V_W), BF16),
                   jax.ShapeDtypeStruct((H, T // DN_CHUNK, DN_DK, DN_DV), BF16)],
        scratch_shapes=[pltpu.VMEM((DN_DK, DN_DV), F32)],
        compiler_params=pltpu.CompilerParams(dimension_semantics=("parallel", "arbitrary")),
    )(qkv, qkv, qkv, gb, p, o_gain)


def _dn_delta_bwd(qkv, gb, p, o_gain, o, states, dog, *, name):
    T = qkv.shape[0]
    tb = min(DN_TB, T)
    nb, nc = T // tb, tb // DN_CHUNK
    H = DN_HEADS
    qscale = DN_DK ** -0.5

    def body(q_ref, k_ref, v_ref, gb_ref, gate_ref, gain_ref, o_ref, st_ref, dog_ref,
             dq_ref, dk_ref, dv_ref, dgate_ref, dgb_ref, dgain_ref, ds_ref):
        head = pl.program_id(0)

        @pl.when(pl.program_id(1) == 0)
        def _():
            ds_ref[...] = jnp.zeros_like(ds_ref)

        @pl.when(jnp.logical_and(head == 0, pl.program_id(1) == 0))
        def _():
            dgain_ref[...] = jnp.zeros_like(dgain_ref)

        lane = lax.broadcasted_iota(jnp.int32, (1, LANES), 1)
        c = DN_CHUNK
        cut = lambda x: [x[i * c:(i + 1) * c] for i in range(nc)]
        cat = lambda xs: jnp.concatenate(xs, axis=0)
        rsum = lambda x: jnp.sum(x, axis=-1, keepdims=True)
        g, beta = _dn_head_cols(gb_ref[...], head)
        ov, gate, gain, dogv = o_ref[...], gate_ref[...], gain_ref[...], dog_ref[...]
        r = lax.rsqrt(jnp.mean(ov * ov, axis=-1, keepdims=True) + RMS_EPS)
        oh = ov * r
        dnrm = dogv * _silu(gate)
        dgate_ref[...] = dogv * (oh * gain) * _dsilu(gate)
        doh = dnrm * gain
        do_l = cut(r * (doh - oh * jnp.mean(doh * oh, axis=-1, keepdims=True)))
        dgain_ref[...] += jnp.sum(dnrm * oh, axis=0, keepdims=True)
        k, v = k_ref[...], v_ref[...]
        t = _dn_local(q_ref[...] * qscale, k, v, g, beta, nc)
        lower, strict, eye = t["lower"], t["strict"], t["eye"]
        s_l = [st_ref[i] for i in range(nc)]
        vn_l = [u - _dot(w, s) for u, w, s in zip(t["u"], t["w"], s_l)]
        dqd_l = [_dot_nt(a, s) for a, s in zip(do_l, s_l)]
        daqk_l = [_dot_nt(a, b) for a, b in zip(do_l, vn_l)]
        aqk_do_l = [_dot_tn(a, b) for a, b in zip(t["aqk"], do_l)]
        qd_do_l = [_dot_tn(a, b) for a, b in zip(t["qd"], do_l)]
        egl_l = [jnp.exp(x) for x in t["g_last"]]
        ds = ds_ref[...]
        dvn_l, dkd_l, dgl_l = [None] * nc, [None] * nc, [None] * nc
        for i in reversed(range(nc)):
            dvn_l[i] = aqk_do_l[i] + _dot(t["kd"][i], ds)
            dkd_l[i] = _dot_nt(vn_l[i], ds)
            dgl_l[i] = jnp.sum(rsum(ds * s_l[i].astype(F32)), axis=0, keepdims=True) * egl_l[i]
            ds = ds * egl_l[i] + qd_do_l[i] - _dot_tn(t["w"][i], dvn_l[i])
        ds_ref[...] = ds
        dw_l = [-_dot_nt(a, s) for a, s in zip(dvn_l, s_l)]
        dbv_l = _hp_l(t["inv"], dvn_l, _TN)
        dbk_l = _hp_l(t["inv"], dw_l, _TN)
        dlow_l = [-(a + b) for a, b in zip(_hp_l(dbv_l, t["u"], _NT), _hp_l(dbk_l, t["w"], _NT))]
        m_l = [jnp.where(strict, a * d, 0.0) for a, d in zip(dlow_l, t["dec"])]
        nmat_l = [jnp.where(lower, a * d, 0.0) for a, d in zip(daqk_l, t["dec"])]
        dkb_l = [_dot(m, kk) + b * e for m, kk, b, e in zip(m_l, t["k"], dbk_l, t["eg"])]
        dqs_l = [_dot(n, kk) + a * e for n, kk, a, e in zip(nmat_l, t["k"], dqd_l, t["eg"])]
        dk1_l = [_dot_tn(m, kb) for m, kb in zip(m_l, t["kb"])]
        dk2_l = [_dot_tn(n, q) for n, q in zip(nmat_l, t["qs"])]
        beta_l, v_l = cut(beta), cut(v)
        rowid = lax.broadcasted_iota(jnp.int32, (c, 1), 0)
        dk_l, dg_l, dbeta_l = [], [], []
        for i in range(nc):
            dk_l.append(dk1_l[i] + dk2_l[i] + dkd_l[i] * t["ekd"][i] + dkb_l[i] * beta_l[i])
            gmat = jnp.where(strict, dlow_l[i] * t["low"][i], 0.0) + daqk_l[i] * t["aqk"][i]
            s_kd = rsum(dkd_l[i] * t["kd"][i])
            dg = (rsum(gmat) + rsum(dqd_l[i] * t["qd"][i]) - s_kd + rsum(dbk_l[i] * t["rhs_k"][i]))
            dg_row = -jnp.sum(gmat, axis=0, keepdims=True)
            dg = dg + rsum(jnp.where(eye, dg_row, 0.0))
            dgl = dgl_l[i] + jnp.sum(s_kd, axis=0, keepdims=True)
            dg_l.append(dg + jnp.where(rowid == c - 1, dgl, 0.0))
            dbeta_l.append(rsum(dbv_l[i] * v_l[i]) + rsum(dkb_l[i] * t["k"][i]))
        dq_ref[...] = cat(dqs_l) * qscale
        dk_ref[...] = cat(dk_l)
        dv_ref[...] = cat(dbv_l) * beta
        dgb_ref[...] = (jnp.where(lane == head, cat(dg_l), 0.0)
                        + jnp.where(lane == head + DN_HEADS, cat(dbeta_l), 0.0))

    rev = lambda i: nb - 1 - i
    qk = lambda off: pl.BlockSpec((tb, DN_DK), lambda h, i: (rev(i), off + h))
    vblk = lambda off: pl.BlockSpec((tb, DN_DV), lambda h, i: (rev(i), off + h))
    gain_spec = pl.BlockSpec((1, DN_DV), lambda h, i: (0, 0))
    return pl.pallas_call(
        body, name=name, grid=(H, nb),
        in_specs=[qk(0), qk(H), vblk(2 * DN_QK_W // DN_DV), pl.BlockSpec((tb, LANES), lambda h, i: (rev(i), 0)),
                  vblk(DN_CONV_W // DN_DV), gain_spec, vblk(0),
                  pl.BlockSpec((None, nc, DN_DK, DN_DV), lambda h, i: (h, rev(i), 0, 0)), vblk(0)],
        out_specs=[qk(0), qk(0), vblk(0), vblk(DN_CONV_W // DN_DV),
                   pl.BlockSpec((None, tb, LANES), lambda h, i: (h, rev(i), 0)), gain_spec],
        out_shape=[jax.ShapeDtypeStruct((T, DN_QK_W), F32), jax.ShapeDtypeStruct((T, DN_QK_W), F32),
                   jax.ShapeDtypeStruct((T, DN_V_W), F32), jax.ShapeDtypeStruct((T, DN_IN_PAD), F32),
                   jax.ShapeDtypeStruct((H, T, LANES), F32), jax.ShapeDtypeStruct((1, DN_DV), F32)],
        scratch_shapes=[pltpu.VMEM((DN_DK, DN_DV), F32)],
        compiler_params=pltpu.CompilerParams(dimension_semantics=("arbitrary", "arbitrary")),
    )(qkv, qkv, qkv, gb, p, o_gain, o, states, dog)


def _dn_conv_bwd(p, conv_w, dq, dk, dv, dp, *, name):
    T = p.shape[0]
    cw = conv_w.shape[1]
    n_q = DN_QK_W // LANES
    n_v = DN_V_W // LANES

    def body(p_ref, w_ref, dq_ref, dk_ref, dv_ref, dp_in, dp_ref, dw_ref):
        del dp_in
        j = pl.program_id(0)
        x = p_ref[...]
        ksz = w_ref.shape[0]
        xs = [_shift_down(x, ksz - 1 - i) for i in range(ksz)]
        xc = sum(w_ref[i:i + 1, :] * xs[i] for i in range(ksz))
        s = _silu(xc)
        r = lax.rsqrt(jnp.sum(s * s, axis=-1, keepdims=True) + L2_EPS)
        y = s * r
        dn = jnp.where(j < n_q, dq_ref[...], dk_ref[...])
        ds_qk = r * (dn - y * jnp.sum(dn * y, axis=-1, keepdims=True))
        ds = jnp.where(j < 2 * n_q, ds_qk, dv_ref[...])
        dxc = ds * _dsilu(xc)
        dp_ref[...] = sum(w_ref[i:i + 1, :] * _shift_up(dxc, ksz - 1 - i) for i in range(ksz))
        for i in range(ksz):
            dw_ref[i:i + 1, :] = jnp.sum(dxc * xs[i], axis=0, keepdims=True)

    colblk = pl.BlockSpec((T, LANES), lambda j: (0, j))
    wblk = pl.BlockSpec((DN_CONV, LANES), lambda j: (0, j))
    return pl.pallas_call(
        body, name=name, grid=(cw // LANES,),
        in_specs=[colblk, wblk,
                  pl.BlockSpec((T, LANES), lambda j: (0, jnp.minimum(j, n_q - 1))),
                  pl.BlockSpec((T, LANES), lambda j: (0, jnp.clip(j - n_q, 0, n_q - 1))),
                  pl.BlockSpec((T, LANES), lambda j: (0, jnp.clip(j - 2 * n_q, 0, n_v - 1))),
                  pl.BlockSpec(memory_space=pl.ANY)],
        out_specs=[colblk, wblk],
        out_shape=[jax.ShapeDtypeStruct(dp.shape, F32), jax.ShapeDtypeStruct((DN_CONV, cw), F32)],
        input_output_aliases={5: 0},
        compiler_params=pltpu.CompilerParams(dimension_semantics=("parallel",), vmem_limit_bytes=VMEM_BIG),
    )(p, conv_w, dq, dk, dv, dp)


def _dn_ab_bwd(p, alog_row, dtb_row, dgb, dp, *, name):
    T = p.shape[0]
    rows = min(DN_TB, T)
    H = DN_HEADS

    def body(p_ref, al_ref, dt_ref, dgb_ref, dp_in, dp_ref, dal_ref, ddt_ref):
        del dp_in

        @pl.when(pl.program_id(0) == 0)
        def _():
            dal_ref[...] = jnp.zeros_like(dal_ref)
            ddt_ref[...] = jnp.zeros_like(ddt_ref)

        blk = p_ref[...]
        is_a, is_b, a_arg, neg_exp, log_a, beta = _dn_ab_parts(blk, al_ref[...], dt_ref[...])
        d = dgb_ref[0]
        for hh in range(1, H):
            d = d + dgb_ref[hh]
        hi, mid, lo_ = _split3(jnp.where(is_a, d, 0.0))
        tri = _dn_chunk_tri(rows, upper=True)
        f = lambda q: jnp.dot(tri, q, preferred_element_type=F32)
        dlog_a = f(hi) + f(mid) + f(lo_)
        da_in = dlog_a * neg_exp * _sigmoid(a_arg)
        db_in = jnp.where(is_b, d, 0.0) * beta * (1.0 - beta)
        dp_ref[...] = jnp.where(is_a, da_in, 0.0) + db_in
        dal_ref[...] += jnp.sum(dlog_a * log_a, axis=0, keepdims=True)
        ddt_ref[...] += jnp.sum(jnp.where(is_a, da_in, 0.0), axis=0, keepdims=True)

    blk = pl.BlockSpec((rows, LANES), lambda i: (i, DN_AB_COL))
    vec = pl.BlockSpec((1, LANES), lambda i: (0, 0))
    return pl.pallas_call(
        body, name=name, grid=(T // rows,),
        in_specs=[blk, vec, vec, pl.BlockSpec((H, rows, LANES), lambda i: (0, i, 0)),
                  pl.BlockSpec(memory_space=pl.ANY)],
        out_specs=[blk, vec, vec],
        out_shape=[jax.ShapeDtypeStruct(dp.shape, F32), jax.ShapeDtypeStruct((1, LANES), F32),
                   jax.ShapeDtypeStruct((1, LANES), F32)],
        input_output_aliases={4: 0},
        compiler_params=pltpu.CompilerParams(dimension_semantics=("arbitrary",)),
    )(p, alog_row, dtb_row, dgb, dp)


def _dn_layer_fwd(x, ng, w_in, conv_w, a_log, dt_bias, o_gain, w_out, tag):
    alog_row, dtb_row = _dn_lane_rows(a_log, dt_bias)
    gain = o_gain.reshape(1, DN_DV)
    h = _rmsnorm_fwd(x, ng, name=f"{tag}_norm")
    p = _matmul(h, w_in, mode="nn", name=f"{tag}_inproj")
    qkv = _dn_prep_fwd(p, conv_w, name=f"{tag}_prep")
    gb = _dn_ab_fwd(p, alog_row, dtb_row, name=f"{tag}_ab")
    o, og, states = _dn_delta_fwd(qkv, gb, p, gain, name=f"{tag}_delta")
    x_new = _matmul(og, w_out, mode="nn", res=x, name=f"{tag}_outproj")
    return x_new, (h, p, qkv, gb, o, og, states)


def _dn_layer_bwd(dx, x, ng, w_in, conv_w, a_log, dt_bias, o_gain, w_out, saved, tag):
    h, p, qkv, gb, o, og, states = saved
    alog_row, dtb_row = _dn_lane_rows(a_log, dt_bias)
    gain = o_gain.reshape(1, DN_DV)
    d_wout = _matmul(og, dx, mode="tn", name=f"{tag}_dwout")
    dog = _matmul(dx, w_out, mode="nt", name=f"{tag}_dog")
    dq, dk, dv, dp, dgb, dgain = _dn_delta_bwd(qkv, gb, p, gain, o, states, dog, name=f"{tag}_deltabwd")
    dp, dconv = _dn_conv_bwd(p, conv_w, dq, dk, dv, dp, name=f"{tag}_convbwd")
    dp, dal, ddt = _dn_ab_bwd(p, alog_row, dtb_row, dgb, dp, name=f"{tag}_abbwd")
    d_win = _matmul(h, dp, mode="tn", name=f"{tag}_dwin")
    dh = _matmul(dp, w_in, mode="nt", name=f"{tag}_dh")
    dx_prev, dng = _rmsnorm_bwd(x, ng, dh, dx, name=f"{tag}_normbwd")
    return dx_prev, dng, d_win, dconv, dal[0, :DN_HEADS], ddt[0, :DN_HEADS], dgain[0], d_wout


def _sb_gains(g):
    return jnp.concatenate([g, g]).reshape(1, LANES)


def _sb_layer_fwd(x, ng, w_in, gq, gk, w_out, tag):
    h = _rmsnorm_fwd(x, ng, name=f"{tag}_norm")
    p3 = _matmul(h, w_in, mode="nn", out_parts=4, name=f"{tag}_inproj")
    og, o, lsum = _sb_attn_fwd(p3, _sb_gains(gq), _sb_gains(gk), name=f"{tag}_attn")
    x_new = _matmul(og, w_out, mode="nn", res=x, name=f"{tag}_outproj")
    return x_new, (h, p3, og, o, lsum)


def _sb_layer_bwd(dx, x, ng, w_in, gq, gk, w_out, saved, tag):
    h, p3, og, o, lsum = saved
    d_wout = _matmul(og, dx, mode="tn", name=f"{tag}_dwout")
    dog = _matmul(dx, w_out, mode="nt", name=f"{tag}_dog")
    dp3, dgq, dgk = _sb_attn_bwd(p3, _sb_gains(gq), _sb_gains(gk), o, lsum, dog, name=f"{tag}_attnbwd")
    fold = lambda d: jnp.sum(d.reshape(-1, SB_DH), axis=0)
    d_win = _matmul(h, dp3, mode="tn", b_parts=4, name=f"{tag}_dwin")
    dh = _matmul(dp3, w_in, mode="nt", a_parts=4, name=f"{tag}_dh")
    dx_prev, dng = _rmsnorm_bwd(x, ng, dh, dx, name=f"{tag}_normbwd")
    return dx_prev, dng, d_win, fold(dgq), fold(dgk), d_wout


N_CHIPS = 4
HBM = pl.BlockSpec(memory_space=pl.ANY)


def _mesh_pos():
    return lax.axis_index("x"), lax.axis_index("y"), lax.axis_index("c")


def _other_chips(x, y):
    return [(1 - x, y), (x, 1 - y), (1 - x, 1 - y)]


def _chip_exchange(srcs, *, send_slot_is_dest, copy_own, name):
    n = len(srcs)

    def body(*refs):
        src_refs, out_refs = refs[:n], refs[n:2 * n]
        send_sems, recv_sems, local_sems = refs[2 * n:]
        x, y, c = _mesh_pos()
        me = 2 * x + y
        chips = _other_chips(x, y)
        local = []
        for a in range(n):
            if not copy_own[a]:
                continue
            own = src_refs[a].at[me] if send_slot_is_dest else src_refs[a]
            local.append(pltpu.make_async_copy(own, out_refs[a].at[me], local_sems.at[a]))
        for cp in local:
            cp.start()

        def copy(a, k, landing_slot):
            px, py = chips[k]
            src = src_refs[a].at[2 * px + py] if send_slot_is_dest else src_refs[a]
            return pltpu.make_async_remote_copy(
                src_ref=src, dst_ref=out_refs[a].at[landing_slot],
                send_sem=send_sems.at[a * 3 + k], recv_sem=recv_sems.at[a * 3 + k],
                device_id=(px, py, c), device_id_type=MESH)

        sends = [copy(a, k, me) for a in range(n) for k in range(3)]
        for cp in sends:
            cp.start()
        for a in range(n):
            for k in range(3):
                px, py = chips[k]
                copy(a, k, 2 * px + py).wait_recv()
        for cp in sends:
            cp.wait_send()
        for cp in local:
            cp.wait()

    outs = []
    for s in srcs:
        shape = s.shape if send_slot_is_dest else (N_CHIPS,) + s.shape
        outs.append(jax.ShapeDtypeStruct(shape, s.dtype))
    return pl.pallas_call(
        body, name=name, in_specs=[HBM] * n, out_specs=[HBM] * n, out_shape=outs,
        scratch_shapes=[pltpu.SemaphoreType.DMA((3 * n,)), pltpu.SemaphoreType.DMA((3 * n,)),
                        pltpu.SemaphoreType.DMA((n,))],
    )(*srcs)


def _sibling_exchange(srcs, *, name):
    n = len(srcs)

    def body(*refs):
        src_refs, out_refs = refs[:n], refs[n:2 * n]
        send_sems, recv_sems = refs[2 * n:]
        x, y, c = _mesh_pos()
        copies = [pltpu.make_async_remote_copy(
            src_ref=src_refs[a], dst_ref=out_refs[a], send_sem=send_sems.at[a], recv_sem=recv_sems.at[a],
            device_id=(x, y, 1 - c), device_id_type=MESH) for a in range(n)]
        for cp in copies:
            cp.start()
        for cp in copies:
            cp.wait()

    return pl.pallas_call(
        body, name=name, in_specs=[HBM] * n, out_specs=[HBM] * n,
        out_shape=[jax.ShapeDtypeStruct(s.shape, s.dtype) for s in srcs],
        scratch_shapes=[pltpu.SemaphoreType.DMA((n,)), pltpu.SemaphoreType.DMA((n,))],
    )(*srcs)


def _sum_big(own4, recv4, me, *, name):
    _, R, C = own4.shape
    tr = _pick(R, (512, 256, 128, 64, 32, 16, 8))

    def body(me_ref, own_ref, r1_ref, r2_ref, r3_ref, o_ref):
        del me_ref
        o_ref[...] = ((own_ref[...] + r1_ref[...].astype(F32)) + r2_ref[...].astype(F32)) + r3_ref[...].astype(F32)

    slot = lambda d: pl.BlockSpec((None, tr, C), lambda i, me_ref: ((me_ref[0] + d) % N_CHIPS, i, 0))
    return pl.pallas_call(
        body, name=name,
        grid_spec=pltpu.PrefetchScalarGridSpec(
            num_scalar_prefetch=1, grid=(R // tr,),
            in_specs=[slot(0), slot(1), slot(2), slot(3)],
            out_specs=pl.BlockSpec((tr, C), lambda i, me_ref: (i, 0))),
        out_shape=jax.ShapeDtypeStruct((R, C), F32),
        compiler_params=pltpu.CompilerParams(dimension_semantics=("parallel",)),
    )(me, own4, recv4, recv4, recv4)


def _sum_small(recv4, *, name):
    _, R, C = recv4.shape

    def body(r_ref, o_ref):
        o_ref[...] = ((r_ref[0] + r_ref[1]) + r_ref[2]) + r_ref[3]

    return pl.pallas_call(body, name=name, out_shape=jax.ShapeDtypeStruct((R, C), F32))(recv4)


def _add(a, b, *, name):
    R, C = a.shape
    tr = _pick(R, (512, 256, 128, 64, 32, 16, 8))
    blk = pl.BlockSpec((tr, C), lambda i: (i, 0))

    def body(a_ref, b_ref, o_ref):
        o_ref[...] = a_ref[...] + b_ref[...]

    return pl.pallas_call(body, name=name, grid=(R // tr,), in_specs=[blk, blk], out_specs=blk,
                          out_shape=jax.ShapeDtypeStruct((R, C), F32),
                          compiler_params=pltpu.CompilerParams(dimension_semantics=("parallel",)))(a, b)


def _adamw(w, g, m, v, *, name):
    shape = w.shape
    C = shape[-1]
    R = w.size // C
    two = lambda a: a.reshape(R, C)
    tr = _pick(R, (256, 128, 64, 32, 16, 8)) if R % 8 == 0 and R > 8 else R
    blk = pl.BlockSpec((tr, C), lambda i: (i, 0))

    def body(w_ref, g_ref, m_ref, v_ref, d_ref, nm_ref, nv_ref):
        gv = g_ref[...]
        nm = ADAM_B1 * m_ref[...] + (1.0 - ADAM_B1) * gv
        nv = ADAM_B2 * v_ref[...] + (1.0 - ADAM_B2) * (gv * gv)
        m_hat = nm / (1.0 - ADAM_B1 ** ADAM_STEP)
        v_hat = nv / (1.0 - ADAM_B2 ** ADAM_STEP)
        d_ref[...] = -ADAM_LR * (m_hat / (jnp.sqrt(v_hat) + ADAM_EPS) + ADAM_WD * w_ref[...])
        nm_ref[...] = nm
        nv_ref[...] = nv

    out = jax.ShapeDtypeStruct((R, C), F32)
    d, nm, nv = pl.pallas_call(
        body, name=name, grid=(R // tr,), in_specs=[blk] * 4, out_specs=[blk] * 3, out_shape=[out] * 3,
        compiler_params=pltpu.CompilerParams(dimension_semantics=("parallel",)),
    )(two(w), two(g), two(m), two(v))
    return d.reshape(shape), nm.reshape(shape), nv.reshape(shape)


PACK_COLS = 1024
BIG = (("dn_w_in", (2, 1024, 1540), 2), ("dn_w_out", (2, 512, 1024), 1), ("sb_w_in", (1, 1024, 1024), 2),
       ("sb_w_out", (1, 256, 1024), 1), ("sc_w_in", (1, 1024, 2048), 2), ("sc_w_out", (1, 512, 1024), 1))
SMALL = (("dn_conv_w", (2, 4, 1024), 2), ("dn_o_norm_g", (2, 64), 1), ("sc_conv_w", (1, 3, 512), 2))
REPL = (("norm_g", (4, 1024)), ("dn_a_log", (2, 8)), ("dn_dt_bias", (2, 8)), ("sb_q_norm_g", (1, 64)),
        ("sb_k_norm_g", (1, 64)))


def _pack(arrays, cols, lead=()):
    flat = jnp.concatenate([a.reshape(lead + (-1,)) for a in arrays], axis=-1)
    n = flat.shape[-1]
    rows = -(-n // cols)
    unit = 512 if rows > 512 else 8
    rows = -(-rows // unit) * unit
    flat = jnp.pad(flat, [(0, 0)] * len(lead) + [(0, rows * cols - n)])
    return flat.reshape(lead + (rows, cols))


def _unpack(buf, table, lead=()):
    flat = buf.reshape(lead + (-1,))
    out, off = {}, 0
    for entry in table:
        name, shape = entry[0], entry[1]
        n = math.prod(shape)
        out[name] = flat[..., off:off + n].reshape(lead + shape)
        off += n
    return out


def _join(shards, axis):
    return jnp.concatenate([shards[j] for j in range(N_CHIPS)], axis=axis)


def _split(full, axis):
    return jnp.stack(jnp.split(full, N_CHIPS, axis=axis), axis=0)


def kernel(x, norm_g, dn_w_in, dn_conv_w, dn_a_log, dn_dt_bias, dn_o_norm_g, dn_w_out, sb_w_in, sb_q_norm_g, sb_k_norm_g, sb_w_out, sc_w_in, sc_conv_w, sc_w_out, loss_target, m_norm_g, m_dn_w_in, m_dn_conv_w, m_dn_a_log, m_dn_dt_bias, m_dn_o_norm_g, m_dn_w_out, m_sb_w_in, m_sb_q_norm_g, m_sb_k_norm_g, m_sb_w_out, m_sc_w_in, m_sc_conv_w, m_sc_w_out, v_norm_g, v_dn_w_in, v_dn_conv_w, v_dn_a_log, v_dn_dt_bias, v_dn_o_norm_g, v_dn_w_out, v_sb_w_in, v_sb_q_norm_g, v_sb_k_norm_g, v_sb_w_out, v_sc_w_in, v_sc_conv_w, v_sc_w_out):
    weights = dict(norm_g=norm_g, dn_w_in=dn_w_in, dn_conv_w=dn_conv_w, dn_a_log=dn_a_log, dn_dt_bias=dn_dt_bias,
                   dn_o_norm_g=dn_o_norm_g, dn_w_out=dn_w_out, sb_w_in=sb_w_in, sb_q_norm_g=sb_q_norm_g,
                   sb_k_norm_g=sb_k_norm_g, sb_w_out=sb_w_out, sc_w_in=sc_w_in, sc_conv_w=sc_conv_w, sc_w_out=sc_w_out)
    m_in = dict(norm_g=m_norm_g, dn_w_in=m_dn_w_in, dn_conv_w=m_dn_conv_w, dn_a_log=m_dn_a_log,
                dn_dt_bias=m_dn_dt_bias, dn_o_norm_g=m_dn_o_norm_g, dn_w_out=m_dn_w_out, sb_w_in=m_sb_w_in,
                sb_q_norm_g=m_sb_q_norm_g, sb_k_norm_g=m_sb_k_norm_g, sb_w_out=m_sb_w_out, sc_w_in=m_sc_w_in,
                sc_conv_w=m_sc_conv_w, sc_w_out=m_sc_w_out)
    v_in = dict(norm_g=v_norm_g, dn_w_in=v_dn_w_in, dn_conv_w=v_dn_conv_w, dn_a_log=v_dn_a_log,
                dn_dt_bias=v_dn_dt_bias, dn_o_norm_g=v_dn_o_norm_g, dn_w_out=v_dn_w_out, sb_w_in=v_sb_w_in,
                sb_q_norm_g=v_sb_q_norm_g, sb_k_norm_g=v_sb_k_norm_g, sb_w_out=v_sb_w_out, sc_w_in=v_sc_w_in,
                sc_conv_w=v_sc_conv_w, sc_w_out=v_sc_w_out)
    order = list(weights)
    xi, yi, _ = _mesh_pos()
    me = (2 * xi + yi).astype(jnp.int32).reshape(1)

    big = _pack([weights[n].astype(BF16) for n, _, _ in BIG], PACK_COLS)
    small = _pack([weights[n] for n, _, _ in SMALL], LANES)
    big4, small4 = _chip_exchange([big, small], send_slot_is_dest=False, copy_own=(True, True),
                                  name="gather_weights")
    full = {n: _join(a, ax) for (n, _, ax), a in zip(BIG, _unpack(big4, BIG, (N_CHIPS,)).values())}
    full.update({n: _join(a, ax) for (n, _, ax), a in zip(SMALL, _unpack(small4, SMALL, (N_CHIPS,)).values())})
    dn_w_in_pad = jnp.pad(full["dn_w_in"], ((0, 0), (0, 0), (0, DN_IN_PAD - DN_IN)))

    def dn_args(j):
        return (dn_w_in_pad[j], full["dn_conv_w"][j], dn_a_log[j], dn_dt_bias[j], full["dn_o_norm_g"][j],
                full["dn_w_out"][j])

    sb_args = (full["sb_w_in"][0], sb_q_norm_g[0], sb_k_norm_g[0], full["sb_w_out"][0])
    sc_args = (full["sc_w_in"][0], full["sc_conv_w"][0], full["sc_w_out"][0])

    x0 = x[0]
    x1, s0 = _dn_layer_fwd(x0, norm_g[0], *dn_args(0), "l0")
    x2, s1 = _sb_layer_fwd(x1, norm_g[1], *sb_args, "l1")
    x3, s2 = _sc_layer_fwd(x2, norm_g[2], *sc_args, "l2")
    x4, s3 = _dn_layer_fwd(x3, norm_g[3], *dn_args(1), "l3")
    dy, loss_local = _loss_head(x4, loss_target[0], name="loss_head")
    loss = lax.psum(loss_local[0, 0], ("x", "y", "c"))

    dx3, dng3, dwin3, dconv3, dal3, ddt3, dgain3, dwout3 = _dn_layer_bwd(dy, x3, norm_g[3], *dn_args(1), s3, "l3")
    dx2, dng2, dwin2, dconv2, dwout2 = _sc_layer_bwd(dx3, x2, norm_g[2], *sc_args, s2, "l2")
    dx1, dng1, dwin1, dgq, dgk, dwout1 = _sb_layer_bwd(dx2, x1, norm_g[1], *sb_args, s1, "l1")
    dx0, dng0, dwin0, dconv0, dal0, ddt0, dgain0, dwout0 = _dn_layer_bwd(dx1, x0, norm_g[0], *dn_args(0), s0, "l0")

    grads = dict(
        norm_g=jnp.concatenate([dng0, dng1, dng2, dng3], axis=0),
        dn_w_in=jnp.stack([dwin0[:, :DN_IN], dwin3[:, :DN_IN]]), dn_conv_w=jnp.stack([dconv0, dconv3]),
        dn_a_log=jnp.stack([dal0, dal3]), dn_dt_bias=jnp.stack([ddt0, ddt3]),
        dn_o_norm_g=jnp.stack([dgain0, dgain3]), dn_w_out=jnp.stack([dwout0, dwout3]),
        sb_w_in=dwin1[None], sb_q_norm_g=dgq[None], sb_k_norm_g=dgk[None], sb_w_out=dwout1[None],
        sc_w_in=dwin2[None], sc_conv_w=dconv2[None], sc_w_out=dwout2[None])

    gbig = _pack([_split(grads[n], ax) for n, _, ax in BIG], PACK_COLS, (N_CHIPS,))
    repl = [jnp.broadcast_to(grads[n][None], (N_CHIPS,) + s) for n, s in REPL]
    gsmall = _pack([_split(grads[n], ax) for n, _, ax in SMALL] + repl, LANES, (N_CHIPS,))
    rbig, rsmall = _chip_exchange([gbig.astype(BF16), gsmall], send_slot_is_dest=True, copy_own=(False, True),
                                  name="scatter_grads")
    pbig = _sum_big(gbig, rbig, me, name="sum_chips_big")
    psmall = _sum_small(rsmall, name="sum_chips_small")
    qbig, qsmall = _sibling_exchange([pbig, psmall], name="swap_cores")
    tbig = _add(pbig, qbig, name="sum_cores_big")
    tsmall = _add(psmall, qsmall, name="sum_cores_small")
    g_out = _unpack(tbig, BIG)
    g_out.update(_unpack(tsmall, SMALL + REPL))

    upd = {n: _adamw(weights[n], g_out[n], m_in[n], v_in[n], name=f"adamw_{n}") for n in order}
    return (loss, dx0[None], *[g_out[n] for n in order], *[upd[n][0] for n in order],
            *[upd[n][1] for n in order], *[upd[n][2] for n in order])
```

```python
import math

import jax
import jax.numpy as jnp
from jax import lax
from jax.experimental import pallas as pl
from jax.experimental.pallas import tpu as pltpu

F32 = jnp.float32
BF16 = jnp.bfloat16
MESH = pl.DeviceIdType.MESH

RMS_EPS = 1e-6
L2_EPS = 1e-6
LANES = 128
VMEM_BIG = 60 * 1024 * 1024
MM_VMEM = 36 * 1024 * 1024

DN_HEADS, DN_DK, DN_DV, DN_CHUNK, DN_CONV = 8, 128, 256, 64, 4
DN_QK_W = DN_HEADS * DN_DK
DN_V_W = DN_HEADS * DN_DV
DN_CONV_W = 2 * DN_QK_W + DN_V_W
DN_IN = DN_CONV_W + DN_V_W + 2 * DN_HEADS
DN_IN_PAD = DN_CONV_W + DN_V_W + LANES
SB_DH = 64
SC_CONV = 3

ADAM_LR, ADAM_B1, ADAM_B2, ADAM_EPS, ADAM_WD, ADAM_STEP = 0.001, 0.9, 0.999, 1e-08, 0.01, 10


def _pick(n, cands):
    for c in cands:
        if n % c == 0:
            return c
    raise ValueError(f"no tile for {n} in {cands}")


def _bf(x):
    return x.astype(BF16)


def _dot(a, b):
    return jnp.dot(_bf(a), _bf(b), preferred_element_type=F32)


def _dot_nt(a, b):
    return lax.dot_general(_bf(a), _bf(b), (((1,), (1,)), ((), ())), preferred_element_type=F32)


def _dot_tn(a, b):
    return lax.dot_general(_bf(a), _bf(b), (((0,), (0,)), ((), ())), preferred_element_type=F32)


def _split3(a):
    hi = _bf(a)
    r = a - hi.astype(F32)
    mid = _bf(r)
    lo = _bf(r - mid.astype(F32))
    return hi, mid, lo


def _sigmoid(x):
    return 1.0 / (1.0 + jnp.exp(-x))


def _silu(x):
    return x * _sigmoid(x)


def _dsilu(x):
    s = _sigmoid(x)
    return s * (1.0 + x * (1.0 - s))


def _softplus(x):
    return jnp.maximum(x, 0.0) + jnp.log(1.0 + jnp.exp(-jnp.abs(x)))


def _shift_down(z, k):
    if k == 0:
        return z
    row = lax.broadcasted_iota(jnp.int32, z.shape, 0)
    return jnp.where(row >= k, pltpu.roll(z, k, 0), 0.0)


def _shift_up(z, k):
    if k == 0:
        return z
    n = z.shape[0]
    row = lax.broadcasted_iota(jnp.int32, z.shape, 0)
    return jnp.where(row < n - k, pltpu.roll(z, n - k, 0), 0.0)


def _matmul(a, b, *, mode, name, res=None, a_parts=1, b_parts=1, out_parts=1, out_dtype=F32):
    def dims2(x, parts):
        if parts == 1:
            return x.shape
        assert x.shape[0] == parts
        return (x.shape[1], x.shape[2] * parts)

    ash, bsh = dims2(a, a_parts), dims2(b, b_parts)
    if mode == "nn":
        (M, K), (K2, N) = ash, bsh
        dn = (((1,), (0,)), ((), ()))
    elif mode == "nt":
        (M, K), (N, K2) = ash, bsh
        dn = (((1,), (1,)), ((), ()))
    else:
        (K, M), (K2, N) = ash, bsh
        dn = (((0,), (0,)), ((), ()))
    assert K == K2, (ash, bsh, mode)
    tm = _pick(M, (512, 256, 128, 64, 32, 16, 8))
    n_unit = N // out_parts if mode == "nn" else (N // b_parts if mode == "tn" else N)
    k_unit = K // a_parts if mode in ("nn", "nt") else K
    tn, tk = min(
        ((n, k) for n in (2048, 1792, 1024, 896, 768, 512, 384, 256, 128) if n_unit % n == 0
         for k in (2048, 1792, 1024, 896, 512, 256, 128) if k_unit % k == 0
         if 2 * (tm * k * a.dtype.itemsize + k * n * b.dtype.itemsize + 2 * tm * n * 4) + tm * n * 4 <= MM_VMEM),
        key=lambda nk_: (-nk_[0] * nk_[1], -nk_[1]))
    nk = K // tk
    grid = (M // tm, N // tn, nk)

    def spec(parts, rows_are, cols_are, tr, tc, width):
        per = width // parts // tc
        if parts == 1:
            return pl.BlockSpec((tr, tc), lambda i, j, k: ((i, j, k)[rows_are], (i, j, k)[cols_are]))
        return pl.BlockSpec((None, tr, tc), lambda i, j, k: ((i, j, k)[cols_are] // per, (i, j, k)[rows_are],
                                                             (i, j, k)[cols_are] % per))

    if mode == "nn":
        a_spec = spec(a_parts, 0, 2, tm, tk, K)
        b_spec = spec(b_parts, 2, 1, tk, tn, N)
    elif mode == "nt":
        a_spec = spec(a_parts, 0, 2, tm, tk, K)
        b_spec = spec(b_parts, 1, 2, tn, tk, K)
    else:
        a_spec = spec(a_parts, 2, 0, tk, tm, M)
        b_spec = spec(b_parts, 2, 1, tk, tn, N)
    o_spec = spec(out_parts, 0, 1, tm, tn, N)
    in_specs = [a_spec, b_spec]
    operands = [a, b]
    if res is not None:
        in_specs.append(pl.BlockSpec((tm, tn), lambda i, j, k: (i, j)))
        operands.append(res)

    def finish(refs, r):
        if res is not None:
            r = refs[2][...] + r
        refs[-2 if nk > 1 else -1][...] = r.astype(out_dtype)

    def body(*refs):
        part = lax.dot_general(_bf(refs[0][...]), _bf(refs[1][...]), dn, preferred_element_type=F32)
        if nk == 1:
            finish(refs, part)
            return
        acc_ref = refs[-1]
        k = pl.program_id(2)

        @pl.when(k == 0)
        def _():
            acc_ref[...] = part

        @pl.when(jnp.logical_and(k > 0, k < nk - 1))
        def _():
            acc_ref[...] += part

        @pl.when(k == nk - 1)
        def _():
            finish(refs, acc_ref[...] + part)

    out_shape = (M, N) if out_parts == 1 else (out_parts, M, N // out_parts)
    return pl.pallas_call(
        body, name=name, grid=grid, in_specs=in_specs, out_specs=o_spec,
        out_shape=jax.ShapeDtypeStruct(out_shape, out_dtype),
        scratch_shapes=[pltpu.VMEM((tm, tn), F32)] if nk > 1 else [],
        compiler_params=pltpu.CompilerParams(dimension_semantics=("parallel", "parallel", "arbitrary"),
                                             vmem_limit_bytes=VMEM_BIG),
    )(*operands)


def _rmsnorm_fwd(x, g, *, name):
    T, D = x.shape
    tm = _pick(T, (512, 256, 128, 64, 32, 16))

    def body(x_ref, g_ref, h_ref):
        xv = x_ref[...]
        r = lax.rsqrt(jnp.mean(xv * xv, axis=-1, keepdims=True) + RMS_EPS)
        h_ref[...] = ((xv * r) * g_ref[...]).astype(BF16)

    return pl.pallas_call(
        body, name=name, grid=(T // tm,),
        in_specs=[pl.BlockSpec((tm, D), lambda i: (i, 0)), pl.BlockSpec((1, D), lambda i: (0, 0))],
        out_specs=pl.BlockSpec((tm, D), lambda i: (i, 0)),
        out_shape=jax.ShapeDtypeStruct((T, D), BF16),
    )(x, g.reshape(1, D))


def _rmsnorm_bwd(x, g, dh, dx_in, *, name):
    T, D = x.shape
    tm = _pick(T, (512, 256, 128, 64, 32, 16))

    def body(x_ref, g_ref, dh_ref, dxin_ref, dx_ref, dg_ref):
        @pl.when(pl.program_id(0) == 0)
        def _():
            dg_ref[...] = jnp.zeros_like(dg_ref)

        xv = x_ref[...]
        r = lax.rsqrt(jnp.mean(xv * xv, axis=-1, keepdims=True) + RMS_EPS)
        xh = xv * r
        dh_v = dh_ref[...]
        dxh = dh_v * g_ref[...]
        dx_ref[...] = dxin_ref[...] + r * (dxh - xh * jnp.mean(dxh * xh, axis=-1, keepdims=True))
        dg_ref[...] += jnp.sum(dh_v * xh, axis=0, keepdims=True)

    row = pl.BlockSpec((tm, D), lambda i: (i, 0))
    vec = pl.BlockSpec((1, D), lambda i: (0, 0))
    return pl.pallas_call(
        body, name=name, grid=(T // tm,),
        in_specs=[row, vec, row, row], out_specs=[row, vec],
        out_shape=[jax.ShapeDtypeStruct((T, D), F32), jax.ShapeDtypeStruct((1, D), F32)],
        compiler_params=pltpu.CompilerParams(dimension_semantics=("arbitrary",)),
    )(x, g.reshape(1, D), dh, dx_in)


def _loss_head(y, target, *, name):
    T, D = y.shape
    tm = _pick(T, (512, 256, 128, 64, 32, 16))

    def body(y_ref, t_ref, dy_ref, l_ref):
        @pl.when(pl.program_id(0) == 0)
        def _():
            l_ref[...] = jnp.zeros_like(l_ref)

        err = y_ref[...] - t_ref[...]
        dy_ref[...] = err * (1.0 / D)
        l_ref[...] += 0.5 * jnp.sum(jnp.mean(err * err, axis=-1, keepdims=True), axis=0, keepdims=True)

    row = pl.BlockSpec((tm, D), lambda i: (i, 0))
    return pl.pallas_call(
        body, name=name, grid=(T // tm,),
        in_specs=[row, row], out_specs=[row, pl.BlockSpec((1, 1), lambda i: (0, 0))],
        out_shape=[jax.ShapeDtypeStruct((T, D), F32), jax.ShapeDtypeStruct((1, 1), F32)],
        compiler_params=pltpu.CompilerParams(dimension_semantics=("arbitrary",)),
    )(y, target)


def _sc_mid_fwd(p3, conv_w, *, name):
    _, T, W = p3.shape
    K = conv_w.shape[0]
    cw = LANES

    def body(p_ref, w_ref, o_ref):
        z = p_ref[1] * p_ref[2]
        cv = sum(w_ref[i:i + 1, :] * _shift_down(z, K - 1 - i) for i in range(K))
        o_ref[...] = ((p_ref[0] * cv) * _silu(p_ref[3])).astype(BF16)

    return pl.pallas_call(
        body, name=name, grid=(W // cw,),
        in_specs=[pl.BlockSpec((4, T, cw), lambda j: (0, 0, j)), pl.BlockSpec((K, cw), lambda j: (0, j))],
        out_specs=pl.BlockSpec((T, cw), lambda j: (0, j)),
        out_shape=jax.ShapeDtypeStruct((T, W), BF16),
        compiler_params=pltpu.CompilerParams(dimension_semantics=("parallel",), vmem_limit_bytes=VMEM_BIG),
    )(p3, conv_w)


def _sc_mid_bwd(p3, conv_w, do, *, name):
    _, T, W = p3.shape
    K = conv_w.shape[0]
    cw = LANES

    def body(p_ref, w_ref, do_ref, dp_ref, dw_ref):
        b, c, u, gate = p_ref[0], p_ref[1], p_ref[2], p_ref[3]
        z = c * u
        zs = [_shift_down(z, K - 1 - i) for i in range(K)]
        cv = sum(w_ref[i:i + 1, :] * zs[i] for i in range(K))
        y = b * cv
        dov = do_ref[...]
        dy = dov * _silu(gate)
        dp_ref[3] = dov * y * _dsilu(gate)
        dp_ref[0] = dy * cv
        dcv = dy * b
        dz = sum(w_ref[i:i + 1, :] * _shift_up(dcv, K - 1 - i) for i in range(K))
        dp_ref[1] = dz * u
        dp_ref[2] = dz * c
        for i in range(K):
            dw_ref[i:i + 1, :] = jnp.sum(dcv * zs[i], axis=0, keepdims=True)

    return pl.pallas_call(
        body, name=name, grid=(W // cw,),
        in_specs=[pl.BlockSpec((4, T, cw), lambda j: (0, 0, j)), pl.BlockSpec((K, cw), lambda j: (0, j)),
                  pl.BlockSpec((T, cw), lambda j: (0, j))],
        out_specs=[pl.BlockSpec((4, T, cw), lambda j: (0, 0, j)), pl.BlockSpec((K, cw), lambda j: (0, j))],
        out_shape=[jax.ShapeDtypeStruct((4, T, W), F32), jax.ShapeDtypeStruct((K, W), F32)],
        compiler_params=pltpu.CompilerParams(dimension_semantics=("parallel",), vmem_limit_bytes=VMEM_BIG),
    )(p3, conv_w, do)


def _sc_layer_fwd(x, ng, w_in, conv_w, w_out, tag):
    h = _rmsnorm_fwd(x, ng, name=f"{tag}_norm")
    p3 = _matmul(h, w_in, mode="nn", out_parts=4, name=f"{tag}_inproj")
    og = _sc_mid_fwd(p3, conv_w, name=f"{tag}_mid")
    x_new = _matmul(og, w_out, mode="nn", res=x, name=f"{tag}_outproj")
    return x_new, (h, p3, og)


def _sc_layer_bwd(dx, x, ng, w_in, conv_w, w_out, saved, tag):
    h, p3, og = saved
    d_wout = _matmul(og, dx, mode="tn", name=f"{tag}_dwout")
    dog = _matmul(dx, w_out, mode="nt", name=f"{tag}_dog")
    dp3, dconv = _sc_mid_bwd(p3, conv_w, dog, name=f"{tag}_midbwd")
    d_win = _matmul(h, dp3, mode="tn", b_parts=4, name=f"{tag}_dwin")
    dh = _matmul(dp3, w_in, mode="nt", a_parts=4, name=f"{tag}_dh")
    dx_prev, dng = _rmsnorm_bwd(x, ng, dh, dx, name=f"{tag}_normbwd")
    return dx_prev, dng, d_win, dconv, d_wout


SB_BQ = 256
SB_BK = 256
SB_ROWS = 512


def _dot_x2(a, b_exact_bf16):
    hi = _bf(a)
    mid = _bf(a - hi.astype(F32))
    return (jnp.dot(hi, b_exact_bf16, preferred_element_type=F32)
            + jnp.dot(mid, b_exact_bf16, preferred_element_type=F32))


def _sb_half_mask():
    return lax.broadcasted_iota(jnp.int32, (1, LANES), 1) < SB_DH


def _sb_headnorm(x, g, lo):
    x2 = x * x
    s_lo = jnp.sum(jnp.where(lo, x2, 0.0), axis=-1, keepdims=True)
    s_hi = jnp.sum(jnp.where(lo, 0.0, x2), axis=-1, keepdims=True)
    r = lax.rsqrt(jnp.where(lo, s_lo, s_hi) * (1.0 / SB_DH) + RMS_EPS)
    xh = x * r
    return xh * g, xh, r


def _sb_stack(xb, lo):
    zero = jnp.zeros_like(xb)
    return jnp.concatenate([jnp.where(lo, xb, zero), jnp.where(lo, zero, xb)], axis=0)


def _sb_rel(bq, bk):
    row = lax.broadcasted_iota(jnp.int32, (2 * bq, bk), 0)
    col = lax.broadcasted_iota(jnp.int32, (2 * bq, bk), 1)
    return col - jnp.where(row >= bq, row - bq, row)


def _sb_tile(qm, kb, valid, scale):
    z = lax.dot_general(qm, kb, (((1,), (1,)), ((), ())), preferred_element_type=F32) * scale
    sp = _softplus(z)
    return z - sp, (-sp if valid is None else jnp.where(valid, -sp, 0.0))


def _sb_attn_fwd(p3, gq2, gk2, *, name):
    _, T, W = p3.shape
    bq, bk = min(SB_BQ, T), min(SB_BK, T)
    rows = min(SB_ROWS, T)
    scale = SB_DH ** -0.5

    def body(p_ref, gq_ref, gk_ref, og_ref, o_ref, ls_ref, qn_ref, kn_ref, v_ref):
        lo = _sb_half_mask()

        def prologue(i, c):
            r0 = pl.multiple_of(i * rows, rows)
            sl = pl.ds(r0, rows)
            qn_ref[sl, :] = _sb_headnorm(p_ref[0, sl, :], gq_ref[...], lo)[0].astype(BF16)
            kn_ref[sl, :] = _sb_headnorm(p_ref[1, sl, :], gk_ref[...], lo)[0].astype(BF16)
            v_ref[sl, :] = p_ref[2, sl, :].astype(BF16)
            return c

        lax.fori_loop(0, T // rows, prologue, 0)

        rel = _sb_rel(bq, bk)
        tri = (lax.broadcasted_iota(jnp.int32, (bk, bk), 0)
               > lax.broadcasted_iota(jnp.int32, (bk, bk), 1)).astype(BF16)

        def qblock(qi, c):
            q0 = pl.multiple_of(qi * bq, bq)
            qm = _sb_stack(qn_ref[pl.ds(q0, bq), :], lo)
            nkb = (q0 + bq - 1) // bk + 1

            def tile(k0, carry, valid):
                o_acc, a_carry = carry
                logsig, log1m = _sb_tile(qm, kn_ref[pl.ds(k0, bk), :], valid, scale)
                wts = jnp.exp(logsig + (_dot_x2(log1m, tri) + a_carry))
                if valid is not None:
                    wts = jnp.where(valid, wts, 0.0)
                o_acc = o_acc + jnp.dot(_bf(wts), v_ref[pl.ds(k0, bk), :], preferred_element_type=F32)
                return o_acc, a_carry + jnp.sum(log1m, axis=-1, keepdims=True)

            k_last = pl.multiple_of((nkb - 1) * bk, bk)
            first = tile(k_last, (jnp.zeros((2 * bq, LANES), F32), jnp.zeros((2 * bq, 1), F32)), rel < q0 - k_last)
            o2, t2 = lax.fori_loop(
                1, nkb, lambda t, cr: tile(pl.multiple_of((nkb - 1 - t) * bk, bk), cr, None), first)
            o = jnp.where(lo, o2[:bq], o2[bq:])
            o_ref[pl.ds(q0, bq), :] = o
            ls_ref[pl.ds(q0, bq), :] = jnp.where(lo, t2[:bq], t2[bq:])
            og_ref[pl.ds(q0, bq), :] = (o * _silu(p_ref[3, pl.ds(q0, bq), :])).astype(BF16)
            return c

        lax.fori_loop(0, T // bq, qblock, 0)

    colblk = pl.BlockSpec((T, LANES), lambda j: (0, j))
    vec = pl.BlockSpec((1, LANES), lambda j: (0, 0))
    return pl.pallas_call(
        body, name=name, grid=(W // LANES,),
        in_specs=[pl.BlockSpec((4, T, LANES), lambda j: (0, 0, j)), vec, vec],
        out_specs=[colblk, colblk, colblk],
        out_shape=[jax.ShapeDtypeStruct((T, W), BF16), jax.ShapeDtypeStruct((T, W), F32),
                   jax.ShapeDtypeStruct((T, W), F32)],
        scratch_shapes=[pltpu.VMEM((T, LANES), BF16)] * 3,
        compiler_params=pltpu.CompilerParams(dimension_semantics=("parallel",), vmem_limit_bytes=VMEM_BIG),
    )(p3, gq2, gk2)


def _sb_attn_bwd(p3, gq2, gk2, o, lsum, dog, *, name):
    _, T, W = p3.shape
    bq, bk = min(SB_BQ, T), min(SB_BK, T)
    rows = min(SB_ROWS, T)
    scale = SB_DH ** -0.5

    def body(p_ref, gq_ref, gk_ref, o_ref, ls_ref, dog_ref, dp_ref, dgq_ref, dgk_ref,
             qn_ref, kn_ref, v_ref, do_ref):
        lo = _sb_half_mask()

        def prologue(i, c):
            r0 = pl.multiple_of(i * rows, rows)
            sl = pl.ds(r0, rows)
            qn_ref[sl, :] = _sb_headnorm(p_ref[0, sl, :], gq_ref[...], lo)[0].astype(BF16)
            kn_ref[sl, :] = _sb_headnorm(p_ref[1, sl, :], gk_ref[...], lo)[0].astype(BF16)
            v_ref[sl, :] = p_ref[2, sl, :].astype(BF16)
            gate = p_ref[3, sl, :]
            dogv = dog_ref[sl, :]
            dp_ref[3, sl, :] = dogv * o_ref[sl, :] * _dsilu(gate)
            do_ref[sl, :] = (dogv * _silu(gate)).astype(BF16)
            zero = jnp.zeros((rows, LANES), F32)
            dp_ref[0, sl, :] = zero
            dp_ref[1, sl, :] = zero
            dp_ref[2, sl, :] = zero
            return c

        lax.fori_loop(0, T // rows, prologue, 0)

        rel = _sb_rel(bq, bk)
        rj = lax.broadcasted_iota(jnp.int32, (bk, bk), 0)
        cj = lax.broadcasted_iota(jnp.int32, (bk, bk), 1)
        upto = (rj <= cj).astype(BF16)
        before_m = (rj < cj).astype(BF16)

        def qblock(qi, c):
            q0 = pl.multiple_of(qi * bq, bq)
            qm = _sb_stack(qn_ref[pl.ds(q0, bq), :], lo)
            dom = _sb_stack(do_ref[pl.ds(q0, bq), :], lo)
            lsb = ls_ref[pl.ds(q0, bq), :]
            total = jnp.concatenate([lsb[:, 0:1], lsb[:, SB_DH:SB_DH + 1]], axis=0)
            nkb = (q0 + bq - 1) // bk + 1

            def tile(k0, carry, valid):
                dq_acc, a_pre, r_pre = carry
                ks = pl.ds(k0, bk)
                kb = kn_ref[ks, :]
                vb = v_ref[ks, :]
                logsig, log1m = _sb_tile(qm, kb, valid, scale)
                after = (total - a_pre) - _dot_x2(log1m, upto)
                wts = jnp.exp(logsig + after)
                if valid is not None:
                    wts = jnp.where(valid, wts, 0.0)
                dw = lax.dot_general(dom, vb, _NT, preferred_element_type=F32)
                ee = dw * wts
                before = r_pre + _dot_x2(ee, before_m)
                beta = jnp.exp(logsig)
                dz = ee * (1.0 - beta) - beta * before
                if valid is not None:
                    dz = jnp.where(valid, dz, 0.0)
                dzb = _bf(dz * scale)
                dq_acc = dq_acc + jnp.dot(dzb, kb, preferred_element_type=F32)
                dp_ref[1, ks, :] += lax.dot_general(dzb, qm, _TN, preferred_element_type=F32)
                dp_ref[2, ks, :] += lax.dot_general(_bf(wts), dom, _TN, preferred_element_type=F32)
                return (dq_acc, a_pre + jnp.sum(log1m, axis=-1, keepdims=True),
                        r_pre + jnp.sum(ee, axis=-1, keepdims=True))

            init = (jnp.zeros((2 * bq, LANES), F32), jnp.zeros((2 * bq, 1), F32), jnp.zeros((2 * bq, 1), F32))
            before_last = lax.fori_loop(
                0, nkb - 1, lambda kj, cr: tile(pl.multiple_of(kj * bk, bk), cr, None), init)
            k_last = pl.multiple_of((nkb - 1) * bk, bk)
            dq2, _, _ = tile(k_last, before_last, rel < q0 - k_last)
            dp_ref[0, pl.ds(q0, bq), :] = jnp.where(lo, dq2[:bq], dq2[bq:])
            return c

        lax.fori_loop(0, T // bq, qblock, 0)

        dgq_ref[...] = jnp.zeros_like(dgq_ref)
        dgk_ref[...] = jnp.zeros_like(dgk_ref)

        def epilogue(i, c):
            r0 = pl.multiple_of(i * rows, rows)
            sl = pl.ds(r0, rows)
            for part, g_ref, dg_ref in ((0, gq_ref, dgq_ref), (1, gk_ref, dgk_ref)):
                _, xh, r = _sb_headnorm(p_ref[part, sl, :], g_ref[...], lo)
                dn = dp_ref[part, sl, :]
                dxh = dn * g_ref[...]
                prod = dxh * xh
                m_lo = jnp.sum(jnp.where(lo, prod, 0.0), axis=-1, keepdims=True)
                m_hi = jnp.sum(jnp.where(lo, 0.0, prod), axis=-1, keepdims=True)
                m = jnp.where(lo, m_lo, m_hi) * (1.0 / SB_DH)
                dp_ref[part, sl, :] = r * (dxh - xh * m)
                dg_ref[...] += jnp.sum(dn * xh, axis=0, keepdims=True)
            return c

        lax.fori_loop(0, T // rows, epilogue, 0)

    colblk = pl.BlockSpec((T, LANES), lambda j: (0, j))
    vec = pl.BlockSpec((1, LANES), lambda j: (0, 0))
    part = pl.BlockSpec((4, T, LANES), lambda j: (0, 0, j))
    gvec = pl.BlockSpec((None, 1, LANES), lambda j: (j, 0, 0))
    npair = W // LANES
    return pl.pallas_call(
        body, name=name, grid=(npair,),
        in_specs=[part, vec, vec, colblk, colblk, colblk],
        out_specs=[part, gvec, gvec],
        out_shape=[jax.ShapeDtypeStruct((4, T, W), F32), jax.ShapeDtypeStruct((npair, 1, LANES), F32),
                   jax.ShapeDtypeStruct((npair, 1, LANES), F32)],
        scratch_shapes=[pltpu.VMEM((T, LANES), BF16)] * 4,
        compiler_params=pltpu.CompilerParams(dimension_semantics=("parallel",), vmem_limit_bytes=VMEM_BIG),
    )(p3, gq2, gk2, o, lsum, dog)


_NN = (((1,), (0,)), ((), ()))
_NT = (((1,), (1,)), ((), ()))
_TN = (((0,), (0,)), ((), ()))
DN_TB = 512
DN_AB_COL = (DN_CONV_W + DN_V_W) // LANES


def _dn_conv(x, w_ref):
    k = w_ref.shape[0]
    return sum(w_ref[i:i + 1, :] * _shift_down(x, k - 1 - i) for i in range(k))


def _dn_prep_fwd(p, conv_w, *, name):
    T = p.shape[0]
    cw = conv_w.shape[1]
    n_qk = 2 * DN_QK_W // LANES

    def body(p_ref, w_ref, o_ref):
        s = _silu(_dn_conv(p_ref[...], w_ref))
        r = lax.rsqrt(jnp.sum(s * s, axis=-1, keepdims=True) + L2_EPS)
        o_ref[...] = jnp.where(pl.program_id(0) < n_qk, s * r, s)

    colblk = pl.BlockSpec((T, LANES), lambda j: (0, j))
    return pl.pallas_call(
        body, name=name, grid=(cw // LANES,),
        in_specs=[colblk, pl.BlockSpec((DN_CONV, LANES), lambda j: (0, j))],
        out_specs=colblk, out_shape=jax.ShapeDtypeStruct((T, cw), F32),
        compiler_params=pltpu.CompilerParams(dimension_semantics=("parallel",), vmem_limit_bytes=VMEM_BIG),
    )(p, conv_w)


def _dn_chunk_tri(rows, upper):
    r = lax.broadcasted_iota(jnp.int32, (rows, rows), 0)
    c = lax.broadcasted_iota(jnp.int32, (rows, rows), 1)
    same = (r // DN_CHUNK) == (c // DN_CHUNK)
    return jnp.logical_and(same, (c >= r) if upper else (c <= r)).astype(BF16)


def _dn_lane_rows(a_log, dt_bias):
    pad = lambda v: jnp.zeros((1, LANES), F32).at[0, :DN_HEADS].set(v)
    return pad(a_log), pad(dt_bias)


def _dn_ab_parts(blk, alog_row, dtb_row):
    lane = lax.broadcasted_iota(jnp.int32, (1, LANES), 1)
    is_a = lane < DN_HEADS
    is_b = jnp.logical_and(lane >= DN_HEADS, lane < 2 * DN_HEADS)
    a_arg = jnp.where(is_a, blk + dtb_row, 0.0)
    neg_exp = jnp.where(is_a, -jnp.exp(alog_row), 0.0)
    log_a = neg_exp * _softplus(a_arg)
    beta = jnp.where(is_b, _sigmoid(blk), 0.0)
    return is_a, is_b, a_arg, neg_exp, log_a, beta


def _dn_ab_fwd(p, alog_row, dtb_row, *, name):
    T = p.shape[0]
    rows = min(DN_TB, T)

    def body(p_ref, al_ref, dt_ref, o_ref):
        _, _, _, _, log_a, beta = _dn_ab_parts(p_ref[...], al_ref[...], dt_ref[...])
        hi, mid, lo_ = _split3(log_a)
        tri = _dn_chunk_tri(rows, upper=False)
        f = lambda q: jnp.dot(tri, q, preferred_element_type=F32)
        o_ref[...] = (f(hi) + f(mid) + f(lo_)) + beta

    blk = pl.BlockSpec((rows, LANES), lambda i: (i, DN_AB_COL))
    vec = pl.BlockSpec((1, LANES), lambda i: (0, 0))
    return pl.pallas_call(
        body, name=name, grid=(T // rows,), in_specs=[blk, vec, vec],
        out_specs=pl.BlockSpec((rows, LANES), lambda i: (i, 0)),
        out_shape=jax.ShapeDtypeStruct((T, LANES), F32),
        compiler_params=pltpu.CompilerParams(dimension_semantics=("parallel",)),
    )(p, alog_row, dtb_row)


def _hp_l(a_l, b_l, dims=_NN):
    sa = [_split3(a)[:2] for a in a_l]
    sb = [_split3(b)[:2] for b in b_l]
    f = lambda p, q: lax.dot_general(p, q, dims, preferred_element_type=F32)
    hh = [f(x[0], y[0]) for x, y in zip(sa, sb)]
    hm = [f(x[0], y[1]) for x, y in zip(sa, sb)]
    mh = [f(x[1], y[0]) for x, y in zip(sa, sb)]
    return [a + (b + c) for a, b, c in zip(hh, hm, mh)]


def _dn_local(qs, k, v, g, beta, nc):
    c = DN_CHUNK
    cut = lambda x: [x[i * c:(i + 1) * c] for i in range(nc)]
    row = lax.broadcasted_iota(jnp.int32, (c, c), 0)
    col = lax.broadcasted_iota(jnp.int32, (c, c), 1)
    eye, lower, strict = row == col, row >= col, row > col
    rowid = lax.broadcasted_iota(jnp.int32, (c, 1), 0)
    eg = jnp.exp(g)
    kb = k * beta
    rhs_k = kb * eg
    g_l, k_l, kb_l, qs_l = cut(g), cut(k), cut(kb), cut(qs)
    g_row_l = [jnp.sum(jnp.where(eye, x, 0.0), axis=0, keepdims=True) for x in g_l]
    dec_l = [jnp.where(lower, jnp.exp(jnp.where(lower, x - y, 0.0)), 0.0) for x, y in zip(g_l, g_row_l)]
    kk_l = [_dot_nt(a, b) for a, b in zip(kb_l, k_l)]
    qk_l = [_dot_nt(a, b) for a, b in zip(qs_l, k_l)]
    low_l = [jnp.where(strict, a * d, 0.0) for a, d in zip(kk_l, dec_l)]
    eye_f = eye.astype(F32)
    pw_l = [-x for x in low_l]
    inv_l = [eye_f + x for x in pw_l]
    for _ in range(int(math.log2(c)) - 1):
        pw_l = _hp_l(pw_l, pw_l)
        inv_l = [a + b for a, b in zip(inv_l, _hp_l(inv_l, pw_l))]
    u_l = _hp_l(inv_l, cut(v * beta))
    w_l = _hp_l(inv_l, cut(rhs_k))
    aqk_l = [jnp.where(lower, a * d, 0.0) for a, d in zip(qk_l, dec_l)]
    g_last_l = [jnp.sum(jnp.where(rowid == c - 1, x, 0.0), axis=0, keepdims=True) for x in g_l]
    ekd_l = [jnp.exp(a - b) for a, b in zip(g_last_l, g_l)]
    kd_l = [a * b for a, b in zip(k_l, ekd_l)]
    return dict(eye=eye, lower=lower, strict=strict, dec=dec_l, k=k_l, kb=kb_l, qs=qs_l, low=low_l, inv=inv_l,
                eg=cut(eg), rhs_k=cut(rhs_k), u=u_l, w=w_l, aqk=aqk_l, g_last=g_last_l, qd=cut(qs * eg),
                ekd=ekd_l, kd=kd_l)


def _dn_head_cols(gb_blk, head):
    lane = lax.broadcasted_iota(jnp.int32, (1, LANES), 1)
    g = jnp.sum(jnp.where(lane == head, gb_blk, 0.0), axis=-1, keepdims=True)
    beta = jnp.sum(jnp.where(lane == head + DN_HEADS, gb_blk, 0.0), axis=-1, keepdims=True)
    return g, beta


def _dn_delta_fwd(qkv, gb, p, o_gain, *, name):
    T = qkv.shape[0]
    tb = min(DN_TB, T)
    nb, nc = T // tb, tb // DN_CHUNK
    H = DN_HEADS
    qscale = DN_DK ** -0.5

    def body(q_ref, k_ref, v_ref, gb_ref, gate_ref, gain_ref, o_ref, og_ref, st_ref, s_ref):
        head = pl.program_id(0)

        @pl.when(pl.program_id(1) == 0)
        def _():
            s_ref[...] = jnp.zeros_like(s_ref)

        g, beta = _dn_head_cols(gb_ref[...], head)
        t = _dn_local(q_ref[...] * qscale, k_ref[...], v_ref[...], g, beta, nc)
        s32 = s_ref[...]
        outs = []
        for i in range(nc):
            s_bf = _bf(s32)
            st_ref[i] = s_bf
            ws = _dot(t["w"][i], s_bf)
            qds = _dot(t["qd"][i], s_bf)
            vn = t["u"][i] - ws
            outs.append(qds + _dot(t["aqk"][i], vn))
            s32 = s32 * jnp.exp(t["g_last"][i]) + _dot_tn(t["kd"][i], vn)
        s_ref[...] = s32
        o = jnp.concatenate(outs, axis=0)
        o_ref[...] = o
        r = lax.rsqrt(jnp.mean(o * o, axis=-1, keepdims=True) + RMS_EPS)
        og_ref[...] = (((o * r) * gain_ref[...]) * _silu(gate_ref[...])).astype(BF16)

    qk = lambda off: pl.BlockSpec((tb, DN_DK), lambda h, i: (i, off + h))
    vblk = lambda off: pl.BlockSpec((tb, DN_DV), lambda h, i: (i, off + h))
    return pl.pallas_call(
        body, name=name, grid=(H, nb),
        in_specs=[qk(0), qk(H), vblk(2 * DN_QK_W // DN_DV), pl.BlockSpec((tb, LANES), lambda h, i: (i, 0)),
                  vblk(DN_CONV_W // DN_DV), pl.BlockSpec((1, DN_DV), lambda h, i: (0, 0))],
        out_specs=[vblk(0), vblk(0), pl.BlockSpec((None, nc, DN_DK, DN_DV), lambda h, i: (h, i, 0, 0))],
        out_shape=[jax.ShapeDtypeStruct((T, DN_V_W), F32), jax.ShapeDtypeStruct((T, DN_V_W), BF16),
                   jax.ShapeDtypeStruct((H, T // DN_CHUNK, DN_DK, DN_DV), BF16)],
        scratch_shapes=[pltpu.VMEM((DN_DK, DN_DV), F32)],
        compiler_params=pltpu.CompilerParams(dimension_semantics=("parallel", "arbitrary")),
    )(qkv, qkv, qkv, gb, p, o_gain)


def _dn_delta_bwd(qkv, gb, p, o_gain, o, states, dog, *, name):
    T = qkv.shape[0]
    tb = min(DN_TB, T)
    nb, nc = T // tb, tb // DN_CHUNK
    H = DN_HEADS
    qscale = DN_DK ** -0.5

    def body(q_ref, k_ref, v_ref, gb_ref, gate_ref, gain_ref, o_ref, st_ref, dog_ref,
             dq_ref, dk_ref, dv_ref, dgate_ref, dgb_ref, dgain_ref, ds_ref):
        head = pl.program_id(0)

        @pl.when(pl.program_id(1) == 0)
        def _():
            ds_ref[...] = jnp.zeros_like(ds_ref)

        @pl.when(jnp.logical_and(head == 0, pl.program_id(1) == 0))
        def _():
            dgain_ref[...] = jnp.zeros_like(dgain_ref)

        lane = lax.broadcasted_iota(jnp.int32, (1, LANES), 1)
        c = DN_CHUNK
        cut = lambda x: [x[i * c:(i + 1) * c] for i in range(nc)]
        cat = lambda xs: jnp.concatenate(xs, axis=0)
        rsum = lambda x: jnp.sum(x, axis=-1, keepdims=True)
        g, beta = _dn_head_cols(gb_ref[...], head)
        ov, gate, gain, dogv = o_ref[...], gate_ref[...], gain_ref[...], dog_ref[...]
        r = lax.rsqrt(jnp.mean(ov * ov, axis=-1, keepdims=True) + RMS_EPS)
        oh = ov * r
        dnrm = dogv * _silu(gate)
        dgate_ref[...] = dogv * (oh * gain) * _dsilu(gate)
        doh = dnrm * gain
        do_l = cut(r * (doh - oh * jnp.mean(doh * oh, axis=-1, keepdims=True)))
        dgain_ref[...] += jnp.sum(dnrm * oh, axis=0, keepdims=True)
        k, v = k_ref[...], v_ref[...]
        t = _dn_local(q_ref[...] * qscale, k, v, g, beta, nc)
        lower, strict, eye = t["lower"], t["strict"], t["eye"]
        s_l = [st_ref[i] for i in range(nc)]
        vn_l = [u - _dot(w, s) for u, w, s in zip(t["u"], t["w"], s_l)]
        dqd_l = [_dot_nt(a, s) for a, s in zip(do_l, s_l)]
        daqk_l = [_dot_nt(a, b) for a, b in zip(do_l, vn_l)]
        aqk_do_l = [_dot_tn(a, b) for a, b in zip(t["aqk"], do_l)]
        qd_do_l = [_dot_tn(a, b) for a, b in zip(t["qd"], do_l)]
        egl_l = [jnp.exp(x) for x in t["g_last"]]
        ds = ds_ref[...]
        dvn_l, dkd_l, dgl_l = [None] * nc, [None] * nc, [None] * nc
        for i in reversed(range(nc)):
            dvn_l[i] = aqk_do_l[i] + _dot(t["kd"][i], ds)
            dkd_l[i] = _dot_nt(vn_l[i], ds)
            dgl_l[i] = jnp.sum(rsum(ds * s_l[i].astype(F32)), axis=0, keepdims=True) * egl_l[i]
            ds = ds * egl_l[i] + qd_do_l[i] - _dot_tn(t["w"][i], dvn_l[i])
        ds_ref[...] = ds
        dw_l = [-_dot_nt(a, s) for a, s in zip(dvn_l, s_l)]
        dbv_l = _hp_l(t["inv"], dvn_l, _TN)
        dbk_l = _hp_l(t["inv"], dw_l, _TN)
        dlow_l = [-(a + b) for a, b in zip(_hp_l(dbv_l, t["u"], _NT), _hp_l(dbk_l, t["w"], _NT))]
        m_l = [jnp.where(strict, a * d, 0.0) for a, d in zip(dlow_l, t["dec"])]
        nmat_l = [jnp.where(lower, a * d, 0.0) for a, d in zip(daqk_l, t["dec"])]
        dkb_l = [_dot(m, kk) + b * e for m, kk, b, e in zip(m_l, t["k"], dbk_l, t["eg"])]
        dqs_l = [_dot(n, kk) + a * e for n, kk, a, e in zip(nmat_l, t["k"], dqd_l, t["eg"])]
        dk1_l = [_dot_tn(m, kb) for m, kb in zip(m_l, t["kb"])]
        dk2_l = [_dot_tn(n, q) for n, q in zip(nmat_l, t["qs"])]
        beta_l, v_l = cut(beta), cut(v)
        rowid = lax.broadcasted_iota(jnp.int32, (c, 1), 0)
        dk_l, dg_l, dbeta_l = [], [], []
        for i in range(nc):
            dk_l.append(dk1_l[i] + dk2_l[i] + dkd_l[i] * t["ekd"][i] + dkb_l[i] * beta_l[i])
            gmat = jnp.where(strict, dlow_l[i] * t["low"][i], 0.0) + daqk_l[i] * t["aqk"][i]
            s_kd = rsum(dkd_l[i] * t["kd"][i])
            dg = (rsum(gmat) + rsum(dqd_l[i] * t["qd"][i]) - s_kd + rsum(dbk_l[i] * t["rhs_k"][i]))
            dg_row = -jnp.sum(gmat, axis=0, keepdims=True)
            dg = dg + rsum(jnp.where(eye, dg_row, 0.0))
            dgl = dgl_l[i] + jnp.sum(s_kd, axis=0, keepdims=True)
            dg_l.append(dg + jnp.where(rowid == c - 1, dgl, 0.0))
            dbeta_l.append(rsum(dbv_l[i] * v_l[i]) + rsum(dkb_l[i] * t["k"][i]))
        dq_ref[...] = cat(dqs_l) * qscale
        dk_ref[...] = cat(dk_l)
        dv_ref[...] = cat(dbv_l) * beta
        dgb_ref[...] = (jnp.where(lane == head, cat(dg_l), 0.0)
                        + jnp.where(lane == head + DN_HEADS, cat(dbeta_l), 0.0))

    rev = lambda i: nb - 1 - i
    qk = lambda off: pl.BlockSpec((tb, DN_DK), lambda h, i: (rev(i), off + h))
    vblk = lambda off: pl.BlockSpec((tb, DN_DV), lambda h, i: (rev(i), off + h))
    gain_spec = pl.BlockSpec((1, DN_DV), lambda h, i: (0, 0))
    return pl.pallas_call(
        body, name=name, grid=(H, nb),
        in_specs=[qk(0), qk(H), vblk(2 * DN_QK_W // DN_DV), pl.BlockSpec((tb, LANES), lambda h, i: (rev(i), 0)),
                  vblk(DN_CONV_W // DN_DV), gain_spec, vblk(0),
                  pl.BlockSpec((None, nc, DN_DK, DN_DV), lambda h, i: (h, rev(i), 0, 0)), vblk(0)],
        out_specs=[qk(0), qk(0), vblk(0), vblk(DN_CONV_W // DN_DV),
                   pl.BlockSpec((None, tb, LANES), lambda h, i: (h, rev(i), 0)), gain_spec],
        out_shape=[jax.ShapeDtypeStruct((T, DN_QK_W), F32), jax.ShapeDtypeStruct((T, DN_QK_W), F32),
                   jax.ShapeDtypeStruct((T, DN_V_W), F32), jax.ShapeDtypeStruct((T, DN_IN_PAD), F32),
                   jax.ShapeDtypeStruct((H, T, LANES), F32), jax.ShapeDtypeStruct((1, DN_DV), F32)],
        scratch_shapes=[pltpu.VMEM((DN_DK, DN_DV), F32)],
        compiler_params=pltpu.CompilerParams(dimension_semantics=("arbitrary", "arbitrary")),
    )(qkv, qkv, qkv, gb, p, o_gain, o, states, dog)


def _dn_conv_bwd(p, conv_w, dq, dk, dv, dp, *, name):
    T = p.shape[0]
    cw = conv_w.shape[1]
    n_q = DN_QK_W // LANES
    n_v = DN_V_W // LANES

    def body(p_ref, w_ref, dq_ref, dk_ref, dv_ref, dp_in, dp_ref, dw_ref):
        del dp_in
        j = pl.program_id(0)
        x = p_ref[...]
        ksz = w_ref.shape[0]
        xs = [_shift_down(x, ksz - 1 - i) for i in range(ksz)]
        xc = sum(w_ref[i:i + 1, :] * xs[i] for i in range(ksz))
        s = _silu(xc)
        r = lax.rsqrt(jnp.sum(s * s, axis=-1, keepdims=True) + L2_EPS)
        y = s * r
        dn = jnp.where(j < n_q, dq_ref[...], dk_ref[...])
        ds_qk = r * (dn - y * jnp.sum(dn * y, axis=-1, keepdims=True))
        ds = jnp.where(j < 2 * n_q, ds_qk, dv_ref[...])
        dxc = ds * _dsilu(xc)
        dp_ref[...] = sum(w_ref[i:i + 1, :] * _shift_up(dxc, ksz - 1 - i) for i in range(ksz))
        for i in range(ksz):
            dw_ref[i:i + 1, :] = jnp.sum(dxc * xs[i], axis=0, keepdims=True)

    colblk = pl.BlockSpec((T, LANES), lambda j: (0, j))
    wblk = pl.BlockSpec((DN_CONV, LANES), lambda j: (0, j))
    return pl.pallas_call(
        body, name=name, grid=(cw // LANES,),
        in_specs=[colblk, wblk,
                  pl.BlockSpec((T, LANES), lambda j: (0, jnp.minimum(j, n_q - 1))),
                  pl.BlockSpec((T, LANES), lambda j: (0, jnp.clip(j - n_q, 0, n_q - 1))),
                  pl.BlockSpec((T, LANES), lambda j: (0, jnp.clip(j - 2 * n_q, 0, n_v - 1))),
                  pl.BlockSpec(memory_space=pl.ANY)],
        out_specs=[colblk, wblk],
        out_shape=[jax.ShapeDtypeStruct(dp.shape, F32), jax.ShapeDtypeStruct((DN_CONV, cw), F32)],
        input_output_aliases={5: 0},
        compiler_params=pltpu.CompilerParams(dimension_semantics=("parallel",), vmem_limit_bytes=VMEM_BIG),
    )(p, conv_w, dq, dk, dv, dp)


def _dn_ab_bwd(p, alog_row, dtb_row, dgb, dp, *, name):
    T = p.shape[0]
    rows = min(DN_TB, T)
    H = DN_HEADS

    def body(p_ref, al_ref, dt_ref, dgb_ref, dp_in, dp_ref, dal_ref, ddt_ref):
        del dp_in

        @pl.when(pl.program_id(0) == 0)
        def _():
            dal_ref[...] = jnp.zeros_like(dal_ref)
            ddt_ref[...] = jnp.zeros_like(ddt_ref)

        blk = p_ref[...]
        is_a, is_b, a_arg, neg_exp, log_a, beta = _dn_ab_parts(blk, al_ref[...], dt_ref[...])
        d = dgb_ref[0]
        for hh in range(1, H):
            d = d + dgb_ref[hh]
        hi, mid, lo_ = _split3(jnp.where(is_a, d, 0.0))
        tri = _dn_chunk_tri(rows, upper=True)
        f = lambda q: jnp.dot(tri, q, preferred_element_type=F32)
        dlog_a = f(hi) + f(mid) + f(lo_)
        da_in = dlog_a * neg_exp * _sigmoid(a_arg)
        db_in = jnp.where(is_b, d, 0.0) * beta * (1.0 - beta)
        dp_ref[...] = jnp.where(is_a, da_in, 0.0) + db_in
        dal_ref[...] += jnp.sum(dlog_a * log_a, axis=0, keepdims=True)
        ddt_ref[...] += jnp.sum(jnp.where(is_a, da_in, 0.0), axis=0, keepdims=True)

    blk = pl.BlockSpec((rows, LANES), lambda i: (i, DN_AB_COL))
    vec = pl.BlockSpec((1, LANES), lambda i: (0, 0))
    return pl.pallas_call(
        body, name=name, grid=(T // rows,),
        in_specs=[blk, vec, vec, pl.BlockSpec((H, rows, LANES), lambda i: (0, i, 0)),
                  pl.BlockSpec(memory_space=pl.ANY)],
        out_specs=[blk, vec, vec],
        out_shape=[jax.ShapeDtypeStruct(dp.shape, F32), jax.ShapeDtypeStruct((1, LANES), F32),
                   jax.ShapeDtypeStruct((1, LANES), F32)],
        input_output_aliases={4: 0},
        compiler_params=pltpu.CompilerParams(dimension_semantics=("arbitrary",)),
    )(p, alog_row, dtb_row, dgb, dp)


def _dn_layer_fwd(x, ng, w_in, conv_w, a_log, dt_bias, o_gain, w_out, tag):
    alog_row, dtb_row = _dn_lane_rows(a_log, dt_bias)
    gain = o_gain.reshape(1, DN_DV)
    h = _rmsnorm_fwd(x, ng, name=f"{tag}_norm")
    p = _matmul(h, w_in, mode="nn", name=f"{tag}_inproj")
    qkv = _dn_prep_fwd(p, conv_w, name=f"{tag}_prep")
    gb = _dn_ab_fwd(p, alog_row, dtb_row, name=f"{tag}_ab")
    o, og, states = _dn_delta_fwd(qkv, gb, p, gain, name=f"{tag}_delta")
    x_new = _matmul(og, w_out, mode="nn", res=x, name=f"{tag}_outproj")
    return x_new, (h, p, qkv, gb, o, og, states)


def _dn_layer_bwd(dx, x, ng, w_in, conv_w, a_log, dt_bias, o_gain, w_out, saved, tag):
    h, p, qkv, gb, o, og, states = saved
    alog_row, dtb_row = _dn_lane_rows(a_log, dt_bias)
    gain = o_gain.reshape(1, DN_DV)
    d_wout = _matmul(og, dx, mode="tn", name=f"{tag}_dwout")
    dog = _matmul(dx, w_out, mode="nt", name=f"{tag}_dog")
    dq, dk, dv, dp, dgb, dgain = _dn_delta_bwd(qkv, gb, p, gain, o, states, dog, name=f"{tag}_deltabwd")
    dp, dconv = _dn_conv_bwd(p, conv_w, dq, dk, dv, dp, name=f"{tag}_convbwd")
    dp, dal, ddt = _dn_ab_bwd(p, alog_row, dtb_row, dgb, dp, name=f"{tag}_abbwd")
    d_win = _matmul(h, dp, mode="tn", name=f"{tag}_dwin")
    dh = _matmul(dp, w_in, mode="nt", name=f"{tag}_dh")
    dx_prev, dng = _rmsnorm_bwd(x, ng, dh, dx, name=f"{tag}_normbwd")
    return dx_prev, dng, d_win, dconv, dal[0, :DN_HEADS], ddt[0, :DN_HEADS], dgain[0], d_wout


def _sb_gains(g):
    return jnp.concatenate([g, g]).reshape(1, LANES)


def _sb_layer_fwd(x, ng, w_in, gq, gk, w_out, tag):
    h = _rmsnorm_fwd(x, ng, name=f"{tag}_norm")
    p3 = _matmul(h, w_in, mode="nn", out_parts=4, name=f"{tag}_inproj")
    og, o, lsum = _sb_attn_fwd(p3, _sb_gains(gq), _sb_gains(gk), name=f"{tag}_attn")
    x_new = _matmul(og, w_out, mode="nn", res=x, name=f"{tag}_outproj")
    return x_new, (h, p3, og, o, lsum)


def _sb_layer_bwd(dx, x, ng, w_in, gq, gk, w_out, saved, tag):
    h, p3, og, o, lsum = saved
    d_wout = _matmul(og, dx, mode="tn", name=f"{tag}_dwout")
    dog = _matmul(dx, w_out, mode="nt", name=f"{tag}_dog")
    dp3, dgq, dgk = _sb_attn_bwd(p3, _sb_gains(gq), _sb_gains(gk), o, lsum, dog, name=f"{tag}_attnbwd")
    fold = lambda d: jnp.sum(d.reshape(-1, SB_DH), axis=0)
    d_win = _matmul(h, dp3, mode="tn", b_parts=4, name=f"{tag}_dwin")
    dh = _matmul(dp3, w_in, mode="nt", a_parts=4, name=f"{tag}_dh")
    dx_prev, dng = _rmsnorm_bwd(x, ng, dh, dx, name=f"{tag}_normbwd")
    return dx_prev, dng, d_win, fold(dgq), fold(dgk), d_wout


N_CHIPS = 4
HBM = pl.BlockSpec(memory_space=pl.ANY)


def _mesh_pos():
    return lax.axis_index("x"), lax.axis_index("y"), lax.axis_index("c")


def _other_chips(x, y):
    return [(1 - x, y), (x, 1 - y), (1 - x, 1 - y)]


def _chip_exchange(srcs, *, send_slot_is_dest, copy_own, name):
    n = len(srcs)

    def body(*refs):
        src_refs, out_refs = refs[:n], refs[n:2 * n]
        send_sems, recv_sems, local_sems = refs[2 * n:]
        x, y, c = _mesh_pos()
        me = 2 * x + y
        chips = _other_chips(x, y)
        local = []
        for a in range(n):
            if not copy_own[a]:
                continue
            own = src_refs[a].at[me] if send_slot_is_dest else src_refs[a]
            local.append(pltpu.make_async_copy(own, out_refs[a].at[me], local_sems.at[a]))
        for cp in local:
            cp.start()

        def copy(a, k, landing_slot):
            px, py = chips[k]
            src = src_refs[a].at[2 * px + py] if send_slot_is_dest else src_refs[a]
            return pltpu.make_async_remote_copy(
                src_ref=src, dst_ref=out_refs[a].at[landing_slot],
                send_sem=send_sems.at[a * 3 + k], recv_sem=recv_sems.at[a * 3 + k],
                device_id=(px, py, c), device_id_type=MESH)

        sends = [copy(a, k, me) for a in range(n) for k in range(3)]
        for cp in sends:
            cp.start()
        for a in range(n):
            for k in range(3):
                px, py = chips[k]
                copy(a, k, 2 * px + py).wait_recv()
        for cp in sends:
            cp.wait_send()
        for cp in local:
            cp.wait()

    outs = []
    for s in srcs:
        shape = s.shape if send_slot_is_dest else (N_CHIPS,) + s.shape
        outs.append(jax.ShapeDtypeStruct(shape, s.dtype))
    return pl.pallas_call(
        body, name=name, in_specs=[HBM] * n, out_specs=[HBM] * n, out_shape=outs,
        scratch_shapes=[pltpu.SemaphoreType.DMA((3 * n,)), pltpu.SemaphoreType.DMA((3 * n,)),
                        pltpu.SemaphoreType.DMA((n,))],
    )(*srcs)


def _sibling_exchange(srcs, *, name):
    n = len(srcs)

    def body(*refs):
        src_refs, out_refs = refs[:n], refs[n:2 * n]
        send_sems, recv_sems = refs[2 * n:]
        x, y, c = _mesh_pos()
        copies = [pltpu.make_async_remote_copy(
            src_ref=src_refs[a], dst_ref=out_refs[a], send_sem=send_sems.at[a], recv_sem=recv_sems.at[a],
            device_id=(x, y, 1 - c), device_id_type=MESH) for a in range(n)]
        for cp in copies:
            cp.start()
        for cp in copies:
            cp.wait()

    return pl.pallas_call(
        body, name=name, in_specs=[HBM] * n, out_specs=[HBM] * n,
        out_shape=[jax.ShapeDtypeStruct(s.shape, s.dtype) for s in srcs],
        scratch_shapes=[pltpu.SemaphoreType.DMA((n,)), pltpu.SemaphoreType.DMA((n,))],
    )(*srcs)


def _gather_halves(big, small, *, name):
    R = big.shape[0]
    half = R // 2

    def body(big_ref, small_ref, obig_ref, osmall_ref, send_sems, recv_sems, local_sems):
        x, y, c = _mesh_pos()
        me = 2 * x + y
        chips = _other_chips(x, y)
        mine = pl.ds(c * half, half)
        theirs = pl.ds((1 - c) * half, half)
        local = [pltpu.make_async_copy(big_ref, obig_ref.at[me], local_sems.at[0]),
                 pltpu.make_async_copy(small_ref, osmall_ref.at[me], local_sems.at[1])]
        for cp in local:
            cp.start()

        def over_ici(k, slot):
            px, py = chips[k]
            return pltpu.make_async_remote_copy(
                src_ref=big_ref.at[mine], dst_ref=obig_ref.at[slot, mine],
                send_sem=send_sems.at[k], recv_sem=recv_sems.at[k], device_id=(px, py, c), device_id_type=MESH)

        def small_copy(k, slot):
            px, py = chips[k]
            return pltpu.make_async_remote_copy(
                src_ref=small_ref, dst_ref=osmall_ref.at[slot],
                send_sem=send_sems.at[3 + k], recv_sem=recv_sems.at[3 + k], device_id=(px, py, c),
                device_id_type=MESH)

        def to_sibling(k, rows):
            px, py = chips[k]
            blk = obig_ref.at[2 * px + py, rows]
            return pltpu.make_async_remote_copy(
                src_ref=blk, dst_ref=blk, send_sem=send_sems.at[6 + k], recv_sem=recv_sems.at[6 + k],
                device_id=(x, y, 1 - c), device_id_type=MESH)

        sends = [over_ici(k, me) for k in range(3)] + [small_copy(k, me) for k in range(3)]
        for cp in sends:
            cp.start()
        passed = [to_sibling(k, mine) for k in range(3)]
        for k in range(3):
            px, py = chips[k]
            over_ici(k, 2 * px + py).wait_recv()
            passed[k].start()
        for k in range(3):
            px, py = chips[k]
            small_copy(k, 2 * px + py).wait_recv()
            to_sibling(k, theirs).wait_recv()
        for cp in sends + passed:
            cp.wait_send()
        for cp in local:
            cp.wait()

    return pl.pallas_call(
        body, name=name, in_specs=[HBM, HBM], out_specs=[HBM, HBM],
        out_shape=[jax.ShapeDtypeStruct((N_CHIPS,) + big.shape, big.dtype),
                   jax.ShapeDtypeStruct((N_CHIPS,) + small.shape, small.dtype)],
        scratch_shapes=[pltpu.SemaphoreType.DMA((9,)), pltpu.SemaphoreType.DMA((9,)), pltpu.SemaphoreType.DMA((2,))],
    )(big, small)


def _swap_other_half(g4, *, name):
    n, R, C = g4.shape
    half = R // 2

    def body(g_ref, o_ref, send_sem, recv_sem):
        x, y, c = _mesh_pos()
        cp = pltpu.make_async_remote_copy(
            src_ref=g_ref.at[:, pl.ds((1 - c) * half, half), :], dst_ref=o_ref, send_sem=send_sem,
            recv_sem=recv_sem, device_id=(x, y, 1 - c), device_id_type=MESH)
        cp.start()
        cp.wait()

    return pl.pallas_call(
        body, name=name, in_specs=[HBM], out_specs=HBM, out_shape=jax.ShapeDtypeStruct((n, half, C), g4.dtype),
        scratch_shapes=[pltpu.SemaphoreType.DMA, pltpu.SemaphoreType.DMA],
    )(g4)


def _add_my_half(g4, sib4, core, *, name):
    n, R, C = g4.shape
    half = R // 2
    tr = _pick(half, (512, 256, 128, 64, 32, 16, 8))
    per = half // tr

    def body(core_ref, g_ref, s_ref, o_ref):
        del core_ref
        o_ref[...] = (g_ref[...].astype(F32) + s_ref[...].astype(F32)).astype(o_ref.dtype)

    return pl.pallas_call(
        body, name=name,
        grid_spec=pltpu.PrefetchScalarGridSpec(
            num_scalar_prefetch=1, grid=(n, per),
            in_specs=[pl.BlockSpec((None, tr, C), lambda j, i, core_ref: (j, core_ref[0] * per + i, 0)),
                      pl.BlockSpec((None, tr, C), lambda j, i, core_ref: (j, i, 0))],
            out_specs=pl.BlockSpec((None, tr, C), lambda j, i, core_ref: (j, i, 0))),
        out_shape=jax.ShapeDtypeStruct((n, half, C), g4.dtype),
        compiler_params=pltpu.CompilerParams(dimension_semantics=("parallel", "parallel")),
    )(core, g4, sib4)


def _scatter_to_chips(p4, *, name):
    def body(p_ref, o_ref, send_sems, recv_sems, local_sem):
        x, y, c = _mesh_pos()
        me = 2 * x + y
        chips = _other_chips(x, y)
        own = pltpu.make_async_copy(p_ref.at[me], o_ref.at[me], local_sem)
        own.start()

        def copy(k, landing_slot):
            px, py = chips[k]
            return pltpu.make_async_remote_copy(
                src_ref=p_ref.at[2 * px + py], dst_ref=o_ref.at[landing_slot], send_sem=send_sems.at[k],
                recv_sem=recv_sems.at[k], device_id=(px, py, c), device_id_type=MESH)

        sends = [copy(k, me) for k in range(3)]
        for cp in sends:
            cp.start()
        for k in range(3):
            px, py = chips[k]
            copy(k, 2 * px + py).wait_recv()
        for cp in sends:
            cp.wait_send()
        own.wait()

    return pl.pallas_call(
        body, name=name, in_specs=[HBM], out_specs=HBM, out_shape=jax.ShapeDtypeStruct(p4.shape, p4.dtype),
        scratch_shapes=[pltpu.SemaphoreType.DMA((3,)), pltpu.SemaphoreType.DMA((3,)), pltpu.SemaphoreType.DMA],
    )(p4)


def _sum_chips(r4, *, name):
    _, r, C = r4.shape
    tr = _pick(r, (512, 256, 128, 64, 32, 16, 8))

    def body(r_ref, o_ref):
        f = lambda j: r_ref[j].astype(F32)
        o_ref[...] = ((f(0) + f(1)) + f(2)) + f(3)

    return pl.pallas_call(
        body, name=name, grid=(r // tr,), in_specs=[pl.BlockSpec((N_CHIPS, tr, C), lambda i: (0, i, 0))],
        out_specs=pl.BlockSpec((tr, C), lambda i: (i, 0)), out_shape=jax.ShapeDtypeStruct((r, C), F32),
        compiler_params=pltpu.CompilerParams(dimension_semantics=("parallel",)),
    )(r4)


def _join_halves(mine, *, name):
    half, C = mine.shape

    def body(m_ref, o_ref, send_sem, recv_sem, local_sem):
        x, y, c = _mesh_pos()
        own = pltpu.make_async_copy(m_ref, o_ref.at[pl.ds(c * half, half)], local_sem)
        own.start()
        cp = pltpu.make_async_remote_copy(
            src_ref=m_ref, dst_ref=o_ref.at[pl.ds(c * half, half)], send_sem=send_sem, recv_sem=recv_sem,
            device_id=(x, y, 1 - c), device_id_type=MESH)
        cp.start()
        pltpu.make_async_remote_copy(
            src_ref=m_ref, dst_ref=o_ref.at[pl.ds((1 - c) * half, half)], send_sem=send_sem, recv_sem=recv_sem,
            device_id=(x, y, 1 - c), device_id_type=MESH).wait_recv()
        cp.wait_send()
        own.wait()

    return pl.pallas_call(
        body, name=name, in_specs=[HBM], out_specs=HBM, out_shape=jax.ShapeDtypeStruct((2 * half, C), mine.dtype),
        scratch_shapes=[pltpu.SemaphoreType.DMA, pltpu.SemaphoreType.DMA, pltpu.SemaphoreType.DMA],
    )(mine)


def _sum_small(recv4, *, name):
    _, R, C = recv4.shape

    def body(r_ref, o_ref):
        o_ref[...] = ((r_ref[0] + r_ref[1]) + r_ref[2]) + r_ref[3]

    return pl.pallas_call(body, name=name, out_shape=jax.ShapeDtypeStruct((R, C), F32))(recv4)


def _add(a, b, *, name):
    R, C = a.shape
    tr = _pick(R, (512, 256, 128, 64, 32, 16, 8))
    blk = pl.BlockSpec((tr, C), lambda i: (i, 0))

    def body(a_ref, b_ref, o_ref):
        o_ref[...] = a_ref[...] + b_ref[...]

    return pl.pallas_call(body, name=name, grid=(R // tr,), in_specs=[blk, blk], out_specs=blk,
                          out_shape=jax.ShapeDtypeStruct((R, C), F32),
                          compiler_params=pltpu.CompilerParams(dimension_semantics=("parallel",)))(a, b)


def _adamw(w, g, m, v, *, name):
    shape = w.shape
    C = shape[-1]
    R = w.size // C
    two = lambda a: a.reshape(R, C)
    tr = _pick(R, (256, 128, 64, 32, 16, 8)) if R % 8 == 0 and R > 8 else R
    blk = pl.BlockSpec((tr, C), lambda i: (i, 0))

    def body(w_ref, g_ref, m_ref, v_ref, d_ref, nm_ref, nv_ref):
        gv = g_ref[...]
        nm = ADAM_B1 * m_ref[...] + (1.0 - ADAM_B1) * gv
        nv = ADAM_B2 * v_ref[...] + (1.0 - ADAM_B2) * (gv * gv)
        m_hat = nm / (1.0 - ADAM_B1 ** ADAM_STEP)
        v_hat = nv / (1.0 - ADAM_B2 ** ADAM_STEP)
        d_ref[...] = -ADAM_LR * (m_hat / (jnp.sqrt(v_hat) + ADAM_EPS) + ADAM_WD * w_ref[...])
        nm_ref[...] = nm
        nv_ref[...] = nv

    out = jax.ShapeDtypeStruct((R, C), F32)
    d, nm, nv = pl.pallas_call(
        body, name=name, grid=(R // tr,), in_specs=[blk] * 4, out_specs=[blk] * 3, out_shape=[out] * 3,
        compiler_params=pltpu.CompilerParams(dimension_semantics=("parallel",)),
    )(two(w), two(g), two(m), two(v))
    return d.reshape(shape), nm.reshape(shape), nv.reshape(shape)


PACK_COLS = 1024
BIG = (("dn_w_in", (2, 1024, 1540), 2), ("dn_w_out", (2, 512, 1024), 1), ("sb_w_in", (1, 1024, 1024), 2),
       ("sb_w_out", (1, 256, 1024), 1), ("sc_w_in", (1, 1024, 2048), 2), ("sc_w_out", (1, 512, 1024), 1))
SMALL = (("dn_conv_w", (2, 4, 1024), 2), ("dn_o_norm_g", (2, 64), 1), ("sc_conv_w", (1, 3, 512), 2))
REPL = (("norm_g", (4, 1024)), ("dn_a_log", (2, 8)), ("dn_dt_bias", (2, 8)), ("sb_q_norm_g", (1, 64)),
        ("sb_k_norm_g", (1, 64)))


def _pack(arrays, cols, lead=()):
    flat = jnp.concatenate([a.reshape(lead + (-1,)) for a in arrays], axis=-1)
    n = flat.shape[-1]
    rows = -(-n // cols)
    unit = 512 if rows > 512 else 8
    rows = -(-rows // unit) * unit
    flat = jnp.pad(flat, [(0, 0)] * len(lead) + [(0, rows * cols - n)])
    return flat.reshape(lead + (rows, cols))


def _unpack(buf, table, lead=()):
    flat = buf.reshape(lead + (-1,))
    out, off = {}, 0
    for entry in table:
        name, shape = entry[0], entry[1]
        n = math.prod(shape)
        out[name] = flat[..., off:off + n].reshape(lead + shape)
        off += n
    return out


def _join(shards, axis):
    return jnp.concatenate([shards[j] for j in range(N_CHIPS)], axis=axis)


def _split(full, axis):
    return jnp.stack(jnp.split(full, N_CHIPS, axis=axis), axis=0)


def kernel(x, norm_g, dn_w_in, dn_conv_w, dn_a_log, dn_dt_bias, dn_o_norm_g, dn_w_out, sb_w_in, sb_q_norm_g, sb_k_norm_g, sb_w_out, sc_w_in, sc_conv_w, sc_w_out, loss_target, m_norm_g, m_dn_w_in, m_dn_conv_w, m_dn_a_log, m_dn_dt_bias, m_dn_o_norm_g, m_dn_w_out, m_sb_w_in, m_sb_q_norm_g, m_sb_k_norm_g, m_sb_w_out, m_sc_w_in, m_sc_conv_w, m_sc_w_out, v_norm_g, v_dn_w_in, v_dn_conv_w, v_dn_a_log, v_dn_dt_bias, v_dn_o_norm_g, v_dn_w_out, v_sb_w_in, v_sb_q_norm_g, v_sb_k_norm_g, v_sb_w_out, v_sc_w_in, v_sc_conv_w, v_sc_w_out):
    weights = dict(norm_g=norm_g, dn_w_in=dn_w_in, dn_conv_w=dn_conv_w, dn_a_log=dn_a_log, dn_dt_bias=dn_dt_bias,
                   dn_o_norm_g=dn_o_norm_g, dn_w_out=dn_w_out, sb_w_in=sb_w_in, sb_q_norm_g=sb_q_norm_g,
                   sb_k_norm_g=sb_k_norm_g, sb_w_out=sb_w_out, sc_w_in=sc_w_in, sc_conv_w=sc_conv_w, sc_w_out=sc_w_out)
    m_in = dict(norm_g=m_norm_g, dn_w_in=m_dn_w_in, dn_conv_w=m_dn_conv_w, dn_a_log=m_dn_a_log,
                dn_dt_bias=m_dn_dt_bias, dn_o_norm_g=m_dn_o_norm_g, dn_w_out=m_dn_w_out, sb_w_in=m_sb_w_in,
                sb_q_norm_g=m_sb_q_norm_g, sb_k_norm_g=m_sb_k_norm_g, sb_w_out=m_sb_w_out, sc_w_in=m_sc_w_in,
                sc_conv_w=m_sc_conv_w, sc_w_out=m_sc_w_out)
    v_in = dict(norm_g=v_norm_g, dn_w_in=v_dn_w_in, dn_conv_w=v_dn_conv_w, dn_a_log=v_dn_a_log,
                dn_dt_bias=v_dn_dt_bias, dn_o_norm_g=v_dn_o_norm_g, dn_w_out=v_dn_w_out, sb_w_in=v_sb_w_in,
                sb_q_norm_g=v_sb_q_norm_g, sb_k_norm_g=v_sb_k_norm_g, sb_w_out=v_sb_w_out, sc_w_in=v_sc_w_in,
                sc_conv_w=v_sc_conv_w, sc_w_out=v_sc_w_out)
    order = list(weights)
    _, _, ci = _mesh_pos()

    big = _pack([weights[n].astype(BF16) for n, _, _ in BIG], PACK_COLS)
    small = _pack([weights[n] for n, _, _ in SMALL], LANES)
    big4, small4 = _gather_halves(big, small, name="gather_weights")
    full = {n: _join(a, ax) for (n, _, ax), a in zip(BIG, _unpack(big4, BIG, (N_CHIPS,)).values())}
    full.update({n: _join(a, ax) for (n, _, ax), a in zip(SMALL, _unpack(small4, SMALL, (N_CHIPS,)).values())})
    dn_w_in_pad = jnp.pad(full["dn_w_in"], ((0, 0), (0, 0), (0, DN_IN_PAD - DN_IN)))

    def dn_args(j):
        return (dn_w_in_pad[j], full["dn_conv_w"][j], dn_a_log[j], dn_dt_bias[j], full["dn_o_norm_g"][j],
                full["dn_w_out"][j])

    sb_args = (full["sb_w_in"][0], sb_q_norm_g[0], sb_k_norm_g[0], full["sb_w_out"][0])
    sc_args = (full["sc_w_in"][0], full["sc_conv_w"][0], full["sc_w_out"][0])

    x0 = x[0]
    x1, s0 = _dn_layer_fwd(x0, norm_g[0], *dn_args(0), "l0")
    x2, s1 = _sb_layer_fwd(x1, norm_g[1], *sb_args, "l1")
    x3, s2 = _sc_layer_fwd(x2, norm_g[2], *sc_args, "l2")
    x4, s3 = _dn_layer_fwd(x3, norm_g[3], *dn_args(1), "l3")
    dy, loss_local = _loss_head(x4, loss_target[0], name="loss_head")
    loss = lax.psum(loss_local[0, 0], ("x", "y", "c"))

    dx3, dng3, dwin3, dconv3, dal3, ddt3, dgain3, dwout3 = _dn_layer_bwd(dy, x3, norm_g[3], *dn_args(1), s3, "l3")
    dx2, dng2, dwin2, dconv2, dwout2 = _sc_layer_bwd(dx3, x2, norm_g[2], *sc_args, s2, "l2")
    dx1, dng1, dwin1, dgq, dgk, dwout1 = _sb_layer_bwd(dx2, x1, norm_g[1], *sb_args, s1, "l1")
    dx0, dng0, dwin0, dconv0, dal0, ddt0, dgain0, dwout0 = _dn_layer_bwd(dx1, x0, norm_g[0], *dn_args(0), s0, "l0")

    grads = dict(
        norm_g=jnp.concatenate([dng0, dng1, dng2, dng3], axis=0),
        dn_w_in=jnp.stack([dwin0[:, :DN_IN], dwin3[:, :DN_IN]]), dn_conv_w=jnp.stack([dconv0, dconv3]),
        dn_a_log=jnp.stack([dal0, dal3]), dn_dt_bias=jnp.stack([ddt0, ddt3]),
        dn_o_norm_g=jnp.stack([dgain0, dgain3]), dn_w_out=jnp.stack([dwout0, dwout3]),
        sb_w_in=dwin1[None], sb_q_norm_g=dgq[None], sb_k_norm_g=dgk[None], sb_w_out=dwout1[None],
        sc_w_in=dwin2[None], sc_conv_w=dconv2[None], sc_w_out=dwout2[None])

    gbig = _pack([_split(grads[n].astype(BF16), ax) for n, _, ax in BIG], PACK_COLS, (N_CHIPS,))
    sib = _swap_other_half(gbig, name="swap_halves")
    part = _add_my_half(gbig, sib, ci.astype(jnp.int32).reshape(1), name="sum_cores")
    tbig = _join_halves(_sum_chips(_scatter_to_chips(part, name="scatter_grads"), name="sum_chips"),
                        name="join_halves")
    repl = [jnp.broadcast_to(grads[n][None], (N_CHIPS,) + s) for n, s in REPL]
    gsmall = _pack([_split(grads[n], ax) for n, _, ax in SMALL] + repl, LANES, (N_CHIPS,))
    rsmall, = _chip_exchange([gsmall], send_slot_is_dest=True, copy_own=(True,), name="scatter_small")
    psmall = _sum_small(rsmall, name="sum_chips_small")
    qsmall, = _sibling_exchange([psmall], name="swap_cores_small")
    tsmall = _add(psmall, qsmall, name="sum_cores_small")
    g_out = _unpack(tbig, BIG)
    g_out.update(_unpack(tsmall, SMALL + REPL))

    upd = {n: _adamw(weights[n], g_out[n], m_in[n], v_in[n], name=f"adamw_{n}") for n in order}
    return (loss, dx0[None], *[g_out[n] for n in order], *[upd[n][0] for n in order],
            *[upd[n][1] for n in order], *[upd[n][2] for n in order])
```

```python
import math

import jax
import jax.numpy as jnp
from jax import lax
from jax.experimental import pallas as pl
from jax.experimental.pallas import tpu as pltpu

F32 = jnp.float32
BF16 = jnp.bfloat16
MESH = pl.DeviceIdType.MESH

RMS_EPS = 1e-6
L2_EPS = 1e-6
LANES = 128
VMEM_BIG = 60 * 1024 * 1024
MM_VMEM = 36 * 1024 * 1024

DN_HEADS, DN_DK, DN_DV, DN_CHUNK, DN_CONV = 8, 128, 256, 64, 4
DN_QK_W = DN_HEADS * DN_DK
DN_V_W = DN_HEADS * DN_DV
DN_CONV_W = 2 * DN_QK_W + DN_V_W
DN_IN = DN_CONV_W + DN_V_W + 2 * DN_HEADS
DN_IN_PAD = DN_CONV_W + DN_V_W + LANES
SB_DH = 64
SC_CONV = 3

ADAM_LR, ADAM_B1, ADAM_B2, ADAM_EPS, ADAM_WD, ADAM_STEP = 0.001, 0.9, 0.999, 1e-08, 0.01, 10


def _pick(n, cands):
    for c in cands:
        if n % c == 0:
            return c
    raise ValueError(f"no tile for {n} in {cands}")


def _bf(x):
    return x.astype(BF16)


def _dot(a, b):
    return jnp.dot(_bf(a), _bf(b), preferred_element_type=F32)


def _dot_nt(a, b):
    return lax.dot_general(_bf(a), _bf(b), (((1,), (1,)), ((), ())), preferred_element_type=F32)


def _dot_tn(a, b):
    return lax.dot_general(_bf(a), _bf(b), (((0,), (0,)), ((), ())), preferred_element_type=F32)


def _split3(a):
    hi = _bf(a)
    r = a - hi.astype(F32)
    mid = _bf(r)
    lo = _bf(r - mid.astype(F32))
    return hi, mid, lo


def _sigmoid(x):
    return 1.0 / (1.0 + jnp.exp(-x))


def _silu(x):
    return x * _sigmoid(x)


def _dsilu(x):
    s = _sigmoid(x)
    return s * (1.0 + x * (1.0 - s))


def _softplus(x):
    return jnp.maximum(x, 0.0) + jnp.log(1.0 + jnp.exp(-jnp.abs(x)))


def _shift_down(z, k):
    if k == 0:
        return z
    row = lax.broadcasted_iota(jnp.int32, z.shape, 0)
    return jnp.where(row >= k, pltpu.roll(z, k, 0), 0.0)


def _shift_up(z, k):
    if k == 0:
        return z
    n = z.shape[0]
    row = lax.broadcasted_iota(jnp.int32, z.shape, 0)
    return jnp.where(row < n - k, pltpu.roll(z, n - k, 0), 0.0)


def _matmul(a, b, *, mode, name, res=None, a_parts=1, b_parts=1, out_parts=1, out_dtype=F32):
    def dims2(x, parts):
        if parts == 1:
            return x.shape
        assert x.shape[0] == parts
        return (x.shape[1], x.shape[2] * parts)

    ash, bsh = dims2(a, a_parts), dims2(b, b_parts)
    if mode == "nn":
        (M, K), (K2, N) = ash, bsh
        dn = (((1,), (0,)), ((), ()))
    elif mode == "nt":
        (M, K), (N, K2) = ash, bsh
        dn = (((1,), (1,)), ((), ()))
    else:
        (K, M), (K2, N) = ash, bsh
        dn = (((0,), (0,)), ((), ()))
    assert K == K2, (ash, bsh, mode)
    tm = _pick(M, (512, 256, 128, 64, 32, 16, 8))
    n_unit = N // max(out_parts, b_parts if mode != "nt" else 1)
    k_unit = K // max(a_parts if mode != "tn" else 1, b_parts if mode == "nt" else 1)
    tn, tk = min(
        ((n, k) for n in (2048, 1792, 1024, 896, 768, 512, 384, 256, 128) if n_unit % n == 0
         for k in (2048, 1792, 1024, 896, 512, 256, 128) if k_unit % k == 0
         if 2 * (tm * k * a.dtype.itemsize + k * n * b.dtype.itemsize + 2 * tm * n * 4) + tm * n * 4 <= MM_VMEM),
        key=lambda nk_: (-nk_[0] * nk_[1], -nk_[1]))
    nk = K // tk
    grid = (M // tm, N // tn, nk)

    def spec(parts, rows_are, cols_are, tr, tc, width):
        per = width // parts // tc
        if parts == 1:
            return pl.BlockSpec((tr, tc), lambda i, j, k: ((i, j, k)[rows_are], (i, j, k)[cols_are]))
        return pl.BlockSpec((None, tr, tc), lambda i, j, k: ((i, j, k)[cols_are] // per, (i, j, k)[rows_are],
                                                             (i, j, k)[cols_are] % per))

    if mode == "nn":
        a_spec = spec(a_parts, 0, 2, tm, tk, K)
        b_spec = spec(b_parts, 2, 1, tk, tn, N)
    elif mode == "nt":
        a_spec = spec(a_parts, 0, 2, tm, tk, K)
        b_spec = spec(b_parts, 1, 2, tn, tk, K)
    else:
        a_spec = spec(a_parts, 2, 0, tk, tm, M)
        b_spec = spec(b_parts, 2, 1, tk, tn, N)
    o_spec = spec(out_parts, 0, 1, tm, tn, N)
    in_specs = [a_spec, b_spec]
    operands = [a, b]
    if res is not None:
        in_specs.append(pl.BlockSpec((tm, tn), lambda i, j, k: (i, j)))
        operands.append(res)

    def finish(refs, r):
        if res is not None:
            r = refs[2][...] + r
        refs[-2 if nk > 1 else -1][...] = r.astype(out_dtype)

    def body(*refs):
        part = lax.dot_general(_bf(refs[0][...]), _bf(refs[1][...]), dn, preferred_element_type=F32)
        if nk == 1:
            finish(refs, part)
            return
        acc_ref = refs[-1]
        k = pl.program_id(2)

        @pl.when(k == 0)
        def _():
            acc_ref[...] = part

        @pl.when(jnp.logical_and(k > 0, k < nk - 1))
        def _():
            acc_ref[...] += part

        @pl.when(k == nk - 1)
        def _():
            finish(refs, acc_ref[...] + part)

    out_shape = (M, N) if out_parts == 1 else (out_parts, M, N // out_parts)
    return pl.pallas_call(
        body, name=name, grid=grid, in_specs=in_specs, out_specs=o_spec,
        out_shape=jax.ShapeDtypeStruct(out_shape, out_dtype),
        scratch_shapes=[pltpu.VMEM((tm, tn), F32)] if nk > 1 else [],
        compiler_params=pltpu.CompilerParams(dimension_semantics=("parallel", "parallel", "arbitrary"),
                                             vmem_limit_bytes=VMEM_BIG),
    )(*operands)


def _rmsnorm_fwd(x, g, *, name):
    T, D = x.shape
    tm = _pick(T, (512, 256, 128, 64, 32, 16))

    def body(x_ref, g_ref, h_ref):
        xv = x_ref[...]
        r = lax.rsqrt(jnp.mean(xv * xv, axis=-1, keepdims=True) + RMS_EPS)
        h_ref[...] = ((xv * r) * g_ref[...]).astype(BF16)

    return pl.pallas_call(
        body, name=name, grid=(T // tm,),
        in_specs=[pl.BlockSpec((tm, D), lambda i: (i, 0)), pl.BlockSpec((1, D), lambda i: (0, 0))],
        out_specs=pl.BlockSpec((tm, D), lambda i: (i, 0)),
        out_shape=jax.ShapeDtypeStruct((T, D), BF16),
    )(x, g.reshape(1, D))


def _rmsnorm_bwd(x, g, dh, dx_in, *, name):
    T, D = x.shape
    tm = _pick(T, (512, 256, 128, 64, 32, 16))

    def body(x_ref, g_ref, dh_ref, dxin_ref, dx_ref, dg_ref):
        @pl.when(pl.program_id(0) == 0)
        def _():
            dg_ref[...] = jnp.zeros_like(dg_ref)

        xv = x_ref[...]
        r = lax.rsqrt(jnp.mean(xv * xv, axis=-1, keepdims=True) + RMS_EPS)
        xh = xv * r
        dh_v = dh_ref[...]
        dxh = dh_v * g_ref[...]
        dx_ref[...] = dxin_ref[...] + r * (dxh - xh * jnp.mean(dxh * xh, axis=-1, keepdims=True))
        dg_ref[...] += jnp.sum(dh_v * xh, axis=0, keepdims=True)

    row = pl.BlockSpec((tm, D), lambda i: (i, 0))
    vec = pl.BlockSpec((1, D), lambda i: (0, 0))
    return pl.pallas_call(
        body, name=name, grid=(T // tm,),
        in_specs=[row, vec, row, row], out_specs=[row, vec],
        out_shape=[jax.ShapeDtypeStruct((T, D), F32), jax.ShapeDtypeStruct((1, D), F32)],
        compiler_params=pltpu.CompilerParams(dimension_semantics=("arbitrary",)),
    )(x, g.reshape(1, D), dh, dx_in)


def _loss_head(y, target, *, name):
    T, D = y.shape
    tm = _pick(T, (512, 256, 128, 64, 32, 16))

    def body(y_ref, t_ref, dy_ref, l_ref):
        @pl.when(pl.program_id(0) == 0)
        def _():
            l_ref[...] = jnp.zeros_like(l_ref)

        err = y_ref[...] - t_ref[...]
        dy_ref[...] = err * (1.0 / D)
        l_ref[...] += 0.5 * jnp.sum(jnp.mean(err * err, axis=-1, keepdims=True), axis=0, keepdims=True)

    row = pl.BlockSpec((tm, D), lambda i: (i, 0))
    return pl.pallas_call(
        body, name=name, grid=(T // tm,),
        in_specs=[row, row], out_specs=[row, pl.BlockSpec((1, 1), lambda i: (0, 0))],
        out_shape=[jax.ShapeDtypeStruct((T, D), F32), jax.ShapeDtypeStruct((1, 1), F32)],
        compiler_params=pltpu.CompilerParams(dimension_semantics=("arbitrary",)),
    )(y, target)


def _sc_mid_fwd(p3, conv_w, *, name):
    _, T, W = p3.shape
    K = conv_w.shape[0]
    cw = LANES

    def body(p_ref, w_ref, o_ref):
        z = p_ref[1] * p_ref[2]
        cv = sum(w_ref[i:i + 1, :] * _shift_down(z, K - 1 - i) for i in range(K))
        o_ref[...] = ((p_ref[0] * cv) * _silu(p_ref[3])).astype(BF16)

    return pl.pallas_call(
        body, name=name, grid=(W // cw,),
        in_specs=[pl.BlockSpec((4, T, cw), lambda j: (0, 0, j)), pl.BlockSpec((K, cw), lambda j: (0, j))],
        out_specs=pl.BlockSpec((T, cw), lambda j: (0, j)),
        out_shape=jax.ShapeDtypeStruct((T, W), BF16),
        compiler_params=pltpu.CompilerParams(dimension_semantics=("parallel",), vmem_limit_bytes=VMEM_BIG),
    )(p3, conv_w)


def _sc_mid_bwd(p3, conv_w, do, *, name):
    _, T, W = p3.shape
    K = conv_w.shape[0]
    cw = LANES

    def body(p_ref, w_ref, do_ref, dp_ref, dw_ref):
        b, c, u, gate = p_ref[0], p_ref[1], p_ref[2], p_ref[3]
        z = c * u
        zs = [_shift_down(z, K - 1 - i) for i in range(K)]
        cv = sum(w_ref[i:i + 1, :] * zs[i] for i in range(K))
        y = b * cv
        dov = do_ref[...]
        dy = dov * _silu(gate)
        dp_ref[3] = dov * y * _dsilu(gate)
        dp_ref[0] = dy * cv
        dcv = dy * b
        dz = sum(w_ref[i:i + 1, :] * _shift_up(dcv, K - 1 - i) for i in range(K))
        dp_ref[1] = dz * u
        dp_ref[2] = dz * c
        for i in range(K):
            dw_ref[i:i + 1, :] = jnp.sum(dcv * zs[i], axis=0, keepdims=True)

    return pl.pallas_call(
        body, name=name, grid=(W // cw,),
        in_specs=[pl.BlockSpec((4, T, cw), lambda j: (0, 0, j)), pl.BlockSpec((K, cw), lambda j: (0, j)),
                  pl.BlockSpec((T, cw), lambda j: (0, j))],
        out_specs=[pl.BlockSpec((4, T, cw), lambda j: (0, 0, j)), pl.BlockSpec((K, cw), lambda j: (0, j))],
        out_shape=[jax.ShapeDtypeStruct((4, T, W), F32), jax.ShapeDtypeStruct((K, W), F32)],
        compiler_params=pltpu.CompilerParams(dimension_semantics=("parallel",), vmem_limit_bytes=VMEM_BIG),
    )(p3, conv_w, do)


def _sc_layer_fwd(x, ng, w_in, conv_w, w_out, tag):
    h = _rmsnorm_fwd(x, ng, name=f"{tag}_norm")
    p3 = _matmul(h, w_in, mode="nn", b_parts=4, out_parts=4, name=f"{tag}_inproj")
    og = _sc_mid_fwd(p3, conv_w, name=f"{tag}_mid")
    x_new = _matmul(og, w_out, mode="nn", res=x, name=f"{tag}_outproj")
    return x_new, (h, p3, og)


def _sc_layer_bwd(dx, x, ng, w_in, conv_w, w_out, saved, tag):
    h, p3, og = saved
    d_wout = _matmul(og, dx, mode="tn", out_dtype=BF16, name=f"{tag}_dwout")
    dog = _matmul(dx, w_out, mode="nt", name=f"{tag}_dog")
    dp3, dconv = _sc_mid_bwd(p3, conv_w, dog, name=f"{tag}_midbwd")
    d_win = _matmul(h, dp3, mode="tn", b_parts=4, out_parts=4, out_dtype=BF16, name=f"{tag}_dwin")
    dh = _matmul(dp3, w_in, mode="nt", a_parts=4, b_parts=4, name=f"{tag}_dh")
    dx_prev, dng = _rmsnorm_bwd(x, ng, dh, dx, name=f"{tag}_normbwd")
    return dx_prev, dng, d_win, dconv, d_wout


SB_BQ = 256
SB_BK = 256
SB_ROWS = 512


def _dot_x2(a, b_exact_bf16):
    hi = _bf(a)
    mid = _bf(a - hi.astype(F32))
    return (jnp.dot(hi, b_exact_bf16, preferred_element_type=F32)
            + jnp.dot(mid, b_exact_bf16, preferred_element_type=F32))


def _sb_half_mask():
    return lax.broadcasted_iota(jnp.int32, (1, LANES), 1) < SB_DH


def _sb_headnorm(x, g, lo):
    x2 = x * x
    s_lo = jnp.sum(jnp.where(lo, x2, 0.0), axis=-1, keepdims=True)
    s_hi = jnp.sum(jnp.where(lo, 0.0, x2), axis=-1, keepdims=True)
    r = lax.rsqrt(jnp.where(lo, s_lo, s_hi) * (1.0 / SB_DH) + RMS_EPS)
    xh = x * r
    return xh * g, xh, r


def _sb_stack(xb, lo):
    zero = jnp.zeros_like(xb)
    return jnp.concatenate([jnp.where(lo, xb, zero), jnp.where(lo, zero, xb)], axis=0)


def _sb_rel(bq, bk):
    row = lax.broadcasted_iota(jnp.int32, (2 * bq, bk), 0)
    col = lax.broadcasted_iota(jnp.int32, (2 * bq, bk), 1)
    return col - jnp.where(row >= bq, row - bq, row)


def _sb_tile(qm, kb, valid, scale):
    z = lax.dot_general(qm, kb, (((1,), (1,)), ((), ())), preferred_element_type=F32) * scale
    sp = _softplus(z)
    return z - sp, (-sp if valid is None else jnp.where(valid, -sp, 0.0))


def _sb_attn_fwd(p3, gq2, gk2, *, name):
    _, T, W = p3.shape
    bq, bk = min(SB_BQ, T), min(SB_BK, T)
    rows = min(SB_ROWS, T)
    scale = SB_DH ** -0.5

    def body(p_ref, gq_ref, gk_ref, og_ref, o_ref, ls_ref, qn_ref, kn_ref, v_ref):
        lo = _sb_half_mask()

        def prologue(i, c):
            r0 = pl.multiple_of(i * rows, rows)
            sl = pl.ds(r0, rows)
            qn_ref[sl, :] = _sb_headnorm(p_ref[0, sl, :], gq_ref[...], lo)[0].astype(BF16)
            kn_ref[sl, :] = _sb_headnorm(p_ref[1, sl, :], gk_ref[...], lo)[0].astype(BF16)
            v_ref[sl, :] = p_ref[2, sl, :].astype(BF16)
            return c

        lax.fori_loop(0, T // rows, prologue, 0)

        rel = _sb_rel(bq, bk)
        tri = (lax.broadcasted_iota(jnp.int32, (bk, bk), 0)
               > lax.broadcasted_iota(jnp.int32, (bk, bk), 1)).astype(BF16)

        def qblock(qi, c):
            q0 = pl.multiple_of(qi * bq, bq)
            qm = _sb_stack(qn_ref[pl.ds(q0, bq), :], lo)
            nkb = (q0 + bq - 1) // bk + 1

            def tile(k0, carry, valid):
                o_acc, a_carry = carry
                logsig, log1m = _sb_tile(qm, kn_ref[pl.ds(k0, bk), :], valid, scale)
                wts = jnp.exp(logsig + (_dot_x2(log1m, tri) + a_carry))
                if valid is not None:
                    wts = jnp.where(valid, wts, 0.0)
                o_acc = o_acc + jnp.dot(_bf(wts), v_ref[pl.ds(k0, bk), :], preferred_element_type=F32)
                return o_acc, a_carry + jnp.sum(log1m, axis=-1, keepdims=True)

            k_last = pl.multiple_of((nkb - 1) * bk, bk)
            first = tile(k_last, (jnp.zeros((2 * bq, LANES), F32), jnp.zeros((2 * bq, 1), F32)), rel < q0 - k_last)
            o2, t2 = lax.fori_loop(
                1, nkb, lambda t, cr: tile(pl.multiple_of((nkb - 1 - t) * bk, bk), cr, None), first)
            o = jnp.where(lo, o2[:bq], o2[bq:])
            o_ref[pl.ds(q0, bq), :] = o
            ls_ref[pl.ds(q0, bq), :] = jnp.where(lo, t2[:bq], t2[bq:])
            og_ref[pl.ds(q0, bq), :] = (o * _silu(p_ref[3, pl.ds(q0, bq), :])).astype(BF16)
            return c

        lax.fori_loop(0, T // bq, qblock, 0)

    colblk = pl.BlockSpec((T, LANES), lambda j: (0, j))
    vec = pl.BlockSpec((1, LANES), lambda j: (0, 0))
    return pl.pallas_call(
        body, name=name, grid=(W // LANES,),
        in_specs=[pl.BlockSpec((4, T, LANES), lambda j: (0, 0, j)), vec, vec],
        out_specs=[colblk, colblk, colblk],
        out_shape=[jax.ShapeDtypeStruct((T, W), BF16), jax.ShapeDtypeStruct((T, W), F32),
                   jax.ShapeDtypeStruct((T, W), F32)],
        scratch_shapes=[pltpu.VMEM((T, LANES), BF16)] * 3,
        compiler_params=pltpu.CompilerParams(dimension_semantics=("parallel",), vmem_limit_bytes=VMEM_BIG),
    )(p3, gq2, gk2)


def _sb_attn_bwd(p3, gq2, gk2, o, lsum, dog, *, name):
    _, T, W = p3.shape
    bq, bk = min(SB_BQ, T), min(SB_BK, T)
    rows = min(SB_ROWS, T)
    scale = SB_DH ** -0.5

    def body(p_ref, gq_ref, gk_ref, o_ref, ls_ref, dog_ref, dp_ref, dgq_ref, dgk_ref,
             qn_ref, kn_ref, v_ref, do_ref):
        lo = _sb_half_mask()

        def prologue(i, c):
            r0 = pl.multiple_of(i * rows, rows)
            sl = pl.ds(r0, rows)
            qn_ref[sl, :] = _sb_headnorm(p_ref[0, sl, :], gq_ref[...], lo)[0].astype(BF16)
            kn_ref[sl, :] = _sb_headnorm(p_ref[1, sl, :], gk_ref[...], lo)[0].astype(BF16)
            v_ref[sl, :] = p_ref[2, sl, :].astype(BF16)
            gate = p_ref[3, sl, :]
            dogv = dog_ref[sl, :]
            dp_ref[3, sl, :] = dogv * o_ref[sl, :] * _dsilu(gate)
            do_ref[sl, :] = (dogv * _silu(gate)).astype(BF16)
            zero = jnp.zeros((rows, LANES), F32)
            dp_ref[0, sl, :] = zero
            dp_ref[1, sl, :] = zero
            dp_ref[2, sl, :] = zero
            return c

        lax.fori_loop(0, T // rows, prologue, 0)

        rel = _sb_rel(bq, bk)
        rj = lax.broadcasted_iota(jnp.int32, (bk, bk), 0)
        cj = lax.broadcasted_iota(jnp.int32, (bk, bk), 1)
        upto = (rj <= cj).astype(BF16)
        before_m = (rj < cj).astype(BF16)

        def qblock(qi, c):
            q0 = pl.multiple_of(qi * bq, bq)
            qm = _sb_stack(qn_ref[pl.ds(q0, bq), :], lo)
            dom = _sb_stack(do_ref[pl.ds(q0, bq), :], lo)
            lsb = ls_ref[pl.ds(q0, bq), :]
            total = jnp.concatenate([lsb[:, 0:1], lsb[:, SB_DH:SB_DH + 1]], axis=0)
            nkb = (q0 + bq - 1) // bk + 1

            def tile(k0, carry, valid):
                dq_acc, a_pre, r_pre = carry
                ks = pl.ds(k0, bk)
                kb = kn_ref[ks, :]
                vb = v_ref[ks, :]
                logsig, log1m = _sb_tile(qm, kb, valid, scale)
                after = (total - a_pre) - _dot_x2(log1m, upto)
                wts = jnp.exp(logsig + after)
                if valid is not None:
                    wts = jnp.where(valid, wts, 0.0)
                dw = lax.dot_general(dom, vb, _NT, preferred_element_type=F32)
                ee = dw * wts
                before = r_pre + _dot_x2(ee, before_m)
                beta = jnp.exp(logsig)
                dz = ee * (1.0 - beta) - beta * before
                if valid is not None:
                    dz = jnp.where(valid, dz, 0.0)
                dzb = _bf(dz * scale)
                dq_acc = dq_acc + jnp.dot(dzb, kb, preferred_element_type=F32)
                dp_ref[1, ks, :] += lax.dot_general(dzb, qm, _TN, preferred_element_type=F32)
                dp_ref[2, ks, :] += lax.dot_general(_bf(wts), dom, _TN, preferred_element_type=F32)
                return (dq_acc, a_pre + jnp.sum(log1m, axis=-1, keepdims=True),
                        r_pre + jnp.sum(ee, axis=-1, keepdims=True))

            init = (jnp.zeros((2 * bq, LANES), F32), jnp.zeros((2 * bq, 1), F32), jnp.zeros((2 * bq, 1), F32))
            before_last = lax.fori_loop(
                0, nkb - 1, lambda kj, cr: tile(pl.multiple_of(kj * bk, bk), cr, None), init)
            k_last = pl.multiple_of((nkb - 1) * bk, bk)
            dq2, _, _ = tile(k_last, before_last, rel < q0 - k_last)
            dp_ref[0, pl.ds(q0, bq), :] = jnp.where(lo, dq2[:bq], dq2[bq:])
            return c

        lax.fori_loop(0, T // bq, qblock, 0)

        dgq_ref[...] = jnp.zeros_like(dgq_ref)
        dgk_ref[...] = jnp.zeros_like(dgk_ref)

        def epilogue(i, c):
            r0 = pl.multiple_of(i * rows, rows)
            sl = pl.ds(r0, rows)
            for part, g_ref, dg_ref in ((0, gq_ref, dgq_ref), (1, gk_ref, dgk_ref)):
                _, xh, r = _sb_headnorm(p_ref[part, sl, :], g_ref[...], lo)
                dn = dp_ref[part, sl, :]
                dxh = dn * g_ref[...]
                prod = dxh * xh
                m_lo = jnp.sum(jnp.where(lo, prod, 0.0), axis=-1, keepdims=True)
                m_hi = jnp.sum(jnp.where(lo, 0.0, prod), axis=-1, keepdims=True)
                m = jnp.where(lo, m_lo, m_hi) * (1.0 / SB_DH)
                dp_ref[part, sl, :] = r * (dxh - xh * m)
                dg_ref[...] += jnp.sum(dn * xh, axis=0, keepdims=True)
            return c

        lax.fori_loop(0, T // rows, epilogue, 0)

    colblk = pl.BlockSpec((T, LANES), lambda j: (0, j))
    vec = pl.BlockSpec((1, LANES), lambda j: (0, 0))
    part = pl.BlockSpec((4, T, LANES), lambda j: (0, 0, j))
    gvec = pl.BlockSpec((None, 1, LANES), lambda j: (j, 0, 0))
    npair = W // LANES
    return pl.pallas_call(
        body, name=name, grid=(npair,),
        in_specs=[part, vec, vec, colblk, colblk, colblk],
        out_specs=[part, gvec, gvec],
        out_shape=[jax.ShapeDtypeStruct((4, T, W), F32), jax.ShapeDtypeStruct((npair, 1, LANES), F32),
                   jax.ShapeDtypeStruct((npair, 1, LANES), F32)],
        scratch_shapes=[pltpu.VMEM((T, LANES), BF16)] * 4,
        compiler_params=pltpu.CompilerParams(dimension_semantics=("parallel",), vmem_limit_bytes=VMEM_BIG),
    )(p3, gq2, gk2, o, lsum, dog)


_NN = (((1,), (0,)), ((), ()))
_NT = (((1,), (1,)), ((), ()))
_TN = (((0,), (0,)), ((), ()))
DN_TB = 512
DN_AB_COL = (DN_CONV_W + DN_V_W) // LANES


def _dn_conv(x, w_ref):
    k = w_ref.shape[0]
    return sum(w_ref[i:i + 1, :] * _shift_down(x, k - 1 - i) for i in range(k))


def _dn_prep_fwd(p, conv_w, *, name):
    T = p.shape[0]
    cw = conv_w.shape[1]
    n_qk = 2 * DN_QK_W // LANES

    def body(p_ref, w_ref, o_ref):
        s = _silu(_dn_conv(p_ref[...], w_ref))
        r = lax.rsqrt(jnp.sum(s * s, axis=-1, keepdims=True) + L2_EPS)
        o_ref[...] = jnp.where(pl.program_id(0) < n_qk, s * r, s)

    colblk = pl.BlockSpec((T, LANES), lambda j: (0, j))
    return pl.pallas_call(
        body, name=name, grid=(cw // LANES,),
        in_specs=[colblk, pl.BlockSpec((DN_CONV, LANES), lambda j: (0, j))],
        out_specs=colblk, out_shape=jax.ShapeDtypeStruct((T, cw), F32),
        compiler_params=pltpu.CompilerParams(dimension_semantics=("parallel",), vmem_limit_bytes=VMEM_BIG),
    )(p, conv_w)


def _dn_chunk_tri(rows, upper):
    r = lax.broadcasted_iota(jnp.int32, (rows, rows), 0)
    c = lax.broadcasted_iota(jnp.int32, (rows, rows), 1)
    same = (r // DN_CHUNK) == (c // DN_CHUNK)
    return jnp.logical_and(same, (c >= r) if upper else (c <= r)).astype(BF16)


def _dn_lane_rows(a_log, dt_bias):
    pad = lambda v: jnp.zeros((1, LANES), F32).at[0, :DN_HEADS].set(v)
    return pad(a_log), pad(dt_bias)


def _dn_ab_parts(blk, alog_row, dtb_row):
    lane = lax.broadcasted_iota(jnp.int32, (1, LANES), 1)
    is_a = lane < DN_HEADS
    is_b = jnp.logical_and(lane >= DN_HEADS, lane < 2 * DN_HEADS)
    a_arg = jnp.where(is_a, blk + dtb_row, 0.0)
    neg_exp = jnp.where(is_a, -jnp.exp(alog_row), 0.0)
    log_a = neg_exp * _softplus(a_arg)
    beta = jnp.where(is_b, _sigmoid(blk), 0.0)
    return is_a, is_b, a_arg, neg_exp, log_a, beta


def _dn_ab_fwd(p, alog_row, dtb_row, *, name):
    T = p.shape[0]
    rows = min(DN_TB, T)

    def body(p_ref, al_ref, dt_ref, o_ref):
        _, _, _, _, log_a, beta = _dn_ab_parts(p_ref[...], al_ref[...], dt_ref[...])
        hi, mid, lo_ = _split3(log_a)
        tri = _dn_chunk_tri(rows, upper=False)
        f = lambda q: jnp.dot(tri, q, preferred_element_type=F32)
        o_ref[...] = (f(hi) + f(mid) + f(lo_)) + beta

    blk = pl.BlockSpec((rows, LANES), lambda i: (i, DN_AB_COL))
    vec = pl.BlockSpec((1, LANES), lambda i: (0, 0))
    return pl.pallas_call(
        body, name=name, grid=(T // rows,), in_specs=[blk, vec, vec],
        out_specs=pl.BlockSpec((rows, LANES), lambda i: (i, 0)),
        out_shape=jax.ShapeDtypeStruct((T, LANES), F32),
        compiler_params=pltpu.CompilerParams(dimension_semantics=("parallel",)),
    )(p, alog_row, dtb_row)


def _hp_l(a_l, b_l, dims=_NN):
    sa = [_split3(a)[:2] for a in a_l]
    sb = [_split3(b)[:2] for b in b_l]
    f = lambda p, q: lax.dot_general(p, q, dims, preferred_element_type=F32)
    hh = [f(x[0], y[0]) for x, y in zip(sa, sb)]
    hm = [f(x[0], y[1]) for x, y in zip(sa, sb)]
    mh = [f(x[1], y[0]) for x, y in zip(sa, sb)]
    return [a + (b + c) for a, b, c in zip(hh, hm, mh)]


def _dn_local(qs, k, v, g, beta, nc):
    c = DN_CHUNK
    cut = lambda x: [x[i * c:(i + 1) * c] for i in range(nc)]
    row = lax.broadcasted_iota(jnp.int32, (c, c), 0)
    col = lax.broadcasted_iota(jnp.int32, (c, c), 1)
    eye, lower, strict = row == col, row >= col, row > col
    rowid = lax.broadcasted_iota(jnp.int32, (c, 1), 0)
    eg = jnp.exp(g)
    kb = k * beta
    rhs_k = kb * eg
    g_l, k_l, kb_l, qs_l = cut(g), cut(k), cut(kb), cut(qs)
    g_row_l = [jnp.sum(jnp.where(eye, x, 0.0), axis=0, keepdims=True) for x in g_l]
    dec_l = [jnp.where(lower, jnp.exp(jnp.where(lower, x - y, 0.0)), 0.0) for x, y in zip(g_l, g_row_l)]
    kk_l = [_dot_nt(a, b) for a, b in zip(kb_l, k_l)]
    qk_l = [_dot_nt(a, b) for a, b in zip(qs_l, k_l)]
    low_l = [jnp.where(strict, a * d, 0.0) for a, d in zip(kk_l, dec_l)]
    eye_f = eye.astype(F32)
    pw_l = [-x for x in low_l]
    inv_l = [eye_f + x for x in pw_l]
    for _ in range(int(math.log2(c)) - 1):
        pw_l = _hp_l(pw_l, pw_l)
        inv_l = [a + b for a, b in zip(inv_l, _hp_l(inv_l, pw_l))]
    u_l = _hp_l(inv_l, cut(v * beta))
    w_l = _hp_l(inv_l, cut(rhs_k))
    aqk_l = [jnp.where(lower, a * d, 0.0) for a, d in zip(qk_l, dec_l)]
    g_last_l = [jnp.sum(jnp.where(rowid == c - 1, x, 0.0), axis=0, keepdims=True) for x in g_l]
    ekd_l = [jnp.exp(a - b) for a, b in zip(g_last_l, g_l)]
    kd_l = [a * b for a, b in zip(k_l, ekd_l)]
    return dict(eye=eye, lower=lower, strict=strict, dec=dec_l, k=k_l, kb=kb_l, qs=qs_l, low=low_l, inv=inv_l,
                eg=cut(eg), rhs_k=cut(rhs_k), u=u_l, w=w_l, aqk=aqk_l, g_last=g_last_l, qd=cut(qs * eg),
                ekd=ekd_l, kd=kd_l)


def _dn_head_cols(gb_blk, head):
    lane = lax.broadcasted_iota(jnp.int32, (1, LANES), 1)
    g = jnp.sum(jnp.where(lane == head, gb_blk, 0.0), axis=-1, keepdims=True)
    beta = jnp.sum(jnp.where(lane == head + DN_HEADS, gb_blk, 0.0), axis=-1, keepdims=True)
    return g, beta


def _dn_delta_fwd(qkv, gb, p, o_gain, *, name):
    T = qkv.shape[0]
    tb = min(DN_TB, T)
    nb, nc = T // tb, tb // DN_CHUNK
    H = DN_HEADS
    qscale = DN_DK ** -0.5

    def body(q_ref, k_ref, v_ref, gb_ref, gate_ref, gain_ref, o_ref, og_ref, st_ref, s_ref):
        head = pl.program_id(0)

        @pl.when(pl.program_id(1) == 0)
        def _():
            s_ref[...] = jnp.zeros_like(s_ref)

        g, beta = _dn_head_cols(gb_ref[...], head)
        t = _dn_local(q_ref[...] * qscale, k_ref[...], v_ref[...], g, beta, nc)
        s32 = s_ref[...]
        outs = []
        for i in range(nc):
            s_bf = _bf(s32)
            st_ref[i] = s_bf
            ws = _dot(t["w"][i], s_bf)
            qds = _dot(t["qd"][i], s_bf)
            vn = t["u"][i] - ws
            outs.append(qds + _dot(t["aqk"][i], vn))
            s32 = s32 * jnp.exp(t["g_last"][i]) + _dot_tn(t["kd"][i], vn)
        s_ref[...] = s32
        o = jnp.concatenate(outs, axis=0)
        o_ref[...] = o
        r = lax.rsqrt(jnp.mean(o * o, axis=-1, keepdims=True) + RMS_EPS)
        og_ref[...] = (((o * r) * gain_ref[...]) * _silu(gate_ref[...])).astype(BF16)

    qk = lambda off: pl.BlockSpec((tb, DN_DK), lambda h, i: (i, off + h))
    vblk = lambda off: pl.BlockSpec((tb, DN_DV), lambda h, i: (i, off + h))
    return pl.pallas_call(
        body, name=name, grid=(H, nb),
        in_specs=[qk(0), qk(H), vblk(2 * DN_QK_W // DN_DV), pl.BlockSpec((tb, LANES), lambda h, i: (i, 0)),
                  vblk(DN_CONV_W // DN_DV), pl.BlockSpec((1, DN_DV), lambda h, i: (0, 0))],
        out_specs=[vblk(0), vblk(0), pl.BlockSpec((None, nc, DN_DK, DN_DV), lambda h, i: (h, i, 0, 0))],
        out_shape=[jax.ShapeDtypeStruct((T, DN_V_W), F32), jax.ShapeDtypeStruct((T, DN_V_W), BF16),
                   jax.ShapeDtypeStruct((H, T // DN_CHUNK, DN_DK, DN_DV), BF16)],
        scratch_shapes=[pltpu.VMEM((DN_DK, DN_DV), F32)],
        compiler_params=pltpu.CompilerParams(dimension_semantics=("parallel", "arbitrary")),
    )(qkv, qkv, qkv, gb, p, o_gain)


def _dn_delta_bwd(qkv, gb, p, o_gain, o, states, dog, *, name):
    T = qkv.shape[0]
    tb = min(DN_TB, T)
    nb, nc = T // tb, tb // DN_CHUNK
    H = DN_HEADS
    qscale = DN_DK ** -0.5

    def body(q_ref, k_ref, v_ref, gb_ref, gate_ref, gain_ref, o_ref, st_ref, dog_ref,
             dq_ref, dk_ref, dv_ref, dgate_ref, dgb_ref, dgain_ref, ds_ref):
        head = pl.program_id(0)

        @pl.when(pl.program_id(1) == 0)
        def _():
            ds_ref[...] = jnp.zeros_like(ds_ref)

        @pl.when(jnp.logical_and(head == 0, pl.program_id(1) == 0))
        def _():
            dgain_ref[...] = jnp.zeros_like(dgain_ref)

        lane = lax.broadcasted_iota(jnp.int32, (1, LANES), 1)
        c = DN_CHUNK
        cut = lambda x: [x[i * c:(i + 1) * c] for i in range(nc)]
        cat = lambda xs: jnp.concatenate(xs, axis=0)
        rsum = lambda x: jnp.sum(x, axis=-1, keepdims=True)
        g, beta = _dn_head_cols(gb_ref[...], head)
        ov, gate, gain, dogv = o_ref[...], gate_ref[...], gain_ref[...], dog_ref[...]
        r = lax.rsqrt(jnp.mean(ov * ov, axis=-1, keepdims=True) + RMS_EPS)
        oh = ov * r
        dnrm = dogv * _silu(gate)
        dgate_ref[...] = dogv * (oh * gain) * _dsilu(gate)
        doh = dnrm * gain
        do_l = cut(r * (doh - oh * jnp.mean(doh * oh, axis=-1, keepdims=True)))
        dgain_ref[...] += jnp.sum(dnrm * oh, axis=0, keepdims=True)
        k, v = k_ref[...], v_ref[...]
        t = _dn_local(q_ref[...] * qscale, k, v, g, beta, nc)
        lower, strict, eye = t["lower"], t["strict"], t["eye"]
        s_l = [st_ref[i] for i in range(nc)]
        vn_l = [u - _dot(w, s) for u, w, s in zip(t["u"], t["w"], s_l)]
        dqd_l = [_dot_nt(a, s) for a, s in zip(do_l, s_l)]
        daqk_l = [_dot_nt(a, b) for a, b in zip(do_l, vn_l)]
        aqk_do_l = [_dot_tn(a, b) for a, b in zip(t["aqk"], do_l)]
        qd_do_l = [_dot_tn(a, b) for a, b in zip(t["qd"], do_l)]
        egl_l = [jnp.exp(x) for x in t["g_last"]]
        ds = ds_ref[...]
        dvn_l, dkd_l, dgl_l = [None] * nc, [None] * nc, [None] * nc
        for i in reversed(range(nc)):
            dvn_l[i] = aqk_do_l[i] + _dot(t["kd"][i], ds)
            dkd_l[i] = _dot_nt(vn_l[i], ds)
            dgl_l[i] = jnp.sum(rsum(ds * s_l[i].astype(F32)), axis=0, keepdims=True) * egl_l[i]
            ds = ds * egl_l[i] + qd_do_l[i] - _dot_tn(t["w"][i], dvn_l[i])
        ds_ref[...] = ds
        dw_l = [-_dot_nt(a, s) for a, s in zip(dvn_l, s_l)]
        dbv_l = _hp_l(t["inv"], dvn_l, _TN)
        dbk_l = _hp_l(t["inv"], dw_l, _TN)
        dlow_l = [-(a + b) for a, b in zip(_hp_l(dbv_l, t["u"], _NT), _hp_l(dbk_l, t["w"], _NT))]
        m_l = [jnp.where(strict, a * d, 0.0) for a, d in zip(dlow_l, t["dec"])]
        nmat_l = [jnp.where(lower, a * d, 0.0) for a, d in zip(daqk_l, t["dec"])]
        dkb_l = [_dot(m, kk) + b * e for m, kk, b, e in zip(m_l, t["k"], dbk_l, t["eg"])]
        dqs_l = [_dot(n, kk) + a * e for n, kk, a, e in zip(nmat_l, t["k"], dqd_l, t["eg"])]
        dk1_l = [_dot_tn(m, kb) for m, kb in zip(m_l, t["kb"])]
        dk2_l = [_dot_tn(n, q) for n, q in zip(nmat_l, t["qs"])]
        beta_l, v_l = cut(beta), cut(v)
        rowid = lax.broadcasted_iota(jnp.int32, (c, 1), 0)
        dk_l, dg_l, dbeta_l = [], [], []
        for i in range(nc):
            dk_l.append(dk1_l[i] + dk2_l[i] + dkd_l[i] * t["ekd"][i] + dkb_l[i] * beta_l[i])
            gmat = jnp.where(strict, dlow_l[i] * t["low"][i], 0.0) + daqk_l[i] * t["aqk"][i]
            s_kd = rsum(dkd_l[i] * t["kd"][i])
            dg = (rsum(gmat) + rsum(dqd_l[i] * t["qd"][i]) - s_kd + rsum(dbk_l[i] * t["rhs_k"][i]))
            dg_row = -jnp.sum(gmat, axis=0, keepdims=True)
            dg = dg + rsum(jnp.where(eye, dg_row, 0.0))
            dgl = dgl_l[i] + jnp.sum(s_kd, axis=0, keepdims=True)
            dg_l.append(dg + jnp.where(rowid == c - 1, dgl, 0.0))
            dbeta_l.append(rsum(dbv_l[i] * v_l[i]) + rsum(dkb_l[i] * t["k"][i]))
        dq_ref[...] = cat(dqs_l) * qscale
        dk_ref[...] = cat(dk_l)
        dv_ref[...] = cat(dbv_l) * beta
        dgb_ref[...] = (jnp.where(lane == head, cat(dg_l), 0.0)
                        + jnp.where(lane == head + DN_HEADS, cat(dbeta_l), 0.0))

    rev = lambda i: nb - 1 - i
    qk = lambda off: pl.BlockSpec((tb, DN_DK), lambda h, i: (rev(i), off + h))
    vblk = lambda off: pl.BlockSpec((tb, DN_DV), lambda h, i: (rev(i), off + h))
    gain_spec = pl.BlockSpec((1, DN_DV), lambda h, i: (0, 0))
    return pl.pallas_call(
        body, name=name, grid=(H, nb),
        in_specs=[qk(0), qk(H), vblk(2 * DN_QK_W // DN_DV), pl.BlockSpec((tb, LANES), lambda h, i: (rev(i), 0)),
                  vblk(DN_CONV_W // DN_DV), gain_spec, vblk(0),
                  pl.BlockSpec((None, nc, DN_DK, DN_DV), lambda h, i: (h, rev(i), 0, 0)), vblk(0)],
        out_specs=[qk(0), qk(0), vblk(0), vblk(DN_CONV_W // DN_DV),
                   pl.BlockSpec((None, tb, LANES), lambda h, i: (h, rev(i), 0)), gain_spec],
        out_shape=[jax.ShapeDtypeStruct((T, DN_QK_W), F32), jax.ShapeDtypeStruct((T, DN_QK_W), F32),
                   jax.ShapeDtypeStruct((T, DN_V_W), F32), jax.ShapeDtypeStruct((T, DN_IN_PAD), F32),
                   jax.ShapeDtypeStruct((H, T, LANES), F32), jax.ShapeDtypeStruct((1, DN_DV), F32)],
        scratch_shapes=[pltpu.VMEM((DN_DK, DN_DV), F32)],
        compiler_params=pltpu.CompilerParams(dimension_semantics=("arbitrary", "arbitrary")),
    )(qkv, qkv, qkv, gb, p, o_gain, o, states, dog)


def _dn_conv_bwd(p, conv_w, dq, dk, dv, dp, *, name):
    T = p.shape[0]
    cw = conv_w.shape[1]
    n_q = DN_QK_W // LANES
    n_v = DN_V_W // LANES

    def body(p_ref, w_ref, dq_ref, dk_ref, dv_ref, dp_in, dp_ref, dw_ref):
        del dp_in
        j = pl.program_id(0)
        x = p_ref[...]
        ksz = w_ref.shape[0]
        xs = [_shift_down(x, ksz - 1 - i) for i in range(ksz)]
        xc = sum(w_ref[i:i + 1, :] * xs[i] for i in range(ksz))
        s = _silu(xc)
        r = lax.rsqrt(jnp.sum(s * s, axis=-1, keepdims=True) + L2_EPS)
        y = s * r
        dn = jnp.where(j < n_q, dq_ref[...], dk_ref[...])
        ds_qk = r * (dn - y * jnp.sum(dn * y, axis=-1, keepdims=True))
        ds = jnp.where(j < 2 * n_q, ds_qk, dv_ref[...])
        dxc = ds * _dsilu(xc)
        dp_ref[...] = sum(w_ref[i:i + 1, :] * _shift_up(dxc, ksz - 1 - i) for i in range(ksz))
        for i in range(ksz):
            dw_ref[i:i + 1, :] = jnp.sum(dxc * xs[i], axis=0, keepdims=True)

    colblk = pl.BlockSpec((T, LANES), lambda j: (0, j))
    wblk = pl.BlockSpec((DN_CONV, LANES), lambda j: (0, j))
    return pl.pallas_call(
        body, name=name, grid=(cw // LANES,),
        in_specs=[colblk, wblk,
                  pl.BlockSpec((T, LANES), lambda j: (0, jnp.minimum(j, n_q - 1))),
                  pl.BlockSpec((T, LANES), lambda j: (0, jnp.clip(j - n_q, 0, n_q - 1))),
                  pl.BlockSpec((T, LANES), lambda j: (0, jnp.clip(j - 2 * n_q, 0, n_v - 1))),
                  pl.BlockSpec(memory_space=pl.ANY)],
        out_specs=[colblk, wblk],
        out_shape=[jax.ShapeDtypeStruct(dp.shape, F32), jax.ShapeDtypeStruct((DN_CONV, cw), F32)],
        input_output_aliases={5: 0},
        compiler_params=pltpu.CompilerParams(dimension_semantics=("parallel",), vmem_limit_bytes=VMEM_BIG),
    )(p, conv_w, dq, dk, dv, dp)


def _dn_ab_bwd(p, alog_row, dtb_row, dgb, dp, *, name):
    T = p.shape[0]
    rows = min(DN_TB, T)
    H = DN_HEADS

    def body(p_ref, al_ref, dt_ref, dgb_ref, dp_in, dp_ref, dal_ref, ddt_ref):
        del dp_in

        @pl.when(pl.program_id(0) == 0)
        def _():
            dal_ref[...] = jnp.zeros_like(dal_ref)
            ddt_ref[...] = jnp.zeros_like(ddt_ref)

        blk = p_ref[...]
        is_a, is_b, a_arg, neg_exp, log_a, beta = _dn_ab_parts(blk, al_ref[...], dt_ref[...])
        d = dgb_ref[0]
        for hh in range(1, H):
            d = d + dgb_ref[hh]
        hi, mid, lo_ = _split3(jnp.where(is_a, d, 0.0))
        tri = _dn_chunk_tri(rows, upper=True)
        f = lambda q: jnp.dot(tri, q, preferred_element_type=F32)
        dlog_a = f(hi) + f(mid) + f(lo_)
        da_in = dlog_a * neg_exp * _sigmoid(a_arg)
        db_in = jnp.where(is_b, d, 0.0) * beta * (1.0 - beta)
        dp_ref[...] = jnp.where(is_a, da_in, 0.0) + db_in
        dal_ref[...] += jnp.sum(dlog_a * log_a, axis=0, keepdims=True)
        ddt_ref[...] += jnp.sum(jnp.where(is_a, da_in, 0.0), axis=0, keepdims=True)

    blk = pl.BlockSpec((rows, LANES), lambda i: (i, DN_AB_COL))
    vec = pl.BlockSpec((1, LANES), lambda i: (0, 0))
    return pl.pallas_call(
        body, name=name, grid=(T // rows,),
        in_specs=[blk, vec, vec, pl.BlockSpec((H, rows, LANES), lambda i: (0, i, 0)),
                  pl.BlockSpec(memory_space=pl.ANY)],
        out_specs=[blk, vec, vec],
        out_shape=[jax.ShapeDtypeStruct(dp.shape, F32), jax.ShapeDtypeStruct((1, LANES), F32),
                   jax.ShapeDtypeStruct((1, LANES), F32)],
        input_output_aliases={4: 0},
        compiler_params=pltpu.CompilerParams(dimension_semantics=("arbitrary",)),
    )(p, alog_row, dtb_row, dgb, dp)


def _dn_layer_fwd(x, ng, w_in, conv_w, a_log, dt_bias, o_gain, w_out, tag):
    alog_row, dtb_row = _dn_lane_rows(a_log, dt_bias)
    gain = o_gain.reshape(1, DN_DV)
    h = _rmsnorm_fwd(x, ng, name=f"{tag}_norm")
    p = _matmul(h, w_in, mode="nn", name=f"{tag}_inproj")
    qkv = _dn_prep_fwd(p, conv_w, name=f"{tag}_prep")
    gb = _dn_ab_fwd(p, alog_row, dtb_row, name=f"{tag}_ab")
    o, og, states = _dn_delta_fwd(qkv, gb, p, gain, name=f"{tag}_delta")
    x_new = _matmul(og, w_out, mode="nn", res=x, name=f"{tag}_outproj")
    return x_new, (h, p, qkv, gb, o, og, states)


def _dn_layer_bwd(dx, x, ng, w_in, conv_w, a_log, dt_bias, o_gain, w_out, saved, tag):
    h, p, qkv, gb, o, og, states = saved
    alog_row, dtb_row = _dn_lane_rows(a_log, dt_bias)
    gain = o_gain.reshape(1, DN_DV)
    d_wout = _matmul(og, dx, mode="tn", out_dtype=BF16, name=f"{tag}_dwout")
    dog = _matmul(dx, w_out, mode="nt", name=f"{tag}_dog")
    dq, dk, dv, dp, dgb, dgain = _dn_delta_bwd(qkv, gb, p, gain, o, states, dog, name=f"{tag}_deltabwd")
    dp, dconv = _dn_conv_bwd(p, conv_w, dq, dk, dv, dp, name=f"{tag}_convbwd")
    dp, dal, ddt = _dn_ab_bwd(p, alog_row, dtb_row, dgb, dp, name=f"{tag}_abbwd")
    d_win = _matmul(h, dp, mode="tn", name=f"{tag}_dwin")
    dh = _matmul(dp, w_in, mode="nt", name=f"{tag}_dh")
    dx_prev, dng = _rmsnorm_bwd(x, ng, dh, dx, name=f"{tag}_normbwd")
    return dx_prev, dng, d_win, dconv, dal[0, :DN_HEADS], ddt[0, :DN_HEADS], dgain[0], d_wout


def _sb_gains(g):
    return jnp.concatenate([g, g]).reshape(1, LANES)


def _sb_layer_fwd(x, ng, w_in, gq, gk, w_out, tag):
    h = _rmsnorm_fwd(x, ng, name=f"{tag}_norm")
    p3 = _matmul(h, w_in, mode="nn", b_parts=4, out_parts=4, name=f"{tag}_inproj")
    og, o, lsum = _sb_attn_fwd(p3, _sb_gains(gq), _sb_gains(gk), name=f"{tag}_attn")
    x_new = _matmul(og, w_out, mode="nn", res=x, name=f"{tag}_outproj")
    return x_new, (h, p3, og, o, lsum)


def _sb_layer_bwd(dx, x, ng, w_in, gq, gk, w_out, saved, tag):
    h, p3, og, o, lsum = saved
    d_wout = _matmul(og, dx, mode="tn", out_dtype=BF16, name=f"{tag}_dwout")
    dog = _matmul(dx, w_out, mode="nt", name=f"{tag}_dog")
    dp3, dgq, dgk = _sb_attn_bwd(p3, _sb_gains(gq), _sb_gains(gk), o, lsum, dog, name=f"{tag}_attnbwd")
    fold = lambda d: jnp.sum(d.reshape(-1, SB_DH), axis=0)
    d_win = _matmul(h, dp3, mode="tn", b_parts=4, out_parts=4, out_dtype=BF16, name=f"{tag}_dwin")
    dh = _matmul(dp3, w_in, mode="nt", a_parts=4, b_parts=4, name=f"{tag}_dh")
    dx_prev, dng = _rmsnorm_bwd(x, ng, dh, dx, name=f"{tag}_normbwd")
    return dx_prev, dng, d_win, fold(dgq), fold(dgk), d_wout


N_CHIPS = 4
HBM = pl.BlockSpec(memory_space=pl.ANY)


def _mesh_pos():
    return lax.axis_index("x"), lax.axis_index("y"), lax.axis_index("c")


def _other_chips(x, y):
    return [(1 - x, y), (x, 1 - y), (1 - x, 1 - y)]


def _chip_exchange(srcs, *, send_slot_is_dest, copy_own, name):
    n = len(srcs)

    def body(*refs):
        src_refs, out_refs = refs[:n], refs[n:2 * n]
        send_sems, recv_sems, local_sems = refs[2 * n:]
        x, y, c = _mesh_pos()
        me = 2 * x + y
        chips = _other_chips(x, y)
        local = []
        for a in range(n):
            if not copy_own[a]:
                continue
            own = src_refs[a].at[me] if send_slot_is_dest else src_refs[a]
            local.append(pltpu.make_async_copy(own, out_refs[a].at[me], local_sems.at[a]))
        for cp in local:
            cp.start()

        def copy(a, k, landing_slot):
            px, py = chips[k]
            src = src_refs[a].at[2 * px + py] if send_slot_is_dest else src_refs[a]
            return pltpu.make_async_remote_copy(
                src_ref=src, dst_ref=out_refs[a].at[landing_slot],
                send_sem=send_sems.at[a * 3 + k], recv_sem=recv_sems.at[a * 3 + k],
                device_id=(px, py, c), device_id_type=MESH)

        sends = [copy(a, k, me) for a in range(n) for k in range(3)]
        for cp in sends:
            cp.start()
        for a in range(n):
            for k in range(3):
                px, py = chips[k]
                copy(a, k, 2 * px + py).wait_recv()
        for cp in sends:
            cp.wait_send()
        for cp in local:
            cp.wait()

    outs = []
    for s in srcs:
        shape = s.shape if send_slot_is_dest else (N_CHIPS,) + s.shape
        outs.append(jax.ShapeDtypeStruct(shape, s.dtype))
    return pl.pallas_call(
        body, name=name, in_specs=[HBM] * n, out_specs=[HBM] * n, out_shape=outs,
        scratch_shapes=[pltpu.SemaphoreType.DMA((3 * n,)), pltpu.SemaphoreType.DMA((3 * n,)),
                        pltpu.SemaphoreType.DMA((n,))],
    )(*srcs)


def _sibling_exchange(srcs, *, name):
    n = len(srcs)

    def body(*refs):
        src_refs, out_refs = refs[:n], refs[n:2 * n]
        send_sems, recv_sems = refs[2 * n:]
        x, y, c = _mesh_pos()
        copies = [pltpu.make_async_remote_copy(
            src_ref=src_refs[a], dst_ref=out_refs[a], send_sem=send_sems.at[a], recv_sem=recv_sems.at[a],
            device_id=(x, y, 1 - c), device_id_type=MESH) for a in range(n)]
        for cp in copies:
            cp.start()
        for cp in copies:
            cp.wait()

    return pl.pallas_call(
        body, name=name, in_specs=[HBM] * n, out_specs=[HBM] * n,
        out_shape=[jax.ShapeDtypeStruct(s.shape, s.dtype) for s in srcs],
        scratch_shapes=[pltpu.SemaphoreType.DMA((n,)), pltpu.SemaphoreType.DMA((n,))],
    )(*srcs)


def _gather_halves(shards, small, *, name):
    n = len(shards)

    def body(*refs):
        s_refs, small_ref = refs[:n], refs[n]
        o_refs, osmall_ref = refs[n + 1:2 * n + 1], refs[2 * n + 1]
        send_sems, recv_sems, local_sems = refs[2 * n + 2:]
        x, y, c = _mesh_pos()
        me = 2 * x + y
        chips = _other_chips(x, y)
        local = [pltpu.make_async_copy(s_refs[a], o_refs[a].at[me], local_sems.at[a]) for a in range(n)]
        local.append(pltpu.make_async_copy(small_ref, osmall_ref.at[me], local_sems.at[n]))
        for cp in local:
            cp.start()

        def over_ici(a, k, slot):
            px, py = chips[k]
            return pltpu.make_async_remote_copy(
                src_ref=s_refs[a].at[c], dst_ref=o_refs[a].at[slot, c], send_sem=send_sems.at[3 * a + k],
                recv_sem=recv_sems.at[3 * a + k], device_id=(px, py, c), device_id_type=MESH)

        def small_copy(k, slot):
            px, py = chips[k]
            return pltpu.make_async_remote_copy(
                src_ref=small_ref, dst_ref=osmall_ref.at[slot], send_sem=send_sems.at[3 * n + k],
                recv_sem=recv_sems.at[3 * n + k], device_id=(px, py, c), device_id_type=MESH)

        def to_sibling(a, k, half):
            px, py = chips[k]
            blk = o_refs[a].at[2 * px + py, half]
            return pltpu.make_async_remote_copy(
                src_ref=blk, dst_ref=blk, send_sem=send_sems.at[3 * n + 3 + 3 * a + k],
                recv_sem=recv_sems.at[3 * n + 3 + 3 * a + k], device_id=(x, y, 1 - c), device_id_type=MESH)

        sends = [over_ici(a, k, me) for a in range(n) for k in range(3)] + [small_copy(k, me) for k in range(3)]
        for cp in sends:
            cp.start()
        passed = []
        for a in range(n):
            for k in range(3):
                px, py = chips[k]
                over_ici(a, k, 2 * px + py).wait_recv()
                passed.append(to_sibling(a, k, c))
                passed[-1].start()
        for k in range(3):
            px, py = chips[k]
            small_copy(k, 2 * px + py).wait_recv()
        for a in range(n):
            for k in range(3):
                to_sibling(a, k, 1 - c).wait_recv()
        for cp in sends + passed:
            cp.wait_send()
        for cp in local:
            cp.wait()

    nsem = 6 * n + 3
    return pl.pallas_call(
        body, name=name, in_specs=[HBM] * (n + 1), out_specs=[HBM] * (n + 1),
        out_shape=[jax.ShapeDtypeStruct((N_CHIPS,) + s.shape, s.dtype) for s in shards + [small]],
        scratch_shapes=[pltpu.SemaphoreType.DMA((nsem,)), pltpu.SemaphoreType.DMA((nsem,)),
                        pltpu.SemaphoreType.DMA((n + 1,))],
    )(*shards, small)


def _swap_other_half(g_list, *, name):
    n = len(g_list)

    def body(*refs):
        g_refs, o_refs = refs[:n], refs[n:2 * n]
        send_sems, recv_sems = refs[2 * n:]
        x, y, c = _mesh_pos()
        copies = [pltpu.make_async_remote_copy(
            src_ref=g_refs[a].at[:, 1 - c], dst_ref=o_refs[a], send_sem=send_sems.at[a], recv_sem=recv_sems.at[a],
            device_id=(x, y, 1 - c), device_id_type=MESH) for a in range(n)]
        for cp in copies:
            cp.start()
        for cp in copies:
            cp.wait()

    return pl.pallas_call(
        body, name=name, in_specs=[HBM] * n, out_specs=[HBM] * n,
        out_shape=[jax.ShapeDtypeStruct((g.shape[0],) + g.shape[2:], g.dtype) for g in g_list],
        scratch_shapes=[pltpu.SemaphoreType.DMA((n,)), pltpu.SemaphoreType.DMA((n,))],
    )(*g_list)


def _row_tile(r):
    return _pick(r, (512, 256, 128, 64, 32, 16, 8))


def _add_my_half(g4, sib4, core, *, name):
    n, _, r, C = g4.shape
    tr = _row_tile(r)

    def body(core_ref, g_ref, s_ref, o_ref):
        del core_ref
        o_ref[...] = (g_ref[...].astype(F32) + s_ref[...].astype(F32)).astype(o_ref.dtype)

    return pl.pallas_call(
        body, name=name,
        grid_spec=pltpu.PrefetchScalarGridSpec(
            num_scalar_prefetch=1, grid=(n, r // tr),
            in_specs=[pl.BlockSpec((None, None, tr, C), lambda j, i, core_ref: (j, core_ref[0], i, 0)),
                      pl.BlockSpec((None, tr, C), lambda j, i, core_ref: (j, i, 0))],
            out_specs=pl.BlockSpec((None, tr, C), lambda j, i, core_ref: (j, i, 0))),
        out_shape=jax.ShapeDtypeStruct((n, r, C), g4.dtype),
        compiler_params=pltpu.CompilerParams(dimension_semantics=("parallel", "parallel")),
    )(core, g4, sib4)


def _scatter_to_chips(p_list, *, name):
    n = len(p_list)

    def body(*refs):
        p_refs, o_refs = refs[:n], refs[n:2 * n]
        send_sems, recv_sems, local_sems = refs[2 * n:]
        x, y, c = _mesh_pos()
        me = 2 * x + y
        chips = _other_chips(x, y)
        own = [pltpu.make_async_copy(p_refs[a].at[me], o_refs[a].at[me], local_sems.at[a]) for a in range(n)]
        for cp in own:
            cp.start()

        def copy(a, k, landing_slot):
            px, py = chips[k]
            return pltpu.make_async_remote_copy(
                src_ref=p_refs[a].at[2 * px + py], dst_ref=o_refs[a].at[landing_slot],
                send_sem=send_sems.at[3 * a + k], recv_sem=recv_sems.at[3 * a + k], device_id=(px, py, c),
                device_id_type=MESH)

        sends = [copy(a, k, me) for a in range(n) for k in range(3)]
        for cp in sends:
            cp.start()
        for a in range(n):
            for k in range(3):
                px, py = chips[k]
                copy(a, k, 2 * px + py).wait_recv()
        for cp in sends:
            cp.wait_send()
        for cp in own:
            cp.wait()

    return pl.pallas_call(
        body, name=name, in_specs=[HBM] * n, out_specs=[HBM] * n,
        out_shape=[jax.ShapeDtypeStruct(p.shape, p.dtype) for p in p_list],
        scratch_shapes=[pltpu.SemaphoreType.DMA((3 * n,)), pltpu.SemaphoreType.DMA((3 * n,)),
                        pltpu.SemaphoreType.DMA((n,))],
    )(*p_list)


def _sum_chips(r4, *, name):
    _, r, C = r4.shape
    tr = _row_tile(r)

    def body(r_ref, o_ref):
        f = lambda j: r_ref[j].astype(F32)
        o_ref[...] = ((f(0) + f(1)) + f(2)) + f(3)

    return pl.pallas_call(
        body, name=name, grid=(r // tr,), in_specs=[pl.BlockSpec((N_CHIPS, tr, C), lambda i: (0, i, 0))],
        out_specs=pl.BlockSpec((tr, C), lambda i: (i, 0)), out_shape=jax.ShapeDtypeStruct((r, C), F32),
        compiler_params=pltpu.CompilerParams(dimension_semantics=("parallel",)),
    )(r4)


def _join_halves(m_list, *, name):
    n = len(m_list)

    def body(*refs):
        m_refs, o_refs = refs[:n], refs[n:2 * n]
        send_sems, recv_sems, local_sems = refs[2 * n:]
        x, y, c = _mesh_pos()
        own = [pltpu.make_async_copy(m_refs[a], o_refs[a].at[c], local_sems.at[a]) for a in range(n)]
        for cp in own:
            cp.start()

        def copy(a, half):
            return pltpu.make_async_remote_copy(
                src_ref=m_refs[a], dst_ref=o_refs[a].at[half], send_sem=send_sems.at[a], recv_sem=recv_sems.at[a],
                device_id=(x, y, 1 - c), device_id_type=MESH)

        sends = [copy(a, c) for a in range(n)]
        for cp in sends:
            cp.start()
        for a in range(n):
            copy(a, 1 - c).wait_recv()
        for cp in sends:
            cp.wait_send()
        for cp in own:
            cp.wait()

    return pl.pallas_call(
        body, name=name, in_specs=[HBM] * n, out_specs=[HBM] * n,
        out_shape=[jax.ShapeDtypeStruct((2,) + m.shape, m.dtype) for m in m_list],
        scratch_shapes=[pltpu.SemaphoreType.DMA((n,)), pltpu.SemaphoreType.DMA((n,)), pltpu.SemaphoreType.DMA((n,))],
    )(*m_list)


def _sum_small(recv4, *, name):
    _, R, C = recv4.shape

    def body(r_ref, o_ref):
        o_ref[...] = ((r_ref[0] + r_ref[1]) + r_ref[2]) + r_ref[3]

    return pl.pallas_call(body, name=name, out_shape=jax.ShapeDtypeStruct((R, C), F32))(recv4)


def _add(a, b, *, name):
    R, C = a.shape
    tr = _pick(R, (512, 256, 128, 64, 32, 16, 8))
    blk = pl.BlockSpec((tr, C), lambda i: (i, 0))

    def body(a_ref, b_ref, o_ref):
        o_ref[...] = a_ref[...] + b_ref[...]

    return pl.pallas_call(body, name=name, grid=(R // tr,), in_specs=[blk, blk], out_specs=blk,
                          out_shape=jax.ShapeDtypeStruct((R, C), F32),
                          compiler_params=pltpu.CompilerParams(dimension_semantics=("parallel",)))(a, b)


def _adamw(w, g, m, v, *, name):
    shape = w.shape
    C = shape[-1]
    R = w.size // C
    two = lambda a: a.reshape(R, C)
    tr = _pick(R, (256, 128, 64, 32, 16, 8)) if R % 8 == 0 and R > 8 else R
    blk = pl.BlockSpec((tr, C), lambda i: (i, 0))

    def body(w_ref, g_ref, m_ref, v_ref, d_ref, nm_ref, nv_ref):
        gv = g_ref[...]
        nm = ADAM_B1 * m_ref[...] + (1.0 - ADAM_B1) * gv
        nv = ADAM_B2 * v_ref[...] + (1.0 - ADAM_B2) * (gv * gv)
        m_hat = nm / (1.0 - ADAM_B1 ** ADAM_STEP)
        v_hat = nv / (1.0 - ADAM_B2 ** ADAM_STEP)
        d_ref[...] = -ADAM_LR * (m_hat / (jnp.sqrt(v_hat) + ADAM_EPS) + ADAM_WD * w_ref[...])
        nm_ref[...] = nm
        nv_ref[...] = nv

    out = jax.ShapeDtypeStruct((R, C), F32)
    d, nm, nv = pl.pallas_call(
        body, name=name, grid=(R // tr,), in_specs=[blk] * 4, out_specs=[blk] * 3, out_shape=[out] * 3,
        compiler_params=pltpu.CompilerParams(dimension_semantics=("parallel",)),
    )(two(w), two(g), two(m), two(v))
    return d.reshape(shape), nm.reshape(shape), nv.reshape(shape)


BIG = (("dn_w_in", (2, 1024, 1540), 2), ("dn_w_out", (2, 512, 1024), 1), ("sb_w_in", (1, 1024, 1024), 2),
       ("sb_w_out", (1, 256, 1024), 1), ("sc_w_in", (1, 1024, 2048), 2), ("sc_w_out", (1, 512, 1024), 1))
SMALL = (("dn_conv_w", (2, 4, 1024), 2), ("dn_o_norm_g", (2, 64), 1), ("sc_conv_w", (1, 3, 512), 2))
REPL = (("norm_g", (4, 1024)), ("dn_a_log", (2, 8)), ("dn_dt_bias", (2, 8)), ("sb_q_norm_g", (1, 64)),
        ("sb_k_norm_g", (1, 64)))


def _halves(shard):
    return shard.reshape(2, -1, shard.shape[-1])


def _pack(arrays, cols, lead=()):
    flat = jnp.concatenate([a.reshape(lead + (-1,)) for a in arrays], axis=-1)
    n = flat.shape[-1]
    rows = -(-n // cols)
    unit = 512 if rows > 512 else 8
    rows = -(-rows // unit) * unit
    flat = jnp.pad(flat, [(0, 0)] * len(lead) + [(0, rows * cols - n)])
    return flat.reshape(lead + (rows, cols))


def _unpack(buf, table, lead=()):
    flat = buf.reshape(lead + (-1,))
    out, off = {}, 0
    for entry in table:
        name, shape = entry[0], entry[1]
        n = math.prod(shape)
        out[name] = flat[..., off:off + n].reshape(lead + shape)
        off += n
    return out


def _join(shards, axis):
    return jnp.concatenate([shards[j] for j in range(N_CHIPS)], axis=axis)


def _split(full, axis):
    return jnp.stack(jnp.split(full, N_CHIPS, axis=axis), axis=0)


def kernel(x, norm_g, dn_w_in, dn_conv_w, dn_a_log, dn_dt_bias, dn_o_norm_g, dn_w_out, sb_w_in, sb_q_norm_g, sb_k_norm_g, sb_w_out, sc_w_in, sc_conv_w, sc_w_out, loss_target, m_norm_g, m_dn_w_in, m_dn_conv_w, m_dn_a_log, m_dn_dt_bias, m_dn_o_norm_g, m_dn_w_out, m_sb_w_in, m_sb_q_norm_g, m_sb_k_norm_g, m_sb_w_out, m_sc_w_in, m_sc_conv_w, m_sc_w_out, v_norm_g, v_dn_w_in, v_dn_conv_w, v_dn_a_log, v_dn_dt_bias, v_dn_o_norm_g, v_dn_w_out, v_sb_w_in, v_sb_q_norm_g, v_sb_k_norm_g, v_sb_w_out, v_sc_w_in, v_sc_conv_w, v_sc_w_out):
    weights = dict(norm_g=norm_g, dn_w_in=dn_w_in, dn_conv_w=dn_conv_w, dn_a_log=dn_a_log, dn_dt_bias=dn_dt_bias,
                   dn_o_norm_g=dn_o_norm_g, dn_w_out=dn_w_out, sb_w_in=sb_w_in, sb_q_norm_g=sb_q_norm_g,
                   sb_k_norm_g=sb_k_norm_g, sb_w_out=sb_w_out, sc_w_in=sc_w_in, sc_conv_w=sc_conv_w, sc_w_out=sc_w_out)
    m_in = dict(norm_g=m_norm_g, dn_w_in=m_dn_w_in, dn_conv_w=m_dn_conv_w, dn_a_log=m_dn_a_log,
                dn_dt_bias=m_dn_dt_bias, dn_o_norm_g=m_dn_o_norm_g, dn_w_out=m_dn_w_out, sb_w_in=m_sb_w_in,
                sb_q_norm_g=m_sb_q_norm_g, sb_k_norm_g=m_sb_k_norm_g, sb_w_out=m_sb_w_out, sc_w_in=m_sc_w_in,
                sc_conv_w=m_sc_conv_w, sc_w_out=m_sc_w_out)
    v_in = dict(norm_g=v_norm_g, dn_w_in=v_dn_w_in, dn_conv_w=v_dn_conv_w, dn_a_log=v_dn_a_log,
                dn_dt_bias=v_dn_dt_bias, dn_o_norm_g=v_dn_o_norm_g, dn_w_out=v_dn_w_out, sb_w_in=v_sb_w_in,
                sb_q_norm_g=v_sb_q_norm_g, sb_k_norm_g=v_sb_k_norm_g, sb_w_out=v_sb_w_out, sc_w_in=v_sc_w_in,
                sc_conv_w=v_sc_conv_w, sc_w_out=v_sc_w_out)
    order = list(weights)
    _, _, ci = _mesh_pos()

    small = _pack([weights[n] for n, _, _ in SMALL], LANES)
    *big4, small4 = _gather_halves([_halves(weights[n].astype(BF16)) for n, _, _ in BIG], small,
                                   name="gather_weights")
    got = {n: g.reshape((N_CHIPS,) + shape) for (n, shape, _), g in zip(BIG, big4)}
    full = {n: _join(a, ax) for (n, _, ax), a in zip(SMALL, _unpack(small4, SMALL, (N_CHIPS,)).values())}
    dn_w_in_pad = jnp.pad(_join(got["dn_w_in"], 2), ((0, 0), (0, 0), (0, DN_IN_PAD - DN_IN)))
    rows_of = lambda w4: w4.reshape(-1, w4.shape[-1])

    def dn_args(j):
        return (dn_w_in_pad[j], full["dn_conv_w"][j], dn_a_log[j], dn_dt_bias[j], full["dn_o_norm_g"][j],
                rows_of(got["dn_w_out"][:, j]))

    sb_args = (got["sb_w_in"][:, 0], sb_q_norm_g[0], sb_k_norm_g[0], rows_of(got["sb_w_out"][:, 0]))
    sc_args = (got["sc_w_in"][:, 0], full["sc_conv_w"][0], rows_of(got["sc_w_out"][:, 0]))

    x0 = x[0]
    x1, s0 = _dn_layer_fwd(x0, norm_g[0], *dn_args(0), "l0")
    x2, s1 = _sb_layer_fwd(x1, norm_g[1], *sb_args, "l1")
    x3, s2 = _sc_layer_fwd(x2, norm_g[2], *sc_args, "l2")
    x4, s3 = _dn_layer_fwd(x3, norm_g[3], *dn_args(1), "l3")
    dy, loss_local = _loss_head(x4, loss_target[0], name="loss_head")
    loss = lax.psum(loss_local[0, 0], ("x", "y", "c"))

    dx3, dng3, dwin3, dconv3, dal3, ddt3, dgain3, dwout3 = _dn_layer_bwd(dy, x3, norm_g[3], *dn_args(1), s3, "l3")
    dx2, dng2, dwin2, dconv2, dwout2 = _sc_layer_bwd(dx3, x2, norm_g[2], *sc_args, s2, "l2")
    dx1, dng1, dwin1, dgq, dgk, dwout1 = _sb_layer_bwd(dx2, x1, norm_g[1], *sb_args, s1, "l1")
    dx0, dng0, dwin0, dconv0, dal0, ddt0, dgain0, dwout0 = _dn_layer_bwd(dx1, x0, norm_g[0], *dn_args(0), s0, "l0")

    grads = dict(
        norm_g=jnp.concatenate([dng0, dng1, dng2, dng3], axis=0), dn_conv_w=jnp.stack([dconv0, dconv3]),
        dn_a_log=jnp.stack([dal0, dal3]), dn_dt_bias=jnp.stack([ddt0, ddt3]),
        dn_o_norm_g=jnp.stack([dgain0, dgain3]), sb_q_norm_g=dgq[None], sb_k_norm_g=dgk[None],
        sc_conv_w=dconv2[None])
    by_cols = lambda dw: _split(dw[:, :DN_IN].astype(BF16), 1)
    by_rows = lambda dw: dw.reshape(N_CHIPS, -1, dw.shape[-1])
    cut2 = lambda g4: g4.reshape(N_CHIPS, 2, -1, g4.shape[-1])
    gbig = [jnp.stack([by_cols(dwin0), by_cols(dwin3)], axis=1), jnp.stack([by_rows(dwout0), by_rows(dwout3)], axis=1),
            cut2(dwin1), cut2(by_rows(dwout1)), cut2(dwin2), cut2(by_rows(dwout2))]

    core = ci.astype(jnp.int32).reshape(1)
    sib = _swap_other_half(gbig, name="swap_halves")
    part = [_add_my_half(g, s, core, name=f"sum_cores_{n}") for (n, _, _), g, s in zip(BIG, gbig, sib)]
    landed = _scatter_to_chips(part, name="scatter_grads")
    mine = [_sum_chips(r, name=f"sum_chips_{n}") for (n, _, _), r in zip(BIG, landed)]
    tbig = _join_halves(mine, name="join_halves")
    g_out = {n: t.reshape(shape) for (n, shape, _), t in zip(BIG, tbig)}
    repl = [jnp.broadcast_to(grads[n][None], (N_CHIPS,) + s) for n, s in REPL]
    gsmall = _pack([_split(grads[n], ax) for n, _, ax in SMALL] + repl, LANES, (N_CHIPS,))
    rsmall, = _chip_exchange([gsmall], send_slot_is_dest=True, copy_own=(True,), name="scatter_small")
    psmall = _sum_small(rsmall, name="sum_chips_small")
    qsmall, = _sibling_exchange([psmall], name="swap_cores_small")
    tsmall = _add(psmall, qsmall, name="sum_cores_small")
    g_out.update(_unpack(tsmall, SMALL + REPL))

    upd = {n: _adamw(weights[n], g_out[n], m_in[n], v_in[n], name=f"adamw_{n}") for n in order}
    return (loss, dx0[None], *[g_out[n] for n in order], *[upd[n][0] for n in order],
            *[upd[n][1] for n in order], *[upd[n][2] for n in order])
```

```python
import math

import jax
import jax.numpy as jnp
from jax import lax
from jax.experimental import pallas as pl
from jax.experimental.pallas import tpu as pltpu

F32 = jnp.float32
BF16 = jnp.bfloat16
MESH = pl.DeviceIdType.MESH

RMS_EPS = 1e-6
L2_EPS = 1e-6
LANES = 128
VMEM_BIG = 60 * 1024 * 1024
MM_VMEM = 36 * 1024 * 1024

DN_HEADS, DN_DK, DN_DV, DN_CHUNK, DN_CONV = 8, 128, 256, 64, 4
DN_QK_W = DN_HEADS * DN_DK
DN_V_W = DN_HEADS * DN_DV
DN_CONV_W = 2 * DN_QK_W + DN_V_W
DN_IN = DN_CONV_W + DN_V_W + 2 * DN_HEADS
DN_IN_PAD = DN_CONV_W + DN_V_W + LANES
SB_DH = 64
SC_CONV = 3

ADAM_LR, ADAM_B1, ADAM_B2, ADAM_EPS, ADAM_WD, ADAM_STEP = 0.001, 0.9, 0.999, 1e-08, 0.01, 10


def _pick(n, cands):
    for c in cands:
        if n % c == 0:
            return c
    raise ValueError(f"no tile for {n} in {cands}")


def _bf(x):
    return x.astype(BF16)


def _dot(a, b):
    return jnp.dot(_bf(a), _bf(b), preferred_element_type=F32)


def _dot_nt(a, b):
    return lax.dot_general(_bf(a), _bf(b), (((1,), (1,)), ((), ())), preferred_element_type=F32)


def _dot_tn(a, b):
    return lax.dot_general(_bf(a), _bf(b), (((0,), (0,)), ((), ())), preferred_element_type=F32)


def _split3(a):
    hi = _bf(a)
    r = a - hi.astype(F32)
    mid = _bf(r)
    lo = _bf(r - mid.astype(F32))
    return hi, mid, lo


def _sigmoid(x):
    return 1.0 / (1.0 + jnp.exp(-x))


def _silu(x):
    return x * _sigmoid(x)


def _dsilu(x):
    s = _sigmoid(x)
    return s * (1.0 + x * (1.0 - s))


def _softplus(x):
    return jnp.maximum(x, 0.0) + jnp.log(1.0 + jnp.exp(-jnp.abs(x)))


def _shift_down(z, k):
    if k == 0:
        return z
    row = lax.broadcasted_iota(jnp.int32, z.shape, 0)
    return jnp.where(row >= k, pltpu.roll(z, k, 0), 0.0)


def _shift_up(z, k):
    if k == 0:
        return z
    n = z.shape[0]
    row = lax.broadcasted_iota(jnp.int32, z.shape, 0)
    return jnp.where(row < n - k, pltpu.roll(z, n - k, 0), 0.0)


def _matmul(a, b, *, mode, name, res=None, a_parts=1, b_parts=1, out_parts=1, out_dtype=F32):
    def dims2(x, parts):
        if parts == 1:
            return x.shape
        assert x.shape[0] == parts
        return (x.shape[1], x.shape[2] * parts)

    ash, bsh = dims2(a, a_parts), dims2(b, b_parts)
    if mode == "nn":
        (M, K), (K2, N) = ash, bsh
        dn = (((1,), (0,)), ((), ()))
    elif mode == "nt":
        (M, K), (N, K2) = ash, bsh
        dn = (((1,), (1,)), ((), ()))
    else:
        (K, M), (K2, N) = ash, bsh
        dn = (((0,), (0,)), ((), ()))
    assert K == K2, (ash, bsh, mode)
    tm = _pick(M, (512, 256, 128, 64, 32, 16, 8))
    n_unit = N // max(out_parts, b_parts if mode != "nt" else 1)
    k_unit = K // max(a_parts if mode != "tn" else 1, b_parts if mode == "nt" else 1)
    tn, tk = min(
        ((n, k) for n in (2048, 1792, 1024, 896, 768, 512, 384, 256, 128) if n_unit % n == 0
         for k in (2048, 1792, 1024, 896, 512, 256, 128) if k_unit % k == 0
         if 2 * (tm * k * a.dtype.itemsize + k * n * b.dtype.itemsize + 2 * tm * n * 4) + tm * n * 4 <= MM_VMEM),
        key=lambda nk_: (-nk_[0] * nk_[1], -nk_[1]))
    nk = K // tk
    grid = (M // tm, N // tn, nk)

    def spec(parts, rows_are, cols_are, tr, tc, width):
        per = width // parts // tc
        if parts == 1:
            return pl.BlockSpec((tr, tc), lambda i, j, k: ((i, j, k)[rows_are], (i, j, k)[cols_are]))
        return pl.BlockSpec((None, tr, tc), lambda i, j, k: ((i, j, k)[cols_are] // per, (i, j, k)[rows_are],
                                                             (i, j, k)[cols_are] % per))

    if mode == "nn":
        a_spec = spec(a_parts, 0, 2, tm, tk, K)
        b_spec = spec(b_parts, 2, 1, tk, tn, N)
    elif mode == "nt":
        a_spec = spec(a_parts, 0, 2, tm, tk, K)
        b_spec = spec(b_parts, 1, 2, tn, tk, K)
    else:
        a_spec = spec(a_parts, 2, 0, tk, tm, M)
        b_spec = spec(b_parts, 2, 1, tk, tn, N)
    o_spec = spec(out_parts, 0, 1, tm, tn, N)
    in_specs = [a_spec, b_spec]
    operands = [a, b]
    if res is not None:
        in_specs.append(pl.BlockSpec((tm, tn), lambda i, j, k: (i, j)))
        operands.append(res)

    def finish(refs, r):
        if res is not None:
            r = refs[2][...] + r
        refs[-2 if nk > 1 else -1][...] = r.astype(out_dtype)

    def body(*refs):
        part = lax.dot_general(_bf(refs[0][...]), _bf(refs[1][...]), dn, preferred_element_type=F32)
        if nk == 1:
            finish(refs, part)
            return
        acc_ref = refs[-1]
        k = pl.program_id(2)

        @pl.when(k == 0)
        def _():
            acc_ref[...] = part

        @pl.when(jnp.logical_and(k > 0, k < nk - 1))
        def _():
            acc_ref[...] += part

        @pl.when(k == nk - 1)
        def _():
            finish(refs, acc_ref[...] + part)

    out_shape = (M, N) if out_parts == 1 else (out_parts, M, N // out_parts)
    return pl.pallas_call(
        body, name=name, grid=grid, in_specs=in_specs, out_specs=o_spec,
        out_shape=jax.ShapeDtypeStruct(out_shape, out_dtype),
        scratch_shapes=[pltpu.VMEM((tm, tn), F32)] if nk > 1 else [],
        compiler_params=pltpu.CompilerParams(dimension_semantics=("parallel", "parallel", "arbitrary"),
                                             vmem_limit_bytes=VMEM_BIG),
    )(*operands)


def _rmsnorm_fwd(x, g, *, name):
    T, D = x.shape
    tm = _pick(T, (512, 256, 128, 64, 32, 16))

    def body(x_ref, g_ref, h_ref):
        xv = x_ref[...]
        r = lax.rsqrt(jnp.mean(xv * xv, axis=-1, keepdims=True) + RMS_EPS)
        h_ref[...] = ((xv * r) * g_ref[...]).astype(BF16)

    return pl.pallas_call(
        body, name=name, grid=(T // tm,),
        in_specs=[pl.BlockSpec((tm, D), lambda i: (i, 0)), pl.BlockSpec((1, D), lambda i: (0, 0))],
        out_specs=pl.BlockSpec((tm, D), lambda i: (i, 0)),
        out_shape=jax.ShapeDtypeStruct((T, D), BF16),
    )(x, g.reshape(1, D))


def _rmsnorm_bwd(x, g, dh, dx_in, *, name):
    T, D = x.shape
    tm = _pick(T, (512, 256, 128, 64, 32, 16))

    def body(x_ref, g_ref, dh_ref, dxin_ref, dx_ref, dg_ref):
        @pl.when(pl.program_id(0) == 0)
        def _():
            dg_ref[...] = jnp.zeros_like(dg_ref)

        xv = x_ref[...]
        r = lax.rsqrt(jnp.mean(xv * xv, axis=-1, keepdims=True) + RMS_EPS)
        xh = xv * r
        dh_v = dh_ref[...]
        dxh = dh_v * g_ref[...]
        dx_ref[...] = dxin_ref[...] + r * (dxh - xh * jnp.mean(dxh * xh, axis=-1, keepdims=True))
        dg_ref[...] += jnp.sum(dh_v * xh, axis=0, keepdims=True)

    row = pl.BlockSpec((tm, D), lambda i: (i, 0))
    vec = pl.BlockSpec((1, D), lambda i: (0, 0))
    return pl.pallas_call(
        body, name=name, grid=(T // tm,),
        in_specs=[row, vec, row, row], out_specs=[row, vec],
        out_shape=[jax.ShapeDtypeStruct((T, D), F32), jax.ShapeDtypeStruct((1, D), F32)],
        compiler_params=pltpu.CompilerParams(dimension_semantics=("arbitrary",)),
    )(x, g.reshape(1, D), dh, dx_in)


def _loss_head(y, target, *, name):
    T, D = y.shape
    tm = _pick(T, (512, 256, 128, 64, 32, 16))

    def body(y_ref, t_ref, dy_ref, l_ref):
        @pl.when(pl.program_id(0) == 0)
        def _():
            l_ref[...] = jnp.zeros_like(l_ref)

        err = y_ref[...] - t_ref[...]
        dy_ref[...] = err * (1.0 / D)
        l_ref[...] += 0.5 * jnp.sum(jnp.mean(err * err, axis=-1, keepdims=True), axis=0, keepdims=True)

    row = pl.BlockSpec((tm, D), lambda i: (i, 0))
    return pl.pallas_call(
        body, name=name, grid=(T // tm,),
        in_specs=[row, row], out_specs=[row, pl.BlockSpec((1, 1), lambda i: (0, 0))],
        out_shape=[jax.ShapeDtypeStruct((T, D), F32), jax.ShapeDtypeStruct((1, 1), F32)],
        compiler_params=pltpu.CompilerParams(dimension_semantics=("arbitrary",)),
    )(y, target)


def _sc_mid_fwd(p3, conv_w, *, name):
    _, T, W = p3.shape
    K = conv_w.shape[0]
    cw = LANES

    def body(p_ref, w_ref, o_ref):
        z = p_ref[1] * p_ref[2]
        cv = sum(w_ref[i:i + 1, :] * _shift_down(z, K - 1 - i) for i in range(K))
        o_ref[...] = ((p_ref[0] * cv) * _silu(p_ref[3])).astype(BF16)

    return pl.pallas_call(
        body, name=name, grid=(W // cw,),
        in_specs=[pl.BlockSpec((4, T, cw), lambda j: (0, 0, j)), pl.BlockSpec((K, cw), lambda j: (0, j))],
        out_specs=pl.BlockSpec((T, cw), lambda j: (0, j)),
        out_shape=jax.ShapeDtypeStruct((T, W), BF16),
        compiler_params=pltpu.CompilerParams(dimension_semantics=("parallel",), vmem_limit_bytes=VMEM_BIG),
    )(p3, conv_w)


def _sc_mid_bwd(p3, conv_w, do, *, name):
    _, T, W = p3.shape
    K = conv_w.shape[0]
    cw = LANES

    def body(p_ref, w_ref, do_ref, dp_ref, dw_ref):
        b, c, u, gate = p_ref[0], p_ref[1], p_ref[2], p_ref[3]
        z = c * u
        zs = [_shift_down(z, K - 1 - i) for i in range(K)]
        cv = sum(w_ref[i:i + 1, :] * zs[i] for i in range(K))
        y = b * cv
        dov = do_ref[...]
        dy = dov * _silu(gate)
        dp_ref[3] = dov * y * _dsilu(gate)
        dp_ref[0] = dy * cv
        dcv = dy * b
        dz = sum(w_ref[i:i + 1, :] * _shift_up(dcv, K - 1 - i) for i in range(K))
        dp_ref[1] = dz * u
        dp_ref[2] = dz * c
        for i in range(K):
            dw_ref[i:i + 1, :] = jnp.sum(dcv * zs[i], axis=0, keepdims=True)

    return pl.pallas_call(
        body, name=name, grid=(W // cw,),
        in_specs=[pl.BlockSpec((4, T, cw), lambda j: (0, 0, j)), pl.BlockSpec((K, cw), lambda j: (0, j)),
                  pl.BlockSpec((T, cw), lambda j: (0, j))],
        out_specs=[pl.BlockSpec((4, T, cw), lambda j: (0, 0, j)), pl.BlockSpec((K, cw), lambda j: (0, j))],
        out_shape=[jax.ShapeDtypeStruct((4, T, W), F32), jax.ShapeDtypeStruct((K, W), F32)],
        compiler_params=pltpu.CompilerParams(dimension_semantics=("parallel",), vmem_limit_bytes=VMEM_BIG),
    )(p3, conv_w, do)


def _sc_layer_fwd(x, ng, w_in, conv_w, w_out, tag):
    h = _rmsnorm_fwd(x, ng, name=f"{tag}_norm")
    p3 = _matmul(h, w_in, mode="nn", b_parts=4, out_parts=4, name=f"{tag}_inproj")
    og = _sc_mid_fwd(p3, conv_w, name=f"{tag}_mid")
    x_new = _matmul(og, w_out, mode="nn", res=x, name=f"{tag}_outproj")
    return x_new, (h, p3, og)


def _sc_layer_bwd(dx, x, ng, w_in, conv_w, w_out, saved, tag):
    h, p3, og = saved
    d_wout = _matmul(og, dx, mode="tn", out_dtype=BF16, name=f"{tag}_dwout")
    dog = _matmul(dx, w_out, mode="nt", name=f"{tag}_dog")
    dp3, dconv = _sc_mid_bwd(p3, conv_w, dog, name=f"{tag}_midbwd")
    d_win = _matmul(h, dp3, mode="tn", b_parts=4, out_parts=4, out_dtype=BF16, name=f"{tag}_dwin")
    dh = _matmul(dp3, w_in, mode="nt", a_parts=4, b_parts=4, name=f"{tag}_dh")
    dx_prev, dng = _rmsnorm_bwd(x, ng, dh, dx, name=f"{tag}_normbwd")
    return dx_prev, dng, d_win, dconv, d_wout


SB_BQ = 256
SB_BK = 256
SB_ROWS = 512


def _dot_x2(a, b_exact_bf16):
    hi = _bf(a)
    mid = _bf(a - hi.astype(F32))
    return (jnp.dot(hi, b_exact_bf16, preferred_element_type=F32)
            + jnp.dot(mid, b_exact_bf16, preferred_element_type=F32))


def _sb_half_mask():
    return lax.broadcasted_iota(jnp.int32, (1, LANES), 1) < SB_DH


def _sb_headnorm(x, g, lo):
    x2 = x * x
    s_lo = jnp.sum(jnp.where(lo, x2, 0.0), axis=-1, keepdims=True)
    s_hi = jnp.sum(jnp.where(lo, 0.0, x2), axis=-1, keepdims=True)
    r = lax.rsqrt(jnp.where(lo, s_lo, s_hi) * (1.0 / SB_DH) + RMS_EPS)
    xh = x * r
    return xh * g, xh, r


def _sb_stack(xb, lo):
    zero = jnp.zeros_like(xb)
    return jnp.concatenate([jnp.where(lo, xb, zero), jnp.where(lo, zero, xb)], axis=0)


def _sb_rel(bq, bk):
    row = lax.broadcasted_iota(jnp.int32, (2 * bq, bk), 0)
    col = lax.broadcasted_iota(jnp.int32, (2 * bq, bk), 1)
    return col - jnp.where(row >= bq, row - bq, row)


def _sb_tile(qm, kb, valid, scale):
    z = lax.dot_general(qm, kb, (((1,), (1,)), ((), ())), preferred_element_type=F32) * scale
    sp = _softplus(z)
    return z - sp, (-sp if valid is None else jnp.where(valid, -sp, 0.0))


def _sb_attn_fwd(p3, gq2, gk2, *, name):
    _, T, W = p3.shape
    bq, bk = min(SB_BQ, T), min(SB_BK, T)
    rows = min(SB_ROWS, T)
    scale = SB_DH ** -0.5

    def body(p_ref, gq_ref, gk_ref, og_ref, o_ref, ls_ref, qn_ref, kn_ref, v_ref):
        lo = _sb_half_mask()

        def prologue(i, c):
            r0 = pl.multiple_of(i * rows, rows)
            sl = pl.ds(r0, rows)
            qn_ref[sl, :] = _sb_headnorm(p_ref[0, sl, :], gq_ref[...], lo)[0].astype(BF16)
            kn_ref[sl, :] = _sb_headnorm(p_ref[1, sl, :], gk_ref[...], lo)[0].astype(BF16)
            v_ref[sl, :] = p_ref[2, sl, :].astype(BF16)
            return c

        lax.fori_loop(0, T // rows, prologue, 0)

        rel = _sb_rel(bq, bk)
        tri = (lax.broadcasted_iota(jnp.int32, (bk, bk), 0)
               > lax.broadcasted_iota(jnp.int32, (bk, bk), 1)).astype(BF16)

        def qblock(qi, c):
            q0 = pl.multiple_of(qi * bq, bq)
            qm = _sb_stack(qn_ref[pl.ds(q0, bq), :], lo)
            nkb = (q0 + bq - 1) // bk + 1

            def tile(k0, carry, valid):
                o_acc, a_carry = carry
                logsig, log1m = _sb_tile(qm, kn_ref[pl.ds(k0, bk), :], valid, scale)
                wts = jnp.exp(logsig + (_dot_x2(log1m, tri) + a_carry))
                if valid is not None:
                    wts = jnp.where(valid, wts, 0.0)
                o_acc = o_acc + jnp.dot(_bf(wts), v_ref[pl.ds(k0, bk), :], preferred_element_type=F32)
                return o_acc, a_carry + jnp.sum(log1m, axis=-1, keepdims=True)

            k_last = pl.multiple_of((nkb - 1) * bk, bk)
            first = tile(k_last, (jnp.zeros((2 * bq, LANES), F32), jnp.zeros((2 * bq, 1), F32)), rel < q0 - k_last)
            o2, t2 = lax.fori_loop(
                1, nkb, lambda t, cr: tile(pl.multiple_of((nkb - 1 - t) * bk, bk), cr, None), first)
            o = jnp.where(lo, o2[:bq], o2[bq:])
            o_ref[pl.ds(q0, bq), :] = o
            ls_ref[pl.ds(q0, bq), :] = jnp.where(lo, t2[:bq], t2[bq:])
            og_ref[pl.ds(q0, bq), :] = (o * _silu(p_ref[3, pl.ds(q0, bq), :])).astype(BF16)
            return c

        lax.fori_loop(0, T // bq, qblock, 0)

    colblk = pl.BlockSpec((T, LANES), lambda j: (0, j))
    vec = pl.BlockSpec((1, LANES), lambda j: (0, 0))
    return pl.pallas_call(
        body, name=name, grid=(W // LANES,),
        in_specs=[pl.BlockSpec((4, T, LANES), lambda j: (0, 0, j)), vec, vec],
        out_specs=[colblk, colblk, colblk],
        out_shape=[jax.ShapeDtypeStruct((T, W), BF16), jax.ShapeDtypeStruct((T, W), F32),
                   jax.ShapeDtypeStruct((T, W), F32)],
        scratch_shapes=[pltpu.VMEM((T, LANES), BF16)] * 3,
        compiler_params=pltpu.CompilerParams(dimension_semantics=("parallel",), vmem_limit_bytes=VMEM_BIG),
    )(p3, gq2, gk2)


def _sb_attn_bwd(p3, gq2, gk2, o, lsum, dog, *, name):
    _, T, W = p3.shape
    bq, bk = min(SB_BQ, T), min(SB_BK, T)
    rows = min(SB_ROWS, T)
    scale = SB_DH ** -0.5

    def body(p_ref, gq_ref, gk_ref, o_ref, ls_ref, dog_ref, dp_ref, dgq_ref, dgk_ref,
             qn_ref, kn_ref, v_ref, do_ref):
        lo = _sb_half_mask()

        def prologue(i, c):
            r0 = pl.multiple_of(i * rows, rows)
            sl = pl.ds(r0, rows)
            qn_ref[sl, :] = _sb_headnorm(p_ref[0, sl, :], gq_ref[...], lo)[0].astype(BF16)
            kn_ref[sl, :] = _sb_headnorm(p_ref[1, sl, :], gk_ref[...], lo)[0].astype(BF16)
            v_ref[sl, :] = p_ref[2, sl, :].astype(BF16)
            gate = p_ref[3, sl, :]
            dogv = dog_ref[sl, :]
            dp_ref[3, sl, :] = dogv * o_ref[sl, :] * _dsilu(gate)
            do_ref[sl, :] = (dogv * _silu(gate)).astype(BF16)
            zero = jnp.zeros((rows, LANES), F32)
            dp_ref[0, sl, :] = zero
            dp_ref[1, sl, :] = zero
            dp_ref[2, sl, :] = zero
            return c

        lax.fori_loop(0, T // rows, prologue, 0)

        rel = _sb_rel(bq, bk)
        rj = lax.broadcasted_iota(jnp.int32, (bk, bk), 0)
        cj = lax.broadcasted_iota(jnp.int32, (bk, bk), 1)
        upto = (rj <= cj).astype(BF16)
        before_m = (rj < cj).astype(BF16)

        def qblock(qi, c):
            q0 = pl.multiple_of(qi * bq, bq)
            qm = _sb_stack(qn_ref[pl.ds(q0, bq), :], lo)
            dom = _sb_stack(do_ref[pl.ds(q0, bq), :], lo)
            lsb = ls_ref[pl.ds(q0, bq), :]
            total = jnp.concatenate([lsb[:, 0:1], lsb[:, SB_DH:SB_DH + 1]], axis=0)
            nkb = (q0 + bq - 1) // bk + 1

            def tile(k0, carry, valid):
                dq_acc, a_pre, r_pre = carry
                ks = pl.ds(k0, bk)
                kb = kn_ref[ks, :]
                vb = v_ref[ks, :]
                logsig, log1m = _sb_tile(qm, kb, valid, scale)
                after = (total - a_pre) - _dot_x2(log1m, upto)
                wts = jnp.exp(logsig + after)
                if valid is not None:
                    wts = jnp.where(valid, wts, 0.0)
                dw = lax.dot_general(dom, vb, _NT, preferred_element_type=F32)
                ee = dw * wts
                before = r_pre + _dot_x2(ee, before_m)
                beta = jnp.exp(logsig)
                dz = ee * (1.0 - beta) - beta * before
                if valid is not None:
                    dz = jnp.where(valid, dz, 0.0)
                dzb = _bf(dz * scale)
                dq_acc = dq_acc + jnp.dot(dzb, kb, preferred_element_type=F32)
                dp_ref[1, ks, :] += lax.dot_general(dzb, qm, _TN, preferred_element_type=F32)
                dp_ref[2, ks, :] += lax.dot_general(_bf(wts), dom, _TN, preferred_element_type=F32)
                return (dq_acc, a_pre + jnp.sum(log1m, axis=-1, keepdims=True),
                        r_pre + jnp.sum(ee, axis=-1, keepdims=True))

            init = (jnp.zeros((2 * bq, LANES), F32), jnp.zeros((2 * bq, 1), F32), jnp.zeros((2 * bq, 1), F32))
            before_last = lax.fori_loop(
                0, nkb - 1, lambda kj, cr: tile(pl.multiple_of(kj * bk, bk), cr, None), init)
            k_last = pl.multiple_of((nkb - 1) * bk, bk)
            dq2, _, _ = tile(k_last, before_last, rel < q0 - k_last)
            dp_ref[0, pl.ds(q0, bq), :] = jnp.where(lo, dq2[:bq], dq2[bq:])
            return c

        lax.fori_loop(0, T // bq, qblock, 0)

        dgq_ref[...] = jnp.zeros_like(dgq_ref)
        dgk_ref[...] = jnp.zeros_like(dgk_ref)

        def epilogue(i, c):
            r0 = pl.multiple_of(i * rows, rows)
            sl = pl.ds(r0, rows)
            for part, g_ref, dg_ref in ((0, gq_ref, dgq_ref), (1, gk_ref, dgk_ref)):
                _, xh, r = _sb_headnorm(p_ref[part, sl, :], g_ref[...], lo)
                dn = dp_ref[part, sl, :]
                dxh = dn * g_ref[...]
                prod = dxh * xh
                m_lo = jnp.sum(jnp.where(lo, prod, 0.0), axis=-1, keepdims=True)
                m_hi = jnp.sum(jnp.where(lo, 0.0, prod), axis=-1, keepdims=True)
                m = jnp.where(lo, m_lo, m_hi) * (1.0 / SB_DH)
                dp_ref[part, sl, :] = r * (dxh - xh * m)
                dg_ref[...] += jnp.sum(dn * xh, axis=0, keepdims=True)
            return c

        lax.fori_loop(0, T // rows, epilogue, 0)

    colblk = pl.BlockSpec((T, LANES), lambda j: (0, j))
    vec = pl.BlockSpec((1, LANES), lambda j: (0, 0))
    part = pl.BlockSpec((4, T, LANES), lambda j: (0, 0, j))
    gvec = pl.BlockSpec((None, 1, LANES), lambda j: (j, 0, 0))
    npair = W // LANES
    return pl.pallas_call(
        body, name=name, grid=(npair,),
        in_specs=[part, vec, vec, colblk, colblk, colblk],
        out_specs=[part, gvec, gvec],
        out_shape=[jax.ShapeDtypeStruct((4, T, W), F32), jax.ShapeDtypeStruct((npair, 1, LANES), F32),
                   jax.ShapeDtypeStruct((npair, 1, LANES), F32)],
        scratch_shapes=[pltpu.VMEM((T, LANES), BF16)] * 4,
        compiler_params=pltpu.CompilerParams(dimension_semantics=("parallel",), vmem_limit_bytes=VMEM_BIG),
    )(p3, gq2, gk2, o, lsum, dog)


_NN = (((1,), (0,)), ((), ()))
_NT = (((1,), (1,)), ((), ()))
_TN = (((0,), (0,)), ((), ()))
DN_TB = 512
DN_AB_COL = (DN_CONV_W + DN_V_W) // LANES


def _dn_conv(x, w_ref):
    k = w_ref.shape[0]
    return sum(w_ref[i:i + 1, :] * _shift_down(x, k - 1 - i) for i in range(k))


def _dn_prep_fwd(p, conv_w, *, name):
    T = p.shape[0]
    cw = conv_w.shape[1]
    n_qk = 2 * DN_QK_W // LANES

    def body(p_ref, w_ref, o_ref):
        s = _silu(_dn_conv(p_ref[...], w_ref))
        r = lax.rsqrt(jnp.sum(s * s, axis=-1, keepdims=True) + L2_EPS)
        o_ref[...] = jnp.where(pl.program_id(0) < n_qk, s * r, s)

    colblk = pl.BlockSpec((T, LANES), lambda j: (0, j))
    return pl.pallas_call(
        body, name=name, grid=(cw // LANES,),
        in_specs=[colblk, pl.BlockSpec((DN_CONV, LANES), lambda j: (0, j))],
        out_specs=colblk, out_shape=jax.ShapeDtypeStruct((T, cw), F32),
        compiler_params=pltpu.CompilerParams(dimension_semantics=("parallel",), vmem_limit_bytes=VMEM_BIG),
    )(p, conv_w)


def _dn_chunk_tri(rows, upper):
    r = lax.broadcasted_iota(jnp.int32, (rows, rows), 0)
    c = lax.broadcasted_iota(jnp.int32, (rows, rows), 1)
    same = (r // DN_CHUNK) == (c // DN_CHUNK)
    return jnp.logical_and(same, (c >= r) if upper else (c <= r)).astype(BF16)


def _dn_lane_rows(a_log, dt_bias):
    pad = lambda v: jnp.zeros((1, LANES), F32).at[0, :DN_HEADS].set(v)
    return pad(a_log), pad(dt_bias)


def _dn_ab_parts(blk, alog_row, dtb_row):
    lane = lax.broadcasted_iota(jnp.int32, (1, LANES), 1)
    is_a = lane < DN_HEADS
    is_b = jnp.logical_and(lane >= DN_HEADS, lane < 2 * DN_HEADS)
    a_arg = jnp.where(is_a, blk + dtb_row, 0.0)
    neg_exp = jnp.where(is_a, -jnp.exp(alog_row), 0.0)
    log_a = neg_exp * _softplus(a_arg)
    beta = jnp.where(is_b, _sigmoid(blk), 0.0)
    return is_a, is_b, a_arg, neg_exp, log_a, beta


def _dn_ab_fwd(p, alog_row, dtb_row, *, name):
    T = p.shape[0]
    rows = min(DN_TB, T)

    def body(p_ref, al_ref, dt_ref, o_ref):
        _, _, _, _, log_a, beta = _dn_ab_parts(p_ref[...], al_ref[...], dt_ref[...])
        hi, mid, lo_ = _split3(log_a)
        tri = _dn_chunk_tri(rows, upper=False)
        f = lambda q: jnp.dot(tri, q, preferred_element_type=F32)
        o_ref[...] = (f(hi) + f(mid) + f(lo_)) + beta

    blk = pl.BlockSpec((rows, LANES), lambda i: (i, DN_AB_COL))
    vec = pl.BlockSpec((1, LANES), lambda i: (0, 0))
    return pl.pallas_call(
        body, name=name, grid=(T // rows,), in_specs=[blk, vec, vec],
        out_specs=pl.BlockSpec((rows, LANES), lambda i: (i, 0)),
        out_shape=jax.ShapeDtypeStruct((T, LANES), F32),
        compiler_params=pltpu.CompilerParams(dimension_semantics=("parallel",)),
    )(p, alog_row, dtb_row)


def _hp_l(a_l, b_l, dims=_NN):
    sa = [_split3(a)[:2] for a in a_l]
    sb = [_split3(b)[:2] for b in b_l]
    f = lambda p, q: lax.dot_general(p, q, dims, preferred_element_type=F32)
    hh = [f(x[0], y[0]) for x, y in zip(sa, sb)]
    hm = [f(x[0], y[1]) for x, y in zip(sa, sb)]
    mh = [f(x[1], y[0]) for x, y in zip(sa, sb)]
    return [a + (b + c) for a, b, c in zip(hh, hm, mh)]


def _dn_local(qs, k, v, g, beta, nc):
    c = DN_CHUNK
    cut = lambda x: [x[i * c:(i + 1) * c] for i in range(nc)]
    row = lax.broadcasted_iota(jnp.int32, (c, c), 0)
    col = lax.broadcasted_iota(jnp.int32, (c, c), 1)
    eye, lower, strict = row == col, row >= col, row > col
    rowid = lax.broadcasted_iota(jnp.int32, (c, 1), 0)
    eg = jnp.exp(g)
    kb = k * beta
    rhs_k = kb * eg
    g_l, k_l, kb_l, qs_l = cut(g), cut(k), cut(kb), cut(qs)
    g_row_l = [jnp.sum(jnp.where(eye, x, 0.0), axis=0, keepdims=True) for x in g_l]
    dec_l = [jnp.where(lower, jnp.exp(jnp.where(lower, x - y, 0.0)), 0.0) for x, y in zip(g_l, g_row_l)]
    kk_l = [_dot_nt(a, b) for a, b in zip(kb_l, k_l)]
    qk_l = [_dot_nt(a, b) for a, b in zip(qs_l, k_l)]
    low_l = [jnp.where(strict, a * d, 0.0) for a, d in zip(kk_l, dec_l)]
    eye_f = eye.astype(F32)
    pw_l = [-x for x in low_l]
    inv_l = [eye_f + x for x in pw_l]
    for _ in range(int(math.log2(c)) - 1):
        pw_l = _hp_l(pw_l, pw_l)
        inv_l = [a + b for a, b in zip(inv_l, _hp_l(inv_l, pw_l))]
    u_l = _hp_l(inv_l, cut(v * beta))
    w_l = _hp_l(inv_l, cut(rhs_k))
    aqk_l = [jnp.where(lower, a * d, 0.0) for a, d in zip(qk_l, dec_l)]
    g_last_l = [jnp.sum(jnp.where(rowid == c - 1, x, 0.0), axis=0, keepdims=True) for x in g_l]
    ekd_l = [jnp.exp(a - b) for a, b in zip(g_last_l, g_l)]
    kd_l = [a * b for a, b in zip(k_l, ekd_l)]
    return dict(eye=eye, lower=lower, strict=strict, dec=dec_l, k=k_l, kb=kb_l, qs=qs_l, low=low_l, inv=inv_l,
                eg=cut(eg), rhs_k=cut(rhs_k), u=u_l, w=w_l, aqk=aqk_l, g_last=g_last_l, qd=cut(qs * eg),
                ekd=ekd_l, kd=kd_l)


def _dn_head_cols(gb_blk, head):
    lane = lax.broadcasted_iota(jnp.int32, (1, LANES), 1)
    g = jnp.sum(jnp.where(lane == head, gb_blk, 0.0), axis=-1, keepdims=True)
    beta = jnp.sum(jnp.where(lane == head + DN_HEADS, gb_blk, 0.0), axis=-1, keepdims=True)
    return g, beta


def _dn_delta_fwd(qkv, gb, p, o_gain, *, name):
    T = qkv.shape[0]
    tb = min(DN_TB, T)
    nb, nc = T // tb, tb // DN_CHUNK
    H = DN_HEADS
    qscale = DN_DK ** -0.5

    def body(q_ref, k_ref, v_ref, gb_ref, gate_ref, gain_ref, o_ref, og_ref, st_ref, s_ref):
        head = pl.program_id(0)

        @pl.when(pl.program_id(1) == 0)
        def _():
            s_ref[...] = jnp.zeros_like(s_ref)

        g, beta = _dn_head_cols(gb_ref[...], head)
        t = _dn_local(q_ref[...] * qscale, k_ref[...], v_ref[...], g, beta, nc)
        s32 = s_ref[...]
        outs = []
        for i in range(nc):
            s_bf = _bf(s32)
            st_ref[i] = s_bf
            ws = _dot(t["w"][i], s_bf)
            qds = _dot(t["qd"][i], s_bf)
            vn = t["u"][i] - ws
            outs.append(qds + _dot(t["aqk"][i], vn))
            s32 = s32 * jnp.exp(t["g_last"][i]) + _dot_tn(t["kd"][i], vn)
        s_ref[...] = s32
        o = jnp.concatenate(outs, axis=0)
        o_ref[...] = o
        r = lax.rsqrt(jnp.mean(o * o, axis=-1, keepdims=True) + RMS_EPS)
        og_ref[...] = (((o * r) * gain_ref[...]) * _silu(gate_ref[...])).astype(BF16)

    qk = lambda off: pl.BlockSpec((tb, DN_DK), lambda h, i: (i, off + h))
    vblk = lambda off: pl.BlockSpec((tb, DN_DV), lambda h, i: (i, off + h))
    return pl.pallas_call(
        body, name=name, grid=(H, nb),
        in_specs=[qk(0), qk(H), vblk(2 * DN_QK_W // DN_DV), pl.BlockSpec((tb, LANES), lambda h, i: (i, 0)),
                  vblk(DN_CONV_W // DN_DV), pl.BlockSpec((1, DN_DV), lambda h, i: (0, 0))],
        out_specs=[vblk(0), vblk(0), pl.BlockSpec((None, nc, DN_DK, DN_DV), lambda h, i: (h, i, 0, 0))],
        out_shape=[jax.ShapeDtypeStruct((T, DN_V_W), F32), jax.ShapeDtypeStruct((T, DN_V_W), BF16),
                   jax.ShapeDtypeStruct((H, T // DN_CHUNK, DN_DK, DN_DV), BF16)],
        scratch_shapes=[pltpu.VMEM((DN_DK, DN_DV), F32)],
        compiler_params=pltpu.CompilerParams(dimension_semantics=("parallel", "arbitrary")),
    )(qkv, qkv, qkv, gb, p, o_gain)


def _dn_delta_bwd(qkv, gb, p, o_gain, o, states, dog, *, name):
    T = qkv.shape[0]
    tb = min(DN_TB, T)
    nb, nc = T // tb, tb // DN_CHUNK
    H = DN_HEADS
    qscale = DN_DK ** -0.5

    def body(q_ref, k_ref, v_ref, gb_ref, gate_ref, gain_ref, o_ref, st_ref, dog_ref,
             dq_ref, dk_ref, dv_ref, dgate_ref, dgb_ref, dgain_ref, ds_ref):
        head = pl.program_id(0)

        @pl.when(pl.program_id(1) == 0)
        def _():
            ds_ref[...] = jnp.zeros_like(ds_ref)

        @pl.when(jnp.logical_and(head == 0, pl.program_id(1) == 0))
        def _():
            dgain_ref[...] = jnp.zeros_like(dgain_ref)

        lane = lax.broadcasted_iota(jnp.int32, (1, LANES), 1)
        c = DN_CHUNK
        cut = lambda x: [x[i * c:(i + 1) * c] for i in range(nc)]
        cat = lambda xs: jnp.concatenate(xs, axis=0)
        rsum = lambda x: jnp.sum(x, axis=-1, keepdims=True)
        g, beta = _dn_head_cols(gb_ref[...], head)
        ov, gate, gain, dogv = o_ref[...], gate_ref[...], gain_ref[...], dog_ref[...]
        r = lax.rsqrt(jnp.mean(ov * ov, axis=-1, keepdims=True) + RMS_EPS)
        oh = ov * r
        dnrm = dogv * _silu(gate)
        dgate_ref[...] = dogv * (oh * gain) * _dsilu(gate)
        doh = dnrm * gain
        do_l = cut(r * (doh - oh * jnp.mean(doh * oh, axis=-1, keepdims=True)))
        dgain_ref[...] += jnp.sum(dnrm * oh, axis=0, keepdims=True)
        k, v = k_ref[...], v_ref[...]
        t = _dn_local(q_ref[...] * qscale, k, v, g, beta, nc)
        lower, strict, eye = t["lower"], t["strict"], t["eye"]
        s_l = [st_ref[i] for i in range(nc)]
        vn_l = [u - _dot(w, s) for u, w, s in zip(t["u"], t["w"], s_l)]
        dqd_l = [_dot_nt(a, s) for a, s in zip(do_l, s_l)]
        daqk_l = [_dot_nt(a, b) for a, b in zip(do_l, vn_l)]
        aqk_do_l = [_dot_tn(a, b) for a, b in zip(t["aqk"], do_l)]
        qd_do_l = [_dot_tn(a, b) for a, b in zip(t["qd"], do_l)]
        egl_l = [jnp.exp(x) for x in t["g_last"]]
        ds = ds_ref[...]
        dvn_l, dkd_l, dgl_l = [None] * nc, [None] * nc, [None] * nc
        for i in reversed(range(nc)):
            dvn_l[i] = aqk_do_l[i] + _dot(t["kd"][i], ds)
            dkd_l[i] = _dot_nt(vn_l[i], ds)
            dgl_l[i] = jnp.sum(rsum(ds * s_l[i].astype(F32)), axis=0, keepdims=True) * egl_l[i]
            ds = ds * egl_l[i] + qd_do_l[i] - _dot_tn(t["w"][i], dvn_l[i])
        ds_ref[...] = ds
        dw_l = [-_dot_nt(a, s) for a, s in zip(dvn_l, s_l)]
        dbv_l = _hp_l(t["inv"], dvn_l, _TN)
        dbk_l = _hp_l(t["inv"], dw_l, _TN)
        dlow_l = [-(a + b) for a, b in zip(_hp_l(dbv_l, t["u"], _NT), _hp_l(dbk_l, t["w"], _NT))]
        m_l = [jnp.where(strict, a * d, 0.0) for a, d in zip(dlow_l, t["dec"])]
        nmat_l = [jnp.where(lower, a * d, 0.0) for a, d in zip(daqk_l, t["dec"])]
        dkb_l = [_dot(m, kk) + b * e for m, kk, b, e in zip(m_l, t["k"], dbk_l, t["eg"])]
        dqs_l = [_dot(n, kk) + a * e for n, kk, a, e in zip(nmat_l, t["k"], dqd_l, t["eg"])]
        dk1_l = [_dot_tn(m, kb) for m, kb in zip(m_l, t["kb"])]
        dk2_l = [_dot_tn(n, q) for n, q in zip(nmat_l, t["qs"])]
        beta_l, v_l = cut(beta), cut(v)
        rowid = lax.broadcasted_iota(jnp.int32, (c, 1), 0)
        dk_l, dg_l, dbeta_l = [], [], []
        for i in range(nc):
            dk_l.append(dk1_l[i] + dk2_l[i] + dkd_l[i] * t["ekd"][i] + dkb_l[i] * beta_l[i])
            gmat = jnp.where(strict, dlow_l[i] * t["low"][i], 0.0) + daqk_l[i] * t["aqk"][i]
            s_kd = rsum(dkd_l[i] * t["kd"][i])
            dg = (rsum(gmat) + rsum(dqd_l[i] * t["qd"][i]) - s_kd + rsum(dbk_l[i] * t["rhs_k"][i]))
            dg_row = -jnp.sum(gmat, axis=0, keepdims=True)
            dg = dg + rsum(jnp.where(eye, dg_row, 0.0))
            dgl = dgl_l[i] + jnp.sum(s_kd, axis=0, keepdims=True)
            dg_l.append(dg + jnp.where(rowid == c - 1, dgl, 0.0))
            dbeta_l.append(rsum(dbv_l[i] * v_l[i]) + rsum(dkb_l[i] * t["k"][i]))
        dq_ref[...] = cat(dqs_l) * qscale
        dk_ref[...] = cat(dk_l)
        dv_ref[...] = cat(dbv_l) * beta
        dgb_ref[...] = (jnp.where(lane == head, cat(dg_l), 0.0)
                        + jnp.where(lane == head + DN_HEADS, cat(dbeta_l), 0.0))

    rev = lambda i: nb - 1 - i
    qk = lambda off: pl.BlockSpec((tb, DN_DK), lambda h, i: (rev(i), off + h))
    vblk = lambda off: pl.BlockSpec((tb, DN_DV), lambda h, i: (rev(i), off + h))
    gain_spec = pl.BlockSpec((1, DN_DV), lambda h, i: (0, 0))
    return pl.pallas_call(
        body, name=name, grid=(H, nb),
        in_specs=[qk(0), qk(H), vblk(2 * DN_QK_W // DN_DV), pl.BlockSpec((tb, LANES), lambda h, i: (rev(i), 0)),
                  vblk(DN_CONV_W // DN_DV), gain_spec, vblk(0),
                  pl.BlockSpec((None, nc, DN_DK, DN_DV), lambda h, i: (h, rev(i), 0, 0)), vblk(0)],
        out_specs=[qk(0), qk(0), vblk(0), vblk(DN_CONV_W // DN_DV),
                   pl.BlockSpec((None, tb, LANES), lambda h, i: (h, rev(i), 0)), gain_spec],
        out_shape=[jax.ShapeDtypeStruct((T, DN_QK_W), F32), jax.ShapeDtypeStruct((T, DN_QK_W), F32),
                   jax.ShapeDtypeStruct((T, DN_V_W), F32), jax.ShapeDtypeStruct((T, DN_IN_PAD), F32),
                   jax.ShapeDtypeStruct((H, T, LANES), F32), jax.ShapeDtypeStruct((1, DN_DV), F32)],
        scratch_shapes=[pltpu.VMEM((DN_DK, DN_DV), F32)],
        compiler_params=pltpu.CompilerParams(dimension_semantics=("arbitrary", "arbitrary")),
    )(qkv, qkv, qkv, gb, p, o_gain, o, states, dog)


def _dn_conv_bwd(p, conv_w, dq, dk, dv, dp, *, name):
    T = p.shape[0]
    cw = conv_w.shape[1]
    n_q = DN_QK_W // LANES
    n_v = DN_V_W // LANES

    def body(p_ref, w_ref, dq_ref, dk_ref, dv_ref, dp_in, dp_ref, dw_ref):
        del dp_in
        j = pl.program_id(0)
        x = p_ref[...]
        ksz = w_ref.shape[0]
        xs = [_shift_down(x, ksz - 1 - i) for i in range(ksz)]
        xc = sum(w_ref[i:i + 1, :] * xs[i] for i in range(ksz))
        s = _silu(xc)
        r = lax.rsqrt(jnp.sum(s * s, axis=-1, keepdims=True) + L2_EPS)
        y = s * r
        dn = jnp.where(j < n_q, dq_ref[...], dk_ref[...])
        ds_qk = r * (dn - y * jnp.sum(dn * y, axis=-1, keepdims=True))
        ds = jnp.where(j < 2 * n_q, ds_qk, dv_ref[...])
        dxc = ds * _dsilu(xc)
        dp_ref[...] = sum(w_ref[i:i + 1, :] * _shift_up(dxc, ksz - 1 - i) for i in range(ksz))
        for i in range(ksz):
            dw_ref[i:i + 1, :] = jnp.sum(dxc * xs[i], axis=0, keepdims=True)

    colblk = pl.BlockSpec((T, LANES), lambda j: (0, j))
    wblk = pl.BlockSpec((DN_CONV, LANES), lambda j: (0, j))
    return pl.pallas_call(
        body, name=name, grid=(cw // LANES,),
        in_specs=[colblk, wblk,
                  pl.BlockSpec((T, LANES), lambda j: (0, jnp.minimum(j, n_q - 1))),
                  pl.BlockSpec((T, LANES), lambda j: (0, jnp.clip(j - n_q, 0, n_q - 1))),
                  pl.BlockSpec((T, LANES), lambda j: (0, jnp.clip(j - 2 * n_q, 0, n_v - 1))),
                  pl.BlockSpec(memory_space=pl.ANY)],
        out_specs=[colblk, wblk],
        out_shape=[jax.ShapeDtypeStruct(dp.shape, F32), jax.ShapeDtypeStruct((DN_CONV, cw), F32)],
        input_output_aliases={5: 0},
        compiler_params=pltpu.CompilerParams(dimension_semantics=("parallel",), vmem_limit_bytes=VMEM_BIG),
    )(p, conv_w, dq, dk, dv, dp)


def _dn_ab_bwd(p, alog_row, dtb_row, dgb, dp, *, name):
    T = p.shape[0]
    rows = min(DN_TB, T)
    H = DN_HEADS

    def body(p_ref, al_ref, dt_ref, dgb_ref, dp_in, dp_ref, dal_ref, ddt_ref):
        del dp_in

        @pl.when(pl.program_id(0) == 0)
        def _():
            dal_ref[...] = jnp.zeros_like(dal_ref)
            ddt_ref[...] = jnp.zeros_like(ddt_ref)

        blk = p_ref[...]
        is_a, is_b, a_arg, neg_exp, log_a, beta = _dn_ab_parts(blk, al_ref[...], dt_ref[...])
        d = dgb_ref[0]
        for hh in range(1, H):
            d = d + dgb_ref[hh]
        hi, mid, lo_ = _split3(jnp.where(is_a, d, 0.0))
        tri = _dn_chunk_tri(rows, upper=True)
        f = lambda q: jnp.dot(tri, q, preferred_element_type=F32)
        dlog_a = f(hi) + f(mid) + f(lo_)
        da_in = dlog_a * neg_exp * _sigmoid(a_arg)
        db_in = jnp.where(is_b, d, 0.0) * beta * (1.0 - beta)
        dp_ref[...] = jnp.where(is_a, da_in, 0.0) + db_in
        dal_ref[...] += jnp.sum(dlog_a * log_a, axis=0, keepdims=True)
        ddt_ref[...] += jnp.sum(jnp.where(is_a, da_in, 0.0), axis=0, keepdims=True)

    blk = pl.BlockSpec((rows, LANES), lambda i: (i, DN_AB_COL))
    vec = pl.BlockSpec((1, LANES), lambda i: (0, 0))
    return pl.pallas_call(
        body, name=name, grid=(T // rows,),
        in_specs=[blk, vec, vec, pl.BlockSpec((H, rows, LANES), lambda i: (0, i, 0)),
                  pl.BlockSpec(memory_space=pl.ANY)],
        out_specs=[blk, vec, vec],
        out_shape=[jax.ShapeDtypeStruct(dp.shape, F32), jax.ShapeDtypeStruct((1, LANES), F32),
                   jax.ShapeDtypeStruct((1, LANES), F32)],
        input_output_aliases={4: 0},
        compiler_params=pltpu.CompilerParams(dimension_semantics=("arbitrary",)),
    )(p, alog_row, dtb_row, dgb, dp)


def _dn_layer_fwd(x, ng, w_in, conv_w, a_log, dt_bias, o_gain, w_out, tag):
    alog_row, dtb_row = _dn_lane_rows(a_log, dt_bias)
    gain = o_gain.reshape(1, DN_DV)
    h = _rmsnorm_fwd(x, ng, name=f"{tag}_norm")
    p = _matmul(h, w_in, mode="nn", name=f"{tag}_inproj")
    qkv = _dn_prep_fwd(p, conv_w, name=f"{tag}_prep")
    gb = _dn_ab_fwd(p, alog_row, dtb_row, name=f"{tag}_ab")
    o, og, states = _dn_delta_fwd(qkv, gb, p, gain, name=f"{tag}_delta")
    x_new = _matmul(og, w_out, mode="nn", res=x, name=f"{tag}_outproj")
    return x_new, (h, p, qkv, gb, o, og, states)


def _dn_layer_bwd(dx, x, ng, w_in, conv_w, a_log, dt_bias, o_gain, w_out, saved, tag):
    h, p, qkv, gb, o, og, states = saved
    alog_row, dtb_row = _dn_lane_rows(a_log, dt_bias)
    gain = o_gain.reshape(1, DN_DV)
    d_wout = _matmul(og, dx, mode="tn", out_dtype=BF16, name=f"{tag}_dwout")
    dog = _matmul(dx, w_out, mode="nt", name=f"{tag}_dog")
    dq, dk, dv, dp, dgb, dgain = _dn_delta_bwd(qkv, gb, p, gain, o, states, dog, name=f"{tag}_deltabwd")
    dp, dconv = _dn_conv_bwd(p, conv_w, dq, dk, dv, dp, name=f"{tag}_convbwd")
    dp, dal, ddt = _dn_ab_bwd(p, alog_row, dtb_row, dgb, dp, name=f"{tag}_abbwd")
    d_win = _matmul(h, dp, mode="tn", name=f"{tag}_dwin")
    dh = _matmul(dp, w_in, mode="nt", name=f"{tag}_dh")
    dx_prev, dng = _rmsnorm_bwd(x, ng, dh, dx, name=f"{tag}_normbwd")
    return dx_prev, dng, d_win, dconv, dal[0, :DN_HEADS], ddt[0, :DN_HEADS], dgain[0], d_wout


def _sb_gains(g):
    return jnp.concatenate([g, g]).reshape(1, LANES)


def _sb_layer_fwd(x, ng, w_in, gq, gk, w_out, tag):
    h = _rmsnorm_fwd(x, ng, name=f"{tag}_norm")
    p3 = _matmul(h, w_in, mode="nn", b_parts=4, out_parts=4, name=f"{tag}_inproj")
    og, o, lsum = _sb_attn_fwd(p3, _sb_gains(gq), _sb_gains(gk), name=f"{tag}_attn")
    x_new = _matmul(og, w_out, mode="nn", res=x, name=f"{tag}_outproj")
    return x_new, (h, p3, og, o, lsum)


def _sb_layer_bwd(dx, x, ng, w_in, gq, gk, w_out, saved, tag):
    h, p3, og, o, lsum = saved
    d_wout = _matmul(og, dx, mode="tn", out_dtype=BF16, name=f"{tag}_dwout")
    dog = _matmul(dx, w_out, mode="nt", name=f"{tag}_dog")
    dp3, dgq, dgk = _sb_attn_bwd(p3, _sb_gains(gq), _sb_gains(gk), o, lsum, dog, name=f"{tag}_attnbwd")
    fold = lambda d: jnp.sum(d.reshape(-1, SB_DH), axis=0)
    d_win = _matmul(h, dp3, mode="tn", b_parts=4, out_parts=4, out_dtype=BF16, name=f"{tag}_dwin")
    dh = _matmul(dp3, w_in, mode="nt", a_parts=4, b_parts=4, name=f"{tag}_dh")
    dx_prev, dng = _rmsnorm_bwd(x, ng, dh, dx, name=f"{tag}_normbwd")
    return dx_prev, dng, d_win, fold(dgq), fold(dgk), d_wout


N_CHIPS = 4
HBM = pl.BlockSpec(memory_space=pl.ANY)


def _mesh_pos():
    return lax.axis_index("x"), lax.axis_index("y"), lax.axis_index("c")


def _other_chips(x, y):
    return [(1 - x, y), (x, 1 - y), (1 - x, 1 - y)]


def _chip_exchange(srcs, *, send_slot_is_dest, copy_own, name):
    n = len(srcs)

    def body(*refs):
        src_refs, out_refs = refs[:n], refs[n:2 * n]
        send_sems, recv_sems, local_sems = refs[2 * n:]
        x, y, c = _mesh_pos()
        me = 2 * x + y
        chips = _other_chips(x, y)
        local = []
        for a in range(n):
            if not copy_own[a]:
                continue
            own = src_refs[a].at[me] if send_slot_is_dest else src_refs[a]
            local.append(pltpu.make_async_copy(own, out_refs[a].at[me], local_sems.at[a]))
        for cp in local:
            cp.start()

        def copy(a, k, landing_slot):
            px, py = chips[k]
            src = src_refs[a].at[2 * px + py] if send_slot_is_dest else src_refs[a]
            return pltpu.make_async_remote_copy(
                src_ref=src, dst_ref=out_refs[a].at[landing_slot],
                send_sem=send_sems.at[a * 3 + k], recv_sem=recv_sems.at[a * 3 + k],
                device_id=(px, py, c), device_id_type=MESH)

        sends = [copy(a, k, me) for a in range(n) for k in range(3)]
        for cp in sends:
            cp.start()
        for a in range(n):
            for k in range(3):
                px, py = chips[k]
                copy(a, k, 2 * px + py).wait_recv()
        for cp in sends:
            cp.wait_send()
        for cp in local:
            cp.wait()

    outs = []
    for s in srcs:
        shape = s.shape if send_slot_is_dest else (N_CHIPS,) + s.shape
        outs.append(jax.ShapeDtypeStruct(shape, s.dtype))
    return pl.pallas_call(
        body, name=name, in_specs=[HBM] * n, out_specs=[HBM] * n, out_shape=outs,
        scratch_shapes=[pltpu.SemaphoreType.DMA((3 * n,)), pltpu.SemaphoreType.DMA((3 * n,)),
                        pltpu.SemaphoreType.DMA((n,))],
    )(*srcs)


def _sibling_exchange(srcs, *, name):
    n = len(srcs)

    def body(*refs):
        src_refs, out_refs = refs[:n], refs[n:2 * n]
        send_sems, recv_sems = refs[2 * n:]
        x, y, c = _mesh_pos()
        copies = [pltpu.make_async_remote_copy(
            src_ref=src_refs[a], dst_ref=out_refs[a], send_sem=send_sems.at[a], recv_sem=recv_sems.at[a],
            device_id=(x, y, 1 - c), device_id_type=MESH) for a in range(n)]
        for cp in copies:
            cp.start()
        for cp in copies:
            cp.wait()

    return pl.pallas_call(
        body, name=name, in_specs=[HBM] * n, out_specs=[HBM] * n,
        out_shape=[jax.ShapeDtypeStruct(s.shape, s.dtype) for s in srcs],
        scratch_shapes=[pltpu.SemaphoreType.DMA((n,)), pltpu.SemaphoreType.DMA((n,))],
    )(*srcs)


def _gather_halves(shards, small, *, name):
    n = len(shards)

    def body(*refs):
        s_refs, small_ref = refs[:n], refs[n]
        o_refs, osmall_ref = refs[n + 1:2 * n + 1], refs[2 * n + 1]
        send_sems, recv_sems, local_sems = refs[2 * n + 2:]
        x, y, c = _mesh_pos()
        me = 2 * x + y
        chips = _other_chips(x, y)
        local = [pltpu.make_async_copy(small_ref, osmall_ref.at[me], local_sems.at[0])]
        for cp in local:
            cp.start()

        def over_ici(a, k, slot):
            px, py = chips[k]
            return pltpu.make_async_remote_copy(
                src_ref=s_refs[a].at[c], dst_ref=o_refs[a].at[slot, c], send_sem=send_sems.at[3 * a + k],
                recv_sem=recv_sems.at[3 * a + k], device_id=(px, py, c), device_id_type=MESH)

        def small_copy(k, slot):
            px, py = chips[k]
            return pltpu.make_async_remote_copy(
                src_ref=small_ref, dst_ref=osmall_ref.at[slot], send_sem=send_sems.at[3 * n + k],
                recv_sem=recv_sems.at[3 * n + k], device_id=(px, py, c), device_id_type=MESH)

        def to_sibling(a, k, half):
            px, py = chips[k]
            blk = o_refs[a].at[2 * px + py, half]
            return pltpu.make_async_remote_copy(
                src_ref=blk, dst_ref=blk, send_sem=send_sems.at[3 * n + 3 + 3 * a + k],
                recv_sem=recv_sems.at[3 * n + 3 + 3 * a + k], device_id=(x, y, 1 - c), device_id_type=MESH)

        sends = [over_ici(a, k, me) for a in range(n) for k in range(3)] + [small_copy(k, me) for k in range(3)]
        for cp in sends:
            cp.start()
        passed = []
        for a in range(n):
            for k in range(3):
                px, py = chips[k]
                over_ici(a, k, 2 * px + py).wait_recv()
                passed.append(to_sibling(a, k, c))
                passed[-1].start()
        for k in range(3):
            px, py = chips[k]
            small_copy(k, 2 * px + py).wait_recv()
        for a in range(n):
            for k in range(3):
                to_sibling(a, k, 1 - c).wait_recv()
        for cp in sends + passed:
            cp.wait_send()
        for cp in local:
            cp.wait()

    nsem = 6 * n + 3
    return pl.pallas_call(
        body, name=name, in_specs=[HBM] * (n + 1), out_specs=[HBM] * (n + 1),
        out_shape=[jax.ShapeDtypeStruct((N_CHIPS,) + s.shape, s.dtype) for s in shards + [small]],
        scratch_shapes=[pltpu.SemaphoreType.DMA((nsem,)), pltpu.SemaphoreType.DMA((nsem,)),
                        pltpu.SemaphoreType.DMA((1,))],
    )(*shards, small)


def _swap_other_half(g_list, *, name):
    n = len(g_list)

    def body(*refs):
        g_refs, o_refs = refs[:n], refs[n:2 * n]
        send_sems, recv_sems = refs[2 * n:]
        x, y, c = _mesh_pos()
        copies = [pltpu.make_async_remote_copy(
            src_ref=g_refs[a].at[:, 1 - c], dst_ref=o_refs[a], send_sem=send_sems.at[a], recv_sem=recv_sems.at[a],
            device_id=(x, y, 1 - c), device_id_type=MESH) for a in range(n)]
        for cp in copies:
            cp.start()
        for cp in copies:
            cp.wait()

    return pl.pallas_call(
        body, name=name, in_specs=[HBM] * n, out_specs=[HBM] * n,
        out_shape=[jax.ShapeDtypeStruct((g.shape[0],) + g.shape[2:], g.dtype) for g in g_list],
        scratch_shapes=[pltpu.SemaphoreType.DMA((n,)), pltpu.SemaphoreType.DMA((n,))],
    )(*g_list)


def _row_tile(r):
    return _pick(r, (512, 256, 128, 64, 32, 16, 8))


def _add_my_half(g4, sib4, core, *, name):
    n, _, r, C = g4.shape
    tr = _row_tile(r)

    def body(core_ref, g_ref, s_ref, o_ref):
        del core_ref
        o_ref[...] = (g_ref[...].astype(F32) + s_ref[...].astype(F32)).astype(o_ref.dtype)

    return pl.pallas_call(
        body, name=name,
        grid_spec=pltpu.PrefetchScalarGridSpec(
            num_scalar_prefetch=1, grid=(n, r // tr),
            in_specs=[pl.BlockSpec((None, None, tr, C), lambda j, i, core_ref: (j, core_ref[0], i, 0)),
                      pl.BlockSpec((None, tr, C), lambda j, i, core_ref: (j, i, 0))],
            out_specs=pl.BlockSpec((None, tr, C), lambda j, i, core_ref: (j, i, 0))),
        out_shape=jax.ShapeDtypeStruct((n, r, C), g4.dtype),
        compiler_params=pltpu.CompilerParams(dimension_semantics=("parallel", "parallel")),
    )(core, g4, sib4)


def _scatter_to_chips(p_list, *, name):
    n = len(p_list)

    def body(*refs):
        p_refs, o_refs = refs[:n], refs[n:2 * n]
        send_sems, recv_sems, local_sems = refs[2 * n:]
        x, y, c = _mesh_pos()
        me = 2 * x + y
        chips = _other_chips(x, y)
        own = [pltpu.make_async_copy(p_refs[a].at[me], o_refs[a].at[me], local_sems.at[a]) for a in range(n)]
        for cp in own:
            cp.start()

        def copy(a, k, landing_slot):
            px, py = chips[k]
            return pltpu.make_async_remote_copy(
                src_ref=p_refs[a].at[2 * px + py], dst_ref=o_refs[a].at[landing_slot],
                send_sem=send_sems.at[3 * a + k], recv_sem=recv_sems.at[3 * a + k], device_id=(px, py, c),
                device_id_type=MESH)

        sends = [copy(a, k, me) for a in range(n) for k in range(3)]
        for cp in sends:
            cp.start()
        for a in range(n):
            for k in range(3):
                px, py = chips[k]
                copy(a, k, 2 * px + py).wait_recv()
        for cp in sends:
            cp.wait_send()
        for cp in own:
            cp.wait()

    return pl.pallas_call(
        body, name=name, in_specs=[HBM] * n, out_specs=[HBM] * n,
        out_shape=[jax.ShapeDtypeStruct(p.shape, p.dtype) for p in p_list],
        scratch_shapes=[pltpu.SemaphoreType.DMA((3 * n,)), pltpu.SemaphoreType.DMA((3 * n,)),
                        pltpu.SemaphoreType.DMA((n,))],
    )(*p_list)


def _sum_chips(r4, *, name):
    _, r, C = r4.shape
    tr = _row_tile(r)

    def body(r_ref, o_ref):
        f = lambda j: r_ref[j].astype(F32)
        o_ref[...] = ((f(0) + f(1)) + f(2)) + f(3)

    return pl.pallas_call(
        body, name=name, grid=(r // tr,), in_specs=[pl.BlockSpec((N_CHIPS, tr, C), lambda i: (0, i, 0))],
        out_specs=pl.BlockSpec((tr, C), lambda i: (i, 0)), out_shape=jax.ShapeDtypeStruct((r, C), F32),
        compiler_params=pltpu.CompilerParams(dimension_semantics=("parallel",)),
    )(r4)


def _adamw_halves(w, mine, theirs, m, v, core, *, name):
    shape = w.shape
    r, C = mine.shape
    tr = _pick(r, (128, 64, 32, 16, 8))
    view = lambda a: a.reshape(2, r, C)

    def body(core_ref, w_ref, gm_ref, gt_ref, m_ref, v_ref, g_ref, d_ref, nm_ref, nv_ref):
        gv = jnp.where(pl.program_id(0) == core_ref[0], gm_ref[...], gt_ref[...])
        g_ref[...] = gv
        d_ref[...], nm_ref[...], nv_ref[...] = _adamw_math(w_ref[...], gv, m_ref[...], v_ref[...])

    half = pl.BlockSpec((None, tr, C), lambda h, i, core_ref: (h, i, 0))
    row = pl.BlockSpec((tr, C), lambda h, i, core_ref: (i, 0))
    out = jax.ShapeDtypeStruct((2, r, C), F32)
    res = pl.pallas_call(
        body, name=name,
        grid_spec=pltpu.PrefetchScalarGridSpec(
            num_scalar_prefetch=1, grid=(2, r // tr), in_specs=[half, row, row, half, half], out_specs=[half] * 4),
        out_shape=[out] * 4,
        compiler_params=pltpu.CompilerParams(dimension_semantics=("parallel", "parallel")),
    )(core, view(w), mine, theirs, view(m), view(v))
    return tuple(a.reshape(shape) for a in res)


def _sum_small(recv4, *, name):
    _, R, C = recv4.shape

    def body(r_ref, o_ref):
        o_ref[...] = ((r_ref[0] + r_ref[1]) + r_ref[2]) + r_ref[3]

    return pl.pallas_call(body, name=name, out_shape=jax.ShapeDtypeStruct((R, C), F32))(recv4)


def _add(a, b, *, name):
    R, C = a.shape
    tr = _pick(R, (512, 256, 128, 64, 32, 16, 8))
    blk = pl.BlockSpec((tr, C), lambda i: (i, 0))

    def body(a_ref, b_ref, o_ref):
        o_ref[...] = a_ref[...] + b_ref[...]

    return pl.pallas_call(body, name=name, grid=(R // tr,), in_specs=[blk, blk], out_specs=blk,
                          out_shape=jax.ShapeDtypeStruct((R, C), F32),
                          compiler_params=pltpu.CompilerParams(dimension_semantics=("parallel",)))(a, b)


def _adamw_math(w, g, m, v):
    nm = ADAM_B1 * m + (1.0 - ADAM_B1) * g
    nv = ADAM_B2 * v + (1.0 - ADAM_B2) * (g * g)
    m_hat = nm / (1.0 - ADAM_B1 ** ADAM_STEP)
    v_hat = nv / (1.0 - ADAM_B2 ** ADAM_STEP)
    return -ADAM_LR * (m_hat / (jnp.sqrt(v_hat) + ADAM_EPS) + ADAM_WD * w), nm, nv


def _adamw(w, g, m, v, *, name):
    shape = w.shape
    C = shape[-1]
    R = w.size // C
    two = lambda a: a.reshape(R, C)
    tr = _pick(R, (256, 128, 64, 32, 16, 8)) if R % 8 == 0 and R > 8 else R
    blk = pl.BlockSpec((tr, C), lambda i: (i, 0))

    def body(w_ref, g_ref, m_ref, v_ref, d_ref, nm_ref, nv_ref):
        d_ref[...], nm_ref[...], nv_ref[...] = _adamw_math(w_ref[...], g_ref[...], m_ref[...], v_ref[...])

    out = jax.ShapeDtypeStruct((R, C), F32)
    d, nm, nv = pl.pallas_call(
        body, name=name, grid=(R // tr,), in_specs=[blk] * 4, out_specs=[blk] * 3, out_shape=[out] * 3,
        compiler_params=pltpu.CompilerParams(dimension_semantics=("parallel",)),
    )(two(w), two(g), two(m), two(v))
    return d.reshape(shape), nm.reshape(shape), nv.reshape(shape)


BIG = (("dn_w_in", (2, 1024, 1540), 2), ("dn_w_out", (2, 512, 1024), 1), ("sb_w_in", (1, 1024, 1024), 2),
       ("sb_w_out", (1, 256, 1024), 1), ("sc_w_in", (1, 1024, 2048), 2), ("sc_w_out", (1, 512, 1024), 1))
SMALL = (("dn_conv_w", (2, 4, 1024), 2), ("dn_o_norm_g", (2, 64), 1), ("sc_conv_w", (1, 3, 512), 2))
REPL = (("norm_g", (4, 1024)), ("dn_a_log", (2, 8)), ("dn_dt_bias", (2, 8)), ("sb_q_norm_g", (1, 64)),
        ("sb_k_norm_g", (1, 64)))


def _halves(shard):
    return shard.reshape(2, -1, shard.shape[-1])


def _pack(arrays, cols, lead=()):
    flat = jnp.concatenate([a.reshape(lead + (-1,)) for a in arrays], axis=-1)
    n = flat.shape[-1]
    rows = -(-n // cols)
    unit = 512 if rows > 512 else 8
    rows = -(-rows // unit) * unit
    flat = jnp.pad(flat, [(0, 0)] * len(lead) + [(0, rows * cols - n)])
    return flat.reshape(lead + (rows, cols))


def _unpack(buf, table, lead=()):
    flat = buf.reshape(lead + (-1,))
    out, off = {}, 0
    for entry in table:
        name, shape = entry[0], entry[1]
        n = math.prod(shape)
        out[name] = flat[..., off:off + n].reshape(lead + shape)
        off += n
    return out


def _join(shards, axis):
    return jnp.concatenate([shards[j] for j in range(N_CHIPS)], axis=axis)


def _split(full, axis):
    return jnp.stack(jnp.split(full, N_CHIPS, axis=axis), axis=0)


def kernel(x, norm_g, dn_w_in, dn_conv_w, dn_a_log, dn_dt_bias, dn_o_norm_g, dn_w_out, sb_w_in, sb_q_norm_g, sb_k_norm_g, sb_w_out, sc_w_in, sc_conv_w, sc_w_out, loss_target, m_norm_g, m_dn_w_in, m_dn_conv_w, m_dn_a_log, m_dn_dt_bias, m_dn_o_norm_g, m_dn_w_out, m_sb_w_in, m_sb_q_norm_g, m_sb_k_norm_g, m_sb_w_out, m_sc_w_in, m_sc_conv_w, m_sc_w_out, v_norm_g, v_dn_w_in, v_dn_conv_w, v_dn_a_log, v_dn_dt_bias, v_dn_o_norm_g, v_dn_w_out, v_sb_w_in, v_sb_q_norm_g, v_sb_k_norm_g, v_sb_w_out, v_sc_w_in, v_sc_conv_w, v_sc_w_out):
    weights = dict(norm_g=norm_g, dn_w_in=dn_w_in, dn_conv_w=dn_conv_w, dn_a_log=dn_a_log, dn_dt_bias=dn_dt_bias,
                   dn_o_norm_g=dn_o_norm_g, dn_w_out=dn_w_out, sb_w_in=sb_w_in, sb_q_norm_g=sb_q_norm_g,
                   sb_k_norm_g=sb_k_norm_g, sb_w_out=sb_w_out, sc_w_in=sc_w_in, sc_conv_w=sc_conv_w, sc_w_out=sc_w_out)
    m_in = dict(norm_g=m_norm_g, dn_w_in=m_dn_w_in, dn_conv_w=m_dn_conv_w, dn_a_log=m_dn_a_log,
                dn_dt_bias=m_dn_dt_bias, dn_o_norm_g=m_dn_o_norm_g, dn_w_out=m_dn_w_out, sb_w_in=m_sb_w_in,
                sb_q_norm_g=m_sb_q_norm_g, sb_k_norm_g=m_sb_k_norm_g, sb_w_out=m_sb_w_out, sc_w_in=m_sc_w_in,
                sc_conv_w=m_sc_conv_w, sc_w_out=m_sc_w_out)
    v_in = dict(norm_g=v_norm_g, dn_w_in=v_dn_w_in, dn_conv_w=v_dn_conv_w, dn_a_log=v_dn_a_log,
                dn_dt_bias=v_dn_dt_bias, dn_o_norm_g=v_dn_o_norm_g, dn_w_out=v_dn_w_out, sb_w_in=v_sb_w_in,
                sb_q_norm_g=v_sb_q_norm_g, sb_k_norm_g=v_sb_k_norm_g, sb_w_out=v_sb_w_out, sc_w_in=v_sc_w_in,
                sc_conv_w=v_sc_conv_w, sc_w_out=v_sc_w_out)
    order = list(weights)
    xi, yi, ci = _mesh_pos()

    small = _pack([weights[n] for n, _, _ in SMALL], LANES)
    own = [_halves(weights[n].astype(BF16)) for n, _, _ in BIG]
    *big4, small4 = _gather_halves(own, small, name="gather_weights")
    me = 2 * xi + yi
    got = {n: lax.dynamic_update_index_in_dim(g, o, me, 0).reshape((N_CHIPS,) + shape)
           for (n, shape, _), g, o in zip(BIG, big4, own)}
    full = {n: _join(a, ax) for (n, _, ax), a in zip(SMALL, _unpack(small4, SMALL, (N_CHIPS,)).values())}
    dn_w_in_pad = jnp.pad(_join(got["dn_w_in"], 2), ((0, 0), (0, 0), (0, DN_IN_PAD - DN_IN)))
    rows_of = lambda w4: w4.reshape(-1, w4.shape[-1])

    def dn_args(j):
        return (dn_w_in_pad[j], full["dn_conv_w"][j], dn_a_log[j], dn_dt_bias[j], full["dn_o_norm_g"][j],
                rows_of(got["dn_w_out"][:, j]))

    sb_args = (got["sb_w_in"][:, 0], sb_q_norm_g[0], sb_k_norm_g[0], rows_of(got["sb_w_out"][:, 0]))
    sc_args = (got["sc_w_in"][:, 0], full["sc_conv_w"][0], rows_of(got["sc_w_out"][:, 0]))

    x0 = x[0]
    x1, s0 = _dn_layer_fwd(x0, norm_g[0], *dn_args(0), "l0")
    x2, s1 = _sb_layer_fwd(x1, norm_g[1], *sb_args, "l1")
    x3, s2 = _sc_layer_fwd(x2, norm_g[2], *sc_args, "l2")
    x4, s3 = _dn_layer_fwd(x3, norm_g[3], *dn_args(1), "l3")
    dy, loss_local = _loss_head(x4, loss_target[0], name="loss_head")
    loss = lax.psum(loss_local[0, 0], ("x", "y", "c"))

    dx3, dng3, dwin3, dconv3, dal3, ddt3, dgain3, dwout3 = _dn_layer_bwd(dy, x3, norm_g[3], *dn_args(1), s3, "l3")
    dx2, dng2, dwin2, dconv2, dwout2 = _sc_layer_bwd(dx3, x2, norm_g[2], *sc_args, s2, "l2")
    dx1, dng1, dwin1, dgq, dgk, dwout1 = _sb_layer_bwd(dx2, x1, norm_g[1], *sb_args, s1, "l1")
    dx0, dng0, dwin0, dconv0, dal0, ddt0, dgain0, dwout0 = _dn_layer_bwd(dx1, x0, norm_g[0], *dn_args(0), s0, "l0")

    grads = dict(
        norm_g=jnp.concatenate([dng0, dng1, dng2, dng3], axis=0), dn_conv_w=jnp.stack([dconv0, dconv3]),
        dn_a_log=jnp.stack([dal0, dal3]), dn_dt_bias=jnp.stack([ddt0, ddt3]),
        dn_o_norm_g=jnp.stack([dgain0, dgain3]), sb_q_norm_g=dgq[None], sb_k_norm_g=dgk[None],
        sc_conv_w=dconv2[None])
    by_cols = lambda dw: _split(dw[:, :DN_IN].astype(BF16), 1)
    by_rows = lambda dw: dw.reshape(N_CHIPS, -1, dw.shape[-1])
    cut2 = lambda g4: g4.reshape(N_CHIPS, 2, -1, g4.shape[-1])
    gbig = [jnp.stack([by_cols(dwin0), by_cols(dwin3)], axis=1), jnp.stack([by_rows(dwout0), by_rows(dwout3)], axis=1),
            cut2(dwin1), cut2(by_rows(dwout1)), cut2(dwin2), cut2(by_rows(dwout2))]

    core = ci.astype(jnp.int32).reshape(1)
    sib = _swap_other_half(gbig, name="swap_halves")
    part = [_add_my_half(g, s, core, name=f"sum_cores_{n}") for (n, _, _), g, s in zip(BIG, gbig, sib)]
    landed = _scatter_to_chips(part, name="scatter_grads")
    mine = [_sum_chips(r, name=f"sum_chips_{n}") for (n, _, _), r in zip(BIG, landed)]
    theirs = _sibling_exchange(mine, name="swap_results")
    upd = {n: _adamw_halves(weights[n], a, b, m_in[n], v_in[n], core, name=f"adamw_{n}")
           for (n, _, _), a, b in zip(BIG, mine, theirs)}
    g_out = {n: upd[n][0] for n, _, _ in BIG}
    repl = [jnp.broadcast_to(grads[n][None], (N_CHIPS,) + s) for n, s in REPL]
    gsmall = _pack([_split(grads[n], ax) for n, _, ax in SMALL] + repl, LANES, (N_CHIPS,))
    rsmall, = _chip_exchange([gsmall], send_slot_is_dest=True, copy_own=(True,), name="scatter_small")
    psmall = _sum_small(rsmall, name="sum_chips_small")
    qsmall, = _sibling_exchange([psmall], name="swap_cores_small")
    tsmall = _add(psmall, qsmall, name="sum_cores_small")
    g_out.update(_unpack(tsmall, SMALL + REPL))

    for n in order:
        if n not in upd:
            upd[n] = (g_out[n],) + _adamw(weights[n], g_out[n], m_in[n], v_in[n], name=f"adamw_{n}")
    return (loss, dx0[None], *[upd[n][0] for n in order], *[upd[n][1] for n in order],
            *[upd[n][2] for n in order], *[upd[n][3] for n in order])
```

```python
import math

import jax
import jax.numpy as jnp
from jax import lax
from jax.experimental import pallas as pl
from jax.experimental.pallas import tpu as pltpu

F32 = jnp.float32
BF16 = jnp.bfloat16
MESH = pl.DeviceIdType.MESH

RMS_EPS = 1e-6
L2_EPS = 1e-6
LANES = 128
VMEM_BIG = 60 * 1024 * 1024
MM_VMEM = 36 * 1024 * 1024

DN_HEADS, DN_DK, DN_DV, DN_CHUNK, DN_CONV = 8, 128, 256, 64, 4
DN_QK_W = DN_HEADS * DN_DK
DN_V_W = DN_HEADS * DN_DV
DN_CONV_W = 2 * DN_QK_W + DN_V_W
DN_IN = DN_CONV_W + DN_V_W + 2 * DN_HEADS
DN_IN_PAD = DN_CONV_W + DN_V_W + LANES
SB_DH = 64
SC_CONV = 3

ADAM_LR, ADAM_B1, ADAM_B2, ADAM_EPS, ADAM_WD, ADAM_STEP = 0.001, 0.9, 0.999, 1e-08, 0.01, 10


def _pick(n, cands):
    for c in cands:
        if n % c == 0:
            return c
    raise ValueError(f"no tile for {n} in {cands}")


def _bf(x):
    return x.astype(BF16)


def _dot(a, b):
    return jnp.dot(_bf(a), _bf(b), preferred_element_type=F32)


def _dot_nt(a, b):
    return lax.dot_general(_bf(a), _bf(b), (((1,), (1,)), ((), ())), preferred_element_type=F32)


def _dot_tn(a, b):
    return lax.dot_general(_bf(a), _bf(b), (((0,), (0,)), ((), ())), preferred_element_type=F32)


def _split3(a):
    hi = _bf(a)
    r = a - hi.astype(F32)
    mid = _bf(r)
    lo = _bf(r - mid.astype(F32))
    return hi, mid, lo


def _sigmoid(x):
    return 1.0 / (1.0 + jnp.exp(-x))


def _silu(x):
    return x * _sigmoid(x)


def _dsilu(x):
    s = _sigmoid(x)
    return s * (1.0 + x * (1.0 - s))


def _softplus(x):
    return jnp.maximum(x, 0.0) + jnp.log(1.0 + jnp.exp(-jnp.abs(x)))


def _shift_down(z, k):
    if k == 0:
        return z
    row = lax.broadcasted_iota(jnp.int32, z.shape, 0)
    return jnp.where(row >= k, pltpu.roll(z, k, 0), 0.0)


def _shift_up(z, k):
    if k == 0:
        return z
    n = z.shape[0]
    row = lax.broadcasted_iota(jnp.int32, z.shape, 0)
    return jnp.where(row < n - k, pltpu.roll(z, n - k, 0), 0.0)


def _matmul(a, b, *, mode, name, res=None, a_parts=1, b_parts=1, out_parts=1, out_dtype=F32):
    def dims2(x, parts):
        if parts == 1:
            return x.shape
        assert x.shape[0] == parts
        return (x.shape[1], x.shape[2] * parts)

    ash, bsh = dims2(a, a_parts), dims2(b, b_parts)
    if mode == "nn":
        (M, K), (K2, N) = ash, bsh
        dn = (((1,), (0,)), ((), ()))
    elif mode == "nt":
        (M, K), (N, K2) = ash, bsh
        dn = (((1,), (1,)), ((), ()))
    else:
        (K, M), (K2, N) = ash, bsh
        dn = (((0,), (0,)), ((), ()))
    assert K == K2, (ash, bsh, mode)
    tm = _pick(M, (512, 256, 128, 64, 32, 16, 8))
    n_unit = N // max(out_parts, b_parts if mode != "nt" else 1)
    k_unit = K // max(a_parts if mode != "tn" else 1, b_parts if mode == "nt" else 1)
    tn, tk = min(
        ((n, k) for n in (2048, 1792, 1024, 896, 768, 512, 384, 256, 128) if n_unit % n == 0
         for k in (2048, 1792, 1024, 896, 512, 256, 128) if k_unit % k == 0
         if 2 * (tm * k * a.dtype.itemsize + k * n * b.dtype.itemsize + 2 * tm * n * 4) + tm * n * 4 <= MM_VMEM),
        key=lambda nk_: (-nk_[0] * nk_[1], -nk_[1]))
    nk = K // tk
    grid = (M // tm, N // tn, nk)

    def spec(parts, rows_are, cols_are, tr, tc, width):
        per = width // parts // tc
        if parts == 1:
            return pl.BlockSpec((tr, tc), lambda i, j, k: ((i, j, k)[rows_are], (i, j, k)[cols_are]))
        return pl.BlockSpec((None, tr, tc), lambda i, j, k: ((i, j, k)[cols_are] // per, (i, j, k)[rows_are],
                                                             (i, j, k)[cols_are] % per))

    if mode == "nn":
        a_spec = spec(a_parts, 0, 2, tm, tk, K)
        b_spec = spec(b_parts, 2, 1, tk, tn, N)
    elif mode == "nt":
        a_spec = spec(a_parts, 0, 2, tm, tk, K)
        b_spec = spec(b_parts, 1, 2, tn, tk, K)
    else:
        a_spec = spec(a_parts, 2, 0, tk, tm, M)
        b_spec = spec(b_parts, 2, 1, tk, tn, N)
    o_spec = spec(out_parts, 0, 1, tm, tn, N)
    in_specs = [a_spec, b_spec]
    operands = [a, b]
    if res is not None:
        in_specs.append(pl.BlockSpec((tm, tn), lambda i, j, k: (i, j)))
        operands.append(res)

    def finish(refs, r):
        if res is not None:
            r = refs[2][...] + r
        refs[-2 if nk > 1 else -1][...] = r.astype(out_dtype)

    def body(*refs):
        part = lax.dot_general(_bf(refs[0][...]), _bf(refs[1][...]), dn, preferred_element_type=F32)
        if nk == 1:
            finish(refs, part)
            return
        acc_ref = refs[-1]
        k = pl.program_id(2)

        @pl.when(k == 0)
        def _():
            acc_ref[...] = part

        @pl.when(jnp.logical_and(k > 0, k < nk - 1))
        def _():
            acc_ref[...] += part

        @pl.when(k == nk - 1)
        def _():
            finish(refs, acc_ref[...] + part)

    out_shape = (M, N) if out_parts == 1 else (out_parts, M, N // out_parts)
    return pl.pallas_call(
        body, name=name, grid=grid, in_specs=in_specs, out_specs=o_spec,
        out_shape=jax.ShapeDtypeStruct(out_shape, out_dtype),
        scratch_shapes=[pltpu.VMEM((tm, tn), F32)] if nk > 1 else [],
        compiler_params=pltpu.CompilerParams(dimension_semantics=("parallel", "parallel", "arbitrary"),
                                             vmem_limit_bytes=VMEM_BIG),
    )(*operands)


def _rmsnorm_fwd(x, g, *, name):
    T, D = x.shape
    tm = _pick(T, (512, 256, 128, 64, 32, 16))

    def body(x_ref, g_ref, h_ref):
        xv = x_ref[...]
        r = lax.rsqrt(jnp.mean(xv * xv, axis=-1, keepdims=True) + RMS_EPS)
        h_ref[...] = ((xv * r) * g_ref[...]).astype(BF16)

    return pl.pallas_call(
        body, name=name, grid=(T // tm,),
        in_specs=[pl.BlockSpec((tm, D), lambda i: (i, 0)), pl.BlockSpec((1, D), lambda i: (0, 0))],
        out_specs=pl.BlockSpec((tm, D), lambda i: (i, 0)),
        out_shape=jax.ShapeDtypeStruct((T, D), BF16),
    )(x, g.reshape(1, D))


def _rmsnorm_bwd(x, g, dh, dx_in, *, name):
    T, D = x.shape
    tm = _pick(T, (512, 256, 128, 64, 32, 16))

    def body(x_ref, g_ref, dh_ref, dxin_ref, dx_ref, dg_ref):
        @pl.when(pl.program_id(0) == 0)
        def _():
            dg_ref[...] = jnp.zeros_like(dg_ref)

        xv = x_ref[...]
        r = lax.rsqrt(jnp.mean(xv * xv, axis=-1, keepdims=True) + RMS_EPS)
        xh = xv * r
        dh_v = dh_ref[...]
        dxh = dh_v * g_ref[...]
        dx_ref[...] = dxin_ref[...] + r * (dxh - xh * jnp.mean(dxh * xh, axis=-1, keepdims=True))
        dg_ref[...] += jnp.sum(dh_v * xh, axis=0, keepdims=True)

    row = pl.BlockSpec((tm, D), lambda i: (i, 0))
    vec = pl.BlockSpec((1, D), lambda i: (0, 0))
    return pl.pallas_call(
        body, name=name, grid=(T // tm,),
        in_specs=[row, vec, row, row], out_specs=[row, vec],
        out_shape=[jax.ShapeDtypeStruct((T, D), F32), jax.ShapeDtypeStruct((1, D), F32)],
        compiler_params=pltpu.CompilerParams(dimension_semantics=("arbitrary",)),
    )(x, g.reshape(1, D), dh, dx_in)


def _loss_head(y, target, *, name):
    T, D = y.shape
    tm = _pick(T, (512, 256, 128, 64, 32, 16))

    def body(y_ref, t_ref, dy_ref, l_ref):
        @pl.when(pl.program_id(0) == 0)
        def _():
            l_ref[...] = jnp.zeros_like(l_ref)

        err = y_ref[...] - t_ref[...]
        dy_ref[...] = err * (1.0 / D)
        l_ref[...] += 0.5 * jnp.sum(jnp.mean(err * err, axis=-1, keepdims=True), axis=0, keepdims=True)

    row = pl.BlockSpec((tm, D), lambda i: (i, 0))
    return pl.pallas_call(
        body, name=name, grid=(T // tm,),
        in_specs=[row, row], out_specs=[row, pl.BlockSpec((1, 1), lambda i: (0, 0))],
        out_shape=[jax.ShapeDtypeStruct((T, D), F32), jax.ShapeDtypeStruct((1, 1), F32)],
        compiler_params=pltpu.CompilerParams(dimension_semantics=("arbitrary",)),
    )(y, target)


def _sc_mid_fwd(p3, conv_w, *, name):
    _, T, W = p3.shape
    K = conv_w.shape[0]
    cw = LANES

    def body(p_ref, w_ref, o_ref):
        z = p_ref[1] * p_ref[2]
        cv = sum(w_ref[i:i + 1, :] * _shift_down(z, K - 1 - i) for i in range(K))
        o_ref[...] = ((p_ref[0] * cv) * _silu(p_ref[3])).astype(BF16)

    return pl.pallas_call(
        body, name=name, grid=(W // cw,),
        in_specs=[pl.BlockSpec((4, T, cw), lambda j: (0, 0, j)), pl.BlockSpec((K, cw), lambda j: (0, j))],
        out_specs=pl.BlockSpec((T, cw), lambda j: (0, j)),
        out_shape=jax.ShapeDtypeStruct((T, W), BF16),
        compiler_params=pltpu.CompilerParams(dimension_semantics=("parallel",), vmem_limit_bytes=VMEM_BIG),
    )(p3, conv_w)


def _sc_mid_bwd(p3, conv_w, do, *, name):
    _, T, W = p3.shape
    K = conv_w.shape[0]
    cw = LANES

    def body(p_ref, w_ref, do_ref, dp_ref, dw_ref):
        b, c, u, gate = p_ref[0], p_ref[1], p_ref[2], p_ref[3]
        z = c * u
        zs = [_shift_down(z, K - 1 - i) for i in range(K)]
        cv = sum(w_ref[i:i + 1, :] * zs[i] for i in range(K))
        y = b * cv
        dov = do_ref[...]
        dy = dov * _silu(gate)
        dp_ref[3] = dov * y * _dsilu(gate)
        dp_ref[0] = dy * cv
        dcv = dy * b
        dz = sum(w_ref[i:i + 1, :] * _shift_up(dcv, K - 1 - i) for i in range(K))
        dp_ref[1] = dz * u
        dp_ref[2] = dz * c
        for i in range(K):
            dw_ref[i:i + 1, :] = jnp.sum(dcv * zs[i], axis=0, keepdims=True)

    return pl.pallas_call(
        body, name=name, grid=(W // cw,),
        in_specs=[pl.BlockSpec((4, T, cw), lambda j: (0, 0, j)), pl.BlockSpec((K, cw), lambda j: (0, j)),
                  pl.BlockSpec((T, cw), lambda j: (0, j))],
        out_specs=[pl.BlockSpec((4, T, cw), lambda j: (0, 0, j)), pl.BlockSpec((K, cw), lambda j: (0, j))],
        out_shape=[jax.ShapeDtypeStruct((4, T, W), F32), jax.ShapeDtypeStruct((K, W), F32)],
        compiler_params=pltpu.CompilerParams(dimension_semantics=("parallel",), vmem_limit_bytes=VMEM_BIG),
    )(p3, conv_w, do)


def _sc_layer_fwd(x, ng, w_in, conv_w, w_out, tag):
    h = _rmsnorm_fwd(x, ng, name=f"{tag}_norm")
    p3 = _matmul(h, w_in, mode="nn", b_parts=4, out_parts=4, name=f"{tag}_inproj")
    og = _sc_mid_fwd(p3, conv_w, name=f"{tag}_mid")
    x_new = _matmul(og, w_out, mode="nn", res=x, name=f"{tag}_outproj")
    return x_new, (h, p3, og)


def _sc_layer_bwd(dx, x, ng, w_in, conv_w, w_out, saved, tag):
    h, p3, og = saved
    d_wout = _matmul(og, dx, mode="tn", out_dtype=BF16, name=f"{tag}_dwout")
    dog = _matmul(dx, w_out, mode="nt", name=f"{tag}_dog")
    dp3, dconv = _sc_mid_bwd(p3, conv_w, dog, name=f"{tag}_midbwd")
    d_win = _matmul(h, dp3, mode="tn", b_parts=4, out_parts=4, out_dtype=BF16, name=f"{tag}_dwin")
    dh = _matmul(dp3, w_in, mode="nt", a_parts=4, b_parts=4, name=f"{tag}_dh")
    dx_prev, dng = _rmsnorm_bwd(x, ng, dh, dx, name=f"{tag}_normbwd")
    return dx_prev, dng, d_win, dconv, d_wout


SB_BQ = 256
SB_BK = 256
SB_ROWS = 512


def _sb_half_mask():
    return lax.broadcasted_iota(jnp.int32, (1, LANES), 1) < SB_DH


def _sb_headnorm(x, g, lo):
    x2 = x * x
    s_lo = jnp.sum(jnp.where(lo, x2, 0.0), axis=-1, keepdims=True)
    s_hi = jnp.sum(jnp.where(lo, 0.0, x2), axis=-1, keepdims=True)
    r = lax.rsqrt(jnp.where(lo, s_lo, s_hi) * (1.0 / SB_DH) + RMS_EPS)
    xh = x * r
    return xh * g, xh, r


def _dot_x2_l(a_l, b_exact_bf16):
    his = [_bf(a) for a in a_l]
    mids = [_bf(a - h.astype(F32)) for a, h in zip(a_l, his)]
    f = lambda p: jnp.dot(p, b_exact_bf16, preferred_element_type=F32)
    return [x + y for x, y in zip([f(h) for h in his], [f(m) for m in mids])]


def _sb_stack(xb, lo):
    zero = jnp.zeros_like(xb)
    return jnp.concatenate([jnp.where(lo, xb, zero), jnp.where(lo, zero, xb)], axis=0)


def _sb_rel(bq, bk):
    row = lax.broadcasted_iota(jnp.int32, (2 * bq, bk), 0)
    col = lax.broadcasted_iota(jnp.int32, (2 * bq, bk), 1)
    return col - jnp.where(row >= bq, row - bq, row)


def _sb_tile(qm, kb, valid):
    z = lax.dot_general(qm, kb, (((1,), (1,)), ((), ())), preferred_element_type=F32)
    sp = _softplus(z)
    return z - sp, (-sp if valid is None else jnp.where(valid, -sp, 0.0))


def _sb_attn_fwd(p3, gq2, gk2, *, name):
    _, T, W = p3.shape
    bq, bk = min(SB_BQ, T), min(SB_BK, T)
    rows = min(SB_ROWS, T)
    scale = SB_DH ** -0.5

    def body(p_ref, gq_ref, gk_ref, og_ref, o_ref, ls_ref, qn_ref, kn_ref, v_ref):
        lo = _sb_half_mask()

        def prologue(i, c):
            r0 = pl.multiple_of(i * rows, rows)
            sl = pl.ds(r0, rows)
            qn_ref[sl, :] = (_sb_headnorm(p_ref[0, sl, :], gq_ref[...], lo)[0] * scale).astype(BF16)
            kn_ref[sl, :] = _sb_headnorm(p_ref[1, sl, :], gk_ref[...], lo)[0].astype(BF16)
            v_ref[sl, :] = p_ref[2, sl, :].astype(BF16)
            return c

        lax.fori_loop(0, T // rows, prologue, 0)

        rel = _sb_rel(bq, bk)
        tri = (lax.broadcasted_iota(jnp.int32, (bk, bk), 0)
               > lax.broadcasted_iota(jnp.int32, (bk, bk), 1)).astype(BF16)

        def qblock(qi, c):
            q0 = pl.multiple_of(qi * bq, bq)
            qm = _sb_stack(qn_ref[pl.ds(q0, bq), :], lo)
            nkb = (q0 + bq - 1) // bk + 1

            def tiles(k0s, carry, valid):
                o_acc, a_carry = carry
                sc = [_sb_tile(qm, kn_ref[pl.ds(k0, bk), :], valid) for k0 in k0s]
                later = _dot_x2_l([log1m for _, log1m in sc], tri)
                for (logsig, log1m), lat, k0 in zip(sc, later, k0s):
                    wts = jnp.exp(logsig + (lat + a_carry))
                    if valid is not None:
                        wts = jnp.where(valid, wts, 0.0)
                    o_acc = o_acc + jnp.dot(_bf(wts), v_ref[pl.ds(k0, bk), :], preferred_element_type=F32)
                    a_carry = a_carry + jnp.sum(log1m, axis=-1, keepdims=True)
                return o_acc, a_carry

            blk0 = lambda j: pl.multiple_of(j * bk, bk)
            k_last = blk0(nkb - 1)
            cr = tiles([k_last], (jnp.zeros((2 * bq, LANES), F32), jnp.zeros((2 * bq, 1), F32)), rel < q0 - k_last)
            cr = lax.fori_loop(0, (nkb - 1) // 2,
                               lambda t, cr: tiles([blk0(nkb - 2 - 2 * t), blk0(nkb - 3 - 2 * t)], cr, None), cr)
            o2, t2 = lax.fori_loop(0, (nkb - 1) % 2, lambda t, cr: tiles([blk0(0)], cr, None), cr)
            o = jnp.where(lo, o2[:bq], o2[bq:])
            o_ref[pl.ds(q0, bq), :] = o
            ls_ref[pl.ds(q0, bq), :] = jnp.where(lo, t2[:bq], t2[bq:])
            og_ref[pl.ds(q0, bq), :] = (o * _silu(p_ref[3, pl.ds(q0, bq), :])).astype(BF16)
            return c

        lax.fori_loop(0, T // bq, qblock, 0)

    colblk = pl.BlockSpec((T, LANES), lambda j: (0, j))
    vec = pl.BlockSpec((1, LANES), lambda j: (0, 0))
    return pl.pallas_call(
        body, name=name, grid=(W // LANES,),
        in_specs=[pl.BlockSpec((4, T, LANES), lambda j: (0, 0, j)), vec, vec],
        out_specs=[colblk, colblk, colblk],
        out_shape=[jax.ShapeDtypeStruct((T, W), BF16), jax.ShapeDtypeStruct((T, W), F32),
                   jax.ShapeDtypeStruct((T, W), F32)],
        scratch_shapes=[pltpu.VMEM((T, LANES), BF16)] * 3,
        compiler_params=pltpu.CompilerParams(dimension_semantics=("parallel",), vmem_limit_bytes=VMEM_BIG),
    )(p3, gq2, gk2)


def _sb_attn_bwd(p3, gq2, gk2, o, lsum, dog, *, name):
    _, T, W = p3.shape
    bq, bk = min(SB_BQ, T), min(SB_BK, T)
    rows = min(SB_ROWS, T)
    scale = SB_DH ** -0.5

    def body(p_ref, gq_ref, gk_ref, o_ref, ls_ref, dog_ref, dp_ref, dgq_ref, dgk_ref,
             qn_ref, kn_ref, v_ref, do_ref):
        lo = _sb_half_mask()

        def prologue(i, c):
            r0 = pl.multiple_of(i * rows, rows)
            sl = pl.ds(r0, rows)
            qn_ref[sl, :] = (_sb_headnorm(p_ref[0, sl, :], gq_ref[...], lo)[0] * scale).astype(BF16)
            kn_ref[sl, :] = _sb_headnorm(p_ref[1, sl, :], gk_ref[...], lo)[0].astype(BF16)
            v_ref[sl, :] = p_ref[2, sl, :].astype(BF16)
            gate = p_ref[3, sl, :]
            dogv = dog_ref[sl, :]
            dp_ref[3, sl, :] = dogv * o_ref[sl, :] * _dsilu(gate)
            do_ref[sl, :] = (dogv * _silu(gate)).astype(BF16)
            zero = jnp.zeros((rows, LANES), F32)
            dp_ref[0, sl, :] = zero
            dp_ref[1, sl, :] = zero
            dp_ref[2, sl, :] = zero
            return c

        lax.fori_loop(0, T // rows, prologue, 0)

        rel = _sb_rel(bq, bk)
        rj = lax.broadcasted_iota(jnp.int32, (bk, bk), 0)
        cj = lax.broadcasted_iota(jnp.int32, (bk, bk), 1)
        upto = (rj <= cj).astype(BF16)
        before_m = (rj < cj).astype(BF16)

        def qblock(qi, c):
            q0 = pl.multiple_of(qi * bq, bq)
            qm = _sb_stack(qn_ref[pl.ds(q0, bq), :], lo)
            dom = _sb_stack(do_ref[pl.ds(q0, bq), :], lo)
            lsb = ls_ref[pl.ds(q0, bq), :]
            total = jnp.concatenate([lsb[:, 0:1], lsb[:, SB_DH:SB_DH + 1]], axis=0)
            nkb = (q0 + bq - 1) // bk + 1

            def tiles(k0s, carry, valid):
                dq_acc, a_pre, r_pre = carry
                kss = [pl.ds(k0, bk) for k0 in k0s]
                kbs = [kn_ref[ks, :] for ks in kss]
                sc = [_sb_tile(qm, kb, valid) for kb in kbs]
                dws = [lax.dot_general(dom, v_ref[ks, :], _NT, preferred_element_type=F32) for ks in kss]
                upto_l = _dot_x2_l([log1m for _, log1m in sc], upto)
                wts_l = []
                for (logsig, log1m), up in zip(sc, upto_l):
                    wts = jnp.exp(logsig + ((total - a_pre) - up))
                    wts_l.append(wts if valid is None else jnp.where(valid, wts, 0.0))
                    a_pre = a_pre + jnp.sum(log1m, axis=-1, keepdims=True)
                ee_l = [dw * wts for dw, wts in zip(dws, wts_l)]
                before_l = _dot_x2_l(ee_l, before_m)
                for (logsig, _), ks, kb, wts, ee, bef in zip(sc, kss, kbs, wts_l, ee_l, before_l):
                    beta = jnp.exp(logsig)
                    dz = ee * (1.0 - beta) - beta * (r_pre + bef)
                    if valid is not None:
                        dz = jnp.where(valid, dz, 0.0)
                    dzb = _bf(dz)
                    dq_acc = dq_acc + jnp.dot(dzb, kb, preferred_element_type=F32)
                    dp_ref[1, ks, :] += lax.dot_general(dzb, qm, _TN, preferred_element_type=F32)
                    dp_ref[2, ks, :] += lax.dot_general(_bf(wts), dom, _TN, preferred_element_type=F32)
                    r_pre = r_pre + jnp.sum(ee, axis=-1, keepdims=True)
                return dq_acc, a_pre, r_pre

            blk0 = lambda j: pl.multiple_of(j * bk, bk)
            cr = (jnp.zeros((2 * bq, LANES), F32), jnp.zeros((2 * bq, 1), F32), jnp.zeros((2 * bq, 1), F32))
            cr = lax.fori_loop(0, (nkb - 1) // 2, lambda t, cr: tiles([blk0(2 * t), blk0(2 * t + 1)], cr, None), cr)
            cr = lax.fori_loop(0, (nkb - 1) % 2, lambda t, cr: tiles([blk0(nkb - 2)], cr, None), cr)
            k_last = blk0(nkb - 1)
            dq2, _, _ = tiles([k_last], cr, rel < q0 - k_last)
            dp_ref[0, pl.ds(q0, bq), :] = jnp.where(lo, dq2[:bq], dq2[bq:]) * scale
            return c

        lax.fori_loop(0, T // bq, qblock, 0)

        dgq_ref[...] = jnp.zeros_like(dgq_ref)
        dgk_ref[...] = jnp.zeros_like(dgk_ref)

        def epilogue(i, c):
            r0 = pl.multiple_of(i * rows, rows)
            sl = pl.ds(r0, rows)
            for part, g_ref, dg_ref in ((0, gq_ref, dgq_ref), (1, gk_ref, dgk_ref)):
                _, xh, r = _sb_headnorm(p_ref[part, sl, :], g_ref[...], lo)
                dn = dp_ref[part, sl, :]
                dxh = dn * g_ref[...]
                prod = dxh * xh
                m_lo = jnp.sum(jnp.where(lo, prod, 0.0), axis=-1, keepdims=True)
                m_hi = jnp.sum(jnp.where(lo, 0.0, prod), axis=-1, keepdims=True)
                m = jnp.where(lo, m_lo, m_hi) * (1.0 / SB_DH)
                dp_ref[part, sl, :] = r * (dxh - xh * m)
                dg_ref[...] += jnp.sum(dn * xh, axis=0, keepdims=True)
            return c

        lax.fori_loop(0, T // rows, epilogue, 0)

    colblk = pl.BlockSpec((T, LANES), lambda j: (0, j))
    vec = pl.BlockSpec((1, LANES), lambda j: (0, 0))
    part = pl.BlockSpec((4, T, LANES), lambda j: (0, 0, j))
    gvec = pl.BlockSpec((None, 1, LANES), lambda j: (j, 0, 0))
    npair = W // LANES
    return pl.pallas_call(
        body, name=name, grid=(npair,),
        in_specs=[part, vec, vec, colblk, colblk, colblk],
        out_specs=[part, gvec, gvec],
        out_shape=[jax.ShapeDtypeStruct((4, T, W), F32), jax.ShapeDtypeStruct((npair, 1, LANES), F32),
                   jax.ShapeDtypeStruct((npair, 1, LANES), F32)],
        scratch_shapes=[pltpu.VMEM((T, LANES), BF16)] * 4,
        compiler_params=pltpu.CompilerParams(dimension_semantics=("parallel",), vmem_limit_bytes=VMEM_BIG),
    )(p3, gq2, gk2, o, lsum, dog)


_NN = (((1,), (0,)), ((), ()))
_NT = (((1,), (1,)), ((), ()))
_TN = (((0,), (0,)), ((), ()))
DN_TB = 512
DN_AB_COL = (DN_CONV_W + DN_V_W) // LANES


def _dn_conv(x, w_ref):
    k = w_ref.shape[0]
    return sum(w_ref[i:i + 1, :] * _shift_down(x, k - 1 - i) for i in range(k))


def _dn_prep_fwd(p, conv_w, *, name):
    T = p.shape[0]
    cw = conv_w.shape[1]
    n_qk = 2 * DN_QK_W // LANES

    def body(p_ref, w_ref, o_ref):
        s = _silu(_dn_conv(p_ref[...], w_ref))
        r = lax.rsqrt(jnp.sum(s * s, axis=-1, keepdims=True) + L2_EPS)
        o_ref[...] = jnp.where(pl.program_id(0) < n_qk, s * r, s)

    colblk = pl.BlockSpec((T, LANES), lambda j: (0, j))
    return pl.pallas_call(
        body, name=name, grid=(cw // LANES,),
        in_specs=[colblk, pl.BlockSpec((DN_CONV, LANES), lambda j: (0, j))],
        out_specs=colblk, out_shape=jax.ShapeDtypeStruct((T, cw), F32),
        compiler_params=pltpu.CompilerParams(dimension_semantics=("parallel",), vmem_limit_bytes=VMEM_BIG),
    )(p, conv_w)


def _dn_chunk_tri(rows, upper):
    r = lax.broadcasted_iota(jnp.int32, (rows, rows), 0)
    c = lax.broadcasted_iota(jnp.int32, (rows, rows), 1)
    same = (r // DN_CHUNK) == (c // DN_CHUNK)
    return jnp.logical_and(same, (c >= r) if upper else (c <= r)).astype(BF16)


def _dn_lane_rows(a_log, dt_bias):
    pad = lambda v: jnp.zeros((1, LANES), F32).at[0, :DN_HEADS].set(v)
    return pad(a_log), pad(dt_bias)


def _dn_ab_parts(blk, alog_row, dtb_row):
    lane = lax.broadcasted_iota(jnp.int32, (1, LANES), 1)
    is_a = lane < DN_HEADS
    is_b = jnp.logical_and(lane >= DN_HEADS, lane < 2 * DN_HEADS)
    a_arg = jnp.where(is_a, blk + dtb_row, 0.0)
    neg_exp = jnp.where(is_a, -jnp.exp(alog_row), 0.0)
    log_a = neg_exp * _softplus(a_arg)
    beta = jnp.where(is_b, _sigmoid(blk), 0.0)
    return is_a, is_b, a_arg, neg_exp, log_a, beta


def _dn_ab_fwd(p, alog_row, dtb_row, *, name):
    T = p.shape[0]
    rows = min(DN_TB, T)

    def body(p_ref, al_ref, dt_ref, o_ref):
        _, _, _, _, log_a, beta = _dn_ab_parts(p_ref[...], al_ref[...], dt_ref[...])
        hi, mid, lo_ = _split3(log_a)
        tri = _dn_chunk_tri(rows, upper=False)
        f = lambda q: jnp.dot(tri, q, preferred_element_type=F32)
        o_ref[...] = (f(hi) + f(mid) + f(lo_)) + beta

    blk = pl.BlockSpec((rows, LANES), lambda i: (i, DN_AB_COL))
    vec = pl.BlockSpec((1, LANES), lambda i: (0, 0))
    return pl.pallas_call(
        body, name=name, grid=(T // rows,), in_specs=[blk, vec, vec],
        out_specs=pl.BlockSpec((rows, LANES), lambda i: (i, 0)),
        out_shape=jax.ShapeDtypeStruct((T, LANES), F32),
        compiler_params=pltpu.CompilerParams(dimension_semantics=("parallel",)),
    )(p, alog_row, dtb_row)


def _hp_l(a_l, b_l, dims=_NN):
    sa = [_split3(a)[:2] for a in a_l]
    sb = [_split3(b)[:2] for b in b_l]
    f = lambda p, q: lax.dot_general(p, q, dims, preferred_element_type=F32)
    hh = [f(x[0], y[0]) for x, y in zip(sa, sb)]
    hm = [f(x[0], y[1]) for x, y in zip(sa, sb)]
    mh = [f(x[1], y[0]) for x, y in zip(sa, sb)]
    return [a + (b + c) for a, b, c in zip(hh, hm, mh)]


def _dn_local(qs, k, v, g, beta, nc):
    c = DN_CHUNK
    cut = lambda x: [x[i * c:(i + 1) * c] for i in range(nc)]
    row = lax.broadcasted_iota(jnp.int32, (c, c), 0)
    col = lax.broadcasted_iota(jnp.int32, (c, c), 1)
    eye, lower, strict = row == col, row >= col, row > col
    rowid = lax.broadcasted_iota(jnp.int32, (c, 1), 0)
    eg = jnp.exp(g)
    kb = k * beta
    rhs_k = kb * eg
    g_l, k_l, kb_l, qs_l = cut(g), cut(k), cut(kb), cut(qs)
    g_row_l = [jnp.sum(jnp.where(eye, x, 0.0), axis=0, keepdims=True) for x in g_l]
    dec_l = [jnp.where(lower, jnp.exp(jnp.where(lower, x - y, 0.0)), 0.0) for x, y in zip(g_l, g_row_l)]
    kk_l = [_dot_nt(a, b) for a, b in zip(kb_l, k_l)]
    qk_l = [_dot_nt(a, b) for a, b in zip(qs_l, k_l)]
    low_l = [jnp.where(strict, a * d, 0.0) for a, d in zip(kk_l, dec_l)]
    eye_f = eye.astype(F32)
    pw_l = [-x for x in low_l]
    inv_l = [eye_f + x for x in pw_l]
    for _ in range(int(math.log2(c)) - 1):
        pw_l = _hp_l(pw_l, pw_l)
        inv_l = [a + b for a, b in zip(inv_l, _hp_l(inv_l, pw_l))]
    u_l = _hp_l(inv_l, cut(v * beta))
    w_l = _hp_l(inv_l, cut(rhs_k))
    aqk_l = [jnp.where(lower, a * d, 0.0) for a, d in zip(qk_l, dec_l)]
    g_last_l = [jnp.sum(jnp.where(rowid == c - 1, x, 0.0), axis=0, keepdims=True) for x in g_l]
    ekd_l = [jnp.exp(a - b) for a, b in zip(g_last_l, g_l)]
    kd_l = [a * b for a, b in zip(k_l, ekd_l)]
    return dict(eye=eye, lower=lower, strict=strict, dec=dec_l, k=k_l, kb=kb_l, qs=qs_l, low=low_l, inv=inv_l,
                eg=cut(eg), rhs_k=cut(rhs_k), u=u_l, w=w_l, aqk=aqk_l, g_last=g_last_l, qd=cut(qs * eg),
                ekd=ekd_l, kd=kd_l)


def _dn_head_cols(gb_blk, head):
    lane = lax.broadcasted_iota(jnp.int32, (1, LANES), 1)
    g = jnp.sum(jnp.where(lane == head, gb_blk, 0.0), axis=-1, keepdims=True)
    beta = jnp.sum(jnp.where(lane == head + DN_HEADS, gb_blk, 0.0), axis=-1, keepdims=True)
    return g, beta


def _dn_delta_fwd(qkv, gb, p, o_gain, *, name):
    T = qkv.shape[0]
    tb = min(DN_TB, T)
    nb, nc = T // tb, tb // DN_CHUNK
    H = DN_HEADS
    qscale = DN_DK ** -0.5

    def body(q_ref, k_ref, v_ref, gb_ref, gate_ref, gain_ref, o_ref, og_ref, st_ref, s_ref):
        head = pl.program_id(0)

        @pl.when(pl.program_id(1) == 0)
        def _():
            s_ref[...] = jnp.zeros_like(s_ref)

        g, beta = _dn_head_cols(gb_ref[...], head)
        t = _dn_local(q_ref[...] * qscale, k_ref[...], v_ref[...], g, beta, nc)
        s32 = s_ref[...]
        outs = []
        for i in range(nc):
            s_bf = _bf(s32)
            st_ref[i] = s_bf
            ws = _dot(t["w"][i], s_bf)
            qds = _dot(t["qd"][i], s_bf)
            vn = t["u"][i] - ws
            outs.append(qds + _dot(t["aqk"][i], vn))
            s32 = s32 * jnp.exp(t["g_last"][i]) + _dot_tn(t["kd"][i], vn)
        s_ref[...] = s32
        o = jnp.concatenate(outs, axis=0)
        o_ref[...] = o
        r = lax.rsqrt(jnp.mean(o * o, axis=-1, keepdims=True) + RMS_EPS)
        og_ref[...] = (((o * r) * gain_ref[...]) * _silu(gate_ref[...])).astype(BF16)

    qk = lambda off: pl.BlockSpec((tb, DN_DK), lambda h, i: (i, off + h))
    vblk = lambda off: pl.BlockSpec((tb, DN_DV), lambda h, i: (i, off + h))
    return pl.pallas_call(
        body, name=name, grid=(H, nb),
        in_specs=[qk(0), qk(H), vblk(2 * DN_QK_W // DN_DV), pl.BlockSpec((tb, LANES), lambda h, i: (i, 0)),
                  vblk(DN_CONV_W // DN_DV), pl.BlockSpec((1, DN_DV), lambda h, i: (0, 0))],
        out_specs=[vblk(0), vblk(0), pl.BlockSpec((None, nc, DN_DK, DN_DV), lambda h, i: (h, i, 0, 0))],
        out_shape=[jax.ShapeDtypeStruct((T, DN_V_W), F32), jax.ShapeDtypeStruct((T, DN_V_W), BF16),
                   jax.ShapeDtypeStruct((H, T // DN_CHUNK, DN_DK, DN_DV), BF16)],
        scratch_shapes=[pltpu.VMEM((DN_DK, DN_DV), F32)],
        compiler_params=pltpu.CompilerParams(dimension_semantics=("parallel", "arbitrary")),
    )(qkv, qkv, qkv, gb, p, o_gain)


def _dn_delta_bwd(qkv, gb, p, o_gain, o, states, dog, *, name):
    T = qkv.shape[0]
    tb = min(DN_TB, T)
    nb, nc = T // tb, tb // DN_CHUNK
    H = DN_HEADS
    qscale = DN_DK ** -0.5

    def body(q_ref, k_ref, v_ref, gb_ref, gate_ref, gain_ref, o_ref, st_ref, dog_ref,
             dq_ref, dk_ref, dv_ref, dgate_ref, dgb_ref, dgain_ref, ds_ref):
        head = pl.program_id(0)

        @pl.when(pl.program_id(1) == 0)
        def _():
            ds_ref[...] = jnp.zeros_like(ds_ref)

        @pl.when(jnp.logical_and(head == 0, pl.program_id(1) == 0))
        def _():
            dgain_ref[...] = jnp.zeros_like(dgain_ref)

        lane = lax.broadcasted_iota(jnp.int32, (1, LANES), 1)
        c = DN_CHUNK
        cut = lambda x: [x[i * c:(i + 1) * c] for i in range(nc)]
        cat = lambda xs: jnp.concatenate(xs, axis=0)
        rsum = lambda x: jnp.sum(x, axis=-1, keepdims=True)
        g, beta = _dn_head_cols(gb_ref[...], head)
        ov, gate, gain, dogv = o_ref[...], gate_ref[...], gain_ref[...], dog_ref[...]
        r = lax.rsqrt(jnp.mean(ov * ov, axis=-1, keepdims=True) + RMS_EPS)
        oh = ov * r
        dnrm = dogv * _silu(gate)
        dgate_ref[...] = dogv * (oh * gain) * _dsilu(gate)
        doh = dnrm * gain
        do_l = cut(r * (doh - oh * jnp.mean(doh * oh, axis=-1, keepdims=True)))
        dgain_ref[...] += jnp.sum(dnrm * oh, axis=0, keepdims=True)
        k, v = k_ref[...], v_ref[...]
        t = _dn_local(q_ref[...] * qscale, k, v, g, beta, nc)
        lower, strict, eye = t["lower"], t["strict"], t["eye"]
        s_l = [st_ref[i] for i in range(nc)]
        vn_l = [u - _dot(w, s) for u, w, s in zip(t["u"], t["w"], s_l)]
        dqd_l = [_dot_nt(a, s) for a, s in zip(do_l, s_l)]
        daqk_l = [_dot_nt(a, b) for a, b in zip(do_l, vn_l)]
        aqk_do_l = [_dot_tn(a, b) for a, b in zip(t["aqk"], do_l)]
        qd_do_l = [_dot_tn(a, b) for a, b in zip(t["qd"], do_l)]
        egl_l = [jnp.exp(x) for x in t["g_last"]]
        ds = ds_ref[...]
        dvn_l, dkd_l, dgl_l = [None] * nc, [None] * nc, [None] * nc
        for i in reversed(range(nc)):
            dvn_l[i] = aqk_do_l[i] + _dot(t["kd"][i], ds)
            dkd_l[i] = _dot_nt(vn_l[i], ds)
            dgl_l[i] = jnp.sum(rsum(ds * s_l[i].astype(F32)), axis=0, keepdims=True) * egl_l[i]
            ds = ds * egl_l[i] + qd_do_l[i] - _dot_tn(t["w"][i], dvn_l[i])
        ds_ref[...] = ds
        dw_l = [-_dot_nt(a, s) for a, s in zip(dvn_l, s_l)]
        dbv_l = _hp_l(t["inv"], dvn_l, _TN)
        dbk_l = _hp_l(t["inv"], dw_l, _TN)
        dlow_l = [-(a + b) for a, b in zip(_hp_l(dbv_l, t["u"], _NT), _hp_l(dbk_l, t["w"], _NT))]
        m_l = [jnp.where(strict, a * d, 0.0) for a, d in zip(dlow_l, t["dec"])]
        nmat_l = [jnp.where(lower, a * d, 0.0) for a, d in zip(daqk_l, t["dec"])]
        dkb_l = [_dot(m, kk) + b * e for m, kk, b, e in zip(m_l, t["k"], dbk_l, t["eg"])]
        dqs_l = [_dot(n, kk) + a * e for n, kk, a, e in zip(nmat_l, t["k"], dqd_l, t["eg"])]
        dk1_l = [_dot_tn(m, kb) for m, kb in zip(m_l, t["kb"])]
        dk2_l = [_dot_tn(n, q) for n, q in zip(nmat_l, t["qs"])]
        beta_l, v_l = cut(beta), cut(v)
        rowid = lax.broadcasted_iota(jnp.int32, (c, 1), 0)
        dk_l, dg_l, dbeta_l = [], [], []
        for i in range(nc):
            dk_l.append(dk1_l[i] + dk2_l[i] + dkd_l[i] * t["ekd"][i] + dkb_l[i] * beta_l[i])
            gmat = jnp.where(strict, dlow_l[i] * t["low"][i], 0.0) + daqk_l[i] * t["aqk"][i]
            s_kd = rsum(dkd_l[i] * t["kd"][i])
            dg = (rsum(gmat) + rsum(dqd_l[i] * t["qd"][i]) - s_kd + rsum(dbk_l[i] * t["rhs_k"][i]))
            dg_row = -jnp.sum(gmat, axis=0, keepdims=True)
            dg = dg + rsum(jnp.where(eye, dg_row, 0.0))
            dgl = dgl_l[i] + jnp.sum(s_kd, axis=0, keepdims=True)
            dg_l.append(dg + jnp.where(rowid == c - 1, dgl, 0.0))
            dbeta_l.append(rsum(dbv_l[i] * v_l[i]) + rsum(dkb_l[i] * t["k"][i]))
        dq_ref[...] = cat(dqs_l) * qscale
        dk_ref[...] = cat(dk_l)
        dv_ref[...] = cat(dbv_l) * beta
        dgb_ref[...] = (jnp.where(lane == head, cat(dg_l), 0.0)
                        + jnp.where(lane == head + DN_HEADS, cat(dbeta_l), 0.0))

    rev = lambda i: nb - 1 - i
    qk = lambda off: pl.BlockSpec((tb, DN_DK), lambda h, i: (rev(i), off + h))
    vblk = lambda off: pl.BlockSpec((tb, DN_DV), lambda h, i: (rev(i), off + h))
    gain_spec = pl.BlockSpec((1, DN_DV), lambda h, i: (0, 0))
    return pl.pallas_call(
        body, name=name, grid=(H, nb),
        in_specs=[qk(0), qk(H), vblk(2 * DN_QK_W // DN_DV), pl.BlockSpec((tb, LANES), lambda h, i: (rev(i), 0)),
                  vblk(DN_CONV_W // DN_DV), gain_spec, vblk(0),
                  pl.BlockSpec((None, nc, DN_DK, DN_DV), lambda h, i: (h, rev(i), 0, 0)), vblk(0)],
        out_specs=[qk(0), qk(0), vblk(0), vblk(DN_CONV_W // DN_DV),
                   pl.BlockSpec((None, tb, LANES), lambda h, i: (h, rev(i), 0)), gain_spec],
        out_shape=[jax.ShapeDtypeStruct((T, DN_QK_W), F32), jax.ShapeDtypeStruct((T, DN_QK_W), F32),
                   jax.ShapeDtypeStruct((T, DN_V_W), F32), jax.ShapeDtypeStruct((T, DN_IN_PAD), F32),
                   jax.ShapeDtypeStruct((H, T, LANES), F32), jax.ShapeDtypeStruct((1, DN_DV), F32)],
        scratch_shapes=[pltpu.VMEM((DN_DK, DN_DV), F32)],
        compiler_params=pltpu.CompilerParams(dimension_semantics=("arbitrary", "arbitrary")),
    )(qkv, qkv, qkv, gb, p, o_gain, o, states, dog)


def _dn_conv_bwd(p, conv_w, dq, dk, dv, dp, *, name):
    T = p.shape[0]
    cw = conv_w.shape[1]
    n_q = DN_QK_W // LANES
    n_v = DN_V_W // LANES

    def body(p_ref, w_ref, dq_ref, dk_ref, dv_ref, dp_in, dp_ref, dw_ref):
        del dp_in
        j = pl.program_id(0)
        x = p_ref[...]
        ksz = w_ref.shape[0]
        xs = [_shift_down(x, ksz - 1 - i) for i in range(ksz)]
        xc = sum(w_ref[i:i + 1, :] * xs[i] for i in range(ksz))
        s = _silu(xc)
        r = lax.rsqrt(jnp.sum(s * s, axis=-1, keepdims=True) + L2_EPS)
        y = s * r
        dn = jnp.where(j < n_q, dq_ref[...], dk_ref[...])
        ds_qk = r * (dn - y * jnp.sum(dn * y, axis=-1, keepdims=True))
        ds = jnp.where(j < 2 * n_q, ds_qk, dv_ref[...])
        dxc = ds * _dsilu(xc)
        dp_ref[...] = sum(w_ref[i:i + 1, :] * _shift_up(dxc, ksz - 1 - i) for i in range(ksz))
        for i in range(ksz):
            dw_ref[i:i + 1, :] = jnp.sum(dxc * xs[i], axis=0, keepdims=True)

    colblk = pl.BlockSpec((T, LANES), lambda j: (0, j))
    wblk = pl.BlockSpec((DN_CONV, LANES), lambda j: (0, j))
    return pl.pallas_call(
        body, name=name, grid=(cw // LANES,),
        in_specs=[colblk, wblk,
                  pl.BlockSpec((T, LANES), lambda j: (0, jnp.minimum(j, n_q - 1))),
                  pl.BlockSpec((T, LANES), lambda j: (0, jnp.clip(j - n_q, 0, n_q - 1))),
                  pl.BlockSpec((T, LANES), lambda j: (0, jnp.clip(j - 2 * n_q, 0, n_v - 1))),
                  pl.BlockSpec(memory_space=pl.ANY)],
        out_specs=[colblk, wblk],
        out_shape=[jax.ShapeDtypeStruct(dp.shape, F32), jax.ShapeDtypeStruct((DN_CONV, cw), F32)],
        input_output_aliases={5: 0},
        compiler_params=pltpu.CompilerParams(dimension_semantics=("parallel",), vmem_limit_bytes=VMEM_BIG),
    )(p, conv_w, dq, dk, dv, dp)


def _dn_ab_bwd(p, alog_row, dtb_row, dgb, dp, *, name):
    T = p.shape[0]
    rows = min(DN_TB, T)
    H = DN_HEADS

    def body(p_ref, al_ref, dt_ref, dgb_ref, dp_in, dp_ref, dal_ref, ddt_ref):
        del dp_in

        @pl.when(pl.program_id(0) == 0)
        def _():
            dal_ref[...] = jnp.zeros_like(dal_ref)
            ddt_ref[...] = jnp.zeros_like(ddt_ref)

        blk = p_ref[...]
        is_a, is_b, a_arg, neg_exp, log_a, beta = _dn_ab_parts(blk, al_ref[...], dt_ref[...])
        d = dgb_ref[0]
        for hh in range(1, H):
            d = d + dgb_ref[hh]
        hi, mid, lo_ = _split3(jnp.where(is_a, d, 0.0))
        tri = _dn_chunk_tri(rows, upper=True)
        f = lambda q: jnp.dot(tri, q, preferred_element_type=F32)
        dlog_a = f(hi) + f(mid) + f(lo_)
        da_in = dlog_a * neg_exp * _sigmoid(a_arg)
        db_in = jnp.where(is_b, d, 0.0) * beta * (1.0 - beta)
        dp_ref[...] = jnp.where(is_a, da_in, 0.0) + db_in
        dal_ref[...] += jnp.sum(dlog_a * log_a, axis=0, keepdims=True)
        ddt_ref[...] += jnp.sum(jnp.where(is_a, da_in, 0.0), axis=0, keepdims=True)

    blk = pl.BlockSpec((rows, LANES), lambda i: (i, DN_AB_COL))
    vec = pl.BlockSpec((1, LANES), lambda i: (0, 0))
    return pl.pallas_call(
        body, name=name, grid=(T // rows,),
        in_specs=[blk, vec, vec, pl.BlockSpec((H, rows, LANES), lambda i: (0, i, 0)),
                  pl.BlockSpec(memory_space=pl.ANY)],
        out_specs=[blk, vec, vec],
        out_shape=[jax.ShapeDtypeStruct(dp.shape, F32), jax.ShapeDtypeStruct((1, LANES), F32),
                   jax.ShapeDtypeStruct((1, LANES), F32)],
        input_output_aliases={4: 0},
        compiler_params=pltpu.CompilerParams(dimension_semantics=("arbitrary",)),
    )(p, alog_row, dtb_row, dgb, dp)


def _dn_layer_fwd(x, ng, w_in, conv_w, a_log, dt_bias, o_gain, w_out, tag):
    alog_row, dtb_row = _dn_lane_rows(a_log, dt_bias)
    gain = o_gain.reshape(1, DN_DV)
    h = _rmsnorm_fwd(x, ng, name=f"{tag}_norm")
    p = _matmul(h, w_in, mode="nn", name=f"{tag}_inproj")
    qkv = _dn_prep_fwd(p, conv_w, name=f"{tag}_prep")
    gb = _dn_ab_fwd(p, alog_row, dtb_row, name=f"{tag}_ab")
    o, og, states = _dn_delta_fwd(qkv, gb, p, gain, name=f"{tag}_delta")
    x_new = _matmul(og, w_out, mode="nn", res=x, name=f"{tag}_outproj")
    return x_new, (h, p, qkv, gb, o, og, states)


def _dn_layer_bwd(dx, x, ng, w_in, conv_w, a_log, dt_bias, o_gain, w_out, saved, tag):
    h, p, qkv, gb, o, og, states = saved
    alog_row, dtb_row = _dn_lane_rows(a_log, dt_bias)
    gain = o_gain.reshape(1, DN_DV)
    d_wout = _matmul(og, dx, mode="tn", out_dtype=BF16, name=f"{tag}_dwout")
    dog = _matmul(dx, w_out, mode="nt", name=f"{tag}_dog")
    dq, dk, dv, dp, dgb, dgain = _dn_delta_bwd(qkv, gb, p, gain, o, states, dog, name=f"{tag}_deltabwd")
    dp, dconv = _dn_conv_bwd(p, conv_w, dq, dk, dv, dp, name=f"{tag}_convbwd")
    dp, dal, ddt = _dn_ab_bwd(p, alog_row, dtb_row, dgb, dp, name=f"{tag}_abbwd")
    d_win = _matmul(h, dp, mode="tn", name=f"{tag}_dwin")
    dh = _matmul(dp, w_in, mode="nt", name=f"{tag}_dh")
    dx_prev, dng = _rmsnorm_bwd(x, ng, dh, dx, name=f"{tag}_normbwd")
    return dx_prev, dng, d_win, dconv, dal[0, :DN_HEADS], ddt[0, :DN_HEADS], dgain[0], d_wout


def _sb_gains(g):
    return jnp.concatenate([g, g]).reshape(1, LANES)


def _sb_layer_fwd(x, ng, w_in, gq, gk, w_out, tag):
    h = _rmsnorm_fwd(x, ng, name=f"{tag}_norm")
    p3 = _matmul(h, w_in, mode="nn", b_parts=4, out_parts=4, name=f"{tag}_inproj")
    og, o, lsum = _sb_attn_fwd(p3, _sb_gains(gq), _sb_gains(gk), name=f"{tag}_attn")
    x_new = _matmul(og, w_out, mode="nn", res=x, name=f"{tag}_outproj")
    return x_new, (h, p3, og, o, lsum)


def _sb_layer_bwd(dx, x, ng, w_in, gq, gk, w_out, saved, tag):
    h, p3, og, o, lsum = saved
    d_wout = _matmul(og, dx, mode="tn", out_dtype=BF16, name=f"{tag}_dwout")
    dog = _matmul(dx, w_out, mode="nt", name=f"{tag}_dog")
    dp3, dgq, dgk = _sb_attn_bwd(p3, _sb_gains(gq), _sb_gains(gk), o, lsum, dog, name=f"{tag}_attnbwd")
    fold = lambda d: jnp.sum(d.reshape(-1, SB_DH), axis=0)
    d_win = _matmul(h, dp3, mode="tn", b_parts=4, out_parts=4, out_dtype=BF16, name=f"{tag}_dwin")
    dh = _matmul(dp3, w_in, mode="nt", a_parts=4, b_parts=4, name=f"{tag}_dh")
    dx_prev, dng = _rmsnorm_bwd(x, ng, dh, dx, name=f"{tag}_normbwd")
    return dx_prev, dng, d_win, fold(dgq), fold(dgk), d_wout


N_CHIPS = 4
HBM = pl.BlockSpec(memory_space=pl.ANY)


def _mesh_pos():
    return lax.axis_index("x"), lax.axis_index("y"), lax.axis_index("c")


def _other_chips(x, y):
    return [(1 - x, y), (x, 1 - y), (1 - x, 1 - y)]


def _chip_exchange(srcs, *, send_slot_is_dest, copy_own, name):
    n = len(srcs)

    def body(*refs):
        src_refs, out_refs = refs[:n], refs[n:2 * n]
        send_sems, recv_sems, local_sems = refs[2 * n:]
        x, y, c = _mesh_pos()
        me = 2 * x + y
        chips = _other_chips(x, y)
        local = []
        for a in range(n):
            if not copy_own[a]:
                continue
            own = src_refs[a].at[me] if send_slot_is_dest else src_refs[a]
            local.append(pltpu.make_async_copy(own, out_refs[a].at[me], local_sems.at[a]))
        for cp in local:
            cp.start()

        def copy(a, k, landing_slot):
            px, py = chips[k]
            src = src_refs[a].at[2 * px + py] if send_slot_is_dest else src_refs[a]
            return pltpu.make_async_remote_copy(
                src_ref=src, dst_ref=out_refs[a].at[landing_slot],
                send_sem=send_sems.at[a * 3 + k], recv_sem=recv_sems.at[a * 3 + k],
                device_id=(px, py, c), device_id_type=MESH)

        sends = [copy(a, k, me) for a in range(n) for k in range(3)]
        for cp in sends:
            cp.start()
        for a in range(n):
            for k in range(3):
                px, py = chips[k]
                copy(a, k, 2 * px + py).wait_recv()
        for cp in sends:
            cp.wait_send()
        for cp in local:
            cp.wait()

    outs = []
    for s in srcs:
        shape = s.shape if send_slot_is_dest else (N_CHIPS,) + s.shape
        outs.append(jax.ShapeDtypeStruct(shape, s.dtype))
    return pl.pallas_call(
        body, name=name, in_specs=[HBM] * n, out_specs=[HBM] * n, out_shape=outs,
        scratch_shapes=[pltpu.SemaphoreType.DMA((3 * n,)), pltpu.SemaphoreType.DMA((3 * n,)),
                        pltpu.SemaphoreType.DMA((n,))],
    )(*srcs)


def _sibling_exchange(srcs, *, name):
    n = len(srcs)

    def body(*refs):
        src_refs, out_refs = refs[:n], refs[n:2 * n]
        send_sems, recv_sems = refs[2 * n:]
        x, y, c = _mesh_pos()
        copies = [pltpu.make_async_remote_copy(
            src_ref=src_refs[a], dst_ref=out_refs[a], send_sem=send_sems.at[a], recv_sem=recv_sems.at[a],
            device_id=(x, y, 1 - c), device_id_type=MESH) for a in range(n)]
        for cp in copies:
            cp.start()
        for cp in copies:
            cp.wait()

    return pl.pallas_call(
        body, name=name, in_specs=[HBM] * n, out_specs=[HBM] * n,
        out_shape=[jax.ShapeDtypeStruct(s.shape, s.dtype) for s in srcs],
        scratch_shapes=[pltpu.SemaphoreType.DMA((n,)), pltpu.SemaphoreType.DMA((n,))],
    )(*srcs)


def _gather_halves(shards, small, *, name):
    n = len(shards)

    def body(*refs):
        s_refs, small_ref = refs[:n], refs[n]
        o_refs, osmall_ref = refs[n + 1:2 * n + 1], refs[2 * n + 1]
        send_sems, recv_sems, local_sems = refs[2 * n + 2:]
        x, y, c = _mesh_pos()
        me = 2 * x + y
        chips = _other_chips(x, y)
        local = [pltpu.make_async_copy(small_ref, osmall_ref.at[me], local_sems.at[0])]
        for cp in local:
            cp.start()

        def over_ici(a, k, slot):
            px, py = chips[k]
            return pltpu.make_async_remote_copy(
                src_ref=s_refs[a].at[c], dst_ref=o_refs[a].at[slot, c], send_sem=send_sems.at[3 * a + k],
                recv_sem=recv_sems.at[3 * a + k], device_id=(px, py, c), device_id_type=MESH)

        def small_copy(k, slot):
            px, py = chips[k]
            return pltpu.make_async_remote_copy(
                src_ref=small_ref, dst_ref=osmall_ref.at[slot], send_sem=send_sems.at[3 * n + k],
                recv_sem=recv_sems.at[3 * n + k], device_id=(px, py, c), device_id_type=MESH)

        def to_sibling(a, k, half):
            px, py = chips[k]
            blk = o_refs[a].at[2 * px + py, half]
            return pltpu.make_async_remote_copy(
                src_ref=blk, dst_ref=blk, send_sem=send_sems.at[3 * n + 3 + 3 * a + k],
                recv_sem=recv_sems.at[3 * n + 3 + 3 * a + k], device_id=(x, y, 1 - c), device_id_type=MESH)

        sends = [over_ici(a, k, me) for a in range(n) for k in range(3)] + [small_copy(k, me) for k in range(3)]
        for cp in sends:
            cp.start()
        passed = []
        for a in range(n):
            for k in range(3):
                px, py = chips[k]
                over_ici(a, k, 2 * px + py).wait_recv()
                passed.append(to_sibling(a, k, c))
                passed[-1].start()
        for k in range(3):
            px, py = chips[k]
            small_copy(k, 2 * px + py).wait_recv()
        for a in range(n):
            for k in range(3):
                to_sibling(a, k, 1 - c).wait_recv()
        for cp in sends + passed:
            cp.wait_send()
        for cp in local:
            cp.wait()

    nsem = 6 * n + 3
    return pl.pallas_call(
        body, name=name, in_specs=[HBM] * (n + 1), out_specs=[HBM] * (n + 1),
        out_shape=[jax.ShapeDtypeStruct((N_CHIPS,) + s.shape, s.dtype) for s in shards + [small]],
        scratch_shapes=[pltpu.SemaphoreType.DMA((nsem,)), pltpu.SemaphoreType.DMA((nsem,)),
                        pltpu.SemaphoreType.DMA((1,))],
    )(*shards, small)


def _swap_other_half(g_list, *, name):
    n = len(g_list)

    def body(*refs):
        g_refs, o_refs = refs[:n], refs[n:2 * n]
        send_sems, recv_sems = refs[2 * n:]
        x, y, c = _mesh_pos()
        copies = [pltpu.make_async_remote_copy(
            src_ref=g_refs[a].at[:, 1 - c], dst_ref=o_refs[a], send_sem=send_sems.at[a], recv_sem=recv_sems.at[a],
            device_id=(x, y, 1 - c), device_id_type=MESH) for a in range(n)]
        for cp in copies:
            cp.start()
        for cp in copies:
            cp.wait()

    return pl.pallas_call(
        body, name=name, in_specs=[HBM] * n, out_specs=[HBM] * n,
        out_shape=[jax.ShapeDtypeStruct((g.shape[0],) + g.shape[2:], g.dtype) for g in g_list],
        scratch_shapes=[pltpu.SemaphoreType.DMA((n,)), pltpu.SemaphoreType.DMA((n,))],
    )(*g_list)


def _row_tile(r):
    return _pick(r, (512, 256, 128, 64, 32, 16, 8))


def _add_my_half(g4, sib4, core, *, name):
    n, _, r, C = g4.shape
    tr = _row_tile(r)

    def body(core_ref, g_ref, s_ref, o_ref):
        del core_ref
        o_ref[...] = (g_ref[...].astype(F32) + s_ref[...].astype(F32)).astype(o_ref.dtype)

    return pl.pallas_call(
        body, name=name,
        grid_spec=pltpu.PrefetchScalarGridSpec(
            num_scalar_prefetch=1, grid=(n, r // tr),
            in_specs=[pl.BlockSpec((None, None, tr, C), lambda j, i, core_ref: (j, core_ref[0], i, 0)),
                      pl.BlockSpec((None, tr, C), lambda j, i, core_ref: (j, i, 0))],
            out_specs=pl.BlockSpec((None, tr, C), lambda j, i, core_ref: (j, i, 0))),
        out_shape=jax.ShapeDtypeStruct((n, r, C), g4.dtype),
        compiler_params=pltpu.CompilerParams(dimension_semantics=("parallel", "parallel")),
    )(core, g4, sib4)


def _scatter_to_chips(p_list, *, name):
    n = len(p_list)

    def body(*refs):
        p_refs, o_refs = refs[:n], refs[n:2 * n]
        send_sems, recv_sems, local_sems = refs[2 * n:]
        x, y, c = _mesh_pos()
        me = 2 * x + y
        chips = _other_chips(x, y)
        own = [pltpu.make_async_copy(p_refs[a].at[me], o_refs[a].at[me], local_sems.at[a]) for a in range(n)]
        for cp in own:
            cp.start()

        def copy(a, k, landing_slot):
            px, py = chips[k]
            return pltpu.make_async_remote_copy(
                src_ref=p_refs[a].at[2 * px + py], dst_ref=o_refs[a].at[landing_slot],
                send_sem=send_sems.at[3 * a + k], recv_sem=recv_sems.at[3 * a + k], device_id=(px, py, c),
                device_id_type=MESH)

        sends = [copy(a, k, me) for a in range(n) for k in range(3)]
        for cp in sends:
            cp.start()
        for a in range(n):
            for k in range(3):
                px, py = chips[k]
                copy(a, k, 2 * px + py).wait_recv()
        for cp in sends:
            cp.wait_send()
        for cp in own:
            cp.wait()

    return pl.pallas_call(
        body, name=name, in_specs=[HBM] * n, out_specs=[HBM] * n,
        out_shape=[jax.ShapeDtypeStruct(p.shape, p.dtype) for p in p_list],
        scratch_shapes=[pltpu.SemaphoreType.DMA((3 * n,)), pltpu.SemaphoreType.DMA((3 * n,)),
                        pltpu.SemaphoreType.DMA((n,))],
    )(*p_list)


def _sum_chips(r4, *, name):
    _, r, C = r4.shape
    tr = _row_tile(r)

    def body(r_ref, o_ref):
        f = lambda j: r_ref[j].astype(F32)
        o_ref[...] = ((f(0) + f(1)) + f(2)) + f(3)

    return pl.pallas_call(
        body, name=name, grid=(r // tr,), in_specs=[pl.BlockSpec((N_CHIPS, tr, C), lambda i: (0, i, 0))],
        out_specs=pl.BlockSpec((tr, C), lambda i: (i, 0)), out_shape=jax.ShapeDtypeStruct((r, C), F32),
        compiler_params=pltpu.CompilerParams(dimension_semantics=("parallel",)),
    )(r4)


def _adamw_halves(w, mine, theirs, m, v, core, *, name):
    shape = w.shape
    r, C = mine.shape
    tr = _pick(r, (128, 64, 32, 16, 8))
    view = lambda a: a.reshape(2, r, C)

    def body(core_ref, w_ref, gm_ref, gt_ref, m_ref, v_ref, g_ref, d_ref, nm_ref, nv_ref):
        gv = jnp.where(pl.program_id(0) == core_ref[0], gm_ref[...], gt_ref[...])
        g_ref[...] = gv
        d_ref[...], nm_ref[...], nv_ref[...] = _adamw_math(w_ref[...], gv, m_ref[...], v_ref[...])

    half = pl.BlockSpec((None, tr, C), lambda h, i, core_ref: (h, i, 0))
    row = pl.BlockSpec((tr, C), lambda h, i, core_ref: (i, 0))
    out = jax.ShapeDtypeStruct((2, r, C), F32)
    res = pl.pallas_call(
        body, name=name,
        grid_spec=pltpu.PrefetchScalarGridSpec(
            num_scalar_prefetch=1, grid=(2, r // tr), in_specs=[half, row, row, half, half], out_specs=[half] * 4),
        out_shape=[out] * 4,
        compiler_params=pltpu.CompilerParams(dimension_semantics=("parallel", "parallel")),
    )(core, view(w), mine, theirs, view(m), view(v))
    return tuple(a.reshape(shape) for a in res)


def _sum_small(recv4, *, name):
    _, R, C = recv4.shape

    def body(r_ref, o_ref):
        o_ref[...] = ((r_ref[0] + r_ref[1]) + r_ref[2]) + r_ref[3]

    return pl.pallas_call(body, name=name, out_shape=jax.ShapeDtypeStruct((R, C), F32))(recv4)


def _add(a, b, *, name):
    R, C = a.shape
    tr = _pick(R, (512, 256, 128, 64, 32, 16, 8))
    blk = pl.BlockSpec((tr, C), lambda i: (i, 0))

    def body(a_ref, b_ref, o_ref):
        o_ref[...] = a_ref[...] + b_ref[...]

    return pl.pallas_call(body, name=name, grid=(R // tr,), in_specs=[blk, blk], out_specs=blk,
                          out_shape=jax.ShapeDtypeStruct((R, C), F32),
                          compiler_params=pltpu.CompilerParams(dimension_semantics=("parallel",)))(a, b)


def _adamw_math(w, g, m, v):
    nm = ADAM_B1 * m + (1.0 - ADAM_B1) * g
    nv = ADAM_B2 * v + (1.0 - ADAM_B2) * (g * g)
    m_hat = nm / (1.0 - ADAM_B1 ** ADAM_STEP)
    v_hat = nv / (1.0 - ADAM_B2 ** ADAM_STEP)
    return -ADAM_LR * (m_hat / (jnp.sqrt(v_hat) + ADAM_EPS) + ADAM_WD * w), nm, nv


def _adamw(w, g, m, v, *, name):
    shape = w.shape
    C = shape[-1]
    R = w.size // C
    two = lambda a: a.reshape(R, C)
    tr = _pick(R, (256, 128, 64, 32, 16, 8)) if R % 8 == 0 and R > 8 else R
    blk = pl.BlockSpec((tr, C), lambda i: (i, 0))

    def body(w_ref, g_ref, m_ref, v_ref, d_ref, nm_ref, nv_ref):
        d_ref[...], nm_ref[...], nv_ref[...] = _adamw_math(w_ref[...], g_ref[...], m_ref[...], v_ref[...])

    out = jax.ShapeDtypeStruct((R, C), F32)
    d, nm, nv = pl.pallas_call(
        body, name=name, grid=(R // tr,), in_specs=[blk] * 4, out_specs=[blk] * 3, out_shape=[out] * 3,
        compiler_params=pltpu.CompilerParams(dimension_semantics=("parallel",)),
    )(two(w), two(g), two(m), two(v))
    return d.reshape(shape), nm.reshape(shape), nv.reshape(shape)


BIG = (("dn_w_in", (2, 1024, 1540), 2), ("dn_w_out", (2, 512, 1024), 1), ("sb_w_in", (1, 1024, 1024), 2),
       ("sb_w_out", (1, 256, 1024), 1), ("sc_w_in", (1, 1024, 2048), 2), ("sc_w_out", (1, 512, 1024), 1))
SMALL = (("dn_conv_w", (2, 4, 1024), 2), ("dn_o_norm_g", (2, 64), 1), ("sc_conv_w", (1, 3, 512), 2))
REPL = (("norm_g", (4, 1024)), ("dn_a_log", (2, 8)), ("dn_dt_bias", (2, 8)), ("sb_q_norm_g", (1, 64)),
        ("sb_k_norm_g", (1, 64)))


def _halves(shard):
    return shard.reshape(2, -1, shard.shape[-1])


def _pack(arrays, cols, lead=()):
    flat = jnp.concatenate([a.reshape(lead + (-1,)) for a in arrays], axis=-1)
    n = flat.shape[-1]
    rows = -(-n // cols)
    unit = 512 if rows > 512 else 8
    rows = -(-rows // unit) * unit
    flat = jnp.pad(flat, [(0, 0)] * len(lead) + [(0, rows * cols - n)])
    return flat.reshape(lead + (rows, cols))


def _unpack(buf, table, lead=()):
    flat = buf.reshape(lead + (-1,))
    out, off = {}, 0
    for entry in table:
        name, shape = entry[0], entry[1]
        n = math.prod(shape)
        out[name] = flat[..., off:off + n].reshape(lead + shape)
        off += n
    return out


def _join(shards, axis):
    return jnp.concatenate([shards[j] for j in range(N_CHIPS)], axis=axis)


def _split(full, axis):
    return jnp.stack(jnp.split(full, N_CHIPS, axis=axis), axis=0)


def kernel(x, norm_g, dn_w_in, dn_conv_w, dn_a_log, dn_dt_bias, dn_o_norm_g, dn_w_out, sb_w_in, sb_q_norm_g, sb_k_norm_g, sb_w_out, sc_w_in, sc_conv_w, sc_w_out, loss_target, m_norm_g, m_dn_w_in, m_dn_conv_w, m_dn_a_log, m_dn_dt_bias, m_dn_o_norm_g, m_dn_w_out, m_sb_w_in, m_sb_q_norm_g, m_sb_k_norm_g, m_sb_w_out, m_sc_w_in, m_sc_conv_w, m_sc_w_out, v_norm_g, v_dn_w_in, v_dn_conv_w, v_dn_a_log, v_dn_dt_bias, v_dn_o_norm_g, v_dn_w_out, v_sb_w_in, v_sb_q_norm_g, v_sb_k_norm_g, v_sb_w_out, v_sc_w_in, v_sc_conv_w, v_sc_w_out):
    weights = dict(norm_g=norm_g, dn_w_in=dn_w_in, dn_conv_w=dn_conv_w, dn_a_log=dn_a_log, dn_dt_bias=dn_dt_bias,
                   dn_o_norm_g=dn_o_norm_g, dn_w_out=dn_w_out, sb_w_in=sb_w_in, sb_q_norm_g=sb_q_norm_g,
                   sb_k_norm_g=sb_k_norm_g, sb_w_out=sb_w_out, sc_w_in=sc_w_in, sc_conv_w=sc_conv_w, sc_w_out=sc_w_out)
    m_in = dict(norm_g=m_norm_g, dn_w_in=m_dn_w_in, dn_conv_w=m_dn_conv_w, dn_a_log=m_dn_a_log,
                dn_dt_bias=m_dn_dt_bias, dn_o_norm_g=m_dn_o_norm_g, dn_w_out=m_dn_w_out, sb_w_in=m_sb_w_in,
                sb_q_norm_g=m_sb_q_norm_g, sb_k_norm_g=m_sb_k_norm_g, sb_w_out=m_sb_w_out, sc_w_in=m_sc_w_in,
                sc_conv_w=m_sc_conv_w, sc_w_out=m_sc_w_out)
    v_in = dict(norm_g=v_norm_g, dn_w_in=v_dn_w_in, dn_conv_w=v_dn_conv_w, dn_a_log=v_dn_a_log,
                dn_dt_bias=v_dn_dt_bias, dn_o_norm_g=v_dn_o_norm_g, dn_w_out=v_dn_w_out, sb_w_in=v_sb_w_in,
                sb_q_norm_g=v_sb_q_norm_g, sb_k_norm_g=v_sb_k_norm_g, sb_w_out=v_sb_w_out, sc_w_in=v_sc_w_in,
                sc_conv_w=v_sc_conv_w, sc_w_out=v_sc_w_out)
    order = list(weights)
    xi, yi, ci = _mesh_pos()

    small = _pack([weights[n] for n, _, _ in SMALL], LANES)
    own = [_halves(weights[n].astype(BF16)) for n, _, _ in BIG]
    *big4, small4 = _gather_halves(own, small, name="gather_weights")
    me = 2 * xi + yi
    got = {n: lax.dynamic_update_index_in_dim(g, o, me, 0).reshape((N_CHIPS,) + shape)
           for (n, shape, _), g, o in zip(BIG, big4, own)}
    full = {n: _join(a, ax) for (n, _, ax), a in zip(SMALL, _unpack(small4, SMALL, (N_CHIPS,)).values())}
    dn_w_in_pad = jnp.pad(_join(got["dn_w_in"], 2), ((0, 0), (0, 0), (0, DN_IN_PAD - DN_IN)))
    rows_of = lambda w4: w4.reshape(-1, w4.shape[-1])

    def dn_args(j):
        return (dn_w_in_pad[j], full["dn_conv_w"][j], dn_a_log[j], dn_dt_bias[j], full["dn_o_norm_g"][j],
                rows_of(got["dn_w_out"][:, j]))

    sb_args = (got["sb_w_in"][:, 0], sb_q_norm_g[0], sb_k_norm_g[0], rows_of(got["sb_w_out"][:, 0]))
    sc_args = (got["sc_w_in"][:, 0], full["sc_conv_w"][0], rows_of(got["sc_w_out"][:, 0]))

    x0 = x[0]
    x1, s0 = _dn_layer_fwd(x0, norm_g[0], *dn_args(0), "l0")
    x2, s1 = _sb_layer_fwd(x1, norm_g[1], *sb_args, "l1")
    x3, s2 = _sc_layer_fwd(x2, norm_g[2], *sc_args, "l2")
    x4, s3 = _dn_layer_fwd(x3, norm_g[3], *dn_args(1), "l3")
    dy, loss_local = _loss_head(x4, loss_target[0], name="loss_head")
    loss = lax.psum(loss_local[0, 0], ("x", "y", "c"))

    dx3, dng3, dwin3, dconv3, dal3, ddt3, dgain3, dwout3 = _dn_layer_bwd(dy, x3, norm_g[3], *dn_args(1), s3, "l3")
    dx2, dng2, dwin2, dconv2, dwout2 = _sc_layer_bwd(dx3, x2, norm_g[2], *sc_args, s2, "l2")
    dx1, dng1, dwin1, dgq, dgk, dwout1 = _sb_layer_bwd(dx2, x1, norm_g[1], *sb_args, s1, "l1")
    dx0, dng0, dwin0, dconv0, dal0, ddt0, dgain0, dwout0 = _dn_layer_bwd(dx1, x0, norm_g[0], *dn_args(0), s0, "l0")

    grads = dict(
        norm_g=jnp.concatenate([dng0, dng1, dng2, dng3], axis=0), dn_conv_w=jnp.stack([dconv0, dconv3]),
        dn_a_log=jnp.stack([dal0, dal3]), dn_dt_bias=jnp.stack([ddt0, ddt3]),
        dn_o_norm_g=jnp.stack([dgain0, dgain3]), sb_q_norm_g=dgq[None], sb_k_norm_g=dgk[None],
        sc_conv_w=dconv2[None])
    by_cols = lambda dw: _split(dw[:, :DN_IN].astype(BF16), 1)
    by_rows = lambda dw: dw.reshape(N_CHIPS, -1, dw.shape[-1])
    cut2 = lambda g4: g4.reshape(N_CHIPS, 2, -1, g4.shape[-1])
    gbig = [jnp.stack([by_cols(dwin0), by_cols(dwin3)], axis=1), jnp.stack([by_rows(dwout0), by_rows(dwout3)], axis=1),
            cut2(dwin1), cut2(by_rows(dwout1)), cut2(dwin2), cut2(by_rows(dwout2))]

    core = ci.astype(jnp.int32).reshape(1)
    sib = _swap_other_half(gbig, name="swap_halves")
    part = [_add_my_half(g, s, core, name=f"sum_cores_{n}") for (n, _, _), g, s in zip(BIG, gbig, sib)]
    landed = _scatter_to_chips(part, name="scatter_grads")
    mine = [_sum_chips(r, name=f"sum_chips_{n}") for (n, _, _), r in zip(BIG, landed)]
    theirs = _sibling_exchange(mine, name="swap_results")
    upd = {n: _adamw_halves(weights[n], a, b, m_in[n], v_in[n], core, name=f"adamw_{n}")
           for (n, _, _), a, b in zip(BIG, mine, theirs)}
    g_out = {n: upd[n][0] for n, _, _ in BIG}
    repl = [jnp.broadcast_to(grads[n][None], (N_CHIPS,) + s) for n, s in REPL]
    gsmall = _pack([_split(grads[n], ax) for n, _, ax in SMALL] + repl, LANES, (N_CHIPS,))
    rsmall, = _chip_exchange([gsmall], send_slot_is_dest=True, copy_own=(True,), name="scatter_small")
    psmall = _sum_small(rsmall, name="sum_chips_small")
    qsmall, = _sibling_exchange([psmall], name="swap_cores_small")
    tsmall = _add(psmall, qsmall, name="sum_cores_small")
    g_out.update(_unpack(tsmall, SMALL + REPL))

    for n in order:
        if n not in upd:
            upd[n] = (g_out[n],) + _adamw(weights[n], g_out[n], m_in[n], v_in[n], name=f"adamw_{n}")
    return (loss, dx0[None], *[upd[n][0] for n in order], *[upd[n][1] for n in order],
            *[upd[n][2] for n in order], *[upd[n][3] for n in order])
```

```python
import math

import jax
import jax.numpy as jnp
from jax import lax
from jax.experimental import pallas as pl
from jax.experimental.pallas import tpu as pltpu

F32 = jnp.float32
BF16 = jnp.bfloat16
MESH = pl.DeviceIdType.MESH

RMS_EPS = 1e-6
L2_EPS = 1e-6
LANES = 128
VMEM_BIG = 60 * 1024 * 1024
MM_VMEM = 36 * 1024 * 1024

DN_HEADS, DN_DK, DN_DV, DN_CHUNK, DN_CONV = 8, 128, 256, 64, 4
DN_QK_W = DN_HEADS * DN_DK
DN_V_W = DN_HEADS * DN_DV
DN_CONV_W = 2 * DN_QK_W + DN_V_W
DN_IN = DN_CONV_W + DN_V_W + 2 * DN_HEADS
DN_IN_PAD = DN_CONV_W + DN_V_W + LANES
SB_DH = 64
SC_CONV = 3

ADAM_LR, ADAM_B1, ADAM_B2, ADAM_EPS, ADAM_WD, ADAM_STEP = 0.001, 0.9, 0.999, 1e-08, 0.01, 10


def _pick(n, cands):
    for c in cands:
        if n % c == 0:
            return c
    raise ValueError(f"no tile for {n} in {cands}")


def _bf(x):
    return x.astype(BF16)


def _dot(a, b):
    return jnp.dot(_bf(a), _bf(b), preferred_element_type=F32)


def _dot_nt(a, b):
    return lax.dot_general(_bf(a), _bf(b), (((1,), (1,)), ((), ())), preferred_element_type=F32)


def _dot_tn(a, b):
    return lax.dot_general(_bf(a), _bf(b), (((0,), (0,)), ((), ())), preferred_element_type=F32)


def _split3(a):
    hi = _bf(a)
    r = a - hi.astype(F32)
    mid = _bf(r)
    lo = _bf(r - mid.astype(F32))
    return hi, mid, lo


def _sigmoid(x):
    return 1.0 / (1.0 + jnp.exp(-x))


def _silu(x):
    return x * _sigmoid(x)


def _dsilu(x):
    s = _sigmoid(x)
    return s * (1.0 + x * (1.0 - s))


def _softplus(x):
    return jnp.maximum(x, 0.0) + jnp.log(1.0 + jnp.exp(-jnp.abs(x)))


def _shift_down(z, k):
    if k == 0:
        return z
    row = lax.broadcasted_iota(jnp.int32, z.shape, 0)
    return jnp.where(row >= k, pltpu.roll(z, k, 0), 0.0)


def _shift_up(z, k):
    if k == 0:
        return z
    n = z.shape[0]
    row = lax.broadcasted_iota(jnp.int32, z.shape, 0)
    return jnp.where(row < n - k, pltpu.roll(z, n - k, 0), 0.0)


def _matmul(a, b, *, mode, name, res=None, a_parts=1, b_parts=1, out_parts=1, out_dtype=F32):
    def dims2(x, parts):
        if parts == 1:
            return x.shape
        assert x.shape[0] == parts
        return (x.shape[1], x.shape[2] * parts)

    ash, bsh = dims2(a, a_parts), dims2(b, b_parts)
    if mode == "nn":
        (M, K), (K2, N) = ash, bsh
        dn = (((1,), (0,)), ((), ()))
    elif mode == "nt":
        (M, K), (N, K2) = ash, bsh
        dn = (((1,), (1,)), ((), ()))
    else:
        (K, M), (K2, N) = ash, bsh
        dn = (((0,), (0,)), ((), ()))
    assert K == K2, (ash, bsh, mode)
    tm = _pick(M, (512, 256, 128, 64, 32, 16, 8))
    n_unit = N // max(out_parts, b_parts if mode != "nt" else 1)
    k_unit = K // max(a_parts if mode != "tn" else 1, b_parts if mode == "nt" else 1)
    tn, tk = min(
        ((n, k) for n in (2048, 1792, 1024, 896, 768, 512, 384, 256, 128) if n_unit % n == 0
         for k in (2048, 1792, 1024, 896, 512, 256, 128) if k_unit % k == 0
         if 2 * (tm * k * a.dtype.itemsize + k * n * b.dtype.itemsize + 2 * tm * n * 4) + tm * n * 4 <= MM_VMEM),
        key=lambda nk_: (-nk_[0] * nk_[1], -nk_[1]))
    nk = K // tk
    grid = (M // tm, N // tn, nk)

    def spec(parts, rows_are, cols_are, tr, tc, width):
        per = width // parts // tc
        if parts == 1:
            return pl.BlockSpec((tr, tc), lambda i, j, k: ((i, j, k)[rows_are], (i, j, k)[cols_are]))
        return pl.BlockSpec((None, tr, tc), lambda i, j, k: ((i, j, k)[cols_are] // per, (i, j, k)[rows_are],
                                                             (i, j, k)[cols_are] % per))

    if mode == "nn":
        a_spec = spec(a_parts, 0, 2, tm, tk, K)
        b_spec = spec(b_parts, 2, 1, tk, tn, N)
    elif mode == "nt":
        a_spec = spec(a_parts, 0, 2, tm, tk, K)
        b_spec = spec(b_parts, 1, 2, tn, tk, K)
    else:
        a_spec = spec(a_parts, 2, 0, tk, tm, M)
        b_spec = spec(b_parts, 2, 1, tk, tn, N)
    o_spec = spec(out_parts, 0, 1, tm, tn, N)
    in_specs = [a_spec, b_spec]
    operands = [a, b]
    if res is not None:
        in_specs.append(pl.BlockSpec((tm, tn), lambda i, j, k: (i, j)))
        operands.append(res)

    def finish(refs, r):
        if res is not None:
            r = refs[2][...] + r
        refs[-2 if nk > 1 else -1][...] = r.astype(out_dtype)

    def body(*refs):
        part = lax.dot_general(_bf(refs[0][...]), _bf(refs[1][...]), dn, preferred_element_type=F32)
        if nk == 1:
            finish(refs, part)
            return
        acc_ref = refs[-1]
        k = pl.program_id(2)

        @pl.when(k == 0)
        def _():
            acc_ref[...] = part

        @pl.when(jnp.logical_and(k > 0, k < nk - 1))
        def _():
            acc_ref[...] += part

        @pl.when(k == nk - 1)
        def _():
            finish(refs, acc_ref[...] + part)

    out_shape = (M, N) if out_parts == 1 else (out_parts, M, N // out_parts)
    return pl.pallas_call(
        body, name=name, grid=grid, in_specs=in_specs, out_specs=o_spec,
        out_shape=jax.ShapeDtypeStruct(out_shape, out_dtype),
        scratch_shapes=[pltpu.VMEM((tm, tn), F32)] if nk > 1 else [],
        compiler_params=pltpu.CompilerParams(dimension_semantics=("parallel", "parallel", "arbitrary"),
                                             vmem_limit_bytes=VMEM_BIG),
    )(*operands)


def _rmsnorm_fwd(x, g, *, name):
    T, D = x.shape
    tm = _pick(T, (512, 256, 128, 64, 32, 16))

    def body(x_ref, g_ref, h_ref):
        xv = x_ref[...]
        r = lax.rsqrt(jnp.mean(xv * xv, axis=-1, keepdims=True) + RMS_EPS)
        h_ref[...] = ((xv * r) * g_ref[...]).astype(BF16)

    return pl.pallas_call(
        body, name=name, grid=(T // tm,),
        in_specs=[pl.BlockSpec((tm, D), lambda i: (i, 0)), pl.BlockSpec((1, D), lambda i: (0, 0))],
        out_specs=pl.BlockSpec((tm, D), lambda i: (i, 0)),
        out_shape=jax.ShapeDtypeStruct((T, D), BF16),
    )(x, g.reshape(1, D))


def _rmsnorm_bwd(x, g, dh, dx_in, *, name):
    T, D = x.shape
    tm = _pick(T, (512, 256, 128, 64, 32, 16))

    def body(x_ref, g_ref, dh_ref, dxin_ref, dx_ref, dg_ref):
        @pl.when(pl.program_id(0) == 0)
        def _():
            dg_ref[...] = jnp.zeros_like(dg_ref)

        xv = x_ref[...]
        r = lax.rsqrt(jnp.mean(xv * xv, axis=-1, keepdims=True) + RMS_EPS)
        xh = xv * r
        dh_v = dh_ref[...]
        dxh = dh_v * g_ref[...]
        dx_ref[...] = dxin_ref[...] + r * (dxh - xh * jnp.mean(dxh * xh, axis=-1, keepdims=True))
        dg_ref[...] += jnp.sum(dh_v * xh, axis=0, keepdims=True)

    row = pl.BlockSpec((tm, D), lambda i: (i, 0))
    vec = pl.BlockSpec((1, D), lambda i: (0, 0))
    return pl.pallas_call(
        body, name=name, grid=(T // tm,),
        in_specs=[row, vec, row, row], out_specs=[row, vec],
        out_shape=[jax.ShapeDtypeStruct((T, D), F32), jax.ShapeDtypeStruct((1, D), F32)],
        compiler_params=pltpu.CompilerParams(dimension_semantics=("arbitrary",)),
    )(x, g.reshape(1, D), dh, dx_in)


def _loss_head(y, target, *, name):
    T, D = y.shape
    tm = _pick(T, (512, 256, 128, 64, 32, 16))

    def body(y_ref, t_ref, dy_ref, l_ref):
        @pl.when(pl.program_id(0) == 0)
        def _():
            l_ref[...] = jnp.zeros_like(l_ref)

        err = y_ref[...] - t_ref[...]
        dy_ref[...] = err * (1.0 / D)
        l_ref[...] += 0.5 * jnp.sum(jnp.mean(err * err, axis=-1, keepdims=True), axis=0, keepdims=True)

    row = pl.BlockSpec((tm, D), lambda i: (i, 0))
    return pl.pallas_call(
        body, name=name, grid=(T // tm,),
        in_specs=[row, row], out_specs=[row, pl.BlockSpec((1, 1), lambda i: (0, 0))],
        out_shape=[jax.ShapeDtypeStruct((T, D), F32), jax.ShapeDtypeStruct((1, 1), F32)],
        compiler_params=pltpu.CompilerParams(dimension_semantics=("arbitrary",)),
    )(y, target)


def _sc_mid_fwd(p3, conv_w, *, name):
    _, T, W = p3.shape
    K = conv_w.shape[0]
    cw = LANES

    def body(p_ref, w_ref, o_ref):
        z = p_ref[1] * p_ref[2]
        cv = sum(w_ref[i:i + 1, :] * _shift_down(z, K - 1 - i) for i in range(K))
        o_ref[...] = ((p_ref[0] * cv) * _silu(p_ref[3])).astype(BF16)

    return pl.pallas_call(
        body, name=name, grid=(W // cw,),
        in_specs=[pl.BlockSpec((4, T, cw), lambda j: (0, 0, j)), pl.BlockSpec((K, cw), lambda j: (0, j))],
        out_specs=pl.BlockSpec((T, cw), lambda j: (0, j)),
        out_shape=jax.ShapeDtypeStruct((T, W), BF16),
        compiler_params=pltpu.CompilerParams(dimension_semantics=("parallel",), vmem_limit_bytes=VMEM_BIG),
    )(p3, conv_w)


def _sc_mid_bwd(p3, conv_w, do, *, name):
    _, T, W = p3.shape
    K = conv_w.shape[0]
    cw = LANES

    def body(p_ref, w_ref, do_ref, dp_ref, dw_ref):
        b, c, u, gate = p_ref[0], p_ref[1], p_ref[2], p_ref[3]
        z = c * u
        zs = [_shift_down(z, K - 1 - i) for i in range(K)]
        cv = sum(w_ref[i:i + 1, :] * zs[i] for i in range(K))
        y = b * cv
        dov = do_ref[...]
        dy = dov * _silu(gate)
        dp_ref[3] = dov * y * _dsilu(gate)
        dp_ref[0] = dy * cv
        dcv = dy * b
        dz = sum(w_ref[i:i + 1, :] * _shift_up(dcv, K - 1 - i) for i in range(K))
        dp_ref[1] = dz * u
        dp_ref[2] = dz * c
        for i in range(K):
            dw_ref[i:i + 1, :] = jnp.sum(dcv * zs[i], axis=0, keepdims=True)

    return pl.pallas_call(
        body, name=name, grid=(W // cw,),
        in_specs=[pl.BlockSpec((4, T, cw), lambda j: (0, 0, j)), pl.BlockSpec((K, cw), lambda j: (0, j)),
                  pl.BlockSpec((T, cw), lambda j: (0, j))],
        out_specs=[pl.BlockSpec((4, T, cw), lambda j: (0, 0, j)), pl.BlockSpec((K, cw), lambda j: (0, j))],
        out_shape=[jax.ShapeDtypeStruct((4, T, W), F32), jax.ShapeDtypeStruct((K, W), F32)],
        compiler_params=pltpu.CompilerParams(dimension_semantics=("parallel",), vmem_limit_bytes=VMEM_BIG),
    )(p3, conv_w, do)


def _sc_layer_fwd(x, ng, w_in, conv_w, w_out, tag):
    h = _rmsnorm_fwd(x, ng, name=f"{tag}_norm")
    p3 = _matmul(h, w_in, mode="nn", b_parts=4, out_parts=4, name=f"{tag}_inproj")
    og = _sc_mid_fwd(p3, conv_w, name=f"{tag}_mid")
    x_new = _matmul(og, w_out, mode="nn", res=x, name=f"{tag}_outproj")
    return x_new, (h, p3, og)


def _sc_layer_bwd(dx, x, ng, w_in, conv_w, w_out, saved, tag):
    h, p3, og = saved
    d_wout = _matmul(og, dx, mode="tn", out_dtype=BF16, name=f"{tag}_dwout")
    dog = _matmul(dx, w_out, mode="nt", name=f"{tag}_dog")
    dp3, dconv = _sc_mid_bwd(p3, conv_w, dog, name=f"{tag}_midbwd")
    d_win = _matmul(h, dp3, mode="tn", b_parts=4, out_parts=4, out_dtype=BF16, name=f"{tag}_dwin")
    dh = _matmul(dp3, w_in, mode="nt", a_parts=4, b_parts=4, name=f"{tag}_dh")
    dx_prev, dng = _rmsnorm_bwd(x, ng, dh, dx, name=f"{tag}_normbwd")
    return dx_prev, dng, d_win, dconv, d_wout


SB_BQ = 256
SB_BK = 256
SB_ROWS = 512


def _sb_half_mask():
    return lax.broadcasted_iota(jnp.int32, (1, LANES), 1) < SB_DH


def _sb_headnorm(x, g, lo):
    x2 = x * x
    s_lo = jnp.sum(jnp.where(lo, x2, 0.0), axis=-1, keepdims=True)
    s_hi = jnp.sum(jnp.where(lo, 0.0, x2), axis=-1, keepdims=True)
    r = lax.rsqrt(jnp.where(lo, s_lo, s_hi) * (1.0 / SB_DH) + RMS_EPS)
    xh = x * r
    return xh * g, xh, r


def _dot_x2_l(a_l, b_exact_bf16):
    his = [_bf(a) for a in a_l]
    mids = [_bf(a - h.astype(F32)) for a, h in zip(a_l, his)]
    f = lambda p: jnp.dot(p, b_exact_bf16, preferred_element_type=F32)
    return [x + y for x, y in zip([f(h) for h in his], [f(m) for m in mids])]


def _sb_stack(xb, lo):
    zero = jnp.zeros_like(xb)
    return jnp.concatenate([jnp.where(lo, xb, zero), jnp.where(lo, zero, xb)], axis=0)


def _sb_rel(bq, bk):
    row = lax.broadcasted_iota(jnp.int32, (2 * bq, bk), 0)
    col = lax.broadcasted_iota(jnp.int32, (2 * bq, bk), 1)
    return col - jnp.where(row >= bq, row - bq, row)


def _sb_tile(qm, kb, valid):
    z = lax.dot_general(qm, kb, (((1,), (1,)), ((), ())), preferred_element_type=F32)
    sp = _softplus(z)
    return z - sp, (-sp if valid is None else jnp.where(valid, -sp, 0.0))


def _sb_attn_fwd(p3, gq2, gk2, *, name):
    _, T, W = p3.shape
    bq, bk = min(SB_BQ, T), min(SB_BK, T)
    rows = min(SB_ROWS, T)
    scale = SB_DH ** -0.5

    def body(p_ref, gq_ref, gk_ref, og_ref, o_ref, ls_ref, qn_ref, kn_ref, v_ref):
        lo = _sb_half_mask()

        def prologue(i, c):
            r0 = pl.multiple_of(i * rows, rows)
            sl = pl.ds(r0, rows)
            qn_ref[sl, :] = (_sb_headnorm(p_ref[0, sl, :], gq_ref[...], lo)[0] * scale).astype(BF16)
            kn_ref[sl, :] = _sb_headnorm(p_ref[1, sl, :], gk_ref[...], lo)[0].astype(BF16)
            v_ref[sl, :] = p_ref[2, sl, :].astype(BF16)
            return c

        lax.fori_loop(0, T // rows, prologue, 0)

        rel = _sb_rel(bq, bk)
        tri = (lax.broadcasted_iota(jnp.int32, (bk, bk), 0)
               > lax.broadcasted_iota(jnp.int32, (bk, bk), 1)).astype(BF16)

        def qblock(qi, c):
            q0 = pl.multiple_of(qi * bq, bq)
            qm = _sb_stack(qn_ref[pl.ds(q0, bq), :], lo)
            nkb = (q0 + bq - 1) // bk + 1

            def tiles(k0s, carry, valid):
                o_acc, a_carry = carry
                sc = [_sb_tile(qm, kn_ref[pl.ds(k0, bk), :], valid) for k0 in k0s]
                later = _dot_x2_l([log1m for _, log1m in sc], tri)
                for (logsig, log1m), lat, k0 in zip(sc, later, k0s):
                    wts = jnp.exp(logsig + (lat + a_carry))
                    if valid is not None:
                        wts = jnp.where(valid, wts, 0.0)
                    o_acc = o_acc + jnp.dot(_bf(wts), v_ref[pl.ds(k0, bk), :], preferred_element_type=F32)
                    a_carry = a_carry + jnp.sum(log1m, axis=-1, keepdims=True)
                return o_acc, a_carry

            blk0 = lambda j: pl.multiple_of(j * bk, bk)
            k_last = blk0(nkb - 1)
            cr = tiles([k_last], (jnp.zeros((2 * bq, LANES), F32), jnp.zeros((2 * bq, 1), F32)), rel < q0 - k_last)
            cr = lax.fori_loop(0, (nkb - 1) // 2,
                               lambda t, cr: tiles([blk0(nkb - 2 - 2 * t), blk0(nkb - 3 - 2 * t)], cr, None), cr)
            o2, t2 = lax.fori_loop(0, (nkb - 1) % 2, lambda t, cr: tiles([blk0(0)], cr, None), cr)
            o = jnp.where(lo, o2[:bq], o2[bq:])
            o_ref[pl.ds(q0, bq), :] = o
            ls_ref[pl.ds(q0, bq), :] = jnp.where(lo, t2[:bq], t2[bq:])
            og_ref[pl.ds(q0, bq), :] = (o * _silu(p_ref[3, pl.ds(q0, bq), :])).astype(BF16)
            return c

        lax.fori_loop(0, T // bq, qblock, 0)

    colblk = pl.BlockSpec((T, LANES), lambda j: (0, j))
    vec = pl.BlockSpec((1, LANES), lambda j: (0, 0))
    return pl.pallas_call(
        body, name=name, grid=(W // LANES,),
        in_specs=[pl.BlockSpec((4, T, LANES), lambda j: (0, 0, j)), vec, vec],
        out_specs=[colblk, colblk, colblk],
        out_shape=[jax.ShapeDtypeStruct((T, W), BF16), jax.ShapeDtypeStruct((T, W), F32),
                   jax.ShapeDtypeStruct((T, W), F32)],
        scratch_shapes=[pltpu.VMEM((T, LANES), BF16)] * 3,
        compiler_params=pltpu.CompilerParams(dimension_semantics=("parallel",), vmem_limit_bytes=VMEM_BIG),
    )(p3, gq2, gk2)


def _sb_attn_bwd(p3, gq2, gk2, o, lsum, dog, *, name):
    _, T, W = p3.shape
    bq, bk = min(SB_BQ, T), min(SB_BK, T)
    rows = min(SB_ROWS, T)
    scale = SB_DH ** -0.5

    def body(p_ref, gq_ref, gk_ref, o_ref, ls_ref, dog_ref, dp_ref, dgq_ref, dgk_ref,
             qn_ref, kn_ref, v_ref, do_ref):
        lo = _sb_half_mask()

        def prologue(i, c):
            r0 = pl.multiple_of(i * rows, rows)
            sl = pl.ds(r0, rows)
            qn_ref[sl, :] = (_sb_headnorm(p_ref[0, sl, :], gq_ref[...], lo)[0] * scale).astype(BF16)
            kn_ref[sl, :] = _sb_headnorm(p_ref[1, sl, :], gk_ref[...], lo)[0].astype(BF16)
            v_ref[sl, :] = p_ref[2, sl, :].astype(BF16)
            gate = p_ref[3, sl, :]
            dogv = dog_ref[sl, :]
            dp_ref[3, sl, :] = dogv * o_ref[sl, :] * _dsilu(gate)
            do_ref[sl, :] = (dogv * _silu(gate)).astype(BF16)
            zero = jnp.zeros((rows, LANES), F32)
            dp_ref[0, sl, :] = zero
            dp_ref[1, sl, :] = zero
            dp_ref[2, sl, :] = zero
            return c

        lax.fori_loop(0, T // rows, prologue, 0)

        rel = _sb_rel(bq, bk)
        rj = lax.broadcasted_iota(jnp.int32, (bk, bk), 0)
        cj = lax.broadcasted_iota(jnp.int32, (bk, bk), 1)
        upto = (rj <= cj).astype(BF16)
        before_m = (rj < cj).astype(BF16)

        def qblock(qi, c):
            q0 = pl.multiple_of(qi * bq, bq)
            qm = _sb_stack(qn_ref[pl.ds(q0, bq), :], lo)
            dom = _sb_stack(do_ref[pl.ds(q0, bq), :], lo)
            lsb = ls_ref[pl.ds(q0, bq), :]
            total = jnp.concatenate([lsb[:, 0:1], lsb[:, SB_DH:SB_DH + 1]], axis=0)
            nkb = (q0 + bq - 1) // bk + 1

            def tiles(k0s, carry, valid):
                dq_acc, a_pre, r_pre = carry
                kss = [pl.ds(k0, bk) for k0 in k0s]
                kbs = [kn_ref[ks, :] for ks in kss]
                sc = [_sb_tile(qm, kb, valid) for kb in kbs]
                dws = [lax.dot_general(dom, v_ref[ks, :], _NT, preferred_element_type=F32) for ks in kss]
                upto_l = _dot_x2_l([log1m for _, log1m in sc], upto)
                wts_l = []
                for (logsig, log1m), up in zip(sc, upto_l):
                    wts = jnp.exp(logsig + ((total - a_pre) - up))
                    wts_l.append(wts if valid is None else jnp.where(valid, wts, 0.0))
                    a_pre = a_pre + jnp.sum(log1m, axis=-1, keepdims=True)
                ee_l = [dw * wts for dw, wts in zip(dws, wts_l)]
                before_l = _dot_x2_l(ee_l, before_m)
                for (logsig, _), ks, kb, wts, ee, bef in zip(sc, kss, kbs, wts_l, ee_l, before_l):
                    beta = jnp.exp(logsig)
                    dz = ee * (1.0 - beta) - beta * (r_pre + bef)
                    if valid is not None:
                        dz = jnp.where(valid, dz, 0.0)
                    dzb = _bf(dz)
                    dq_acc = dq_acc + jnp.dot(dzb, kb, preferred_element_type=F32)
                    dp_ref[1, ks, :] += lax.dot_general(dzb, qm, _TN, preferred_element_type=F32)
                    dp_ref[2, ks, :] += lax.dot_general(_bf(wts), dom, _TN, preferred_element_type=F32)
                    r_pre = r_pre + jnp.sum(ee, axis=-1, keepdims=True)
                return dq_acc, a_pre, r_pre

            blk0 = lambda j: pl.multiple_of(j * bk, bk)
            cr = (jnp.zeros((2 * bq, LANES), F32), jnp.zeros((2 * bq, 1), F32), jnp.zeros((2 * bq, 1), F32))
            cr = lax.fori_loop(0, (nkb - 1) // 2, lambda t, cr: tiles([blk0(2 * t), blk0(2 * t + 1)], cr, None), cr)
            cr = lax.fori_loop(0, (nkb - 1) % 2, lambda t, cr: tiles([blk0(nkb - 2)], cr, None), cr)
            k_last = blk0(nkb - 1)
            dq2, _, _ = tiles([k_last], cr, rel < q0 - k_last)
            dp_ref[0, pl.ds(q0, bq), :] = jnp.where(lo, dq2[:bq], dq2[bq:]) * scale
            return c

        lax.fori_loop(0, T // bq, qblock, 0)

        dgq_ref[...] = jnp.zeros_like(dgq_ref)
        dgk_ref[...] = jnp.zeros_like(dgk_ref)

        def epilogue(i, c):
            r0 = pl.multiple_of(i * rows, rows)
            sl = pl.ds(r0, rows)
            for part, g_ref, dg_ref in ((0, gq_ref, dgq_ref), (1, gk_ref, dgk_ref)):
                _, xh, r = _sb_headnorm(p_ref[part, sl, :], g_ref[...], lo)
                dn = dp_ref[part, sl, :]
                dxh = dn * g_ref[...]
                prod = dxh * xh
                m_lo = jnp.sum(jnp.where(lo, prod, 0.0), axis=-1, keepdims=True)
                m_hi = jnp.sum(jnp.where(lo, 0.0, prod), axis=-1, keepdims=True)
                m = jnp.where(lo, m_lo, m_hi) * (1.0 / SB_DH)
                dp_ref[part, sl, :] = r * (dxh - xh * m)
                dg_ref[...] += jnp.sum(dn * xh, axis=0, keepdims=True)
            return c

        lax.fori_loop(0, T // rows, epilogue, 0)

    colblk = pl.BlockSpec((T, LANES), lambda j: (0, j))
    vec = pl.BlockSpec((1, LANES), lambda j: (0, 0))
    part = pl.BlockSpec((4, T, LANES), lambda j: (0, 0, j))
    gvec = pl.BlockSpec((None, 1, LANES), lambda j: (j, 0, 0))
    npair = W // LANES
    return pl.pallas_call(
        body, name=name, grid=(npair,),
        in_specs=[part, vec, vec, colblk, colblk, colblk],
        out_specs=[part, gvec, gvec],
        out_shape=[jax.ShapeDtypeStruct((4, T, W), F32), jax.ShapeDtypeStruct((npair, 1, LANES), F32),
                   jax.ShapeDtypeStruct((npair, 1, LANES), F32)],
        scratch_shapes=[pltpu.VMEM((T, LANES), BF16)] * 4,
        compiler_params=pltpu.CompilerParams(dimension_semantics=("parallel",), vmem_limit_bytes=VMEM_BIG),
    )(p3, gq2, gk2, o, lsum, dog)


_NN = (((1,), (0,)), ((), ()))
_NT = (((1,), (1,)), ((), ()))
_TN = (((0,), (0,)), ((), ()))
DN_TB = 512
DN_AB_COL = (DN_CONV_W + DN_V_W) // LANES


def _dn_conv(x, w_ref):
    k = w_ref.shape[0]
    return sum(w_ref[i:i + 1, :] * _shift_down(x, k - 1 - i) for i in range(k))


def _dn_prep_fwd(p, conv_w, *, name):
    T = p.shape[0]
    cw = conv_w.shape[1]
    n_qk = 2 * DN_QK_W // LANES

    def body(p_ref, w_ref, o_ref):
        s = _silu(_dn_conv(p_ref[...], w_ref))
        r = lax.rsqrt(jnp.sum(s * s, axis=-1, keepdims=True) + L2_EPS)
        o_ref[...] = jnp.where(pl.program_id(0) < n_qk, s * r, s)

    colblk = pl.BlockSpec((T, LANES), lambda j: (0, j))
    return pl.pallas_call(
        body, name=name, grid=(cw // LANES,),
        in_specs=[colblk, pl.BlockSpec((DN_CONV, LANES), lambda j: (0, j))],
        out_specs=colblk, out_shape=jax.ShapeDtypeStruct((T, cw), F32),
        compiler_params=pltpu.CompilerParams(dimension_semantics=("parallel",), vmem_limit_bytes=VMEM_BIG),
    )(p, conv_w)


def _dn_chunk_tri(rows, upper):
    r = lax.broadcasted_iota(jnp.int32, (rows, rows), 0)
    c = lax.broadcasted_iota(jnp.int32, (rows, rows), 1)
    same = (r // DN_CHUNK) == (c // DN_CHUNK)
    return jnp.logical_and(same, (c >= r) if upper else (c <= r)).astype(BF16)


def _dn_lane_rows(a_log, dt_bias):
    pad = lambda v: jnp.zeros((1, LANES), F32).at[0, :DN_HEADS].set(v)
    return pad(a_log), pad(dt_bias)


def _dn_ab_parts(blk, alog_row, dtb_row):
    lane = lax.broadcasted_iota(jnp.int32, (1, LANES), 1)
    is_a = lane < DN_HEADS
    is_b = jnp.logical_and(lane >= DN_HEADS, lane < 2 * DN_HEADS)
    a_arg = jnp.where(is_a, blk + dtb_row, 0.0)
    neg_exp = jnp.where(is_a, -jnp.exp(alog_row), 0.0)
    log_a = neg_exp * _softplus(a_arg)
    beta = jnp.where(is_b, _sigmoid(blk), 0.0)
    return is_a, is_b, a_arg, neg_exp, log_a, beta


def _dn_ab_fwd(p, alog_row, dtb_row, *, name):
    T = p.shape[0]
    rows = min(DN_TB, T)

    def body(p_ref, al_ref, dt_ref, o_ref):
        _, _, _, _, log_a, beta = _dn_ab_parts(p_ref[...], al_ref[...], dt_ref[...])
        hi, mid, lo_ = _split3(log_a)
        tri = _dn_chunk_tri(rows, upper=False)
        f = lambda q: jnp.dot(tri, q, preferred_element_type=F32)
        o_ref[...] = (f(hi) + f(mid) + f(lo_)) + beta

    blk = pl.BlockSpec((rows, LANES), lambda i: (i, DN_AB_COL))
    vec = pl.BlockSpec((1, LANES), lambda i: (0, 0))
    return pl.pallas_call(
        body, name=name, grid=(T // rows,), in_specs=[blk, vec, vec],
        out_specs=pl.BlockSpec((rows, LANES), lambda i: (i, 0)),
        out_shape=jax.ShapeDtypeStruct((T, LANES), F32),
        compiler_params=pltpu.CompilerParams(dimension_semantics=("parallel",)),
    )(p, alog_row, dtb_row)


def _hp_l(a_l, b_l, dims=_NN):
    sa = [_split3(a)[:2] for a in a_l]
    sb = [_split3(b)[:2] for b in b_l]
    f = lambda p, q: lax.dot_general(p, q, dims, preferred_element_type=F32)
    hh = [f(x[0], y[0]) for x, y in zip(sa, sb)]
    hm = [f(x[0], y[1]) for x, y in zip(sa, sb)]
    mh = [f(x[1], y[0]) for x, y in zip(sa, sb)]
    return [a + (b + c) for a, b, c in zip(hh, hm, mh)]


def _dn_local(qs, k, v, g, beta, nc):
    c = DN_CHUNK
    cut = lambda x: [x[i * c:(i + 1) * c] for i in range(nc)]
    row = lax.broadcasted_iota(jnp.int32, (c, c), 0)
    col = lax.broadcasted_iota(jnp.int32, (c, c), 1)
    eye, lower, strict = row == col, row >= col, row > col
    rowid = lax.broadcasted_iota(jnp.int32, (c, 1), 0)
    eg = jnp.exp(g)
    kb = k * beta
    rhs_k = kb * eg
    g_l, k_l, kb_l, qs_l = cut(g), cut(k), cut(kb), cut(qs)
    g_row_l = [jnp.sum(jnp.where(eye, x, 0.0), axis=0, keepdims=True) for x in g_l]
    dec_l = [jnp.where(lower, jnp.exp(jnp.where(lower, x - y, 0.0)), 0.0) for x, y in zip(g_l, g_row_l)]
    kk_l = [_dot_nt(a, b) for a, b in zip(kb_l, k_l)]
    qk_l = [_dot_nt(a, b) for a, b in zip(qs_l, k_l)]
    low_l = [jnp.where(strict, a * d, 0.0) for a, d in zip(kk_l, dec_l)]
    eye_f = eye.astype(F32)
    pw_l = [-x for x in low_l]
    inv_l = [eye_f + x for x in pw_l]
    for _ in range(int(math.log2(c)) - 1):
        pw_l = _hp_l(pw_l, pw_l)
        inv_l = [a + b for a, b in zip(inv_l, _hp_l(inv_l, pw_l))]
    u_l = _hp_l(inv_l, cut(v * beta))
    w_l = _hp_l(inv_l, cut(rhs_k))
    aqk_l = [jnp.where(lower, a * d, 0.0) for a, d in zip(qk_l, dec_l)]
    g_last_l = [jnp.sum(jnp.where(rowid == c - 1, x, 0.0), axis=0, keepdims=True) for x in g_l]
    ekd_l = [jnp.exp(a - b) for a, b in zip(g_last_l, g_l)]
    kd_l = [a * b for a, b in zip(k_l, ekd_l)]
    qd_l = cut(qs * eg)
    kw_l = [_dot_tn(a, b) for a, b in zip(kd_l, w_l)]
    qp_l = [q - _dot(a, w) for q, a, w in zip(qd_l, aqk_l, w_l)]
    return dict(eye=eye, lower=lower, strict=strict, dec=dec_l, k=k_l, kb=kb_l, qs=qs_l, low=low_l, inv=inv_l,
                eg=cut(eg), rhs_k=cut(rhs_k), u=u_l, w=w_l, aqk=aqk_l, g_last=g_last_l, qd=qd_l,
                ekd=ekd_l, kd=kd_l, kw=kw_l, qp=qp_l)


def _dn_head_cols(gb_blk, head):
    lane = lax.broadcasted_iota(jnp.int32, (1, LANES), 1)
    g = jnp.sum(jnp.where(lane == head, gb_blk, 0.0), axis=-1, keepdims=True)
    beta = jnp.sum(jnp.where(lane == head + DN_HEADS, gb_blk, 0.0), axis=-1, keepdims=True)
    return g, beta


def _halves_over_ici(s_refs, o_refs, send_sems, recv_sems, first, last):
    x, y, c = _mesh_pos()
    me = 2 * x + y
    chips = _other_chips(x, y)
    pairs = [(a, k) for a in range(len(s_refs)) for k in range(3)]

    def copy(a, k, slot):
        px, py = chips[k]
        return pltpu.make_async_remote_copy(
            src_ref=s_refs[a].at[c], dst_ref=o_refs[a].at[slot, c], send_sem=send_sems.at[3 * a + k],
            recv_sem=recv_sems.at[3 * a + k], device_id=(px, py, c), device_id_type=MESH)

    @pl.when(first)
    def _():
        for a, k in pairs:
            copy(a, k, me).start()

    @pl.when(last)
    def _():
        for a, k in pairs:
            px, py = chips[k]
            copy(a, k, 2 * px + py).wait_recv()
        for a, k in pairs:
            copy(a, k, me).wait_send()


def _dn_delta_fwd(qkv, gb, p, o_gain, *, name, send=()):
    T = qkv.shape[0]
    tb = min(DN_TB, T)
    nb, nc = T // tb, tb // DN_CHUNK
    H = DN_HEADS
    qscale = DN_DK ** -0.5
    ns = len(send)

    def body(*refs):
        q_ref, k_ref, v_ref, gb_ref, gate_ref, gain_ref = refs[:6]
        o_ref, og_ref, st_ref = refs[6 + ns:9 + ns]
        s_ref = refs[9 + 2 * ns]
        head, blk = pl.program_id(0), pl.program_id(1)
        if ns:
            _halves_over_ici(refs[6:6 + ns], refs[9 + ns:9 + 2 * ns], refs[10 + 2 * ns], refs[11 + 2 * ns],
                             jnp.logical_and(head == 0, blk == 0), jnp.logical_and(head == H - 1, blk == nb - 1))

        @pl.when(blk == 0)
        def _():
            s_ref[...] = jnp.zeros_like(s_ref)

        g, beta = _dn_head_cols(gb_ref[...], head)
        t = _dn_local(q_ref[...] * qscale, k_ref[...], v_ref[...], g, beta, nc)
        ku_l = [_dot_tn(a, b) for a, b in zip(t["kd"], t["u"])]
        op_l = [_dot(a, b) for a, b in zip(t["aqk"], t["u"])]
        s32 = s_ref[...]
        s_l = []
        for i in range(nc):
            s_bf = _bf(s32)
            st_ref[i] = s_bf
            s_l.append(s_bf)
            s32 = s32 * jnp.exp(t["g_last"][i]) - _dot(t["kw"][i], s_bf) + ku_l[i]
        s_ref[...] = s32
        o = jnp.concatenate([_dot(qp, sb) + op for qp, sb, op in zip(t["qp"], s_l, op_l)], axis=0)
        o_ref[...] = o
        r = lax.rsqrt(jnp.mean(o * o, axis=-1, keepdims=True) + RMS_EPS)
        og_ref[...] = (((o * r) * gain_ref[...]) * _silu(gate_ref[...])).astype(BF16)

    qk = lambda off: pl.BlockSpec((tb, DN_DK), lambda h, i: (i, off + h))
    vblk = lambda off: pl.BlockSpec((tb, DN_DV), lambda h, i: (i, off + h))
    return pl.pallas_call(
        body, name=name, grid=(H, nb),
        in_specs=[qk(0), qk(H), vblk(2 * DN_QK_W // DN_DV), pl.BlockSpec((tb, LANES), lambda h, i: (i, 0)),
                  vblk(DN_CONV_W // DN_DV), pl.BlockSpec((1, DN_DV), lambda h, i: (0, 0))] + [HBM] * ns,
        out_specs=[vblk(0), vblk(0), pl.BlockSpec((None, nc, DN_DK, DN_DV), lambda h, i: (h, i, 0, 0))] + [HBM] * ns,
        out_shape=[jax.ShapeDtypeStruct((T, DN_V_W), F32), jax.ShapeDtypeStruct((T, DN_V_W), BF16),
                   jax.ShapeDtypeStruct((H, T // DN_CHUNK, DN_DK, DN_DV), BF16)]
        + [jax.ShapeDtypeStruct((N_CHIPS,) + a.shape, a.dtype) for a in send],
        scratch_shapes=[pltpu.VMEM((DN_DK, DN_DV), F32)]
        + ([pltpu.SemaphoreType.DMA((3 * ns,)), pltpu.SemaphoreType.DMA((3 * ns,))] if ns else []),
        compiler_params=pltpu.CompilerParams(dimension_semantics=("arbitrary", "arbitrary")),
    )(qkv, qkv, qkv, gb, p, o_gain, *send)


def _dn_delta_bwd(qkv, gb, p, o_gain, o, states, dog, *, name):
    T = qkv.shape[0]
    tb = min(DN_TB, T)
    nb, nc = T // tb, tb // DN_CHUNK
    H = DN_HEADS
    qscale = DN_DK ** -0.5

    def body(q_ref, k_ref, v_ref, gb_ref, gate_ref, gain_ref, o_ref, st_ref, dog_ref,
             dq_ref, dk_ref, dv_ref, dgate_ref, dgb_ref, dgain_ref, ds_ref):
        head = pl.program_id(0)

        @pl.when(pl.program_id(1) == 0)
        def _():
            ds_ref[...] = jnp.zeros_like(ds_ref)

        @pl.when(jnp.logical_and(head == 0, pl.program_id(1) == 0))
        def _():
            dgain_ref[...] = jnp.zeros_like(dgain_ref)

        lane = lax.broadcasted_iota(jnp.int32, (1, LANES), 1)
        c = DN_CHUNK
        cut = lambda x: [x[i * c:(i + 1) * c] for i in range(nc)]
        cat = lambda xs: jnp.concatenate(xs, axis=0)
        rsum = lambda x: jnp.sum(x, axis=-1, keepdims=True)
        g, beta = _dn_head_cols(gb_ref[...], head)
        ov, gate, gain, dogv = o_ref[...], gate_ref[...], gain_ref[...], dog_ref[...]
        r = lax.rsqrt(jnp.mean(ov * ov, axis=-1, keepdims=True) + RMS_EPS)
        oh = ov * r
        dnrm = dogv * _silu(gate)
        dgate_ref[...] = dogv * (oh * gain) * _dsilu(gate)
        doh = dnrm * gain
        do_l = cut(r * (doh - oh * jnp.mean(doh * oh, axis=-1, keepdims=True)))
        dgain_ref[...] += jnp.sum(dnrm * oh, axis=0, keepdims=True)
        k, v = k_ref[...], v_ref[...]
        t = _dn_local(q_ref[...] * qscale, k, v, g, beta, nc)
        lower, strict, eye = t["lower"], t["strict"], t["eye"]
        s_l = [st_ref[i] for i in range(nc)]
        vn_l = [u - _dot(w, s) for u, w, s in zip(t["u"], t["w"], s_l)]
        dqd_l = [_dot_nt(a, s) for a, s in zip(do_l, s_l)]
        daqk_l = [_dot_nt(a, b) for a, b in zip(do_l, vn_l)]
        aqk_do_l = [_dot_tn(a, b) for a, b in zip(t["aqk"], do_l)]
        egl_l = [jnp.exp(x) for x in t["g_last"]]
        qp_do_l = [_dot_tn(a, b) for a, b in zip(t["qp"], do_l)]
        ds = ds_ref[...]
        ds_l = [None] * nc
        for i in reversed(range(nc)):
            ds_l[i] = ds
            ds = ds * egl_l[i] - _dot_tn(t["kw"][i], ds) + qp_do_l[i]
        ds_ref[...] = ds
        dvn_l = [a + _dot(kd, d) for a, kd, d in zip(aqk_do_l, t["kd"], ds_l)]
        dkd_l = [_dot_nt(a, d) for a, d in zip(vn_l, ds_l)]
        dgl_l = [jnp.sum(rsum(d * sb.astype(F32)), axis=0, keepdims=True) * e for d, sb, e in zip(ds_l, s_l, egl_l)]
        dw_l = [-_dot_nt(a, s) for a, s in zip(dvn_l, s_l)]
        dbv_l = _hp_l(t["inv"], dvn_l, _TN)
        dbk_l = _hp_l(t["inv"], dw_l, _TN)
        dlow_l = [-(a + b) for a, b in zip(_hp_l(dbv_l, t["u"], _NT), _hp_l(dbk_l, t["w"], _NT))]
        m_l = [jnp.where(strict, a * d, 0.0) for a, d in zip(dlow_l, t["dec"])]
        nmat_l = [jnp.where(lower, a * d, 0.0) for a, d in zip(daqk_l, t["dec"])]
        dkb_l = [_dot(m, kk) + b * e for m, kk, b, e in zip(m_l, t["k"], dbk_l, t["eg"])]
        dqs_l = [_dot(n, kk) + a * e for n, kk, a, e in zip(nmat_l, t["k"], dqd_l, t["eg"])]
        dk1_l = [_dot_tn(m, kb) for m, kb in zip(m_l, t["kb"])]
        dk2_l = [_dot_tn(n, q) for n, q in zip(nmat_l, t["qs"])]
        beta_l, v_l = cut(beta), cut(v)
        rowid = lax.broadcasted_iota(jnp.int32, (c, 1), 0)
        dk_l, dg_l, dbeta_l = [], [], []
        for i in range(nc):
            dk_l.append(dk1_l[i] + dk2_l[i] + dkd_l[i] * t["ekd"][i] + dkb_l[i] * beta_l[i])
            gmat = jnp.where(strict, dlow_l[i] * t["low"][i], 0.0) + daqk_l[i] * t["aqk"][i]
            s_kd = rsum(dkd_l[i] * t["kd"][i])
            dg = (rsum(gmat) + rsum(dqd_l[i] * t["qd"][i]) - s_kd + rsum(dbk_l[i] * t["rhs_k"][i]))
            dg_row = -jnp.sum(gmat, axis=0, keepdims=True)
            dg = dg + rsum(jnp.where(eye, dg_row, 0.0))
            dgl = dgl_l[i] + jnp.sum(s_kd, axis=0, keepdims=True)
            dg_l.append(dg + jnp.where(rowid == c - 1, dgl, 0.0))
            dbeta_l.append(rsum(dbv_l[i] * v_l[i]) + rsum(dkb_l[i] * t["k"][i]))
        dq_ref[...] = cat(dqs_l) * qscale
        dk_ref[...] = cat(dk_l)
        dv_ref[...] = cat(dbv_l) * beta
        dgb_ref[...] = (jnp.where(lane == head, cat(dg_l), 0.0)
                        + jnp.where(lane == head + DN_HEADS, cat(dbeta_l), 0.0))

    rev = lambda i: nb - 1 - i
    qk = lambda off: pl.BlockSpec((tb, DN_DK), lambda h, i: (rev(i), off + h))
    vblk = lambda off: pl.BlockSpec((tb, DN_DV), lambda h, i: (rev(i), off + h))
    gain_spec = pl.BlockSpec((1, DN_DV), lambda h, i: (0, 0))
    return pl.pallas_call(
        body, name=name, grid=(H, nb),
        in_specs=[qk(0), qk(H), vblk(2 * DN_QK_W // DN_DV), pl.BlockSpec((tb, LANES), lambda h, i: (rev(i), 0)),
                  vblk(DN_CONV_W // DN_DV), gain_spec, vblk(0),
                  pl.BlockSpec((None, nc, DN_DK, DN_DV), lambda h, i: (h, rev(i), 0, 0)), vblk(0)],
        out_specs=[qk(0), qk(0), vblk(0), vblk(DN_CONV_W // DN_DV),
                   pl.BlockSpec((None, tb, LANES), lambda h, i: (h, rev(i), 0)), gain_spec],
        out_shape=[jax.ShapeDtypeStruct((T, DN_QK_W), F32), jax.ShapeDtypeStruct((T, DN_QK_W), F32),
                   jax.ShapeDtypeStruct((T, DN_V_W), F32), jax.ShapeDtypeStruct((T, DN_IN_PAD), F32),
                   jax.ShapeDtypeStruct((H, T, LANES), F32), jax.ShapeDtypeStruct((1, DN_DV), F32)],
        scratch_shapes=[pltpu.VMEM((DN_DK, DN_DV), F32)],
        compiler_params=pltpu.CompilerParams(dimension_semantics=("arbitrary", "arbitrary")),
    )(qkv, qkv, qkv, gb, p, o_gain, o, states, dog)


def _dn_conv_bwd(p, conv_w, d, dp, *, first, normed, name):
    T, width = d.shape

    def body(p_ref, w_ref, d_ref, dp_in, dp_ref, dw_ref):
        del dp_in
        x = p_ref[...]
        ksz = w_ref.shape[0]
        xs = [_shift_down(x, ksz - 1 - i) for i in range(ksz)]
        xc = sum(w_ref[i:i + 1, :] * xs[i] for i in range(ksz))
        ds = d_ref[...]
        if normed:
            s = _silu(xc)
            r = lax.rsqrt(jnp.sum(s * s, axis=-1, keepdims=True) + L2_EPS)
            y = s * r
            ds = r * (ds - y * jnp.sum(ds * y, axis=-1, keepdims=True))
        dxc = ds * _dsilu(xc)
        dp_ref[...] = sum(w_ref[i:i + 1, :] * _shift_up(dxc, ksz - 1 - i) for i in range(ksz))
        for i in range(ksz):
            dw_ref[i:i + 1, :] = jnp.sum(dxc * xs[i], axis=0, keepdims=True)

    shifted = pl.BlockSpec((T, LANES), lambda j: (0, first + j))
    return pl.pallas_call(
        body, name=name, grid=(width // LANES,),
        in_specs=[shifted, pl.BlockSpec((DN_CONV, LANES), lambda j: (0, first + j)),
                  pl.BlockSpec((T, LANES), lambda j: (0, j)), pl.BlockSpec(memory_space=pl.ANY)],
        out_specs=[shifted, pl.BlockSpec((DN_CONV, LANES), lambda j: (0, j))],
        out_shape=[jax.ShapeDtypeStruct(dp.shape, F32), jax.ShapeDtypeStruct((DN_CONV, width), F32)],
        input_output_aliases={3: 0},
        compiler_params=pltpu.CompilerParams(dimension_semantics=("parallel",), vmem_limit_bytes=VMEM_BIG),
    )(p, conv_w, d, dp)


def _dn_ab_bwd(p, alog_row, dtb_row, dgb, dp, *, name):
    T = p.shape[0]
    rows = min(DN_TB, T)
    H = DN_HEADS

    def body(p_ref, al_ref, dt_ref, dgb_ref, dp_in, dp_ref, dal_ref, ddt_ref):
        del dp_in

        @pl.when(pl.program_id(0) == 0)
        def _():
            dal_ref[...] = jnp.zeros_like(dal_ref)
            ddt_ref[...] = jnp.zeros_like(ddt_ref)

        blk = p_ref[...]
        is_a, is_b, a_arg, neg_exp, log_a, beta = _dn_ab_parts(blk, al_ref[...], dt_ref[...])
        d = dgb_ref[0]
        for hh in range(1, H):
            d = d + dgb_ref[hh]
        hi, mid, lo_ = _split3(jnp.where(is_a, d, 0.0))
        tri = _dn_chunk_tri(rows, upper=True)
        f = lambda q: jnp.dot(tri, q, preferred_element_type=F32)
        dlog_a = f(hi) + f(mid) + f(lo_)
        da_in = dlog_a * neg_exp * _sigmoid(a_arg)
        db_in = jnp.where(is_b, d, 0.0) * beta * (1.0 - beta)
        dp_ref[...] = jnp.where(is_a, da_in, 0.0) + db_in
        dal_ref[...] += jnp.sum(dlog_a * log_a, axis=0, keepdims=True)
        ddt_ref[...] += jnp.sum(jnp.where(is_a, da_in, 0.0), axis=0, keepdims=True)

    blk = pl.BlockSpec((rows, LANES), lambda i: (i, DN_AB_COL))
    vec = pl.BlockSpec((1, LANES), lambda i: (0, 0))
    return pl.pallas_call(
        body, name=name, grid=(T // rows,),
        in_specs=[blk, vec, vec, pl.BlockSpec((H, rows, LANES), lambda i: (0, i, 0)),
                  pl.BlockSpec(memory_space=pl.ANY)],
        out_specs=[blk, vec, vec],
        out_shape=[jax.ShapeDtypeStruct(dp.shape, F32), jax.ShapeDtypeStruct((1, LANES), F32),
                   jax.ShapeDtypeStruct((1, LANES), F32)],
        input_output_aliases={4: 0},
        compiler_params=pltpu.CompilerParams(dimension_semantics=("arbitrary",)),
    )(p, alog_row, dtb_row, dgb, dp)


def _dn_layer_fwd(x, ng, w_in, conv_w, a_log, dt_bias, o_gain, w_out, tag, send=()):
    alog_row, dtb_row = _dn_lane_rows(a_log, dt_bias)
    gain = o_gain.reshape(1, DN_DV)
    h = _rmsnorm_fwd(x, ng, name=f"{tag}_norm")
    p = _matmul(h, w_in, mode="nn", name=f"{tag}_inproj")
    qkv = _dn_prep_fwd(p, conv_w, name=f"{tag}_prep")
    gb = _dn_ab_fwd(p, alog_row, dtb_row, name=f"{tag}_ab")
    o, og, states, *landed = _dn_delta_fwd(qkv, gb, p, gain, name=f"{tag}_delta", send=send)
    x_new = _matmul(og, w_out, mode="nn", res=x, name=f"{tag}_outproj")
    return x_new, (h, p, qkv, gb, o, og, states), landed


def _dn_layer_bwd(dx, x, ng, w_in, conv_w, a_log, dt_bias, o_gain, w_out, saved, tag):
    h, p, qkv, gb, o, og, states = saved
    alog_row, dtb_row = _dn_lane_rows(a_log, dt_bias)
    gain = o_gain.reshape(1, DN_DV)
    d_wout = _matmul(og, dx, mode="tn", out_dtype=BF16, name=f"{tag}_dwout")
    dog = _matmul(dx, w_out, mode="nt", name=f"{tag}_dog")
    dq, dk, dv, dp, dgb, dgain = _dn_delta_bwd(qkv, gb, p, gain, o, states, dog, name=f"{tag}_deltabwd")
    n_qk = DN_QK_W // LANES
    dp, dconv_q = _dn_conv_bwd(p, conv_w, dq, dp, first=0, normed=True, name=f"{tag}_convbwd_q")
    dp, dconv_k = _dn_conv_bwd(p, conv_w, dk, dp, first=n_qk, normed=True, name=f"{tag}_convbwd_k")
    dp, dconv_v = _dn_conv_bwd(p, conv_w, dv, dp, first=2 * n_qk, normed=False, name=f"{tag}_convbwd_v")
    dconv = jnp.concatenate([dconv_q, dconv_k, dconv_v], axis=1)
    dp, dal, ddt = _dn_ab_bwd(p, alog_row, dtb_row, dgb, dp, name=f"{tag}_abbwd")
    d_win = _matmul(h, dp, mode="tn", name=f"{tag}_dwin")
    dh = _matmul(dp, w_in, mode="nt", name=f"{tag}_dh")
    dx_prev, dng = _rmsnorm_bwd(x, ng, dh, dx, name=f"{tag}_normbwd")
    return dx_prev, dng, d_win, dconv, dal[0, :DN_HEADS], ddt[0, :DN_HEADS], dgain[0], d_wout


def _sb_gains(g):
    return jnp.concatenate([g, g]).reshape(1, LANES)


def _sb_layer_fwd(x, ng, w_in, gq, gk, w_out, tag):
    h = _rmsnorm_fwd(x, ng, name=f"{tag}_norm")
    p3 = _matmul(h, w_in, mode="nn", b_parts=4, out_parts=4, name=f"{tag}_inproj")
    og, o, lsum = _sb_attn_fwd(p3, _sb_gains(gq), _sb_gains(gk), name=f"{tag}_attn")
    x_new = _matmul(og, w_out, mode="nn", res=x, name=f"{tag}_outproj")
    return x_new, (h, p3, og, o, lsum)


def _sb_layer_bwd(dx, x, ng, w_in, gq, gk, w_out, saved, tag):
    h, p3, og, o, lsum = saved
    d_wout = _matmul(og, dx, mode="tn", out_dtype=BF16, name=f"{tag}_dwout")
    dog = _matmul(dx, w_out, mode="nt", name=f"{tag}_dog")
    dp3, dgq, dgk = _sb_attn_bwd(p3, _sb_gains(gq), _sb_gains(gk), o, lsum, dog, name=f"{tag}_attnbwd")
    fold = lambda d: jnp.sum(d.reshape(-1, SB_DH), axis=0)
    d_win = _matmul(h, dp3, mode="tn", b_parts=4, out_parts=4, out_dtype=BF16, name=f"{tag}_dwin")
    dh = _matmul(dp3, w_in, mode="nt", a_parts=4, b_parts=4, name=f"{tag}_dh")
    dx_prev, dng = _rmsnorm_bwd(x, ng, dh, dx, name=f"{tag}_normbwd")
    return dx_prev, dng, d_win, fold(dgq), fold(dgk), d_wout


N_CHIPS = 4
HBM = pl.BlockSpec(memory_space=pl.ANY)


def _mesh_pos():
    return lax.axis_index("x"), lax.axis_index("y"), lax.axis_index("c")


def _other_chips(x, y):
    return [(1 - x, y), (x, 1 - y), (1 - x, 1 - y)]


def _chip_exchange(srcs, *, send_slot_is_dest, copy_own, name):
    n = len(srcs)

    def body(*refs):
        src_refs, out_refs = refs[:n], refs[n:2 * n]
        send_sems, recv_sems, local_sems = refs[2 * n:]
        x, y, c = _mesh_pos()
        me = 2 * x + y
        chips = _other_chips(x, y)
        local = []
        for a in range(n):
            if not copy_own[a]:
                continue
            own = src_refs[a].at[me] if send_slot_is_dest else src_refs[a]
            local.append(pltpu.make_async_copy(own, out_refs[a].at[me], local_sems.at[a]))
        for cp in local:
            cp.start()

        def copy(a, k, landing_slot):
            px, py = chips[k]
            src = src_refs[a].at[2 * px + py] if send_slot_is_dest else src_refs[a]
            return pltpu.make_async_remote_copy(
                src_ref=src, dst_ref=out_refs[a].at[landing_slot],
                send_sem=send_sems.at[a * 3 + k], recv_sem=recv_sems.at[a * 3 + k],
                device_id=(px, py, c), device_id_type=MESH)

        sends = [copy(a, k, me) for a in range(n) for k in range(3)]
        for cp in sends:
            cp.start()
        for a in range(n):
            for k in range(3):
                px, py = chips[k]
                copy(a, k, 2 * px + py).wait_recv()
        for cp in sends:
            cp.wait_send()
        for cp in local:
            cp.wait()

    outs = []
    for s in srcs:
        shape = s.shape if send_slot_is_dest else (N_CHIPS,) + s.shape
        outs.append(jax.ShapeDtypeStruct(shape, s.dtype))
    return pl.pallas_call(
        body, name=name, in_specs=[HBM] * n, out_specs=[HBM] * n, out_shape=outs,
        scratch_shapes=[pltpu.SemaphoreType.DMA((3 * n,)), pltpu.SemaphoreType.DMA((3 * n,)),
                        pltpu.SemaphoreType.DMA((n,))],
    )(*srcs)


def _sibling_exchange(srcs, *, name):
    n = len(srcs)

    def body(*refs):
        src_refs, out_refs = refs[:n], refs[n:2 * n]
        send_sems, recv_sems = refs[2 * n:]
        x, y, c = _mesh_pos()
        copies = [pltpu.make_async_remote_copy(
            src_ref=src_refs[a], dst_ref=out_refs[a], send_sem=send_sems.at[a], recv_sem=recv_sems.at[a],
            device_id=(x, y, 1 - c), device_id_type=MESH) for a in range(n)]
        for cp in copies:
            cp.start()
        for cp in copies:
            cp.wait()

    return pl.pallas_call(
        body, name=name, in_specs=[HBM] * n, out_specs=[HBM] * n,
        out_shape=[jax.ShapeDtypeStruct(s.shape, s.dtype) for s in srcs],
        scratch_shapes=[pltpu.SemaphoreType.DMA((n,)), pltpu.SemaphoreType.DMA((n,))],
    )(*srcs)


def _gather_halves(shards, small, *, name):
    n = len(shards)

    def body(*refs):
        s_refs, small_ref = refs[:n], refs[n]
        o_refs, osmall_ref = refs[n + 1:2 * n + 1], refs[2 * n + 1]
        send_sems, recv_sems, local_sems = refs[2 * n + 2:]
        x, y, c = _mesh_pos()
        me = 2 * x + y
        chips = _other_chips(x, y)
        local = [pltpu.make_async_copy(small_ref, osmall_ref.at[me], local_sems.at[0])]
        for cp in local:
            cp.start()

        def over_ici(a, k, slot):
            px, py = chips[k]
            return pltpu.make_async_remote_copy(
                src_ref=s_refs[a].at[c], dst_ref=o_refs[a].at[slot, c], send_sem=send_sems.at[3 * a + k],
                recv_sem=recv_sems.at[3 * a + k], device_id=(px, py, c), device_id_type=MESH)

        def small_copy(k, slot):
            px, py = chips[k]
            return pltpu.make_async_remote_copy(
                src_ref=small_ref, dst_ref=osmall_ref.at[slot], send_sem=send_sems.at[3 * n + k],
                recv_sem=recv_sems.at[3 * n + k], device_id=(px, py, c), device_id_type=MESH)

        def to_sibling(a, k, half):
            px, py = chips[k]
            blk = o_refs[a].at[2 * px + py, half]
            return pltpu.make_async_remote_copy(
                src_ref=blk, dst_ref=blk, send_sem=send_sems.at[3 * n + 3 + 3 * a + k],
                recv_sem=recv_sems.at[3 * n + 3 + 3 * a + k], device_id=(x, y, 1 - c), device_id_type=MESH)

        sends = [over_ici(a, k, me) for a in range(n) for k in range(3)] + [small_copy(k, me) for k in range(3)]
        for cp in sends:
            cp.start()
        passed = []
        for a in range(n):
            for k in range(3):
                px, py = chips[k]
                over_ici(a, k, 2 * px + py).wait_recv()
                passed.append(to_sibling(a, k, c))
                passed[-1].start()
        for k in range(3):
            px, py = chips[k]
            small_copy(k, 2 * px + py).wait_recv()
        for a in range(n):
            for k in range(3):
                to_sibling(a, k, 1 - c).wait_recv()
        for cp in sends + passed:
            cp.wait_send()
        for cp in local:
            cp.wait()

    nsem = 6 * n + 3
    return pl.pallas_call(
        body, name=name, in_specs=[HBM] * (n + 1), out_specs=[HBM] * (n + 1),
        out_shape=[jax.ShapeDtypeStruct((N_CHIPS,) + s.shape, s.dtype) for s in shards + [small]],
        scratch_shapes=[pltpu.SemaphoreType.DMA((nsem,)), pltpu.SemaphoreType.DMA((nsem,)),
                        pltpu.SemaphoreType.DMA((1,))],
    )(*shards, small)


def _forward_halves(landed, *, name):
    n = len(landed)

    def body(*refs):
        o_refs = refs[n:2 * n]
        send_sems, recv_sems = refs[2 * n:]
        x, y, c = _mesh_pos()
        chips = _other_chips(x, y)
        pairs = [(a, k) for a in range(n) for k in range(3)]

        def copy(a, k, half):
            px, py = chips[k]
            blk = o_refs[a].at[2 * px + py, half]
            return pltpu.make_async_remote_copy(
                src_ref=blk, dst_ref=blk, send_sem=send_sems.at[3 * a + k], recv_sem=recv_sems.at[3 * a + k],
                device_id=(x, y, 1 - c), device_id_type=MESH)

        sends = [copy(a, k, c) for a, k in pairs]
        for cp in sends:
            cp.start()
        for a, k in pairs:
            copy(a, k, 1 - c).wait_recv()
        for cp in sends:
            cp.wait_send()

    return pl.pallas_call(
        body, name=name, in_specs=[HBM] * n, out_specs=[HBM] * n,
        out_shape=[jax.ShapeDtypeStruct(a.shape, a.dtype) for a in landed],
        input_output_aliases={a: a for a in range(n)},
        scratch_shapes=[pltpu.SemaphoreType.DMA((3 * n,)), pltpu.SemaphoreType.DMA((3 * n,))],
    )(*landed)


def _swap_other_half(g_list, *, name):
    n = len(g_list)

    def body(*refs):
        g_refs, o_refs = refs[:n], refs[n:2 * n]
        send_sems, recv_sems = refs[2 * n:]
        x, y, c = _mesh_pos()
        copies = [pltpu.make_async_remote_copy(
            src_ref=g_refs[a].at[:, 1 - c], dst_ref=o_refs[a], send_sem=send_sems.at[a], recv_sem=recv_sems.at[a],
            device_id=(x, y, 1 - c), device_id_type=MESH) for a in range(n)]
        for cp in copies:
            cp.start()
        for cp in copies:
            cp.wait()

    return pl.pallas_call(
        body, name=name, in_specs=[HBM] * n, out_specs=[HBM] * n,
        out_shape=[jax.ShapeDtypeStruct((g.shape[0],) + g.shape[2:], g.dtype) for g in g_list],
        scratch_shapes=[pltpu.SemaphoreType.DMA((n,)), pltpu.SemaphoreType.DMA((n,))],
    )(*g_list)


def _row_tile(r):
    return _pick(r, (512, 256, 128, 64, 32, 16, 8))


def _add_my_half(g4, sib4, core, *, name):
    n, _, r, C = g4.shape
    tr = _row_tile(r)

    def body(core_ref, g_ref, s_ref, o_ref):
        del core_ref
        o_ref[...] = (g_ref[...].astype(F32) + s_ref[...].astype(F32)).astype(o_ref.dtype)

    return pl.pallas_call(
        body, name=name,
        grid_spec=pltpu.PrefetchScalarGridSpec(
            num_scalar_prefetch=1, grid=(n, r // tr),
            in_specs=[pl.BlockSpec((None, None, tr, C), lambda j, i, core_ref: (j, core_ref[0], i, 0)),
                      pl.BlockSpec((None, tr, C), lambda j, i, core_ref: (j, i, 0))],
            out_specs=pl.BlockSpec((None, tr, C), lambda j, i, core_ref: (j, i, 0))),
        out_shape=jax.ShapeDtypeStruct((n, r, C), g4.dtype),
        compiler_params=pltpu.CompilerParams(dimension_semantics=("parallel", "parallel")),
    )(core, g4, sib4)


def _scatter_to_chips(p_list, *, name):
    n = len(p_list)

    def body(*refs):
        p_refs, o_refs = refs[:n], refs[n:2 * n]
        send_sems, recv_sems, local_sems = refs[2 * n:]
        x, y, c = _mesh_pos()
        me = 2 * x + y
        chips = _other_chips(x, y)
        own = [pltpu.make_async_copy(p_refs[a].at[me], o_refs[a].at[me], local_sems.at[a]) for a in range(n)]
        for cp in own:
            cp.start()

        def copy(a, k, landing_slot):
            px, py = chips[k]
            return pltpu.make_async_remote_copy(
                src_ref=p_refs[a].at[2 * px + py], dst_ref=o_refs[a].at[landing_slot],
                send_sem=send_sems.at[3 * a + k], recv_sem=recv_sems.at[3 * a + k], device_id=(px, py, c),
                device_id_type=MESH)

        sends = [copy(a, k, me) for a in range(n) for k in range(3)]
        for cp in sends:
            cp.start()
        for a in range(n):
            for k in range(3):
                px, py = chips[k]
                copy(a, k, 2 * px + py).wait_recv()
        for cp in sends:
            cp.wait_send()
        for cp in own:
            cp.wait()

    return pl.pallas_call(
        body, name=name, in_specs=[HBM] * n, out_specs=[HBM] * n,
        out_shape=[jax.ShapeDtypeStruct(p.shape, p.dtype) for p in p_list],
        scratch_shapes=[pltpu.SemaphoreType.DMA((3 * n,)), pltpu.SemaphoreType.DMA((3 * n,)),
                        pltpu.SemaphoreType.DMA((n,))],
    )(*p_list)


def _sum_chips(r4, *, name):
    _, r, C = r4.shape
    tr = _row_tile(r)

    def body(r_ref, o_ref):
        f = lambda j: r_ref[j].astype(F32)
        o_ref[...] = ((f(0) + f(1)) + f(2)) + f(3)

    return pl.pallas_call(
        body, name=name, grid=(r // tr,), in_specs=[pl.BlockSpec((N_CHIPS, tr, C), lambda i: (0, i, 0))],
        out_specs=pl.BlockSpec((tr, C), lambda i: (i, 0)), out_shape=jax.ShapeDtypeStruct((r, C), F32),
        compiler_params=pltpu.CompilerParams(dimension_semantics=("parallel",)),
    )(r4)


def _adamw_halves(w, mine, theirs, m, v, core, *, name):
    shape = w.shape
    r, C = mine.shape
    tr = _pick(r, (128, 64, 32, 16, 8))
    per = r // tr
    view = lambda a: a.reshape(2 * r, C)

    def body(core_ref, w_ref, gm_ref, gt_ref, m_ref, v_ref, g_ref, d_ref, nm_ref, nv_ref):
        gv = jnp.where(pl.program_id(0) == core_ref[0], gm_ref[...], gt_ref[...])
        g_ref[...] = gv
        d_ref[...], nm_ref[...], nv_ref[...] = _adamw_math(w_ref[...], gv, m_ref[...], v_ref[...])

    half = pl.BlockSpec((tr, C), lambda h, i, core_ref: (h * per + i, 0))
    row = pl.BlockSpec((tr, C), lambda h, i, core_ref: (i, 0))
    out = jax.ShapeDtypeStruct((2 * r, C), F32)
    res = pl.pallas_call(
        body, name=name,
        grid_spec=pltpu.PrefetchScalarGridSpec(
            num_scalar_prefetch=1, grid=(2, r // tr), in_specs=[half, row, row, half, half], out_specs=[half] * 4),
        out_shape=[out] * 4,
        compiler_params=pltpu.CompilerParams(dimension_semantics=("parallel", "parallel")),
    )(core, view(w), mine, theirs, view(m), view(v))
    return tuple(a.reshape(shape) for a in res)


def _sum_small(recv4, *, name):
    _, R, C = recv4.shape

    def body(r_ref, o_ref):
        o_ref[...] = ((r_ref[0] + r_ref[1]) + r_ref[2]) + r_ref[3]

    return pl.pallas_call(body, name=name, out_shape=jax.ShapeDtypeStruct((R, C), F32))(recv4)


def _add(a, b, *, name):
    R, C = a.shape
    tr = _pick(R, (512, 256, 128, 64, 32, 16, 8))
    blk = pl.BlockSpec((tr, C), lambda i: (i, 0))

    def body(a_ref, b_ref, o_ref):
        o_ref[...] = a_ref[...] + b_ref[...]

    return pl.pallas_call(body, name=name, grid=(R // tr,), in_specs=[blk, blk], out_specs=blk,
                          out_shape=jax.ShapeDtypeStruct((R, C), F32),
                          compiler_params=pltpu.CompilerParams(dimension_semantics=("parallel",)))(a, b)


def _adamw_math(w, g, m, v):
    nm = ADAM_B1 * m + (1.0 - ADAM_B1) * g
    nv = ADAM_B2 * v + (1.0 - ADAM_B2) * (g * g)
    m_hat = nm / (1.0 - ADAM_B1 ** ADAM_STEP)
    v_hat = nv / (1.0 - ADAM_B2 ** ADAM_STEP)
    return -ADAM_LR * (m_hat / (jnp.sqrt(v_hat) + ADAM_EPS) + ADAM_WD * w), nm, nv


def _adamw(w, g, m, v, *, name):
    shape = w.shape
    C = shape[-1]
    R = w.size // C
    two = lambda a: a.reshape(R, C)
    tr = _pick(R, (256, 128, 64, 32, 16, 8)) if R % 8 == 0 and R > 8 else R
    blk = pl.BlockSpec((tr, C), lambda i: (i, 0))

    def body(w_ref, g_ref, m_ref, v_ref, d_ref, nm_ref, nv_ref):
        d_ref[...], nm_ref[...], nv_ref[...] = _adamw_math(w_ref[...], g_ref[...], m_ref[...], v_ref[...])

    out = jax.ShapeDtypeStruct((R, C), F32)
    d, nm, nv = pl.pallas_call(
        body, name=name, grid=(R // tr,), in_specs=[blk] * 4, out_specs=[blk] * 3, out_shape=[out] * 3,
        compiler_params=pltpu.CompilerParams(dimension_semantics=("parallel",)),
    )(two(w), two(g), two(m), two(v))
    return d.reshape(shape), nm.reshape(shape), nv.reshape(shape)


BIG = (("dn_w_in", (2, 1024, 1540), 2), ("dn_w_out", (2, 512, 1024), 1), ("sb_w_in", (1, 1024, 1024), 2),
       ("sb_w_out", (1, 256, 1024), 1), ("sc_w_in", (1, 1024, 2048), 2), ("sc_w_out", (1, 512, 1024), 1))
SMALL = (("dn_conv_w", (2, 4, 1024), 2), ("dn_o_norm_g", (2, 64), 1), ("sc_conv_w", (1, 3, 512), 2))
REPL = (("norm_g", (4, 1024)), ("dn_a_log", (2, 8)), ("dn_dt_bias", (2, 8)), ("sb_q_norm_g", (1, 64)),
        ("sb_k_norm_g", (1, 64)))


def _halves(shard):
    return shard.reshape(2, -1, shard.shape[-1])


def _pack(arrays, cols, lead=()):
    flat = jnp.concatenate([a.reshape(lead + (-1,)) for a in arrays], axis=-1)
    n = flat.shape[-1]
    rows = -(-n // cols)
    unit = 512 if rows > 512 else 8
    rows = -(-rows // unit) * unit
    flat = jnp.pad(flat, [(0, 0)] * len(lead) + [(0, rows * cols - n)])
    return flat.reshape(lead + (rows, cols))


def _unpack(buf, table, lead=()):
    flat = buf.reshape(lead + (-1,))
    out, off = {}, 0
    for entry in table:
        name, shape = entry[0], entry[1]
        n = math.prod(shape)
        out[name] = flat[..., off:off + n].reshape(lead + shape)
        off += n
    return out


def _join(shards, axis):
    return jnp.concatenate([shards[j] for j in range(N_CHIPS)], axis=axis)


def _split(full, axis):
    return jnp.stack(jnp.split(full, N_CHIPS, axis=axis), axis=0)


def kernel(x, norm_g, dn_w_in, dn_conv_w, dn_a_log, dn_dt_bias, dn_o_norm_g, dn_w_out, sb_w_in, sb_q_norm_g, sb_k_norm_g, sb_w_out, sc_w_in, sc_conv_w, sc_w_out, loss_target, m_norm_g, m_dn_w_in, m_dn_conv_w, m_dn_a_log, m_dn_dt_bias, m_dn_o_norm_g, m_dn_w_out, m_sb_w_in, m_sb_q_norm_g, m_sb_k_norm_g, m_sb_w_out, m_sc_w_in, m_sc_conv_w, m_sc_w_out, v_norm_g, v_dn_w_in, v_dn_conv_w, v_dn_a_log, v_dn_dt_bias, v_dn_o_norm_g, v_dn_w_out, v_sb_w_in, v_sb_q_norm_g, v_sb_k_norm_g, v_sb_w_out, v_sc_w_in, v_sc_conv_w, v_sc_w_out):
    weights = dict(norm_g=norm_g, dn_w_in=dn_w_in, dn_conv_w=dn_conv_w, dn_a_log=dn_a_log, dn_dt_bias=dn_dt_bias,
                   dn_o_norm_g=dn_o_norm_g, dn_w_out=dn_w_out, sb_w_in=sb_w_in, sb_q_norm_g=sb_q_norm_g,
                   sb_k_norm_g=sb_k_norm_g, sb_w_out=sb_w_out, sc_w_in=sc_w_in, sc_conv_w=sc_conv_w, sc_w_out=sc_w_out)
    m_in = dict(norm_g=m_norm_g, dn_w_in=m_dn_w_in, dn_conv_w=m_dn_conv_w, dn_a_log=m_dn_a_log,
                dn_dt_bias=m_dn_dt_bias, dn_o_norm_g=m_dn_o_norm_g, dn_w_out=m_dn_w_out, sb_w_in=m_sb_w_in,
                sb_q_norm_g=m_sb_q_norm_g, sb_k_norm_g=m_sb_k_norm_g, sb_w_out=m_sb_w_out, sc_w_in=m_sc_w_in,
                sc_conv_w=m_sc_conv_w, sc_w_out=m_sc_w_out)
    v_in = dict(norm_g=v_norm_g, dn_w_in=v_dn_w_in, dn_conv_w=v_dn_conv_w, dn_a_log=v_dn_a_log,
                dn_dt_bias=v_dn_dt_bias, dn_o_norm_g=v_dn_o_norm_g, dn_w_out=v_dn_w_out, sb_w_in=v_sb_w_in,
                sb_q_norm_g=v_sb_q_norm_g, sb_k_norm_g=v_sb_k_norm_g, sb_w_out=v_sb_w_out, sc_w_in=v_sc_w_in,
                sc_conv_w=v_sc_conv_w, sc_w_out=v_sc_w_out)
    order = list(weights)
    xi, yi, ci = _mesh_pos()

    small = _pack([weights[n] for n, _, _ in SMALL], LANES)
    later = [("dn_w_in", 1), ("dn_w_out", 1), ("sb_w_in", 0), ("sb_w_out", 0), ("sc_w_in", 0), ("sc_w_out", 0)]
    piece = lambda n, l: _halves(weights[n][l].astype(BF16)[None])
    own_first = [piece("dn_w_in", 0), piece("dn_w_out", 0)]
    own_later = [piece(n, l) for n, l in later]
    me = 2 * xi + yi
    whole = lambda g4, own: lax.dynamic_update_index_in_dim(g4, own, me, 0)
    flat = lambda g4: g4.reshape(N_CHIPS, -1, g4.shape[-1])
    rows_of = lambda w4: w4.reshape(-1, w4.shape[-1])
    dn_in = lambda w4: jnp.pad(_join(w4, 1), ((0, 0), (0, DN_IN_PAD - DN_IN)))
    w_in0, w_out0, small4 = _gather_halves(own_first, small, name="gather_first")
    full = {n: _join(a, ax) for (n, _, ax), a in zip(SMALL, _unpack(small4, SMALL, (N_CHIPS,)).values())}

    def dn_args(j, w_in4, w_out4):
        return (dn_in(flat(w_in4)), full["dn_conv_w"][j], dn_a_log[j], dn_dt_bias[j], full["dn_o_norm_g"][j],
                rows_of(w_out4))

    x0 = x[0]
    dn0 = dn_args(0, whole(w_in0, own_first[0]), whole(w_out0, own_first[1]))
    x1, s0, landed = _dn_layer_fwd(x0, norm_g[0], *dn0, "l0", send=own_later)
    landed = _forward_halves(landed, name="forward_halves")
    w_in3, w_out3, sb_in, sb_out, sc_in, sc_out = [whole(g4, own) for g4, own in zip(landed, own_later)]
    dn1 = dn_args(1, w_in3, w_out3)
    sb_args = (flat(sb_in), sb_q_norm_g[0], sb_k_norm_g[0], rows_of(sb_out))
    sc_args = (flat(sc_in), full["sc_conv_w"][0], rows_of(sc_out))
    x2, s1 = _sb_layer_fwd(x1, norm_g[1], *sb_args, "l1")
    x3, s2 = _sc_layer_fwd(x2, norm_g[2], *sc_args, "l2")
    x4, s3, _ = _dn_layer_fwd(x3, norm_g[3], *dn1, "l3")
    dy, loss_local = _loss_head(x4, loss_target[0], name="loss_head")
    loss = lax.psum(loss_local[0, 0], ("x", "y", "c"))

    dx3, dng3, dwin3, dconv3, dal3, ddt3, dgain3, dwout3 = _dn_layer_bwd(dy, x3, norm_g[3], *dn1, s3, "l3")
    dx2, dng2, dwin2, dconv2, dwout2 = _sc_layer_bwd(dx3, x2, norm_g[2], *sc_args, s2, "l2")
    dx1, dng1, dwin1, dgq, dgk, dwout1 = _sb_layer_bwd(dx2, x1, norm_g[1], *sb_args, s1, "l1")
    dx0, dng0, dwin0, dconv0, dal0, ddt0, dgain0, dwout0 = _dn_layer_bwd(dx1, x0, norm_g[0], *dn0, s0, "l0")

    grads = dict(
        norm_g=jnp.concatenate([dng0, dng1, dng2, dng3], axis=0), dn_conv_w=jnp.stack([dconv0, dconv3]),
        dn_a_log=jnp.stack([dal0, dal3]), dn_dt_bias=jnp.stack([ddt0, ddt3]),
        dn_o_norm_g=jnp.stack([dgain0, dgain3]), sb_q_norm_g=dgq[None], sb_k_norm_g=dgk[None],
        sc_conv_w=dconv2[None])
    by_cols = lambda dw: _split(dw[:, :DN_IN].astype(BF16), 1)
    by_rows = lambda dw: dw.reshape(N_CHIPS, -1, dw.shape[-1])
    cut2 = lambda g4: g4.reshape(N_CHIPS, 2, -1, g4.shape[-1])
    gbig = [jnp.stack([by_cols(dwin0), by_cols(dwin3)], axis=1), jnp.stack([by_rows(dwout0), by_rows(dwout3)], axis=1),
            cut2(dwin1), cut2(by_rows(dwout1)), cut2(dwin2), cut2(by_rows(dwout2))]

    core = ci.astype(jnp.int32).reshape(1)
    sib = _swap_other_half(gbig, name="swap_halves")
    part = [_add_my_half(g, s, core, name=f"sum_cores_{n}") for (n, _, _), g, s in zip(BIG, gbig, sib)]
    landed = _scatter_to_chips(part, name="scatter_grads")
    mine = [_sum_chips(r, name=f"sum_chips_{n}") for (n, _, _), r in zip(BIG, landed)]
    theirs = _sibling_exchange(mine, name="swap_results")
    upd = {n: _adamw_halves(weights[n], a, b, m_in[n], v_in[n], core, name=f"adamw_{n}")
           for (n, _, _), a, b in zip(BIG, mine, theirs)}
    g_out = {n: upd[n][0] for n, _, _ in BIG}
    repl = [jnp.broadcast_to(grads[n][None], (N_CHIPS,) + s) for n, s in REPL]
    gsmall = _pack([_split(grads[n], ax) for n, _, ax in SMALL] + repl, LANES, (N_CHIPS,))
    rsmall, = _chip_exchange([gsmall], send_slot_is_dest=True, copy_own=(True,), name="scatter_small")
    psmall = _sum_small(rsmall, name="sum_chips_small")
    qsmall, = _sibling_exchange([psmall], name="swap_cores_small")
    tsmall = _add(psmall, qsmall, name="sum_cores_small")
    g_out.update(_unpack(tsmall, SMALL + REPL))

    for n in order:
        if n not in upd:
            upd[n] = (g_out[n],) + _adamw(weights[n], g_out[n], m_in[n], v_in[n], name=f"adamw_{n}")
    return (loss, dx0[None], *[upd[n][0] for n in order], *[upd[n][1] for n in order],
            *[upd[n][2] for n in order], *[upd[n][3] for n in order])
```

```python
import math

import jax
import jax.numpy as jnp
from jax import lax
from jax.experimental import pallas as pl
from jax.experimental.pallas import tpu as pltpu

F32 = jnp.float32
BF16 = jnp.bfloat16
MESH = pl.DeviceIdType.MESH

RMS_EPS = 1e-6
L2_EPS = 1e-6
LANES = 128
VMEM_BIG = 60 * 1024 * 1024
MM_VMEM = 36 * 1024 * 1024

DN_HEADS, DN_DK, DN_DV, DN_CHUNK, DN_CONV = 8, 128, 256, 64, 4
DN_QK_W = DN_HEADS * DN_DK
DN_V_W = DN_HEADS * DN_DV
DN_CONV_W = 2 * DN_QK_W + DN_V_W
DN_IN = DN_CONV_W + DN_V_W + 2 * DN_HEADS
DN_IN_PAD = DN_CONV_W + DN_V_W + LANES
SB_DH = 64
SC_CONV = 3

ADAM_LR, ADAM_B1, ADAM_B2, ADAM_EPS, ADAM_WD, ADAM_STEP = 0.001, 0.9, 0.999, 1e-08, 0.01, 10


def _pick(n, cands):
    for c in cands:
        if n % c == 0:
            return c
    raise ValueError(f"no tile for {n} in {cands}")


def _bf(x):
    return x.astype(BF16)


def _dot(a, b):
    return jnp.dot(_bf(a), _bf(b), preferred_element_type=F32)


def _dot_nt(a, b):
    return lax.dot_general(_bf(a), _bf(b), (((1,), (1,)), ((), ())), preferred_element_type=F32)


def _dot_tn(a, b):
    return lax.dot_general(_bf(a), _bf(b), (((0,), (0,)), ((), ())), preferred_element_type=F32)


def _split3(a):
    hi = _bf(a)
    r = a - hi.astype(F32)
    mid = _bf(r)
    lo = _bf(r - mid.astype(F32))
    return hi, mid, lo


def _sigmoid(x):
    return 1.0 / (1.0 + jnp.exp(-x))


def _silu(x):
    return x * _sigmoid(x)


def _dsilu(x):
    s = _sigmoid(x)
    return s * (1.0 + x * (1.0 - s))


def _softplus(x):
    return jnp.maximum(x, 0.0) + jnp.log(1.0 + jnp.exp(-jnp.abs(x)))


def _shift_down(z, k):
    if k == 0:
        return z
    row = lax.broadcasted_iota(jnp.int32, z.shape, 0)
    return jnp.where(row >= k, pltpu.roll(z, k, 0), 0.0)


def _shift_up(z, k):
    if k == 0:
        return z
    n = z.shape[0]
    row = lax.broadcasted_iota(jnp.int32, z.shape, 0)
    return jnp.where(row < n - k, pltpu.roll(z, n - k, 0), 0.0)


def _matmul(a, b, *, mode, name, res=None, a_parts=1, b_parts=1, out_parts=1, out_dtype=F32):
    def dims2(x, parts):
        if parts == 1:
            return x.shape
        assert x.shape[0] == parts
        return (x.shape[1], x.shape[2] * parts)

    ash, bsh = dims2(a, a_parts), dims2(b, b_parts)
    if mode == "nn":
        (M, K), (K2, N) = ash, bsh
        dn = (((1,), (0,)), ((), ()))
    elif mode == "nt":
        (M, K), (N, K2) = ash, bsh
        dn = (((1,), (1,)), ((), ()))
    else:
        (K, M), (K2, N) = ash, bsh
        dn = (((0,), (0,)), ((), ()))
    assert K == K2, (ash, bsh, mode)
    tm = _pick(M, (512, 256, 128, 64, 32, 16, 8))
    n_unit = N // max(out_parts, b_parts if mode != "nt" else 1)
    k_unit = K // max(a_parts if mode != "tn" else 1, b_parts if mode == "nt" else 1)
    tn, tk = min(
        ((n, k) for n in (2048, 1792, 1024, 896, 768, 512, 384, 256, 128) if n_unit % n == 0
         for k in (2048, 1792, 1024, 896, 512, 256, 128) if k_unit % k == 0
         if 2 * (tm * k * a.dtype.itemsize + k * n * b.dtype.itemsize + 2 * tm * n * 4) + tm * n * 4 <= MM_VMEM),
        key=lambda nk_: (-nk_[0] * nk_[1], -nk_[1]))
    nk = K // tk
    grid = (M // tm, N // tn, nk)

    def spec(parts, rows_are, cols_are, tr, tc, width):
        per = width // parts // tc
        if parts == 1:
            return pl.BlockSpec((tr, tc), lambda i, j, k: ((i, j, k)[rows_are], (i, j, k)[cols_are]))
        return pl.BlockSpec((None, tr, tc), lambda i, j, k: ((i, j, k)[cols_are] // per, (i, j, k)[rows_are],
                                                             (i, j, k)[cols_are] % per))

    if mode == "nn":
        a_spec = spec(a_parts, 0, 2, tm, tk, K)
        b_spec = spec(b_parts, 2, 1, tk, tn, N)
    elif mode == "nt":
        a_spec = spec(a_parts, 0, 2, tm, tk, K)
        b_spec = spec(b_parts, 1, 2, tn, tk, K)
    else:
        a_spec = spec(a_parts, 2, 0, tk, tm, M)
        b_spec = spec(b_parts, 2, 1, tk, tn, N)
    o_spec = spec(out_parts, 0, 1, tm, tn, N)
    in_specs = [a_spec, b_spec]
    operands = [a, b]
    if res is not None:
        in_specs.append(pl.BlockSpec((tm, tn), lambda i, j, k: (i, j)))
        operands.append(res)

    def finish(refs, r):
        if res is not None:
            r = refs[2][...] + r
        refs[-2 if nk > 1 else -1][...] = r.astype(out_dtype)

    def body(*refs):
        part = lax.dot_general(_bf(refs[0][...]), _bf(refs[1][...]), dn, preferred_element_type=F32)
        if nk == 1:
            finish(refs, part)
            return
        acc_ref = refs[-1]
        k = pl.program_id(2)

        @pl.when(k == 0)
        def _():
            acc_ref[...] = part

        @pl.when(jnp.logical_and(k > 0, k < nk - 1))
        def _():
            acc_ref[...] += part

        @pl.when(k == nk - 1)
        def _():
            finish(refs, acc_ref[...] + part)

    out_shape = (M, N) if out_parts == 1 else (out_parts, M, N // out_parts)
    return pl.pallas_call(
        body, name=name, grid=grid, in_specs=in_specs, out_specs=o_spec,
        out_shape=jax.ShapeDtypeStruct(out_shape, out_dtype),
        scratch_shapes=[pltpu.VMEM((tm, tn), F32)] if nk > 1 else [],
        compiler_params=pltpu.CompilerParams(dimension_semantics=("parallel", "parallel", "arbitrary"),
                                             vmem_limit_bytes=VMEM_BIG),
    )(*operands)


def _rmsnorm_fwd(x, g, *, name):
    T, D = x.shape
    tm = _pick(T, (512, 256, 128, 64, 32, 16))

    def body(x_ref, g_ref, h_ref):
        xv = x_ref[...]
        r = lax.rsqrt(jnp.mean(xv * xv, axis=-1, keepdims=True) + RMS_EPS)
        h_ref[...] = ((xv * r) * g_ref[...]).astype(BF16)

    return pl.pallas_call(
        body, name=name, grid=(T // tm,),
        in_specs=[pl.BlockSpec((tm, D), lambda i: (i, 0)), pl.BlockSpec((1, D), lambda i: (0, 0))],
        out_specs=pl.BlockSpec((tm, D), lambda i: (i, 0)),
        out_shape=jax.ShapeDtypeStruct((T, D), BF16),
    )(x, g.reshape(1, D))


def _rmsnorm_bwd(x, g, dh, dx_in, *, name):
    T, D = x.shape
    tm = _pick(T, (512, 256, 128, 64, 32, 16))

    def body(x_ref, g_ref, dh_ref, dxin_ref, dx_ref, dg_ref):
        @pl.when(pl.program_id(0) == 0)
        def _():
            dg_ref[...] = jnp.zeros_like(dg_ref)

        xv = x_ref[...]
        r = lax.rsqrt(jnp.mean(xv * xv, axis=-1, keepdims=True) + RMS_EPS)
        xh = xv * r
        dh_v = dh_ref[...]
        dxh = dh_v * g_ref[...]
        dx_ref[...] = dxin_ref[...] + r * (dxh - xh * jnp.mean(dxh * xh, axis=-1, keepdims=True))
        dg_ref[...] += jnp.sum(dh_v * xh, axis=0, keepdims=True)

    row = pl.BlockSpec((tm, D), lambda i: (i, 0))
    vec = pl.BlockSpec((1, D), lambda i: (0, 0))
    return pl.pallas_call(
        body, name=name, grid=(T // tm,),
        in_specs=[row, vec, row, row], out_specs=[row, vec],
        out_shape=[jax.ShapeDtypeStruct((T, D), F32), jax.ShapeDtypeStruct((1, D), F32)],
        compiler_params=pltpu.CompilerParams(dimension_semantics=("arbitrary",)),
    )(x, g.reshape(1, D), dh, dx_in)


def _loss_head(y, target, *, name):
    T, D = y.shape
    tm = _pick(T, (512, 256, 128, 64, 32, 16))

    def body(y_ref, t_ref, dy_ref, l_ref):
        @pl.when(pl.program_id(0) == 0)
        def _():
            l_ref[...] = jnp.zeros_like(l_ref)

        err = y_ref[...] - t_ref[...]
        dy_ref[...] = err * (1.0 / D)
        l_ref[...] += 0.5 * jnp.sum(jnp.mean(err * err, axis=-1, keepdims=True), axis=0, keepdims=True)

    row = pl.BlockSpec((tm, D), lambda i: (i, 0))
    return pl.pallas_call(
        body, name=name, grid=(T // tm,),
        in_specs=[row, row], out_specs=[row, pl.BlockSpec((1, 1), lambda i: (0, 0))],
        out_shape=[jax.ShapeDtypeStruct((T, D), F32), jax.ShapeDtypeStruct((1, 1), F32)],
        compiler_params=pltpu.CompilerParams(dimension_semantics=("arbitrary",)),
    )(y, target)


def _sc_mid_fwd(p3, conv_w, *, name):
    _, T, W = p3.shape
    K = conv_w.shape[0]
    cw = LANES

    def body(p_ref, w_ref, o_ref):
        z = p_ref[1] * p_ref[2]
        cv = sum(w_ref[i:i + 1, :] * _shift_down(z, K - 1 - i) for i in range(K))
        o_ref[...] = ((p_ref[0] * cv) * _silu(p_ref[3])).astype(BF16)

    return pl.pallas_call(
        body, name=name, grid=(W // cw,),
        in_specs=[pl.BlockSpec((4, T, cw), lambda j: (0, 0, j)), pl.BlockSpec((K, cw), lambda j: (0, j))],
        out_specs=pl.BlockSpec((T, cw), lambda j: (0, j)),
        out_shape=jax.ShapeDtypeStruct((T, W), BF16),
        compiler_params=pltpu.CompilerParams(dimension_semantics=("parallel",), vmem_limit_bytes=VMEM_BIG),
    )(p3, conv_w)


def _sc_mid_bwd(p3, conv_w, do, *, name):
    _, T, W = p3.shape
    K = conv_w.shape[0]
    cw = LANES

    def body(p_ref, w_ref, do_ref, dp_ref, dw_ref):
        b, c, u, gate = p_ref[0], p_ref[1], p_ref[2], p_ref[3]
        z = c * u
        zs = [_shift_down(z, K - 1 - i) for i in range(K)]
        cv = sum(w_ref[i:i + 1, :] * zs[i] for i in range(K))
        y = b * cv
        dov = do_ref[...]
        dy = dov * _silu(gate)
        dp_ref[3] = dov * y * _dsilu(gate)
        dp_ref[0] = dy * cv
        dcv = dy * b
        dz = sum(w_ref[i:i + 1, :] * _shift_up(dcv, K - 1 - i) for i in range(K))
        dp_ref[1] = dz * u
        dp_ref[2] = dz * c
        for i in range(K):
            dw_ref[i:i + 1, :] = jnp.sum(dcv * zs[i], axis=0, keepdims=True)

    return pl.pallas_call(
        body, name=name, grid=(W // cw,),
        in_specs=[pl.BlockSpec((4, T, cw), lambda j: (0, 0, j)), pl.BlockSpec((K, cw), lambda j: (0, j)),
                  pl.BlockSpec((T, cw), lambda j: (0, j))],
        out_specs=[pl.BlockSpec((4, T, cw), lambda j: (0, 0, j)), pl.BlockSpec((K, cw), lambda j: (0, j))],
        out_shape=[jax.ShapeDtypeStruct((4, T, W), F32), jax.ShapeDtypeStruct((K, W), F32)],
        compiler_params=pltpu.CompilerParams(dimension_semantics=("parallel",), vmem_limit_bytes=VMEM_BIG),
    )(p3, conv_w, do)


def _sc_layer_fwd(x, ng, w_in, conv_w, w_out, tag):
    h = _rmsnorm_fwd(x, ng, name=f"{tag}_norm")
    p3 = _matmul(h, w_in, mode="nn", b_parts=4, out_parts=4, name=f"{tag}_inproj")
    og = _sc_mid_fwd(p3, conv_w, name=f"{tag}_mid")
    x_new = _matmul(og, w_out, mode="nn", res=x, name=f"{tag}_outproj")
    return x_new, (h, p3, og)


def _sc_layer_bwd(dx, x, ng, w_in, conv_w, w_out, saved, tag):
    h, p3, og = saved
    d_wout = _matmul(og, dx, mode="tn", out_dtype=BF16, name=f"{tag}_dwout")
    dog = _matmul(dx, w_out, mode="nt", name=f"{tag}_dog")
    dp3, dconv = _sc_mid_bwd(p3, conv_w, dog, name=f"{tag}_midbwd")
    d_win = _matmul(h, dp3, mode="tn", b_parts=4, out_parts=4, out_dtype=BF16, name=f"{tag}_dwin")
    dh = _matmul(dp3, w_in, mode="nt", a_parts=4, b_parts=4, name=f"{tag}_dh")
    dx_prev, dng = _rmsnorm_bwd(x, ng, dh, dx, name=f"{tag}_normbwd")
    return dx_prev, dng, d_win, dconv, d_wout


SB_BQ = 256
SB_BK = 256
SB_ROWS = 512


def _sb_half_mask():
    return lax.broadcasted_iota(jnp.int32, (1, LANES), 1) < SB_DH


def _sb_headnorm(x, g, lo):
    x2 = x * x
    s_lo = jnp.sum(jnp.where(lo, x2, 0.0), axis=-1, keepdims=True)
    s_hi = jnp.sum(jnp.where(lo, 0.0, x2), axis=-1, keepdims=True)
    r = lax.rsqrt(jnp.where(lo, s_lo, s_hi) * (1.0 / SB_DH) + RMS_EPS)
    xh = x * r
    return xh * g, xh, r


def _dot_x2_l(a_l, b_exact_bf16):
    his = [_bf(a) for a in a_l]
    mids = [_bf(a - h.astype(F32)) for a, h in zip(a_l, his)]
    f = lambda p: jnp.dot(p, b_exact_bf16, preferred_element_type=F32)
    return [x + y for x, y in zip([f(h) for h in his], [f(m) for m in mids])]


def _sb_stack(xb, lo):
    zero = jnp.zeros_like(xb)
    return jnp.concatenate([jnp.where(lo, xb, zero), jnp.where(lo, zero, xb)], axis=0)


def _sb_rel(bq, bk):
    row = lax.broadcasted_iota(jnp.int32, (2 * bq, bk), 0)
    col = lax.broadcasted_iota(jnp.int32, (2 * bq, bk), 1)
    return col - jnp.where(row >= bq, row - bq, row)


def _sb_tile(qm, kb, valid):
    z = lax.dot_general(qm, kb, (((1,), (1,)), ((), ())), preferred_element_type=F32)
    sp = _softplus(z)
    return z - sp, (-sp if valid is None else jnp.where(valid, -sp, 0.0))


def _sb_attn_fwd(p3, gq2, gk2, *, name):
    _, T, W = p3.shape
    bq, bk = min(SB_BQ, T), min(SB_BK, T)
    rows = min(SB_ROWS, T)
    scale = SB_DH ** -0.5

    def body(p_ref, gq_ref, gk_ref, og_ref, o_ref, ls_ref, qn_ref, kn_ref, v_ref):
        lo = _sb_half_mask()

        def prologue(i, c):
            r0 = pl.multiple_of(i * rows, rows)
            sl = pl.ds(r0, rows)
            qn_ref[sl, :] = (_sb_headnorm(p_ref[0, sl, :], gq_ref[...], lo)[0] * scale).astype(BF16)
            kn_ref[sl, :] = _sb_headnorm(p_ref[1, sl, :], gk_ref[...], lo)[0].astype(BF16)
            v_ref[sl, :] = p_ref[2, sl, :].astype(BF16)
            return c

        lax.fori_loop(0, T // rows, prologue, 0)

        rel = _sb_rel(bq, bk)
        tri = (lax.broadcasted_iota(jnp.int32, (bk, bk), 0)
               > lax.broadcasted_iota(jnp.int32, (bk, bk), 1)).astype(BF16)

        def qblock(qi, c):
            q0 = pl.multiple_of(qi * bq, bq)
            qm = _sb_stack(qn_ref[pl.ds(q0, bq), :], lo)
            nkb = (q0 + bq - 1) // bk + 1

            def tiles(k0s, carry, valid):
                o_acc, a_carry = carry
                sc = [_sb_tile(qm, kn_ref[pl.ds(k0, bk), :], valid) for k0 in k0s]
                later = _dot_x2_l([log1m for _, log1m in sc], tri)
                for (logsig, log1m), lat, k0 in zip(sc, later, k0s):
                    wts = jnp.exp(logsig + (lat + a_carry))
                    if valid is not None:
                        wts = jnp.where(valid, wts, 0.0)
                    o_acc = o_acc + jnp.dot(_bf(wts), v_ref[pl.ds(k0, bk), :], preferred_element_type=F32)
                    a_carry = a_carry + jnp.sum(log1m, axis=-1, keepdims=True)
                return o_acc, a_carry

            blk0 = lambda j: pl.multiple_of(j * bk, bk)
            k_last = blk0(nkb - 1)
            cr = tiles([k_last], (jnp.zeros((2 * bq, LANES), F32), jnp.zeros((2 * bq, 1), F32)), rel < q0 - k_last)
            cr = lax.fori_loop(0, (nkb - 1) // 2,
                               lambda t, cr: tiles([blk0(nkb - 2 - 2 * t), blk0(nkb - 3 - 2 * t)], cr, None), cr)
            o2, t2 = lax.fori_loop(0, (nkb - 1) % 2, lambda t, cr: tiles([blk0(0)], cr, None), cr)
            o = jnp.where(lo, o2[:bq], o2[bq:])
            o_ref[pl.ds(q0, bq), :] = o
            ls_ref[pl.ds(q0, bq), :] = jnp.where(lo, t2[:bq], t2[bq:])
            og_ref[pl.ds(q0, bq), :] = (o * _silu(p_ref[3, pl.ds(q0, bq), :])).astype(BF16)
            return c

        lax.fori_loop(0, T // bq, qblock, 0)

    colblk = pl.BlockSpec((T, LANES), lambda j: (0, j))
    vec = pl.BlockSpec((1, LANES), lambda j: (0, 0))
    return pl.pallas_call(
        body, name=name, grid=(W // LANES,),
        in_specs=[pl.BlockSpec((4, T, LANES), lambda j: (0, 0, j)), vec, vec],
        out_specs=[colblk, colblk, colblk],
        out_shape=[jax.ShapeDtypeStruct((T, W), BF16), jax.ShapeDtypeStruct((T, W), F32),
                   jax.ShapeDtypeStruct((T, W), F32)],
        scratch_shapes=[pltpu.VMEM((T, LANES), BF16)] * 3,
        compiler_params=pltpu.CompilerParams(dimension_semantics=("parallel",), vmem_limit_bytes=VMEM_BIG),
    )(p3, gq2, gk2)


def _sb_attn_bwd(p3, gq2, gk2, o, lsum, dog, *, name):
    _, T, W = p3.shape
    bq, bk = min(SB_BQ, T), min(SB_BK, T)
    rows = min(SB_ROWS, T)
    scale = SB_DH ** -0.5

    def body(p_ref, gq_ref, gk_ref, o_ref, ls_ref, dog_ref, dp_ref, dgq_ref, dgk_ref,
             qn_ref, kn_ref, v_ref, do_ref):
        lo = _sb_half_mask()

        def prologue(i, c):
            r0 = pl.multiple_of(i * rows, rows)
            sl = pl.ds(r0, rows)
            qn_ref[sl, :] = (_sb_headnorm(p_ref[0, sl, :], gq_ref[...], lo)[0] * scale).astype(BF16)
            kn_ref[sl, :] = _sb_headnorm(p_ref[1, sl, :], gk_ref[...], lo)[0].astype(BF16)
            v_ref[sl, :] = p_ref[2, sl, :].astype(BF16)
            gate = p_ref[3, sl, :]
            dogv = dog_ref[sl, :]
            dp_ref[3, sl, :] = dogv * o_ref[sl, :] * _dsilu(gate)
            do_ref[sl, :] = (dogv * _silu(gate)).astype(BF16)
            zero = jnp.zeros((rows, LANES), F32)
            dp_ref[0, sl, :] = zero
            dp_ref[1, sl, :] = zero
            dp_ref[2, sl, :] = zero
            return c

        lax.fori_loop(0, T // rows, prologue, 0)

        rel = _sb_rel(bq, bk)
        rj = lax.broadcasted_iota(jnp.int32, (bk, bk), 0)
        cj = lax.broadcasted_iota(jnp.int32, (bk, bk), 1)
        upto = (rj <= cj).astype(BF16)
        before_m = (rj < cj).astype(BF16)

        def qblock(qi, c):
            q0 = pl.multiple_of(qi * bq, bq)
            qm = _sb_stack(qn_ref[pl.ds(q0, bq), :], lo)
            dom = _sb_stack(do_ref[pl.ds(q0, bq), :], lo)
            lsb = ls_ref[pl.ds(q0, bq), :]
            total = jnp.concatenate([lsb[:, 0:1], lsb[:, SB_DH:SB_DH + 1]], axis=0)
            nkb = (q0 + bq - 1) // bk + 1

            def tiles(k0s, carry, valid):
                dq_acc, a_pre, r_pre = carry
                kss = [pl.ds(k0, bk) for k0 in k0s]
                kbs = [kn_ref[ks, :] for ks in kss]
                sc = [_sb_tile(qm, kb, valid) for kb in kbs]
                dws = [lax.dot_general(dom, v_ref[ks, :], _NT, preferred_element_type=F32) for ks in kss]
                upto_l = _dot_x2_l([log1m for _, log1m in sc], upto)
                wts_l = []
                for (logsig, log1m), up in zip(sc, upto_l):
                    wts = jnp.exp(logsig + ((total - a_pre) - up))
                    wts_l.append(wts if valid is None else jnp.where(valid, wts, 0.0))
                    a_pre = a_pre + jnp.sum(log1m, axis=-1, keepdims=True)
                ee_l = [dw * wts for dw, wts in zip(dws, wts_l)]
                before_l = _dot_x2_l(ee_l, before_m)
                for (logsig, _), ks, kb, wts, ee, bef in zip(sc, kss, kbs, wts_l, ee_l, before_l):
                    beta = jnp.exp(logsig)
                    dz = ee * (1.0 - beta) - beta * (r_pre + bef)
                    if valid is not None:
                        dz = jnp.where(valid, dz, 0.0)
                    dzb = _bf(dz)
                    dq_acc = dq_acc + jnp.dot(dzb, kb, preferred_element_type=F32)
                    dp_ref[1, ks, :] += lax.dot_general(dzb, qm, _TN, preferred_element_type=F32)
                    dp_ref[2, ks, :] += lax.dot_general(_bf(wts), dom, _TN, preferred_element_type=F32)
                    r_pre = r_pre + jnp.sum(ee, axis=-1, keepdims=True)
                return dq_acc, a_pre, r_pre

            blk0 = lambda j: pl.multiple_of(j * bk, bk)
            cr = (jnp.zeros((2 * bq, LANES), F32), jnp.zeros((2 * bq, 1), F32), jnp.zeros((2 * bq, 1), F32))
            cr = lax.fori_loop(0, (nkb - 1) // 2, lambda t, cr: tiles([blk0(2 * t), blk0(2 * t + 1)], cr, None), cr)
            cr = lax.fori_loop(0, (nkb - 1) % 2, lambda t, cr: tiles([blk0(nkb - 2)], cr, None), cr)
            k_last = blk0(nkb - 1)
            dq2, _, _ = tiles([k_last], cr, rel < q0 - k_last)
            dp_ref[0, pl.ds(q0, bq), :] = jnp.where(lo, dq2[:bq], dq2[bq:]) * scale
            return c

        lax.fori_loop(0, T // bq, qblock, 0)

        dgq_ref[...] = jnp.zeros_like(dgq_ref)
        dgk_ref[...] = jnp.zeros_like(dgk_ref)

        def epilogue(i, c):
            r0 = pl.multiple_of(i * rows, rows)
            sl = pl.ds(r0, rows)
            for part, g_ref, dg_ref in ((0, gq_ref, dgq_ref), (1, gk_ref, dgk_ref)):
                _, xh, r = _sb_headnorm(p_ref[part, sl, :], g_ref[...], lo)
                dn = dp_ref[part, sl, :]
                dxh = dn * g_ref[...]
                prod = dxh * xh
                m_lo = jnp.sum(jnp.where(lo, prod, 0.0), axis=-1, keepdims=True)
                m_hi = jnp.sum(jnp.where(lo, 0.0, prod), axis=-1, keepdims=True)
                m = jnp.where(lo, m_lo, m_hi) * (1.0 / SB_DH)
                dp_ref[part, sl, :] = r * (dxh - xh * m)
                dg_ref[...] += jnp.sum(dn * xh, axis=0, keepdims=True)
            return c

        lax.fori_loop(0, T // rows, epilogue, 0)

    colblk = pl.BlockSpec((T, LANES), lambda j: (0, j))
    vec = pl.BlockSpec((1, LANES), lambda j: (0, 0))
    part = pl.BlockSpec((4, T, LANES), lambda j: (0, 0, j))
    gvec = pl.BlockSpec((None, 1, LANES), lambda j: (j, 0, 0))
    npair = W // LANES
    return pl.pallas_call(
        body, name=name, grid=(npair,),
        in_specs=[part, vec, vec, colblk, colblk, colblk],
        out_specs=[part, gvec, gvec],
        out_shape=[jax.ShapeDtypeStruct((4, T, W), F32), jax.ShapeDtypeStruct((npair, 1, LANES), F32),
                   jax.ShapeDtypeStruct((npair, 1, LANES), F32)],
        scratch_shapes=[pltpu.VMEM((T, LANES), BF16)] * 4,
        compiler_params=pltpu.CompilerParams(dimension_semantics=("parallel",), vmem_limit_bytes=VMEM_BIG),
    )(p3, gq2, gk2, o, lsum, dog)


_NN = (((1,), (0,)), ((), ()))
_NT = (((1,), (1,)), ((), ()))
_TN = (((0,), (0,)), ((), ()))
DN_TB = 512
DN_AB_COL = (DN_CONV_W + DN_V_W) // LANES


def _dn_conv(x, w_ref):
    k = w_ref.shape[0]
    return sum(w_ref[i:i + 1, :] * _shift_down(x, k - 1 - i) for i in range(k))


def _dn_prep_fwd(p, conv_w, *, name):
    T = p.shape[0]
    cw = conv_w.shape[1]
    n_qk = 2 * DN_QK_W // LANES

    def body(p_ref, w_ref, o_ref):
        s = _silu(_dn_conv(p_ref[...], w_ref))
        r = lax.rsqrt(jnp.sum(s * s, axis=-1, keepdims=True) + L2_EPS)
        o_ref[...] = jnp.where(pl.program_id(0) < n_qk, s * r, s)

    colblk = pl.BlockSpec((T, LANES), lambda j: (0, j))
    return pl.pallas_call(
        body, name=name, grid=(cw // LANES,),
        in_specs=[colblk, pl.BlockSpec((DN_CONV, LANES), lambda j: (0, j))],
        out_specs=colblk, out_shape=jax.ShapeDtypeStruct((T, cw), F32),
        compiler_params=pltpu.CompilerParams(dimension_semantics=("parallel",), vmem_limit_bytes=VMEM_BIG),
    )(p, conv_w)


def _dn_chunk_tri(rows, upper):
    r = lax.broadcasted_iota(jnp.int32, (rows, rows), 0)
    c = lax.broadcasted_iota(jnp.int32, (rows, rows), 1)
    same = (r // DN_CHUNK) == (c // DN_CHUNK)
    return jnp.logical_and(same, (c >= r) if upper else (c <= r)).astype(BF16)


def _dn_lane_rows(a_log, dt_bias):
    pad = lambda v: jnp.zeros((1, LANES), F32).at[0, :DN_HEADS].set(v)
    return pad(a_log), pad(dt_bias)


def _dn_ab_parts(blk, alog_row, dtb_row):
    lane = lax.broadcasted_iota(jnp.int32, (1, LANES), 1)
    is_a = lane < DN_HEADS
    is_b = jnp.logical_and(lane >= DN_HEADS, lane < 2 * DN_HEADS)
    a_arg = jnp.where(is_a, blk + dtb_row, 0.0)
    neg_exp = jnp.where(is_a, -jnp.exp(alog_row), 0.0)
    log_a = neg_exp * _softplus(a_arg)
    beta = jnp.where(is_b, _sigmoid(blk), 0.0)
    return is_a, is_b, a_arg, neg_exp, log_a, beta


def _dn_ab_fwd(p, alog_row, dtb_row, *, name):
    T = p.shape[0]
    rows = min(DN_TB, T)

    def body(p_ref, al_ref, dt_ref, o_ref):
        _, _, _, _, log_a, beta = _dn_ab_parts(p_ref[...], al_ref[...], dt_ref[...])
        hi, mid, lo_ = _split3(log_a)
        tri = _dn_chunk_tri(rows, upper=False)
        f = lambda q: jnp.dot(tri, q, preferred_element_type=F32)
        o_ref[...] = (f(hi) + f(mid) + f(lo_)) + beta

    blk = pl.BlockSpec((rows, LANES), lambda i: (i, DN_AB_COL))
    vec = pl.BlockSpec((1, LANES), lambda i: (0, 0))
    return pl.pallas_call(
        body, name=name, grid=(T // rows,), in_specs=[blk, vec, vec],
        out_specs=pl.BlockSpec((rows, LANES), lambda i: (i, 0)),
        out_shape=jax.ShapeDtypeStruct((T, LANES), F32),
        compiler_params=pltpu.CompilerParams(dimension_semantics=("parallel",)),
    )(p, alog_row, dtb_row)


def _hp_l(a_l, b_l, dims=_NN):
    sa = [_split3(a)[:2] for a in a_l]
    sb = [_split3(b)[:2] for b in b_l]
    f = lambda p, q: lax.dot_general(p, q, dims, preferred_element_type=F32)
    hh = [f(x[0], y[0]) for x, y in zip(sa, sb)]
    hm = [f(x[0], y[1]) for x, y in zip(sa, sb)]
    mh = [f(x[1], y[0]) for x, y in zip(sa, sb)]
    return [a + (b + c) for a, b, c in zip(hh, hm, mh)]


def _dn_local(qs, k, v, g, beta, nc):
    c = DN_CHUNK
    cut = lambda x: [x[i * c:(i + 1) * c] for i in range(nc)]
    row = lax.broadcasted_iota(jnp.int32, (c, c), 0)
    col = lax.broadcasted_iota(jnp.int32, (c, c), 1)
    eye, lower, strict = row == col, row >= col, row > col
    rowid = lax.broadcasted_iota(jnp.int32, (c, 1), 0)
    eg = jnp.exp(g)
    kb = k * beta
    rhs_k = kb * eg
    g_l, k_l, kb_l, qs_l = cut(g), cut(k), cut(kb), cut(qs)
    g_row_l = [jnp.sum(jnp.where(eye, x, 0.0), axis=0, keepdims=True) for x in g_l]
    dec_l = [jnp.where(lower, jnp.exp(jnp.where(lower, x - y, 0.0)), 0.0) for x, y in zip(g_l, g_row_l)]
    kk_l = [_dot_nt(a, b) for a, b in zip(kb_l, k_l)]
    qk_l = [_dot_nt(a, b) for a, b in zip(qs_l, k_l)]
    low_l = [jnp.where(strict, a * d, 0.0) for a, d in zip(kk_l, dec_l)]
    eye_f = eye.astype(F32)
    pw_l = [-x for x in low_l]
    inv_l = [eye_f + x for x in pw_l]
    for _ in range(int(math.log2(c)) - 1):
        pw_l = _hp_l(pw_l, pw_l)
        inv_l = [a + b for a, b in zip(inv_l, _hp_l(inv_l, pw_l))]
    u_l = _hp_l(inv_l, cut(v * beta))
    w_l = _hp_l(inv_l, cut(rhs_k))
    aqk_l = [jnp.where(lower, a * d, 0.0) for a, d in zip(qk_l, dec_l)]
    g_last_l = [jnp.sum(jnp.where(rowid == c - 1, x, 0.0), axis=0, keepdims=True) for x in g_l]
    ekd_l = [jnp.exp(a - b) for a, b in zip(g_last_l, g_l)]
    kd_l = [a * b for a, b in zip(k_l, ekd_l)]
    qd_l = cut(qs * eg)
    kw_l = [_dot_tn(a, b) for a, b in zip(kd_l, w_l)]
    qp_l = [q - _dot(a, w) for q, a, w in zip(qd_l, aqk_l, w_l)]
    return dict(eye=eye, lower=lower, strict=strict, dec=dec_l, k=k_l, kb=kb_l, qs=qs_l, low=low_l, inv=inv_l,
                eg=cut(eg), rhs_k=cut(rhs_k), u=u_l, w=w_l, aqk=aqk_l, g_last=g_last_l, qd=qd_l,
                ekd=ekd_l, kd=kd_l, kw=kw_l, qp=qp_l)


def _dn_head_cols(gb_blk, head):
    lane = lax.broadcasted_iota(jnp.int32, (1, LANES), 1)
    g = jnp.sum(jnp.where(lane == head, gb_blk, 0.0), axis=-1, keepdims=True)
    beta = jnp.sum(jnp.where(lane == head + DN_HEADS, gb_blk, 0.0), axis=-1, keepdims=True)
    return g, beta


def _halves_over_ici(s_refs, o_refs, send_sems, recv_sems, first, last):
    x, y, c = _mesh_pos()
    me = 2 * x + y
    chips = _other_chips(x, y)
    pairs = [(a, k) for a in range(len(s_refs)) for k in range(3)]

    def copy(a, k, slot):
        px, py = chips[k]
        return pltpu.make_async_remote_copy(
            src_ref=s_refs[a].at[c], dst_ref=o_refs[a].at[slot, c], send_sem=send_sems.at[3 * a + k],
            recv_sem=recv_sems.at[3 * a + k], device_id=(px, py, c), device_id_type=MESH)

    @pl.when(first)
    def _():
        for a, k in pairs:
            copy(a, k, me).start()

    @pl.when(last)
    def _():
        for a, k in pairs:
            px, py = chips[k]
            copy(a, k, 2 * px + py).wait_recv()
        for a, k in pairs:
            copy(a, k, me).wait_send()


def _dn_delta_fwd(qkv, gb, p, o_gain, *, name, send=()):
    T = qkv.shape[0]
    tb = min(DN_TB, T)
    nb, nc = T // tb, tb // DN_CHUNK
    H = DN_HEADS
    qscale = DN_DK ** -0.5
    ns = len(send)

    def body(*refs):
        q_ref, k_ref, v_ref, gb_ref, gate_ref, gain_ref = refs[:6]
        o_ref, og_ref, st_ref = refs[6 + ns:9 + ns]
        s_ref = refs[9 + 2 * ns]
        head, blk = pl.program_id(0), pl.program_id(1)
        if ns:
            _halves_over_ici(refs[6:6 + ns], refs[9 + ns:9 + 2 * ns], refs[10 + 2 * ns], refs[11 + 2 * ns],
                             jnp.logical_and(head == 0, blk == 0), jnp.logical_and(head == H - 1, blk == nb - 1))

        @pl.when(blk == 0)
        def _():
            s_ref[...] = jnp.zeros_like(s_ref)

        g, beta = _dn_head_cols(gb_ref[...], head)
        t = _dn_local(q_ref[...] * qscale, k_ref[...], v_ref[...], g, beta, nc)
        ku_l = [_dot_tn(a, b) for a, b in zip(t["kd"], t["u"])]
        op_l = [_dot(a, b) for a, b in zip(t["aqk"], t["u"])]
        s32 = s_ref[...]
        s_l = []
        for i in range(nc):
            s_bf = _bf(s32)
            st_ref[i] = s_bf
            s_l.append(s_bf)
            s32 = s32 * jnp.exp(t["g_last"][i]) - _dot(t["kw"][i], s_bf) + ku_l[i]
        s_ref[...] = s32
        o = jnp.concatenate([_dot(qp, sb) + op for qp, sb, op in zip(t["qp"], s_l, op_l)], axis=0)
        o_ref[...] = o
        r = lax.rsqrt(jnp.mean(o * o, axis=-1, keepdims=True) + RMS_EPS)
        og_ref[...] = (((o * r) * gain_ref[...]) * _silu(gate_ref[...])).astype(BF16)

    qk = lambda off: pl.BlockSpec((tb, DN_DK), lambda h, i: (i, off + h))
    vblk = lambda off: pl.BlockSpec((tb, DN_DV), lambda h, i: (i, off + h))
    return pl.pallas_call(
        body, name=name, grid=(H, nb),
        in_specs=[qk(0), qk(H), vblk(2 * DN_QK_W // DN_DV), pl.BlockSpec((tb, LANES), lambda h, i: (i, 0)),
                  vblk(DN_CONV_W // DN_DV), pl.BlockSpec((1, DN_DV), lambda h, i: (0, 0))] + [HBM] * ns,
        out_specs=[vblk(0), vblk(0), pl.BlockSpec((None, nc, DN_DK, DN_DV), lambda h, i: (h, i, 0, 0))] + [HBM] * ns,
        out_shape=[jax.ShapeDtypeStruct((T, DN_V_W), F32), jax.ShapeDtypeStruct((T, DN_V_W), BF16),
                   jax.ShapeDtypeStruct((H, T // DN_CHUNK, DN_DK, DN_DV), BF16)]
        + [jax.ShapeDtypeStruct((N_CHIPS,) + a.shape, a.dtype) for a in send],
        scratch_shapes=[pltpu.VMEM((DN_DK, DN_DV), F32)]
        + ([pltpu.SemaphoreType.DMA((3 * ns,)), pltpu.SemaphoreType.DMA((3 * ns,))] if ns else []),
        compiler_params=pltpu.CompilerParams(dimension_semantics=("arbitrary", "arbitrary")),
    )(qkv, qkv, qkv, gb, p, o_gain, *send)


def _blocks_over_ici(p_refs, o_refs, send_sems, recv_sems, first, last):
    x, y, c = _mesh_pos()
    me = 2 * x + y
    chips = _other_chips(x, y)
    pairs = [(a, k) for a in range(len(p_refs)) for k in range(3)]

    def copy(a, k, slot):
        px, py = chips[k]
        return pltpu.make_async_remote_copy(
            src_ref=p_refs[a].at[2 * px + py], dst_ref=o_refs[a].at[slot], send_sem=send_sems.at[3 * a + k],
            recv_sem=recv_sems.at[3 * a + k], device_id=(px, py, c), device_id_type=MESH)

    @pl.when(first)
    def _():
        for a, k in pairs:
            copy(a, k, me).start()

    @pl.when(last)
    def _():
        for a, k in pairs:
            px, py = chips[k]
            copy(a, k, 2 * px + py).wait_recv()
        for a, k in pairs:
            copy(a, k, me).wait_send()


def _dn_delta_bwd(qkv, gb, p, o_gain, o, states, dog, *, name, send=()):
    T = qkv.shape[0]
    tb = min(DN_TB, T)
    nb, nc = T // tb, tb // DN_CHUNK
    H = DN_HEADS
    qscale = DN_DK ** -0.5
    ns = len(send)

    def body(*refs):
        q_ref, k_ref, v_ref, gb_ref, gate_ref, gain_ref, o_ref, st_ref, dog_ref = refs[:9]
        dq_ref, dk_ref, dv_ref, dgate_ref, dgb_ref, dgain_ref = refs[9 + ns:15 + ns]
        ds_ref = refs[15 + 2 * ns]
        head = pl.program_id(0)
        if ns:
            _blocks_over_ici(refs[9:9 + ns], refs[15 + ns:15 + 2 * ns], refs[16 + 2 * ns], refs[17 + 2 * ns],
                             jnp.logical_and(head == 0, pl.program_id(1) == 0),
                             jnp.logical_and(head == H - 1, pl.program_id(1) == nb - 1))

        @pl.when(pl.program_id(1) == 0)
        def _():
            ds_ref[...] = jnp.zeros_like(ds_ref)

        @pl.when(jnp.logical_and(head == 0, pl.program_id(1) == 0))
        def _():
            dgain_ref[...] = jnp.zeros_like(dgain_ref)

        lane = lax.broadcasted_iota(jnp.int32, (1, LANES), 1)
        c = DN_CHUNK
        cut = lambda x: [x[i * c:(i + 1) * c] for i in range(nc)]
        cat = lambda xs: jnp.concatenate(xs, axis=0)
        rsum = lambda x: jnp.sum(x, axis=-1, keepdims=True)
        g, beta = _dn_head_cols(gb_ref[...], head)
        ov, gate, gain, dogv = o_ref[...], gate_ref[...], gain_ref[...], dog_ref[...]
        r = lax.rsqrt(jnp.mean(ov * ov, axis=-1, keepdims=True) + RMS_EPS)
        oh = ov * r
        dnrm = dogv * _silu(gate)
        dgate_ref[...] = dogv * (oh * gain) * _dsilu(gate)
        doh = dnrm * gain
        do_l = cut(r * (doh - oh * jnp.mean(doh * oh, axis=-1, keepdims=True)))
        dgain_ref[...] += jnp.sum(dnrm * oh, axis=0, keepdims=True)
        k, v = k_ref[...], v_ref[...]
        t = _dn_local(q_ref[...] * qscale, k, v, g, beta, nc)
        lower, strict, eye = t["lower"], t["strict"], t["eye"]
        s_l = [st_ref[i] for i in range(nc)]
        vn_l = [u - _dot(w, s) for u, w, s in zip(t["u"], t["w"], s_l)]
        dqd_l = [_dot_nt(a, s) for a, s in zip(do_l, s_l)]
        daqk_l = [_dot_nt(a, b) for a, b in zip(do_l, vn_l)]
        aqk_do_l = [_dot_tn(a, b) for a, b in zip(t["aqk"], do_l)]
        egl_l = [jnp.exp(x) for x in t["g_last"]]
        qp_do_l = [_dot_tn(a, b) for a, b in zip(t["qp"], do_l)]
        ds = ds_ref[...]
        ds_l = [None] * nc
        for i in reversed(range(nc)):
            ds_l[i] = ds
            ds = ds * egl_l[i] - _dot_tn(t["kw"][i], ds) + qp_do_l[i]
        ds_ref[...] = ds
        dvn_l = [a + _dot(kd, d) for a, kd, d in zip(aqk_do_l, t["kd"], ds_l)]
        dkd_l = [_dot_nt(a, d) for a, d in zip(vn_l, ds_l)]
        dgl_l = [jnp.sum(rsum(d * sb.astype(F32)), axis=0, keepdims=True) * e for d, sb, e in zip(ds_l, s_l, egl_l)]
        dw_l = [-_dot_nt(a, s) for a, s in zip(dvn_l, s_l)]
        dbv_l = _hp_l(t["inv"], dvn_l, _TN)
        dbk_l = _hp_l(t["inv"], dw_l, _TN)
        dlow_l = [-(a + b) for a, b in zip(_hp_l(dbv_l, t["u"], _NT), _hp_l(dbk_l, t["w"], _NT))]
        m_l = [jnp.where(strict, a * d, 0.0) for a, d in zip(dlow_l, t["dec"])]
        nmat_l = [jnp.where(lower, a * d, 0.0) for a, d in zip(daqk_l, t["dec"])]
        dkb_l = [_dot(m, kk) + b * e for m, kk, b, e in zip(m_l, t["k"], dbk_l, t["eg"])]
        dqs_l = [_dot(n, kk) + a * e for n, kk, a, e in zip(nmat_l, t["k"], dqd_l, t["eg"])]
        dk1_l = [_dot_tn(m, kb) for m, kb in zip(m_l, t["kb"])]
        dk2_l = [_dot_tn(n, q) for n, q in zip(nmat_l, t["qs"])]
        beta_l, v_l = cut(beta), cut(v)
        rowid = lax.broadcasted_iota(jnp.int32, (c, 1), 0)
        dk_l, dg_l, dbeta_l = [], [], []
        for i in range(nc):
            dk_l.append(dk1_l[i] + dk2_l[i] + dkd_l[i] * t["ekd"][i] + dkb_l[i] * beta_l[i])
            gmat = jnp.where(strict, dlow_l[i] * t["low"][i], 0.0) + daqk_l[i] * t["aqk"][i]
            s_kd = rsum(dkd_l[i] * t["kd"][i])
            dg = (rsum(gmat) + rsum(dqd_l[i] * t["qd"][i]) - s_kd + rsum(dbk_l[i] * t["rhs_k"][i]))
            dg_row = -jnp.sum(gmat, axis=0, keepdims=True)
            dg = dg + rsum(jnp.where(eye, dg_row, 0.0))
            dgl = dgl_l[i] + jnp.sum(s_kd, axis=0, keepdims=True)
            dg_l.append(dg + jnp.where(rowid == c - 1, dgl, 0.0))
            dbeta_l.append(rsum(dbv_l[i] * v_l[i]) + rsum(dkb_l[i] * t["k"][i]))
        dq_ref[...] = cat(dqs_l) * qscale
        dk_ref[...] = cat(dk_l)
        dv_ref[...] = cat(dbv_l) * beta
        dgb_ref[...] = (jnp.where(lane == head, cat(dg_l), 0.0)
                        + jnp.where(lane == head + DN_HEADS, cat(dbeta_l), 0.0))

    rev = lambda i: nb - 1 - i
    qk = lambda off: pl.BlockSpec((tb, DN_DK), lambda h, i: (rev(i), off + h))
    vblk = lambda off: pl.BlockSpec((tb, DN_DV), lambda h, i: (rev(i), off + h))
    gain_spec = pl.BlockSpec((1, DN_DV), lambda h, i: (0, 0))
    return pl.pallas_call(
        body, name=name, grid=(H, nb),
        in_specs=[qk(0), qk(H), vblk(2 * DN_QK_W // DN_DV), pl.BlockSpec((tb, LANES), lambda h, i: (rev(i), 0)),
                  vblk(DN_CONV_W // DN_DV), gain_spec, vblk(0),
                  pl.BlockSpec((None, nc, DN_DK, DN_DV), lambda h, i: (h, rev(i), 0, 0)), vblk(0)] + [HBM] * ns,
        out_specs=[qk(0), qk(0), vblk(0), vblk(DN_CONV_W // DN_DV),
                   pl.BlockSpec((None, tb, LANES), lambda h, i: (h, rev(i), 0)), gain_spec] + [HBM] * ns,
        out_shape=[jax.ShapeDtypeStruct((T, DN_QK_W), F32), jax.ShapeDtypeStruct((T, DN_QK_W), F32),
                   jax.ShapeDtypeStruct((T, DN_V_W), F32), jax.ShapeDtypeStruct((T, DN_IN_PAD), F32),
                   jax.ShapeDtypeStruct((H, T, LANES), F32), jax.ShapeDtypeStruct((1, DN_DV), F32)]
        + [jax.ShapeDtypeStruct(a.shape, a.dtype) for a in send],
        scratch_shapes=[pltpu.VMEM((DN_DK, DN_DV), F32)]
        + ([pltpu.SemaphoreType.DMA((3 * ns,)), pltpu.SemaphoreType.DMA((3 * ns,))] if ns else []),
        compiler_params=pltpu.CompilerParams(dimension_semantics=("arbitrary", "arbitrary")),
    )(qkv, qkv, qkv, gb, p, o_gain, o, states, dog, *send)


def _dn_conv_bwd(p, conv_w, d, dp, *, first, normed, name):
    T, width = d.shape

    def body(p_ref, w_ref, d_ref, dp_in, dp_ref, dw_ref):
        del dp_in
        x = p_ref[...]
        ksz = w_ref.shape[0]
        xs = [_shift_down(x, ksz - 1 - i) for i in range(ksz)]
        xc = sum(w_ref[i:i + 1, :] * xs[i] for i in range(ksz))
        ds = d_ref[...]
        if normed:
            s = _silu(xc)
            r = lax.rsqrt(jnp.sum(s * s, axis=-1, keepdims=True) + L2_EPS)
            y = s * r
            ds = r * (ds - y * jnp.sum(ds * y, axis=-1, keepdims=True))
        dxc = ds * _dsilu(xc)
        dp_ref[...] = sum(w_ref[i:i + 1, :] * _shift_up(dxc, ksz - 1 - i) for i in range(ksz))
        for i in range(ksz):
            dw_ref[i:i + 1, :] = jnp.sum(dxc * xs[i], axis=0, keepdims=True)

    shifted = pl.BlockSpec((T, LANES), lambda j: (0, first + j))
    return pl.pallas_call(
        body, name=name, grid=(width // LANES,),
        in_specs=[shifted, pl.BlockSpec((DN_CONV, LANES), lambda j: (0, first + j)),
                  pl.BlockSpec((T, LANES), lambda j: (0, j)), pl.BlockSpec(memory_space=pl.ANY)],
        out_specs=[shifted, pl.BlockSpec((DN_CONV, LANES), lambda j: (0, j))],
        out_shape=[jax.ShapeDtypeStruct(dp.shape, F32), jax.ShapeDtypeStruct((DN_CONV, width), F32)],
        input_output_aliases={3: 0},
        compiler_params=pltpu.CompilerParams(dimension_semantics=("parallel",), vmem_limit_bytes=VMEM_BIG),
    )(p, conv_w, d, dp)


def _dn_ab_bwd(p, alog_row, dtb_row, dgb, dp, *, name):
    T = p.shape[0]
    rows = min(DN_TB, T)
    H = DN_HEADS

    def body(p_ref, al_ref, dt_ref, dgb_ref, dp_in, dp_ref, dal_ref, ddt_ref):
        del dp_in

        @pl.when(pl.program_id(0) == 0)
        def _():
            dal_ref[...] = jnp.zeros_like(dal_ref)
            ddt_ref[...] = jnp.zeros_like(ddt_ref)

        blk = p_ref[...]
        is_a, is_b, a_arg, neg_exp, log_a, beta = _dn_ab_parts(blk, al_ref[...], dt_ref[...])
        d = dgb_ref[0]
        for hh in range(1, H):
            d = d + dgb_ref[hh]
        hi, mid, lo_ = _split3(jnp.where(is_a, d, 0.0))
        tri = _dn_chunk_tri(rows, upper=True)
        f = lambda q: jnp.dot(tri, q, preferred_element_type=F32)
        dlog_a = f(hi) + f(mid) + f(lo_)
        da_in = dlog_a * neg_exp * _sigmoid(a_arg)
        db_in = jnp.where(is_b, d, 0.0) * beta * (1.0 - beta)
        dp_ref[...] = jnp.where(is_a, da_in, 0.0) + db_in
        dal_ref[...] += jnp.sum(dlog_a * log_a, axis=0, keepdims=True)
        ddt_ref[...] += jnp.sum(jnp.where(is_a, da_in, 0.0), axis=0, keepdims=True)

    blk = pl.BlockSpec((rows, LANES), lambda i: (i, DN_AB_COL))
    vec = pl.BlockSpec((1, LANES), lambda i: (0, 0))
    return pl.pallas_call(
        body, name=name, grid=(T // rows,),
        in_specs=[blk, vec, vec, pl.BlockSpec((H, rows, LANES), lambda i: (0, i, 0)),
                  pl.BlockSpec(memory_space=pl.ANY)],
        out_specs=[blk, vec, vec],
        out_shape=[jax.ShapeDtypeStruct(dp.shape, F32), jax.ShapeDtypeStruct((1, LANES), F32),
                   jax.ShapeDtypeStruct((1, LANES), F32)],
        input_output_aliases={4: 0},
        compiler_params=pltpu.CompilerParams(dimension_semantics=("arbitrary",)),
    )(p, alog_row, dtb_row, dgb, dp)


def _dn_layer_fwd(x, ng, w_in, conv_w, a_log, dt_bias, o_gain, w_out, tag, send=()):
    alog_row, dtb_row = _dn_lane_rows(a_log, dt_bias)
    gain = o_gain.reshape(1, DN_DV)
    h = _rmsnorm_fwd(x, ng, name=f"{tag}_norm")
    p = _matmul(h, w_in, mode="nn", name=f"{tag}_inproj")
    qkv = _dn_prep_fwd(p, conv_w, name=f"{tag}_prep")
    gb = _dn_ab_fwd(p, alog_row, dtb_row, name=f"{tag}_ab")
    o, og, states, *landed = _dn_delta_fwd(qkv, gb, p, gain, name=f"{tag}_delta", send=send)
    x_new = _matmul(og, w_out, mode="nn", res=x, name=f"{tag}_outproj")
    return x_new, (h, p, qkv, gb, o, og, states), landed


def _dn_layer_bwd(dx, x, ng, w_in, conv_w, a_log, dt_bias, o_gain, w_out, saved, tag, send=()):
    h, p, qkv, gb, o, og, states = saved
    alog_row, dtb_row = _dn_lane_rows(a_log, dt_bias)
    gain = o_gain.reshape(1, DN_DV)
    d_wout = _matmul(og, dx, mode="tn", out_dtype=BF16, name=f"{tag}_dwout")
    dog = _matmul(dx, w_out, mode="nt", name=f"{tag}_dog")
    dq, dk, dv, dp, dgb, dgain, *landed = _dn_delta_bwd(qkv, gb, p, gain, o, states, dog, name=f"{tag}_deltabwd",
                                                        send=send)
    n_qk = DN_QK_W // LANES
    dp, dconv_q = _dn_conv_bwd(p, conv_w, dq, dp, first=0, normed=True, name=f"{tag}_convbwd_q")
    dp, dconv_k = _dn_conv_bwd(p, conv_w, dk, dp, first=n_qk, normed=True, name=f"{tag}_convbwd_k")
    dp, dconv_v = _dn_conv_bwd(p, conv_w, dv, dp, first=2 * n_qk, normed=False, name=f"{tag}_convbwd_v")
    dconv = jnp.concatenate([dconv_q, dconv_k, dconv_v], axis=1)
    dp, dal, ddt = _dn_ab_bwd(p, alog_row, dtb_row, dgb, dp, name=f"{tag}_abbwd")
    d_win = _matmul(h, dp, mode="tn", name=f"{tag}_dwin")
    dh = _matmul(dp, w_in, mode="nt", name=f"{tag}_dh")
    dx_prev, dng = _rmsnorm_bwd(x, ng, dh, dx, name=f"{tag}_normbwd")
    return dx_prev, dng, d_win, dconv, dal[0, :DN_HEADS], ddt[0, :DN_HEADS], dgain[0], d_wout, landed


def _sb_gains(g):
    return jnp.concatenate([g, g]).reshape(1, LANES)


def _sb_layer_fwd(x, ng, w_in, gq, gk, w_out, tag):
    h = _rmsnorm_fwd(x, ng, name=f"{tag}_norm")
    p3 = _matmul(h, w_in, mode="nn", b_parts=4, out_parts=4, name=f"{tag}_inproj")
    og, o, lsum = _sb_attn_fwd(p3, _sb_gains(gq), _sb_gains(gk), name=f"{tag}_attn")
    x_new = _matmul(og, w_out, mode="nn", res=x, name=f"{tag}_outproj")
    return x_new, (h, p3, og, o, lsum)


def _sb_layer_bwd(dx, x, ng, w_in, gq, gk, w_out, saved, tag):
    h, p3, og, o, lsum = saved
    d_wout = _matmul(og, dx, mode="tn", out_dtype=BF16, name=f"{tag}_dwout")
    dog = _matmul(dx, w_out, mode="nt", name=f"{tag}_dog")
    dp3, dgq, dgk = _sb_attn_bwd(p3, _sb_gains(gq), _sb_gains(gk), o, lsum, dog, name=f"{tag}_attnbwd")
    fold = lambda d: jnp.sum(d.reshape(-1, SB_DH), axis=0)
    d_win = _matmul(h, dp3, mode="tn", b_parts=4, out_parts=4, out_dtype=BF16, name=f"{tag}_dwin")
    dh = _matmul(dp3, w_in, mode="nt", a_parts=4, b_parts=4, name=f"{tag}_dh")
    dx_prev, dng = _rmsnorm_bwd(x, ng, dh, dx, name=f"{tag}_normbwd")
    return dx_prev, dng, d_win, fold(dgq), fold(dgk), d_wout


N_CHIPS = 4
HBM = pl.BlockSpec(memory_space=pl.ANY)


def _mesh_pos():
    return lax.axis_index("x"), lax.axis_index("y"), lax.axis_index("c")


def _other_chips(x, y):
    return [(1 - x, y), (x, 1 - y), (1 - x, 1 - y)]


def _chip_exchange(srcs, *, send_slot_is_dest, copy_own, name):
    n = len(srcs)

    def body(*refs):
        src_refs, out_refs = refs[:n], refs[n:2 * n]
        send_sems, recv_sems, local_sems = refs[2 * n:]
        x, y, c = _mesh_pos()
        me = 2 * x + y
        chips = _other_chips(x, y)
        local = []
        for a in range(n):
            if not copy_own[a]:
                continue
            own = src_refs[a].at[me] if send_slot_is_dest else src_refs[a]
            local.append(pltpu.make_async_copy(own, out_refs[a].at[me], local_sems.at[a]))
        for cp in local:
            cp.start()

        def copy(a, k, landing_slot):
            px, py = chips[k]
            src = src_refs[a].at[2 * px + py] if send_slot_is_dest else src_refs[a]
            return pltpu.make_async_remote_copy(
                src_ref=src, dst_ref=out_refs[a].at[landing_slot],
                send_sem=send_sems.at[a * 3 + k], recv_sem=recv_sems.at[a * 3 + k],
                device_id=(px, py, c), device_id_type=MESH)

        sends = [copy(a, k, me) for a in range(n) for k in range(3)]
        for cp in sends:
            cp.start()
        for a in range(n):
            for k in range(3):
                px, py = chips[k]
                copy(a, k, 2 * px + py).wait_recv()
        for cp in sends:
            cp.wait_send()
        for cp in local:
            cp.wait()

    outs = []
    for s in srcs:
        shape = s.shape if send_slot_is_dest else (N_CHIPS,) + s.shape
        outs.append(jax.ShapeDtypeStruct(shape, s.dtype))
    return pl.pallas_call(
        body, name=name, in_specs=[HBM] * n, out_specs=[HBM] * n, out_shape=outs,
        scratch_shapes=[pltpu.SemaphoreType.DMA((3 * n,)), pltpu.SemaphoreType.DMA((3 * n,)),
                        pltpu.SemaphoreType.DMA((n,))],
    )(*srcs)


def _sibling_exchange(srcs, *, name):
    n = len(srcs)

    def body(*refs):
        src_refs, out_refs = refs[:n], refs[n:2 * n]
        send_sems, recv_sems = refs[2 * n:]
        x, y, c = _mesh_pos()
        copies = [pltpu.make_async_remote_copy(
            src_ref=src_refs[a], dst_ref=out_refs[a], send_sem=send_sems.at[a], recv_sem=recv_sems.at[a],
            device_id=(x, y, 1 - c), device_id_type=MESH) for a in range(n)]
        for cp in copies:
            cp.start()
        for cp in copies:
            cp.wait()

    return pl.pallas_call(
        body, name=name, in_specs=[HBM] * n, out_specs=[HBM] * n,
        out_shape=[jax.ShapeDtypeStruct(s.shape, s.dtype) for s in srcs],
        scratch_shapes=[pltpu.SemaphoreType.DMA((n,)), pltpu.SemaphoreType.DMA((n,))],
    )(*srcs)


def _gather_halves(shards, small, *, name):
    n = len(shards)

    def body(*refs):
        s_refs, small_ref = refs[:n], refs[n]
        o_refs, osmall_ref = refs[n + 1:2 * n + 1], refs[2 * n + 1]
        send_sems, recv_sems, local_sems = refs[2 * n + 2:]
        x, y, c = _mesh_pos()
        me = 2 * x + y
        chips = _other_chips(x, y)
        local = [pltpu.make_async_copy(small_ref, osmall_ref.at[me], local_sems.at[0])]
        for cp in local:
            cp.start()

        def over_ici(a, k, slot):
            px, py = chips[k]
            return pltpu.make_async_remote_copy(
                src_ref=s_refs[a].at[c], dst_ref=o_refs[a].at[slot, c], send_sem=send_sems.at[3 * a + k],
                recv_sem=recv_sems.at[3 * a + k], device_id=(px, py, c), device_id_type=MESH)

        def small_copy(k, slot):
            px, py = chips[k]
            return pltpu.make_async_remote_copy(
                src_ref=small_ref, dst_ref=osmall_ref.at[slot], send_sem=send_sems.at[3 * n + k],
                recv_sem=recv_sems.at[3 * n + k], device_id=(px, py, c), device_id_type=MESH)

        def to_sibling(a, k, half):
            px, py = chips[k]
            blk = o_refs[a].at[2 * px + py, half]
            return pltpu.make_async_remote_copy(
                src_ref=blk, dst_ref=blk, send_sem=send_sems.at[3 * n + 3 + 3 * a + k],
                recv_sem=recv_sems.at[3 * n + 3 + 3 * a + k], device_id=(x, y, 1 - c), device_id_type=MESH)

        sends = [over_ici(a, k, me) for a in range(n) for k in range(3)] + [small_copy(k, me) for k in range(3)]
        for cp in sends:
            cp.start()
        passed = []
        for a in range(n):
            for k in range(3):
                px, py = chips[k]
                over_ici(a, k, 2 * px + py).wait_recv()
                passed.append(to_sibling(a, k, c))
                passed[-1].start()
        for k in range(3):
            px, py = chips[k]
            small_copy(k, 2 * px + py).wait_recv()
        for a in range(n):
            for k in range(3):
                to_sibling(a, k, 1 - c).wait_recv()
        for cp in sends + passed:
            cp.wait_send()
        for cp in local:
            cp.wait()

    nsem = 6 * n + 3
    return pl.pallas_call(
        body, name=name, in_specs=[HBM] * (n + 1), out_specs=[HBM] * (n + 1),
        out_shape=[jax.ShapeDtypeStruct((N_CHIPS,) + s.shape, s.dtype) for s in shards + [small]],
        scratch_shapes=[pltpu.SemaphoreType.DMA((nsem,)), pltpu.SemaphoreType.DMA((nsem,)),
                        pltpu.SemaphoreType.DMA((1,))],
    )(*shards, small)


def _forward_halves(landed, *, name):
    n = len(landed)

    def body(*refs):
        o_refs = refs[n:2 * n]
        send_sems, recv_sems = refs[2 * n:]
        x, y, c = _mesh_pos()
        chips = _other_chips(x, y)
        pairs = [(a, k) for a in range(n) for k in range(3)]

        def copy(a, k, half):
            px, py = chips[k]
            blk = o_refs[a].at[2 * px + py, half]
            return pltpu.make_async_remote_copy(
                src_ref=blk, dst_ref=blk, send_sem=send_sems.at[3 * a + k], recv_sem=recv_sems.at[3 * a + k],
                device_id=(x, y, 1 - c), device_id_type=MESH)

        sends = [copy(a, k, c) for a, k in pairs]
        for cp in sends:
            cp.start()
        for a, k in pairs:
            copy(a, k, 1 - c).wait_recv()
        for cp in sends:
            cp.wait_send()

    return pl.pallas_call(
        body, name=name, in_specs=[HBM] * n, out_specs=[HBM] * n,
        out_shape=[jax.ShapeDtypeStruct(a.shape, a.dtype) for a in landed],
        input_output_aliases={a: a for a in range(n)},
        scratch_shapes=[pltpu.SemaphoreType.DMA((3 * n,)), pltpu.SemaphoreType.DMA((3 * n,))],
    )(*landed)


def _swap_other_half(g_list, *, name):
    n = len(g_list)

    def body(*refs):
        g_refs, o_refs = refs[:n], refs[n:2 * n]
        send_sems, recv_sems = refs[2 * n:]
        x, y, c = _mesh_pos()
        copies = [pltpu.make_async_remote_copy(
            src_ref=g_refs[a].at[:, 1 - c], dst_ref=o_refs[a], send_sem=send_sems.at[a], recv_sem=recv_sems.at[a],
            device_id=(x, y, 1 - c), device_id_type=MESH) for a in range(n)]
        for cp in copies:
            cp.start()
        for cp in copies:
            cp.wait()

    return pl.pallas_call(
        body, name=name, in_specs=[HBM] * n, out_specs=[HBM] * n,
        out_shape=[jax.ShapeDtypeStruct((g.shape[0],) + g.shape[2:], g.dtype) for g in g_list],
        scratch_shapes=[pltpu.SemaphoreType.DMA((n,)), pltpu.SemaphoreType.DMA((n,))],
    )(*g_list)


def _row_tile(r):
    return _pick(r, (512, 256, 128, 64, 32, 16, 8))


def _add_my_half(g4, sib4, core, *, name):
    n, _, r, C = g4.shape
    tr = _row_tile(r)

    def body(core_ref, g_ref, s_ref, o_ref):
        del core_ref
        o_ref[...] = (g_ref[...].astype(F32) + s_ref[...].astype(F32)).astype(o_ref.dtype)

    return pl.pallas_call(
        body, name=name,
        grid_spec=pltpu.PrefetchScalarGridSpec(
            num_scalar_prefetch=1, grid=(n, r // tr),
            in_specs=[pl.BlockSpec((None, None, tr, C), lambda j, i, core_ref: (j, core_ref[0], i, 0)),
                      pl.BlockSpec((None, tr, C), lambda j, i, core_ref: (j, i, 0))],
            out_specs=pl.BlockSpec((None, tr, C), lambda j, i, core_ref: (j, i, 0))),
        out_shape=jax.ShapeDtypeStruct((n, r, C), g4.dtype),
        compiler_params=pltpu.CompilerParams(dimension_semantics=("parallel", "parallel")),
    )(core, g4, sib4)


def _scatter_to_chips(p_list, *, name):
    n = len(p_list)

    def body(*refs):
        p_refs, o_refs = refs[:n], refs[n:2 * n]
        send_sems, recv_sems = refs[2 * n:]
        x, y, c = _mesh_pos()
        me = 2 * x + y
        chips = _other_chips(x, y)
        pairs = [(a, k) for a in range(n) for k in range(3)]

        def copy(a, k, landing_slot):
            px, py = chips[k]
            return pltpu.make_async_remote_copy(
                src_ref=p_refs[a].at[2 * px + py], dst_ref=o_refs[a].at[landing_slot],
                send_sem=send_sems.at[3 * a + k], recv_sem=recv_sems.at[3 * a + k], device_id=(px, py, c),
                device_id_type=MESH)

        sends = [copy(a, k, me) for a, k in pairs]
        for cp in sends:
            cp.start()
        for a, k in pairs:
            px, py = chips[k]
            copy(a, k, 2 * px + py).wait_recv()
        for cp in sends:
            cp.wait_send()

    return pl.pallas_call(
        body, name=name, in_specs=[HBM] * n, out_specs=[HBM] * n,
        out_shape=[jax.ShapeDtypeStruct(p.shape, p.dtype) for p in p_list],
        scratch_shapes=[pltpu.SemaphoreType.DMA((3 * n,)), pltpu.SemaphoreType.DMA((3 * n,))],
    )(*p_list)


def _sum_chips(landed, part, me, *, name):
    _, r, C = landed.shape
    tr = _row_tile(r)

    def body(me_ref, own_ref, r1_ref, r2_ref, r3_ref, o_ref):
        del me_ref
        f = lambda ref: ref[...].astype(F32)
        o_ref[...] = ((f(own_ref) + f(r1_ref)) + f(r2_ref)) + f(r3_ref)

    slot = lambda d: pl.BlockSpec((None, tr, C), lambda i, me_ref: ((me_ref[0] + d) % N_CHIPS, i, 0))
    return pl.pallas_call(
        body, name=name,
        grid_spec=pltpu.PrefetchScalarGridSpec(
            num_scalar_prefetch=1, grid=(r // tr,), in_specs=[slot(0), slot(1), slot(2), slot(3)],
            out_specs=pl.BlockSpec((tr, C), lambda i, me_ref: (i, 0))),
        out_shape=jax.ShapeDtypeStruct((r, C), F32),
        compiler_params=pltpu.CompilerParams(dimension_semantics=("parallel",)),
    )(me, part, landed, landed, landed)


def _adamw_halves(w, mine, theirs, m, v, core, *, layer, prev, name):
    shape = w.shape
    r, C = mine.shape
    tr = _pick(r, (128, 64, 32, 16, 8))
    per = r // tr
    view = lambda a: a.reshape(-1, C)
    n_prev = 0 if prev is None else 4

    def body(*refs):
        core_ref, w_ref, gm_ref, gt_ref, m_ref, v_ref = refs[:6]
        g_ref, d_ref, nm_ref, nv_ref = refs[6 + n_prev:]
        gv = jnp.where(pl.program_id(0) == core_ref[0], gm_ref[...], gt_ref[...])
        g_ref[...] = gv
        d_ref[...], nm_ref[...], nv_ref[...] = _adamw_math(w_ref[...], gv, m_ref[...], v_ref[...])

    half = pl.BlockSpec((tr, C), lambda h, i, core_ref: ((2 * layer + h) * per + i, 0))
    row = pl.BlockSpec((tr, C), lambda h, i, core_ref: (i, 0))
    out = jax.ShapeDtypeStruct((math.prod(shape) // C, C), F32)
    res = pl.pallas_call(
        body, name=name,
        grid_spec=pltpu.PrefetchScalarGridSpec(
            num_scalar_prefetch=1, grid=(2, per), in_specs=[half, row, row, half, half] + [HBM] * n_prev,
            out_specs=[half] * 4),
        out_shape=[out] * 4,
        input_output_aliases={6 + j: j for j in range(n_prev)},
        compiler_params=pltpu.CompilerParams(dimension_semantics=("parallel", "parallel")),
    )(core, view(w), mine, theirs, view(m), view(v), *([] if prev is None else [view(a) for a in prev]))
    return tuple(a.reshape(shape) for a in res)


def _sum_small(recv4, *, name):
    _, R, C = recv4.shape

    def body(r_ref, o_ref):
        o_ref[...] = ((r_ref[0] + r_ref[1]) + r_ref[2]) + r_ref[3]

    return pl.pallas_call(body, name=name, out_shape=jax.ShapeDtypeStruct((R, C), F32))(recv4)


def _add(a, b, *, name):
    R, C = a.shape
    tr = _pick(R, (512, 256, 128, 64, 32, 16, 8))
    blk = pl.BlockSpec((tr, C), lambda i: (i, 0))

    def body(a_ref, b_ref, o_ref):
        o_ref[...] = a_ref[...] + b_ref[...]

    return pl.pallas_call(body, name=name, grid=(R // tr,), in_specs=[blk, blk], out_specs=blk,
                          out_shape=jax.ShapeDtypeStruct((R, C), F32),
                          compiler_params=pltpu.CompilerParams(dimension_semantics=("parallel",)))(a, b)


def _adamw_math(w, g, m, v):
    nm = ADAM_B1 * m + (1.0 - ADAM_B1) * g
    nv = ADAM_B2 * v + (1.0 - ADAM_B2) * (g * g)
    m_hat = nm / (1.0 - ADAM_B1 ** ADAM_STEP)
    v_hat = nv / (1.0 - ADAM_B2 ** ADAM_STEP)
    return -ADAM_LR * (m_hat / (jnp.sqrt(v_hat) + ADAM_EPS) + ADAM_WD * w), nm, nv


def _adamw(w, g, m, v, *, name):
    shape = w.shape
    C = shape[-1]
    R = w.size // C
    two = lambda a: a.reshape(R, C)
    tr = _pick(R, (256, 128, 64, 32, 16, 8)) if R % 8 == 0 and R > 8 else R
    blk = pl.BlockSpec((tr, C), lambda i: (i, 0))

    def body(w_ref, g_ref, m_ref, v_ref, d_ref, nm_ref, nv_ref):
        d_ref[...], nm_ref[...], nv_ref[...] = _adamw_math(w_ref[...], g_ref[...], m_ref[...], v_ref[...])

    out = jax.ShapeDtypeStruct((R, C), F32)
    d, nm, nv = pl.pallas_call(
        body, name=name, grid=(R // tr,), in_specs=[blk] * 4, out_specs=[blk] * 3, out_shape=[out] * 3,
        compiler_params=pltpu.CompilerParams(dimension_semantics=("parallel",)),
    )(two(w), two(g), two(m), two(v))
    return d.reshape(shape), nm.reshape(shape), nv.reshape(shape)


BIG = (("dn_w_in", (2, 1024, 1540), 2), ("dn_w_out", (2, 512, 1024), 1), ("sb_w_in", (1, 1024, 1024), 2),
       ("sb_w_out", (1, 256, 1024), 1), ("sc_w_in", (1, 1024, 2048), 2), ("sc_w_out", (1, 512, 1024), 1))
SMALL = (("dn_conv_w", (2, 4, 1024), 2), ("dn_o_norm_g", (2, 64), 1), ("sc_conv_w", (1, 3, 512), 2))
REPL = (("norm_g", (4, 1024)), ("dn_a_log", (2, 8)), ("dn_dt_bias", (2, 8)), ("sb_q_norm_g", (1, 64)),
        ("sb_k_norm_g", (1, 64)))


def _halves(shard):
    return shard.reshape(2, -1, shard.shape[-1])


def _pack(arrays, cols, lead=()):
    flat = jnp.concatenate([a.reshape(lead + (-1,)) for a in arrays], axis=-1)
    n = flat.shape[-1]
    rows = -(-n // cols)
    unit = 512 if rows > 512 else 8
    rows = -(-rows // unit) * unit
    flat = jnp.pad(flat, [(0, 0)] * len(lead) + [(0, rows * cols - n)])
    return flat.reshape(lead + (rows, cols))


def _unpack(buf, table, lead=()):
    flat = buf.reshape(lead + (-1,))
    out, off = {}, 0
    for entry in table:
        name, shape = entry[0], entry[1]
        n = math.prod(shape)
        out[name] = flat[..., off:off + n].reshape(lead + shape)
        off += n
    return out


def _join(shards, axis):
    return jnp.concatenate([shards[j] for j in range(N_CHIPS)], axis=axis)


def _split(full, axis):
    return jnp.stack(jnp.split(full, N_CHIPS, axis=axis), axis=0)


def kernel(x, norm_g, dn_w_in, dn_conv_w, dn_a_log, dn_dt_bias, dn_o_norm_g, dn_w_out, sb_w_in, sb_q_norm_g, sb_k_norm_g, sb_w_out, sc_w_in, sc_conv_w, sc_w_out, loss_target, m_norm_g, m_dn_w_in, m_dn_conv_w, m_dn_a_log, m_dn_dt_bias, m_dn_o_norm_g, m_dn_w_out, m_sb_w_in, m_sb_q_norm_g, m_sb_k_norm_g, m_sb_w_out, m_sc_w_in, m_sc_conv_w, m_sc_w_out, v_norm_g, v_dn_w_in, v_dn_conv_w, v_dn_a_log, v_dn_dt_bias, v_dn_o_norm_g, v_dn_w_out, v_sb_w_in, v_sb_q_norm_g, v_sb_k_norm_g, v_sb_w_out, v_sc_w_in, v_sc_conv_w, v_sc_w_out):
    weights = dict(norm_g=norm_g, dn_w_in=dn_w_in, dn_conv_w=dn_conv_w, dn_a_log=dn_a_log, dn_dt_bias=dn_dt_bias,
                   dn_o_norm_g=dn_o_norm_g, dn_w_out=dn_w_out, sb_w_in=sb_w_in, sb_q_norm_g=sb_q_norm_g,
                   sb_k_norm_g=sb_k_norm_g, sb_w_out=sb_w_out, sc_w_in=sc_w_in, sc_conv_w=sc_conv_w, sc_w_out=sc_w_out)
    m_in = dict(norm_g=m_norm_g, dn_w_in=m_dn_w_in, dn_conv_w=m_dn_conv_w, dn_a_log=m_dn_a_log,
                dn_dt_bias=m_dn_dt_bias, dn_o_norm_g=m_dn_o_norm_g, dn_w_out=m_dn_w_out, sb_w_in=m_sb_w_in,
                sb_q_norm_g=m_sb_q_norm_g, sb_k_norm_g=m_sb_k_norm_g, sb_w_out=m_sb_w_out, sc_w_in=m_sc_w_in,
                sc_conv_w=m_sc_conv_w, sc_w_out=m_sc_w_out)
    v_in = dict(norm_g=v_norm_g, dn_w_in=v_dn_w_in, dn_conv_w=v_dn_conv_w, dn_a_log=v_dn_a_log,
                dn_dt_bias=v_dn_dt_bias, dn_o_norm_g=v_dn_o_norm_g, dn_w_out=v_dn_w_out, sb_w_in=v_sb_w_in,
                sb_q_norm_g=v_sb_q_norm_g, sb_k_norm_g=v_sb_k_norm_g, sb_w_out=v_sb_w_out, sc_w_in=v_sc_w_in,
                sc_conv_w=v_sc_conv_w, sc_w_out=v_sc_w_out)
    order = list(weights)
    xi, yi, ci = _mesh_pos()

    small = _pack([weights[n] for n, _, _ in SMALL], LANES)
    later = [("dn_w_in", 1), ("dn_w_out", 1), ("sb_w_in", 0), ("sb_w_out", 0), ("sc_w_in", 0), ("sc_w_out", 0)]
    piece = lambda n, l: _halves(weights[n][l].astype(BF16)[None])
    own_first = [piece("dn_w_in", 0), piece("dn_w_out", 0)]
    own_later = [piece(n, l) for n, l in later]
    me = 2 * xi + yi
    whole = lambda g4, own: lax.dynamic_update_index_in_dim(g4, own, me, 0)
    flat = lambda g4: g4.reshape(N_CHIPS, -1, g4.shape[-1])
    rows_of = lambda w4: w4.reshape(-1, w4.shape[-1])
    dn_in = lambda w4: jnp.pad(_join(w4, 1), ((0, 0), (0, DN_IN_PAD - DN_IN)))
    w_in0, w_out0, small4 = _gather_halves(own_first, small, name="gather_first")
    full = {n: _join(a, ax) for (n, _, ax), a in zip(SMALL, _unpack(small4, SMALL, (N_CHIPS,)).values())}

    def dn_args(j, w_in4, w_out4):
        return (dn_in(flat(w_in4)), full["dn_conv_w"][j], dn_a_log[j], dn_dt_bias[j], full["dn_o_norm_g"][j],
                rows_of(w_out4))

    x0 = x[0]
    dn0 = dn_args(0, whole(w_in0, own_first[0]), whole(w_out0, own_first[1]))
    x1, s0, landed = _dn_layer_fwd(x0, norm_g[0], *dn0, "l0", send=own_later)
    landed = _forward_halves(landed, name="forward_halves")
    w_in3, w_out3, sb_in, sb_out, sc_in, sc_out = [whole(g4, own) for g4, own in zip(landed, own_later)]
    dn1 = dn_args(1, w_in3, w_out3)
    sb_args = (flat(sb_in), sb_q_norm_g[0], sb_k_norm_g[0], rows_of(sb_out))
    sc_args = (flat(sc_in), full["sc_conv_w"][0], rows_of(sc_out))
    x2, s1 = _sb_layer_fwd(x1, norm_g[1], *sb_args, "l1")
    x3, s2 = _sc_layer_fwd(x2, norm_g[2], *sc_args, "l2")
    x4, s3, _ = _dn_layer_fwd(x3, norm_g[3], *dn1, "l3")
    dy, loss_local = _loss_head(x4, loss_target[0], name="loss_head")
    loss = lax.psum(loss_local[0, 0], ("x", "y", "c"))

    by_cols = lambda dw: _split(dw[:, :DN_IN].astype(BF16), 1)
    by_rows = lambda dw: dw.reshape(N_CHIPS, -1, dw.shape[-1])
    cut2 = lambda g4: g4.reshape(N_CHIPS, 2, -1, g4.shape[-1])
    core = ci.astype(jnp.int32).reshape(1)
    chip = me.astype(jnp.int32).reshape(1)

    def chip_sums(g_list, tag):
        sib = _swap_other_half(g_list, name=f"swap_halves_{tag}")
        return [_add_my_half(g, s, core, name=f"sum_cores_{tag}{i}") for i, (g, s) in enumerate(zip(g_list, sib))]

    dx3, dng3, dwin3, dconv3, dal3, ddt3, dgain3, dwout3, _ = _dn_layer_bwd(dy, x3, norm_g[3], *dn1, s3, "l3")
    dx2, dng2, dwin2, dconv2, dwout2 = _sc_layer_bwd(dx3, x2, norm_g[2], *sc_args, s2, "l2")
    dx1, dng1, dwin1, dgq, dgk, dwout1 = _sb_layer_bwd(dx2, x1, norm_g[1], *sb_args, s1, "l1")
    part_later = chip_sums([cut2(by_cols(dwin3)), cut2(by_rows(dwout3)), cut2(dwin1), cut2(by_rows(dwout1)),
                            cut2(dwin2), cut2(by_rows(dwout2))], "later")
    dx0, dng0, dwin0, dconv0, dal0, ddt0, dgain0, dwout0, landed_later = _dn_layer_bwd(
        dx1, x0, norm_g[0], *dn0, s0, "l0", send=part_later)
    part_first = chip_sums([cut2(by_cols(dwin0)), cut2(by_rows(dwout0))], "first")
    landed_first = _scatter_to_chips(part_first, name="scatter_first")
    pieces = [("dn_w_in", 0), ("dn_w_out", 0)] + later
    mine = [_sum_chips(r, p, chip, name=f"sum_chips_{n}{l}")
            for (n, l), r, p in zip(pieces, list(landed_first) + list(landed_later), part_first + part_later)]
    theirs = _sibling_exchange(mine, name="swap_results")
    upd = {}
    for (n, l), a, b in zip(pieces, mine, theirs):
        upd[n] = _adamw_halves(weights[n], a, b, m_in[n], v_in[n], core, layer=l, prev=upd.get(n),
                               name=f"adamw_{n}{l}")
    g_out = {n: upd[n][0] for n, _, _ in BIG}

    grads = dict(
        norm_g=jnp.concatenate([dng0, dng1, dng2, dng3], axis=0), dn_conv_w=jnp.stack([dconv0, dconv3]),
        dn_a_log=jnp.stack([dal0, dal3]), dn_dt_bias=jnp.stack([ddt0, ddt3]),
        dn_o_norm_g=jnp.stack([dgain0, dgain3]), sb_q_norm_g=dgq[None], sb_k_norm_g=dgk[None],
        sc_conv_w=dconv2[None])
    repl = [jnp.broadcast_to(grads[n][None], (N_CHIPS,) + s) for n, s in REPL]
    gsmall = _pack([_split(grads[n], ax) for n, _, ax in SMALL] + repl, LANES, (N_CHIPS,))
    rsmall, = _chip_exchange([gsmall], send_slot_is_dest=True, copy_own=(True,), name="scatter_small")
    psmall = _sum_small(rsmall, name="sum_chips_small")
    qsmall, = _sibling_exchange([psmall], name="swap_cores_small")
    tsmall = _add(psmall, qsmall, name="sum_cores_small")
    g_out.update(_unpack(tsmall, SMALL + REPL))

    for n in order:
        if n not in upd:
            upd[n] = (g_out[n],) + _adamw(weights[n], g_out[n], m_in[n], v_in[n], name=f"adamw_{n}")
    return (loss, dx0[None], *[upd[n][0] for n in order], *[upd[n][1] for n in order],
            *[upd[n][2] for n in order], *[upd[n][3] for n in order])
```

```python
import math

import jax
import jax.numpy as jnp
from jax import lax
from jax.experimental import pallas as pl
from jax.experimental.pallas import tpu as pltpu

F32 = jnp.float32
BF16 = jnp.bfloat16
MESH = pl.DeviceIdType.MESH

RMS_EPS = 1e-6
L2_EPS = 1e-6
LANES = 128
VMEM_BIG = 60 * 1024 * 1024
MM_VMEM = 36 * 1024 * 1024

DN_HEADS, DN_DK, DN_DV, DN_CHUNK, DN_CONV = 8, 128, 256, 64, 4
DN_QK_W = DN_HEADS * DN_DK
DN_V_W = DN_HEADS * DN_DV
DN_CONV_W = 2 * DN_QK_W + DN_V_W
DN_IN = DN_CONV_W + DN_V_W + 2 * DN_HEADS
DN_IN_PAD = DN_CONV_W + DN_V_W + LANES
SB_DH = 64
SC_CONV = 3

ADAM_LR, ADAM_B1, ADAM_B2, ADAM_EPS, ADAM_WD, ADAM_STEP = 0.001, 0.9, 0.999, 1e-08, 0.01, 10


def _pick(n, cands):
    for c in cands:
        if n % c == 0:
            return c
    raise ValueError(f"no tile for {n} in {cands}")


def _bf(x):
    return x.astype(BF16)


def _dot(a, b):
    return jnp.dot(_bf(a), _bf(b), preferred_element_type=F32)


def _dot_nt(a, b):
    return lax.dot_general(_bf(a), _bf(b), (((1,), (1,)), ((), ())), preferred_element_type=F32)


def _dot_tn(a, b):
    return lax.dot_general(_bf(a), _bf(b), (((0,), (0,)), ((), ())), preferred_element_type=F32)


def _split3(a):
    hi = _bf(a)
    r = a - hi.astype(F32)
    mid = _bf(r)
    lo = _bf(r - mid.astype(F32))
    return hi, mid, lo


def _sigmoid(x):
    return 1.0 / (1.0 + jnp.exp(-x))


def _silu(x):
    return x * _sigmoid(x)


def _dsilu(x):
    s = _sigmoid(x)
    return s * (1.0 + x * (1.0 - s))


def _softplus(x):
    return jnp.maximum(x, 0.0) + jnp.log(1.0 + jnp.exp(-jnp.abs(x)))


def _shift_down(z, k):
    if k == 0:
        return z
    row = lax.broadcasted_iota(jnp.int32, z.shape, 0)
    return jnp.where(row >= k, pltpu.roll(z, k, 0), 0.0)


def _shift_up(z, k):
    if k == 0:
        return z
    n = z.shape[0]
    row = lax.broadcasted_iota(jnp.int32, z.shape, 0)
    return jnp.where(row < n - k, pltpu.roll(z, n - k, 0), 0.0)


def _matmul(a, b, *, mode, name, res=None, a_parts=1, b_parts=1, out_parts=1, out_dtype=F32):
    def dims2(x, parts):
        if parts == 1:
            return x.shape
        assert x.shape[0] == parts
        return (x.shape[1], x.shape[2] * parts)

    ash, bsh = dims2(a, a_parts), dims2(b, b_parts)
    if mode == "nn":
        (M, K), (K2, N) = ash, bsh
        dn = (((1,), (0,)), ((), ()))
    elif mode == "nt":
        (M, K), (N, K2) = ash, bsh
        dn = (((1,), (1,)), ((), ()))
    else:
        (K, M), (K2, N) = ash, bsh
        dn = (((0,), (0,)), ((), ()))
    assert K == K2, (ash, bsh, mode)
    tm = _pick(M, (512, 256, 128, 64, 32, 16, 8))
    n_unit = N // max(out_parts, b_parts if mode != "nt" else 1)
    k_unit = K // max(a_parts if mode != "tn" else 1, b_parts if mode == "nt" else 1)
    tn, tk = min(
        ((n, k) for n in (2048, 1792, 1024, 896, 768, 512, 384, 256, 128) if n_unit % n == 0
         for k in (2048, 1792, 1024, 896, 512, 256, 128) if k_unit % k == 0
         if 2 * (tm * k * a.dtype.itemsize + k * n * b.dtype.itemsize + 2 * tm * n * 4) + tm * n * 4 <= MM_VMEM),
        key=lambda nk_: (-nk_[0] * nk_[1], -nk_[1]))
    nk = K // tk
    grid = (M // tm, N // tn, nk)

    def spec(parts, rows_are, cols_are, tr, tc, width):
        per = width // parts // tc
        if parts == 1:
            return pl.BlockSpec((tr, tc), lambda i, j, k: ((i, j, k)[rows_are], (i, j, k)[cols_are]))
        return pl.BlockSpec((None, tr, tc), lambda i, j, k: ((i, j, k)[cols_are] // per, (i, j, k)[rows_are],
                                                             (i, j, k)[cols_are] % per))

    if mode == "nn":
        a_spec = spec(a_parts, 0, 2, tm, tk, K)
        b_spec = spec(b_parts, 2, 1, tk, tn, N)
    elif mode == "nt":
        a_spec = spec(a_parts, 0, 2, tm, tk, K)
        b_spec = spec(b_parts, 1, 2, tn, tk, K)
    else:
        a_spec = spec(a_parts, 2, 0, tk, tm, M)
        b_spec = spec(b_parts, 2, 1, tk, tn, N)
    o_spec = spec(out_parts, 0, 1, tm, tn, N)
    in_specs = [a_spec, b_spec]
    operands = [a, b]
    if res is not None:
        in_specs.append(pl.BlockSpec((tm, tn), lambda i, j, k: (i, j)))
        operands.append(res)

    def finish(refs, r):
        if res is not None:
            r = refs[2][...] + r
        refs[-2 if nk > 1 else -1][...] = r.astype(out_dtype)

    def body(*refs):
        part = lax.dot_general(_bf(refs[0][...]), _bf(refs[1][...]), dn, preferred_element_type=F32)
        if nk == 1:
            finish(refs, part)
            return
        acc_ref = refs[-1]
        k = pl.program_id(2)

        @pl.when(k == 0)
        def _():
            acc_ref[...] = part

        @pl.when(jnp.logical_and(k > 0, k < nk - 1))
        def _():
            acc_ref[...] += part

        @pl.when(k == nk - 1)
        def _():
            finish(refs, acc_ref[...] + part)

    out_shape = (M, N) if out_parts == 1 else (out_parts, M, N // out_parts)
    return pl.pallas_call(
        body, name=name, grid=grid, in_specs=in_specs, out_specs=o_spec,
        out_shape=jax.ShapeDtypeStruct(out_shape, out_dtype),
        scratch_shapes=[pltpu.VMEM((tm, tn), F32)] if nk > 1 else [],
        compiler_params=pltpu.CompilerParams(dimension_semantics=("parallel", "parallel", "arbitrary"),
                                             vmem_limit_bytes=VMEM_BIG),
    )(*operands)


def _rmsnorm_fwd(x, g, *, name):
    T, D = x.shape
    tm = _pick(T, (512, 256, 128, 64, 32, 16))

    def body(x_ref, g_ref, h_ref):
        xv = x_ref[...]
        r = lax.rsqrt(jnp.mean(xv * xv, axis=-1, keepdims=True) + RMS_EPS)
        h_ref[...] = ((xv * r) * g_ref[...]).astype(BF16)

    return pl.pallas_call(
        body, name=name, grid=(T // tm,),
        in_specs=[pl.BlockSpec((tm, D), lambda i: (i, 0)), pl.BlockSpec((1, D), lambda i: (0, 0))],
        out_specs=pl.BlockSpec((tm, D), lambda i: (i, 0)),
        out_shape=jax.ShapeDtypeStruct((T, D), BF16),
    )(x, g.reshape(1, D))


def _rmsnorm_bwd(x, g, dh, dx_in, *, name):
    T, D = x.shape
    tm = _pick(T, (512, 256, 128, 64, 32, 16))

    def body(x_ref, g_ref, dh_ref, dxin_ref, dx_ref, dg_ref):
        @pl.when(pl.program_id(0) == 0)
        def _():
            dg_ref[...] = jnp.zeros_like(dg_ref)

        xv = x_ref[...]
        r = lax.rsqrt(jnp.mean(xv * xv, axis=-1, keepdims=True) + RMS_EPS)
        xh = xv * r
        dh_v = dh_ref[...]
        dxh = dh_v * g_ref[...]
        dx_ref[...] = dxin_ref[...] + r * (dxh - xh * jnp.mean(dxh * xh, axis=-1, keepdims=True))
        dg_ref[...] += jnp.sum(dh_v * xh, axis=0, keepdims=True)

    row = pl.BlockSpec((tm, D), lambda i: (i, 0))
    vec = pl.BlockSpec((1, D), lambda i: (0, 0))
    return pl.pallas_call(
        body, name=name, grid=(T // tm,),
        in_specs=[row, vec, row, row], out_specs=[row, vec],
        out_shape=[jax.ShapeDtypeStruct((T, D), F32), jax.ShapeDtypeStruct((1, D), F32)],
        compiler_params=pltpu.CompilerParams(dimension_semantics=("arbitrary",)),
    )(x, g.reshape(1, D), dh, dx_in)


def _loss_head(y, target, *, name):
    T, D = y.shape
    tm = _pick(T, (512, 256, 128, 64, 32, 16))

    def body(y_ref, t_ref, dy_ref, l_ref):
        @pl.when(pl.program_id(0) == 0)
        def _():
            l_ref[...] = jnp.zeros_like(l_ref)

        err = y_ref[...] - t_ref[...]
        dy_ref[...] = err * (1.0 / D)
        l_ref[...] += 0.5 * jnp.sum(jnp.mean(err * err, axis=-1, keepdims=True), axis=0, keepdims=True)

    row = pl.BlockSpec((tm, D), lambda i: (i, 0))
    return pl.pallas_call(
        body, name=name, grid=(T // tm,),
        in_specs=[row, row], out_specs=[row, pl.BlockSpec((1, 1), lambda i: (0, 0))],
        out_shape=[jax.ShapeDtypeStruct((T, D), F32), jax.ShapeDtypeStruct((1, 1), F32)],
        compiler_params=pltpu.CompilerParams(dimension_semantics=("arbitrary",)),
    )(y, target)


def _sc_mid_fwd(p3, conv_w, *, name):
    _, T, W = p3.shape
    K = conv_w.shape[0]
    cw = LANES

    def body(p_ref, w_ref, o_ref):
        z = p_ref[1] * p_ref[2]
        cv = sum(w_ref[i:i + 1, :] * _shift_down(z, K - 1 - i) for i in range(K))
        o_ref[...] = ((p_ref[0] * cv) * _silu(p_ref[3])).astype(BF16)

    return pl.pallas_call(
        body, name=name, grid=(W // cw,),
        in_specs=[pl.BlockSpec((4, T, cw), lambda j: (0, 0, j)), pl.BlockSpec((K, cw), lambda j: (0, j))],
        out_specs=pl.BlockSpec((T, cw), lambda j: (0, j)),
        out_shape=jax.ShapeDtypeStruct((T, W), BF16),
        compiler_params=pltpu.CompilerParams(dimension_semantics=("parallel",), vmem_limit_bytes=VMEM_BIG),
    )(p3, conv_w)


def _sc_mid_bwd(p3, conv_w, do, *, name):
    _, T, W = p3.shape
    K = conv_w.shape[0]
    cw = LANES

    def body(p_ref, w_ref, do_ref, dp_ref, dw_ref):
        b, c, u, gate = p_ref[0], p_ref[1], p_ref[2], p_ref[3]
        z = c * u
        zs = [_shift_down(z, K - 1 - i) for i in range(K)]
        cv = sum(w_ref[i:i + 1, :] * zs[i] for i in range(K))
        y = b * cv
        dov = do_ref[...]
        dy = dov * _silu(gate)
        dp_ref[3] = dov * y * _dsilu(gate)
        dp_ref[0] = dy * cv
        dcv = dy * b
        dz = sum(w_ref[i:i + 1, :] * _shift_up(dcv, K - 1 - i) for i in range(K))
        dp_ref[1] = dz * u
        dp_ref[2] = dz * c
        for i in range(K):
            dw_ref[i:i + 1, :] = jnp.sum(dcv * zs[i], axis=0, keepdims=True)

    return pl.pallas_call(
        body, name=name, grid=(W // cw,),
        in_specs=[pl.BlockSpec((4, T, cw), lambda j: (0, 0, j)), pl.BlockSpec((K, cw), lambda j: (0, j)),
                  pl.BlockSpec((T, cw), lambda j: (0, j))],
        out_specs=[pl.BlockSpec((4, T, cw), lambda j: (0, 0, j)), pl.BlockSpec((K, cw), lambda j: (0, j))],
        out_shape=[jax.ShapeDtypeStruct((4, T, W), F32), jax.ShapeDtypeStruct((K, W), F32)],
        compiler_params=pltpu.CompilerParams(dimension_semantics=("parallel",), vmem_limit_bytes=VMEM_BIG),
    )(p3, conv_w, do)


def _sc_layer_fwd(x, ng, w_in, conv_w, w_out, tag):
    h = _rmsnorm_fwd(x, ng, name=f"{tag}_norm")
    p3 = _matmul(h, w_in, mode="nn", b_parts=4, out_parts=4, name=f"{tag}_inproj")
    og = _sc_mid_fwd(p3, conv_w, name=f"{tag}_mid")
    x_new = _matmul(og, w_out, mode="nn", res=x, name=f"{tag}_outproj")
    return x_new, (h, p3, og)


def _sc_layer_bwd(dx, x, ng, w_in, conv_w, w_out, saved, tag):
    h, p3, og = saved
    d_wout = _matmul(og, dx, mode="tn", out_dtype=BF16, name=f"{tag}_dwout")
    dog = _matmul(dx, w_out, mode="nt", name=f"{tag}_dog")
    dp3, dconv = _sc_mid_bwd(p3, conv_w, dog, name=f"{tag}_midbwd")
    d_win = _matmul(h, dp3, mode="tn", b_parts=4, out_parts=4, out_dtype=BF16, name=f"{tag}_dwin")
    dh = _matmul(dp3, w_in, mode="nt", a_parts=4, b_parts=4, name=f"{tag}_dh")
    dx_prev, dng = _rmsnorm_bwd(x, ng, dh, dx, name=f"{tag}_normbwd")
    return dx_prev, dng, d_win, dconv, d_wout


SB_BQ = 256
SB_BK = 256
SB_ROWS = 512


def _sb_half_mask():
    return lax.broadcasted_iota(jnp.int32, (1, LANES), 1) < SB_DH


def _sb_headnorm(x, g, lo):
    x2 = x * x
    s_lo = jnp.sum(jnp.where(lo, x2, 0.0), axis=-1, keepdims=True)
    s_hi = jnp.sum(jnp.where(lo, 0.0, x2), axis=-1, keepdims=True)
    r = lax.rsqrt(jnp.where(lo, s_lo, s_hi) * (1.0 / SB_DH) + RMS_EPS)
    xh = x * r
    return xh * g, xh, r


def _dot_x2_l(a_l, b_exact_bf16):
    his = [_bf(a) for a in a_l]
    mids = [_bf(a - h.astype(F32)) for a, h in zip(a_l, his)]
    f = lambda p: jnp.dot(p, b_exact_bf16, preferred_element_type=F32)
    return [x + y for x, y in zip([f(h) for h in his], [f(m) for m in mids])]


def _sb_stack(xb, lo):
    zero = jnp.zeros_like(xb)
    return jnp.concatenate([jnp.where(lo, xb, zero), jnp.where(lo, zero, xb)], axis=0)


def _sb_rel(bq, bk):
    row = lax.broadcasted_iota(jnp.int32, (2 * bq, bk), 0)
    col = lax.broadcasted_iota(jnp.int32, (2 * bq, bk), 1)
    return col - jnp.where(row >= bq, row - bq, row)


def _sb_tile(qm, kb, valid):
    z = lax.dot_general(qm, kb, (((1,), (1,)), ((), ())), preferred_element_type=F32)
    sp = _softplus(z)
    return z - sp, (-sp if valid is None else jnp.where(valid, -sp, 0.0))


def _sb_attn_fwd(p3, gq2, gk2, *, name):
    _, T, W = p3.shape
    bq, bk = min(SB_BQ, T), min(SB_BK, T)
    rows = min(SB_ROWS, T)
    scale = SB_DH ** -0.5

    def body(p_ref, gq_ref, gk_ref, og_ref, o_ref, ls_ref, qn_ref, kn_ref, v_ref):
        lo = _sb_half_mask()

        def prologue(i, c):
            r0 = pl.multiple_of(i * rows, rows)
            sl = pl.ds(r0, rows)
            qn_ref[sl, :] = (_sb_headnorm(p_ref[0, sl, :], gq_ref[...], lo)[0] * scale).astype(BF16)
            kn_ref[sl, :] = _sb_headnorm(p_ref[1, sl, :], gk_ref[...], lo)[0].astype(BF16)
            v_ref[sl, :] = p_ref[2, sl, :].astype(BF16)
            return c

        lax.fori_loop(0, T // rows, prologue, 0)

        rel = _sb_rel(bq, bk)
        tri = (lax.broadcasted_iota(jnp.int32, (bk, bk), 0)
               > lax.broadcasted_iota(jnp.int32, (bk, bk), 1)).astype(BF16)

        def qblock(qi, c):
            q0 = pl.multiple_of(qi * bq, bq)
            qm = _sb_stack(qn_ref[pl.ds(q0, bq), :], lo)
            nkb = (q0 + bq - 1) // bk + 1

            def tiles(k0s, carry, valid):
                o_acc, a_carry = carry
                sc = [_sb_tile(qm, kn_ref[pl.ds(k0, bk), :], valid) for k0 in k0s]
                later = _dot_x2_l([log1m for _, log1m in sc], tri)
                for (logsig, log1m), lat, k0 in zip(sc, later, k0s):
                    wts = jnp.exp(logsig + (lat + a_carry))
                    if valid is not None:
                        wts = jnp.where(valid, wts, 0.0)
                    o_acc = o_acc + jnp.dot(_bf(wts), v_ref[pl.ds(k0, bk), :], preferred_element_type=F32)
                    a_carry = a_carry + jnp.sum(log1m, axis=-1, keepdims=True)
                return o_acc, a_carry

            blk0 = lambda j: pl.multiple_of(j * bk, bk)
            k_last = blk0(nkb - 1)
            cr = tiles([k_last], (jnp.zeros((2 * bq, LANES), F32), jnp.zeros((2 * bq, 1), F32)), rel < q0 - k_last)
            cr = lax.fori_loop(0, (nkb - 1) // 2,
                               lambda t, cr: tiles([blk0(nkb - 2 - 2 * t), blk0(nkb - 3 - 2 * t)], cr, None), cr)
            o2, t2 = lax.fori_loop(0, (nkb - 1) % 2, lambda t, cr: tiles([blk0(0)], cr, None), cr)
            o = jnp.where(lo, o2[:bq], o2[bq:])
            o_ref[pl.ds(q0, bq), :] = o
            ls_ref[pl.ds(q0, bq), :] = jnp.where(lo, t2[:bq], t2[bq:])
            og_ref[pl.ds(q0, bq), :] = (o * _silu(p_ref[3, pl.ds(q0, bq), :])).astype(BF16)
            return c

        lax.fori_loop(0, T // bq, qblock, 0)

    colblk = pl.BlockSpec((T, LANES), lambda j: (0, j))
    vec = pl.BlockSpec((1, LANES), lambda j: (0, 0))
    return pl.pallas_call(
        body, name=name, grid=(W // LANES,),
        in_specs=[pl.BlockSpec((4, T, LANES), lambda j: (0, 0, j)), vec, vec],
        out_specs=[colblk, colblk, colblk],
        out_shape=[jax.ShapeDtypeStruct((T, W), BF16), jax.ShapeDtypeStruct((T, W), F32),
                   jax.ShapeDtypeStruct((T, W), F32)],
        scratch_shapes=[pltpu.VMEM((T, LANES), BF16)] * 3,
        compiler_params=pltpu.CompilerParams(dimension_semantics=("parallel",), vmem_limit_bytes=VMEM_BIG),
    )(p3, gq2, gk2)


def _sb_attn_bwd(p3, gq2, gk2, o, lsum, dog, *, name):
    _, T, W = p3.shape
    bq, bk = min(SB_BQ, T), min(SB_BK, T)
    rows = min(SB_ROWS, T)
    scale = SB_DH ** -0.5

    def body(p_ref, gq_ref, gk_ref, o_ref, ls_ref, dog_ref, dp_ref, dgq_ref, dgk_ref,
             qn_ref, kn_ref, v_ref, do_ref):
        lo = _sb_half_mask()

        def prologue(i, c):
            r0 = pl.multiple_of(i * rows, rows)
            sl = pl.ds(r0, rows)
            qn_ref[sl, :] = (_sb_headnorm(p_ref[0, sl, :], gq_ref[...], lo)[0] * scale).astype(BF16)
            kn_ref[sl, :] = _sb_headnorm(p_ref[1, sl, :], gk_ref[...], lo)[0].astype(BF16)
            v_ref[sl, :] = p_ref[2, sl, :].astype(BF16)
            gate = p_ref[3, sl, :]
            dogv = dog_ref[sl, :]
            dp_ref[3, sl, :] = dogv * o_ref[sl, :] * _dsilu(gate)
            do_ref[sl, :] = (dogv * _silu(gate)).astype(BF16)
            zero = jnp.zeros((rows, LANES), F32)
            dp_ref[0, sl, :] = zero
            dp_ref[1, sl, :] = zero
            dp_ref[2, sl, :] = zero
            return c

        lax.fori_loop(0, T // rows, prologue, 0)

        rel = _sb_rel(bq, bk)
        rj = lax.broadcasted_iota(jnp.int32, (bk, bk), 0)
        cj = lax.broadcasted_iota(jnp.int32, (bk, bk), 1)
        upto = (rj <= cj).astype(BF16)
        before_m = (rj < cj).astype(BF16)

        def qblock(qi, c):
            q0 = pl.multiple_of(qi * bq, bq)
            qm = _sb_stack(qn_ref[pl.ds(q0, bq), :], lo)
            dom = _sb_stack(do_ref[pl.ds(q0, bq), :], lo)
            lsb = ls_ref[pl.ds(q0, bq), :]
            total = jnp.concatenate([lsb[:, 0:1], lsb[:, SB_DH:SB_DH + 1]], axis=0)
            nkb = (q0 + bq - 1) // bk + 1

            def tiles(k0s, carry, valid):
                dq_acc, a_pre, r_pre = carry
                kss = [pl.ds(k0, bk) for k0 in k0s]
                kbs = [kn_ref[ks, :] for ks in kss]
                sc = [_sb_tile(qm, kb, valid) for kb in kbs]
                dws = [lax.dot_general(dom, v_ref[ks, :], _NT, preferred_element_type=F32) for ks in kss]
                upto_l = _dot_x2_l([log1m for _, log1m in sc], upto)
                wts_l = []
                for (logsig, log1m), up in zip(sc, upto_l):
                    wts = jnp.exp(logsig + ((total - a_pre) - up))
                    wts_l.append(wts if valid is None else jnp.where(valid, wts, 0.0))
                    a_pre = a_pre + jnp.sum(log1m, axis=-1, keepdims=True)
                ee_l = [dw * wts for dw, wts in zip(dws, wts_l)]
                before_l = _dot_x2_l(ee_l, before_m)
                for (logsig, _), ks, kb, wts, ee, bef in zip(sc, kss, kbs, wts_l, ee_l, before_l):
                    beta = jnp.exp(logsig)
                    dz = ee * (1.0 - beta) - beta * (r_pre + bef)
                    if valid is not None:
                        dz = jnp.where(valid, dz, 0.0)
                    dzb = _bf(dz)
                    dq_acc = dq_acc + jnp.dot(dzb, kb, preferred_element_type=F32)
                    dp_ref[1, ks, :] += lax.dot_general(dzb, qm, _TN, preferred_element_type=F32)
                    dp_ref[2, ks, :] += lax.dot_general(_bf(wts), dom, _TN, preferred_element_type=F32)
                    r_pre = r_pre + jnp.sum(ee, axis=-1, keepdims=True)
                return dq_acc, a_pre, r_pre

            blk0 = lambda j: pl.multiple_of(j * bk, bk)
            cr = (jnp.zeros((2 * bq, LANES), F32), jnp.zeros((2 * bq, 1), F32), jnp.zeros((2 * bq, 1), F32))
            cr = lax.fori_loop(0, (nkb - 1) // 2, lambda t, cr: tiles([blk0(2 * t), blk0(2 * t + 1)], cr, None), cr)
            cr = lax.fori_loop(0, (nkb - 1) % 2, lambda t, cr: tiles([blk0(nkb - 2)], cr, None), cr)
            k_last = blk0(nkb - 1)
            dq2, _, _ = tiles([k_last], cr, rel < q0 - k_last)
            dp_ref[0, pl.ds(q0, bq), :] = jnp.where(lo, dq2[:bq], dq2[bq:]) * scale
            return c

        lax.fori_loop(0, T // bq, qblock, 0)

        dgq_ref[...] = jnp.zeros_like(dgq_ref)
        dgk_ref[...] = jnp.zeros_like(dgk_ref)

        def epilogue(i, c):
            r0 = pl.multiple_of(i * rows, rows)
            sl = pl.ds(r0, rows)
            for part, g_ref, dg_ref in ((0, gq_ref, dgq_ref), (1, gk_ref, dgk_ref)):
                _, xh, r = _sb_headnorm(p_ref[part, sl, :], g_ref[...], lo)
                dn = dp_ref[part, sl, :]
                dxh = dn * g_ref[...]
                prod = dxh * xh
                m_lo = jnp.sum(jnp.where(lo, prod, 0.0), axis=-1, keepdims=True)
                m_hi = jnp.sum(jnp.where(lo, 0.0, prod), axis=-1, keepdims=True)
                m = jnp.where(lo, m_lo, m_hi) * (1.0 / SB_DH)
                dp_ref[part, sl, :] = r * (dxh - xh * m)
                dg_ref[...] += jnp.sum(dn * xh, axis=0, keepdims=True)
            return c

        lax.fori_loop(0, T // rows, epilogue, 0)

    colblk = pl.BlockSpec((T, LANES), lambda j: (0, j))
    vec = pl.BlockSpec((1, LANES), lambda j: (0, 0))
    part = pl.BlockSpec((4, T, LANES), lambda j: (0, 0, j))
    gvec = pl.BlockSpec((None, 1, LANES), lambda j: (j, 0, 0))
    npair = W // LANES
    return pl.pallas_call(
        body, name=name, grid=(npair,),
        in_specs=[part, vec, vec, colblk, colblk, colblk],
        out_specs=[part, gvec, gvec],
        out_shape=[jax.ShapeDtypeStruct((4, T, W), F32), jax.ShapeDtypeStruct((npair, 1, LANES), F32),
                   jax.ShapeDtypeStruct((npair, 1, LANES), F32)],
        scratch_shapes=[pltpu.VMEM((T, LANES), BF16)] * 4,
        compiler_params=pltpu.CompilerParams(dimension_semantics=("parallel",), vmem_limit_bytes=VMEM_BIG),
    )(p3, gq2, gk2, o, lsum, dog)


_NN = (((1,), (0,)), ((), ()))
_NT = (((1,), (1,)), ((), ()))
_TN = (((0,), (0,)), ((), ()))
DN_TB = 512
DN_HEADS_PER_STEP = 2
DN_AB_COL = (DN_CONV_W + DN_V_W) // LANES


def _dn_conv(x, w_ref):
    k = w_ref.shape[0]
    return sum(w_ref[i:i + 1, :] * _shift_down(x, k - 1 - i) for i in range(k))


def _dn_prep_fwd(p, conv_w, *, name):
    T = p.shape[0]
    cw = conv_w.shape[1]
    n_qk = 2 * DN_QK_W // LANES

    def body(p_ref, w_ref, o_ref):
        s = _silu(_dn_conv(p_ref[...], w_ref))
        r = lax.rsqrt(jnp.sum(s * s, axis=-1, keepdims=True) + L2_EPS)
        o_ref[...] = jnp.where(pl.program_id(0) < n_qk, s * r, s)

    colblk = pl.BlockSpec((T, LANES), lambda j: (0, j))
    return pl.pallas_call(
        body, name=name, grid=(cw // LANES,),
        in_specs=[colblk, pl.BlockSpec((DN_CONV, LANES), lambda j: (0, j))],
        out_specs=colblk, out_shape=jax.ShapeDtypeStruct((T, cw), F32),
        compiler_params=pltpu.CompilerParams(dimension_semantics=("parallel",), vmem_limit_bytes=VMEM_BIG),
    )(p, conv_w)


def _dn_chunk_tri(rows, upper):
    r = lax.broadcasted_iota(jnp.int32, (rows, rows), 0)
    c = lax.broadcasted_iota(jnp.int32, (rows, rows), 1)
    same = (r // DN_CHUNK) == (c // DN_CHUNK)
    return jnp.logical_and(same, (c >= r) if upper else (c <= r)).astype(BF16)


def _dn_lane_rows(a_log, dt_bias):
    pad = lambda v: jnp.zeros((1, LANES), F32).at[0, :DN_HEADS].set(v)
    return pad(a_log), pad(dt_bias)


def _dn_ab_parts(blk, alog_row, dtb_row):
    lane = lax.broadcasted_iota(jnp.int32, (1, LANES), 1)
    is_a = lane < DN_HEADS
    is_b = jnp.logical_and(lane >= DN_HEADS, lane < 2 * DN_HEADS)
    a_arg = jnp.where(is_a, blk + dtb_row, 0.0)
    neg_exp = jnp.where(is_a, -jnp.exp(alog_row), 0.0)
    log_a = neg_exp * _softplus(a_arg)
    beta = jnp.where(is_b, _sigmoid(blk), 0.0)
    return is_a, is_b, a_arg, neg_exp, log_a, beta


def _dn_ab_fwd(p, alog_row, dtb_row, *, name):
    T = p.shape[0]
    rows = min(DN_TB, T)

    def body(p_ref, al_ref, dt_ref, o_ref):
        _, _, _, _, log_a, beta = _dn_ab_parts(p_ref[...], al_ref[...], dt_ref[...])
        hi, mid, lo_ = _split3(log_a)
        tri = _dn_chunk_tri(rows, upper=False)
        f = lambda q: jnp.dot(tri, q, preferred_element_type=F32)
        o_ref[...] = (f(hi) + f(mid) + f(lo_)) + beta

    blk = pl.BlockSpec((rows, LANES), lambda i: (i, DN_AB_COL))
    vec = pl.BlockSpec((1, LANES), lambda i: (0, 0))
    return pl.pallas_call(
        body, name=name, grid=(T // rows,), in_specs=[blk, vec, vec],
        out_specs=pl.BlockSpec((rows, LANES), lambda i: (i, 0)),
        out_shape=jax.ShapeDtypeStruct((T, LANES), F32),
        compiler_params=pltpu.CompilerParams(dimension_semantics=("parallel",)),
    )(p, alog_row, dtb_row)


def _hp_l(a_l, b_l, dims=_NN):
    sa = [_split3(a)[:2] for a in a_l]
    sb = [_split3(b)[:2] for b in b_l]
    f = lambda p, q: lax.dot_general(p, q, dims, preferred_element_type=F32)
    hh = [f(x[0], y[0]) for x, y in zip(sa, sb)]
    hm = [f(x[0], y[1]) for x, y in zip(sa, sb)]
    mh = [f(x[1], y[0]) for x, y in zip(sa, sb)]
    return [a + (b + c) for a, b, c in zip(hh, hm, mh)]


def _dn_local(qs, k, v, g, beta, nc):
    c = DN_CHUNK
    cut = lambda x: [x[i * c:(i + 1) * c] for i in range(nc)]
    row = lax.broadcasted_iota(jnp.int32, (c, c), 0)
    col = lax.broadcasted_iota(jnp.int32, (c, c), 1)
    eye, lower, strict = row == col, row >= col, row > col
    rowid = lax.broadcasted_iota(jnp.int32, (c, 1), 0)
    eg = jnp.exp(g)
    kb = k * beta
    rhs_k = kb * eg
    g_l, k_l, kb_l, qs_l = cut(g), cut(k), cut(kb), cut(qs)
    g_row_l = [jnp.sum(jnp.where(eye, x, 0.0), axis=0, keepdims=True) for x in g_l]
    dec_l = [jnp.where(lower, jnp.exp(jnp.where(lower, x - y, 0.0)), 0.0) for x, y in zip(g_l, g_row_l)]
    kk_l = [_dot_nt(a, b) for a, b in zip(kb_l, k_l)]
    qk_l = [_dot_nt(a, b) for a, b in zip(qs_l, k_l)]
    low_l = [jnp.where(strict, a * d, 0.0) for a, d in zip(kk_l, dec_l)]
    eye_f = eye.astype(F32)
    pw_l = [-x for x in low_l]
    inv_l = [eye_f + x for x in pw_l]
    for _ in range(int(math.log2(c)) - 1):
        pw_l = _hp_l(pw_l, pw_l)
        inv_l = [a + b for a, b in zip(inv_l, _hp_l(inv_l, pw_l))]
    u_l = [_dot(a, b) for a, b in zip(inv_l, cut(v * beta))]
    w_l = [_dot(a, b) for a, b in zip(inv_l, cut(rhs_k))]
    aqk_l = [jnp.where(lower, a * d, 0.0) for a, d in zip(qk_l, dec_l)]
    g_last_l = [jnp.sum(jnp.where(rowid == c - 1, x, 0.0), axis=0, keepdims=True) for x in g_l]
    ekd_l = [jnp.exp(a - b) for a, b in zip(g_last_l, g_l)]
    kd_l = [a * b for a, b in zip(k_l, ekd_l)]
    qd_l = cut(qs * eg)
    kw_l = [_dot_tn(a, b) for a, b in zip(kd_l, w_l)]
    qp_l = [q - _dot(a, w) for q, a, w in zip(qd_l, aqk_l, w_l)]
    return dict(eye=eye, lower=lower, strict=strict, dec=dec_l, k=k_l, kb=kb_l, qs=qs_l, low=low_l, inv=inv_l,
                eg=cut(eg), rhs_k=cut(rhs_k), u=u_l, w=w_l, aqk=aqk_l, g_last=g_last_l, qd=qd_l,
                ekd=ekd_l, kd=kd_l, kw=kw_l, qp=qp_l)


def _dn_head_cols(gb_blk, head):
    lane = lax.broadcasted_iota(jnp.int32, (1, LANES), 1)
    g = jnp.sum(jnp.where(lane == head, gb_blk, 0.0), axis=-1, keepdims=True)
    beta = jnp.sum(jnp.where(lane == head + DN_HEADS, gb_blk, 0.0), axis=-1, keepdims=True)
    return g, beta


def _halves_over_ici(s_refs, o_refs, send_sems, recv_sems, first, last):
    x, y, c = _mesh_pos()
    me = 2 * x + y
    chips = _other_chips(x, y)
    pairs = [(a, k) for a in range(len(s_refs)) for k in range(3)]

    def copy(a, k, slot):
        px, py = chips[k]
        return pltpu.make_async_remote_copy(
            src_ref=s_refs[a].at[c], dst_ref=o_refs[a].at[slot, c], send_sem=send_sems.at[3 * a + k],
            recv_sem=recv_sems.at[3 * a + k], device_id=(px, py, c), device_id_type=MESH)

    @pl.when(first)
    def _():
        for a, k in pairs:
            copy(a, k, me).start()

    @pl.when(last)
    def _():
        for a, k in pairs:
            px, py = chips[k]
            copy(a, k, 2 * px + py).wait_recv()
        for a, k in pairs:
            copy(a, k, me).wait_send()


def _dn_delta_fwd(qkv, gb, p, o_gain, *, name, send=()):
    T = qkv.shape[0]
    tb = min(DN_TB, T)
    nb, nc = T // tb, tb // DN_CHUNK
    H = DN_HEADS
    qscale = DN_DK ** -0.5
    ns = len(send)
    hp = DN_HEADS_PER_STEP

    def body(*refs):
        q_ref, k_ref, v_ref, gb_ref, gate_ref, gain_ref = refs[:6]
        o_ref, og_ref, st_ref = refs[6 + ns:9 + ns]
        s_ref = refs[9 + 2 * ns]
        pair, blk = pl.program_id(0), pl.program_id(1)
        if ns:
            _halves_over_ici(refs[6:6 + ns], refs[9 + ns:9 + 2 * ns], refs[10 + 2 * ns], refs[11 + 2 * ns],
                             jnp.logical_and(pair == 0, blk == 0),
                             jnp.logical_and(pair == H // hp - 1, blk == nb - 1))

        @pl.when(blk == 0)
        def _():
            s_ref[...] = jnp.zeros_like(s_ref)

        gbv = gb_ref[...]
        ts, ku, op = [], [], []
        for e in range(hp):
            qk_e, v_e = slice(e * DN_DK, (e + 1) * DN_DK), slice(e * DN_DV, (e + 1) * DN_DV)
            g, beta = _dn_head_cols(gbv, hp * pair + e)
            t = _dn_local(q_ref[:, qk_e] * qscale, k_ref[:, qk_e], v_ref[:, v_e], g, beta, nc)
            ts.append(t)
            ku.append([_dot_tn(a, b) for a, b in zip(t["kd"], t["u"])])
            op.append([_dot(a, b) for a, b in zip(t["aqk"], t["u"])])
        s32 = [s_ref[e] for e in range(hp)]
        s_l = [[] for _ in range(hp)]
        for i in range(nc):
            sb = [_bf(x) for x in s32]
            for e in range(hp):
                st_ref[e, i] = sb[e]
                s_l[e].append(sb[e])
            prod = [_dot(ts[e]["kw"][i], sb[e]) for e in range(hp)]
            s32 = [s32[e] * jnp.exp(ts[e]["g_last"][i]) - prod[e] + ku[e][i] for e in range(hp)]
        for e in range(hp):
            s_ref[e] = s32[e]
        o = jnp.concatenate(
            [jnp.concatenate([_dot(qp, sb) + x for qp, sb, x in zip(ts[e]["qp"], s_l[e], op[e])], axis=0)
             for e in range(hp)], axis=1)
        o_ref[...] = o
        gain = gain_ref[...]
        for e in range(hp):
            v_e = slice(e * DN_DV, (e + 1) * DN_DV)
            oe = o[:, v_e]
            r = lax.rsqrt(jnp.mean(oe * oe, axis=-1, keepdims=True) + RMS_EPS)
            og_ref[:, v_e] = (((oe * r) * gain) * _silu(gate_ref[:, v_e])).astype(BF16)

    qk = lambda col0: pl.BlockSpec((tb, hp * DN_DK), lambda h, i: (i, col0 // (hp * DN_DK) + h))
    vblk = lambda col0: pl.BlockSpec((tb, hp * DN_DV), lambda h, i: (i, col0 // (hp * DN_DV) + h))
    return pl.pallas_call(
        body, name=name, grid=(H // hp, nb),
        in_specs=[qk(0), qk(DN_QK_W), vblk(2 * DN_QK_W), pl.BlockSpec((tb, LANES), lambda h, i: (i, 0)),
                  vblk(DN_CONV_W), pl.BlockSpec((1, DN_DV), lambda h, i: (0, 0))] + [HBM] * ns,
        out_specs=[vblk(0), vblk(0), pl.BlockSpec((hp, nc, DN_DK, DN_DV), lambda h, i: (h, i, 0, 0))] + [HBM] * ns,
        out_shape=[jax.ShapeDtypeStruct((T, DN_V_W), F32), jax.ShapeDtypeStruct((T, DN_V_W), BF16),
                   jax.ShapeDtypeStruct((H, T // DN_CHUNK, DN_DK, DN_DV), BF16)]
        + [jax.ShapeDtypeStruct((N_CHIPS,) + a.shape, a.dtype) for a in send],
        scratch_shapes=[pltpu.VMEM((hp, DN_DK, DN_DV), F32)]
        + ([pltpu.SemaphoreType.DMA((3 * ns,)), pltpu.SemaphoreType.DMA((3 * ns,))] if ns else []),
        compiler_params=pltpu.CompilerParams(dimension_semantics=("arbitrary", "arbitrary")),
    )(qkv, qkv, qkv, gb, p, o_gain, *send)


def _blocks_over_ici(p_refs, o_refs, send_sems, recv_sems, first, last):
    x, y, c = _mesh_pos()
    me = 2 * x + y
    chips = _other_chips(x, y)
    pairs = [(a, k) for a in range(len(p_refs)) for k in range(3)]

    def copy(a, k, slot):
        px, py = chips[k]
        return pltpu.make_async_remote_copy(
            src_ref=p_refs[a].at[2 * px + py], dst_ref=o_refs[a].at[slot], send_sem=send_sems.at[3 * a + k],
            recv_sem=recv_sems.at[3 * a + k], device_id=(px, py, c), device_id_type=MESH)

    @pl.when(first)
    def _():
        for a, k in pairs:
            copy(a, k, me).start()

    @pl.when(last)
    def _():
        for a, k in pairs:
            px, py = chips[k]
            copy(a, k, 2 * px + py).wait_recv()
        for a, k in pairs:
            copy(a, k, me).wait_send()


def _dn_delta_bwd(qkv, gb, p, o_gain, o, states, dog, *, name, send=()):
    T = qkv.shape[0]
    tb = min(DN_TB, T)
    nb, nc = T // tb, tb // DN_CHUNK
    H = DN_HEADS
    qscale = DN_DK ** -0.5
    ns = len(send)
    hp = DN_HEADS_PER_STEP

    def body(*refs):
        q_ref, k_ref, v_ref, gb_ref, gate_ref, gain_ref, o_ref, st_ref, dog_ref = refs[:9]
        dq_ref, dk_ref, dv_ref, dgate_ref, dgb_ref, dgain_ref = refs[9 + ns:15 + ns]
        ds_ref = refs[15 + 2 * ns]
        pair, blk = pl.program_id(0), pl.program_id(1)
        first = jnp.logical_and(pair == 0, blk == 0)
        if ns:
            _blocks_over_ici(refs[9:9 + ns], refs[15 + ns:15 + 2 * ns], refs[16 + 2 * ns], refs[17 + 2 * ns],
                             first, jnp.logical_and(pair == H // hp - 1, blk == nb - 1))

        @pl.when(blk == 0)
        def _():
            ds_ref[...] = jnp.zeros_like(ds_ref)

        @pl.when(first)
        def _():
            dgain_ref[...] = jnp.zeros_like(dgain_ref)

        lane = lax.broadcasted_iota(jnp.int32, (1, LANES), 1)
        c = DN_CHUNK
        cut = lambda x: [x[i * c:(i + 1) * c] for i in range(nc)]
        cat = lambda xs: jnp.concatenate(xs, axis=0)
        rsum = lambda x: jnp.sum(x, axis=-1, keepdims=True)
        gbv, gain = gb_ref[...], gain_ref[...]

        def before_chain(e):
            qk_e, v_e = slice(e * DN_DK, (e + 1) * DN_DK), slice(e * DN_DV, (e + 1) * DN_DV)
            g, beta = _dn_head_cols(gbv, hp * pair + e)
            ov, gate, dogv = o_ref[:, v_e], gate_ref[:, v_e], dog_ref[:, v_e]
            r = lax.rsqrt(jnp.mean(ov * ov, axis=-1, keepdims=True) + RMS_EPS)
            oh = ov * r
            dnrm = dogv * _silu(gate)
            dgate_ref[:, v_e] = dogv * (oh * gain) * _dsilu(gate)
            doh = dnrm * gain
            do_l = cut(r * (doh - oh * jnp.mean(doh * oh, axis=-1, keepdims=True)))
            dgain_ref[...] += jnp.sum(dnrm * oh, axis=0, keepdims=True)
            k, v = k_ref[:, qk_e], v_ref[:, v_e]
            t = _dn_local(q_ref[:, qk_e] * qscale, k, v, g, beta, nc)
            s_l = [st_ref[e, i] for i in range(nc)]
            vn_l = [u - _dot(w, sb) for u, w, sb in zip(t["u"], t["w"], s_l)]
            return dict(
                t=t, beta=beta, v=v, s=s_l, vn=vn_l, egl=[jnp.exp(x) for x in t["g_last"]],
                dqd=[_dot_nt(a, sb) for a, sb in zip(do_l, s_l)], daqk=[_dot_nt(a, b) for a, b in zip(do_l, vn_l)],
                aqk_do=[_dot_tn(a, b) for a, b in zip(t["aqk"], do_l)],
                qp_do=[_dot_tn(a, b) for a, b in zip(t["qp"], do_l)])

        hs = [before_chain(e) for e in range(hp)]
        ds = [ds_ref[e] for e in range(hp)]
        ds_l = [[None] * nc for _ in range(hp)]
        for i in reversed(range(nc)):
            for e in range(hp):
                ds_l[e][i] = ds[e]
            prod = [_dot_tn(hs[e]["t"]["kw"][i], ds[e]) for e in range(hp)]
            ds = [ds[e] * hs[e]["egl"][i] - prod[e] + hs[e]["qp_do"][i] for e in range(hp)]
        for e in range(hp):
            ds_ref[e] = ds[e]

        def after_chain(e):
            hd, t = hs[e], hs[e]["t"]
            lower, strict, eye = t["lower"], t["strict"], t["eye"]
            s_l, vn_l, dqd_l, daqk_l, egl_l, beta, v = (hd["s"], hd["vn"], hd["dqd"], hd["daqk"], hd["egl"],
                                                         hd["beta"], hd["v"])
            dvn_l = [a + _dot(kd, d) for a, kd, d in zip(hd["aqk_do"], t["kd"], ds_l[e])]
            dkd_l = [_dot_nt(a, d) for a, d in zip(vn_l, ds_l[e])]
            dgl_l = [jnp.sum(rsum(d * sb.astype(F32)), axis=0, keepdims=True) * x
                     for d, sb, x in zip(ds_l[e], s_l, egl_l)]
            dw_l = [-_dot_nt(a, sb) for a, sb in zip(dvn_l, s_l)]
            dbv_l = [_dot_tn(a, b) for a, b in zip(t["inv"], dvn_l)]
            dbk_l = [_dot_tn(a, b) for a, b in zip(t["inv"], dw_l)]
            dlow_l = [-(_dot_nt(a, b) + _dot_nt(x, y)) for a, b, x, y in zip(dbv_l, t["u"], dbk_l, t["w"])]
            m_l = [jnp.where(strict, a * d, 0.0) for a, d in zip(dlow_l, t["dec"])]
            nmat_l = [jnp.where(lower, a * d, 0.0) for a, d in zip(daqk_l, t["dec"])]
            dkb_l = [_dot(m, kk) + b * x for m, kk, b, x in zip(m_l, t["k"], dbk_l, t["eg"])]
            dqs_l = [_dot(n, kk) + a * x for n, kk, a, x in zip(nmat_l, t["k"], dqd_l, t["eg"])]
            dk1_l = [_dot_tn(m, kb) for m, kb in zip(m_l, t["kb"])]
            dk2_l = [_dot_tn(n, q) for n, q in zip(nmat_l, t["qs"])]
            beta_l, v_l = cut(beta), cut(v)
            rowid = lax.broadcasted_iota(jnp.int32, (c, 1), 0)
            dk_l, dg_l, dbeta_l = [], [], []
            for i in range(nc):
                dk_l.append(dk1_l[i] + dk2_l[i] + dkd_l[i] * t["ekd"][i] + dkb_l[i] * beta_l[i])
                gmat = jnp.where(strict, dlow_l[i] * t["low"][i], 0.0) + daqk_l[i] * t["aqk"][i]
                s_kd = rsum(dkd_l[i] * t["kd"][i])
                dg = (rsum(gmat) + rsum(dqd_l[i] * t["qd"][i]) - s_kd + rsum(dbk_l[i] * t["rhs_k"][i]))
                dg_row = -jnp.sum(gmat, axis=0, keepdims=True)
                dg = dg + rsum(jnp.where(eye, dg_row, 0.0))
                dgl = dgl_l[i] + jnp.sum(s_kd, axis=0, keepdims=True)
                dg_l.append(dg + jnp.where(rowid == c - 1, dgl, 0.0))
                dbeta_l.append(rsum(dbv_l[i] * v_l[i]) + rsum(dkb_l[i] * t["k"][i]))
            head = hp * pair + e
            dgb = (jnp.where(lane == head, cat(dg_l), 0.0) + jnp.where(lane == head + DN_HEADS, cat(dbeta_l), 0.0))
            return cat(dqs_l) * qscale, cat(dk_l), cat(dbv_l) * beta, dgb

        for e in range(hp):
            dq, dk, dv, dgb = after_chain(e)
            dq_ref[:, e * DN_DK:(e + 1) * DN_DK] = dq
            dk_ref[:, e * DN_DK:(e + 1) * DN_DK] = dk
            dv_ref[:, e * DN_DV:(e + 1) * DN_DV] = dv
            dgb_ref[e] = dgb

    rev = lambda i: nb - 1 - i
    qk = lambda col0: pl.BlockSpec((tb, hp * DN_DK), lambda h, i: (rev(i), col0 // (hp * DN_DK) + h))
    vblk = lambda col0: pl.BlockSpec((tb, hp * DN_DV), lambda h, i: (rev(i), col0 // (hp * DN_DV) + h))
    gain_spec = pl.BlockSpec((1, DN_DV), lambda h, i: (0, 0))
    return pl.pallas_call(
        body, name=name, grid=(H // hp, nb),
        in_specs=[qk(0), qk(DN_QK_W), vblk(2 * DN_QK_W), pl.BlockSpec((tb, LANES), lambda h, i: (rev(i), 0)),
                  vblk(DN_CONV_W), gain_spec, vblk(0),
                  pl.BlockSpec((hp, nc, DN_DK, DN_DV), lambda h, i: (h, rev(i), 0, 0)), vblk(0)] + [HBM] * ns,
        out_specs=[qk(0), qk(0), vblk(0), vblk(DN_CONV_W),
                   pl.BlockSpec((hp, tb, LANES), lambda h, i: (h, rev(i), 0)), gain_spec] + [HBM] * ns,
        out_shape=[jax.ShapeDtypeStruct((T, DN_QK_W), F32), jax.ShapeDtypeStruct((T, DN_QK_W), F32),
                   jax.ShapeDtypeStruct((T, DN_V_W), F32), jax.ShapeDtypeStruct((T, DN_IN_PAD), F32),
                   jax.ShapeDtypeStruct((H, T, LANES), F32), jax.ShapeDtypeStruct((1, DN_DV), F32)]
        + [jax.ShapeDtypeStruct(a.shape, a.dtype) for a in send],
        scratch_shapes=[pltpu.VMEM((hp, DN_DK, DN_DV), F32)]
        + ([pltpu.SemaphoreType.DMA((3 * ns,)), pltpu.SemaphoreType.DMA((3 * ns,))] if ns else []),
        compiler_params=pltpu.CompilerParams(dimension_semantics=("arbitrary", "arbitrary")),
    )(qkv, qkv, qkv, gb, p, o_gain, o, states, dog, *send)


def _dn_conv_bwd(p, conv_w, d, dp, *, first, normed, name):
    T, width = d.shape

    def body(p_ref, w_ref, d_ref, dp_in, dp_ref, dw_ref):
        del dp_in
        x = p_ref[...]
        ksz = w_ref.shape[0]
        xs = [_shift_down(x, ksz - 1 - i) for i in range(ksz)]
        xc = sum(w_ref[i:i + 1, :] * xs[i] for i in range(ksz))
        ds = d_ref[...]
        if normed:
            s = _silu(xc)
            r = lax.rsqrt(jnp.sum(s * s, axis=-1, keepdims=True) + L2_EPS)
            y = s * r
            ds = r * (ds - y * jnp.sum(ds * y, axis=-1, keepdims=True))
        dxc = ds * _dsilu(xc)
        dp_ref[...] = sum(w_ref[i:i + 1, :] * _shift_up(dxc, ksz - 1 - i) for i in range(ksz))
        for i in range(ksz):
            dw_ref[i:i + 1, :] = jnp.sum(dxc * xs[i], axis=0, keepdims=True)

    shifted = pl.BlockSpec((T, LANES), lambda j: (0, first + j))
    return pl.pallas_call(
        body, name=name, grid=(width // LANES,),
        in_specs=[shifted, pl.BlockSpec((DN_CONV, LANES), lambda j: (0, first + j)),
                  pl.BlockSpec((T, LANES), lambda j: (0, j)), pl.BlockSpec(memory_space=pl.ANY)],
        out_specs=[shifted, pl.BlockSpec((DN_CONV, LANES), lambda j: (0, j))],
        out_shape=[jax.ShapeDtypeStruct(dp.shape, F32), jax.ShapeDtypeStruct((DN_CONV, width), F32)],
        input_output_aliases={3: 0},
        compiler_params=pltpu.CompilerParams(dimension_semantics=("parallel",), vmem_limit_bytes=VMEM_BIG),
    )(p, conv_w, d, dp)


def _dn_ab_bwd(p, alog_row, dtb_row, dgb, dp, *, name):
    T = p.shape[0]
    rows = min(DN_TB, T)
    H = DN_HEADS

    def body(p_ref, al_ref, dt_ref, dgb_ref, dp_in, dp_ref, dal_ref, ddt_ref):
        del dp_in

        @pl.when(pl.program_id(0) == 0)
        def _():
            dal_ref[...] = jnp.zeros_like(dal_ref)
            ddt_ref[...] = jnp.zeros_like(ddt_ref)

        blk = p_ref[...]
        is_a, is_b, a_arg, neg_exp, log_a, beta = _dn_ab_parts(blk, al_ref[...], dt_ref[...])
        d = dgb_ref[0]
        for hh in range(1, H):
            d = d + dgb_ref[hh]
        hi, mid, lo_ = _split3(jnp.where(is_a, d, 0.0))
        tri = _dn_chunk_tri(rows, upper=True)
        f = lambda q: jnp.dot(tri, q, preferred_element_type=F32)
        dlog_a = f(hi) + f(mid) + f(lo_)
        da_in = dlog_a * neg_exp * _sigmoid(a_arg)
        db_in = jnp.where(is_b, d, 0.0) * beta * (1.0 - beta)
        dp_ref[...] = jnp.where(is_a, da_in, 0.0) + db_in
        dal_ref[...] += jnp.sum(dlog_a * log_a, axis=0, keepdims=True)
        ddt_ref[...] += jnp.sum(jnp.where(is_a, da_in, 0.0), axis=0, keepdims=True)

    blk = pl.BlockSpec((rows, LANES), lambda i: (i, DN_AB_COL))
    vec = pl.BlockSpec((1, LANES), lambda i: (0, 0))
    return pl.pallas_call(
        body, name=name, grid=(T // rows,),
        in_specs=[blk, vec, vec, pl.BlockSpec((H, rows, LANES), lambda i: (0, i, 0)),
                  pl.BlockSpec(memory_space=pl.ANY)],
        out_specs=[blk, vec, vec],
        out_shape=[jax.ShapeDtypeStruct(dp.shape, F32), jax.ShapeDtypeStruct((1, LANES), F32),
                   jax.ShapeDtypeStruct((1, LANES), F32)],
        input_output_aliases={4: 0},
        compiler_params=pltpu.CompilerParams(dimension_semantics=("arbitrary",)),
    )(p, alog_row, dtb_row, dgb, dp)


def _dn_layer_fwd(x, ng, w_in, conv_w, a_log, dt_bias, o_gain, w_out, tag, send=()):
    alog_row, dtb_row = _dn_lane_rows(a_log, dt_bias)
    gain = o_gain.reshape(1, DN_DV)
    h = _rmsnorm_fwd(x, ng, name=f"{tag}_norm")
    p = _matmul(h, w_in, mode="nn", name=f"{tag}_inproj")
    qkv = _dn_prep_fwd(p, conv_w, name=f"{tag}_prep")
    gb = _dn_ab_fwd(p, alog_row, dtb_row, name=f"{tag}_ab")
    o, og, states, *landed = _dn_delta_fwd(qkv, gb, p, gain, name=f"{tag}_delta", send=send)
    x_new = _matmul(og, w_out, mode="nn", res=x, name=f"{tag}_outproj")
    return x_new, (h, p, qkv, gb, o, og, states), landed


def _dn_layer_bwd(dx, x, ng, w_in, conv_w, a_log, dt_bias, o_gain, w_out, saved, tag, send=()):
    h, p, qkv, gb, o, og, states = saved
    alog_row, dtb_row = _dn_lane_rows(a_log, dt_bias)
    gain = o_gain.reshape(1, DN_DV)
    d_wout = _matmul(og, dx, mode="tn", out_dtype=BF16, name=f"{tag}_dwout")
    dog = _matmul(dx, w_out, mode="nt", name=f"{tag}_dog")
    dq, dk, dv, dp, dgb, dgain, *landed = _dn_delta_bwd(qkv, gb, p, gain, o, states, dog, name=f"{tag}_deltabwd",
                                                        send=send)
    n_qk = DN_QK_W // LANES
    dp, dconv_q = _dn_conv_bwd(p, conv_w, dq, dp, first=0, normed=True, name=f"{tag}_convbwd_q")
    dp, dconv_k = _dn_conv_bwd(p, conv_w, dk, dp, first=n_qk, normed=True, name=f"{tag}_convbwd_k")
    dp, dconv_v = _dn_conv_bwd(p, conv_w, dv, dp, first=2 * n_qk, normed=False, name=f"{tag}_convbwd_v")
    dconv = jnp.concatenate([dconv_q, dconv_k, dconv_v], axis=1)
    dp, dal, ddt = _dn_ab_bwd(p, alog_row, dtb_row, dgb, dp, name=f"{tag}_abbwd")
    d_win = _matmul(h, dp, mode="tn", name=f"{tag}_dwin")
    dh = _matmul(dp, w_in, mode="nt", name=f"{tag}_dh")
    dx_prev, dng = _rmsnorm_bwd(x, ng, dh, dx, name=f"{tag}_normbwd")
    return dx_prev, dng, d_win, dconv, dal[0, :DN_HEADS], ddt[0, :DN_HEADS], dgain[0], d_wout, landed


def _sb_gains(g):
    return jnp.concatenate([g, g]).reshape(1, LANES)


def _sb_layer_fwd(x, ng, w_in, gq, gk, w_out, tag):
    h = _rmsnorm_fwd(x, ng, name=f"{tag}_norm")
    p3 = _matmul(h, w_in, mode="nn", b_parts=4, out_parts=4, name=f"{tag}_inproj")
    og, o, lsum = _sb_attn_fwd(p3, _sb_gains(gq), _sb_gains(gk), name=f"{tag}_attn")
    x_new = _matmul(og, w_out, mode="nn", res=x, name=f"{tag}_outproj")
    return x_new, (h, p3, og, o, lsum)


def _sb_layer_bwd(dx, x, ng, w_in, gq, gk, w_out, saved, tag):
    h, p3, og, o, lsum = saved
    d_wout = _matmul(og, dx, mode="tn", out_dtype=BF16, name=f"{tag}_dwout")
    dog = _matmul(dx, w_out, mode="nt", name=f"{tag}_dog")
    dp3, dgq, dgk = _sb_attn_bwd(p3, _sb_gains(gq), _sb_gains(gk), o, lsum, dog, name=f"{tag}_attnbwd")
    fold = lambda d: jnp.sum(d.reshape(-1, SB_DH), axis=0)
    d_win = _matmul(h, dp3, mode="tn", b_parts=4, out_parts=4, out_dtype=BF16, name=f"{tag}_dwin")
    dh = _matmul(dp3, w_in, mode="nt", a_parts=4, b_parts=4, name=f"{tag}_dh")
    dx_prev, dng = _rmsnorm_bwd(x, ng, dh, dx, name=f"{tag}_normbwd")
    return dx_prev, dng, d_win, fold(dgq), fold(dgk), d_wout


N_CHIPS = 4
HBM = pl.BlockSpec(memory_space=pl.ANY)


def _mesh_pos():
    return lax.axis_index("x"), lax.axis_index("y"), lax.axis_index("c")


def _other_chips(x, y):
    return [(1 - x, y), (x, 1 - y), (1 - x, 1 - y)]


def _chip_exchange(srcs, *, send_slot_is_dest, copy_own, name):
    n = len(srcs)

    def body(*refs):
        src_refs, out_refs = refs[:n], refs[n:2 * n]
        send_sems, recv_sems, local_sems = refs[2 * n:]
        x, y, c = _mesh_pos()
        me = 2 * x + y
        chips = _other_chips(x, y)
        local = []
        for a in range(n):
            if not copy_own[a]:
                continue
            own = src_refs[a].at[me] if send_slot_is_dest else src_refs[a]
            local.append(pltpu.make_async_copy(own, out_refs[a].at[me], local_sems.at[a]))
        for cp in local:
            cp.start()

        def copy(a, k, landing_slot):
            px, py = chips[k]
            src = src_refs[a].at[2 * px + py] if send_slot_is_dest else src_refs[a]
            return pltpu.make_async_remote_copy(
                src_ref=src, dst_ref=out_refs[a].at[landing_slot],
                send_sem=send_sems.at[a * 3 + k], recv_sem=recv_sems.at[a * 3 + k],
                device_id=(px, py, c), device_id_type=MESH)

        sends = [copy(a, k, me) for a in range(n) for k in range(3)]
        for cp in sends:
            cp.start()
        for a in range(n):
            for k in range(3):
                px, py = chips[k]
                copy(a, k, 2 * px + py).wait_recv()
        for cp in sends:
            cp.wait_send()
        for cp in local:
            cp.wait()

    outs = []
    for s in srcs:
        shape = s.shape if send_slot_is_dest else (N_CHIPS,) + s.shape
        outs.append(jax.ShapeDtypeStruct(shape, s.dtype))
    return pl.pallas_call(
        body, name=name, in_specs=[HBM] * n, out_specs=[HBM] * n, out_shape=outs,
        scratch_shapes=[pltpu.SemaphoreType.DMA((3 * n,)), pltpu.SemaphoreType.DMA((3 * n,)),
                        pltpu.SemaphoreType.DMA((n,))],
    )(*srcs)


def _sibling_exchange(srcs, *, name):
    n = len(srcs)

    def body(*refs):
        src_refs, out_refs = refs[:n], refs[n:2 * n]
        send_sems, recv_sems = refs[2 * n:]
        x, y, c = _mesh_pos()
        copies = [pltpu.make_async_remote_copy(
            src_ref=src_refs[a], dst_ref=out_refs[a], send_sem=send_sems.at[a], recv_sem=recv_sems.at[a],
            device_id=(x, y, 1 - c), device_id_type=MESH) for a in range(n)]
        for cp in copies:
            cp.start()
        for cp in copies:
            cp.wait()

    return pl.pallas_call(
        body, name=name, in_specs=[HBM] * n, out_specs=[HBM] * n,
        out_shape=[jax.ShapeDtypeStruct(s.shape, s.dtype) for s in srcs],
        scratch_shapes=[pltpu.SemaphoreType.DMA((n,)), pltpu.SemaphoreType.DMA((n,))],
    )(*srcs)


def _gather_halves(shards, small, *, name):
    n = len(shards)

    def body(*refs):
        s_refs, small_ref = refs[:n], refs[n]
        o_refs, osmall_ref = refs[n + 1:2 * n + 1], refs[2 * n + 1]
        send_sems, recv_sems, local_sems = refs[2 * n + 2:]
        x, y, c = _mesh_pos()
        me = 2 * x + y
        chips = _other_chips(x, y)
        local = [pltpu.make_async_copy(small_ref, osmall_ref.at[me], local_sems.at[0])]
        for cp in local:
            cp.start()

        def over_ici(a, k, slot):
            px, py = chips[k]
            return pltpu.make_async_remote_copy(
                src_ref=s_refs[a].at[c], dst_ref=o_refs[a].at[slot, c], send_sem=send_sems.at[3 * a + k],
                recv_sem=recv_sems.at[3 * a + k], device_id=(px, py, c), device_id_type=MESH)

        def small_copy(k, slot):
            px, py = chips[k]
            return pltpu.make_async_remote_copy(
                src_ref=small_ref, dst_ref=osmall_ref.at[slot], send_sem=send_sems.at[3 * n + k],
                recv_sem=recv_sems.at[3 * n + k], device_id=(px, py, c), device_id_type=MESH)

        def to_sibling(a, k, half):
            px, py = chips[k]
            blk = o_refs[a].at[2 * px + py, half]
            return pltpu.make_async_remote_copy(
                src_ref=blk, dst_ref=blk, send_sem=send_sems.at[3 * n + 3 + 3 * a + k],
                recv_sem=recv_sems.at[3 * n + 3 + 3 * a + k], device_id=(x, y, 1 - c), device_id_type=MESH)

        sends = [over_ici(a, k, me) for a in range(n) for k in range(3)] + [small_copy(k, me) for k in range(3)]
        for cp in sends:
            cp.start()
        passed = []
        for a in range(n):
            for k in range(3):
                px, py = chips[k]
                over_ici(a, k, 2 * px + py).wait_recv()
                passed.append(to_sibling(a, k, c))
                passed[-1].start()
        for k in range(3):
            px, py = chips[k]
            small_copy(k, 2 * px + py).wait_recv()
        for a in range(n):
            for k in range(3):
                to_sibling(a, k, 1 - c).wait_recv()
        for cp in sends + passed:
            cp.wait_send()
        for cp in local:
            cp.wait()

    nsem = 6 * n + 3
    return pl.pallas_call(
        body, name=name, in_specs=[HBM] * (n + 1), out_specs=[HBM] * (n + 1),
        out_shape=[jax.ShapeDtypeStruct((N_CHIPS,) + s.shape, s.dtype) for s in shards + [small]],
        scratch_shapes=[pltpu.SemaphoreType.DMA((nsem,)), pltpu.SemaphoreType.DMA((nsem,)),
                        pltpu.SemaphoreType.DMA((1,))],
    )(*shards, small)


def _forward_halves(landed, *, name):
    n = len(landed)

    def body(*refs):
        o_refs = refs[n:2 * n]
        send_sems, recv_sems = refs[2 * n:]
        x, y, c = _mesh_pos()
        chips = _other_chips(x, y)
        pairs = [(a, k) for a in range(n) for k in range(3)]

        def copy(a, k, half):
            px, py = chips[k]
            blk = o_refs[a].at[2 * px + py, half]
            return pltpu.make_async_remote_copy(
                src_ref=blk, dst_ref=blk, send_sem=send_sems.at[3 * a + k], recv_sem=recv_sems.at[3 * a + k],
                device_id=(x, y, 1 - c), device_id_type=MESH)

        sends = [copy(a, k, c) for a, k in pairs]
        for cp in sends:
            cp.start()
        for a, k in pairs:
            copy(a, k, 1 - c).wait_recv()
        for cp in sends:
            cp.wait_send()

    return pl.pallas_call(
        body, name=name, in_specs=[HBM] * n, out_specs=[HBM] * n,
        out_shape=[jax.ShapeDtypeStruct(a.shape, a.dtype) for a in landed],
        input_output_aliases={a: a for a in range(n)},
        scratch_shapes=[pltpu.SemaphoreType.DMA((3 * n,)), pltpu.SemaphoreType.DMA((3 * n,))],
    )(*landed)


def _swap_other_half(g_list, *, name):
    n = len(g_list)

    def body(*refs):
        g_refs, o_refs = refs[:n], refs[n:2 * n]
        send_sems, recv_sems = refs[2 * n:]
        x, y, c = _mesh_pos()
        copies = [pltpu.make_async_remote_copy(
            src_ref=g_refs[a].at[:, 1 - c], dst_ref=o_refs[a], send_sem=send_sems.at[a], recv_sem=recv_sems.at[a],
            device_id=(x, y, 1 - c), device_id_type=MESH) for a in range(n)]
        for cp in copies:
            cp.start()
        for cp in copies:
            cp.wait()

    return pl.pallas_call(
        body, name=name, in_specs=[HBM] * n, out_specs=[HBM] * n,
        out_shape=[jax.ShapeDtypeStruct((g.shape[0],) + g.shape[2:], g.dtype) for g in g_list],
        scratch_shapes=[pltpu.SemaphoreType.DMA((n,)), pltpu.SemaphoreType.DMA((n,))],
    )(*g_list)


def _row_tile(r):
    return _pick(r, (512, 256, 128, 64, 32, 16, 8))


def _add_my_half(g4, sib4, core, *, name):
    n, _, r, C = g4.shape
    tr = _row_tile(r)

    def body(core_ref, g_ref, s_ref, o_ref):
        del core_ref
        o_ref[...] = (g_ref[...].astype(F32) + s_ref[...].astype(F32)).astype(o_ref.dtype)

    return pl.pallas_call(
        body, name=name,
        grid_spec=pltpu.PrefetchScalarGridSpec(
            num_scalar_prefetch=1, grid=(n, r // tr),
            in_specs=[pl.BlockSpec((None, None, tr, C), lambda j, i, core_ref: (j, core_ref[0], i, 0)),
                      pl.BlockSpec((None, tr, C), lambda j, i, core_ref: (j, i, 0))],
            out_specs=pl.BlockSpec((None, tr, C), lambda j, i, core_ref: (j, i, 0))),
        out_shape=jax.ShapeDtypeStruct((n, r, C), g4.dtype),
        compiler_params=pltpu.CompilerParams(dimension_semantics=("parallel", "parallel")),
    )(core, g4, sib4)


def _scatter_to_chips(p_list, *, name):
    n = len(p_list)

    def body(*refs):
        p_refs, o_refs = refs[:n], refs[n:2 * n]
        send_sems, recv_sems = refs[2 * n:]
        x, y, c = _mesh_pos()
        me = 2 * x + y
        chips = _other_chips(x, y)
        pairs = [(a, k) for a in range(n) for k in range(3)]

        def copy(a, k, landing_slot):
            px, py = chips[k]
            return pltpu.make_async_remote_copy(
                src_ref=p_refs[a].at[2 * px + py], dst_ref=o_refs[a].at[landing_slot],
                send_sem=send_sems.at[3 * a + k], recv_sem=recv_sems.at[3 * a + k], device_id=(px, py, c),
                device_id_type=MESH)

        sends = [copy(a, k, me) for a, k in pairs]
        for cp in sends:
            cp.start()
        for a, k in pairs:
            px, py = chips[k]
            copy(a, k, 2 * px + py).wait_recv()
        for cp in sends:
            cp.wait_send()

    return pl.pallas_call(
        body, name=name, in_specs=[HBM] * n, out_specs=[HBM] * n,
        out_shape=[jax.ShapeDtypeStruct(p.shape, p.dtype) for p in p_list],
        scratch_shapes=[pltpu.SemaphoreType.DMA((3 * n,)), pltpu.SemaphoreType.DMA((3 * n,))],
    )(*p_list)


def _sum_chips(landed, part, me, *, name):
    _, r, C = landed.shape
    tr = _row_tile(r)

    def body(me_ref, own_ref, r1_ref, r2_ref, r3_ref, o_ref):
        del me_ref
        f = lambda ref: ref[...].astype(F32)
        o_ref[...] = ((f(own_ref) + f(r1_ref)) + f(r2_ref)) + f(r3_ref)

    slot = lambda d: pl.BlockSpec((None, tr, C), lambda i, me_ref: ((me_ref[0] + d) % N_CHIPS, i, 0))
    return pl.pallas_call(
        body, name=name,
        grid_spec=pltpu.PrefetchScalarGridSpec(
            num_scalar_prefetch=1, grid=(r // tr,), in_specs=[slot(0), slot(1), slot(2), slot(3)],
            out_specs=pl.BlockSpec((tr, C), lambda i, me_ref: (i, 0))),
        out_shape=jax.ShapeDtypeStruct((r, C), F32),
        compiler_params=pltpu.CompilerParams(dimension_semantics=("parallel",)),
    )(me, part, landed, landed, landed)


def _adamw_halves(w, mine, theirs, m, v, core, *, layer, prev, name):
    shape = w.shape
    r, C = mine.shape
    tr = _pick(r, (128, 64, 32, 16, 8))
    per = r // tr
    view = lambda a: a.reshape(-1, C)
    n_prev = 0 if prev is None else 4

    def body(*refs):
        core_ref, w_ref, gm_ref, gt_ref, m_ref, v_ref = refs[:6]
        g_ref, d_ref, nm_ref, nv_ref = refs[6 + n_prev:]
        gv = jnp.where(pl.program_id(0) == core_ref[0], gm_ref[...], gt_ref[...])
        g_ref[...] = gv
        d_ref[...], nm_ref[...], nv_ref[...] = _adamw_math(w_ref[...], gv, m_ref[...], v_ref[...])

    half = pl.BlockSpec((tr, C), lambda h, i, core_ref: ((2 * layer + h) * per + i, 0))
    row = pl.BlockSpec((tr, C), lambda h, i, core_ref: (i, 0))
    out = jax.ShapeDtypeStruct((math.prod(shape) // C, C), F32)
    res = pl.pallas_call(
        body, name=name,
        grid_spec=pltpu.PrefetchScalarGridSpec(
            num_scalar_prefetch=1, grid=(2, per), in_specs=[half, row, row, half, half] + [HBM] * n_prev,
            out_specs=[half] * 4),
        out_shape=[out] * 4,
        input_output_aliases={6 + j: j for j in range(n_prev)},
        compiler_params=pltpu.CompilerParams(dimension_semantics=("parallel", "parallel")),
    )(core, view(w), mine, theirs, view(m), view(v), *([] if prev is None else [view(a) for a in prev]))
    return tuple(a.reshape(shape) for a in res)


def _sum_small(recv4, *, name):
    _, R, C = recv4.shape

    def body(r_ref, o_ref):
        o_ref[...] = ((r_ref[0] + r_ref[1]) + r_ref[2]) + r_ref[3]

    return pl.pallas_call(body, name=name, out_shape=jax.ShapeDtypeStruct((R, C), F32))(recv4)


def _add(a, b, *, name):
    R, C = a.shape
    tr = _pick(R, (512, 256, 128, 64, 32, 16, 8))
    blk = pl.BlockSpec((tr, C), lambda i: (i, 0))

    def body(a_ref, b_ref, o_ref):
        o_ref[...] = a_ref[...] + b_ref[...]

    return pl.pallas_call(body, name=name, grid=(R // tr,), in_specs=[blk, blk], out_specs=blk,
                          out_shape=jax.ShapeDtypeStruct((R, C), F32),
                          compiler_params=pltpu.CompilerParams(dimension_semantics=("parallel",)))(a, b)


def _adamw_math(w, g, m, v):
    nm = ADAM_B1 * m + (1.0 - ADAM_B1) * g
    nv = ADAM_B2 * v + (1.0 - ADAM_B2) * (g * g)
    m_hat = nm / (1.0 - ADAM_B1 ** ADAM_STEP)
    v_hat = nv / (1.0 - ADAM_B2 ** ADAM_STEP)
    return -ADAM_LR * (m_hat / (jnp.sqrt(v_hat) + ADAM_EPS) + ADAM_WD * w), nm, nv


def _adamw(w, g, m, v, *, name):
    shape = w.shape
    C = shape[-1]
    R = w.size // C
    two = lambda a: a.reshape(R, C)
    tr = _pick(R, (256, 128, 64, 32, 16, 8)) if R % 8 == 0 and R > 8 else R
    blk = pl.BlockSpec((tr, C), lambda i: (i, 0))

    def body(w_ref, g_ref, m_ref, v_ref, d_ref, nm_ref, nv_ref):
        d_ref[...], nm_ref[...], nv_ref[...] = _adamw_math(w_ref[...], g_ref[...], m_ref[...], v_ref[...])

    out = jax.ShapeDtypeStruct((R, C), F32)
    d, nm, nv = pl.pallas_call(
        body, name=name, grid=(R // tr,), in_specs=[blk] * 4, out_specs=[blk] * 3, out_shape=[out] * 3,
        compiler_params=pltpu.CompilerParams(dimension_semantics=("parallel",)),
    )(two(w), two(g), two(m), two(v))
    return d.reshape(shape), nm.reshape(shape), nv.reshape(shape)


BIG = (("dn_w_in", (2, 1024, 1540), 2), ("dn_w_out", (2, 512, 1024), 1), ("sb_w_in", (1, 1024, 1024), 2),
       ("sb_w_out", (1, 256, 1024), 1), ("sc_w_in", (1, 1024, 2048), 2), ("sc_w_out", (1, 512, 1024), 1))
SMALL = (("dn_conv_w", (2, 4, 1024), 2), ("dn_o_norm_g", (2, 64), 1), ("sc_conv_w", (1, 3, 512), 2))
REPL = (("norm_g", (4, 1024)), ("dn_a_log", (2, 8)), ("dn_dt_bias", (2, 8)), ("sb_q_norm_g", (1, 64)),
        ("sb_k_norm_g", (1, 64)))


def _halves(shard):
    return shard.reshape(2, -1, shard.shape[-1])


def _pack(arrays, cols, lead=()):
    flat = jnp.concatenate([a.reshape(lead + (-1,)) for a in arrays], axis=-1)
    n = flat.shape[-1]
    rows = -(-n // cols)
    unit = 512 if rows > 512 else 8
    rows = -(-rows // unit) * unit
    flat = jnp.pad(flat, [(0, 0)] * len(lead) + [(0, rows * cols - n)])
    return flat.reshape(lead + (rows, cols))


def _unpack(buf, table, lead=()):
    flat = buf.reshape(lead + (-1,))
    out, off = {}, 0
    for entry in table:
        name, shape = entry[0], entry[1]
        n = math.prod(shape)
        out[name] = flat[..., off:off + n].reshape(lead + shape)
        off += n
    return out


def _join(shards, axis):
    return jnp.concatenate([shards[j] for j in range(N_CHIPS)], axis=axis)


def _split(full, axis):
    return jnp.stack(jnp.split(full, N_CHIPS, axis=axis), axis=0)


def kernel(x, norm_g, dn_w_in, dn_conv_w, dn_a_log, dn_dt_bias, dn_o_norm_g, dn_w_out, sb_w_in, sb_q_norm_g, sb_k_norm_g, sb_w_out, sc_w_in, sc_conv_w, sc_w_out, loss_target, m_norm_g, m_dn_w_in, m_dn_conv_w, m_dn_a_log, m_dn_dt_bias, m_dn_o_norm_g, m_dn_w_out, m_sb_w_in, m_sb_q_norm_g, m_sb_k_norm_g, m_sb_w_out, m_sc_w_in, m_sc_conv_w, m_sc_w_out, v_norm_g, v_dn_w_in, v_dn_conv_w, v_dn_a_log, v_dn_dt_bias, v_dn_o_norm_g, v_dn_w_out, v_sb_w_in, v_sb_q_norm_g, v_sb_k_norm_g, v_sb_w_out, v_sc_w_in, v_sc_conv_w, v_sc_w_out):
    weights = dict(norm_g=norm_g, dn_w_in=dn_w_in, dn_conv_w=dn_conv_w, dn_a_log=dn_a_log, dn_dt_bias=dn_dt_bias,
                   dn_o_norm_g=dn_o_norm_g, dn_w_out=dn_w_out, sb_w_in=sb_w_in, sb_q_norm_g=sb_q_norm_g,
                   sb_k_norm_g=sb_k_norm_g, sb_w_out=sb_w_out, sc_w_in=sc_w_in, sc_conv_w=sc_conv_w, sc_w_out=sc_w_out)
    m_in = dict(norm_g=m_norm_g, dn_w_in=m_dn_w_in, dn_conv_w=m_dn_conv_w, dn_a_log=m_dn_a_log,
                dn_dt_bias=m_dn_dt_bias, dn_o_norm_g=m_dn_o_norm_g, dn_w_out=m_dn_w_out, sb_w_in=m_sb_w_in,
                sb_q_norm_g=m_sb_q_norm_g, sb_k_norm_g=m_sb_k_norm_g, sb_w_out=m_sb_w_out, sc_w_in=m_sc_w_in,
                sc_conv_w=m_sc_conv_w, sc_w_out=m_sc_w_out)
    v_in = dict(norm_g=v_norm_g, dn_w_in=v_dn_w_in, dn_conv_w=v_dn_conv_w, dn_a_log=v_dn_a_log,
                dn_dt_bias=v_dn_dt_bias, dn_o_norm_g=v_dn_o_norm_g, dn_w_out=v_dn_w_out, sb_w_in=v_sb_w_in,
                sb_q_norm_g=v_sb_q_norm_g, sb_k_norm_g=v_sb_k_norm_g, sb_w_out=v_sb_w_out, sc_w_in=v_sc_w_in,
                sc_conv_w=v_sc_conv_w, sc_w_out=v_sc_w_out)
    order = list(weights)
    xi, yi, ci = _mesh_pos()

    small = _pack([weights[n] for n, _, _ in SMALL], LANES)
    later = [("dn_w_in", 1), ("dn_w_out", 1), ("sb_w_in", 0), ("sb_w_out", 0), ("sc_w_in", 0), ("sc_w_out", 0)]
    piece = lambda n, l: _halves(weights[n][l].astype(BF16)[None])
    own_first = [piece("dn_w_in", 0), piece("dn_w_out", 0)]
    own_later = [piece(n, l) for n, l in later]
    me = 2 * xi + yi
    whole = lambda g4, own: lax.dynamic_update_index_in_dim(g4, own, me, 0)
    flat = lambda g4: g4.reshape(N_CHIPS, -1, g4.shape[-1])
    rows_of = lambda w4: w4.reshape(-1, w4.shape[-1])
    dn_in = lambda w4: jnp.pad(_join(w4, 1), ((0, 0), (0, DN_IN_PAD - DN_IN)))
    w_in0, w_out0, small4 = _gather_halves(own_first, small, name="gather_first")
    full = {n: _join(a, ax) for (n, _, ax), a in zip(SMALL, _unpack(small4, SMALL, (N_CHIPS,)).values())}

    def dn_args(j, w_in4, w_out4):
        return (dn_in(flat(w_in4)), full["dn_conv_w"][j], dn_a_log[j], dn_dt_bias[j], full["dn_o_norm_g"][j],
                rows_of(w_out4))

    x0 = x[0]
    dn0 = dn_args(0, whole(w_in0, own_first[0]), whole(w_out0, own_first[1]))
    x1, s0, landed = _dn_layer_fwd(x0, norm_g[0], *dn0, "l0", send=own_later)
    landed = _forward_halves(landed, name="forward_halves")
    w_in3, w_out3, sb_in, sb_out, sc_in, sc_out = [whole(g4, own) for g4, own in zip(landed, own_later)]
    dn1 = dn_args(1, w_in3, w_out3)
    sb_args = (flat(sb_in), sb_q_norm_g[0], sb_k_norm_g[0], rows_of(sb_out))
    sc_args = (flat(sc_in), full["sc_conv_w"][0], rows_of(sc_out))
    x2, s1 = _sb_layer_fwd(x1, norm_g[1], *sb_args, "l1")
    x3, s2 = _sc_layer_fwd(x2, norm_g[2], *sc_args, "l2")
    x4, s3, _ = _dn_layer_fwd(x3, norm_g[3], *dn1, "l3")
    dy, loss_local = _loss_head(x4, loss_target[0], name="loss_head")
    loss = lax.psum(loss_local[0, 0], ("x", "y", "c"))

    by_cols = lambda dw: _split(dw[:, :DN_IN].astype(BF16), 1)
    by_rows = lambda dw: dw.reshape(N_CHIPS, -1, dw.shape[-1])
    cut2 = lambda g4: g4.reshape(N_CHIPS, 2, -1, g4.shape[-1])
    core = ci.astype(jnp.int32).reshape(1)
    chip = me.astype(jnp.int32).reshape(1)

    def chip_sums(g_list, tag):
        sib = _swap_other_half(g_list, name=f"swap_halves_{tag}")
        return [_add_my_half(g, s, core, name=f"sum_cores_{tag}{i}") for i, (g, s) in enumerate(zip(g_list, sib))]

    dx3, dng3, dwin3, dconv3, dal3, ddt3, dgain3, dwout3, _ = _dn_layer_bwd(dy, x3, norm_g[3], *dn1, s3, "l3")
    dx2, dng2, dwin2, dconv2, dwout2 = _sc_layer_bwd(dx3, x2, norm_g[2], *sc_args, s2, "l2")
    dx1, dng1, dwin1, dgq, dgk, dwout1 = _sb_layer_bwd(dx2, x1, norm_g[1], *sb_args, s1, "l1")
    part_later = chip_sums([cut2(by_cols(dwin3)), cut2(by_rows(dwout3)), cut2(dwin1), cut2(by_rows(dwout1)),
                            cut2(dwin2), cut2(by_rows(dwout2))], "later")
    dx0, dng0, dwin0, dconv0, dal0, ddt0, dgain0, dwout0, landed_later = _dn_layer_bwd(
        dx1, x0, norm_g[0], *dn0, s0, "l0", send=part_later)
    part_first = chip_sums([cut2(by_cols(dwin0)), cut2(by_rows(dwout0))], "first")
    landed_first = _scatter_to_chips(part_first, name="scatter_first")
    pieces = [("dn_w_in", 0), ("dn_w_out", 0)] + later
    mine = [_sum_chips(r, p, chip, name=f"sum_chips_{n}{l}")
            for (n, l), r, p in zip(pieces, list(landed_first) + list(landed_later), part_first + part_later)]
    theirs = _sibling_exchange(mine, name="swap_results")
    upd = {}
    for (n, l), a, b in zip(pieces, mine, theirs):
        upd[n] = _adamw_halves(weights[n], a, b, m_in[n], v_in[n], core, layer=l, prev=upd.get(n),
                               name=f"adamw_{n}{l}")
    g_out = {n: upd[n][0] for n, _, _ in BIG}

    grads = dict(
        norm_g=jnp.concatenate([dng0, dng1, dng2, dng3], axis=0), dn_conv_w=jnp.stack([dconv0, dconv3]),
        dn_a_log=jnp.stack([dal0, dal3]), dn_dt_bias=jnp.stack([ddt0, ddt3]),
        dn_o_norm_g=jnp.stack([dgain0, dgain3]), sb_q_norm_g=dgq[None], sb_k_norm_g=dgk[None],
        sc_conv_w=dconv2[None])
    repl = [jnp.broadcast_to(grads[n][None], (N_CHIPS,) + s) for n, s in REPL]
    gsmall = _pack([_split(grads[n], ax) for n, _, ax in SMALL] + repl, LANES, (N_CHIPS,))
    rsmall, = _chip_exchange([gsmall], send_slot_is_dest=True, copy_own=(True,), name="scatter_small")
    psmall = _sum_small(rsmall, name="sum_chips_small")
    qsmall, = _sibling_exchange([psmall], name="swap_cores_small")
    tsmall = _add(psmall, qsmall, name="sum_cores_small")
    g_out.update(_unpack(tsmall, SMALL + REPL))

    for n in order:
        if n not in upd:
            upd[n] = (g_out[n],) + _adamw(weights[n], g_out[n], m_in[n], v_in[n], name=f"adamw_{n}")
    return (loss, dx0[None], *[upd[n][0] for n in order], *[upd[n][1] for n in order],
            *[upd[n][2] for n in order], *[upd[n][3] for n in order])
```

```python
import math

import jax
import jax.numpy as jnp
from jax import lax
from jax.experimental import pallas as pl
from jax.experimental.pallas import tpu as pltpu

F32 = jnp.float32
BF16 = jnp.bfloat16
MESH = pl.DeviceIdType.MESH

RMS_EPS = 1e-6
L2_EPS = 1e-6
LANES = 128
VMEM_BIG = 60 * 1024 * 1024
MM_VMEM = 36 * 1024 * 1024

DN_HEADS, DN_DK, DN_DV, DN_CHUNK, DN_CONV = 8, 128, 256, 64, 4
DN_QK_W = DN_HEADS * DN_DK
DN_V_W = DN_HEADS * DN_DV
DN_CONV_W = 2 * DN_QK_W + DN_V_W
DN_IN = DN_CONV_W + DN_V_W + 2 * DN_HEADS
DN_IN_PAD = DN_CONV_W + DN_V_W + LANES
SB_DH = 64
SC_CONV = 3

ADAM_LR, ADAM_B1, ADAM_B2, ADAM_EPS, ADAM_WD, ADAM_STEP = 0.001, 0.9, 0.999, 1e-08, 0.01, 10


def _pick(n, cands):
    for c in cands:
        if n % c == 0:
            return c
    raise ValueError(f"no tile for {n} in {cands}")


def _bf(x):
    return x.astype(BF16)


def _dot(a, b):
    return jnp.dot(_bf(a), _bf(b), preferred_element_type=F32)


def _dot_nt(a, b):
    return lax.dot_general(_bf(a), _bf(b), (((1,), (1,)), ((), ())), preferred_element_type=F32)


def _dot_tn(a, b):
    return lax.dot_general(_bf(a), _bf(b), (((0,), (0,)), ((), ())), preferred_element_type=F32)


def _split3(a):
    hi = _bf(a)
    r = a - hi.astype(F32)
    mid = _bf(r)
    lo = _bf(r - mid.astype(F32))
    return hi, mid, lo


def _sigmoid(x):
    return 1.0 / (1.0 + jnp.exp(-x))


def _silu(x):
    return x * _sigmoid(x)


def _dsilu(x):
    s = _sigmoid(x)
    return s * (1.0 + x * (1.0 - s))


def _softplus(x):
    return jnp.maximum(x, 0.0) + jnp.log(1.0 + jnp.exp(-jnp.abs(x)))


def _shift_down(z, k):
    if k == 0:
        return z
    row = lax.broadcasted_iota(jnp.int32, z.shape, 0)
    return jnp.where(row >= k, pltpu.roll(z, k, 0), 0.0)


def _shift_up(z, k):
    if k == 0:
        return z
    n = z.shape[0]
    row = lax.broadcasted_iota(jnp.int32, z.shape, 0)
    return jnp.where(row < n - k, pltpu.roll(z, n - k, 0), 0.0)


def _matmul(a, b, *, mode, name, res=None, a_parts=1, b_parts=1, out_parts=1, out_dtype=F32):
    def dims2(x, parts):
        if parts == 1:
            return x.shape
        assert x.shape[0] == parts
        return (x.shape[1], x.shape[2] * parts)

    ash, bsh = dims2(a, a_parts), dims2(b, b_parts)
    if mode == "nn":
        (M, K), (K2, N) = ash, bsh
        dn = (((1,), (0,)), ((), ()))
    elif mode == "nt":
        (M, K), (N, K2) = ash, bsh
        dn = (((1,), (1,)), ((), ()))
    else:
        (K, M), (K2, N) = ash, bsh
        dn = (((0,), (0,)), ((), ()))
    assert K == K2, (ash, bsh, mode)
    tm = _pick(M, (512, 256, 128, 64, 32, 16, 8))
    n_unit = N // max(out_parts, b_parts if mode != "nt" else 1)
    k_unit = K // max(a_parts if mode != "tn" else 1, b_parts if mode == "nt" else 1)
    tn, tk = min(
        ((n, k) for n in (2048, 1792, 1024, 896, 768, 512, 384, 256, 128) if n_unit % n == 0
         for k in (2048, 1792, 1024, 896, 512, 256, 128) if k_unit % k == 0
         if 2 * (tm * k * a.dtype.itemsize + k * n * b.dtype.itemsize + 2 * tm * n * 4) + tm * n * 4 <= MM_VMEM),
        key=lambda nk_: (-nk_[0] * nk_[1], -nk_[1]))
    nk = K // tk
    grid = (M // tm, N // tn, nk)

    def spec(parts, rows_are, cols_are, tr, tc, width):
        per = width // parts // tc
        if parts == 1:
            return pl.BlockSpec((tr, tc), lambda i, j, k: ((i, j, k)[rows_are], (i, j, k)[cols_are]))
        return pl.BlockSpec((None, tr, tc), lambda i, j, k: ((i, j, k)[cols_are] // per, (i, j, k)[rows_are],
                                                             (i, j, k)[cols_are] % per))

    if mode == "nn":
        a_spec = spec(a_parts, 0, 2, tm, tk, K)
        b_spec = spec(b_parts, 2, 1, tk, tn, N)
    elif mode == "nt":
        a_spec = spec(a_parts, 0, 2, tm, tk, K)
        b_spec = spec(b_parts, 1, 2, tn, tk, K)
    else:
        a_spec = spec(a_parts, 2, 0, tk, tm, M)
        b_spec = spec(b_parts, 2, 1, tk, tn, N)
    o_spec = spec(out_parts, 0, 1, tm, tn, N)
    in_specs = [a_spec, b_spec]
    operands = [a, b]
    if res is not None:
        in_specs.append(pl.BlockSpec((tm, tn), lambda i, j, k: (i, j)))
        operands.append(res)

    def finish(refs, r):
        if res is not None:
            r = refs[2][...] + r
        refs[-2 if nk > 1 else -1][...] = r.astype(out_dtype)

    def body(*refs):
        part = lax.dot_general(_bf(refs[0][...]), _bf(refs[1][...]), dn, preferred_element_type=F32)
        if nk == 1:
            finish(refs, part)
            return
        acc_ref = refs[-1]
        k = pl.program_id(2)

        @pl.when(k == 0)
        def _():
            acc_ref[...] = part

        @pl.when(jnp.logical_and(k > 0, k < nk - 1))
        def _():
            acc_ref[...] += part

        @pl.when(k == nk - 1)
        def _():
            finish(refs, acc_ref[...] + part)

    out_shape = (M, N) if out_parts == 1 else (out_parts, M, N // out_parts)
    return pl.pallas_call(
        body, name=name, grid=grid, in_specs=in_specs, out_specs=o_spec,
        out_shape=jax.ShapeDtypeStruct(out_shape, out_dtype),
        scratch_shapes=[pltpu.VMEM((tm, tn), F32)] if nk > 1 else [],
        compiler_params=pltpu.CompilerParams(dimension_semantics=("parallel", "parallel", "arbitrary"),
                                             vmem_limit_bytes=VMEM_BIG),
    )(*operands)


def _rmsnorm_fwd(x, g, *, name):
    T, D = x.shape
    tm = _pick(T, (512, 256, 128, 64, 32, 16))

    def body(x_ref, g_ref, h_ref):
        xv = x_ref[...]
        r = lax.rsqrt(jnp.mean(xv * xv, axis=-1, keepdims=True) + RMS_EPS)
        h_ref[...] = ((xv * r) * g_ref[...]).astype(BF16)

    return pl.pallas_call(
        body, name=name, grid=(T // tm,),
        in_specs=[pl.BlockSpec((tm, D), lambda i: (i, 0)), pl.BlockSpec((1, D), lambda i: (0, 0))],
        out_specs=pl.BlockSpec((tm, D), lambda i: (i, 0)),
        out_shape=jax.ShapeDtypeStruct((T, D), BF16),
    )(x, g.reshape(1, D))


def _rmsnorm_bwd(x, g, dh, dx_in, *, name):
    T, D = x.shape
    tm = _pick(T, (512, 256, 128, 64, 32, 16))

    def body(x_ref, g_ref, dh_ref, dxin_ref, dx_ref, dg_ref):
        @pl.when(pl.program_id(0) == 0)
        def _():
            dg_ref[...] = jnp.zeros_like(dg_ref)

        xv = x_ref[...]
        r = lax.rsqrt(jnp.mean(xv * xv, axis=-1, keepdims=True) + RMS_EPS)
        xh = xv * r
        dh_v = dh_ref[...]
        dxh = dh_v * g_ref[...]
        dx_ref[...] = dxin_ref[...] + r * (dxh - xh * jnp.mean(dxh * xh, axis=-1, keepdims=True))
        dg_ref[...] += jnp.sum(dh_v * xh, axis=0, keepdims=True)

    row = pl.BlockSpec((tm, D), lambda i: (i, 0))
    vec = pl.BlockSpec((1, D), lambda i: (0, 0))
    return pl.pallas_call(
        body, name=name, grid=(T // tm,),
        in_specs=[row, vec, row, row], out_specs=[row, vec],
        out_shape=[jax.ShapeDtypeStruct((T, D), F32), jax.ShapeDtypeStruct((1, D), F32)],
        compiler_params=pltpu.CompilerParams(dimension_semantics=("arbitrary",)),
    )(x, g.reshape(1, D), dh, dx_in)


def _loss_head(y, target, *, name):
    T, D = y.shape
    tm = _pick(T, (512, 256, 128, 64, 32, 16))

    def body(y_ref, t_ref, dy_ref, l_ref):
        @pl.when(pl.program_id(0) == 0)
        def _():
            l_ref[...] = jnp.zeros_like(l_ref)

        err = y_ref[...] - t_ref[...]
        dy_ref[...] = err * (1.0 / D)
        l_ref[...] += 0.5 * jnp.sum(jnp.mean(err * err, axis=-1, keepdims=True), axis=0, keepdims=True)

    row = pl.BlockSpec((tm, D), lambda i: (i, 0))
    return pl.pallas_call(
        body, name=name, grid=(T // tm,),
        in_specs=[row, row], out_specs=[row, pl.BlockSpec((1, 1), lambda i: (0, 0))],
        out_shape=[jax.ShapeDtypeStruct((T, D), F32), jax.ShapeDtypeStruct((1, 1), F32)],
        compiler_params=pltpu.CompilerParams(dimension_semantics=("arbitrary",)),
    )(y, target)


def _sc_mid_fwd(p3, conv_w, *, name):
    _, T, W = p3.shape
    K = conv_w.shape[0]
    cw = LANES

    def body(p_ref, w_ref, o_ref):
        z = p_ref[1] * p_ref[2]
        cv = sum(w_ref[i:i + 1, :] * _shift_down(z, K - 1 - i) for i in range(K))
        o_ref[...] = ((p_ref[0] * cv) * _silu(p_ref[3])).astype(BF16)

    return pl.pallas_call(
        body, name=name, grid=(W // cw,),
        in_specs=[pl.BlockSpec((4, T, cw), lambda j: (0, 0, j)), pl.BlockSpec((K, cw), lambda j: (0, j))],
        out_specs=pl.BlockSpec((T, cw), lambda j: (0, j)),
        out_shape=jax.ShapeDtypeStruct((T, W), BF16),
        compiler_params=pltpu.CompilerParams(dimension_semantics=("parallel",), vmem_limit_bytes=VMEM_BIG),
    )(p3, conv_w)


def _sc_mid_bwd(p3, conv_w, do, *, name):
    _, T, W = p3.shape
    K = conv_w.shape[0]
    cw = LANES

    def body(p_ref, w_ref, do_ref, dp_ref, dw_ref):
        b, c, u, gate = p_ref[0], p_ref[1], p_ref[2], p_ref[3]
        z = c * u
        zs = [_shift_down(z, K - 1 - i) for i in range(K)]
        cv = sum(w_ref[i:i + 1, :] * zs[i] for i in range(K))
        y = b * cv
        dov = do_ref[...]
        dy = dov * _silu(gate)
        dp_ref[3] = dov * y * _dsilu(gate)
        dp_ref[0] = dy * cv
        dcv = dy * b
        dz = sum(w_ref[i:i + 1, :] * _shift_up(dcv, K - 1 - i) for i in range(K))
        dp_ref[1] = dz * u
        dp_ref[2] = dz * c
        for i in range(K):
            dw_ref[i:i + 1, :] = jnp.sum(dcv * zs[i], axis=0, keepdims=True)

    return pl.pallas_call(
        body, name=name, grid=(W // cw,),
        in_specs=[pl.BlockSpec((4, T, cw), lambda j: (0, 0, j)), pl.BlockSpec((K, cw), lambda j: (0, j)),
                  pl.BlockSpec((T, cw), lambda j: (0, j))],
        out_specs=[pl.BlockSpec((4, T, cw), lambda j: (0, 0, j)), pl.BlockSpec((K, cw), lambda j: (0, j))],
        out_shape=[jax.ShapeDtypeStruct((4, T, W), F32), jax.ShapeDtypeStruct((K, W), F32)],
        compiler_params=pltpu.CompilerParams(dimension_semantics=("parallel",), vmem_limit_bytes=VMEM_BIG),
    )(p3, conv_w, do)


def _sc_layer_fwd(x, ng, w_in, conv_w, w_out, tag):
    h = _rmsnorm_fwd(x, ng, name=f"{tag}_norm")
    p3 = _matmul(h, w_in, mode="nn", b_parts=4, out_parts=4, name=f"{tag}_inproj")
    og = _sc_mid_fwd(p3, conv_w, name=f"{tag}_mid")
    x_new = _matmul(og, w_out, mode="nn", res=x, name=f"{tag}_outproj")
    return x_new, (h, p3, og)


def _sc_layer_bwd(dx, x, ng, w_in, conv_w, w_out, saved, tag):
    h, p3, og = saved
    d_wout = _matmul(og, dx, mode="tn", out_dtype=BF16, name=f"{tag}_dwout")
    dog = _matmul(dx, w_out, mode="nt", name=f"{tag}_dog")
    dp3, dconv = _sc_mid_bwd(p3, conv_w, dog, name=f"{tag}_midbwd")
    d_win = _matmul(h, dp3, mode="tn", b_parts=4, out_parts=4, out_dtype=BF16, name=f"{tag}_dwin")
    dh = _matmul(dp3, w_in, mode="nt", a_parts=4, b_parts=4, name=f"{tag}_dh")
    dx_prev, dng = _rmsnorm_bwd(x, ng, dh, dx, name=f"{tag}_normbwd")
    return dx_prev, dng, d_win, dconv, d_wout


SB_BQ = 256
SB_BK = 256
SB_ROWS = 512
SB_DEAD = -110.0


def _sb_half_mask():
    return lax.broadcasted_iota(jnp.int32, (1, LANES), 1) < SB_DH


def _sb_headnorm(x, g, lo):
    x2 = x * x
    s_lo = jnp.sum(jnp.where(lo, x2, 0.0), axis=-1, keepdims=True)
    s_hi = jnp.sum(jnp.where(lo, 0.0, x2), axis=-1, keepdims=True)
    r = lax.rsqrt(jnp.where(lo, s_lo, s_hi) * (1.0 / SB_DH) + RMS_EPS)
    xh = x * r
    return xh * g, xh, r


def _dot_x2_l(a_l, b_exact_bf16):
    his = [_bf(a) for a in a_l]
    mids = [_bf(a - h.astype(F32)) for a, h in zip(a_l, his)]
    f = lambda p: jnp.dot(p, b_exact_bf16, preferred_element_type=F32)
    return [x + y for x, y in zip([f(h) for h in his], [f(m) for m in mids])]


def _sb_stack(xb, lo):
    zero = jnp.zeros_like(xb)
    return jnp.concatenate([jnp.where(lo, xb, zero), jnp.where(lo, zero, xb)], axis=0)


def _sb_rel(bq, bk):
    row = lax.broadcasted_iota(jnp.int32, (2 * bq, bk), 0)
    col = lax.broadcasted_iota(jnp.int32, (2 * bq, bk), 1)
    return col - jnp.where(row >= bq, row - bq, row)


def _sb_tile(qm, kb, valid):
    z = lax.dot_general(qm, kb, (((1,), (1,)), ((), ())), preferred_element_type=F32)
    sp = _softplus(z)
    return z - sp, (-sp if valid is None else jnp.where(valid, -sp, 0.0))


def _sb_attn_fwd(p3, gq2, gk2, *, name):
    _, T, W = p3.shape
    bq, bk = min(SB_BQ, T), min(SB_BK, T)
    rows = min(SB_ROWS, T)
    scale = SB_DH ** -0.5

    def body(p_ref, gq_ref, gk_ref, og_ref, o_ref, qn_ref, kn_ref, v_ref):
        lo = _sb_half_mask()

        def prologue(i, c):
            r0 = pl.multiple_of(i * rows, rows)
            sl = pl.ds(r0, rows)
            qn_ref[sl, :] = (_sb_headnorm(p_ref[0, sl, :], gq_ref[...], lo)[0] * scale).astype(BF16)
            kn_ref[sl, :] = _sb_headnorm(p_ref[1, sl, :], gk_ref[...], lo)[0].astype(BF16)
            v_ref[sl, :] = p_ref[2, sl, :].astype(BF16)
            return c

        lax.fori_loop(0, T // rows, prologue, 0)

        rel = _sb_rel(bq, bk)
        tri = (lax.broadcasted_iota(jnp.int32, (bk, bk), 0)
               > lax.broadcasted_iota(jnp.int32, (bk, bk), 1)).astype(BF16)

        def qblock(qi, c):
            q0 = pl.multiple_of(qi * bq, bq)
            qm = _sb_stack(qn_ref[pl.ds(q0, bq), :], lo)
            nkb = (q0 + bq - 1) // bk + 1

            def tiles(k0s, carry, valid):
                o_acc, a_carry = carry
                sc = [_sb_tile(qm, kn_ref[pl.ds(k0, bk), :], valid) for k0 in k0s]
                later = _dot_x2_l([log1m for _, log1m in sc], tri)
                for (logsig, log1m), lat, k0 in zip(sc, later, k0s):
                    wts = jnp.exp(logsig + (lat + a_carry))
                    if valid is not None:
                        wts = jnp.where(valid, wts, 0.0)
                    o_acc = o_acc + jnp.dot(_bf(wts), v_ref[pl.ds(k0, bk), :], preferred_element_type=F32)
                    a_carry = a_carry + jnp.sum(log1m, axis=-1, keepdims=True)
                return o_acc, a_carry

            blk0 = lambda j: pl.multiple_of(j * bk, bk)
            k_last = blk0(nkb - 1)
            o2, t2 = tiles([k_last], (jnp.zeros((2 * bq, LANES), F32), jnp.zeros((2 * bq, 1), F32)), rel < q0 - k_last)

            def alive(st):
                return jnp.logical_and(st[0] < nkb - 1, jnp.max(st[2]) > SB_DEAD)

            def back_one(st):
                return (st[0] + 1,) + tiles([blk0(nkb - 2 - st[0])], st[1:], None)

            _, o2, _ = lax.while_loop(alive, back_one, (jnp.int32(0), o2, t2))
            o = jnp.where(lo, o2[:bq], o2[bq:])
            o_ref[pl.ds(q0, bq), :] = o
            og_ref[pl.ds(q0, bq), :] = (o * _silu(p_ref[3, pl.ds(q0, bq), :])).astype(BF16)
            return c

        lax.fori_loop(0, T // bq, qblock, 0)

    colblk = pl.BlockSpec((T, LANES), lambda j: (0, j))
    vec = pl.BlockSpec((1, LANES), lambda j: (0, 0))
    return pl.pallas_call(
        body, name=name, grid=(W // LANES,),
        in_specs=[pl.BlockSpec((4, T, LANES), lambda j: (0, 0, j)), vec, vec],
        out_specs=[colblk, colblk],
        out_shape=[jax.ShapeDtypeStruct((T, W), BF16), jax.ShapeDtypeStruct((T, W), F32)],
        scratch_shapes=[pltpu.VMEM((T, LANES), BF16)] * 3,
        compiler_params=pltpu.CompilerParams(dimension_semantics=("parallel",), vmem_limit_bytes=VMEM_BIG),
    )(p3, gq2, gk2)


def _sb_attn_bwd(p3, gq2, gk2, o, dog, *, name):
    _, T, W = p3.shape
    bq, bk = min(SB_BQ, T), min(SB_BK, T)
    rows = min(SB_ROWS, T)
    scale = SB_DH ** -0.5

    def body(p_ref, gq_ref, gk_ref, o_ref, dog_ref, dp_ref, dgq_ref, dgk_ref,
             qn_ref, kn_ref, v_ref, do_ref):
        lo = _sb_half_mask()

        def prologue(i, c):
            r0 = pl.multiple_of(i * rows, rows)
            sl = pl.ds(r0, rows)
            qn_ref[sl, :] = (_sb_headnorm(p_ref[0, sl, :], gq_ref[...], lo)[0] * scale).astype(BF16)
            kn_ref[sl, :] = _sb_headnorm(p_ref[1, sl, :], gk_ref[...], lo)[0].astype(BF16)
            v_ref[sl, :] = p_ref[2, sl, :].astype(BF16)
            gate = p_ref[3, sl, :]
            dogv = dog_ref[sl, :]
            dp_ref[3, sl, :] = dogv * o_ref[sl, :] * _dsilu(gate)
            do_ref[sl, :] = (dogv * _silu(gate)).astype(BF16)
            zero = jnp.zeros((rows, LANES), F32)
            dp_ref[0, sl, :] = zero
            dp_ref[1, sl, :] = zero
            dp_ref[2, sl, :] = zero
            return c

        lax.fori_loop(0, T // rows, prologue, 0)

        rel = _sb_rel(bq, bk)
        rj = lax.broadcasted_iota(jnp.int32, (bk, bk), 0)
        cj = lax.broadcasted_iota(jnp.int32, (bk, bk), 1)
        upto = (rj <= cj).astype(BF16)
        before_m = (rj < cj).astype(BF16)

        def qblock(qi, c):
            q0 = pl.multiple_of(qi * bq, bq)
            qm = _sb_stack(qn_ref[pl.ds(q0, bq), :], lo)
            dom = _sb_stack(do_ref[pl.ds(q0, bq), :], lo)
            nkb = (q0 + bq - 1) // bk + 1
            blk0 = lambda j: pl.multiple_of(j * bk, bk)
            k_last = blk0(nkb - 1)

            def row_sums(k0, valid):
                return jnp.sum(_sb_tile(qm, kn_ref[pl.ds(k0, bk), :], valid)[1], axis=-1, keepdims=True)

            def alive(st):
                return jnp.logical_and(st[0] < nkb, jnp.max(st[1]) > SB_DEAD)

            def back_one(st):
                return st[0] + 1, st[1] + row_sums(blk0(nkb - 1 - st[0]), None)

            n_live, total = lax.while_loop(alive, back_one, (jnp.int32(1), row_sums(k_last, rel < q0 - k_last)))
            k_first = nkb - n_live

            def tiles(k0s, carry, valid):
                dq_acc, a_pre, r_pre = carry
                kss = [pl.ds(k0, bk) for k0 in k0s]
                kbs = [kn_ref[ks, :] for ks in kss]
                sc = [_sb_tile(qm, kb, valid) for kb in kbs]
                dws = [lax.dot_general(dom, v_ref[ks, :], _NT, preferred_element_type=F32) for ks in kss]
                upto_l = _dot_x2_l([log1m for _, log1m in sc], upto)
                wts_l = []
                for (logsig, log1m), up in zip(sc, upto_l):
                    wts = jnp.exp(logsig + ((total - a_pre) - up))
                    wts_l.append(wts if valid is None else jnp.where(valid, wts, 0.0))
                    a_pre = a_pre + jnp.sum(log1m, axis=-1, keepdims=True)
                ee_l = [dw * wts for dw, wts in zip(dws, wts_l)]
                before_l = _dot_x2_l(ee_l, before_m)
                for (logsig, _), ks, kb, wts, ee, bef in zip(sc, kss, kbs, wts_l, ee_l, before_l):
                    beta = jnp.exp(logsig)
                    dz = ee * (1.0 - beta) - beta * (r_pre + bef)
                    if valid is not None:
                        dz = jnp.where(valid, dz, 0.0)
                    dzb = _bf(dz)
                    dq_acc = dq_acc + jnp.dot(dzb, kb, preferred_element_type=F32)
                    dp_ref[1, ks, :] += lax.dot_general(dzb, qm, _TN, preferred_element_type=F32)
                    dp_ref[2, ks, :] += lax.dot_general(_bf(wts), dom, _TN, preferred_element_type=F32)
                    r_pre = r_pre + jnp.sum(ee, axis=-1, keepdims=True)
                return dq_acc, a_pre, r_pre

            cr = (jnp.zeros((2 * bq, LANES), F32), jnp.zeros((2 * bq, 1), F32), jnp.zeros((2 * bq, 1), F32))
            cr = lax.fori_loop(0, (n_live - 1) // 2,
                               lambda t, cr: tiles([blk0(k_first + 2 * t), blk0(k_first + 2 * t + 1)], cr, None), cr)
            cr = lax.fori_loop(0, (n_live - 1) % 2, lambda t, cr: tiles([blk0(nkb - 2)], cr, None), cr)
            dq2, _, _ = tiles([k_last], cr, rel < q0 - k_last)
            dp_ref[0, pl.ds(q0, bq), :] = jnp.where(lo, dq2[:bq], dq2[bq:]) * scale
            return c

        lax.fori_loop(0, T // bq, qblock, 0)

        dgq_ref[...] = jnp.zeros_like(dgq_ref)
        dgk_ref[...] = jnp.zeros_like(dgk_ref)

        def epilogue(i, c):
            r0 = pl.multiple_of(i * rows, rows)
            sl = pl.ds(r0, rows)
            for part, g_ref, dg_ref in ((0, gq_ref, dgq_ref), (1, gk_ref, dgk_ref)):
                _, xh, r = _sb_headnorm(p_ref[part, sl, :], g_ref[...], lo)
                dn = dp_ref[part, sl, :]
                dxh = dn * g_ref[...]
                prod = dxh * xh
                m_lo = jnp.sum(jnp.where(lo, prod, 0.0), axis=-1, keepdims=True)
                m_hi = jnp.sum(jnp.where(lo, 0.0, prod), axis=-1, keepdims=True)
                m = jnp.where(lo, m_lo, m_hi) * (1.0 / SB_DH)
                dp_ref[part, sl, :] = r * (dxh - xh * m)
                dg_ref[...] += jnp.sum(dn * xh, axis=0, keepdims=True)
            return c

        lax.fori_loop(0, T // rows, epilogue, 0)

    colblk = pl.BlockSpec((T, LANES), lambda j: (0, j))
    vec = pl.BlockSpec((1, LANES), lambda j: (0, 0))
    part = pl.BlockSpec((4, T, LANES), lambda j: (0, 0, j))
    gvec = pl.BlockSpec((None, 1, LANES), lambda j: (j, 0, 0))
    npair = W // LANES
    return pl.pallas_call(
        body, name=name, grid=(npair,),
        in_specs=[part, vec, vec, colblk, colblk],
        out_specs=[part, gvec, gvec],
        out_shape=[jax.ShapeDtypeStruct((4, T, W), F32), jax.ShapeDtypeStruct((npair, 1, LANES), F32),
                   jax.ShapeDtypeStruct((npair, 1, LANES), F32)],
        scratch_shapes=[pltpu.VMEM((T, LANES), BF16)] * 4,
        compiler_params=pltpu.CompilerParams(dimension_semantics=("parallel",), vmem_limit_bytes=VMEM_BIG),
    )(p3, gq2, gk2, o, dog)


_NN = (((1,), (0,)), ((), ()))
_NT = (((1,), (1,)), ((), ()))
_TN = (((0,), (0,)), ((), ()))
DN_TB = 512
DN_HEADS_PER_STEP = 2
DN_AB_COL = (DN_CONV_W + DN_V_W) // LANES


def _dn_conv(x, w_ref):
    k = w_ref.shape[0]
    return sum(w_ref[i:i + 1, :] * _shift_down(x, k - 1 - i) for i in range(k))


def _dn_prep_fwd(p, conv_w, *, name):
    T = p.shape[0]
    cw = conv_w.shape[1]
    n_qk = 2 * DN_QK_W // LANES

    def body(p_ref, w_ref, o_ref):
        s = _silu(_dn_conv(p_ref[...], w_ref))
        r = lax.rsqrt(jnp.sum(s * s, axis=-1, keepdims=True) + L2_EPS)
        o_ref[...] = jnp.where(pl.program_id(0) < n_qk, s * r, s)

    colblk = pl.BlockSpec((T, LANES), lambda j: (0, j))
    return pl.pallas_call(
        body, name=name, grid=(cw // LANES,),
        in_specs=[colblk, pl.BlockSpec((DN_CONV, LANES), lambda j: (0, j))],
        out_specs=colblk, out_shape=jax.ShapeDtypeStruct((T, cw), F32),
        compiler_params=pltpu.CompilerParams(dimension_semantics=("parallel",), vmem_limit_bytes=VMEM_BIG),
    )(p, conv_w)


def _dn_chunk_tri(rows, upper):
    r = lax.broadcasted_iota(jnp.int32, (rows, rows), 0)
    c = lax.broadcasted_iota(jnp.int32, (rows, rows), 1)
    same = (r // DN_CHUNK) == (c // DN_CHUNK)
    return jnp.logical_and(same, (c >= r) if upper else (c <= r)).astype(BF16)


def _dn_lane_rows(a_log, dt_bias):
    pad = lambda v: jnp.zeros((1, LANES), F32).at[0, :DN_HEADS].set(v)
    return pad(a_log), pad(dt_bias)


def _dn_ab_parts(blk, alog_row, dtb_row):
    lane = lax.broadcasted_iota(jnp.int32, (1, LANES), 1)
    is_a = lane < DN_HEADS
    is_b = jnp.logical_and(lane >= DN_HEADS, lane < 2 * DN_HEADS)
    a_arg = jnp.where(is_a, blk + dtb_row, 0.0)
    neg_exp = jnp.where(is_a, -jnp.exp(alog_row), 0.0)
    log_a = neg_exp * _softplus(a_arg)
    beta = jnp.where(is_b, _sigmoid(blk), 0.0)
    return is_a, is_b, a_arg, neg_exp, log_a, beta


def _dn_ab_fwd(p, alog_row, dtb_row, *, name):
    T = p.shape[0]
    rows = min(DN_TB, T)

    def body(p_ref, al_ref, dt_ref, o_ref):
        _, _, _, _, log_a, beta = _dn_ab_parts(p_ref[...], al_ref[...], dt_ref[...])
        hi, mid, lo_ = _split3(log_a)
        tri = _dn_chunk_tri(rows, upper=False)
        f = lambda q: jnp.dot(tri, q, preferred_element_type=F32)
        o_ref[...] = (f(hi) + f(mid) + f(lo_)) + beta

    blk = pl.BlockSpec((rows, LANES), lambda i: (i, DN_AB_COL))
    vec = pl.BlockSpec((1, LANES), lambda i: (0, 0))
    return pl.pallas_call(
        body, name=name, grid=(T // rows,), in_specs=[blk, vec, vec],
        out_specs=pl.BlockSpec((rows, LANES), lambda i: (i, 0)),
        out_shape=jax.ShapeDtypeStruct((T, LANES), F32),
        compiler_params=pltpu.CompilerParams(dimension_semantics=("parallel",)),
    )(p, alog_row, dtb_row)


def _hp_l(a_l, b_l, dims=_NN):
    sa = [_split3(a)[:2] for a in a_l]
    sb = [_split3(b)[:2] for b in b_l]
    f = lambda p, q: lax.dot_general(p, q, dims, preferred_element_type=F32)
    hh = [f(x[0], y[0]) for x, y in zip(sa, sb)]
    hm = [f(x[0], y[1]) for x, y in zip(sa, sb)]
    mh = [f(x[1], y[0]) for x, y in zip(sa, sb)]
    return [a + (b + c) for a, b, c in zip(hh, hm, mh)]


def _dn_local(qs, k, v, g, beta, nc):
    c = DN_CHUNK
    cut = lambda x: [x[i * c:(i + 1) * c] for i in range(nc)]
    row = lax.broadcasted_iota(jnp.int32, (c, c), 0)
    col = lax.broadcasted_iota(jnp.int32, (c, c), 1)
    eye, lower, strict = row == col, row >= col, row > col
    rowid = lax.broadcasted_iota(jnp.int32, (c, 1), 0)
    eg = jnp.exp(g)
    kb = k * beta
    rhs_k = kb * eg
    g_l, k_l, kb_l, qs_l = cut(g), cut(k), cut(kb), cut(qs)
    g_row_l = [jnp.sum(jnp.where(eye, x, 0.0), axis=0, keepdims=True) for x in g_l]
    dec_l = [jnp.where(lower, jnp.exp(jnp.where(lower, x - y, 0.0)), 0.0) for x, y in zip(g_l, g_row_l)]
    kk_l = [_dot_nt(a, b) for a, b in zip(kb_l, k_l)]
    qk_l = [_dot_nt(a, b) for a, b in zip(qs_l, k_l)]
    low_l = [jnp.where(strict, a * d, 0.0) for a, d in zip(kk_l, dec_l)]
    eye_f = eye.astype(F32)
    pw_l = [-x for x in low_l]
    inv_l = [eye_f + x for x in pw_l]
    for _ in range(int(math.log2(c)) - 1):
        pw_l = _hp_l(pw_l, pw_l)
        inv_l = [a + b for a, b in zip(inv_l, _hp_l(inv_l, pw_l))]
    u_l = [_dot(a, b) for a, b in zip(inv_l, cut(v * beta))]
    w_l = [_dot(a, b) for a, b in zip(inv_l, cut(rhs_k))]
    aqk_l = [jnp.where(lower, a * d, 0.0) for a, d in zip(qk_l, dec_l)]
    g_last_l = [jnp.sum(jnp.where(rowid == c - 1, x, 0.0), axis=0, keepdims=True) for x in g_l]
    ekd_l = [jnp.exp(a - b) for a, b in zip(g_last_l, g_l)]
    kd_l = [a * b for a, b in zip(k_l, ekd_l)]
    qd_l = cut(qs * eg)
    kw_l = [_dot_tn(a, b) for a, b in zip(kd_l, w_l)]
    qp_l = [q - _dot(a, w) for q, a, w in zip(qd_l, aqk_l, w_l)]
    return dict(eye=eye, lower=lower, strict=strict, dec=dec_l, k=k_l, kb=kb_l, qs=qs_l, low=low_l, inv=inv_l,
                eg=cut(eg), rhs_k=cut(rhs_k), u=u_l, w=w_l, aqk=aqk_l, g_last=g_last_l, qd=qd_l,
                ekd=ekd_l, kd=kd_l, kw=kw_l, qp=qp_l)


def _dn_head_cols(gb_blk, head):
    lane = lax.broadcasted_iota(jnp.int32, (1, LANES), 1)
    g = jnp.sum(jnp.where(lane == head, gb_blk, 0.0), axis=-1, keepdims=True)
    beta = jnp.sum(jnp.where(lane == head + DN_HEADS, gb_blk, 0.0), axis=-1, keepdims=True)
    return g, beta


def _halves_over_ici(s_refs, o_refs, send_sems, recv_sems, first, last):
    x, y, c = _mesh_pos()
    me = 2 * x + y
    chips = _other_chips(x, y)
    pairs = [(a, k) for a in range(len(s_refs)) for k in range(3)]

    def copy(a, k, slot):
        px, py = chips[k]
        return pltpu.make_async_remote_copy(
            src_ref=s_refs[a].at[c], dst_ref=o_refs[a].at[slot, c], send_sem=send_sems.at[3 * a + k],
            recv_sem=recv_sems.at[3 * a + k], device_id=(px, py, c), device_id_type=MESH)

    @pl.when(first)
    def _():
        for a, k in pairs:
            copy(a, k, me).start()

    @pl.when(last)
    def _():
        for a, k in pairs:
            px, py = chips[k]
            copy(a, k, 2 * px + py).wait_recv()
        for a, k in pairs:
            copy(a, k, me).wait_send()


def _dn_delta_fwd(qkv, gb, p, o_gain, *, name, send=()):
    T = qkv.shape[0]
    tb = min(DN_TB, T)
    nb, nc = T // tb, tb // DN_CHUNK
    H = DN_HEADS
    qscale = DN_DK ** -0.5
    ns = len(send)
    hp = DN_HEADS_PER_STEP

    def body(*refs):
        q_ref, k_ref, v_ref, gb_ref, gate_ref, gain_ref = refs[:6]
        o_ref, og_ref, st_ref = refs[6 + ns:9 + ns]
        s_ref = refs[9 + 2 * ns]
        pair, blk = pl.program_id(0), pl.program_id(1)
        if ns:
            _halves_over_ici(refs[6:6 + ns], refs[9 + ns:9 + 2 * ns], refs[10 + 2 * ns], refs[11 + 2 * ns],
                             jnp.logical_and(pair == 0, blk == 0),
                             jnp.logical_and(pair == H // hp - 1, blk == nb - 1))

        @pl.when(blk == 0)
        def _():
            s_ref[...] = jnp.zeros_like(s_ref)

        gbv = gb_ref[...]
        ts, ku, op = [], [], []
        for e in range(hp):
            qk_e, v_e = slice(e * DN_DK, (e + 1) * DN_DK), slice(e * DN_DV, (e + 1) * DN_DV)
            g, beta = _dn_head_cols(gbv, hp * pair + e)
            t = _dn_local(q_ref[:, qk_e] * qscale, k_ref[:, qk_e], v_ref[:, v_e], g, beta, nc)
            ts.append(t)
            ku.append([_dot_tn(a, b) for a, b in zip(t["kd"], t["u"])])
            op.append([_dot(a, b) for a, b in zip(t["aqk"], t["u"])])
        s32 = [s_ref[e] for e in range(hp)]
        s_l = [[] for _ in range(hp)]
        for i in range(nc):
            sb = [_bf(x) for x in s32]
            for e in range(hp):
                st_ref[e, i] = sb[e]
                s_l[e].append(sb[e])
            prod = [_dot(ts[e]["kw"][i], sb[e]) for e in range(hp)]
            s32 = [s32[e] * jnp.exp(ts[e]["g_last"][i]) - prod[e] + ku[e][i] for e in range(hp)]
        for e in range(hp):
            s_ref[e] = s32[e]
        o = jnp.concatenate(
            [jnp.concatenate([_dot(qp, sb) + x for qp, sb, x in zip(ts[e]["qp"], s_l[e], op[e])], axis=0)
             for e in range(hp)], axis=1)
        o_ref[...] = o
        gain = gain_ref[...]
        for e in range(hp):
            v_e = slice(e * DN_DV, (e + 1) * DN_DV)
            oe = o[:, v_e]
            r = lax.rsqrt(jnp.mean(oe * oe, axis=-1, keepdims=True) + RMS_EPS)
            og_ref[:, v_e] = (((oe * r) * gain) * _silu(gate_ref[:, v_e])).astype(BF16)

    qk = lambda col0: pl.BlockSpec((tb, hp * DN_DK), lambda h, i: (i, col0 // (hp * DN_DK) + h))
    vblk = lambda col0: pl.BlockSpec((tb, hp * DN_DV), lambda h, i: (i, col0 // (hp * DN_DV) + h))
    return pl.pallas_call(
        body, name=name, grid=(H // hp, nb),
        in_specs=[qk(0), qk(DN_QK_W), vblk(2 * DN_QK_W), pl.BlockSpec((tb, LANES), lambda h, i: (i, 0)),
                  vblk(DN_CONV_W), pl.BlockSpec((1, DN_DV), lambda h, i: (0, 0))] + [HBM] * ns,
        out_specs=[vblk(0), vblk(0), pl.BlockSpec((hp, nc, DN_DK, DN_DV), lambda h, i: (h, i, 0, 0))] + [HBM] * ns,
        out_shape=[jax.ShapeDtypeStruct((T, DN_V_W), F32), jax.ShapeDtypeStruct((T, DN_V_W), BF16),
                   jax.ShapeDtypeStruct((H, T // DN_CHUNK, DN_DK, DN_DV), BF16)]
        + [jax.ShapeDtypeStruct((N_CHIPS,) + a.shape, a.dtype) for a in send],
        scratch_shapes=[pltpu.VMEM((hp, DN_DK, DN_DV), F32)]
        + ([pltpu.SemaphoreType.DMA((3 * ns,)), pltpu.SemaphoreType.DMA((3 * ns,))] if ns else []),
        compiler_params=pltpu.CompilerParams(dimension_semantics=("arbitrary", "arbitrary")),
    )(qkv, qkv, qkv, gb, p, o_gain, *send)


def _blocks_over_ici(p_refs, o_refs, send_sems, recv_sems, first, last):
    x, y, c = _mesh_pos()
    me = 2 * x + y
    chips = _other_chips(x, y)
    pairs = [(a, k) for a in range(len(p_refs)) for k in range(3)]

    def copy(a, k, slot):
        px, py = chips[k]
        return pltpu.make_async_remote_copy(
            src_ref=p_refs[a].at[2 * px + py], dst_ref=o_refs[a].at[slot], send_sem=send_sems.at[3 * a + k],
            recv_sem=recv_sems.at[3 * a + k], device_id=(px, py, c), device_id_type=MESH)

    @pl.when(first)
    def _():
        for a, k in pairs:
            copy(a, k, me).start()

    @pl.when(last)
    def _():
        for a, k in pairs:
            px, py = chips[k]
            copy(a, k, 2 * px + py).wait_recv()
        for a, k in pairs:
            copy(a, k, me).wait_send()


def _dn_delta_bwd(qkv, gb, p, o_gain, o, states, dog, *, name, send=()):
    T = qkv.shape[0]
    tb = min(DN_TB, T)
    nb, nc = T // tb, tb // DN_CHUNK
    H = DN_HEADS
    qscale = DN_DK ** -0.5
    ns = len(send)
    hp = DN_HEADS_PER_STEP

    def body(*refs):
        q_ref, k_ref, v_ref, gb_ref, gate_ref, gain_ref, o_ref, st_ref, dog_ref = refs[:9]
        dq_ref, dk_ref, dv_ref, dgate_ref, dgb_ref, dgain_ref = refs[9 + ns:15 + ns]
        ds_ref = refs[15 + 2 * ns]
        pair, blk = pl.program_id(0), pl.program_id(1)
        first = jnp.logical_and(pair == 0, blk == 0)
        if ns:
            _blocks_over_ici(refs[9:9 + ns], refs[15 + ns:15 + 2 * ns], refs[16 + 2 * ns], refs[17 + 2 * ns],
                             first, jnp.logical_and(pair == H // hp - 1, blk == nb - 1))

        @pl.when(blk == 0)
        def _():
            ds_ref[...] = jnp.zeros_like(ds_ref)

        @pl.when(first)
        def _():
            dgain_ref[...] = jnp.zeros_like(dgain_ref)

        lane = lax.broadcasted_iota(jnp.int32, (1, LANES), 1)
        c = DN_CHUNK
        cut = lambda x: [x[i * c:(i + 1) * c] for i in range(nc)]
        cat = lambda xs: jnp.concatenate(xs, axis=0)
        rsum = lambda x: jnp.sum(x, axis=-1, keepdims=True)
        gbv, gain = gb_ref[...], gain_ref[...]

        def before_chain(e):
            qk_e, v_e = slice(e * DN_DK, (e + 1) * DN_DK), slice(e * DN_DV, (e + 1) * DN_DV)
            g, beta = _dn_head_cols(gbv, hp * pair + e)
            ov, gate, dogv = o_ref[:, v_e], gate_ref[:, v_e], dog_ref[:, v_e]
            r = lax.rsqrt(jnp.mean(ov * ov, axis=-1, keepdims=True) + RMS_EPS)
            oh = ov * r
            dnrm = dogv * _silu(gate)
            dgate_ref[:, v_e] = dogv * (oh * gain) * _dsilu(gate)
            doh = dnrm * gain
            do_l = cut(r * (doh - oh * jnp.mean(doh * oh, axis=-1, keepdims=True)))
            dgain_ref[...] += jnp.sum(dnrm * oh, axis=0, keepdims=True)
            k, v = k_ref[:, qk_e], v_ref[:, v_e]
            t = _dn_local(q_ref[:, qk_e] * qscale, k, v, g, beta, nc)
            s_l = [st_ref[e, i] for i in range(nc)]
            vn_l = [u - _dot(w, sb) for u, w, sb in zip(t["u"], t["w"], s_l)]
            return dict(
                t=t, beta=beta, v=v, s=s_l, vn=vn_l, egl=[jnp.exp(x) for x in t["g_last"]],
                dqd=[_dot_nt(a, sb) for a, sb in zip(do_l, s_l)], daqk=[_dot_nt(a, b) for a, b in zip(do_l, vn_l)],
                aqk_do=[_dot_tn(a, b) for a, b in zip(t["aqk"], do_l)],
                qp_do=[_dot_tn(a, b) for a, b in zip(t["qp"], do_l)])

        hs = [before_chain(e) for e in range(hp)]
        ds = [ds_ref[e] for e in range(hp)]
        ds_l = [[None] * nc for _ in range(hp)]
        for i in reversed(range(nc)):
            for e in range(hp):
                ds_l[e][i] = ds[e]
            prod = [_dot_tn(hs[e]["t"]["kw"][i], ds[e]) for e in range(hp)]
            ds = [ds[e] * hs[e]["egl"][i] - prod[e] + hs[e]["qp_do"][i] for e in range(hp)]
        for e in range(hp):
            ds_ref[e] = ds[e]

        def after_chain(e):
            hd, t = hs[e], hs[e]["t"]
            lower, strict, eye = t["lower"], t["strict"], t["eye"]
            s_l, vn_l, dqd_l, daqk_l, egl_l, beta, v = (hd["s"], hd["vn"], hd["dqd"], hd["daqk"], hd["egl"],
                                                         hd["beta"], hd["v"])
            dvn_l = [a + _dot(kd, d) for a, kd, d in zip(hd["aqk_do"], t["kd"], ds_l[e])]
            dkd_l = [_dot_nt(a, d) for a, d in zip(vn_l, ds_l[e])]
            dgl_l = [jnp.sum(rsum(d * sb.astype(F32)), axis=0, keepdims=True) * x
                     for d, sb, x in zip(ds_l[e], s_l, egl_l)]
            dw_l = [-_dot_nt(a, sb) for a, sb in zip(dvn_l, s_l)]
            dbv_l = [_dot_tn(a, b) for a, b in zip(t["inv"], dvn_l)]
            dbk_l = [_dot_tn(a, b) for a, b in zip(t["inv"], dw_l)]
            dlow_l = [-(_dot_nt(a, b) + _dot_nt(x, y)) for a, b, x, y in zip(dbv_l, t["u"], dbk_l, t["w"])]
            m_l = [jnp.where(strict, a * d, 0.0) for a, d in zip(dlow_l, t["dec"])]
            nmat_l = [jnp.where(lower, a * d, 0.0) for a, d in zip(daqk_l, t["dec"])]
            dkb_l = [_dot(m, kk) + b * x for m, kk, b, x in zip(m_l, t["k"], dbk_l, t["eg"])]
            dqs_l = [_dot(n, kk) + a * x for n, kk, a, x in zip(nmat_l, t["k"], dqd_l, t["eg"])]
            dk1_l = [_dot_tn(m, kb) for m, kb in zip(m_l, t["kb"])]
            dk2_l = [_dot_tn(n, q) for n, q in zip(nmat_l, t["qs"])]
            beta_l, v_l = cut(beta), cut(v)
            rowid = lax.broadcasted_iota(jnp.int32, (c, 1), 0)
            dk_l, dg_l, dbeta_l = [], [], []
            for i in range(nc):
                dk_l.append(dk1_l[i] + dk2_l[i] + dkd_l[i] * t["ekd"][i] + dkb_l[i] * beta_l[i])
                gmat = jnp.where(strict, dlow_l[i] * t["low"][i], 0.0) + daqk_l[i] * t["aqk"][i]
                s_kd = rsum(dkd_l[i] * t["kd"][i])
                dg = (rsum(gmat) + rsum(dqd_l[i] * t["qd"][i]) - s_kd + rsum(dbk_l[i] * t["rhs_k"][i]))
                dg_row = -jnp.sum(gmat, axis=0, keepdims=True)
                dg = dg + rsum(jnp.where(eye, dg_row, 0.0))
                dgl = dgl_l[i] + jnp.sum(s_kd, axis=0, keepdims=True)
                dg_l.append(dg + jnp.where(rowid == c - 1, dgl, 0.0))
                dbeta_l.append(rsum(dbv_l[i] * v_l[i]) + rsum(dkb_l[i] * t["k"][i]))
            head = hp * pair + e
            dgb = (jnp.where(lane == head, cat(dg_l), 0.0) + jnp.where(lane == head + DN_HEADS, cat(dbeta_l), 0.0))
            return cat(dqs_l) * qscale, cat(dk_l), cat(dbv_l) * beta, dgb

        for e in range(hp):
            dq, dk, dv, dgb = after_chain(e)
            dq_ref[:, e * DN_DK:(e + 1) * DN_DK] = dq
            dk_ref[:, e * DN_DK:(e + 1) * DN_DK] = dk
            dv_ref[:, e * DN_DV:(e + 1) * DN_DV] = dv
            dgb_ref[e] = dgb

    rev = lambda i: nb - 1 - i
    qk = lambda col0: pl.BlockSpec((tb, hp * DN_DK), lambda h, i: (rev(i), col0 // (hp * DN_DK) + h))
    vblk = lambda col0: pl.BlockSpec((tb, hp * DN_DV), lambda h, i: (rev(i), col0 // (hp * DN_DV) + h))
    gain_spec = pl.BlockSpec((1, DN_DV), lambda h, i: (0, 0))
    return pl.pallas_call(
        body, name=name, grid=(H // hp, nb),
        in_specs=[qk(0), qk(DN_QK_W), vblk(2 * DN_QK_W), pl.BlockSpec((tb, LANES), lambda h, i: (rev(i), 0)),
                  vblk(DN_CONV_W), gain_spec, vblk(0),
                  pl.BlockSpec((hp, nc, DN_DK, DN_DV), lambda h, i: (h, rev(i), 0, 0)), vblk(0)] + [HBM] * ns,
        out_specs=[qk(0), qk(0), vblk(0), vblk(DN_CONV_W),
                   pl.BlockSpec((hp, tb, LANES), lambda h, i: (h, rev(i), 0)), gain_spec] + [HBM] * ns,
        out_shape=[jax.ShapeDtypeStruct((T, DN_QK_W), F32), jax.ShapeDtypeStruct((T, DN_QK_W), F32),
                   jax.ShapeDtypeStruct((T, DN_V_W), F32), jax.ShapeDtypeStruct((T, DN_IN_PAD), F32),
                   jax.ShapeDtypeStruct((H, T, LANES), F32), jax.ShapeDtypeStruct((1, DN_DV), F32)]
        + [jax.ShapeDtypeStruct(a.shape, a.dtype) for a in send],
        scratch_shapes=[pltpu.VMEM((hp, DN_DK, DN_DV), F32)]
        + ([pltpu.SemaphoreType.DMA((3 * ns,)), pltpu.SemaphoreType.DMA((3 * ns,))] if ns else []),
        compiler_params=pltpu.CompilerParams(dimension_semantics=("arbitrary", "arbitrary")),
    )(qkv, qkv, qkv, gb, p, o_gain, o, states, dog, *send)


def _dn_conv_bwd(p, conv_w, d, dp, *, first, normed, name):
    T, width = d.shape

    def body(p_ref, w_ref, d_ref, dp_in, dp_ref, dw_ref):
        del dp_in
        x = p_ref[...]
        ksz = w_ref.shape[0]
        xs = [_shift_down(x, ksz - 1 - i) for i in range(ksz)]
        xc = sum(w_ref[i:i + 1, :] * xs[i] for i in range(ksz))
        ds = d_ref[...]
        if normed:
            s = _silu(xc)
            r = lax.rsqrt(jnp.sum(s * s, axis=-1, keepdims=True) + L2_EPS)
            y = s * r
            ds = r * (ds - y * jnp.sum(ds * y, axis=-1, keepdims=True))
        dxc = ds * _dsilu(xc)
        dp_ref[...] = sum(w_ref[i:i + 1, :] * _shift_up(dxc, ksz - 1 - i) for i in range(ksz))
        for i in range(ksz):
            dw_ref[i:i + 1, :] = jnp.sum(dxc * xs[i], axis=0, keepdims=True)

    shifted = pl.BlockSpec((T, LANES), lambda j: (0, first + j))
    return pl.pallas_call(
        body, name=name, grid=(width // LANES,),
        in_specs=[shifted, pl.BlockSpec((DN_CONV, LANES), lambda j: (0, first + j)),
                  pl.BlockSpec((T, LANES), lambda j: (0, j)), pl.BlockSpec(memory_space=pl.ANY)],
        out_specs=[shifted, pl.BlockSpec((DN_CONV, LANES), lambda j: (0, j))],
        out_shape=[jax.ShapeDtypeStruct(dp.shape, F32), jax.ShapeDtypeStruct((DN_CONV, width), F32)],
        input_output_aliases={3: 0},
        compiler_params=pltpu.CompilerParams(dimension_semantics=("parallel",), vmem_limit_bytes=VMEM_BIG),
    )(p, conv_w, d, dp)


def _dn_ab_bwd(p, alog_row, dtb_row, dgb, dp, *, name):
    T = p.shape[0]
    rows = min(DN_TB, T)
    H = DN_HEADS

    def body(p_ref, al_ref, dt_ref, dgb_ref, dp_in, dp_ref, dal_ref, ddt_ref):
        del dp_in

        @pl.when(pl.program_id(0) == 0)
        def _():
            dal_ref[...] = jnp.zeros_like(dal_ref)
            ddt_ref[...] = jnp.zeros_like(ddt_ref)

        blk = p_ref[...]
        is_a, is_b, a_arg, neg_exp, log_a, beta = _dn_ab_parts(blk, al_ref[...], dt_ref[...])
        d = dgb_ref[0]
        for hh in range(1, H):
            d = d + dgb_ref[hh]
        hi, mid, lo_ = _split3(jnp.where(is_a, d, 0.0))
        tri = _dn_chunk_tri(rows, upper=True)
        f = lambda q: jnp.dot(tri, q, preferred_element_type=F32)
        dlog_a = f(hi) + f(mid) + f(lo_)
        da_in = dlog_a * neg_exp * _sigmoid(a_arg)
        db_in = jnp.where(is_b, d, 0.0) * beta * (1.0 - beta)
        dp_ref[...] = jnp.where(is_a, da_in, 0.0) + db_in
        dal_ref[...] += jnp.sum(dlog_a * log_a, axis=0, keepdims=True)
        ddt_ref[...] += jnp.sum(jnp.where(is_a, da_in, 0.0), axis=0, keepdims=True)

    blk = pl.BlockSpec((rows, LANES), lambda i: (i, DN_AB_COL))
    vec = pl.BlockSpec((1, LANES), lambda i: (0, 0))
    return pl.pallas_call(
        body, name=name, grid=(T // rows,),
        in_specs=[blk, vec, vec, pl.BlockSpec((H, rows, LANES), lambda i: (0, i, 0)),
                  pl.BlockSpec(memory_space=pl.ANY)],
        out_specs=[blk, vec, vec],
        out_shape=[jax.ShapeDtypeStruct(dp.shape, F32), jax.ShapeDtypeStruct((1, LANES), F32),
                   jax.ShapeDtypeStruct((1, LANES), F32)],
        input_output_aliases={4: 0},
        compiler_params=pltpu.CompilerParams(dimension_semantics=("arbitrary",)),
    )(p, alog_row, dtb_row, dgb, dp)


def _dn_layer_fwd(x, ng, w_in, conv_w, a_log, dt_bias, o_gain, w_out, tag, send=()):
    alog_row, dtb_row = _dn_lane_rows(a_log, dt_bias)
    gain = o_gain.reshape(1, DN_DV)
    h = _rmsnorm_fwd(x, ng, name=f"{tag}_norm")
    p = _matmul(h, w_in, mode="nn", name=f"{tag}_inproj")
    qkv = _dn_prep_fwd(p, conv_w, name=f"{tag}_prep")
    gb = _dn_ab_fwd(p, alog_row, dtb_row, name=f"{tag}_ab")
    o, og, states, *landed = _dn_delta_fwd(qkv, gb, p, gain, name=f"{tag}_delta", send=send)
    x_new = _matmul(og, w_out, mode="nn", res=x, name=f"{tag}_outproj")
    return x_new, (h, p, qkv, gb, o, og, states), landed


def _dn_layer_bwd(dx, x, ng, w_in, conv_w, a_log, dt_bias, o_gain, w_out, saved, tag, send=()):
    h, p, qkv, gb, o, og, states = saved
    alog_row, dtb_row = _dn_lane_rows(a_log, dt_bias)
    gain = o_gain.reshape(1, DN_DV)
    d_wout = _matmul(og, dx, mode="tn", out_dtype=BF16, name=f"{tag}_dwout")
    dog = _matmul(dx, w_out, mode="nt", name=f"{tag}_dog")
    dq, dk, dv, dp, dgb, dgain, *landed = _dn_delta_bwd(qkv, gb, p, gain, o, states, dog, name=f"{tag}_deltabwd",
                                                        send=send)
    n_qk = DN_QK_W // LANES
    dp, dconv_q = _dn_conv_bwd(p, conv_w, dq, dp, first=0, normed=True, name=f"{tag}_convbwd_q")
    dp, dconv_k = _dn_conv_bwd(p, conv_w, dk, dp, first=n_qk, normed=True, name=f"{tag}_convbwd_k")
    dp, dconv_v = _dn_conv_bwd(p, conv_w, dv, dp, first=2 * n_qk, normed=False, name=f"{tag}_convbwd_v")
    dconv = jnp.concatenate([dconv_q, dconv_k, dconv_v], axis=1)
    dp, dal, ddt = _dn_ab_bwd(p, alog_row, dtb_row, dgb, dp, name=f"{tag}_abbwd")
    d_win = _matmul(h, dp, mode="tn", name=f"{tag}_dwin")
    dh = _matmul(dp, w_in, mode="nt", name=f"{tag}_dh")
    dx_prev, dng = _rmsnorm_bwd(x, ng, dh, dx, name=f"{tag}_normbwd")
    return dx_prev, dng, d_win, dconv, dal[0, :DN_HEADS], ddt[0, :DN_HEADS], dgain[0], d_wout, landed


def _sb_gains(g):
    return jnp.concatenate([g, g]).reshape(1, LANES)


def _sb_layer_fwd(x, ng, w_in, gq, gk, w_out, tag):
    h = _rmsnorm_fwd(x, ng, name=f"{tag}_norm")
    p3 = _matmul(h, w_in, mode="nn", b_parts=4, out_parts=4, name=f"{tag}_inproj")
    og, o = _sb_attn_fwd(p3, _sb_gains(gq), _sb_gains(gk), name=f"{tag}_attn")
    x_new = _matmul(og, w_out, mode="nn", res=x, name=f"{tag}_outproj")
    return x_new, (h, p3, og, o)


def _sb_layer_bwd(dx, x, ng, w_in, gq, gk, w_out, saved, tag):
    h, p3, og, o = saved
    d_wout = _matmul(og, dx, mode="tn", out_dtype=BF16, name=f"{tag}_dwout")
    dog = _matmul(dx, w_out, mode="nt", name=f"{tag}_dog")
    dp3, dgq, dgk = _sb_attn_bwd(p3, _sb_gains(gq), _sb_gains(gk), o, dog, name=f"{tag}_attnbwd")
    fold = lambda d: jnp.sum(d.reshape(-1, SB_DH), axis=0)
    d_win = _matmul(h, dp3, mode="tn", b_parts=4, out_parts=4, out_dtype=BF16, name=f"{tag}_dwin")
    dh = _matmul(dp3, w_in, mode="nt", a_parts=4, b_parts=4, name=f"{tag}_dh")
    dx_prev, dng = _rmsnorm_bwd(x, ng, dh, dx, name=f"{tag}_normbwd")
    return dx_prev, dng, d_win, fold(dgq), fold(dgk), d_wout


N_CHIPS = 4
HBM = pl.BlockSpec(memory_space=pl.ANY)


def _mesh_pos():
    return lax.axis_index("x"), lax.axis_index("y"), lax.axis_index("c")


def _other_chips(x, y):
    return [(1 - x, y), (x, 1 - y), (1 - x, 1 - y)]


def _chip_exchange(srcs, *, send_slot_is_dest, copy_own, name):
    n = len(srcs)

    def body(*refs):
        src_refs, out_refs = refs[:n], refs[n:2 * n]
        send_sems, recv_sems, local_sems = refs[2 * n:]
        x, y, c = _mesh_pos()
        me = 2 * x + y
        chips = _other_chips(x, y)
        local = []
        for a in range(n):
            if not copy_own[a]:
                continue
            own = src_refs[a].at[me] if send_slot_is_dest else src_refs[a]
            local.append(pltpu.make_async_copy(own, out_refs[a].at[me], local_sems.at[a]))
        for cp in local:
            cp.start()

        def copy(a, k, landing_slot):
            px, py = chips[k]
            src = src_refs[a].at[2 * px + py] if send_slot_is_dest else src_refs[a]
            return pltpu.make_async_remote_copy(
                src_ref=src, dst_ref=out_refs[a].at[landing_slot],
                send_sem=send_sems.at[a * 3 + k], recv_sem=recv_sems.at[a * 3 + k],
                device_id=(px, py, c), device_id_type=MESH)

        sends = [copy(a, k, me) for a in range(n) for k in range(3)]
        for cp in sends:
            cp.start()
        for a in range(n):
            for k in range(3):
                px, py = chips[k]
                copy(a, k, 2 * px + py).wait_recv()
        for cp in sends:
            cp.wait_send()
        for cp in local:
            cp.wait()

    outs = []
    for s in srcs:
        shape = s.shape if send_slot_is_dest else (N_CHIPS,) + s.shape
        outs.append(jax.ShapeDtypeStruct(shape, s.dtype))
    return pl.pallas_call(
        body, name=name, in_specs=[HBM] * n, out_specs=[HBM] * n, out_shape=outs,
        scratch_shapes=[pltpu.SemaphoreType.DMA((3 * n,)), pltpu.SemaphoreType.DMA((3 * n,)),
                        pltpu.SemaphoreType.DMA((n,))],
    )(*srcs)


def _sibling_exchange(srcs, *, name):
    n = len(srcs)

    def body(*refs):
        src_refs, out_refs = refs[:n], refs[n:2 * n]
        send_sems, recv_sems = refs[2 * n:]
        x, y, c = _mesh_pos()
        copies = [pltpu.make_async_remote_copy(
            src_ref=src_refs[a], dst_ref=out_refs[a], send_sem=send_sems.at[a], recv_sem=recv_sems.at[a],
            device_id=(x, y, 1 - c), device_id_type=MESH) for a in range(n)]
        for cp in copies:
            cp.start()
        for cp in copies:
            cp.wait()

    return pl.pallas_call(
        body, name=name, in_specs=[HBM] * n, out_specs=[HBM] * n,
        out_shape=[jax.ShapeDtypeStruct(s.shape, s.dtype) for s in srcs],
        scratch_shapes=[pltpu.SemaphoreType.DMA((n,)), pltpu.SemaphoreType.DMA((n,))],
    )(*srcs)


def _gather_halves(shards, small, *, name):
    n = len(shards)

    def body(*refs):
        s_refs, small_ref = refs[:n], refs[n]
        o_refs, osmall_ref = refs[n + 1:2 * n + 1], refs[2 * n + 1]
        send_sems, recv_sems, local_sems = refs[2 * n + 2:]
        x, y, c = _mesh_pos()
        me = 2 * x + y
        chips = _other_chips(x, y)
        local = [pltpu.make_async_copy(small_ref, osmall_ref.at[me], local_sems.at[0])]
        for cp in local:
            cp.start()

        def over_ici(a, k, slot):
            px, py = chips[k]
            return pltpu.make_async_remote_copy(
                src_ref=s_refs[a].at[c], dst_ref=o_refs[a].at[slot, c], send_sem=send_sems.at[3 * a + k],
                recv_sem=recv_sems.at[3 * a + k], device_id=(px, py, c), device_id_type=MESH)

        def small_copy(k, slot):
            px, py = chips[k]
            return pltpu.make_async_remote_copy(
                src_ref=small_ref, dst_ref=osmall_ref.at[slot], send_sem=send_sems.at[3 * n + k],
                recv_sem=recv_sems.at[3 * n + k], device_id=(px, py, c), device_id_type=MESH)

        def to_sibling(a, k, half):
            px, py = chips[k]
            blk = o_refs[a].at[2 * px + py, half]
            return pltpu.make_async_remote_copy(
                src_ref=blk, dst_ref=blk, send_sem=send_sems.at[3 * n + 3 + 3 * a + k],
                recv_sem=recv_sems.at[3 * n + 3 + 3 * a + k], device_id=(x, y, 1 - c), device_id_type=MESH)

        sends = [over_ici(a, k, me) for a in range(n) for k in range(3)] + [small_copy(k, me) for k in range(3)]
        for cp in sends:
            cp.start()
        passed = []
        for a in range(n):
            for k in range(3):
                px, py = chips[k]
                over_ici(a, k, 2 * px + py).wait_recv()
                passed.append(to_sibling(a, k, c))
                passed[-1].start()
        for k in range(3):
            px, py = chips[k]
            small_copy(k, 2 * px + py).wait_recv()
        for a in range(n):
            for k in range(3):
                to_sibling(a, k, 1 - c).wait_recv()
        for cp in sends + passed:
            cp.wait_send()
        for cp in local:
            cp.wait()

    nsem = 6 * n + 3
    return pl.pallas_call(
        body, name=name, in_specs=[HBM] * (n + 1), out_specs=[HBM] * (n + 1),
        out_shape=[jax.ShapeDtypeStruct((N_CHIPS,) + s.shape, s.dtype) for s in shards + [small]],
        scratch_shapes=[pltpu.SemaphoreType.DMA((nsem,)), pltpu.SemaphoreType.DMA((nsem,)),
                        pltpu.SemaphoreType.DMA((1,))],
    )(*shards, small)


def _forward_halves(landed, *, name):
    n = len(landed)

    def body(*refs):
        o_refs = refs[n:2 * n]
        send_sems, recv_sems = refs[2 * n:]
        x, y, c = _mesh_pos()
        chips = _other_chips(x, y)
        pairs = [(a, k) for a in range(n) for k in range(3)]

        def copy(a, k, half):
            px, py = chips[k]
            blk = o_refs[a].at[2 * px + py, half]
            return pltpu.make_async_remote_copy(
                src_ref=blk, dst_ref=blk, send_sem=send_sems.at[3 * a + k], recv_sem=recv_sems.at[3 * a + k],
                device_id=(x, y, 1 - c), device_id_type=MESH)

        sends = [copy(a, k, c) for a, k in pairs]
        for cp in sends:
            cp.start()
        for a, k in pairs:
            copy(a, k, 1 - c).wait_recv()
        for cp in sends:
            cp.wait_send()

    return pl.pallas_call(
        body, name=name, in_specs=[HBM] * n, out_specs=[HBM] * n,
        out_shape=[jax.ShapeDtypeStruct(a.shape, a.dtype) for a in landed],
        input_output_aliases={a: a for a in range(n)},
        scratch_shapes=[pltpu.SemaphoreType.DMA((3 * n,)), pltpu.SemaphoreType.DMA((3 * n,))],
    )(*landed)


def _swap_other_half(g_list, *, name):
    n = len(g_list)

    def body(*refs):
        g_refs, o_refs = refs[:n], refs[n:2 * n]
        send_sems, recv_sems = refs[2 * n:]
        x, y, c = _mesh_pos()
        copies = [pltpu.make_async_remote_copy(
            src_ref=g_refs[a].at[:, 1 - c], dst_ref=o_refs[a], send_sem=send_sems.at[a], recv_sem=recv_sems.at[a],
            device_id=(x, y, 1 - c), device_id_type=MESH) for a in range(n)]
        for cp in copies:
            cp.start()
        for cp in copies:
            cp.wait()

    return pl.pallas_call(
        body, name=name, in_specs=[HBM] * n, out_specs=[HBM] * n,
        out_shape=[jax.ShapeDtypeStruct((g.shape[0],) + g.shape[2:], g.dtype) for g in g_list],
        scratch_shapes=[pltpu.SemaphoreType.DMA((n,)), pltpu.SemaphoreType.DMA((n,))],
    )(*g_list)


def _row_tile(r):
    return _pick(r, (512, 256, 128, 64, 32, 16, 8))


def _add_my_half(g4, sib4, core, *, name):
    n, _, r, C = g4.shape
    tr = _row_tile(r)

    def body(core_ref, g_ref, s_ref, o_ref):
        del core_ref
        o_ref[...] = (g_ref[...].astype(F32) + s_ref[...].astype(F32)).astype(o_ref.dtype)

    return pl.pallas_call(
        body, name=name,
        grid_spec=pltpu.PrefetchScalarGridSpec(
            num_scalar_prefetch=1, grid=(n, r // tr),
            in_specs=[pl.BlockSpec((None, None, tr, C), lambda j, i, core_ref: (j, core_ref[0], i, 0)),
                      pl.BlockSpec((None, tr, C), lambda j, i, core_ref: (j, i, 0))],
            out_specs=pl.BlockSpec((None, tr, C), lambda j, i, core_ref: (j, i, 0))),
        out_shape=jax.ShapeDtypeStruct((n, r, C), g4.dtype),
        compiler_params=pltpu.CompilerParams(dimension_semantics=("parallel", "parallel")),
    )(core, g4, sib4)


def _scatter_to_chips(p_list, *, name):
    n = len(p_list)

    def body(*refs):
        p_refs, o_refs = refs[:n], refs[n:2 * n]
        send_sems, recv_sems = refs[2 * n:]
        x, y, c = _mesh_pos()
        me = 2 * x + y
        chips = _other_chips(x, y)
        pairs = [(a, k) for a in range(n) for k in range(3)]

        def copy(a, k, landing_slot):
            px, py = chips[k]
            return pltpu.make_async_remote_copy(
                src_ref=p_refs[a].at[2 * px + py], dst_ref=o_refs[a].at[landing_slot],
                send_sem=send_sems.at[3 * a + k], recv_sem=recv_sems.at[3 * a + k], device_id=(px, py, c),
                device_id_type=MESH)

        sends = [copy(a, k, me) for a, k in pairs]
        for cp in sends:
            cp.start()
        for a, k in pairs:
            px, py = chips[k]
            copy(a, k, 2 * px + py).wait_recv()
        for cp in sends:
            cp.wait_send()

    return pl.pallas_call(
        body, name=name, in_specs=[HBM] * n, out_specs=[HBM] * n,
        out_shape=[jax.ShapeDtypeStruct(p.shape, p.dtype) for p in p_list],
        scratch_shapes=[pltpu.SemaphoreType.DMA((3 * n,)), pltpu.SemaphoreType.DMA((3 * n,))],
    )(*p_list)


def _sum_chips(landed, part, me, *, name):
    _, r, C = landed.shape
    tr = _row_tile(r)

    def body(me_ref, own_ref, r1_ref, r2_ref, r3_ref, o_ref):
        del me_ref
        f = lambda ref: ref[...].astype(F32)
        o_ref[...] = ((f(own_ref) + f(r1_ref)) + f(r2_ref)) + f(r3_ref)

    slot = lambda d: pl.BlockSpec((None, tr, C), lambda i, me_ref: ((me_ref[0] + d) % N_CHIPS, i, 0))
    return pl.pallas_call(
        body, name=name,
        grid_spec=pltpu.PrefetchScalarGridSpec(
            num_scalar_prefetch=1, grid=(r // tr,), in_specs=[slot(0), slot(1), slot(2), slot(3)],
            out_specs=pl.BlockSpec((tr, C), lambda i, me_ref: (i, 0))),
        out_shape=jax.ShapeDtypeStruct((r, C), F32),
        compiler_params=pltpu.CompilerParams(dimension_semantics=("parallel",)),
    )(me, part, landed, landed, landed)


def _adamw_halves(w, mine, theirs, m, v, core, *, layer, prev, name):
    shape = w.shape
    r, C = mine.shape
    tr = _pick(r, (128, 64, 32, 16, 8))
    per = r // tr
    view = lambda a: a.reshape(-1, C)
    n_prev = 0 if prev is None else 4

    def body(*refs):
        core_ref, w_ref, gm_ref, gt_ref, m_ref, v_ref = refs[:6]
        g_ref, d_ref, nm_ref, nv_ref = refs[6 + n_prev:]
        gv = jnp.where(pl.program_id(0) == core_ref[0], gm_ref[...], gt_ref[...])
        g_ref[...] = gv
        d_ref[...], nm_ref[...], nv_ref[...] = _adamw_math(w_ref[...], gv, m_ref[...], v_ref[...])

    half = pl.BlockSpec((tr, C), lambda h, i, core_ref: ((2 * layer + h) * per + i, 0))
    row = pl.BlockSpec((tr, C), lambda h, i, core_ref: (i, 0))
    out = jax.ShapeDtypeStruct((math.prod(shape) // C, C), F32)
    res = pl.pallas_call(
        body, name=name,
        grid_spec=pltpu.PrefetchScalarGridSpec(
            num_scalar_prefetch=1, grid=(2, per), in_specs=[half, row, row, half, half] + [HBM] * n_prev,
            out_specs=[half] * 4),
        out_shape=[out] * 4,
        input_output_aliases={6 + j: j for j in range(n_prev)},
        compiler_params=pltpu.CompilerParams(dimension_semantics=("parallel", "parallel")),
    )(core, view(w), mine, theirs, view(m), view(v), *([] if prev is None else [view(a) for a in prev]))
    return tuple(a.reshape(shape) for a in res)


def _sum_small(recv4, *, name):
    _, R, C = recv4.shape

    def body(r_ref, o_ref):
        o_ref[...] = ((r_ref[0] + r_ref[1]) + r_ref[2]) + r_ref[3]

    return pl.pallas_call(body, name=name, out_shape=jax.ShapeDtypeStruct((R, C), F32))(recv4)


def _add(a, b, *, name):
    R, C = a.shape
    tr = _pick(R, (512, 256, 128, 64, 32, 16, 8))
    blk = pl.BlockSpec((tr, C), lambda i: (i, 0))

    def body(a_ref, b_ref, o_ref):
        o_ref[...] = a_ref[...] + b_ref[...]

    return pl.pallas_call(body, name=name, grid=(R // tr,), in_specs=[blk, blk], out_specs=blk,
                          out_shape=jax.ShapeDtypeStruct((R, C), F32),
                          compiler_params=pltpu.CompilerParams(dimension_semantics=("parallel",)))(a, b)


def _adamw_math(w, g, m, v):
    nm = ADAM_B1 * m + (1.0 - ADAM_B1) * g
    nv = ADAM_B2 * v + (1.0 - ADAM_B2) * (g * g)
    m_hat = nm / (1.0 - ADAM_B1 ** ADAM_STEP)
    v_hat = nv / (1.0 - ADAM_B2 ** ADAM_STEP)
    return -ADAM_LR * (m_hat / (jnp.sqrt(v_hat) + ADAM_EPS) + ADAM_WD * w), nm, nv


def _adamw(w, g, m, v, *, name):
    shape = w.shape
    C = shape[-1]
    R = w.size // C
    two = lambda a: a.reshape(R, C)
    tr = _pick(R, (256, 128, 64, 32, 16, 8)) if R % 8 == 0 and R > 8 else R
    blk = pl.BlockSpec((tr, C), lambda i: (i, 0))

    def body(w_ref, g_ref, m_ref, v_ref, d_ref, nm_ref, nv_ref):
        d_ref[...], nm_ref[...], nv_ref[...] = _adamw_math(w_ref[...], g_ref[...], m_ref[...], v_ref[...])

    out = jax.ShapeDtypeStruct((R, C), F32)
    d, nm, nv = pl.pallas_call(
        body, name=name, grid=(R // tr,), in_specs=[blk] * 4, out_specs=[blk] * 3, out_shape=[out] * 3,
        compiler_params=pltpu.CompilerParams(dimension_semantics=("parallel",)),
    )(two(w), two(g), two(m), two(v))
    return d.reshape(shape), nm.reshape(shape), nv.reshape(shape)


BIG = (("dn_w_in", (2, 1024, 1540), 2), ("dn_w_out", (2, 512, 1024), 1), ("sb_w_in", (1, 1024, 1024), 2),
       ("sb_w_out", (1, 256, 1024), 1), ("sc_w_in", (1, 1024, 2048), 2), ("sc_w_out", (1, 512, 1024), 1))
SMALL = (("dn_conv_w", (2, 4, 1024), 2), ("dn_o_norm_g", (2, 64), 1), ("sc_conv_w", (1, 3, 512), 2))
REPL = (("norm_g", (4, 1024)), ("dn_a_log", (2, 8)), ("dn_dt_bias", (2, 8)), ("sb_q_norm_g", (1, 64)),
        ("sb_k_norm_g", (1, 64)))


def _halves(shard):
    return shard.reshape(2, -1, shard.shape[-1])


def _pack(arrays, cols, lead=()):
    flat = jnp.concatenate([a.reshape(lead + (-1,)) for a in arrays], axis=-1)
    n = flat.shape[-1]
    rows = -(-n // cols)
    unit = 512 if rows > 512 else 8
    rows = -(-rows // unit) * unit
    flat = jnp.pad(flat, [(0, 0)] * len(lead) + [(0, rows * cols - n)])
    return flat.reshape(lead + (rows, cols))


def _unpack(buf, table, lead=()):
    flat = buf.reshape(lead + (-1,))
    out, off = {}, 0
    for entry in table:
        name, shape = entry[0], entry[1]
        n = math.prod(shape)
        out[name] = flat[..., off:off + n].reshape(lead + shape)
        off += n
    return out


def _join(shards, axis):
    return jnp.concatenate([shards[j] for j in range(N_CHIPS)], axis=axis)


def _split(full, axis):
    return jnp.stack(jnp.split(full, N_CHIPS, axis=axis), axis=0)


def kernel(x, norm_g, dn_w_in, dn_conv_w, dn_a_log, dn_dt_bias, dn_o_norm_g, dn_w_out, sb_w_in, sb_q_norm_g, sb_k_norm_g, sb_w_out, sc_w_in, sc_conv_w, sc_w_out, loss_target, m_norm_g, m_dn_w_in, m_dn_conv_w, m_dn_a_log, m_dn_dt_bias, m_dn_o_norm_g, m_dn_w_out, m_sb_w_in, m_sb_q_norm_g, m_sb_k_norm_g, m_sb_w_out, m_sc_w_in, m_sc_conv_w, m_sc_w_out, v_norm_g, v_dn_w_in, v_dn_conv_w, v_dn_a_log, v_dn_dt_bias, v_dn_o_norm_g, v_dn_w_out, v_sb_w_in, v_sb_q_norm_g, v_sb_k_norm_g, v_sb_w_out, v_sc_w_in, v_sc_conv_w, v_sc_w_out):
    weights = dict(norm_g=norm_g, dn_w_in=dn_w_in, dn_conv_w=dn_conv_w, dn_a_log=dn_a_log, dn_dt_bias=dn_dt_bias,
                   dn_o_norm_g=dn_o_norm_g, dn_w_out=dn_w_out, sb_w_in=sb_w_in, sb_q_norm_g=sb_q_norm_g,
                   sb_k_norm_g=sb_k_norm_g, sb_w_out=sb_w_out, sc_w_in=sc_w_in, sc_conv_w=sc_conv_w, sc_w_out=sc_w_out)
    m_in = dict(norm_g=m_norm_g, dn_w_in=m_dn_w_in, dn_conv_w=m_dn_conv_w, dn_a_log=m_dn_a_log,
                dn_dt_bias=m_dn_dt_bias, dn_o_norm_g=m_dn_o_norm_g, dn_w_out=m_dn_w_out, sb_w_in=m_sb_w_in,
                sb_q_norm_g=m_sb_q_norm_g, sb_k_norm_g=m_sb_k_norm_g, sb_w_out=m_sb_w_out, sc_w_in=m_sc_w_in,
                sc_conv_w=m_sc_conv_w, sc_w_out=m_sc_w_out)
    v_in = dict(norm_g=v_norm_g, dn_w_in=v_dn_w_in, dn_conv_w=v_dn_conv_w, dn_a_log=v_dn_a_log,
                dn_dt_bias=v_dn_dt_bias, dn_o_norm_g=v_dn_o_norm_g, dn_w_out=v_dn_w_out, sb_w_in=v_sb_w_in,
                sb_q_norm_g=v_sb_q_norm_g, sb_k_norm_g=v_sb_k_norm_g, sb_w_out=v_sb_w_out, sc_w_in=v_sc_w_in,
                sc_conv_w=v_sc_conv_w, sc_w_out=v_sc_w_out)
    order = list(weights)
    xi, yi, ci = _mesh_pos()

    small = _pack([weights[n] for n, _, _ in SMALL], LANES)
    later = [("dn_w_in", 1), ("dn_w_out", 1), ("sb_w_in", 0), ("sb_w_out", 0), ("sc_w_in", 0), ("sc_w_out", 0)]
    piece = lambda n, l: _halves(weights[n][l].astype(BF16)[None])
    own_first = [piece("dn_w_in", 0), piece("dn_w_out", 0)]
    own_later = [piece(n, l) for n, l in later]
    me = 2 * xi + yi
    whole = lambda g4, own: lax.dynamic_update_index_in_dim(g4, own, me, 0)
    flat = lambda g4: g4.reshape(N_CHIPS, -1, g4.shape[-1])
    rows_of = lambda w4: w4.reshape(-1, w4.shape[-1])
    dn_in = lambda w4: jnp.pad(_join(w4, 1), ((0, 0), (0, DN_IN_PAD - DN_IN)))
    w_in0, w_out0, small4 = _gather_halves(own_first, small, name="gather_first")
    full = {n: _join(a, ax) for (n, _, ax), a in zip(SMALL, _unpack(small4, SMALL, (N_CHIPS,)).values())}

    def dn_args(j, w_in4, w_out4):
        return (dn_in(flat(w_in4)), full["dn_conv_w"][j], dn_a_log[j], dn_dt_bias[j], full["dn_o_norm_g"][j],
                rows_of(w_out4))

    x0 = x[0]
    dn0 = dn_args(0, whole(w_in0, own_first[0]), whole(w_out0, own_first[1]))
    x1, s0, landed = _dn_layer_fwd(x0, norm_g[0], *dn0, "l0", send=own_later)
    landed = _forward_halves(landed, name="forward_halves")
    w_in3, w_out3, sb_in, sb_out, sc_in, sc_out = [whole(g4, own) for g4, own in zip(landed, own_later)]
    dn1 = dn_args(1, w_in3, w_out3)
    sb_args = (flat(sb_in), sb_q_norm_g[0], sb_k_norm_g[0], rows_of(sb_out))
    sc_args = (flat(sc_in), full["sc_conv_w"][0], rows_of(sc_out))
    x2, s1 = _sb_layer_fwd(x1, norm_g[1], *sb_args, "l1")
    x3, s2 = _sc_layer_fwd(x2, norm_g[2], *sc_args, "l2")
    x4, s3, _ = _dn_layer_fwd(x3, norm_g[3], *dn1, "l3")
    dy, loss_local = _loss_head(x4, loss_target[0], name="loss_head")
    loss = lax.psum(loss_local[0, 0], ("x", "y", "c"))

    by_cols = lambda dw: _split(dw[:, :DN_IN].astype(BF16), 1)
    by_rows = lambda dw: dw.reshape(N_CHIPS, -1, dw.shape[-1])
    cut2 = lambda g4: g4.reshape(N_CHIPS, 2, -1, g4.shape[-1])
    core = ci.astype(jnp.int32).reshape(1)
    chip = me.astype(jnp.int32).reshape(1)

    def chip_sums(g_list, tag):
        sib = _swap_other_half(g_list, name=f"swap_halves_{tag}")
        return [_add_my_half(g, s, core, name=f"sum_cores_{tag}{i}") for i, (g, s) in enumerate(zip(g_list, sib))]

    dx3, dng3, dwin3, dconv3, dal3, ddt3, dgain3, dwout3, _ = _dn_layer_bwd(dy, x3, norm_g[3], *dn1, s3, "l3")
    dx2, dng2, dwin2, dconv2, dwout2 = _sc_layer_bwd(dx3, x2, norm_g[2], *sc_args, s2, "l2")
    dx1, dng1, dwin1, dgq, dgk, dwout1 = _sb_layer_bwd(dx2, x1, norm_g[1], *sb_args, s1, "l1")
    part_later = chip_sums([cut2(by_cols(dwin3)), cut2(by_rows(dwout3)), cut2(dwin1), cut2(by_rows(dwout1)),
                            cut2(dwin2), cut2(by_rows(dwout2))], "later")
    dx0, dng0, dwin0, dconv0, dal0, ddt0, dgain0, dwout0, landed_later = _dn_layer_bwd(
        dx1, x0, norm_g[0], *dn0, s0, "l0", send=part_later)
    part_first = chip_sums([cut2(by_cols(dwin0)), cut2(by_rows(dwout0))], "first")
    landed_first = _scatter_to_chips(part_first, name="scatter_first")
    pieces = [("dn_w_in", 0), ("dn_w_out", 0)] + later
    mine = [_sum_chips(r, p, chip, name=f"sum_chips_{n}{l}")
            for (n, l), r, p in zip(pieces, list(landed_first) + list(landed_later), part_first + part_later)]
    theirs = _sibling_exchange(mine, name="swap_results")
    upd = {}
    for (n, l), a, b in zip(pieces, mine, theirs):
        upd[n] = _adamw_halves(weights[n], a, b, m_in[n], v_in[n], core, layer=l, prev=upd.get(n),
                               name=f"adamw_{n}{l}")
    g_out = {n: upd[n][0] for n, _, _ in BIG}

    grads = dict(
        norm_g=jnp.concatenate([dng0, dng1, dng2, dng3], axis=0), dn_conv_w=jnp.stack([dconv0, dconv3]),
        dn_a_log=jnp.stack([dal0, dal3]), dn_dt_bias=jnp.stack([ddt0, ddt3]),
        dn_o_norm_g=jnp.stack([dgain0, dgain3]), sb_q_norm_g=dgq[None], sb_k_norm_g=dgk[None],
        sc_conv_w=dconv2[None])
    repl = [jnp.broadcast_to(grads[n][None], (N_CHIPS,) + s) for n, s in REPL]
    gsmall = _pack([_split(grads[n], ax) for n, _, ax in SMALL] + repl, LANES, (N_CHIPS,))
    rsmall, = _chip_exchange([gsmall], send_slot_is_dest=True, copy_own=(True,), name="scatter_small")
    psmall = _sum_small(rsmall, name="sum_chips_small")
    qsmall, = _sibling_exchange([psmall], name="swap_cores_small")
    tsmall = _add(psmall, qsmall, name="sum_cores_small")
    g_out.update(_unpack(tsmall, SMALL + REPL))

    for n in order:
        if n not in upd:
            upd[n] = (g_out[n],) + _adamw(weights[n], g_out[n], m_in[n], v_in[n], name=f"adamw_{n}")
    return (loss, dx0[None], *[upd[n][0] for n in order], *[upd[n][1] for n in order],
            *[upd[n][2] for n in order], *[upd[n][3] for n in order])
```

```python
import math

import jax
import jax.numpy as jnp
from jax import lax
from jax.experimental import pallas as pl
from jax.experimental.pallas import tpu as pltpu

F32 = jnp.float32
BF16 = jnp.bfloat16
MESH = pl.DeviceIdType.MESH

RMS_EPS = 1e-6
L2_EPS = 1e-6
LANES = 128
VMEM_BIG = 60 * 1024 * 1024
MM_VMEM = 44 * 1024 * 1024

DN_HEADS, DN_DK, DN_DV, DN_CHUNK, DN_CONV = 8, 128, 256, 64, 4
DN_QK_W = DN_HEADS * DN_DK
DN_V_W = DN_HEADS * DN_DV
DN_CONV_W = 2 * DN_QK_W + DN_V_W
DN_IN = DN_CONV_W + DN_V_W + 2 * DN_HEADS
DN_IN_PAD = DN_CONV_W + DN_V_W + LANES
SB_DH = 64
SC_CONV = 3

ADAM_LR, ADAM_B1, ADAM_B2, ADAM_EPS, ADAM_WD, ADAM_STEP = 0.001, 0.9, 0.999, 1e-08, 0.01, 10


def _pick(n, cands):
    for c in cands:
        if n % c == 0:
            return c
    raise ValueError(f"no tile for {n} in {cands}")


def _bf(x):
    return x.astype(BF16)


def _dot(a, b):
    return jnp.dot(_bf(a), _bf(b), preferred_element_type=F32)


def _dot_nt(a, b):
    return lax.dot_general(_bf(a), _bf(b), (((1,), (1,)), ((), ())), preferred_element_type=F32)


def _dot_tn(a, b):
    return lax.dot_general(_bf(a), _bf(b), (((0,), (0,)), ((), ())), preferred_element_type=F32)


def _split3(a):
    hi = _bf(a)
    r = a - hi.astype(F32)
    mid = _bf(r)
    lo = _bf(r - mid.astype(F32))
    return hi, mid, lo


def _sigmoid(x):
    return 1.0 / (1.0 + jnp.exp(-x))


def _silu(x):
    return x * _sigmoid(x)


def _dsilu(x):
    s = _sigmoid(x)
    return s * (1.0 + x * (1.0 - s))


def _softplus(x):
    return jnp.maximum(x, 0.0) + jnp.log(1.0 + jnp.exp(-jnp.abs(x)))


def _shift_down(z, k):
    if k == 0:
        return z
    row = lax.broadcasted_iota(jnp.int32, z.shape, 0)
    return jnp.where(row >= k, pltpu.roll(z, k, 0), 0.0)


def _shift_up(z, k):
    if k == 0:
        return z
    n = z.shape[0]
    row = lax.broadcasted_iota(jnp.int32, z.shape, 0)
    return jnp.where(row < n - k, pltpu.roll(z, n - k, 0), 0.0)


def _matmul(a, b, *, mode, name, res=None, a_parts=1, b_parts=1, out_parts=1, out_dtype=F32):
    def dims2(x, parts):
        if parts == 1:
            return x.shape
        assert x.shape[0] == parts
        return (x.shape[1], x.shape[2] * parts)

    ash, bsh = dims2(a, a_parts), dims2(b, b_parts)
    if mode == "nn":
        (M, K), (K2, N) = ash, bsh
        dn = (((1,), (0,)), ((), ()))
    elif mode == "nt":
        (M, K), (N, K2) = ash, bsh
        dn = (((1,), (1,)), ((), ()))
    else:
        (K, M), (K2, N) = ash, bsh
        dn = (((0,), (0,)), ((), ()))
    assert K == K2, (ash, bsh, mode)
    tm_max = _pick(M, (512, 256, 128, 64, 32, 16, 8))
    n_unit = N // max(out_parts, b_parts if mode != "nt" else 1)
    k_unit = K // max(a_parts if mode != "tn" else 1, b_parts if mode == "nt" else 1)
    tm, tn, tk = min(
        ((m, n, k) for m in {tm_max, max(tm_max // 2, 8)}
         for n in (2048, 1792, 1024, 896, 768, 512, 384, 256, 128) if n_unit % n == 0
         for k in (k_unit, 2048, 1792, 1024, 896, 512, 256, 128) if k_unit % k == 0
         if 2 * (m * k * a.dtype.itemsize + k * n * b.dtype.itemsize + 2 * m * n * 4) + m * n * 4 <= MM_VMEM),
        key=lambda t: (-t[0] * t[1] * t[2], -t[2], -t[0]))
    nk = K // tk
    grid = (M // tm, N // tn, nk)

    def spec(parts, rows_are, cols_are, tr, tc, width):
        per = width // parts // tc
        if parts == 1:
            return pl.BlockSpec((tr, tc), lambda i, j, k: ((i, j, k)[rows_are], (i, j, k)[cols_are]))
        return pl.BlockSpec((None, tr, tc), lambda i, j, k: ((i, j, k)[cols_are] // per, (i, j, k)[rows_are],
                                                             (i, j, k)[cols_are] % per))

    if mode == "nn":
        a_spec = spec(a_parts, 0, 2, tm, tk, K)
        b_spec = spec(b_parts, 2, 1, tk, tn, N)
    elif mode == "nt":
        a_spec = spec(a_parts, 0, 2, tm, tk, K)
        b_spec = spec(b_parts, 1, 2, tn, tk, K)
    else:
        a_spec = spec(a_parts, 2, 0, tk, tm, M)
        b_spec = spec(b_parts, 2, 1, tk, tn, N)
    o_spec = spec(out_parts, 0, 1, tm, tn, N)
    in_specs = [a_spec, b_spec]
    operands = [a, b]
    if res is not None:
        in_specs.append(pl.BlockSpec((tm, tn), lambda i, j, k: (i, j)))
        operands.append(res)

    def finish(refs, r):
        if res is not None:
            r = refs[2][...] + r
        refs[-2 if nk > 1 else -1][...] = r.astype(out_dtype)

    def body(*refs):
        part = lax.dot_general(_bf(refs[0][...]), _bf(refs[1][...]), dn, preferred_element_type=F32)
        if nk == 1:
            finish(refs, part)
            return
        acc_ref = refs[-1]
        k = pl.program_id(2)

        @pl.when(k == 0)
        def _():
            acc_ref[...] = part

        @pl.when(jnp.logical_and(k > 0, k < nk - 1))
        def _():
            acc_ref[...] += part

        @pl.when(k == nk - 1)
        def _():
            finish(refs, acc_ref[...] + part)

    out_shape = (M, N) if out_parts == 1 else (out_parts, M, N // out_parts)
    return pl.pallas_call(
        body, name=name, grid=grid, in_specs=in_specs, out_specs=o_spec,
        out_shape=jax.ShapeDtypeStruct(out_shape, out_dtype),
        scratch_shapes=[pltpu.VMEM((tm, tn), F32)] if nk > 1 else [],
        compiler_params=pltpu.CompilerParams(dimension_semantics=("parallel", "parallel", "arbitrary"),
                                             vmem_limit_bytes=VMEM_BIG),
    )(*operands)


def _rmsnorm_fwd(x, g, *, name):
    T, D = x.shape
    tm = _pick(T, (512, 256, 128, 64, 32, 16))

    def body(x_ref, g_ref, h_ref):
        xv = x_ref[...]
        r = lax.rsqrt(jnp.mean(xv * xv, axis=-1, keepdims=True) + RMS_EPS)
        h_ref[...] = ((xv * r) * g_ref[...]).astype(BF16)

    return pl.pallas_call(
        body, name=name, grid=(T // tm,),
        in_specs=[pl.BlockSpec((tm, D), lambda i: (i, 0)), pl.BlockSpec((1, D), lambda i: (0, 0))],
        out_specs=pl.BlockSpec((tm, D), lambda i: (i, 0)),
        out_shape=jax.ShapeDtypeStruct((T, D), BF16),
    )(x, g.reshape(1, D))


def _rmsnorm_bwd(x, g, dh, dx_in, *, name):
    T, D = x.shape
    tm = _pick(T, (512, 256, 128, 64, 32, 16))

    def body(x_ref, g_ref, dh_ref, dxin_ref, dx_ref, dg_ref):
        @pl.when(pl.program_id(0) == 0)
        def _():
            dg_ref[...] = jnp.zeros_like(dg_ref)

        xv = x_ref[...]
        r = lax.rsqrt(jnp.mean(xv * xv, axis=-1, keepdims=True) + RMS_EPS)
        xh = xv * r
        dh_v = dh_ref[...]
        dxh = dh_v * g_ref[...]
        dx_ref[...] = dxin_ref[...] + r * (dxh - xh * jnp.mean(dxh * xh, axis=-1, keepdims=True))
        dg_ref[...] += jnp.sum(dh_v * xh, axis=0, keepdims=True)

    row = pl.BlockSpec((tm, D), lambda i: (i, 0))
    vec = pl.BlockSpec((1, D), lambda i: (0, 0))
    return pl.pallas_call(
        body, name=name, grid=(T // tm,),
        in_specs=[row, vec, row, row], out_specs=[row, vec],
        out_shape=[jax.ShapeDtypeStruct((T, D), F32), jax.ShapeDtypeStruct((1, D), F32)],
        compiler_params=pltpu.CompilerParams(dimension_semantics=("arbitrary",)),
    )(x, g.reshape(1, D), dh, dx_in)


def _loss_head(y, target, *, name):
    T, D = y.shape
    tm = _pick(T, (512, 256, 128, 64, 32, 16))

    def body(y_ref, t_ref, dy_ref, l_ref):
        @pl.when(pl.program_id(0) == 0)
        def _():
            l_ref[...] = jnp.zeros_like(l_ref)

        err = y_ref[...] - t_ref[...]
        dy_ref[...] = err * (1.0 / D)
        l_ref[...] += 0.5 * jnp.sum(jnp.mean(err * err, axis=-1, keepdims=True), axis=0, keepdims=True)

    row = pl.BlockSpec((tm, D), lambda i: (i, 0))
    return pl.pallas_call(
        body, name=name, grid=(T // tm,),
        in_specs=[row, row], out_specs=[row, pl.BlockSpec((1, 1), lambda i: (0, 0))],
        out_shape=[jax.ShapeDtypeStruct((T, D), F32), jax.ShapeDtypeStruct((1, 1), F32)],
        compiler_params=pltpu.CompilerParams(dimension_semantics=("arbitrary",)),
    )(y, target)


def _sc_mid_fwd(p3, conv_w, *, name):
    _, T, W = p3.shape
    K = conv_w.shape[0]
    cw = LANES

    def body(p_ref, w_ref, o_ref):
        z = p_ref[1] * p_ref[2]
        cv = sum(w_ref[i:i + 1, :] * _shift_down(z, K - 1 - i) for i in range(K))
        o_ref[...] = ((p_ref[0] * cv) * _silu(p_ref[3])).astype(BF16)

    return pl.pallas_call(
        body, name=name, grid=(W // cw,),
        in_specs=[pl.BlockSpec((4, T, cw), lambda j: (0, 0, j)), pl.BlockSpec((K, cw), lambda j: (0, j))],
        out_specs=pl.BlockSpec((T, cw), lambda j: (0, j)),
        out_shape=jax.ShapeDtypeStruct((T, W), BF16),
        compiler_params=pltpu.CompilerParams(dimension_semantics=("parallel",), vmem_limit_bytes=VMEM_BIG),
    )(p3, conv_w)


def _sc_mid_bwd(p3, conv_w, do, *, name):
    _, T, W = p3.shape
    K = conv_w.shape[0]
    cw = LANES

    def body(p_ref, w_ref, do_ref, dp_ref, dw_ref):
        b, c, u, gate = p_ref[0], p_ref[1], p_ref[2], p_ref[3]
        z = c * u
        zs = [_shift_down(z, K - 1 - i) for i in range(K)]
        cv = sum(w_ref[i:i + 1, :] * zs[i] for i in range(K))
        y = b * cv
        dov = do_ref[...]
        dy = dov * _silu(gate)
        dp_ref[3] = dov * y * _dsilu(gate)
        dp_ref[0] = dy * cv
        dcv = dy * b
        dz = sum(w_ref[i:i + 1, :] * _shift_up(dcv, K - 1 - i) for i in range(K))
        dp_ref[1] = dz * u
        dp_ref[2] = dz * c
        for i in range(K):
            dw_ref[i:i + 1, :] = jnp.sum(dcv * zs[i], axis=0, keepdims=True)

    return pl.pallas_call(
        body, name=name, grid=(W // cw,),
        in_specs=[pl.BlockSpec((4, T, cw), lambda j: (0, 0, j)), pl.BlockSpec((K, cw), lambda j: (0, j)),
                  pl.BlockSpec((T, cw), lambda j: (0, j))],
        out_specs=[pl.BlockSpec((4, T, cw), lambda j: (0, 0, j)), pl.BlockSpec((K, cw), lambda j: (0, j))],
        out_shape=[jax.ShapeDtypeStruct((4, T, W), F32), jax.ShapeDtypeStruct((K, W), F32)],
        compiler_params=pltpu.CompilerParams(dimension_semantics=("parallel",), vmem_limit_bytes=VMEM_BIG),
    )(p3, conv_w, do)


def _sc_layer_fwd(x, ng, w_in, conv_w, w_out, tag):
    h = _rmsnorm_fwd(x, ng, name=f"{tag}_norm")
    p3 = _matmul(h, w_in, mode="nn", b_parts=4, out_parts=4, name=f"{tag}_inproj")
    og = _sc_mid_fwd(p3, conv_w, name=f"{tag}_mid")
    x_new = _matmul(og, w_out, mode="nn", res=x, name=f"{tag}_outproj")
    return x_new, (h, p3, og)


def _sc_layer_bwd(dx, x, ng, w_in, conv_w, w_out, saved, tag):
    h, p3, og = saved
    d_wout = _matmul(og, dx, mode="tn", out_dtype=BF16, name=f"{tag}_dwout")
    dog = _matmul(dx, w_out, mode="nt", name=f"{tag}_dog")
    dp3, dconv = _sc_mid_bwd(p3, conv_w, dog, name=f"{tag}_midbwd")
    d_win = _matmul(h, dp3, mode="tn", b_parts=4, out_parts=4, out_dtype=BF16, name=f"{tag}_dwin")
    dh = _matmul(dp3, w_in, mode="nt", a_parts=4, b_parts=4, name=f"{tag}_dh")
    dx_prev, dng = _rmsnorm_bwd(x, ng, dh, dx, name=f"{tag}_normbwd")
    return dx_prev, dng, d_win, dconv, d_wout


SB_BQ = 256
SB_BK = 256
SB_ROWS = 512
SB_DEAD = -110.0


def _sb_half_mask():
    return lax.broadcasted_iota(jnp.int32, (1, LANES), 1) < SB_DH


def _sb_headnorm(x, g, lo):
    x2 = x * x
    s_lo = jnp.sum(jnp.where(lo, x2, 0.0), axis=-1, keepdims=True)
    s_hi = jnp.sum(jnp.where(lo, 0.0, x2), axis=-1, keepdims=True)
    r = lax.rsqrt(jnp.where(lo, s_lo, s_hi) * (1.0 / SB_DH) + RMS_EPS)
    xh = x * r
    return xh * g, xh, r


def _dot_x2_l(a_l, b_exact_bf16):
    his = [_bf(a) for a in a_l]
    mids = [_bf(a - h.astype(F32)) for a, h in zip(a_l, his)]
    f = lambda p: jnp.dot(p, b_exact_bf16, preferred_element_type=F32)
    return [x + y for x, y in zip([f(h) for h in his], [f(m) for m in mids])]


def _sb_stack(xb, lo):
    zero = jnp.zeros_like(xb)
    return jnp.concatenate([jnp.where(lo, xb, zero), jnp.where(lo, zero, xb)], axis=0)


def _sb_rel(bq, bk):
    row = lax.broadcasted_iota(jnp.int32, (2 * bq, bk), 0)
    col = lax.broadcasted_iota(jnp.int32, (2 * bq, bk), 1)
    return col - jnp.where(row >= bq, row - bq, row)


def _sb_tile(qm, kb, valid):
    z = lax.dot_general(qm, kb, (((1,), (1,)), ((), ())), preferred_element_type=F32)
    sp = _softplus(z)
    return z - sp, (-sp if valid is None else jnp.where(valid, -sp, 0.0))


def _sb_attn_fwd(p3, gq2, gk2, *, name):
    _, T, W = p3.shape
    bq, bk = min(SB_BQ, T), min(SB_BK, T)
    rows = min(SB_ROWS, T)
    scale = SB_DH ** -0.5

    def body(p_ref, gq_ref, gk_ref, og_ref, o_ref, qn_ref, kn_ref, v_ref):
        lo = _sb_half_mask()

        def prologue(i, c):
            r0 = pl.multiple_of(i * rows, rows)
            sl = pl.ds(r0, rows)
            qn_ref[sl, :] = (_sb_headnorm(p_ref[0, sl, :], gq_ref[...], lo)[0] * scale).astype(BF16)
            kn_ref[sl, :] = _sb_headnorm(p_ref[1, sl, :], gk_ref[...], lo)[0].astype(BF16)
            v_ref[sl, :] = p_ref[2, sl, :].astype(BF16)
            return c

        lax.fori_loop(0, T // rows, prologue, 0)

        rel = _sb_rel(bq, bk)
        tri = (lax.broadcasted_iota(jnp.int32, (bk, bk), 0)
               > lax.broadcasted_iota(jnp.int32, (bk, bk), 1)).astype(BF16)

        def qblock(qi, c):
            q0 = pl.multiple_of(qi * bq, bq)
            qm = _sb_stack(qn_ref[pl.ds(q0, bq), :], lo)
            nkb = (q0 + bq - 1) // bk + 1

            def tiles(k0s, carry, valid):
                o_acc, a_carry = carry
                sc = [_sb_tile(qm, kn_ref[pl.ds(k0, bk), :], valid) for k0 in k0s]
                later = _dot_x2_l([log1m for _, log1m in sc], tri)
                for (logsig, log1m), lat, k0 in zip(sc, later, k0s):
                    wts = jnp.exp(logsig + (lat + a_carry))
                    if valid is not None:
                        wts = jnp.where(valid, wts, 0.0)
                    o_acc = o_acc + jnp.dot(_bf(wts), v_ref[pl.ds(k0, bk), :], preferred_element_type=F32)
                    a_carry = a_carry + jnp.sum(log1m, axis=-1, keepdims=True)
                return o_acc, a_carry

            blk0 = lambda j: pl.multiple_of(j * bk, bk)
            k_last = blk0(nkb - 1)
            o2, t2 = tiles([k_last], (jnp.zeros((2 * bq, LANES), F32), jnp.zeros((2 * bq, 1), F32)), rel < q0 - k_last)

            def alive(st):
                return jnp.logical_and(st[0] < nkb - 1, jnp.max(st[2]) > SB_DEAD)

            def back_one(st):
                return (st[0] + 1,) + tiles([blk0(nkb - 2 - st[0])], st[1:], None)

            _, o2, _ = lax.while_loop(alive, back_one, (jnp.int32(0), o2, t2))
            o = jnp.where(lo, o2[:bq], o2[bq:])
            o_ref[pl.ds(q0, bq), :] = o
            og_ref[pl.ds(q0, bq), :] = (o * _silu(p_ref[3, pl.ds(q0, bq), :])).astype(BF16)
            return c

        lax.fori_loop(0, T // bq, qblock, 0)

    colblk = pl.BlockSpec((T, LANES), lambda j: (0, j))
    vec = pl.BlockSpec((1, LANES), lambda j: (0, 0))
    return pl.pallas_call(
        body, name=name, grid=(W // LANES,),
        in_specs=[pl.BlockSpec((4, T, LANES), lambda j: (0, 0, j)), vec, vec],
        out_specs=[colblk, colblk],
        out_shape=[jax.ShapeDtypeStruct((T, W), BF16), jax.ShapeDtypeStruct((T, W), F32)],
        scratch_shapes=[pltpu.VMEM((T, LANES), BF16)] * 3,
        compiler_params=pltpu.CompilerParams(dimension_semantics=("parallel",), vmem_limit_bytes=VMEM_BIG),
    )(p3, gq2, gk2)


def _sb_attn_bwd(p3, gq2, gk2, o, dog, *, name):
    _, T, W = p3.shape
    bq, bk = min(SB_BQ, T), min(SB_BK, T)
    rows = min(SB_ROWS, T)
    scale = SB_DH ** -0.5

    def body(p_ref, gq_ref, gk_ref, o_ref, dog_ref, dp_ref, dgq_ref, dgk_ref,
             qn_ref, kn_ref, v_ref, do_ref):
        lo = _sb_half_mask()

        def prologue(i, c):
            r0 = pl.multiple_of(i * rows, rows)
            sl = pl.ds(r0, rows)
            qn_ref[sl, :] = (_sb_headnorm(p_ref[0, sl, :], gq_ref[...], lo)[0] * scale).astype(BF16)
            kn_ref[sl, :] = _sb_headnorm(p_ref[1, sl, :], gk_ref[...], lo)[0].astype(BF16)
            v_ref[sl, :] = p_ref[2, sl, :].astype(BF16)
            gate = p_ref[3, sl, :]
            dogv = dog_ref[sl, :]
            dp_ref[3, sl, :] = dogv * o_ref[sl, :] * _dsilu(gate)
            do_ref[sl, :] = (dogv * _silu(gate)).astype(BF16)
            zero = jnp.zeros((rows, LANES), F32)
            dp_ref[0, sl, :] = zero
            dp_ref[1, sl, :] = zero
            dp_ref[2, sl, :] = zero
            return c

        lax.fori_loop(0, T // rows, prologue, 0)

        rel = _sb_rel(bq, bk)
        rj = lax.broadcasted_iota(jnp.int32, (bk, bk), 0)
        cj = lax.broadcasted_iota(jnp.int32, (bk, bk), 1)
        upto = (rj <= cj).astype(BF16)
        before_m = (rj < cj).astype(BF16)

        def qblock(qi, c):
            q0 = pl.multiple_of(qi * bq, bq)
            qm = _sb_stack(qn_ref[pl.ds(q0, bq), :], lo)
            dom = _sb_stack(do_ref[pl.ds(q0, bq), :], lo)
            nkb = (q0 + bq - 1) // bk + 1
            blk0 = lambda j: pl.multiple_of(j * bk, bk)
            k_last = blk0(nkb - 1)

            def row_sums(k0, valid):
                return jnp.sum(_sb_tile(qm, kn_ref[pl.ds(k0, bk), :], valid)[1], axis=-1, keepdims=True)

            def alive(st):
                return jnp.logical_and(st[0] < nkb, jnp.max(st[1]) > SB_DEAD)

            def back_one(st):
                return st[0] + 1, st[1] + row_sums(blk0(nkb - 1 - st[0]), None)

            n_live, total = lax.while_loop(alive, back_one, (jnp.int32(1), row_sums(k_last, rel < q0 - k_last)))
            k_first = nkb - n_live

            def tiles(k0s, carry, valid):
                dq_acc, a_pre, r_pre = carry
                kss = [pl.ds(k0, bk) for k0 in k0s]
                kbs = [kn_ref[ks, :] for ks in kss]
                sc = [_sb_tile(qm, kb, valid) for kb in kbs]
                dws = [lax.dot_general(dom, v_ref[ks, :], _NT, preferred_element_type=F32) for ks in kss]
                upto_l = _dot_x2_l([log1m for _, log1m in sc], upto)
                wts_l = []
                for (logsig, log1m), up in zip(sc, upto_l):
                    wts = jnp.exp(logsig + ((total - a_pre) - up))
                    wts_l.append(wts if valid is None else jnp.where(valid, wts, 0.0))
                    a_pre = a_pre + jnp.sum(log1m, axis=-1, keepdims=True)
                ee_l = [dw * wts for dw, wts in zip(dws, wts_l)]
                before_l = _dot_x2_l(ee_l, before_m)
                for (logsig, _), ks, kb, wts, ee, bef in zip(sc, kss, kbs, wts_l, ee_l, before_l):
                    beta = jnp.exp(logsig)
                    dz = ee * (1.0 - beta) - beta * (r_pre + bef)
                    if valid is not None:
                        dz = jnp.where(valid, dz, 0.0)
                    dzb = _bf(dz)
                    dq_acc = dq_acc + jnp.dot(dzb, kb, preferred_element_type=F32)
                    dp_ref[1, ks, :] += lax.dot_general(dzb, qm, _TN, preferred_element_type=F32)
                    dp_ref[2, ks, :] += lax.dot_general(_bf(wts), dom, _TN, preferred_element_type=F32)
                    r_pre = r_pre + jnp.sum(ee, axis=-1, keepdims=True)
                return dq_acc, a_pre, r_pre

            cr = (jnp.zeros((2 * bq, LANES), F32), jnp.zeros((2 * bq, 1), F32), jnp.zeros((2 * bq, 1), F32))
            cr = lax.fori_loop(0, (n_live - 1) // 2,
                               lambda t, cr: tiles([blk0(k_first + 2 * t), blk0(k_first + 2 * t + 1)], cr, None), cr)
            cr = lax.fori_loop(0, (n_live - 1) % 2, lambda t, cr: tiles([blk0(nkb - 2)], cr, None), cr)
            dq2, _, _ = tiles([k_last], cr, rel < q0 - k_last)
            dp_ref[0, pl.ds(q0, bq), :] = jnp.where(lo, dq2[:bq], dq2[bq:]) * scale
            return c

        lax.fori_loop(0, T // bq, qblock, 0)

        dgq_ref[...] = jnp.zeros_like(dgq_ref)
        dgk_ref[...] = jnp.zeros_like(dgk_ref)

        def epilogue(i, c):
            r0 = pl.multiple_of(i * rows, rows)
            sl = pl.ds(r0, rows)
            for part, g_ref, dg_ref in ((0, gq_ref, dgq_ref), (1, gk_ref, dgk_ref)):
                _, xh, r = _sb_headnorm(p_ref[part, sl, :], g_ref[...], lo)
                dn = dp_ref[part, sl, :]
                dxh = dn * g_ref[...]
                prod = dxh * xh
                m_lo = jnp.sum(jnp.where(lo, prod, 0.0), axis=-1, keepdims=True)
                m_hi = jnp.sum(jnp.where(lo, 0.0, prod), axis=-1, keepdims=True)
                m = jnp.where(lo, m_lo, m_hi) * (1.0 / SB_DH)
                dp_ref[part, sl, :] = r * (dxh - xh * m)
                dg_ref[...] += jnp.sum(dn * xh, axis=0, keepdims=True)
            return c

        lax.fori_loop(0, T // rows, epilogue, 0)

    colblk = pl.BlockSpec((T, LANES), lambda j: (0, j))
    vec = pl.BlockSpec((1, LANES), lambda j: (0, 0))
    part = pl.BlockSpec((4, T, LANES), lambda j: (0, 0, j))
    gvec = pl.BlockSpec((None, 1, LANES), lambda j: (j, 0, 0))
    npair = W // LANES
    return pl.pallas_call(
        body, name=name, grid=(npair,),
        in_specs=[part, vec, vec, colblk, colblk],
        out_specs=[part, gvec, gvec],
        out_shape=[jax.ShapeDtypeStruct((4, T, W), F32), jax.ShapeDtypeStruct((npair, 1, LANES), F32),
                   jax.ShapeDtypeStruct((npair, 1, LANES), F32)],
        scratch_shapes=[pltpu.VMEM((T, LANES), BF16)] * 4,
        compiler_params=pltpu.CompilerParams(dimension_semantics=("parallel",), vmem_limit_bytes=VMEM_BIG),
    )(p3, gq2, gk2, o, dog)


_NN = (((1,), (0,)), ((), ()))
_NT = (((1,), (1,)), ((), ()))
_TN = (((0,), (0,)), ((), ()))
DN_TB = 512
DN_HEADS_PER_STEP = 2
DN_INV_EXACT_LEVELS = 2
DN_AB_COL = (DN_CONV_W + DN_V_W) // LANES


def _dn_conv(x, w_ref):
    k = w_ref.shape[0]
    return sum(w_ref[i:i + 1, :] * _shift_down(x, k - 1 - i) for i in range(k))


def _dn_prep_fwd(p, conv_w, *, name):
    T = p.shape[0]
    cw = conv_w.shape[1]
    n_qk = 2 * DN_QK_W // LANES

    def body(p_ref, w_ref, o_ref):
        s = _silu(_dn_conv(p_ref[...], w_ref))
        r = lax.rsqrt(jnp.sum(s * s, axis=-1, keepdims=True) + L2_EPS)
        o_ref[...] = jnp.where(pl.program_id(0) < n_qk, s * r, s)

    colblk = pl.BlockSpec((T, LANES), lambda j: (0, j))
    return pl.pallas_call(
        body, name=name, grid=(cw // LANES,),
        in_specs=[colblk, pl.BlockSpec((DN_CONV, LANES), lambda j: (0, j))],
        out_specs=colblk, out_shape=jax.ShapeDtypeStruct((T, cw), F32),
        compiler_params=pltpu.CompilerParams(dimension_semantics=("parallel",), vmem_limit_bytes=VMEM_BIG),
    )(p, conv_w)


def _dn_chunk_tri(rows, upper):
    r = lax.broadcasted_iota(jnp.int32, (rows, rows), 0)
    c = lax.broadcasted_iota(jnp.int32, (rows, rows), 1)
    same = (r // DN_CHUNK) == (c // DN_CHUNK)
    return jnp.logical_and(same, (c >= r) if upper else (c <= r)).astype(BF16)


def _dn_lane_rows(a_log, dt_bias):
    pad = lambda v: jnp.zeros((1, LANES), F32).at[0, :DN_HEADS].set(v)
    return pad(a_log), pad(dt_bias)


def _dn_ab_parts(blk, alog_row, dtb_row):
    lane = lax.broadcasted_iota(jnp.int32, (1, LANES), 1)
    is_a = lane < DN_HEADS
    is_b = jnp.logical_and(lane >= DN_HEADS, lane < 2 * DN_HEADS)
    a_arg = jnp.where(is_a, blk + dtb_row, 0.0)
    neg_exp = jnp.where(is_a, -jnp.exp(alog_row), 0.0)
    log_a = neg_exp * _softplus(a_arg)
    beta = jnp.where(is_b, _sigmoid(blk), 0.0)
    return is_a, is_b, a_arg, neg_exp, log_a, beta


def _dn_ab_fwd(p, alog_row, dtb_row, *, name):
    T = p.shape[0]
    rows = min(DN_TB, T)

    def body(p_ref, al_ref, dt_ref, o_ref):
        _, _, _, _, log_a, beta = _dn_ab_parts(p_ref[...], al_ref[...], dt_ref[...])
        hi, mid, lo_ = _split3(log_a)
        tri = _dn_chunk_tri(rows, upper=False)
        f = lambda q: jnp.dot(tri, q, preferred_element_type=F32)
        o_ref[...] = (f(hi) + f(mid) + f(lo_)) + beta

    blk = pl.BlockSpec((rows, LANES), lambda i: (i, DN_AB_COL))
    vec = pl.BlockSpec((1, LANES), lambda i: (0, 0))
    return pl.pallas_call(
        body, name=name, grid=(T // rows,), in_specs=[blk, vec, vec],
        out_specs=pl.BlockSpec((rows, LANES), lambda i: (i, 0)),
        out_shape=jax.ShapeDtypeStruct((T, LANES), F32),
        compiler_params=pltpu.CompilerParams(dimension_semantics=("parallel",)),
    )(p, alog_row, dtb_row)


def _hp_l(a_l, b_l, dims=_NN):
    sa = [_split3(a)[:2] for a in a_l]
    sb = [_split3(b)[:2] for b in b_l]
    f = lambda p, q: lax.dot_general(p, q, dims, preferred_element_type=F32)
    hh = [f(x[0], y[0]) for x, y in zip(sa, sb)]
    hm = [f(x[0], y[1]) for x, y in zip(sa, sb)]
    mh = [f(x[1], y[0]) for x, y in zip(sa, sb)]
    return [a + (b + c) for a, b, c in zip(hh, hm, mh)]


def _dn_local(qs, k, v, g, beta, nc):
    c = DN_CHUNK
    cut = lambda x: [x[i * c:(i + 1) * c] for i in range(nc)]
    row = lax.broadcasted_iota(jnp.int32, (c, c), 0)
    col = lax.broadcasted_iota(jnp.int32, (c, c), 1)
    eye, lower, strict = row == col, row >= col, row > col
    rowid = lax.broadcasted_iota(jnp.int32, (c, 1), 0)
    eg = jnp.exp(g)
    kb = k * beta
    rhs_k = kb * eg
    g_l, k_l, kb_l, qs_l = cut(g), cut(k), cut(kb), cut(qs)
    g_row_l = [jnp.sum(jnp.where(eye, x, 0.0), axis=0, keepdims=True) for x in g_l]
    dec_l = [jnp.where(lower, jnp.exp(jnp.where(lower, x - y, 0.0)), 0.0) for x, y in zip(g_l, g_row_l)]
    kk_l = [_dot_nt(a, b) for a, b in zip(kb_l, k_l)]
    qk_l = [_dot_nt(a, b) for a, b in zip(qs_l, k_l)]
    low_l = [jnp.where(strict, a * d, 0.0) for a, d in zip(kk_l, dec_l)]
    eye_f = eye.astype(F32)
    pw_l = [-x for x in low_l]
    inv_l = [eye_f + x for x in pw_l]
    plain = lambda a_l, b_l: [_dot(a, b) for a, b in zip(a_l, b_l)]
    for level in range(int(math.log2(c)) - 1):
        mul = _hp_l if level < DN_INV_EXACT_LEVELS else plain
        pw_l = mul(pw_l, pw_l)
        inv_l = [a + b for a, b in zip(inv_l, mul(inv_l, pw_l))]
    u_l = [_dot(a, b) for a, b in zip(inv_l, cut(v * beta))]
    w_l = [_dot(a, b) for a, b in zip(inv_l, cut(rhs_k))]
    aqk_l = [jnp.where(lower, a * d, 0.0) for a, d in zip(qk_l, dec_l)]
    g_last_l = [jnp.sum(jnp.where(rowid == c - 1, x, 0.0), axis=0, keepdims=True) for x in g_l]
    ekd_l = [jnp.exp(a - b) for a, b in zip(g_last_l, g_l)]
    kd_l = [a * b for a, b in zip(k_l, ekd_l)]
    qd_l = cut(qs * eg)
    kw_l = [_dot_tn(a, b) for a, b in zip(kd_l, w_l)]
    qp_l = [q - _dot(a, w) for q, a, w in zip(qd_l, aqk_l, w_l)]
    return dict(eye=eye, lower=lower, strict=strict, dec=dec_l, k=k_l, kb=kb_l, qs=qs_l, low=low_l, inv=inv_l,
                eg=cut(eg), rhs_k=cut(rhs_k), u=u_l, w=w_l, aqk=aqk_l, g_last=g_last_l, qd=qd_l,
                ekd=ekd_l, kd=kd_l, kw=kw_l, qp=qp_l)


def _dn_head_cols(gb_blk, head):
    lane = lax.broadcasted_iota(jnp.int32, (1, LANES), 1)
    g = jnp.sum(jnp.where(lane == head, gb_blk, 0.0), axis=-1, keepdims=True)
    beta = jnp.sum(jnp.where(lane == head + DN_HEADS, gb_blk, 0.0), axis=-1, keepdims=True)
    return g, beta


def _halves_over_ici(s_refs, o_refs, send_sems, recv_sems, first, last):
    x, y, c = _mesh_pos()
    me = 2 * x + y
    chips = _other_chips(x, y)
    pairs = [(a, k) for a in range(len(s_refs)) for k in range(3)]

    def copy(a, k, slot):
        px, py = chips[k]
        return pltpu.make_async_remote_copy(
            src_ref=s_refs[a].at[c], dst_ref=o_refs[a].at[slot, c], send_sem=send_sems.at[3 * a + k],
            recv_sem=recv_sems.at[3 * a + k], device_id=(px, py, c), device_id_type=MESH)

    @pl.when(first)
    def _():
        for a, k in pairs:
            copy(a, k, me).start()

    @pl.when(last)
    def _():
        for a, k in pairs:
            px, py = chips[k]
            copy(a, k, 2 * px + py).wait_recv()
        for a, k in pairs:
            copy(a, k, me).wait_send()


def _dn_delta_fwd(qkv, gb, p, o_gain, *, name, send=()):
    T = qkv.shape[0]
    tb = min(DN_TB, T)
    nb, nc = T // tb, tb // DN_CHUNK
    H = DN_HEADS
    qscale = DN_DK ** -0.5
    ns = len(send)
    hp = DN_HEADS_PER_STEP

    def body(*refs):
        q_ref, k_ref, v_ref, gb_ref, gate_ref, gain_ref = refs[:6]
        o_ref, og_ref, st_ref = refs[6 + ns:9 + ns]
        s_ref = refs[9 + 2 * ns]
        pair, blk = pl.program_id(0), pl.program_id(1)
        if ns:
            _halves_over_ici(refs[6:6 + ns], refs[9 + ns:9 + 2 * ns], refs[10 + 2 * ns], refs[11 + 2 * ns],
                             jnp.logical_and(pair == 0, blk == 0),
                             jnp.logical_and(pair == H // hp - 1, blk == nb - 1))

        @pl.when(blk == 0)
        def _():
            s_ref[...] = jnp.zeros_like(s_ref)

        gbv = gb_ref[...]
        ts, ku, op = [], [], []
        for e in range(hp):
            qk_e, v_e = slice(e * DN_DK, (e + 1) * DN_DK), slice(e * DN_DV, (e + 1) * DN_DV)
            g, beta = _dn_head_cols(gbv, hp * pair + e)
            t = _dn_local(q_ref[:, qk_e] * qscale, k_ref[:, qk_e], v_ref[:, v_e], g, beta, nc)
            ts.append(t)
            ku.append([_dot_tn(a, b) for a, b in zip(t["kd"], t["u"])])
            op.append([_dot(a, b) for a, b in zip(t["aqk"], t["u"])])
        s32 = [s_ref[e] for e in range(hp)]
        s_l = [[] for _ in range(hp)]
        for i in range(nc):
            sb = [_bf(x) for x in s32]
            for e in range(hp):
                st_ref[e, i] = sb[e]
                s_l[e].append(sb[e])
            prod = [_dot(ts[e]["kw"][i], sb[e]) for e in range(hp)]
            s32 = [s32[e] * jnp.exp(ts[e]["g_last"][i]) - prod[e] + ku[e][i] for e in range(hp)]
        for e in range(hp):
            s_ref[e] = s32[e]
        o = jnp.concatenate(
            [jnp.concatenate([_dot(qp, sb) + x for qp, sb, x in zip(ts[e]["qp"], s_l[e], op[e])], axis=0)
             for e in range(hp)], axis=1)
        o_ref[...] = o
        gain = gain_ref[...]
        for e in range(hp):
            v_e = slice(e * DN_DV, (e + 1) * DN_DV)
            oe = o[:, v_e]
            r = lax.rsqrt(jnp.mean(oe * oe, axis=-1, keepdims=True) + RMS_EPS)
            og_ref[:, v_e] = (((oe * r) * gain) * _silu(gate_ref[:, v_e])).astype(BF16)

    qk = lambda col0: pl.BlockSpec((tb, hp * DN_DK), lambda h, i: (i, col0 // (hp * DN_DK) + h))
    vblk = lambda col0: pl.BlockSpec((tb, hp * DN_DV), lambda h, i: (i, col0 // (hp * DN_DV) + h))
    return pl.pallas_call(
        body, name=name, grid=(H // hp, nb),
        in_specs=[qk(0), qk(DN_QK_W), vblk(2 * DN_QK_W), pl.BlockSpec((tb, LANES), lambda h, i: (i, 0)),
                  vblk(DN_CONV_W), pl.BlockSpec((1, DN_DV), lambda h, i: (0, 0))] + [HBM] * ns,
        out_specs=[vblk(0), vblk(0), pl.BlockSpec((hp, nc, DN_DK, DN_DV), lambda h, i: (h, i, 0, 0))] + [HBM] * ns,
        out_shape=[jax.ShapeDtypeStruct((T, DN_V_W), F32), jax.ShapeDtypeStruct((T, DN_V_W), BF16),
                   jax.ShapeDtypeStruct((H, T // DN_CHUNK, DN_DK, DN_DV), BF16)]
        + [jax.ShapeDtypeStruct((N_CHIPS,) + a.shape, a.dtype) for a in send],
        scratch_shapes=[pltpu.VMEM((hp, DN_DK, DN_DV), F32)]
        + ([pltpu.SemaphoreType.DMA((3 * ns,)), pltpu.SemaphoreType.DMA((3 * ns,))] if ns else []),
        compiler_params=pltpu.CompilerParams(dimension_semantics=("arbitrary", "arbitrary")),
    )(qkv, qkv, qkv, gb, p, o_gain, *send)


def _blocks_over_ici(p_refs, o_refs, send_sems, recv_sems, first, last):
    x, y, c = _mesh_pos()
    me = 2 * x + y
    chips = _other_chips(x, y)
    pairs = [(a, k) for a in range(len(p_refs)) for k in range(3)]

    def copy(a, k, slot):
        px, py = chips[k]
        return pltpu.make_async_remote_copy(
            src_ref=p_refs[a].at[2 * px + py], dst_ref=o_refs[a].at[slot], send_sem=send_sems.at[3 * a + k],
            recv_sem=recv_sems.at[3 * a + k], device_id=(px, py, c), device_id_type=MESH)

    @pl.when(first)
    def _():
        for a, k in pairs:
            copy(a, k, me).start()

    @pl.when(last)
    def _():
        for a, k in pairs:
            px, py = chips[k]
            copy(a, k, 2 * px + py).wait_recv()
        for a, k in pairs:
            copy(a, k, me).wait_send()


def _dn_delta_bwd(qkv, gb, p, o_gain, o, states, dog, *, name, send=()):
    T = qkv.shape[0]
    tb = min(DN_TB, T)
    nb, nc = T // tb, tb // DN_CHUNK
    H = DN_HEADS
    qscale = DN_DK ** -0.5
    ns = len(send)
    hp = DN_HEADS_PER_STEP

    def body(*refs):
        q_ref, k_ref, v_ref, gb_ref, gate_ref, gain_ref, o_ref, st_ref, dog_ref = refs[:9]
        dq_ref, dk_ref, dv_ref, dgate_ref, dgb_ref, dgain_ref = refs[9 + ns:15 + ns]
        ds_ref = refs[15 + 2 * ns]
        pair, blk = pl.program_id(0), pl.program_id(1)
        first = jnp.logical_and(pair == 0, blk == 0)
        if ns:
            _blocks_over_ici(refs[9:9 + ns], refs[15 + ns:15 + 2 * ns], refs[16 + 2 * ns], refs[17 + 2 * ns],
                             first, jnp.logical_and(pair == H // hp - 1, blk == nb - 1))

        @pl.when(blk == 0)
        def _():
            ds_ref[...] = jnp.zeros_like(ds_ref)

        @pl.when(first)
        def _():
            dgain_ref[...] = jnp.zeros_like(dgain_ref)

        lane = lax.broadcasted_iota(jnp.int32, (1, LANES), 1)
        c = DN_CHUNK
        cut = lambda x: [x[i * c:(i + 1) * c] for i in range(nc)]
        cat = lambda xs: jnp.concatenate(xs, axis=0)
        rsum = lambda x: jnp.sum(x, axis=-1, keepdims=True)
        gbv, gain = gb_ref[...], gain_ref[...]

        def before_chain(e):
            qk_e, v_e = slice(e * DN_DK, (e + 1) * DN_DK), slice(e * DN_DV, (e + 1) * DN_DV)
            g, beta = _dn_head_cols(gbv, hp * pair + e)
            ov, gate, dogv = o_ref[:, v_e], gate_ref[:, v_e], dog_ref[:, v_e]
            r = lax.rsqrt(jnp.mean(ov * ov, axis=-1, keepdims=True) + RMS_EPS)
            oh = ov * r
            dnrm = dogv * _silu(gate)
            dgate_ref[:, v_e] = dogv * (oh * gain) * _dsilu(gate)
            doh = dnrm * gain
            do_l = cut(r * (doh - oh * jnp.mean(doh * oh, axis=-1, keepdims=True)))
            dgain_ref[...] += jnp.sum(dnrm * oh, axis=0, keepdims=True)
            k, v = k_ref[:, qk_e], v_ref[:, v_e]
            t = _dn_local(q_ref[:, qk_e] * qscale, k, v, g, beta, nc)
            s_l = [st_ref[e, i] for i in range(nc)]
            vn_l = [u - _dot(w, sb) for u, w, sb in zip(t["u"], t["w"], s_l)]
            return dict(
                t=t, beta=beta, v=v, s=s_l, vn=vn_l, egl=[jnp.exp(x) for x in t["g_last"]],
                dqd=[_dot_nt(a, sb) for a, sb in zip(do_l, s_l)], daqk=[_dot_nt(a, b) for a, b in zip(do_l, vn_l)],
                aqk_do=[_dot_tn(a, b) for a, b in zip(t["aqk"], do_l)],
                qp_do=[_dot_tn(a, b) for a, b in zip(t["qp"], do_l)])

        hs = [before_chain(e) for e in range(hp)]
        ds = [ds_ref[e] for e in range(hp)]
        ds_l = [[None] * nc for _ in range(hp)]
        for i in reversed(range(nc)):
            for e in range(hp):
                ds_l[e][i] = ds[e]
            prod = [_dot_tn(hs[e]["t"]["kw"][i], ds[e]) for e in range(hp)]
            ds = [ds[e] * hs[e]["egl"][i] - prod[e] + hs[e]["qp_do"][i] for e in range(hp)]
        for e in range(hp):
            ds_ref[e] = ds[e]

        def after_chain(e):
            hd, t = hs[e], hs[e]["t"]
            lower, strict, eye = t["lower"], t["strict"], t["eye"]
            s_l, vn_l, dqd_l, daqk_l, egl_l, beta, v = (hd["s"], hd["vn"], hd["dqd"], hd["daqk"], hd["egl"],
                                                         hd["beta"], hd["v"])
            dvn_l = [a + _dot(kd, d) for a, kd, d in zip(hd["aqk_do"], t["kd"], ds_l[e])]
            dkd_l = [_dot_nt(a, d) for a, d in zip(vn_l, ds_l[e])]
            dgl_l = [jnp.sum(rsum(d * sb.astype(F32)), axis=0, keepdims=True) * x
                     for d, sb, x in zip(ds_l[e], s_l, egl_l)]
            dw_l = [-_dot_nt(a, sb) for a, sb in zip(dvn_l, s_l)]
            dbv_l = [_dot_tn(a, b) for a, b in zip(t["inv"], dvn_l)]
            dbk_l = [_dot_tn(a, b) for a, b in zip(t["inv"], dw_l)]
            dlow_l = [-(_dot_nt(a, b) + _dot_nt(x, y)) for a, b, x, y in zip(dbv_l, t["u"], dbk_l, t["w"])]
            m_l = [jnp.where(strict, a * d, 0.0) for a, d in zip(dlow_l, t["dec"])]
            nmat_l = [jnp.where(lower, a * d, 0.0) for a, d in zip(daqk_l, t["dec"])]
            dkb_l = [_dot(m, kk) + b * x for m, kk, b, x in zip(m_l, t["k"], dbk_l, t["eg"])]
            dqs_l = [_dot(n, kk) + a * x for n, kk, a, x in zip(nmat_l, t["k"], dqd_l, t["eg"])]
            dk1_l = [_dot_tn(m, kb) for m, kb in zip(m_l, t["kb"])]
            dk2_l = [_dot_tn(n, q) for n, q in zip(nmat_l, t["qs"])]
            beta_l, v_l = cut(beta), cut(v)
            rowid = lax.broadcasted_iota(jnp.int32, (c, 1), 0)
            dk_l, dg_l, dbeta_l = [], [], []
            for i in range(nc):
                dk_l.append(dk1_l[i] + dk2_l[i] + dkd_l[i] * t["ekd"][i] + dkb_l[i] * beta_l[i])
                gmat = jnp.where(strict, dlow_l[i] * t["low"][i], 0.0) + daqk_l[i] * t["aqk"][i]
                s_kd = rsum(dkd_l[i] * t["kd"][i])
                dg = (rsum(gmat) + rsum(dqd_l[i] * t["qd"][i]) - s_kd + rsum(dbk_l[i] * t["rhs_k"][i]))
                dg_row = -jnp.sum(gmat, axis=0, keepdims=True)
                dg = dg + rsum(jnp.where(eye, dg_row, 0.0))
                dgl = dgl_l[i] + jnp.sum(s_kd, axis=0, keepdims=True)
                dg_l.append(dg + jnp.where(rowid == c - 1, dgl, 0.0))
                dbeta_l.append(rsum(dbv_l[i] * v_l[i]) + rsum(dkb_l[i] * t["k"][i]))
            head = hp * pair + e
            dgb = (jnp.where(lane == head, cat(dg_l), 0.0) + jnp.where(lane == head + DN_HEADS, cat(dbeta_l), 0.0))
            return cat(dqs_l) * qscale, cat(dk_l), cat(dbv_l) * beta, dgb

        for e in range(hp):
            dq, dk, dv, dgb = after_chain(e)
            dq_ref[:, e * DN_DK:(e + 1) * DN_DK] = dq
            dk_ref[:, e * DN_DK:(e + 1) * DN_DK] = dk
            dv_ref[:, e * DN_DV:(e + 1) * DN_DV] = dv
            dgb_ref[e] = dgb

    rev = lambda i: nb - 1 - i
    qk = lambda col0: pl.BlockSpec((tb, hp * DN_DK), lambda h, i: (rev(i), col0 // (hp * DN_DK) + h))
    vblk = lambda col0: pl.BlockSpec((tb, hp * DN_DV), lambda h, i: (rev(i), col0 // (hp * DN_DV) + h))
    gain_spec = pl.BlockSpec((1, DN_DV), lambda h, i: (0, 0))
    return pl.pallas_call(
        body, name=name, grid=(H // hp, nb),
        in_specs=[qk(0), qk(DN_QK_W), vblk(2 * DN_QK_W), pl.BlockSpec((tb, LANES), lambda h, i: (rev(i), 0)),
                  vblk(DN_CONV_W), gain_spec, vblk(0),
                  pl.BlockSpec((hp, nc, DN_DK, DN_DV), lambda h, i: (h, rev(i), 0, 0)), vblk(0)] + [HBM] * ns,
        out_specs=[qk(0), qk(0), vblk(0), vblk(DN_CONV_W),
                   pl.BlockSpec((hp, tb, LANES), lambda h, i: (h, rev(i), 0)), gain_spec] + [HBM] * ns,
        out_shape=[jax.ShapeDtypeStruct((T, DN_QK_W), F32), jax.ShapeDtypeStruct((T, DN_QK_W), F32),
                   jax.ShapeDtypeStruct((T, DN_V_W), F32), jax.ShapeDtypeStruct((T, DN_IN_PAD), F32),
                   jax.ShapeDtypeStruct((H, T, LANES), F32), jax.ShapeDtypeStruct((1, DN_DV), F32)]
        + [jax.ShapeDtypeStruct(a.shape, a.dtype) for a in send],
        scratch_shapes=[pltpu.VMEM((hp, DN_DK, DN_DV), F32)]
        + ([pltpu.SemaphoreType.DMA((3 * ns,)), pltpu.SemaphoreType.DMA((3 * ns,))] if ns else []),
        compiler_params=pltpu.CompilerParams(dimension_semantics=("arbitrary", "arbitrary")),
    )(qkv, qkv, qkv, gb, p, o_gain, o, states, dog, *send)


def _dn_conv_bwd(p, conv_w, d, dp, *, first, normed, name):
    T, width = d.shape

    def body(p_ref, w_ref, d_ref, dp_in, dp_ref, dw_ref):
        del dp_in
        x = p_ref[...]
        ksz = w_ref.shape[0]
        xs = [_shift_down(x, ksz - 1 - i) for i in range(ksz)]
        xc = sum(w_ref[i:i + 1, :] * xs[i] for i in range(ksz))
        ds = d_ref[...]
        if normed:
            s = _silu(xc)
            r = lax.rsqrt(jnp.sum(s * s, axis=-1, keepdims=True) + L2_EPS)
            y = s * r
            ds = r * (ds - y * jnp.sum(ds * y, axis=-1, keepdims=True))
        dxc = ds * _dsilu(xc)
        dp_ref[...] = sum(w_ref[i:i + 1, :] * _shift_up(dxc, ksz - 1 - i) for i in range(ksz))
        for i in range(ksz):
            dw_ref[i:i + 1, :] = jnp.sum(dxc * xs[i], axis=0, keepdims=True)

    shifted = pl.BlockSpec((T, LANES), lambda j: (0, first + j))
    return pl.pallas_call(
        body, name=name, grid=(width // LANES,),
        in_specs=[shifted, pl.BlockSpec((DN_CONV, LANES), lambda j: (0, first + j)),
                  pl.BlockSpec((T, LANES), lambda j: (0, j)), pl.BlockSpec(memory_space=pl.ANY)],
        out_specs=[shifted, pl.BlockSpec((DN_CONV, LANES), lambda j: (0, j))],
        out_shape=[jax.ShapeDtypeStruct(dp.shape, F32), jax.ShapeDtypeStruct((DN_CONV, width), F32)],
        input_output_aliases={3: 0},
        compiler_params=pltpu.CompilerParams(dimension_semantics=("parallel",), vmem_limit_bytes=VMEM_BIG),
    )(p, conv_w, d, dp)


def _dn_ab_bwd(p, alog_row, dtb_row, dgb, dp, *, name):
    T = p.shape[0]
    rows = min(DN_TB, T)
    H = DN_HEADS

    def body(p_ref, al_ref, dt_ref, dgb_ref, dp_in, dp_ref, dal_ref, ddt_ref):
        del dp_in

        @pl.when(pl.program_id(0) == 0)
        def _():
            dal_ref[...] = jnp.zeros_like(dal_ref)
            ddt_ref[...] = jnp.zeros_like(ddt_ref)

        blk = p_ref[...]
        is_a, is_b, a_arg, neg_exp, log_a, beta = _dn_ab_parts(blk, al_ref[...], dt_ref[...])
        d = dgb_ref[0]
        for hh in range(1, H):
            d = d + dgb_ref[hh]
        hi, mid, lo_ = _split3(jnp.where(is_a, d, 0.0))
        tri = _dn_chunk_tri(rows, upper=True)
        f = lambda q: jnp.dot(tri, q, preferred_element_type=F32)
        dlog_a = f(hi) + f(mid) + f(lo_)
        da_in = dlog_a * neg_exp * _sigmoid(a_arg)
        db_in = jnp.where(is_b, d, 0.0) * beta * (1.0 - beta)
        dp_ref[...] = jnp.where(is_a, da_in, 0.0) + db_in
        dal_ref[...] += jnp.sum(dlog_a * log_a, axis=0, keepdims=True)
        ddt_ref[...] += jnp.sum(jnp.where(is_a, da_in, 0.0), axis=0, keepdims=True)

    blk = pl.BlockSpec((rows, LANES), lambda i: (i, DN_AB_COL))
    vec = pl.BlockSpec((1, LANES), lambda i: (0, 0))
    return pl.pallas_call(
        body, name=name, grid=(T // rows,),
        in_specs=[blk, vec, vec, pl.BlockSpec((H, rows, LANES), lambda i: (0, i, 0)),
                  pl.BlockSpec(memory_space=pl.ANY)],
        out_specs=[blk, vec, vec],
        out_shape=[jax.ShapeDtypeStruct(dp.shape, F32), jax.ShapeDtypeStruct((1, LANES), F32),
                   jax.ShapeDtypeStruct((1, LANES), F32)],
        input_output_aliases={4: 0},
        compiler_params=pltpu.CompilerParams(dimension_semantics=("arbitrary",)),
    )(p, alog_row, dtb_row, dgb, dp)


def _dn_layer_fwd(x, ng, w_in, conv_w, a_log, dt_bias, o_gain, w_out, tag, send=()):
    alog_row, dtb_row = _dn_lane_rows(a_log, dt_bias)
    gain = o_gain.reshape(1, DN_DV)
    h = _rmsnorm_fwd(x, ng, name=f"{tag}_norm")
    p = _matmul(h, w_in, mode="nn", name=f"{tag}_inproj")
    qkv = _dn_prep_fwd(p, conv_w, name=f"{tag}_prep")
    gb = _dn_ab_fwd(p, alog_row, dtb_row, name=f"{tag}_ab")
    o, og, states, *landed = _dn_delta_fwd(qkv, gb, p, gain, name=f"{tag}_delta", send=send)
    x_new = _matmul(og, w_out, mode="nn", res=x, name=f"{tag}_outproj")
    return x_new, (h, p, qkv, gb, o, og, states), landed


def _dn_layer_bwd(dx, x, ng, w_in, conv_w, a_log, dt_bias, o_gain, w_out, saved, tag, send=()):
    h, p, qkv, gb, o, og, states = saved
    alog_row, dtb_row = _dn_lane_rows(a_log, dt_bias)
    gain = o_gain.reshape(1, DN_DV)
    d_wout = _matmul(og, dx, mode="tn", out_dtype=BF16, name=f"{tag}_dwout")
    dog = _matmul(dx, w_out, mode="nt", name=f"{tag}_dog")
    dq, dk, dv, dp, dgb, dgain, *landed = _dn_delta_bwd(qkv, gb, p, gain, o, states, dog, name=f"{tag}_deltabwd",
                                                        send=send)
    n_qk = DN_QK_W // LANES
    dp, dconv_q = _dn_conv_bwd(p, conv_w, dq, dp, first=0, normed=True, name=f"{tag}_convbwd_q")
    dp, dconv_k = _dn_conv_bwd(p, conv_w, dk, dp, first=n_qk, normed=True, name=f"{tag}_convbwd_k")
    dp, dconv_v = _dn_conv_bwd(p, conv_w, dv, dp, first=2 * n_qk, normed=False, name=f"{tag}_convbwd_v")
    dconv = jnp.concatenate([dconv_q, dconv_k, dconv_v], axis=1)
    dp, dal, ddt = _dn_ab_bwd(p, alog_row, dtb_row, dgb, dp, name=f"{tag}_abbwd")
    d_win = _matmul(h, dp, mode="tn", name=f"{tag}_dwin")
    dh = _matmul(dp, w_in, mode="nt", name=f"{tag}_dh")
    dx_prev, dng = _rmsnorm_bwd(x, ng, dh, dx, name=f"{tag}_normbwd")
    return dx_prev, dng, d_win, dconv, dal[0, :DN_HEADS], ddt[0, :DN_HEADS], dgain[0], d_wout, landed


def _sb_gains(g):
    return jnp.concatenate([g, g]).reshape(1, LANES)


def _sb_layer_fwd(x, ng, w_in, gq, gk, w_out, tag):
    h = _rmsnorm_fwd(x, ng, name=f"{tag}_norm")
    p3 = _matmul(h, w_in, mode="nn", b_parts=4, out_parts=4, name=f"{tag}_inproj")
    og, o = _sb_attn_fwd(p3, _sb_gains(gq), _sb_gains(gk), name=f"{tag}_attn")
    x_new = _matmul(og, w_out, mode="nn", res=x, name=f"{tag}_outproj")
    return x_new, (h, p3, og, o)


def _sb_layer_bwd(dx, x, ng, w_in, gq, gk, w_out, saved, tag):
    h, p3, og, o = saved
    d_wout = _matmul(og, dx, mode="tn", out_dtype=BF16, name=f"{tag}_dwout")
    dog = _matmul(dx, w_out, mode="nt", name=f"{tag}_dog")
    dp3, dgq, dgk = _sb_attn_bwd(p3, _sb_gains(gq), _sb_gains(gk), o, dog, name=f"{tag}_attnbwd")
    fold = lambda d: jnp.sum(d.reshape(-1, SB_DH), axis=0)
    d_win = _matmul(h, dp3, mode="tn", b_parts=4, out_parts=4, out_dtype=BF16, name=f"{tag}_dwin")
    dh = _matmul(dp3, w_in, mode="nt", a_parts=4, b_parts=4, name=f"{tag}_dh")
    dx_prev, dng = _rmsnorm_bwd(x, ng, dh, dx, name=f"{tag}_normbwd")
    return dx_prev, dng, d_win, fold(dgq), fold(dgk), d_wout


N_CHIPS = 4
HBM = pl.BlockSpec(memory_space=pl.ANY)


def _mesh_pos():
    return lax.axis_index("x"), lax.axis_index("y"), lax.axis_index("c")


def _other_chips(x, y):
    return [(1 - x, y), (x, 1 - y), (1 - x, 1 - y)]


def _chip_exchange(srcs, *, send_slot_is_dest, copy_own, name):
    n = len(srcs)

    def body(*refs):
        src_refs, out_refs = refs[:n], refs[n:2 * n]
        send_sems, recv_sems, local_sems = refs[2 * n:]
        x, y, c = _mesh_pos()
        me = 2 * x + y
        chips = _other_chips(x, y)
        local = []
        for a in range(n):
            if not copy_own[a]:
                continue
            own = src_refs[a].at[me] if send_slot_is_dest else src_refs[a]
            local.append(pltpu.make_async_copy(own, out_refs[a].at[me], local_sems.at[a]))
        for cp in local:
            cp.start()

        def copy(a, k, landing_slot):
            px, py = chips[k]
            src = src_refs[a].at[2 * px + py] if send_slot_is_dest else src_refs[a]
            return pltpu.make_async_remote_copy(
                src_ref=src, dst_ref=out_refs[a].at[landing_slot],
                send_sem=send_sems.at[a * 3 + k], recv_sem=recv_sems.at[a * 3 + k],
                device_id=(px, py, c), device_id_type=MESH)

        sends = [copy(a, k, me) for a in range(n) for k in range(3)]
        for cp in sends:
            cp.start()
        for a in range(n):
            for k in range(3):
                px, py = chips[k]
                copy(a, k, 2 * px + py).wait_recv()
        for cp in sends:
            cp.wait_send()
        for cp in local:
            cp.wait()

    outs = []
    for s in srcs:
        shape = s.shape if send_slot_is_dest else (N_CHIPS,) + s.shape
        outs.append(jax.ShapeDtypeStruct(shape, s.dtype))
    return pl.pallas_call(
        body, name=name, in_specs=[HBM] * n, out_specs=[HBM] * n, out_shape=outs,
        scratch_shapes=[pltpu.SemaphoreType.DMA((3 * n,)), pltpu.SemaphoreType.DMA((3 * n,)),
                        pltpu.SemaphoreType.DMA((n,))],
    )(*srcs)


def _sibling_exchange(srcs, *, name):
    n = len(srcs)

    def body(*refs):
        src_refs, out_refs = refs[:n], refs[n:2 * n]
        send_sems, recv_sems = refs[2 * n:]
        x, y, c = _mesh_pos()
        copies = [pltpu.make_async_remote_copy(
            src_ref=src_refs[a], dst_ref=out_refs[a], send_sem=send_sems.at[a], recv_sem=recv_sems.at[a],
            device_id=(x, y, 1 - c), device_id_type=MESH) for a in range(n)]
        for cp in copies:
            cp.start()
        for cp in copies:
            cp.wait()

    return pl.pallas_call(
        body, name=name, in_specs=[HBM] * n, out_specs=[HBM] * n,
        out_shape=[jax.ShapeDtypeStruct(s.shape, s.dtype) for s in srcs],
        scratch_shapes=[pltpu.SemaphoreType.DMA((n,)), pltpu.SemaphoreType.DMA((n,))],
    )(*srcs)


def _gather_halves(shards, small, *, name):
    n = len(shards)

    def body(*refs):
        s_refs, small_ref = refs[:n], refs[n]
        o_refs, osmall_ref = refs[n + 1:2 * n + 1], refs[2 * n + 1]
        send_sems, recv_sems, local_sems = refs[2 * n + 2:]
        x, y, c = _mesh_pos()
        me = 2 * x + y
        chips = _other_chips(x, y)
        local = [pltpu.make_async_copy(small_ref, osmall_ref.at[me], local_sems.at[0])]
        for cp in local:
            cp.start()

        def over_ici(a, k, slot):
            px, py = chips[k]
            return pltpu.make_async_remote_copy(
                src_ref=s_refs[a].at[c], dst_ref=o_refs[a].at[slot, c], send_sem=send_sems.at[3 * a + k],
                recv_sem=recv_sems.at[3 * a + k], device_id=(px, py, c), device_id_type=MESH)

        def small_copy(k, slot):
            px, py = chips[k]
            return pltpu.make_async_remote_copy(
                src_ref=small_ref, dst_ref=osmall_ref.at[slot], send_sem=send_sems.at[3 * n + k],
                recv_sem=recv_sems.at[3 * n + k], device_id=(px, py, c), device_id_type=MESH)

        def to_sibling(a, k, half):
            px, py = chips[k]
            blk = o_refs[a].at[2 * px + py, half]
            return pltpu.make_async_remote_copy(
                src_ref=blk, dst_ref=blk, send_sem=send_sems.at[3 * n + 3 + 3 * a + k],
                recv_sem=recv_sems.at[3 * n + 3 + 3 * a + k], device_id=(x, y, 1 - c), device_id_type=MESH)

        sends = [over_ici(a, k, me) for a in range(n) for k in range(3)] + [small_copy(k, me) for k in range(3)]
        for cp in sends:
            cp.start()
        passed = []
        for a in range(n):
            for k in range(3):
                px, py = chips[k]
                over_ici(a, k, 2 * px + py).wait_recv()
                passed.append(to_sibling(a, k, c))
                passed[-1].start()
        for k in range(3):
            px, py = chips[k]
            small_copy(k, 2 * px + py).wait_recv()
        for a in range(n):
            for k in range(3):
                to_sibling(a, k, 1 - c).wait_recv()
        for cp in sends + passed:
            cp.wait_send()
        for cp in local:
            cp.wait()

    nsem = 6 * n + 3
    return pl.pallas_call(
        body, name=name, in_specs=[HBM] * (n + 1), out_specs=[HBM] * (n + 1),
        out_shape=[jax.ShapeDtypeStruct((N_CHIPS,) + s.shape, s.dtype) for s in shards + [small]],
        scratch_shapes=[pltpu.SemaphoreType.DMA((nsem,)), pltpu.SemaphoreType.DMA((nsem,)),
                        pltpu.SemaphoreType.DMA((1,))],
    )(*shards, small)


def _forward_halves(landed, *, name):
    n = len(landed)

    def body(*refs):
        o_refs = refs[n:2 * n]
        send_sems, recv_sems = refs[2 * n:]
        x, y, c = _mesh_pos()
        chips = _other_chips(x, y)
        pairs = [(a, k) for a in range(n) for k in range(3)]

        def copy(a, k, half):
            px, py = chips[k]
            blk = o_refs[a].at[2 * px + py, half]
            return pltpu.make_async_remote_copy(
                src_ref=blk, dst_ref=blk, send_sem=send_sems.at[3 * a + k], recv_sem=recv_sems.at[3 * a + k],
                device_id=(x, y, 1 - c), device_id_type=MESH)

        sends = [copy(a, k, c) for a, k in pairs]
        for cp in sends:
            cp.start()
        for a, k in pairs:
            copy(a, k, 1 - c).wait_recv()
        for cp in sends:
            cp.wait_send()

    return pl.pallas_call(
        body, name=name, in_specs=[HBM] * n, out_specs=[HBM] * n,
        out_shape=[jax.ShapeDtypeStruct(a.shape, a.dtype) for a in landed],
        input_output_aliases={a: a for a in range(n)},
        scratch_shapes=[pltpu.SemaphoreType.DMA((3 * n,)), pltpu.SemaphoreType.DMA((3 * n,))],
    )(*landed)


def _swap_other_half(g_list, *, name):
    n = len(g_list)

    def body(*refs):
        g_refs, o_refs = refs[:n], refs[n:2 * n]
        send_sems, recv_sems = refs[2 * n:]
        x, y, c = _mesh_pos()
        copies = [pltpu.make_async_remote_copy(
            src_ref=g_refs[a].at[:, 1 - c], dst_ref=o_refs[a], send_sem=send_sems.at[a], recv_sem=recv_sems.at[a],
            device_id=(x, y, 1 - c), device_id_type=MESH) for a in range(n)]
        for cp in copies:
            cp.start()
        for cp in copies:
            cp.wait()

    return pl.pallas_call(
        body, name=name, in_specs=[HBM] * n, out_specs=[HBM] * n,
        out_shape=[jax.ShapeDtypeStruct((g.shape[0],) + g.shape[2:], g.dtype) for g in g_list],
        scratch_shapes=[pltpu.SemaphoreType.DMA((n,)), pltpu.SemaphoreType.DMA((n,))],
    )(*g_list)


def _row_tile(r):
    return _pick(r, (512, 256, 128, 64, 32, 16, 8))


def _add_my_half(g4, sib4, core, *, name):
    n, _, r, C = g4.shape
    tr = _row_tile(r)

    def body(core_ref, g_ref, s_ref, o_ref):
        del core_ref
        o_ref[...] = (g_ref[...].astype(F32) + s_ref[...].astype(F32)).astype(o_ref.dtype)

    return pl.pallas_call(
        body, name=name,
        grid_spec=pltpu.PrefetchScalarGridSpec(
            num_scalar_prefetch=1, grid=(n, r // tr),
            in_specs=[pl.BlockSpec((None, None, tr, C), lambda j, i, core_ref: (j, core_ref[0], i, 0)),
                      pl.BlockSpec((None, tr, C), lambda j, i, core_ref: (j, i, 0))],
            out_specs=pl.BlockSpec((None, tr, C), lambda j, i, core_ref: (j, i, 0))),
        out_shape=jax.ShapeDtypeStruct((n, r, C), g4.dtype),
        compiler_params=pltpu.CompilerParams(dimension_semantics=("parallel", "parallel")),
    )(core, g4, sib4)


def _scatter_to_chips(p_list, *, name):
    n = len(p_list)

    def body(*refs):
        p_refs, o_refs = refs[:n], refs[n:2 * n]
        send_sems, recv_sems = refs[2 * n:]
        x, y, c = _mesh_pos()
        me = 2 * x + y
        chips = _other_chips(x, y)
        pairs = [(a, k) for a in range(n) for k in range(3)]

        def copy(a, k, landing_slot):
            px, py = chips[k]
            return pltpu.make_async_remote_copy(
                src_ref=p_refs[a].at[2 * px + py], dst_ref=o_refs[a].at[landing_slot],
                send_sem=send_sems.at[3 * a + k], recv_sem=recv_sems.at[3 * a + k], device_id=(px, py, c),
                device_id_type=MESH)

        sends = [copy(a, k, me) for a, k in pairs]
        for cp in sends:
            cp.start()
        for a, k in pairs:
            px, py = chips[k]
            copy(a, k, 2 * px + py).wait_recv()
        for cp in sends:
            cp.wait_send()

    return pl.pallas_call(
        body, name=name, in_specs=[HBM] * n, out_specs=[HBM] * n,
        out_shape=[jax.ShapeDtypeStruct(p.shape, p.dtype) for p in p_list],
        scratch_shapes=[pltpu.SemaphoreType.DMA((3 * n,)), pltpu.SemaphoreType.DMA((3 * n,))],
    )(*p_list)


def _sum_chips(landed, part, me, *, name):
    _, r, C = landed.shape
    tr = _row_tile(r)

    def body(me_ref, own_ref, r1_ref, r2_ref, r3_ref, o_ref):
        del me_ref
        f = lambda ref: ref[...].astype(F32)
        o_ref[...] = ((f(own_ref) + f(r1_ref)) + f(r2_ref)) + f(r3_ref)

    slot = lambda d: pl.BlockSpec((None, tr, C), lambda i, me_ref: ((me_ref[0] + d) % N_CHIPS, i, 0))
    return pl.pallas_call(
        body, name=name,
        grid_spec=pltpu.PrefetchScalarGridSpec(
            num_scalar_prefetch=1, grid=(r // tr,), in_specs=[slot(0), slot(1), slot(2), slot(3)],
            out_specs=pl.BlockSpec((tr, C), lambda i, me_ref: (i, 0))),
        out_shape=jax.ShapeDtypeStruct((r, C), F32),
        compiler_params=pltpu.CompilerParams(dimension_semantics=("parallel",)),
    )(me, part, landed, landed, landed)


def _adamw_halves(w, mine, theirs, m, v, core, *, layer, prev, name):
    shape = w.shape
    r, C = mine.shape
    tr = _pick(r, (128, 64, 32, 16, 8))
    per = r // tr
    view = lambda a: a.reshape(-1, C)
    n_prev = 0 if prev is None else 4

    def body(*refs):
        core_ref, w_ref, gm_ref, gt_ref, m_ref, v_ref = refs[:6]
        g_ref, d_ref, nm_ref, nv_ref = refs[6 + n_prev:]
        gv = jnp.where(pl.program_id(0) == core_ref[0], gm_ref[...], gt_ref[...])
        g_ref[...] = gv
        d_ref[...], nm_ref[...], nv_ref[...] = _adamw_math(w_ref[...], gv, m_ref[...], v_ref[...])

    half = pl.BlockSpec((tr, C), lambda h, i, core_ref: ((2 * layer + h) * per + i, 0))
    row = pl.BlockSpec((tr, C), lambda h, i, core_ref: (i, 0))
    out = jax.ShapeDtypeStruct((math.prod(shape) // C, C), F32)
    res = pl.pallas_call(
        body, name=name,
        grid_spec=pltpu.PrefetchScalarGridSpec(
            num_scalar_prefetch=1, grid=(2, per), in_specs=[half, row, row, half, half] + [HBM] * n_prev,
            out_specs=[half] * 4),
        out_shape=[out] * 4,
        input_output_aliases={6 + j: j for j in range(n_prev)},
        compiler_params=pltpu.CompilerParams(dimension_semantics=("parallel", "parallel")),
    )(core, view(w), mine, theirs, view(m), view(v), *([] if prev is None else [view(a) for a in prev]))
    return tuple(a.reshape(shape) for a in res)


def _sum_small(recv4, *, name):
    _, R, C = recv4.shape

    def body(r_ref, o_ref):
        o_ref[...] = ((r_ref[0] + r_ref[1]) + r_ref[2]) + r_ref[3]

    return pl.pallas_call(body, name=name, out_shape=jax.ShapeDtypeStruct((R, C), F32))(recv4)


def _add(a, b, *, name):
    R, C = a.shape
    tr = _pick(R, (512, 256, 128, 64, 32, 16, 8))
    blk = pl.BlockSpec((tr, C), lambda i: (i, 0))

    def body(a_ref, b_ref, o_ref):
        o_ref[...] = a_ref[...] + b_ref[...]

    return pl.pallas_call(body, name=name, grid=(R // tr,), in_specs=[blk, blk], out_specs=blk,
                          out_shape=jax.ShapeDtypeStruct((R, C), F32),
                          compiler_params=pltpu.CompilerParams(dimension_semantics=("parallel",)))(a, b)


def _adamw_math(w, g, m, v):
    nm = ADAM_B1 * m + (1.0 - ADAM_B1) * g
    nv = ADAM_B2 * v + (1.0 - ADAM_B2) * (g * g)
    m_hat = nm / (1.0 - ADAM_B1 ** ADAM_STEP)
    v_hat = nv / (1.0 - ADAM_B2 ** ADAM_STEP)
    return -ADAM_LR * (m_hat / (jnp.sqrt(v_hat) + ADAM_EPS) + ADAM_WD * w), nm, nv


def _adamw(w, g, m, v, *, name):
    shape = w.shape
    C = shape[-1]
    R = w.size // C
    two = lambda a: a.reshape(R, C)
    tr = _pick(R, (256, 128, 64, 32, 16, 8)) if R % 8 == 0 and R > 8 else R
    blk = pl.BlockSpec((tr, C), lambda i: (i, 0))

    def body(w_ref, g_ref, m_ref, v_ref, d_ref, nm_ref, nv_ref):
        d_ref[...], nm_ref[...], nv_ref[...] = _adamw_math(w_ref[...], g_ref[...], m_ref[...], v_ref[...])

    out = jax.ShapeDtypeStruct((R, C), F32)
    d, nm, nv = pl.pallas_call(
        body, name=name, grid=(R // tr,), in_specs=[blk] * 4, out_specs=[blk] * 3, out_shape=[out] * 3,
        compiler_params=pltpu.CompilerParams(dimension_semantics=("parallel",)),
    )(two(w), two(g), two(m), two(v))
    return d.reshape(shape), nm.reshape(shape), nv.reshape(shape)


BIG = (("dn_w_in", (2, 1024, 1540), 2), ("dn_w_out", (2, 512, 1024), 1), ("sb_w_in", (1, 1024, 1024), 2),
       ("sb_w_out", (1, 256, 1024), 1), ("sc_w_in", (1, 1024, 2048), 2), ("sc_w_out", (1, 512, 1024), 1))
SMALL = (("dn_conv_w", (2, 4, 1024), 2), ("dn_o_norm_g", (2, 64), 1), ("sc_conv_w", (1, 3, 512), 2))
REPL = (("norm_g", (4, 1024)), ("dn_a_log", (2, 8)), ("dn_dt_bias", (2, 8)), ("sb_q_norm_g", (1, 64)),
        ("sb_k_norm_g", (1, 64)))


def _halves(shard):
    return shard.reshape(2, -1, shard.shape[-1])


def _pack(arrays, cols, lead=()):
    flat = jnp.concatenate([a.reshape(lead + (-1,)) for a in arrays], axis=-1)
    n = flat.shape[-1]
    rows = -(-n // cols)
    unit = 512 if rows > 512 else 8
    rows = -(-rows // unit) * unit
    flat = jnp.pad(flat, [(0, 0)] * len(lead) + [(0, rows * cols - n)])
    return flat.reshape(lead + (rows, cols))


def _unpack(buf, table, lead=()):
    flat = buf.reshape(lead + (-1,))
    out, off = {}, 0
    for entry in table:
        name, shape = entry[0], entry[1]
        n = math.prod(shape)
        out[name] = flat[..., off:off + n].reshape(lead + shape)
        off += n
    return out


def _join(shards, axis):
    return jnp.concatenate([shards[j] for j in range(N_CHIPS)], axis=axis)


def _split(full, axis):
    return jnp.stack(jnp.split(full, N_CHIPS, axis=axis), axis=0)


def kernel(x, norm_g, dn_w_in, dn_conv_w, dn_a_log, dn_dt_bias, dn_o_norm_g, dn_w_out, sb_w_in, sb_q_norm_g, sb_k_norm_g, sb_w_out, sc_w_in, sc_conv_w, sc_w_out, loss_target, m_norm_g, m_dn_w_in, m_dn_conv_w, m_dn_a_log, m_dn_dt_bias, m_dn_o_norm_g, m_dn_w_out, m_sb_w_in, m_sb_q_norm_g, m_sb_k_norm_g, m_sb_w_out, m_sc_w_in, m_sc_conv_w, m_sc_w_out, v_norm_g, v_dn_w_in, v_dn_conv_w, v_dn_a_log, v_dn_dt_bias, v_dn_o_norm_g, v_dn_w_out, v_sb_w_in, v_sb_q_norm_g, v_sb_k_norm_g, v_sb_w_out, v_sc_w_in, v_sc_conv_w, v_sc_w_out):
    weights = dict(norm_g=norm_g, dn_w_in=dn_w_in, dn_conv_w=dn_conv_w, dn_a_log=dn_a_log, dn_dt_bias=dn_dt_bias,
                   dn_o_norm_g=dn_o_norm_g, dn_w_out=dn_w_out, sb_w_in=sb_w_in, sb_q_norm_g=sb_q_norm_g,
                   sb_k_norm_g=sb_k_norm_g, sb_w_out=sb_w_out, sc_w_in=sc_w_in, sc_conv_w=sc_conv_w, sc_w_out=sc_w_out)
    m_in = dict(norm_g=m_norm_g, dn_w_in=m_dn_w_in, dn_conv_w=m_dn_conv_w, dn_a_log=m_dn_a_log,
                dn_dt_bias=m_dn_dt_bias, dn_o_norm_g=m_dn_o_norm_g, dn_w_out=m_dn_w_out, sb_w_in=m_sb_w_in,
                sb_q_norm_g=m_sb_q_norm_g, sb_k_norm_g=m_sb_k_norm_g, sb_w_out=m_sb_w_out, sc_w_in=m_sc_w_in,
                sc_conv_w=m_sc_conv_w, sc_w_out=m_sc_w_out)
    v_in = dict(norm_g=v_norm_g, dn_w_in=v_dn_w_in, dn_conv_w=v_dn_conv_w, dn_a_log=v_dn_a_log,
                dn_dt_bias=v_dn_dt_bias, dn_o_norm_g=v_dn_o_norm_g, dn_w_out=v_dn_w_out, sb_w_in=v_sb_w_in,
                sb_q_norm_g=v_sb_q_norm_g, sb_k_norm_g=v_sb_k_norm_g, sb_w_out=v_sb_w_out, sc_w_in=v_sc_w_in,
                sc_conv_w=v_sc_conv_w, sc_w_out=v_sc_w_out)
    order = list(weights)
    xi, yi, ci = _mesh_pos()

    small = _pack([weights[n] for n, _, _ in SMALL], LANES)
    later = [("dn_w_in", 1), ("dn_w_out", 1), ("sb_w_in", 0), ("sb_w_out", 0), ("sc_w_in", 0), ("sc_w_out", 0)]
    piece = lambda n, l: _halves(weights[n][l].astype(BF16)[None])
    own_first = [piece("dn_w_in", 0), piece("dn_w_out", 0)]
    own_later = [piece(n, l) for n, l in later]
    me = 2 * xi + yi
    whole = lambda g4, own: lax.dynamic_update_index_in_dim(g4, own, me, 0)
    flat = lambda g4: g4.reshape(N_CHIPS, -1, g4.shape[-1])
    rows_of = lambda w4: w4.reshape(-1, w4.shape[-1])
    dn_in = lambda w4: jnp.pad(_join(w4, 1), ((0, 0), (0, DN_IN_PAD - DN_IN)))
    w_in0, w_out0, small4 = _gather_halves(own_first, small, name="gather_first")
    full = {n: _join(a, ax) for (n, _, ax), a in zip(SMALL, _unpack(small4, SMALL, (N_CHIPS,)).values())}

    def dn_args(j, w_in4, w_out4):
        return (dn_in(flat(w_in4)), full["dn_conv_w"][j], dn_a_log[j], dn_dt_bias[j], full["dn_o_norm_g"][j],
                rows_of(w_out4))

    x0 = x[0]
    dn0 = dn_args(0, whole(w_in0, own_first[0]), whole(w_out0, own_first[1]))
    x1, s0, landed = _dn_layer_fwd(x0, norm_g[0], *dn0, "l0", send=own_later)
    landed = _forward_halves(landed, name="forward_halves")
    w_in3, w_out3, sb_in, sb_out, sc_in, sc_out = [whole(g4, own) for g4, own in zip(landed, own_later)]
    dn1 = dn_args(1, w_in3, w_out3)
    sb_args = (flat(sb_in), sb_q_norm_g[0], sb_k_norm_g[0], rows_of(sb_out))
    sc_args = (flat(sc_in), full["sc_conv_w"][0], rows_of(sc_out))
    x2, s1 = _sb_layer_fwd(x1, norm_g[1], *sb_args, "l1")
    x3, s2 = _sc_layer_fwd(x2, norm_g[2], *sc_args, "l2")
    x4, s3, _ = _dn_layer_fwd(x3, norm_g[3], *dn1, "l3")
    dy, loss_local = _loss_head(x4, loss_target[0], name="loss_head")
    loss = lax.psum(loss_local[0, 0], ("x", "y", "c"))

    by_cols = lambda dw: _split(dw[:, :DN_IN].astype(BF16), 1)
    by_rows = lambda dw: dw.reshape(N_CHIPS, -1, dw.shape[-1])
    cut2 = lambda g4: g4.reshape(N_CHIPS, 2, -1, g4.shape[-1])
    core = ci.astype(jnp.int32).reshape(1)
    chip = me.astype(jnp.int32).reshape(1)

    def chip_sums(g_list, tag):
        sib = _swap_other_half(g_list, name=f"swap_halves_{tag}")
        return [_add_my_half(g, s, core, name=f"sum_cores_{tag}{i}") for i, (g, s) in enumerate(zip(g_list, sib))]

    dx3, dng3, dwin3, dconv3, dal3, ddt3, dgain3, dwout3, _ = _dn_layer_bwd(dy, x3, norm_g[3], *dn1, s3, "l3")
    dx2, dng2, dwin2, dconv2, dwout2 = _sc_layer_bwd(dx3, x2, norm_g[2], *sc_args, s2, "l2")
    dx1, dng1, dwin1, dgq, dgk, dwout1 = _sb_layer_bwd(dx2, x1, norm_g[1], *sb_args, s1, "l1")
    part_later = chip_sums([cut2(by_cols(dwin3)), cut2(by_rows(dwout3)), cut2(dwin1), cut2(by_rows(dwout1)),
                            cut2(dwin2), cut2(by_rows(dwout2))], "later")
    dx0, dng0, dwin0, dconv0, dal0, ddt0, dgain0, dwout0, landed_later = _dn_layer_bwd(
        dx1, x0, norm_g[0], *dn0, s0, "l0", send=part_later)
    part_first = chip_sums([cut2(by_cols(dwin0)), cut2(by_rows(dwout0))], "first")
    landed_first = _scatter_to_chips(part_first, name="scatter_first")
    pieces = [("dn_w_in", 0), ("dn_w_out", 0)] + later
    mine = [_sum_chips(r, p, chip, name=f"sum_chips_{n}{l}")
            for (n, l), r, p in zip(pieces, list(landed_first) + list(landed_later), part_first + part_later)]
    theirs = _sibling_exchange(mine, name="swap_results")
    upd = {}
    for (n, l), a, b in zip(pieces, mine, theirs):
        upd[n] = _adamw_halves(weights[n], a, b, m_in[n], v_in[n], core, layer=l, prev=upd.get(n),
                               name=f"adamw_{n}{l}")
    g_out = {n: upd[n][0] for n, _, _ in BIG}

    grads = dict(
        norm_g=jnp.concatenate([dng0, dng1, dng2, dng3], axis=0), dn_conv_w=jnp.stack([dconv0, dconv3]),
        dn_a_log=jnp.stack([dal0, dal3]), dn_dt_bias=jnp.stack([ddt0, ddt3]),
        dn_o_norm_g=jnp.stack([dgain0, dgain3]), sb_q_norm_g=dgq[None], sb_k_norm_g=dgk[None],
        sc_conv_w=dconv2[None])
    repl = [jnp.broadcast_to(grads[n][None], (N_CHIPS,) + s) for n, s in REPL]
    gsmall = _pack([_split(grads[n], ax) for n, _, ax in SMALL] + repl, LANES, (N_CHIPS,))
    rsmall, = _chip_exchange([gsmall], send_slot_is_dest=True, copy_own=(True,), name="scatter_small")
    psmall = _sum_small(rsmall, name="sum_chips_small")
    qsmall, = _sibling_exchange([psmall], name="swap_cores_small")
    tsmall = _add(psmall, qsmall, name="sum_cores_small")
    g_out.update(_unpack(tsmall, SMALL + REPL))

    for n in order:
        if n not in upd:
            upd[n] = (g_out[n],) + _adamw(weights[n], g_out[n], m_in[n], v_in[n], name=f"adamw_{n}")
    return (loss, dx0[None], *[upd[n][0] for n in order], *[upd[n][1] for n in order],
            *[upd[n][2] for n in order], *[upd[n][3] for n in order])
```

```python
import math

import jax
import jax.numpy as jnp
from jax import lax
from jax.experimental import pallas as pl
from jax.experimental.pallas import tpu as pltpu

F32 = jnp.float32
BF16 = jnp.bfloat16
MESH = pl.DeviceIdType.MESH

RMS_EPS = 1e-6
L2_EPS = 1e-6
LANES = 128
VMEM_BIG = 60 * 1024 * 1024
MM_VMEM = 44 * 1024 * 1024

DN_HEADS, DN_DK, DN_DV, DN_CHUNK, DN_CONV = 8, 128, 256, 64, 4
DN_QK_W = DN_HEADS * DN_DK
DN_V_W = DN_HEADS * DN_DV
DN_CONV_W = 2 * DN_QK_W + DN_V_W
DN_IN = DN_CONV_W + DN_V_W + 2 * DN_HEADS
DN_IN_PAD = DN_CONV_W + DN_V_W + LANES
SB_DH = 64
SC_CONV = 3

ADAM_LR, ADAM_B1, ADAM_B2, ADAM_EPS, ADAM_WD, ADAM_STEP = 0.001, 0.9, 0.999, 1e-08, 0.01, 10


def _pick(n, cands):
    for c in cands:
        if n % c == 0:
            return c
    raise ValueError(f"no tile for {n} in {cands}")


def _bf(x):
    return x.astype(BF16)


def _dot(a, b):
    return jnp.dot(_bf(a), _bf(b), preferred_element_type=F32)


def _dot_nt(a, b):
    return lax.dot_general(_bf(a), _bf(b), (((1,), (1,)), ((), ())), preferred_element_type=F32)


def _dot_tn(a, b):
    return lax.dot_general(_bf(a), _bf(b), (((0,), (0,)), ((), ())), preferred_element_type=F32)


def _split3(a):
    hi = _bf(a)
    r = a - hi.astype(F32)
    mid = _bf(r)
    lo = _bf(r - mid.astype(F32))
    return hi, mid, lo


def _sigmoid(x):
    return 1.0 / (1.0 + jnp.exp(-x))


def _silu(x):
    return x * _sigmoid(x)


def _dsilu(x):
    s = _sigmoid(x)
    return s * (1.0 + x * (1.0 - s))


def _softplus(x):
    return jnp.maximum(x, 0.0) + jnp.log(1.0 + jnp.exp(-jnp.abs(x)))


def _shift_down(z, k):
    if k == 0:
        return z
    row = lax.broadcasted_iota(jnp.int32, z.shape, 0)
    return jnp.where(row >= k, pltpu.roll(z, k, 0), 0.0)


def _shift_up(z, k):
    if k == 0:
        return z
    n = z.shape[0]
    row = lax.broadcasted_iota(jnp.int32, z.shape, 0)
    return jnp.where(row < n - k, pltpu.roll(z, n - k, 0), 0.0)


def _matmul(a, b, *, mode, name, res=None, a_parts=1, b_parts=1, out_parts=1, out_dtype=F32):
    def dims2(x, parts):
        if parts == 1:
            return x.shape
        assert x.shape[0] == parts
        return (x.shape[1], x.shape[2] * parts)

    ash, bsh = dims2(a, a_parts), dims2(b, b_parts)
    if mode == "nn":
        (M, K), (K2, N) = ash, bsh
        dn = (((1,), (0,)), ((), ()))
    elif mode == "nt":
        (M, K), (N, K2) = ash, bsh
        dn = (((1,), (1,)), ((), ()))
    else:
        (K, M), (K2, N) = ash, bsh
        dn = (((0,), (0,)), ((), ()))
    assert K == K2, (ash, bsh, mode)
    tm_max = _pick(M, (512, 256, 128, 64, 32, 16, 8))
    n_unit = N // max(out_parts, b_parts if mode != "nt" else 1)
    k_unit = K // max(a_parts if mode != "tn" else 1, b_parts if mode == "nt" else 1)
    tm, tn, tk = min(
        ((m, n, k) for m in {tm_max, max(tm_max // 2, 8)}
         for n in (2048, 1792, 1024, 896, 768, 512, 384, 256, 128) if n_unit % n == 0
         for k in (k_unit, 2048, 1792, 1024, 896, 512, 256, 128) if k_unit % k == 0
         if 2 * (m * k * a.dtype.itemsize + k * n * b.dtype.itemsize + 2 * m * n * 4) + m * n * 4 <= MM_VMEM),
        key=lambda t: (-t[0] * t[1] * t[2], -t[0], -t[2]))
    nk = K // tk
    grid = (M // tm, N // tn, nk)

    def spec(parts, rows_are, cols_are, tr, tc, width):
        per = width // parts // tc
        if parts == 1:
            return pl.BlockSpec((tr, tc), lambda i, j, k: ((i, j, k)[rows_are], (i, j, k)[cols_are]))
        return pl.BlockSpec((None, tr, tc), lambda i, j, k: ((i, j, k)[cols_are] // per, (i, j, k)[rows_are],
                                                             (i, j, k)[cols_are] % per))

    if mode == "nn":
        a_spec = spec(a_parts, 0, 2, tm, tk, K)
        b_spec = spec(b_parts, 2, 1, tk, tn, N)
    elif mode == "nt":
        a_spec = spec(a_parts, 0, 2, tm, tk, K)
        b_spec = spec(b_parts, 1, 2, tn, tk, K)
    else:
        a_spec = spec(a_parts, 2, 0, tk, tm, M)
        b_spec = spec(b_parts, 2, 1, tk, tn, N)
    o_spec = spec(out_parts, 0, 1, tm, tn, N)
    in_specs = [a_spec, b_spec]
    operands = [a, b]
    if res is not None:
        in_specs.append(pl.BlockSpec((tm, tn), lambda i, j, k: (i, j)))
        operands.append(res)

    def finish(refs, r):
        if res is not None:
            r = refs[2][...] + r
        refs[-2 if nk > 1 else -1][...] = r.astype(out_dtype)

    def body(*refs):
        part = lax.dot_general(_bf(refs[0][...]), _bf(refs[1][...]), dn, preferred_element_type=F32)
        if nk == 1:
            finish(refs, part)
            return
        acc_ref = refs[-1]
        k = pl.program_id(2)

        @pl.when(k == 0)
        def _():
            acc_ref[...] = part

        @pl.when(jnp.logical_and(k > 0, k < nk - 1))
        def _():
            acc_ref[...] += part

        @pl.when(k == nk - 1)
        def _():
            finish(refs, acc_ref[...] + part)

    out_shape = (M, N) if out_parts == 1 else (out_parts, M, N // out_parts)
    return pl.pallas_call(
        body, name=name, grid=grid, in_specs=in_specs, out_specs=o_spec,
        out_shape=jax.ShapeDtypeStruct(out_shape, out_dtype),
        scratch_shapes=[pltpu.VMEM((tm, tn), F32)] if nk > 1 else [],
        compiler_params=pltpu.CompilerParams(dimension_semantics=("parallel", "parallel", "arbitrary"),
                                             vmem_limit_bytes=VMEM_BIG),
    )(*operands)


def _rmsnorm_fwd(x, g, *, name):
    T, D = x.shape
    tm = _pick(T, (512, 256, 128, 64, 32, 16))

    def body(x_ref, g_ref, h_ref):
        xv = x_ref[...]
        r = lax.rsqrt(jnp.mean(xv * xv, axis=-1, keepdims=True) + RMS_EPS)
        h_ref[...] = ((xv * r) * g_ref[...]).astype(BF16)

    return pl.pallas_call(
        body, name=name, grid=(T // tm,),
        in_specs=[pl.BlockSpec((tm, D), lambda i: (i, 0)), pl.BlockSpec((1, D), lambda i: (0, 0))],
        out_specs=pl.BlockSpec((tm, D), lambda i: (i, 0)),
        out_shape=jax.ShapeDtypeStruct((T, D), BF16),
    )(x, g.reshape(1, D))


def _rmsnorm_bwd(x, g, dh, dx_in, *, name):
    T, D = x.shape
    tm = _pick(T, (512, 256, 128, 64, 32, 16))

    def body(x_ref, g_ref, dh_ref, dxin_ref, dx_ref, dg_ref):
        @pl.when(pl.program_id(0) == 0)
        def _():
            dg_ref[...] = jnp.zeros_like(dg_ref)

        xv = x_ref[...]
        r = lax.rsqrt(jnp.mean(xv * xv, axis=-1, keepdims=True) + RMS_EPS)
        xh = xv * r
        dh_v = dh_ref[...]
        dxh = dh_v * g_ref[...]
        dx_ref[...] = dxin_ref[...] + r * (dxh - xh * jnp.mean(dxh * xh, axis=-1, keepdims=True))
        dg_ref[...] += jnp.sum(dh_v * xh, axis=0, keepdims=True)

    row = pl.BlockSpec((tm, D), lambda i: (i, 0))
    vec = pl.BlockSpec((1, D), lambda i: (0, 0))
    return pl.pallas_call(
        body, name=name, grid=(T // tm,),
        in_specs=[row, vec, row, row], out_specs=[row, vec],
        out_shape=[jax.ShapeDtypeStruct((T, D), F32), jax.ShapeDtypeStruct((1, D), F32)],
        compiler_params=pltpu.CompilerParams(dimension_semantics=("arbitrary",)),
    )(x, g.reshape(1, D), dh, dx_in)


def _loss_head(y, target, *, name):
    T, D = y.shape
    tm = _pick(T, (512, 256, 128, 64, 32, 16))

    def body(y_ref, t_ref, dy_ref, l_ref):
        @pl.when(pl.program_id(0) == 0)
        def _():
            l_ref[...] = jnp.zeros_like(l_ref)

        err = y_ref[...] - t_ref[...]
        dy_ref[...] = err * (1.0 / D)
        l_ref[...] += 0.5 * jnp.sum(jnp.mean(err * err, axis=-1, keepdims=True), axis=0, keepdims=True)

    row = pl.BlockSpec((tm, D), lambda i: (i, 0))
    return pl.pallas_call(
        body, name=name, grid=(T // tm,),
        in_specs=[row, row], out_specs=[row, pl.BlockSpec((1, 1), lambda i: (0, 0))],
        out_shape=[jax.ShapeDtypeStruct((T, D), F32), jax.ShapeDtypeStruct((1, 1), F32)],
        compiler_params=pltpu.CompilerParams(dimension_semantics=("arbitrary",)),
    )(y, target)


def _sc_mid_fwd(p3, conv_w, *, name):
    _, T, W = p3.shape
    K = conv_w.shape[0]
    cw = LANES

    def body(p_ref, w_ref, o_ref):
        z = p_ref[1] * p_ref[2]
        cv = sum(w_ref[i:i + 1, :] * _shift_down(z, K - 1 - i) for i in range(K))
        o_ref[...] = ((p_ref[0] * cv) * _silu(p_ref[3])).astype(BF16)

    return pl.pallas_call(
        body, name=name, grid=(W // cw,),
        in_specs=[pl.BlockSpec((4, T, cw), lambda j: (0, 0, j)), pl.BlockSpec((K, cw), lambda j: (0, j))],
        out_specs=pl.BlockSpec((T, cw), lambda j: (0, j)),
        out_shape=jax.ShapeDtypeStruct((T, W), BF16),
        compiler_params=pltpu.CompilerParams(dimension_semantics=("parallel",), vmem_limit_bytes=VMEM_BIG),
    )(p3, conv_w)


def _sc_mid_bwd(p3, conv_w, do, *, name):
    _, T, W = p3.shape
    K = conv_w.shape[0]
    cw = LANES

    def body(p_ref, w_ref, do_ref, dp_ref, dw_ref):
        b, c, u, gate = p_ref[0], p_ref[1], p_ref[2], p_ref[3]
        z = c * u
        zs = [_shift_down(z, K - 1 - i) for i in range(K)]
        cv = sum(w_ref[i:i + 1, :] * zs[i] for i in range(K))
        y = b * cv
        dov = do_ref[...]
        dy = dov * _silu(gate)
        dp_ref[3] = dov * y * _dsilu(gate)
        dp_ref[0] = dy * cv
        dcv = dy * b
        dz = sum(w_ref[i:i + 1, :] * _shift_up(dcv, K - 1 - i) for i in range(K))
        dp_ref[1] = dz * u
        dp_ref[2] = dz * c
        for i in range(K):
            dw_ref[i:i + 1, :] = jnp.sum(dcv * zs[i], axis=0, keepdims=True)

    return pl.pallas_call(
        body, name=name, grid=(W // cw,),
        in_specs=[pl.BlockSpec((4, T, cw), lambda j: (0, 0, j)), pl.BlockSpec((K, cw), lambda j: (0, j)),
                  pl.BlockSpec((T, cw), lambda j: (0, j))],
        out_specs=[pl.BlockSpec((4, T, cw), lambda j: (0, 0, j)), pl.BlockSpec((K, cw), lambda j: (0, j))],
        out_shape=[jax.ShapeDtypeStruct((4, T, W), F32), jax.ShapeDtypeStruct((K, W), F32)],
        compiler_params=pltpu.CompilerParams(dimension_semantics=("parallel",), vmem_limit_bytes=VMEM_BIG),
    )(p3, conv_w, do)


def _sc_layer_fwd(x, ng, w_in, conv_w, w_out, tag):
    h = _rmsnorm_fwd(x, ng, name=f"{tag}_norm")
    p3 = _matmul(h, w_in, mode="nn", b_parts=4, out_parts=4, name=f"{tag}_inproj")
    og = _sc_mid_fwd(p3, conv_w, name=f"{tag}_mid")
    x_new = _matmul(og, w_out, mode="nn", res=x, name=f"{tag}_outproj")
    return x_new, (h, p3, og)


def _sc_layer_bwd(dx, x, ng, w_in, conv_w, w_out, saved, tag):
    h, p3, og = saved
    d_wout = _matmul(og, dx, mode="tn", out_dtype=BF16, name=f"{tag}_dwout")
    dog = _matmul(dx, w_out, mode="nt", name=f"{tag}_dog")
    dp3, dconv = _sc_mid_bwd(p3, conv_w, dog, name=f"{tag}_midbwd")
    d_win = _matmul(h, dp3, mode="tn", b_parts=4, out_parts=4, out_dtype=BF16, name=f"{tag}_dwin")
    dh = _matmul(dp3, w_in, mode="nt", a_parts=4, b_parts=4, name=f"{tag}_dh")
    dx_prev, dng = _rmsnorm_bwd(x, ng, dh, dx, name=f"{tag}_normbwd")
    return dx_prev, dng, d_win, dconv, d_wout


SB_BQ = 256
SB_BK = 256
SB_ROWS = 512
SB_DEAD = -110.0


def _sb_half_mask():
    return lax.broadcasted_iota(jnp.int32, (1, LANES), 1) < SB_DH


def _sb_headnorm(x, g, lo):
    x2 = x * x
    s_lo = jnp.sum(jnp.where(lo, x2, 0.0), axis=-1, keepdims=True)
    s_hi = jnp.sum(jnp.where(lo, 0.0, x2), axis=-1, keepdims=True)
    r = lax.rsqrt(jnp.where(lo, s_lo, s_hi) * (1.0 / SB_DH) + RMS_EPS)
    xh = x * r
    return xh * g, xh, r


def _dot_x2_l(a_l, b_exact_bf16):
    his = [_bf(a) for a in a_l]
    mids = [_bf(a - h.astype(F32)) for a, h in zip(a_l, his)]
    f = lambda p: jnp.dot(p, b_exact_bf16, preferred_element_type=F32)
    return [x + y for x, y in zip([f(h) for h in his], [f(m) for m in mids])]


def _sb_stack(xb, lo):
    zero = jnp.zeros_like(xb)
    return jnp.concatenate([jnp.where(lo, xb, zero), jnp.where(lo, zero, xb)], axis=0)


def _sb_rel(bq, bk):
    row = lax.broadcasted_iota(jnp.int32, (2 * bq, bk), 0)
    col = lax.broadcasted_iota(jnp.int32, (2 * bq, bk), 1)
    return col - jnp.where(row >= bq, row - bq, row)


def _sb_tile(qm, kb, valid):
    z = lax.dot_general(qm, kb, (((1,), (1,)), ((), ())), preferred_element_type=F32)
    sp = _softplus(z)
    return z - sp, (-sp if valid is None else jnp.where(valid, -sp, 0.0))


def _sb_attn_fwd(p3, gq2, gk2, *, name, send=()):
    _, T, W = p3.shape
    bq, bk = min(SB_BQ, T), min(SB_BK, T)
    rows = min(SB_ROWS, T)
    scale = SB_DH ** -0.5
    ns = len(send)
    npair = W // LANES

    def body(*refs):
        p_ref, gq_ref, gk_ref = refs[:3]
        og_ref, o_ref = refs[3 + ns:5 + ns]
        qn_ref, kn_ref, v_ref = refs[5 + 2 * ns:8 + 2 * ns]
        if ns:
            _halves_over_ici(refs[3:3 + ns], refs[5 + ns:5 + 2 * ns], refs[8 + 2 * ns], refs[9 + 2 * ns],
                             pl.program_id(0) == 0, pl.program_id(0) == npair - 1)
        lo = _sb_half_mask()

        def prologue(i, c):
            r0 = pl.multiple_of(i * rows, rows)
            sl = pl.ds(r0, rows)
            qn_ref[sl, :] = (_sb_headnorm(p_ref[0, sl, :], gq_ref[...], lo)[0] * scale).astype(BF16)
            kn_ref[sl, :] = _sb_headnorm(p_ref[1, sl, :], gk_ref[...], lo)[0].astype(BF16)
            v_ref[sl, :] = p_ref[2, sl, :].astype(BF16)
            return c

        lax.fori_loop(0, T // rows, prologue, 0)

        rel = _sb_rel(bq, bk)
        tri = (lax.broadcasted_iota(jnp.int32, (bk, bk), 0)
               > lax.broadcasted_iota(jnp.int32, (bk, bk), 1)).astype(BF16)

        def qblock(qi, c):
            q0 = pl.multiple_of(qi * bq, bq)
            qm = _sb_stack(qn_ref[pl.ds(q0, bq), :], lo)
            nkb = (q0 + bq - 1) // bk + 1

            def tiles(k0s, carry, valid):
                o_acc, a_carry = carry
                sc = [_sb_tile(qm, kn_ref[pl.ds(k0, bk), :], valid) for k0 in k0s]
                later = _dot_x2_l([log1m for _, log1m in sc], tri)
                for (logsig, log1m), lat, k0 in zip(sc, later, k0s):
                    wts = jnp.exp(logsig + (lat + a_carry))
                    if valid is not None:
                        wts = jnp.where(valid, wts, 0.0)
                    o_acc = o_acc + jnp.dot(_bf(wts), v_ref[pl.ds(k0, bk), :], preferred_element_type=F32)
                    a_carry = a_carry + jnp.sum(log1m, axis=-1, keepdims=True)
                return o_acc, a_carry

            blk0 = lambda j: pl.multiple_of(j * bk, bk)
            k_last = blk0(nkb - 1)
            o2, t2 = tiles([k_last], (jnp.zeros((2 * bq, LANES), F32), jnp.zeros((2 * bq, 1), F32)), rel < q0 - k_last)

            def alive(st):
                return jnp.logical_and(st[0] < nkb - 1, jnp.max(st[2]) > SB_DEAD)

            def back_one(st):
                return (st[0] + 1,) + tiles([blk0(nkb - 2 - st[0])], st[1:], None)

            _, o2, _ = lax.while_loop(alive, back_one, (jnp.int32(0), o2, t2))
            o = jnp.where(lo, o2[:bq], o2[bq:])
            o_ref[pl.ds(q0, bq), :] = o
            og_ref[pl.ds(q0, bq), :] = (o * _silu(p_ref[3, pl.ds(q0, bq), :])).astype(BF16)
            return c

        lax.fori_loop(0, T // bq, qblock, 0)

    colblk = pl.BlockSpec((T, LANES), lambda j: (0, j))
    vec = pl.BlockSpec((1, LANES), lambda j: (0, 0))
    return pl.pallas_call(
        body, name=name, grid=(npair,),
        in_specs=[pl.BlockSpec((4, T, LANES), lambda j: (0, 0, j)), vec, vec] + [HBM] * ns,
        out_specs=[colblk, colblk] + [HBM] * ns,
        out_shape=[jax.ShapeDtypeStruct((T, W), BF16), jax.ShapeDtypeStruct((T, W), F32)]
        + [jax.ShapeDtypeStruct((N_CHIPS,) + a.shape, a.dtype) for a in send],
        scratch_shapes=[pltpu.VMEM((T, LANES), BF16)] * 3
        + ([pltpu.SemaphoreType.DMA((3 * ns,)), pltpu.SemaphoreType.DMA((3 * ns,))] if ns else []),
        compiler_params=pltpu.CompilerParams(dimension_semantics=("arbitrary",), vmem_limit_bytes=VMEM_BIG),
    )(p3, gq2, gk2, *send)


def _sb_attn_bwd(p3, gq2, gk2, o, dog, *, name):
    _, T, W = p3.shape
    bq, bk = min(SB_BQ, T), min(SB_BK, T)
    rows = min(SB_ROWS, T)
    scale = SB_DH ** -0.5

    def body(p_ref, gq_ref, gk_ref, o_ref, dog_ref, dp_ref, dgq_ref, dgk_ref,
             qn_ref, kn_ref, v_ref, do_ref):
        lo = _sb_half_mask()

        def prologue(i, c):
            r0 = pl.multiple_of(i * rows, rows)
            sl = pl.ds(r0, rows)
            qn_ref[sl, :] = (_sb_headnorm(p_ref[0, sl, :], gq_ref[...], lo)[0] * scale).astype(BF16)
            kn_ref[sl, :] = _sb_headnorm(p_ref[1, sl, :], gk_ref[...], lo)[0].astype(BF16)
            v_ref[sl, :] = p_ref[2, sl, :].astype(BF16)
            gate = p_ref[3, sl, :]
            dogv = dog_ref[sl, :]
            dp_ref[3, sl, :] = dogv * o_ref[sl, :] * _dsilu(gate)
            do_ref[sl, :] = (dogv * _silu(gate)).astype(BF16)
            zero = jnp.zeros((rows, LANES), F32)
            dp_ref[0, sl, :] = zero
            dp_ref[1, sl, :] = zero
            dp_ref[2, sl, :] = zero
            return c

        lax.fori_loop(0, T // rows, prologue, 0)

        rel = _sb_rel(bq, bk)
        rj = lax.broadcasted_iota(jnp.int32, (bk, bk), 0)
        cj = lax.broadcasted_iota(jnp.int32, (bk, bk), 1)
        upto = (rj <= cj).astype(BF16)
        before_m = (rj < cj).astype(BF16)

        def qblock(qi, c):
            q0 = pl.multiple_of(qi * bq, bq)
            qm = _sb_stack(qn_ref[pl.ds(q0, bq), :], lo)
            dom = _sb_stack(do_ref[pl.ds(q0, bq), :], lo)
            nkb = (q0 + bq - 1) // bk + 1
            blk0 = lambda j: pl.multiple_of(j * bk, bk)
            k_last = blk0(nkb - 1)

            def row_sums(k0, valid):
                return jnp.sum(_sb_tile(qm, kn_ref[pl.ds(k0, bk), :], valid)[1], axis=-1, keepdims=True)

            def alive(st):
                return jnp.logical_and(st[0] < nkb, jnp.max(st[1]) > SB_DEAD)

            def back_one(st):
                return st[0] + 1, st[1] + row_sums(blk0(nkb - 1 - st[0]), None)

            n_live, total = lax.while_loop(alive, back_one, (jnp.int32(1), row_sums(k_last, rel < q0 - k_last)))
            k_first = nkb - n_live

            def tiles(k0s, carry, valid):
                dq_acc, a_pre, r_pre = carry
                kss = [pl.ds(k0, bk) for k0 in k0s]
                kbs = [kn_ref[ks, :] for ks in kss]
                sc = [_sb_tile(qm, kb, valid) for kb in kbs]
                dws = [lax.dot_general(dom, v_ref[ks, :], _NT, preferred_element_type=F32) for ks in kss]
                upto_l = _dot_x2_l([log1m for _, log1m in sc], upto)
                wts_l = []
                for (logsig, log1m), up in zip(sc, upto_l):
                    wts = jnp.exp(logsig + ((total - a_pre) - up))
                    wts_l.append(wts if valid is None else jnp.where(valid, wts, 0.0))
                    a_pre = a_pre + jnp.sum(log1m, axis=-1, keepdims=True)
                ee_l = [dw * wts for dw, wts in zip(dws, wts_l)]
                before_l = _dot_x2_l(ee_l, before_m)
                for (logsig, _), ks, kb, wts, ee, bef in zip(sc, kss, kbs, wts_l, ee_l, before_l):
                    beta = jnp.exp(logsig)
                    dz = ee * (1.0 - beta) - beta * (r_pre + bef)
                    if valid is not None:
                        dz = jnp.where(valid, dz, 0.0)
                    dzb = _bf(dz)
                    dq_acc = dq_acc + jnp.dot(dzb, kb, preferred_element_type=F32)
                    dp_ref[1, ks, :] += lax.dot_general(dzb, qm, _TN, preferred_element_type=F32)
                    dp_ref[2, ks, :] += lax.dot_general(_bf(wts), dom, _TN, preferred_element_type=F32)
                    r_pre = r_pre + jnp.sum(ee, axis=-1, keepdims=True)
                return dq_acc, a_pre, r_pre

            cr = (jnp.zeros((2 * bq, LANES), F32), jnp.zeros((2 * bq, 1), F32), jnp.zeros((2 * bq, 1), F32))
            cr = lax.fori_loop(0, (n_live - 1) // 2,
                               lambda t, cr: tiles([blk0(k_first + 2 * t), blk0(k_first + 2 * t + 1)], cr, None), cr)
            cr = lax.fori_loop(0, (n_live - 1) % 2, lambda t, cr: tiles([blk0(nkb - 2)], cr, None), cr)
            dq2, _, _ = tiles([k_last], cr, rel < q0 - k_last)
            dp_ref[0, pl.ds(q0, bq), :] = jnp.where(lo, dq2[:bq], dq2[bq:]) * scale
            return c

        lax.fori_loop(0, T // bq, qblock, 0)

        dgq_ref[...] = jnp.zeros_like(dgq_ref)
        dgk_ref[...] = jnp.zeros_like(dgk_ref)

        def epilogue(i, c):
            r0 = pl.multiple_of(i * rows, rows)
            sl = pl.ds(r0, rows)
            for part, g_ref, dg_ref in ((0, gq_ref, dgq_ref), (1, gk_ref, dgk_ref)):
                _, xh, r = _sb_headnorm(p_ref[part, sl, :], g_ref[...], lo)
                dn = dp_ref[part, sl, :]
                dxh = dn * g_ref[...]
                prod = dxh * xh
                m_lo = jnp.sum(jnp.where(lo, prod, 0.0), axis=-1, keepdims=True)
                m_hi = jnp.sum(jnp.where(lo, 0.0, prod), axis=-1, keepdims=True)
                m = jnp.where(lo, m_lo, m_hi) * (1.0 / SB_DH)
                dp_ref[part, sl, :] = r * (dxh - xh * m)
                dg_ref[...] += jnp.sum(dn * xh, axis=0, keepdims=True)
            return c

        lax.fori_loop(0, T // rows, epilogue, 0)

    colblk = pl.BlockSpec((T, LANES), lambda j: (0, j))
    vec = pl.BlockSpec((1, LANES), lambda j: (0, 0))
    part = pl.BlockSpec((4, T, LANES), lambda j: (0, 0, j))
    gvec = pl.BlockSpec((None, 1, LANES), lambda j: (j, 0, 0))
    npair = W // LANES
    return pl.pallas_call(
        body, name=name, grid=(npair,),
        in_specs=[part, vec, vec, colblk, colblk],
        out_specs=[part, gvec, gvec],
        out_shape=[jax.ShapeDtypeStruct((4, T, W), F32), jax.ShapeDtypeStruct((npair, 1, LANES), F32),
                   jax.ShapeDtypeStruct((npair, 1, LANES), F32)],
        scratch_shapes=[pltpu.VMEM((T, LANES), BF16)] * 4,
        compiler_params=pltpu.CompilerParams(dimension_semantics=("parallel",), vmem_limit_bytes=VMEM_BIG),
    )(p3, gq2, gk2, o, dog)


_NN = (((1,), (0,)), ((), ()))
_NT = (((1,), (1,)), ((), ()))
_TN = (((0,), (0,)), ((), ()))
DN_TB = 512
DN_HEADS_PER_STEP = 2
DN_INV_EXACT_LEVELS = 2
DN_AB_COL = (DN_CONV_W + DN_V_W) // LANES


def _dn_conv(x, w_ref):
    k = w_ref.shape[0]
    return sum(w_ref[i:i + 1, :] * _shift_down(x, k - 1 - i) for i in range(k))


def _dn_prep_fwd(p, conv_w, *, name):
    T = p.shape[0]
    cw = conv_w.shape[1]
    n_qk = 2 * DN_QK_W // LANES

    def body(p_ref, w_ref, o_ref):
        s = _silu(_dn_conv(p_ref[...], w_ref))
        r = lax.rsqrt(jnp.sum(s * s, axis=-1, keepdims=True) + L2_EPS)
        o_ref[...] = jnp.where(pl.program_id(0) < n_qk, s * r, s)

    colblk = pl.BlockSpec((T, LANES), lambda j: (0, j))
    return pl.pallas_call(
        body, name=name, grid=(cw // LANES,),
        in_specs=[colblk, pl.BlockSpec((DN_CONV, LANES), lambda j: (0, j))],
        out_specs=colblk, out_shape=jax.ShapeDtypeStruct((T, cw), F32),
        compiler_params=pltpu.CompilerParams(dimension_semantics=("parallel",), vmem_limit_bytes=VMEM_BIG),
    )(p, conv_w)


def _dn_chunk_tri(rows, upper):
    r = lax.broadcasted_iota(jnp.int32, (rows, rows), 0)
    c = lax.broadcasted_iota(jnp.int32, (rows, rows), 1)
    same = (r // DN_CHUNK) == (c // DN_CHUNK)
    return jnp.logical_and(same, (c >= r) if upper else (c <= r)).astype(BF16)


def _dn_lane_rows(a_log, dt_bias):
    pad = lambda v: jnp.zeros((1, LANES), F32).at[0, :DN_HEADS].set(v)
    return pad(a_log), pad(dt_bias)


def _dn_ab_parts(blk, alog_row, dtb_row):
    lane = lax.broadcasted_iota(jnp.int32, (1, LANES), 1)
    is_a = lane < DN_HEADS
    is_b = jnp.logical_and(lane >= DN_HEADS, lane < 2 * DN_HEADS)
    a_arg = jnp.where(is_a, blk + dtb_row, 0.0)
    neg_exp = jnp.where(is_a, -jnp.exp(alog_row), 0.0)
    log_a = neg_exp * _softplus(a_arg)
    beta = jnp.where(is_b, _sigmoid(blk), 0.0)
    return is_a, is_b, a_arg, neg_exp, log_a, beta


def _dn_ab_fwd(p, alog_row, dtb_row, *, name):
    T = p.shape[0]
    rows = min(DN_TB, T)

    def body(p_ref, al_ref, dt_ref, o_ref):
        _, _, _, _, log_a, beta = _dn_ab_parts(p_ref[...], al_ref[...], dt_ref[...])
        hi, mid, lo_ = _split3(log_a)
        tri = _dn_chunk_tri(rows, upper=False)
        f = lambda q: jnp.dot(tri, q, preferred_element_type=F32)
        o_ref[...] = (f(hi) + f(mid) + f(lo_)) + beta

    blk = pl.BlockSpec((rows, LANES), lambda i: (i, DN_AB_COL))
    vec = pl.BlockSpec((1, LANES), lambda i: (0, 0))
    return pl.pallas_call(
        body, name=name, grid=(T // rows,), in_specs=[blk, vec, vec],
        out_specs=pl.BlockSpec((rows, LANES), lambda i: (i, 0)),
        out_shape=jax.ShapeDtypeStruct((T, LANES), F32),
        compiler_params=pltpu.CompilerParams(dimension_semantics=("parallel",)),
    )(p, alog_row, dtb_row)


def _hp_l(a_l, b_l, dims=_NN):
    sa = [_split3(a)[:2] for a in a_l]
    sb = [_split3(b)[:2] for b in b_l]
    f = lambda p, q: lax.dot_general(p, q, dims, preferred_element_type=F32)
    hh = [f(x[0], y[0]) for x, y in zip(sa, sb)]
    hm = [f(x[0], y[1]) for x, y in zip(sa, sb)]
    mh = [f(x[1], y[0]) for x, y in zip(sa, sb)]
    return [a + (b + c) for a, b, c in zip(hh, hm, mh)]


def _dn_local(qs, k, v, g, beta, nc):
    c = DN_CHUNK
    cut = lambda x: [x[i * c:(i + 1) * c] for i in range(nc)]
    row = lax.broadcasted_iota(jnp.int32, (c, c), 0)
    col = lax.broadcasted_iota(jnp.int32, (c, c), 1)
    eye, lower, strict = row == col, row >= col, row > col
    rowid = lax.broadcasted_iota(jnp.int32, (c, 1), 0)
    eg = jnp.exp(g)
    kb = k * beta
    rhs_k = kb * eg
    g_l, k_l, kb_l, qs_l = cut(g), cut(k), cut(kb), cut(qs)
    g_row_l = [jnp.sum(jnp.where(eye, x, 0.0), axis=0, keepdims=True) for x in g_l]
    dec_l = [jnp.where(lower, jnp.exp(jnp.where(lower, x - y, 0.0)), 0.0) for x, y in zip(g_l, g_row_l)]
    kk_l = [_dot_nt(a, b) for a, b in zip(kb_l, k_l)]
    qk_l = [_dot_nt(a, b) for a, b in zip(qs_l, k_l)]
    low_l = [jnp.where(strict, a * d, 0.0) for a, d in zip(kk_l, dec_l)]
    eye_f = eye.astype(F32)
    pw_l = [-x for x in low_l]
    inv_l = [eye_f + x for x in pw_l]
    plain = lambda a_l, b_l: [_dot(a, b) for a, b in zip(a_l, b_l)]
    for level in range(int(math.log2(c)) - 1):
        mul = _hp_l if level < DN_INV_EXACT_LEVELS else plain
        pw_l = mul(pw_l, pw_l)
        inv_l = [a + b for a, b in zip(inv_l, mul(inv_l, pw_l))]
    u_l = [_dot(a, b) for a, b in zip(inv_l, cut(v * beta))]
    w_l = [_dot(a, b) for a, b in zip(inv_l, cut(rhs_k))]
    aqk_l = [jnp.where(lower, a * d, 0.0) for a, d in zip(qk_l, dec_l)]
    g_last_l = [jnp.sum(jnp.where(rowid == c - 1, x, 0.0), axis=0, keepdims=True) for x in g_l]
    ekd_l = [jnp.exp(a - b) for a, b in zip(g_last_l, g_l)]
    kd_l = [a * b for a, b in zip(k_l, ekd_l)]
    qd_l = cut(qs * eg)
    kw_l = [_dot_tn(a, b) for a, b in zip(kd_l, w_l)]
    qp_l = [q - _dot(a, w) for q, a, w in zip(qd_l, aqk_l, w_l)]
    return dict(eye=eye, lower=lower, strict=strict, dec=dec_l, k=k_l, kb=kb_l, qs=qs_l, low=low_l, inv=inv_l,
                eg=cut(eg), rhs_k=cut(rhs_k), u=u_l, w=w_l, aqk=aqk_l, g_last=g_last_l, qd=qd_l,
                ekd=ekd_l, kd=kd_l, kw=kw_l, qp=qp_l)


def _dn_head_cols(gb_blk, head):
    lane = lax.broadcasted_iota(jnp.int32, (1, LANES), 1)
    g = jnp.sum(jnp.where(lane == head, gb_blk, 0.0), axis=-1, keepdims=True)
    beta = jnp.sum(jnp.where(lane == head + DN_HEADS, gb_blk, 0.0), axis=-1, keepdims=True)
    return g, beta


def _halves_over_ici(s_refs, o_refs, send_sems, recv_sems, first, last):
    x, y, c = _mesh_pos()
    me = 2 * x + y
    chips = _other_chips(x, y)
    pairs = [(a, k) for a in range(len(s_refs)) for k in range(3)]

    def copy(a, k, slot):
        px, py = chips[k]
        return pltpu.make_async_remote_copy(
            src_ref=s_refs[a].at[c], dst_ref=o_refs[a].at[slot, c], send_sem=send_sems.at[3 * a + k],
            recv_sem=recv_sems.at[3 * a + k], device_id=(px, py, c), device_id_type=MESH)

    @pl.when(first)
    def _():
        for a, k in pairs:
            copy(a, k, me).start()

    @pl.when(last)
    def _():
        for a, k in pairs:
            px, py = chips[k]
            copy(a, k, 2 * px + py).wait_recv()
        for a, k in pairs:
            copy(a, k, me).wait_send()


def _dn_delta_fwd(qkv, gb, p, o_gain, *, name, send=()):
    T = qkv.shape[0]
    tb = min(DN_TB, T)
    nb, nc = T // tb, tb // DN_CHUNK
    H = DN_HEADS
    qscale = DN_DK ** -0.5
    ns = len(send)
    hp = DN_HEADS_PER_STEP

    def body(*refs):
        q_ref, k_ref, v_ref, gb_ref, gate_ref, gain_ref = refs[:6]
        o_ref, og_ref, st_ref = refs[6 + ns:9 + ns]
        s_ref = refs[9 + 2 * ns]
        pair, blk = pl.program_id(0), pl.program_id(1)
        if ns:
            _halves_over_ici(refs[6:6 + ns], refs[9 + ns:9 + 2 * ns], refs[10 + 2 * ns], refs[11 + 2 * ns],
                             jnp.logical_and(pair == 0, blk == 0),
                             jnp.logical_and(pair == H // hp - 1, blk == nb - 1))

        @pl.when(blk == 0)
        def _():
            s_ref[...] = jnp.zeros_like(s_ref)

        gbv = gb_ref[...]
        ts, ku, op = [], [], []
        for e in range(hp):
            qk_e, v_e = slice(e * DN_DK, (e + 1) * DN_DK), slice(e * DN_DV, (e + 1) * DN_DV)
            g, beta = _dn_head_cols(gbv, hp * pair + e)
            t = _dn_local(q_ref[:, qk_e] * qscale, k_ref[:, qk_e], v_ref[:, v_e], g, beta, nc)
            ts.append(t)
            ku.append([_dot_tn(a, b) for a, b in zip(t["kd"], t["u"])])
            op.append([_dot(a, b) for a, b in zip(t["aqk"], t["u"])])
        s32 = [s_ref[e] for e in range(hp)]
        s_l = [[] for _ in range(hp)]
        for i in range(nc):
            sb = [_bf(x) for x in s32]
            for e in range(hp):
                st_ref[e, i] = sb[e]
                s_l[e].append(sb[e])
            prod = [_dot(ts[e]["kw"][i], sb[e]) for e in range(hp)]
            s32 = [s32[e] * jnp.exp(ts[e]["g_last"][i]) - prod[e] + ku[e][i] for e in range(hp)]
        for e in range(hp):
            s_ref[e] = s32[e]
        o = jnp.concatenate(
            [jnp.concatenate([_dot(qp, sb) + x for qp, sb, x in zip(ts[e]["qp"], s_l[e], op[e])], axis=0)
             for e in range(hp)], axis=1)
        o_ref[...] = o
        gain = gain_ref[...]
        for e in range(hp):
            v_e = slice(e * DN_DV, (e + 1) * DN_DV)
            oe = o[:, v_e]
            r = lax.rsqrt(jnp.mean(oe * oe, axis=-1, keepdims=True) + RMS_EPS)
            og_ref[:, v_e] = (((oe * r) * gain) * _silu(gate_ref[:, v_e])).astype(BF16)

    qk = lambda col0: pl.BlockSpec((tb, hp * DN_DK), lambda h, i: (i, col0 // (hp * DN_DK) + h))
    vblk = lambda col0: pl.BlockSpec((tb, hp * DN_DV), lambda h, i: (i, col0 // (hp * DN_DV) + h))
    return pl.pallas_call(
        body, name=name, grid=(H // hp, nb),
        in_specs=[qk(0), qk(DN_QK_W), vblk(2 * DN_QK_W), pl.BlockSpec((tb, LANES), lambda h, i: (i, 0)),
                  vblk(DN_CONV_W), pl.BlockSpec((1, DN_DV), lambda h, i: (0, 0))] + [HBM] * ns,
        out_specs=[vblk(0), vblk(0), pl.BlockSpec((hp, nc, DN_DK, DN_DV), lambda h, i: (h, i, 0, 0))] + [HBM] * ns,
        out_shape=[jax.ShapeDtypeStruct((T, DN_V_W), F32), jax.ShapeDtypeStruct((T, DN_V_W), BF16),
                   jax.ShapeDtypeStruct((H, T // DN_CHUNK, DN_DK, DN_DV), BF16)]
        + [jax.ShapeDtypeStruct((N_CHIPS,) + a.shape, a.dtype) for a in send],
        scratch_shapes=[pltpu.VMEM((hp, DN_DK, DN_DV), F32)]
        + ([pltpu.SemaphoreType.DMA((3 * ns,)), pltpu.SemaphoreType.DMA((3 * ns,))] if ns else []),
        compiler_params=pltpu.CompilerParams(dimension_semantics=("arbitrary", "arbitrary")),
    )(qkv, qkv, qkv, gb, p, o_gain, *send)


def _blocks_over_ici(p_refs, o_refs, send_sems, recv_sems, first, last):
    x, y, c = _mesh_pos()
    me = 2 * x + y
    chips = _other_chips(x, y)
    pairs = [(a, k) for a in range(len(p_refs)) for k in range(3)]

    def copy(a, k, slot):
        px, py = chips[k]
        return pltpu.make_async_remote_copy(
            src_ref=p_refs[a].at[2 * px + py], dst_ref=o_refs[a].at[slot], send_sem=send_sems.at[3 * a + k],
            recv_sem=recv_sems.at[3 * a + k], device_id=(px, py, c), device_id_type=MESH)

    @pl.when(first)
    def _():
        for a, k in pairs:
            copy(a, k, me).start()

    @pl.when(last)
    def _():
        for a, k in pairs:
            px, py = chips[k]
            copy(a, k, 2 * px + py).wait_recv()
        for a, k in pairs:
            copy(a, k, me).wait_send()


def _dn_delta_bwd(qkv, gb, p, o_gain, o, states, dog, *, name, send=()):
    T = qkv.shape[0]
    tb = min(DN_TB, T)
    nb, nc = T // tb, tb // DN_CHUNK
    H = DN_HEADS
    qscale = DN_DK ** -0.5
    ns = len(send)
    hp = DN_HEADS_PER_STEP

    def body(*refs):
        q_ref, k_ref, v_ref, gb_ref, gate_ref, gain_ref, o_ref, st_ref, dog_ref = refs[:9]
        dq_ref, dk_ref, dv_ref, dgate_ref, dgb_ref, dgain_ref = refs[9 + ns:15 + ns]
        ds_ref = refs[15 + 2 * ns]
        pair, blk = pl.program_id(0), pl.program_id(1)
        first = jnp.logical_and(pair == 0, blk == 0)
        if ns:
            _blocks_over_ici(refs[9:9 + ns], refs[15 + ns:15 + 2 * ns], refs[16 + 2 * ns], refs[17 + 2 * ns],
                             first, jnp.logical_and(pair == H // hp - 1, blk == nb - 1))

        @pl.when(blk == 0)
        def _():
            ds_ref[...] = jnp.zeros_like(ds_ref)

        @pl.when(first)
        def _():
            dgain_ref[...] = jnp.zeros_like(dgain_ref)

        lane = lax.broadcasted_iota(jnp.int32, (1, LANES), 1)
        c = DN_CHUNK
        cut = lambda x: [x[i * c:(i + 1) * c] for i in range(nc)]
        cat = lambda xs: jnp.concatenate(xs, axis=0)
        rsum = lambda x: jnp.sum(x, axis=-1, keepdims=True)
        gbv, gain = gb_ref[...], gain_ref[...]

        def before_chain(e):
            qk_e, v_e = slice(e * DN_DK, (e + 1) * DN_DK), slice(e * DN_DV, (e + 1) * DN_DV)
            g, beta = _dn_head_cols(gbv, hp * pair + e)
            ov, gate, dogv = o_ref[:, v_e], gate_ref[:, v_e], dog_ref[:, v_e]
            r = lax.rsqrt(jnp.mean(ov * ov, axis=-1, keepdims=True) + RMS_EPS)
            oh = ov * r
            dnrm = dogv * _silu(gate)
            dgate_ref[:, v_e] = dogv * (oh * gain) * _dsilu(gate)
            doh = dnrm * gain
            do_l = cut(r * (doh - oh * jnp.mean(doh * oh, axis=-1, keepdims=True)))
            dgain_ref[...] += jnp.sum(dnrm * oh, axis=0, keepdims=True)
            k, v = k_ref[:, qk_e], v_ref[:, v_e]
            t = _dn_local(q_ref[:, qk_e] * qscale, k, v, g, beta, nc)
            s_l = [st_ref[e, i] for i in range(nc)]
            vn_l = [u - _dot(w, sb) for u, w, sb in zip(t["u"], t["w"], s_l)]
            return dict(
                t=t, beta=beta, v=v, s=s_l, vn=vn_l, egl=[jnp.exp(x) for x in t["g_last"]],
                dqd=[_dot_nt(a, sb) for a, sb in zip(do_l, s_l)], daqk=[_dot_nt(a, b) for a, b in zip(do_l, vn_l)],
                aqk_do=[_dot_tn(a, b) for a, b in zip(t["aqk"], do_l)],
                qp_do=[_dot_tn(a, b) for a, b in zip(t["qp"], do_l)])

        hs = [before_chain(e) for e in range(hp)]
        ds = [ds_ref[e] for e in range(hp)]
        ds_l = [[None] * nc for _ in range(hp)]
        for i in reversed(range(nc)):
            for e in range(hp):
                ds_l[e][i] = ds[e]
            prod = [_dot_tn(hs[e]["t"]["kw"][i], ds[e]) for e in range(hp)]
            ds = [ds[e] * hs[e]["egl"][i] - prod[e] + hs[e]["qp_do"][i] for e in range(hp)]
        for e in range(hp):
            ds_ref[e] = ds[e]

        def after_chain(e):
            hd, t = hs[e], hs[e]["t"]
            lower, strict, eye = t["lower"], t["strict"], t["eye"]
            s_l, vn_l, dqd_l, daqk_l, egl_l, beta, v = (hd["s"], hd["vn"], hd["dqd"], hd["daqk"], hd["egl"],
                                                         hd["beta"], hd["v"])
            dvn_l = [a + _dot(kd, d) for a, kd, d in zip(hd["aqk_do"], t["kd"], ds_l[e])]
            dkd_l = [_dot_nt(a, d) for a, d in zip(vn_l, ds_l[e])]
            dgl_l = [jnp.sum(rsum(d * sb.astype(F32)), axis=0, keepdims=True) * x
                     for d, sb, x in zip(ds_l[e], s_l, egl_l)]
            dw_l = [-_dot_nt(a, sb) for a, sb in zip(dvn_l, s_l)]
            dbv_l = [_dot_tn(a, b) for a, b in zip(t["inv"], dvn_l)]
            dbk_l = [_dot_tn(a, b) for a, b in zip(t["inv"], dw_l)]
            dlow_l = [-(_dot_nt(a, b) + _dot_nt(x, y)) for a, b, x, y in zip(dbv_l, t["u"], dbk_l, t["w"])]
            m_l = [jnp.where(strict, a * d, 0.0) for a, d in zip(dlow_l, t["dec"])]
            nmat_l = [jnp.where(lower, a * d, 0.0) for a, d in zip(daqk_l, t["dec"])]
            dkb_l = [_dot(m, kk) + b * x for m, kk, b, x in zip(m_l, t["k"], dbk_l, t["eg"])]
            dqs_l = [_dot(n, kk) + a * x for n, kk, a, x in zip(nmat_l, t["k"], dqd_l, t["eg"])]
            dk1_l = [_dot_tn(m, kb) for m, kb in zip(m_l, t["kb"])]
            dk2_l = [_dot_tn(n, q) for n, q in zip(nmat_l, t["qs"])]
            beta_l, v_l = cut(beta), cut(v)
            rowid = lax.broadcasted_iota(jnp.int32, (c, 1), 0)
            dk_l, dg_l, dbeta_l = [], [], []
            for i in range(nc):
                dk_l.append(dk1_l[i] + dk2_l[i] + dkd_l[i] * t["ekd"][i] + dkb_l[i] * beta_l[i])
                gmat = jnp.where(strict, dlow_l[i] * t["low"][i], 0.0) + daqk_l[i] * t["aqk"][i]
                s_kd = rsum(dkd_l[i] * t["kd"][i])
                dg = (rsum(gmat) + rsum(dqd_l[i] * t["qd"][i]) - s_kd + rsum(dbk_l[i] * t["rhs_k"][i]))
                dg_row = -jnp.sum(gmat, axis=0, keepdims=True)
                dg = dg + rsum(jnp.where(eye, dg_row, 0.0))
                dgl = dgl_l[i] + jnp.sum(s_kd, axis=0, keepdims=True)
                dg_l.append(dg + jnp.where(rowid == c - 1, dgl, 0.0))
                dbeta_l.append(rsum(dbv_l[i] * v_l[i]) + rsum(dkb_l[i] * t["k"][i]))
            head = hp * pair + e
            dgb = (jnp.where(lane == head, cat(dg_l), 0.0) + jnp.where(lane == head + DN_HEADS, cat(dbeta_l), 0.0))
            return cat(dqs_l) * qscale, cat(dk_l), cat(dbv_l) * beta, dgb

        for e in range(hp):
            dq, dk, dv, dgb = after_chain(e)
            dq_ref[:, e * DN_DK:(e + 1) * DN_DK] = dq
            dk_ref[:, e * DN_DK:(e + 1) * DN_DK] = dk
            dv_ref[:, e * DN_DV:(e + 1) * DN_DV] = dv
            dgb_ref[e] = dgb

    rev = lambda i: nb - 1 - i
    qk = lambda col0: pl.BlockSpec((tb, hp * DN_DK), lambda h, i: (rev(i), col0 // (hp * DN_DK) + h))
    vblk = lambda col0: pl.BlockSpec((tb, hp * DN_DV), lambda h, i: (rev(i), col0 // (hp * DN_DV) + h))
    gain_spec = pl.BlockSpec((1, DN_DV), lambda h, i: (0, 0))
    return pl.pallas_call(
        body, name=name, grid=(H // hp, nb),
        in_specs=[qk(0), qk(DN_QK_W), vblk(2 * DN_QK_W), pl.BlockSpec((tb, LANES), lambda h, i: (rev(i), 0)),
                  vblk(DN_CONV_W), gain_spec, vblk(0),
                  pl.BlockSpec((hp, nc, DN_DK, DN_DV), lambda h, i: (h, rev(i), 0, 0)), vblk(0)] + [HBM] * ns,
        out_specs=[qk(0), qk(0), vblk(0), vblk(DN_CONV_W),
                   pl.BlockSpec((hp, tb, LANES), lambda h, i: (h, rev(i), 0)), gain_spec] + [HBM] * ns,
        out_shape=[jax.ShapeDtypeStruct((T, DN_QK_W), F32), jax.ShapeDtypeStruct((T, DN_QK_W), F32),
                   jax.ShapeDtypeStruct((T, DN_V_W), F32), jax.ShapeDtypeStruct((T, DN_IN_PAD), F32),
                   jax.ShapeDtypeStruct((H, T, LANES), F32), jax.ShapeDtypeStruct((1, DN_DV), F32)]
        + [jax.ShapeDtypeStruct(a.shape, a.dtype) for a in send],
        scratch_shapes=[pltpu.VMEM((hp, DN_DK, DN_DV), F32)]
        + ([pltpu.SemaphoreType.DMA((3 * ns,)), pltpu.SemaphoreType.DMA((3 * ns,))] if ns else []),
        compiler_params=pltpu.CompilerParams(dimension_semantics=("arbitrary", "arbitrary")),
    )(qkv, qkv, qkv, gb, p, o_gain, o, states, dog, *send)


def _dn_conv_bwd(p, conv_w, d, dp, *, first, normed, name):
    T, width = d.shape

    def body(p_ref, w_ref, d_ref, dp_in, dp_ref, dw_ref):
        del dp_in
        x = p_ref[...]
        ksz = w_ref.shape[0]
        xs = [_shift_down(x, ksz - 1 - i) for i in range(ksz)]
        xc = sum(w_ref[i:i + 1, :] * xs[i] for i in range(ksz))
        ds = d_ref[...]
        if normed:
            s = _silu(xc)
            r = lax.rsqrt(jnp.sum(s * s, axis=-1, keepdims=True) + L2_EPS)
            y = s * r
            ds = r * (ds - y * jnp.sum(ds * y, axis=-1, keepdims=True))
        dxc = ds * _dsilu(xc)
        dp_ref[...] = sum(w_ref[i:i + 1, :] * _shift_up(dxc, ksz - 1 - i) for i in range(ksz))
        for i in range(ksz):
            dw_ref[i:i + 1, :] = jnp.sum(dxc * xs[i], axis=0, keepdims=True)

    shifted = pl.BlockSpec((T, LANES), lambda j: (0, first + j))
    return pl.pallas_call(
        body, name=name, grid=(width // LANES,),
        in_specs=[shifted, pl.BlockSpec((DN_CONV, LANES), lambda j: (0, first + j)),
                  pl.BlockSpec((T, LANES), lambda j: (0, j)), pl.BlockSpec(memory_space=pl.ANY)],
        out_specs=[shifted, pl.BlockSpec((DN_CONV, LANES), lambda j: (0, j))],
        out_shape=[jax.ShapeDtypeStruct(dp.shape, F32), jax.ShapeDtypeStruct((DN_CONV, width), F32)],
        input_output_aliases={3: 0},
        compiler_params=pltpu.CompilerParams(dimension_semantics=("parallel",), vmem_limit_bytes=VMEM_BIG),
    )(p, conv_w, d, dp)


def _dn_ab_bwd(p, alog_row, dtb_row, dgb, dp, *, name):
    T = p.shape[0]
    rows = min(DN_TB, T)
    H = DN_HEADS

    def body(p_ref, al_ref, dt_ref, dgb_ref, dp_in, dp_ref, dal_ref, ddt_ref):
        del dp_in

        @pl.when(pl.program_id(0) == 0)
        def _():
            dal_ref[...] = jnp.zeros_like(dal_ref)
            ddt_ref[...] = jnp.zeros_like(ddt_ref)

        blk = p_ref[...]
        is_a, is_b, a_arg, neg_exp, log_a, beta = _dn_ab_parts(blk, al_ref[...], dt_ref[...])
        d = dgb_ref[0]
        for hh in range(1, H):
            d = d + dgb_ref[hh]
        hi, mid, lo_ = _split3(jnp.where(is_a, d, 0.0))
        tri = _dn_chunk_tri(rows, upper=True)
        f = lambda q: jnp.dot(tri, q, preferred_element_type=F32)
        dlog_a = f(hi) + f(mid) + f(lo_)
        da_in = dlog_a * neg_exp * _sigmoid(a_arg)
        db_in = jnp.where(is_b, d, 0.0) * beta * (1.0 - beta)
        dp_ref[...] = jnp.where(is_a, da_in, 0.0) + db_in
        dal_ref[...] += jnp.sum(dlog_a * log_a, axis=0, keepdims=True)
        ddt_ref[...] += jnp.sum(jnp.where(is_a, da_in, 0.0), axis=0, keepdims=True)

    blk = pl.BlockSpec((rows, LANES), lambda i: (i, DN_AB_COL))
    vec = pl.BlockSpec((1, LANES), lambda i: (0, 0))
    return pl.pallas_call(
        body, name=name, grid=(T // rows,),
        in_specs=[blk, vec, vec, pl.BlockSpec((H, rows, LANES), lambda i: (0, i, 0)),
                  pl.BlockSpec(memory_space=pl.ANY)],
        out_specs=[blk, vec, vec],
        out_shape=[jax.ShapeDtypeStruct(dp.shape, F32), jax.ShapeDtypeStruct((1, LANES), F32),
                   jax.ShapeDtypeStruct((1, LANES), F32)],
        input_output_aliases={4: 0},
        compiler_params=pltpu.CompilerParams(dimension_semantics=("arbitrary",)),
    )(p, alog_row, dtb_row, dgb, dp)


def _dn_layer_fwd(x, ng, w_in, conv_w, a_log, dt_bias, o_gain, w_out, tag, send=()):
    alog_row, dtb_row = _dn_lane_rows(a_log, dt_bias)
    gain = o_gain.reshape(1, DN_DV)
    h = _rmsnorm_fwd(x, ng, name=f"{tag}_norm")
    p = _matmul(h, w_in, mode="nn", name=f"{tag}_inproj")
    qkv = _dn_prep_fwd(p, conv_w, name=f"{tag}_prep")
    gb = _dn_ab_fwd(p, alog_row, dtb_row, name=f"{tag}_ab")
    o, og, states, *landed = _dn_delta_fwd(qkv, gb, p, gain, name=f"{tag}_delta", send=send)
    x_new = _matmul(og, w_out, mode="nn", res=x, name=f"{tag}_outproj")
    return x_new, (h, p, qkv, gb, o, og, states), landed


def _dn_layer_bwd(dx, x, ng, w_in, conv_w, a_log, dt_bias, o_gain, w_out, saved, tag, send=()):
    h, p, qkv, gb, o, og, states = saved
    alog_row, dtb_row = _dn_lane_rows(a_log, dt_bias)
    gain = o_gain.reshape(1, DN_DV)
    d_wout = _matmul(og, dx, mode="tn", out_dtype=BF16, name=f"{tag}_dwout")
    dog = _matmul(dx, w_out, mode="nt", name=f"{tag}_dog")
    dq, dk, dv, dp, dgb, dgain, *landed = _dn_delta_bwd(qkv, gb, p, gain, o, states, dog, name=f"{tag}_deltabwd",
                                                        send=send)
    n_qk = DN_QK_W // LANES
    dp, dconv_q = _dn_conv_bwd(p, conv_w, dq, dp, first=0, normed=True, name=f"{tag}_convbwd_q")
    dp, dconv_k = _dn_conv_bwd(p, conv_w, dk, dp, first=n_qk, normed=True, name=f"{tag}_convbwd_k")
    dp, dconv_v = _dn_conv_bwd(p, conv_w, dv, dp, first=2 * n_qk, normed=False, name=f"{tag}_convbwd_v")
    dconv = jnp.concatenate([dconv_q, dconv_k, dconv_v], axis=1)
    dp, dal, ddt = _dn_ab_bwd(p, alog_row, dtb_row, dgb, dp, name=f"{tag}_abbwd")
    d_win = _matmul(h, dp, mode="tn", name=f"{tag}_dwin")
    dh = _matmul(dp, w_in, mode="nt", name=f"{tag}_dh")
    dx_prev, dng = _rmsnorm_bwd(x, ng, dh, dx, name=f"{tag}_normbwd")
    return dx_prev, dng, d_win, dconv, dal[0, :DN_HEADS], ddt[0, :DN_HEADS], dgain[0], d_wout, landed


def _sb_gains(g):
    return jnp.concatenate([g, g]).reshape(1, LANES)


def _sb_layer_fwd(x, ng, w_in, gq, gk, w_out, tag, send=()):
    h = _rmsnorm_fwd(x, ng, name=f"{tag}_norm")
    p3 = _matmul(h, w_in, mode="nn", b_parts=4, out_parts=4, name=f"{tag}_inproj")
    og, o, *landed = _sb_attn_fwd(p3, _sb_gains(gq), _sb_gains(gk), name=f"{tag}_attn", send=send)
    x_new = _matmul(og, w_out, mode="nn", res=x, name=f"{tag}_outproj")
    return x_new, (h, p3, og, o), landed


def _sb_layer_bwd(dx, x, ng, w_in, gq, gk, w_out, saved, tag):
    h, p3, og, o = saved
    d_wout = _matmul(og, dx, mode="tn", out_dtype=BF16, name=f"{tag}_dwout")
    dog = _matmul(dx, w_out, mode="nt", name=f"{tag}_dog")
    dp3, dgq, dgk = _sb_attn_bwd(p3, _sb_gains(gq), _sb_gains(gk), o, dog, name=f"{tag}_attnbwd")
    fold = lambda d: jnp.sum(d.reshape(-1, SB_DH), axis=0)
    d_win = _matmul(h, dp3, mode="tn", b_parts=4, out_parts=4, out_dtype=BF16, name=f"{tag}_dwin")
    dh = _matmul(dp3, w_in, mode="nt", a_parts=4, b_parts=4, name=f"{tag}_dh")
    dx_prev, dng = _rmsnorm_bwd(x, ng, dh, dx, name=f"{tag}_normbwd")
    return dx_prev, dng, d_win, fold(dgq), fold(dgk), d_wout


N_CHIPS = 4
HBM = pl.BlockSpec(memory_space=pl.ANY)


def _mesh_pos():
    return lax.axis_index("x"), lax.axis_index("y"), lax.axis_index("c")


def _other_chips(x, y):
    return [(1 - x, y), (x, 1 - y), (1 - x, 1 - y)]


def _chip_exchange(srcs, *, send_slot_is_dest, copy_own, name):
    n = len(srcs)

    def body(*refs):
        src_refs, out_refs = refs[:n], refs[n:2 * n]
        send_sems, recv_sems, local_sems = refs[2 * n:]
        x, y, c = _mesh_pos()
        me = 2 * x + y
        chips = _other_chips(x, y)
        local = []
        for a in range(n):
            if not copy_own[a]:
                continue
            own = src_refs[a].at[me] if send_slot_is_dest else src_refs[a]
            local.append(pltpu.make_async_copy(own, out_refs[a].at[me], local_sems.at[a]))
        for cp in local:
            cp.start()

        def copy(a, k, landing_slot):
            px, py = chips[k]
            src = src_refs[a].at[2 * px + py] if send_slot_is_dest else src_refs[a]
            return pltpu.make_async_remote_copy(
                src_ref=src, dst_ref=out_refs[a].at[landing_slot],
                send_sem=send_sems.at[a * 3 + k], recv_sem=recv_sems.at[a * 3 + k],
                device_id=(px, py, c), device_id_type=MESH)

        sends = [copy(a, k, me) for a in range(n) for k in range(3)]
        for cp in sends:
            cp.start()
        for a in range(n):
            for k in range(3):
                px, py = chips[k]
                copy(a, k, 2 * px + py).wait_recv()
        for cp in sends:
            cp.wait_send()
        for cp in local:
            cp.wait()

    outs = []
    for s in srcs:
        shape = s.shape if send_slot_is_dest else (N_CHIPS,) + s.shape
        outs.append(jax.ShapeDtypeStruct(shape, s.dtype))
    return pl.pallas_call(
        body, name=name, in_specs=[HBM] * n, out_specs=[HBM] * n, out_shape=outs,
        scratch_shapes=[pltpu.SemaphoreType.DMA((3 * n,)), pltpu.SemaphoreType.DMA((3 * n,)),
                        pltpu.SemaphoreType.DMA((n,))],
    )(*srcs)


def _sibling_exchange(srcs, *, name):
    n = len(srcs)

    def body(*refs):
        src_refs, out_refs = refs[:n], refs[n:2 * n]
        send_sems, recv_sems = refs[2 * n:]
        x, y, c = _mesh_pos()
        copies = [pltpu.make_async_remote_copy(
            src_ref=src_refs[a], dst_ref=out_refs[a], send_sem=send_sems.at[a], recv_sem=recv_sems.at[a],
            device_id=(x, y, 1 - c), device_id_type=MESH) for a in range(n)]
        for cp in copies:
            cp.start()
        for cp in copies:
            cp.wait()

    return pl.pallas_call(
        body, name=name, in_specs=[HBM] * n, out_specs=[HBM] * n,
        out_shape=[jax.ShapeDtypeStruct(s.shape, s.dtype) for s in srcs],
        scratch_shapes=[pltpu.SemaphoreType.DMA((n,)), pltpu.SemaphoreType.DMA((n,))],
    )(*srcs)


def _gather_halves(shards, small, *, name):
    n = len(shards)

    def body(*refs):
        s_refs, small_ref = refs[:n], refs[n]
        o_refs, osmall_ref = refs[n + 1:2 * n + 1], refs[2 * n + 1]
        send_sems, recv_sems, local_sems = refs[2 * n + 2:]
        x, y, c = _mesh_pos()
        me = 2 * x + y
        chips = _other_chips(x, y)
        local = [pltpu.make_async_copy(small_ref, osmall_ref.at[me], local_sems.at[0])]
        for cp in local:
            cp.start()

        def over_ici(a, k, slot):
            px, py = chips[k]
            return pltpu.make_async_remote_copy(
                src_ref=s_refs[a].at[c], dst_ref=o_refs[a].at[slot, c], send_sem=send_sems.at[3 * a + k],
                recv_sem=recv_sems.at[3 * a + k], device_id=(px, py, c), device_id_type=MESH)

        def small_copy(k, slot):
            px, py = chips[k]
            return pltpu.make_async_remote_copy(
                src_ref=small_ref, dst_ref=osmall_ref.at[slot], send_sem=send_sems.at[3 * n + k],
                recv_sem=recv_sems.at[3 * n + k], device_id=(px, py, c), device_id_type=MESH)

        def to_sibling(a, k, half):
            px, py = chips[k]
            blk = o_refs[a].at[2 * px + py, half]
            return pltpu.make_async_remote_copy(
                src_ref=blk, dst_ref=blk, send_sem=send_sems.at[3 * n + 3 + 3 * a + k],
                recv_sem=recv_sems.at[3 * n + 3 + 3 * a + k], device_id=(x, y, 1 - c), device_id_type=MESH)

        sends = [over_ici(a, k, me) for a in range(n) for k in range(3)] + [small_copy(k, me) for k in range(3)]
        for cp in sends:
            cp.start()
        passed = []
        for a in range(n):
            for k in range(3):
                px, py = chips[k]
                over_ici(a, k, 2 * px + py).wait_recv()
                passed.append(to_sibling(a, k, c))
                passed[-1].start()
        for k in range(3):
            px, py = chips[k]
            small_copy(k, 2 * px + py).wait_recv()
        for a in range(n):
            for k in range(3):
                to_sibling(a, k, 1 - c).wait_recv()
        for cp in sends + passed:
            cp.wait_send()
        for cp in local:
            cp.wait()

    nsem = 6 * n + 3
    return pl.pallas_call(
        body, name=name, in_specs=[HBM] * (n + 1), out_specs=[HBM] * (n + 1),
        out_shape=[jax.ShapeDtypeStruct((N_CHIPS,) + s.shape, s.dtype) for s in shards + [small]],
        scratch_shapes=[pltpu.SemaphoreType.DMA((nsem,)), pltpu.SemaphoreType.DMA((nsem,)),
                        pltpu.SemaphoreType.DMA((1,))],
    )(*shards, small)


def _forward_halves(landed, *, name):
    n = len(landed)

    def body(*refs):
        o_refs = refs[n:2 * n]
        send_sems, recv_sems = refs[2 * n:]
        x, y, c = _mesh_pos()
        chips = _other_chips(x, y)
        pairs = [(a, k) for a in range(n) for k in range(3)]

        def copy(a, k, half):
            px, py = chips[k]
            blk = o_refs[a].at[2 * px + py, half]
            return pltpu.make_async_remote_copy(
                src_ref=blk, dst_ref=blk, send_sem=send_sems.at[3 * a + k], recv_sem=recv_sems.at[3 * a + k],
                device_id=(x, y, 1 - c), device_id_type=MESH)

        sends = [copy(a, k, c) for a, k in pairs]
        for cp in sends:
            cp.start()
        for a, k in pairs:
            copy(a, k, 1 - c).wait_recv()
        for cp in sends:
            cp.wait_send()

    return pl.pallas_call(
        body, name=name, in_specs=[HBM] * n, out_specs=[HBM] * n,
        out_shape=[jax.ShapeDtypeStruct(a.shape, a.dtype) for a in landed],
        input_output_aliases={a: a for a in range(n)},
        scratch_shapes=[pltpu.SemaphoreType.DMA((3 * n,)), pltpu.SemaphoreType.DMA((3 * n,))],
    )(*landed)


def _swap_other_half(g_list, *, name):
    n = len(g_list)

    def body(*refs):
        g_refs, o_refs = refs[:n], refs[n:2 * n]
        send_sems, recv_sems = refs[2 * n:]
        x, y, c = _mesh_pos()
        copies = [pltpu.make_async_remote_copy(
            src_ref=g_refs[a].at[:, 1 - c], dst_ref=o_refs[a], send_sem=send_sems.at[a], recv_sem=recv_sems.at[a],
            device_id=(x, y, 1 - c), device_id_type=MESH) for a in range(n)]
        for cp in copies:
            cp.start()
        for cp in copies:
            cp.wait()

    return pl.pallas_call(
        body, name=name, in_specs=[HBM] * n, out_specs=[HBM] * n,
        out_shape=[jax.ShapeDtypeStruct((g.shape[0],) + g.shape[2:], g.dtype) for g in g_list],
        scratch_shapes=[pltpu.SemaphoreType.DMA((n,)), pltpu.SemaphoreType.DMA((n,))],
    )(*g_list)


def _row_tile(r):
    return _pick(r, (512, 256, 128, 64, 32, 16, 8))


def _add_my_half(g4, sib4, core, *, name):
    n, _, r, C = g4.shape
    tr = _row_tile(r)

    def body(core_ref, g_ref, s_ref, o_ref):
        del core_ref
        o_ref[...] = (g_ref[...].astype(F32) + s_ref[...].astype(F32)).astype(o_ref.dtype)

    return pl.pallas_call(
        body, name=name,
        grid_spec=pltpu.PrefetchScalarGridSpec(
            num_scalar_prefetch=1, grid=(n, r // tr),
            in_specs=[pl.BlockSpec((None, None, tr, C), lambda j, i, core_ref: (j, core_ref[0], i, 0)),
                      pl.BlockSpec((None, tr, C), lambda j, i, core_ref: (j, i, 0))],
            out_specs=pl.BlockSpec((None, tr, C), lambda j, i, core_ref: (j, i, 0))),
        out_shape=jax.ShapeDtypeStruct((n, r, C), g4.dtype),
        compiler_params=pltpu.CompilerParams(dimension_semantics=("parallel", "parallel")),
    )(core, g4, sib4)


def _scatter_to_chips(p_list, *, name):
    n = len(p_list)

    def body(*refs):
        p_refs, o_refs = refs[:n], refs[n:2 * n]
        send_sems, recv_sems = refs[2 * n:]
        x, y, c = _mesh_pos()
        me = 2 * x + y
        chips = _other_chips(x, y)
        pairs = [(a, k) for a in range(n) for k in range(3)]

        def copy(a, k, landing_slot):
            px, py = chips[k]
            return pltpu.make_async_remote_copy(
                src_ref=p_refs[a].at[2 * px + py], dst_ref=o_refs[a].at[landing_slot],
                send_sem=send_sems.at[3 * a + k], recv_sem=recv_sems.at[3 * a + k], device_id=(px, py, c),
                device_id_type=MESH)

        sends = [copy(a, k, me) for a, k in pairs]
        for cp in sends:
            cp.start()
        for a, k in pairs:
            px, py = chips[k]
            copy(a, k, 2 * px + py).wait_recv()
        for cp in sends:
            cp.wait_send()

    return pl.pallas_call(
        body, name=name, in_specs=[HBM] * n, out_specs=[HBM] * n,
        out_shape=[jax.ShapeDtypeStruct(p.shape, p.dtype) for p in p_list],
        scratch_shapes=[pltpu.SemaphoreType.DMA((3 * n,)), pltpu.SemaphoreType.DMA((3 * n,))],
    )(*p_list)


def _sum_chips(landed, part, me, *, name):
    _, r, C = landed.shape
    tr = _row_tile(r)

    def body(me_ref, own_ref, r1_ref, r2_ref, r3_ref, o_ref):
        del me_ref
        f = lambda ref: ref[...].astype(F32)
        o_ref[...] = ((f(own_ref) + f(r1_ref)) + f(r2_ref)) + f(r3_ref)

    slot = lambda d: pl.BlockSpec((None, tr, C), lambda i, me_ref: ((me_ref[0] + d) % N_CHIPS, i, 0))
    return pl.pallas_call(
        body, name=name,
        grid_spec=pltpu.PrefetchScalarGridSpec(
            num_scalar_prefetch=1, grid=(r // tr,), in_specs=[slot(0), slot(1), slot(2), slot(3)],
            out_specs=pl.BlockSpec((tr, C), lambda i, me_ref: (i, 0))),
        out_shape=jax.ShapeDtypeStruct((r, C), F32),
        compiler_params=pltpu.CompilerParams(dimension_semantics=("parallel",)),
    )(me, part, landed, landed, landed)


def _adamw_halves(w, mine, theirs, m, v, core, *, layer, prev, name):
    shape = w.shape
    r, C = mine.shape
    tr = _pick(r, (128, 64, 32, 16, 8))
    per = r // tr
    view = lambda a: a.reshape(-1, C)
    n_prev = 0 if prev is None else 4

    def body(*refs):
        core_ref, w_ref, gm_ref, gt_ref, m_ref, v_ref = refs[:6]
        g_ref, d_ref, nm_ref, nv_ref = refs[6 + n_prev:]
        gv = jnp.where(pl.program_id(0) == core_ref[0], gm_ref[...], gt_ref[...])
        g_ref[...] = gv
        d_ref[...], nm_ref[...], nv_ref[...] = _adamw_math(w_ref[...], gv, m_ref[...], v_ref[...])

    half = pl.BlockSpec((tr, C), lambda h, i, core_ref: ((2 * layer + h) * per + i, 0))
    row = pl.BlockSpec((tr, C), lambda h, i, core_ref: (i, 0))
    out = jax.ShapeDtypeStruct((math.prod(shape) // C, C), F32)
    res = pl.pallas_call(
        body, name=name,
        grid_spec=pltpu.PrefetchScalarGridSpec(
            num_scalar_prefetch=1, grid=(2, per), in_specs=[half, row, row, half, half] + [HBM] * n_prev,
            out_specs=[half] * 4),
        out_shape=[out] * 4,
        input_output_aliases={6 + j: j for j in range(n_prev)},
        compiler_params=pltpu.CompilerParams(dimension_semantics=("parallel", "parallel")),
    )(core, view(w), mine, theirs, view(m), view(v), *([] if prev is None else [view(a) for a in prev]))
    return tuple(a.reshape(shape) for a in res)


def _sum_small(recv4, *, name):
    _, R, C = recv4.shape

    def body(r_ref, o_ref):
        o_ref[...] = ((r_ref[0] + r_ref[1]) + r_ref[2]) + r_ref[3]

    return pl.pallas_call(body, name=name, out_shape=jax.ShapeDtypeStruct((R, C), F32))(recv4)


def _add(a, b, *, name):
    R, C = a.shape
    tr = _pick(R, (512, 256, 128, 64, 32, 16, 8))
    blk = pl.BlockSpec((tr, C), lambda i: (i, 0))

    def body(a_ref, b_ref, o_ref):
        o_ref[...] = a_ref[...] + b_ref[...]

    return pl.pallas_call(body, name=name, grid=(R // tr,), in_specs=[blk, blk], out_specs=blk,
                          out_shape=jax.ShapeDtypeStruct((R, C), F32),
                          compiler_params=pltpu.CompilerParams(dimension_semantics=("parallel",)))(a, b)


def _adamw_math(w, g, m, v):
    nm = ADAM_B1 * m + (1.0 - ADAM_B1) * g
    nv = ADAM_B2 * v + (1.0 - ADAM_B2) * (g * g)
    m_hat = nm / (1.0 - ADAM_B1 ** ADAM_STEP)
    v_hat = nv / (1.0 - ADAM_B2 ** ADAM_STEP)
    return -ADAM_LR * (m_hat / (jnp.sqrt(v_hat) + ADAM_EPS) + ADAM_WD * w), nm, nv


def _adamw(w, g, m, v, *, name):
    shape = w.shape
    C = shape[-1]
    R = w.size // C
    two = lambda a: a.reshape(R, C)
    tr = _pick(R, (256, 128, 64, 32, 16, 8)) if R % 8 == 0 and R > 8 else R
    blk = pl.BlockSpec((tr, C), lambda i: (i, 0))

    def body(w_ref, g_ref, m_ref, v_ref, d_ref, nm_ref, nv_ref):
        d_ref[...], nm_ref[...], nv_ref[...] = _adamw_math(w_ref[...], g_ref[...], m_ref[...], v_ref[...])

    out = jax.ShapeDtypeStruct((R, C), F32)
    d, nm, nv = pl.pallas_call(
        body, name=name, grid=(R // tr,), in_specs=[blk] * 4, out_specs=[blk] * 3, out_shape=[out] * 3,
        compiler_params=pltpu.CompilerParams(dimension_semantics=("parallel",)),
    )(two(w), two(g), two(m), two(v))
    return d.reshape(shape), nm.reshape(shape), nv.reshape(shape)


BIG = (("dn_w_in", (2, 1024, 1540), 2), ("dn_w_out", (2, 512, 1024), 1), ("sb_w_in", (1, 1024, 1024), 2),
       ("sb_w_out", (1, 256, 1024), 1), ("sc_w_in", (1, 1024, 2048), 2), ("sc_w_out", (1, 512, 1024), 1))
SMALL = (("dn_conv_w", (2, 4, 1024), 2), ("dn_o_norm_g", (2, 64), 1), ("sc_conv_w", (1, 3, 512), 2))
REPL = (("norm_g", (4, 1024)), ("dn_a_log", (2, 8)), ("dn_dt_bias", (2, 8)), ("sb_q_norm_g", (1, 64)),
        ("sb_k_norm_g", (1, 64)))


def _halves(shard):
    return shard.reshape(2, -1, shard.shape[-1])


def _pack(arrays, cols, lead=()):
    flat = jnp.concatenate([a.reshape(lead + (-1,)) for a in arrays], axis=-1)
    n = flat.shape[-1]
    rows = -(-n // cols)
    unit = 512 if rows > 512 else 8
    rows = -(-rows // unit) * unit
    flat = jnp.pad(flat, [(0, 0)] * len(lead) + [(0, rows * cols - n)])
    return flat.reshape(lead + (rows, cols))


def _unpack(buf, table, lead=()):
    flat = buf.reshape(lead + (-1,))
    out, off = {}, 0
    for entry in table:
        name, shape = entry[0], entry[1]
        n = math.prod(shape)
        out[name] = flat[..., off:off + n].reshape(lead + shape)
        off += n
    return out


def _join(shards, axis):
    return jnp.concatenate([shards[j] for j in range(N_CHIPS)], axis=axis)


def _split(full, axis):
    return jnp.stack(jnp.split(full, N_CHIPS, axis=axis), axis=0)


def kernel(x, norm_g, dn_w_in, dn_conv_w, dn_a_log, dn_dt_bias, dn_o_norm_g, dn_w_out, sb_w_in, sb_q_norm_g, sb_k_norm_g, sb_w_out, sc_w_in, sc_conv_w, sc_w_out, loss_target, m_norm_g, m_dn_w_in, m_dn_conv_w, m_dn_a_log, m_dn_dt_bias, m_dn_o_norm_g, m_dn_w_out, m_sb_w_in, m_sb_q_norm_g, m_sb_k_norm_g, m_sb_w_out, m_sc_w_in, m_sc_conv_w, m_sc_w_out, v_norm_g, v_dn_w_in, v_dn_conv_w, v_dn_a_log, v_dn_dt_bias, v_dn_o_norm_g, v_dn_w_out, v_sb_w_in, v_sb_q_norm_g, v_sb_k_norm_g, v_sb_w_out, v_sc_w_in, v_sc_conv_w, v_sc_w_out):
    weights = dict(norm_g=norm_g, dn_w_in=dn_w_in, dn_conv_w=dn_conv_w, dn_a_log=dn_a_log, dn_dt_bias=dn_dt_bias,
                   dn_o_norm_g=dn_o_norm_g, dn_w_out=dn_w_out, sb_w_in=sb_w_in, sb_q_norm_g=sb_q_norm_g,
                   sb_k_norm_g=sb_k_norm_g, sb_w_out=sb_w_out, sc_w_in=sc_w_in, sc_conv_w=sc_conv_w, sc_w_out=sc_w_out)
    m_in = dict(norm_g=m_norm_g, dn_w_in=m_dn_w_in, dn_conv_w=m_dn_conv_w, dn_a_log=m_dn_a_log,
                dn_dt_bias=m_dn_dt_bias, dn_o_norm_g=m_dn_o_norm_g, dn_w_out=m_dn_w_out, sb_w_in=m_sb_w_in,
                sb_q_norm_g=m_sb_q_norm_g, sb_k_norm_g=m_sb_k_norm_g, sb_w_out=m_sb_w_out, sc_w_in=m_sc_w_in,
                sc_conv_w=m_sc_conv_w, sc_w_out=m_sc_w_out)
    v_in = dict(norm_g=v_norm_g, dn_w_in=v_dn_w_in, dn_conv_w=v_dn_conv_w, dn_a_log=v_dn_a_log,
                dn_dt_bias=v_dn_dt_bias, dn_o_norm_g=v_dn_o_norm_g, dn_w_out=v_dn_w_out, sb_w_in=v_sb_w_in,
                sb_q_norm_g=v_sb_q_norm_g, sb_k_norm_g=v_sb_k_norm_g, sb_w_out=v_sb_w_out, sc_w_in=v_sc_w_in,
                sc_conv_w=v_sc_conv_w, sc_w_out=v_sc_w_out)
    order = list(weights)
    xi, yi, ci = _mesh_pos()

    small = _pack([weights[n] for n, _, _ in SMALL], LANES)
    later = [("dn_w_in", 1), ("dn_w_out", 1), ("sb_w_in", 0), ("sb_w_out", 0), ("sc_w_in", 0), ("sc_w_out", 0)]
    piece = lambda n, l: _halves(weights[n][l].astype(BF16)[None])
    own_first = [piece("dn_w_in", 0), piece("dn_w_out", 0)]
    own_later = [piece(n, l) for n, l in later]
    own_last, own_mid = own_later[:2], own_later[2:]
    me = 2 * xi + yi
    whole = lambda g4, own: lax.dynamic_update_index_in_dim(g4, own, me, 0)
    flat = lambda g4: g4.reshape(N_CHIPS, -1, g4.shape[-1])
    rows_of = lambda w4: w4.reshape(-1, w4.shape[-1])
    dn_in = lambda w4: jnp.pad(_join(w4, 1), ((0, 0), (0, DN_IN_PAD - DN_IN)))
    w_in0, w_out0, small4 = _gather_halves(own_first, small, name="gather_first")
    full = {n: _join(a, ax) for (n, _, ax), a in zip(SMALL, _unpack(small4, SMALL, (N_CHIPS,)).values())}

    def dn_args(j, w_in4, w_out4):
        return (dn_in(flat(w_in4)), full["dn_conv_w"][j], dn_a_log[j], dn_dt_bias[j], full["dn_o_norm_g"][j],
                rows_of(w_out4))

    x0 = x[0]
    dn0 = dn_args(0, whole(w_in0, own_first[0]), whole(w_out0, own_first[1]))
    x1, s0, landed = _dn_layer_fwd(x0, norm_g[0], *dn0, "l0", send=own_mid)
    landed = _forward_halves(landed, name="forward_halves_mid")
    sb_in, sb_out, sc_in, sc_out = [whole(g4, own) for g4, own in zip(landed, own_mid)]
    sb_args = (flat(sb_in), sb_q_norm_g[0], sb_k_norm_g[0], rows_of(sb_out))
    sc_args = (flat(sc_in), full["sc_conv_w"][0], rows_of(sc_out))
    x2, s1, landed = _sb_layer_fwd(x1, norm_g[1], *sb_args, "l1", send=own_last)
    landed = _forward_halves(landed, name="forward_halves_last")
    dn1 = dn_args(1, *[whole(g4, own) for g4, own in zip(landed, own_last)])
    x3, s2 = _sc_layer_fwd(x2, norm_g[2], *sc_args, "l2")
    x4, s3, _ = _dn_layer_fwd(x3, norm_g[3], *dn1, "l3")
    dy, loss_local = _loss_head(x4, loss_target[0], name="loss_head")
    loss = lax.psum(loss_local[0, 0], ("x", "y", "c"))

    by_cols = lambda dw: _split(dw[:, :DN_IN].astype(BF16), 1)
    by_rows = lambda dw: dw.reshape(N_CHIPS, -1, dw.shape[-1])
    cut2 = lambda g4: g4.reshape(N_CHIPS, 2, -1, g4.shape[-1])
    core = ci.astype(jnp.int32).reshape(1)
    chip = me.astype(jnp.int32).reshape(1)

    def chip_sums(g_list, tag):
        sib = _swap_other_half(g_list, name=f"swap_halves_{tag}")
        return [_add_my_half(g, s, core, name=f"sum_cores_{tag}{i}") for i, (g, s) in enumerate(zip(g_list, sib))]

    dx3, dng3, dwin3, dconv3, dal3, ddt3, dgain3, dwout3, _ = _dn_layer_bwd(dy, x3, norm_g[3], *dn1, s3, "l3")
    dx2, dng2, dwin2, dconv2, dwout2 = _sc_layer_bwd(dx3, x2, norm_g[2], *sc_args, s2, "l2")
    dx1, dng1, dwin1, dgq, dgk, dwout1 = _sb_layer_bwd(dx2, x1, norm_g[1], *sb_args, s1, "l1")
    part_later = chip_sums([cut2(by_cols(dwin3)), cut2(by_rows(dwout3)), cut2(dwin1), cut2(by_rows(dwout1)),
                            cut2(dwin2), cut2(by_rows(dwout2))], "later")
    dx0, dng0, dwin0, dconv0, dal0, ddt0, dgain0, dwout0, landed_later = _dn_layer_bwd(
        dx1, x0, norm_g[0], *dn0, s0, "l0", send=part_later)
    part_first = chip_sums([cut2(by_cols(dwin0)), cut2(by_rows(dwout0))], "first")
    landed_first = _scatter_to_chips(part_first, name="scatter_first")
    pieces = [("dn_w_in", 0), ("dn_w_out", 0)] + later
    mine = [_sum_chips(r, p, chip, name=f"sum_chips_{n}{l}")
            for (n, l), r, p in zip(pieces, list(landed_first) + list(landed_later), part_first + part_later)]
    theirs = _sibling_exchange(mine, name="swap_results")
    upd = {}
    for (n, l), a, b in zip(pieces, mine, theirs):
        upd[n] = _adamw_halves(weights[n], a, b, m_in[n], v_in[n], core, layer=l, prev=upd.get(n),
                               name=f"adamw_{n}{l}")
    g_out = {n: upd[n][0] for n, _, _ in BIG}

    grads = dict(
        norm_g=jnp.concatenate([dng0, dng1, dng2, dng3], axis=0), dn_conv_w=jnp.stack([dconv0, dconv3]),
        dn_a_log=jnp.stack([dal0, dal3]), dn_dt_bias=jnp.stack([ddt0, ddt3]),
        dn_o_norm_g=jnp.stack([dgain0, dgain3]), sb_q_norm_g=dgq[None], sb_k_norm_g=dgk[None],
        sc_conv_w=dconv2[None])
    repl = [jnp.broadcast_to(grads[n][None], (N_CHIPS,) + s) for n, s in REPL]
    gsmall = _pack([_split(grads[n], ax) for n, _, ax in SMALL] + repl, LANES, (N_CHIPS,))
    rsmall, = _chip_exchange([gsmall], send_slot_is_dest=True, copy_own=(True,), name="scatter_small")
    psmall = _sum_small(rsmall, name="sum_chips_small")
    qsmall, = _sibling_exchange([psmall], name="swap_cores_small")
    tsmall = _add(psmall, qsmall, name="sum_cores_small")
    g_out.update(_unpack(tsmall, SMALL + REPL))

    for n in order:
        if n not in upd:
            upd[n] = (g_out[n],) + _adamw(weights[n], g_out[n], m_in[n], v_in[n], name=f"adamw_{n}")
    return (loss, dx0[None], *[upd[n][0] for n in order], *[upd[n][1] for n in order],
            *[upd[n][2] for n in order], *[upd[n][3] for n in order])
```

```python
import math

import jax
import jax.numpy as jnp
from jax import lax
from jax.experimental import pallas as pl
from jax.experimental.pallas import tpu as pltpu

F32 = jnp.float32
BF16 = jnp.bfloat16
MESH = pl.DeviceIdType.MESH

RMS_EPS = 1e-6
L2_EPS = 1e-6
LANES = 128
VMEM_BIG = 60 * 1024 * 1024
MM_VMEM = 44 * 1024 * 1024

DN_HEADS, DN_DK, DN_DV, DN_CHUNK, DN_CONV = 8, 128, 256, 64, 4
DN_QK_W = DN_HEADS * DN_DK
DN_V_W = DN_HEADS * DN_DV
DN_CONV_W = 2 * DN_QK_W + DN_V_W
DN_IN = DN_CONV_W + DN_V_W + 2 * DN_HEADS
DN_IN_PAD = DN_CONV_W + DN_V_W + LANES
SB_DH = 64
SC_CONV = 3

ADAM_LR, ADAM_B1, ADAM_B2, ADAM_EPS, ADAM_WD, ADAM_STEP = 0.001, 0.9, 0.999, 1e-08, 0.01, 10


def _pick(n, cands):
    for c in cands:
        if n % c == 0:
            return c
    raise ValueError(f"no tile for {n} in {cands}")


def _bf(x):
    return x.astype(BF16)


def _dot(a, b):
    return jnp.dot(_bf(a), _bf(b), preferred_element_type=F32)


def _dot_nt(a, b):
    return lax.dot_general(_bf(a), _bf(b), (((1,), (1,)), ((), ())), preferred_element_type=F32)


def _dot_tn(a, b):
    return lax.dot_general(_bf(a), _bf(b), (((0,), (0,)), ((), ())), preferred_element_type=F32)


def _split3(a):
    hi = _bf(a)
    r = a - hi.astype(F32)
    mid = _bf(r)
    lo = _bf(r - mid.astype(F32))
    return hi, mid, lo


def _sigmoid(x):
    return 1.0 / (1.0 + jnp.exp(-x))


def _silu(x):
    return x * _sigmoid(x)


def _dsilu(x):
    s = _sigmoid(x)
    return s * (1.0 + x * (1.0 - s))


def _softplus(x):
    return jnp.maximum(x, 0.0) + jnp.log(1.0 + jnp.exp(-jnp.abs(x)))


def _shift_down(z, k):
    if k == 0:
        return z
    row = lax.broadcasted_iota(jnp.int32, z.shape, 0)
    return jnp.where(row >= k, pltpu.roll(z, k, 0), 0.0)


def _shift_up(z, k):
    if k == 0:
        return z
    n = z.shape[0]
    row = lax.broadcasted_iota(jnp.int32, z.shape, 0)
    return jnp.where(row < n - k, pltpu.roll(z, n - k, 0), 0.0)


def _matmul(a, b, *, mode, name, res=None, a_parts=1, b_parts=1, out_parts=1, out_dtype=F32):
    def dims2(x, parts):
        if parts == 1:
            return x.shape
        assert x.shape[0] == parts
        return (x.shape[1], x.shape[2] * parts)

    ash, bsh = dims2(a, a_parts), dims2(b, b_parts)
    if mode == "nn":
        (M, K), (K2, N) = ash, bsh
        dn = (((1,), (0,)), ((), ()))
    elif mode == "nt":
        (M, K), (N, K2) = ash, bsh
        dn = (((1,), (1,)), ((), ()))
    else:
        (K, M), (K2, N) = ash, bsh
        dn = (((0,), (0,)), ((), ()))
    assert K == K2, (ash, bsh, mode)
    tm_max = _pick(M, (512, 256, 128, 64, 32, 16, 8))
    n_unit = N // max(out_parts, b_parts if mode != "nt" else 1)
    k_unit = K // max(a_parts if mode != "tn" else 1, b_parts if mode == "nt" else 1)
    tm, tn, tk = min(
        ((m, n, k) for m in {tm_max, max(tm_max // 2, 8)}
         for n in (2048, 1792, 1024, 896, 768, 512, 384, 256, 128) if n_unit % n == 0
         for k in (k_unit, 2048, 1792, 1024, 896, 512, 256, 128) if k_unit % k == 0
         if 2 * (m * k * a.dtype.itemsize + k * n * b.dtype.itemsize + 2 * m * n * 4) + m * n * 4 <= MM_VMEM),
        key=lambda t: (-t[0] * t[1] * t[2], -t[0], -t[2]))
    nk = K // tk
    grid = (M // tm, N // tn, nk)

    def spec(parts, rows_are, cols_are, tr, tc, width):
        per = width // parts // tc
        if parts == 1:
            return pl.BlockSpec((tr, tc), lambda i, j, k: ((i, j, k)[rows_are], (i, j, k)[cols_are]))
        return pl.BlockSpec((None, tr, tc), lambda i, j, k: ((i, j, k)[cols_are] // per, (i, j, k)[rows_are],
                                                             (i, j, k)[cols_are] % per))

    if mode == "nn":
        a_spec = spec(a_parts, 0, 2, tm, tk, K)
        b_spec = spec(b_parts, 2, 1, tk, tn, N)
    elif mode == "nt":
        a_spec = spec(a_parts, 0, 2, tm, tk, K)
        b_spec = spec(b_parts, 1, 2, tn, tk, K)
    else:
        a_spec = spec(a_parts, 2, 0, tk, tm, M)
        b_spec = spec(b_parts, 2, 1, tk, tn, N)
    o_spec = spec(out_parts, 0, 1, tm, tn, N)
    in_specs = [a_spec, b_spec]
    operands = [a, b]
    if res is not None:
        in_specs.append(pl.BlockSpec((tm, tn), lambda i, j, k: (i, j)))
        operands.append(res)

    def finish(refs, r):
        if res is not None:
            r = refs[2][...] + r
        refs[-2 if nk > 1 else -1][...] = r.astype(out_dtype)

    def body(*refs):
        part = lax.dot_general(_bf(refs[0][...]), _bf(refs[1][...]), dn, preferred_element_type=F32)
        if nk == 1:
            finish(refs, part)
            return
        acc_ref = refs[-1]
        k = pl.program_id(2)

        @pl.when(k == 0)
        def _():
            acc_ref[...] = part

        @pl.when(jnp.logical_and(k > 0, k < nk - 1))
        def _():
            acc_ref[...] += part

        @pl.when(k == nk - 1)
        def _():
            finish(refs, acc_ref[...] + part)

    out_shape = (M, N) if out_parts == 1 else (out_parts, M, N // out_parts)
    return pl.pallas_call(
        body, name=name, grid=grid, in_specs=in_specs, out_specs=o_spec,
        out_shape=jax.ShapeDtypeStruct(out_shape, out_dtype),
        scratch_shapes=[pltpu.VMEM((tm, tn), F32)] if nk > 1 else [],
        compiler_params=pltpu.CompilerParams(dimension_semantics=("parallel", "parallel", "arbitrary"),
                                             vmem_limit_bytes=VMEM_BIG),
    )(*operands)


def _rmsnorm_fwd(x, g, *, name):
    T, D = x.shape
    tm = _pick(T, (512, 256, 128, 64, 32, 16))

    def body(x_ref, g_ref, h_ref):
        xv = x_ref[...]
        r = lax.rsqrt(jnp.mean(xv * xv, axis=-1, keepdims=True) + RMS_EPS)
        h_ref[...] = ((xv * r) * g_ref[...]).astype(BF16)

    return pl.pallas_call(
        body, name=name, grid=(T // tm,),
        in_specs=[pl.BlockSpec((tm, D), lambda i: (i, 0)), pl.BlockSpec((1, D), lambda i: (0, 0))],
        out_specs=pl.BlockSpec((tm, D), lambda i: (i, 0)),
        out_shape=jax.ShapeDtypeStruct((T, D), BF16),
    )(x, g.reshape(1, D))


def _rmsnorm_bwd(x, g, dh, dx_in, *, name):
    T, D = x.shape
    tm = _pick(T, (512, 256, 128, 64, 32, 16))

    def body(x_ref, g_ref, dh_ref, dxin_ref, dx_ref, dg_ref):
        @pl.when(pl.program_id(0) == 0)
        def _():
            dg_ref[...] = jnp.zeros_like(dg_ref)

        xv = x_ref[...]
        r = lax.rsqrt(jnp.mean(xv * xv, axis=-1, keepdims=True) + RMS_EPS)
        xh = xv * r
        dh_v = dh_ref[...]
        dxh = dh_v * g_ref[...]
        dx_ref[...] = dxin_ref[...] + r * (dxh - xh * jnp.mean(dxh * xh, axis=-1, keepdims=True))
        dg_ref[...] += jnp.sum(dh_v * xh, axis=0, keepdims=True)

    row = pl.BlockSpec((tm, D), lambda i: (i, 0))
    vec = pl.BlockSpec((1, D), lambda i: (0, 0))
    return pl.pallas_call(
        body, name=name, grid=(T // tm,),
        in_specs=[row, vec, row, row], out_specs=[row, vec],
        out_shape=[jax.ShapeDtypeStruct((T, D), F32), jax.ShapeDtypeStruct((1, D), F32)],
        compiler_params=pltpu.CompilerParams(dimension_semantics=("arbitrary",)),
    )(x, g.reshape(1, D), dh, dx_in)


def _loss_head(y, target, *, name):
    T, D = y.shape
    tm = _pick(T, (512, 256, 128, 64, 32, 16))

    def body(y_ref, t_ref, dy_ref, l_ref):
        @pl.when(pl.program_id(0) == 0)
        def _():
            l_ref[...] = jnp.zeros_like(l_ref)

        err = y_ref[...] - t_ref[...]
        dy_ref[...] = err * (1.0 / D)
        l_ref[...] += 0.5 * jnp.sum(jnp.mean(err * err, axis=-1, keepdims=True), axis=0, keepdims=True)

    row = pl.BlockSpec((tm, D), lambda i: (i, 0))
    return pl.pallas_call(
        body, name=name, grid=(T // tm,),
        in_specs=[row, row], out_specs=[row, pl.BlockSpec((1, 1), lambda i: (0, 0))],
        out_shape=[jax.ShapeDtypeStruct((T, D), F32), jax.ShapeDtypeStruct((1, 1), F32)],
        compiler_params=pltpu.CompilerParams(dimension_semantics=("arbitrary",)),
    )(y, target)


def _sc_mid_fwd(p3, conv_w, *, name):
    _, T, W = p3.shape
    K = conv_w.shape[0]
    cw = LANES

    def body(p_ref, w_ref, o_ref):
        z = p_ref[1] * p_ref[2]
        cv = sum(w_ref[i:i + 1, :] * _shift_down(z, K - 1 - i) for i in range(K))
        o_ref[...] = ((p_ref[0] * cv) * _silu(p_ref[3])).astype(BF16)

    return pl.pallas_call(
        body, name=name, grid=(W // cw,),
        in_specs=[pl.BlockSpec((4, T, cw), lambda j: (0, 0, j)), pl.BlockSpec((K, cw), lambda j: (0, j))],
        out_specs=pl.BlockSpec((T, cw), lambda j: (0, j)),
        out_shape=jax.ShapeDtypeStruct((T, W), BF16),
        compiler_params=pltpu.CompilerParams(dimension_semantics=("parallel",), vmem_limit_bytes=VMEM_BIG),
    )(p3, conv_w)


def _sc_mid_bwd(p3, conv_w, do, *, name):
    _, T, W = p3.shape
    K = conv_w.shape[0]
    cw = LANES

    def body(p_ref, w_ref, do_ref, dp_ref, dw_ref):
        b, c, u, gate = p_ref[0], p_ref[1], p_ref[2], p_ref[3]
        z = c * u
        zs = [_shift_down(z, K - 1 - i) for i in range(K)]
        cv = sum(w_ref[i:i + 1, :] * zs[i] for i in range(K))
        y = b * cv
        dov = do_ref[...]
        dy = dov * _silu(gate)
        dp_ref[3] = dov * y * _dsilu(gate)
        dp_ref[0] = dy * cv
        dcv = dy * b
        dz = sum(w_ref[i:i + 1, :] * _shift_up(dcv, K - 1 - i) for i in range(K))
        dp_ref[1] = dz * u
        dp_ref[2] = dz * c
        for i in range(K):
            dw_ref[i:i + 1, :] = jnp.sum(dcv * zs[i], axis=0, keepdims=True)

    return pl.pallas_call(
        body, name=name, grid=(W // cw,),
        in_specs=[pl.BlockSpec((4, T, cw), lambda j: (0, 0, j)), pl.BlockSpec((K, cw), lambda j: (0, j)),
                  pl.BlockSpec((T, cw), lambda j: (0, j))],
        out_specs=[pl.BlockSpec((4, T, cw), lambda j: (0, 0, j)), pl.BlockSpec((K, cw), lambda j: (0, j))],
        out_shape=[jax.ShapeDtypeStruct((4, T, W), F32), jax.ShapeDtypeStruct((K, W), F32)],
        compiler_params=pltpu.CompilerParams(dimension_semantics=("parallel",), vmem_limit_bytes=VMEM_BIG),
    )(p3, conv_w, do)


def _sc_layer_fwd(x, ng, w_in, conv_w, w_out, tag):
    h = _rmsnorm_fwd(x, ng, name=f"{tag}_norm")
    p3 = _matmul(h, w_in, mode="nn", b_parts=4, out_parts=4, name=f"{tag}_inproj")
    og = _sc_mid_fwd(p3, conv_w, name=f"{tag}_mid")
    x_new = _matmul(og, w_out, mode="nn", res=x, name=f"{tag}_outproj")
    return x_new, (h, p3, og)


def _sc_layer_bwd(dx, x, ng, w_in, conv_w, w_out, saved, tag):
    h, p3, og = saved
    d_wout = _matmul(og, dx, mode="tn", out_dtype=BF16, name=f"{tag}_dwout")
    dog = _matmul(dx, w_out, mode="nt", name=f"{tag}_dog")
    dp3, dconv = _sc_mid_bwd(p3, conv_w, dog, name=f"{tag}_midbwd")
    d_win = _matmul(h, dp3, mode="tn", b_parts=4, out_parts=4, out_dtype=BF16, name=f"{tag}_dwin")
    dh = _matmul(dp3, w_in, mode="nt", a_parts=4, b_parts=4, name=f"{tag}_dh")
    dx_prev, dng = _rmsnorm_bwd(x, ng, dh, dx, name=f"{tag}_normbwd")
    return dx_prev, dng, d_win, dconv, d_wout


SB_BQ = 256
SB_BK = 256
SB_ROWS = 512
SB_DEAD = -110.0


def _sb_half_mask():
    return lax.broadcasted_iota(jnp.int32, (1, LANES), 1) < SB_DH


def _sb_headnorm(x, g, lo):
    x2 = x * x
    s_lo = jnp.sum(jnp.where(lo, x2, 0.0), axis=-1, keepdims=True)
    s_hi = jnp.sum(jnp.where(lo, 0.0, x2), axis=-1, keepdims=True)
    r = lax.rsqrt(jnp.where(lo, s_lo, s_hi) * (1.0 / SB_DH) + RMS_EPS)
    xh = x * r
    return xh * g, xh, r


def _dot_x2_l(a_l, b_exact_bf16):
    his = [_bf(a) for a in a_l]
    mids = [_bf(a - h.astype(F32)) for a, h in zip(a_l, his)]
    f = lambda p: jnp.dot(p, b_exact_bf16, preferred_element_type=F32)
    return [x + y for x, y in zip([f(h) for h in his], [f(m) for m in mids])]


def _sb_stack(xb, lo):
    zero = jnp.zeros_like(xb)
    return jnp.concatenate([jnp.where(lo, xb, zero), jnp.where(lo, zero, xb)], axis=0)


def _sb_rel(bq, bk):
    row = lax.broadcasted_iota(jnp.int32, (2 * bq, bk), 0)
    col = lax.broadcasted_iota(jnp.int32, (2 * bq, bk), 1)
    return col - jnp.where(row >= bq, row - bq, row)


def _sb_tile(qm, kb, valid):
    z = lax.dot_general(qm, kb, (((1,), (1,)), ((), ())), preferred_element_type=F32)
    sp = _softplus(z)
    return z - sp, (-sp if valid is None else jnp.where(valid, -sp, 0.0))


def _sb_attn_fwd(p3, gq2, gk2, *, name, send=()):
    _, T, W = p3.shape
    bq, bk = min(SB_BQ, T), min(SB_BK, T)
    rows = min(SB_ROWS, T)
    scale = SB_DH ** -0.5
    ns = len(send)
    npair = W // LANES

    def body(*refs):
        p_ref, gq_ref, gk_ref = refs[:3]
        og_ref, o_ref = refs[3 + ns:5 + ns]
        qn_ref, kn_ref, v_ref = refs[5 + 2 * ns:8 + 2 * ns]
        if ns:
            _halves_over_ici(refs[3:3 + ns], refs[5 + ns:5 + 2 * ns], refs[8 + 2 * ns], refs[9 + 2 * ns],
                             pl.program_id(0) == 0, pl.program_id(0) == npair - 1)
        lo = _sb_half_mask()

        def prologue(i, c):
            r0 = pl.multiple_of(i * rows, rows)
            sl = pl.ds(r0, rows)
            qn_ref[sl, :] = (_sb_headnorm(p_ref[0, sl, :], gq_ref[...], lo)[0] * scale).astype(BF16)
            kn_ref[sl, :] = _sb_headnorm(p_ref[1, sl, :], gk_ref[...], lo)[0].astype(BF16)
            v_ref[sl, :] = p_ref[2, sl, :].astype(BF16)
            return c

        lax.fori_loop(0, T // rows, prologue, 0)

        rel = _sb_rel(bq, bk)
        tri = (lax.broadcasted_iota(jnp.int32, (bk, bk), 0)
               > lax.broadcasted_iota(jnp.int32, (bk, bk), 1)).astype(BF16)

        def qblock(qi, c):
            q0 = pl.multiple_of(qi * bq, bq)
            qm = _sb_stack(qn_ref[pl.ds(q0, bq), :], lo)
            nkb = (q0 + bq - 1) // bk + 1

            def tiles(k0s, carry, valids):
                o_acc, a_carry = carry
                sc = [_sb_tile(qm, kn_ref[pl.ds(k0, bk), :], valid) for k0, valid in zip(k0s, valids)]
                later = _dot_x2_l([log1m for _, log1m in sc], tri)
                for (logsig, log1m), lat, k0, valid in zip(sc, later, k0s, valids):
                    wts = jnp.exp(logsig + (lat + a_carry))
                    if valid is not None:
                        wts = jnp.where(valid, wts, 0.0)
                    o_acc = o_acc + jnp.dot(_bf(wts), v_ref[pl.ds(k0, bk), :], preferred_element_type=F32)
                    a_carry = a_carry + jnp.sum(log1m, axis=-1, keepdims=True)
                return o_acc, a_carry

            blk0 = lambda j: pl.multiple_of(j * bk, bk)
            k_last = blk0(nkb - 1)
            o2, t2 = tiles([k_last, blk0(jnp.maximum(nkb - 2, 0))],
                           (jnp.zeros((2 * bq, LANES), F32), jnp.zeros((2 * bq, 1), F32)),
                           [rel < q0 - k_last, nkb >= 2])

            def alive(st):
                return jnp.logical_and(st[0] < nkb - 1, jnp.max(st[2]) > SB_DEAD)

            def back_one(st):
                return (st[0] + 1,) + tiles([blk0(nkb - 2 - st[0])], st[1:], [None])

            _, o2, _ = lax.while_loop(alive, back_one, (jnp.int32(1), o2, t2))
            o = jnp.where(lo, o2[:bq], o2[bq:])
            o_ref[pl.ds(q0, bq), :] = o
            og_ref[pl.ds(q0, bq), :] = (o * _silu(p_ref[3, pl.ds(q0, bq), :])).astype(BF16)
            return c

        lax.fori_loop(0, T // bq, qblock, 0)

    colblk = pl.BlockSpec((T, LANES), lambda j: (0, j))
    vec = pl.BlockSpec((1, LANES), lambda j: (0, 0))
    return pl.pallas_call(
        body, name=name, grid=(npair,),
        in_specs=[pl.BlockSpec((4, T, LANES), lambda j: (0, 0, j)), vec, vec] + [HBM] * ns,
        out_specs=[colblk, colblk] + [HBM] * ns,
        out_shape=[jax.ShapeDtypeStruct((T, W), BF16), jax.ShapeDtypeStruct((T, W), F32)]
        + [jax.ShapeDtypeStruct((N_CHIPS,) + a.shape, a.dtype) for a in send],
        scratch_shapes=[pltpu.VMEM((T, LANES), BF16)] * 3
        + ([pltpu.SemaphoreType.DMA((3 * ns,)), pltpu.SemaphoreType.DMA((3 * ns,))] if ns else []),
        compiler_params=pltpu.CompilerParams(dimension_semantics=("arbitrary",), vmem_limit_bytes=VMEM_BIG),
    )(p3, gq2, gk2, *send)


def _sb_attn_bwd(p3, gq2, gk2, o, dog, *, name):
    _, T, W = p3.shape
    bq, bk = min(SB_BQ, T), min(SB_BK, T)
    rows = min(SB_ROWS, T)
    scale = SB_DH ** -0.5

    def body(p_ref, gq_ref, gk_ref, o_ref, dog_ref, dp_ref, dgq_ref, dgk_ref,
             qn_ref, kn_ref, v_ref, do_ref):
        lo = _sb_half_mask()

        def prologue(i, c):
            r0 = pl.multiple_of(i * rows, rows)
            sl = pl.ds(r0, rows)
            qn_ref[sl, :] = (_sb_headnorm(p_ref[0, sl, :], gq_ref[...], lo)[0] * scale).astype(BF16)
            kn_ref[sl, :] = _sb_headnorm(p_ref[1, sl, :], gk_ref[...], lo)[0].astype(BF16)
            v_ref[sl, :] = p_ref[2, sl, :].astype(BF16)
            gate = p_ref[3, sl, :]
            dogv = dog_ref[sl, :]
            dp_ref[3, sl, :] = dogv * o_ref[sl, :] * _dsilu(gate)
            do_ref[sl, :] = (dogv * _silu(gate)).astype(BF16)
            zero = jnp.zeros((rows, LANES), F32)
            dp_ref[0, sl, :] = zero
            dp_ref[1, sl, :] = zero
            dp_ref[2, sl, :] = zero
            return c

        lax.fori_loop(0, T // rows, prologue, 0)

        rel = _sb_rel(bq, bk)
        rj = lax.broadcasted_iota(jnp.int32, (bk, bk), 0)
        cj = lax.broadcasted_iota(jnp.int32, (bk, bk), 1)
        upto = (rj <= cj).astype(BF16)
        before_m = (rj < cj).astype(BF16)

        def qblock(qi, c):
            q0 = pl.multiple_of(qi * bq, bq)
            qm = _sb_stack(qn_ref[pl.ds(q0, bq), :], lo)
            dom = _sb_stack(do_ref[pl.ds(q0, bq), :], lo)
            nkb = (q0 + bq - 1) // bk + 1
            blk0 = lambda j: pl.multiple_of(j * bk, bk)
            k_last = blk0(nkb - 1)

            def row_sums(k0, valid):
                return jnp.sum(_sb_tile(qm, kn_ref[pl.ds(k0, bk), :], valid)[1], axis=-1, keepdims=True)

            def alive(st):
                return jnp.logical_and(st[0] < nkb, jnp.max(st[1]) > SB_DEAD)

            def back_one(st):
                return st[0] + 1, st[1] + row_sums(blk0(nkb - 1 - st[0]), None)

            n_live, total = lax.while_loop(alive, back_one, (jnp.int32(1), row_sums(k_last, rel < q0 - k_last)))
            k_first = nkb - n_live

            def tiles(k0s, carry, valids):
                dq_acc, a_pre, r_pre = carry
                kss = [pl.ds(k0, bk) for k0 in k0s]
                kbs = [kn_ref[ks, :] for ks in kss]
                sc = [_sb_tile(qm, kb, valid) for kb, valid in zip(kbs, valids)]
                dws = [lax.dot_general(dom, v_ref[ks, :], _NT, preferred_element_type=F32) for ks in kss]
                upto_l = _dot_x2_l([log1m for _, log1m in sc], upto)
                wts_l = []
                for (logsig, log1m), up, valid in zip(sc, upto_l, valids):
                    wts = jnp.exp(logsig + ((total - a_pre) - up))
                    wts_l.append(wts if valid is None else jnp.where(valid, wts, 0.0))
                    a_pre = a_pre + jnp.sum(log1m, axis=-1, keepdims=True)
                ee_l = [dw * wts for dw, wts in zip(dws, wts_l)]
                before_l = _dot_x2_l(ee_l, before_m)
                for (logsig, _), ks, kb, wts, ee, bef, valid in zip(sc, kss, kbs, wts_l, ee_l, before_l, valids):
                    beta = jnp.exp(logsig)
                    dz = ee * (1.0 - beta) - beta * (r_pre + bef)
                    if valid is not None:
                        dz = jnp.where(valid, dz, 0.0)
                    dzb = _bf(dz)
                    dq_acc = dq_acc + jnp.dot(dzb, kb, preferred_element_type=F32)
                    dp_ref[1, ks, :] += lax.dot_general(dzb, qm, _TN, preferred_element_type=F32)
                    dp_ref[2, ks, :] += lax.dot_general(_bf(wts), dom, _TN, preferred_element_type=F32)
                    r_pre = r_pre + jnp.sum(ee, axis=-1, keepdims=True)
                return dq_acc, a_pre, r_pre

            cr = (jnp.zeros((2 * bq, LANES), F32), jnp.zeros((2 * bq, 1), F32), jnp.zeros((2 * bq, 1), F32))
            n_before = jnp.maximum(n_live - 2, 0)
            cr = lax.fori_loop(0, n_before % 2, lambda t, cr: tiles([blk0(k_first)], cr, [None]), cr)
            k_pairs = k_first + n_before % 2
            cr = lax.fori_loop(0, n_before // 2,
                               lambda t, cr: tiles([blk0(k_pairs + 2 * t), blk0(k_pairs + 2 * t + 1)], cr,
                                                   [None, None]), cr)
            dq2, _, _ = tiles([blk0(jnp.maximum(nkb - 2, 0)), k_last], cr, [n_live >= 2, rel < q0 - k_last])
            dp_ref[0, pl.ds(q0, bq), :] = jnp.where(lo, dq2[:bq], dq2[bq:]) * scale
            return c

        lax.fori_loop(0, T // bq, qblock, 0)

        dgq_ref[...] = jnp.zeros_like(dgq_ref)
        dgk_ref[...] = jnp.zeros_like(dgk_ref)

        def epilogue(i, c):
            r0 = pl.multiple_of(i * rows, rows)
            sl = pl.ds(r0, rows)
            for part, g_ref, dg_ref in ((0, gq_ref, dgq_ref), (1, gk_ref, dgk_ref)):
                _, xh, r = _sb_headnorm(p_ref[part, sl, :], g_ref[...], lo)
                dn = dp_ref[part, sl, :]
                dxh = dn * g_ref[...]
                prod = dxh * xh
                m_lo = jnp.sum(jnp.where(lo, prod, 0.0), axis=-1, keepdims=True)
                m_hi = jnp.sum(jnp.where(lo, 0.0, prod), axis=-1, keepdims=True)
                m = jnp.where(lo, m_lo, m_hi) * (1.0 / SB_DH)
                dp_ref[part, sl, :] = r * (dxh - xh * m)
                dg_ref[...] += jnp.sum(dn * xh, axis=0, keepdims=True)
            return c

        lax.fori_loop(0, T // rows, epilogue, 0)

    colblk = pl.BlockSpec((T, LANES), lambda j: (0, j))
    vec = pl.BlockSpec((1, LANES), lambda j: (0, 0))
    part = pl.BlockSpec((4, T, LANES), lambda j: (0, 0, j))
    gvec = pl.BlockSpec((None, 1, LANES), lambda j: (j, 0, 0))
    npair = W // LANES
    return pl.pallas_call(
        body, name=name, grid=(npair,),
        in_specs=[part, vec, vec, colblk, colblk],
        out_specs=[part, gvec, gvec],
        out_shape=[jax.ShapeDtypeStruct((4, T, W), F32), jax.ShapeDtypeStruct((npair, 1, LANES), F32),
                   jax.ShapeDtypeStruct((npair, 1, LANES), F32)],
        scratch_shapes=[pltpu.VMEM((T, LANES), BF16)] * 4,
        compiler_params=pltpu.CompilerParams(dimension_semantics=("parallel",), vmem_limit_bytes=VMEM_BIG),
    )(p3, gq2, gk2, o, dog)


_NN = (((1,), (0,)), ((), ()))
_NT = (((1,), (1,)), ((), ()))
_TN = (((0,), (0,)), ((), ()))
DN_TB = 512
DN_HEADS_FWD = 4
DN_HEADS_BWD = 2
DN_INV_EXACT_LEVELS = 2
DN_AB_COL = (DN_CONV_W + DN_V_W) // LANES


def _dn_conv(x, w_ref):
    k = w_ref.shape[0]
    return sum(w_ref[i:i + 1, :] * _shift_down(x, k - 1 - i) for i in range(k))


def _dn_prep_fwd(p, conv_w, *, name):
    T = p.shape[0]
    cw = conv_w.shape[1]
    n_qk = 2 * DN_QK_W // LANES

    def body(p_ref, w_ref, o_ref):
        s = _silu(_dn_conv(p_ref[...], w_ref))
        r = lax.rsqrt(jnp.sum(s * s, axis=-1, keepdims=True) + L2_EPS)
        o_ref[...] = jnp.where(pl.program_id(0) < n_qk, s * r, s)

    colblk = pl.BlockSpec((T, LANES), lambda j: (0, j))
    return pl.pallas_call(
        body, name=name, grid=(cw // LANES,),
        in_specs=[colblk, pl.BlockSpec((DN_CONV, LANES), lambda j: (0, j))],
        out_specs=colblk, out_shape=jax.ShapeDtypeStruct((T, cw), F32),
        compiler_params=pltpu.CompilerParams(dimension_semantics=("parallel",), vmem_limit_bytes=VMEM_BIG),
    )(p, conv_w)


def _dn_chunk_tri(rows, upper):
    r = lax.broadcasted_iota(jnp.int32, (rows, rows), 0)
    c = lax.broadcasted_iota(jnp.int32, (rows, rows), 1)
    same = (r // DN_CHUNK) == (c // DN_CHUNK)
    return jnp.logical_and(same, (c >= r) if upper else (c <= r)).astype(BF16)


def _dn_lane_rows(a_log, dt_bias):
    pad = lambda v: jnp.zeros((1, LANES), F32).at[0, :DN_HEADS].set(v)
    return pad(a_log), pad(dt_bias)


def _dn_ab_parts(blk, alog_row, dtb_row):
    lane = lax.broadcasted_iota(jnp.int32, (1, LANES), 1)
    is_a = lane < DN_HEADS
    is_b = jnp.logical_and(lane >= DN_HEADS, lane < 2 * DN_HEADS)
    a_arg = jnp.where(is_a, blk + dtb_row, 0.0)
    neg_exp = jnp.where(is_a, -jnp.exp(alog_row), 0.0)
    log_a = neg_exp * _softplus(a_arg)
    beta = jnp.where(is_b, _sigmoid(blk), 0.0)
    return is_a, is_b, a_arg, neg_exp, log_a, beta


def _dn_ab_fwd(p, alog_row, dtb_row, *, name):
    T = p.shape[0]
    rows = min(DN_TB, T)

    def body(p_ref, al_ref, dt_ref, o_ref):
        _, _, _, _, log_a, beta = _dn_ab_parts(p_ref[...], al_ref[...], dt_ref[...])
        hi, mid, lo_ = _split3(log_a)
        tri = _dn_chunk_tri(rows, upper=False)
        f = lambda q: jnp.dot(tri, q, preferred_element_type=F32)
        o_ref[...] = (f(hi) + f(mid) + f(lo_)) + beta

    blk = pl.BlockSpec((rows, LANES), lambda i: (i, DN_AB_COL))
    vec = pl.BlockSpec((1, LANES), lambda i: (0, 0))
    return pl.pallas_call(
        body, name=name, grid=(T // rows,), in_specs=[blk, vec, vec],
        out_specs=pl.BlockSpec((rows, LANES), lambda i: (i, 0)),
        out_shape=jax.ShapeDtypeStruct((T, LANES), F32),
        compiler_params=pltpu.CompilerParams(dimension_semantics=("parallel",)),
    )(p, alog_row, dtb_row)


def _hp_l(a_l, b_l, dims=_NN):
    sa = [_split3(a)[:2] for a in a_l]
    sb = [_split3(b)[:2] for b in b_l]
    f = lambda p, q: lax.dot_general(p, q, dims, preferred_element_type=F32)
    hh = [f(x[0], y[0]) for x, y in zip(sa, sb)]
    hm = [f(x[0], y[1]) for x, y in zip(sa, sb)]
    mh = [f(x[1], y[0]) for x, y in zip(sa, sb)]
    return [a + (b + c) for a, b, c in zip(hh, hm, mh)]


def _dn_local(qs, k, v, g, beta, nc):
    c = DN_CHUNK
    cut = lambda x: [x[i * c:(i + 1) * c] for i in range(nc)]
    row = lax.broadcasted_iota(jnp.int32, (c, c), 0)
    col = lax.broadcasted_iota(jnp.int32, (c, c), 1)
    eye, lower, strict = row == col, row >= col, row > col
    rowid = lax.broadcasted_iota(jnp.int32, (c, 1), 0)
    eg = jnp.exp(g)
    kb = k * beta
    rhs_k = kb * eg
    g_l, k_l, kb_l, qs_l = cut(g), cut(k), cut(kb), cut(qs)
    g_row_l = [jnp.sum(jnp.where(eye, x, 0.0), axis=0, keepdims=True) for x in g_l]
    dec_l = [jnp.where(lower, jnp.exp(jnp.where(lower, x - y, 0.0)), 0.0) for x, y in zip(g_l, g_row_l)]
    kk_l = [_dot_nt(a, b) for a, b in zip(kb_l, k_l)]
    qk_l = [_dot_nt(a, b) for a, b in zip(qs_l, k_l)]
    low_l = [jnp.where(strict, a * d, 0.0) for a, d in zip(kk_l, dec_l)]
    eye_f = eye.astype(F32)
    pw_l = [-x for x in low_l]
    inv_l = [eye_f + x for x in pw_l]
    plain = lambda a_l, b_l: [_dot(a, b) for a, b in zip(a_l, b_l)]
    for level in range(int(math.log2(c)) - 1):
        mul = _hp_l if level < DN_INV_EXACT_LEVELS else plain
        pw_l = mul(pw_l, pw_l)
        inv_l = [a + b for a, b in zip(inv_l, mul(inv_l, pw_l))]
    u_l = [_dot(a, b) for a, b in zip(inv_l, cut(v * beta))]
    w_l = [_dot(a, b) for a, b in zip(inv_l, cut(rhs_k))]
    aqk_l = [jnp.where(lower, a * d, 0.0) for a, d in zip(qk_l, dec_l)]
    g_last_l = [jnp.sum(jnp.where(rowid == c - 1, x, 0.0), axis=0, keepdims=True) for x in g_l]
    ekd_l = [jnp.exp(a - b) for a, b in zip(g_last_l, g_l)]
    kd_l = [a * b for a, b in zip(k_l, ekd_l)]
    qd_l = cut(qs * eg)
    kw_l = [_dot_tn(a, b) for a, b in zip(kd_l, w_l)]
    qp_l = [q - _dot(a, w) for q, a, w in zip(qd_l, aqk_l, w_l)]
    return dict(eye=eye, lower=lower, strict=strict, dec=dec_l, k=k_l, kb=kb_l, qs=qs_l, low=low_l, inv=inv_l,
                eg=cut(eg), rhs_k=cut(rhs_k), u=u_l, w=w_l, aqk=aqk_l, g_last=g_last_l, qd=qd_l,
                ekd=ekd_l, kd=kd_l, kw=kw_l, qp=qp_l)


def _dn_head_cols(gb_blk, head):
    lane = lax.broadcasted_iota(jnp.int32, (1, LANES), 1)
    g = jnp.sum(jnp.where(lane == head, gb_blk, 0.0), axis=-1, keepdims=True)
    beta = jnp.sum(jnp.where(lane == head + DN_HEADS, gb_blk, 0.0), axis=-1, keepdims=True)
    return g, beta


def _halves_over_ici(s_refs, o_refs, send_sems, recv_sems, first, last):
    x, y, c = _mesh_pos()
    me = 2 * x + y
    chips = _other_chips(x, y)
    pairs = [(a, k) for a in range(len(s_refs)) for k in range(3)]

    def copy(a, k, slot):
        px, py = chips[k]
        return pltpu.make_async_remote_copy(
            src_ref=s_refs[a].at[c], dst_ref=o_refs[a].at[slot, c], send_sem=send_sems.at[3 * a + k],
            recv_sem=recv_sems.at[3 * a + k], device_id=(px, py, c), device_id_type=MESH)

    @pl.when(first)
    def _():
        for a, k in pairs:
            copy(a, k, me).start()

    @pl.when(last)
    def _():
        for a, k in pairs:
            px, py = chips[k]
            copy(a, k, 2 * px + py).wait_recv()
        for a, k in pairs:
            copy(a, k, me).wait_send()


def _dn_delta_fwd(qkv, gb, p, o_gain, *, name, send=()):
    T = qkv.shape[0]
    tb = min(DN_TB, T)
    nb, nc = T // tb, tb // DN_CHUNK
    H = DN_HEADS
    qscale = DN_DK ** -0.5
    ns = len(send)
    hp = DN_HEADS_FWD

    def body(*refs):
        q_ref, k_ref, v_ref, gb_ref, gate_ref, gain_ref = refs[:6]
        o_ref, og_ref, st_ref = refs[6 + ns:9 + ns]
        s_ref = refs[9 + 2 * ns]
        pair, blk = pl.program_id(0), pl.program_id(1)
        if ns:
            _halves_over_ici(refs[6:6 + ns], refs[9 + ns:9 + 2 * ns], refs[10 + 2 * ns], refs[11 + 2 * ns],
                             jnp.logical_and(pair == 0, blk == 0),
                             jnp.logical_and(pair == H // hp - 1, blk == nb - 1))

        @pl.when(blk == 0)
        def _():
            s_ref[...] = jnp.zeros_like(s_ref)

        gbv = gb_ref[...]
        ts, ku, op = [], [], []
        for e in range(hp):
            qk_e, v_e = slice(e * DN_DK, (e + 1) * DN_DK), slice(e * DN_DV, (e + 1) * DN_DV)
            g, beta = _dn_head_cols(gbv, hp * pair + e)
            t = _dn_local(q_ref[:, qk_e] * qscale, k_ref[:, qk_e], v_ref[:, v_e], g, beta, nc)
            ts.append(t)
            ku.append([_dot_tn(a, b) for a, b in zip(t["kd"], t["u"])])
            op.append([_dot(a, b) for a, b in zip(t["aqk"], t["u"])])
        s32 = [s_ref[e] for e in range(hp)]
        s_l = [[] for _ in range(hp)]
        for i in range(nc):
            sb = [_bf(x) for x in s32]
            for e in range(hp):
                st_ref[e, i] = sb[e]
                s_l[e].append(sb[e])
            prod = [_dot(ts[e]["kw"][i], sb[e]) for e in range(hp)]
            s32 = [s32[e] * jnp.exp(ts[e]["g_last"][i]) - prod[e] + ku[e][i] for e in range(hp)]
        for e in range(hp):
            s_ref[e] = s32[e]
        o = jnp.concatenate(
            [jnp.concatenate([_dot(qp, sb) + x for qp, sb, x in zip(ts[e]["qp"], s_l[e], op[e])], axis=0)
             for e in range(hp)], axis=1)
        o_ref[...] = o
        gain = gain_ref[...]
        for e in range(hp):
            v_e = slice(e * DN_DV, (e + 1) * DN_DV)
            oe = o[:, v_e]
            r = lax.rsqrt(jnp.mean(oe * oe, axis=-1, keepdims=True) + RMS_EPS)
            og_ref[:, v_e] = (((oe * r) * gain) * _silu(gate_ref[:, v_e])).astype(BF16)

    qk = lambda col0: pl.BlockSpec((tb, hp * DN_DK), lambda h, i: (i, col0 // (hp * DN_DK) + h))
    vblk = lambda col0: pl.BlockSpec((tb, hp * DN_DV), lambda h, i: (i, col0 // (hp * DN_DV) + h))
    return pl.pallas_call(
        body, name=name, grid=(H // hp, nb),
        in_specs=[qk(0), qk(DN_QK_W), vblk(2 * DN_QK_W), pl.BlockSpec((tb, LANES), lambda h, i: (i, 0)),
                  vblk(DN_CONV_W), pl.BlockSpec((1, DN_DV), lambda h, i: (0, 0))] + [HBM] * ns,
        out_specs=[vblk(0), vblk(0), pl.BlockSpec((hp, nc, DN_DK, DN_DV), lambda h, i: (h, i, 0, 0))] + [HBM] * ns,
        out_shape=[jax.ShapeDtypeStruct((T, DN_V_W), F32), jax.ShapeDtypeStruct((T, DN_V_W), BF16),
                   jax.ShapeDtypeStruct((H, T // DN_CHUNK, DN_DK, DN_DV), BF16)]
        + [jax.ShapeDtypeStruct((N_CHIPS,) + a.shape, a.dtype) for a in send],
        scratch_shapes=[pltpu.VMEM((hp, DN_DK, DN_DV), F32)]
        + ([pltpu.SemaphoreType.DMA((3 * ns,)), pltpu.SemaphoreType.DMA((3 * ns,))] if ns else []),
        compiler_params=pltpu.CompilerParams(dimension_semantics=("arbitrary", "arbitrary")),
    )(qkv, qkv, qkv, gb, p, o_gain, *send)


def _blocks_over_ici(p_refs, o_refs, send_sems, recv_sems, first, last):
    x, y, c = _mesh_pos()
    me = 2 * x + y
    chips = _other_chips(x, y)
    pairs = [(a, k) for a in range(len(p_refs)) for k in range(3)]

    def copy(a, k, slot):
        px, py = chips[k]
        return pltpu.make_async_remote_copy(
            src_ref=p_refs[a].at[2 * px + py], dst_ref=o_refs[a].at[slot], send_sem=send_sems.at[3 * a + k],
            recv_sem=recv_sems.at[3 * a + k], device_id=(px, py, c), device_id_type=MESH)

    @pl.when(first)
    def _():
        for a, k in pairs:
            copy(a, k, me).start()

    @pl.when(last)
    def _():
        for a, k in pairs:
            px, py = chips[k]
            copy(a, k, 2 * px + py).wait_recv()
        for a, k in pairs:
            copy(a, k, me).wait_send()


def _dn_delta_bwd(qkv, gb, p, o_gain, o, states, dog, *, name, send=()):
    T = qkv.shape[0]
    tb = min(DN_TB, T)
    nb, nc = T // tb, tb // DN_CHUNK
    H = DN_HEADS
    qscale = DN_DK ** -0.5
    ns = len(send)
    hp = DN_HEADS_BWD

    def body(*refs):
        q_ref, k_ref, v_ref, gb_ref, gate_ref, gain_ref, o_ref, st_ref, dog_ref = refs[:9]
        dq_ref, dk_ref, dv_ref, dgate_ref, dgb_ref, dgain_ref = refs[9 + ns:15 + ns]
        ds_ref = refs[15 + 2 * ns]
        pair, blk = pl.program_id(0), pl.program_id(1)
        first = jnp.logical_and(pair == 0, blk == 0)
        if ns:
            _blocks_over_ici(refs[9:9 + ns], refs[15 + ns:15 + 2 * ns], refs[16 + 2 * ns], refs[17 + 2 * ns],
                             first, jnp.logical_and(pair == H // hp - 1, blk == nb - 1))

        @pl.when(blk == 0)
        def _():
            ds_ref[...] = jnp.zeros_like(ds_ref)

        @pl.when(first)
        def _():
            dgain_ref[...] = jnp.zeros_like(dgain_ref)

        lane = lax.broadcasted_iota(jnp.int32, (1, LANES), 1)
        c = DN_CHUNK
        cut = lambda x: [x[i * c:(i + 1) * c] for i in range(nc)]
        cat = lambda xs: jnp.concatenate(xs, axis=0)
        rsum = lambda x: jnp.sum(x, axis=-1, keepdims=True)
        gbv, gain = gb_ref[...], gain_ref[...]

        def before_chain(e):
            qk_e, v_e = slice(e * DN_DK, (e + 1) * DN_DK), slice(e * DN_DV, (e + 1) * DN_DV)
            g, beta = _dn_head_cols(gbv, hp * pair + e)
            ov, gate, dogv = o_ref[:, v_e], gate_ref[:, v_e], dog_ref[:, v_e]
            r = lax.rsqrt(jnp.mean(ov * ov, axis=-1, keepdims=True) + RMS_EPS)
            oh = ov * r
            dnrm = dogv * _silu(gate)
            dgate_ref[:, v_e] = dogv * (oh * gain) * _dsilu(gate)
            doh = dnrm * gain
            do_l = cut(r * (doh - oh * jnp.mean(doh * oh, axis=-1, keepdims=True)))
            dgain_ref[...] += jnp.sum(dnrm * oh, axis=0, keepdims=True)
            k, v = k_ref[:, qk_e], v_ref[:, v_e]
            t = _dn_local(q_ref[:, qk_e] * qscale, k, v, g, beta, nc)
            s_l = [st_ref[e, i] for i in range(nc)]
            vn_l = [u - _dot(w, sb) for u, w, sb in zip(t["u"], t["w"], s_l)]
            return dict(
                t=t, beta=beta, v=v, s=s_l, vn=vn_l, egl=[jnp.exp(x) for x in t["g_last"]],
                dqd=[_dot_nt(a, sb) for a, sb in zip(do_l, s_l)], daqk=[_dot_nt(a, b) for a, b in zip(do_l, vn_l)],
                aqk_do=[_dot_tn(a, b) for a, b in zip(t["aqk"], do_l)],
                qp_do=[_dot_tn(a, b) for a, b in zip(t["qp"], do_l)])

        hs = [before_chain(e) for e in range(hp)]
        ds = [ds_ref[e] for e in range(hp)]
        ds_l = [[None] * nc for _ in range(hp)]
        for i in reversed(range(nc)):
            for e in range(hp):
                ds_l[e][i] = ds[e]
            prod = [_dot_tn(hs[e]["t"]["kw"][i], ds[e]) for e in range(hp)]
            ds = [ds[e] * hs[e]["egl"][i] - prod[e] + hs[e]["qp_do"][i] for e in range(hp)]
        for e in range(hp):
            ds_ref[e] = ds[e]

        def after_chain(e):
            hd, t = hs[e], hs[e]["t"]
            lower, strict, eye = t["lower"], t["strict"], t["eye"]
            s_l, vn_l, dqd_l, daqk_l, egl_l, beta, v = (hd["s"], hd["vn"], hd["dqd"], hd["daqk"], hd["egl"],
                                                         hd["beta"], hd["v"])
            dvn_l = [a + _dot(kd, d) for a, kd, d in zip(hd["aqk_do"], t["kd"], ds_l[e])]
            dkd_l = [_dot_nt(a, d) for a, d in zip(vn_l, ds_l[e])]
            dgl_l = [jnp.sum(rsum(d * sb.astype(F32)), axis=0, keepdims=True) * x
                     for d, sb, x in zip(ds_l[e], s_l, egl_l)]
            dw_l = [-_dot_nt(a, sb) for a, sb in zip(dvn_l, s_l)]
            dbv_l = [_dot_tn(a, b) for a, b in zip(t["inv"], dvn_l)]
            dbk_l = [_dot_tn(a, b) for a, b in zip(t["inv"], dw_l)]
            dlow_l = [-(_dot_nt(a, b) + _dot_nt(x, y)) for a, b, x, y in zip(dbv_l, t["u"], dbk_l, t["w"])]
            m_l = [jnp.where(strict, a * d, 0.0) for a, d in zip(dlow_l, t["dec"])]
            nmat_l = [jnp.where(lower, a * d, 0.0) for a, d in zip(daqk_l, t["dec"])]
            dkb_l = [_dot(m, kk) + b * x for m, kk, b, x in zip(m_l, t["k"], dbk_l, t["eg"])]
            dqs_l = [_dot(n, kk) + a * x for n, kk, a, x in zip(nmat_l, t["k"], dqd_l, t["eg"])]
            dk1_l = [_dot_tn(m, kb) for m, kb in zip(m_l, t["kb"])]
            dk2_l = [_dot_tn(n, q) for n, q in zip(nmat_l, t["qs"])]
            beta_l, v_l = cut(beta), cut(v)
            rowid = lax.broadcasted_iota(jnp.int32, (c, 1), 0)
            dk_l, dg_l, dbeta_l = [], [], []
            for i in range(nc):
                dk_l.append(dk1_l[i] + dk2_l[i] + dkd_l[i] * t["ekd"][i] + dkb_l[i] * beta_l[i])
                gmat = jnp.where(strict, dlow_l[i] * t["low"][i], 0.0) + daqk_l[i] * t["aqk"][i]
                s_kd = rsum(dkd_l[i] * t["kd"][i])
                dg = (rsum(gmat) + rsum(dqd_l[i] * t["qd"][i]) - s_kd + rsum(dbk_l[i] * t["rhs_k"][i]))
                dg_row = -jnp.sum(gmat, axis=0, keepdims=True)
                dg = dg + rsum(jnp.where(eye, dg_row, 0.0))
                dgl = dgl_l[i] + jnp.sum(s_kd, axis=0, keepdims=True)
                dg_l.append(dg + jnp.where(rowid == c - 1, dgl, 0.0))
                dbeta_l.append(rsum(dbv_l[i] * v_l[i]) + rsum(dkb_l[i] * t["k"][i]))
            head = hp * pair + e
            dgb = (jnp.where(lane == head, cat(dg_l), 0.0) + jnp.where(lane == head + DN_HEADS, cat(dbeta_l), 0.0))
            return cat(dqs_l) * qscale, cat(dk_l), cat(dbv_l) * beta, dgb

        for e in range(hp):
            dq, dk, dv, dgb = after_chain(e)
            dq_ref[:, e * DN_DK:(e + 1) * DN_DK] = dq
            dk_ref[:, e * DN_DK:(e + 1) * DN_DK] = dk
            dv_ref[:, e * DN_DV:(e + 1) * DN_DV] = dv
            dgb_ref[e] = dgb

    rev = lambda i: nb - 1 - i
    qk = lambda col0: pl.BlockSpec((tb, hp * DN_DK), lambda h, i: (rev(i), col0 // (hp * DN_DK) + h))
    vblk = lambda col0: pl.BlockSpec((tb, hp * DN_DV), lambda h, i: (rev(i), col0 // (hp * DN_DV) + h))
    gain_spec = pl.BlockSpec((1, DN_DV), lambda h, i: (0, 0))
    return pl.pallas_call(
        body, name=name, grid=(H // hp, nb),
        in_specs=[qk(0), qk(DN_QK_W), vblk(2 * DN_QK_W), pl.BlockSpec((tb, LANES), lambda h, i: (rev(i), 0)),
                  vblk(DN_CONV_W), gain_spec, vblk(0),
                  pl.BlockSpec((hp, nc, DN_DK, DN_DV), lambda h, i: (h, rev(i), 0, 0)), vblk(0)] + [HBM] * ns,
        out_specs=[qk(0), qk(0), vblk(0), vblk(DN_CONV_W),
                   pl.BlockSpec((hp, tb, LANES), lambda h, i: (h, rev(i), 0)), gain_spec] + [HBM] * ns,
        out_shape=[jax.ShapeDtypeStruct((T, DN_QK_W), F32), jax.ShapeDtypeStruct((T, DN_QK_W), F32),
                   jax.ShapeDtypeStruct((T, DN_V_W), F32), jax.ShapeDtypeStruct((T, DN_IN_PAD), F32),
                   jax.ShapeDtypeStruct((H, T, LANES), F32), jax.ShapeDtypeStruct((1, DN_DV), F32)]
        + [jax.ShapeDtypeStruct(a.shape, a.dtype) for a in send],
        scratch_shapes=[pltpu.VMEM((hp, DN_DK, DN_DV), F32)]
        + ([pltpu.SemaphoreType.DMA((3 * ns,)), pltpu.SemaphoreType.DMA((3 * ns,))] if ns else []),
        compiler_params=pltpu.CompilerParams(dimension_semantics=("arbitrary", "arbitrary")),
    )(qkv, qkv, qkv, gb, p, o_gain, o, states, dog, *send)


def _dn_conv_bwd(p, conv_w, d, dp, *, first, normed, name):
    T, width = d.shape

    def body(p_ref, w_ref, d_ref, dp_in, dp_ref, dw_ref):
        del dp_in
        x = p_ref[...]
        ksz = w_ref.shape[0]
        xs = [_shift_down(x, ksz - 1 - i) for i in range(ksz)]
        xc = sum(w_ref[i:i + 1, :] * xs[i] for i in range(ksz))
        ds = d_ref[...]
        if normed:
            s = _silu(xc)
            r = lax.rsqrt(jnp.sum(s * s, axis=-1, keepdims=True) + L2_EPS)
            y = s * r
            ds = r * (ds - y * jnp.sum(ds * y, axis=-1, keepdims=True))
        dxc = ds * _dsilu(xc)
        dp_ref[...] = sum(w_ref[i:i + 1, :] * _shift_up(dxc, ksz - 1 - i) for i in range(ksz))
        for i in range(ksz):
            dw_ref[i:i + 1, :] = jnp.sum(dxc * xs[i], axis=0, keepdims=True)

    shifted = pl.BlockSpec((T, LANES), lambda j: (0, first + j))
    return pl.pallas_call(
        body, name=name, grid=(width // LANES,),
        in_specs=[shifted, pl.BlockSpec((DN_CONV, LANES), lambda j: (0, first + j)),
                  pl.BlockSpec((T, LANES), lambda j: (0, j)), pl.BlockSpec(memory_space=pl.ANY)],
        out_specs=[shifted, pl.BlockSpec((DN_CONV, LANES), lambda j: (0, j))],
        out_shape=[jax.ShapeDtypeStruct(dp.shape, F32), jax.ShapeDtypeStruct((DN_CONV, width), F32)],
        input_output_aliases={3: 0},
        compiler_params=pltpu.CompilerParams(dimension_semantics=("parallel",), vmem_limit_bytes=VMEM_BIG),
    )(p, conv_w, d, dp)


def _dn_ab_bwd(p, alog_row, dtb_row, dgb, dp, *, name):
    T = p.shape[0]
    rows = min(DN_TB, T)
    H = DN_HEADS

    def body(p_ref, al_ref, dt_ref, dgb_ref, dp_in, dp_ref, dal_ref, ddt_ref):
        del dp_in

        @pl.when(pl.program_id(0) == 0)
        def _():
            dal_ref[...] = jnp.zeros_like(dal_ref)
            ddt_ref[...] = jnp.zeros_like(ddt_ref)

        blk = p_ref[...]
        is_a, is_b, a_arg, neg_exp, log_a, beta = _dn_ab_parts(blk, al_ref[...], dt_ref[...])
        d = dgb_ref[0]
        for hh in range(1, H):
            d = d + dgb_ref[hh]
        hi, mid, lo_ = _split3(jnp.where(is_a, d, 0.0))
        tri = _dn_chunk_tri(rows, upper=True)
        f = lambda q: jnp.dot(tri, q, preferred_element_type=F32)
        dlog_a = f(hi) + f(mid) + f(lo_)
        da_in = dlog_a * neg_exp * _sigmoid(a_arg)
        db_in = jnp.where(is_b, d, 0.0) * beta * (1.0 - beta)
        dp_ref[...] = jnp.where(is_a, da_in, 0.0) + db_in
        dal_ref[...] += jnp.sum(dlog_a * log_a, axis=0, keepdims=True)
        ddt_ref[...] += jnp.sum(jnp.where(is_a, da_in, 0.0), axis=0, keepdims=True)

    blk = pl.BlockSpec((rows, LANES), lambda i: (i, DN_AB_COL))
    vec = pl.BlockSpec((1, LANES), lambda i: (0, 0))
    return pl.pallas_call(
        body, name=name, grid=(T // rows,),
        in_specs=[blk, vec, vec, pl.BlockSpec((H, rows, LANES), lambda i: (0, i, 0)),
                  pl.BlockSpec(memory_space=pl.ANY)],
        out_specs=[blk, vec, vec],
        out_shape=[jax.ShapeDtypeStruct(dp.shape, F32), jax.ShapeDtypeStruct((1, LANES), F32),
                   jax.ShapeDtypeStruct((1, LANES), F32)],
        input_output_aliases={4: 0},
        compiler_params=pltpu.CompilerParams(dimension_semantics=("arbitrary",)),
    )(p, alog_row, dtb_row, dgb, dp)


def _dn_layer_fwd(x, ng, w_in, conv_w, a_log, dt_bias, o_gain, w_out, tag, send=()):
    alog_row, dtb_row = _dn_lane_rows(a_log, dt_bias)
    gain = o_gain.reshape(1, DN_DV)
    h = _rmsnorm_fwd(x, ng, name=f"{tag}_norm")
    p = _matmul(h, w_in, mode="nn", name=f"{tag}_inproj")
    qkv = _dn_prep_fwd(p, conv_w, name=f"{tag}_prep")
    gb = _dn_ab_fwd(p, alog_row, dtb_row, name=f"{tag}_ab")
    o, og, states, *landed = _dn_delta_fwd(qkv, gb, p, gain, name=f"{tag}_delta", send=send)
    x_new = _matmul(og, w_out, mode="nn", res=x, name=f"{tag}_outproj")
    return x_new, (h, p, qkv, gb, o, og, states), landed


def _dn_layer_bwd(dx, x, ng, w_in, conv_w, a_log, dt_bias, o_gain, w_out, saved, tag, send=()):
    h, p, qkv, gb, o, og, states = saved
    alog_row, dtb_row = _dn_lane_rows(a_log, dt_bias)
    gain = o_gain.reshape(1, DN_DV)
    d_wout = _matmul(og, dx, mode="tn", out_dtype=BF16, name=f"{tag}_dwout")
    dog = _matmul(dx, w_out, mode="nt", name=f"{tag}_dog")
    dq, dk, dv, dp, dgb, dgain, *landed = _dn_delta_bwd(qkv, gb, p, gain, o, states, dog, name=f"{tag}_deltabwd",
                                                        send=send)
    n_qk = DN_QK_W // LANES
    dp, dconv_q = _dn_conv_bwd(p, conv_w, dq, dp, first=0, normed=True, name=f"{tag}_convbwd_q")
    dp, dconv_k = _dn_conv_bwd(p, conv_w, dk, dp, first=n_qk, normed=True, name=f"{tag}_convbwd_k")
    dp, dconv_v = _dn_conv_bwd(p, conv_w, dv, dp, first=2 * n_qk, normed=False, name=f"{tag}_convbwd_v")
    dconv = jnp.concatenate([dconv_q, dconv_k, dconv_v], axis=1)
    dp, dal, ddt = _dn_ab_bwd(p, alog_row, dtb_row, dgb, dp, name=f"{tag}_abbwd")
    d_win = _matmul(h, dp, mode="tn", name=f"{tag}_dwin")
    dh = _matmul(dp, w_in, mode="nt", name=f"{tag}_dh")
    dx_prev, dng = _rmsnorm_bwd(x, ng, dh, dx, name=f"{tag}_normbwd")
    return dx_prev, dng, d_win, dconv, dal[0, :DN_HEADS], ddt[0, :DN_HEADS], dgain[0], d_wout, landed


def _sb_gains(g):
    return jnp.concatenate([g, g]).reshape(1, LANES)


def _sb_layer_fwd(x, ng, w_in, gq, gk, w_out, tag, send=()):
    h = _rmsnorm_fwd(x, ng, name=f"{tag}_norm")
    p3 = _matmul(h, w_in, mode="nn", b_parts=4, out_parts=4, name=f"{tag}_inproj")
    og, o, *landed = _sb_attn_fwd(p3, _sb_gains(gq), _sb_gains(gk), name=f"{tag}_attn", send=send)
    x_new = _matmul(og, w_out, mode="nn", res=x, name=f"{tag}_outproj")
    return x_new, (h, p3, og, o), landed


def _sb_layer_bwd(dx, x, ng, w_in, gq, gk, w_out, saved, tag):
    h, p3, og, o = saved
    d_wout = _matmul(og, dx, mode="tn", out_dtype=BF16, name=f"{tag}_dwout")
    dog = _matmul(dx, w_out, mode="nt", name=f"{tag}_dog")
    dp3, dgq, dgk = _sb_attn_bwd(p3, _sb_gains(gq), _sb_gains(gk), o, dog, name=f"{tag}_attnbwd")
    fold = lambda d: jnp.sum(d.reshape(-1, SB_DH), axis=0)
    d_win = _matmul(h, dp3, mode="tn", b_parts=4, out_parts=4, out_dtype=BF16, name=f"{tag}_dwin")
    dh = _matmul(dp3, w_in, mode="nt", a_parts=4, b_parts=4, name=f"{tag}_dh")
    dx_prev, dng = _rmsnorm_bwd(x, ng, dh, dx, name=f"{tag}_normbwd")
    return dx_prev, dng, d_win, fold(dgq), fold(dgk), d_wout


N_CHIPS = 4
HBM = pl.BlockSpec(memory_space=pl.ANY)


def _mesh_pos():
    return lax.axis_index("x"), lax.axis_index("y"), lax.axis_index("c")


def _other_chips(x, y):
    return [(1 - x, y), (x, 1 - y), (1 - x, 1 - y)]


def _chip_exchange(srcs, *, send_slot_is_dest, copy_own, name):
    n = len(srcs)

    def body(*refs):
        src_refs, out_refs = refs[:n], refs[n:2 * n]
        send_sems, recv_sems, local_sems = refs[2 * n:]
        x, y, c = _mesh_pos()
        me = 2 * x + y
        chips = _other_chips(x, y)
        local = []
        for a in range(n):
            if not copy_own[a]:
                continue
            own = src_refs[a].at[me] if send_slot_is_dest else src_refs[a]
            local.append(pltpu.make_async_copy(own, out_refs[a].at[me], local_sems.at[a]))
        for cp in local:
            cp.start()

        def copy(a, k, landing_slot):
            px, py = chips[k]
            src = src_refs[a].at[2 * px + py] if send_slot_is_dest else src_refs[a]
            return pltpu.make_async_remote_copy(
                src_ref=src, dst_ref=out_refs[a].at[landing_slot],
                send_sem=send_sems.at[a * 3 + k], recv_sem=recv_sems.at[a * 3 + k],
                device_id=(px, py, c), device_id_type=MESH)

        sends = [copy(a, k, me) for a in range(n) for k in range(3)]
        for cp in sends:
            cp.start()
        for a in range(n):
            for k in range(3):
                px, py = chips[k]
                copy(a, k, 2 * px + py).wait_recv()
        for cp in sends:
            cp.wait_send()
        for cp in local:
            cp.wait()

    outs = []
    for s in srcs:
        shape = s.shape if send_slot_is_dest else (N_CHIPS,) + s.shape
        outs.append(jax.ShapeDtypeStruct(shape, s.dtype))
    return pl.pallas_call(
        body, name=name, in_specs=[HBM] * n, out_specs=[HBM] * n, out_shape=outs,
        scratch_shapes=[pltpu.SemaphoreType.DMA((3 * n,)), pltpu.SemaphoreType.DMA((3 * n,)),
                        pltpu.SemaphoreType.DMA((n,))],
    )(*srcs)


def _sibling_exchange(srcs, *, name):
    n = len(srcs)

    def body(*refs):
        src_refs, out_refs = refs[:n], refs[n:2 * n]
        send_sems, recv_sems = refs[2 * n:]
        x, y, c = _mesh_pos()
        copies = [pltpu.make_async_remote_copy(
            src_ref=src_refs[a], dst_ref=out_refs[a], send_sem=send_sems.at[a], recv_sem=recv_sems.at[a],
            device_id=(x, y, 1 - c), device_id_type=MESH) for a in range(n)]
        for cp in copies:
            cp.start()
        for cp in copies:
            cp.wait()

    return pl.pallas_call(
        body, name=name, in_specs=[HBM] * n, out_specs=[HBM] * n,
        out_shape=[jax.ShapeDtypeStruct(s.shape, s.dtype) for s in srcs],
        scratch_shapes=[pltpu.SemaphoreType.DMA((n,)), pltpu.SemaphoreType.DMA((n,))],
    )(*srcs)


def _gather_halves(shards, small, *, name):
    n = len(shards)

    def body(*refs):
        s_refs, small_ref = refs[:n], refs[n]
        o_refs, osmall_ref = refs[n + 1:2 * n + 1], refs[2 * n + 1]
        send_sems, recv_sems, local_sems = refs[2 * n + 2:]
        x, y, c = _mesh_pos()
        me = 2 * x + y
        chips = _other_chips(x, y)
        local = [pltpu.make_async_copy(small_ref, osmall_ref.at[me], local_sems.at[0])]
        for cp in local:
            cp.start()

        def over_ici(a, k, slot):
            px, py = chips[k]
            return pltpu.make_async_remote_copy(
                src_ref=s_refs[a].at[c], dst_ref=o_refs[a].at[slot, c], send_sem=send_sems.at[3 * a + k],
                recv_sem=recv_sems.at[3 * a + k], device_id=(px, py, c), device_id_type=MESH)

        def small_copy(k, slot):
            px, py = chips[k]
            return pltpu.make_async_remote_copy(
                src_ref=small_ref, dst_ref=osmall_ref.at[slot], send_sem=send_sems.at[3 * n + k],
                recv_sem=recv_sems.at[3 * n + k], device_id=(px, py, c), device_id_type=MESH)

        def to_sibling(a, k, half):
            px, py = chips[k]
            blk = o_refs[a].at[2 * px + py, half]
            return pltpu.make_async_remote_copy(
                src_ref=blk, dst_ref=blk, send_sem=send_sems.at[3 * n + 3 + 3 * a + k],
                recv_sem=recv_sems.at[3 * n + 3 + 3 * a + k], device_id=(x, y, 1 - c), device_id_type=MESH)

        sends = [over_ici(a, k, me) for a in range(n) for k in range(3)] + [small_copy(k, me) for k in range(3)]
        for cp in sends:
            cp.start()
        passed = []
        for a in range(n):
            for k in range(3):
                px, py = chips[k]
                over_ici(a, k, 2 * px + py).wait_recv()
                passed.append(to_sibling(a, k, c))
                passed[-1].start()
        for k in range(3):
            px, py = chips[k]
            small_copy(k, 2 * px + py).wait_recv()
        for a in range(n):
            for k in range(3):
                to_sibling(a, k, 1 - c).wait_recv()
        for cp in sends + passed:
            cp.wait_send()
        for cp in local:
            cp.wait()

    nsem = 6 * n + 3
    return pl.pallas_call(
        body, name=name, in_specs=[HBM] * (n + 1), out_specs=[HBM] * (n + 1),
        out_shape=[jax.ShapeDtypeStruct((N_CHIPS,) + s.shape, s.dtype) for s in shards + [small]],
        scratch_shapes=[pltpu.SemaphoreType.DMA((nsem,)), pltpu.SemaphoreType.DMA((nsem,)),
                        pltpu.SemaphoreType.DMA((1,))],
    )(*shards, small)


def _forward_halves(landed, *, name):
    n = len(landed)

    def body(*refs):
        o_refs = refs[n:2 * n]
        send_sems, recv_sems = refs[2 * n:]
        x, y, c = _mesh_pos()
        chips = _other_chips(x, y)
        pairs = [(a, k) for a in range(n) for k in range(3)]

        def copy(a, k, half):
            px, py = chips[k]
            blk = o_refs[a].at[2 * px + py, half]
            return pltpu.make_async_remote_copy(
                src_ref=blk, dst_ref=blk, send_sem=send_sems.at[3 * a + k], recv_sem=recv_sems.at[3 * a + k],
                device_id=(x, y, 1 - c), device_id_type=MESH)

        sends = [copy(a, k, c) for a, k in pairs]
        for cp in sends:
            cp.start()
        for a, k in pairs:
            copy(a, k, 1 - c).wait_recv()
        for cp in sends:
            cp.wait_send()

    return pl.pallas_call(
        body, name=name, in_specs=[HBM] * n, out_specs=[HBM] * n,
        out_shape=[jax.ShapeDtypeStruct(a.shape, a.dtype) for a in landed],
        input_output_aliases={a: a for a in range(n)},
        scratch_shapes=[pltpu.SemaphoreType.DMA((3 * n,)), pltpu.SemaphoreType.DMA((3 * n,))],
    )(*landed)


def _swap_other_half(g_list, *, name):
    n = len(g_list)

    def body(*refs):
        g_refs, o_refs = refs[:n], refs[n:2 * n]
        send_sems, recv_sems = refs[2 * n:]
        x, y, c = _mesh_pos()
        copies = [pltpu.make_async_remote_copy(
            src_ref=g_refs[a].at[:, 1 - c], dst_ref=o_refs[a], send_sem=send_sems.at[a], recv_sem=recv_sems.at[a],
            device_id=(x, y, 1 - c), device_id_type=MESH) for a in range(n)]
        for cp in copies:
            cp.start()
        for cp in copies:
            cp.wait()

    return pl.pallas_call(
        body, name=name, in_specs=[HBM] * n, out_specs=[HBM] * n,
        out_shape=[jax.ShapeDtypeStruct((g.shape[0],) + g.shape[2:], g.dtype) for g in g_list],
        scratch_shapes=[pltpu.SemaphoreType.DMA((n,)), pltpu.SemaphoreType.DMA((n,))],
    )(*g_list)


def _row_tile(r):
    return _pick(r, (512, 256, 128, 64, 32, 16, 8))


def _add_my_half(g4, sib4, core, *, name):
    n, _, r, C = g4.shape
    tr = _row_tile(r)

    def body(core_ref, g_ref, s_ref, o_ref):
        del core_ref
        o_ref[...] = (g_ref[...].astype(F32) + s_ref[...].astype(F32)).astype(o_ref.dtype)

    return pl.pallas_call(
        body, name=name,
        grid_spec=pltpu.PrefetchScalarGridSpec(
            num_scalar_prefetch=1, grid=(n, r // tr),
            in_specs=[pl.BlockSpec((None, None, tr, C), lambda j, i, core_ref: (j, core_ref[0], i, 0)),
                      pl.BlockSpec((None, tr, C), lambda j, i, core_ref: (j, i, 0))],
            out_specs=pl.BlockSpec((None, tr, C), lambda j, i, core_ref: (j, i, 0))),
        out_shape=jax.ShapeDtypeStruct((n, r, C), g4.dtype),
        compiler_params=pltpu.CompilerParams(dimension_semantics=("parallel", "parallel")),
    )(core, g4, sib4)


def _scatter_to_chips(p_list, *, name):
    n = len(p_list)

    def body(*refs):
        p_refs, o_refs = refs[:n], refs[n:2 * n]
        send_sems, recv_sems = refs[2 * n:]
        x, y, c = _mesh_pos()
        me = 2 * x + y
        chips = _other_chips(x, y)
        pairs = [(a, k) for a in range(n) for k in range(3)]

        def copy(a, k, landing_slot):
            px, py = chips[k]
            return pltpu.make_async_remote_copy(
                src_ref=p_refs[a].at[2 * px + py], dst_ref=o_refs[a].at[landing_slot],
                send_sem=send_sems.at[3 * a + k], recv_sem=recv_sems.at[3 * a + k], device_id=(px, py, c),
                device_id_type=MESH)

        sends = [copy(a, k, me) for a, k in pairs]
        for cp in sends:
            cp.start()
        for a, k in pairs:
            px, py = chips[k]
            copy(a, k, 2 * px + py).wait_recv()
        for cp in sends:
            cp.wait_send()

    return pl.pallas_call(
        body, name=name, in_specs=[HBM] * n, out_specs=[HBM] * n,
        out_shape=[jax.ShapeDtypeStruct(p.shape, p.dtype) for p in p_list],
        scratch_shapes=[pltpu.SemaphoreType.DMA((3 * n,)), pltpu.SemaphoreType.DMA((3 * n,))],
    )(*p_list)


def _sum_chips(landed, part, me, *, name):
    _, r, C = landed.shape
    tr = _row_tile(r)

    def body(me_ref, own_ref, r1_ref, r2_ref, r3_ref, o_ref):
        del me_ref
        f = lambda ref: ref[...].astype(F32)
        o_ref[...] = ((f(own_ref) + f(r1_ref)) + f(r2_ref)) + f(r3_ref)

    slot = lambda d: pl.BlockSpec((None, tr, C), lambda i, me_ref: ((me_ref[0] + d) % N_CHIPS, i, 0))
    return pl.pallas_call(
        body, name=name,
        grid_spec=pltpu.PrefetchScalarGridSpec(
            num_scalar_prefetch=1, grid=(r // tr,), in_specs=[slot(0), slot(1), slot(2), slot(3)],
            out_specs=pl.BlockSpec((tr, C), lambda i, me_ref: (i, 0))),
        out_shape=jax.ShapeDtypeStruct((r, C), F32),
        compiler_params=pltpu.CompilerParams(dimension_semantics=("parallel",)),
    )(me, part, landed, landed, landed)


def _adamw_halves(w, mine, theirs, m, v, core, *, layer, prev, name):
    shape = w.shape
    r, C = mine.shape
    tr = _pick(r, (128, 64, 32, 16, 8))
    per = r // tr
    view = lambda a: a.reshape(-1, C)
    n_prev = 0 if prev is None else 4

    def body(*refs):
        core_ref, w_ref, gm_ref, gt_ref, m_ref, v_ref = refs[:6]
        g_ref, d_ref, nm_ref, nv_ref = refs[6 + n_prev:]
        gv = jnp.where(pl.program_id(0) == core_ref[0], gm_ref[...], gt_ref[...])
        g_ref[...] = gv
        d_ref[...], nm_ref[...], nv_ref[...] = _adamw_math(w_ref[...], gv, m_ref[...], v_ref[...])

    half = pl.BlockSpec((tr, C), lambda h, i, core_ref: ((2 * layer + h) * per + i, 0))
    row = pl.BlockSpec((tr, C), lambda h, i, core_ref: (i, 0))
    out = jax.ShapeDtypeStruct((math.prod(shape) // C, C), F32)
    res = pl.pallas_call(
        body, name=name,
        grid_spec=pltpu.PrefetchScalarGridSpec(
            num_scalar_prefetch=1, grid=(2, per), in_specs=[half, row, row, half, half] + [HBM] * n_prev,
            out_specs=[half] * 4),
        out_shape=[out] * 4,
        input_output_aliases={6 + j: j for j in range(n_prev)},
        compiler_params=pltpu.CompilerParams(dimension_semantics=("parallel", "parallel")),
    )(core, view(w), mine, theirs, view(m), view(v), *([] if prev is None else [view(a) for a in prev]))
    return tuple(a.reshape(shape) for a in res)


def _sum_small(recv4, *, name):
    _, R, C = recv4.shape

    def body(r_ref, o_ref):
        o_ref[...] = ((r_ref[0] + r_ref[1]) + r_ref[2]) + r_ref[3]

    return pl.pallas_call(body, name=name, out_shape=jax.ShapeDtypeStruct((R, C), F32))(recv4)


def _add(a, b, *, name):
    R, C = a.shape
    tr = _pick(R, (512, 256, 128, 64, 32, 16, 8))
    blk = pl.BlockSpec((tr, C), lambda i: (i, 0))

    def body(a_ref, b_ref, o_ref):
        o_ref[...] = a_ref[...] + b_ref[...]

    return pl.pallas_call(body, name=name, grid=(R // tr,), in_specs=[blk, blk], out_specs=blk,
                          out_shape=jax.ShapeDtypeStruct((R, C), F32),
                          compiler_params=pltpu.CompilerParams(dimension_semantics=("parallel",)))(a, b)


def _adamw_math(w, g, m, v):
    nm = ADAM_B1 * m + (1.0 - ADAM_B1) * g
    nv = ADAM_B2 * v + (1.0 - ADAM_B2) * (g * g)
    m_hat = nm / (1.0 - ADAM_B1 ** ADAM_STEP)
    v_hat = nv / (1.0 - ADAM_B2 ** ADAM_STEP)
    return -ADAM_LR * (m_hat / (jnp.sqrt(v_hat) + ADAM_EPS) + ADAM_WD * w), nm, nv


def _adamw(w, g, m, v, *, name):
    shape = w.shape
    C = shape[-1]
    R = w.size // C
    two = lambda a: a.reshape(R, C)
    tr = _pick(R, (256, 128, 64, 32, 16, 8)) if R % 8 == 0 and R > 8 else R
    blk = pl.BlockSpec((tr, C), lambda i: (i, 0))

    def body(w_ref, g_ref, m_ref, v_ref, d_ref, nm_ref, nv_ref):
        d_ref[...], nm_ref[...], nv_ref[...] = _adamw_math(w_ref[...], g_ref[...], m_ref[...], v_ref[...])

    out = jax.ShapeDtypeStruct((R, C), F32)
    d, nm, nv = pl.pallas_call(
        body, name=name, grid=(R // tr,), in_specs=[blk] * 4, out_specs=[blk] * 3, out_shape=[out] * 3,
        compiler_params=pltpu.CompilerParams(dimension_semantics=("parallel",)),
    )(two(w), two(g), two(m), two(v))
    return d.reshape(shape), nm.reshape(shape), nv.reshape(shape)


BIG = (("dn_w_in", (2, 1024, 1540), 2), ("dn_w_out", (2, 512, 1024), 1), ("sb_w_in", (1, 1024, 1024), 2),
       ("sb_w_out", (1, 256, 1024), 1), ("sc_w_in", (1, 1024, 2048), 2), ("sc_w_out", (1, 512, 1024), 1))
SMALL = (("dn_conv_w", (2, 4, 1024), 2), ("dn_o_norm_g", (2, 64), 1), ("sc_conv_w", (1, 3, 512), 2))
REPL = (("norm_g", (4, 1024)), ("dn_a_log", (2, 8)), ("dn_dt_bias", (2, 8)), ("sb_q_norm_g", (1, 64)),
        ("sb_k_norm_g", (1, 64)))


def _halves(shard):
    return shard.reshape(2, -1, shard.shape[-1])


def _pack(arrays, cols, lead=()):
    flat = jnp.concatenate([a.reshape(lead + (-1,)) for a in arrays], axis=-1)
    n = flat.shape[-1]
    rows = -(-n // cols)
    unit = 512 if rows > 512 else 8
    rows = -(-rows // unit) * unit
    flat = jnp.pad(flat, [(0, 0)] * len(lead) + [(0, rows * cols - n)])
    return flat.reshape(lead + (rows, cols))


def _unpack(buf, table, lead=()):
    flat = buf.reshape(lead + (-1,))
    out, off = {}, 0
    for entry in table:
        name, shape = entry[0], entry[1]
        n = math.prod(shape)
        out[name] = flat[..., off:off + n].reshape(lead + shape)
        off += n
    return out


def _join(shards, axis):
    return jnp.concatenate([shards[j] for j in range(N_CHIPS)], axis=axis)


def _split(full, axis):
    return jnp.stack(jnp.split(full, N_CHIPS, axis=axis), axis=0)


def kernel(x, norm_g, dn_w_in, dn_conv_w, dn_a_log, dn_dt_bias, dn_o_norm_g, dn_w_out, sb_w_in, sb_q_norm_g, sb_k_norm_g, sb_w_out, sc_w_in, sc_conv_w, sc_w_out, loss_target, m_norm_g, m_dn_w_in, m_dn_conv_w, m_dn_a_log, m_dn_dt_bias, m_dn_o_norm_g, m_dn_w_out, m_sb_w_in, m_sb_q_norm_g, m_sb_k_norm_g, m_sb_w_out, m_sc_w_in, m_sc_conv_w, m_sc_w_out, v_norm_g, v_dn_w_in, v_dn_conv_w, v_dn_a_log, v_dn_dt_bias, v_dn_o_norm_g, v_dn_w_out, v_sb_w_in, v_sb_q_norm_g, v_sb_k_norm_g, v_sb_w_out, v_sc_w_in, v_sc_conv_w, v_sc_w_out):
    weights = dict(norm_g=norm_g, dn_w_in=dn_w_in, dn_conv_w=dn_conv_w, dn_a_log=dn_a_log, dn_dt_bias=dn_dt_bias,
                   dn_o_norm_g=dn_o_norm_g, dn_w_out=dn_w_out, sb_w_in=sb_w_in, sb_q_norm_g=sb_q_norm_g,
                   sb_k_norm_g=sb_k_norm_g, sb_w_out=sb_w_out, sc_w_in=sc_w_in, sc_conv_w=sc_conv_w, sc_w_out=sc_w_out)
    m_in = dict(norm_g=m_norm_g, dn_w_in=m_dn_w_in, dn_conv_w=m_dn_conv_w, dn_a_log=m_dn_a_log,
                dn_dt_bias=m_dn_dt_bias, dn_o_norm_g=m_dn_o_norm_g, dn_w_out=m_dn_w_out, sb_w_in=m_sb_w_in,
                sb_q_norm_g=m_sb_q_norm_g, sb_k_norm_g=m_sb_k_norm_g, sb_w_out=m_sb_w_out, sc_w_in=m_sc_w_in,
                sc_conv_w=m_sc_conv_w, sc_w_out=m_sc_w_out)
    v_in = dict(norm_g=v_norm_g, dn_w_in=v_dn_w_in, dn_conv_w=v_dn_conv_w, dn_a_log=v_dn_a_log,
                dn_dt_bias=v_dn_dt_bias, dn_o_norm_g=v_dn_o_norm_g, dn_w_out=v_dn_w_out, sb_w_in=v_sb_w_in,
                sb_q_norm_g=v_sb_q_norm_g, sb_k_norm_g=v_sb_k_norm_g, sb_w_out=v_sb_w_out, sc_w_in=v_sc_w_in,
                sc_conv_w=v_sc_conv_w, sc_w_out=v_sc_w_out)
    order = list(weights)
    xi, yi, ci = _mesh_pos()

    small = _pack([weights[n] for n, _, _ in SMALL], LANES)
    later = [("dn_w_in", 1), ("dn_w_out", 1), ("sb_w_in", 0), ("sb_w_out", 0), ("sc_w_in", 0), ("sc_w_out", 0)]
    piece = lambda n, l: _halves(weights[n][l].astype(BF16)[None])
    own_first = [piece("dn_w_in", 0), piece("dn_w_out", 0)]
    own_later = [piece(n, l) for n, l in later]
    own_last, own_mid = own_later[:2], own_later[2:]
    me = 2 * xi + yi
    whole = lambda g4, own: lax.dynamic_update_index_in_dim(g4, own, me, 0)
    flat = lambda g4: g4.reshape(N_CHIPS, -1, g4.shape[-1])
    rows_of = lambda w4: w4.reshape(-1, w4.shape[-1])
    dn_in = lambda w4: jnp.pad(_join(w4, 1), ((0, 0), (0, DN_IN_PAD - DN_IN)))
    w_in0, w_out0, small4 = _gather_halves(own_first, small, name="gather_first")
    full = {n: _join(a, ax) for (n, _, ax), a in zip(SMALL, _unpack(small4, SMALL, (N_CHIPS,)).values())}

    def dn_args(j, w_in4, w_out4):
        return (dn_in(flat(w_in4)), full["dn_conv_w"][j], dn_a_log[j], dn_dt_bias[j], full["dn_o_norm_g"][j],
                rows_of(w_out4))

    x0 = x[0]
    dn0 = dn_args(0, whole(w_in0, own_first[0]), whole(w_out0, own_first[1]))
    x1, s0, landed = _dn_layer_fwd(x0, norm_g[0], *dn0, "l0", send=own_mid)
    landed = _forward_halves(landed, name="forward_halves_mid")
    sb_in, sb_out, sc_in, sc_out = [whole(g4, own) for g4, own in zip(landed, own_mid)]
    sb_args = (flat(sb_in), sb_q_norm_g[0], sb_k_norm_g[0], rows_of(sb_out))
    sc_args = (flat(sc_in), full["sc_conv_w"][0], rows_of(sc_out))
    x2, s1, landed = _sb_layer_fwd(x1, norm_g[1], *sb_args, "l1", send=own_last)
    landed = _forward_halves(landed, name="forward_halves_last")
    dn1 = dn_args(1, *[whole(g4, own) for g4, own in zip(landed, own_last)])
    x3, s2 = _sc_layer_fwd(x2, norm_g[2], *sc_args, "l2")
    x4, s3, _ = _dn_layer_fwd(x3, norm_g[3], *dn1, "l3")
    dy, loss_local = _loss_head(x4, loss_target[0], name="loss_head")
    loss = lax.psum(loss_local[0, 0], ("x", "y", "c"))

    by_cols = lambda dw: _split(dw[:, :DN_IN].astype(BF16), 1)
    by_rows = lambda dw: dw.reshape(N_CHIPS, -1, dw.shape[-1])
    cut2 = lambda g4: g4.reshape(N_CHIPS, 2, -1, g4.shape[-1])
    core = ci.astype(jnp.int32).reshape(1)
    chip = me.astype(jnp.int32).reshape(1)

    def chip_sums(g_list, tag):
        sib = _swap_other_half(g_list, name=f"swap_halves_{tag}")
        return [_add_my_half(g, s, core, name=f"sum_cores_{tag}{i}") for i, (g, s) in enumerate(zip(g_list, sib))]

    dx3, dng3, dwin3, dconv3, dal3, ddt3, dgain3, dwout3, _ = _dn_layer_bwd(dy, x3, norm_g[3], *dn1, s3, "l3")
    dx2, dng2, dwin2, dconv2, dwout2 = _sc_layer_bwd(dx3, x2, norm_g[2], *sc_args, s2, "l2")
    dx1, dng1, dwin1, dgq, dgk, dwout1 = _sb_layer_bwd(dx2, x1, norm_g[1], *sb_args, s1, "l1")
    part_later = chip_sums([cut2(by_cols(dwin3)), cut2(by_rows(dwout3)), cut2(dwin1), cut2(by_rows(dwout1)),
                            cut2(dwin2), cut2(by_rows(dwout2))], "later")
    dx0, dng0, dwin0, dconv0, dal0, ddt0, dgain0, dwout0, landed_later = _dn_layer_bwd(
        dx1, x0, norm_g[0], *dn0, s0, "l0", send=part_later)
    part_first = chip_sums([cut2(by_cols(dwin0)), cut2(by_rows(dwout0))], "first")
    landed_first = _scatter_to_chips(part_first, name="scatter_first")
    pieces = [("dn_w_in", 0), ("dn_w_out", 0)] + later
    mine = [_sum_chips(r, p, chip, name=f"sum_chips_{n}{l}")
            for (n, l), r, p in zip(pieces, list(landed_first) + list(landed_later), part_first + part_later)]
    theirs = _sibling_exchange(mine, name="swap_results")
    upd = {}
    for (n, l), a, b in zip(pieces, mine, theirs):
        upd[n] = _adamw_halves(weights[n], a, b, m_in[n], v_in[n], core, layer=l, prev=upd.get(n),
                               name=f"adamw_{n}{l}")
    g_out = {n: upd[n][0] for n, _, _ in BIG}

    grads = dict(
        norm_g=jnp.concatenate([dng0, dng1, dng2, dng3], axis=0), dn_conv_w=jnp.stack([dconv0, dconv3]),
        dn_a_log=jnp.stack([dal0, dal3]), dn_dt_bias=jnp.stack([ddt0, ddt3]),
        dn_o_norm_g=jnp.stack([dgain0, dgain3]), sb_q_norm_g=dgq[None], sb_k_norm_g=dgk[None],
        sc_conv_w=dconv2[None])
    repl = [jnp.broadcast_to(grads[n][None], (N_CHIPS,) + s) for n, s in REPL]
    gsmall = _pack([_split(grads[n], ax) for n, _, ax in SMALL] + repl, LANES, (N_CHIPS,))
    rsmall, = _chip_exchange([gsmall], send_slot_is_dest=True, copy_own=(True,), name="scatter_small")
    psmall = _sum_small(rsmall, name="sum_chips_small")
    qsmall, = _sibling_exchange([psmall], name="swap_cores_small")
    tsmall = _add(psmall, qsmall, name="sum_cores_small")
    g_out.update(_unpack(tsmall, SMALL + REPL))

    for n in order:
        if n not in upd:
            upd[n] = (g_out[n],) + _adamw(weights[n], g_out[n], m_in[n], v_in[n], name=f"adamw_{n}")
    return (loss, dx0[None], *[upd[n][0] for n in order], *[upd[n][1] for n in order],
            *[upd[n][2] for n in order], *[upd[n][3] for n in order])
```

```python
import functools
import math

import jax
import jax.numpy as jnp
from jax import lax
from jax.experimental import pallas as pl
from jax.experimental.pallas import tpu as pltpu

F32 = jnp.float32
BF16 = jnp.bfloat16
MESH = pl.DeviceIdType.MESH

RMS_EPS = 1e-6
L2_EPS = 1e-6
LANES = 128
VMEM_BIG = 60 * 1024 * 1024
MM_VMEM = 44 * 1024 * 1024

DN_HEADS, DN_DK, DN_DV, DN_CHUNK, DN_CONV = 8, 128, 256, 64, 4
DN_QK_W = DN_HEADS * DN_DK
DN_V_W = DN_HEADS * DN_DV
DN_CONV_W = 2 * DN_QK_W + DN_V_W
DN_IN = DN_CONV_W + DN_V_W + 2 * DN_HEADS
DN_IN_PAD = DN_CONV_W + DN_V_W + LANES
SB_DH = 64
SC_CONV = 3

ADAM_LR, ADAM_B1, ADAM_B2, ADAM_EPS, ADAM_WD, ADAM_STEP = 0.001, 0.9, 0.999, 1e-08, 0.01, 10


def _pick(n, cands):
    for c in cands:
        if n % c == 0:
            return c
    raise ValueError(f"no tile for {n} in {cands}")


def _bf(x):
    return x.astype(BF16)


def _dot(a, b):
    return jnp.dot(_bf(a), _bf(b), preferred_element_type=F32)


def _dot_nt(a, b):
    return lax.dot_general(_bf(a), _bf(b), (((1,), (1,)), ((), ())), preferred_element_type=F32)


def _dot_tn(a, b):
    return lax.dot_general(_bf(a), _bf(b), (((0,), (0,)), ((), ())), preferred_element_type=F32)


def _split3(a):
    hi = _bf(a)
    r = a - hi.astype(F32)
    mid = _bf(r)
    lo = _bf(r - mid.astype(F32))
    return hi, mid, lo


def _sigmoid(x):
    return 1.0 / (1.0 + jnp.exp(-x))


def _silu(x):
    return x * _sigmoid(x)


def _dsilu(x):
    s = _sigmoid(x)
    return s * (1.0 + x * (1.0 - s))


def _softplus(x):
    return jnp.maximum(x, 0.0) + jnp.log(1.0 + jnp.exp(-jnp.abs(x)))


def _shift_down(z, k):
    if k == 0:
        return z
    row = lax.broadcasted_iota(jnp.int32, z.shape, 0)
    return jnp.where(row >= k, pltpu.roll(z, k, 0), 0.0)


def _shift_up(z, k):
    if k == 0:
        return z
    n = z.shape[0]
    row = lax.broadcasted_iota(jnp.int32, z.shape, 0)
    return jnp.where(row < n - k, pltpu.roll(z, n - k, 0), 0.0)


def _matmul(a, b, *, mode, name, res=None, a_parts=1, b_parts=1, out_parts=1, out_dtype=F32, send=()):
    def dims2(x, parts):
        if parts == 1:
            return x.shape
        assert x.shape[0] == parts
        return (x.shape[1], x.shape[2] * parts)

    ash, bsh = dims2(a, a_parts), dims2(b, b_parts)
    if mode == "nn":
        (M, K), (K2, N) = ash, bsh
        dn = (((1,), (0,)), ((), ()))
    elif mode == "nt":
        (M, K), (N, K2) = ash, bsh
        dn = (((1,), (1,)), ((), ()))
    else:
        (K, M), (K2, N) = ash, bsh
        dn = (((0,), (0,)), ((), ()))
    assert K == K2, (ash, bsh, mode)
    tm_max = _pick(M, (512, 256, 128, 64, 32, 16, 8))
    n_unit = N // max(out_parts, b_parts if mode != "nt" else 1)
    k_unit = K // max(a_parts if mode != "tn" else 1, b_parts if mode == "nt" else 1)
    tm, tn, tk = min(
        ((m, n, k) for m in {tm_max, max(tm_max // 2, 8)}
         for n in (n_unit, 2048, 1792, 1024, 896, 768, 512, 384, 256, 128) if n_unit % n == 0
         for k in (k_unit, 2048, 1792, 1024, 896, 512, 256, 128) if k_unit % k == 0
         if 2 * (m * k * a.dtype.itemsize + k * n * b.dtype.itemsize + 2 * m * n * 4) + m * n * 4 <= MM_VMEM),
        key=lambda t: (-t[0] * t[1] * t[2], -t[0], -t[2]))
    nk = K // tk
    grid = (M // tm, N // tn, nk)

    def spec(parts, rows_are, cols_are, tr, tc, width):
        per = width // parts // tc
        if parts == 1:
            return pl.BlockSpec((tr, tc), lambda i, j, k: ((i, j, k)[rows_are], (i, j, k)[cols_are]))
        return pl.BlockSpec((None, tr, tc), lambda i, j, k: ((i, j, k)[cols_are] // per, (i, j, k)[rows_are],
                                                             (i, j, k)[cols_are] % per))

    if mode == "nn":
        a_spec = spec(a_parts, 0, 2, tm, tk, K)
        b_spec = spec(b_parts, 2, 1, tk, tn, N)
    elif mode == "nt":
        a_spec = spec(a_parts, 0, 2, tm, tk, K)
        b_spec = spec(b_parts, 1, 2, tn, tk, K)
    else:
        a_spec = spec(a_parts, 2, 0, tk, tm, M)
        b_spec = spec(b_parts, 2, 1, tk, tn, N)
    o_spec = spec(out_parts, 0, 1, tm, tn, N)
    in_specs = [a_spec, b_spec]
    operands = [a, b]
    if res is not None:
        in_specs.append(pl.BlockSpec((tm, tn), lambda i, j, k: (i, j)))
        operands.append(res)

    n_in = len(operands)
    ns = len(send)

    def finish(refs, r):
        if res is not None:
            r = refs[2][...] + r
        refs[n_in + ns][...] = r.astype(out_dtype)

    def body(*refs):
        if ns:
            at = lambda step: functools.reduce(jnp.logical_and, [pl.program_id(d) == step[d] for d in range(3)])
            _blocks_over_ici(refs[n_in:n_in + ns], refs[n_in + ns + 1:n_in + 2 * ns + 1], refs[-2], refs[-1],
                             at((0, 0, 0)), at(tuple(g - 1 for g in grid)))
        part = lax.dot_general(_bf(refs[0][...]), _bf(refs[1][...]), dn, preferred_element_type=F32)
        if nk == 1:
            finish(refs, part)
            return
        acc_ref = refs[n_in + 2 * ns + 1]
        k = pl.program_id(2)

        @pl.when(k == 0)
        def _():
            acc_ref[...] = part

        @pl.when(jnp.logical_and(k > 0, k < nk - 1))
        def _():
            acc_ref[...] += part

        @pl.when(k == nk - 1)
        def _():
            finish(refs, acc_ref[...] + part)

    out_shape = (M, N) if out_parts == 1 else (out_parts, M, N // out_parts)
    out = pl.pallas_call(
        body, name=name, grid=grid, in_specs=in_specs + [HBM] * ns, out_specs=[o_spec] + [HBM] * ns,
        out_shape=[jax.ShapeDtypeStruct(out_shape, out_dtype)] + [jax.ShapeDtypeStruct(x.shape, x.dtype) for x in send],
        scratch_shapes=([pltpu.VMEM((tm, tn), F32)] if nk > 1 else [])
        + ([pltpu.SemaphoreType.DMA((3 * ns,)), pltpu.SemaphoreType.DMA((3 * ns,))] if ns else []),
        compiler_params=pltpu.CompilerParams(
            dimension_semantics=("arbitrary",) * 3 if ns else ("parallel", "parallel", "arbitrary"),
            vmem_limit_bytes=VMEM_BIG),
    )(*operands, *send)
    return out if ns else out[0]


def _rmsnorm_fwd(x, g, *, name):
    T, D = x.shape
    tm = _pick(T, (512, 256, 128, 64, 32, 16))

    def body(x_ref, g_ref, h_ref):
        xv = x_ref[...]
        r = lax.rsqrt(jnp.mean(xv * xv, axis=-1, keepdims=True) + RMS_EPS)
        h_ref[...] = ((xv * r) * g_ref[...]).astype(BF16)

    return pl.pallas_call(
        body, name=name, grid=(T // tm,),
        in_specs=[pl.BlockSpec((tm, D), lambda i: (i, 0)), pl.BlockSpec((1, D), lambda i: (0, 0))],
        out_specs=pl.BlockSpec((tm, D), lambda i: (i, 0)),
        out_shape=jax.ShapeDtypeStruct((T, D), BF16),
    )(x, g.reshape(1, D))


def _rmsnorm_bwd(x, g, dh, dx_in, *, name):
    T, D = x.shape
    tm = _pick(T, (512, 256, 128, 64, 32, 16))

    def body(x_ref, g_ref, dh_ref, dxin_ref, dx_ref, dg_ref):
        @pl.when(pl.program_id(0) == 0)
        def _():
            dg_ref[...] = jnp.zeros_like(dg_ref)

        xv = x_ref[...]
        r = lax.rsqrt(jnp.mean(xv * xv, axis=-1, keepdims=True) + RMS_EPS)
        xh = xv * r
        dh_v = dh_ref[...]
        dxh = dh_v * g_ref[...]
        dx_ref[...] = dxin_ref[...] + r * (dxh - xh * jnp.mean(dxh * xh, axis=-1, keepdims=True))
        dg_ref[...] += jnp.sum(dh_v * xh, axis=0, keepdims=True)

    row = pl.BlockSpec((tm, D), lambda i: (i, 0))
    vec = pl.BlockSpec((1, D), lambda i: (0, 0))
    return pl.pallas_call(
        body, name=name, grid=(T // tm,),
        in_specs=[row, vec, row, row], out_specs=[row, vec],
        out_shape=[jax.ShapeDtypeStruct((T, D), F32), jax.ShapeDtypeStruct((1, D), F32)],
        compiler_params=pltpu.CompilerParams(dimension_semantics=("arbitrary",)),
    )(x, g.reshape(1, D), dh, dx_in)


def _loss_head(y, target, *, name):
    T, D = y.shape
    tm = _pick(T, (512, 256, 128, 64, 32, 16))

    def body(y_ref, t_ref, dy_ref, l_ref):
        @pl.when(pl.program_id(0) == 0)
        def _():
            l_ref[...] = jnp.zeros_like(l_ref)

        err = y_ref[...] - t_ref[...]
        dy_ref[...] = err * (1.0 / D)
        l_ref[...] += 0.5 * jnp.sum(jnp.mean(err * err, axis=-1, keepdims=True), axis=0, keepdims=True)

    row = pl.BlockSpec((tm, D), lambda i: (i, 0))
    return pl.pallas_call(
        body, name=name, grid=(T // tm,),
        in_specs=[row, row], out_specs=[row, pl.BlockSpec((1, 1), lambda i: (0, 0))],
        out_shape=[jax.ShapeDtypeStruct((T, D), F32), jax.ShapeDtypeStruct((1, 1), F32)],
        compiler_params=pltpu.CompilerParams(dimension_semantics=("arbitrary",)),
    )(y, target)


def _sc_mid_fwd(p3, conv_w, *, name):
    _, T, W = p3.shape
    K = conv_w.shape[0]
    cw = LANES

    def body(p_ref, w_ref, o_ref):
        z = p_ref[1] * p_ref[2]
        cv = sum(w_ref[i:i + 1, :] * _shift_down(z, K - 1 - i) for i in range(K))
        o_ref[...] = ((p_ref[0] * cv) * _silu(p_ref[3])).astype(BF16)

    return pl.pallas_call(
        body, name=name, grid=(W // cw,),
        in_specs=[pl.BlockSpec((4, T, cw), lambda j: (0, 0, j)), pl.BlockSpec((K, cw), lambda j: (0, j))],
        out_specs=pl.BlockSpec((T, cw), lambda j: (0, j)),
        out_shape=jax.ShapeDtypeStruct((T, W), BF16),
        compiler_params=pltpu.CompilerParams(dimension_semantics=("parallel",), vmem_limit_bytes=VMEM_BIG),
    )(p3, conv_w)


def _sc_mid_bwd(p3, conv_w, do, *, name):
    _, T, W = p3.shape
    K = conv_w.shape[0]
    cw = LANES

    def body(p_ref, w_ref, do_ref, dp_ref, dw_ref):
        b, c, u, gate = p_ref[0], p_ref[1], p_ref[2], p_ref[3]
        z = c * u
        zs = [_shift_down(z, K - 1 - i) for i in range(K)]
        cv = sum(w_ref[i:i + 1, :] * zs[i] for i in range(K))
        y = b * cv
        dov = do_ref[...]
        dy = dov * _silu(gate)
        dp_ref[3] = dov * y * _dsilu(gate)
        dp_ref[0] = dy * cv
        dcv = dy * b
        dz = sum(w_ref[i:i + 1, :] * _shift_up(dcv, K - 1 - i) for i in range(K))
        dp_ref[1] = dz * u
        dp_ref[2] = dz * c
        for i in range(K):
            dw_ref[i:i + 1, :] = jnp.sum(dcv * zs[i], axis=0, keepdims=True)

    return pl.pallas_call(
        body, name=name, grid=(W // cw,),
        in_specs=[pl.BlockSpec((4, T, cw), lambda j: (0, 0, j)), pl.BlockSpec((K, cw), lambda j: (0, j)),
                  pl.BlockSpec((T, cw), lambda j: (0, j))],
        out_specs=[pl.BlockSpec((4, T, cw), lambda j: (0, 0, j)), pl.BlockSpec((K, cw), lambda j: (0, j))],
        out_shape=[jax.ShapeDtypeStruct((4, T, W), F32), jax.ShapeDtypeStruct((K, W), F32)],
        compiler_params=pltpu.CompilerParams(dimension_semantics=("parallel",), vmem_limit_bytes=VMEM_BIG),
    )(p3, conv_w, do)


def _sc_layer_fwd(x, ng, w_in, conv_w, w_out, tag):
    h = _rmsnorm_fwd(x, ng, name=f"{tag}_norm")
    p3 = _matmul(h, w_in, mode="nn", b_parts=4, out_parts=4, name=f"{tag}_inproj")
    og = _sc_mid_fwd(p3, conv_w, name=f"{tag}_mid")
    x_new = _matmul(og, w_out, mode="nn", res=x, name=f"{tag}_outproj")
    return x_new, (h, p3, og)


def _sc_layer_bwd(dx, x, ng, w_in, conv_w, w_out, saved, tag):
    h, p3, og = saved
    d_wout = _matmul(og, dx, mode="tn", out_dtype=BF16, name=f"{tag}_dwout")
    dog = _matmul(dx, w_out, mode="nt", name=f"{tag}_dog")
    dp3, dconv = _sc_mid_bwd(p3, conv_w, dog, name=f"{tag}_midbwd")
    d_win = _matmul(h, dp3, mode="tn", b_parts=4, out_parts=4, out_dtype=BF16, name=f"{tag}_dwin")
    dh = _matmul(dp3, w_in, mode="nt", a_parts=4, b_parts=4, name=f"{tag}_dh")
    dx_prev, dng = _rmsnorm_bwd(x, ng, dh, dx, name=f"{tag}_normbwd")
    return dx_prev, dng, d_win, dconv, d_wout


SB_BQ = 256
SB_BK = 256
SB_ROWS = 512
SB_DEAD = -110.0


def _sb_half_mask():
    return lax.broadcasted_iota(jnp.int32, (1, LANES), 1) < SB_DH


def _sb_headnorm(x, g, lo):
    x2 = x * x
    s_lo = jnp.sum(jnp.where(lo, x2, 0.0), axis=-1, keepdims=True)
    s_hi = jnp.sum(jnp.where(lo, 0.0, x2), axis=-1, keepdims=True)
    r = lax.rsqrt(jnp.where(lo, s_lo, s_hi) * (1.0 / SB_DH) + RMS_EPS)
    xh = x * r
    return xh * g, xh, r


def _dot_x2_l(a_l, b_exact_bf16):
    his = [_bf(a) for a in a_l]
    mids = [_bf(a - h.astype(F32)) for a, h in zip(a_l, his)]
    f = lambda p: jnp.dot(p, b_exact_bf16, preferred_element_type=F32)
    return [x + y for x, y in zip([f(h) for h in his], [f(m) for m in mids])]


def _sb_stack(xb, lo):
    zero = jnp.zeros_like(xb)
    return jnp.concatenate([jnp.where(lo, xb, zero), jnp.where(lo, zero, xb)], axis=0)


def _sb_rel(bq, bk):
    row = lax.broadcasted_iota(jnp.int32, (2 * bq, bk), 0)
    col = lax.broadcasted_iota(jnp.int32, (2 * bq, bk), 1)
    return col - jnp.where(row >= bq, row - bq, row)


def _sb_tile(qm, kb, valid):
    z = lax.dot_general(qm, kb, (((1,), (1,)), ((), ())), preferred_element_type=F32)
    sp = _softplus(z)
    return z - sp, (-sp if valid is None else jnp.where(valid, -sp, 0.0))


def _sb_attn_fwd(p3, gq2, gk2, *, name, send=()):
    _, T, W = p3.shape
    bq, bk = min(SB_BQ, T), min(SB_BK, T)
    rows = min(SB_ROWS, T)
    scale = SB_DH ** -0.5
    ns = len(send)
    npair = W // LANES

    def body(*refs):
        p_ref, gq_ref, gk_ref = refs[:3]
        og_ref, o_ref = refs[3 + ns:5 + ns]
        qn_ref, kn_ref, v_ref = refs[5 + 2 * ns:8 + 2 * ns]
        if ns:
            _halves_over_ici(refs[3:3 + ns], refs[5 + ns:5 + 2 * ns], refs[8 + 2 * ns], refs[9 + 2 * ns],
                             pl.program_id(0) == 0, pl.program_id(0) == npair - 1)
        lo = _sb_half_mask()

        def prologue(i, c):
            r0 = pl.multiple_of(i * rows, rows)
            sl = pl.ds(r0, rows)
            qn_ref[sl, :] = (_sb_headnorm(p_ref[0, sl, :], gq_ref[...], lo)[0] * scale).astype(BF16)
            kn_ref[sl, :] = _sb_headnorm(p_ref[1, sl, :], gk_ref[...], lo)[0].astype(BF16)
            v_ref[sl, :] = p_ref[2, sl, :].astype(BF16)
            return c

        lax.fori_loop(0, T // rows, prologue, 0)

        rel = _sb_rel(bq, bk)
        tri = (lax.broadcasted_iota(jnp.int32, (bk, bk), 0)
               > lax.broadcasted_iota(jnp.int32, (bk, bk), 1)).astype(BF16)

        def qblock(qi, c):
            q0 = pl.multiple_of(qi * bq, bq)
            qm = _sb_stack(qn_ref[pl.ds(q0, bq), :], lo)
            nkb = (q0 + bq - 1) // bk + 1

            def tiles(k0s, carry, valids):
                o_acc, a_carry = carry
                sc = [_sb_tile(qm, kn_ref[pl.ds(k0, bk), :], valid) for k0, valid in zip(k0s, valids)]
                later = _dot_x2_l([log1m for _, log1m in sc], tri)
                for (logsig, log1m), lat, k0, valid in zip(sc, later, k0s, valids):
                    wts = jnp.exp(logsig + (lat + a_carry))
                    if valid is not None:
                        wts = jnp.where(valid, wts, 0.0)
                    o_acc = o_acc + jnp.dot(_bf(wts), v_ref[pl.ds(k0, bk), :], preferred_element_type=F32)
                    a_carry = a_carry + jnp.sum(log1m, axis=-1, keepdims=True)
                return o_acc, a_carry

            blk0 = lambda j: pl.multiple_of(j * bk, bk)
            k_last = blk0(nkb - 1)
            o2, t2 = tiles([k_last, blk0(jnp.maximum(nkb - 2, 0))],
                           (jnp.zeros((2 * bq, LANES), F32), jnp.zeros((2 * bq, 1), F32)),
                           [rel < q0 - k_last, nkb >= 2])

            def alive(st):
                return jnp.logical_and(st[0] < nkb - 1, jnp.max(st[2]) > SB_DEAD)

            def back_one(st):
                return (st[0] + 1,) + tiles([blk0(nkb - 2 - st[0])], st[1:], [None])

            _, o2, _ = lax.while_loop(alive, back_one, (jnp.int32(1), o2, t2))
            o = jnp.where(lo, o2[:bq], o2[bq:])
            o_ref[pl.ds(q0, bq), :] = o
            og_ref[pl.ds(q0, bq), :] = (o * _silu(p_ref[3, pl.ds(q0, bq), :])).astype(BF16)
            return c

        lax.fori_loop(0, T // bq, qblock, 0)

    colblk = pl.BlockSpec((T, LANES), lambda j: (0, j))
    vec = pl.BlockSpec((1, LANES), lambda j: (0, 0))
    return pl.pallas_call(
        body, name=name, grid=(npair,),
        in_specs=[pl.BlockSpec((4, T, LANES), lambda j: (0, 0, j)), vec, vec] + [HBM] * ns,
        out_specs=[colblk, colblk] + [HBM] * ns,
        out_shape=[jax.ShapeDtypeStruct((T, W), BF16), jax.ShapeDtypeStruct((T, W), F32)]
        + [jax.ShapeDtypeStruct((N_CHIPS,) + a.shape, a.dtype) for a in send],
        scratch_shapes=[pltpu.VMEM((T, LANES), BF16)] * 3
        + ([pltpu.SemaphoreType.DMA((3 * ns,)), pltpu.SemaphoreType.DMA((3 * ns,))] if ns else []),
        compiler_params=pltpu.CompilerParams(dimension_semantics=("arbitrary",), vmem_limit_bytes=VMEM_BIG),
    )(p3, gq2, gk2, *send)


def _sb_attn_bwd(p3, gq2, gk2, o, dog, *, name):
    _, T, W = p3.shape
    bq, bk = min(SB_BQ, T), min(SB_BK, T)
    rows = min(SB_ROWS, T)
    scale = SB_DH ** -0.5

    def body(p_ref, gq_ref, gk_ref, o_ref, dog_ref, dp_ref, dgq_ref, dgk_ref,
             qn_ref, kn_ref, v_ref, do_ref):
        lo = _sb_half_mask()

        def prologue(i, c):
            r0 = pl.multiple_of(i * rows, rows)
            sl = pl.ds(r0, rows)
            qn_ref[sl, :] = (_sb_headnorm(p_ref[0, sl, :], gq_ref[...], lo)[0] * scale).astype(BF16)
            kn_ref[sl, :] = _sb_headnorm(p_ref[1, sl, :], gk_ref[...], lo)[0].astype(BF16)
            v_ref[sl, :] = p_ref[2, sl, :].astype(BF16)
            gate = p_ref[3, sl, :]
            dogv = dog_ref[sl, :]
            dp_ref[3, sl, :] = dogv * o_ref[sl, :] * _dsilu(gate)
            do_ref[sl, :] = (dogv * _silu(gate)).astype(BF16)
            zero = jnp.zeros((rows, LANES), F32)
            dp_ref[0, sl, :] = zero
            dp_ref[1, sl, :] = zero
            dp_ref[2, sl, :] = zero
            return c

        lax.fori_loop(0, T // rows, prologue, 0)

        rel = _sb_rel(bq, bk)
        rj = lax.broadcasted_iota(jnp.int32, (bk, bk), 0)
        cj = lax.broadcasted_iota(jnp.int32, (bk, bk), 1)
        upto = (rj <= cj).astype(BF16)
        before_m = (rj < cj).astype(BF16)

        def qblock(qi, c):
            q0 = pl.multiple_of(qi * bq, bq)
            qm = _sb_stack(qn_ref[pl.ds(q0, bq), :], lo)
            dom = _sb_stack(do_ref[pl.ds(q0, bq), :], lo)
            nkb = (q0 + bq - 1) // bk + 1
            blk0 = lambda j: pl.multiple_of(j * bk, bk)
            k_last = blk0(nkb - 1)

            def row_sums(k0, valid):
                return jnp.sum(_sb_tile(qm, kn_ref[pl.ds(k0, bk), :], valid)[1], axis=-1, keepdims=True)

            def alive(st):
                return jnp.logical_and(st[0] < nkb, jnp.max(st[1]) > SB_DEAD)

            def back_one(st):
                return st[0] + 1, st[1] + row_sums(blk0(nkb - 1 - st[0]), None)

            n_live, total = lax.while_loop(alive, back_one, (jnp.int32(1), row_sums(k_last, rel < q0 - k_last)))
            k_first = nkb - n_live

            def tiles(k0s, carry, valids):
                dq_acc, a_pre, r_pre = carry
                kss = [pl.ds(k0, bk) for k0 in k0s]
                kbs = [kn_ref[ks, :] for ks in kss]
                sc = [_sb_tile(qm, kb, valid) for kb, valid in zip(kbs, valids)]
                dws = [lax.dot_general(dom, v_ref[ks, :], _NT, preferred_element_type=F32) for ks in kss]
                upto_l = _dot_x2_l([log1m for _, log1m in sc], upto)
                wts_l = []
                for (logsig, log1m), up, valid in zip(sc, upto_l, valids):
                    wts = jnp.exp(logsig + ((total - a_pre) - up))
                    wts_l.append(wts if valid is None else jnp.where(valid, wts, 0.0))
                    a_pre = a_pre + jnp.sum(log1m, axis=-1, keepdims=True)
                ee_l = [dw * wts for dw, wts in zip(dws, wts_l)]
                before_l = _dot_x2_l(ee_l, before_m)
                for (logsig, _), ks, kb, wts, ee, bef, valid in zip(sc, kss, kbs, wts_l, ee_l, before_l, valids):
                    beta = jnp.exp(logsig)
                    dz = ee * (1.0 - beta) - beta * (r_pre + bef)
                    if valid is not None:
                        dz = jnp.where(valid, dz, 0.0)
                    dzb = _bf(dz)
                    dq_acc = dq_acc + jnp.dot(dzb, kb, preferred_element_type=F32)
                    dp_ref[1, ks, :] += lax.dot_general(dzb, qm, _TN, preferred_element_type=F32)
                    dp_ref[2, ks, :] += lax.dot_general(_bf(wts), dom, _TN, preferred_element_type=F32)
                    r_pre = r_pre + jnp.sum(ee, axis=-1, keepdims=True)
                return dq_acc, a_pre, r_pre

            cr = (jnp.zeros((2 * bq, LANES), F32), jnp.zeros((2 * bq, 1), F32), jnp.zeros((2 * bq, 1), F32))
            n_before = jnp.maximum(n_live - 2, 0)
            cr = lax.fori_loop(0, n_before % 2, lambda t, cr: tiles([blk0(k_first)], cr, [None]), cr)
            k_pairs = k_first + n_before % 2
            cr = lax.fori_loop(0, n_before // 2,
                               lambda t, cr: tiles([blk0(k_pairs + 2 * t), blk0(k_pairs + 2 * t + 1)], cr,
                                                   [None, None]), cr)
            dq2, _, _ = tiles([blk0(jnp.maximum(nkb - 2, 0)), k_last], cr, [n_live >= 2, rel < q0 - k_last])
            dp_ref[0, pl.ds(q0, bq), :] = jnp.where(lo, dq2[:bq], dq2[bq:]) * scale
            return c

        lax.fori_loop(0, T // bq, qblock, 0)

        dgq_ref[...] = jnp.zeros_like(dgq_ref)
        dgk_ref[...] = jnp.zeros_like(dgk_ref)

        def epilogue(i, c):
            r0 = pl.multiple_of(i * rows, rows)
            sl = pl.ds(r0, rows)
            for part, g_ref, dg_ref in ((0, gq_ref, dgq_ref), (1, gk_ref, dgk_ref)):
                _, xh, r = _sb_headnorm(p_ref[part, sl, :], g_ref[...], lo)
                dn = dp_ref[part, sl, :]
                dxh = dn * g_ref[...]
                prod = dxh * xh
                m_lo = jnp.sum(jnp.where(lo, prod, 0.0), axis=-1, keepdims=True)
                m_hi = jnp.sum(jnp.where(lo, 0.0, prod), axis=-1, keepdims=True)
                m = jnp.where(lo, m_lo, m_hi) * (1.0 / SB_DH)
                dp_ref[part, sl, :] = r * (dxh - xh * m)
                dg_ref[...] += jnp.sum(dn * xh, axis=0, keepdims=True)
            return c

        lax.fori_loop(0, T // rows, epilogue, 0)

    colblk = pl.BlockSpec((T, LANES), lambda j: (0, j))
    vec = pl.BlockSpec((1, LANES), lambda j: (0, 0))
    part = pl.BlockSpec((4, T, LANES), lambda j: (0, 0, j))
    gvec = pl.BlockSpec((None, 1, LANES), lambda j: (j, 0, 0))
    npair = W // LANES
    return pl.pallas_call(
        body, name=name, grid=(npair,),
        in_specs=[part, vec, vec, colblk, colblk],
        out_specs=[part, gvec, gvec],
        out_shape=[jax.ShapeDtypeStruct((4, T, W), F32), jax.ShapeDtypeStruct((npair, 1, LANES), F32),
                   jax.ShapeDtypeStruct((npair, 1, LANES), F32)],
        scratch_shapes=[pltpu.VMEM((T, LANES), BF16)] * 4,
        compiler_params=pltpu.CompilerParams(dimension_semantics=("parallel",), vmem_limit_bytes=VMEM_BIG),
    )(p3, gq2, gk2, o, dog)


_NN = (((1,), (0,)), ((), ()))
_NT = (((1,), (1,)), ((), ()))
_TN = (((0,), (0,)), ((), ()))
DN_TB = 512
DN_HEADS_FWD = 4
DN_HEADS_BWD = 2
DN_INV_EXACT_LEVELS = 2
DN_AB_COL = (DN_CONV_W + DN_V_W) // LANES


def _dn_conv(x, w_ref):
    k = w_ref.shape[0]
    return sum(w_ref[i:i + 1, :] * _shift_down(x, k - 1 - i) for i in range(k))


def _dn_prep_fwd(p, conv_w, *, name):
    T = p.shape[0]
    cw = conv_w.shape[1]
    n_qk = 2 * DN_QK_W // LANES

    def body(p_ref, w_ref, o_ref):
        s = _silu(_dn_conv(p_ref[...], w_ref))
        r = lax.rsqrt(jnp.sum(s * s, axis=-1, keepdims=True) + L2_EPS)
        o_ref[...] = jnp.where(pl.program_id(0) < n_qk, s * r, s)

    colblk = pl.BlockSpec((T, LANES), lambda j: (0, j))
    return pl.pallas_call(
        body, name=name, grid=(cw // LANES,),
        in_specs=[colblk, pl.BlockSpec((DN_CONV, LANES), lambda j: (0, j))],
        out_specs=colblk, out_shape=jax.ShapeDtypeStruct((T, cw), F32),
        compiler_params=pltpu.CompilerParams(dimension_semantics=("parallel",), vmem_limit_bytes=VMEM_BIG),
    )(p, conv_w)


def _dn_chunk_tri(rows, upper):
    r = lax.broadcasted_iota(jnp.int32, (rows, rows), 0)
    c = lax.broadcasted_iota(jnp.int32, (rows, rows), 1)
    same = (r // DN_CHUNK) == (c // DN_CHUNK)
    return jnp.logical_and(same, (c >= r) if upper else (c <= r)).astype(BF16)


def _dn_lane_rows(a_log, dt_bias):
    pad = lambda v: jnp.zeros((1, LANES), F32).at[0, :DN_HEADS].set(v)
    return pad(a_log), pad(dt_bias)


def _dn_ab_parts(blk, alog_row, dtb_row):
    lane = lax.broadcasted_iota(jnp.int32, (1, LANES), 1)
    is_a = lane < DN_HEADS
    is_b = jnp.logical_and(lane >= DN_HEADS, lane < 2 * DN_HEADS)
    a_arg = jnp.where(is_a, blk + dtb_row, 0.0)
    neg_exp = jnp.where(is_a, -jnp.exp(alog_row), 0.0)
    log_a = neg_exp * _softplus(a_arg)
    beta = jnp.where(is_b, _sigmoid(blk), 0.0)
    return is_a, is_b, a_arg, neg_exp, log_a, beta


def _dn_ab_fwd(p, alog_row, dtb_row, *, name):
    T = p.shape[0]
    rows = min(DN_TB, T)

    def body(p_ref, al_ref, dt_ref, o_ref):
        _, _, _, _, log_a, beta = _dn_ab_parts(p_ref[...], al_ref[...], dt_ref[...])
        hi, mid, lo_ = _split3(log_a)
        tri = _dn_chunk_tri(rows, upper=False)
        f = lambda q: jnp.dot(tri, q, preferred_element_type=F32)
        o_ref[...] = (f(hi) + f(mid) + f(lo_)) + beta

    blk = pl.BlockSpec((rows, LANES), lambda i: (i, DN_AB_COL))
    vec = pl.BlockSpec((1, LANES), lambda i: (0, 0))
    return pl.pallas_call(
        body, name=name, grid=(T // rows,), in_specs=[blk, vec, vec],
        out_specs=pl.BlockSpec((rows, LANES), lambda i: (i, 0)),
        out_shape=jax.ShapeDtypeStruct((T, LANES), F32),
        compiler_params=pltpu.CompilerParams(dimension_semantics=("parallel",)),
    )(p, alog_row, dtb_row)


def _hp_l(a_l, b_l, dims=_NN):
    sa = [_split3(a)[:2] for a in a_l]
    sb = [_split3(b)[:2] for b in b_l]
    f = lambda p, q: lax.dot_general(p, q, dims, preferred_element_type=F32)
    hh = [f(x[0], y[0]) for x, y in zip(sa, sb)]
    hm = [f(x[0], y[1]) for x, y in zip(sa, sb)]
    mh = [f(x[1], y[0]) for x, y in zip(sa, sb)]
    return [a + (b + c) for a, b, c in zip(hh, hm, mh)]


def _dn_local(qs, k, v, g, beta, nc):
    c = DN_CHUNK
    cut = lambda x: [x[i * c:(i + 1) * c] for i in range(nc)]
    row = lax.broadcasted_iota(jnp.int32, (c, c), 0)
    col = lax.broadcasted_iota(jnp.int32, (c, c), 1)
    eye, lower, strict = row == col, row >= col, row > col
    rowid = lax.broadcasted_iota(jnp.int32, (c, 1), 0)
    eg = jnp.exp(g)
    kb = k * beta
    rhs_k = kb * eg
    g_l, k_l, kb_l, qs_l = cut(g), cut(k), cut(kb), cut(qs)
    g_row_l = [jnp.sum(jnp.where(eye, x, 0.0), axis=0, keepdims=True) for x in g_l]
    dec_l = [jnp.where(lower, jnp.exp(jnp.where(lower, x - y, 0.0)), 0.0) for x, y in zip(g_l, g_row_l)]
    kk_l = [_dot_nt(a, b) for a, b in zip(kb_l, k_l)]
    qk_l = [_dot_nt(a, b) for a, b in zip(qs_l, k_l)]
    low_l = [jnp.where(strict, a * d, 0.0) for a, d in zip(kk_l, dec_l)]
    eye_f = eye.astype(F32)
    pw_l = [-x for x in low_l]
    inv_l = [eye_f + x for x in pw_l]
    plain = lambda a_l, b_l: [_dot(a, b) for a, b in zip(a_l, b_l)]
    for level in range(int(math.log2(c)) - 1):
        mul = _hp_l if level < DN_INV_EXACT_LEVELS else plain
        pw_l = mul(pw_l, pw_l)
        inv_l = [a + b for a, b in zip(inv_l, mul(inv_l, pw_l))]
    u_l = [_dot(a, b) for a, b in zip(inv_l, cut(v * beta))]
    w_l = [_dot(a, b) for a, b in zip(inv_l, cut(rhs_k))]
    aqk_l = [jnp.where(lower, a * d, 0.0) for a, d in zip(qk_l, dec_l)]
    g_last_l = [jnp.sum(jnp.where(rowid == c - 1, x, 0.0), axis=0, keepdims=True) for x in g_l]
    ekd_l = [jnp.exp(a - b) for a, b in zip(g_last_l, g_l)]
    kd_l = [a * b for a, b in zip(k_l, ekd_l)]
    qd_l = cut(qs * eg)
    kw_l = [_dot_tn(a, b) for a, b in zip(kd_l, w_l)]
    qp_l = [q - _dot(a, w) for q, a, w in zip(qd_l, aqk_l, w_l)]
    return dict(eye=eye, lower=lower, strict=strict, dec=dec_l, k=k_l, kb=kb_l, qs=qs_l, low=low_l, inv=inv_l,
                eg=cut(eg), rhs_k=cut(rhs_k), u=u_l, w=w_l, aqk=aqk_l, g_last=g_last_l, qd=qd_l,
                ekd=ekd_l, kd=kd_l, kw=kw_l, qp=qp_l)


def _dn_head_cols(gb_blk, head):
    lane = lax.broadcasted_iota(jnp.int32, (1, LANES), 1)
    g = jnp.sum(jnp.where(lane == head, gb_blk, 0.0), axis=-1, keepdims=True)
    beta = jnp.sum(jnp.where(lane == head + DN_HEADS, gb_blk, 0.0), axis=-1, keepdims=True)
    return g, beta


def _halves_over_ici(s_refs, o_refs, send_sems, recv_sems, first, last):
    x, y, c = _mesh_pos()
    me = 2 * x + y
    chips = _other_chips(x, y)
    pairs = [(a, k) for a in range(len(s_refs)) for k in range(3)]

    def copy(a, k, slot):
        px, py = chips[k]
        return pltpu.make_async_remote_copy(
            src_ref=s_refs[a].at[c], dst_ref=o_refs[a].at[slot, c], send_sem=send_sems.at[3 * a + k],
            recv_sem=recv_sems.at[3 * a + k], device_id=(px, py, c), device_id_type=MESH)

    @pl.when(first)
    def _():
        for a, k in pairs:
            copy(a, k, me).start()

    @pl.when(last)
    def _():
        for a, k in pairs:
            px, py = chips[k]
            copy(a, k, 2 * px + py).wait_recv()
        for a, k in pairs:
            copy(a, k, me).wait_send()


def _dn_delta_fwd(qkv, gb, p, o_gain, *, name, send=()):
    T = qkv.shape[0]
    tb = min(DN_TB, T)
    nb, nc = T // tb, tb // DN_CHUNK
    H = DN_HEADS
    qscale = DN_DK ** -0.5
    ns = len(send)
    hp = DN_HEADS_FWD

    def body(*refs):
        q_ref, k_ref, v_ref, gb_ref, gate_ref, gain_ref = refs[:6]
        o_ref, og_ref, st_ref = refs[6 + ns:9 + ns]
        s_ref = refs[9 + 2 * ns]
        pair, blk = pl.program_id(0), pl.program_id(1)
        if ns:
            _halves_over_ici(refs[6:6 + ns], refs[9 + ns:9 + 2 * ns], refs[10 + 2 * ns], refs[11 + 2 * ns],
                             jnp.logical_and(pair == 0, blk == 0),
                             jnp.logical_and(pair == H // hp - 1, blk == nb - 1))

        @pl.when(blk == 0)
        def _():
            s_ref[...] = jnp.zeros_like(s_ref)

        gbv = gb_ref[...]
        ts, ku, op = [], [], []
        for e in range(hp):
            qk_e, v_e = slice(e * DN_DK, (e + 1) * DN_DK), slice(e * DN_DV, (e + 1) * DN_DV)
            g, beta = _dn_head_cols(gbv, hp * pair + e)
            t = _dn_local(q_ref[:, qk_e] * qscale, k_ref[:, qk_e], v_ref[:, v_e], g, beta, nc)
            ts.append(t)
            ku.append([_dot_tn(a, b) for a, b in zip(t["kd"], t["u"])])
            op.append([_dot(a, b) for a, b in zip(t["aqk"], t["u"])])
        s32 = [s_ref[e] for e in range(hp)]
        s_l = [[] for _ in range(hp)]
        for i in range(nc):
            sb = [_bf(x) for x in s32]
            for e in range(hp):
                st_ref[e, i] = sb[e]
                s_l[e].append(sb[e])
            prod = [_dot(ts[e]["kw"][i], sb[e]) for e in range(hp)]
            s32 = [s32[e] * jnp.exp(ts[e]["g_last"][i]) - prod[e] + ku[e][i] for e in range(hp)]
        for e in range(hp):
            s_ref[e] = s32[e]
        o = jnp.concatenate(
            [jnp.concatenate([_dot(qp, sb) + x for qp, sb, x in zip(ts[e]["qp"], s_l[e], op[e])], axis=0)
             for e in range(hp)], axis=1)
        o_ref[...] = o
        gain = gain_ref[...]
        for e in range(hp):
            v_e = slice(e * DN_DV, (e + 1) * DN_DV)
            oe = o[:, v_e]
            r = lax.rsqrt(jnp.mean(oe * oe, axis=-1, keepdims=True) + RMS_EPS)
            og_ref[:, v_e] = (((oe * r) * gain) * _silu(gate_ref[:, v_e])).astype(BF16)

    qk = lambda col0: pl.BlockSpec((tb, hp * DN_DK), lambda h, i: (i, col0 // (hp * DN_DK) + h))
    vblk = lambda col0: pl.BlockSpec((tb, hp * DN_DV), lambda h, i: (i, col0 // (hp * DN_DV) + h))
    return pl.pallas_call(
        body, name=name, grid=(H // hp, nb),
        in_specs=[qk(0), qk(DN_QK_W), vblk(2 * DN_QK_W), pl.BlockSpec((tb, LANES), lambda h, i: (i, 0)),
                  vblk(DN_CONV_W), pl.BlockSpec((1, DN_DV), lambda h, i: (0, 0))] + [HBM] * ns,
        out_specs=[vblk(0), vblk(0), pl.BlockSpec((hp, nc, DN_DK, DN_DV), lambda h, i: (h, i, 0, 0))] + [HBM] * ns,
        out_shape=[jax.ShapeDtypeStruct((T, DN_V_W), F32), jax.ShapeDtypeStruct((T, DN_V_W), BF16),
                   jax.ShapeDtypeStruct((H, T // DN_CHUNK, DN_DK, DN_DV), BF16)]
        + [jax.ShapeDtypeStruct((N_CHIPS,) + a.shape, a.dtype) for a in send],
        scratch_shapes=[pltpu.VMEM((hp, DN_DK, DN_DV), F32)]
        + ([pltpu.SemaphoreType.DMA((3 * ns,)), pltpu.SemaphoreType.DMA((3 * ns,))] if ns else []),
        compiler_params=pltpu.CompilerParams(dimension_semantics=("arbitrary", "arbitrary")),
    )(qkv, qkv, qkv, gb, p, o_gain, *send)


def _blocks_over_ici(p_refs, o_refs, send_sems, recv_sems, first, last):
    x, y, c = _mesh_pos()
    me = 2 * x + y
    chips = _other_chips(x, y)
    pairs = [(a, k) for a in range(len(p_refs)) for k in range(3)]

    def copy(a, k, slot):
        px, py = chips[k]
        return pltpu.make_async_remote_copy(
            src_ref=p_refs[a].at[2 * px + py], dst_ref=o_refs[a].at[slot], send_sem=send_sems.at[3 * a + k],
            recv_sem=recv_sems.at[3 * a + k], device_id=(px, py, c), device_id_type=MESH)

    @pl.when(first)
    def _():
        for a, k in pairs:
            copy(a, k, me).start()

    @pl.when(last)
    def _():
        for a, k in pairs:
            px, py = chips[k]
            copy(a, k, 2 * px + py).wait_recv()
        for a, k in pairs:
            copy(a, k, me).wait_send()


def _dn_delta_bwd(qkv, gb, p, o_gain, o, states, dog, *, name, send=()):
    T = qkv.shape[0]
    tb = min(DN_TB, T)
    nb, nc = T // tb, tb // DN_CHUNK
    H = DN_HEADS
    qscale = DN_DK ** -0.5
    ns = len(send)
    hp = DN_HEADS_BWD

    def body(*refs):
        q_ref, k_ref, v_ref, gb_ref, gate_ref, gain_ref, o_ref, st_ref, dog_ref = refs[:9]
        dq_ref, dk_ref, dv_ref, dgate_ref, dgb_ref, dgain_ref = refs[9 + ns:15 + ns]
        ds_ref = refs[15 + 2 * ns]
        pair, blk = pl.program_id(0), pl.program_id(1)
        first = jnp.logical_and(pair == 0, blk == 0)
        if ns:
            _blocks_over_ici(refs[9:9 + ns], refs[15 + ns:15 + 2 * ns], refs[16 + 2 * ns], refs[17 + 2 * ns],
                             first, jnp.logical_and(pair == H // hp - 1, blk == nb - 1))

        @pl.when(blk == 0)
        def _():
            ds_ref[...] = jnp.zeros_like(ds_ref)

        @pl.when(first)
        def _():
            dgain_ref[...] = jnp.zeros_like(dgain_ref)

        lane = lax.broadcasted_iota(jnp.int32, (1, LANES), 1)
        c = DN_CHUNK
        cut = lambda x: [x[i * c:(i + 1) * c] for i in range(nc)]
        cat = lambda xs: jnp.concatenate(xs, axis=0)
        rsum = lambda x: jnp.sum(x, axis=-1, keepdims=True)
        gbv, gain = gb_ref[...], gain_ref[...]

        def before_chain(e):
            qk_e, v_e = slice(e * DN_DK, (e + 1) * DN_DK), slice(e * DN_DV, (e + 1) * DN_DV)
            g, beta = _dn_head_cols(gbv, hp * pair + e)
            ov, gate, dogv = o_ref[:, v_e], gate_ref[:, v_e], dog_ref[:, v_e]
            r = lax.rsqrt(jnp.mean(ov * ov, axis=-1, keepdims=True) + RMS_EPS)
            oh = ov * r
            dnrm = dogv * _silu(gate)
            dgate_ref[:, v_e] = dogv * (oh * gain) * _dsilu(gate)
            doh = dnrm * gain
            do_l = cut(r * (doh - oh * jnp.mean(doh * oh, axis=-1, keepdims=True)))
            dgain_ref[...] += jnp.sum(dnrm * oh, axis=0, keepdims=True)
            k, v = k_ref[:, qk_e], v_ref[:, v_e]
            t = _dn_local(q_ref[:, qk_e] * qscale, k, v, g, beta, nc)
            s_l = [st_ref[e, i] for i in range(nc)]
            vn_l = [u - _dot(w, sb) for u, w, sb in zip(t["u"], t["w"], s_l)]
            return dict(
                t=t, beta=beta, v=v, s=s_l, vn=vn_l, egl=[jnp.exp(x) for x in t["g_last"]],
                dqd=[_dot_nt(a, sb) for a, sb in zip(do_l, s_l)], daqk=[_dot_nt(a, b) for a, b in zip(do_l, vn_l)],
                aqk_do=[_dot_tn(a, b) for a, b in zip(t["aqk"], do_l)],
                qp_do=[_dot_tn(a, b) for a, b in zip(t["qp"], do_l)])

        hs = [before_chain(e) for e in range(hp)]
        ds = [ds_ref[e] for e in range(hp)]
        ds_l = [[None] * nc for _ in range(hp)]
        for i in reversed(range(nc)):
            for e in range(hp):
                ds_l[e][i] = ds[e]
            prod = [_dot_tn(hs[e]["t"]["kw"][i], ds[e]) for e in range(hp)]
            ds = [ds[e] * hs[e]["egl"][i] - prod[e] + hs[e]["qp_do"][i] for e in range(hp)]
        for e in range(hp):
            ds_ref[e] = ds[e]

        def after_chain(e):
            hd, t = hs[e], hs[e]["t"]
            lower, strict, eye = t["lower"], t["strict"], t["eye"]
            s_l, vn_l, dqd_l, daqk_l, egl_l, beta, v = (hd["s"], hd["vn"], hd["dqd"], hd["daqk"], hd["egl"],
                                                         hd["beta"], hd["v"])
            dvn_l = [a + _dot(kd, d) for a, kd, d in zip(hd["aqk_do"], t["kd"], ds_l[e])]
            dkd_l = [_dot_nt(a, d) for a, d in zip(vn_l, ds_l[e])]
            dgl_l = [jnp.sum(rsum(d * sb.astype(F32)), axis=0, keepdims=True) * x
                     for d, sb, x in zip(ds_l[e], s_l, egl_l)]
            dw_l = [-_dot_nt(a, sb) for a, sb in zip(dvn_l, s_l)]
            dbv_l = [_dot_tn(a, b) for a, b in zip(t["inv"], dvn_l)]
            dbk_l = [_dot_tn(a, b) for a, b in zip(t["inv"], dw_l)]
            dlow_l = [-(_dot_nt(a, b) + _dot_nt(x, y)) for a, b, x, y in zip(dbv_l, t["u"], dbk_l, t["w"])]
            m_l = [jnp.where(strict, a * d, 0.0) for a, d in zip(dlow_l, t["dec"])]
            nmat_l = [jnp.where(lower, a * d, 0.0) for a, d in zip(daqk_l, t["dec"])]
            dkb_l = [_dot(m, kk) + b * x for m, kk, b, x in zip(m_l, t["k"], dbk_l, t["eg"])]
            dqs_l = [_dot(n, kk) + a * x for n, kk, a, x in zip(nmat_l, t["k"], dqd_l, t["eg"])]
            dk1_l = [_dot_tn(m, kb) for m, kb in zip(m_l, t["kb"])]
            dk2_l = [_dot_tn(n, q) for n, q in zip(nmat_l, t["qs"])]
            beta_l, v_l = cut(beta), cut(v)
            rowid = lax.broadcasted_iota(jnp.int32, (c, 1), 0)
            dk_l, dg_l, dbeta_l = [], [], []
            for i in range(nc):
                dk_l.append(dk1_l[i] + dk2_l[i] + dkd_l[i] * t["ekd"][i] + dkb_l[i] * beta_l[i])
                gmat = jnp.where(strict, dlow_l[i] * t["low"][i], 0.0) + daqk_l[i] * t["aqk"][i]
                s_kd = rsum(dkd_l[i] * t["kd"][i])
                dg = (rsum(gmat) + rsum(dqd_l[i] * t["qd"][i]) - s_kd + rsum(dbk_l[i] * t["rhs_k"][i]))
                dg_row = -jnp.sum(gmat, axis=0, keepdims=True)
                dg = dg + rsum(jnp.where(eye, dg_row, 0.0))
                dgl = dgl_l[i] + jnp.sum(s_kd, axis=0, keepdims=True)
                dg_l.append(dg + jnp.where(rowid == c - 1, dgl, 0.0))
                dbeta_l.append(rsum(dbv_l[i] * v_l[i]) + rsum(dkb_l[i] * t["k"][i]))
            head = hp * pair + e
            dgb = (jnp.where(lane == head, cat(dg_l), 0.0) + jnp.where(lane == head + DN_HEADS, cat(dbeta_l), 0.0))
            return cat(dqs_l) * qscale, cat(dk_l), cat(dbv_l) * beta, dgb

        for e in range(hp):
            dq, dk, dv, dgb = after_chain(e)
            dq_ref[:, e * DN_DK:(e + 1) * DN_DK] = dq
            dk_ref[:, e * DN_DK:(e + 1) * DN_DK] = dk
            dv_ref[:, e * DN_DV:(e + 1) * DN_DV] = dv
            dgb_ref[e] = dgb

    rev = lambda i: nb - 1 - i
    qk = lambda col0: pl.BlockSpec((tb, hp * DN_DK), lambda h, i: (rev(i), col0 // (hp * DN_DK) + h))
    vblk = lambda col0: pl.BlockSpec((tb, hp * DN_DV), lambda h, i: (rev(i), col0 // (hp * DN_DV) + h))
    gain_spec = pl.BlockSpec((1, DN_DV), lambda h, i: (0, 0))
    return pl.pallas_call(
        body, name=name, grid=(H // hp, nb),
        in_specs=[qk(0), qk(DN_QK_W), vblk(2 * DN_QK_W), pl.BlockSpec((tb, LANES), lambda h, i: (rev(i), 0)),
                  vblk(DN_CONV_W), gain_spec, vblk(0),
                  pl.BlockSpec((hp, nc, DN_DK, DN_DV), lambda h, i: (h, rev(i), 0, 0)), vblk(0)] + [HBM] * ns,
        out_specs=[qk(0), qk(0), vblk(0), vblk(DN_CONV_W),
                   pl.BlockSpec((hp, tb, LANES), lambda h, i: (h, rev(i), 0)), gain_spec] + [HBM] * ns,
        out_shape=[jax.ShapeDtypeStruct((T, DN_QK_W), F32), jax.ShapeDtypeStruct((T, DN_QK_W), F32),
                   jax.ShapeDtypeStruct((T, DN_V_W), F32), jax.ShapeDtypeStruct((T, DN_IN_PAD), F32),
                   jax.ShapeDtypeStruct((H, T, LANES), F32), jax.ShapeDtypeStruct((1, DN_DV), F32)]
        + [jax.ShapeDtypeStruct(a.shape, a.dtype) for a in send],
        scratch_shapes=[pltpu.VMEM((hp, DN_DK, DN_DV), F32)]
        + ([pltpu.SemaphoreType.DMA((3 * ns,)), pltpu.SemaphoreType.DMA((3 * ns,))] if ns else []),
        compiler_params=pltpu.CompilerParams(dimension_semantics=("arbitrary", "arbitrary")),
    )(qkv, qkv, qkv, gb, p, o_gain, o, states, dog, *send)


def _dn_conv_bwd(p, conv_w, d, dp, *, first, normed, name):
    T, width = d.shape

    def body(p_ref, w_ref, d_ref, dp_in, dp_ref, dw_ref):
        del dp_in
        x = p_ref[...]
        ksz = w_ref.shape[0]
        xs = [_shift_down(x, ksz - 1 - i) for i in range(ksz)]
        xc = sum(w_ref[i:i + 1, :] * xs[i] for i in range(ksz))
        ds = d_ref[...]
        if normed:
            s = _silu(xc)
            r = lax.rsqrt(jnp.sum(s * s, axis=-1, keepdims=True) + L2_EPS)
            y = s * r
            ds = r * (ds - y * jnp.sum(ds * y, axis=-1, keepdims=True))
        dxc = ds * _dsilu(xc)
        dp_ref[...] = sum(w_ref[i:i + 1, :] * _shift_up(dxc, ksz - 1 - i) for i in range(ksz))
        for i in range(ksz):
            dw_ref[i:i + 1, :] = jnp.sum(dxc * xs[i], axis=0, keepdims=True)

    shifted = pl.BlockSpec((T, LANES), lambda j: (0, first + j))
    return pl.pallas_call(
        body, name=name, grid=(width // LANES,),
        in_specs=[shifted, pl.BlockSpec((DN_CONV, LANES), lambda j: (0, first + j)),
                  pl.BlockSpec((T, LANES), lambda j: (0, j)), pl.BlockSpec(memory_space=pl.ANY)],
        out_specs=[shifted, pl.BlockSpec((DN_CONV, LANES), lambda j: (0, j))],
        out_shape=[jax.ShapeDtypeStruct(dp.shape, F32), jax.ShapeDtypeStruct((DN_CONV, width), F32)],
        input_output_aliases={3: 0},
        compiler_params=pltpu.CompilerParams(dimension_semantics=("parallel",), vmem_limit_bytes=VMEM_BIG),
    )(p, conv_w, d, dp)


def _dn_ab_bwd(p, alog_row, dtb_row, dgb, dp, *, name):
    T = p.shape[0]
    rows = min(DN_TB, T)
    H = DN_HEADS

    def body(p_ref, al_ref, dt_ref, dgb_ref, dp_in, dp_ref, dal_ref, ddt_ref):
        del dp_in

        @pl.when(pl.program_id(0) == 0)
        def _():
            dal_ref[...] = jnp.zeros_like(dal_ref)
            ddt_ref[...] = jnp.zeros_like(ddt_ref)

        blk = p_ref[...]
        is_a, is_b, a_arg, neg_exp, log_a, beta = _dn_ab_parts(blk, al_ref[...], dt_ref[...])
        d = dgb_ref[0]
        for hh in range(1, H):
            d = d + dgb_ref[hh]
        hi, mid, lo_ = _split3(jnp.where(is_a, d, 0.0))
        tri = _dn_chunk_tri(rows, upper=True)
        f = lambda q: jnp.dot(tri, q, preferred_element_type=F32)
        dlog_a = f(hi) + f(mid) + f(lo_)
        da_in = dlog_a * neg_exp * _sigmoid(a_arg)
        db_in = jnp.where(is_b, d, 0.0) * beta * (1.0 - beta)
        dp_ref[...] = jnp.where(is_a, da_in, 0.0) + db_in
        dal_ref[...] += jnp.sum(dlog_a * log_a, axis=0, keepdims=True)
        ddt_ref[...] += jnp.sum(jnp.where(is_a, da_in, 0.0), axis=0, keepdims=True)

    blk = pl.BlockSpec((rows, LANES), lambda i: (i, DN_AB_COL))
    vec = pl.BlockSpec((1, LANES), lambda i: (0, 0))
    return pl.pallas_call(
        body, name=name, grid=(T // rows,),
        in_specs=[blk, vec, vec, pl.BlockSpec((H, rows, LANES), lambda i: (0, i, 0)),
                  pl.BlockSpec(memory_space=pl.ANY)],
        out_specs=[blk, vec, vec],
        out_shape=[jax.ShapeDtypeStruct(dp.shape, F32), jax.ShapeDtypeStruct((1, LANES), F32),
                   jax.ShapeDtypeStruct((1, LANES), F32)],
        input_output_aliases={4: 0},
        compiler_params=pltpu.CompilerParams(dimension_semantics=("arbitrary",)),
    )(p, alog_row, dtb_row, dgb, dp)


def _dn_layer_fwd(x, ng, w_in, conv_w, a_log, dt_bias, o_gain, w_out, tag, send=()):
    alog_row, dtb_row = _dn_lane_rows(a_log, dt_bias)
    gain = o_gain.reshape(1, DN_DV)
    h = _rmsnorm_fwd(x, ng, name=f"{tag}_norm")
    p = _matmul(h, w_in, mode="nn", name=f"{tag}_inproj")
    qkv = _dn_prep_fwd(p, conv_w, name=f"{tag}_prep")
    gb = _dn_ab_fwd(p, alog_row, dtb_row, name=f"{tag}_ab")
    o, og, states, *landed = _dn_delta_fwd(qkv, gb, p, gain, name=f"{tag}_delta", send=send)
    x_new = _matmul(og, w_out, mode="nn", res=x, name=f"{tag}_outproj")
    return x_new, (h, p, qkv, gb, o, og, states), landed


def _dn_layer_bwd(dx, x, ng, w_in, conv_w, a_log, dt_bias, o_gain, w_out, saved, tag, send=(), chip_sums=None):
    h, p, qkv, gb, o, og, states = saved
    alog_row, dtb_row = _dn_lane_rows(a_log, dt_bias)
    gain = o_gain.reshape(1, DN_DV)
    d_wout = _matmul(og, dx, mode="tn", out_dtype=BF16, name=f"{tag}_dwout")
    if chip_sums is not None:
        d_wout, = chip_sums([_cut2(_by_rows(d_wout))], f"{tag}wout")
        send = list(send) + [d_wout]
    dog = _matmul(dx, w_out, mode="nt", name=f"{tag}_dog")
    dq, dk, dv, dp, dgb, dgain, *landed = _dn_delta_bwd(qkv, gb, p, gain, o, states, dog, name=f"{tag}_deltabwd",
                                                        send=send)
    n_qk = DN_QK_W // LANES
    dp, dconv_q = _dn_conv_bwd(p, conv_w, dq, dp, first=0, normed=True, name=f"{tag}_convbwd_q")
    dp, dconv_k = _dn_conv_bwd(p, conv_w, dk, dp, first=n_qk, normed=True, name=f"{tag}_convbwd_k")
    dp, dconv_v = _dn_conv_bwd(p, conv_w, dv, dp, first=2 * n_qk, normed=False, name=f"{tag}_convbwd_v")
    dconv = jnp.concatenate([dconv_q, dconv_k, dconv_v], axis=1)
    dp, dal, ddt = _dn_ab_bwd(p, alog_row, dtb_row, dgb, dp, name=f"{tag}_abbwd")
    d_win = _matmul(h, dp, mode="tn", name=f"{tag}_dwin")
    if chip_sums is not None:
        d_win, = chip_sums([_cut2(_by_cols(d_win))], f"{tag}win")
        dh, landed_win = _matmul(dp, w_in, mode="nt", name=f"{tag}_dh", send=[d_win])
        landed = landed + [landed_win]
    else:
        dh = _matmul(dp, w_in, mode="nt", name=f"{tag}_dh")
    dx_prev, dng = _rmsnorm_bwd(x, ng, dh, dx, name=f"{tag}_normbwd")
    return dx_prev, dng, d_win, dconv, dal[0, :DN_HEADS], ddt[0, :DN_HEADS], dgain[0], d_wout, landed


def _by_cols(dw):
    return _split(dw[:, :DN_IN].astype(BF16), 1)


def _by_rows(dw):
    return dw.reshape(N_CHIPS, -1, dw.shape[-1])


def _cut2(g4):
    return g4.reshape(N_CHIPS, 2, -1, g4.shape[-1])


def _sb_gains(g):
    return jnp.concatenate([g, g]).reshape(1, LANES)


def _sb_layer_fwd(x, ng, w_in, gq, gk, w_out, tag, send=()):
    h = _rmsnorm_fwd(x, ng, name=f"{tag}_norm")
    p3 = _matmul(h, w_in, mode="nn", b_parts=4, out_parts=4, name=f"{tag}_inproj")
    og, o, *landed = _sb_attn_fwd(p3, _sb_gains(gq), _sb_gains(gk), name=f"{tag}_attn", send=send)
    x_new = _matmul(og, w_out, mode="nn", res=x, name=f"{tag}_outproj")
    return x_new, (h, p3, og, o), landed


def _sb_layer_bwd(dx, x, ng, w_in, gq, gk, w_out, saved, tag):
    h, p3, og, o = saved
    d_wout = _matmul(og, dx, mode="tn", out_dtype=BF16, name=f"{tag}_dwout")
    dog = _matmul(dx, w_out, mode="nt", name=f"{tag}_dog")
    dp3, dgq, dgk = _sb_attn_bwd(p3, _sb_gains(gq), _sb_gains(gk), o, dog, name=f"{tag}_attnbwd")
    fold = lambda d: jnp.sum(d.reshape(-1, SB_DH), axis=0)
    d_win = _matmul(h, dp3, mode="tn", b_parts=4, out_parts=4, out_dtype=BF16, name=f"{tag}_dwin")
    dh = _matmul(dp3, w_in, mode="nt", a_parts=4, b_parts=4, name=f"{tag}_dh")
    dx_prev, dng = _rmsnorm_bwd(x, ng, dh, dx, name=f"{tag}_normbwd")
    return dx_prev, dng, d_win, fold(dgq), fold(dgk), d_wout


N_CHIPS = 4
HBM = pl.BlockSpec(memory_space=pl.ANY)


def _mesh_pos():
    return lax.axis_index("x"), lax.axis_index("y"), lax.axis_index("c")


def _other_chips(x, y):
    return [(1 - x, y), (x, 1 - y), (1 - x, 1 - y)]


def _chip_exchange(srcs, *, send_slot_is_dest, copy_own, name):
    n = len(srcs)

    def body(*refs):
        src_refs, out_refs = refs[:n], refs[n:2 * n]
        send_sems, recv_sems, local_sems = refs[2 * n:]
        x, y, c = _mesh_pos()
        me = 2 * x + y
        chips = _other_chips(x, y)
        local = []
        for a in range(n):
            if not copy_own[a]:
                continue
            own = src_refs[a].at[me] if send_slot_is_dest else src_refs[a]
            local.append(pltpu.make_async_copy(own, out_refs[a].at[me], local_sems.at[a]))
        for cp in local:
            cp.start()

        def copy(a, k, landing_slot):
            px, py = chips[k]
            src = src_refs[a].at[2 * px + py] if send_slot_is_dest else src_refs[a]
            return pltpu.make_async_remote_copy(
                src_ref=src, dst_ref=out_refs[a].at[landing_slot],
                send_sem=send_sems.at[a * 3 + k], recv_sem=recv_sems.at[a * 3 + k],
                device_id=(px, py, c), device_id_type=MESH)

        sends = [copy(a, k, me) for a in range(n) for k in range(3)]
        for cp in sends:
            cp.start()
        for a in range(n):
            for k in range(3):
                px, py = chips[k]
                copy(a, k, 2 * px + py).wait_recv()
        for cp in sends:
            cp.wait_send()
        for cp in local:
            cp.wait()

    outs = []
    for s in srcs:
        shape = s.shape if send_slot_is_dest else (N_CHIPS,) + s.shape
        outs.append(jax.ShapeDtypeStruct(shape, s.dtype))
    return pl.pallas_call(
        body, name=name, in_specs=[HBM] * n, out_specs=[HBM] * n, out_shape=outs,
        scratch_shapes=[pltpu.SemaphoreType.DMA((3 * n,)), pltpu.SemaphoreType.DMA((3 * n,)),
                        pltpu.SemaphoreType.DMA((n,))],
    )(*srcs)


def _sibling_exchange(srcs, *, name):
    n = len(srcs)

    def body(*refs):
        src_refs, out_refs = refs[:n], refs[n:2 * n]
        send_sems, recv_sems = refs[2 * n:]
        x, y, c = _mesh_pos()
        copies = [pltpu.make_async_remote_copy(
            src_ref=src_refs[a], dst_ref=out_refs[a], send_sem=send_sems.at[a], recv_sem=recv_sems.at[a],
            device_id=(x, y, 1 - c), device_id_type=MESH) for a in range(n)]
        for cp in copies:
            cp.start()
        for cp in copies:
            cp.wait()

    return pl.pallas_call(
        body, name=name, in_specs=[HBM] * n, out_specs=[HBM] * n,
        out_shape=[jax.ShapeDtypeStruct(s.shape, s.dtype) for s in srcs],
        scratch_shapes=[pltpu.SemaphoreType.DMA((n,)), pltpu.SemaphoreType.DMA((n,))],
    )(*srcs)


def _gather_halves(shards, small, *, name):
    n = len(shards)

    def body(*refs):
        s_refs, small_ref = refs[:n], refs[n]
        o_refs, osmall_ref = refs[n + 1:2 * n + 1], refs[2 * n + 1]
        send_sems, recv_sems, local_sems = refs[2 * n + 2:]
        x, y, c = _mesh_pos()
        me = 2 * x + y
        chips = _other_chips(x, y)
        local = [pltpu.make_async_copy(small_ref, osmall_ref.at[me], local_sems.at[0])]
        for cp in local:
            cp.start()

        def over_ici(a, k, slot):
            px, py = chips[k]
            return pltpu.make_async_remote_copy(
                src_ref=s_refs[a].at[c], dst_ref=o_refs[a].at[slot, c], send_sem=send_sems.at[3 * a + k],
                recv_sem=recv_sems.at[3 * a + k], device_id=(px, py, c), device_id_type=MESH)

        def small_copy(k, slot):
            px, py = chips[k]
            return pltpu.make_async_remote_copy(
                src_ref=small_ref, dst_ref=osmall_ref.at[slot], send_sem=send_sems.at[3 * n + k],
                recv_sem=recv_sems.at[3 * n + k], device_id=(px, py, c), device_id_type=MESH)

        def to_sibling(a, k, half):
            px, py = chips[k]
            blk = o_refs[a].at[2 * px + py, half]
            return pltpu.make_async_remote_copy(
                src_ref=blk, dst_ref=blk, send_sem=send_sems.at[3 * n + 3 + 3 * a + k],
                recv_sem=recv_sems.at[3 * n + 3 + 3 * a + k], device_id=(x, y, 1 - c), device_id_type=MESH)

        sends = [over_ici(a, k, me) for a in range(n) for k in range(3)] + [small_copy(k, me) for k in range(3)]
        for cp in sends:
            cp.start()
        passed = []
        for a in range(n):
            for k in range(3):
                px, py = chips[k]
                over_ici(a, k, 2 * px + py).wait_recv()
                passed.append(to_sibling(a, k, c))
                passed[-1].start()
        for k in range(3):
            px, py = chips[k]
            small_copy(k, 2 * px + py).wait_recv()
        for a in range(n):
            for k in range(3):
                to_sibling(a, k, 1 - c).wait_recv()
        for cp in sends + passed:
            cp.wait_send()
        for cp in local:
            cp.wait()

    nsem = 6 * n + 3
    return pl.pallas_call(
        body, name=name, in_specs=[HBM] * (n + 1), out_specs=[HBM] * (n + 1),
        out_shape=[jax.ShapeDtypeStruct((N_CHIPS,) + s.shape, s.dtype) for s in shards + [small]],
        scratch_shapes=[pltpu.SemaphoreType.DMA((nsem,)), pltpu.SemaphoreType.DMA((nsem,)),
                        pltpu.SemaphoreType.DMA((1,))],
    )(*shards, small)


def _forward_halves(landed, *, name):
    n = len(landed)

    def body(*refs):
        o_refs = refs[n:2 * n]
        send_sems, recv_sems = refs[2 * n:]
        x, y, c = _mesh_pos()
        chips = _other_chips(x, y)
        pairs = [(a, k) for a in range(n) for k in range(3)]

        def copy(a, k, half):
            px, py = chips[k]
            blk = o_refs[a].at[2 * px + py, half]
            return pltpu.make_async_remote_copy(
                src_ref=blk, dst_ref=blk, send_sem=send_sems.at[3 * a + k], recv_sem=recv_sems.at[3 * a + k],
                device_id=(x, y, 1 - c), device_id_type=MESH)

        sends = [copy(a, k, c) for a, k in pairs]
        for cp in sends:
            cp.start()
        for a, k in pairs:
            copy(a, k, 1 - c).wait_recv()
        for cp in sends:
            cp.wait_send()

    return pl.pallas_call(
        body, name=name, in_specs=[HBM] * n, out_specs=[HBM] * n,
        out_shape=[jax.ShapeDtypeStruct(a.shape, a.dtype) for a in landed],
        input_output_aliases={a: a for a in range(n)},
        scratch_shapes=[pltpu.SemaphoreType.DMA((3 * n,)), pltpu.SemaphoreType.DMA((3 * n,))],
    )(*landed)


def _swap_other_half(g_list, *, name):
    n = len(g_list)

    def body(*refs):
        g_refs, o_refs = refs[:n], refs[n:2 * n]
        send_sems, recv_sems = refs[2 * n:]
        x, y, c = _mesh_pos()
        copies = [pltpu.make_async_remote_copy(
            src_ref=g_refs[a].at[:, 1 - c], dst_ref=o_refs[a], send_sem=send_sems.at[a], recv_sem=recv_sems.at[a],
            device_id=(x, y, 1 - c), device_id_type=MESH) for a in range(n)]
        for cp in copies:
            cp.start()
        for cp in copies:
            cp.wait()

    return pl.pallas_call(
        body, name=name, in_specs=[HBM] * n, out_specs=[HBM] * n,
        out_shape=[jax.ShapeDtypeStruct((g.shape[0],) + g.shape[2:], g.dtype) for g in g_list],
        scratch_shapes=[pltpu.SemaphoreType.DMA((n,)), pltpu.SemaphoreType.DMA((n,))],
    )(*g_list)


def _row_tile(r):
    return _pick(r, (512, 256, 128, 64, 32, 16, 8))


def _add_my_half(g4, sib4, core, *, name):
    n, _, r, C = g4.shape
    tr = _row_tile(r)

    def body(core_ref, g_ref, s_ref, o_ref):
        del core_ref
        o_ref[...] = (g_ref[...].astype(F32) + s_ref[...].astype(F32)).astype(o_ref.dtype)

    return pl.pallas_call(
        body, name=name,
        grid_spec=pltpu.PrefetchScalarGridSpec(
            num_scalar_prefetch=1, grid=(n, r // tr),
            in_specs=[pl.BlockSpec((None, None, tr, C), lambda j, i, core_ref: (j, core_ref[0], i, 0)),
                      pl.BlockSpec((None, tr, C), lambda j, i, core_ref: (j, i, 0))],
            out_specs=pl.BlockSpec((None, tr, C), lambda j, i, core_ref: (j, i, 0))),
        out_shape=jax.ShapeDtypeStruct((n, r, C), g4.dtype),
        compiler_params=pltpu.CompilerParams(dimension_semantics=("parallel", "parallel")),
    )(core, g4, sib4)


def _scatter_to_chips(p_list, *, name):
    n = len(p_list)

    def body(*refs):
        p_refs, o_refs = refs[:n], refs[n:2 * n]
        send_sems, recv_sems = refs[2 * n:]
        x, y, c = _mesh_pos()
        me = 2 * x + y
        chips = _other_chips(x, y)
        pairs = [(a, k) for a in range(n) for k in range(3)]

        def copy(a, k, landing_slot):
            px, py = chips[k]
            return pltpu.make_async_remote_copy(
                src_ref=p_refs[a].at[2 * px + py], dst_ref=o_refs[a].at[landing_slot],
                send_sem=send_sems.at[3 * a + k], recv_sem=recv_sems.at[3 * a + k], device_id=(px, py, c),
                device_id_type=MESH)

        sends = [copy(a, k, me) for a, k in pairs]
        for cp in sends:
            cp.start()
        for a, k in pairs:
            px, py = chips[k]
            copy(a, k, 2 * px + py).wait_recv()
        for cp in sends:
            cp.wait_send()

    return pl.pallas_call(
        body, name=name, in_specs=[HBM] * n, out_specs=[HBM] * n,
        out_shape=[jax.ShapeDtypeStruct(p.shape, p.dtype) for p in p_list],
        scratch_shapes=[pltpu.SemaphoreType.DMA((3 * n,)), pltpu.SemaphoreType.DMA((3 * n,))],
    )(*p_list)


def _sum_chips(landed, part, me, *, name):
    _, r, C = landed.shape
    tr = _row_tile(r)

    def body(me_ref, own_ref, r1_ref, r2_ref, r3_ref, o_ref):
        del me_ref
        f = lambda ref: ref[...].astype(F32)
        o_ref[...] = ((f(own_ref) + f(r1_ref)) + f(r2_ref)) + f(r3_ref)

    slot = lambda d: pl.BlockSpec((None, tr, C), lambda i, me_ref: ((me_ref[0] + d) % N_CHIPS, i, 0))
    return pl.pallas_call(
        body, name=name,
        grid_spec=pltpu.PrefetchScalarGridSpec(
            num_scalar_prefetch=1, grid=(r // tr,), in_specs=[slot(0), slot(1), slot(2), slot(3)],
            out_specs=pl.BlockSpec((tr, C), lambda i, me_ref: (i, 0))),
        out_shape=jax.ShapeDtypeStruct((r, C), F32),
        compiler_params=pltpu.CompilerParams(dimension_semantics=("parallel",)),
    )(me, part, landed, landed, landed)


def _adamw_halves(w, mine, theirs, m, v, core, *, layer, prev, name):
    shape = w.shape
    r, C = mine.shape
    tr = _pick(r, (128, 64, 32, 16, 8))
    per = r // tr
    view = lambda a: a.reshape(-1, C)
    n_prev = 0 if prev is None else 4

    def body(*refs):
        core_ref, w_ref, gm_ref, gt_ref, m_ref, v_ref = refs[:6]
        g_ref, d_ref, nm_ref, nv_ref = refs[6 + n_prev:]
        gv = jnp.where(pl.program_id(0) == core_ref[0], gm_ref[...], gt_ref[...])
        g_ref[...] = gv
        d_ref[...], nm_ref[...], nv_ref[...] = _adamw_math(w_ref[...], gv, m_ref[...], v_ref[...])

    half = pl.BlockSpec((tr, C), lambda h, i, core_ref: ((2 * layer + h) * per + i, 0))
    row = pl.BlockSpec((tr, C), lambda h, i, core_ref: (i, 0))
    out = jax.ShapeDtypeStruct((math.prod(shape) // C, C), F32)
    res = pl.pallas_call(
        body, name=name,
        grid_spec=pltpu.PrefetchScalarGridSpec(
            num_scalar_prefetch=1, grid=(2, per), in_specs=[half, row, row, half, half] + [HBM] * n_prev,
            out_specs=[half] * 4),
        out_shape=[out] * 4,
        input_output_aliases={6 + j: j for j in range(n_prev)},
        compiler_params=pltpu.CompilerParams(dimension_semantics=("parallel", "parallel")),
    )(core, view(w), mine, theirs, view(m), view(v), *([] if prev is None else [view(a) for a in prev]))
    return tuple(a.reshape(shape) for a in res)


def _sum_small(recv4, *, name):
    _, R, C = recv4.shape

    def body(r_ref, o_ref):
        o_ref[...] = ((r_ref[0] + r_ref[1]) + r_ref[2]) + r_ref[3]

    return pl.pallas_call(body, name=name, out_shape=jax.ShapeDtypeStruct((R, C), F32))(recv4)


def _add(a, b, *, name):
    R, C = a.shape
    tr = _pick(R, (512, 256, 128, 64, 32, 16, 8))
    blk = pl.BlockSpec((tr, C), lambda i: (i, 0))

    def body(a_ref, b_ref, o_ref):
        o_ref[...] = a_ref[...] + b_ref[...]

    return pl.pallas_call(body, name=name, grid=(R // tr,), in_specs=[blk, blk], out_specs=blk,
                          out_shape=jax.ShapeDtypeStruct((R, C), F32),
                          compiler_params=pltpu.CompilerParams(dimension_semantics=("parallel",)))(a, b)


def _adamw_math(w, g, m, v):
    nm = ADAM_B1 * m + (1.0 - ADAM_B1) * g
    nv = ADAM_B2 * v + (1.0 - ADAM_B2) * (g * g)
    m_hat = nm / (1.0 - ADAM_B1 ** ADAM_STEP)
    v_hat = nv / (1.0 - ADAM_B2 ** ADAM_STEP)
    return -ADAM_LR * (m_hat / (jnp.sqrt(v_hat) + ADAM_EPS) + ADAM_WD * w), nm, nv


def _adamw(w, g, m, v, *, name):
    shape = w.shape
    C = shape[-1]
    R = w.size // C
    two = lambda a: a.reshape(R, C)
    tr = _pick(R, (256, 128, 64, 32, 16, 8)) if R % 8 == 0 and R > 8 else R
    blk = pl.BlockSpec((tr, C), lambda i: (i, 0))

    def body(w_ref, g_ref, m_ref, v_ref, d_ref, nm_ref, nv_ref):
        d_ref[...], nm_ref[...], nv_ref[...] = _adamw_math(w_ref[...], g_ref[...], m_ref[...], v_ref[...])

    out = jax.ShapeDtypeStruct((R, C), F32)
    d, nm, nv = pl.pallas_call(
        body, name=name, grid=(R // tr,), in_specs=[blk] * 4, out_specs=[blk] * 3, out_shape=[out] * 3,
        compiler_params=pltpu.CompilerParams(dimension_semantics=("parallel",)),
    )(two(w), two(g), two(m), two(v))
    return d.reshape(shape), nm.reshape(shape), nv.reshape(shape)


BIG = (("dn_w_in", (2, 1024, 1540), 2), ("dn_w_out", (2, 512, 1024), 1), ("sb_w_in", (1, 1024, 1024), 2),
       ("sb_w_out", (1, 256, 1024), 1), ("sc_w_in", (1, 1024, 2048), 2), ("sc_w_out", (1, 512, 1024), 1))
SMALL = (("dn_conv_w", (2, 4, 1024), 2), ("dn_o_norm_g", (2, 64), 1), ("sc_conv_w", (1, 3, 512), 2))
REPL = (("norm_g", (4, 1024)), ("dn_a_log", (2, 8)), ("dn_dt_bias", (2, 8)), ("sb_q_norm_g", (1, 64)),
        ("sb_k_norm_g", (1, 64)))


def _halves(shard):
    return shard.reshape(2, -1, shard.shape[-1])


def _pack(arrays, cols, lead=()):
    flat = jnp.concatenate([a.reshape(lead + (-1,)) for a in arrays], axis=-1)
    n = flat.shape[-1]
    rows = -(-n // cols)
    unit = 512 if rows > 512 else 8
    rows = -(-rows // unit) * unit
    flat = jnp.pad(flat, [(0, 0)] * len(lead) + [(0, rows * cols - n)])
    return flat.reshape(lead + (rows, cols))


def _unpack(buf, table, lead=()):
    flat = buf.reshape(lead + (-1,))
    out, off = {}, 0
    for entry in table:
        name, shape = entry[0], entry[1]
        n = math.prod(shape)
        out[name] = flat[..., off:off + n].reshape(lead + shape)
        off += n
    return out


def _join(shards, axis):
    return jnp.concatenate([shards[j] for j in range(N_CHIPS)], axis=axis)


def _split(full, axis):
    return jnp.stack(jnp.split(full, N_CHIPS, axis=axis), axis=0)


def kernel(x, norm_g, dn_w_in, dn_conv_w, dn_a_log, dn_dt_bias, dn_o_norm_g, dn_w_out, sb_w_in, sb_q_norm_g, sb_k_norm_g, sb_w_out, sc_w_in, sc_conv_w, sc_w_out, loss_target, m_norm_g, m_dn_w_in, m_dn_conv_w, m_dn_a_log, m_dn_dt_bias, m_dn_o_norm_g, m_dn_w_out, m_sb_w_in, m_sb_q_norm_g, m_sb_k_norm_g, m_sb_w_out, m_sc_w_in, m_sc_conv_w, m_sc_w_out, v_norm_g, v_dn_w_in, v_dn_conv_w, v_dn_a_log, v_dn_dt_bias, v_dn_o_norm_g, v_dn_w_out, v_sb_w_in, v_sb_q_norm_g, v_sb_k_norm_g, v_sb_w_out, v_sc_w_in, v_sc_conv_w, v_sc_w_out):
    weights = dict(norm_g=norm_g, dn_w_in=dn_w_in, dn_conv_w=dn_conv_w, dn_a_log=dn_a_log, dn_dt_bias=dn_dt_bias,
                   dn_o_norm_g=dn_o_norm_g, dn_w_out=dn_w_out, sb_w_in=sb_w_in, sb_q_norm_g=sb_q_norm_g,
                   sb_k_norm_g=sb_k_norm_g, sb_w_out=sb_w_out, sc_w_in=sc_w_in, sc_conv_w=sc_conv_w, sc_w_out=sc_w_out)
    m_in = dict(norm_g=m_norm_g, dn_w_in=m_dn_w_in, dn_conv_w=m_dn_conv_w, dn_a_log=m_dn_a_log,
                dn_dt_bias=m_dn_dt_bias, dn_o_norm_g=m_dn_o_norm_g, dn_w_out=m_dn_w_out, sb_w_in=m_sb_w_in,
                sb_q_norm_g=m_sb_q_norm_g, sb_k_norm_g=m_sb_k_norm_g, sb_w_out=m_sb_w_out, sc_w_in=m_sc_w_in,
                sc_conv_w=m_sc_conv_w, sc_w_out=m_sc_w_out)
    v_in = dict(norm_g=v_norm_g, dn_w_in=v_dn_w_in, dn_conv_w=v_dn_conv_w, dn_a_log=v_dn_a_log,
                dn_dt_bias=v_dn_dt_bias, dn_o_norm_g=v_dn_o_norm_g, dn_w_out=v_dn_w_out, sb_w_in=v_sb_w_in,
                sb_q_norm_g=v_sb_q_norm_g, sb_k_norm_g=v_sb_k_norm_g, sb_w_out=v_sb_w_out, sc_w_in=v_sc_w_in,
                sc_conv_w=v_sc_conv_w, sc_w_out=v_sc_w_out)
    order = list(weights)
    xi, yi, ci = _mesh_pos()

    small = _pack([weights[n] for n, _, _ in SMALL], LANES)
    later = [("dn_w_in", 1), ("dn_w_out", 1), ("sb_w_in", 0), ("sb_w_out", 0), ("sc_w_in", 0), ("sc_w_out", 0)]
    piece = lambda n, l: _halves(weights[n][l].astype(BF16)[None])
    own_first = [piece("dn_w_in", 0), piece("dn_w_out", 0)]
    own_later = [piece(n, l) for n, l in later]
    own_last, own_mid = own_later[:2], own_later[2:]
    me = 2 * xi + yi
    whole = lambda g4, own: lax.dynamic_update_index_in_dim(g4, own, me, 0)
    flat = lambda g4: g4.reshape(N_CHIPS, -1, g4.shape[-1])
    rows_of = lambda w4: w4.reshape(-1, w4.shape[-1])
    dn_in = lambda w4: jnp.pad(_join(w4, 1), ((0, 0), (0, DN_IN_PAD - DN_IN)))
    w_in0, w_out0, small4 = _gather_halves(own_first, small, name="gather_first")
    full = {n: _join(a, ax) for (n, _, ax), a in zip(SMALL, _unpack(small4, SMALL, (N_CHIPS,)).values())}

    def dn_args(j, w_in4, w_out4):
        return (dn_in(flat(w_in4)), full["dn_conv_w"][j], dn_a_log[j], dn_dt_bias[j], full["dn_o_norm_g"][j],
                rows_of(w_out4))

    x0 = x[0]
    dn0 = dn_args(0, whole(w_in0, own_first[0]), whole(w_out0, own_first[1]))
    x1, s0, landed = _dn_layer_fwd(x0, norm_g[0], *dn0, "l0", send=own_mid)
    landed = _forward_halves(landed, name="forward_halves_mid")
    sb_in, sb_out, sc_in, sc_out = [whole(g4, own) for g4, own in zip(landed, own_mid)]
    sb_args = (flat(sb_in), sb_q_norm_g[0], sb_k_norm_g[0], rows_of(sb_out))
    sc_args = (flat(sc_in), full["sc_conv_w"][0], rows_of(sc_out))
    x2, s1, landed = _sb_layer_fwd(x1, norm_g[1], *sb_args, "l1", send=own_last)
    landed = _forward_halves(landed, name="forward_halves_last")
    dn1 = dn_args(1, *[whole(g4, own) for g4, own in zip(landed, own_last)])
    x3, s2 = _sc_layer_fwd(x2, norm_g[2], *sc_args, "l2")
    x4, s3, _ = _dn_layer_fwd(x3, norm_g[3], *dn1, "l3")
    dy, loss_local = _loss_head(x4, loss_target[0], name="loss_head")
    loss = lax.psum(loss_local[0, 0], ("x", "y", "c"))

    core = ci.astype(jnp.int32).reshape(1)
    chip = me.astype(jnp.int32).reshape(1)

    def chip_sums(g_list, tag):
        sib = _swap_other_half(g_list, name=f"swap_halves_{tag}")
        return [_add_my_half(g, s, core, name=f"sum_cores_{tag}{i}") for i, (g, s) in enumerate(zip(g_list, sib))]

    dx3, dng3, dwin3, dconv3, dal3, ddt3, dgain3, dwout3, _ = _dn_layer_bwd(dy, x3, norm_g[3], *dn1, s3, "l3")
    dx2, dng2, dwin2, dconv2, dwout2 = _sc_layer_bwd(dx3, x2, norm_g[2], *sc_args, s2, "l2")
    dx1, dng1, dwin1, dgq, dgk, dwout1 = _sb_layer_bwd(dx2, x1, norm_g[1], *sb_args, s1, "l1")
    part_later = chip_sums([_cut2(_by_cols(dwin3)), _cut2(_by_rows(dwout3)), _cut2(dwin1), _cut2(_by_rows(dwout1)),
                            _cut2(dwin2), _cut2(_by_rows(dwout2))], "later")
    dx0, dng0, part_win0, dconv0, dal0, ddt0, dgain0, part_wout0, landed = _dn_layer_bwd(
        dx1, x0, norm_g[0], *dn0, s0, "l0", send=part_later, chip_sums=chip_sums)
    pieces = later + [("dn_w_out", 0), ("dn_w_in", 0)]
    mine = {(n, l): _sum_chips(r, p, chip, name=f"sum_chips_{n}{l}")
            for (n, l), r, p in zip(pieces, landed, part_later + [part_wout0, part_win0])}
    pieces = sorted(pieces, key=lambda nl: nl[1])
    mine = [mine[nl] for nl in pieces]
    theirs = _sibling_exchange(mine, name="swap_results")
    upd = {}
    for (n, l), a, b in zip(pieces, mine, theirs):
        upd[n] = _adamw_halves(weights[n], a, b, m_in[n], v_in[n], core, layer=l, prev=upd.get(n),
                               name=f"adamw_{n}{l}")
    g_out = {n: upd[n][0] for n, _, _ in BIG}

    grads = dict(
        norm_g=jnp.concatenate([dng0, dng1, dng2, dng3], axis=0), dn_conv_w=jnp.stack([dconv0, dconv3]),
        dn_a_log=jnp.stack([dal0, dal3]), dn_dt_bias=jnp.stack([ddt0, ddt3]),
        dn_o_norm_g=jnp.stack([dgain0, dgain3]), sb_q_norm_g=dgq[None], sb_k_norm_g=dgk[None],
        sc_conv_w=dconv2[None])
    repl = [jnp.broadcast_to(grads[n][None], (N_CHIPS,) + s) for n, s in REPL]
    gsmall = _pack([_split(grads[n], ax) for n, _, ax in SMALL] + repl, LANES, (N_CHIPS,))
    rsmall, = _chip_exchange([gsmall], send_slot_is_dest=True, copy_own=(True,), name="scatter_small")
    psmall = _sum_small(rsmall, name="sum_chips_small")
    qsmall, = _sibling_exchange([psmall], name="swap_cores_small")
    tsmall = _add(psmall, qsmall, name="sum_cores_small")
    g_out.update(_unpack(tsmall, SMALL + REPL))

    for n in order:
        if n not in upd:
            upd[n] = (g_out[n],) + _adamw(weights[n], g_out[n], m_in[n], v_in[n], name=f"adamw_{n}")
    return (loss, dx0[None], *[upd[n][0] for n in order], *[upd[n][1] for n in order],
            *[upd[n][2] for n in order], *[upd[n][3] for n in order])
```

```python
import functools
import math

import jax
import jax.numpy as jnp
from jax import lax
from jax.experimental import pallas as pl
from jax.experimental.pallas import tpu as pltpu

F32 = jnp.float32
BF16 = jnp.bfloat16
MESH = pl.DeviceIdType.MESH

RMS_EPS = 1e-6
L2_EPS = 1e-6
LANES = 128
VMEM_BIG = 60 * 1024 * 1024
MM_VMEM = 44 * 1024 * 1024

DN_HEADS, DN_DK, DN_DV, DN_CHUNK, DN_CONV = 8, 128, 256, 64, 4
DN_QK_W = DN_HEADS * DN_DK
DN_V_W = DN_HEADS * DN_DV
DN_CONV_W = 2 * DN_QK_W + DN_V_W
DN_IN = DN_CONV_W + DN_V_W + 2 * DN_HEADS
DN_IN_PAD = DN_CONV_W + DN_V_W + LANES
SB_DH = 64
SC_CONV = 3

ADAM_LR, ADAM_B1, ADAM_B2, ADAM_EPS, ADAM_WD, ADAM_STEP = 0.001, 0.9, 0.999, 1e-08, 0.01, 10


def _pick(n, cands):
    for c in cands:
        if n % c == 0:
            return c
    raise ValueError(f"no tile for {n} in {cands}")


def _bf(x):
    return x.astype(BF16)


def _dot(a, b):
    return jnp.dot(_bf(a), _bf(b), preferred_element_type=F32)


def _dot_nt(a, b):
    return lax.dot_general(_bf(a), _bf(b), (((1,), (1,)), ((), ())), preferred_element_type=F32)


def _dot_tn(a, b):
    return lax.dot_general(_bf(a), _bf(b), (((0,), (0,)), ((), ())), preferred_element_type=F32)


def _split3(a):
    hi = _bf(a)
    r = a - hi.astype(F32)
    mid = _bf(r)
    lo = _bf(r - mid.astype(F32))
    return hi, mid, lo


def _sigmoid(x):
    return 1.0 / (1.0 + jnp.exp(-x))


def _silu(x):
    return x * _sigmoid(x)


def _dsilu(x):
    s = _sigmoid(x)
    return s * (1.0 + x * (1.0 - s))


def _softplus(x):
    return jnp.maximum(x, 0.0) + jnp.log(1.0 + jnp.exp(-jnp.abs(x)))


def _shift_down(z, k):
    if k == 0:
        return z
    row = lax.broadcasted_iota(jnp.int32, z.shape, 0)
    return jnp.where(row >= k, pltpu.roll(z, k, 0), 0.0)


def _shift_up(z, k):
    if k == 0:
        return z
    n = z.shape[0]
    row = lax.broadcasted_iota(jnp.int32, z.shape, 0)
    return jnp.where(row < n - k, pltpu.roll(z, n - k, 0), 0.0)


def _matmul(a, b, *, mode, name, res=None, a_parts=1, b_parts=1, out_parts=1, out_dtype=F32, send=()):
    def dims2(x, parts):
        if parts == 1:
            return x.shape
        assert x.shape[0] == parts
        return (x.shape[1], x.shape[2] * parts)

    ash, bsh = dims2(a, a_parts), dims2(b, b_parts)
    if mode == "nn":
        (M, K), (K2, N) = ash, bsh
        dn = (((1,), (0,)), ((), ()))
    elif mode == "nt":
        (M, K), (N, K2) = ash, bsh
        dn = (((1,), (1,)), ((), ()))
    else:
        (K, M), (K2, N) = ash, bsh
        dn = (((0,), (0,)), ((), ()))
    assert K == K2, (ash, bsh, mode)
    tm_max = _pick(M, (512, 256, 128, 64, 32, 16, 8))
    n_unit = N // max(out_parts, b_parts if mode != "nt" else 1)
    k_unit = K // max(a_parts if mode != "tn" else 1, b_parts if mode == "nt" else 1)
    tm, tn, tk = min(
        ((m, n, k) for m in {tm_max, max(tm_max // 2, 8)}
         for n in (2048, 1792, 1024, 896, 768, 512, 384, 256, 128) if n_unit % n == 0
         for k in (k_unit, 2048, 1792, 1024, 896, 512, 256, 128) if k_unit % k == 0
         if 2 * (m * k * a.dtype.itemsize + k * n * b.dtype.itemsize + 2 * m * n * 4) + m * n * 4 <= MM_VMEM),
        key=lambda t: (-t[0] * t[1] * t[2], -t[0], -t[2]))
    nk = K // tk
    grid = (M // tm, N // tn, nk)

    def spec(parts, rows_are, cols_are, tr, tc, width):
        per = width // parts // tc
        if parts == 1:
            return pl.BlockSpec((tr, tc), lambda i, j, k: ((i, j, k)[rows_are], (i, j, k)[cols_are]))
        return pl.BlockSpec((None, tr, tc), lambda i, j, k: ((i, j, k)[cols_are] // per, (i, j, k)[rows_are],
                                                             (i, j, k)[cols_are] % per))

    if mode == "nn":
        a_spec = spec(a_parts, 0, 2, tm, tk, K)
        b_spec = spec(b_parts, 2, 1, tk, tn, N)
    elif mode == "nt":
        a_spec = spec(a_parts, 0, 2, tm, tk, K)
        b_spec = spec(b_parts, 1, 2, tn, tk, K)
    else:
        a_spec = spec(a_parts, 2, 0, tk, tm, M)
        b_spec = spec(b_parts, 2, 1, tk, tn, N)
    o_spec = spec(out_parts, 0, 1, tm, tn, N)
    in_specs = [a_spec, b_spec]
    operands = [a, b]
    if res is not None:
        in_specs.append(pl.BlockSpec((tm, tn), lambda i, j, k: (i, j)))
        operands.append(res)

    n_in = len(operands)
    ns = len(send)

    def finish(refs, r):
        if res is not None:
            r = refs[2][...] + r
        refs[n_in + ns][...] = r.astype(out_dtype)

    def body(*refs):
        if ns:
            at = lambda step: functools.reduce(jnp.logical_and, [pl.program_id(d) == step[d] for d in range(3)])
            _blocks_over_ici(refs[n_in:n_in + ns], refs[n_in + ns + 1:n_in + 2 * ns + 1], refs[-2], refs[-1],
                             at((0, 0, 0)), at(tuple(g - 1 for g in grid)))
        part = lax.dot_general(_bf(refs[0][...]), _bf(refs[1][...]), dn, preferred_element_type=F32)
        if nk == 1:
            finish(refs, part)
            return
        acc_ref = refs[n_in + 2 * ns + 1]
        k = pl.program_id(2)

        @pl.when(k == 0)
        def _():
            acc_ref[...] = part

        @pl.when(jnp.logical_and(k > 0, k < nk - 1))
        def _():
            acc_ref[...] += part

        @pl.when(k == nk - 1)
        def _():
            finish(refs, acc_ref[...] + part)

    out_shape = (M, N) if out_parts == 1 else (out_parts, M, N // out_parts)
    out = pl.pallas_call(
        body, name=name, grid=grid, in_specs=in_specs + [HBM] * ns, out_specs=[o_spec] + [HBM] * ns,
        out_shape=[jax.ShapeDtypeStruct(out_shape, out_dtype)] + [jax.ShapeDtypeStruct(x.shape, x.dtype) for x in send],
        scratch_shapes=([pltpu.VMEM((tm, tn), F32)] if nk > 1 else [])
        + ([pltpu.SemaphoreType.DMA((3 * ns,)), pltpu.SemaphoreType.DMA((3 * ns,))] if ns else []),
        compiler_params=pltpu.CompilerParams(
            dimension_semantics=("arbitrary",) * 3 if ns else ("parallel", "parallel", "arbitrary"),
            vmem_limit_bytes=VMEM_BIG),
    )(*operands, *send)
    return out if ns else out[0]


def _rmsnorm_fwd(x, g, *, name):
    T, D = x.shape
    tm = _pick(T, (512, 256, 128, 64, 32, 16))

    def body(x_ref, g_ref, h_ref):
        xv = x_ref[...]
        r = lax.rsqrt(jnp.mean(xv * xv, axis=-1, keepdims=True) + RMS_EPS)
        h_ref[...] = ((xv * r) * g_ref[...]).astype(BF16)

    return pl.pallas_call(
        body, name=name, grid=(T // tm,),
        in_specs=[pl.BlockSpec((tm, D), lambda i: (i, 0)), pl.BlockSpec((1, D), lambda i: (0, 0))],
        out_specs=pl.BlockSpec((tm, D), lambda i: (i, 0)),
        out_shape=jax.ShapeDtypeStruct((T, D), BF16),
    )(x, g.reshape(1, D))


def _rmsnorm_bwd(x, g, dh, dx_in, *, name):
    T, D = x.shape
    tm = _pick(T, (512, 256, 128, 64, 32, 16))

    def body(x_ref, g_ref, dh_ref, dxin_ref, dx_ref, dg_ref):
        @pl.when(pl.program_id(0) == 0)
        def _():
            dg_ref[...] = jnp.zeros_like(dg_ref)

        xv = x_ref[...]
        r = lax.rsqrt(jnp.mean(xv * xv, axis=-1, keepdims=True) + RMS_EPS)
        xh = xv * r
        dh_v = dh_ref[...]
        dxh = dh_v * g_ref[...]
        dx_ref[...] = dxin_ref[...] + r * (dxh - xh * jnp.mean(dxh * xh, axis=-1, keepdims=True))
        dg_ref[...] += jnp.sum(dh_v * xh, axis=0, keepdims=True)

    row = pl.BlockSpec((tm, D), lambda i: (i, 0))
    vec = pl.BlockSpec((1, D), lambda i: (0, 0))
    return pl.pallas_call(
        body, name=name, grid=(T // tm,),
        in_specs=[row, vec, row, row], out_specs=[row, vec],
        out_shape=[jax.ShapeDtypeStruct((T, D), F32), jax.ShapeDtypeStruct((1, D), F32)],
        compiler_params=pltpu.CompilerParams(dimension_semantics=("arbitrary",)),
    )(x, g.reshape(1, D), dh, dx_in)


def _loss_head(y, target, *, name):
    T, D = y.shape
    tm = _pick(T, (512, 256, 128, 64, 32, 16))

    def body(y_ref, t_ref, dy_ref, l_ref):
        @pl.when(pl.program_id(0) == 0)
        def _():
            l_ref[...] = jnp.zeros_like(l_ref)

        err = y_ref[...] - t_ref[...]
        dy_ref[...] = err * (1.0 / D)
        l_ref[...] += 0.5 * jnp.sum(jnp.mean(err * err, axis=-1, keepdims=True), axis=0, keepdims=True)

    row = pl.BlockSpec((tm, D), lambda i: (i, 0))
    return pl.pallas_call(
        body, name=name, grid=(T // tm,),
        in_specs=[row, row], out_specs=[row, pl.BlockSpec((1, 1), lambda i: (0, 0))],
        out_shape=[jax.ShapeDtypeStruct((T, D), F32), jax.ShapeDtypeStruct((1, 1), F32)],
        compiler_params=pltpu.CompilerParams(dimension_semantics=("arbitrary",)),
    )(y, target)


def _sc_mid_fwd(p3, conv_w, *, name):
    _, T, W = p3.shape
    K = conv_w.shape[0]
    cw = LANES

    def body(p_ref, w_ref, o_ref):
        z = p_ref[1] * p_ref[2]
        cv = sum(w_ref[i:i + 1, :] * _shift_down(z, K - 1 - i) for i in range(K))
        o_ref[...] = ((p_ref[0] * cv) * _silu(p_ref[3])).astype(BF16)

    return pl.pallas_call(
        body, name=name, grid=(W // cw,),
        in_specs=[pl.BlockSpec((4, T, cw), lambda j: (0, 0, j)), pl.BlockSpec((K, cw), lambda j: (0, j))],
        out_specs=pl.BlockSpec((T, cw), lambda j: (0, j)),
        out_shape=jax.ShapeDtypeStruct((T, W), BF16),
        compiler_params=pltpu.CompilerParams(dimension_semantics=("parallel",), vmem_limit_bytes=VMEM_BIG),
    )(p3, conv_w)


def _sc_mid_bwd(p3, conv_w, do, *, name):
    _, T, W = p3.shape
    K = conv_w.shape[0]
    cw = LANES

    def body(p_ref, w_ref, do_ref, dp_ref, dw_ref):
        b, c, u, gate = p_ref[0], p_ref[1], p_ref[2], p_ref[3]
        z = c * u
        zs = [_shift_down(z, K - 1 - i) for i in range(K)]
        cv = sum(w_ref[i:i + 1, :] * zs[i] for i in range(K))
        y = b * cv
        dov = do_ref[...]
        dy = dov * _silu(gate)
        dp_ref[3] = dov * y * _dsilu(gate)
        dp_ref[0] = dy * cv
        dcv = dy * b
        dz = sum(w_ref[i:i + 1, :] * _shift_up(dcv, K - 1 - i) for i in range(K))
        dp_ref[1] = dz * u
        dp_ref[2] = dz * c
        for i in range(K):
            dw_ref[i:i + 1, :] = jnp.sum(dcv * zs[i], axis=0, keepdims=True)

    return pl.pallas_call(
        body, name=name, grid=(W // cw,),
        in_specs=[pl.BlockSpec((4, T, cw), lambda j: (0, 0, j)), pl.BlockSpec((K, cw), lambda j: (0, j)),
                  pl.BlockSpec((T, cw), lambda j: (0, j))],
        out_specs=[pl.BlockSpec((4, T, cw), lambda j: (0, 0, j)), pl.BlockSpec((K, cw), lambda j: (0, j))],
        out_shape=[jax.ShapeDtypeStruct((4, T, W), F32), jax.ShapeDtypeStruct((K, W), F32)],
        compiler_params=pltpu.CompilerParams(dimension_semantics=("parallel",), vmem_limit_bytes=VMEM_BIG),
    )(p3, conv_w, do)


def _sc_layer_fwd(x, ng, w_in, conv_w, w_out, tag):
    h = _rmsnorm_fwd(x, ng, name=f"{tag}_norm")
    p3 = _matmul(h, w_in, mode="nn", b_parts=4, out_parts=4, name=f"{tag}_inproj")
    og = _sc_mid_fwd(p3, conv_w, name=f"{tag}_mid")
    x_new = _matmul(og, w_out, mode="nn", res=x, name=f"{tag}_outproj")
    return x_new, (h, p3, og)


def _sc_layer_bwd(dx, x, ng, w_in, conv_w, w_out, saved, tag):
    h, p3, og = saved
    d_wout = _matmul(og, dx, mode="tn", out_dtype=BF16, name=f"{tag}_dwout")
    dog = _matmul(dx, w_out, mode="nt", name=f"{tag}_dog")
    dp3, dconv = _sc_mid_bwd(p3, conv_w, dog, name=f"{tag}_midbwd")
    d_win = _matmul(h, dp3, mode="tn", b_parts=4, out_parts=4, out_dtype=BF16, name=f"{tag}_dwin")
    dh = _matmul(dp3, w_in, mode="nt", a_parts=4, b_parts=4, name=f"{tag}_dh")
    dx_prev, dng = _rmsnorm_bwd(x, ng, dh, dx, name=f"{tag}_normbwd")
    return dx_prev, dng, d_win, dconv, d_wout


SB_BQ = 256
SB_BK = 256
SB_ROWS = 512
SB_DEAD = -110.0


def _sb_half_mask():
    return lax.broadcasted_iota(jnp.int32, (1, LANES), 1) < SB_DH


def _sb_headnorm(x, g, lo):
    x2 = x * x
    s_lo = jnp.sum(jnp.where(lo, x2, 0.0), axis=-1, keepdims=True)
    s_hi = jnp.sum(jnp.where(lo, 0.0, x2), axis=-1, keepdims=True)
    r = lax.rsqrt(jnp.where(lo, s_lo, s_hi) * (1.0 / SB_DH) + RMS_EPS)
    xh = x * r
    return xh * g, xh, r


def _dot_x2_l(a_l, b_exact_bf16):
    his = [_bf(a) for a in a_l]
    mids = [_bf(a - h.astype(F32)) for a, h in zip(a_l, his)]
    f = lambda p: jnp.dot(p, b_exact_bf16, preferred_element_type=F32)
    return [x + y for x, y in zip([f(h) for h in his], [f(m) for m in mids])]


def _sb_stack(xb, lo):
    zero = jnp.zeros_like(xb)
    return jnp.concatenate([jnp.where(lo, xb, zero), jnp.where(lo, zero, xb)], axis=0)


def _sb_rel(bq, bk):
    row = lax.broadcasted_iota(jnp.int32, (2 * bq, bk), 0)
    col = lax.broadcasted_iota(jnp.int32, (2 * bq, bk), 1)
    return col - jnp.where(row >= bq, row - bq, row)


def _sb_tile(qm, kb, valid):
    z = lax.dot_general(qm, kb, (((1,), (1,)), ((), ())), preferred_element_type=F32)
    sp = _softplus(z)
    return z - sp, (-sp if valid is None else jnp.where(valid, -sp, 0.0))


def _sb_attn_fwd(p3, gq2, gk2, *, name, send=()):
    _, T, W = p3.shape
    bq, bk = min(SB_BQ, T), min(SB_BK, T)
    rows = min(SB_ROWS, T)
    scale = SB_DH ** -0.5
    ns = len(send)
    npair = W // LANES

    def body(*refs):
        p_ref, gq_ref, gk_ref = refs[:3]
        og_ref, o_ref = refs[3 + ns:5 + ns]
        qn_ref, kn_ref, v_ref = refs[5 + 2 * ns:8 + 2 * ns]
        if ns:
            _halves_over_ici(refs[3:3 + ns], refs[5 + ns:5 + 2 * ns], refs[8 + 2 * ns], refs[9 + 2 * ns],
                             pl.program_id(0) == 0, pl.program_id(0) == npair - 1)
        lo = _sb_half_mask()

        def prologue(i, c):
            r0 = pl.multiple_of(i * rows, rows)
            sl = pl.ds(r0, rows)
            qn_ref[sl, :] = (_sb_headnorm(p_ref[0, sl, :], gq_ref[...], lo)[0] * scale).astype(BF16)
            kn_ref[sl, :] = _sb_headnorm(p_ref[1, sl, :], gk_ref[...], lo)[0].astype(BF16)
            v_ref[sl, :] = p_ref[2, sl, :].astype(BF16)
            return c

        lax.fori_loop(0, T // rows, prologue, 0)

        rel = _sb_rel(bq, bk)
        tri = (lax.broadcasted_iota(jnp.int32, (bk, bk), 0)
               > lax.broadcasted_iota(jnp.int32, (bk, bk), 1)).astype(BF16)

        def qblock(qi, c):
            q0 = pl.multiple_of(qi * bq, bq)
            qm = _sb_stack(qn_ref[pl.ds(q0, bq), :], lo)
            nkb = (q0 + bq - 1) // bk + 1

            def tiles(k0s, carry, valids):
                o_acc, a_carry = carry
                sc = [_sb_tile(qm, kn_ref[pl.ds(k0, bk), :], valid) for k0, valid in zip(k0s, valids)]
                later = _dot_x2_l([log1m for _, log1m in sc], tri)
                for (logsig, log1m), lat, k0, valid in zip(sc, later, k0s, valids):
                    wts = jnp.exp(logsig + (lat + a_carry))
                    if valid is not None:
                        wts = jnp.where(valid, wts, 0.0)
                    o_acc = o_acc + jnp.dot(_bf(wts), v_ref[pl.ds(k0, bk), :], preferred_element_type=F32)
                    a_carry = a_carry + jnp.sum(log1m, axis=-1, keepdims=True)
                return o_acc, a_carry

            blk0 = lambda j: pl.multiple_of(j * bk, bk)
            k_last = blk0(nkb - 1)
            o2, t2 = tiles([k_last, blk0(jnp.maximum(nkb - 2, 0))],
                           (jnp.zeros((2 * bq, LANES), F32), jnp.zeros((2 * bq, 1), F32)),
                           [rel < q0 - k_last, nkb >= 2])

            def alive(st):
                return jnp.logical_and(st[0] < nkb - 1, jnp.max(st[2]) > SB_DEAD)

            def back_one(st):
                return (st[0] + 1,) + tiles([blk0(nkb - 2 - st[0])], st[1:], [None])

            _, o2, _ = lax.while_loop(alive, back_one, (jnp.int32(1), o2, t2))
            o = jnp.where(lo, o2[:bq], o2[bq:])
            o_ref[pl.ds(q0, bq), :] = o
            og_ref[pl.ds(q0, bq), :] = (o * _silu(p_ref[3, pl.ds(q0, bq), :])).astype(BF16)
            return c

        lax.fori_loop(0, T // bq, qblock, 0)

    colblk = pl.BlockSpec((T, LANES), lambda j: (0, j))
    vec = pl.BlockSpec((1, LANES), lambda j: (0, 0))
    return pl.pallas_call(
        body, name=name, grid=(npair,),
        in_specs=[pl.BlockSpec((4, T, LANES), lambda j: (0, 0, j)), vec, vec] + [HBM] * ns,
        out_specs=[colblk, colblk] + [HBM] * ns,
        out_shape=[jax.ShapeDtypeStruct((T, W), BF16), jax.ShapeDtypeStruct((T, W), F32)]
        + [jax.ShapeDtypeStruct((N_CHIPS,) + a.shape, a.dtype) for a in send],
        scratch_shapes=[pltpu.VMEM((T, LANES), BF16)] * 3
        + ([pltpu.SemaphoreType.DMA((3 * ns,)), pltpu.SemaphoreType.DMA((3 * ns,))] if ns else []),
        compiler_params=pltpu.CompilerParams(dimension_semantics=("arbitrary",), vmem_limit_bytes=VMEM_BIG),
    )(p3, gq2, gk2, *send)


def _sb_attn_bwd(p3, gq2, gk2, o, dog, *, name):
    _, T, W = p3.shape
    bq, bk = min(SB_BQ, T), min(SB_BK, T)
    rows = min(SB_ROWS, T)
    scale = SB_DH ** -0.5

    def body(p_ref, gq_ref, gk_ref, o_ref, dog_ref, dp_ref, dgq_ref, dgk_ref,
             qn_ref, kn_ref, v_ref, do_ref):
        lo = _sb_half_mask()

        def prologue(i, c):
            r0 = pl.multiple_of(i * rows, rows)
            sl = pl.ds(r0, rows)
            qn_ref[sl, :] = (_sb_headnorm(p_ref[0, sl, :], gq_ref[...], lo)[0] * scale).astype(BF16)
            kn_ref[sl, :] = _sb_headnorm(p_ref[1, sl, :], gk_ref[...], lo)[0].astype(BF16)
            v_ref[sl, :] = p_ref[2, sl, :].astype(BF16)
            gate = p_ref[3, sl, :]
            dogv = dog_ref[sl, :]
            dp_ref[3, sl, :] = dogv * o_ref[sl, :] * _dsilu(gate)
            do_ref[sl, :] = (dogv * _silu(gate)).astype(BF16)
            zero = jnp.zeros((rows, LANES), F32)
            dp_ref[0, sl, :] = zero
            dp_ref[1, sl, :] = zero
            dp_ref[2, sl, :] = zero
            return c

        lax.fori_loop(0, T // rows, prologue, 0)

        rel = _sb_rel(bq, bk)
        rj = lax.broadcasted_iota(jnp.int32, (bk, bk), 0)
        cj = lax.broadcasted_iota(jnp.int32, (bk, bk), 1)
        upto = (rj <= cj).astype(BF16)
        before_m = (rj < cj).astype(BF16)

        def qblock(qi, c):
            q0 = pl.multiple_of(qi * bq, bq)
            qm = _sb_stack(qn_ref[pl.ds(q0, bq), :], lo)
            dom = _sb_stack(do_ref[pl.ds(q0, bq), :], lo)
            nkb = (q0 + bq - 1) // bk + 1
            blk0 = lambda j: pl.multiple_of(j * bk, bk)
            k_last = blk0(nkb - 1)

            def row_sums(k0, valid):
                return jnp.sum(_sb_tile(qm, kn_ref[pl.ds(k0, bk), :], valid)[1], axis=-1, keepdims=True)

            def alive(st):
                return jnp.logical_and(st[0] < nkb, jnp.max(st[1]) > SB_DEAD)

            def back_one(st):
                return st[0] + 1, st[1] + row_sums(blk0(nkb - 1 - st[0]), None)

            n_live, total = lax.while_loop(alive, back_one, (jnp.int32(1), row_sums(k_last, rel < q0 - k_last)))
            k_first = nkb - n_live

            def tiles(k0s, carry, valids):
                dq_acc, a_pre, r_pre = carry
                kss = [pl.ds(k0, bk) for k0 in k0s]
                kbs = [kn_ref[ks, :] for ks in kss]
                sc = [_sb_tile(qm, kb, valid) for kb, valid in zip(kbs, valids)]
                dws = [lax.dot_general(dom, v_ref[ks, :], _NT, preferred_element_type=F32) for ks in kss]
                upto_l = _dot_x2_l([log1m for _, log1m in sc], upto)
                wts_l = []
                for (logsig, log1m), up, valid in zip(sc, upto_l, valids):
                    wts = jnp.exp(logsig + ((total - a_pre) - up))
                    wts_l.append(wts if valid is None else jnp.where(valid, wts, 0.0))
                    a_pre = a_pre + jnp.sum(log1m, axis=-1, keepdims=True)
                ee_l = [dw * wts for dw, wts in zip(dws, wts_l)]
                before_l = _dot_x2_l(ee_l, before_m)
                for (logsig, _), ks, kb, wts, ee, bef, valid in zip(sc, kss, kbs, wts_l, ee_l, before_l, valids):
                    beta = jnp.exp(logsig)
                    dz = ee * (1.0 - beta) - beta * (r_pre + bef)
                    if valid is not None:
                        dz = jnp.where(valid, dz, 0.0)
                    dzb = _bf(dz)
                    dq_acc = dq_acc + jnp.dot(dzb, kb, preferred_element_type=F32)
                    dp_ref[1, ks, :] += lax.dot_general(dzb, qm, _TN, preferred_element_type=F32)
                    dp_ref[2, ks, :] += lax.dot_general(_bf(wts), dom, _TN, preferred_element_type=F32)
                    r_pre = r_pre + jnp.sum(ee, axis=-1, keepdims=True)
                return dq_acc, a_pre, r_pre

            cr = (jnp.zeros((2 * bq, LANES), F32), jnp.zeros((2 * bq, 1), F32), jnp.zeros((2 * bq, 1), F32))
            n_before = jnp.maximum(n_live - 2, 0)
            cr = lax.fori_loop(0, n_before % 2, lambda t, cr: tiles([blk0(k_first)], cr, [None]), cr)
            k_pairs = k_first + n_before % 2
            cr = lax.fori_loop(0, n_before // 2,
                               lambda t, cr: tiles([blk0(k_pairs + 2 * t), blk0(k_pairs + 2 * t + 1)], cr,
                                                   [None, None]), cr)
            dq2, _, _ = tiles([blk0(jnp.maximum(nkb - 2, 0)), k_last], cr, [n_live >= 2, rel < q0 - k_last])
            dp_ref[0, pl.ds(q0, bq), :] = jnp.where(lo, dq2[:bq], dq2[bq:]) * scale
            return c

        lax.fori_loop(0, T // bq, qblock, 0)

        dgq_ref[...] = jnp.zeros_like(dgq_ref)
        dgk_ref[...] = jnp.zeros_like(dgk_ref)

        def epilogue(i, c):
            r0 = pl.multiple_of(i * rows, rows)
            sl = pl.ds(r0, rows)
            for part, g_ref, dg_ref in ((0, gq_ref, dgq_ref), (1, gk_ref, dgk_ref)):
                _, xh, r = _sb_headnorm(p_ref[part, sl, :], g_ref[...], lo)
                dn = dp_ref[part, sl, :]
                dxh = dn * g_ref[...]
                prod = dxh * xh
                m_lo = jnp.sum(jnp.where(lo, prod, 0.0), axis=-1, keepdims=True)
                m_hi = jnp.sum(jnp.where(lo, 0.0, prod), axis=-1, keepdims=True)
                m = jnp.where(lo, m_lo, m_hi) * (1.0 / SB_DH)
                dp_ref[part, sl, :] = r * (dxh - xh * m)
                dg_ref[...] += jnp.sum(dn * xh, axis=0, keepdims=True)
            return c

        lax.fori_loop(0, T // rows, epilogue, 0)

    colblk = pl.BlockSpec((T, LANES), lambda j: (0, j))
    vec = pl.BlockSpec((1, LANES), lambda j: (0, 0))
    part = pl.BlockSpec((4, T, LANES), lambda j: (0, 0, j))
    gvec = pl.BlockSpec((None, 1, LANES), lambda j: (j, 0, 0))
    npair = W // LANES
    return pl.pallas_call(
        body, name=name, grid=(npair,),
        in_specs=[part, vec, vec, colblk, colblk],
        out_specs=[part, gvec, gvec],
        out_shape=[jax.ShapeDtypeStruct((4, T, W), F32), jax.ShapeDtypeStruct((npair, 1, LANES), F32),
                   jax.ShapeDtypeStruct((npair, 1, LANES), F32)],
        scratch_shapes=[pltpu.VMEM((T, LANES), BF16)] * 4,
        compiler_params=pltpu.CompilerParams(dimension_semantics=("parallel",), vmem_limit_bytes=VMEM_BIG),
    )(p3, gq2, gk2, o, dog)


_NN = (((1,), (0,)), ((), ()))
_NT = (((1,), (1,)), ((), ()))
_TN = (((0,), (0,)), ((), ()))
DN_TB = 512
DN_HEADS_FWD = 4
DN_HEADS_BWD = 4
DN_INV_EXACT_LEVELS = 2
DN_AB_COL = (DN_CONV_W + DN_V_W) // LANES


def _dn_conv(x, w_ref):
    k = w_ref.shape[0]
    return sum(w_ref[i:i + 1, :] * _shift_down(x, k - 1 - i) for i in range(k))


def _dn_prep_fwd(p, conv_w, *, name):
    T = p.shape[0]
    cw = conv_w.shape[1]
    n_qk = 2 * DN_QK_W // LANES

    def body(p_ref, w_ref, o_ref):
        s = _silu(_dn_conv(p_ref[...], w_ref))
        r = lax.rsqrt(jnp.sum(s * s, axis=-1, keepdims=True) + L2_EPS)
        o_ref[...] = jnp.where(pl.program_id(0) < n_qk, s * r, s)

    colblk = pl.BlockSpec((T, LANES), lambda j: (0, j))
    return pl.pallas_call(
        body, name=name, grid=(cw // LANES,),
        in_specs=[colblk, pl.BlockSpec((DN_CONV, LANES), lambda j: (0, j))],
        out_specs=colblk, out_shape=jax.ShapeDtypeStruct((T, cw), F32),
        compiler_params=pltpu.CompilerParams(dimension_semantics=("parallel",), vmem_limit_bytes=VMEM_BIG),
    )(p, conv_w)


def _dn_chunk_tri(rows, upper):
    r = lax.broadcasted_iota(jnp.int32, (rows, rows), 0)
    c = lax.broadcasted_iota(jnp.int32, (rows, rows), 1)
    same = (r // DN_CHUNK) == (c // DN_CHUNK)
    return jnp.logical_and(same, (c >= r) if upper else (c <= r)).astype(BF16)


def _dn_lane_rows(a_log, dt_bias):
    pad = lambda v: jnp.zeros((1, LANES), F32).at[0, :DN_HEADS].set(v)
    return pad(a_log), pad(dt_bias)


def _dn_ab_parts(blk, alog_row, dtb_row):
    lane = lax.broadcasted_iota(jnp.int32, (1, LANES), 1)
    is_a = lane < DN_HEADS
    is_b = jnp.logical_and(lane >= DN_HEADS, lane < 2 * DN_HEADS)
    a_arg = jnp.where(is_a, blk + dtb_row, 0.0)
    neg_exp = jnp.where(is_a, -jnp.exp(alog_row), 0.0)
    log_a = neg_exp * _softplus(a_arg)
    beta = jnp.where(is_b, _sigmoid(blk), 0.0)
    return is_a, is_b, a_arg, neg_exp, log_a, beta


def _dn_ab_fwd(p, alog_row, dtb_row, *, name):
    T = p.shape[0]
    rows = min(DN_TB, T)

    def body(p_ref, al_ref, dt_ref, o_ref):
        _, _, _, _, log_a, beta = _dn_ab_parts(p_ref[...], al_ref[...], dt_ref[...])
        hi, mid, lo_ = _split3(log_a)
        tri = _dn_chunk_tri(rows, upper=False)
        f = lambda q: jnp.dot(tri, q, preferred_element_type=F32)
        o_ref[...] = (f(hi) + f(mid) + f(lo_)) + beta

    blk = pl.BlockSpec((rows, LANES), lambda i: (i, DN_AB_COL))
    vec = pl.BlockSpec((1, LANES), lambda i: (0, 0))
    return pl.pallas_call(
        body, name=name, grid=(T // rows,), in_specs=[blk, vec, vec],
        out_specs=pl.BlockSpec((rows, LANES), lambda i: (i, 0)),
        out_shape=jax.ShapeDtypeStruct((T, LANES), F32),
        compiler_params=pltpu.CompilerParams(dimension_semantics=("parallel",)),
    )(p, alog_row, dtb_row)


def _hp_l(a_l, b_l, dims=_NN):
    sa = [_split3(a)[:2] for a in a_l]
    sb = [_split3(b)[:2] for b in b_l]
    f = lambda p, q: lax.dot_general(p, q, dims, preferred_element_type=F32)
    hh = [f(x[0], y[0]) for x, y in zip(sa, sb)]
    hm = [f(x[0], y[1]) for x, y in zip(sa, sb)]
    mh = [f(x[1], y[0]) for x, y in zip(sa, sb)]
    return [a + (b + c) for a, b, c in zip(hh, hm, mh)]


def _dn_local(qs, k, v, g, beta, nc):
    c = DN_CHUNK
    cut = lambda x: [x[i * c:(i + 1) * c] for i in range(nc)]
    row = lax.broadcasted_iota(jnp.int32, (c, c), 0)
    col = lax.broadcasted_iota(jnp.int32, (c, c), 1)
    eye, lower, strict = row == col, row >= col, row > col
    rowid = lax.broadcasted_iota(jnp.int32, (c, 1), 0)
    eg = jnp.exp(g)
    kb = k * beta
    rhs_k = kb * eg
    g_l, k_l, kb_l, qs_l = cut(g), cut(k), cut(kb), cut(qs)
    g_row_l = [jnp.sum(jnp.where(eye, x, 0.0), axis=0, keepdims=True) for x in g_l]
    dec_l = [jnp.where(lower, jnp.exp(jnp.where(lower, x - y, 0.0)), 0.0) for x, y in zip(g_l, g_row_l)]
    kk_l = [_dot_nt(a, b) for a, b in zip(kb_l, k_l)]
    qk_l = [_dot_nt(a, b) for a, b in zip(qs_l, k_l)]
    low_l = [jnp.where(strict, a * d, 0.0) for a, d in zip(kk_l, dec_l)]
    eye_f = eye.astype(F32)
    pw_l = [-x for x in low_l]
    inv_l = [eye_f + x for x in pw_l]
    plain = lambda a_l, b_l: [_dot(a, b) for a, b in zip(a_l, b_l)]
    for level in range(int(math.log2(c)) - 1):
        mul = _hp_l if level < DN_INV_EXACT_LEVELS else plain
        pw_l = mul(pw_l, pw_l)
        inv_l = [a + b for a, b in zip(inv_l, mul(inv_l, pw_l))]
    u_l = [_dot(a, b) for a, b in zip(inv_l, cut(v * beta))]
    w_l = [_dot(a, b) for a, b in zip(inv_l, cut(rhs_k))]
    aqk_l = [jnp.where(lower, a * d, 0.0) for a, d in zip(qk_l, dec_l)]
    g_last_l = [jnp.sum(jnp.where(rowid == c - 1, x, 0.0), axis=0, keepdims=True) for x in g_l]
    ekd_l = [jnp.exp(a - b) for a, b in zip(g_last_l, g_l)]
    kd_l = [a * b for a, b in zip(k_l, ekd_l)]
    qd_l = cut(qs * eg)
    kw_l = [_dot_tn(a, b) for a, b in zip(kd_l, w_l)]
    qp_l = [q - _dot(a, w) for q, a, w in zip(qd_l, aqk_l, w_l)]
    return dict(eye=eye, lower=lower, strict=strict, dec=dec_l, k=k_l, kb=kb_l, qs=qs_l, low=low_l, inv=inv_l,
                eg=cut(eg), rhs_k=cut(rhs_k), u=u_l, w=w_l, aqk=aqk_l, g_last=g_last_l, qd=qd_l,
                ekd=ekd_l, kd=kd_l, kw=kw_l, qp=qp_l)


def _dn_head_cols(gb_blk, head):
    lane = lax.broadcasted_iota(jnp.int32, (1, LANES), 1)
    g = jnp.sum(jnp.where(lane == head, gb_blk, 0.0), axis=-1, keepdims=True)
    beta = jnp.sum(jnp.where(lane == head + DN_HEADS, gb_blk, 0.0), axis=-1, keepdims=True)
    return g, beta


def _halves_over_ici(s_refs, o_refs, send_sems, recv_sems, first, last):
    x, y, c = _mesh_pos()
    me = 2 * x + y
    chips = _other_chips(x, y)
    pairs = [(a, k) for a in range(len(s_refs)) for k in range(3)]

    def copy(a, k, slot):
        px, py = chips[k]
        return pltpu.make_async_remote_copy(
            src_ref=s_refs[a].at[c], dst_ref=o_refs[a].at[slot, c], send_sem=send_sems.at[3 * a + k],
            recv_sem=recv_sems.at[3 * a + k], device_id=(px, py, c), device_id_type=MESH)

    @pl.when(first)
    def _():
        for a, k in pairs:
            copy(a, k, me).start()

    @pl.when(last)
    def _():
        for a, k in pairs:
            px, py = chips[k]
            copy(a, k, 2 * px + py).wait_recv()
        for a, k in pairs:
            copy(a, k, me).wait_send()


def _dn_delta_fwd(qkv, gb, p, o_gain, *, name, send=()):
    T = qkv.shape[0]
    tb = min(DN_TB, T)
    nb, nc = T // tb, tb // DN_CHUNK
    H = DN_HEADS
    qscale = DN_DK ** -0.5
    ns = len(send)
    hp = DN_HEADS_FWD

    def body(*refs):
        q_ref, k_ref, v_ref, gb_ref, gate_ref, gain_ref = refs[:6]
        o_ref, og_ref, st_ref = refs[6 + ns:9 + ns]
        s_ref = refs[9 + 2 * ns]
        pair, blk = pl.program_id(0), pl.program_id(1)
        if ns:
            _halves_over_ici(refs[6:6 + ns], refs[9 + ns:9 + 2 * ns], refs[10 + 2 * ns], refs[11 + 2 * ns],
                             jnp.logical_and(pair == 0, blk == 0),
                             jnp.logical_and(pair == H // hp - 1, blk == nb - 1))

        @pl.when(blk == 0)
        def _():
            s_ref[...] = jnp.zeros_like(s_ref)

        gbv = gb_ref[...]
        ts, ku, op = [], [], []
        for e in range(hp):
            qk_e, v_e = slice(e * DN_DK, (e + 1) * DN_DK), slice(e * DN_DV, (e + 1) * DN_DV)
            g, beta = _dn_head_cols(gbv, hp * pair + e)
            t = _dn_local(q_ref[:, qk_e] * qscale, k_ref[:, qk_e], v_ref[:, v_e], g, beta, nc)
            ts.append(t)
            ku.append([_dot_tn(a, b) for a, b in zip(t["kd"], t["u"])])
            op.append([_dot(a, b) for a, b in zip(t["aqk"], t["u"])])
        s32 = [s_ref[e] for e in range(hp)]
        s_l = [[] for _ in range(hp)]
        for i in range(nc):
            sb = [_bf(x) for x in s32]
            for e in range(hp):
                st_ref[e, i] = sb[e]
                s_l[e].append(sb[e])
            prod = [_dot(ts[e]["kw"][i], sb[e]) for e in range(hp)]
            s32 = [s32[e] * jnp.exp(ts[e]["g_last"][i]) - prod[e] + ku[e][i] for e in range(hp)]
        for e in range(hp):
            s_ref[e] = s32[e]
        o = jnp.concatenate(
            [jnp.concatenate([_dot(qp, sb) + x for qp, sb, x in zip(ts[e]["qp"], s_l[e], op[e])], axis=0)
             for e in range(hp)], axis=1)
        o_ref[...] = o
        gain = gain_ref[...]
        for e in range(hp):
            v_e = slice(e * DN_DV, (e + 1) * DN_DV)
            oe = o[:, v_e]
            r = lax.rsqrt(jnp.mean(oe * oe, axis=-1, keepdims=True) + RMS_EPS)
            og_ref[:, v_e] = (((oe * r) * gain) * _silu(gate_ref[:, v_e])).astype(BF16)

    qk = lambda col0: pl.BlockSpec((tb, hp * DN_DK), lambda h, i: (i, col0 // (hp * DN_DK) + h))
    vblk = lambda col0: pl.BlockSpec((tb, hp * DN_DV), lambda h, i: (i, col0 // (hp * DN_DV) + h))
    return pl.pallas_call(
        body, name=name, grid=(H // hp, nb),
        in_specs=[qk(0), qk(DN_QK_W), vblk(2 * DN_QK_W), pl.BlockSpec((tb, LANES), lambda h, i: (i, 0)),
                  vblk(DN_CONV_W), pl.BlockSpec((1, DN_DV), lambda h, i: (0, 0))] + [HBM] * ns,
        out_specs=[vblk(0), vblk(0), pl.BlockSpec((hp, nc, DN_DK, DN_DV), lambda h, i: (h, i, 0, 0))] + [HBM] * ns,
        out_shape=[jax.ShapeDtypeStruct((T, DN_V_W), F32), jax.ShapeDtypeStruct((T, DN_V_W), BF16),
                   jax.ShapeDtypeStruct((H, T // DN_CHUNK, DN_DK, DN_DV), BF16)]
        + [jax.ShapeDtypeStruct((N_CHIPS,) + a.shape, a.dtype) for a in send],
        scratch_shapes=[pltpu.VMEM((hp, DN_DK, DN_DV), F32)]
        + ([pltpu.SemaphoreType.DMA((3 * ns,)), pltpu.SemaphoreType.DMA((3 * ns,))] if ns else []),
        compiler_params=pltpu.CompilerParams(dimension_semantics=("arbitrary", "arbitrary")),
    )(qkv, qkv, qkv, gb, p, o_gain, *send)


def _blocks_over_ici(p_refs, o_refs, send_sems, recv_sems, first, last):
    x, y, c = _mesh_pos()
    me = 2 * x + y
    chips = _other_chips(x, y)
    pairs = [(a, k) for a in range(len(p_refs)) for k in range(3)]

    def copy(a, k, slot):
        px, py = chips[k]
        return pltpu.make_async_remote_copy(
            src_ref=p_refs[a].at[2 * px + py], dst_ref=o_refs[a].at[slot], send_sem=send_sems.at[3 * a + k],
            recv_sem=recv_sems.at[3 * a + k], device_id=(px, py, c), device_id_type=MESH)

    @pl.when(first)
    def _():
        for a, k in pairs:
            copy(a, k, me).start()

    @pl.when(last)
    def _():
        for a, k in pairs:
            px, py = chips[k]
            copy(a, k, 2 * px + py).wait_recv()
        for a, k in pairs:
            copy(a, k, me).wait_send()


def _dn_delta_bwd(qkv, gb, p, o_gain, o, states, dog, *, name, send=()):
    T = qkv.shape[0]
    tb = min(DN_TB, T)
    nb, nc = T // tb, tb // DN_CHUNK
    H = DN_HEADS
    qscale = DN_DK ** -0.5
    ns = len(send)
    hp = DN_HEADS_BWD

    def body(*refs):
        q_ref, k_ref, v_ref, gb_ref, gate_ref, gain_ref, o_ref, st_ref, dog_ref = refs[:9]
        dq_ref, dk_ref, dv_ref, dgate_ref, dgb_ref, dgain_ref = refs[9 + ns:15 + ns]
        ds_ref = refs[15 + 2 * ns]
        pair, blk = pl.program_id(0), pl.program_id(1)
        first = jnp.logical_and(pair == 0, blk == 0)
        if ns:
            _blocks_over_ici(refs[9:9 + ns], refs[15 + ns:15 + 2 * ns], refs[16 + 2 * ns], refs[17 + 2 * ns],
                             first, jnp.logical_and(pair == H // hp - 1, blk == nb - 1))

        @pl.when(blk == 0)
        def _():
            ds_ref[...] = jnp.zeros_like(ds_ref)

        @pl.when(first)
        def _():
            dgain_ref[...] = jnp.zeros_like(dgain_ref)

        lane = lax.broadcasted_iota(jnp.int32, (1, LANES), 1)
        c = DN_CHUNK
        cut = lambda x: [x[i * c:(i + 1) * c] for i in range(nc)]
        cat = lambda xs: jnp.concatenate(xs, axis=0)
        rsum = lambda x: jnp.sum(x, axis=-1, keepdims=True)
        gbv, gain = gb_ref[...], gain_ref[...]

        def before_chain(e):
            qk_e, v_e = slice(e * DN_DK, (e + 1) * DN_DK), slice(e * DN_DV, (e + 1) * DN_DV)
            g, beta = _dn_head_cols(gbv, hp * pair + e)
            ov, gate, dogv = o_ref[:, v_e], gate_ref[:, v_e], dog_ref[:, v_e]
            r = lax.rsqrt(jnp.mean(ov * ov, axis=-1, keepdims=True) + RMS_EPS)
            oh = ov * r
            dnrm = dogv * _silu(gate)
            dgate_ref[:, v_e] = dogv * (oh * gain) * _dsilu(gate)
            doh = dnrm * gain
            do_l = cut(r * (doh - oh * jnp.mean(doh * oh, axis=-1, keepdims=True)))
            dgain_ref[...] += jnp.sum(dnrm * oh, axis=0, keepdims=True)
            k, v = k_ref[:, qk_e], v_ref[:, v_e]
            t = _dn_local(q_ref[:, qk_e] * qscale, k, v, g, beta, nc)
            s_l = [st_ref[e, i] for i in range(nc)]
            vn_l = [u - _dot(w, sb) for u, w, sb in zip(t["u"], t["w"], s_l)]
            return dict(
                t=t, beta=beta, v=v, s=s_l, vn=vn_l, egl=[jnp.exp(x) for x in t["g_last"]],
                dqd=[_dot_nt(a, sb) for a, sb in zip(do_l, s_l)], daqk=[_dot_nt(a, b) for a, b in zip(do_l, vn_l)],
                aqk_do=[_dot_tn(a, b) for a, b in zip(t["aqk"], do_l)],
                qp_do=[_dot_tn(a, b) for a, b in zip(t["qp"], do_l)])

        hs = [before_chain(e) for e in range(hp)]
        ds = [ds_ref[e] for e in range(hp)]
        ds_l = [[None] * nc for _ in range(hp)]
        for i in reversed(range(nc)):
            for e in range(hp):
                ds_l[e][i] = ds[e]
            prod = [_dot_tn(hs[e]["t"]["kw"][i], ds[e]) for e in range(hp)]
            ds = [ds[e] * hs[e]["egl"][i] - prod[e] + hs[e]["qp_do"][i] for e in range(hp)]
        for e in range(hp):
            ds_ref[e] = ds[e]

        def after_chain(e):
            hd, t = hs[e], hs[e]["t"]
            lower, strict, eye = t["lower"], t["strict"], t["eye"]
            s_l, vn_l, dqd_l, daqk_l, egl_l, beta, v = (hd["s"], hd["vn"], hd["dqd"], hd["daqk"], hd["egl"],
                                                         hd["beta"], hd["v"])
            dvn_l = [a + _dot(kd, d) for a, kd, d in zip(hd["aqk_do"], t["kd"], ds_l[e])]
            dkd_l = [_dot_nt(a, d) for a, d in zip(vn_l, ds_l[e])]
            dgl_l = [jnp.sum(rsum(d * sb.astype(F32)), axis=0, keepdims=True) * x
                     for d, sb, x in zip(ds_l[e], s_l, egl_l)]
            dw_l = [-_dot_nt(a, sb) for a, sb in zip(dvn_l, s_l)]
            dbv_l = [_dot_tn(a, b) for a, b in zip(t["inv"], dvn_l)]
            dbk_l = [_dot_tn(a, b) for a, b in zip(t["inv"], dw_l)]
            dlow_l = [-(_dot_nt(a, b) + _dot_nt(x, y)) for a, b, x, y in zip(dbv_l, t["u"], dbk_l, t["w"])]
            m_l = [jnp.where(strict, a * d, 0.0) for a, d in zip(dlow_l, t["dec"])]
            nmat_l = [jnp.where(lower, a * d, 0.0) for a, d in zip(daqk_l, t["dec"])]
            dkb_l = [_dot(m, kk) + b * x for m, kk, b, x in zip(m_l, t["k"], dbk_l, t["eg"])]
            dqs_l = [_dot(n, kk) + a * x for n, kk, a, x in zip(nmat_l, t["k"], dqd_l, t["eg"])]
            dk1_l = [_dot_tn(m, kb) for m, kb in zip(m_l, t["kb"])]
            dk2_l = [_dot_tn(n, q) for n, q in zip(nmat_l, t["qs"])]
            beta_l, v_l = cut(beta), cut(v)
            rowid = lax.broadcasted_iota(jnp.int32, (c, 1), 0)
            dk_l, dg_l, dbeta_l = [], [], []
            for i in range(nc):
                dk_l.append(dk1_l[i] + dk2_l[i] + dkd_l[i] * t["ekd"][i] + dkb_l[i] * beta_l[i])
                gmat = jnp.where(strict, dlow_l[i] * t["low"][i], 0.0) + daqk_l[i] * t["aqk"][i]
                s_kd = rsum(dkd_l[i] * t["kd"][i])
                dg = (rsum(gmat) + rsum(dqd_l[i] * t["qd"][i]) - s_kd + rsum(dbk_l[i] * t["rhs_k"][i]))
                dg_row = -jnp.sum(gmat, axis=0, keepdims=True)
                dg = dg + rsum(jnp.where(eye, dg_row, 0.0))
                dgl = dgl_l[i] + jnp.sum(s_kd, axis=0, keepdims=True)
                dg_l.append(dg + jnp.where(rowid == c - 1, dgl, 0.0))
                dbeta_l.append(rsum(dbv_l[i] * v_l[i]) + rsum(dkb_l[i] * t["k"][i]))
            head = hp * pair + e
            dgb = (jnp.where(lane == head, cat(dg_l), 0.0) + jnp.where(lane == head + DN_HEADS, cat(dbeta_l), 0.0))
            return cat(dqs_l) * qscale, cat(dk_l), cat(dbv_l) * beta, dgb

        for e in range(hp):
            dq, dk, dv, dgb = after_chain(e)
            dq_ref[:, e * DN_DK:(e + 1) * DN_DK] = dq
            dk_ref[:, e * DN_DK:(e + 1) * DN_DK] = dk
            dv_ref[:, e * DN_DV:(e + 1) * DN_DV] = dv
            dgb_ref[e] = dgb

    rev = lambda i: nb - 1 - i
    qk = lambda col0: pl.BlockSpec((tb, hp * DN_DK), lambda h, i: (rev(i), col0 // (hp * DN_DK) + h))
    vblk = lambda col0, **kw: pl.BlockSpec((tb, hp * DN_DV), lambda h, i: (rev(i), col0 // (hp * DN_DV) + h), **kw)
    gain_spec = pl.BlockSpec((1, DN_DV), lambda h, i: (0, 0))
    once = dict(pipeline_mode=pl.Buffered(1))
    return pl.pallas_call(
        body, name=name, grid=(H // hp, nb),
        in_specs=[qk(0), qk(DN_QK_W), vblk(2 * DN_QK_W, **once),
                  pl.BlockSpec((tb, LANES), lambda h, i: (rev(i), 0)), vblk(DN_CONV_W, **once), gain_spec,
                  vblk(0, **once), pl.BlockSpec((hp, nc, DN_DK, DN_DV), lambda h, i: (h, rev(i), 0, 0)),
                  vblk(0, **once)] + [HBM] * ns,
        out_specs=[qk(0), qk(0), vblk(0), vblk(DN_CONV_W),
                   pl.BlockSpec((hp, tb, LANES), lambda h, i: (h, rev(i), 0)), gain_spec] + [HBM] * ns,
        out_shape=[jax.ShapeDtypeStruct((T, DN_QK_W), F32), jax.ShapeDtypeStruct((T, DN_QK_W), F32),
                   jax.ShapeDtypeStruct((T, DN_V_W), F32), jax.ShapeDtypeStruct((T, DN_IN_PAD), F32),
                   jax.ShapeDtypeStruct((H, T, LANES), F32), jax.ShapeDtypeStruct((1, DN_DV), F32)]
        + [jax.ShapeDtypeStruct(a.shape, a.dtype) for a in send],
        scratch_shapes=[pltpu.VMEM((hp, DN_DK, DN_DV), F32)]
        + ([pltpu.SemaphoreType.DMA((3 * ns,)), pltpu.SemaphoreType.DMA((3 * ns,))] if ns else []),
        compiler_params=pltpu.CompilerParams(dimension_semantics=("arbitrary", "arbitrary")),
    )(qkv, qkv, qkv, gb, p, o_gain, o, states, dog, *send)


def _dn_conv_bwd(p, conv_w, d, dp, *, first, normed, name):
    T, width = d.shape

    def body(p_ref, w_ref, d_ref, dp_in, dp_ref, dw_ref):
        del dp_in
        x = p_ref[...]
        ksz = w_ref.shape[0]
        xs = [_shift_down(x, ksz - 1 - i) for i in range(ksz)]
        xc = sum(w_ref[i:i + 1, :] * xs[i] for i in range(ksz))
        ds = d_ref[...]
        if normed:
            s = _silu(xc)
            r = lax.rsqrt(jnp.sum(s * s, axis=-1, keepdims=True) + L2_EPS)
            y = s * r
            ds = r * (ds - y * jnp.sum(ds * y, axis=-1, keepdims=True))
        dxc = ds * _dsilu(xc)
        dp_ref[...] = sum(w_ref[i:i + 1, :] * _shift_up(dxc, ksz - 1 - i) for i in range(ksz))
        for i in range(ksz):
            dw_ref[i:i + 1, :] = jnp.sum(dxc * xs[i], axis=0, keepdims=True)

    shifted = pl.BlockSpec((T, LANES), lambda j: (0, first + j))
    return pl.pallas_call(
        body, name=name, grid=(width // LANES,),
        in_specs=[shifted, pl.BlockSpec((DN_CONV, LANES), lambda j: (0, first + j)),
                  pl.BlockSpec((T, LANES), lambda j: (0, j)), pl.BlockSpec(memory_space=pl.ANY)],
        out_specs=[shifted, pl.BlockSpec((DN_CONV, LANES), lambda j: (0, j))],
        out_shape=[jax.ShapeDtypeStruct(dp.shape, F32), jax.ShapeDtypeStruct((DN_CONV, width), F32)],
        input_output_aliases={3: 0},
        compiler_params=pltpu.CompilerParams(dimension_semantics=("parallel",), vmem_limit_bytes=VMEM_BIG),
    )(p, conv_w, d, dp)


def _dn_ab_bwd(p, alog_row, dtb_row, dgb, dp, *, name):
    T = p.shape[0]
    rows = min(DN_TB, T)
    H = DN_HEADS

    def body(p_ref, al_ref, dt_ref, dgb_ref, dp_in, dp_ref, dal_ref, ddt_ref):
        del dp_in

        @pl.when(pl.program_id(0) == 0)
        def _():
            dal_ref[...] = jnp.zeros_like(dal_ref)
            ddt_ref[...] = jnp.zeros_like(ddt_ref)

        blk = p_ref[...]
        is_a, is_b, a_arg, neg_exp, log_a, beta = _dn_ab_parts(blk, al_ref[...], dt_ref[...])
        d = dgb_ref[0]
        for hh in range(1, H):
            d = d + dgb_ref[hh]
        hi, mid, lo_ = _split3(jnp.where(is_a, d, 0.0))
        tri = _dn_chunk_tri(rows, upper=True)
        f = lambda q: jnp.dot(tri, q, preferred_element_type=F32)
        dlog_a = f(hi) + f(mid) + f(lo_)
        da_in = dlog_a * neg_exp * _sigmoid(a_arg)
        db_in = jnp.where(is_b, d, 0.0) * beta * (1.0 - beta)
        dp_ref[...] = jnp.where(is_a, da_in, 0.0) + db_in
        dal_ref[...] += jnp.sum(dlog_a * log_a, axis=0, keepdims=True)
        ddt_ref[...] += jnp.sum(jnp.where(is_a, da_in, 0.0), axis=0, keepdims=True)

    blk = pl.BlockSpec((rows, LANES), lambda i: (i, DN_AB_COL))
    vec = pl.BlockSpec((1, LANES), lambda i: (0, 0))
    return pl.pallas_call(
        body, name=name, grid=(T // rows,),
        in_specs=[blk, vec, vec, pl.BlockSpec((H, rows, LANES), lambda i: (0, i, 0)),
                  pl.BlockSpec(memory_space=pl.ANY)],
        out_specs=[blk, vec, vec],
        out_shape=[jax.ShapeDtypeStruct(dp.shape, F32), jax.ShapeDtypeStruct((1, LANES), F32),
                   jax.ShapeDtypeStruct((1, LANES), F32)],
        input_output_aliases={4: 0},
        compiler_params=pltpu.CompilerParams(dimension_semantics=("arbitrary",)),
    )(p, alog_row, dtb_row, dgb, dp)


def _dn_layer_fwd(x, ng, w_in, conv_w, a_log, dt_bias, o_gain, w_out, tag, send=()):
    alog_row, dtb_row = _dn_lane_rows(a_log, dt_bias)
    gain = o_gain.reshape(1, DN_DV)
    h = _rmsnorm_fwd(x, ng, name=f"{tag}_norm")
    p = _matmul(h, w_in, mode="nn", name=f"{tag}_inproj")
    qkv = _dn_prep_fwd(p, conv_w, name=f"{tag}_prep")
    gb = _dn_ab_fwd(p, alog_row, dtb_row, name=f"{tag}_ab")
    o, og, states, *landed = _dn_delta_fwd(qkv, gb, p, gain, name=f"{tag}_delta", send=send)
    x_new = _matmul(og, w_out, mode="nn", res=x, name=f"{tag}_outproj")
    return x_new, (h, p, qkv, gb, o, og, states), landed


def _dn_layer_bwd(dx, x, ng, w_in, conv_w, a_log, dt_bias, o_gain, w_out, saved, tag, send=(), chip_sums=None):
    h, p, qkv, gb, o, og, states = saved
    alog_row, dtb_row = _dn_lane_rows(a_log, dt_bias)
    gain = o_gain.reshape(1, DN_DV)
    d_wout = _matmul(og, dx, mode="tn", out_dtype=BF16, name=f"{tag}_dwout")
    if chip_sums is not None:
        d_wout, = chip_sums([_cut2(_by_rows(d_wout))], f"{tag}wout")
        send = list(send) + [d_wout]
    dog = _matmul(dx, w_out, mode="nt", name=f"{tag}_dog")
    dq, dk, dv, dp, dgb, dgain, *landed = _dn_delta_bwd(qkv, gb, p, gain, o, states, dog, name=f"{tag}_deltabwd",
                                                        send=send)
    n_qk = DN_QK_W // LANES
    dp, dconv_q = _dn_conv_bwd(p, conv_w, dq, dp, first=0, normed=True, name=f"{tag}_convbwd_q")
    dp, dconv_k = _dn_conv_bwd(p, conv_w, dk, dp, first=n_qk, normed=True, name=f"{tag}_convbwd_k")
    dp, dconv_v = _dn_conv_bwd(p, conv_w, dv, dp, first=2 * n_qk, normed=False, name=f"{tag}_convbwd_v")
    dconv = jnp.concatenate([dconv_q, dconv_k, dconv_v], axis=1)
    dp, dal, ddt = _dn_ab_bwd(p, alog_row, dtb_row, dgb, dp, name=f"{tag}_abbwd")
    d_win = _matmul(h, dp, mode="tn", name=f"{tag}_dwin")
    if chip_sums is not None:
        d_win, = chip_sums([_cut2(_by_cols(d_win))], f"{tag}win")
        dh, landed_win = _matmul(dp, w_in, mode="nt", name=f"{tag}_dh", send=[d_win])
        landed = landed + [landed_win]
    else:
        dh = _matmul(dp, w_in, mode="nt", name=f"{tag}_dh")
    dx_prev, dng = _rmsnorm_bwd(x, ng, dh, dx, name=f"{tag}_normbwd")
    return dx_prev, dng, d_win, dconv, dal[0, :DN_HEADS], ddt[0, :DN_HEADS], dgain[0], d_wout, landed


def _by_cols(dw):
    return _split(dw[:, :DN_IN].astype(BF16), 1)


def _by_rows(dw):
    return dw.reshape(N_CHIPS, -1, dw.shape[-1])


def _cut2(g4):
    return g4.reshape(N_CHIPS, 2, -1, g4.shape[-1])


def _sb_gains(g):
    return jnp.concatenate([g, g]).reshape(1, LANES)


def _sb_layer_fwd(x, ng, w_in, gq, gk, w_out, tag, send=()):
    h = _rmsnorm_fwd(x, ng, name=f"{tag}_norm")
    p3 = _matmul(h, w_in, mode="nn", b_parts=4, out_parts=4, name=f"{tag}_inproj")
    og, o, *landed = _sb_attn_fwd(p3, _sb_gains(gq), _sb_gains(gk), name=f"{tag}_attn", send=send)
    x_new = _matmul(og, w_out, mode="nn", res=x, name=f"{tag}_outproj")
    return x_new, (h, p3, og, o), landed


def _sb_layer_bwd(dx, x, ng, w_in, gq, gk, w_out, saved, tag):
    h, p3, og, o = saved
    d_wout = _matmul(og, dx, mode="tn", out_dtype=BF16, name=f"{tag}_dwout")
    dog = _matmul(dx, w_out, mode="nt", name=f"{tag}_dog")
    dp3, dgq, dgk = _sb_attn_bwd(p3, _sb_gains(gq), _sb_gains(gk), o, dog, name=f"{tag}_attnbwd")
    fold = lambda d: jnp.sum(d.reshape(-1, SB_DH), axis=0)
    d_win = _matmul(h, dp3, mode="tn", b_parts=4, out_parts=4, out_dtype=BF16, name=f"{tag}_dwin")
    dh = _matmul(dp3, w_in, mode="nt", a_parts=4, b_parts=4, name=f"{tag}_dh")
    dx_prev, dng = _rmsnorm_bwd(x, ng, dh, dx, name=f"{tag}_normbwd")
    return dx_prev, dng, d_win, fold(dgq), fold(dgk), d_wout


N_CHIPS = 4
HBM = pl.BlockSpec(memory_space=pl.ANY)


def _mesh_pos():
    return lax.axis_index("x"), lax.axis_index("y"), lax.axis_index("c")


def _other_chips(x, y):
    return [(1 - x, y), (x, 1 - y), (1 - x, 1 - y)]


def _chip_exchange(srcs, *, send_slot_is_dest, copy_own, name):
    n = len(srcs)

    def body(*refs):
        src_refs, out_refs = refs[:n], refs[n:2 * n]
        send_sems, recv_sems, local_sems = refs[2 * n:]
        x, y, c = _mesh_pos()
        me = 2 * x + y
        chips = _other_chips(x, y)
        local = []
        for a in range(n):
            if not copy_own[a]:
                continue
            own = src_refs[a].at[me] if send_slot_is_dest else src_refs[a]
            local.append(pltpu.make_async_copy(own, out_refs[a].at[me], local_sems.at[a]))
        for cp in local:
            cp.start()

        def copy(a, k, landing_slot):
            px, py = chips[k]
            src = src_refs[a].at[2 * px + py] if send_slot_is_dest else src_refs[a]
            return pltpu.make_async_remote_copy(
                src_ref=src, dst_ref=out_refs[a].at[landing_slot],
                send_sem=send_sems.at[a * 3 + k], recv_sem=recv_sems.at[a * 3 + k],
                device_id=(px, py, c), device_id_type=MESH)

        sends = [copy(a, k, me) for a in range(n) for k in range(3)]
        for cp in sends:
            cp.start()
        for a in range(n):
            for k in range(3):
                px, py = chips[k]
                copy(a, k, 2 * px + py).wait_recv()
        for cp in sends:
            cp.wait_send()
        for cp in local:
            cp.wait()

    outs = []
    for s in srcs:
        shape = s.shape if send_slot_is_dest else (N_CHIPS,) + s.shape
        outs.append(jax.ShapeDtypeStruct(shape, s.dtype))
    return pl.pallas_call(
        body, name=name, in_specs=[HBM] * n, out_specs=[HBM] * n, out_shape=outs,
        scratch_shapes=[pltpu.SemaphoreType.DMA((3 * n,)), pltpu.SemaphoreType.DMA((3 * n,)),
                        pltpu.SemaphoreType.DMA((n,))],
    )(*srcs)


def _sibling_exchange(srcs, *, name):
    n = len(srcs)

    def body(*refs):
        src_refs, out_refs = refs[:n], refs[n:2 * n]
        send_sems, recv_sems = refs[2 * n:]
        x, y, c = _mesh_pos()
        copies = [pltpu.make_async_remote_copy(
            src_ref=src_refs[a], dst_ref=out_refs[a], send_sem=send_sems.at[a], recv_sem=recv_sems.at[a],
            device_id=(x, y, 1 - c), device_id_type=MESH) for a in range(n)]
        for cp in copies:
            cp.start()
        for cp in copies:
            cp.wait()

    return pl.pallas_call(
        body, name=name, in_specs=[HBM] * n, out_specs=[HBM] * n,
        out_shape=[jax.ShapeDtypeStruct(s.shape, s.dtype) for s in srcs],
        scratch_shapes=[pltpu.SemaphoreType.DMA((n,)), pltpu.SemaphoreType.DMA((n,))],
    )(*srcs)


def _gather_halves(shards, small, *, name):
    n = len(shards)

    def body(*refs):
        s_refs, small_ref = refs[:n], refs[n]
        o_refs, osmall_ref = refs[n + 1:2 * n + 1], refs[2 * n + 1]
        send_sems, recv_sems, local_sems = refs[2 * n + 2:]
        x, y, c = _mesh_pos()
        me = 2 * x + y
        chips = _other_chips(x, y)
        local = [pltpu.make_async_copy(small_ref, osmall_ref.at[me], local_sems.at[0])]
        for cp in local:
            cp.start()

        def over_ici(a, k, slot):
            px, py = chips[k]
            return pltpu.make_async_remote_copy(
                src_ref=s_refs[a].at[c], dst_ref=o_refs[a].at[slot, c], send_sem=send_sems.at[3 * a + k],
                recv_sem=recv_sems.at[3 * a + k], device_id=(px, py, c), device_id_type=MESH)

        def small_copy(k, slot):
            px, py = chips[k]
            return pltpu.make_async_remote_copy(
                src_ref=small_ref, dst_ref=osmall_ref.at[slot], send_sem=send_sems.at[3 * n + k],
                recv_sem=recv_sems.at[3 * n + k], device_id=(px, py, c), device_id_type=MESH)

        def to_sibling(a, k, half):
            px, py = chips[k]
            blk = o_refs[a].at[2 * px + py, half]
            return pltpu.make_async_remote_copy(
                src_ref=blk, dst_ref=blk, send_sem=send_sems.at[3 * n + 3 + 3 * a + k],
                recv_sem=recv_sems.at[3 * n + 3 + 3 * a + k], device_id=(x, y, 1 - c), device_id_type=MESH)

        sends = [over_ici(a, k, me) for a in range(n) for k in range(3)] + [small_copy(k, me) for k in range(3)]
        for cp in sends:
            cp.start()
        passed = []
        for a in range(n):
            for k in range(3):
                px, py = chips[k]
                over_ici(a, k, 2 * px + py).wait_recv()
                passed.append(to_sibling(a, k, c))
                passed[-1].start()
        for k in range(3):
            px, py = chips[k]
            small_copy(k, 2 * px + py).wait_recv()
        for a in range(n):
            for k in range(3):
                to_sibling(a, k, 1 - c).wait_recv()
        for cp in sends + passed:
            cp.wait_send()
        for cp in local:
            cp.wait()

    nsem = 6 * n + 3
    return pl.pallas_call(
        body, name=name, in_specs=[HBM] * (n + 1), out_specs=[HBM] * (n + 1),
        out_shape=[jax.ShapeDtypeStruct((N_CHIPS,) + s.shape, s.dtype) for s in shards + [small]],
        scratch_shapes=[pltpu.SemaphoreType.DMA((nsem,)), pltpu.SemaphoreType.DMA((nsem,)),
                        pltpu.SemaphoreType.DMA((1,))],
    )(*shards, small)


def _forward_halves(landed, *, name):
    n = len(landed)

    def body(*refs):
        o_refs = refs[n:2 * n]
        send_sems, recv_sems = refs[2 * n:]
        x, y, c = _mesh_pos()
        chips = _other_chips(x, y)
        pairs = [(a, k) for a in range(n) for k in range(3)]

        def copy(a, k, half):
            px, py = chips[k]
            blk = o_refs[a].at[2 * px + py, half]
            return pltpu.make_async_remote_copy(
                src_ref=blk, dst_ref=blk, send_sem=send_sems.at[3 * a + k], recv_sem=recv_sems.at[3 * a + k],
                device_id=(x, y, 1 - c), device_id_type=MESH)

        sends = [copy(a, k, c) for a, k in pairs]
        for cp in sends:
            cp.start()
        for a, k in pairs:
            copy(a, k, 1 - c).wait_recv()
        for cp in sends:
            cp.wait_send()

    return pl.pallas_call(
        body, name=name, in_specs=[HBM] * n, out_specs=[HBM] * n,
        out_shape=[jax.ShapeDtypeStruct(a.shape, a.dtype) for a in landed],
        input_output_aliases={a: a for a in range(n)},
        scratch_shapes=[pltpu.SemaphoreType.DMA((3 * n,)), pltpu.SemaphoreType.DMA((3 * n,))],
    )(*landed)


def _swap_other_half(g_list, *, name):
    n = len(g_list)

    def body(*refs):
        g_refs, o_refs = refs[:n], refs[n:2 * n]
        send_sems, recv_sems = refs[2 * n:]
        x, y, c = _mesh_pos()
        copies = [pltpu.make_async_remote_copy(
            src_ref=g_refs[a].at[:, 1 - c], dst_ref=o_refs[a], send_sem=send_sems.at[a], recv_sem=recv_sems.at[a],
            device_id=(x, y, 1 - c), device_id_type=MESH) for a in range(n)]
        for cp in copies:
            cp.start()
        for cp in copies:
            cp.wait()

    return pl.pallas_call(
        body, name=name, in_specs=[HBM] * n, out_specs=[HBM] * n,
        out_shape=[jax.ShapeDtypeStruct((g.shape[0],) + g.shape[2:], g.dtype) for g in g_list],
        scratch_shapes=[pltpu.SemaphoreType.DMA((n,)), pltpu.SemaphoreType.DMA((n,))],
    )(*g_list)


def _row_tile(r):
    return _pick(r, (512, 256, 128, 64, 32, 16, 8))


def _add_my_half(g4, sib4, core, *, name):
    n, _, r, C = g4.shape
    tr = _row_tile(r)

    def body(core_ref, g_ref, s_ref, o_ref):
        del core_ref
        o_ref[...] = (g_ref[...].astype(F32) + s_ref[...].astype(F32)).astype(o_ref.dtype)

    return pl.pallas_call(
        body, name=name,
        grid_spec=pltpu.PrefetchScalarGridSpec(
            num_scalar_prefetch=1, grid=(n, r // tr),
            in_specs=[pl.BlockSpec((None, None, tr, C), lambda j, i, core_ref: (j, core_ref[0], i, 0)),
                      pl.BlockSpec((None, tr, C), lambda j, i, core_ref: (j, i, 0))],
            out_specs=pl.BlockSpec((None, tr, C), lambda j, i, core_ref: (j, i, 0))),
        out_shape=jax.ShapeDtypeStruct((n, r, C), g4.dtype),
        compiler_params=pltpu.CompilerParams(dimension_semantics=("parallel", "parallel")),
    )(core, g4, sib4)


def _sum_chips(landed, part, me, *, name):
    _, r, C = landed.shape
    tr = _row_tile(r)

    def body(me_ref, own_ref, r1_ref, r2_ref, r3_ref, o_ref):
        del me_ref
        f = lambda ref: ref[...].astype(F32)
        o_ref[...] = ((f(own_ref) + f(r1_ref)) + f(r2_ref)) + f(r3_ref)

    slot = lambda d: pl.BlockSpec((None, tr, C), lambda i, me_ref: ((me_ref[0] + d) % N_CHIPS, i, 0))
    return pl.pallas_call(
        body, name=name,
        grid_spec=pltpu.PrefetchScalarGridSpec(
            num_scalar_prefetch=1, grid=(r // tr,), in_specs=[slot(0), slot(1), slot(2), slot(3)],
            out_specs=pl.BlockSpec((tr, C), lambda i, me_ref: (i, 0))),
        out_shape=jax.ShapeDtypeStruct((r, C), F32),
        compiler_params=pltpu.CompilerParams(dimension_semantics=("parallel",)),
    )(me, part, landed, landed, landed)


def _adamw_halves(w, mine, theirs, m, v, core, *, layer, prev, name):
    shape = w.shape
    r, C = mine.shape
    tr = _pick(r, (128, 64, 32, 16, 8))
    per = r // tr
    view = lambda a: a.reshape(-1, C)
    n_prev = 0 if prev is None else 4

    def body(*refs):
        core_ref, w_ref, gm_ref, gt_ref, m_ref, v_ref = refs[:6]
        g_ref, d_ref, nm_ref, nv_ref = refs[6 + n_prev:]
        gv = jnp.where(pl.program_id(0) == core_ref[0], gm_ref[...], gt_ref[...])
        g_ref[...] = gv
        d_ref[...], nm_ref[...], nv_ref[...] = _adamw_math(w_ref[...], gv, m_ref[...], v_ref[...])

    half = pl.BlockSpec((tr, C), lambda h, i, core_ref: ((2 * layer + h) * per + i, 0))
    row = pl.BlockSpec((tr, C), lambda h, i, core_ref: (i, 0))
    out = jax.ShapeDtypeStruct((math.prod(shape) // C, C), F32)
    res = pl.pallas_call(
        body, name=name,
        grid_spec=pltpu.PrefetchScalarGridSpec(
            num_scalar_prefetch=1, grid=(2, per), in_specs=[half, row, row, half, half] + [HBM] * n_prev,
            out_specs=[half] * 4),
        out_shape=[out] * 4,
        input_output_aliases={6 + j: j for j in range(n_prev)},
        compiler_params=pltpu.CompilerParams(dimension_semantics=("parallel", "parallel")),
    )(core, view(w), mine, theirs, view(m), view(v), *([] if prev is None else [view(a) for a in prev]))
    return tuple(a.reshape(shape) for a in res)


def _sum_small(recv4, *, name):
    _, R, C = recv4.shape

    def body(r_ref, o_ref):
        o_ref[...] = ((r_ref[0] + r_ref[1]) + r_ref[2]) + r_ref[3]

    return pl.pallas_call(body, name=name, out_shape=jax.ShapeDtypeStruct((R, C), F32))(recv4)


def _add(a, b, *, name):
    R, C = a.shape
    tr = _pick(R, (512, 256, 128, 64, 32, 16, 8))
    blk = pl.BlockSpec((tr, C), lambda i: (i, 0))

    def body(a_ref, b_ref, o_ref):
        o_ref[...] = a_ref[...] + b_ref[...]

    return pl.pallas_call(body, name=name, grid=(R // tr,), in_specs=[blk, blk], out_specs=blk,
                          out_shape=jax.ShapeDtypeStruct((R, C), F32),
                          compiler_params=pltpu.CompilerParams(dimension_semantics=("parallel",)))(a, b)


def _adamw_math(w, g, m, v):
    nm = ADAM_B1 * m + (1.0 - ADAM_B1) * g
    nv = ADAM_B2 * v + (1.0 - ADAM_B2) * (g * g)
    m_hat = nm / (1.0 - ADAM_B1 ** ADAM_STEP)
    v_hat = nv / (1.0 - ADAM_B2 ** ADAM_STEP)
    return -ADAM_LR * (m_hat / (jnp.sqrt(v_hat) + ADAM_EPS) + ADAM_WD * w), nm, nv


def _adamw(w, g, m, v, *, name):
    shape = w.shape
    C = shape[-1]
    R = w.size // C
    two = lambda a: a.reshape(R, C)
    tr = _pick(R, (256, 128, 64, 32, 16, 8)) if R % 8 == 0 and R > 8 else R
    blk = pl.BlockSpec((tr, C), lambda i: (i, 0))

    def body(w_ref, g_ref, m_ref, v_ref, d_ref, nm_ref, nv_ref):
        d_ref[...], nm_ref[...], nv_ref[...] = _adamw_math(w_ref[...], g_ref[...], m_ref[...], v_ref[...])

    out = jax.ShapeDtypeStruct((R, C), F32)
    d, nm, nv = pl.pallas_call(
        body, name=name, grid=(R // tr,), in_specs=[blk] * 4, out_specs=[blk] * 3, out_shape=[out] * 3,
        compiler_params=pltpu.CompilerParams(dimension_semantics=("parallel",)),
    )(two(w), two(g), two(m), two(v))
    return d.reshape(shape), nm.reshape(shape), nv.reshape(shape)


BIG = (("dn_w_in", (2, 1024, 1540), 2), ("dn_w_out", (2, 512, 1024), 1), ("sb_w_in", (1, 1024, 1024), 2),
       ("sb_w_out", (1, 256, 1024), 1), ("sc_w_in", (1, 1024, 2048), 2), ("sc_w_out", (1, 512, 1024), 1))
SMALL = (("dn_conv_w", (2, 4, 1024), 2), ("dn_o_norm_g", (2, 64), 1), ("sc_conv_w", (1, 3, 512), 2))
REPL = (("norm_g", (4, 1024)), ("dn_a_log", (2, 8)), ("dn_dt_bias", (2, 8)), ("sb_q_norm_g", (1, 64)),
        ("sb_k_norm_g", (1, 64)))


def _halves(shard):
    return shard.reshape(2, -1, shard.shape[-1])


def _pack(arrays, cols, lead=()):
    flat = jnp.concatenate([a.reshape(lead + (-1,)) for a in arrays], axis=-1)
    n = flat.shape[-1]
    rows = -(-n // cols)
    unit = 512 if rows > 512 else 8
    rows = -(-rows // unit) * unit
    flat = jnp.pad(flat, [(0, 0)] * len(lead) + [(0, rows * cols - n)])
    return flat.reshape(lead + (rows, cols))


def _unpack(buf, table, lead=()):
    flat = buf.reshape(lead + (-1,))
    out, off = {}, 0
    for entry in table:
        name, shape = entry[0], entry[1]
        n = math.prod(shape)
        out[name] = flat[..., off:off + n].reshape(lead + shape)
        off += n
    return out


def _join(shards, axis):
    return jnp.concatenate([shards[j] for j in range(N_CHIPS)], axis=axis)


def _split(full, axis):
    return jnp.stack(jnp.split(full, N_CHIPS, axis=axis), axis=0)


def kernel(x, norm_g, dn_w_in, dn_conv_w, dn_a_log, dn_dt_bias, dn_o_norm_g, dn_w_out, sb_w_in, sb_q_norm_g, sb_k_norm_g, sb_w_out, sc_w_in, sc_conv_w, sc_w_out, loss_target, m_norm_g, m_dn_w_in, m_dn_conv_w, m_dn_a_log, m_dn_dt_bias, m_dn_o_norm_g, m_dn_w_out, m_sb_w_in, m_sb_q_norm_g, m_sb_k_norm_g, m_sb_w_out, m_sc_w_in, m_sc_conv_w, m_sc_w_out, v_norm_g, v_dn_w_in, v_dn_conv_w, v_dn_a_log, v_dn_dt_bias, v_dn_o_norm_g, v_dn_w_out, v_sb_w_in, v_sb_q_norm_g, v_sb_k_norm_g, v_sb_w_out, v_sc_w_in, v_sc_conv_w, v_sc_w_out):
    weights = dict(norm_g=norm_g, dn_w_in=dn_w_in, dn_conv_w=dn_conv_w, dn_a_log=dn_a_log, dn_dt_bias=dn_dt_bias,
                   dn_o_norm_g=dn_o_norm_g, dn_w_out=dn_w_out, sb_w_in=sb_w_in, sb_q_norm_g=sb_q_norm_g,
                   sb_k_norm_g=sb_k_norm_g, sb_w_out=sb_w_out, sc_w_in=sc_w_in, sc_conv_w=sc_conv_w, sc_w_out=sc_w_out)
    m_in = dict(norm_g=m_norm_g, dn_w_in=m_dn_w_in, dn_conv_w=m_dn_conv_w, dn_a_log=m_dn_a_log,
                dn_dt_bias=m_dn_dt_bias, dn_o_norm_g=m_dn_o_norm_g, dn_w_out=m_dn_w_out, sb_w_in=m_sb_w_in,
                sb_q_norm_g=m_sb_q_norm_g, sb_k_norm_g=m_sb_k_norm_g, sb_w_out=m_sb_w_out, sc_w_in=m_sc_w_in,
                sc_conv_w=m_sc_conv_w, sc_w_out=m_sc_w_out)
    v_in = dict(norm_g=v_norm_g, dn_w_in=v_dn_w_in, dn_conv_w=v_dn_conv_w, dn_a_log=v_dn_a_log,
                dn_dt_bias=v_dn_dt_bias, dn_o_norm_g=v_dn_o_norm_g, dn_w_out=v_dn_w_out, sb_w_in=v_sb_w_in,
                sb_q_norm_g=v_sb_q_norm_g, sb_k_norm_g=v_sb_k_norm_g, sb_w_out=v_sb_w_out, sc_w_in=v_sc_w_in,
                sc_conv_w=v_sc_conv_w, sc_w_out=v_sc_w_out)
    order = list(weights)
    xi, yi, ci = _mesh_pos()

    small = _pack([weights[n] for n, _, _ in SMALL], LANES)
    later = [("dn_w_in", 1), ("dn_w_out", 1), ("sb_w_in", 0), ("sb_w_out", 0), ("sc_w_in", 0), ("sc_w_out", 0)]
    piece = lambda n, l: _halves(weights[n][l].astype(BF16)[None])
    own_first = [piece("dn_w_in", 0), piece("dn_w_out", 0)]
    own_later = [piece(n, l) for n, l in later]
    own_last, own_mid = own_later[:2], own_later[2:]
    me = 2 * xi + yi
    whole = lambda g4, own: lax.dynamic_update_index_in_dim(g4, own, me, 0)
    flat = lambda g4: g4.reshape(N_CHIPS, -1, g4.shape[-1])
    rows_of = lambda w4: w4.reshape(-1, w4.shape[-1])
    dn_in = lambda w4: jnp.pad(_join(w4, 1), ((0, 0), (0, DN_IN_PAD - DN_IN)))
    w_in0, w_out0, small4 = _gather_halves(own_first, small, name="gather_first")
    full = {n: _join(a, ax) for (n, _, ax), a in zip(SMALL, _unpack(small4, SMALL, (N_CHIPS,)).values())}

    def dn_args(j, w_in4, w_out4):
        return (dn_in(flat(w_in4)), full["dn_conv_w"][j], dn_a_log[j], dn_dt_bias[j], full["dn_o_norm_g"][j],
                rows_of(w_out4))

    x0 = x[0]
    dn0 = dn_args(0, whole(w_in0, own_first[0]), whole(w_out0, own_first[1]))
    x1, s0, landed = _dn_layer_fwd(x0, norm_g[0], *dn0, "l0", send=own_mid)
    landed = _forward_halves(landed, name="forward_halves_mid")
    sb_in, sb_out, sc_in, sc_out = [whole(g4, own) for g4, own in zip(landed, own_mid)]
    sb_args = (flat(sb_in), sb_q_norm_g[0], sb_k_norm_g[0], rows_of(sb_out))
    sc_args = (flat(sc_in), full["sc_conv_w"][0], rows_of(sc_out))
    x2, s1, landed = _sb_layer_fwd(x1, norm_g[1], *sb_args, "l1", send=own_last)
    landed = _forward_halves(landed, name="forward_halves_last")
    dn1 = dn_args(1, *[whole(g4, own) for g4, own in zip(landed, own_last)])
    x3, s2 = _sc_layer_fwd(x2, norm_g[2], *sc_args, "l2")
    x4, s3, _ = _dn_layer_fwd(x3, norm_g[3], *dn1, "l3")
    dy, loss_local = _loss_head(x4, loss_target[0], name="loss_head")
    loss = lax.psum(loss_local[0, 0], ("x", "y", "c"))

    core = ci.astype(jnp.int32).reshape(1)
    chip = me.astype(jnp.int32).reshape(1)

    def chip_sums(g_list, tag):
        sib = _swap_other_half(g_list, name=f"swap_halves_{tag}")
        return [_add_my_half(g, s, core, name=f"sum_cores_{tag}{i}") for i, (g, s) in enumerate(zip(g_list, sib))]

    dx3, dng3, dwin3, dconv3, dal3, ddt3, dgain3, dwout3, _ = _dn_layer_bwd(dy, x3, norm_g[3], *dn1, s3, "l3")
    dx2, dng2, dwin2, dconv2, dwout2 = _sc_layer_bwd(dx3, x2, norm_g[2], *sc_args, s2, "l2")
    dx1, dng1, dwin1, dgq, dgk, dwout1 = _sb_layer_bwd(dx2, x1, norm_g[1], *sb_args, s1, "l1")
    part_later = chip_sums([_cut2(_by_cols(dwin3)), _cut2(_by_rows(dwout3)), _cut2(dwin1), _cut2(_by_rows(dwout1)),
                            _cut2(dwin2), _cut2(_by_rows(dwout2))], "later")
    dx0, dng0, part_win0, dconv0, dal0, ddt0, dgain0, part_wout0, landed = _dn_layer_bwd(
        dx1, x0, norm_g[0], *dn0, s0, "l0", send=part_later, chip_sums=chip_sums)
    pieces = later + [("dn_w_out", 0), ("dn_w_in", 0)]
    mine = {(n, l): _sum_chips(r, p, chip, name=f"sum_chips_{n}{l}")
            for (n, l), r, p in zip(pieces, landed, part_later + [part_wout0, part_win0])}
    pieces = sorted(pieces, key=lambda nl: nl[1])
    mine = [mine[nl] for nl in pieces]
    theirs = _sibling_exchange(mine, name="swap_results")
    upd = {}
    for (n, l), a, b in zip(pieces, mine, theirs):
        upd[n] = _adamw_halves(weights[n], a, b, m_in[n], v_in[n], core, layer=l, prev=upd.get(n),
                               name=f"adamw_{n}{l}")
    g_out = {n: upd[n][0] for n, _, _ in BIG}

    grads = dict(
        norm_g=jnp.concatenate([dng0, dng1, dng2, dng3], axis=0), dn_conv_w=jnp.stack([dconv0, dconv3]),
        dn_a_log=jnp.stack([dal0, dal3]), dn_dt_bias=jnp.stack([ddt0, ddt3]),
        dn_o_norm_g=jnp.stack([dgain0, dgain3]), sb_q_norm_g=dgq[None], sb_k_norm_g=dgk[None],
        sc_conv_w=dconv2[None])
    repl = [jnp.broadcast_to(grads[n][None], (N_CHIPS,) + s) for n, s in REPL]
    gsmall = _pack([_split(grads[n], ax) for n, _, ax in SMALL] + repl, LANES, (N_CHIPS,))
    rsmall, = _chip_exchange([gsmall], send_slot_is_dest=True, copy_own=(True,), name="scatter_small")
    psmall = _sum_small(rsmall, name="sum_chips_small")
    qsmall, = _sibling_exchange([psmall], name="swap_cores_small")
    tsmall = _add(psmall, qsmall, name="sum_cores_small")
    g_out.update(_unpack(tsmall, SMALL + REPL))

    for n in order:
        if n not in upd:
            upd[n] = (g_out[n],) + _adamw(weights[n], g_out[n], m_in[n], v_in[n], name=f"adamw_{n}")
    return (loss, dx0[None], *[upd[n][0] for n in order], *[upd[n][1] for n in order],
            *[upd[n][2] for n in order], *[upd[n][3] for n in order])
```

```python
import functools
import math

import jax
import jax.numpy as jnp
from jax import lax
from jax.experimental import pallas as pl
from jax.experimental.pallas import tpu as pltpu

F32 = jnp.float32
BF16 = jnp.bfloat16
MESH = pl.DeviceIdType.MESH

RMS_EPS = 1e-6
L2_EPS = 1e-6
LANES = 128
VMEM_BIG = 60 * 1024 * 1024
MM_VMEM = 44 * 1024 * 1024

DN_HEADS, DN_DK, DN_DV, DN_CHUNK, DN_CONV = 8, 128, 256, 64, 4
DN_QK_W = DN_HEADS * DN_DK
DN_V_W = DN_HEADS * DN_DV
DN_CONV_W = 2 * DN_QK_W + DN_V_W
DN_IN = DN_CONV_W + DN_V_W + 2 * DN_HEADS
DN_IN_PAD = DN_CONV_W + DN_V_W + LANES
SB_DH = 64
SC_CONV = 3

ADAM_LR, ADAM_B1, ADAM_B2, ADAM_EPS, ADAM_WD, ADAM_STEP = 0.001, 0.9, 0.999, 1e-08, 0.01, 10


def _pick(n, cands):
    for c in cands:
        if n % c == 0:
            return c
    raise ValueError(f"no tile for {n} in {cands}")


def _bf(x):
    return x.astype(BF16)


def _dot(a, b):
    return jnp.dot(_bf(a), _bf(b), preferred_element_type=F32)


def _dot_nt(a, b):
    return lax.dot_general(_bf(a), _bf(b), (((1,), (1,)), ((), ())), preferred_element_type=F32)


def _dot_tn(a, b):
    return lax.dot_general(_bf(a), _bf(b), (((0,), (0,)), ((), ())), preferred_element_type=F32)


def _split3(a):
    hi = _bf(a)
    r = a - hi.astype(F32)
    mid = _bf(r)
    lo = _bf(r - mid.astype(F32))
    return hi, mid, lo


def _sigmoid(x):
    return 1.0 / (1.0 + jnp.exp(-x))


def _silu(x):
    return x * _sigmoid(x)


def _dsilu(x):
    s = _sigmoid(x)
    return s * (1.0 + x * (1.0 - s))


def _softplus(x):
    return jnp.maximum(x, 0.0) + jnp.log(1.0 + jnp.exp(-jnp.abs(x)))


def _shift_down(z, k):
    if k == 0:
        return z
    row = lax.broadcasted_iota(jnp.int32, z.shape, 0)
    return jnp.where(row >= k, pltpu.roll(z, k, 0), 0.0)


def _shift_up(z, k):
    if k == 0:
        return z
    n = z.shape[0]
    row = lax.broadcasted_iota(jnp.int32, z.shape, 0)
    return jnp.where(row < n - k, pltpu.roll(z, n - k, 0), 0.0)


def _matmul(a, b, *, mode, name, res=None, a_parts=1, b_parts=1, out_parts=1, out_dtype=F32, send=()):
    def dims2(x, parts):
        if parts == 1:
            return x.shape
        assert x.shape[0] == parts
        return (x.shape[1], x.shape[2] * parts)

    ash, bsh = dims2(a, a_parts), dims2(b, b_parts)
    if mode == "nn":
        (M, K), (K2, N) = ash, bsh
        dn = (((1,), (0,)), ((), ()))
    elif mode == "nt":
        (M, K), (N, K2) = ash, bsh
        dn = (((1,), (1,)), ((), ()))
    else:
        (K, M), (K2, N) = ash, bsh
        dn = (((0,), (0,)), ((), ()))
    assert K == K2, (ash, bsh, mode)
    tm_max = _pick(M, (512, 256, 128, 64, 32, 16, 8))
    n_unit = N // max(out_parts, b_parts if mode != "nt" else 1)
    k_unit = K // max(a_parts if mode != "tn" else 1, b_parts if mode == "nt" else 1)
    tm, tn, tk = min(
        ((m, n, k) for m in {tm_max, max(tm_max // 2, 8)}
         for n in (2048, 1792, 1024, 896, 768, 512, 384, 256, 128) if n_unit % n == 0
         for k in (k_unit, 2048, 1792, 1024, 896, 512, 256, 128) if k_unit % k == 0
         if 2 * (m * k * a.dtype.itemsize + k * n * b.dtype.itemsize + 2 * m * n * 4) + m * n * 4 <= MM_VMEM),
        key=lambda t: (-t[0] * t[1] * t[2], -t[0], -t[2]))
    nk = K // tk
    grid = (M // tm, N // tn, nk)

    def spec(parts, rows_are, cols_are, tr, tc, width):
        per = width // parts // tc
        if parts == 1:
            return pl.BlockSpec((tr, tc), lambda i, j, k: ((i, j, k)[rows_are], (i, j, k)[cols_are]))
        return pl.BlockSpec((None, tr, tc), lambda i, j, k: ((i, j, k)[cols_are] // per, (i, j, k)[rows_are],
                                                             (i, j, k)[cols_are] % per))

    if mode == "nn":
        a_spec = spec(a_parts, 0, 2, tm, tk, K)
        b_spec = spec(b_parts, 2, 1, tk, tn, N)
    elif mode == "nt":
        a_spec = spec(a_parts, 0, 2, tm, tk, K)
        b_spec = spec(b_parts, 1, 2, tn, tk, K)
    else:
        a_spec = spec(a_parts, 2, 0, tk, tm, M)
        b_spec = spec(b_parts, 2, 1, tk, tn, N)
    o_spec = spec(out_parts, 0, 1, tm, tn, N)
    in_specs = [a_spec, b_spec]
    operands = [a, b]
    if res is not None:
        in_specs.append(pl.BlockSpec((tm, tn), lambda i, j, k: (i, j)))
        operands.append(res)

    n_in = len(operands)
    ns = len(send)

    def finish(refs, r):
        if res is not None:
            r = refs[2][...] + r
        refs[n_in + ns][...] = r.astype(out_dtype)

    def body(*refs):
        if ns:
            at = lambda step: functools.reduce(jnp.logical_and, [pl.program_id(d) == step[d] for d in range(3)])
            _blocks_over_ici(refs[n_in:n_in + ns], refs[n_in + ns + 1:n_in + 2 * ns + 1], refs[-2], refs[-1],
                             at((0, 0, 0)), at(tuple(g - 1 for g in grid)))
        part = lax.dot_general(_bf(refs[0][...]), _bf(refs[1][...]), dn, preferred_element_type=F32)
        if nk == 1:
            finish(refs, part)
            return
        acc_ref = refs[n_in + 2 * ns + 1]
        k = pl.program_id(2)

        @pl.when(k == 0)
        def _():
            acc_ref[...] = part

        @pl.when(jnp.logical_and(k > 0, k < nk - 1))
        def _():
            acc_ref[...] += part

        @pl.when(k == nk - 1)
        def _():
            finish(refs, acc_ref[...] + part)

    out_shape = (M, N) if out_parts == 1 else (out_parts, M, N // out_parts)
    out = pl.pallas_call(
        body, name=name, grid=grid, in_specs=in_specs + [HBM] * ns, out_specs=[o_spec] + [HBM] * ns,
        out_shape=[jax.ShapeDtypeStruct(out_shape, out_dtype)] + [jax.ShapeDtypeStruct(x.shape, x.dtype) for x in send],
        scratch_shapes=([pltpu.VMEM((tm, tn), F32)] if nk > 1 else [])
        + ([pltpu.SemaphoreType.DMA((3 * ns,)), pltpu.SemaphoreType.DMA((3 * ns,))] if ns else []),
        compiler_params=pltpu.CompilerParams(
            dimension_semantics=("arbitrary",) * 3 if ns else ("parallel", "parallel", "arbitrary"),
            vmem_limit_bytes=VMEM_BIG),
    )(*operands, *send)
    return out if ns else out[0]


def _rmsnorm_fwd(x, g, *, name):
    T, D = x.shape
    tm = _pick(T, (512, 256, 128, 64, 32, 16))

    def body(x_ref, g_ref, h_ref):
        xv = x_ref[...]
        r = lax.rsqrt(jnp.mean(xv * xv, axis=-1, keepdims=True) + RMS_EPS)
        h_ref[...] = ((xv * r) * g_ref[...]).astype(BF16)

    return pl.pallas_call(
        body, name=name, grid=(T // tm,),
        in_specs=[pl.BlockSpec((tm, D), lambda i: (i, 0)), pl.BlockSpec((1, D), lambda i: (0, 0))],
        out_specs=pl.BlockSpec((tm, D), lambda i: (i, 0)),
        out_shape=jax.ShapeDtypeStruct((T, D), BF16),
    )(x, g.reshape(1, D))


def _rmsnorm_bwd(x, g, dh, dx_in, *, name):
    T, D = x.shape
    tm = _pick(T, (512, 256, 128, 64, 32, 16))

    def body(x_ref, g_ref, dh_ref, dxin_ref, dx_ref, dg_ref):
        @pl.when(pl.program_id(0) == 0)
        def _():
            dg_ref[...] = jnp.zeros_like(dg_ref)

        xv = x_ref[...]
        r = lax.rsqrt(jnp.mean(xv * xv, axis=-1, keepdims=True) + RMS_EPS)
        xh = xv * r
        dh_v = dh_ref[...]
        dxh = dh_v * g_ref[...]
        dx_ref[...] = dxin_ref[...] + r * (dxh - xh * jnp.mean(dxh * xh, axis=-1, keepdims=True))
        dg_ref[...] += jnp.sum(dh_v * xh, axis=0, keepdims=True)

    row = pl.BlockSpec((tm, D), lambda i: (i, 0))
    vec = pl.BlockSpec((1, D), lambda i: (0, 0))
    return pl.pallas_call(
        body, name=name, grid=(T // tm,),
        in_specs=[row, vec, row, row], out_specs=[row, vec],
        out_shape=[jax.ShapeDtypeStruct((T, D), F32), jax.ShapeDtypeStruct((1, D), F32)],
        compiler_params=pltpu.CompilerParams(dimension_semantics=("arbitrary",)),
    )(x, g.reshape(1, D), dh, dx_in)


def _loss_head(y, target, *, name):
    T, D = y.shape
    tm = _pick(T, (512, 256, 128, 64, 32, 16))

    def body(y_ref, t_ref, dy_ref, l_ref):
        @pl.when(pl.program_id(0) == 0)
        def _():
            l_ref[...] = jnp.zeros_like(l_ref)

        err = y_ref[...] - t_ref[...]
        dy_ref[...] = err * (1.0 / D)
        l_ref[...] += 0.5 * jnp.sum(jnp.mean(err * err, axis=-1, keepdims=True), axis=0, keepdims=True)

    row = pl.BlockSpec((tm, D), lambda i: (i, 0))
    return pl.pallas_call(
        body, name=name, grid=(T // tm,),
        in_specs=[row, row], out_specs=[row, pl.BlockSpec((1, 1), lambda i: (0, 0))],
        out_shape=[jax.ShapeDtypeStruct((T, D), F32), jax.ShapeDtypeStruct((1, 1), F32)],
        compiler_params=pltpu.CompilerParams(dimension_semantics=("arbitrary",)),
    )(y, target)


def _sc_mid_fwd(p3, conv_w, *, name):
    _, T, W = p3.shape
    K = conv_w.shape[0]
    cw = LANES

    def body(p_ref, w_ref, o_ref):
        z = p_ref[1] * p_ref[2]
        cv = sum(w_ref[i:i + 1, :] * _shift_down(z, K - 1 - i) for i in range(K))
        o_ref[...] = ((p_ref[0] * cv) * _silu(p_ref[3])).astype(BF16)

    return pl.pallas_call(
        body, name=name, grid=(W // cw,),
        in_specs=[pl.BlockSpec((4, T, cw), lambda j: (0, 0, j)), pl.BlockSpec((K, cw), lambda j: (0, j))],
        out_specs=pl.BlockSpec((T, cw), lambda j: (0, j)),
        out_shape=jax.ShapeDtypeStruct((T, W), BF16),
        compiler_params=pltpu.CompilerParams(dimension_semantics=("parallel",), vmem_limit_bytes=VMEM_BIG),
    )(p3, conv_w)


def _sc_mid_bwd(p3, conv_w, do, *, name):
    _, T, W = p3.shape
    K = conv_w.shape[0]
    cw = LANES

    def body(p_ref, w_ref, do_ref, dp_ref, dw_ref):
        b, c, u, gate = p_ref[0], p_ref[1], p_ref[2], p_ref[3]
        z = c * u
        zs = [_shift_down(z, K - 1 - i) for i in range(K)]
        cv = sum(w_ref[i:i + 1, :] * zs[i] for i in range(K))
        y = b * cv
        dov = do_ref[...]
        dy = dov * _silu(gate)
        dp_ref[3] = dov * y * _dsilu(gate)
        dp_ref[0] = dy * cv
        dcv = dy * b
        dz = sum(w_ref[i:i + 1, :] * _shift_up(dcv, K - 1 - i) for i in range(K))
        dp_ref[1] = dz * u
        dp_ref[2] = dz * c
        for i in range(K):
            dw_ref[i:i + 1, :] = jnp.sum(dcv * zs[i], axis=0, keepdims=True)

    return pl.pallas_call(
        body, name=name, grid=(W // cw,),
        in_specs=[pl.BlockSpec((4, T, cw), lambda j: (0, 0, j)), pl.BlockSpec((K, cw), lambda j: (0, j)),
                  pl.BlockSpec((T, cw), lambda j: (0, j))],
        out_specs=[pl.BlockSpec((4, T, cw), lambda j: (0, 0, j)), pl.BlockSpec((K, cw), lambda j: (0, j))],
        out_shape=[jax.ShapeDtypeStruct((4, T, W), F32), jax.ShapeDtypeStruct((K, W), F32)],
        compiler_params=pltpu.CompilerParams(dimension_semantics=("parallel",), vmem_limit_bytes=VMEM_BIG),
    )(p3, conv_w, do)


def _sc_layer_fwd(x, ng, w_in, conv_w, w_out, tag):
    h = _rmsnorm_fwd(x, ng, name=f"{tag}_norm")
    p3 = _matmul(h, w_in, mode="nn", b_parts=4, out_parts=4, name=f"{tag}_inproj")
    og = _sc_mid_fwd(p3, conv_w, name=f"{tag}_mid")
    x_new = _matmul(og, w_out, mode="nn", res=x, name=f"{tag}_outproj")
    return x_new, (h, p3, og)


def _sc_layer_bwd(dx, x, ng, w_in, conv_w, w_out, saved, tag):
    h, p3, og = saved
    d_wout = _matmul(og, dx, mode="tn", out_dtype=BF16, name=f"{tag}_dwout")
    dog = _matmul(dx, w_out, mode="nt", name=f"{tag}_dog")
    dp3, dconv = _sc_mid_bwd(p3, conv_w, dog, name=f"{tag}_midbwd")
    d_win = _matmul(h, dp3, mode="tn", b_parts=4, out_parts=4, out_dtype=BF16, name=f"{tag}_dwin")
    dh = _matmul(dp3, w_in, mode="nt", a_parts=4, b_parts=4, name=f"{tag}_dh")
    dx_prev, dng = _rmsnorm_bwd(x, ng, dh, dx, name=f"{tag}_normbwd")
    return dx_prev, dng, d_win, dconv, d_wout


SB_BQ = 256
SB_BK = 256
SB_ROWS = 512
SB_DEAD = -110.0


def _sb_half_mask():
    return lax.broadcasted_iota(jnp.int32, (1, LANES), 1) < SB_DH


def _sb_headnorm(x, g, lo):
    x2 = x * x
    s_lo = jnp.sum(jnp.where(lo, x2, 0.0), axis=-1, keepdims=True)
    s_hi = jnp.sum(jnp.where(lo, 0.0, x2), axis=-1, keepdims=True)
    r = lax.rsqrt(jnp.where(lo, s_lo, s_hi) * (1.0 / SB_DH) + RMS_EPS)
    xh = x * r
    return xh * g, xh, r


def _dot_x2_l(a_l, b_exact_bf16):
    his = [_bf(a) for a in a_l]
    mids = [_bf(a - h.astype(F32)) for a, h in zip(a_l, his)]
    f = lambda p: jnp.dot(p, b_exact_bf16, preferred_element_type=F32)
    return [x + y for x, y in zip([f(h) for h in his], [f(m) for m in mids])]


def _sb_stack(xb, lo):
    zero = jnp.zeros_like(xb)
    return jnp.concatenate([jnp.where(lo, xb, zero), jnp.where(lo, zero, xb)], axis=0)


def _sb_rel(bq, bk):
    row = lax.broadcasted_iota(jnp.int32, (2 * bq, bk), 0)
    col = lax.broadcasted_iota(jnp.int32, (2 * bq, bk), 1)
    return col - jnp.where(row >= bq, row - bq, row)


def _sb_tile(qm, kb, valid):
    z = lax.dot_general(qm, kb, (((1,), (1,)), ((), ())), preferred_element_type=F32)
    sp = _softplus(z)
    return z - sp, (-sp if valid is None else jnp.where(valid, -sp, 0.0))


def _sb_attn_fwd(p3, gq2, gk2, *, name, send=()):
    _, T, W = p3.shape
    bq, bk = min(SB_BQ, T), min(SB_BK, T)
    rows = min(SB_ROWS, T)
    scale = SB_DH ** -0.5
    ns = len(send)
    npair = W // LANES

    def body(*refs):
        p_ref, gq_ref, gk_ref = refs[:3]
        og_ref, o_ref = refs[3 + ns:5 + ns]
        qn_ref, kn_ref, v_ref = refs[5 + 2 * ns:8 + 2 * ns]
        if ns:
            _halves_over_ici(refs[3:3 + ns], refs[5 + ns:5 + 2 * ns], refs[8 + 2 * ns], refs[9 + 2 * ns],
                             pl.program_id(0) == 0, pl.program_id(0) == npair - 1)
        lo = _sb_half_mask()

        def prologue(i, c):
            r0 = pl.multiple_of(i * rows, rows)
            sl = pl.ds(r0, rows)
            qn_ref[sl, :] = (_sb_headnorm(p_ref[0, sl, :], gq_ref[...], lo)[0] * scale).astype(BF16)
            kn_ref[sl, :] = _sb_headnorm(p_ref[1, sl, :], gk_ref[...], lo)[0].astype(BF16)
            v_ref[sl, :] = p_ref[2, sl, :].astype(BF16)
            return c

        lax.fori_loop(0, T // rows, prologue, 0)

        rel = _sb_rel(bq, bk)
        tri = (lax.broadcasted_iota(jnp.int32, (bk, bk), 0)
               > lax.broadcasted_iota(jnp.int32, (bk, bk), 1)).astype(BF16)

        def qblock(qi, c):
            q0 = pl.multiple_of(qi * bq, bq)
            qm = _sb_stack(qn_ref[pl.ds(q0, bq), :], lo)
            nkb = (q0 + bq - 1) // bk + 1

            def tiles(k0s, carry, valids):
                o_acc, a_carry = carry
                sc = [_sb_tile(qm, kn_ref[pl.ds(k0, bk), :], valid) for k0, valid in zip(k0s, valids)]
                later = _dot_x2_l([log1m for _, log1m in sc], tri)
                for (logsig, log1m), lat, k0, valid in zip(sc, later, k0s, valids):
                    wts = jnp.exp(logsig + (lat + a_carry))
                    if valid is not None:
                        wts = jnp.where(valid, wts, 0.0)
                    o_acc = o_acc + jnp.dot(_bf(wts), v_ref[pl.ds(k0, bk), :], preferred_element_type=F32)
                    a_carry = a_carry + jnp.sum(log1m, axis=-1, keepdims=True)
                return o_acc, a_carry

            blk0 = lambda j: pl.multiple_of(j * bk, bk)
            k_last = blk0(nkb - 1)
            o2, t2 = tiles([k_last, blk0(jnp.maximum(nkb - 2, 0))],
                           (jnp.zeros((2 * bq, LANES), F32), jnp.zeros((2 * bq, 1), F32)),
                           [rel < q0 - k_last, nkb >= 2])

            def alive(st):
                return jnp.logical_and(st[0] < nkb - 1, jnp.max(st[2]) > SB_DEAD)

            def back_one(st):
                return (st[0] + 1,) + tiles([blk0(nkb - 2 - st[0])], st[1:], [None])

            _, o2, _ = lax.while_loop(alive, back_one, (jnp.int32(1), o2, t2))
            o = jnp.where(lo, o2[:bq], o2[bq:])
            o_ref[pl.ds(q0, bq), :] = o
            og_ref[pl.ds(q0, bq), :] = (o * _silu(p_ref[3, pl.ds(q0, bq), :])).astype(BF16)
            return c

        lax.fori_loop(0, T // bq, qblock, 0)

    colblk = pl.BlockSpec((T, LANES), lambda j: (0, j))
    vec = pl.BlockSpec((1, LANES), lambda j: (0, 0))
    return pl.pallas_call(
        body, name=name, grid=(npair,),
        in_specs=[pl.BlockSpec((4, T, LANES), lambda j: (0, 0, j)), vec, vec] + [HBM] * ns,
        out_specs=[colblk, colblk] + [HBM] * ns,
        out_shape=[jax.ShapeDtypeStruct((T, W), BF16), jax.ShapeDtypeStruct((T, W), F32)]
        + [jax.ShapeDtypeStruct((N_CHIPS,) + a.shape, a.dtype) for a in send],
        scratch_shapes=[pltpu.VMEM((T, LANES), BF16)] * 3
        + ([pltpu.SemaphoreType.DMA((3 * ns,)), pltpu.SemaphoreType.DMA((3 * ns,))] if ns else []),
        compiler_params=pltpu.CompilerParams(dimension_semantics=("arbitrary",), vmem_limit_bytes=VMEM_BIG),
    )(p3, gq2, gk2, *send)


def _sb_attn_bwd(p3, gq2, gk2, o, dog, *, name):
    _, T, W = p3.shape
    bq, bk = min(SB_BQ, T), min(SB_BK, T)
    rows = min(SB_ROWS, T)
    scale = SB_DH ** -0.5

    def body(p_ref, gq_ref, gk_ref, o_ref, dog_ref, dp_ref, dgq_ref, dgk_ref,
             qn_ref, kn_ref, v_ref, do_ref):
        lo = _sb_half_mask()

        def prologue(i, c):
            r0 = pl.multiple_of(i * rows, rows)
            sl = pl.ds(r0, rows)
            qn_ref[sl, :] = (_sb_headnorm(p_ref[0, sl, :], gq_ref[...], lo)[0] * scale).astype(BF16)
            kn_ref[sl, :] = _sb_headnorm(p_ref[1, sl, :], gk_ref[...], lo)[0].astype(BF16)
            v_ref[sl, :] = p_ref[2, sl, :].astype(BF16)
            gate = p_ref[3, sl, :]
            dogv = dog_ref[sl, :]
            dp_ref[3, sl, :] = dogv * o_ref[sl, :] * _dsilu(gate)
            do_ref[sl, :] = (dogv * _silu(gate)).astype(BF16)
            zero = jnp.zeros((rows, LANES), F32)
            dp_ref[0, sl, :] = zero
            dp_ref[1, sl, :] = zero
            dp_ref[2, sl, :] = zero
            return c

        lax.fori_loop(0, T // rows, prologue, 0)

        rel = _sb_rel(bq, bk)
        rj = lax.broadcasted_iota(jnp.int32, (bk, bk), 0)
        cj = lax.broadcasted_iota(jnp.int32, (bk, bk), 1)
        upto = (rj <= cj).astype(BF16)
        before_m = (rj < cj).astype(BF16)

        def qblock(qi, c):
            q0 = pl.multiple_of(qi * bq, bq)
            qm = _sb_stack(qn_ref[pl.ds(q0, bq), :], lo)
            dom = _sb_stack(do_ref[pl.ds(q0, bq), :], lo)
            nkb = (q0 + bq - 1) // bk + 1
            blk0 = lambda j: pl.multiple_of(j * bk, bk)
            k_last = blk0(nkb - 1)

            def row_sums(k0, valid):
                return jnp.sum(_sb_tile(qm, kn_ref[pl.ds(k0, bk), :], valid)[1], axis=-1, keepdims=True)

            def alive(st):
                return jnp.logical_and(st[0] < nkb, jnp.max(st[1]) > SB_DEAD)

            def back_one(st):
                return st[0] + 1, st[1] + row_sums(blk0(nkb - 1 - st[0]), None)

            n_live, total = lax.while_loop(alive, back_one, (jnp.int32(1), row_sums(k_last, rel < q0 - k_last)))
            k_first = nkb - n_live

            def tiles(k0s, carry, valids):
                dq_acc, a_pre, r_pre = carry
                kss = [pl.ds(k0, bk) for k0 in k0s]
                kbs = [kn_ref[ks, :] for ks in kss]
                sc = [_sb_tile(qm, kb, valid) for kb, valid in zip(kbs, valids)]
                dws = [lax.dot_general(dom, v_ref[ks, :], _NT, preferred_element_type=F32) for ks in kss]
                upto_l = _dot_x2_l([log1m for _, log1m in sc], upto)
                wts_l = []
                for (logsig, log1m), up, valid in zip(sc, upto_l, valids):
                    wts = jnp.exp(logsig + ((total - a_pre) - up))
                    wts_l.append(wts if valid is None else jnp.where(valid, wts, 0.0))
                    a_pre = a_pre + jnp.sum(log1m, axis=-1, keepdims=True)
                ee_l = [dw * wts for dw, wts in zip(dws, wts_l)]
                before_l = _dot_x2_l(ee_l, before_m)
                for (logsig, _), ks, kb, wts, ee, bef, valid in zip(sc, kss, kbs, wts_l, ee_l, before_l, valids):
                    beta = jnp.exp(logsig)
                    dz = ee * (1.0 - beta) - beta * (r_pre + bef)
                    if valid is not None:
                        dz = jnp.where(valid, dz, 0.0)
                    dzb = _bf(dz)
                    dq_acc = dq_acc + jnp.dot(dzb, kb, preferred_element_type=F32)
                    dp_ref[1, ks, :] += lax.dot_general(dzb, qm, _TN, preferred_element_type=F32)
                    dp_ref[2, ks, :] += lax.dot_general(_bf(wts), dom, _TN, preferred_element_type=F32)
                    r_pre = r_pre + jnp.sum(ee, axis=-1, keepdims=True)
                return dq_acc, a_pre, r_pre

            cr = (jnp.zeros((2 * bq, LANES), F32), jnp.zeros((2 * bq, 1), F32), jnp.zeros((2 * bq, 1), F32))
            n_before = jnp.maximum(n_live - 2, 0)
            cr = lax.fori_loop(0, n_before % 2, lambda t, cr: tiles([blk0(k_first)], cr, [None]), cr)
            k_pairs = k_first + n_before % 2
            cr = lax.fori_loop(0, n_before // 2,
                               lambda t, cr: tiles([blk0(k_pairs + 2 * t), blk0(k_pairs + 2 * t + 1)], cr,
                                                   [None, None]), cr)
            dq2, _, _ = tiles([blk0(jnp.maximum(nkb - 2, 0)), k_last], cr, [n_live >= 2, rel < q0 - k_last])
            dp_ref[0, pl.ds(q0, bq), :] = jnp.where(lo, dq2[:bq], dq2[bq:]) * scale
            return c

        lax.fori_loop(0, T // bq, qblock, 0)

        dgq_ref[...] = jnp.zeros_like(dgq_ref)
        dgk_ref[...] = jnp.zeros_like(dgk_ref)

        def epilogue(i, c):
            r0 = pl.multiple_of(i * rows, rows)
            sl = pl.ds(r0, rows)
            for part, g_ref, dg_ref in ((0, gq_ref, dgq_ref), (1, gk_ref, dgk_ref)):
                _, xh, r = _sb_headnorm(p_ref[part, sl, :], g_ref[...], lo)
                dn = dp_ref[part, sl, :]
                dxh = dn * g_ref[...]
                prod = dxh * xh
                m_lo = jnp.sum(jnp.where(lo, prod, 0.0), axis=-1, keepdims=True)
                m_hi = jnp.sum(jnp.where(lo, 0.0, prod), axis=-1, keepdims=True)
                m = jnp.where(lo, m_lo, m_hi) * (1.0 / SB_DH)
                dp_ref[part, sl, :] = r * (dxh - xh * m)
                dg_ref[...] += jnp.sum(dn * xh, axis=0, keepdims=True)
            return c

        lax.fori_loop(0, T // rows, epilogue, 0)

    colblk = pl.BlockSpec((T, LANES), lambda j: (0, j))
    vec = pl.BlockSpec((1, LANES), lambda j: (0, 0))
    part = pl.BlockSpec((4, T, LANES), lambda j: (0, 0, j))
    gvec = pl.BlockSpec((None, 1, LANES), lambda j: (j, 0, 0))
    npair = W // LANES
    return pl.pallas_call(
        body, name=name, grid=(npair,),
        in_specs=[part, vec, vec, colblk, colblk],
        out_specs=[part, gvec, gvec],
        out_shape=[jax.ShapeDtypeStruct((4, T, W), F32), jax.ShapeDtypeStruct((npair, 1, LANES), F32),
                   jax.ShapeDtypeStruct((npair, 1, LANES), F32)],
        scratch_shapes=[pltpu.VMEM((T, LANES), BF16)] * 4,
        compiler_params=pltpu.CompilerParams(dimension_semantics=("parallel",), vmem_limit_bytes=VMEM_BIG),
    )(p3, gq2, gk2, o, dog)


_NN = (((1,), (0,)), ((), ()))
_NT = (((1,), (1,)), ((), ()))
_TN = (((0,), (0,)), ((), ()))
DN_TB = 512
DN_HEADS_FWD = 4
DN_HEADS_BWD = 2
DN_INV_EXACT_LEVELS = 2
DN_AB_COL = (DN_CONV_W + DN_V_W) // LANES


def _dn_conv(x, w_ref):
    k = w_ref.shape[0]
    return sum(w_ref[i:i + 1, :] * _shift_down(x, k - 1 - i) for i in range(k))


def _dn_prep_fwd(p, conv_w, *, name):
    T = p.shape[0]
    cw = conv_w.shape[1]
    n_qk = 2 * DN_QK_W // LANES

    def body(p_ref, w_ref, o_ref):
        s = _silu(_dn_conv(p_ref[...], w_ref))
        r = lax.rsqrt(jnp.sum(s * s, axis=-1, keepdims=True) + L2_EPS)
        o_ref[...] = jnp.where(pl.program_id(0) < n_qk, s * r, s)

    colblk = pl.BlockSpec((T, LANES), lambda j: (0, j))
    return pl.pallas_call(
        body, name=name, grid=(cw // LANES,),
        in_specs=[colblk, pl.BlockSpec((DN_CONV, LANES), lambda j: (0, j))],
        out_specs=colblk, out_shape=jax.ShapeDtypeStruct((T, cw), F32),
        compiler_params=pltpu.CompilerParams(dimension_semantics=("parallel",), vmem_limit_bytes=VMEM_BIG),
    )(p, conv_w)


def _dn_chunk_tri(rows, upper):
    r = lax.broadcasted_iota(jnp.int32, (rows, rows), 0)
    c = lax.broadcasted_iota(jnp.int32, (rows, rows), 1)
    same = (r // DN_CHUNK) == (c // DN_CHUNK)
    return jnp.logical_and(same, (c >= r) if upper else (c <= r)).astype(BF16)


def _dn_lane_rows(a_log, dt_bias):
    pad = lambda v: jnp.zeros((1, LANES), F32).at[0, :DN_HEADS].set(v)
    return pad(a_log), pad(dt_bias)


def _dn_ab_parts(blk, alog_row, dtb_row):
    lane = lax.broadcasted_iota(jnp.int32, (1, LANES), 1)
    is_a = lane < DN_HEADS
    is_b = jnp.logical_and(lane >= DN_HEADS, lane < 2 * DN_HEADS)
    a_arg = jnp.where(is_a, blk + dtb_row, 0.0)
    neg_exp = jnp.where(is_a, -jnp.exp(alog_row), 0.0)
    log_a = neg_exp * _softplus(a_arg)
    beta = jnp.where(is_b, _sigmoid(blk), 0.0)
    return is_a, is_b, a_arg, neg_exp, log_a, beta


def _dn_ab_fwd(p, alog_row, dtb_row, *, name):
    T = p.shape[0]
    rows = min(DN_TB, T)

    def body(p_ref, al_ref, dt_ref, o_ref):
        _, _, _, _, log_a, beta = _dn_ab_parts(p_ref[...], al_ref[...], dt_ref[...])
        hi, mid, lo_ = _split3(log_a)
        tri = _dn_chunk_tri(rows, upper=False)
        f = lambda q: jnp.dot(tri, q, preferred_element_type=F32)
        o_ref[...] = (f(hi) + f(mid) + f(lo_)) + beta

    blk = pl.BlockSpec((rows, LANES), lambda i: (i, DN_AB_COL))
    vec = pl.BlockSpec((1, LANES), lambda i: (0, 0))
    return pl.pallas_call(
        body, name=name, grid=(T // rows,), in_specs=[blk, vec, vec],
        out_specs=pl.BlockSpec((rows, LANES), lambda i: (i, 0)),
        out_shape=jax.ShapeDtypeStruct((T, LANES), F32),
        compiler_params=pltpu.CompilerParams(dimension_semantics=("parallel",)),
    )(p, alog_row, dtb_row)


def _hp_l(a_l, b_l, dims=_NN):
    sa = [_split3(a)[:2] for a in a_l]
    sb = [_split3(b)[:2] for b in b_l]
    f = lambda p, q: lax.dot_general(p, q, dims, preferred_element_type=F32)
    hh = [f(x[0], y[0]) for x, y in zip(sa, sb)]
    hm = [f(x[0], y[1]) for x, y in zip(sa, sb)]
    mh = [f(x[1], y[0]) for x, y in zip(sa, sb)]
    return [a + (b + c) for a, b, c in zip(hh, hm, mh)]


def _dn_local(qs, k, v, g, beta, nc):
    c = DN_CHUNK
    cut = lambda x: [x[i * c:(i + 1) * c] for i in range(nc)]
    row = lax.broadcasted_iota(jnp.int32, (c, c), 0)
    col = lax.broadcasted_iota(jnp.int32, (c, c), 1)
    eye, lower, strict = row == col, row >= col, row > col
    rowid = lax.broadcasted_iota(jnp.int32, (c, 1), 0)
    eg = jnp.exp(g)
    kb = k * beta
    rhs_k = kb * eg
    g_l, k_l, kb_l, qs_l = cut(g), cut(k), cut(kb), cut(qs)
    g_row_l = [jnp.sum(jnp.where(eye, x, 0.0), axis=0, keepdims=True) for x in g_l]
    dec_l = [jnp.where(lower, jnp.exp(jnp.where(lower, x - y, 0.0)), 0.0) for x, y in zip(g_l, g_row_l)]
    kk_l = [_dot_nt(a, b) for a, b in zip(kb_l, k_l)]
    qk_l = [_dot_nt(a, b) for a, b in zip(qs_l, k_l)]
    low_l = [jnp.where(strict, a * d, 0.0) for a, d in zip(kk_l, dec_l)]
    eye_f = eye.astype(F32)
    pw_l = [-x for x in low_l]
    inv_l = [eye_f + x for x in pw_l]
    plain = lambda a_l, b_l: [_dot(a, b) for a, b in zip(a_l, b_l)]
    for level in range(int(math.log2(c)) - 1):
        mul = _hp_l if level < DN_INV_EXACT_LEVELS else plain
        pw_l = mul(pw_l, pw_l)
        inv_l = [a + b for a, b in zip(inv_l, mul(inv_l, pw_l))]
    u_l = [_dot(a, b) for a, b in zip(inv_l, cut(v * beta))]
    w_l = [_dot(a, b) for a, b in zip(inv_l, cut(rhs_k))]
    aqk_l = [jnp.where(lower, a * d, 0.0) for a, d in zip(qk_l, dec_l)]
    g_last_l = [jnp.sum(jnp.where(rowid == c - 1, x, 0.0), axis=0, keepdims=True) for x in g_l]
    ekd_l = [jnp.exp(a - b) for a, b in zip(g_last_l, g_l)]
    kd_l = [a * b for a, b in zip(k_l, ekd_l)]
    qd_l = cut(qs * eg)
    kw_l = [_dot_tn(a, b) for a, b in zip(kd_l, w_l)]
    qp_l = [q - _dot(a, w) for q, a, w in zip(qd_l, aqk_l, w_l)]
    return dict(eye=eye, lower=lower, strict=strict, dec=dec_l, k=k_l, kb=kb_l, qs=qs_l, low=low_l, inv=inv_l,
                eg=cut(eg), rhs_k=cut(rhs_k), u=u_l, w=w_l, aqk=aqk_l, g_last=g_last_l, qd=qd_l,
                ekd=ekd_l, kd=kd_l, kw=kw_l, qp=qp_l)


def _dn_head_cols(gb_blk, head):
    lane = lax.broadcasted_iota(jnp.int32, (1, LANES), 1)
    g = jnp.sum(jnp.where(lane == head, gb_blk, 0.0), axis=-1, keepdims=True)
    beta = jnp.sum(jnp.where(lane == head + DN_HEADS, gb_blk, 0.0), axis=-1, keepdims=True)
    return g, beta


def _halves_over_ici(s_refs, o_refs, send_sems, recv_sems, first, last):
    x, y, c = _mesh_pos()
    me = 2 * x + y
    chips = _other_chips(x, y)
    pairs = [(a, k) for a in range(len(s_refs)) for k in range(3)]

    def copy(a, k, slot):
        px, py = chips[k]
        return pltpu.make_async_remote_copy(
            src_ref=s_refs[a].at[c], dst_ref=o_refs[a].at[slot, c], send_sem=send_sems.at[3 * a + k],
            recv_sem=recv_sems.at[3 * a + k], device_id=(px, py, c), device_id_type=MESH)

    @pl.when(first)
    def _():
        for a, k in pairs:
            copy(a, k, me).start()

    @pl.when(last)
    def _():
        for a, k in pairs:
            px, py = chips[k]
            copy(a, k, 2 * px + py).wait_recv()
        for a, k in pairs:
            copy(a, k, me).wait_send()


def _dn_delta_fwd(qkv, gb, p, o_gain, *, name, send=()):
    T = qkv.shape[0]
    tb = min(DN_TB, T)
    nb, nc = T // tb, tb // DN_CHUNK
    H = DN_HEADS
    qscale = DN_DK ** -0.5
    ns = len(send)
    hp = DN_HEADS_FWD

    def body(*refs):
        q_ref, k_ref, v_ref, gb_ref, gate_ref, gain_ref = refs[:6]
        o_ref, og_ref, st_ref = refs[6 + ns:9 + ns]
        s_ref = refs[9 + 2 * ns]
        pair, blk = pl.program_id(0), pl.program_id(1)
        if ns:
            _halves_over_ici(refs[6:6 + ns], refs[9 + ns:9 + 2 * ns], refs[10 + 2 * ns], refs[11 + 2 * ns],
                             jnp.logical_and(pair == 0, blk == 0),
                             jnp.logical_and(pair == H // hp - 1, blk == nb - 1))

        @pl.when(blk == 0)
        def _():
            s_ref[...] = jnp.zeros_like(s_ref)

        gbv = gb_ref[...]
        ts, ku, op = [], [], []
        for e in range(hp):
            qk_e, v_e = slice(e * DN_DK, (e + 1) * DN_DK), slice(e * DN_DV, (e + 1) * DN_DV)
            g, beta = _dn_head_cols(gbv, hp * pair + e)
            t = _dn_local(q_ref[:, qk_e] * qscale, k_ref[:, qk_e], v_ref[:, v_e], g, beta, nc)
            ts.append(t)
            ku.append([_dot_tn(a, b) for a, b in zip(t["kd"], t["u"])])
            op.append([_dot(a, b) for a, b in zip(t["aqk"], t["u"])])
        s32 = [s_ref[e] for e in range(hp)]
        s_l = [[] for _ in range(hp)]
        for i in range(nc):
            sb = [_bf(x) for x in s32]
            for e in range(hp):
                st_ref[e, i] = sb[e]
                s_l[e].append(sb[e])
            prod = [_dot(ts[e]["kw"][i], sb[e]) for e in range(hp)]
            s32 = [s32[e] * jnp.exp(ts[e]["g_last"][i]) - prod[e] + ku[e][i] for e in range(hp)]
        for e in range(hp):
            s_ref[e] = s32[e]
        o = jnp.concatenate(
            [jnp.concatenate([_dot(qp, sb) + x for qp, sb, x in zip(ts[e]["qp"], s_l[e], op[e])], axis=0)
             for e in range(hp)], axis=1)
        o_ref[...] = o
        gain = gain_ref[...]
        for e in range(hp):
            v_e = slice(e * DN_DV, (e + 1) * DN_DV)
            oe = o[:, v_e]
            r = lax.rsqrt(jnp.mean(oe * oe, axis=-1, keepdims=True) + RMS_EPS)
            og_ref[:, v_e] = (((oe * r) * gain) * _silu(gate_ref[:, v_e])).astype(BF16)

    qk = lambda col0: pl.BlockSpec((tb, hp * DN_DK), lambda h, i: (i, col0 // (hp * DN_DK) + h))
    vblk = lambda col0: pl.BlockSpec((tb, hp * DN_DV), lambda h, i: (i, col0 // (hp * DN_DV) + h))
    return pl.pallas_call(
        body, name=name, grid=(H // hp, nb),
        in_specs=[qk(0), qk(DN_QK_W), vblk(2 * DN_QK_W), pl.BlockSpec((tb, LANES), lambda h, i: (i, 0)),
                  vblk(DN_CONV_W), pl.BlockSpec((1, DN_DV), lambda h, i: (0, 0))] + [HBM] * ns,
        out_specs=[vblk(0), vblk(0), pl.BlockSpec((hp, nc, DN_DK, DN_DV), lambda h, i: (h, i, 0, 0))] + [HBM] * ns,
        out_shape=[jax.ShapeDtypeStruct((T, DN_V_W), F32), jax.ShapeDtypeStruct((T, DN_V_W), BF16),
                   jax.ShapeDtypeStruct((H, T // DN_CHUNK, DN_DK, DN_DV), BF16)]
        + [jax.ShapeDtypeStruct((N_CHIPS,) + a.shape, a.dtype) for a in send],
        scratch_shapes=[pltpu.VMEM((hp, DN_DK, DN_DV), F32)]
        + ([pltpu.SemaphoreType.DMA((3 * ns,)), pltpu.SemaphoreType.DMA((3 * ns,))] if ns else []),
        compiler_params=pltpu.CompilerParams(dimension_semantics=("arbitrary", "arbitrary")),
    )(qkv, qkv, qkv, gb, p, o_gain, *send)


def _blocks_over_ici(p_refs, o_refs, send_sems, recv_sems, first, last):
    x, y, c = _mesh_pos()
    me = 2 * x + y
    chips = _other_chips(x, y)
    pairs = [(a, k) for a in range(len(p_refs)) for k in range(3)]

    def copy(a, k, slot):
        px, py = chips[k]
        return pltpu.make_async_remote_copy(
            src_ref=p_refs[a].at[2 * px + py], dst_ref=o_refs[a].at[slot], send_sem=send_sems.at[3 * a + k],
            recv_sem=recv_sems.at[3 * a + k], device_id=(px, py, c), device_id_type=MESH)

    @pl.when(first)
    def _():
        for a, k in pairs:
            copy(a, k, me).start()

    @pl.when(last)
    def _():
        for a, k in pairs:
            px, py = chips[k]
            copy(a, k, 2 * px + py).wait_recv()
        for a, k in pairs:
            copy(a, k, me).wait_send()


def _dn_delta_bwd(qkv, gb, p, o_gain, o, states, dog, *, name, send=()):
    T = qkv.shape[0]
    tb = min(DN_TB, T)
    nb, nc = T // tb, tb // DN_CHUNK
    H = DN_HEADS
    qscale = DN_DK ** -0.5
    ns = len(send)
    hp = DN_HEADS_BWD

    def body(*refs):
        q_ref, k_ref, v_ref, gb_ref, gate_ref, gain_ref, o_ref, st_ref, dog_ref = refs[:9]
        dq_ref, dk_ref, dv_ref, dgate_ref, dgb_ref, dgain_ref = refs[9 + ns:15 + ns]
        ds_ref = refs[15 + 2 * ns]
        pair, blk = pl.program_id(0), pl.program_id(1)
        first = jnp.logical_and(pair == 0, blk == 0)
        if ns:
            _blocks_over_ici(refs[9:9 + ns], refs[15 + ns:15 + 2 * ns], refs[16 + 2 * ns], refs[17 + 2 * ns],
                             first, jnp.logical_and(pair == H // hp - 1, blk == nb - 1))

        @pl.when(blk == 0)
        def _():
            ds_ref[...] = jnp.zeros_like(ds_ref)

        @pl.when(first)
        def _():
            dgain_ref[...] = jnp.zeros_like(dgain_ref)

        lane = lax.broadcasted_iota(jnp.int32, (1, LANES), 1)
        c = DN_CHUNK
        cut = lambda x: [x[i * c:(i + 1) * c] for i in range(nc)]
        cat = lambda xs: jnp.concatenate(xs, axis=0)
        rsum = lambda x: jnp.sum(x, axis=-1, keepdims=True)
        gbv, gain = gb_ref[...], gain_ref[...]

        def before_chain(e):
            qk_e, v_e = slice(e * DN_DK, (e + 1) * DN_DK), slice(e * DN_DV, (e + 1) * DN_DV)
            g, beta = _dn_head_cols(gbv, hp * pair + e)
            ov, gate, dogv = o_ref[:, v_e], gate_ref[:, v_e], dog_ref[:, v_e]
            r = lax.rsqrt(jnp.mean(ov * ov, axis=-1, keepdims=True) + RMS_EPS)
            oh = ov * r
            dnrm = dogv * _silu(gate)
            dgate_ref[:, v_e] = dogv * (oh * gain) * _dsilu(gate)
            doh = dnrm * gain
            do_l = cut(r * (doh - oh * jnp.mean(doh * oh, axis=-1, keepdims=True)))
            dgain_ref[...] += jnp.sum(dnrm * oh, axis=0, keepdims=True)
            k, v = k_ref[:, qk_e], v_ref[:, v_e]
            t = _dn_local(q_ref[:, qk_e] * qscale, k, v, g, beta, nc)
            s_l = [st_ref[e, i] for i in range(nc)]
            vn_l = [u - _dot(w, sb) for u, w, sb in zip(t["u"], t["w"], s_l)]
            return dict(
                t=t, beta=beta, v=v, s=s_l, vn=vn_l, egl=[jnp.exp(x) for x in t["g_last"]],
                dqd=[_dot_nt(a, sb) for a, sb in zip(do_l, s_l)], daqk=[_dot_nt(a, b) for a, b in zip(do_l, vn_l)],
                aqk_do=[_dot_tn(a, b) for a, b in zip(t["aqk"], do_l)],
                qp_do=[_dot_tn(a, b) for a, b in zip(t["qp"], do_l)])

        hs = [before_chain(e) for e in range(hp)]
        ds = [ds_ref[e] for e in range(hp)]
        ds_l = [[None] * nc for _ in range(hp)]
        for i in reversed(range(nc)):
            for e in range(hp):
                ds_l[e][i] = ds[e]
            prod = [_dot_tn(hs[e]["t"]["kw"][i], ds[e]) for e in range(hp)]
            ds = [ds[e] * hs[e]["egl"][i] - prod[e] + hs[e]["qp_do"][i] for e in range(hp)]
        for e in range(hp):
            ds_ref[e] = ds[e]

        def after_chain(e):
            hd, t = hs[e], hs[e]["t"]
            lower, strict, eye = t["lower"], t["strict"], t["eye"]
            s_l, vn_l, dqd_l, daqk_l, egl_l, beta, v = (hd["s"], hd["vn"], hd["dqd"], hd["daqk"], hd["egl"],
                                                         hd["beta"], hd["v"])
            dvn_l = [a + _dot(kd, d) for a, kd, d in zip(hd["aqk_do"], t["kd"], ds_l[e])]
            dkd_l = [_dot_nt(a, d) for a, d in zip(vn_l, ds_l[e])]
            dgl_l = [jnp.sum(rsum(d * sb.astype(F32)), axis=0, keepdims=True) * x
                     for d, sb, x in zip(ds_l[e], s_l, egl_l)]
            dw_l = [-_dot_nt(a, sb) for a, sb in zip(dvn_l, s_l)]
            dbv_l = [_dot_tn(a, b) for a, b in zip(t["inv"], dvn_l)]
            dbk_l = [_dot_tn(a, b) for a, b in zip(t["inv"], dw_l)]
            dlow_l = [-(_dot_nt(a, b) + _dot_nt(x, y)) for a, b, x, y in zip(dbv_l, t["u"], dbk_l, t["w"])]
            m_l = [jnp.where(strict, a * d, 0.0) for a, d in zip(dlow_l, t["dec"])]
            nmat_l = [jnp.where(lower, a * d, 0.0) for a, d in zip(daqk_l, t["dec"])]
            dkb_l = [_dot(m, kk) + b * x for m, kk, b, x in zip(m_l, t["k"], dbk_l, t["eg"])]
            dqs_l = [_dot(n, kk) + a * x for n, kk, a, x in zip(nmat_l, t["k"], dqd_l, t["eg"])]
            dk1_l = [_dot_tn(m, kb) for m, kb in zip(m_l, t["kb"])]
            dk2_l = [_dot_tn(n, q) for n, q in zip(nmat_l, t["qs"])]
            beta_l, v_l = cut(beta), cut(v)
            rowid = lax.broadcasted_iota(jnp.int32, (c, 1), 0)
            dk_l, dg_l, dbeta_l = [], [], []
            for i in range(nc):
                dk_l.append(dk1_l[i] + dk2_l[i] + dkd_l[i] * t["ekd"][i] + dkb_l[i] * beta_l[i])
                gmat = jnp.where(strict, dlow_l[i] * t["low"][i], 0.0) + daqk_l[i] * t["aqk"][i]
                s_kd = rsum(dkd_l[i] * t["kd"][i])
                dg = (rsum(gmat) + rsum(dqd_l[i] * t["qd"][i]) - s_kd + rsum(dbk_l[i] * t["rhs_k"][i]))
                dg_row = -jnp.sum(gmat, axis=0, keepdims=True)
                dg = dg + rsum(jnp.where(eye, dg_row, 0.0))
                dgl = dgl_l[i] + jnp.sum(s_kd, axis=0, keepdims=True)
                dg_l.append(dg + jnp.where(rowid == c - 1, dgl, 0.0))
                dbeta_l.append(rsum(dbv_l[i] * v_l[i]) + rsum(dkb_l[i] * t["k"][i]))
            head = hp * pair + e
            dgb = (jnp.where(lane == head, cat(dg_l), 0.0) + jnp.where(lane == head + DN_HEADS, cat(dbeta_l), 0.0))
            return cat(dqs_l) * qscale, cat(dk_l), cat(dbv_l) * beta, dgb

        for e in range(hp):
            dq, dk, dv, dgb = after_chain(e)
            dq_ref[:, e * DN_DK:(e + 1) * DN_DK] = dq
            dk_ref[:, e * DN_DK:(e + 1) * DN_DK] = dk
            dv_ref[:, e * DN_DV:(e + 1) * DN_DV] = dv
            dgb_ref[e] = dgb

    rev = lambda i: nb - 1 - i
    qk = lambda col0: pl.BlockSpec((tb, hp * DN_DK), lambda h, i: (rev(i), col0 // (hp * DN_DK) + h))
    vblk = lambda col0: pl.BlockSpec((tb, hp * DN_DV), lambda h, i: (rev(i), col0 // (hp * DN_DV) + h))
    gain_spec = pl.BlockSpec((1, DN_DV), lambda h, i: (0, 0))
    return pl.pallas_call(
        body, name=name, grid=(H // hp, nb),
        in_specs=[qk(0), qk(DN_QK_W), vblk(2 * DN_QK_W), pl.BlockSpec((tb, LANES), lambda h, i: (rev(i), 0)),
                  vblk(DN_CONV_W), gain_spec, vblk(0),
                  pl.BlockSpec((hp, nc, DN_DK, DN_DV), lambda h, i: (h, rev(i), 0, 0)), vblk(0)] + [HBM] * ns,
        out_specs=[qk(0), qk(0), vblk(0), vblk(DN_CONV_W),
                   pl.BlockSpec((hp, tb, LANES), lambda h, i: (h, rev(i), 0)), gain_spec] + [HBM] * ns,
        out_shape=[jax.ShapeDtypeStruct((T, DN_QK_W), F32), jax.ShapeDtypeStruct((T, DN_QK_W), F32),
                   jax.ShapeDtypeStruct((T, DN_V_W), F32), jax.ShapeDtypeStruct((T, DN_IN_PAD), F32),
                   jax.ShapeDtypeStruct((H, T, LANES), F32), jax.ShapeDtypeStruct((1, DN_DV), F32)]
        + [jax.ShapeDtypeStruct(a.shape, a.dtype) for a in send],
        scratch_shapes=[pltpu.VMEM((hp, DN_DK, DN_DV), F32)]
        + ([pltpu.SemaphoreType.DMA((3 * ns,)), pltpu.SemaphoreType.DMA((3 * ns,))] if ns else []),
        compiler_params=pltpu.CompilerParams(dimension_semantics=("arbitrary", "arbitrary")),
    )(qkv, qkv, qkv, gb, p, o_gain, o, states, dog, *send)


def _dn_conv_bwd(p, conv_w, d, dp, *, first, normed, name):
    T, width = d.shape

    def body(p_ref, w_ref, d_ref, dp_in, dp_ref, dw_ref):
        del dp_in
        x = p_ref[...]
        ksz = w_ref.shape[0]
        xs = [_shift_down(x, ksz - 1 - i) for i in range(ksz)]
        xc = sum(w_ref[i:i + 1, :] * xs[i] for i in range(ksz))
        ds = d_ref[...]
        if normed:
            s = _silu(xc)
            r = lax.rsqrt(jnp.sum(s * s, axis=-1, keepdims=True) + L2_EPS)
            y = s * r
            ds = r * (ds - y * jnp.sum(ds * y, axis=-1, keepdims=True))
        dxc = ds * _dsilu(xc)
        dp_ref[...] = sum(w_ref[i:i + 1, :] * _shift_up(dxc, ksz - 1 - i) for i in range(ksz))
        for i in range(ksz):
            dw_ref[i:i + 1, :] = jnp.sum(dxc * xs[i], axis=0, keepdims=True)

    shifted = pl.BlockSpec((T, LANES), lambda j: (0, first + j))
    return pl.pallas_call(
        body, name=name, grid=(width // LANES,),
        in_specs=[shifted, pl.BlockSpec((DN_CONV, LANES), lambda j: (0, first + j)),
                  pl.BlockSpec((T, LANES), lambda j: (0, j)), pl.BlockSpec(memory_space=pl.ANY)],
        out_specs=[shifted, pl.BlockSpec((DN_CONV, LANES), lambda j: (0, j))],
        out_shape=[jax.ShapeDtypeStruct(dp.shape, F32), jax.ShapeDtypeStruct((DN_CONV, width), F32)],
        input_output_aliases={3: 0},
        compiler_params=pltpu.CompilerParams(dimension_semantics=("parallel",), vmem_limit_bytes=VMEM_BIG),
    )(p, conv_w, d, dp)


def _dn_ab_bwd(p, alog_row, dtb_row, dgb, dp, *, name):
    T = p.shape[0]
    rows = min(DN_TB, T)
    H = DN_HEADS

    def body(p_ref, al_ref, dt_ref, dgb_ref, dp_in, dp_ref, dal_ref, ddt_ref):
        del dp_in

        @pl.when(pl.program_id(0) == 0)
        def _():
            dal_ref[...] = jnp.zeros_like(dal_ref)
            ddt_ref[...] = jnp.zeros_like(ddt_ref)

        blk = p_ref[...]
        is_a, is_b, a_arg, neg_exp, log_a, beta = _dn_ab_parts(blk, al_ref[...], dt_ref[...])
        d = dgb_ref[0]
        for hh in range(1, H):
            d = d + dgb_ref[hh]
        hi, mid, lo_ = _split3(jnp.where(is_a, d, 0.0))
        tri = _dn_chunk_tri(rows, upper=True)
        f = lambda q: jnp.dot(tri, q, preferred_element_type=F32)
        dlog_a = f(hi) + f(mid) + f(lo_)
        da_in = dlog_a * neg_exp * _sigmoid(a_arg)
        db_in = jnp.where(is_b, d, 0.0) * beta * (1.0 - beta)
        dp_ref[...] = jnp.where(is_a, da_in, 0.0) + db_in
        dal_ref[...] += jnp.sum(dlog_a * log_a, axis=0, keepdims=True)
        ddt_ref[...] += jnp.sum(jnp.where(is_a, da_in, 0.0), axis=0, keepdims=True)

    blk = pl.BlockSpec((rows, LANES), lambda i: (i, DN_AB_COL))
    vec = pl.BlockSpec((1, LANES), lambda i: (0, 0))
    return pl.pallas_call(
        body, name=name, grid=(T // rows,),
        in_specs=[blk, vec, vec, pl.BlockSpec((H, rows, LANES), lambda i: (0, i, 0)),
                  pl.BlockSpec(memory_space=pl.ANY)],
        out_specs=[blk, vec, vec],
        out_shape=[jax.ShapeDtypeStruct(dp.shape, F32), jax.ShapeDtypeStruct((1, LANES), F32),
                   jax.ShapeDtypeStruct((1, LANES), F32)],
        input_output_aliases={4: 0},
        compiler_params=pltpu.CompilerParams(dimension_semantics=("arbitrary",)),
    )(p, alog_row, dtb_row, dgb, dp)


def _dn_layer_fwd(x, ng, w_in, conv_w, a_log, dt_bias, o_gain, w_out, tag, send=()):
    alog_row, dtb_row = _dn_lane_rows(a_log, dt_bias)
    gain = o_gain.reshape(1, DN_DV)
    h = _rmsnorm_fwd(x, ng, name=f"{tag}_norm")
    p = _matmul(h, w_in, mode="nn", name=f"{tag}_inproj")
    qkv = _dn_prep_fwd(p, conv_w, name=f"{tag}_prep")
    gb = _dn_ab_fwd(p, alog_row, dtb_row, name=f"{tag}_ab")
    o, og, states, *landed = _dn_delta_fwd(qkv, gb, p, gain, name=f"{tag}_delta", send=send)
    x_new = _matmul(og, w_out, mode="nn", res=x, name=f"{tag}_outproj")
    return x_new, (h, p, qkv, gb, o, og, states), landed


def _dn_layer_bwd(dx, x, ng, w_in, conv_w, a_log, dt_bias, o_gain, w_out, saved, tag, send=(), chip_sums=None):
    h, p, qkv, gb, o, og, states = saved
    alog_row, dtb_row = _dn_lane_rows(a_log, dt_bias)
    gain = o_gain.reshape(1, DN_DV)
    d_wout = _matmul(og, dx, mode="tn", out_dtype=BF16, name=f"{tag}_dwout")
    if chip_sums is not None:
        d_wout, = chip_sums([_cut2(_by_rows(d_wout))], f"{tag}wout")
        send = list(send) + [d_wout]
    dog = _matmul(dx, w_out, mode="nt", name=f"{tag}_dog")
    dq, dk, dv, dp, dgb, dgain, *landed = _dn_delta_bwd(qkv, gb, p, gain, o, states, dog, name=f"{tag}_deltabwd",
                                                        send=send)
    n_qk = DN_QK_W // LANES
    dp, dconv_q = _dn_conv_bwd(p, conv_w, dq, dp, first=0, normed=True, name=f"{tag}_convbwd_q")
    dp, dconv_k = _dn_conv_bwd(p, conv_w, dk, dp, first=n_qk, normed=True, name=f"{tag}_convbwd_k")
    dp, dconv_v = _dn_conv_bwd(p, conv_w, dv, dp, first=2 * n_qk, normed=False, name=f"{tag}_convbwd_v")
    dconv = jnp.concatenate([dconv_q, dconv_k, dconv_v], axis=1)
    dp, dal, ddt = _dn_ab_bwd(p, alog_row, dtb_row, dgb, dp, name=f"{tag}_abbwd")
    d_win = _matmul(h, dp, mode="tn", name=f"{tag}_dwin")
    if chip_sums is not None:
        d_win, = chip_sums([_cut2(_by_cols(d_win))], f"{tag}win")
        dh, landed_win = _matmul(dp, w_in, mode="nt", name=f"{tag}_dh", send=[d_win])
        landed = landed + [landed_win]
    else:
        dh = _matmul(dp, w_in, mode="nt", name=f"{tag}_dh")
    dx_prev, dng = _rmsnorm_bwd(x, ng, dh, dx, name=f"{tag}_normbwd")
    return dx_prev, dng, d_win, dconv, dal[0, :DN_HEADS], ddt[0, :DN_HEADS], dgain[0], d_wout, landed


def _by_cols(dw):
    return _split(dw[:, :DN_IN].astype(BF16), 1)


def _by_rows(dw):
    return dw.reshape(N_CHIPS, -1, dw.shape[-1])


def _cut2(g4):
    return g4.reshape(N_CHIPS, 2, -1, g4.shape[-1])


def _sb_gains(g):
    return jnp.concatenate([g, g]).reshape(1, LANES)


def _sb_layer_fwd(x, ng, w_in, gq, gk, w_out, tag, send=()):
    h = _rmsnorm_fwd(x, ng, name=f"{tag}_norm")
    p3 = _matmul(h, w_in, mode="nn", b_parts=4, out_parts=4, name=f"{tag}_inproj")
    og, o, *landed = _sb_attn_fwd(p3, _sb_gains(gq), _sb_gains(gk), name=f"{tag}_attn", send=send)
    x_new = _matmul(og, w_out, mode="nn", res=x, name=f"{tag}_outproj")
    return x_new, (h, p3, og, o), landed


def _sb_layer_bwd(dx, x, ng, w_in, gq, gk, w_out, saved, tag):
    h, p3, og, o = saved
    d_wout = _matmul(og, dx, mode="tn", out_dtype=BF16, name=f"{tag}_dwout")
    dog = _matmul(dx, w_out, mode="nt", name=f"{tag}_dog")
    dp3, dgq, dgk = _sb_attn_bwd(p3, _sb_gains(gq), _sb_gains(gk), o, dog, name=f"{tag}_attnbwd")
    fold = lambda d: jnp.sum(d.reshape(-1, SB_DH), axis=0)
    d_win = _matmul(h, dp3, mode="tn", b_parts=4, out_parts=4, out_dtype=BF16, name=f"{tag}_dwin")
    dh = _matmul(dp3, w_in, mode="nt", a_parts=4, b_parts=4, name=f"{tag}_dh")
    dx_prev, dng = _rmsnorm_bwd(x, ng, dh, dx, name=f"{tag}_normbwd")
    return dx_prev, dng, d_win, fold(dgq), fold(dgk), d_wout


N_CHIPS = 4
HBM = pl.BlockSpec(memory_space=pl.ANY)


def _mesh_pos():
    return lax.axis_index("x"), lax.axis_index("y"), lax.axis_index("c")


def _other_chips(x, y):
    return [(1 - x, y), (x, 1 - y), (1 - x, 1 - y)]


def _chip_exchange(srcs, *, send_slot_is_dest, copy_own, name):
    n = len(srcs)

    def body(*refs):
        src_refs, out_refs = refs[:n], refs[n:2 * n]
        send_sems, recv_sems, local_sems = refs[2 * n:]
        x, y, c = _mesh_pos()
        me = 2 * x + y
        chips = _other_chips(x, y)
        local = []
        for a in range(n):
            if not copy_own[a]:
                continue
            own = src_refs[a].at[me] if send_slot_is_dest else src_refs[a]
            local.append(pltpu.make_async_copy(own, out_refs[a].at[me], local_sems.at[a]))
        for cp in local:
            cp.start()

        def copy(a, k, landing_slot):
            px, py = chips[k]
            src = src_refs[a].at[2 * px + py] if send_slot_is_dest else src_refs[a]
            return pltpu.make_async_remote_copy(
                src_ref=src, dst_ref=out_refs[a].at[landing_slot],
                send_sem=send_sems.at[a * 3 + k], recv_sem=recv_sems.at[a * 3 + k],
                device_id=(px, py, c), device_id_type=MESH)

        sends = [copy(a, k, me) for a in range(n) for k in range(3)]
        for cp in sends:
            cp.start()
        for a in range(n):
            for k in range(3):
                px, py = chips[k]
                copy(a, k, 2 * px + py).wait_recv()
        for cp in sends:
            cp.wait_send()
        for cp in local:
            cp.wait()

    outs = []
    for s in srcs:
        shape = s.shape if send_slot_is_dest else (N_CHIPS,) + s.shape
        outs.append(jax.ShapeDtypeStruct(shape, s.dtype))
    return pl.pallas_call(
        body, name=name, in_specs=[HBM] * n, out_specs=[HBM] * n, out_shape=outs,
        scratch_shapes=[pltpu.SemaphoreType.DMA((3 * n,)), pltpu.SemaphoreType.DMA((3 * n,)),
                        pltpu.SemaphoreType.DMA((n,))],
    )(*srcs)


def _sibling_exchange(srcs, *, name):
    n = len(srcs)

    def body(*refs):
        src_refs, out_refs = refs[:n], refs[n:2 * n]
        send_sems, recv_sems = refs[2 * n:]
        x, y, c = _mesh_pos()
        copies = [pltpu.make_async_remote_copy(
            src_ref=src_refs[a], dst_ref=out_refs[a], send_sem=send_sems.at[a], recv_sem=recv_sems.at[a],
            device_id=(x, y, 1 - c), device_id_type=MESH) for a in range(n)]
        for cp in copies:
            cp.start()
        for cp in copies:
            cp.wait()

    return pl.pallas_call(
        body, name=name, in_specs=[HBM] * n, out_specs=[HBM] * n,
        out_shape=[jax.ShapeDtypeStruct(s.shape, s.dtype) for s in srcs],
        scratch_shapes=[pltpu.SemaphoreType.DMA((n,)), pltpu.SemaphoreType.DMA((n,))],
    )(*srcs)


def _gather_halves(shards, small, *, name):
    n = len(shards)

    def body(*refs):
        s_refs, small_ref = refs[:n], refs[n]
        o_refs, osmall_ref = refs[n + 1:2 * n + 1], refs[2 * n + 1]
        send_sems, recv_sems, local_sems = refs[2 * n + 2:]
        x, y, c = _mesh_pos()
        me = 2 * x + y
        chips = _other_chips(x, y)
        local = [pltpu.make_async_copy(small_ref, osmall_ref.at[me], local_sems.at[0])]
        for cp in local:
            cp.start()

        def over_ici(a, k, slot):
            px, py = chips[k]
            return pltpu.make_async_remote_copy(
                src_ref=s_refs[a].at[c], dst_ref=o_refs[a].at[slot, c], send_sem=send_sems.at[3 * a + k],
                recv_sem=recv_sems.at[3 * a + k], device_id=(px, py, c), device_id_type=MESH)

        def small_copy(k, slot):
            px, py = chips[k]
            return pltpu.make_async_remote_copy(
                src_ref=small_ref, dst_ref=osmall_ref.at[slot], send_sem=send_sems.at[3 * n + k],
                recv_sem=recv_sems.at[3 * n + k], device_id=(px, py, c), device_id_type=MESH)

        def to_sibling(a, k, half):
            px, py = chips[k]
            blk = o_refs[a].at[2 * px + py, half]
            return pltpu.make_async_remote_copy(
                src_ref=blk, dst_ref=blk, send_sem=send_sems.at[3 * n + 3 + 3 * a + k],
                recv_sem=recv_sems.at[3 * n + 3 + 3 * a + k], device_id=(x, y, 1 - c), device_id_type=MESH)

        sends = [over_ici(a, k, me) for a in range(n) for k in range(3)] + [small_copy(k, me) for k in range(3)]
        for cp in sends:
            cp.start()
        passed = []
        for a in range(n):
            for k in range(3):
                px, py = chips[k]
                over_ici(a, k, 2 * px + py).wait_recv()
                passed.append(to_sibling(a, k, c))
                passed[-1].start()
        for k in range(3):
            px, py = chips[k]
            small_copy(k, 2 * px + py).wait_recv()
        for a in range(n):
            for k in range(3):
                to_sibling(a, k, 1 - c).wait_recv()
        for cp in sends + passed:
            cp.wait_send()
        for cp in local:
            cp.wait()

    nsem = 6 * n + 3
    return pl.pallas_call(
        body, name=name, in_specs=[HBM] * (n + 1), out_specs=[HBM] * (n + 1),
        out_shape=[jax.ShapeDtypeStruct((N_CHIPS,) + s.shape, s.dtype) for s in shards + [small]],
        scratch_shapes=[pltpu.SemaphoreType.DMA((nsem,)), pltpu.SemaphoreType.DMA((nsem,)),
                        pltpu.SemaphoreType.DMA((1,))],
    )(*shards, small)


def _forward_halves(landed, *, name):
    n = len(landed)

    def body(*refs):
        o_refs = refs[n:2 * n]
        send_sems, recv_sems = refs[2 * n:]
        x, y, c = _mesh_pos()
        chips = _other_chips(x, y)
        pairs = [(a, k) for a in range(n) for k in range(3)]

        def copy(a, k, half):
            px, py = chips[k]
            blk = o_refs[a].at[2 * px + py, half]
            return pltpu.make_async_remote_copy(
                src_ref=blk, dst_ref=blk, send_sem=send_sems.at[3 * a + k], recv_sem=recv_sems.at[3 * a + k],
                device_id=(x, y, 1 - c), device_id_type=MESH)

        sends = [copy(a, k, c) for a, k in pairs]
        for cp in sends:
            cp.start()
        for a, k in pairs:
            copy(a, k, 1 - c).wait_recv()
        for cp in sends:
            cp.wait_send()

    return pl.pallas_call(
        body, name=name, in_specs=[HBM] * n, out_specs=[HBM] * n,
        out_shape=[jax.ShapeDtypeStruct(a.shape, a.dtype) for a in landed],
        input_output_aliases={a: a for a in range(n)},
        scratch_shapes=[pltpu.SemaphoreType.DMA((3 * n,)), pltpu.SemaphoreType.DMA((3 * n,))],
    )(*landed)


def _swap_other_half(g_list, *, name):
    n = len(g_list)

    def body(*refs):
        g_refs, o_refs = refs[:n], refs[n:2 * n]
        send_sems, recv_sems = refs[2 * n:]
        x, y, c = _mesh_pos()
        copies = [pltpu.make_async_remote_copy(
            src_ref=g_refs[a].at[:, 1 - c], dst_ref=o_refs[a], send_sem=send_sems.at[a], recv_sem=recv_sems.at[a],
            device_id=(x, y, 1 - c), device_id_type=MESH) for a in range(n)]
        for cp in copies:
            cp.start()
        for cp in copies:
            cp.wait()

    return pl.pallas_call(
        body, name=name, in_specs=[HBM] * n, out_specs=[HBM] * n,
        out_shape=[jax.ShapeDtypeStruct((g.shape[0],) + g.shape[2:], g.dtype) for g in g_list],
        scratch_shapes=[pltpu.SemaphoreType.DMA((n,)), pltpu.SemaphoreType.DMA((n,))],
    )(*g_list)


def _row_tile(r):
    return _pick(r, (512, 256, 128, 64, 32, 16, 8))


def _add_my_half(g4, sib4, core, *, name):
    n, _, r, C = g4.shape
    tr = _row_tile(r)

    def body(core_ref, g_ref, s_ref, o_ref):
        del core_ref
        o_ref[...] = (g_ref[...].astype(F32) + s_ref[...].astype(F32)).astype(o_ref.dtype)

    return pl.pallas_call(
        body, name=name,
        grid_spec=pltpu.PrefetchScalarGridSpec(
            num_scalar_prefetch=1, grid=(n, r // tr),
            in_specs=[pl.BlockSpec((None, None, tr, C), lambda j, i, core_ref: (j, core_ref[0], i, 0)),
                      pl.BlockSpec((None, tr, C), lambda j, i, core_ref: (j, i, 0))],
            out_specs=pl.BlockSpec((None, tr, C), lambda j, i, core_ref: (j, i, 0))),
        out_shape=jax.ShapeDtypeStruct((n, r, C), g4.dtype),
        compiler_params=pltpu.CompilerParams(dimension_semantics=("parallel", "parallel")),
    )(core, g4, sib4)


def _sum_chips(landed, part, me, *, name):
    _, r, C = landed.shape
    tr = _row_tile(r)

    def body(me_ref, own_ref, r1_ref, r2_ref, r3_ref, o_ref):
        del me_ref
        f = lambda ref: ref[...].astype(F32)
        o_ref[...] = ((f(own_ref) + f(r1_ref)) + f(r2_ref)) + f(r3_ref)

    slot = lambda d: pl.BlockSpec((None, tr, C), lambda i, me_ref: ((me_ref[0] + d) % N_CHIPS, i, 0))
    return pl.pallas_call(
        body, name=name,
        grid_spec=pltpu.PrefetchScalarGridSpec(
            num_scalar_prefetch=1, grid=(r // tr,), in_specs=[slot(0), slot(1), slot(2), slot(3)],
            out_specs=pl.BlockSpec((tr, C), lambda i, me_ref: (i, 0))),
        out_shape=jax.ShapeDtypeStruct((r, C), F32),
        compiler_params=pltpu.CompilerParams(dimension_semantics=("parallel",)),
    )(me, part, landed, landed, landed)


def _adamw_halves(w, mine, theirs, m, v, core, *, layer, prev, name):
    shape = w.shape
    r, C = mine.shape
    tr = _pick(r, (128, 64, 32, 16, 8))
    per = r // tr
    view = lambda a: a.reshape(-1, C)
    n_prev = 0 if prev is None else 4

    def body(*refs):
        core_ref, w_ref, gm_ref, gt_ref, m_ref, v_ref = refs[:6]
        g_ref, d_ref, nm_ref, nv_ref = refs[6 + n_prev:]
        gv = jnp.where(pl.program_id(0) == core_ref[0], gm_ref[...], gt_ref[...])
        g_ref[...] = gv
        d_ref[...], nm_ref[...], nv_ref[...] = _adamw_math(w_ref[...], gv, m_ref[...], v_ref[...])

    half = pl.BlockSpec((tr, C), lambda h, i, core_ref: ((2 * layer + h) * per + i, 0))
    row = pl.BlockSpec((tr, C), lambda h, i, core_ref: (i, 0))
    out = jax.ShapeDtypeStruct((math.prod(shape) // C, C), F32)
    res = pl.pallas_call(
        body, name=name,
        grid_spec=pltpu.PrefetchScalarGridSpec(
            num_scalar_prefetch=1, grid=(2, per), in_specs=[half, row, row, half, half] + [HBM] * n_prev,
            out_specs=[half] * 4),
        out_shape=[out] * 4,
        input_output_aliases={6 + j: j for j in range(n_prev)},
        compiler_params=pltpu.CompilerParams(dimension_semantics=("parallel", "parallel")),
    )(core, view(w), mine, theirs, view(m), view(v), *([] if prev is None else [view(a) for a in prev]))
    return tuple(a.reshape(shape) for a in res)


def _sum_small(recv4, *, name):
    _, R, C = recv4.shape

    def body(r_ref, o_ref):
        o_ref[...] = ((r_ref[0] + r_ref[1]) + r_ref[2]) + r_ref[3]

    return pl.pallas_call(body, name=name, out_shape=jax.ShapeDtypeStruct((R, C), F32))(recv4)


def _add(a, b, *, name):
    R, C = a.shape
    tr = _pick(R, (512, 256, 128, 64, 32, 16, 8))
    blk = pl.BlockSpec((tr, C), lambda i: (i, 0))

    def body(a_ref, b_ref, o_ref):
        o_ref[...] = a_ref[...] + b_ref[...]

    return pl.pallas_call(body, name=name, grid=(R // tr,), in_specs=[blk, blk], out_specs=blk,
                          out_shape=jax.ShapeDtypeStruct((R, C), F32),
                          compiler_params=pltpu.CompilerParams(dimension_semantics=("parallel",)))(a, b)


def _adamw_math(w, g, m, v):
    nm = ADAM_B1 * m + (1.0 - ADAM_B1) * g
    nv = ADAM_B2 * v + (1.0 - ADAM_B2) * (g * g)
    m_hat = nm / (1.0 - ADAM_B1 ** ADAM_STEP)
    v_hat = nv / (1.0 - ADAM_B2 ** ADAM_STEP)
    return -ADAM_LR * (m_hat / (jnp.sqrt(v_hat) + ADAM_EPS) + ADAM_WD * w), nm, nv


def _adamw(w, g, m, v, *, name):
    shape = w.shape
    C = shape[-1]
    R = w.size // C
    two = lambda a: a.reshape(R, C)
    tr = _pick(R, (256, 128, 64, 32, 16, 8)) if R % 8 == 0 and R > 8 else R
    blk = pl.BlockSpec((tr, C), lambda i: (i, 0))

    def body(w_ref, g_ref, m_ref, v_ref, d_ref, nm_ref, nv_ref):
        d_ref[...], nm_ref[...], nv_ref[...] = _adamw_math(w_ref[...], g_ref[...], m_ref[...], v_ref[...])

    out = jax.ShapeDtypeStruct((R, C), F32)
    d, nm, nv = pl.pallas_call(
        body, name=name, grid=(R // tr,), in_specs=[blk] * 4, out_specs=[blk] * 3, out_shape=[out] * 3,
        compiler_params=pltpu.CompilerParams(dimension_semantics=("parallel",)),
    )(two(w), two(g), two(m), two(v))
    return d.reshape(shape), nm.reshape(shape), nv.reshape(shape)


BIG = (("dn_w_in", (2, 1024, 1540), 2), ("dn_w_out", (2, 512, 1024), 1), ("sb_w_in", (1, 1024, 1024), 2),
       ("sb_w_out", (1, 256, 1024), 1), ("sc_w_in", (1, 1024, 2048), 2), ("sc_w_out", (1, 512, 1024), 1))
SMALL = (("dn_conv_w", (2, 4, 1024), 2), ("dn_o_norm_g", (2, 64), 1), ("sc_conv_w", (1, 3, 512), 2))
REPL = (("norm_g", (4, 1024)), ("dn_a_log", (2, 8)), ("dn_dt_bias", (2, 8)), ("sb_q_norm_g", (1, 64)),
        ("sb_k_norm_g", (1, 64)))


def _halves(shard):
    return shard.reshape(2, -1, shard.shape[-1])


def _pack(arrays, cols, lead=()):
    flat = jnp.concatenate([a.reshape(lead + (-1,)) for a in arrays], axis=-1)
    n = flat.shape[-1]
    rows = -(-n // cols)
    unit = 512 if rows > 512 else 8
    rows = -(-rows // unit) * unit
    flat = jnp.pad(flat, [(0, 0)] * len(lead) + [(0, rows * cols - n)])
    return flat.reshape(lead + (rows, cols))


def _unpack(buf, table, lead=()):
    flat = buf.reshape(lead + (-1,))
    out, off = {}, 0
    for entry in table:
        name, shape = entry[0], entry[1]
        n = math.prod(shape)
        out[name] = flat[..., off:off + n].reshape(lead + shape)
        off += n
    return out


def _join(shards, axis):
    return jnp.concatenate([shards[j] for j in range(N_CHIPS)], axis=axis)


def _split(full, axis):
    return jnp.stack(jnp.split(full, N_CHIPS, axis=axis), axis=0)


def kernel(x, norm_g, dn_w_in, dn_conv_w, dn_a_log, dn_dt_bias, dn_o_norm_g, dn_w_out, sb_w_in, sb_q_norm_g, sb_k_norm_g, sb_w_out, sc_w_in, sc_conv_w, sc_w_out, loss_target, m_norm_g, m_dn_w_in, m_dn_conv_w, m_dn_a_log, m_dn_dt_bias, m_dn_o_norm_g, m_dn_w_out, m_sb_w_in, m_sb_q_norm_g, m_sb_k_norm_g, m_sb_w_out, m_sc_w_in, m_sc_conv_w, m_sc_w_out, v_norm_g, v_dn_w_in, v_dn_conv_w, v_dn_a_log, v_dn_dt_bias, v_dn_o_norm_g, v_dn_w_out, v_sb_w_in, v_sb_q_norm_g, v_sb_k_norm_g, v_sb_w_out, v_sc_w_in, v_sc_conv_w, v_sc_w_out):
    weights = dict(norm_g=norm_g, dn_w_in=dn_w_in, dn_conv_w=dn_conv_w, dn_a_log=dn_a_log, dn_dt_bias=dn_dt_bias,
                   dn_o_norm_g=dn_o_norm_g, dn_w_out=dn_w_out, sb_w_in=sb_w_in, sb_q_norm_g=sb_q_norm_g,
                   sb_k_norm_g=sb_k_norm_g, sb_w_out=sb_w_out, sc_w_in=sc_w_in, sc_conv_w=sc_conv_w, sc_w_out=sc_w_out)
    m_in = dict(norm_g=m_norm_g, dn_w_in=m_dn_w_in, dn_conv_w=m_dn_conv_w, dn_a_log=m_dn_a_log,
                dn_dt_bias=m_dn_dt_bias, dn_o_norm_g=m_dn_o_norm_g, dn_w_out=m_dn_w_out, sb_w_in=m_sb_w_in,
                sb_q_norm_g=m_sb_q_norm_g, sb_k_norm_g=m_sb_k_norm_g, sb_w_out=m_sb_w_out, sc_w_in=m_sc_w_in,
                sc_conv_w=m_sc_conv_w, sc_w_out=m_sc_w_out)
    v_in = dict(norm_g=v_norm_g, dn_w_in=v_dn_w_in, dn_conv_w=v_dn_conv_w, dn_a_log=v_dn_a_log,
                dn_dt_bias=v_dn_dt_bias, dn_o_norm_g=v_dn_o_norm_g, dn_w_out=v_dn_w_out, sb_w_in=v_sb_w_in,
                sb_q_norm_g=v_sb_q_norm_g, sb_k_norm_g=v_sb_k_norm_g, sb_w_out=v_sb_w_out, sc_w_in=v_sc_w_in,
                sc_conv_w=v_sc_conv_w, sc_w_out=v_sc_w_out)
    order = list(weights)
    xi, yi, ci = _mesh_pos()

    small = _pack([weights[n] for n, _, _ in SMALL], LANES)
    later = [("dn_w_in", 1), ("dn_w_out", 1), ("sb_w_in", 0), ("sb_w_out", 0), ("sc_w_in", 0), ("sc_w_out", 0)]
    piece = lambda n, l: _halves(weights[n][l].astype(BF16)[None])
    own_first = [piece("dn_w_in", 0), piece("dn_w_out", 0)]
    own_later = [piece(n, l) for n, l in later]
    own_last, own_mid = own_later[:2], own_later[2:]
    me = 2 * xi + yi
    whole = lambda g4, own: lax.dynamic_update_index_in_dim(g4, own, me, 0)
    flat = lambda g4: g4.reshape(N_CHIPS, -1, g4.shape[-1])
    rows_of = lambda w4: w4.reshape(-1, w4.shape[-1])
    dn_in = lambda w4: jnp.pad(_join(w4, 1), ((0, 0), (0, DN_IN_PAD - DN_IN)))
    w_in0, w_out0, small4 = _gather_halves(own_first, small, name="gather_first")
    full = {n: _join(a, ax) for (n, _, ax), a in zip(SMALL, _unpack(small4, SMALL, (N_CHIPS,)).values())}

    def dn_args(j, w_in4, w_out4):
        return (dn_in(flat(w_in4)), full["dn_conv_w"][j], dn_a_log[j], dn_dt_bias[j], full["dn_o_norm_g"][j],
                rows_of(w_out4))

    x0 = x[0]
    dn0 = dn_args(0, whole(w_in0, own_first[0]), whole(w_out0, own_first[1]))
    x1, s0, landed = _dn_layer_fwd(x0, norm_g[0], *dn0, "l0", send=own_mid)
    landed = _forward_halves(landed, name="forward_halves_mid")
    sb_in, sb_out, sc_in, sc_out = [whole(g4, own) for g4, own in zip(landed, own_mid)]
    sb_args = (flat(sb_in), sb_q_norm_g[0], sb_k_norm_g[0], rows_of(sb_out))
    sc_args = (flat(sc_in), full["sc_conv_w"][0], rows_of(sc_out))
    x2, s1, landed = _sb_layer_fwd(x1, norm_g[1], *sb_args, "l1", send=own_last)
    landed = _forward_halves(landed, name="forward_halves_last")
    dn1 = dn_args(1, *[whole(g4, own) for g4, own in zip(landed, own_last)])
    x3, s2 = _sc_layer_fwd(x2, norm_g[2], *sc_args, "l2")
    x4, s3, _ = _dn_layer_fwd(x3, norm_g[3], *dn1, "l3")
    dy, loss_local = _loss_head(x4, loss_target[0], name="loss_head")
    loss = lax.psum(loss_local[0, 0], ("x", "y", "c"))

    core = ci.astype(jnp.int32).reshape(1)
    chip = me.astype(jnp.int32).reshape(1)

    def chip_sums(g_list, tag):
        sib = _swap_other_half(g_list, name=f"swap_halves_{tag}")
        return [_add_my_half(g, s, core, name=f"sum_cores_{tag}{i}") for i, (g, s) in enumerate(zip(g_list, sib))]

    dx3, dng3, dwin3, dconv3, dal3, ddt3, dgain3, dwout3, _ = _dn_layer_bwd(dy, x3, norm_g[3], *dn1, s3, "l3")
    dx2, dng2, dwin2, dconv2, dwout2 = _sc_layer_bwd(dx3, x2, norm_g[2], *sc_args, s2, "l2")
    dx1, dng1, dwin1, dgq, dgk, dwout1 = _sb_layer_bwd(dx2, x1, norm_g[1], *sb_args, s1, "l1")
    part_later = chip_sums([_cut2(_by_cols(dwin3)), _cut2(_by_rows(dwout3)), _cut2(dwin1), _cut2(_by_rows(dwout1)),
                            _cut2(dwin2), _cut2(_by_rows(dwout2))], "later")
    dx0, dng0, part_win0, dconv0, dal0, ddt0, dgain0, part_wout0, landed = _dn_layer_bwd(
        dx1, x0, norm_g[0], *dn0, s0, "l0", send=part_later, chip_sums=chip_sums)
    pieces = later + [("dn_w_out", 0), ("dn_w_in", 0)]
    mine = {(n, l): _sum_chips(r, p, chip, name=f"sum_chips_{n}{l}")
            for (n, l), r, p in zip(pieces, landed, part_later + [part_wout0, part_win0])}
    pieces = sorted(pieces, key=lambda nl: nl[1])
    mine = [mine[nl] for nl in pieces]
    theirs = _sibling_exchange(mine, name="swap_results")
    upd = {}
    for (n, l), a, b in zip(pieces, mine, theirs):
        upd[n] = _adamw_halves(weights[n], a, b, m_in[n], v_in[n], core, layer=l, prev=upd.get(n),
                               name=f"adamw_{n}{l}")
    g_out = {n: upd[n][0] for n, _, _ in BIG}

    grads = dict(
        norm_g=jnp.concatenate([dng0, dng1, dng2, dng3], axis=0), dn_conv_w=jnp.stack([dconv0, dconv3]),
        dn_a_log=jnp.stack([dal0, dal3]), dn_dt_bias=jnp.stack([ddt0, ddt3]),
        dn_o_norm_g=jnp.stack([dgain0, dgain3]), sb_q_norm_g=dgq[None], sb_k_norm_g=dgk[None],
        sc_conv_w=dconv2[None])
    repl = [jnp.broadcast_to(grads[n][None], (N_CHIPS,) + s) for n, s in REPL]
    gsmall = _pack([_split(grads[n], ax) for n, _, ax in SMALL] + repl, LANES, (N_CHIPS,))
    rsmall, = _chip_exchange([gsmall], send_slot_is_dest=True, copy_own=(True,), name="scatter_small")
    psmall = _sum_small(rsmall, name="sum_chips_small")
    qsmall, = _sibling_exchange([psmall], name="swap_cores_small")
    tsmall = _add(psmall, qsmall, name="sum_cores_small")
    g_out.update(_unpack(tsmall, SMALL + REPL))

    for n in order:
        if n not in upd:
            upd[n] = (g_out[n],) + _adamw(weights[n], g_out[n], m_in[n], v_in[n], name=f"adamw_{n}")
    return (loss, dx0[None], *[upd[n][0] for n in order], *[upd[n][1] for n in order],
            *[upd[n][2] for n in order], *[upd[n][3] for n in order])
```

```python
import functools
import math

import jax
import jax.numpy as jnp
from jax import lax
from jax.experimental import pallas as pl
from jax.experimental.pallas import tpu as pltpu

F32 = jnp.float32
BF16 = jnp.bfloat16
MESH = pl.DeviceIdType.MESH

RMS_EPS = 1e-6
L2_EPS = 1e-6
LANES = 128
VMEM_BIG = 60 * 1024 * 1024
MM_VMEM = 44 * 1024 * 1024

DN_HEADS, DN_DK, DN_DV, DN_CHUNK, DN_CONV = 8, 128, 256, 64, 4
DN_QK_W = DN_HEADS * DN_DK
DN_V_W = DN_HEADS * DN_DV
DN_CONV_W = 2 * DN_QK_W + DN_V_W
DN_IN = DN_CONV_W + DN_V_W + 2 * DN_HEADS
DN_IN_PAD = DN_CONV_W + DN_V_W + LANES
SB_DH = 64
SC_CONV = 3

ADAM_LR, ADAM_B1, ADAM_B2, ADAM_EPS, ADAM_WD, ADAM_STEP = 0.001, 0.9, 0.999, 1e-08, 0.01, 10


def _pick(n, cands):
    for c in cands:
        if n % c == 0:
            return c
    raise ValueError(f"no tile for {n} in {cands}")


def _bf(x):
    return x.astype(BF16)


def _dot(a, b):
    return jnp.dot(_bf(a), _bf(b), preferred_element_type=F32)


def _dot_nt(a, b):
    return lax.dot_general(_bf(a), _bf(b), (((1,), (1,)), ((), ())), preferred_element_type=F32)


def _dot_tn(a, b):
    return lax.dot_general(_bf(a), _bf(b), (((0,), (0,)), ((), ())), preferred_element_type=F32)


def _split3(a):
    hi = _bf(a)
    r = a - hi.astype(F32)
    mid = _bf(r)
    lo = _bf(r - mid.astype(F32))
    return hi, mid, lo


def _sigmoid(x):
    return 1.0 / (1.0 + jnp.exp(-x))


def _silu(x):
    return x * _sigmoid(x)


def _dsilu(x):
    s = _sigmoid(x)
    return s * (1.0 + x * (1.0 - s))


def _softplus(x):
    return jnp.maximum(x, 0.0) + jnp.log(1.0 + jnp.exp(-jnp.abs(x)))


def _shift_down(z, k):
    if k == 0:
        return z
    row = lax.broadcasted_iota(jnp.int32, z.shape, 0)
    return jnp.where(row >= k, pltpu.roll(z, k, 0), 0.0)


def _shift_up(z, k):
    if k == 0:
        return z
    n = z.shape[0]
    row = lax.broadcasted_iota(jnp.int32, z.shape, 0)
    return jnp.where(row < n - k, pltpu.roll(z, n - k, 0), 0.0)


def _matmul(a, b, *, mode, name, res=None, a_parts=1, b_parts=1, out_parts=1, out_dtype=F32, send=()):
    def dims2(x, parts):
        if parts == 1:
            return x.shape
        assert x.shape[0] == parts
        return (x.shape[1], x.shape[2] * parts)

    ash, bsh = dims2(a, a_parts), dims2(b, b_parts)
    if mode == "nn":
        (M, K), (K2, N) = ash, bsh
        dn = (((1,), (0,)), ((), ()))
    elif mode == "nt":
        (M, K), (N, K2) = ash, bsh
        dn = (((1,), (1,)), ((), ()))
    else:
        (K, M), (K2, N) = ash, bsh
        dn = (((0,), (0,)), ((), ()))
    assert K == K2, (ash, bsh, mode)
    tm_max = _pick(M, (512, 256, 128, 64, 32, 16, 8))
    n_unit = N // max(out_parts, b_parts if mode != "nt" else 1)
    k_unit = K // max(a_parts if mode != "tn" else 1, b_parts if mode == "nt" else 1)
    tm, tn, tk = min(
        ((m, n, k) for m in {tm_max, max(tm_max // 2, 8)}
         for n in (2048, 1792, 1024, 896, 768, 512, 384, 256, 128) if n_unit % n == 0
         for k in (k_unit, 2048, 1792, 1024, 896, 512, 256, 128) if k_unit % k == 0
         if 2 * (m * k * a.dtype.itemsize + k * n * b.dtype.itemsize + 2 * m * n * 4) + m * n * 4 <= MM_VMEM),
        key=lambda t: (-t[0] * t[1] * t[2], -t[0], -t[2]))
    nk = K // tk
    grid = (M // tm, N // tn, nk)

    def spec(parts, rows_are, cols_are, tr, tc, width):
        per = width // parts // tc
        if parts == 1:
            return pl.BlockSpec((tr, tc), lambda i, j, k: ((i, j, k)[rows_are], (i, j, k)[cols_are]))
        return pl.BlockSpec((None, tr, tc), lambda i, j, k: ((i, j, k)[cols_are] // per, (i, j, k)[rows_are],
                                                             (i, j, k)[cols_are] % per))

    if mode == "nn":
        a_spec = spec(a_parts, 0, 2, tm, tk, K)
        b_spec = spec(b_parts, 2, 1, tk, tn, N)
    elif mode == "nt":
        a_spec = spec(a_parts, 0, 2, tm, tk, K)
        b_spec = spec(b_parts, 1, 2, tn, tk, K)
    else:
        a_spec = spec(a_parts, 2, 0, tk, tm, M)
        b_spec = spec(b_parts, 2, 1, tk, tn, N)
    o_spec = spec(out_parts, 0, 1, tm, tn, N)
    in_specs = [a_spec, b_spec]
    operands = [a, b]
    if res is not None:
        in_specs.append(pl.BlockSpec((tm, tn), lambda i, j, k: (i, j)))
        operands.append(res)

    n_in = len(operands)
    ns = len(send)

    def finish(refs, r):
        if res is not None:
            r = refs[2][...] + r
        refs[n_in + ns][...] = r.astype(out_dtype)

    def body(*refs):
        if ns:
            at = lambda step: functools.reduce(jnp.logical_and, [pl.program_id(d) == step[d] for d in range(3)])
            _blocks_over_ici(refs[n_in:n_in + ns], refs[n_in + ns + 1:n_in + 2 * ns + 1], refs[-2], refs[-1],
                             at((0, 0, 0)), at(tuple(g - 1 for g in grid)))
        part = lax.dot_general(_bf(refs[0][...]), _bf(refs[1][...]), dn, preferred_element_type=F32)
        if nk == 1:
            finish(refs, part)
            return
        acc_ref = refs[n_in + 2 * ns + 1]
        k = pl.program_id(2)

        @pl.when(k == 0)
        def _():
            acc_ref[...] = part

        @pl.when(jnp.logical_and(k > 0, k < nk - 1))
        def _():
            acc_ref[...] += part

        @pl.when(k == nk - 1)
        def _():
            finish(refs, acc_ref[...] + part)

    out_shape = (M, N) if out_parts == 1 else (out_parts, M, N // out_parts)
    out = pl.pallas_call(
        body, name=name, grid=grid, in_specs=in_specs + [HBM] * ns, out_specs=[o_spec] + [HBM] * ns,
        out_shape=[jax.ShapeDtypeStruct(out_shape, out_dtype)] + [jax.ShapeDtypeStruct(x.shape, x.dtype) for x in send],
        scratch_shapes=([pltpu.VMEM((tm, tn), F32)] if nk > 1 else [])
        + ([pltpu.SemaphoreType.DMA((3 * ns,)), pltpu.SemaphoreType.DMA((3 * ns,))] if ns else []),
        compiler_params=pltpu.CompilerParams(
            dimension_semantics=("arbitrary",) * 3 if ns else ("parallel", "parallel", "arbitrary"),
            vmem_limit_bytes=VMEM_BIG),
    )(*operands, *send)
    return out if ns else out[0]


def _rmsnorm_fwd(x, g, *, name):
    T, D = x.shape
    tm = _pick(T, (512, 256, 128, 64, 32, 16))

    def body(x_ref, g_ref, h_ref):
        xv = x_ref[...]
        r = lax.rsqrt(jnp.mean(xv * xv, axis=-1, keepdims=True) + RMS_EPS)
        h_ref[...] = ((xv * r) * g_ref[...]).astype(BF16)

    return pl.pallas_call(
        body, name=name, grid=(T // tm,),
        in_specs=[pl.BlockSpec((tm, D), lambda i: (i, 0)), pl.BlockSpec((1, D), lambda i: (0, 0))],
        out_specs=pl.BlockSpec((tm, D), lambda i: (i, 0)),
        out_shape=jax.ShapeDtypeStruct((T, D), BF16),
    )(x, g.reshape(1, D))


def _rmsnorm_bwd(x, g, dh, dx_in, *, name):
    T, D = x.shape
    tm = _pick(T, (512, 256, 128, 64, 32, 16))

    def body(x_ref, g_ref, dh_ref, dxin_ref, dx_ref, dg_ref):
        @pl.when(pl.program_id(0) == 0)
        def _():
            dg_ref[...] = jnp.zeros_like(dg_ref)

        xv = x_ref[...]
        r = lax.rsqrt(jnp.mean(xv * xv, axis=-1, keepdims=True) + RMS_EPS)
        xh = xv * r
        dh_v = dh_ref[...]
        dxh = dh_v * g_ref[...]
        dx_ref[...] = dxin_ref[...] + r * (dxh - xh * jnp.mean(dxh * xh, axis=-1, keepdims=True))
        dg_ref[...] += jnp.sum(dh_v * xh, axis=0, keepdims=True)

    row = pl.BlockSpec((tm, D), lambda i: (i, 0))
    vec = pl.BlockSpec((1, D), lambda i: (0, 0))
    return pl.pallas_call(
        body, name=name, grid=(T // tm,),
        in_specs=[row, vec, row, row], out_specs=[row, vec],
        out_shape=[jax.ShapeDtypeStruct((T, D), F32), jax.ShapeDtypeStruct((1, D), F32)],
        compiler_params=pltpu.CompilerParams(dimension_semantics=("arbitrary",)),
    )(x, g.reshape(1, D), dh, dx_in)


def _loss_head(y, target, *, name):
    T, D = y.shape
    tm = _pick(T, (512, 256, 128, 64, 32, 16))

    def body(y_ref, t_ref, dy_ref, l_ref):
        @pl.when(pl.program_id(0) == 0)
        def _():
            l_ref[...] = jnp.zeros_like(l_ref)

        err = y_ref[...] - t_ref[...]
        dy_ref[...] = err * (1.0 / D)
        l_ref[...] += 0.5 * jnp.sum(jnp.mean(err * err, axis=-1, keepdims=True), axis=0, keepdims=True)

    row = pl.BlockSpec((tm, D), lambda i: (i, 0))
    return pl.pallas_call(
        body, name=name, grid=(T // tm,),
        in_specs=[row, row], out_specs=[row, pl.BlockSpec((1, 1), lambda i: (0, 0))],
        out_shape=[jax.ShapeDtypeStruct((T, D), F32), jax.ShapeDtypeStruct((1, 1), F32)],
        compiler_params=pltpu.CompilerParams(dimension_semantics=("arbitrary",)),
    )(y, target)


def _sc_mid_fwd(p3, conv_w, *, name):
    _, T, W = p3.shape
    K = conv_w.shape[0]
    cw = LANES

    def body(p_ref, w_ref, o_ref):
        z = p_ref[1] * p_ref[2]
        cv = sum(w_ref[i:i + 1, :] * _shift_down(z, K - 1 - i) for i in range(K))
        o_ref[...] = ((p_ref[0] * cv) * _silu(p_ref[3])).astype(BF16)

    return pl.pallas_call(
        body, name=name, grid=(W // cw,),
        in_specs=[pl.BlockSpec((4, T, cw), lambda j: (0, 0, j)), pl.BlockSpec((K, cw), lambda j: (0, j))],
        out_specs=pl.BlockSpec((T, cw), lambda j: (0, j)),
        out_shape=jax.ShapeDtypeStruct((T, W), BF16),
        compiler_params=pltpu.CompilerParams(dimension_semantics=("parallel",), vmem_limit_bytes=VMEM_BIG),
    )(p3, conv_w)


def _sc_mid_bwd(p3, conv_w, do, *, name):
    _, T, W = p3.shape
    K = conv_w.shape[0]
    cw = LANES

    def body(p_ref, w_ref, do_ref, dp_ref, dw_ref):
        b, c, u, gate = p_ref[0], p_ref[1], p_ref[2], p_ref[3]
        z = c * u
        zs = [_shift_down(z, K - 1 - i) for i in range(K)]
        cv = sum(w_ref[i:i + 1, :] * zs[i] for i in range(K))
        y = b * cv
        dov = do_ref[...]
        dy = dov * _silu(gate)
        dp_ref[3] = dov * y * _dsilu(gate)
        dp_ref[0] = dy * cv
        dcv = dy * b
        dz = sum(w_ref[i:i + 1, :] * _shift_up(dcv, K - 1 - i) for i in range(K))
        dp_ref[1] = dz * u
        dp_ref[2] = dz * c
        for i in range(K):
            dw_ref[i:i + 1, :] = jnp.sum(dcv * zs[i], axis=0, keepdims=True)

    return pl.pallas_call(
        body, name=name, grid=(W // cw,),
        in_specs=[pl.BlockSpec((4, T, cw), lambda j: (0, 0, j)), pl.BlockSpec((K, cw), lambda j: (0, j)),
                  pl.BlockSpec((T, cw), lambda j: (0, j))],
        out_specs=[pl.BlockSpec((4, T, cw), lambda j: (0, 0, j)), pl.BlockSpec((K, cw), lambda j: (0, j))],
        out_shape=[jax.ShapeDtypeStruct((4, T, W), F32), jax.ShapeDtypeStruct((K, W), F32)],
        compiler_params=pltpu.CompilerParams(dimension_semantics=("parallel",), vmem_limit_bytes=VMEM_BIG),
    )(p3, conv_w, do)


def _sc_layer_fwd(x, ng, w_in, conv_w, w_out, tag):
    h = _rmsnorm_fwd(x, ng, name=f"{tag}_norm")
    p3 = _matmul(h, w_in, mode="nn", b_parts=4, out_parts=4, name=f"{tag}_inproj")
    og = _sc_mid_fwd(p3, conv_w, name=f"{tag}_mid")
    x_new = _matmul(og, w_out, mode="nn", res=x, name=f"{tag}_outproj")
    return x_new, (h, p3, og)


def _sc_layer_bwd(dx, x, ng, w_in, conv_w, w_out, saved, tag):
    h, p3, og = saved
    d_wout = _matmul(og, dx, mode="tn", out_dtype=BF16, name=f"{tag}_dwout")
    dog = _matmul(dx, w_out, mode="nt", name=f"{tag}_dog")
    dp3, dconv = _sc_mid_bwd(p3, conv_w, dog, name=f"{tag}_midbwd")
    d_win = _matmul(h, dp3, mode="tn", b_parts=4, out_parts=4, out_dtype=BF16, name=f"{tag}_dwin")
    dh = _matmul(dp3, w_in, mode="nt", a_parts=4, b_parts=4, name=f"{tag}_dh")
    dx_prev, dng = _rmsnorm_bwd(x, ng, dh, dx, name=f"{tag}_normbwd")
    return dx_prev, dng, d_win, dconv, d_wout


SB_BQ = 256
SB_BK = 256
SB_ROWS = 512
SB_DEAD = -110.0


def _sb_half_mask():
    return lax.broadcasted_iota(jnp.int32, (1, LANES), 1) < SB_DH


def _sb_headnorm(x, g, lo):
    x2 = x * x
    s_lo = jnp.sum(jnp.where(lo, x2, 0.0), axis=-1, keepdims=True)
    s_hi = jnp.sum(jnp.where(lo, 0.0, x2), axis=-1, keepdims=True)
    r = lax.rsqrt(jnp.where(lo, s_lo, s_hi) * (1.0 / SB_DH) + RMS_EPS)
    xh = x * r
    return xh * g, xh, r


def _dot_x2_l(a_l, b_exact_bf16):
    his = [_bf(a) for a in a_l]
    mids = [_bf(a - h.astype(F32)) for a, h in zip(a_l, his)]
    f = lambda p: jnp.dot(p, b_exact_bf16, preferred_element_type=F32)
    return [x + y for x, y in zip([f(h) for h in his], [f(m) for m in mids])]


def _sb_stack(xb, lo):
    zero = jnp.zeros_like(xb)
    return jnp.concatenate([jnp.where(lo, xb, zero), jnp.where(lo, zero, xb)], axis=0)


def _sb_rel(bq, bk):
    row = lax.broadcasted_iota(jnp.int32, (2 * bq, bk), 0)
    col = lax.broadcasted_iota(jnp.int32, (2 * bq, bk), 1)
    return col - jnp.where(row >= bq, row - bq, row)


def _sb_tile(qm, kb, valid):
    z = lax.dot_general(qm, kb, (((1,), (1,)), ((), ())), preferred_element_type=F32)
    sp = _softplus(z)
    return z - sp, (-sp if valid is None else jnp.where(valid, -sp, 0.0))


def _sb_attn_fwd(p3, gq2, gk2, *, name, send=()):
    _, T, W = p3.shape
    bq, bk = min(SB_BQ, T), min(SB_BK, T)
    rows = min(SB_ROWS, T)
    scale = SB_DH ** -0.5
    ns = len(send)
    npair = W // LANES

    def body(*refs):
        p_ref, gq_ref, gk_ref = refs[:3]
        og_ref, o_ref, ls_ref, cnt_ref = refs[3 + ns:7 + ns]
        qn_ref, kn_ref, v_ref = refs[7 + 2 * ns:10 + 2 * ns]
        if ns:
            _halves_over_ici(refs[3:3 + ns], refs[7 + ns:7 + 2 * ns], refs[10 + 2 * ns], refs[11 + 2 * ns],
                             pl.program_id(0) == 0, pl.program_id(0) == npair - 1)
        lo = _sb_half_mask()

        def prologue(i, c):
            r0 = pl.multiple_of(i * rows, rows)
            sl = pl.ds(r0, rows)
            qn_ref[sl, :] = (_sb_headnorm(p_ref[0, sl, :], gq_ref[...], lo)[0] * scale).astype(BF16)
            kn_ref[sl, :] = _sb_headnorm(p_ref[1, sl, :], gk_ref[...], lo)[0].astype(BF16)
            v_ref[sl, :] = p_ref[2, sl, :].astype(BF16)
            return c

        lax.fori_loop(0, T // rows, prologue, 0)

        rel = _sb_rel(bq, bk)
        tri = (lax.broadcasted_iota(jnp.int32, (bk, bk), 0)
               > lax.broadcasted_iota(jnp.int32, (bk, bk), 1)).astype(BF16)

        def qblock(qi, c):
            q0 = pl.multiple_of(qi * bq, bq)
            qm = _sb_stack(qn_ref[pl.ds(q0, bq), :], lo)
            nkb = (q0 + bq - 1) // bk + 1

            def tiles(k0s, carry, valids):
                o_acc, a_carry = carry
                sc = [_sb_tile(qm, kn_ref[pl.ds(k0, bk), :], valid) for k0, valid in zip(k0s, valids)]
                later = _dot_x2_l([log1m for _, log1m in sc], tri)
                for (logsig, log1m), lat, k0, valid in zip(sc, later, k0s, valids):
                    wts = jnp.exp(logsig + (lat + a_carry))
                    if valid is not None:
                        wts = jnp.where(valid, wts, 0.0)
                    o_acc = o_acc + jnp.dot(_bf(wts), v_ref[pl.ds(k0, bk), :], preferred_element_type=F32)
                    a_carry = a_carry + jnp.sum(log1m, axis=-1, keepdims=True)
                return o_acc, a_carry

            blk0 = lambda j: pl.multiple_of(j * bk, bk)
            k_last = blk0(nkb - 1)
            o2, t2 = tiles([k_last, blk0(jnp.maximum(nkb - 2, 0))],
                           (jnp.zeros((2 * bq, LANES), F32), jnp.zeros((2 * bq, 1), F32)),
                           [rel < q0 - k_last, nkb >= 2])

            def alive(st):
                return jnp.logical_and(st[0] < nkb - 1, jnp.max(st[2]) > SB_DEAD)

            def back_one(st):
                return (st[0] + 1,) + tiles([blk0(nkb - 2 - st[0])], st[1:], [None])

            n_back, o2, t2 = lax.while_loop(alive, back_one, (jnp.int32(1), o2, t2))
            o = jnp.where(lo, o2[:bq], o2[bq:])
            o_ref[pl.ds(q0, bq), :] = o
            ls_ref[pl.ds(q0, bq), :] = jnp.where(lo, t2[:bq], t2[bq:])
            cnt_ref[qi] = jnp.full((8, LANES), jnp.minimum(n_back + 1, nkb).astype(F32))
            og_ref[pl.ds(q0, bq), :] = (o * _silu(p_ref[3, pl.ds(q0, bq), :])).astype(BF16)
            return c

        lax.fori_loop(0, T // bq, qblock, 0)

    colblk = pl.BlockSpec((T, LANES), lambda j: (0, j))
    vec = pl.BlockSpec((1, LANES), lambda j: (0, 0))
    return pl.pallas_call(
        body, name=name, grid=(npair,),
        in_specs=[pl.BlockSpec((4, T, LANES), lambda j: (0, 0, j)), vec, vec] + [HBM] * ns,
        out_specs=[colblk, colblk, colblk, pl.BlockSpec((None, T // bq, 8, LANES), lambda j: (j, 0, 0, 0))]
        + [HBM] * ns,
        out_shape=[jax.ShapeDtypeStruct((T, W), BF16), jax.ShapeDtypeStruct((T, W), F32),
                   jax.ShapeDtypeStruct((T, W), F32), jax.ShapeDtypeStruct((npair, T // bq, 8, LANES), F32)]
        + [jax.ShapeDtypeStruct((N_CHIPS,) + a.shape, a.dtype) for a in send],
        scratch_shapes=[pltpu.VMEM((T, LANES), BF16)] * 3
        + ([pltpu.SemaphoreType.DMA((3 * ns,)), pltpu.SemaphoreType.DMA((3 * ns,))] if ns else []),
        compiler_params=pltpu.CompilerParams(dimension_semantics=("arbitrary",), vmem_limit_bytes=VMEM_BIG),
    )(p3, gq2, gk2, *send)


def _sb_attn_bwd(p3, gq2, gk2, o, lsum, live, dog, *, name):
    _, T, W = p3.shape
    bq, bk = min(SB_BQ, T), min(SB_BK, T)
    rows = min(SB_ROWS, T)
    scale = SB_DH ** -0.5

    def body(p_ref, gq_ref, gk_ref, o_ref, ls_ref, cnt_ref, dog_ref, dp_ref, dgq_ref, dgk_ref,
             qn_ref, kn_ref, v_ref, do_ref):
        lo = _sb_half_mask()

        def prologue(i, c):
            r0 = pl.multiple_of(i * rows, rows)
            sl = pl.ds(r0, rows)
            qn_ref[sl, :] = (_sb_headnorm(p_ref[0, sl, :], gq_ref[...], lo)[0] * scale).astype(BF16)
            kn_ref[sl, :] = _sb_headnorm(p_ref[1, sl, :], gk_ref[...], lo)[0].astype(BF16)
            v_ref[sl, :] = p_ref[2, sl, :].astype(BF16)
            gate = p_ref[3, sl, :]
            dogv = dog_ref[sl, :]
            dp_ref[3, sl, :] = dogv * o_ref[sl, :] * _dsilu(gate)
            do_ref[sl, :] = (dogv * _silu(gate)).astype(BF16)
            zero = jnp.zeros((rows, LANES), F32)
            dp_ref[0, sl, :] = zero
            dp_ref[1, sl, :] = zero
            dp_ref[2, sl, :] = zero
            return c

        lax.fori_loop(0, T // rows, prologue, 0)

        rel = _sb_rel(bq, bk)
        rj = lax.broadcasted_iota(jnp.int32, (bk, bk), 0)
        cj = lax.broadcasted_iota(jnp.int32, (bk, bk), 1)
        upto = (rj <= cj).astype(BF16)
        before_m = (rj < cj).astype(BF16)

        def qblock(qi, c):
            q0 = pl.multiple_of(qi * bq, bq)
            qm = _sb_stack(qn_ref[pl.ds(q0, bq), :], lo)
            dom = _sb_stack(do_ref[pl.ds(q0, bq), :], lo)
            nkb = (q0 + bq - 1) // bk + 1
            blk0 = lambda j: pl.multiple_of(j * bk, bk)
            k_last = blk0(nkb - 1)

            lsb = ls_ref[pl.ds(q0, bq), :]
            total = jnp.concatenate([lsb[:, 0:1], lsb[:, SB_DH:SB_DH + 1]], axis=0)
            n_live = jnp.clip(jnp.max(cnt_ref[qi]).astype(jnp.int32), 1, nkb)
            k_first = nkb - n_live

            def tiles(k0s, carry, valids):
                dq_acc, a_pre, r_pre = carry
                kss = [pl.ds(k0, bk) for k0 in k0s]
                kbs = [kn_ref[ks, :] for ks in kss]
                sc = [_sb_tile(qm, kb, valid) for kb, valid in zip(kbs, valids)]
                dws = [lax.dot_general(dom, v_ref[ks, :], _NT, preferred_element_type=F32) for ks in kss]
                upto_l = _dot_x2_l([log1m for _, log1m in sc], upto)
                wts_l = []
                for (logsig, log1m), up, valid in zip(sc, upto_l, valids):
                    wts = jnp.exp(logsig + ((total - a_pre) - up))
                    wts_l.append(wts if valid is None else jnp.where(valid, wts, 0.0))
                    a_pre = a_pre + jnp.sum(log1m, axis=-1, keepdims=True)
                ee_l = [dw * wts for dw, wts in zip(dws, wts_l)]
                before_l = _dot_x2_l(ee_l, before_m)
                for (logsig, _), ks, kb, wts, ee, bef, valid in zip(sc, kss, kbs, wts_l, ee_l, before_l, valids):
                    beta = jnp.exp(logsig)
                    dz = ee * (1.0 - beta) - beta * (r_pre + bef)
                    if valid is not None:
                        dz = jnp.where(valid, dz, 0.0)
                    dzb = _bf(dz)
                    dq_acc = dq_acc + jnp.dot(dzb, kb, preferred_element_type=F32)
                    dp_ref[1, ks, :] += lax.dot_general(dzb, qm, _TN, preferred_element_type=F32)
                    dp_ref[2, ks, :] += lax.dot_general(_bf(wts), dom, _TN, preferred_element_type=F32)
                    r_pre = r_pre + jnp.sum(ee, axis=-1, keepdims=True)
                return dq_acc, a_pre, r_pre

            cr = (jnp.zeros((2 * bq, LANES), F32), jnp.zeros((2 * bq, 1), F32), jnp.zeros((2 * bq, 1), F32))
            n_before = jnp.maximum(n_live - 2, 0)
            cr = lax.fori_loop(0, n_before % 2, lambda t, cr: tiles([blk0(k_first)], cr, [None]), cr)
            k_pairs = k_first + n_before % 2
            cr = lax.fori_loop(0, n_before // 2,
                               lambda t, cr: tiles([blk0(k_pairs + 2 * t), blk0(k_pairs + 2 * t + 1)], cr,
                                                   [None, None]), cr)
            dq2, _, _ = tiles([blk0(jnp.maximum(nkb - 2, 0)), k_last], cr, [n_live >= 2, rel < q0 - k_last])
            dp_ref[0, pl.ds(q0, bq), :] = jnp.where(lo, dq2[:bq], dq2[bq:]) * scale
            return c

        lax.fori_loop(0, T // bq, qblock, 0)

        dgq_ref[...] = jnp.zeros_like(dgq_ref)
        dgk_ref[...] = jnp.zeros_like(dgk_ref)

        def epilogue(i, c):
            r0 = pl.multiple_of(i * rows, rows)
            sl = pl.ds(r0, rows)
            for part, g_ref, dg_ref in ((0, gq_ref, dgq_ref), (1, gk_ref, dgk_ref)):
                _, xh, r = _sb_headnorm(p_ref[part, sl, :], g_ref[...], lo)
                dn = dp_ref[part, sl, :]
                dxh = dn * g_ref[...]
                prod = dxh * xh
                m_lo = jnp.sum(jnp.where(lo, prod, 0.0), axis=-1, keepdims=True)
                m_hi = jnp.sum(jnp.where(lo, 0.0, prod), axis=-1, keepdims=True)
                m = jnp.where(lo, m_lo, m_hi) * (1.0 / SB_DH)
                dp_ref[part, sl, :] = r * (dxh - xh * m)
                dg_ref[...] += jnp.sum(dn * xh, axis=0, keepdims=True)
            return c

        lax.fori_loop(0, T // rows, epilogue, 0)

    colblk = pl.BlockSpec((T, LANES), lambda j: (0, j))
    vec = pl.BlockSpec((1, LANES), lambda j: (0, 0))
    part = pl.BlockSpec((4, T, LANES), lambda j: (0, 0, j))
    gvec = pl.BlockSpec((None, 1, LANES), lambda j: (j, 0, 0))
    npair = W // LANES
    return pl.pallas_call(
        body, name=name, grid=(npair,),
        in_specs=[part, vec, vec, colblk, colblk, pl.BlockSpec((None, T // bq, 8, LANES), lambda j: (j, 0, 0, 0)),
                  colblk],
        out_specs=[part, gvec, gvec],
        out_shape=[jax.ShapeDtypeStruct((4, T, W), F32), jax.ShapeDtypeStruct((npair, 1, LANES), F32),
                   jax.ShapeDtypeStruct((npair, 1, LANES), F32)],
        scratch_shapes=[pltpu.VMEM((T, LANES), BF16)] * 4,
        compiler_params=pltpu.CompilerParams(dimension_semantics=("parallel",), vmem_limit_bytes=VMEM_BIG),
    )(p3, gq2, gk2, o, lsum, live, dog)


_NN = (((1,), (0,)), ((), ()))
_NT = (((1,), (1,)), ((), ()))
_TN = (((0,), (0,)), ((), ()))
DN_TB = 512
DN_HEADS_FWD = 4
DN_HEADS_BWD = 2
DN_INV_EXACT_LEVELS = 2
DN_AB_COL = (DN_CONV_W + DN_V_W) // LANES


def _dn_conv(x, w_ref):
    k = w_ref.shape[0]
    return sum(w_ref[i:i + 1, :] * _shift_down(x, k - 1 - i) for i in range(k))


def _dn_prep_fwd(p, conv_w, *, name):
    T = p.shape[0]
    cw = conv_w.shape[1]
    n_qk = 2 * DN_QK_W // LANES

    def body(p_ref, w_ref, o_ref):
        s = _silu(_dn_conv(p_ref[...], w_ref))
        r = lax.rsqrt(jnp.sum(s * s, axis=-1, keepdims=True) + L2_EPS)
        o_ref[...] = jnp.where(pl.program_id(0) < n_qk, s * r, s)

    colblk = pl.BlockSpec((T, LANES), lambda j: (0, j))
    return pl.pallas_call(
        body, name=name, grid=(cw // LANES,),
        in_specs=[colblk, pl.BlockSpec((DN_CONV, LANES), lambda j: (0, j))],
        out_specs=colblk, out_shape=jax.ShapeDtypeStruct((T, cw), F32),
        compiler_params=pltpu.CompilerParams(dimension_semantics=("parallel",), vmem_limit_bytes=VMEM_BIG),
    )(p, conv_w)


def _dn_chunk_tri(rows, upper):
    r = lax.broadcasted_iota(jnp.int32, (rows, rows), 0)
    c = lax.broadcasted_iota(jnp.int32, (rows, rows), 1)
    same = (r // DN_CHUNK) == (c // DN_CHUNK)
    return jnp.logical_and(same, (c >= r) if upper else (c <= r)).astype(BF16)


def _dn_lane_rows(a_log, dt_bias):
    pad = lambda v: jnp.zeros((1, LANES), F32).at[0, :DN_HEADS].set(v)
    return pad(a_log), pad(dt_bias)


def _dn_ab_parts(blk, alog_row, dtb_row):
    lane = lax.broadcasted_iota(jnp.int32, (1, LANES), 1)
    is_a = lane < DN_HEADS
    is_b = jnp.logical_and(lane >= DN_HEADS, lane < 2 * DN_HEADS)
    a_arg = jnp.where(is_a, blk + dtb_row, 0.0)
    neg_exp = jnp.where(is_a, -jnp.exp(alog_row), 0.0)
    log_a = neg_exp * _softplus(a_arg)
    beta = jnp.where(is_b, _sigmoid(blk), 0.0)
    return is_a, is_b, a_arg, neg_exp, log_a, beta


def _dn_ab_fwd(p, alog_row, dtb_row, *, name):
    T = p.shape[0]
    rows = min(DN_TB, T)

    def body(p_ref, al_ref, dt_ref, o_ref):
        _, _, _, _, log_a, beta = _dn_ab_parts(p_ref[...], al_ref[...], dt_ref[...])
        hi, mid, lo_ = _split3(log_a)
        tri = _dn_chunk_tri(rows, upper=False)
        f = lambda q: jnp.dot(tri, q, preferred_element_type=F32)
        o_ref[...] = (f(hi) + f(mid) + f(lo_)) + beta

    blk = pl.BlockSpec((rows, LANES), lambda i: (i, DN_AB_COL))
    vec = pl.BlockSpec((1, LANES), lambda i: (0, 0))
    return pl.pallas_call(
        body, name=name, grid=(T // rows,), in_specs=[blk, vec, vec],
        out_specs=pl.BlockSpec((rows, LANES), lambda i: (i, 0)),
        out_shape=jax.ShapeDtypeStruct((T, LANES), F32),
        compiler_params=pltpu.CompilerParams(dimension_semantics=("parallel",)),
    )(p, alog_row, dtb_row)


def _hp_l(a_l, b_l, dims=_NN):
    sa = [_split3(a)[:2] for a in a_l]
    sb = [_split3(b)[:2] for b in b_l]
    f = lambda p, q: lax.dot_general(p, q, dims, preferred_element_type=F32)
    hh = [f(x[0], y[0]) for x, y in zip(sa, sb)]
    hm = [f(x[0], y[1]) for x, y in zip(sa, sb)]
    mh = [f(x[1], y[0]) for x, y in zip(sa, sb)]
    return [a + (b + c) for a, b, c in zip(hh, hm, mh)]


def _dn_local(qs, k, v, g, beta, nc):
    c = DN_CHUNK
    cut = lambda x: [x[i * c:(i + 1) * c] for i in range(nc)]
    row = lax.broadcasted_iota(jnp.int32, (c, c), 0)
    col = lax.broadcasted_iota(jnp.int32, (c, c), 1)
    eye, lower, strict = row == col, row >= col, row > col
    rowid = lax.broadcasted_iota(jnp.int32, (c, 1), 0)
    eg = jnp.exp(g)
    kb = k * beta
    rhs_k = kb * eg
    g_l, k_l, kb_l, qs_l = cut(g), cut(k), cut(kb), cut(qs)
    g_row_l = [jnp.sum(jnp.where(eye, x, 0.0), axis=0, keepdims=True) for x in g_l]
    dec_l = [jnp.where(lower, jnp.exp(jnp.where(lower, x - y, 0.0)), 0.0) for x, y in zip(g_l, g_row_l)]
    kk_l = [_dot_nt(a, b) for a, b in zip(kb_l, k_l)]
    qk_l = [_dot_nt(a, b) for a, b in zip(qs_l, k_l)]
    low_l = [jnp.where(strict, a * d, 0.0) for a, d in zip(kk_l, dec_l)]
    eye_f = eye.astype(F32)
    pw_l = [-x for x in low_l]
    inv_l = [eye_f + x for x in pw_l]
    plain = lambda a_l, b_l: [_dot(a, b) for a, b in zip(a_l, b_l)]
    for level in range(int(math.log2(c)) - 1):
        mul = _hp_l if level < DN_INV_EXACT_LEVELS else plain
        pw_l = mul(pw_l, pw_l)
        inv_l = [a + b for a, b in zip(inv_l, mul(inv_l, pw_l))]
    u_l = [_dot(a, b) for a, b in zip(inv_l, cut(v * beta))]
    w_l = [_dot(a, b) for a, b in zip(inv_l, cut(rhs_k))]
    aqk_l = [jnp.where(lower, a * d, 0.0) for a, d in zip(qk_l, dec_l)]
    g_last_l = [jnp.sum(jnp.where(rowid == c - 1, x, 0.0), axis=0, keepdims=True) for x in g_l]
    ekd_l = [jnp.exp(a - b) for a, b in zip(g_last_l, g_l)]
    kd_l = [a * b for a, b in zip(k_l, ekd_l)]
    qd_l = cut(qs * eg)
    kw_l = [_dot_tn(a, b) for a, b in zip(kd_l, w_l)]
    qp_l = [q - _dot(a, w) for q, a, w in zip(qd_l, aqk_l, w_l)]
    return dict(eye=eye, lower=lower, strict=strict, dec=dec_l, k=k_l, kb=kb_l, qs=qs_l, low=low_l, inv=inv_l,
                eg=cut(eg), rhs_k=cut(rhs_k), u=u_l, w=w_l, aqk=aqk_l, g_last=g_last_l, qd=qd_l,
                ekd=ekd_l, kd=kd_l, kw=kw_l, qp=qp_l)


def _dn_head_cols(gb_blk, head):
    lane = lax.broadcasted_iota(jnp.int32, (1, LANES), 1)
    g = jnp.sum(jnp.where(lane == head, gb_blk, 0.0), axis=-1, keepdims=True)
    beta = jnp.sum(jnp.where(lane == head + DN_HEADS, gb_blk, 0.0), axis=-1, keepdims=True)
    return g, beta


def _halves_over_ici(s_refs, o_refs, send_sems, recv_sems, first, last):
    x, y, c = _mesh_pos()
    me = 2 * x + y
    chips = _other_chips(x, y)
    pairs = [(a, k) for a in range(len(s_refs)) for k in range(3)]

    def copy(a, k, slot):
        px, py = chips[k]
        return pltpu.make_async_remote_copy(
            src_ref=s_refs[a].at[c], dst_ref=o_refs[a].at[slot, c], send_sem=send_sems.at[3 * a + k],
            recv_sem=recv_sems.at[3 * a + k], device_id=(px, py, c), device_id_type=MESH)

    @pl.when(first)
    def _():
        for a, k in pairs:
            copy(a, k, me).start()

    @pl.when(last)
    def _():
        for a, k in pairs:
            px, py = chips[k]
            copy(a, k, 2 * px + py).wait_recv()
        for a, k in pairs:
            copy(a, k, me).wait_send()


def _dn_delta_fwd(qkv, gb, p, o_gain, *, name, send=()):
    T = qkv.shape[0]
    tb = min(DN_TB, T)
    nb, nc = T // tb, tb // DN_CHUNK
    H = DN_HEADS
    qscale = DN_DK ** -0.5
    ns = len(send)
    hp = DN_HEADS_FWD

    def body(*refs):
        q_ref, k_ref, v_ref, gb_ref, gate_ref, gain_ref = refs[:6]
        o_ref, og_ref, st_ref = refs[6 + ns:9 + ns]
        s_ref = refs[9 + 2 * ns]
        pair, blk = pl.program_id(0), pl.program_id(1)
        if ns:
            _halves_over_ici(refs[6:6 + ns], refs[9 + ns:9 + 2 * ns], refs[10 + 2 * ns], refs[11 + 2 * ns],
                             jnp.logical_and(pair == 0, blk == 0),
                             jnp.logical_and(pair == H // hp - 1, blk == nb - 1))

        @pl.when(blk == 0)
        def _():
            s_ref[...] = jnp.zeros_like(s_ref)

        gbv = gb_ref[...]
        ts, ku, op = [], [], []
        for e in range(hp):
            qk_e, v_e = slice(e * DN_DK, (e + 1) * DN_DK), slice(e * DN_DV, (e + 1) * DN_DV)
            g, beta = _dn_head_cols(gbv, hp * pair + e)
            t = _dn_local(q_ref[:, qk_e] * qscale, k_ref[:, qk_e], v_ref[:, v_e], g, beta, nc)
            ts.append(t)
            ku.append([_dot_tn(a, b) for a, b in zip(t["kd"], t["u"])])
            op.append([_dot(a, b) for a, b in zip(t["aqk"], t["u"])])
        s32 = [s_ref[e] for e in range(hp)]
        s_l = [[] for _ in range(hp)]
        for i in range(nc):
            sb = [_bf(x) for x in s32]
            for e in range(hp):
                st_ref[e, i] = sb[e]
                s_l[e].append(sb[e])
            prod = [_dot(ts[e]["kw"][i], sb[e]) for e in range(hp)]
            s32 = [s32[e] * jnp.exp(ts[e]["g_last"][i]) - prod[e] + ku[e][i] for e in range(hp)]
        for e in range(hp):
            s_ref[e] = s32[e]
        o = jnp.concatenate(
            [jnp.concatenate([_dot(qp, sb) + x for qp, sb, x in zip(ts[e]["qp"], s_l[e], op[e])], axis=0)
             for e in range(hp)], axis=1)
        o_ref[...] = o
        gain = gain_ref[...]
        for e in range(hp):
            v_e = slice(e * DN_DV, (e + 1) * DN_DV)
            oe = o[:, v_e]
            r = lax.rsqrt(jnp.mean(oe * oe, axis=-1, keepdims=True) + RMS_EPS)
            og_ref[:, v_e] = (((oe * r) * gain) * _silu(gate_ref[:, v_e])).astype(BF16)

    qk = lambda col0: pl.BlockSpec((tb, hp * DN_DK), lambda h, i: (i, col0 // (hp * DN_DK) + h))
    vblk = lambda col0: pl.BlockSpec((tb, hp * DN_DV), lambda h, i: (i, col0 // (hp * DN_DV) + h))
    return pl.pallas_call(
        body, name=name, grid=(H // hp, nb),
        in_specs=[qk(0), qk(DN_QK_W), vblk(2 * DN_QK_W), pl.BlockSpec((tb, LANES), lambda h, i: (i, 0)),
                  vblk(DN_CONV_W), pl.BlockSpec((1, DN_DV), lambda h, i: (0, 0))] + [HBM] * ns,
        out_specs=[vblk(0), vblk(0), pl.BlockSpec((hp, nc, DN_DK, DN_DV), lambda h, i: (h, i, 0, 0))] + [HBM] * ns,
        out_shape=[jax.ShapeDtypeStruct((T, DN_V_W), F32), jax.ShapeDtypeStruct((T, DN_V_W), BF16),
                   jax.ShapeDtypeStruct((H, T // DN_CHUNK, DN_DK, DN_DV), BF16)]
        + [jax.ShapeDtypeStruct((N_CHIPS,) + a.shape, a.dtype) for a in send],
        scratch_shapes=[pltpu.VMEM((hp, DN_DK, DN_DV), F32)]
        + ([pltpu.SemaphoreType.DMA((3 * ns,)), pltpu.SemaphoreType.DMA((3 * ns,))] if ns else []),
        compiler_params=pltpu.CompilerParams(dimension_semantics=("arbitrary", "arbitrary")),
    )(qkv, qkv, qkv, gb, p, o_gain, *send)


def _blocks_over_ici(p_refs, o_refs, send_sems, recv_sems, first, last):
    x, y, c = _mesh_pos()
    me = 2 * x + y
    chips = _other_chips(x, y)
    pairs = [(a, k) for a in range(len(p_refs)) for k in range(3)]

    def copy(a, k, slot):
        px, py = chips[k]
        return pltpu.make_async_remote_copy(
            src_ref=p_refs[a].at[2 * px + py], dst_ref=o_refs[a].at[slot], send_sem=send_sems.at[3 * a + k],
            recv_sem=recv_sems.at[3 * a + k], device_id=(px, py, c), device_id_type=MESH)

    @pl.when(first)
    def _():
        for a, k in pairs:
            copy(a, k, me).start()

    @pl.when(last)
    def _():
        for a, k in pairs:
            px, py = chips[k]
            copy(a, k, 2 * px + py).wait_recv()
        for a, k in pairs:
            copy(a, k, me).wait_send()


def _dn_delta_bwd(qkv, gb, p, o_gain, o, states, dog, *, name, send=()):
    T = qkv.shape[0]
    tb = min(DN_TB, T)
    nb, nc = T // tb, tb // DN_CHUNK
    H = DN_HEADS
    qscale = DN_DK ** -0.5
    ns = len(send)
    hp = DN_HEADS_BWD

    def body(*refs):
        q_ref, k_ref, v_ref, gb_ref, gate_ref, gain_ref, o_ref, st_ref, dog_ref = refs[:9]
        dq_ref, dk_ref, dv_ref, dgate_ref, dgb_ref, dgain_ref = refs[9 + ns:15 + ns]
        ds_ref = refs[15 + 2 * ns]
        pair, blk = pl.program_id(0), pl.program_id(1)
        first = jnp.logical_and(pair == 0, blk == 0)
        if ns:
            _blocks_over_ici(refs[9:9 + ns], refs[15 + ns:15 + 2 * ns], refs[16 + 2 * ns], refs[17 + 2 * ns],
                             first, jnp.logical_and(pair == H // hp - 1, blk == nb - 1))

        @pl.when(blk == 0)
        def _():
            ds_ref[...] = jnp.zeros_like(ds_ref)

        @pl.when(first)
        def _():
            dgain_ref[...] = jnp.zeros_like(dgain_ref)

        lane = lax.broadcasted_iota(jnp.int32, (1, LANES), 1)
        c = DN_CHUNK
        cut = lambda x: [x[i * c:(i + 1) * c] for i in range(nc)]
        cat = lambda xs: jnp.concatenate(xs, axis=0)
        rsum = lambda x: jnp.sum(x, axis=-1, keepdims=True)
        gbv, gain = gb_ref[...], gain_ref[...]

        def before_chain(e):
            qk_e, v_e = slice(e * DN_DK, (e + 1) * DN_DK), slice(e * DN_DV, (e + 1) * DN_DV)
            g, beta = _dn_head_cols(gbv, hp * pair + e)
            ov, gate, dogv = o_ref[:, v_e], gate_ref[:, v_e], dog_ref[:, v_e]
            r = lax.rsqrt(jnp.mean(ov * ov, axis=-1, keepdims=True) + RMS_EPS)
            oh = ov * r
            dnrm = dogv * _silu(gate)
            dgate_ref[:, v_e] = dogv * (oh * gain) * _dsilu(gate)
            doh = dnrm * gain
            do_l = cut(r * (doh - oh * jnp.mean(doh * oh, axis=-1, keepdims=True)))
            dgain_ref[...] += jnp.sum(dnrm * oh, axis=0, keepdims=True)
            k, v = k_ref[:, qk_e], v_ref[:, v_e]
            t = _dn_local(q_ref[:, qk_e] * qscale, k, v, g, beta, nc)
            s_l = [st_ref[e, i] for i in range(nc)]
            vn_l = [u - _dot(w, sb) for u, w, sb in zip(t["u"], t["w"], s_l)]
            return dict(
                t=t, beta=beta, v=v, s=s_l, vn=vn_l, egl=[jnp.exp(x) for x in t["g_last"]],
                dqd=[_dot_nt(a, sb) for a, sb in zip(do_l, s_l)], daqk=[_dot_nt(a, b) for a, b in zip(do_l, vn_l)],
                aqk_do=[_dot_tn(a, b) for a, b in zip(t["aqk"], do_l)],
                qp_do=[_dot_tn(a, b) for a, b in zip(t["qp"], do_l)])

        hs = [before_chain(e) for e in range(hp)]
        ds = [ds_ref[e] for e in range(hp)]
        ds_l = [[None] * nc for _ in range(hp)]
        for i in reversed(range(nc)):
            for e in range(hp):
                ds_l[e][i] = ds[e]
            prod = [_dot_tn(hs[e]["t"]["kw"][i], ds[e]) for e in range(hp)]
            ds = [ds[e] * hs[e]["egl"][i] - prod[e] + hs[e]["qp_do"][i] for e in range(hp)]
        for e in range(hp):
            ds_ref[e] = ds[e]

        def after_chain(e):
            hd, t = hs[e], hs[e]["t"]
            lower, strict, eye = t["lower"], t["strict"], t["eye"]
            s_l, vn_l, dqd_l, daqk_l, egl_l, beta, v = (hd["s"], hd["vn"], hd["dqd"], hd["daqk"], hd["egl"],
                                                         hd["beta"], hd["v"])
            dvn_l = [a + _dot(kd, d) for a, kd, d in zip(hd["aqk_do"], t["kd"], ds_l[e])]
            dkd_l = [_dot_nt(a, d) for a, d in zip(vn_l, ds_l[e])]
            dgl_l = [jnp.sum(rsum(d * sb.astype(F32)), axis=0, keepdims=True) * x
                     for d, sb, x in zip(ds_l[e], s_l, egl_l)]
            dw_l = [-_dot_nt(a, sb) for a, sb in zip(dvn_l, s_l)]
            dbv_l = [_dot_tn(a, b) for a, b in zip(t["inv"], dvn_l)]
            dbk_l = [_dot_tn(a, b) for a, b in zip(t["inv"], dw_l)]
            dlow_l = [-(_dot_nt(a, b) + _dot_nt(x, y)) for a, b, x, y in zip(dbv_l, t["u"], dbk_l, t["w"])]
            m_l = [jnp.where(strict, a * d, 0.0) for a, d in zip(dlow_l, t["dec"])]
            nmat_l = [jnp.where(lower, a * d, 0.0) for a, d in zip(daqk_l, t["dec"])]
            dkb_l = [_dot(m, kk) + b * x for m, kk, b, x in zip(m_l, t["k"], dbk_l, t["eg"])]
            dqs_l = [_dot(n, kk) + a * x for n, kk, a, x in zip(nmat_l, t["k"], dqd_l, t["eg"])]
            dk1_l = [_dot_tn(m, kb) for m, kb in zip(m_l, t["kb"])]
            dk2_l = [_dot_tn(n, q) for n, q in zip(nmat_l, t["qs"])]
            beta_l, v_l = cut(beta), cut(v)
            rowid = lax.broadcasted_iota(jnp.int32, (c, 1), 0)
            dk_l, dg_l, dbeta_l = [], [], []
            for i in range(nc):
                dk_l.append(dk1_l[i] + dk2_l[i] + dkd_l[i] * t["ekd"][i] + dkb_l[i] * beta_l[i])
                gmat = jnp.where(strict, dlow_l[i] * t["low"][i], 0.0) + daqk_l[i] * t["aqk"][i]
                s_kd = rsum(dkd_l[i] * t["kd"][i])
                dg = (rsum(gmat) + rsum(dqd_l[i] * t["qd"][i]) - s_kd + rsum(dbk_l[i] * t["rhs_k"][i]))
                dg_row = -jnp.sum(gmat, axis=0, keepdims=True)
                dg = dg + rsum(jnp.where(eye, dg_row, 0.0))
                dgl = dgl_l[i] + jnp.sum(s_kd, axis=0, keepdims=True)
                dg_l.append(dg + jnp.where(rowid == c - 1, dgl, 0.0))
                dbeta_l.append(rsum(dbv_l[i] * v_l[i]) + rsum(dkb_l[i] * t["k"][i]))
            head = hp * pair + e
            dgb = (jnp.where(lane == head, cat(dg_l), 0.0) + jnp.where(lane == head + DN_HEADS, cat(dbeta_l), 0.0))
            return cat(dqs_l) * qscale, cat(dk_l), cat(dbv_l) * beta, dgb

        for e in range(hp):
            dq, dk, dv, dgb = after_chain(e)
            dq_ref[:, e * DN_DK:(e + 1) * DN_DK] = dq
            dk_ref[:, e * DN_DK:(e + 1) * DN_DK] = dk
            dv_ref[:, e * DN_DV:(e + 1) * DN_DV] = dv
            dgb_ref[e] = dgb

    rev = lambda i: nb - 1 - i
    qk = lambda col0: pl.BlockSpec((tb, hp * DN_DK), lambda h, i: (rev(i), col0 // (hp * DN_DK) + h))
    vblk = lambda col0: pl.BlockSpec((tb, hp * DN_DV), lambda h, i: (rev(i), col0 // (hp * DN_DV) + h))
    gain_spec = pl.BlockSpec((1, DN_DV), lambda h, i: (0, 0))
    return pl.pallas_call(
        body, name=name, grid=(H // hp, nb),
        in_specs=[qk(0), qk(DN_QK_W), vblk(2 * DN_QK_W), pl.BlockSpec((tb, LANES), lambda h, i: (rev(i), 0)),
                  vblk(DN_CONV_W), gain_spec, vblk(0),
                  pl.BlockSpec((hp, nc, DN_DK, DN_DV), lambda h, i: (h, rev(i), 0, 0)), vblk(0)] + [HBM] * ns,
        out_specs=[qk(0), qk(0), vblk(0), vblk(DN_CONV_W),
                   pl.BlockSpec((hp, tb, LANES), lambda h, i: (h, rev(i), 0)), gain_spec] + [HBM] * ns,
        out_shape=[jax.ShapeDtypeStruct((T, DN_QK_W), F32), jax.ShapeDtypeStruct((T, DN_QK_W), F32),
                   jax.ShapeDtypeStruct((T, DN_V_W), F32), jax.ShapeDtypeStruct((T, DN_IN_PAD), F32),
                   jax.ShapeDtypeStruct((H, T, LANES), F32), jax.ShapeDtypeStruct((1, DN_DV), F32)]
        + [jax.ShapeDtypeStruct(a.shape, a.dtype) for a in send],
        scratch_shapes=[pltpu.VMEM((hp, DN_DK, DN_DV), F32)]
        + ([pltpu.SemaphoreType.DMA((3 * ns,)), pltpu.SemaphoreType.DMA((3 * ns,))] if ns else []),
        compiler_params=pltpu.CompilerParams(dimension_semantics=("arbitrary", "arbitrary")),
    )(qkv, qkv, qkv, gb, p, o_gain, o, states, dog, *send)


def _dn_conv_bwd(p, conv_w, d, dp, *, first, normed, name):
    T, width = d.shape

    def body(p_ref, w_ref, d_ref, dp_in, dp_ref, dw_ref):
        del dp_in
        x = p_ref[...]
        ksz = w_ref.shape[0]
        xs = [_shift_down(x, ksz - 1 - i) for i in range(ksz)]
        xc = sum(w_ref[i:i + 1, :] * xs[i] for i in range(ksz))
        ds = d_ref[...]
        if normed:
            s = _silu(xc)
            r = lax.rsqrt(jnp.sum(s * s, axis=-1, keepdims=True) + L2_EPS)
            y = s * r
            ds = r * (ds - y * jnp.sum(ds * y, axis=-1, keepdims=True))
        dxc = ds * _dsilu(xc)
        dp_ref[...] = sum(w_ref[i:i + 1, :] * _shift_up(dxc, ksz - 1 - i) for i in range(ksz))
        for i in range(ksz):
            dw_ref[i:i + 1, :] = jnp.sum(dxc * xs[i], axis=0, keepdims=True)

    shifted = pl.BlockSpec((T, LANES), lambda j: (0, first + j))
    return pl.pallas_call(
        body, name=name, grid=(width // LANES,),
        in_specs=[shifted, pl.BlockSpec((DN_CONV, LANES), lambda j: (0, first + j)),
                  pl.BlockSpec((T, LANES), lambda j: (0, j)), pl.BlockSpec(memory_space=pl.ANY)],
        out_specs=[shifted, pl.BlockSpec((DN_CONV, LANES), lambda j: (0, j))],
        out_shape=[jax.ShapeDtypeStruct(dp.shape, F32), jax.ShapeDtypeStruct((DN_CONV, width), F32)],
        input_output_aliases={3: 0},
        compiler_params=pltpu.CompilerParams(dimension_semantics=("parallel",), vmem_limit_bytes=VMEM_BIG),
    )(p, conv_w, d, dp)


def _dn_ab_bwd(p, alog_row, dtb_row, dgb, dp, *, name):
    T = p.shape[0]
    rows = min(DN_TB, T)
    H = DN_HEADS

    def body(p_ref, al_ref, dt_ref, dgb_ref, dp_in, dp_ref, dal_ref, ddt_ref):
        del dp_in

        @pl.when(pl.program_id(0) == 0)
        def _():
            dal_ref[...] = jnp.zeros_like(dal_ref)
            ddt_ref[...] = jnp.zeros_like(ddt_ref)

        blk = p_ref[...]
        is_a, is_b, a_arg, neg_exp, log_a, beta = _dn_ab_parts(blk, al_ref[...], dt_ref[...])
        d = dgb_ref[0]
        for hh in range(1, H):
            d = d + dgb_ref[hh]
        hi, mid, lo_ = _split3(jnp.where(is_a, d, 0.0))
        tri = _dn_chunk_tri(rows, upper=True)
        f = lambda q: jnp.dot(tri, q, preferred_element_type=F32)
        dlog_a = f(hi) + f(mid) + f(lo_)
        da_in = dlog_a * neg_exp * _sigmoid(a_arg)
        db_in = jnp.where(is_b, d, 0.0) * beta * (1.0 - beta)
        dp_ref[...] = jnp.where(is_a, da_in, 0.0) + db_in
        dal_ref[...] += jnp.sum(dlog_a * log_a, axis=0, keepdims=True)
        ddt_ref[...] += jnp.sum(jnp.where(is_a, da_in, 0.0), axis=0, keepdims=True)

    blk = pl.BlockSpec((rows, LANES), lambda i: (i, DN_AB_COL))
    vec = pl.BlockSpec((1, LANES), lambda i: (0, 0))
    return pl.pallas_call(
        body, name=name, grid=(T // rows,),
        in_specs=[blk, vec, vec, pl.BlockSpec((H, rows, LANES), lambda i: (0, i, 0)),
                  pl.BlockSpec(memory_space=pl.ANY)],
        out_specs=[blk, vec, vec],
        out_shape=[jax.ShapeDtypeStruct(dp.shape, F32), jax.ShapeDtypeStruct((1, LANES), F32),
                   jax.ShapeDtypeStruct((1, LANES), F32)],
        input_output_aliases={4: 0},
        compiler_params=pltpu.CompilerParams(dimension_semantics=("arbitrary",)),
    )(p, alog_row, dtb_row, dgb, dp)


def _dn_layer_fwd(x, ng, w_in, conv_w, a_log, dt_bias, o_gain, w_out, tag, send=()):
    alog_row, dtb_row = _dn_lane_rows(a_log, dt_bias)
    gain = o_gain.reshape(1, DN_DV)
    h = _rmsnorm_fwd(x, ng, name=f"{tag}_norm")
    p = _matmul(h, w_in, mode="nn", name=f"{tag}_inproj")
    qkv = _dn_prep_fwd(p, conv_w, name=f"{tag}_prep")
    gb = _dn_ab_fwd(p, alog_row, dtb_row, name=f"{tag}_ab")
    o, og, states, *landed = _dn_delta_fwd(qkv, gb, p, gain, name=f"{tag}_delta", send=send)
    x_new = _matmul(og, w_out, mode="nn", res=x, name=f"{tag}_outproj")
    return x_new, (h, p, qkv, gb, o, og, states), landed


def _dn_layer_bwd(dx, x, ng, w_in, conv_w, a_log, dt_bias, o_gain, w_out, saved, tag, send=(), chip_sums=None):
    h, p, qkv, gb, o, og, states = saved
    alog_row, dtb_row = _dn_lane_rows(a_log, dt_bias)
    gain = o_gain.reshape(1, DN_DV)
    d_wout = _matmul(og, dx, mode="tn", out_dtype=BF16, name=f"{tag}_dwout")
    if chip_sums is not None:
        d_wout, = chip_sums([_cut2(_by_rows(d_wout))], f"{tag}wout")
        send = list(send) + [d_wout]
    dog = _matmul(dx, w_out, mode="nt", name=f"{tag}_dog")
    dq, dk, dv, dp, dgb, dgain, *landed = _dn_delta_bwd(qkv, gb, p, gain, o, states, dog, name=f"{tag}_deltabwd",
                                                        send=send)
    n_qk = DN_QK_W // LANES
    dp, dconv_q = _dn_conv_bwd(p, conv_w, dq, dp, first=0, normed=True, name=f"{tag}_convbwd_q")
    dp, dconv_k = _dn_conv_bwd(p, conv_w, dk, dp, first=n_qk, normed=True, name=f"{tag}_convbwd_k")
    dp, dconv_v = _dn_conv_bwd(p, conv_w, dv, dp, first=2 * n_qk, normed=False, name=f"{tag}_convbwd_v")
    dconv = jnp.concatenate([dconv_q, dconv_k, dconv_v], axis=1)
    dp, dal, ddt = _dn_ab_bwd(p, alog_row, dtb_row, dgb, dp, name=f"{tag}_abbwd")
    d_win = _matmul(h, dp, mode="tn", name=f"{tag}_dwin")
    if chip_sums is not None:
        d_win, = chip_sums([_cut2(_by_cols(d_win))], f"{tag}win")
        dh, landed_win = _matmul(dp, w_in, mode="nt", name=f"{tag}_dh", send=[d_win])
        landed = landed + [landed_win]
    else:
        dh = _matmul(dp, w_in, mode="nt", name=f"{tag}_dh")
    dx_prev, dng = _rmsnorm_bwd(x, ng, dh, dx, name=f"{tag}_normbwd")
    return dx_prev, dng, d_win, dconv, dal[0, :DN_HEADS], ddt[0, :DN_HEADS], dgain[0], d_wout, landed


def _by_cols(dw):
    return _split(dw[:, :DN_IN].astype(BF16), 1)


def _by_rows(dw):
    return dw.reshape(N_CHIPS, -1, dw.shape[-1])


def _cut2(g4):
    return g4.reshape(N_CHIPS, 2, -1, g4.shape[-1])


def _sb_gains(g):
    return jnp.concatenate([g, g]).reshape(1, LANES)


def _sb_layer_fwd(x, ng, w_in, gq, gk, w_out, tag, send=()):
    h = _rmsnorm_fwd(x, ng, name=f"{tag}_norm")
    p3 = _matmul(h, w_in, mode="nn", b_parts=4, out_parts=4, name=f"{tag}_inproj")
    og, o, lsum, live, *landed = _sb_attn_fwd(p3, _sb_gains(gq), _sb_gains(gk), name=f"{tag}_attn", send=send)
    x_new = _matmul(og, w_out, mode="nn", res=x, name=f"{tag}_outproj")
    return x_new, (h, p3, og, o, lsum, live), landed


def _sb_layer_bwd(dx, x, ng, w_in, gq, gk, w_out, saved, tag):
    h, p3, og, o, lsum, live = saved
    d_wout = _matmul(og, dx, mode="tn", out_dtype=BF16, name=f"{tag}_dwout")
    dog = _matmul(dx, w_out, mode="nt", name=f"{tag}_dog")
    dp3, dgq, dgk = _sb_attn_bwd(p3, _sb_gains(gq), _sb_gains(gk), o, lsum, live, dog, name=f"{tag}_attnbwd")
    fold = lambda d: jnp.sum(d.reshape(-1, SB_DH), axis=0)
    d_win = _matmul(h, dp3, mode="tn", b_parts=4, out_parts=4, out_dtype=BF16, name=f"{tag}_dwin")
    dh = _matmul(dp3, w_in, mode="nt", a_parts=4, b_parts=4, name=f"{tag}_dh")
    dx_prev, dng = _rmsnorm_bwd(x, ng, dh, dx, name=f"{tag}_normbwd")
    return dx_prev, dng, d_win, fold(dgq), fold(dgk), d_wout


N_CHIPS = 4
HBM = pl.BlockSpec(memory_space=pl.ANY)


def _mesh_pos():
    return lax.axis_index("x"), lax.axis_index("y"), lax.axis_index("c")


def _other_chips(x, y):
    return [(1 - x, y), (x, 1 - y), (1 - x, 1 - y)]


def _chip_exchange(srcs, *, send_slot_is_dest, copy_own, name):
    n = len(srcs)

    def body(*refs):
        src_refs, out_refs = refs[:n], refs[n:2 * n]
        send_sems, recv_sems, local_sems = refs[2 * n:]
        x, y, c = _mesh_pos()
        me = 2 * x + y
        chips = _other_chips(x, y)
        local = []
        for a in range(n):
            if not copy_own[a]:
                continue
            own = src_refs[a].at[me] if send_slot_is_dest else src_refs[a]
            local.append(pltpu.make_async_copy(own, out_refs[a].at[me], local_sems.at[a]))
        for cp in local:
            cp.start()

        def copy(a, k, landing_slot):
            px, py = chips[k]
            src = src_refs[a].at[2 * px + py] if send_slot_is_dest else src_refs[a]
            return pltpu.make_async_remote_copy(
                src_ref=src, dst_ref=out_refs[a].at[landing_slot],
                send_sem=send_sems.at[a * 3 + k], recv_sem=recv_sems.at[a * 3 + k],
                device_id=(px, py, c), device_id_type=MESH)

        sends = [copy(a, k, me) for a in range(n) for k in range(3)]
        for cp in sends:
            cp.start()
        for a in range(n):
            for k in range(3):
                px, py = chips[k]
                copy(a, k, 2 * px + py).wait_recv()
        for cp in sends:
            cp.wait_send()
        for cp in local:
            cp.wait()

    outs = []
    for s in srcs:
        shape = s.shape if send_slot_is_dest else (N_CHIPS,) + s.shape
        outs.append(jax.ShapeDtypeStruct(shape, s.dtype))
    return pl.pallas_call(
        body, name=name, in_specs=[HBM] * n, out_specs=[HBM] * n, out_shape=outs,
        scratch_shapes=[pltpu.SemaphoreType.DMA((3 * n,)), pltpu.SemaphoreType.DMA((3 * n,)),
                        pltpu.SemaphoreType.DMA((n,))],
    )(*srcs)


def _sibling_exchange(srcs, *, name):
    n = len(srcs)

    def body(*refs):
        src_refs, out_refs = refs[:n], refs[n:2 * n]
        send_sems, recv_sems = refs[2 * n:]
        x, y, c = _mesh_pos()
        copies = [pltpu.make_async_remote_copy(
            src_ref=src_refs[a], dst_ref=out_refs[a], send_sem=send_sems.at[a], recv_sem=recv_sems.at[a],
            device_id=(x, y, 1 - c), device_id_type=MESH) for a in range(n)]
        for cp in copies:
            cp.start()
        for cp in copies:
            cp.wait()

    return pl.pallas_call(
        body, name=name, in_specs=[HBM] * n, out_specs=[HBM] * n,
        out_shape=[jax.ShapeDtypeStruct(s.shape, s.dtype) for s in srcs],
        scratch_shapes=[pltpu.SemaphoreType.DMA((n,)), pltpu.SemaphoreType.DMA((n,))],
    )(*srcs)


def _gather_halves(shards, small, *, name):
    n = len(shards)

    def body(*refs):
        s_refs, small_ref = refs[:n], refs[n]
        o_refs, osmall_ref = refs[n + 1:2 * n + 1], refs[2 * n + 1]
        send_sems, recv_sems, local_sems = refs[2 * n + 2:]
        x, y, c = _mesh_pos()
        me = 2 * x + y
        chips = _other_chips(x, y)
        local = [pltpu.make_async_copy(small_ref, osmall_ref.at[me], local_sems.at[0])]
        for cp in local:
            cp.start()

        def over_ici(a, k, slot):
            px, py = chips[k]
            return pltpu.make_async_remote_copy(
                src_ref=s_refs[a].at[c], dst_ref=o_refs[a].at[slot, c], send_sem=send_sems.at[3 * a + k],
                recv_sem=recv_sems.at[3 * a + k], device_id=(px, py, c), device_id_type=MESH)

        def small_copy(k, slot):
            px, py = chips[k]
            return pltpu.make_async_remote_copy(
                src_ref=small_ref, dst_ref=osmall_ref.at[slot], send_sem=send_sems.at[3 * n + k],
                recv_sem=recv_sems.at[3 * n + k], device_id=(px, py, c), device_id_type=MESH)

        def to_sibling(a, k, half):
            px, py = chips[k]
            blk = o_refs[a].at[2 * px + py, half]
            return pltpu.make_async_remote_copy(
                src_ref=blk, dst_ref=blk, send_sem=send_sems.at[3 * n + 3 + 3 * a + k],
                recv_sem=recv_sems.at[3 * n + 3 + 3 * a + k], device_id=(x, y, 1 - c), device_id_type=MESH)

        sends = [over_ici(a, k, me) for a in range(n) for k in range(3)] + [small_copy(k, me) for k in range(3)]
        for cp in sends:
            cp.start()
        passed = []
        for a in range(n):
            for k in range(3):
                px, py = chips[k]
                over_ici(a, k, 2 * px + py).wait_recv()
                passed.append(to_sibling(a, k, c))
                passed[-1].start()
        for k in range(3):
            px, py = chips[k]
            small_copy(k, 2 * px + py).wait_recv()
        for a in range(n):
            for k in range(3):
                to_sibling(a, k, 1 - c).wait_recv()
        for cp in sends + passed:
            cp.wait_send()
        for cp in local:
            cp.wait()

    nsem = 6 * n + 3
    return pl.pallas_call(
        body, name=name, in_specs=[HBM] * (n + 1), out_specs=[HBM] * (n + 1),
        out_shape=[jax.ShapeDtypeStruct((N_CHIPS,) + s.shape, s.dtype) for s in shards + [small]],
        scratch_shapes=[pltpu.SemaphoreType.DMA((nsem,)), pltpu.SemaphoreType.DMA((nsem,)),
                        pltpu.SemaphoreType.DMA((1,))],
    )(*shards, small)


def _forward_halves(landed, *, name):
    n = len(landed)

    def body(*refs):
        o_refs = refs[n:2 * n]
        send_sems, recv_sems = refs[2 * n:]
        x, y, c = _mesh_pos()
        chips = _other_chips(x, y)
        pairs = [(a, k) for a in range(n) for k in range(3)]

        def copy(a, k, half):
            px, py = chips[k]
            blk = o_refs[a].at[2 * px + py, half]
            return pltpu.make_async_remote_copy(
                src_ref=blk, dst_ref=blk, send_sem=send_sems.at[3 * a + k], recv_sem=recv_sems.at[3 * a + k],
                device_id=(x, y, 1 - c), device_id_type=MESH)

        sends = [copy(a, k, c) for a, k in pairs]
        for cp in sends:
            cp.start()
        for a, k in pairs:
            copy(a, k, 1 - c).wait_recv()
        for cp in sends:
            cp.wait_send()

    return pl.pallas_call(
        body, name=name, in_specs=[HBM] * n, out_specs=[HBM] * n,
        out_shape=[jax.ShapeDtypeStruct(a.shape, a.dtype) for a in landed],
        input_output_aliases={a: a for a in range(n)},
        scratch_shapes=[pltpu.SemaphoreType.DMA((3 * n,)), pltpu.SemaphoreType.DMA((3 * n,))],
    )(*landed)


def _swap_other_half(g_list, *, name):
    n = len(g_list)

    def body(*refs):
        g_refs, o_refs = refs[:n], refs[n:2 * n]
        send_sems, recv_sems = refs[2 * n:]
        x, y, c = _mesh_pos()
        copies = [pltpu.make_async_remote_copy(
            src_ref=g_refs[a].at[:, 1 - c], dst_ref=o_refs[a], send_sem=send_sems.at[a], recv_sem=recv_sems.at[a],
            device_id=(x, y, 1 - c), device_id_type=MESH) for a in range(n)]
        for cp in copies:
            cp.start()
        for cp in copies:
            cp.wait()

    return pl.pallas_call(
        body, name=name, in_specs=[HBM] * n, out_specs=[HBM] * n,
        out_shape=[jax.ShapeDtypeStruct((g.shape[0],) + g.shape[2:], g.dtype) for g in g_list],
        scratch_shapes=[pltpu.SemaphoreType.DMA((n,)), pltpu.SemaphoreType.DMA((n,))],
    )(*g_list)


def _row_tile(r):
    return _pick(r, (512, 256, 128, 64, 32, 16, 8))


def _add_my_half(g4, sib4, core, *, name):
    n, _, r, C = g4.shape
    tr = _row_tile(r)

    def body(core_ref, g_ref, s_ref, o_ref):
        del core_ref
        o_ref[...] = (g_ref[...].astype(F32) + s_ref[...].astype(F32)).astype(o_ref.dtype)

    return pl.pallas_call(
        body, name=name,
        grid_spec=pltpu.PrefetchScalarGridSpec(
            num_scalar_prefetch=1, grid=(n, r // tr),
            in_specs=[pl.BlockSpec((None, None, tr, C), lambda j, i, core_ref: (j, core_ref[0], i, 0)),
                      pl.BlockSpec((None, tr, C), lambda j, i, core_ref: (j, i, 0))],
            out_specs=pl.BlockSpec((None, tr, C), lambda j, i, core_ref: (j, i, 0))),
        out_shape=jax.ShapeDtypeStruct((n, r, C), g4.dtype),
        compiler_params=pltpu.CompilerParams(dimension_semantics=("parallel", "parallel")),
    )(core, g4, sib4)


def _sum_chips(landed, part, me, *, name):
    _, r, C = landed.shape
    tr = _row_tile(r)

    def body(me_ref, own_ref, r1_ref, r2_ref, r3_ref, o_ref):
        del me_ref
        f = lambda ref: ref[...].astype(F32)
        o_ref[...] = ((f(own_ref) + f(r1_ref)) + f(r2_ref)) + f(r3_ref)

    slot = lambda d: pl.BlockSpec((None, tr, C), lambda i, me_ref: ((me_ref[0] + d) % N_CHIPS, i, 0))
    return pl.pallas_call(
        body, name=name,
        grid_spec=pltpu.PrefetchScalarGridSpec(
            num_scalar_prefetch=1, grid=(r // tr,), in_specs=[slot(0), slot(1), slot(2), slot(3)],
            out_specs=pl.BlockSpec((tr, C), lambda i, me_ref: (i, 0))),
        out_shape=jax.ShapeDtypeStruct((r, C), F32),
        compiler_params=pltpu.CompilerParams(dimension_semantics=("parallel",)),
    )(me, part, landed, landed, landed)


def _adamw_halves(w, mine, theirs, m, v, core, *, layer, prev, name):
    shape = w.shape
    r, C = mine.shape
    tr = _pick(r, (128, 64, 32, 16, 8))
    per = r // tr
    view = lambda a: a.reshape(-1, C)
    n_prev = 0 if prev is None else 4

    def body(*refs):
        core_ref, w_ref, gm_ref, gt_ref, m_ref, v_ref = refs[:6]
        g_ref, d_ref, nm_ref, nv_ref = refs[6 + n_prev:]
        gv = jnp.where(pl.program_id(0) == core_ref[0], gm_ref[...], gt_ref[...])
        g_ref[...] = gv
        d_ref[...], nm_ref[...], nv_ref[...] = _adamw_math(w_ref[...], gv, m_ref[...], v_ref[...])

    half = pl.BlockSpec((tr, C), lambda h, i, core_ref: ((2 * layer + h) * per + i, 0))
    row = pl.BlockSpec((tr, C), lambda h, i, core_ref: (i, 0))
    out = jax.ShapeDtypeStruct((math.prod(shape) // C, C), F32)
    res = pl.pallas_call(
        body, name=name,
        grid_spec=pltpu.PrefetchScalarGridSpec(
            num_scalar_prefetch=1, grid=(2, per), in_specs=[half, row, row, half, half] + [HBM] * n_prev,
            out_specs=[half] * 4),
        out_shape=[out] * 4,
        input_output_aliases={6 + j: j for j in range(n_prev)},
        compiler_params=pltpu.CompilerParams(dimension_semantics=("parallel", "parallel")),
    )(core, view(w), mine, theirs, view(m), view(v), *([] if prev is None else [view(a) for a in prev]))
    return tuple(a.reshape(shape) for a in res)


def _sum_small(recv4, *, name):
    _, R, C = recv4.shape

    def body(r_ref, o_ref):
        o_ref[...] = ((r_ref[0] + r_ref[1]) + r_ref[2]) + r_ref[3]

    return pl.pallas_call(body, name=name, out_shape=jax.ShapeDtypeStruct((R, C), F32))(recv4)


def _add(a, b, *, name):
    R, C = a.shape
    tr = _pick(R, (512, 256, 128, 64, 32, 16, 8))
    blk = pl.BlockSpec((tr, C), lambda i: (i, 0))

    def body(a_ref, b_ref, o_ref):
        o_ref[...] = a_ref[...] + b_ref[...]

    return pl.pallas_call(body, name=name, grid=(R // tr,), in_specs=[blk, blk], out_specs=blk,
                          out_shape=jax.ShapeDtypeStruct((R, C), F32),
                          compiler_params=pltpu.CompilerParams(dimension_semantics=("parallel",)))(a, b)


def _adamw_math(w, g, m, v):
    nm = ADAM_B1 * m + (1.0 - ADAM_B1) * g
    nv = ADAM_B2 * v + (1.0 - ADAM_B2) * (g * g)
    m_hat = nm / (1.0 - ADAM_B1 ** ADAM_STEP)
    v_hat = nv / (1.0 - ADAM_B2 ** ADAM_STEP)
    return -ADAM_LR * (m_hat / (jnp.sqrt(v_hat) + ADAM_EPS) + ADAM_WD * w), nm, nv


def _adamw(w, g, m, v, *, name):
    shape = w.shape
    C = shape[-1]
    R = w.size // C
    two = lambda a: a.reshape(R, C)
    tr = _pick(R, (256, 128, 64, 32, 16, 8)) if R % 8 == 0 and R > 8 else R
    blk = pl.BlockSpec((tr, C), lambda i: (i, 0))

    def body(w_ref, g_ref, m_ref, v_ref, d_ref, nm_ref, nv_ref):
        d_ref[...], nm_ref[...], nv_ref[...] = _adamw_math(w_ref[...], g_ref[...], m_ref[...], v_ref[...])

    out = jax.ShapeDtypeStruct((R, C), F32)
    d, nm, nv = pl.pallas_call(
        body, name=name, grid=(R // tr,), in_specs=[blk] * 4, out_specs=[blk] * 3, out_shape=[out] * 3,
        compiler_params=pltpu.CompilerParams(dimension_semantics=("parallel",)),
    )(two(w), two(g), two(m), two(v))
    return d.reshape(shape), nm.reshape(shape), nv.reshape(shape)


BIG = (("dn_w_in", (2, 1024, 1540), 2), ("dn_w_out", (2, 512, 1024), 1), ("sb_w_in", (1, 1024, 1024), 2),
       ("sb_w_out", (1, 256, 1024), 1), ("sc_w_in", (1, 1024, 2048), 2), ("sc_w_out", (1, 512, 1024), 1))
SMALL = (("dn_conv_w", (2, 4, 1024), 2), ("dn_o_norm_g", (2, 64), 1), ("sc_conv_w", (1, 3, 512), 2))
REPL = (("norm_g", (4, 1024)), ("dn_a_log", (2, 8)), ("dn_dt_bias", (2, 8)), ("sb_q_norm_g", (1, 64)),
        ("sb_k_norm_g", (1, 64)))


def _halves(shard):
    return shard.reshape(2, -1, shard.shape[-1])


def _pack(arrays, cols, lead=()):
    flat = jnp.concatenate([a.reshape(lead + (-1,)) for a in arrays], axis=-1)
    n = flat.shape[-1]
    rows = -(-n // cols)
    unit = 512 if rows > 512 else 8
    rows = -(-rows // unit) * unit
    flat = jnp.pad(flat, [(0, 0)] * len(lead) + [(0, rows * cols - n)])
    return flat.reshape(lead + (rows, cols))


def _unpack(buf, table, lead=()):
    flat = buf.reshape(lead + (-1,))
    out, off = {}, 0
    for entry in table:
        name, shape = entry[0], entry[1]
        n = math.prod(shape)
        out[name] = flat[..., off:off + n].reshape(lead + shape)
        off += n
    return out


def _join(shards, axis):
    return jnp.concatenate([shards[j] for j in range(N_CHIPS)], axis=axis)


def _split(full, axis):
    return jnp.stack(jnp.split(full, N_CHIPS, axis=axis), axis=0)


def kernel(x, norm_g, dn_w_in, dn_conv_w, dn_a_log, dn_dt_bias, dn_o_norm_g, dn_w_out, sb_w_in, sb_q_norm_g, sb_k_norm_g, sb_w_out, sc_w_in, sc_conv_w, sc_w_out, loss_target, m_norm_g, m_dn_w_in, m_dn_conv_w, m_dn_a_log, m_dn_dt_bias, m_dn_o_norm_g, m_dn_w_out, m_sb_w_in, m_sb_q_norm_g, m_sb_k_norm_g, m_sb_w_out, m_sc_w_in, m_sc_conv_w, m_sc_w_out, v_norm_g, v_dn_w_in, v_dn_conv_w, v_dn_a_log, v_dn_dt_bias, v_dn_o_norm_g, v_dn_w_out, v_sb_w_in, v_sb_q_norm_g, v_sb_k_norm_g, v_sb_w_out, v_sc_w_in, v_sc_conv_w, v_sc_w_out):
    weights = dict(norm_g=norm_g, dn_w_in=dn_w_in, dn_conv_w=dn_conv_w, dn_a_log=dn_a_log, dn_dt_bias=dn_dt_bias,
                   dn_o_norm_g=dn_o_norm_g, dn_w_out=dn_w_out, sb_w_in=sb_w_in, sb_q_norm_g=sb_q_norm_g,
                   sb_k_norm_g=sb_k_norm_g, sb_w_out=sb_w_out, sc_w_in=sc_w_in, sc_conv_w=sc_conv_w, sc_w_out=sc_w_out)
    m_in = dict(norm_g=m_norm_g, dn_w_in=m_dn_w_in, dn_conv_w=m_dn_conv_w, dn_a_log=m_dn_a_log,
                dn_dt_bias=m_dn_dt_bias, dn_o_norm_g=m_dn_o_norm_g, dn_w_out=m_dn_w_out, sb_w_in=m_sb_w_in,
                sb_q_norm_g=m_sb_q_norm_g, sb_k_norm_g=m_sb_k_norm_g, sb_w_out=m_sb_w_out, sc_w_in=m_sc_w_in,
                sc_conv_w=m_sc_conv_w, sc_w_out=m_sc_w_out)
    v_in = dict(norm_g=v_norm_g, dn_w_in=v_dn_w_in, dn_conv_w=v_dn_conv_w, dn_a_log=v_dn_a_log,
                dn_dt_bias=v_dn_dt_bias, dn_o_norm_g=v_dn_o_norm_g, dn_w_out=v_dn_w_out, sb_w_in=v_sb_w_in,
                sb_q_norm_g=v_sb_q_norm_g, sb_k_norm_g=v_sb_k_norm_g, sb_w_out=v_sb_w_out, sc_w_in=v_sc_w_in,
                sc_conv_w=v_sc_conv_w, sc_w_out=v_sc_w_out)
    order = list(weights)
    xi, yi, ci = _mesh_pos()

    small = _pack([weights[n] for n, _, _ in SMALL], LANES)
    later = [("dn_w_in", 1), ("dn_w_out", 1), ("sb_w_in", 0), ("sb_w_out", 0), ("sc_w_in", 0), ("sc_w_out", 0)]
    piece = lambda n, l: _halves(weights[n][l].astype(BF16)[None])
    own_first = [piece("dn_w_in", 0), piece("dn_w_out", 0)]
    own_later = [piece(n, l) for n, l in later]
    own_last, own_mid = own_later[:2], own_later[2:]
    me = 2 * xi + yi
    whole = lambda g4, own: lax.dynamic_update_index_in_dim(g4, own, me, 0)
    flat = lambda g4: g4.reshape(N_CHIPS, -1, g4.shape[-1])
    rows_of = lambda w4: w4.reshape(-1, w4.shape[-1])
    dn_in = lambda w4: jnp.pad(_join(w4, 1), ((0, 0), (0, DN_IN_PAD - DN_IN)))
    w_in0, w_out0, small4 = _gather_halves(own_first, small, name="gather_first")
    full = {n: _join(a, ax) for (n, _, ax), a in zip(SMALL, _unpack(small4, SMALL, (N_CHIPS,)).values())}

    def dn_args(j, w_in4, w_out4):
        return (dn_in(flat(w_in4)), full["dn_conv_w"][j], dn_a_log[j], dn_dt_bias[j], full["dn_o_norm_g"][j],
                rows_of(w_out4))

    x0 = x[0]
    dn0 = dn_args(0, whole(w_in0, own_first[0]), whole(w_out0, own_first[1]))
    x1, s0, landed = _dn_layer_fwd(x0, norm_g[0], *dn0, "l0", send=own_mid)
    landed = _forward_halves(landed, name="forward_halves_mid")
    sb_in, sb_out, sc_in, sc_out = [whole(g4, own) for g4, own in zip(landed, own_mid)]
    sb_args = (flat(sb_in), sb_q_norm_g[0], sb_k_norm_g[0], rows_of(sb_out))
    sc_args = (flat(sc_in), full["sc_conv_w"][0], rows_of(sc_out))
    x2, s1, landed = _sb_layer_fwd(x1, norm_g[1], *sb_args, "l1", send=own_last)
    landed = _forward_halves(landed, name="forward_halves_last")
    dn1 = dn_args(1, *[whole(g4, own) for g4, own in zip(landed, own_last)])
    x3, s2 = _sc_layer_fwd(x2, norm_g[2], *sc_args, "l2")
    x4, s3, _ = _dn_layer_fwd(x3, norm_g[3], *dn1, "l3")
    dy, loss_local = _loss_head(x4, loss_target[0], name="loss_head")
    loss = lax.psum(loss_local[0, 0], ("x", "y", "c"))

    core = ci.astype(jnp.int32).reshape(1)
    chip = me.astype(jnp.int32).reshape(1)

    def chip_sums(g_list, tag):
        sib = _swap_other_half(g_list, name=f"swap_halves_{tag}")
        return [_add_my_half(g, s, core, name=f"sum_cores_{tag}{i}") for i, (g, s) in enumerate(zip(g_list, sib))]

    dx3, dng3, dwin3, dconv3, dal3, ddt3, dgain3, dwout3, _ = _dn_layer_bwd(dy, x3, norm_g[3], *dn1, s3, "l3")
    dx2, dng2, dwin2, dconv2, dwout2 = _sc_layer_bwd(dx3, x2, norm_g[2], *sc_args, s2, "l2")
    dx1, dng1, dwin1, dgq, dgk, dwout1 = _sb_layer_bwd(dx2, x1, norm_g[1], *sb_args, s1, "l1")
    part_later = chip_sums([_cut2(_by_cols(dwin3)), _cut2(_by_rows(dwout3)), _cut2(dwin1), _cut2(_by_rows(dwout1)),
                            _cut2(dwin2), _cut2(_by_rows(dwout2))], "later")
    dx0, dng0, part_win0, dconv0, dal0, ddt0, dgain0, part_wout0, landed = _dn_layer_bwd(
        dx1, x0, norm_g[0], *dn0, s0, "l0", send=part_later, chip_sums=chip_sums)
    pieces = later + [("dn_w_out", 0), ("dn_w_in", 0)]
    mine = {(n, l): _sum_chips(r, p, chip, name=f"sum_chips_{n}{l}")
            for (n, l), r, p in zip(pieces, landed, part_later + [part_wout0, part_win0])}
    pieces = sorted(pieces, key=lambda nl: nl[1])
    mine = [mine[nl] for nl in pieces]
    theirs = _sibling_exchange(mine, name="swap_results")
    upd = {}
    for (n, l), a, b in zip(pieces, mine, theirs):
        upd[n] = _adamw_halves(weights[n], a, b, m_in[n], v_in[n], core, layer=l, prev=upd.get(n),
                               name=f"adamw_{n}{l}")
    g_out = {n: upd[n][0] for n, _, _ in BIG}

    grads = dict(
        norm_g=jnp.concatenate([dng0, dng1, dng2, dng3], axis=0), dn_conv_w=jnp.stack([dconv0, dconv3]),
        dn_a_log=jnp.stack([dal0, dal3]), dn_dt_bias=jnp.stack([ddt0, ddt3]),
        dn_o_norm_g=jnp.stack([dgain0, dgain3]), sb_q_norm_g=dgq[None], sb_k_norm_g=dgk[None],
        sc_conv_w=dconv2[None])
    repl = [jnp.broadcast_to(grads[n][None], (N_CHIPS,) + s) for n, s in REPL]
    gsmall = _pack([_split(grads[n], ax) for n, _, ax in SMALL] + repl, LANES, (N_CHIPS,))
    rsmall, = _chip_exchange([gsmall], send_slot_is_dest=True, copy_own=(True,), name="scatter_small")
    psmall = _sum_small(rsmall, name="sum_chips_small")
    qsmall, = _sibling_exchange([psmall], name="swap_cores_small")
    tsmall = _add(psmall, qsmall, name="sum_cores_small")
    g_out.update(_unpack(tsmall, SMALL + REPL))

    for n in order:
        if n not in upd:
            upd[n] = (g_out[n],) + _adamw(weights[n], g_out[n], m_in[n], v_in[n], name=f"adamw_{n}")
    return (loss, dx0[None], *[upd[n][0] for n in order], *[upd[n][1] for n in order],
            *[upd[n][2] for n in order], *[upd[n][3] for n in order])
```

```python
import functools
import math

import jax
import jax.numpy as jnp
from jax import lax
from jax.experimental import pallas as pl
from jax.experimental.pallas import tpu as pltpu

F32 = jnp.float32
BF16 = jnp.bfloat16
MESH = pl.DeviceIdType.MESH

RMS_EPS = 1e-6
L2_EPS = 1e-6
LANES = 128
VMEM_BIG = 60 * 1024 * 1024
MM_VMEM = 44 * 1024 * 1024

DN_HEADS, DN_DK, DN_DV, DN_CHUNK, DN_CONV = 8, 128, 256, 64, 4
DN_QK_W = DN_HEADS * DN_DK
DN_V_W = DN_HEADS * DN_DV
DN_CONV_W = 2 * DN_QK_W + DN_V_W
DN_IN = DN_CONV_W + DN_V_W + 2 * DN_HEADS
DN_IN_PAD = DN_CONV_W + DN_V_W + LANES
SB_DH = 64
SC_CONV = 3

ADAM_LR, ADAM_B1, ADAM_B2, ADAM_EPS, ADAM_WD, ADAM_STEP = 0.001, 0.9, 0.999, 1e-08, 0.01, 10


def _pick(n, cands):
    for c in cands:
        if n % c == 0:
            return c
    raise ValueError(f"no tile for {n} in {cands}")


def _bf(x):
    return x.astype(BF16)


def _dot(a, b):
    return jnp.dot(_bf(a), _bf(b), preferred_element_type=F32)


def _dot_nt(a, b):
    return lax.dot_general(_bf(a), _bf(b), (((1,), (1,)), ((), ())), preferred_element_type=F32)


def _dot_tn(a, b):
    return lax.dot_general(_bf(a), _bf(b), (((0,), (0,)), ((), ())), preferred_element_type=F32)


def _split3(a):
    hi = _bf(a)
    r = a - hi.astype(F32)
    mid = _bf(r)
    lo = _bf(r - mid.astype(F32))
    return hi, mid, lo


def _sigmoid(x):
    return 1.0 / (1.0 + jnp.exp(-x))


def _silu(x):
    return x * _sigmoid(x)


def _dsilu(x):
    s = _sigmoid(x)
    return s * (1.0 + x * (1.0 - s))


def _softplus(x):
    return jnp.maximum(x, 0.0) + jnp.log(1.0 + jnp.exp(-jnp.abs(x)))


def _shift_down(z, k):
    if k == 0:
        return z
    row = lax.broadcasted_iota(jnp.int32, z.shape, 0)
    return jnp.where(row >= k, pltpu.roll(z, k, 0), 0.0)


def _shift_up(z, k):
    if k == 0:
        return z
    n = z.shape[0]
    row = lax.broadcasted_iota(jnp.int32, z.shape, 0)
    return jnp.where(row < n - k, pltpu.roll(z, n - k, 0), 0.0)


def _matmul(a, b, *, mode, name, res=None, a_parts=1, b_parts=1, out_parts=1, out_dtype=F32, send=()):
    def dims2(x, parts):
        if parts == 1:
            return x.shape
        assert x.shape[0] == parts
        return (x.shape[1], x.shape[2] * parts)

    ash, bsh = dims2(a, a_parts), dims2(b, b_parts)
    if mode == "nn":
        (M, K), (K2, N) = ash, bsh
        dn = (((1,), (0,)), ((), ()))
    elif mode == "nt":
        (M, K), (N, K2) = ash, bsh
        dn = (((1,), (1,)), ((), ()))
    else:
        (K, M), (K2, N) = ash, bsh
        dn = (((0,), (0,)), ((), ()))
    assert K == K2, (ash, bsh, mode)
    tm_max = _pick(M, (512, 256, 128, 64, 32, 16, 8))
    n_unit = N // max(out_parts, b_parts if mode != "nt" else 1)
    k_unit = K // max(a_parts if mode != "tn" else 1, b_parts if mode == "nt" else 1)
    tm, tn, tk = min(
        ((m, n, k) for m in {tm_max, max(tm_max // 2, 8)}
         for n in (2048, 1792, 1024, 896, 768, 512, 384, 256, 128) if n_unit % n == 0
         for k in (k_unit, 2048, 1792, 1024, 896, 512, 256, 128) if k_unit % k == 0
         if 2 * (m * k * a.dtype.itemsize + k * n * b.dtype.itemsize + 2 * m * n * 4) + m * n * 4 <= MM_VMEM),
        key=lambda t: (-t[0] * t[1] * t[2], -t[0], -t[2]))
    nk = K // tk
    grid = (M // tm, N // tn, nk)

    def spec(parts, rows_are, cols_are, tr, tc, width):
        per = width // parts // tc
        if parts == 1:
            return pl.BlockSpec((tr, tc), lambda i, j, k: ((i, j, k)[rows_are], (i, j, k)[cols_are]))
        return pl.BlockSpec((None, tr, tc), lambda i, j, k: ((i, j, k)[cols_are] // per, (i, j, k)[rows_are],
                                                             (i, j, k)[cols_are] % per))

    if mode == "nn":
        a_spec = spec(a_parts, 0, 2, tm, tk, K)
        b_spec = spec(b_parts, 2, 1, tk, tn, N)
    elif mode == "nt":
        a_spec = spec(a_parts, 0, 2, tm, tk, K)
        b_spec = spec(b_parts, 1, 2, tn, tk, K)
    else:
        a_spec = spec(a_parts, 2, 0, tk, tm, M)
        b_spec = spec(b_parts, 2, 1, tk, tn, N)
    o_spec = spec(out_parts, 0, 1, tm, tn, N)
    in_specs = [a_spec, b_spec]
    operands = [a, b]
    if res is not None:
        in_specs.append(pl.BlockSpec((tm, tn), lambda i, j, k: (i, j)))
        operands.append(res)

    n_in = len(operands)
    ns = len(send)

    def finish(refs, r):
        if res is not None:
            r = refs[2][...] + r
        refs[n_in + ns][...] = r.astype(out_dtype)

    def body(*refs):
        if ns:
            at = lambda step: functools.reduce(jnp.logical_and, [pl.program_id(d) == step[d] for d in range(3)])
            _blocks_over_ici(refs[n_in:n_in + ns], refs[n_in + ns + 1:n_in + 2 * ns + 1], refs[-2], refs[-1],
                             at((0, 0, 0)), at(tuple(g - 1 for g in grid)))
        part = lax.dot_general(_bf(refs[0][...]), _bf(refs[1][...]), dn, preferred_element_type=F32)
        if nk == 1:
            finish(refs, part)
            return
        acc_ref = refs[n_in + 2 * ns + 1]
        k = pl.program_id(2)

        @pl.when(k == 0)
        def _():
            acc_ref[...] = part

        @pl.when(jnp.logical_and(k > 0, k < nk - 1))
        def _():
            acc_ref[...] += part

        @pl.when(k == nk - 1)
        def _():
            finish(refs, acc_ref[...] + part)

    out_shape = (M, N) if out_parts == 1 else (out_parts, M, N // out_parts)
    out = pl.pallas_call(
        body, name=name, grid=grid, in_specs=in_specs + [HBM] * ns, out_specs=[o_spec] + [HBM] * ns,
        out_shape=[jax.ShapeDtypeStruct(out_shape, out_dtype)] + [jax.ShapeDtypeStruct(x.shape, x.dtype) for x in send],
        scratch_shapes=([pltpu.VMEM((tm, tn), F32)] if nk > 1 else [])
        + ([pltpu.SemaphoreType.DMA((3 * ns,)), pltpu.SemaphoreType.DMA((3 * ns,))] if ns else []),
        compiler_params=pltpu.CompilerParams(
            dimension_semantics=("arbitrary",) * 3 if ns else ("parallel", "parallel", "arbitrary"),
            vmem_limit_bytes=VMEM_BIG),
    )(*operands, *send)
    return out if ns else out[0]


def _rmsnorm_fwd(x, g, *, name):
    T, D = x.shape
    tm = _pick(T, (512, 256, 128, 64, 32, 16))

    def body(x_ref, g_ref, h_ref):
        xv = x_ref[...]
        r = lax.rsqrt(jnp.mean(xv * xv, axis=-1, keepdims=True) + RMS_EPS)
        h_ref[...] = ((xv * r) * g_ref[...]).astype(BF16)

    return pl.pallas_call(
        body, name=name, grid=(T // tm,),
        in_specs=[pl.BlockSpec((tm, D), lambda i: (i, 0)), pl.BlockSpec((1, D), lambda i: (0, 0))],
        out_specs=pl.BlockSpec((tm, D), lambda i: (i, 0)),
        out_shape=jax.ShapeDtypeStruct((T, D), BF16),
    )(x, g.reshape(1, D))


def _rmsnorm_bwd(x, g, dh, dx_in, *, name):
    T, D = x.shape
    tm = _pick(T, (512, 256, 128, 64, 32, 16))

    def body(x_ref, g_ref, dh_ref, dxin_ref, dx_ref, dg_ref):
        @pl.when(pl.program_id(0) == 0)
        def _():
            dg_ref[...] = jnp.zeros_like(dg_ref)

        xv = x_ref[...]
        r = lax.rsqrt(jnp.mean(xv * xv, axis=-1, keepdims=True) + RMS_EPS)
        xh = xv * r
        dh_v = dh_ref[...]
        dxh = dh_v * g_ref[...]
        dx_ref[...] = dxin_ref[...] + r * (dxh - xh * jnp.mean(dxh * xh, axis=-1, keepdims=True))
        dg_ref[...] += jnp.sum(dh_v * xh, axis=0, keepdims=True)

    row = pl.BlockSpec((tm, D), lambda i: (i, 0))
    vec = pl.BlockSpec((1, D), lambda i: (0, 0))
    return pl.pallas_call(
        body, name=name, grid=(T // tm,),
        in_specs=[row, vec, row, row], out_specs=[row, vec],
        out_shape=[jax.ShapeDtypeStruct((T, D), F32), jax.ShapeDtypeStruct((1, D), F32)],
        compiler_params=pltpu.CompilerParams(dimension_semantics=("arbitrary",)),
    )(x, g.reshape(1, D), dh, dx_in)


def _loss_head(y, target, *, name):
    T, D = y.shape
    tm = _pick(T, (512, 256, 128, 64, 32, 16))

    def body(y_ref, t_ref, dy_ref, l_ref):
        @pl.when(pl.program_id(0) == 0)
        def _():
            l_ref[...] = jnp.zeros_like(l_ref)

        err = y_ref[...] - t_ref[...]
        dy_ref[...] = err * (1.0 / D)
        l_ref[...] += 0.5 * jnp.sum(jnp.mean(err * err, axis=-1, keepdims=True), axis=0, keepdims=True)

    row = pl.BlockSpec((tm, D), lambda i: (i, 0))
    return pl.pallas_call(
        body, name=name, grid=(T // tm,),
        in_specs=[row, row], out_specs=[row, pl.BlockSpec((1, 1), lambda i: (0, 0))],
        out_shape=[jax.ShapeDtypeStruct((T, D), F32), jax.ShapeDtypeStruct((1, 1), F32)],
        compiler_params=pltpu.CompilerParams(dimension_semantics=("arbitrary",)),
    )(y, target)


def _sc_mid_fwd(p3, conv_w, *, name):
    _, T, W = p3.shape
    K = conv_w.shape[0]
    cw = LANES

    def body(p_ref, w_ref, o_ref):
        z = p_ref[1] * p_ref[2]
        cv = sum(w_ref[i:i + 1, :] * _shift_down(z, K - 1 - i) for i in range(K))
        o_ref[...] = ((p_ref[0] * cv) * _silu(p_ref[3])).astype(BF16)

    return pl.pallas_call(
        body, name=name, grid=(W // cw,),
        in_specs=[pl.BlockSpec((4, T, cw), lambda j: (0, 0, j)), pl.BlockSpec((K, cw), lambda j: (0, j))],
        out_specs=pl.BlockSpec((T, cw), lambda j: (0, j)),
        out_shape=jax.ShapeDtypeStruct((T, W), BF16),
        compiler_params=pltpu.CompilerParams(dimension_semantics=("parallel",), vmem_limit_bytes=VMEM_BIG),
    )(p3, conv_w)


def _sc_mid_bwd(p3, conv_w, do, *, name):
    _, T, W = p3.shape
    K = conv_w.shape[0]
    cw = LANES

    def body(p_ref, w_ref, do_ref, dp_ref, dw_ref):
        b, c, u, gate = p_ref[0], p_ref[1], p_ref[2], p_ref[3]
        z = c * u
        zs = [_shift_down(z, K - 1 - i) for i in range(K)]
        cv = sum(w_ref[i:i + 1, :] * zs[i] for i in range(K))
        y = b * cv
        dov = do_ref[...]
        dy = dov * _silu(gate)
        dp_ref[3] = dov * y * _dsilu(gate)
        dp_ref[0] = dy * cv
        dcv = dy * b
        dz = sum(w_ref[i:i + 1, :] * _shift_up(dcv, K - 1 - i) for i in range(K))
        dp_ref[1] = dz * u
        dp_ref[2] = dz * c
        for i in range(K):
            dw_ref[i:i + 1, :] = jnp.sum(dcv * zs[i], axis=0, keepdims=True)

    return pl.pallas_call(
        body, name=name, grid=(W // cw,),
        in_specs=[pl.BlockSpec((4, T, cw), lambda j: (0, 0, j)), pl.BlockSpec((K, cw), lambda j: (0, j)),
                  pl.BlockSpec((T, cw), lambda j: (0, j))],
        out_specs=[pl.BlockSpec((4, T, cw), lambda j: (0, 0, j)), pl.BlockSpec((K, cw), lambda j: (0, j))],
        out_shape=[jax.ShapeDtypeStruct((4, T, W), F32), jax.ShapeDtypeStruct((K, W), F32)],
        compiler_params=pltpu.CompilerParams(dimension_semantics=("parallel",), vmem_limit_bytes=VMEM_BIG),
    )(p3, conv_w, do)


def _sc_layer_fwd(x, ng, w_in, conv_w, w_out, tag):
    h = _rmsnorm_fwd(x, ng, name=f"{tag}_norm")
    p3 = _matmul(h, w_in, mode="nn", b_parts=4, out_parts=4, name=f"{tag}_inproj")
    og = _sc_mid_fwd(p3, conv_w, name=f"{tag}_mid")
    x_new = _matmul(og, w_out, mode="nn", res=x, name=f"{tag}_outproj")
    return x_new, (h, p3, og)


def _sc_layer_bwd(dx, x, ng, w_in, conv_w, w_out, saved, tag):
    h, p3, og = saved
    d_wout = _matmul(og, dx, mode="tn", out_dtype=BF16, name=f"{tag}_dwout")
    dog = _matmul(dx, w_out, mode="nt", name=f"{tag}_dog")
    dp3, dconv = _sc_mid_bwd(p3, conv_w, dog, name=f"{tag}_midbwd")
    d_win = _matmul(h, dp3, mode="tn", b_parts=4, out_parts=4, out_dtype=BF16, name=f"{tag}_dwin")
    dh = _matmul(dp3, w_in, mode="nt", a_parts=4, b_parts=4, name=f"{tag}_dh")
    dx_prev, dng = _rmsnorm_bwd(x, ng, dh, dx, name=f"{tag}_normbwd")
    return dx_prev, dng, d_win, dconv, d_wout


SB_BQ = 256
SB_BK = 256
SB_ROWS = 512
SB_DEAD = -110.0


def _sb_half_mask():
    return lax.broadcasted_iota(jnp.int32, (1, LANES), 1) < SB_DH


def _sb_headnorm(x, g, lo):
    x2 = x * x
    s_lo = jnp.sum(jnp.where(lo, x2, 0.0), axis=-1, keepdims=True)
    s_hi = jnp.sum(jnp.where(lo, 0.0, x2), axis=-1, keepdims=True)
    r = lax.rsqrt(jnp.where(lo, s_lo, s_hi) * (1.0 / SB_DH) + RMS_EPS)
    xh = x * r
    return xh * g, xh, r


def _dot_x2_l(a_l, b_exact_bf16):
    his = [_bf(a) for a in a_l]
    mids = [_bf(a - h.astype(F32)) for a, h in zip(a_l, his)]
    f = lambda p: jnp.dot(p, b_exact_bf16, preferred_element_type=F32)
    return [x + y for x, y in zip([f(h) for h in his], [f(m) for m in mids])]


def _sb_stack(xb, lo):
    zero = jnp.zeros_like(xb)
    return jnp.concatenate([jnp.where(lo, xb, zero), jnp.where(lo, zero, xb)], axis=0)


def _sb_rel(bq, bk):
    row = lax.broadcasted_iota(jnp.int32, (2 * bq, bk), 0)
    col = lax.broadcasted_iota(jnp.int32, (2 * bq, bk), 1)
    return col - jnp.where(row >= bq, row - bq, row)


def _sb_tile(qm, kb, valid):
    z = lax.dot_general(qm, kb, (((1,), (1,)), ((), ())), preferred_element_type=F32)
    sp = _softplus(z)
    return z - sp, (-sp if valid is None else jnp.where(valid, -sp, 0.0))


def _sb_attn_fwd(p3, gq2, gk2, *, name, send=()):
    _, T, W = p3.shape
    bq, bk = min(SB_BQ, T), min(SB_BK, T)
    rows = min(SB_ROWS, T)
    scale = SB_DH ** -0.5
    ns = len(send)
    npair = W // LANES

    def body(*refs):
        p_ref, gq_ref, gk_ref = refs[:3]
        og_ref, o_ref, ls_ref, cnt_ref = refs[3 + ns:7 + ns]
        qn_ref, kn_ref, v_ref = refs[7 + 2 * ns:10 + 2 * ns]
        if ns:
            _halves_over_ici(refs[3:3 + ns], refs[7 + ns:7 + 2 * ns], refs[10 + 2 * ns], refs[11 + 2 * ns],
                             pl.program_id(0) == 0, pl.program_id(0) == npair - 1)
        lo = _sb_half_mask()

        def prologue(i, c):
            r0 = pl.multiple_of(i * rows, rows)
            sl = pl.ds(r0, rows)
            qn_ref[sl, :] = (_sb_headnorm(p_ref[0, sl, :], gq_ref[...], lo)[0] * scale).astype(BF16)
            kn_ref[sl, :] = _sb_headnorm(p_ref[1, sl, :], gk_ref[...], lo)[0].astype(BF16)
            v_ref[sl, :] = p_ref[2, sl, :].astype(BF16)
            return c

        lax.fori_loop(0, T // rows, prologue, 0)

        rel = _sb_rel(bq, bk)
        tri = (lax.broadcasted_iota(jnp.int32, (bk, bk), 0)
               > lax.broadcasted_iota(jnp.int32, (bk, bk), 1)).astype(BF16)

        def qblock(qi, c):
            q0 = pl.multiple_of(qi * bq, bq)
            qm = _sb_stack(qn_ref[pl.ds(q0, bq), :], lo)
            nkb = (q0 + bq - 1) // bk + 1

            def tiles(k0s, carry, valids):
                o_acc, a_carry = carry
                sc = [_sb_tile(qm, kn_ref[pl.ds(k0, bk), :], valid) for k0, valid in zip(k0s, valids)]
                later = _dot_x2_l([log1m for _, log1m in sc], tri)
                for (logsig, log1m), lat, k0, valid in zip(sc, later, k0s, valids):
                    wts = jnp.exp(logsig + (lat + a_carry))
                    if valid is not None:
                        wts = jnp.where(valid, wts, 0.0)
                    o_acc = o_acc + jnp.dot(_bf(wts), v_ref[pl.ds(k0, bk), :], preferred_element_type=F32)
                    a_carry = a_carry + jnp.sum(log1m, axis=-1, keepdims=True)
                return o_acc, a_carry

            blk0 = lambda j: pl.multiple_of(j * bk, bk)
            k_last = blk0(nkb - 1)
            o2, t2 = tiles([k_last, blk0(jnp.maximum(nkb - 2, 0))],
                           (jnp.zeros((2 * bq, LANES), F32), jnp.zeros((2 * bq, 1), F32)),
                           [rel < q0 - k_last, nkb >= 2])

            def alive(st):
                return jnp.logical_and(st[0] < nkb - 1, jnp.max(st[2]) > SB_DEAD)

            def back_one(st):
                return (st[0] + 1,) + tiles([blk0(nkb - 2 - st[0])], st[1:], [None])

            n_back, o2, t2 = lax.while_loop(alive, back_one, (jnp.int32(1), o2, t2))
            o = jnp.where(lo, o2[:bq], o2[bq:])
            o_ref[pl.ds(q0, bq), :] = o
            ls_ref[pl.ds(q0, bq), :] = jnp.where(lo, t2[:bq], t2[bq:])
            cnt_ref[qi] = jnp.full((8, LANES), jnp.minimum(n_back + 1, nkb).astype(F32))
            og_ref[pl.ds(q0, bq), :] = (o * _silu(p_ref[3, pl.ds(q0, bq), :])).astype(BF16)
            return c

        lax.fori_loop(0, T // bq, qblock, 0)

    colblk = pl.BlockSpec((T, LANES), lambda j: (0, j))
    vec = pl.BlockSpec((1, LANES), lambda j: (0, 0))
    return pl.pallas_call(
        body, name=name, grid=(npair,),
        in_specs=[pl.BlockSpec((4, T, LANES), lambda j: (0, 0, j)), vec, vec] + [HBM] * ns,
        out_specs=[colblk, colblk, colblk, pl.BlockSpec((None, T // bq, 8, LANES), lambda j: (j, 0, 0, 0))]
        + [HBM] * ns,
        out_shape=[jax.ShapeDtypeStruct((T, W), BF16), jax.ShapeDtypeStruct((T, W), F32),
                   jax.ShapeDtypeStruct((T, W), F32), jax.ShapeDtypeStruct((npair, T // bq, 8, LANES), F32)]
        + [jax.ShapeDtypeStruct((N_CHIPS,) + a.shape, a.dtype) for a in send],
        scratch_shapes=[pltpu.VMEM((T, LANES), BF16)] * 3
        + ([pltpu.SemaphoreType.DMA((3 * ns,)), pltpu.SemaphoreType.DMA((3 * ns,))] if ns else []),
        compiler_params=pltpu.CompilerParams(dimension_semantics=("arbitrary",), vmem_limit_bytes=VMEM_BIG),
    )(p3, gq2, gk2, *send)


def _sb_attn_bwd(p3, gq2, gk2, o, lsum, live, dog, *, name):
    _, T, W = p3.shape
    bq, bk = min(SB_BQ, T), min(SB_BK, T)
    rows = min(SB_ROWS, T)
    scale = SB_DH ** -0.5

    def body(p_ref, gq_ref, gk_ref, o_ref, ls_ref, cnt_ref, dog_ref, dp_ref, dgq_ref, dgk_ref,
             qn_ref, kn_ref, v_ref, do_ref):
        lo = _sb_half_mask()

        def prologue(i, c):
            r0 = pl.multiple_of(i * rows, rows)
            sl = pl.ds(r0, rows)
            qn_ref[sl, :] = (_sb_headnorm(p_ref[0, sl, :], gq_ref[...], lo)[0] * scale).astype(BF16)
            kn_ref[sl, :] = _sb_headnorm(p_ref[1, sl, :], gk_ref[...], lo)[0].astype(BF16)
            v_ref[sl, :] = p_ref[2, sl, :].astype(BF16)
            gate = p_ref[3, sl, :]
            dogv = dog_ref[sl, :]
            dp_ref[3, sl, :] = dogv * o_ref[sl, :] * _dsilu(gate)
            do_ref[sl, :] = (dogv * _silu(gate)).astype(BF16)
            zero = jnp.zeros((rows, LANES), F32)
            dp_ref[0, sl, :] = zero
            dp_ref[1, sl, :] = zero
            dp_ref[2, sl, :] = zero
            return c

        lax.fori_loop(0, T // rows, prologue, 0)

        rel = _sb_rel(bq, bk)
        rj = lax.broadcasted_iota(jnp.int32, (bk, bk), 0)
        cj = lax.broadcasted_iota(jnp.int32, (bk, bk), 1)
        upto = (rj <= cj).astype(BF16)
        before_m = (rj < cj).astype(BF16)

        def qblock(qi, c):
            q0 = pl.multiple_of(qi * bq, bq)
            qm = _sb_stack(qn_ref[pl.ds(q0, bq), :], lo)
            dom = _sb_stack(do_ref[pl.ds(q0, bq), :], lo)
            nkb = (q0 + bq - 1) // bk + 1
            blk0 = lambda j: pl.multiple_of(j * bk, bk)
            k_last = blk0(nkb - 1)

            lsb = ls_ref[pl.ds(q0, bq), :]
            total = jnp.concatenate([lsb[:, 0:1], lsb[:, SB_DH:SB_DH + 1]], axis=0)
            n_live = jnp.clip(jnp.max(cnt_ref[qi]).astype(jnp.int32), 1, nkb)
            k_first = nkb - n_live

            def tiles(k0s, carry, valids):
                dq_acc, a_pre, r_pre = carry
                kss = [pl.ds(k0, bk) for k0 in k0s]
                kbs = [kn_ref[ks, :] for ks in kss]
                sc = [_sb_tile(qm, kb, valid) for kb, valid in zip(kbs, valids)]
                dws = [lax.dot_general(dom, v_ref[ks, :], _NT, preferred_element_type=F32) for ks in kss]
                upto_l = _dot_x2_l([log1m for _, log1m in sc], upto)
                wts_l = []
                for (logsig, log1m), up, valid in zip(sc, upto_l, valids):
                    wts = jnp.exp(logsig + ((total - a_pre) - up))
                    wts_l.append(wts if valid is None else jnp.where(valid, wts, 0.0))
                    a_pre = a_pre + jnp.sum(log1m, axis=-1, keepdims=True)
                ee_l = [dw * wts for dw, wts in zip(dws, wts_l)]
                before_l = _dot_x2_l(ee_l, before_m)
                for (logsig, _), ks, kb, wts, ee, bef, valid in zip(sc, kss, kbs, wts_l, ee_l, before_l, valids):
                    beta = jnp.exp(logsig)
                    dz = ee * (1.0 - beta) - beta * (r_pre + bef)
                    if valid is not None:
                        dz = jnp.where(valid, dz, 0.0)
                    dzb = _bf(dz)
                    dq_acc = dq_acc + jnp.dot(dzb, kb, preferred_element_type=F32)
                    dp_ref[1, ks, :] += lax.dot_general(dzb, qm, _TN, preferred_element_type=F32)
                    dp_ref[2, ks, :] += lax.dot_general(_bf(wts), dom, _TN, preferred_element_type=F32)
                    r_pre = r_pre + jnp.sum(ee, axis=-1, keepdims=True)
                return dq_acc, a_pre, r_pre

            cr = (jnp.zeros((2 * bq, LANES), F32), jnp.zeros((2 * bq, 1), F32), jnp.zeros((2 * bq, 1), F32))
            n_before = jnp.maximum(n_live - 2, 0)
            cr = lax.fori_loop(0, n_before % 2, lambda t, cr: tiles([blk0(k_first)], cr, [None]), cr)
            k_pairs = k_first + n_before % 2
            cr = lax.fori_loop(0, n_before // 2,
                               lambda t, cr: tiles([blk0(k_pairs + 2 * t), blk0(k_pairs + 2 * t + 1)], cr,
                                                   [None, None]), cr)
            dq2, _, _ = tiles([blk0(jnp.maximum(nkb - 2, 0)), k_last], cr, [n_live >= 2, rel < q0 - k_last])
            dp_ref[0, pl.ds(q0, bq), :] = jnp.where(lo, dq2[:bq], dq2[bq:]) * scale
            return c

        lax.fori_loop(0, T // bq, qblock, 0)

        dgq_ref[...] = jnp.zeros_like(dgq_ref)
        dgk_ref[...] = jnp.zeros_like(dgk_ref)

        def epilogue(i, c):
            r0 = pl.multiple_of(i * rows, rows)
            sl = pl.ds(r0, rows)
            for part, g_ref, dg_ref in ((0, gq_ref, dgq_ref), (1, gk_ref, dgk_ref)):
                _, xh, r = _sb_headnorm(p_ref[part, sl, :], g_ref[...], lo)
                dn = dp_ref[part, sl, :]
                dxh = dn * g_ref[...]
                prod = dxh * xh
                m_lo = jnp.sum(jnp.where(lo, prod, 0.0), axis=-1, keepdims=True)
                m_hi = jnp.sum(jnp.where(lo, 0.0, prod), axis=-1, keepdims=True)
                m = jnp.where(lo, m_lo, m_hi) * (1.0 / SB_DH)
                dp_ref[part, sl, :] = r * (dxh - xh * m)
                dg_ref[...] += jnp.sum(dn * xh, axis=0, keepdims=True)
            return c

        lax.fori_loop(0, T // rows, epilogue, 0)

    colblk = pl.BlockSpec((T, LANES), lambda j: (0, j))
    vec = pl.BlockSpec((1, LANES), lambda j: (0, 0))
    part = pl.BlockSpec((4, T, LANES), lambda j: (0, 0, j))
    gvec = pl.BlockSpec((None, 1, LANES), lambda j: (j, 0, 0))
    npair = W // LANES
    return pl.pallas_call(
        body, name=name, grid=(npair,),
        in_specs=[part, vec, vec, colblk, colblk, pl.BlockSpec((None, T // bq, 8, LANES), lambda j: (j, 0, 0, 0)),
                  colblk],
        out_specs=[part, gvec, gvec],
        out_shape=[jax.ShapeDtypeStruct((4, T, W), F32), jax.ShapeDtypeStruct((npair, 1, LANES), F32),
                   jax.ShapeDtypeStruct((npair, 1, LANES), F32)],
        scratch_shapes=[pltpu.VMEM((T, LANES), BF16)] * 4,
        compiler_params=pltpu.CompilerParams(dimension_semantics=("parallel",), vmem_limit_bytes=VMEM_BIG),
    )(p3, gq2, gk2, o, lsum, live, dog)


_NN = (((1,), (0,)), ((), ()))
_NT = (((1,), (1,)), ((), ()))
_TN = (((0,), (0,)), ((), ()))
DN_TB = 512
DN_HEADS_FWD = 4
DN_HEADS_BWD = 2
DN_INV_EXACT_LEVELS = 2
DN_AB_COL = (DN_CONV_W + DN_V_W) // LANES


def _dn_conv(x, w_ref):
    k = w_ref.shape[0]
    return sum(w_ref[i:i + 1, :] * _shift_down(x, k - 1 - i) for i in range(k))


def _dn_prep_fwd(p, conv_w, *, name):
    T = p.shape[0]
    cw = conv_w.shape[1]
    n_qk = 2 * DN_QK_W // LANES

    def body(p_ref, w_ref, o_ref):
        s = _silu(_dn_conv(p_ref[...], w_ref))
        r = lax.rsqrt(jnp.sum(s * s, axis=-1, keepdims=True) + L2_EPS)
        o_ref[...] = jnp.where(pl.program_id(0) < n_qk, s * r, s)

    colblk = pl.BlockSpec((T, LANES), lambda j: (0, j))
    return pl.pallas_call(
        body, name=name, grid=(cw // LANES,),
        in_specs=[colblk, pl.BlockSpec((DN_CONV, LANES), lambda j: (0, j))],
        out_specs=colblk, out_shape=jax.ShapeDtypeStruct((T, cw), F32),
        compiler_params=pltpu.CompilerParams(dimension_semantics=("parallel",), vmem_limit_bytes=VMEM_BIG),
    )(p, conv_w)


def _dn_chunk_tri(rows, upper):
    r = lax.broadcasted_iota(jnp.int32, (rows, rows), 0)
    c = lax.broadcasted_iota(jnp.int32, (rows, rows), 1)
    same = (r // DN_CHUNK) == (c // DN_CHUNK)
    return jnp.logical_and(same, (c >= r) if upper else (c <= r)).astype(BF16)


def _dn_lane_rows(a_log, dt_bias):
    pad = lambda v: jnp.zeros((1, LANES), F32).at[0, :DN_HEADS].set(v)
    return pad(a_log), pad(dt_bias)


def _dn_ab_parts(blk, alog_row, dtb_row):
    lane = lax.broadcasted_iota(jnp.int32, (1, LANES), 1)
    is_a = lane < DN_HEADS
    is_b = jnp.logical_and(lane >= DN_HEADS, lane < 2 * DN_HEADS)
    a_arg = jnp.where(is_a, blk + dtb_row, 0.0)
    neg_exp = jnp.where(is_a, -jnp.exp(alog_row), 0.0)
    log_a = neg_exp * _softplus(a_arg)
    beta = jnp.where(is_b, _sigmoid(blk), 0.0)
    return is_a, is_b, a_arg, neg_exp, log_a, beta


def _dn_ab_fwd(p, alog_row, dtb_row, *, name):
    T = p.shape[0]
    rows = min(DN_TB, T)

    def body(p_ref, al_ref, dt_ref, o_ref):
        _, _, _, _, log_a, beta = _dn_ab_parts(p_ref[...], al_ref[...], dt_ref[...])
        hi, mid, lo_ = _split3(log_a)
        tri = _dn_chunk_tri(rows, upper=False)
        f = lambda q: jnp.dot(tri, q, preferred_element_type=F32)
        o_ref[...] = (f(hi) + f(mid) + f(lo_)) + beta

    blk = pl.BlockSpec((rows, LANES), lambda i: (i, DN_AB_COL))
    vec = pl.BlockSpec((1, LANES), lambda i: (0, 0))
    return pl.pallas_call(
        body, name=name, grid=(T // rows,), in_specs=[blk, vec, vec],
        out_specs=pl.BlockSpec((rows, LANES), lambda i: (i, 0)),
        out_shape=jax.ShapeDtypeStruct((T, LANES), F32),
        compiler_params=pltpu.CompilerParams(dimension_semantics=("parallel",)),
    )(p, alog_row, dtb_row)


def _hp_l(a_l, b_l, dims=_NN):
    sa = [_split3(a)[:2] for a in a_l]
    sb = [_split3(b)[:2] for b in b_l]
    f = lambda p, q: lax.dot_general(p, q, dims, preferred_element_type=F32)
    hh = [f(x[0], y[0]) for x, y in zip(sa, sb)]
    hm = [f(x[0], y[1]) for x, y in zip(sa, sb)]
    mh = [f(x[1], y[0]) for x, y in zip(sa, sb)]
    return [a + (b + c) for a, b, c in zip(hh, hm, mh)]


def _dn_local(qs, k, v, g, beta, nc, inv_l=None):
    c = DN_CHUNK
    cut = lambda x: [x[i * c:(i + 1) * c] for i in range(nc)]
    row = lax.broadcasted_iota(jnp.int32, (c, c), 0)
    col = lax.broadcasted_iota(jnp.int32, (c, c), 1)
    eye, lower, strict = row == col, row >= col, row > col
    rowid = lax.broadcasted_iota(jnp.int32, (c, 1), 0)
    eg = jnp.exp(g)
    kb = k * beta
    rhs_k = kb * eg
    g_l, k_l, kb_l, qs_l = cut(g), cut(k), cut(kb), cut(qs)
    g_row_l = [jnp.sum(jnp.where(eye, x, 0.0), axis=0, keepdims=True) for x in g_l]
    dec_l = [jnp.where(lower, jnp.exp(jnp.where(lower, x - y, 0.0)), 0.0) for x, y in zip(g_l, g_row_l)]
    kk_l = [_dot_nt(a, b) for a, b in zip(kb_l, k_l)]
    qk_l = [_dot_nt(a, b) for a, b in zip(qs_l, k_l)]
    low_l = [jnp.where(strict, a * d, 0.0) for a, d in zip(kk_l, dec_l)]
    if inv_l is None:
        pw_l = [-x for x in low_l]
        inv_l = [eye.astype(F32) + x for x in pw_l]
        plain = lambda a_l, b_l: [_dot(a, b) for a, b in zip(a_l, b_l)]
        for level in range(int(math.log2(c)) - 1):
            mul = _hp_l if level < DN_INV_EXACT_LEVELS else plain
            pw_l = mul(pw_l, pw_l)
            inv_l = [a + b for a, b in zip(inv_l, mul(inv_l, pw_l))]
    u_l = [_dot(a, b) for a, b in zip(inv_l, cut(v * beta))]
    w_l = [_dot(a, b) for a, b in zip(inv_l, cut(rhs_k))]
    aqk_l = [jnp.where(lower, a * d, 0.0) for a, d in zip(qk_l, dec_l)]
    g_last_l = [jnp.sum(jnp.where(rowid == c - 1, x, 0.0), axis=0, keepdims=True) for x in g_l]
    ekd_l = [jnp.exp(a - b) for a, b in zip(g_last_l, g_l)]
    kd_l = [a * b for a, b in zip(k_l, ekd_l)]
    qd_l = cut(qs * eg)
    kw_l = [_dot_tn(a, b) for a, b in zip(kd_l, w_l)]
    qp_l = [q - _dot(a, w) for q, a, w in zip(qd_l, aqk_l, w_l)]
    return dict(eye=eye, lower=lower, strict=strict, dec=dec_l, k=k_l, kb=kb_l, qs=qs_l, low=low_l, inv=inv_l,
                eg=cut(eg), rhs_k=cut(rhs_k), u=u_l, w=w_l, aqk=aqk_l, g_last=g_last_l, qd=qd_l,
                ekd=ekd_l, kd=kd_l, kw=kw_l, qp=qp_l)


def _dn_head_cols(gb_blk, head):
    lane = lax.broadcasted_iota(jnp.int32, (1, LANES), 1)
    g = jnp.sum(jnp.where(lane == head, gb_blk, 0.0), axis=-1, keepdims=True)
    beta = jnp.sum(jnp.where(lane == head + DN_HEADS, gb_blk, 0.0), axis=-1, keepdims=True)
    return g, beta


def _halves_over_ici(s_refs, o_refs, send_sems, recv_sems, first, last):
    x, y, c = _mesh_pos()
    me = 2 * x + y
    chips = _other_chips(x, y)
    pairs = [(a, k) for a in range(len(s_refs)) for k in range(3)]

    def copy(a, k, slot):
        px, py = chips[k]
        return pltpu.make_async_remote_copy(
            src_ref=s_refs[a].at[c], dst_ref=o_refs[a].at[slot, c], send_sem=send_sems.at[3 * a + k],
            recv_sem=recv_sems.at[3 * a + k], device_id=(px, py, c), device_id_type=MESH)

    @pl.when(first)
    def _():
        for a, k in pairs:
            copy(a, k, me).start()

    @pl.when(last)
    def _():
        for a, k in pairs:
            px, py = chips[k]
            copy(a, k, 2 * px + py).wait_recv()
        for a, k in pairs:
            copy(a, k, me).wait_send()


def _dn_delta_fwd(qkv, gb, p, o_gain, *, name, send=()):
    T = qkv.shape[0]
    tb = min(DN_TB, T)
    nb, nc = T // tb, tb // DN_CHUNK
    H = DN_HEADS
    qscale = DN_DK ** -0.5
    ns = len(send)
    hp = DN_HEADS_FWD

    def body(*refs):
        q_ref, k_ref, v_ref, gb_ref, gate_ref, gain_ref = refs[:6]
        o_ref, og_ref, st_ref, inv_ref = refs[6 + ns:10 + ns]
        s_ref = refs[10 + 2 * ns]
        pair, blk = pl.program_id(0), pl.program_id(1)
        if ns:
            _halves_over_ici(refs[6:6 + ns], refs[10 + ns:10 + 2 * ns], refs[11 + 2 * ns], refs[12 + 2 * ns],
                             jnp.logical_and(pair == 0, blk == 0),
                             jnp.logical_and(pair == H // hp - 1, blk == nb - 1))

        @pl.when(blk == 0)
        def _():
            s_ref[...] = jnp.zeros_like(s_ref)

        gbv = gb_ref[...]
        ts, ku, op = [], [], []
        for e in range(hp):
            qk_e, v_e = slice(e * DN_DK, (e + 1) * DN_DK), slice(e * DN_DV, (e + 1) * DN_DV)
            g, beta = _dn_head_cols(gbv, hp * pair + e)
            t = _dn_local(q_ref[:, qk_e] * qscale, k_ref[:, qk_e], v_ref[:, v_e], g, beta, nc)
            ts.append(t)
            for i in range(nc):
                inv_ref[e, i] = t["inv"][i]
            ku.append([_dot_tn(a, b) for a, b in zip(t["kd"], t["u"])])
            op.append([_dot(a, b) for a, b in zip(t["aqk"], t["u"])])
        s32 = [s_ref[e] for e in range(hp)]
        s_l = [[] for _ in range(hp)]
        for i in range(nc):
            sb = [_bf(x) for x in s32]
            for e in range(hp):
                st_ref[e, i] = sb[e]
                s_l[e].append(sb[e])
            prod = [_dot(ts[e]["kw"][i], sb[e]) for e in range(hp)]
            s32 = [s32[e] * jnp.exp(ts[e]["g_last"][i]) - prod[e] + ku[e][i] for e in range(hp)]
        for e in range(hp):
            s_ref[e] = s32[e]
        o = jnp.concatenate(
            [jnp.concatenate([_dot(qp, sb) + x for qp, sb, x in zip(ts[e]["qp"], s_l[e], op[e])], axis=0)
             for e in range(hp)], axis=1)
        o_ref[...] = o
        gain = gain_ref[...]
        for e in range(hp):
            v_e = slice(e * DN_DV, (e + 1) * DN_DV)
            oe = o[:, v_e]
            r = lax.rsqrt(jnp.mean(oe * oe, axis=-1, keepdims=True) + RMS_EPS)
            og_ref[:, v_e] = (((oe * r) * gain) * _silu(gate_ref[:, v_e])).astype(BF16)

    qk = lambda col0: pl.BlockSpec((tb, hp * DN_DK), lambda h, i: (i, col0 // (hp * DN_DK) + h))
    vblk = lambda col0: pl.BlockSpec((tb, hp * DN_DV), lambda h, i: (i, col0 // (hp * DN_DV) + h))
    return pl.pallas_call(
        body, name=name, grid=(H // hp, nb),
        in_specs=[qk(0), qk(DN_QK_W), vblk(2 * DN_QK_W), pl.BlockSpec((tb, LANES), lambda h, i: (i, 0)),
                  vblk(DN_CONV_W), pl.BlockSpec((1, DN_DV), lambda h, i: (0, 0))] + [HBM] * ns,
        out_specs=[vblk(0), vblk(0), pl.BlockSpec((hp, nc, DN_DK, DN_DV), lambda h, i: (h, i, 0, 0)),
                   pl.BlockSpec((hp, nc, DN_CHUNK, DN_CHUNK), lambda h, i: (h, i, 0, 0))] + [HBM] * ns,
        out_shape=[jax.ShapeDtypeStruct((T, DN_V_W), F32), jax.ShapeDtypeStruct((T, DN_V_W), BF16),
                   jax.ShapeDtypeStruct((H, T // DN_CHUNK, DN_DK, DN_DV), BF16),
                   jax.ShapeDtypeStruct((H, T // DN_CHUNK, DN_CHUNK, DN_CHUNK), F32)]
        + [jax.ShapeDtypeStruct((N_CHIPS,) + a.shape, a.dtype) for a in send],
        scratch_shapes=[pltpu.VMEM((hp, DN_DK, DN_DV), F32)]
        + ([pltpu.SemaphoreType.DMA((3 * ns,)), pltpu.SemaphoreType.DMA((3 * ns,))] if ns else []),
        compiler_params=pltpu.CompilerParams(dimension_semantics=("arbitrary", "arbitrary")),
    )(qkv, qkv, qkv, gb, p, o_gain, *send)


def _blocks_over_ici(p_refs, o_refs, send_sems, recv_sems, first, last):
    x, y, c = _mesh_pos()
    me = 2 * x + y
    chips = _other_chips(x, y)
    pairs = [(a, k) for a in range(len(p_refs)) for k in range(3)]

    def copy(a, k, slot):
        px, py = chips[k]
        return pltpu.make_async_remote_copy(
            src_ref=p_refs[a].at[2 * px + py], dst_ref=o_refs[a].at[slot], send_sem=send_sems.at[3 * a + k],
            recv_sem=recv_sems.at[3 * a + k], device_id=(px, py, c), device_id_type=MESH)

    @pl.when(first)
    def _():
        for a, k in pairs:
            copy(a, k, me).start()

    @pl.when(last)
    def _():
        for a, k in pairs:
            px, py = chips[k]
            copy(a, k, 2 * px + py).wait_recv()
        for a, k in pairs:
            copy(a, k, me).wait_send()


def _dn_delta_bwd(qkv, gb, p, o_gain, o, states, invs, dog, *, name, send=()):
    T = qkv.shape[0]
    tb = min(DN_TB, T)
    nb, nc = T // tb, tb // DN_CHUNK
    H = DN_HEADS
    qscale = DN_DK ** -0.5
    ns = len(send)
    hp = DN_HEADS_BWD

    def body(*refs):
        q_ref, k_ref, v_ref, gb_ref, gate_ref, gain_ref, o_ref, st_ref, inv_ref, dog_ref = refs[:10]
        dq_ref, dk_ref, dv_ref, dgate_ref, dgb_ref, dgain_ref = refs[10 + ns:16 + ns]
        ds_ref = refs[16 + 2 * ns]
        pair, blk = pl.program_id(0), pl.program_id(1)
        first = jnp.logical_and(pair == 0, blk == 0)
        if ns:
            _blocks_over_ici(refs[10:10 + ns], refs[16 + ns:16 + 2 * ns], refs[17 + 2 * ns], refs[18 + 2 * ns],
                             first, jnp.logical_and(pair == H // hp - 1, blk == nb - 1))

        @pl.when(blk == 0)
        def _():
            ds_ref[...] = jnp.zeros_like(ds_ref)

        @pl.when(first)
        def _():
            dgain_ref[...] = jnp.zeros_like(dgain_ref)

        lane = lax.broadcasted_iota(jnp.int32, (1, LANES), 1)
        c = DN_CHUNK
        cut = lambda x: [x[i * c:(i + 1) * c] for i in range(nc)]
        cat = lambda xs: jnp.concatenate(xs, axis=0)
        rsum = lambda x: jnp.sum(x, axis=-1, keepdims=True)
        gbv, gain = gb_ref[...], gain_ref[...]

        def before_chain(e):
            qk_e, v_e = slice(e * DN_DK, (e + 1) * DN_DK), slice(e * DN_DV, (e + 1) * DN_DV)
            g, beta = _dn_head_cols(gbv, hp * pair + e)
            ov, gate, dogv = o_ref[:, v_e], gate_ref[:, v_e], dog_ref[:, v_e]
            r = lax.rsqrt(jnp.mean(ov * ov, axis=-1, keepdims=True) + RMS_EPS)
            oh = ov * r
            dnrm = dogv * _silu(gate)
            dgate_ref[:, v_e] = dogv * (oh * gain) * _dsilu(gate)
            doh = dnrm * gain
            do_l = cut(r * (doh - oh * jnp.mean(doh * oh, axis=-1, keepdims=True)))
            dgain_ref[...] += jnp.sum(dnrm * oh, axis=0, keepdims=True)
            k, v = k_ref[:, qk_e], v_ref[:, v_e]
            t = _dn_local(q_ref[:, qk_e] * qscale, k, v, g, beta, nc, [inv_ref[e, i] for i in range(nc)])
            s_l = [st_ref[e, i] for i in range(nc)]
            vn_l = [u - _dot(w, sb) for u, w, sb in zip(t["u"], t["w"], s_l)]
            return dict(
                t=t, beta=beta, v=v, s=s_l, vn=vn_l, egl=[jnp.exp(x) for x in t["g_last"]],
                dqd=[_dot_nt(a, sb) for a, sb in zip(do_l, s_l)], daqk=[_dot_nt(a, b) for a, b in zip(do_l, vn_l)],
                aqk_do=[_dot_tn(a, b) for a, b in zip(t["aqk"], do_l)],
                qp_do=[_dot_tn(a, b) for a, b in zip(t["qp"], do_l)])

        hs = [before_chain(e) for e in range(hp)]
        ds = [ds_ref[e] for e in range(hp)]
        ds_l = [[None] * nc for _ in range(hp)]
        for i in reversed(range(nc)):
            for e in range(hp):
                ds_l[e][i] = ds[e]
            prod = [_dot_tn(hs[e]["t"]["kw"][i], ds[e]) for e in range(hp)]
            ds = [ds[e] * hs[e]["egl"][i] - prod[e] + hs[e]["qp_do"][i] for e in range(hp)]
        for e in range(hp):
            ds_ref[e] = ds[e]

        def after_chain(e):
            hd, t = hs[e], hs[e]["t"]
            lower, strict, eye = t["lower"], t["strict"], t["eye"]
            s_l, vn_l, dqd_l, daqk_l, egl_l, beta, v = (hd["s"], hd["vn"], hd["dqd"], hd["daqk"], hd["egl"],
                                                         hd["beta"], hd["v"])
            dvn_l = [a + _dot(kd, d) for a, kd, d in zip(hd["aqk_do"], t["kd"], ds_l[e])]
            dkd_l = [_dot_nt(a, d) for a, d in zip(vn_l, ds_l[e])]
            dgl_l = [jnp.sum(rsum(d * sb.astype(F32)), axis=0, keepdims=True) * x
                     for d, sb, x in zip(ds_l[e], s_l, egl_l)]
            dw_l = [-_dot_nt(a, sb) for a, sb in zip(dvn_l, s_l)]
            dbv_l = [_dot_tn(a, b) for a, b in zip(t["inv"], dvn_l)]
            dbk_l = [_dot_tn(a, b) for a, b in zip(t["inv"], dw_l)]
            dlow_l = [-(_dot_nt(a, b) + _dot_nt(x, y)) for a, b, x, y in zip(dbv_l, t["u"], dbk_l, t["w"])]
            m_l = [jnp.where(strict, a * d, 0.0) for a, d in zip(dlow_l, t["dec"])]
            nmat_l = [jnp.where(lower, a * d, 0.0) for a, d in zip(daqk_l, t["dec"])]
            dkb_l = [_dot(m, kk) + b * x for m, kk, b, x in zip(m_l, t["k"], dbk_l, t["eg"])]
            dqs_l = [_dot(n, kk) + a * x for n, kk, a, x in zip(nmat_l, t["k"], dqd_l, t["eg"])]
            dk1_l = [_dot_tn(m, kb) for m, kb in zip(m_l, t["kb"])]
            dk2_l = [_dot_tn(n, q) for n, q in zip(nmat_l, t["qs"])]
            beta_l, v_l = cut(beta), cut(v)
            rowid = lax.broadcasted_iota(jnp.int32, (c, 1), 0)
            dk_l, dg_l, dbeta_l = [], [], []
            for i in range(nc):
                dk_l.append(dk1_l[i] + dk2_l[i] + dkd_l[i] * t["ekd"][i] + dkb_l[i] * beta_l[i])
                gmat = jnp.where(strict, dlow_l[i] * t["low"][i], 0.0) + daqk_l[i] * t["aqk"][i]
                s_kd = rsum(dkd_l[i] * t["kd"][i])
                dg = (rsum(gmat) + rsum(dqd_l[i] * t["qd"][i]) - s_kd + rsum(dbk_l[i] * t["rhs_k"][i]))
                dg_row = -jnp.sum(gmat, axis=0, keepdims=True)
                dg = dg + rsum(jnp.where(eye, dg_row, 0.0))
                dgl = dgl_l[i] + jnp.sum(s_kd, axis=0, keepdims=True)
                dg_l.append(dg + jnp.where(rowid == c - 1, dgl, 0.0))
                dbeta_l.append(rsum(dbv_l[i] * v_l[i]) + rsum(dkb_l[i] * t["k"][i]))
            head = hp * pair + e
            dgb = (jnp.where(lane == head, cat(dg_l), 0.0) + jnp.where(lane == head + DN_HEADS, cat(dbeta_l), 0.0))
            return cat(dqs_l) * qscale, cat(dk_l), cat(dbv_l) * beta, dgb

        for e in range(hp):
            dq, dk, dv, dgb = after_chain(e)
            dq_ref[:, e * DN_DK:(e + 1) * DN_DK] = dq
            dk_ref[:, e * DN_DK:(e + 1) * DN_DK] = dk
            dv_ref[:, e * DN_DV:(e + 1) * DN_DV] = dv
            dgb_ref[e] = dgb

    rev = lambda i: nb - 1 - i
    qk = lambda col0: pl.BlockSpec((tb, hp * DN_DK), lambda h, i: (rev(i), col0 // (hp * DN_DK) + h))
    vblk = lambda col0: pl.BlockSpec((tb, hp * DN_DV), lambda h, i: (rev(i), col0 // (hp * DN_DV) + h))
    gain_spec = pl.BlockSpec((1, DN_DV), lambda h, i: (0, 0))
    return pl.pallas_call(
        body, name=name, grid=(H // hp, nb),
        in_specs=[qk(0), qk(DN_QK_W), vblk(2 * DN_QK_W), pl.BlockSpec((tb, LANES), lambda h, i: (rev(i), 0)),
                  vblk(DN_CONV_W), gain_spec, vblk(0),
                  pl.BlockSpec((hp, nc, DN_DK, DN_DV), lambda h, i: (h, rev(i), 0, 0)),
                  pl.BlockSpec((hp, nc, DN_CHUNK, DN_CHUNK), lambda h, i: (h, rev(i), 0, 0)), vblk(0)] + [HBM] * ns,
        out_specs=[qk(0), qk(0), vblk(0), vblk(DN_CONV_W),
                   pl.BlockSpec((hp, tb, LANES), lambda h, i: (h, rev(i), 0)), gain_spec] + [HBM] * ns,
        out_shape=[jax.ShapeDtypeStruct((T, DN_QK_W), F32), jax.ShapeDtypeStruct((T, DN_QK_W), F32),
                   jax.ShapeDtypeStruct((T, DN_V_W), F32), jax.ShapeDtypeStruct((T, DN_IN_PAD), F32),
                   jax.ShapeDtypeStruct((H, T, LANES), F32), jax.ShapeDtypeStruct((1, DN_DV), F32)]
        + [jax.ShapeDtypeStruct(a.shape, a.dtype) for a in send],
        scratch_shapes=[pltpu.VMEM((hp, DN_DK, DN_DV), F32)]
        + ([pltpu.SemaphoreType.DMA((3 * ns,)), pltpu.SemaphoreType.DMA((3 * ns,))] if ns else []),
        compiler_params=pltpu.CompilerParams(dimension_semantics=("arbitrary", "arbitrary")),
    )(qkv, qkv, qkv, gb, p, o_gain, o, states, invs, dog, *send)


def _dn_conv_bwd(p, conv_w, d, dp, *, first, normed, name):
    T, width = d.shape

    def body(p_ref, w_ref, d_ref, dp_in, dp_ref, dw_ref):
        del dp_in
        x = p_ref[...]
        ksz = w_ref.shape[0]
        xs = [_shift_down(x, ksz - 1 - i) for i in range(ksz)]
        xc = sum(w_ref[i:i + 1, :] * xs[i] for i in range(ksz))
        ds = d_ref[...]
        if normed:
            s = _silu(xc)
            r = lax.rsqrt(jnp.sum(s * s, axis=-1, keepdims=True) + L2_EPS)
            y = s * r
            ds = r * (ds - y * jnp.sum(ds * y, axis=-1, keepdims=True))
        dxc = ds * _dsilu(xc)
        dp_ref[...] = sum(w_ref[i:i + 1, :] * _shift_up(dxc, ksz - 1 - i) for i in range(ksz))
        for i in range(ksz):
            dw_ref[i:i + 1, :] = jnp.sum(dxc * xs[i], axis=0, keepdims=True)

    shifted = pl.BlockSpec((T, LANES), lambda j: (0, first + j))
    return pl.pallas_call(
        body, name=name, grid=(width // LANES,),
        in_specs=[shifted, pl.BlockSpec((DN_CONV, LANES), lambda j: (0, first + j)),
                  pl.BlockSpec((T, LANES), lambda j: (0, j)), pl.BlockSpec(memory_space=pl.ANY)],
        out_specs=[shifted, pl.BlockSpec((DN_CONV, LANES), lambda j: (0, j))],
        out_shape=[jax.ShapeDtypeStruct(dp.shape, F32), jax.ShapeDtypeStruct((DN_CONV, width), F32)],
        input_output_aliases={3: 0},
        compiler_params=pltpu.CompilerParams(dimension_semantics=("parallel",), vmem_limit_bytes=VMEM_BIG),
    )(p, conv_w, d, dp)


def _dn_ab_bwd(p, alog_row, dtb_row, dgb, dp, *, name):
    T = p.shape[0]
    rows = min(DN_TB, T)
    H = DN_HEADS

    def body(p_ref, al_ref, dt_ref, dgb_ref, dp_in, dp_ref, dal_ref, ddt_ref):
        del dp_in

        @pl.when(pl.program_id(0) == 0)
        def _():
            dal_ref[...] = jnp.zeros_like(dal_ref)
            ddt_ref[...] = jnp.zeros_like(ddt_ref)

        blk = p_ref[...]
        is_a, is_b, a_arg, neg_exp, log_a, beta = _dn_ab_parts(blk, al_ref[...], dt_ref[...])
        d = dgb_ref[0]
        for hh in range(1, H):
            d = d + dgb_ref[hh]
        hi, mid, lo_ = _split3(jnp.where(is_a, d, 0.0))
        tri = _dn_chunk_tri(rows, upper=True)
        f = lambda q: jnp.dot(tri, q, preferred_element_type=F32)
        dlog_a = f(hi) + f(mid) + f(lo_)
        da_in = dlog_a * neg_exp * _sigmoid(a_arg)
        db_in = jnp.where(is_b, d, 0.0) * beta * (1.0 - beta)
        dp_ref[...] = jnp.where(is_a, da_in, 0.0) + db_in
        dal_ref[...] += jnp.sum(dlog_a * log_a, axis=0, keepdims=True)
        ddt_ref[...] += jnp.sum(jnp.where(is_a, da_in, 0.0), axis=0, keepdims=True)

    blk = pl.BlockSpec((rows, LANES), lambda i: (i, DN_AB_COL))
    vec = pl.BlockSpec((1, LANES), lambda i: (0, 0))
    return pl.pallas_call(
        body, name=name, grid=(T // rows,),
        in_specs=[blk, vec, vec, pl.BlockSpec((H, rows, LANES), lambda i: (0, i, 0)),
                  pl.BlockSpec(memory_space=pl.ANY)],
        out_specs=[blk, vec, vec],
        out_shape=[jax.ShapeDtypeStruct(dp.shape, F32), jax.ShapeDtypeStruct((1, LANES), F32),
                   jax.ShapeDtypeStruct((1, LANES), F32)],
        input_output_aliases={4: 0},
        compiler_params=pltpu.CompilerParams(dimension_semantics=("arbitrary",)),
    )(p, alog_row, dtb_row, dgb, dp)


def _dn_layer_fwd(x, ng, w_in, conv_w, a_log, dt_bias, o_gain, w_out, tag, send=()):
    alog_row, dtb_row = _dn_lane_rows(a_log, dt_bias)
    gain = o_gain.reshape(1, DN_DV)
    h = _rmsnorm_fwd(x, ng, name=f"{tag}_norm")
    p = _matmul(h, w_in, mode="nn", name=f"{tag}_inproj")
    qkv = _dn_prep_fwd(p, conv_w, name=f"{tag}_prep")
    gb = _dn_ab_fwd(p, alog_row, dtb_row, name=f"{tag}_ab")
    o, og, states, invs, *landed = _dn_delta_fwd(qkv, gb, p, gain, name=f"{tag}_delta", send=send)
    x_new = _matmul(og, w_out, mode="nn", res=x, name=f"{tag}_outproj")
    return x_new, (h, p, qkv, gb, o, og, states, invs), landed


def _dn_layer_bwd(dx, x, ng, w_in, conv_w, a_log, dt_bias, o_gain, w_out, saved, tag, send=(), chip_sums=None):
    h, p, qkv, gb, o, og, states, invs = saved
    alog_row, dtb_row = _dn_lane_rows(a_log, dt_bias)
    gain = o_gain.reshape(1, DN_DV)
    d_wout = _matmul(og, dx, mode="tn", out_dtype=BF16, name=f"{tag}_dwout")
    if chip_sums is not None:
        d_wout, = chip_sums([_cut2(_by_rows(d_wout))], f"{tag}wout")
        send = list(send) + [d_wout]
    dog = _matmul(dx, w_out, mode="nt", name=f"{tag}_dog")
    dq, dk, dv, dp, dgb, dgain, *landed = _dn_delta_bwd(qkv, gb, p, gain, o, states, invs, dog,
                                                        name=f"{tag}_deltabwd", send=send)
    n_qk = DN_QK_W // LANES
    dp, dconv_q = _dn_conv_bwd(p, conv_w, dq, dp, first=0, normed=True, name=f"{tag}_convbwd_q")
    dp, dconv_k = _dn_conv_bwd(p, conv_w, dk, dp, first=n_qk, normed=True, name=f"{tag}_convbwd_k")
    dp, dconv_v = _dn_conv_bwd(p, conv_w, dv, dp, first=2 * n_qk, normed=False, name=f"{tag}_convbwd_v")
    dconv = jnp.concatenate([dconv_q, dconv_k, dconv_v], axis=1)
    dp, dal, ddt = _dn_ab_bwd(p, alog_row, dtb_row, dgb, dp, name=f"{tag}_abbwd")
    d_win = _matmul(h, dp, mode="tn", name=f"{tag}_dwin")
    if chip_sums is not None:
        d_win, = chip_sums([_cut2(_by_cols(d_win))], f"{tag}win")
        dh, landed_win = _matmul(dp, w_in, mode="nt", name=f"{tag}_dh", send=[d_win])
        landed = landed + [landed_win]
    else:
        dh = _matmul(dp, w_in, mode="nt", name=f"{tag}_dh")
    dx_prev, dng = _rmsnorm_bwd(x, ng, dh, dx, name=f"{tag}_normbwd")
    return dx_prev, dng, d_win, dconv, dal[0, :DN_HEADS], ddt[0, :DN_HEADS], dgain[0], d_wout, landed


def _by_cols(dw):
    return _split(dw[:, :DN_IN].astype(BF16), 1)


def _by_rows(dw):
    return dw.reshape(N_CHIPS, -1, dw.shape[-1])


def _cut2(g4):
    return g4.reshape(N_CHIPS, 2, -1, g4.shape[-1])


def _sb_gains(g):
    return jnp.concatenate([g, g]).reshape(1, LANES)


def _sb_layer_fwd(x, ng, w_in, gq, gk, w_out, tag, send=()):
    h = _rmsnorm_fwd(x, ng, name=f"{tag}_norm")
    p3 = _matmul(h, w_in, mode="nn", b_parts=4, out_parts=4, name=f"{tag}_inproj")
    og, o, lsum, live, *landed = _sb_attn_fwd(p3, _sb_gains(gq), _sb_gains(gk), name=f"{tag}_attn", send=send)
    x_new = _matmul(og, w_out, mode="nn", res=x, name=f"{tag}_outproj")
    return x_new, (h, p3, og, o, lsum, live), landed


def _sb_layer_bwd(dx, x, ng, w_in, gq, gk, w_out, saved, tag):
    h, p3, og, o, lsum, live = saved
    d_wout = _matmul(og, dx, mode="tn", out_dtype=BF16, name=f"{tag}_dwout")
    dog = _matmul(dx, w_out, mode="nt", name=f"{tag}_dog")
    dp3, dgq, dgk = _sb_attn_bwd(p3, _sb_gains(gq), _sb_gains(gk), o, lsum, live, dog, name=f"{tag}_attnbwd")
    fold = lambda d: jnp.sum(d.reshape(-1, SB_DH), axis=0)
    d_win = _matmul(h, dp3, mode="tn", b_parts=4, out_parts=4, out_dtype=BF16, name=f"{tag}_dwin")
    dh = _matmul(dp3, w_in, mode="nt", a_parts=4, b_parts=4, name=f"{tag}_dh")
    dx_prev, dng = _rmsnorm_bwd(x, ng, dh, dx, name=f"{tag}_normbwd")
    return dx_prev, dng, d_win, fold(dgq), fold(dgk), d_wout


N_CHIPS = 4
HBM = pl.BlockSpec(memory_space=pl.ANY)


def _mesh_pos():
    return lax.axis_index("x"), lax.axis_index("y"), lax.axis_index("c")


def _other_chips(x, y):
    return [(1 - x, y), (x, 1 - y), (1 - x, 1 - y)]


def _chip_exchange(srcs, *, send_slot_is_dest, copy_own, name):
    n = len(srcs)

    def body(*refs):
        src_refs, out_refs = refs[:n], refs[n:2 * n]
        send_sems, recv_sems, local_sems = refs[2 * n:]
        x, y, c = _mesh_pos()
        me = 2 * x + y
        chips = _other_chips(x, y)
        local = []
        for a in range(n):
            if not copy_own[a]:
                continue
            own = src_refs[a].at[me] if send_slot_is_dest else src_refs[a]
            local.append(pltpu.make_async_copy(own, out_refs[a].at[me], local_sems.at[a]))
        for cp in local:
            cp.start()

        def copy(a, k, landing_slot):
            px, py = chips[k]
            src = src_refs[a].at[2 * px + py] if send_slot_is_dest else src_refs[a]
            return pltpu.make_async_remote_copy(
                src_ref=src, dst_ref=out_refs[a].at[landing_slot],
                send_sem=send_sems.at[a * 3 + k], recv_sem=recv_sems.at[a * 3 + k],
                device_id=(px, py, c), device_id_type=MESH)

        sends = [copy(a, k, me) for a in range(n) for k in range(3)]
        for cp in sends:
            cp.start()
        for a in range(n):
            for k in range(3):
                px, py = chips[k]
                copy(a, k, 2 * px + py).wait_recv()
        for cp in sends:
            cp.wait_send()
        for cp in local:
            cp.wait()

    outs = []
    for s in srcs:
        shape = s.shape if send_slot_is_dest else (N_CHIPS,) + s.shape
        outs.append(jax.ShapeDtypeStruct(shape, s.dtype))
    return pl.pallas_call(
        body, name=name, in_specs=[HBM] * n, out_specs=[HBM] * n, out_shape=outs,
        scratch_shapes=[pltpu.SemaphoreType.DMA((3 * n,)), pltpu.SemaphoreType.DMA((3 * n,)),
                        pltpu.SemaphoreType.DMA((n,))],
    )(*srcs)


def _sibling_exchange(srcs, *, name):
    n = len(srcs)

    def body(*refs):
        src_refs, out_refs = refs[:n], refs[n:2 * n]
        send_sems, recv_sems = refs[2 * n:]
        x, y, c = _mesh_pos()
        copies = [pltpu.make_async_remote_copy(
            src_ref=src_refs[a], dst_ref=out_refs[a], send_sem=send_sems.at[a], recv_sem=recv_sems.at[a],
            device_id=(x, y, 1 - c), device_id_type=MESH) for a in range(n)]
        for cp in copies:
            cp.start()
        for cp in copies:
            cp.wait()

    return pl.pallas_call(
        body, name=name, in_specs=[HBM] * n, out_specs=[HBM] * n,
        out_shape=[jax.ShapeDtypeStruct(s.shape, s.dtype) for s in srcs],
        scratch_shapes=[pltpu.SemaphoreType.DMA((n,)), pltpu.SemaphoreType.DMA((n,))],
    )(*srcs)


def _gather_halves(shards, small, *, name):
    n = len(shards)

    def body(*refs):
        s_refs, small_ref = refs[:n], refs[n]
        o_refs, osmall_ref = refs[n + 1:2 * n + 1], refs[2 * n + 1]
        send_sems, recv_sems, local_sems = refs[2 * n + 2:]
        x, y, c = _mesh_pos()
        me = 2 * x + y
        chips = _other_chips(x, y)
        local = [pltpu.make_async_copy(small_ref, osmall_ref.at[me], local_sems.at[0])]
        for cp in local:
            cp.start()

        def over_ici(a, k, slot):
            px, py = chips[k]
            return pltpu.make_async_remote_copy(
                src_ref=s_refs[a].at[c], dst_ref=o_refs[a].at[slot, c], send_sem=send_sems.at[3 * a + k],
                recv_sem=recv_sems.at[3 * a + k], device_id=(px, py, c), device_id_type=MESH)

        def small_copy(k, slot):
            px, py = chips[k]
            return pltpu.make_async_remote_copy(
                src_ref=small_ref, dst_ref=osmall_ref.at[slot], send_sem=send_sems.at[3 * n + k],
                recv_sem=recv_sems.at[3 * n + k], device_id=(px, py, c), device_id_type=MESH)

        def to_sibling(a, k, half):
            px, py = chips[k]
            blk = o_refs[a].at[2 * px + py, half]
            return pltpu.make_async_remote_copy(
                src_ref=blk, dst_ref=blk, send_sem=send_sems.at[3 * n + 3 + 3 * a + k],
                recv_sem=recv_sems.at[3 * n + 3 + 3 * a + k], device_id=(x, y, 1 - c), device_id_type=MESH)

        sends = [over_ici(a, k, me) for a in range(n) for k in range(3)] + [small_copy(k, me) for k in range(3)]
        for cp in sends:
            cp.start()
        passed = []
        for a in range(n):
            for k in range(3):
                px, py = chips[k]
                over_ici(a, k, 2 * px + py).wait_recv()
                passed.append(to_sibling(a, k, c))
                passed[-1].start()
        for k in range(3):
            px, py = chips[k]
            small_copy(k, 2 * px + py).wait_recv()
        for a in range(n):
            for k in range(3):
                to_sibling(a, k, 1 - c).wait_recv()
        for cp in sends + passed:
            cp.wait_send()
        for cp in local:
            cp.wait()

    nsem = 6 * n + 3
    return pl.pallas_call(
        body, name=name, in_specs=[HBM] * (n + 1), out_specs=[HBM] * (n + 1),
        out_shape=[jax.ShapeDtypeStruct((N_CHIPS,) + s.shape, s.dtype) for s in shards + [small]],
        scratch_shapes=[pltpu.SemaphoreType.DMA((nsem,)), pltpu.SemaphoreType.DMA((nsem,)),
                        pltpu.SemaphoreType.DMA((1,))],
    )(*shards, small)


def _forward_halves(landed, *, name):
    n = len(landed)

    def body(*refs):
        o_refs = refs[n:2 * n]
        send_sems, recv_sems = refs[2 * n:]
        x, y, c = _mesh_pos()
        chips = _other_chips(x, y)
        pairs = [(a, k) for a in range(n) for k in range(3)]

        def copy(a, k, half):
            px, py = chips[k]
            blk = o_refs[a].at[2 * px + py, half]
            return pltpu.make_async_remote_copy(
                src_ref=blk, dst_ref=blk, send_sem=send_sems.at[3 * a + k], recv_sem=recv_sems.at[3 * a + k],
                device_id=(x, y, 1 - c), device_id_type=MESH)

        sends = [copy(a, k, c) for a, k in pairs]
        for cp in sends:
            cp.start()
        for a, k in pairs:
            copy(a, k, 1 - c).wait_recv()
        for cp in sends:
            cp.wait_send()

    return pl.pallas_call(
        body, name=name, in_specs=[HBM] * n, out_specs=[HBM] * n,
        out_shape=[jax.ShapeDtypeStruct(a.shape, a.dtype) for a in landed],
        input_output_aliases={a: a for a in range(n)},
        scratch_shapes=[pltpu.SemaphoreType.DMA((3 * n,)), pltpu.SemaphoreType.DMA((3 * n,))],
    )(*landed)


def _swap_other_half(g_list, *, name):
    n = len(g_list)

    def body(*refs):
        g_refs, o_refs = refs[:n], refs[n:2 * n]
        send_sems, recv_sems = refs[2 * n:]
        x, y, c = _mesh_pos()
        copies = [pltpu.make_async_remote_copy(
            src_ref=g_refs[a].at[:, 1 - c], dst_ref=o_refs[a], send_sem=send_sems.at[a], recv_sem=recv_sems.at[a],
            device_id=(x, y, 1 - c), device_id_type=MESH) for a in range(n)]
        for cp in copies:
            cp.start()
        for cp in copies:
            cp.wait()

    return pl.pallas_call(
        body, name=name, in_specs=[HBM] * n, out_specs=[HBM] * n,
        out_shape=[jax.ShapeDtypeStruct((g.shape[0],) + g.shape[2:], g.dtype) for g in g_list],
        scratch_shapes=[pltpu.SemaphoreType.DMA((n,)), pltpu.SemaphoreType.DMA((n,))],
    )(*g_list)


def _row_tile(r):
    return _pick(r, (512, 256, 128, 64, 32, 16, 8))


def _add_my_half(g4, sib4, core, *, name):
    n, _, r, C = g4.shape
    tr = _row_tile(r)

    def body(core_ref, g_ref, s_ref, o_ref):
        del core_ref
        o_ref[...] = (g_ref[...].astype(F32) + s_ref[...].astype(F32)).astype(o_ref.dtype)

    return pl.pallas_call(
        body, name=name,
        grid_spec=pltpu.PrefetchScalarGridSpec(
            num_scalar_prefetch=1, grid=(n, r // tr),
            in_specs=[pl.BlockSpec((None, None, tr, C), lambda j, i, core_ref: (j, core_ref[0], i, 0)),
                      pl.BlockSpec((None, tr, C), lambda j, i, core_ref: (j, i, 0))],
            out_specs=pl.BlockSpec((None, tr, C), lambda j, i, core_ref: (j, i, 0))),
        out_shape=jax.ShapeDtypeStruct((n, r, C), g4.dtype),
        compiler_params=pltpu.CompilerParams(dimension_semantics=("parallel", "parallel")),
    )(core, g4, sib4)


def _sum_chips(landed, part, me, *, name):
    _, r, C = landed.shape
    tr = _row_tile(r)

    def body(me_ref, own_ref, r1_ref, r2_ref, r3_ref, o_ref):
        del me_ref
        f = lambda ref: ref[...].astype(F32)
        o_ref[...] = ((f(own_ref) + f(r1_ref)) + f(r2_ref)) + f(r3_ref)

    slot = lambda d: pl.BlockSpec((None, tr, C), lambda i, me_ref: ((me_ref[0] + d) % N_CHIPS, i, 0))
    return pl.pallas_call(
        body, name=name,
        grid_spec=pltpu.PrefetchScalarGridSpec(
            num_scalar_prefetch=1, grid=(r // tr,), in_specs=[slot(0), slot(1), slot(2), slot(3)],
            out_specs=pl.BlockSpec((tr, C), lambda i, me_ref: (i, 0))),
        out_shape=jax.ShapeDtypeStruct((r, C), F32),
        compiler_params=pltpu.CompilerParams(dimension_semantics=("parallel",)),
    )(me, part, landed, landed, landed)


def _adamw_halves(w, mine, theirs, m, v, core, *, layer, prev, name):
    shape = w.shape
    r, C = mine.shape
    tr = _pick(r, (128, 64, 32, 16, 8))
    per = r // tr
    view = lambda a: a.reshape(-1, C)
    n_prev = 0 if prev is None else 4

    def body(*refs):
        core_ref, w_ref, gm_ref, gt_ref, m_ref, v_ref = refs[:6]
        g_ref, d_ref, nm_ref, nv_ref = refs[6 + n_prev:]
        gv = jnp.where(pl.program_id(0) == core_ref[0], gm_ref[...], gt_ref[...])
        g_ref[...] = gv
        d_ref[...], nm_ref[...], nv_ref[...] = _adamw_math(w_ref[...], gv, m_ref[...], v_ref[...])

    half = pl.BlockSpec((tr, C), lambda h, i, core_ref: ((2 * layer + h) * per + i, 0))
    row = pl.BlockSpec((tr, C), lambda h, i, core_ref: (i, 0))
    out = jax.ShapeDtypeStruct((math.prod(shape) // C, C), F32)
    res = pl.pallas_call(
        body, name=name,
        grid_spec=pltpu.PrefetchScalarGridSpec(
            num_scalar_prefetch=1, grid=(2, per), in_specs=[half, row, row, half, half] + [HBM] * n_prev,
            out_specs=[half] * 4),
        out_shape=[out] * 4,
        input_output_aliases={6 + j: j for j in range(n_prev)},
        compiler_params=pltpu.CompilerParams(dimension_semantics=("parallel", "parallel")),
    )(core, view(w), mine, theirs, view(m), view(v), *([] if prev is None else [view(a) for a in prev]))
    return tuple(a.reshape(shape) for a in res)


def _sum_small(recv4, *, name):
    _, R, C = recv4.shape

    def body(r_ref, o_ref):
        o_ref[...] = ((r_ref[0] + r_ref[1]) + r_ref[2]) + r_ref[3]

    return pl.pallas_call(body, name=name, out_shape=jax.ShapeDtypeStruct((R, C), F32))(recv4)


def _add(a, b, *, name):
    R, C = a.shape
    tr = _pick(R, (512, 256, 128, 64, 32, 16, 8))
    blk = pl.BlockSpec((tr, C), lambda i: (i, 0))

    def body(a_ref, b_ref, o_ref):
        o_ref[...] = a_ref[...] + b_ref[...]

    return pl.pallas_call(body, name=name, grid=(R // tr,), in_specs=[blk, blk], out_specs=blk,
                          out_shape=jax.ShapeDtypeStruct((R, C), F32),
                          compiler_params=pltpu.CompilerParams(dimension_semantics=("parallel",)))(a, b)


def _adamw_math(w, g, m, v):
    nm = ADAM_B1 * m + (1.0 - ADAM_B1) * g
    nv = ADAM_B2 * v + (1.0 - ADAM_B2) * (g * g)
    m_hat = nm / (1.0 - ADAM_B1 ** ADAM_STEP)
    v_hat = nv / (1.0 - ADAM_B2 ** ADAM_STEP)
    return -ADAM_LR * (m_hat / (jnp.sqrt(v_hat) + ADAM_EPS) + ADAM_WD * w), nm, nv


def _adamw(w, g, m, v, *, name):
    shape = w.shape
    C = shape[-1]
    R = w.size // C
    two = lambda a: a.reshape(R, C)
    tr = _pick(R, (256, 128, 64, 32, 16, 8)) if R % 8 == 0 and R > 8 else R
    blk = pl.BlockSpec((tr, C), lambda i: (i, 0))

    def body(w_ref, g_ref, m_ref, v_ref, d_ref, nm_ref, nv_ref):
        d_ref[...], nm_ref[...], nv_ref[...] = _adamw_math(w_ref[...], g_ref[...], m_ref[...], v_ref[...])

    out = jax.ShapeDtypeStruct((R, C), F32)
    d, nm, nv = pl.pallas_call(
        body, name=name, grid=(R // tr,), in_specs=[blk] * 4, out_specs=[blk] * 3, out_shape=[out] * 3,
        compiler_params=pltpu.CompilerParams(dimension_semantics=("parallel",)),
    )(two(w), two(g), two(m), two(v))
    return d.reshape(shape), nm.reshape(shape), nv.reshape(shape)


BIG = (("dn_w_in", (2, 1024, 1540), 2), ("dn_w_out", (2, 512, 1024), 1), ("sb_w_in", (1, 1024, 1024), 2),
       ("sb_w_out", (1, 256, 1024), 1), ("sc_w_in", (1, 1024, 2048), 2), ("sc_w_out", (1, 512, 1024), 1))
SMALL = (("dn_conv_w", (2, 4, 1024), 2), ("dn_o_norm_g", (2, 64), 1), ("sc_conv_w", (1, 3, 512), 2))
REPL = (("norm_g", (4, 1024)), ("dn_a_log", (2, 8)), ("dn_dt_bias", (2, 8)), ("sb_q_norm_g", (1, 64)),
        ("sb_k_norm_g", (1, 64)))


def _halves(shard):
    return shard.reshape(2, -1, shard.shape[-1])


def _pack(arrays, cols, lead=()):
    flat = jnp.concatenate([a.reshape(lead + (-1,)) for a in arrays], axis=-1)
    n = flat.shape[-1]
    rows = -(-n // cols)
    unit = 512 if rows > 512 else 8
    rows = -(-rows // unit) * unit
    flat = jnp.pad(flat, [(0, 0)] * len(lead) + [(0, rows * cols - n)])
    return flat.reshape(lead + (rows, cols))


def _unpack(buf, table, lead=()):
    flat = buf.reshape(lead + (-1,))
    out, off = {}, 0
    for entry in table:
        name, shape = entry[0], entry[1]
        n = math.prod(shape)
        out[name] = flat[..., off:off + n].reshape(lead + shape)
        off += n
    return out


def _join(shards, axis):
    return jnp.concatenate([shards[j] for j in range(N_CHIPS)], axis=axis)


def _split(full, axis):
    return jnp.stack(jnp.split(full, N_CHIPS, axis=axis), axis=0)


def kernel(x, norm_g, dn_w_in, dn_conv_w, dn_a_log, dn_dt_bias, dn_o_norm_g, dn_w_out, sb_w_in, sb_q_norm_g, sb_k_norm_g, sb_w_out, sc_w_in, sc_conv_w, sc_w_out, loss_target, m_norm_g, m_dn_w_in, m_dn_conv_w, m_dn_a_log, m_dn_dt_bias, m_dn_o_norm_g, m_dn_w_out, m_sb_w_in, m_sb_q_norm_g, m_sb_k_norm_g, m_sb_w_out, m_sc_w_in, m_sc_conv_w, m_sc_w_out, v_norm_g, v_dn_w_in, v_dn_conv_w, v_dn_a_log, v_dn_dt_bias, v_dn_o_norm_g, v_dn_w_out, v_sb_w_in, v_sb_q_norm_g, v_sb_k_norm_g, v_sb_w_out, v_sc_w_in, v_sc_conv_w, v_sc_w_out):
    weights = dict(norm_g=norm_g, dn_w_in=dn_w_in, dn_conv_w=dn_conv_w, dn_a_log=dn_a_log, dn_dt_bias=dn_dt_bias,
                   dn_o_norm_g=dn_o_norm_g, dn_w_out=dn_w_out, sb_w_in=sb_w_in, sb_q_norm_g=sb_q_norm_g,
                   sb_k_norm_g=sb_k_norm_g, sb_w_out=sb_w_out, sc_w_in=sc_w_in, sc_conv_w=sc_conv_w, sc_w_out=sc_w_out)
    m_in = dict(norm_g=m_norm_g, dn_w_in=m_dn_w_in, dn_conv_w=m_dn_conv_w, dn_a_log=m_dn_a_log,
                dn_dt_bias=m_dn_dt_bias, dn_o_norm_g=m_dn_o_norm_g, dn_w_out=m_dn_w_out, sb_w_in=m_sb_w_in,
                sb_q_norm_g=m_sb_q_norm_g, sb_k_norm_g=m_sb_k_norm_g, sb_w_out=m_sb_w_out, sc_w_in=m_sc_w_in,
                sc_conv_w=m_sc_conv_w, sc_w_out=m_sc_w_out)
    v_in = dict(norm_g=v_norm_g, dn_w_in=v_dn_w_in, dn_conv_w=v_dn_conv_w, dn_a_log=v_dn_a_log,
                dn_dt_bias=v_dn_dt_bias, dn_o_norm_g=v_dn_o_norm_g, dn_w_out=v_dn_w_out, sb_w_in=v_sb_w_in,
                sb_q_norm_g=v_sb_q_norm_g, sb_k_norm_g=v_sb_k_norm_g, sb_w_out=v_sb_w_out, sc_w_in=v_sc_w_in,
                sc_conv_w=v_sc_conv_w, sc_w_out=v_sc_w_out)
    order = list(weights)
    xi, yi, ci = _mesh_pos()

    small = _pack([weights[n] for n, _, _ in SMALL], LANES)
    later = [("dn_w_in", 1), ("dn_w_out", 1), ("sb_w_in", 0), ("sb_w_out", 0), ("sc_w_in", 0), ("sc_w_out", 0)]
    piece = lambda n, l: _halves(weights[n][l].astype(BF16)[None])
    own_first = [piece("dn_w_in", 0), piece("dn_w_out", 0)]
    own_later = [piece(n, l) for n, l in later]
    own_last, own_mid = own_later[:2], own_later[2:]
    me = 2 * xi + yi
    whole = lambda g4, own: lax.dynamic_update_index_in_dim(g4, own, me, 0)
    flat = lambda g4: g4.reshape(N_CHIPS, -1, g4.shape[-1])
    rows_of = lambda w4: w4.reshape(-1, w4.shape[-1])
    dn_in = lambda w4: jnp.pad(_join(w4, 1), ((0, 0), (0, DN_IN_PAD - DN_IN)))
    w_in0, w_out0, small4 = _gather_halves(own_first, small, name="gather_first")
    full = {n: _join(a, ax) for (n, _, ax), a in zip(SMALL, _unpack(small4, SMALL, (N_CHIPS,)).values())}

    def dn_args(j, w_in4, w_out4):
        return (dn_in(flat(w_in4)), full["dn_conv_w"][j], dn_a_log[j], dn_dt_bias[j], full["dn_o_norm_g"][j],
                rows_of(w_out4))

    x0 = x[0]
    dn0 = dn_args(0, whole(w_in0, own_first[0]), whole(w_out0, own_first[1]))
    x1, s0, landed = _dn_layer_fwd(x0, norm_g[0], *dn0, "l0", send=own_mid)
    landed = _forward_halves(landed, name="forward_halves_mid")
    sb_in, sb_out, sc_in, sc_out = [whole(g4, own) for g4, own in zip(landed, own_mid)]
    sb_args = (flat(sb_in), sb_q_norm_g[0], sb_k_norm_g[0], rows_of(sb_out))
    sc_args = (flat(sc_in), full["sc_conv_w"][0], rows_of(sc_out))
    x2, s1, landed = _sb_layer_fwd(x1, norm_g[1], *sb_args, "l1", send=own_last)
    landed = _forward_halves(landed, name="forward_halves_last")
    dn1 = dn_args(1, *[whole(g4, own) for g4, own in zip(landed, own_last)])
    x3, s2 = _sc_layer_fwd(x2, norm_g[2], *sc_args, "l2")
    x4, s3, _ = _dn_layer_fwd(x3, norm_g[3], *dn1, "l3")
    dy, loss_local = _loss_head(x4, loss_target[0], name="loss_head")
    loss = lax.psum(loss_local[0, 0], ("x", "y", "c"))

    core = ci.astype(jnp.int32).reshape(1)
    chip = me.astype(jnp.int32).reshape(1)

    def chip_sums(g_list, tag):
        sib = _swap_other_half(g_list, name=f"swap_halves_{tag}")
        return [_add_my_half(g, s, core, name=f"sum_cores_{tag}{i}") for i, (g, s) in enumerate(zip(g_list, sib))]

    dx3, dng3, dwin3, dconv3, dal3, ddt3, dgain3, dwout3, _ = _dn_layer_bwd(dy, x3, norm_g[3], *dn1, s3, "l3")
    dx2, dng2, dwin2, dconv2, dwout2 = _sc_layer_bwd(dx3, x2, norm_g[2], *sc_args, s2, "l2")
    dx1, dng1, dwin1, dgq, dgk, dwout1 = _sb_layer_bwd(dx2, x1, norm_g[1], *sb_args, s1, "l1")
    part_later = chip_sums([_cut2(_by_cols(dwin3)), _cut2(_by_rows(dwout3)), _cut2(dwin1), _cut2(_by_rows(dwout1)),
                            _cut2(dwin2), _cut2(_by_rows(dwout2))], "later")
    dx0, dng0, part_win0, dconv0, dal0, ddt0, dgain0, part_wout0, landed = _dn_layer_bwd(
        dx1, x0, norm_g[0], *dn0, s0, "l0", send=part_later, chip_sums=chip_sums)
    pieces = later + [("dn_w_out", 0), ("dn_w_in", 0)]
    mine = {(n, l): _sum_chips(r, p, chip, name=f"sum_chips_{n}{l}")
            for (n, l), r, p in zip(pieces, landed, part_later + [part_wout0, part_win0])}
    pieces = sorted(pieces, key=lambda nl: nl[1])
    mine = [mine[nl] for nl in pieces]
    theirs = _sibling_exchange(mine, name="swap_results")
    upd = {}
    for (n, l), a, b in zip(pieces, mine, theirs):
        upd[n] = _adamw_halves(weights[n], a, b, m_in[n], v_in[n], core, layer=l, prev=upd.get(n),
                               name=f"adamw_{n}{l}")
    g_out = {n: upd[n][0] for n, _, _ in BIG}

    grads = dict(
        norm_g=jnp.concatenate([dng0, dng1, dng2, dng3], axis=0), dn_conv_w=jnp.stack([dconv0, dconv3]),
        dn_a_log=jnp.stack([dal0, dal3]), dn_dt_bias=jnp.stack([ddt0, ddt3]),
        dn_o_norm_g=jnp.stack([dgain0, dgain3]), sb_q_norm_g=dgq[None], sb_k_norm_g=dgk[None],
        sc_conv_w=dconv2[None])
    repl = [jnp.broadcast_to(grads[n][None], (N_CHIPS,) + s) for n, s in REPL]
    gsmall = _pack([_split(grads[n], ax) for n, _, ax in SMALL] + repl, LANES, (N_CHIPS,))
    rsmall, = _chip_exchange([gsmall], send_slot_is_dest=True, copy_own=(True,), name="scatter_small")
    psmall = _sum_small(rsmall, name="sum_chips_small")
    qsmall, = _sibling_exchange([psmall], name="swap_cores_small")
    tsmall = _add(psmall, qsmall, name="sum_cores_small")
    g_out.update(_unpack(tsmall, SMALL + REPL))

    for n in order:
        if n not in upd:
            upd[n] = (g_out[n],) + _adamw(weights[n], g_out[n], m_in[n], v_in[n], name=f"adamw_{n}")
    return (loss, dx0[None], *[upd[n][0] for n in order], *[upd[n][1] for n in order],
            *[upd[n][2] for n in order], *[upd[n][3] for n in order])
```

```python
import functools
import math

import jax
import jax.numpy as jnp
from jax import lax
from jax.experimental import pallas as pl
from jax.experimental.pallas import tpu as pltpu

F32 = jnp.float32
BF16 = jnp.bfloat16
MESH = pl.DeviceIdType.MESH

RMS_EPS = 1e-6
L2_EPS = 1e-6
LANES = 128
VMEM_BIG = 60 * 1024 * 1024
MM_VMEM = 44 * 1024 * 1024

DN_HEADS, DN_DK, DN_DV, DN_CHUNK, DN_CONV = 8, 128, 256, 64, 4
DN_QK_W = DN_HEADS * DN_DK
DN_V_W = DN_HEADS * DN_DV
DN_CONV_W = 2 * DN_QK_W + DN_V_W
DN_IN = DN_CONV_W + DN_V_W + 2 * DN_HEADS
DN_IN_PAD = DN_CONV_W + DN_V_W + LANES
SB_DH = 64
SC_CONV = 3

ADAM_LR, ADAM_B1, ADAM_B2, ADAM_EPS, ADAM_WD, ADAM_STEP = 0.001, 0.9, 0.999, 1e-08, 0.01, 10


def _pick(n, cands):
    for c in cands:
        if n % c == 0:
            return c
    raise ValueError(f"no tile for {n} in {cands}")


def _bf(x):
    return x.astype(BF16)


def _dot(a, b):
    return jnp.dot(_bf(a), _bf(b), preferred_element_type=F32)


def _dot_nt(a, b):
    return lax.dot_general(_bf(a), _bf(b), (((1,), (1,)), ((), ())), preferred_element_type=F32)


def _dot_tn(a, b):
    return lax.dot_general(_bf(a), _bf(b), (((0,), (0,)), ((), ())), preferred_element_type=F32)


def _split3(a):
    hi = _bf(a)
    r = a - hi.astype(F32)
    mid = _bf(r)
    lo = _bf(r - mid.astype(F32))
    return hi, mid, lo


def _sigmoid(x):
    return 1.0 / (1.0 + jnp.exp(-x))


def _silu(x):
    return x * _sigmoid(x)


def _dsilu(x):
    s = _sigmoid(x)
    return s * (1.0 + x * (1.0 - s))


def _softplus(x):
    return jnp.maximum(x, 0.0) + jnp.log(1.0 + jnp.exp(-jnp.abs(x)))


def _shift_down(z, k):
    if k == 0:
        return z
    row = lax.broadcasted_iota(jnp.int32, z.shape, 0)
    return jnp.where(row >= k, pltpu.roll(z, k, 0), 0.0)


def _shift_up(z, k):
    if k == 0:
        return z
    n = z.shape[0]
    row = lax.broadcasted_iota(jnp.int32, z.shape, 0)
    return jnp.where(row < n - k, pltpu.roll(z, n - k, 0), 0.0)


def _matmul(a, b, *, mode, name, res=None, a_parts=1, b_parts=1, out_parts=1, out_dtype=F32, send=()):
    def dims2(x, parts):
        if parts == 1:
            return x.shape
        assert x.shape[0] == parts
        return (x.shape[1], x.shape[2] * parts)

    ash, bsh = dims2(a, a_parts), dims2(b, b_parts)
    if mode == "nn":
        (M, K), (K2, N) = ash, bsh
        dn = (((1,), (0,)), ((), ()))
    elif mode == "nt":
        (M, K), (N, K2) = ash, bsh
        dn = (((1,), (1,)), ((), ()))
    else:
        (K, M), (K2, N) = ash, bsh
        dn = (((0,), (0,)), ((), ()))
    assert K == K2, (ash, bsh, mode)
    tm_max = _pick(M, (512, 256, 128, 64, 32, 16, 8))
    n_unit = N // max(out_parts, b_parts if mode != "nt" else 1)
    k_unit = K // max(a_parts if mode != "tn" else 1, b_parts if mode == "nt" else 1)
    tm, tn, tk = min(
        ((m, n, k) for m in {tm_max, max(tm_max // 2, 8)}
         for n in (2048, 1792, 1024, 896, 768, 512, 384, 256, 128) if n_unit % n == 0
         for k in (k_unit, 2048, 1792, 1024, 896, 512, 256, 128) if k_unit % k == 0
         if 2 * (m * k * a.dtype.itemsize + k * n * b.dtype.itemsize + 2 * m * n * 4) + m * n * 4 <= MM_VMEM),
        key=lambda t: (-t[0] * t[1] * t[2], -t[0], -t[2]))
    nk = K // tk
    grid = (M // tm, N // tn, nk)

    def spec(parts, rows_are, cols_are, tr, tc, width):
        per = width // parts // tc
        if parts == 1:
            return pl.BlockSpec((tr, tc), lambda i, j, k: ((i, j, k)[rows_are], (i, j, k)[cols_are]))
        return pl.BlockSpec((None, tr, tc), lambda i, j, k: ((i, j, k)[cols_are] // per, (i, j, k)[rows_are],
                                                             (i, j, k)[cols_are] % per))

    if mode == "nn":
        a_spec = spec(a_parts, 0, 2, tm, tk, K)
        b_spec = spec(b_parts, 2, 1, tk, tn, N)
    elif mode == "nt":
        a_spec = spec(a_parts, 0, 2, tm, tk, K)
        b_spec = spec(b_parts, 1, 2, tn, tk, K)
    else:
        a_spec = spec(a_parts, 2, 0, tk, tm, M)
        b_spec = spec(b_parts, 2, 1, tk, tn, N)
    o_spec = spec(out_parts, 0, 1, tm, tn, N)
    in_specs = [a_spec, b_spec]
    operands = [a, b]
    if res is not None:
        in_specs.append(pl.BlockSpec((tm, tn), lambda i, j, k: (i, j)))
        operands.append(res)

    n_in = len(operands)
    ns = len(send)

    def finish(refs, r):
        if res is not None:
            r = refs[2][...] + r
        refs[n_in + ns][...] = r.astype(out_dtype)

    def body(*refs):
        if ns:
            at = lambda step: functools.reduce(jnp.logical_and, [pl.program_id(d) == step[d] for d in range(3)])
            _blocks_over_ici(refs[n_in:n_in + ns], refs[n_in + ns + 1:n_in + 2 * ns + 1], refs[-2], refs[-1],
                             at((0, 0, 0)), at(tuple(g - 1 for g in grid)))
        part = lax.dot_general(_bf(refs[0][...]), _bf(refs[1][...]), dn, preferred_element_type=F32)
        if nk == 1:
            finish(refs, part)
            return
        acc_ref = refs[n_in + 2 * ns + 1]
        k = pl.program_id(2)

        @pl.when(k == 0)
        def _():
            acc_ref[...] = part

        @pl.when(jnp.logical_and(k > 0, k < nk - 1))
        def _():
            acc_ref[...] += part

        @pl.when(k == nk - 1)
        def _():
            finish(refs, acc_ref[...] + part)

    out_shape = (M, N) if out_parts == 1 else (out_parts, M, N // out_parts)
    out = pl.pallas_call(
        body, name=name, grid=grid, in_specs=in_specs + [HBM] * ns, out_specs=[o_spec] + [HBM] * ns,
        out_shape=[jax.ShapeDtypeStruct(out_shape, out_dtype)] + [jax.ShapeDtypeStruct(x.shape, x.dtype) for x in send],
        scratch_shapes=([pltpu.VMEM((tm, tn), F32)] if nk > 1 else [])
        + ([pltpu.SemaphoreType.DMA((3 * ns,)), pltpu.SemaphoreType.DMA((3 * ns,))] if ns else []),
        compiler_params=pltpu.CompilerParams(
            dimension_semantics=("arbitrary",) * 3 if ns else ("parallel", "parallel", "arbitrary"),
            vmem_limit_bytes=VMEM_BIG),
    )(*operands, *send)
    return out if ns else out[0]


def _rmsnorm_fwd(x, g, *, name):
    T, D = x.shape
    tm = _pick(T, (512, 256, 128, 64, 32, 16))

    def body(x_ref, g_ref, h_ref):
        xv = x_ref[...]
        r = lax.rsqrt(jnp.mean(xv * xv, axis=-1, keepdims=True) + RMS_EPS)
        h_ref[...] = ((xv * r) * g_ref[...]).astype(BF16)

    return pl.pallas_call(
        body, name=name, grid=(T // tm,),
        in_specs=[pl.BlockSpec((tm, D), lambda i: (i, 0)), pl.BlockSpec((1, D), lambda i: (0, 0))],
        out_specs=pl.BlockSpec((tm, D), lambda i: (i, 0)),
        out_shape=jax.ShapeDtypeStruct((T, D), BF16),
    )(x, g.reshape(1, D))


def _rmsnorm_bwd(x, g, dh, dx_in, *, name):
    T, D = x.shape
    tm = _pick(T, (512, 256, 128, 64, 32, 16))

    def body(x_ref, g_ref, dh_ref, dxin_ref, dx_ref, dg_ref):
        @pl.when(pl.program_id(0) == 0)
        def _():
            dg_ref[...] = jnp.zeros_like(dg_ref)

        xv = x_ref[...]
        r = lax.rsqrt(jnp.mean(xv * xv, axis=-1, keepdims=True) + RMS_EPS)
        xh = xv * r
        dh_v = dh_ref[...]
        dxh = dh_v * g_ref[...]
        dx_ref[...] = dxin_ref[...] + r * (dxh - xh * jnp.mean(dxh * xh, axis=-1, keepdims=True))
        dg_ref[...] += jnp.sum(dh_v * xh, axis=0, keepdims=True)

    row = pl.BlockSpec((tm, D), lambda i: (i, 0))
    vec = pl.BlockSpec((1, D), lambda i: (0, 0))
    return pl.pallas_call(
        body, name=name, grid=(T // tm,),
        in_specs=[row, vec, row, row], out_specs=[row, vec],
        out_shape=[jax.ShapeDtypeStruct((T, D), F32), jax.ShapeDtypeStruct((1, D), F32)],
        compiler_params=pltpu.CompilerParams(dimension_semantics=("arbitrary",)),
    )(x, g.reshape(1, D), dh, dx_in)


def _loss_head(y, target, *, name):
    T, D = y.shape
    tm = _pick(T, (512, 256, 128, 64, 32, 16))

    def body(y_ref, t_ref, dy_ref, l_ref):
        @pl.when(pl.program_id(0) == 0)
        def _():
            l_ref[...] = jnp.zeros_like(l_ref)

        err = y_ref[...] - t_ref[...]
        dy_ref[...] = err * (1.0 / D)
        l_ref[...] += 0.5 * jnp.sum(jnp.mean(err * err, axis=-1, keepdims=True), axis=0, keepdims=True)

    row = pl.BlockSpec((tm, D), lambda i: (i, 0))
    return pl.pallas_call(
        body, name=name, grid=(T // tm,),
        in_specs=[row, row], out_specs=[row, pl.BlockSpec((1, 1), lambda i: (0, 0))],
        out_shape=[jax.ShapeDtypeStruct((T, D), F32), jax.ShapeDtypeStruct((1, 1), F32)],
        compiler_params=pltpu.CompilerParams(dimension_semantics=("arbitrary",)),
    )(y, target)


def _sc_mid_fwd(p3, conv_w, *, name):
    _, T, W = p3.shape
    K = conv_w.shape[0]
    cw = LANES

    def body(p_ref, w_ref, o_ref):
        z = p_ref[1] * p_ref[2]
        cv = sum(w_ref[i:i + 1, :] * _shift_down(z, K - 1 - i) for i in range(K))
        o_ref[...] = ((p_ref[0] * cv) * _silu(p_ref[3])).astype(BF16)

    return pl.pallas_call(
        body, name=name, grid=(W // cw,),
        in_specs=[pl.BlockSpec((4, T, cw), lambda j: (0, 0, j)), pl.BlockSpec((K, cw), lambda j: (0, j))],
        out_specs=pl.BlockSpec((T, cw), lambda j: (0, j)),
        out_shape=jax.ShapeDtypeStruct((T, W), BF16),
        compiler_params=pltpu.CompilerParams(dimension_semantics=("parallel",), vmem_limit_bytes=VMEM_BIG),
    )(p3, conv_w)


def _sc_mid_bwd(p3, conv_w, do, *, name):
    _, T, W = p3.shape
    K = conv_w.shape[0]
    cw = LANES

    def body(p_ref, w_ref, do_ref, dp_ref, dw_ref):
        b, c, u, gate = p_ref[0], p_ref[1], p_ref[2], p_ref[3]
        z = c * u
        zs = [_shift_down(z, K - 1 - i) for i in range(K)]
        cv = sum(w_ref[i:i + 1, :] * zs[i] for i in range(K))
        y = b * cv
        dov = do_ref[...]
        dy = dov * _silu(gate)
        dp_ref[3] = dov * y * _dsilu(gate)
        dp_ref[0] = dy * cv
        dcv = dy * b
        dz = sum(w_ref[i:i + 1, :] * _shift_up(dcv, K - 1 - i) for i in range(K))
        dp_ref[1] = dz * u
        dp_ref[2] = dz * c
        for i in range(K):
            dw_ref[i:i + 1, :] = jnp.sum(dcv * zs[i], axis=0, keepdims=True)

    return pl.pallas_call(
        body, name=name, grid=(W // cw,),
        in_specs=[pl.BlockSpec((4, T, cw), lambda j: (0, 0, j)), pl.BlockSpec((K, cw), lambda j: (0, j)),
                  pl.BlockSpec((T, cw), lambda j: (0, j))],
        out_specs=[pl.BlockSpec((4, T, cw), lambda j: (0, 0, j)), pl.BlockSpec((K, cw), lambda j: (0, j))],
        out_shape=[jax.ShapeDtypeStruct((4, T, W), F32), jax.ShapeDtypeStruct((K, W), F32)],
        compiler_params=pltpu.CompilerParams(dimension_semantics=("parallel",), vmem_limit_bytes=VMEM_BIG),
    )(p3, conv_w, do)


def _sc_layer_fwd(x, ng, w_in, conv_w, w_out, tag):
    h = _rmsnorm_fwd(x, ng, name=f"{tag}_norm")
    p3 = _matmul(h, w_in, mode="nn", b_parts=4, out_parts=4, name=f"{tag}_inproj")
    og = _sc_mid_fwd(p3, conv_w, name=f"{tag}_mid")
    x_new = _matmul(og, w_out, mode="nn", res=x, name=f"{tag}_outproj")
    return x_new, (h, p3, og)


def _sc_layer_bwd(dx, x, ng, w_in, conv_w, w_out, saved, tag):
    h, p3, og = saved
    d_wout = _matmul(og, dx, mode="tn", out_dtype=BF16, name=f"{tag}_dwout")
    dog = _matmul(dx, w_out, mode="nt", name=f"{tag}_dog")
    dp3, dconv = _sc_mid_bwd(p3, conv_w, dog, name=f"{tag}_midbwd")
    d_win = _matmul(h, dp3, mode="tn", b_parts=4, out_parts=4, out_dtype=BF16, name=f"{tag}_dwin")
    dh = _matmul(dp3, w_in, mode="nt", a_parts=4, b_parts=4, name=f"{tag}_dh")
    dx_prev, dng = _rmsnorm_bwd(x, ng, dh, dx, name=f"{tag}_normbwd")
    return dx_prev, dng, d_win, dconv, d_wout


SB_BQ = 256
SB_BK = 256
SB_ROWS = 512
SB_DEAD = -110.0


def _sb_half_mask():
    return lax.broadcasted_iota(jnp.int32, (1, LANES), 1) < SB_DH


def _sb_headnorm(x, g, lo):
    x2 = x * x
    s_lo = jnp.sum(jnp.where(lo, x2, 0.0), axis=-1, keepdims=True)
    s_hi = jnp.sum(jnp.where(lo, 0.0, x2), axis=-1, keepdims=True)
    r = lax.rsqrt(jnp.where(lo, s_lo, s_hi) * (1.0 / SB_DH) + RMS_EPS)
    xh = x * r
    return xh * g, xh, r


def _dot_x2_l(a_l, b_exact_bf16):
    his = [_bf(a) for a in a_l]
    mids = [_bf(a - h.astype(F32)) for a, h in zip(a_l, his)]
    f = lambda p: jnp.dot(p, b_exact_bf16, preferred_element_type=F32)
    return [x + y for x, y in zip([f(h) for h in his], [f(m) for m in mids])]


def _sb_stack(xb, lo):
    zero = jnp.zeros_like(xb)
    return jnp.concatenate([jnp.where(lo, xb, zero), jnp.where(lo, zero, xb)], axis=0)


def _sb_rel(bq, bk):
    row = lax.broadcasted_iota(jnp.int32, (2 * bq, bk), 0)
    col = lax.broadcasted_iota(jnp.int32, (2 * bq, bk), 1)
    return col - jnp.where(row >= bq, row - bq, row)


def _sb_tile(qm, kb, valid):
    z = lax.dot_general(qm, kb, (((1,), (1,)), ((), ())), preferred_element_type=F32)
    sp = _softplus(z)
    return z - sp, (-sp if valid is None else jnp.where(valid, -sp, 0.0))


def _sb_attn_fwd(p3, gq2, gk2, *, name, send=()):
    _, T, W = p3.shape
    bq, bk = min(SB_BQ, T), min(SB_BK, T)
    rows = min(SB_ROWS, T)
    scale = SB_DH ** -0.5
    ns = len(send)
    npair = W // LANES

    def body(*refs):
        p_ref, gq_ref, gk_ref = refs[:3]
        og_ref, o_ref, ls_ref, cnt_ref = refs[3 + ns:7 + ns]
        qn_ref, kn_ref, v_ref = refs[7 + 2 * ns:10 + 2 * ns]
        if ns:
            _halves_over_ici(refs[3:3 + ns], refs[7 + ns:7 + 2 * ns], refs[10 + 2 * ns], refs[11 + 2 * ns],
                             pl.program_id(0) == 0, pl.program_id(0) == npair - 1)
        lo = _sb_half_mask()

        def prologue(i, c):
            r0 = pl.multiple_of(i * rows, rows)
            sl = pl.ds(r0, rows)
            qn_ref[sl, :] = (_sb_headnorm(p_ref[0, sl, :], gq_ref[...], lo)[0] * scale).astype(BF16)
            kn_ref[sl, :] = _sb_headnorm(p_ref[1, sl, :], gk_ref[...], lo)[0].astype(BF16)
            v_ref[sl, :] = p_ref[2, sl, :].astype(BF16)
            return c

        lax.fori_loop(0, T // rows, prologue, 0)

        rel = _sb_rel(bq, bk)
        tri = (lax.broadcasted_iota(jnp.int32, (bk, bk), 0)
               > lax.broadcasted_iota(jnp.int32, (bk, bk), 1)).astype(BF16)

        def qblock(qi, c):
            q0 = pl.multiple_of(qi * bq, bq)
            qm = _sb_stack(qn_ref[pl.ds(q0, bq), :], lo)
            nkb = (q0 + bq - 1) // bk + 1

            def tiles(k0s, carry, valids):
                o_acc, a_carry = carry
                sc = [_sb_tile(qm, kn_ref[pl.ds(k0, bk), :], valid) for k0, valid in zip(k0s, valids)]
                later = _dot_x2_l([log1m for _, log1m in sc], tri)
                for (logsig, log1m), lat, k0, valid in zip(sc, later, k0s, valids):
                    wts = jnp.exp(logsig + (lat + a_carry))
                    if valid is not None:
                        wts = jnp.where(valid, wts, 0.0)
                    o_acc = o_acc + jnp.dot(_bf(wts), v_ref[pl.ds(k0, bk), :], preferred_element_type=F32)
                    a_carry = a_carry + jnp.sum(log1m, axis=-1, keepdims=True)
                return o_acc, a_carry

            blk0 = lambda j: pl.multiple_of(j * bk, bk)
            k_last = blk0(nkb - 1)
            o2, t2 = tiles([k_last, blk0(jnp.maximum(nkb - 2, 0))],
                           (jnp.zeros((2 * bq, LANES), F32), jnp.zeros((2 * bq, 1), F32)),
                           [rel < q0 - k_last, nkb >= 2])

            def alive(st):
                return jnp.logical_and(st[0] < nkb - 1, jnp.max(st[2]) > SB_DEAD)

            def back_one(st):
                return (st[0] + 1,) + tiles([blk0(nkb - 2 - st[0])], st[1:], [None])

            n_back, o2, t2 = lax.while_loop(alive, back_one, (jnp.int32(1), o2, t2))
            o = jnp.where(lo, o2[:bq], o2[bq:])
            o_ref[pl.ds(q0, bq), :] = o
            ls_ref[pl.ds(q0, bq), :] = jnp.where(lo, t2[:bq], t2[bq:])
            cnt_ref[qi] = jnp.full((8, LANES), jnp.minimum(n_back + 1, nkb).astype(F32))
            og_ref[pl.ds(q0, bq), :] = (o * _silu(p_ref[3, pl.ds(q0, bq), :])).astype(BF16)
            return c

        lax.fori_loop(0, T // bq, qblock, 0)

    colblk = pl.BlockSpec((T, LANES), lambda j: (0, j))
    vec = pl.BlockSpec((1, LANES), lambda j: (0, 0))
    return pl.pallas_call(
        body, name=name, grid=(npair,),
        in_specs=[pl.BlockSpec((4, T, LANES), lambda j: (0, 0, j)), vec, vec] + [HBM] * ns,
        out_specs=[colblk, colblk, colblk, pl.BlockSpec((None, T // bq, 8, LANES), lambda j: (j, 0, 0, 0))]
        + [HBM] * ns,
        out_shape=[jax.ShapeDtypeStruct((T, W), BF16), jax.ShapeDtypeStruct((T, W), F32),
                   jax.ShapeDtypeStruct((T, W), F32), jax.ShapeDtypeStruct((npair, T // bq, 8, LANES), F32)]
        + [jax.ShapeDtypeStruct((N_CHIPS,) + a.shape, a.dtype) for a in send],
        scratch_shapes=[pltpu.VMEM((T, LANES), BF16)] * 3
        + ([pltpu.SemaphoreType.DMA((3 * ns,)), pltpu.SemaphoreType.DMA((3 * ns,))] if ns else []),
        compiler_params=pltpu.CompilerParams(dimension_semantics=("arbitrary",), vmem_limit_bytes=VMEM_BIG),
    )(p3, gq2, gk2, *send)


def _sb_attn_bwd(p3, gq2, gk2, o, lsum, live, dog, *, name):
    _, T, W = p3.shape
    bq, bk = min(SB_BQ, T), min(SB_BK, T)
    rows = min(SB_ROWS, T)
    scale = SB_DH ** -0.5

    def body(p_ref, gq_ref, gk_ref, o_ref, ls_ref, cnt_ref, dog_ref, dp_ref, dgq_ref, dgk_ref,
             qn_ref, kn_ref, v_ref, do_ref):
        lo = _sb_half_mask()

        def prologue(i, c):
            r0 = pl.multiple_of(i * rows, rows)
            sl = pl.ds(r0, rows)
            qn_ref[sl, :] = (_sb_headnorm(p_ref[0, sl, :], gq_ref[...], lo)[0] * scale).astype(BF16)
            kn_ref[sl, :] = _sb_headnorm(p_ref[1, sl, :], gk_ref[...], lo)[0].astype(BF16)
            v_ref[sl, :] = p_ref[2, sl, :].astype(BF16)
            gate = p_ref[3, sl, :]
            dogv = dog_ref[sl, :]
            dp_ref[3, sl, :] = dogv * o_ref[sl, :] * _dsilu(gate)
            do_ref[sl, :] = (dogv * _silu(gate)).astype(BF16)
            zero = jnp.zeros((rows, LANES), F32)
            dp_ref[0, sl, :] = zero
            dp_ref[1, sl, :] = zero
            dp_ref[2, sl, :] = zero
            return c

        lax.fori_loop(0, T // rows, prologue, 0)

        rel = _sb_rel(bq, bk)
        rj = lax.broadcasted_iota(jnp.int32, (bk, bk), 0)
        cj = lax.broadcasted_iota(jnp.int32, (bk, bk), 1)
        upto = (rj <= cj).astype(BF16)
        before_m = (rj < cj).astype(BF16)

        def qblock(qi, c):
            q0 = pl.multiple_of(qi * bq, bq)
            qm = _sb_stack(qn_ref[pl.ds(q0, bq), :], lo)
            dom = _sb_stack(do_ref[pl.ds(q0, bq), :], lo)
            nkb = (q0 + bq - 1) // bk + 1
            blk0 = lambda j: pl.multiple_of(j * bk, bk)
            k_last = blk0(nkb - 1)

            lsb = ls_ref[pl.ds(q0, bq), :]
            total = jnp.concatenate([lsb[:, 0:1], lsb[:, SB_DH:SB_DH + 1]], axis=0)
            n_live = jnp.clip(jnp.max(cnt_ref[qi]).astype(jnp.int32), 1, nkb)
            k_first = nkb - n_live

            def tiles(k0s, carry, valids):
                dq_acc, a_pre, r_pre = carry
                kss = [pl.ds(k0, bk) for k0 in k0s]
                kbs = [kn_ref[ks, :] for ks in kss]
                sc = [_sb_tile(qm, kb, valid) for kb, valid in zip(kbs, valids)]
                dws = [lax.dot_general(dom, v_ref[ks, :], _NT, preferred_element_type=F32) for ks in kss]
                upto_l = _dot_x2_l([log1m for _, log1m in sc], upto)
                wts_l = []
                for (logsig, log1m), up, valid in zip(sc, upto_l, valids):
                    wts = jnp.exp(logsig + ((total - a_pre) - up))
                    wts_l.append(wts if valid is None else jnp.where(valid, wts, 0.0))
                    a_pre = a_pre + jnp.sum(log1m, axis=-1, keepdims=True)
                ee_l = [dw * wts for dw, wts in zip(dws, wts_l)]
                before_l = _dot_x2_l(ee_l, before_m)
                for (logsig, _), ks, kb, wts, ee, bef, valid in zip(sc, kss, kbs, wts_l, ee_l, before_l, valids):
                    beta = jnp.exp(logsig)
                    dz = ee * (1.0 - beta) - beta * (r_pre + bef)
                    if valid is not None:
                        dz = jnp.where(valid, dz, 0.0)
                    dzb = _bf(dz)
                    dq_acc = dq_acc + jnp.dot(dzb, kb, preferred_element_type=F32)
                    dp_ref[1, ks, :] += lax.dot_general(dzb, qm, _TN, preferred_element_type=F32)
                    dp_ref[2, ks, :] += lax.dot_general(_bf(wts), dom, _TN, preferred_element_type=F32)
                    r_pre = r_pre + jnp.sum(ee, axis=-1, keepdims=True)
                return dq_acc, a_pre, r_pre

            cr = (jnp.zeros((2 * bq, LANES), F32), jnp.zeros((2 * bq, 1), F32), jnp.zeros((2 * bq, 1), F32))
            n_before = jnp.maximum(n_live - 2, 0)
            cr = lax.fori_loop(0, n_before % 2, lambda t, cr: tiles([blk0(k_first)], cr, [None]), cr)
            k_pairs = k_first + n_before % 2
            cr = lax.fori_loop(0, n_before // 2,
                               lambda t, cr: tiles([blk0(k_pairs + 2 * t), blk0(k_pairs + 2 * t + 1)], cr,
                                                   [None, None]), cr)
            dq2, _, _ = tiles([blk0(jnp.maximum(nkb - 2, 0)), k_last], cr, [n_live >= 2, rel < q0 - k_last])
            dp_ref[0, pl.ds(q0, bq), :] = jnp.where(lo, dq2[:bq], dq2[bq:]) * scale
            return c

        lax.fori_loop(0, T // bq, qblock, 0)

        dgq_ref[...] = jnp.zeros_like(dgq_ref)
        dgk_ref[...] = jnp.zeros_like(dgk_ref)

        def epilogue(i, c):
            r0 = pl.multiple_of(i * rows, rows)
            sl = pl.ds(r0, rows)
            for part, g_ref, dg_ref in ((0, gq_ref, dgq_ref), (1, gk_ref, dgk_ref)):
                _, xh, r = _sb_headnorm(p_ref[part, sl, :], g_ref[...], lo)
                dn = dp_ref[part, sl, :]
                dxh = dn * g_ref[...]
                prod = dxh * xh
                m_lo = jnp.sum(jnp.where(lo, prod, 0.0), axis=-1, keepdims=True)
                m_hi = jnp.sum(jnp.where(lo, 0.0, prod), axis=-1, keepdims=True)
                m = jnp.where(lo, m_lo, m_hi) * (1.0 / SB_DH)
                dp_ref[part, sl, :] = r * (dxh - xh * m)
                dg_ref[...] += jnp.sum(dn * xh, axis=0, keepdims=True)
            return c

        lax.fori_loop(0, T // rows, epilogue, 0)

    colblk = pl.BlockSpec((T, LANES), lambda j: (0, j))
    vec = pl.BlockSpec((1, LANES), lambda j: (0, 0))
    part = pl.BlockSpec((4, T, LANES), lambda j: (0, 0, j))
    gvec = pl.BlockSpec((None, 1, LANES), lambda j: (j, 0, 0))
    npair = W // LANES
    return pl.pallas_call(
        body, name=name, grid=(npair,),
        in_specs=[part, vec, vec, colblk, colblk, pl.BlockSpec((None, T // bq, 8, LANES), lambda j: (j, 0, 0, 0)),
                  colblk],
        out_specs=[part, gvec, gvec],
        out_shape=[jax.ShapeDtypeStruct((4, T, W), F32), jax.ShapeDtypeStruct((npair, 1, LANES), F32),
                   jax.ShapeDtypeStruct((npair, 1, LANES), F32)],
        scratch_shapes=[pltpu.VMEM((T, LANES), BF16)] * 4,
        compiler_params=pltpu.CompilerParams(dimension_semantics=("parallel",), vmem_limit_bytes=VMEM_BIG),
    )(p3, gq2, gk2, o, lsum, live, dog)


_NN = (((1,), (0,)), ((), ()))
_NT = (((1,), (1,)), ((), ()))
_TN = (((0,), (0,)), ((), ()))
DN_TB = 512
DN_HEADS_FWD = 4
DN_HEADS_BWD = 2
DN_INV_EXACT_LEVELS = 2
DN_AB_COL = (DN_CONV_W + DN_V_W) // LANES


def _dn_conv(x, w_ref):
    k = w_ref.shape[0]
    return sum(w_ref[i:i + 1, :] * _shift_down(x, k - 1 - i) for i in range(k))


def _dn_prep_fwd(p, conv_w, *, name):
    T = p.shape[0]
    cw = conv_w.shape[1]
    n_qk = 2 * DN_QK_W // LANES

    def body(p_ref, w_ref, o_ref):
        s = _silu(_dn_conv(p_ref[...], w_ref))
        r = lax.rsqrt(jnp.sum(s * s, axis=-1, keepdims=True) + L2_EPS)
        o_ref[...] = jnp.where(pl.program_id(0) < n_qk, s * r, s)

    colblk = pl.BlockSpec((T, LANES), lambda j: (0, j))
    return pl.pallas_call(
        body, name=name, grid=(cw // LANES,),
        in_specs=[colblk, pl.BlockSpec((DN_CONV, LANES), lambda j: (0, j))],
        out_specs=colblk, out_shape=jax.ShapeDtypeStruct((T, cw), F32),
        compiler_params=pltpu.CompilerParams(dimension_semantics=("parallel",), vmem_limit_bytes=VMEM_BIG),
    )(p, conv_w)


def _dn_chunk_tri(rows, upper):
    r = lax.broadcasted_iota(jnp.int32, (rows, rows), 0)
    c = lax.broadcasted_iota(jnp.int32, (rows, rows), 1)
    same = (r // DN_CHUNK) == (c // DN_CHUNK)
    return jnp.logical_and(same, (c >= r) if upper else (c <= r)).astype(BF16)


def _dn_lane_rows(a_log, dt_bias):
    pad = lambda v: jnp.zeros((1, LANES), F32).at[0, :DN_HEADS].set(v)
    return pad(a_log), pad(dt_bias)


def _dn_ab_parts(blk, alog_row, dtb_row):
    lane = lax.broadcasted_iota(jnp.int32, (1, LANES), 1)
    is_a = lane < DN_HEADS
    is_b = jnp.logical_and(lane >= DN_HEADS, lane < 2 * DN_HEADS)
    a_arg = jnp.where(is_a, blk + dtb_row, 0.0)
    neg_exp = jnp.where(is_a, -jnp.exp(alog_row), 0.0)
    log_a = neg_exp * _softplus(a_arg)
    beta = jnp.where(is_b, _sigmoid(blk), 0.0)
    return is_a, is_b, a_arg, neg_exp, log_a, beta


def _dn_ab_fwd(p, alog_row, dtb_row, *, name):
    T = p.shape[0]
    rows = min(DN_TB, T)

    def body(p_ref, al_ref, dt_ref, o_ref):
        _, _, _, _, log_a, beta = _dn_ab_parts(p_ref[...], al_ref[...], dt_ref[...])
        hi, mid, lo_ = _split3(log_a)
        tri = _dn_chunk_tri(rows, upper=False)
        f = lambda q: jnp.dot(tri, q, preferred_element_type=F32)
        o_ref[...] = (f(hi) + f(mid) + f(lo_)) + beta

    blk = pl.BlockSpec((rows, LANES), lambda i: (i, DN_AB_COL))
    vec = pl.BlockSpec((1, LANES), lambda i: (0, 0))
    return pl.pallas_call(
        body, name=name, grid=(T // rows,), in_specs=[blk, vec, vec],
        out_specs=pl.BlockSpec((rows, LANES), lambda i: (i, 0)),
        out_shape=jax.ShapeDtypeStruct((T, LANES), F32),
        compiler_params=pltpu.CompilerParams(dimension_semantics=("parallel",)),
    )(p, alog_row, dtb_row)


def _hp_l(a_l, b_l, dims=_NN):
    sa = [_split3(a)[:2] for a in a_l]
    sb = [_split3(b)[:2] for b in b_l]
    f = lambda p, q: lax.dot_general(p, q, dims, preferred_element_type=F32)
    hh = [f(x[0], y[0]) for x, y in zip(sa, sb)]
    hm = [f(x[0], y[1]) for x, y in zip(sa, sb)]
    mh = [f(x[1], y[0]) for x, y in zip(sa, sb)]
    return [a + (b + c) for a, b, c in zip(hh, hm, mh)]


def _dn_local(qs, k, v, g, beta, nc, inv_l=None):
    c = DN_CHUNK
    cut = lambda x: [x[i * c:(i + 1) * c] for i in range(nc)]
    row = lax.broadcasted_iota(jnp.int32, (c, c), 0)
    col = lax.broadcasted_iota(jnp.int32, (c, c), 1)
    eye, lower, strict = row == col, row >= col, row > col
    rowid = lax.broadcasted_iota(jnp.int32, (c, 1), 0)
    eg = jnp.exp(g)
    kb = k * beta
    rhs_k = kb * eg
    g_l, k_l, kb_l, qs_l = cut(g), cut(k), cut(kb), cut(qs)
    g_row_l = [jnp.sum(jnp.where(eye, x, 0.0), axis=0, keepdims=True) for x in g_l]
    dec_l = [jnp.where(lower, jnp.exp(jnp.where(lower, x - y, 0.0)), 0.0) for x, y in zip(g_l, g_row_l)]
    kk_l = [_dot_nt(a, b) for a, b in zip(kb_l, k_l)]
    qk_l = [_dot_nt(a, b) for a, b in zip(qs_l, k_l)]
    low_l = [jnp.where(strict, a * d, 0.0) for a, d in zip(kk_l, dec_l)]
    if inv_l is None:
        pw_l = [-x for x in low_l]
        inv_l = [eye.astype(F32) + x for x in pw_l]
        plain = lambda a_l, b_l: [_dot(a, b) for a, b in zip(a_l, b_l)]
        for level in range(int(math.log2(c)) - 1):
            mul = _hp_l if level < DN_INV_EXACT_LEVELS else plain
            pw_l = mul(pw_l, pw_l)
            inv_l = [a + b for a, b in zip(inv_l, mul(inv_l, pw_l))]
    u_l = [_dot(a, b) for a, b in zip(inv_l, cut(v * beta))]
    w_l = [_dot(a, b) for a, b in zip(inv_l, cut(rhs_k))]
    aqk_l = [jnp.where(lower, a * d, 0.0) for a, d in zip(qk_l, dec_l)]
    g_last_l = [jnp.sum(jnp.where(rowid == c - 1, x, 0.0), axis=0, keepdims=True) for x in g_l]
    ekd_l = [jnp.exp(a - b) for a, b in zip(g_last_l, g_l)]
    kd_l = [a * b for a, b in zip(k_l, ekd_l)]
    qd_l = cut(qs * eg)
    kw_l = [_dot_tn(a, b) for a, b in zip(kd_l, w_l)]
    qp_l = [q - _dot(a, w) for q, a, w in zip(qd_l, aqk_l, w_l)]
    return dict(eye=eye, lower=lower, strict=strict, dec=dec_l, k=k_l, kb=kb_l, qs=qs_l, low=low_l, inv=inv_l,
                eg=cut(eg), rhs_k=cut(rhs_k), u=u_l, w=w_l, aqk=aqk_l, g_last=g_last_l, qd=qd_l,
                ekd=ekd_l, kd=kd_l, kw=kw_l, qp=qp_l)


def _dn_head_cols(gb_blk, head):
    lane = lax.broadcasted_iota(jnp.int32, (1, LANES), 1)
    g = jnp.sum(jnp.where(lane == head, gb_blk, 0.0), axis=-1, keepdims=True)
    beta = jnp.sum(jnp.where(lane == head + DN_HEADS, gb_blk, 0.0), axis=-1, keepdims=True)
    return g, beta


def _halves_over_ici(s_refs, o_refs, send_sems, recv_sems, first, last):
    x, y, c = _mesh_pos()
    me = 2 * x + y
    chips = _other_chips(x, y)
    pairs = [(a, k) for a in range(len(s_refs)) for k in range(3)]

    def copy(a, k, slot):
        px, py = chips[k]
        return pltpu.make_async_remote_copy(
            src_ref=s_refs[a].at[c], dst_ref=o_refs[a].at[slot, c], send_sem=send_sems.at[3 * a + k],
            recv_sem=recv_sems.at[3 * a + k], device_id=(px, py, c), device_id_type=MESH)

    @pl.when(first)
    def _():
        for a, k in pairs:
            copy(a, k, me).start()

    @pl.when(last)
    def _():
        for a, k in pairs:
            px, py = chips[k]
            copy(a, k, 2 * px + py).wait_recv()
        for a, k in pairs:
            copy(a, k, me).wait_send()


def _dn_delta_fwd(qkv, gb, p, o_gain, *, name, send=()):
    T = qkv.shape[0]
    tb = min(DN_TB, T)
    nb, nc = T // tb, tb // DN_CHUNK
    H = DN_HEADS
    qscale = DN_DK ** -0.5
    ns = len(send)
    hp = DN_HEADS_FWD

    def body(*refs):
        q_ref, k_ref, v_ref, gb_ref, gate_ref, gain_ref = refs[:6]
        o_ref, og_ref, st_ref, inv_ref = refs[6 + ns:10 + ns]
        s_ref = refs[10 + 2 * ns]
        pair, blk = pl.program_id(0), pl.program_id(1)
        if ns:
            _halves_over_ici(refs[6:6 + ns], refs[10 + ns:10 + 2 * ns], refs[11 + 2 * ns], refs[12 + 2 * ns],
                             jnp.logical_and(pair == 0, blk == 0),
                             jnp.logical_and(pair == H // hp - 1, blk == nb - 1))

        @pl.when(blk == 0)
        def _():
            s_ref[...] = jnp.zeros_like(s_ref)

        gbv = gb_ref[...]
        ts, ku, op = [], [], []
        for e in range(hp):
            qk_e, v_e = slice(e * DN_DK, (e + 1) * DN_DK), slice(e * DN_DV, (e + 1) * DN_DV)
            g, beta = _dn_head_cols(gbv, hp * pair + e)
            t = _dn_local(q_ref[:, qk_e] * qscale, k_ref[:, qk_e], v_ref[:, v_e], g, beta, nc)
            ts.append(t)
            for i in range(nc):
                inv_ref[e, i] = t["inv"][i]
            ku.append([_dot_tn(a, b) for a, b in zip(t["kd"], t["u"])])
            op.append([_dot(a, b) for a, b in zip(t["aqk"], t["u"])])
        s32 = [s_ref[e] for e in range(hp)]
        s_l = [[] for _ in range(hp)]
        for i in range(nc):
            sb = [_bf(x) for x in s32]
            for e in range(hp):
                st_ref[e, i] = sb[e]
                s_l[e].append(sb[e])
            prod = [_dot(ts[e]["kw"][i], sb[e]) for e in range(hp)]
            s32 = [s32[e] * jnp.exp(ts[e]["g_last"][i]) - prod[e] + ku[e][i] for e in range(hp)]
        for e in range(hp):
            s_ref[e] = s32[e]
        o = jnp.concatenate(
            [jnp.concatenate([_dot(qp, sb) + x for qp, sb, x in zip(ts[e]["qp"], s_l[e], op[e])], axis=0)
             for e in range(hp)], axis=1)
        o_ref[...] = o
        gain = gain_ref[...]
        for e in range(hp):
            v_e = slice(e * DN_DV, (e + 1) * DN_DV)
            oe = o[:, v_e]
            r = lax.rsqrt(jnp.mean(oe * oe, axis=-1, keepdims=True) + RMS_EPS)
            og_ref[:, v_e] = (((oe * r) * gain) * _silu(gate_ref[:, v_e])).astype(BF16)

    qk = lambda col0: pl.BlockSpec((tb, hp * DN_DK), lambda h, i: (i, col0 // (hp * DN_DK) + h))
    vblk = lambda col0: pl.BlockSpec((tb, hp * DN_DV), lambda h, i: (i, col0 // (hp * DN_DV) + h))
    return pl.pallas_call(
        body, name=name, grid=(H // hp, nb),
        in_specs=[qk(0), qk(DN_QK_W), vblk(2 * DN_QK_W), pl.BlockSpec((tb, LANES), lambda h, i: (i, 0)),
                  vblk(DN_CONV_W), pl.BlockSpec((1, DN_DV), lambda h, i: (0, 0))] + [HBM] * ns,
        out_specs=[vblk(0), vblk(0), pl.BlockSpec((hp, nc, DN_DK, DN_DV), lambda h, i: (h, i, 0, 0)),
                   pl.BlockSpec((hp, nc, DN_CHUNK, DN_CHUNK), lambda h, i: (h, i, 0, 0))] + [HBM] * ns,
        out_shape=[jax.ShapeDtypeStruct((T, DN_V_W), F32), jax.ShapeDtypeStruct((T, DN_V_W), BF16),
                   jax.ShapeDtypeStruct((H, T // DN_CHUNK, DN_DK, DN_DV), BF16),
                   jax.ShapeDtypeStruct((H, T // DN_CHUNK, DN_CHUNK, DN_CHUNK), F32)]
        + [jax.ShapeDtypeStruct((N_CHIPS,) + a.shape, a.dtype) for a in send],
        scratch_shapes=[pltpu.VMEM((hp, DN_DK, DN_DV), F32)]
        + ([pltpu.SemaphoreType.DMA((3 * ns,)), pltpu.SemaphoreType.DMA((3 * ns,))] if ns else []),
        compiler_params=pltpu.CompilerParams(dimension_semantics=("arbitrary", "arbitrary")),
    )(qkv, qkv, qkv, gb, p, o_gain, *send)


def _blocks_over_ici(p_refs, o_refs, send_sems, recv_sems, first, last):
    x, y, c = _mesh_pos()
    me = 2 * x + y
    chips = _other_chips(x, y)
    pairs = [(a, k) for a in range(len(p_refs)) for k in range(3)]

    def copy(a, k, slot):
        px, py = chips[k]
        return pltpu.make_async_remote_copy(
            src_ref=p_refs[a].at[2 * px + py], dst_ref=o_refs[a].at[slot], send_sem=send_sems.at[3 * a + k],
            recv_sem=recv_sems.at[3 * a + k], device_id=(px, py, c), device_id_type=MESH)

    @pl.when(first)
    def _():
        for a, k in pairs:
            copy(a, k, me).start()

    @pl.when(last)
    def _():
        for a, k in pairs:
            px, py = chips[k]
            copy(a, k, 2 * px + py).wait_recv()
        for a, k in pairs:
            copy(a, k, me).wait_send()


def _dn_delta_bwd(qkv, gb, p, o_gain, o, states, invs, dog, *, name, send=()):
    T = qkv.shape[0]
    tb = min(DN_TB, T)
    nb, nc = T // tb, tb // DN_CHUNK
    H = DN_HEADS
    qscale = DN_DK ** -0.5
    ns = len(send)
    hp = DN_HEADS_BWD

    def body(*refs):
        q_ref, k_ref, v_ref, gb_ref, gate_ref, gain_ref, o_ref, st_ref, inv_ref, dog_ref = refs[:10]
        dq_ref, dk_ref, dv_ref, dgate_ref, dgb_ref, dgain_ref = refs[10 + ns:16 + ns]
        ds_ref = refs[16 + 2 * ns]
        pair, blk = pl.program_id(0), pl.program_id(1)
        first = jnp.logical_and(pair == 0, blk == 0)
        if ns:
            _blocks_over_ici(refs[10:10 + ns], refs[16 + ns:16 + 2 * ns], refs[17 + 2 * ns], refs[18 + 2 * ns],
                             first, jnp.logical_and(pair == H // hp - 1, blk == nb - 1))

        @pl.when(blk == 0)
        def _():
            ds_ref[...] = jnp.zeros_like(ds_ref)

        @pl.when(first)
        def _():
            dgain_ref[...] = jnp.zeros_like(dgain_ref)

        lane = lax.broadcasted_iota(jnp.int32, (1, LANES), 1)
        c = DN_CHUNK
        cut = lambda x: [x[i * c:(i + 1) * c] for i in range(nc)]
        cat = lambda xs: jnp.concatenate(xs, axis=0)
        rsum = lambda x: jnp.sum(x, axis=-1, keepdims=True)
        gbv, gain = gb_ref[...], gain_ref[...]

        def before_chain(e):
            qk_e, v_e = slice(e * DN_DK, (e + 1) * DN_DK), slice(e * DN_DV, (e + 1) * DN_DV)
            g, beta = _dn_head_cols(gbv, hp * pair + e)
            ov, gate, dogv = o_ref[:, v_e], gate_ref[:, v_e], dog_ref[:, v_e]
            r = lax.rsqrt(jnp.mean(ov * ov, axis=-1, keepdims=True) + RMS_EPS)
            oh = ov * r
            dnrm = dogv * _silu(gate)
            dgate_ref[:, v_e] = dogv * (oh * gain) * _dsilu(gate)
            doh = dnrm * gain
            do_l = cut(r * (doh - oh * jnp.mean(doh * oh, axis=-1, keepdims=True)))
            dgain_ref[...] += jnp.sum(dnrm * oh, axis=0, keepdims=True)
            k, v = k_ref[:, qk_e], v_ref[:, v_e]
            t = _dn_local(q_ref[:, qk_e] * qscale, k, v, g, beta, nc, [inv_ref[e, i] for i in range(nc)])
            s_l = [st_ref[e, i] for i in range(nc)]
            vn_l = [u - _dot(w, sb) for u, w, sb in zip(t["u"], t["w"], s_l)]
            return dict(
                t=t, beta=beta, v=v, s=s_l, vn=vn_l, egl=[jnp.exp(x) for x in t["g_last"]],
                dqd=[_dot_nt(a, sb) for a, sb in zip(do_l, s_l)], daqk=[_dot_nt(a, b) for a, b in zip(do_l, vn_l)],
                aqk_do=[_dot_tn(a, b) for a, b in zip(t["aqk"], do_l)],
                qp_do=[_dot_tn(a, b) for a, b in zip(t["qp"], do_l)])

        hs = [before_chain(e) for e in range(hp)]
        ds = [ds_ref[e] for e in range(hp)]
        ds_l = [[None] * nc for _ in range(hp)]
        for i in reversed(range(nc)):
            for e in range(hp):
                ds_l[e][i] = ds[e]
            prod = [_dot_tn(hs[e]["t"]["kw"][i], ds[e]) for e in range(hp)]
            ds = [ds[e] * hs[e]["egl"][i] - prod[e] + hs[e]["qp_do"][i] for e in range(hp)]
        for e in range(hp):
            ds_ref[e] = ds[e]

        def after_chain(e):
            hd, t = hs[e], hs[e]["t"]
            lower, strict, eye = t["lower"], t["strict"], t["eye"]
            s_l, vn_l, dqd_l, daqk_l, egl_l, beta, v = (hd["s"], hd["vn"], hd["dqd"], hd["daqk"], hd["egl"],
                                                         hd["beta"], hd["v"])
            dvn_l = [a + _dot(kd, d) for a, kd, d in zip(hd["aqk_do"], t["kd"], ds_l[e])]
            dkd_l = [_dot_nt(a, d) for a, d in zip(vn_l, ds_l[e])]
            dgl_l = [jnp.sum(rsum(d * sb.astype(F32)), axis=0, keepdims=True) * x
                     for d, sb, x in zip(ds_l[e], s_l, egl_l)]
            dw_l = [-_dot_nt(a, sb) for a, sb in zip(dvn_l, s_l)]
            dbv_l = [_dot_tn(a, b) for a, b in zip(t["inv"], dvn_l)]
            dbk_l = [_dot_tn(a, b) for a, b in zip(t["inv"], dw_l)]
            dlow_l = [-(_dot_nt(a, b) + _dot_nt(x, y)) for a, b, x, y in zip(dbv_l, t["u"], dbk_l, t["w"])]
            m_l = [jnp.where(strict, a * d, 0.0) for a, d in zip(dlow_l, t["dec"])]
            nmat_l = [jnp.where(lower, a * d, 0.0) for a, d in zip(daqk_l, t["dec"])]
            dkb_l = [_dot(m, kk) + b * x for m, kk, b, x in zip(m_l, t["k"], dbk_l, t["eg"])]
            dqs_l = [_dot(n, kk) + a * x for n, kk, a, x in zip(nmat_l, t["k"], dqd_l, t["eg"])]
            dk1_l = [_dot_tn(m, kb) for m, kb in zip(m_l, t["kb"])]
            dk2_l = [_dot_tn(n, q) for n, q in zip(nmat_l, t["qs"])]
            beta_l, v_l = cut(beta), cut(v)
            rowid = lax.broadcasted_iota(jnp.int32, (c, 1), 0)
            dk_l, dg_l, dbeta_l = [], [], []
            for i in range(nc):
                dk_l.append(dk1_l[i] + dk2_l[i] + dkd_l[i] * t["ekd"][i] + dkb_l[i] * beta_l[i])
                gmat = jnp.where(strict, dlow_l[i] * t["low"][i], 0.0) + daqk_l[i] * t["aqk"][i]
                s_kd = rsum(dkd_l[i] * t["kd"][i])
                dg = (rsum(gmat) + rsum(dqd_l[i] * t["qd"][i]) - s_kd + rsum(dbk_l[i] * t["rhs_k"][i]))
                dg_row = -jnp.sum(gmat, axis=0, keepdims=True)
                dg = dg + rsum(jnp.where(eye, dg_row, 0.0))
                dgl = dgl_l[i] + jnp.sum(s_kd, axis=0, keepdims=True)
                dg_l.append(dg + jnp.where(rowid == c - 1, dgl, 0.0))
                dbeta_l.append(rsum(dbv_l[i] * v_l[i]) + rsum(dkb_l[i] * t["k"][i]))
            head = hp * pair + e
            dgb = (jnp.where(lane == head, cat(dg_l), 0.0) + jnp.where(lane == head + DN_HEADS, cat(dbeta_l), 0.0))
            return cat(dqs_l) * qscale, cat(dk_l), cat(dbv_l) * beta, dgb

        for e in range(hp):
            dq, dk, dv, dgb = after_chain(e)
            dq_ref[:, e * DN_DK:(e + 1) * DN_DK] = dq
            dk_ref[:, e * DN_DK:(e + 1) * DN_DK] = dk
            dv_ref[:, e * DN_DV:(e + 1) * DN_DV] = dv
            dgb_ref[e] = dgb

    rev = lambda i: nb - 1 - i
    qk = lambda col0: pl.BlockSpec((tb, hp * DN_DK), lambda h, i: (rev(i), col0 // (hp * DN_DK) + h))
    vblk = lambda col0: pl.BlockSpec((tb, hp * DN_DV), lambda h, i: (rev(i), col0 // (hp * DN_DV) + h))
    gain_spec = pl.BlockSpec((1, DN_DV), lambda h, i: (0, 0))
    return pl.pallas_call(
        body, name=name, grid=(H // hp, nb),
        in_specs=[qk(0), qk(DN_QK_W), vblk(2 * DN_QK_W), pl.BlockSpec((tb, LANES), lambda h, i: (rev(i), 0)),
                  vblk(DN_CONV_W), gain_spec, vblk(0),
                  pl.BlockSpec((hp, nc, DN_DK, DN_DV), lambda h, i: (h, rev(i), 0, 0)),
                  pl.BlockSpec((hp, nc, DN_CHUNK, DN_CHUNK), lambda h, i: (h, rev(i), 0, 0)), vblk(0)] + [HBM] * ns,
        out_specs=[qk(0), qk(0), vblk(0), vblk(DN_CONV_W),
                   pl.BlockSpec((hp, tb, LANES), lambda h, i: (h, rev(i), 0)), gain_spec] + [HBM] * ns,
        out_shape=[jax.ShapeDtypeStruct((T, DN_QK_W), F32), jax.ShapeDtypeStruct((T, DN_QK_W), F32),
                   jax.ShapeDtypeStruct((T, DN_V_W), F32), jax.ShapeDtypeStruct((T, DN_IN_PAD), F32),
                   jax.ShapeDtypeStruct((H, T, LANES), F32), jax.ShapeDtypeStruct((1, DN_DV), F32)]
        + [jax.ShapeDtypeStruct(a.shape, a.dtype) for a in send],
        scratch_shapes=[pltpu.VMEM((hp, DN_DK, DN_DV), F32)]
        + ([pltpu.SemaphoreType.DMA((3 * ns,)), pltpu.SemaphoreType.DMA((3 * ns,))] if ns else []),
        compiler_params=pltpu.CompilerParams(dimension_semantics=("arbitrary", "arbitrary")),
    )(qkv, qkv, qkv, gb, p, o_gain, o, states, invs, dog, *send)


def _dn_conv_bwd(p, conv_w, d, dp, *, first, normed, name):
    T, width = d.shape

    def body(p_ref, w_ref, d_ref, dp_in, dp_ref, dw_ref):
        del dp_in
        x = p_ref[...]
        ksz = w_ref.shape[0]
        xs = [_shift_down(x, ksz - 1 - i) for i in range(ksz)]
        xc = sum(w_ref[i:i + 1, :] * xs[i] for i in range(ksz))
        ds = d_ref[...]
        if normed:
            s = _silu(xc)
            r = lax.rsqrt(jnp.sum(s * s, axis=-1, keepdims=True) + L2_EPS)
            y = s * r
            ds = r * (ds - y * jnp.sum(ds * y, axis=-1, keepdims=True))
        dxc = ds * _dsilu(xc)
        dp_ref[...] = sum(w_ref[i:i + 1, :] * _shift_up(dxc, ksz - 1 - i) for i in range(ksz))
        for i in range(ksz):
            dw_ref[i:i + 1, :] = jnp.sum(dxc * xs[i], axis=0, keepdims=True)

    shifted = pl.BlockSpec((T, LANES), lambda j: (0, first + j))
    return pl.pallas_call(
        body, name=name, grid=(width // LANES,),
        in_specs=[shifted, pl.BlockSpec((DN_CONV, LANES), lambda j: (0, first + j)),
                  pl.BlockSpec((T, LANES), lambda j: (0, j)), pl.BlockSpec(memory_space=pl.ANY)],
        out_specs=[shifted, pl.BlockSpec((DN_CONV, LANES), lambda j: (0, j))],
        out_shape=[jax.ShapeDtypeStruct(dp.shape, F32), jax.ShapeDtypeStruct((DN_CONV, width), F32)],
        input_output_aliases={3: 0},
        compiler_params=pltpu.CompilerParams(dimension_semantics=("parallel",), vmem_limit_bytes=VMEM_BIG),
    )(p, conv_w, d, dp)


def _dn_ab_bwd(p, alog_row, dtb_row, dgb, dp, *, name):
    T = p.shape[0]
    rows = min(DN_TB, T)
    H = DN_HEADS

    def body(p_ref, al_ref, dt_ref, dgb_ref, dp_in, dp_ref, dal_ref, ddt_ref):
        del dp_in

        @pl.when(pl.program_id(0) == 0)
        def _():
            dal_ref[...] = jnp.zeros_like(dal_ref)
            ddt_ref[...] = jnp.zeros_like(ddt_ref)

        blk = p_ref[...]
        is_a, is_b, a_arg, neg_exp, log_a, beta = _dn_ab_parts(blk, al_ref[...], dt_ref[...])
        d = dgb_ref[0]
        for hh in range(1, H):
            d = d + dgb_ref[hh]
        hi, mid, lo_ = _split3(jnp.where(is_a, d, 0.0))
        tri = _dn_chunk_tri(rows, upper=True)
        f = lambda q: jnp.dot(tri, q, preferred_element_type=F32)
        dlog_a = f(hi) + f(mid) + f(lo_)
        da_in = dlog_a * neg_exp * _sigmoid(a_arg)
        db_in = jnp.where(is_b, d, 0.0) * beta * (1.0 - beta)
        dp_ref[...] = jnp.where(is_a, da_in, 0.0) + db_in
        dal_ref[...] += jnp.sum(dlog_a * log_a, axis=0, keepdims=True)
        ddt_ref[...] += jnp.sum(jnp.where(is_a, da_in, 0.0), axis=0, keepdims=True)

    blk = pl.BlockSpec((rows, LANES), lambda i: (i, DN_AB_COL))
    vec = pl.BlockSpec((1, LANES), lambda i: (0, 0))
    return pl.pallas_call(
        body, name=name, grid=(T // rows,),
        in_specs=[blk, vec, vec, pl.BlockSpec((H, rows, LANES), lambda i: (0, i, 0)),
                  pl.BlockSpec(memory_space=pl.ANY)],
        out_specs=[blk, vec, vec],
        out_shape=[jax.ShapeDtypeStruct(dp.shape, F32), jax.ShapeDtypeStruct((1, LANES), F32),
                   jax.ShapeDtypeStruct((1, LANES), F32)],
        input_output_aliases={4: 0},
        compiler_params=pltpu.CompilerParams(dimension_semantics=("arbitrary",)),
    )(p, alog_row, dtb_row, dgb, dp)


def _dn_layer_fwd(x, ng, w_in, conv_w, a_log, dt_bias, o_gain, w_out, tag, send=()):
    alog_row, dtb_row = _dn_lane_rows(a_log, dt_bias)
    gain = o_gain.reshape(1, DN_DV)
    h = _rmsnorm_fwd(x, ng, name=f"{tag}_norm")
    p = _matmul(h, w_in, mode="nn", name=f"{tag}_inproj")
    qkv = _dn_prep_fwd(p, conv_w, name=f"{tag}_prep")
    gb = _dn_ab_fwd(p, alog_row, dtb_row, name=f"{tag}_ab")
    o, og, states, invs, *landed = _dn_delta_fwd(qkv, gb, p, gain, name=f"{tag}_delta", send=send)
    x_new = _matmul(og, w_out, mode="nn", res=x, name=f"{tag}_outproj")
    return x_new, (h, p, qkv, gb, o, og, states, invs), landed


def _dn_layer_bwd(dx, x, ng, w_in, conv_w, a_log, dt_bias, o_gain, w_out, saved, tag, send=(), send_dwin=(),
                  chip_sums=None):
    h, p, qkv, gb, o, og, states, invs = saved
    alog_row, dtb_row = _dn_lane_rows(a_log, dt_bias)
    gain = o_gain.reshape(1, DN_DV)
    d_wout = _matmul(og, dx, mode="tn", out_dtype=BF16, name=f"{tag}_dwout")
    if chip_sums is not None:
        d_wout, = chip_sums([_cut2(_by_rows(d_wout))], f"{tag}wout")
        send = list(send) + [d_wout]
    dog = _matmul(dx, w_out, mode="nt", name=f"{tag}_dog")
    dq, dk, dv, dp, dgb, dgain, *landed = _dn_delta_bwd(qkv, gb, p, gain, o, states, invs, dog,
                                                        name=f"{tag}_deltabwd", send=send)
    n_qk = DN_QK_W // LANES
    dp, dconv_q = _dn_conv_bwd(p, conv_w, dq, dp, first=0, normed=True, name=f"{tag}_convbwd_q")
    dp, dconv_k = _dn_conv_bwd(p, conv_w, dk, dp, first=n_qk, normed=True, name=f"{tag}_convbwd_k")
    dp, dconv_v = _dn_conv_bwd(p, conv_w, dv, dp, first=2 * n_qk, normed=False, name=f"{tag}_convbwd_v")
    dconv = jnp.concatenate([dconv_q, dconv_k, dconv_v], axis=1)
    dp, dal, ddt = _dn_ab_bwd(p, alog_row, dtb_row, dgb, dp, name=f"{tag}_abbwd")
    d_win = _matmul(h, dp, mode="tn", name=f"{tag}_dwin", send=send_dwin)
    if send_dwin:
        d_win, *landed_dwin = d_win
        landed = landed + landed_dwin
    if chip_sums is not None:
        d_win, = chip_sums([_cut2(_by_cols(d_win))], f"{tag}win")
        dh, landed_win = _matmul(dp, w_in, mode="nt", name=f"{tag}_dh", send=[d_win])
        landed = landed + [landed_win]
    else:
        dh = _matmul(dp, w_in, mode="nt", name=f"{tag}_dh")
    dx_prev, dng = _rmsnorm_bwd(x, ng, dh, dx, name=f"{tag}_normbwd")
    return dx_prev, dng, d_win, dconv, dal[0, :DN_HEADS], ddt[0, :DN_HEADS], dgain[0], d_wout, landed


def _by_cols(dw):
    return _split(dw[:, :DN_IN].astype(BF16), 1)


def _by_rows(dw):
    return dw.reshape(N_CHIPS, -1, dw.shape[-1])


def _cut2(g4):
    return g4.reshape(N_CHIPS, 2, -1, g4.shape[-1])


def _sb_gains(g):
    return jnp.concatenate([g, g]).reshape(1, LANES)


def _sb_layer_fwd(x, ng, w_in, gq, gk, w_out, tag, send=()):
    h = _rmsnorm_fwd(x, ng, name=f"{tag}_norm")
    p3 = _matmul(h, w_in, mode="nn", b_parts=4, out_parts=4, name=f"{tag}_inproj")
    og, o, lsum, live, *landed = _sb_attn_fwd(p3, _sb_gains(gq), _sb_gains(gk), name=f"{tag}_attn", send=send)
    x_new = _matmul(og, w_out, mode="nn", res=x, name=f"{tag}_outproj")
    return x_new, (h, p3, og, o, lsum, live), landed


def _sb_layer_bwd(dx, x, ng, w_in, gq, gk, w_out, saved, tag):
    h, p3, og, o, lsum, live = saved
    d_wout = _matmul(og, dx, mode="tn", out_dtype=BF16, name=f"{tag}_dwout")
    dog = _matmul(dx, w_out, mode="nt", name=f"{tag}_dog")
    dp3, dgq, dgk = _sb_attn_bwd(p3, _sb_gains(gq), _sb_gains(gk), o, lsum, live, dog, name=f"{tag}_attnbwd")
    fold = lambda d: jnp.sum(d.reshape(-1, SB_DH), axis=0)
    d_win = _matmul(h, dp3, mode="tn", b_parts=4, out_parts=4, out_dtype=BF16, name=f"{tag}_dwin")
    dh = _matmul(dp3, w_in, mode="nt", a_parts=4, b_parts=4, name=f"{tag}_dh")
    dx_prev, dng = _rmsnorm_bwd(x, ng, dh, dx, name=f"{tag}_normbwd")
    return dx_prev, dng, d_win, fold(dgq), fold(dgk), d_wout


N_CHIPS = 4
HBM = pl.BlockSpec(memory_space=pl.ANY)


def _mesh_pos():
    return lax.axis_index("x"), lax.axis_index("y"), lax.axis_index("c")


def _other_chips(x, y):
    return [(1 - x, y), (x, 1 - y), (1 - x, 1 - y)]


def _chip_exchange(srcs, *, send_slot_is_dest, copy_own, name):
    n = len(srcs)

    def body(*refs):
        src_refs, out_refs = refs[:n], refs[n:2 * n]
        send_sems, recv_sems, local_sems = refs[2 * n:]
        x, y, c = _mesh_pos()
        me = 2 * x + y
        chips = _other_chips(x, y)
        local = []
        for a in range(n):
            if not copy_own[a]:
                continue
            own = src_refs[a].at[me] if send_slot_is_dest else src_refs[a]
            local.append(pltpu.make_async_copy(own, out_refs[a].at[me], local_sems.at[a]))
        for cp in local:
            cp.start()

        def copy(a, k, landing_slot):
            px, py = chips[k]
            src = src_refs[a].at[2 * px + py] if send_slot_is_dest else src_refs[a]
            return pltpu.make_async_remote_copy(
                src_ref=src, dst_ref=out_refs[a].at[landing_slot],
                send_sem=send_sems.at[a * 3 + k], recv_sem=recv_sems.at[a * 3 + k],
                device_id=(px, py, c), device_id_type=MESH)

        sends = [copy(a, k, me) for a in range(n) for k in range(3)]
        for cp in sends:
            cp.start()
        for a in range(n):
            for k in range(3):
                px, py = chips[k]
                copy(a, k, 2 * px + py).wait_recv()
        for cp in sends:
            cp.wait_send()
        for cp in local:
            cp.wait()

    outs = []
    for s in srcs:
        shape = s.shape if send_slot_is_dest else (N_CHIPS,) + s.shape
        outs.append(jax.ShapeDtypeStruct(shape, s.dtype))
    return pl.pallas_call(
        body, name=name, in_specs=[HBM] * n, out_specs=[HBM] * n, out_shape=outs,
        scratch_shapes=[pltpu.SemaphoreType.DMA((3 * n,)), pltpu.SemaphoreType.DMA((3 * n,)),
                        pltpu.SemaphoreType.DMA((n,))],
    )(*srcs)


def _sibling_exchange(srcs, *, name):
    n = len(srcs)

    def body(*refs):
        src_refs, out_refs = refs[:n], refs[n:2 * n]
        send_sems, recv_sems = refs[2 * n:]
        x, y, c = _mesh_pos()
        copies = [pltpu.make_async_remote_copy(
            src_ref=src_refs[a], dst_ref=out_refs[a], send_sem=send_sems.at[a], recv_sem=recv_sems.at[a],
            device_id=(x, y, 1 - c), device_id_type=MESH) for a in range(n)]
        for cp in copies:
            cp.start()
        for cp in copies:
            cp.wait()

    return pl.pallas_call(
        body, name=name, in_specs=[HBM] * n, out_specs=[HBM] * n,
        out_shape=[jax.ShapeDtypeStruct(s.shape, s.dtype) for s in srcs],
        scratch_shapes=[pltpu.SemaphoreType.DMA((n,)), pltpu.SemaphoreType.DMA((n,))],
    )(*srcs)


def _gather_halves(shards, small, *, name):
    n = len(shards)

    def body(*refs):
        s_refs, small_ref = refs[:n], refs[n]
        o_refs, osmall_ref = refs[n + 1:2 * n + 1], refs[2 * n + 1]
        send_sems, recv_sems, local_sems = refs[2 * n + 2:]
        x, y, c = _mesh_pos()
        me = 2 * x + y
        chips = _other_chips(x, y)
        local = [pltpu.make_async_copy(small_ref, osmall_ref.at[me], local_sems.at[0])]
        for cp in local:
            cp.start()

        def over_ici(a, k, slot):
            px, py = chips[k]
            return pltpu.make_async_remote_copy(
                src_ref=s_refs[a].at[c], dst_ref=o_refs[a].at[slot, c], send_sem=send_sems.at[3 * a + k],
                recv_sem=recv_sems.at[3 * a + k], device_id=(px, py, c), device_id_type=MESH)

        def small_copy(k, slot):
            px, py = chips[k]
            return pltpu.make_async_remote_copy(
                src_ref=small_ref, dst_ref=osmall_ref.at[slot], send_sem=send_sems.at[3 * n + k],
                recv_sem=recv_sems.at[3 * n + k], device_id=(px, py, c), device_id_type=MESH)

        def to_sibling(a, k, half):
            px, py = chips[k]
            blk = o_refs[a].at[2 * px + py, half]
            return pltpu.make_async_remote_copy(
                src_ref=blk, dst_ref=blk, send_sem=send_sems.at[3 * n + 3 + 3 * a + k],
                recv_sem=recv_sems.at[3 * n + 3 + 3 * a + k], device_id=(x, y, 1 - c), device_id_type=MESH)

        sends = [over_ici(a, k, me) for a in range(n) for k in range(3)] + [small_copy(k, me) for k in range(3)]
        for cp in sends:
            cp.start()
        passed = []
        for a in range(n):
            for k in range(3):
                px, py = chips[k]
                over_ici(a, k, 2 * px + py).wait_recv()
                passed.append(to_sibling(a, k, c))
                passed[-1].start()
        for k in range(3):
            px, py = chips[k]
            small_copy(k, 2 * px + py).wait_recv()
        for a in range(n):
            for k in range(3):
                to_sibling(a, k, 1 - c).wait_recv()
        for cp in sends + passed:
            cp.wait_send()
        for cp in local:
            cp.wait()

    nsem = 6 * n + 3
    return pl.pallas_call(
        body, name=name, in_specs=[HBM] * (n + 1), out_specs=[HBM] * (n + 1),
        out_shape=[jax.ShapeDtypeStruct((N_CHIPS,) + s.shape, s.dtype) for s in shards + [small]],
        scratch_shapes=[pltpu.SemaphoreType.DMA((nsem,)), pltpu.SemaphoreType.DMA((nsem,)),
                        pltpu.SemaphoreType.DMA((1,))],
    )(*shards, small)


def _forward_halves(landed, *, name):
    n = len(landed)

    def body(*refs):
        o_refs = refs[n:2 * n]
        send_sems, recv_sems = refs[2 * n:]
        x, y, c = _mesh_pos()
        chips = _other_chips(x, y)
        pairs = [(a, k) for a in range(n) for k in range(3)]

        def copy(a, k, half):
            px, py = chips[k]
            blk = o_refs[a].at[2 * px + py, half]
            return pltpu.make_async_remote_copy(
                src_ref=blk, dst_ref=blk, send_sem=send_sems.at[3 * a + k], recv_sem=recv_sems.at[3 * a + k],
                device_id=(x, y, 1 - c), device_id_type=MESH)

        sends = [copy(a, k, c) for a, k in pairs]
        for cp in sends:
            cp.start()
        for a, k in pairs:
            copy(a, k, 1 - c).wait_recv()
        for cp in sends:
            cp.wait_send()

    return pl.pallas_call(
        body, name=name, in_specs=[HBM] * n, out_specs=[HBM] * n,
        out_shape=[jax.ShapeDtypeStruct(a.shape, a.dtype) for a in landed],
        input_output_aliases={a: a for a in range(n)},
        scratch_shapes=[pltpu.SemaphoreType.DMA((3 * n,)), pltpu.SemaphoreType.DMA((3 * n,))],
    )(*landed)


def _swap_other_half(g_list, *, name):
    n = len(g_list)

    def body(*refs):
        g_refs, o_refs = refs[:n], refs[n:2 * n]
        send_sems, recv_sems = refs[2 * n:]
        x, y, c = _mesh_pos()
        copies = [pltpu.make_async_remote_copy(
            src_ref=g_refs[a].at[:, 1 - c], dst_ref=o_refs[a], send_sem=send_sems.at[a], recv_sem=recv_sems.at[a],
            device_id=(x, y, 1 - c), device_id_type=MESH) for a in range(n)]
        for cp in copies:
            cp.start()
        for cp in copies:
            cp.wait()

    return pl.pallas_call(
        body, name=name, in_specs=[HBM] * n, out_specs=[HBM] * n,
        out_shape=[jax.ShapeDtypeStruct((g.shape[0],) + g.shape[2:], g.dtype) for g in g_list],
        scratch_shapes=[pltpu.SemaphoreType.DMA((n,)), pltpu.SemaphoreType.DMA((n,))],
    )(*g_list)


def _row_tile(r):
    return _pick(r, (512, 256, 128, 64, 32, 16, 8))


def _add_my_half(g4, sib4, core, *, name):
    n, _, r, C = g4.shape
    tr = _row_tile(r)

    def body(core_ref, g_ref, s_ref, o_ref):
        del core_ref
        o_ref[...] = (g_ref[...].astype(F32) + s_ref[...].astype(F32)).astype(o_ref.dtype)

    return pl.pallas_call(
        body, name=name,
        grid_spec=pltpu.PrefetchScalarGridSpec(
            num_scalar_prefetch=1, grid=(n, r // tr),
            in_specs=[pl.BlockSpec((None, None, tr, C), lambda j, i, core_ref: (j, core_ref[0], i, 0)),
                      pl.BlockSpec((None, tr, C), lambda j, i, core_ref: (j, i, 0))],
            out_specs=pl.BlockSpec((None, tr, C), lambda j, i, core_ref: (j, i, 0))),
        out_shape=jax.ShapeDtypeStruct((n, r, C), g4.dtype),
        compiler_params=pltpu.CompilerParams(dimension_semantics=("parallel", "parallel")),
    )(core, g4, sib4)


def _sum_chips(landed, part, me, *, name):
    _, r, C = landed.shape
    tr = _row_tile(r)

    def body(me_ref, own_ref, r1_ref, r2_ref, r3_ref, o_ref):
        del me_ref
        f = lambda ref: ref[...].astype(F32)
        o_ref[...] = ((f(own_ref) + f(r1_ref)) + f(r2_ref)) + f(r3_ref)

    slot = lambda d: pl.BlockSpec((None, tr, C), lambda i, me_ref: ((me_ref[0] + d) % N_CHIPS, i, 0))
    return pl.pallas_call(
        body, name=name,
        grid_spec=pltpu.PrefetchScalarGridSpec(
            num_scalar_prefetch=1, grid=(r // tr,), in_specs=[slot(0), slot(1), slot(2), slot(3)],
            out_specs=pl.BlockSpec((tr, C), lambda i, me_ref: (i, 0))),
        out_shape=jax.ShapeDtypeStruct((r, C), F32),
        compiler_params=pltpu.CompilerParams(dimension_semantics=("parallel",)),
    )(me, part, landed, landed, landed)


def _adamw_halves(w, mine, theirs, m, v, core, *, layer, prev, name):
    shape = w.shape
    r, C = mine.shape
    tr = _pick(r, (128, 64, 32, 16, 8))
    per = r // tr
    view = lambda a: a.reshape(-1, C)
    n_prev = 0 if prev is None else 4

    def body(*refs):
        core_ref, w_ref, gm_ref, gt_ref, m_ref, v_ref = refs[:6]
        g_ref, d_ref, nm_ref, nv_ref = refs[6 + n_prev:]
        gv = jnp.where(pl.program_id(0) == core_ref[0], gm_ref[...], gt_ref[...])
        g_ref[...] = gv
        d_ref[...], nm_ref[...], nv_ref[...] = _adamw_math(w_ref[...], gv, m_ref[...], v_ref[...])

    half = pl.BlockSpec((tr, C), lambda h, i, core_ref: ((2 * layer + h) * per + i, 0))
    row = pl.BlockSpec((tr, C), lambda h, i, core_ref: (i, 0))
    out = jax.ShapeDtypeStruct((math.prod(shape) // C, C), F32)
    res = pl.pallas_call(
        body, name=name,
        grid_spec=pltpu.PrefetchScalarGridSpec(
            num_scalar_prefetch=1, grid=(2, per), in_specs=[half, row, row, half, half] + [HBM] * n_prev,
            out_specs=[half] * 4),
        out_shape=[out] * 4,
        input_output_aliases={6 + j: j for j in range(n_prev)},
        compiler_params=pltpu.CompilerParams(dimension_semantics=("parallel", "parallel")),
    )(core, view(w), mine, theirs, view(m), view(v), *([] if prev is None else [view(a) for a in prev]))
    return tuple(a.reshape(shape) for a in res)


def _sum_small(recv4, *, name):
    _, R, C = recv4.shape

    def body(r_ref, o_ref):
        o_ref[...] = ((r_ref[0] + r_ref[1]) + r_ref[2]) + r_ref[3]

    return pl.pallas_call(body, name=name, out_shape=jax.ShapeDtypeStruct((R, C), F32))(recv4)


def _add(a, b, *, name):
    R, C = a.shape
    tr = _pick(R, (512, 256, 128, 64, 32, 16, 8))
    blk = pl.BlockSpec((tr, C), lambda i: (i, 0))

    def body(a_ref, b_ref, o_ref):
        o_ref[...] = a_ref[...] + b_ref[...]

    return pl.pallas_call(body, name=name, grid=(R // tr,), in_specs=[blk, blk], out_specs=blk,
                          out_shape=jax.ShapeDtypeStruct((R, C), F32),
                          compiler_params=pltpu.CompilerParams(dimension_semantics=("parallel",)))(a, b)


def _adamw_math(w, g, m, v):
    nm = ADAM_B1 * m + (1.0 - ADAM_B1) * g
    nv = ADAM_B2 * v + (1.0 - ADAM_B2) * (g * g)
    m_hat = nm / (1.0 - ADAM_B1 ** ADAM_STEP)
    v_hat = nv / (1.0 - ADAM_B2 ** ADAM_STEP)
    return -ADAM_LR * (m_hat / (jnp.sqrt(v_hat) + ADAM_EPS) + ADAM_WD * w), nm, nv


def _adamw(w, g, m, v, *, name):
    shape = w.shape
    C = shape[-1]
    R = w.size // C
    two = lambda a: a.reshape(R, C)
    tr = _pick(R, (256, 128, 64, 32, 16, 8)) if R % 8 == 0 and R > 8 else R
    blk = pl.BlockSpec((tr, C), lambda i: (i, 0))

    def body(w_ref, g_ref, m_ref, v_ref, d_ref, nm_ref, nv_ref):
        d_ref[...], nm_ref[...], nv_ref[...] = _adamw_math(w_ref[...], g_ref[...], m_ref[...], v_ref[...])

    out = jax.ShapeDtypeStruct((R, C), F32)
    d, nm, nv = pl.pallas_call(
        body, name=name, grid=(R // tr,), in_specs=[blk] * 4, out_specs=[blk] * 3, out_shape=[out] * 3,
        compiler_params=pltpu.CompilerParams(dimension_semantics=("parallel",)),
    )(two(w), two(g), two(m), two(v))
    return d.reshape(shape), nm.reshape(shape), nv.reshape(shape)


BIG = (("dn_w_in", (2, 1024, 1540), 2), ("dn_w_out", (2, 512, 1024), 1), ("sb_w_in", (1, 1024, 1024), 2),
       ("sb_w_out", (1, 256, 1024), 1), ("sc_w_in", (1, 1024, 2048), 2), ("sc_w_out", (1, 512, 1024), 1))
SMALL = (("dn_conv_w", (2, 4, 1024), 2), ("dn_o_norm_g", (2, 64), 1), ("sc_conv_w", (1, 3, 512), 2))
REPL = (("norm_g", (4, 1024)), ("dn_a_log", (2, 8)), ("dn_dt_bias", (2, 8)), ("sb_q_norm_g", (1, 64)),
        ("sb_k_norm_g", (1, 64)))


def _halves(shard):
    return shard.reshape(2, -1, shard.shape[-1])


def _pack(arrays, cols, lead=()):
    flat = jnp.concatenate([a.reshape(lead + (-1,)) for a in arrays], axis=-1)
    n = flat.shape[-1]
    rows = -(-n // cols)
    unit = 512 if rows > 512 else 8
    rows = -(-rows // unit) * unit
    flat = jnp.pad(flat, [(0, 0)] * len(lead) + [(0, rows * cols - n)])
    return flat.reshape(lead + (rows, cols))


def _unpack(buf, table, lead=()):
    flat = buf.reshape(lead + (-1,))
    out, off = {}, 0
    for entry in table:
        name, shape = entry[0], entry[1]
        n = math.prod(shape)
        out[name] = flat[..., off:off + n].reshape(lead + shape)
        off += n
    return out


def _join(shards, axis):
    return jnp.concatenate([shards[j] for j in range(N_CHIPS)], axis=axis)


def _split(full, axis):
    return jnp.stack(jnp.split(full, N_CHIPS, axis=axis), axis=0)


def kernel(x, norm_g, dn_w_in, dn_conv_w, dn_a_log, dn_dt_bias, dn_o_norm_g, dn_w_out, sb_w_in, sb_q_norm_g, sb_k_norm_g, sb_w_out, sc_w_in, sc_conv_w, sc_w_out, loss_target, m_norm_g, m_dn_w_in, m_dn_conv_w, m_dn_a_log, m_dn_dt_bias, m_dn_o_norm_g, m_dn_w_out, m_sb_w_in, m_sb_q_norm_g, m_sb_k_norm_g, m_sb_w_out, m_sc_w_in, m_sc_conv_w, m_sc_w_out, v_norm_g, v_dn_w_in, v_dn_conv_w, v_dn_a_log, v_dn_dt_bias, v_dn_o_norm_g, v_dn_w_out, v_sb_w_in, v_sb_q_norm_g, v_sb_k_norm_g, v_sb_w_out, v_sc_w_in, v_sc_conv_w, v_sc_w_out):
    weights = dict(norm_g=norm_g, dn_w_in=dn_w_in, dn_conv_w=dn_conv_w, dn_a_log=dn_a_log, dn_dt_bias=dn_dt_bias,
                   dn_o_norm_g=dn_o_norm_g, dn_w_out=dn_w_out, sb_w_in=sb_w_in, sb_q_norm_g=sb_q_norm_g,
                   sb_k_norm_g=sb_k_norm_g, sb_w_out=sb_w_out, sc_w_in=sc_w_in, sc_conv_w=sc_conv_w, sc_w_out=sc_w_out)
    m_in = dict(norm_g=m_norm_g, dn_w_in=m_dn_w_in, dn_conv_w=m_dn_conv_w, dn_a_log=m_dn_a_log,
                dn_dt_bias=m_dn_dt_bias, dn_o_norm_g=m_dn_o_norm_g, dn_w_out=m_dn_w_out, sb_w_in=m_sb_w_in,
                sb_q_norm_g=m_sb_q_norm_g, sb_k_norm_g=m_sb_k_norm_g, sb_w_out=m_sb_w_out, sc_w_in=m_sc_w_in,
                sc_conv_w=m_sc_conv_w, sc_w_out=m_sc_w_out)
    v_in = dict(norm_g=v_norm_g, dn_w_in=v_dn_w_in, dn_conv_w=v_dn_conv_w, dn_a_log=v_dn_a_log,
                dn_dt_bias=v_dn_dt_bias, dn_o_norm_g=v_dn_o_norm_g, dn_w_out=v_dn_w_out, sb_w_in=v_sb_w_in,
                sb_q_norm_g=v_sb_q_norm_g, sb_k_norm_g=v_sb_k_norm_g, sb_w_out=v_sb_w_out, sc_w_in=v_sc_w_in,
                sc_conv_w=v_sc_conv_w, sc_w_out=v_sc_w_out)
    order = list(weights)
    xi, yi, ci = _mesh_pos()

    small = _pack([weights[n] for n, _, _ in SMALL], LANES)
    later = [("dn_w_in", 1), ("dn_w_out", 1), ("sb_w_in", 0), ("sb_w_out", 0), ("sc_w_in", 0), ("sc_w_out", 0)]
    piece = lambda n, l: _halves(weights[n][l].astype(BF16)[None])
    own_first = [piece("dn_w_in", 0), piece("dn_w_out", 0)]
    own_later = [piece(n, l) for n, l in later]
    own_last, own_mid = own_later[:2], own_later[2:]
    me = 2 * xi + yi
    whole = lambda g4, own: lax.dynamic_update_index_in_dim(g4, own, me, 0)
    flat = lambda g4: g4.reshape(N_CHIPS, -1, g4.shape[-1])
    rows_of = lambda w4: w4.reshape(-1, w4.shape[-1])
    dn_in = lambda w4: jnp.pad(_join(w4, 1), ((0, 0), (0, DN_IN_PAD - DN_IN)))
    w_in0, w_out0, small4 = _gather_halves(own_first, small, name="gather_first")
    full = {n: _join(a, ax) for (n, _, ax), a in zip(SMALL, _unpack(small4, SMALL, (N_CHIPS,)).values())}

    def dn_args(j, w_in4, w_out4):
        return (dn_in(flat(w_in4)), full["dn_conv_w"][j], dn_a_log[j], dn_dt_bias[j], full["dn_o_norm_g"][j],
                rows_of(w_out4))

    x0 = x[0]
    dn0 = dn_args(0, whole(w_in0, own_first[0]), whole(w_out0, own_first[1]))
    x1, s0, landed = _dn_layer_fwd(x0, norm_g[0], *dn0, "l0", send=own_mid)
    landed = _forward_halves(landed, name="forward_halves_mid")
    sb_in, sb_out, sc_in, sc_out = [whole(g4, own) for g4, own in zip(landed, own_mid)]
    sb_args = (flat(sb_in), sb_q_norm_g[0], sb_k_norm_g[0], rows_of(sb_out))
    sc_args = (flat(sc_in), full["sc_conv_w"][0], rows_of(sc_out))
    x2, s1, landed = _sb_layer_fwd(x1, norm_g[1], *sb_args, "l1", send=own_last)
    landed = _forward_halves(landed, name="forward_halves_last")
    dn1 = dn_args(1, *[whole(g4, own) for g4, own in zip(landed, own_last)])
    x3, s2 = _sc_layer_fwd(x2, norm_g[2], *sc_args, "l2")
    x4, s3, _ = _dn_layer_fwd(x3, norm_g[3], *dn1, "l3")
    dy, loss_local = _loss_head(x4, loss_target[0], name="loss_head")
    loss = lax.psum(loss_local[0, 0], ("x", "y", "c"))

    core = ci.astype(jnp.int32).reshape(1)
    chip = me.astype(jnp.int32).reshape(1)

    def chip_sums(g_list, tag):
        sib = _swap_other_half(g_list, name=f"swap_halves_{tag}")
        return [_add_my_half(g, s, core, name=f"sum_cores_{tag}{i}") for i, (g, s) in enumerate(zip(g_list, sib))]

    dx3, dng3, dwin3, dconv3, dal3, ddt3, dgain3, dwout3, _ = _dn_layer_bwd(dy, x3, norm_g[3], *dn1, s3, "l3")
    dx2, dng2, dwin2, dconv2, dwout2 = _sc_layer_bwd(dx3, x2, norm_g[2], *sc_args, s2, "l2")
    dx1, dng1, dwin1, dgq, dgk, dwout1 = _sb_layer_bwd(dx2, x1, norm_g[1], *sb_args, s1, "l1")
    part_later = chip_sums([_cut2(_by_cols(dwin3)), _cut2(_by_rows(dwout3)), _cut2(dwin1), _cut2(_by_rows(dwout1)),
                            _cut2(dwin2), _cut2(_by_rows(dwout2))], "later")
    dx0, dng0, part_win0, dconv0, dal0, ddt0, dgain0, part_wout0, landed = _dn_layer_bwd(
        dx1, x0, norm_g[0], *dn0, s0, "l0", send=part_later[2:], send_dwin=part_later[:2], chip_sums=chip_sums)
    pieces = later[2:] + [("dn_w_out", 0)] + later[:2] + [("dn_w_in", 0)]
    mine = {(n, l): _sum_chips(r, p, chip, name=f"sum_chips_{n}{l}")
            for (n, l), r, p in zip(pieces, landed, part_later[2:] + [part_wout0] + part_later[:2] + [part_win0])}
    pieces = sorted(pieces, key=lambda nl: nl[1])
    mine = [mine[nl] for nl in pieces]
    theirs = _sibling_exchange(mine, name="swap_results")
    upd = {}
    for (n, l), a, b in zip(pieces, mine, theirs):
        upd[n] = _adamw_halves(weights[n], a, b, m_in[n], v_in[n], core, layer=l, prev=upd.get(n),
                               name=f"adamw_{n}{l}")
    g_out = {n: upd[n][0] for n, _, _ in BIG}

    grads = dict(
        norm_g=jnp.concatenate([dng0, dng1, dng2, dng3], axis=0), dn_conv_w=jnp.stack([dconv0, dconv3]),
        dn_a_log=jnp.stack([dal0, dal3]), dn_dt_bias=jnp.stack([ddt0, ddt3]),
        dn_o_norm_g=jnp.stack([dgain0, dgain3]), sb_q_norm_g=dgq[None], sb_k_norm_g=dgk[None],
        sc_conv_w=dconv2[None])
    repl = [jnp.broadcast_to(grads[n][None], (N_CHIPS,) + s) for n, s in REPL]
    gsmall = _pack([_split(grads[n], ax) for n, _, ax in SMALL] + repl, LANES, (N_CHIPS,))
    rsmall, = _chip_exchange([gsmall], send_slot_is_dest=True, copy_own=(True,), name="scatter_small")
    psmall = _sum_small(rsmall, name="sum_chips_small")
    qsmall, = _sibling_exchange([psmall], name="swap_cores_small")
    tsmall = _add(psmall, qsmall, name="sum_cores_small")
    g_out.update(_unpack(tsmall, SMALL + REPL))

    for n in order:
        if n not in upd:
            upd[n] = (g_out[n],) + _adamw(weights[n], g_out[n], m_in[n], v_in[n], name=f"adamw_{n}")
    return (loss, dx0[None], *[upd[n][0] for n in order], *[upd[n][1] for n in order],
            *[upd[n][2] for n in order], *[upd[n][3] for n in order])
```

```python
import functools
import math

import jax
import jax.numpy as jnp
from jax import lax
from jax.experimental import pallas as pl
from jax.experimental.pallas import tpu as pltpu

F32 = jnp.float32
BF16 = jnp.bfloat16
MESH = pl.DeviceIdType.MESH

RMS_EPS = 1e-6
L2_EPS = 1e-6
LANES = 128
VMEM_BIG = 60 * 1024 * 1024
MM_VMEM = 44 * 1024 * 1024

DN_HEADS, DN_DK, DN_DV, DN_CHUNK, DN_CONV = 8, 128, 256, 64, 4
DN_QK_W = DN_HEADS * DN_DK
DN_V_W = DN_HEADS * DN_DV
DN_CONV_W = 2 * DN_QK_W + DN_V_W
DN_IN = DN_CONV_W + DN_V_W + 2 * DN_HEADS
DN_IN_PAD = DN_CONV_W + DN_V_W + LANES
SB_DH = 64
SC_CONV = 3

ADAM_LR, ADAM_B1, ADAM_B2, ADAM_EPS, ADAM_WD, ADAM_STEP = 0.001, 0.9, 0.999, 1e-08, 0.01, 10


def _pick(n, cands):
    for c in cands:
        if n % c == 0:
            return c
    raise ValueError(f"no tile for {n} in {cands}")


def _bf(x):
    return x.astype(BF16)


def _dot(a, b):
    return jnp.dot(_bf(a), _bf(b), preferred_element_type=F32)


def _dot_nt(a, b):
    return lax.dot_general(_bf(a), _bf(b), (((1,), (1,)), ((), ())), preferred_element_type=F32)


def _dot_tn(a, b):
    return lax.dot_general(_bf(a), _bf(b), (((0,), (0,)), ((), ())), preferred_element_type=F32)


def _split3(a):
    hi = _bf(a)
    r = a - hi.astype(F32)
    mid = _bf(r)
    lo = _bf(r - mid.astype(F32))
    return hi, mid, lo


def _sigmoid(x):
    return 1.0 / (1.0 + jnp.exp(-x))


def _silu(x):
    return x * _sigmoid(x)


def _dsilu(x):
    s = _sigmoid(x)
    return s * (1.0 + x * (1.0 - s))


def _softplus(x):
    return jnp.maximum(x, 0.0) + jnp.log(1.0 + jnp.exp(-jnp.abs(x)))


def _shift_down(z, k):
    if k == 0:
        return z
    row = lax.broadcasted_iota(jnp.int32, z.shape, 0)
    return jnp.where(row >= k, pltpu.roll(z, k, 0), 0.0)


def _shift_up(z, k):
    if k == 0:
        return z
    n = z.shape[0]
    row = lax.broadcasted_iota(jnp.int32, z.shape, 0)
    return jnp.where(row < n - k, pltpu.roll(z, n - k, 0), 0.0)


def _matmul(a, b, *, mode, name, res=None, a_parts=1, b_parts=1, out_parts=1, out_dtype=F32, send=()):
    def dims2(x, parts):
        if parts == 1:
            return x.shape
        assert x.shape[0] == parts
        return (x.shape[1], x.shape[2] * parts)

    ash, bsh = dims2(a, a_parts), dims2(b, b_parts)
    if mode == "nn":
        (M, K), (K2, N) = ash, bsh
        dn = (((1,), (0,)), ((), ()))
    elif mode == "nt":
        (M, K), (N, K2) = ash, bsh
        dn = (((1,), (1,)), ((), ()))
    else:
        (K, M), (K2, N) = ash, bsh
        dn = (((0,), (0,)), ((), ()))
    assert K == K2, (ash, bsh, mode)
    tm_max = _pick(M, (512, 256, 128, 64, 32, 16, 8))
    n_unit = N // max(out_parts, b_parts if mode != "nt" else 1)
    k_unit = K // max(a_parts if mode != "tn" else 1, b_parts if mode == "nt" else 1)
    tm, tn, tk = min(
        ((m, n, k) for m in {tm_max, max(tm_max // 2, 8)}
         for n in (2048, 1792, 1024, 896, 768, 512, 384, 256, 128) if n_unit % n == 0
         for k in (k_unit, 2048, 1792, 1024, 896, 512, 256, 128) if k_unit % k == 0
         if 2 * (m * k * a.dtype.itemsize + k * n * b.dtype.itemsize + 2 * m * n * 4) + m * n * 4 <= MM_VMEM),
        key=lambda t: (-t[0] * t[1] * t[2], -t[0], -t[2]))
    nk = K // tk
    grid = (M // tm, N // tn, nk)

    def spec(parts, rows_are, cols_are, tr, tc, width):
        per = width // parts // tc
        if parts == 1:
            return pl.BlockSpec((tr, tc), lambda i, j, k: ((i, j, k)[rows_are], (i, j, k)[cols_are]))
        return pl.BlockSpec((None, tr, tc), lambda i, j, k: ((i, j, k)[cols_are] // per, (i, j, k)[rows_are],
                                                             (i, j, k)[cols_are] % per))

    if mode == "nn":
        a_spec = spec(a_parts, 0, 2, tm, tk, K)
        b_spec = spec(b_parts, 2, 1, tk, tn, N)
    elif mode == "nt":
        a_spec = spec(a_parts, 0, 2, tm, tk, K)
        b_spec = spec(b_parts, 1, 2, tn, tk, K)
    else:
        a_spec = spec(a_parts, 2, 0, tk, tm, M)
        b_spec = spec(b_parts, 2, 1, tk, tn, N)
    o_spec = spec(out_parts, 0, 1, tm, tn, N)
    in_specs = [a_spec, b_spec]
    operands = [a, b]
    if res is not None:
        in_specs.append(pl.BlockSpec((tm, tn), lambda i, j, k: (i, j)))
        operands.append(res)

    n_in = len(operands)
    ns = len(send)

    def finish(refs, r):
        if res is not None:
            r = refs[2][...] + r
        refs[n_in + ns][...] = r.astype(out_dtype)

    def body(*refs):
        if ns:
            at = lambda step: functools.reduce(jnp.logical_and, [pl.program_id(d) == step[d] for d in range(3)])
            _blocks_over_ici(refs[n_in:n_in + ns], refs[n_in + ns + 1:n_in + 2 * ns + 1], refs[-2], refs[-1],
                             at((0, 0, 0)), at(tuple(g - 1 for g in grid)))
        part = lax.dot_general(_bf(refs[0][...]), _bf(refs[1][...]), dn, preferred_element_type=F32)
        if nk == 1:
            finish(refs, part)
            return
        acc_ref = refs[n_in + 2 * ns + 1]
        k = pl.program_id(2)

        @pl.when(k == 0)
        def _():
            acc_ref[...] = part

        @pl.when(jnp.logical_and(k > 0, k < nk - 1))
        def _():
            acc_ref[...] += part

        @pl.when(k == nk - 1)
        def _():
            finish(refs, acc_ref[...] + part)

    out_shape = (M, N) if out_parts == 1 else (out_parts, M, N // out_parts)
    out = pl.pallas_call(
        body, name=name, grid=grid, in_specs=in_specs + [HBM] * ns, out_specs=[o_spec] + [HBM] * ns,
        out_shape=[jax.ShapeDtypeStruct(out_shape, out_dtype)] + [jax.ShapeDtypeStruct(x.shape, x.dtype) for x in send],
        scratch_shapes=([pltpu.VMEM((tm, tn), F32)] if nk > 1 else [])
        + ([pltpu.SemaphoreType.DMA((3 * ns,)), pltpu.SemaphoreType.DMA((3 * ns,))] if ns else []),
        compiler_params=pltpu.CompilerParams(
            dimension_semantics=("arbitrary",) * 3 if ns else ("parallel", "parallel", "arbitrary"),
            vmem_limit_bytes=VMEM_BIG),
    )(*operands, *send)
    return out if ns else out[0]


def _rmsnorm_fwd(x, g, *, name):
    T, D = x.shape
    tm = _pick(T, (512, 256, 128, 64, 32, 16))

    def body(x_ref, g_ref, h_ref):
        xv = x_ref[...]
        r = lax.rsqrt(jnp.mean(xv * xv, axis=-1, keepdims=True) + RMS_EPS)
        h_ref[...] = ((xv * r) * g_ref[...]).astype(BF16)

    return pl.pallas_call(
        body, name=name, grid=(T // tm,),
        in_specs=[pl.BlockSpec((tm, D), lambda i: (i, 0)), pl.BlockSpec((1, D), lambda i: (0, 0))],
        out_specs=pl.BlockSpec((tm, D), lambda i: (i, 0)),
        out_shape=jax.ShapeDtypeStruct((T, D), BF16),
    )(x, g.reshape(1, D))


def _rmsnorm_bwd(x, g, dh, dx_in, *, name):
    T, D = x.shape
    tm = _pick(T, (512, 256, 128, 64, 32, 16))

    def body(x_ref, g_ref, dh_ref, dxin_ref, dx_ref, dg_ref):
        @pl.when(pl.program_id(0) == 0)
        def _():
            dg_ref[...] = jnp.zeros_like(dg_ref)

        xv = x_ref[...]
        r = lax.rsqrt(jnp.mean(xv * xv, axis=-1, keepdims=True) + RMS_EPS)
        xh = xv * r
        dh_v = dh_ref[...]
        dxh = dh_v * g_ref[...]
        dx_ref[...] = dxin_ref[...] + r * (dxh - xh * jnp.mean(dxh * xh, axis=-1, keepdims=True))
        dg_ref[...] += jnp.sum(dh_v * xh, axis=0, keepdims=True)

    row = pl.BlockSpec((tm, D), lambda i: (i, 0))
    vec = pl.BlockSpec((1, D), lambda i: (0, 0))
    return pl.pallas_call(
        body, name=name, grid=(T // tm,),
        in_specs=[row, vec, row, row], out_specs=[row, vec],
        out_shape=[jax.ShapeDtypeStruct((T, D), F32), jax.ShapeDtypeStruct((1, D), F32)],
        compiler_params=pltpu.CompilerParams(dimension_semantics=("arbitrary",)),
    )(x, g.reshape(1, D), dh, dx_in)


def _loss_head(y, target, *, name):
    T, D = y.shape
    tm = _pick(T, (512, 256, 128, 64, 32, 16))

    def body(y_ref, t_ref, dy_ref, l_ref):
        @pl.when(pl.program_id(0) == 0)
        def _():
            l_ref[...] = jnp.zeros_like(l_ref)

        err = y_ref[...] - t_ref[...]
        dy_ref[...] = err * (1.0 / D)
        l_ref[...] += 0.5 * jnp.sum(jnp.mean(err * err, axis=-1, keepdims=True), axis=0, keepdims=True)

    row = pl.BlockSpec((tm, D), lambda i: (i, 0))
    return pl.pallas_call(
        body, name=name, grid=(T // tm,),
        in_specs=[row, row], out_specs=[row, pl.BlockSpec((1, 1), lambda i: (0, 0))],
        out_shape=[jax.ShapeDtypeStruct((T, D), F32), jax.ShapeDtypeStruct((1, 1), F32)],
        compiler_params=pltpu.CompilerParams(dimension_semantics=("arbitrary",)),
    )(y, target)


def _sc_mid_fwd(p3, conv_w, *, name):
    _, T, W = p3.shape
    K = conv_w.shape[0]
    cw = LANES

    def body(p_ref, w_ref, o_ref):
        z = p_ref[1] * p_ref[2]
        cv = sum(w_ref[i:i + 1, :] * _shift_down(z, K - 1 - i) for i in range(K))
        o_ref[...] = ((p_ref[0] * cv) * _silu(p_ref[3])).astype(BF16)

    return pl.pallas_call(
        body, name=name, grid=(W // cw,),
        in_specs=[pl.BlockSpec((4, T, cw), lambda j: (0, 0, j)), pl.BlockSpec((K, cw), lambda j: (0, j))],
        out_specs=pl.BlockSpec((T, cw), lambda j: (0, j)),
        out_shape=jax.ShapeDtypeStruct((T, W), BF16),
        compiler_params=pltpu.CompilerParams(dimension_semantics=("parallel",), vmem_limit_bytes=VMEM_BIG),
    )(p3, conv_w)


def _sc_mid_bwd(p3, conv_w, do, *, name):
    _, T, W = p3.shape
    K = conv_w.shape[0]
    cw = LANES

    def body(p_ref, w_ref, do_ref, dp_ref, dw_ref):
        b, c, u, gate = p_ref[0], p_ref[1], p_ref[2], p_ref[3]
        z = c * u
        zs = [_shift_down(z, K - 1 - i) for i in range(K)]
        cv = sum(w_ref[i:i + 1, :] * zs[i] for i in range(K))
        y = b * cv
        dov = do_ref[...]
        dy = dov * _silu(gate)
        dp_ref[3] = dov * y * _dsilu(gate)
        dp_ref[0] = dy * cv
        dcv = dy * b
        dz = sum(w_ref[i:i + 1, :] * _shift_up(dcv, K - 1 - i) for i in range(K))
        dp_ref[1] = dz * u
        dp_ref[2] = dz * c
        for i in range(K):
            dw_ref[i:i + 1, :] = jnp.sum(dcv * zs[i], axis=0, keepdims=True)

    return pl.pallas_call(
        body, name=name, grid=(W // cw,),
        in_specs=[pl.BlockSpec((4, T, cw), lambda j: (0, 0, j)), pl.BlockSpec((K, cw), lambda j: (0, j)),
                  pl.BlockSpec((T, cw), lambda j: (0, j))],
        out_specs=[pl.BlockSpec((4, T, cw), lambda j: (0, 0, j)), pl.BlockSpec((K, cw), lambda j: (0, j))],
        out_shape=[jax.ShapeDtypeStruct((4, T, W), F32), jax.ShapeDtypeStruct((K, W), F32)],
        compiler_params=pltpu.CompilerParams(dimension_semantics=("parallel",), vmem_limit_bytes=VMEM_BIG),
    )(p3, conv_w, do)


def _sc_layer_fwd(x, ng, w_in, conv_w, w_out, tag):
    h = _rmsnorm_fwd(x, ng, name=f"{tag}_norm")
    p3 = _matmul(h, w_in, mode="nn", b_parts=4, out_parts=4, name=f"{tag}_inproj")
    og = _sc_mid_fwd(p3, conv_w, name=f"{tag}_mid")
    x_new = _matmul(og, w_out, mode="nn", res=x, name=f"{tag}_outproj")
    return x_new, (h, p3, og)


def _sc_layer_bwd(dx, x, ng, w_in, conv_w, w_out, saved, tag):
    h, p3, og = saved
    d_wout = _matmul(og, dx, mode="tn", out_dtype=BF16, name=f"{tag}_dwout")
    dog = _matmul(dx, w_out, mode="nt", name=f"{tag}_dog")
    dp3, dconv = _sc_mid_bwd(p3, conv_w, dog, name=f"{tag}_midbwd")
    d_win = _matmul(h, dp3, mode="tn", b_parts=4, out_parts=4, out_dtype=BF16, name=f"{tag}_dwin")
    dh = _matmul(dp3, w_in, mode="nt", a_parts=4, b_parts=4, name=f"{tag}_dh")
    dx_prev, dng = _rmsnorm_bwd(x, ng, dh, dx, name=f"{tag}_normbwd")
    return dx_prev, dng, d_win, dconv, d_wout


SB_BQ = 256
SB_BK = 256
SB_ROWS = 512
SB_DEAD = -110.0


def _sb_half_mask():
    return lax.broadcasted_iota(jnp.int32, (1, LANES), 1) < SB_DH


def _sb_headnorm(x, g, lo):
    x2 = x * x
    s_lo = jnp.sum(jnp.where(lo, x2, 0.0), axis=-1, keepdims=True)
    s_hi = jnp.sum(jnp.where(lo, 0.0, x2), axis=-1, keepdims=True)
    r = lax.rsqrt(jnp.where(lo, s_lo, s_hi) * (1.0 / SB_DH) + RMS_EPS)
    xh = x * r
    return xh * g, xh, r


def _dot_x2_l(a_l, b_exact_bf16):
    his = [_bf(a) for a in a_l]
    mids = [_bf(a - h.astype(F32)) for a, h in zip(a_l, his)]
    f = lambda p: jnp.dot(p, b_exact_bf16, preferred_element_type=F32)
    return [x + y for x, y in zip([f(h) for h in his], [f(m) for m in mids])]


def _sb_stack(xb, lo):
    zero = jnp.zeros_like(xb)
    return jnp.concatenate([jnp.where(lo, xb, zero), jnp.where(lo, zero, xb)], axis=0)


def _sb_rel(bq, bk):
    row = lax.broadcasted_iota(jnp.int32, (2 * bq, bk), 0)
    col = lax.broadcasted_iota(jnp.int32, (2 * bq, bk), 1)
    return col - jnp.where(row >= bq, row - bq, row)


def _sb_tile(qm, kb, valid):
    z = lax.dot_general(qm, kb, (((1,), (1,)), ((), ())), preferred_element_type=F32)
    sp = _softplus(z)
    return z - sp, (-sp if valid is None else jnp.where(valid, -sp, 0.0))


def _sb_attn_fwd(p3, gq2, gk2, *, name, send=()):
    _, T, W = p3.shape
    bq, bk = min(SB_BQ, T), min(SB_BK, T)
    rows = min(SB_ROWS, T)
    scale = SB_DH ** -0.5
    ns = len(send)
    npair = W // LANES

    def body(*refs):
        p_ref, gq_ref, gk_ref = refs[:3]
        og_ref, o_ref, ls_ref, cnt_ref = refs[3 + ns:7 + ns]
        qn_ref, kn_ref, v_ref = refs[7 + 2 * ns:10 + 2 * ns]
        if ns:
            _halves_over_ici(refs[3:3 + ns], refs[7 + ns:7 + 2 * ns], refs[10 + 2 * ns], refs[11 + 2 * ns],
                             pl.program_id(0) == 0, pl.program_id(0) == npair - 1)
        lo = _sb_half_mask()

        def prologue(i, c):
            r0 = pl.multiple_of(i * rows, rows)
            sl = pl.ds(r0, rows)
            qn_ref[sl, :] = (_sb_headnorm(p_ref[0, sl, :], gq_ref[...], lo)[0] * scale).astype(BF16)
            kn_ref[sl, :] = _sb_headnorm(p_ref[1, sl, :], gk_ref[...], lo)[0].astype(BF16)
            v_ref[sl, :] = p_ref[2, sl, :].astype(BF16)
            return c

        lax.fori_loop(0, T // rows, prologue, 0)

        rel = _sb_rel(bq, bk)
        tri = (lax.broadcasted_iota(jnp.int32, (bk, bk), 0)
               > lax.broadcasted_iota(jnp.int32, (bk, bk), 1)).astype(BF16)

        def qblock(qi, c):
            q0 = pl.multiple_of(qi * bq, bq)
            qm = _sb_stack(qn_ref[pl.ds(q0, bq), :], lo)
            nkb = (q0 + bq - 1) // bk + 1

            def tiles(k0s, carry, valids):
                o_acc, a_carry = carry
                sc = [_sb_tile(qm, kn_ref[pl.ds(k0, bk), :], valid) for k0, valid in zip(k0s, valids)]
                later = _dot_x2_l([log1m for _, log1m in sc], tri)
                for (logsig, log1m), lat, k0, valid in zip(sc, later, k0s, valids):
                    wts = jnp.exp(logsig + (lat + a_carry))
                    if valid is not None:
                        wts = jnp.where(valid, wts, 0.0)
                    o_acc = o_acc + jnp.dot(_bf(wts), v_ref[pl.ds(k0, bk), :], preferred_element_type=F32)
                    a_carry = a_carry + jnp.sum(log1m, axis=-1, keepdims=True)
                return o_acc, a_carry

            blk0 = lambda j: pl.multiple_of(j * bk, bk)
            k_last = blk0(nkb - 1)
            o2, t2 = tiles([k_last, blk0(jnp.maximum(nkb - 2, 0))],
                           (jnp.zeros((2 * bq, LANES), F32), jnp.zeros((2 * bq, 1), F32)),
                           [rel < q0 - k_last, nkb >= 2])

            def alive(st):
                return jnp.logical_and(st[0] < nkb - 1, jnp.max(st[2]) > SB_DEAD)

            def back_one(st):
                return (st[0] + 1,) + tiles([blk0(nkb - 2 - st[0])], st[1:], [None])

            n_back, o2, t2 = lax.while_loop(alive, back_one, (jnp.int32(1), o2, t2))
            o = jnp.where(lo, o2[:bq], o2[bq:])
            o_ref[pl.ds(q0, bq), :] = o
            ls_ref[pl.ds(q0, bq), :] = jnp.where(lo, t2[:bq], t2[bq:])
            cnt_ref[qi] = jnp.full((8, LANES), jnp.minimum(n_back + 1, nkb).astype(F32))
            og_ref[pl.ds(q0, bq), :] = (o * _silu(p_ref[3, pl.ds(q0, bq), :])).astype(BF16)
            return c

        lax.fori_loop(0, T // bq, qblock, 0)

    colblk = pl.BlockSpec((T, LANES), lambda j: (0, j))
    vec = pl.BlockSpec((1, LANES), lambda j: (0, 0))
    return pl.pallas_call(
        body, name=name, grid=(npair,),
        in_specs=[pl.BlockSpec((4, T, LANES), lambda j: (0, 0, j)), vec, vec] + [HBM] * ns,
        out_specs=[colblk, colblk, colblk, pl.BlockSpec((None, T // bq, 8, LANES), lambda j: (j, 0, 0, 0))]
        + [HBM] * ns,
        out_shape=[jax.ShapeDtypeStruct((T, W), BF16), jax.ShapeDtypeStruct((T, W), F32),
                   jax.ShapeDtypeStruct((T, W), F32), jax.ShapeDtypeStruct((npair, T // bq, 8, LANES), F32)]
        + [jax.ShapeDtypeStruct((N_CHIPS,) + a.shape, a.dtype) for a in send],
        scratch_shapes=[pltpu.VMEM((T, LANES), BF16)] * 3
        + ([pltpu.SemaphoreType.DMA((3 * ns,)), pltpu.SemaphoreType.DMA((3 * ns,))] if ns else []),
        compiler_params=pltpu.CompilerParams(dimension_semantics=("arbitrary",), vmem_limit_bytes=VMEM_BIG),
    )(p3, gq2, gk2, *send)


def _sb_attn_bwd(p3, gq2, gk2, o, lsum, live, dog, *, name):
    _, T, W = p3.shape
    bq, bk = min(SB_BQ, T), min(SB_BK, T)
    rows = min(SB_ROWS, T)
    scale = SB_DH ** -0.5

    def body(p_ref, gq_ref, gk_ref, o_ref, ls_ref, cnt_ref, dog_ref, dp_ref, dgq_ref, dgk_ref,
             qn_ref, kn_ref, v_ref, do_ref):
        lo = _sb_half_mask()

        def prologue(i, c):
            r0 = pl.multiple_of(i * rows, rows)
            sl = pl.ds(r0, rows)
            qn_ref[sl, :] = (_sb_headnorm(p_ref[0, sl, :], gq_ref[...], lo)[0] * scale).astype(BF16)
            kn_ref[sl, :] = _sb_headnorm(p_ref[1, sl, :], gk_ref[...], lo)[0].astype(BF16)
            v_ref[sl, :] = p_ref[2, sl, :].astype(BF16)
            gate = p_ref[3, sl, :]
            dogv = dog_ref[sl, :]
            dp_ref[3, sl, :] = dogv * o_ref[sl, :] * _dsilu(gate)
            do_ref[sl, :] = (dogv * _silu(gate)).astype(BF16)
            zero = jnp.zeros((rows, LANES), F32)
            dp_ref[0, sl, :] = zero
            dp_ref[1, sl, :] = zero
            dp_ref[2, sl, :] = zero
            return c

        lax.fori_loop(0, T // rows, prologue, 0)

        rel = _sb_rel(bq, bk)
        rj = lax.broadcasted_iota(jnp.int32, (bk, bk), 0)
        cj = lax.broadcasted_iota(jnp.int32, (bk, bk), 1)
        upto = (rj <= cj).astype(BF16)
        before_m = (rj < cj).astype(BF16)

        def qblock(qi, c):
            q0 = pl.multiple_of(qi * bq, bq)
            qm = _sb_stack(qn_ref[pl.ds(q0, bq), :], lo)
            dom = _sb_stack(do_ref[pl.ds(q0, bq), :], lo)
            nkb = (q0 + bq - 1) // bk + 1
            blk0 = lambda j: pl.multiple_of(j * bk, bk)
            k_last = blk0(nkb - 1)

            lsb = ls_ref[pl.ds(q0, bq), :]
            total = jnp.concatenate([lsb[:, 0:1], lsb[:, SB_DH:SB_DH + 1]], axis=0)
            n_live = jnp.clip(jnp.max(cnt_ref[qi]).astype(jnp.int32), 1, nkb)
            k_first = nkb - n_live

            def tiles(k0s, carry, valids):
                dq_acc, a_pre, r_pre = carry
                kss = [pl.ds(k0, bk) for k0 in k0s]
                kbs = [kn_ref[ks, :] for ks in kss]
                sc = [_sb_tile(qm, kb, valid) for kb, valid in zip(kbs, valids)]
                dws = [lax.dot_general(dom, v_ref[ks, :], _NT, preferred_element_type=F32) for ks in kss]
                upto_l = _dot_x2_l([log1m for _, log1m in sc], upto)
                wts_l = []
                for (logsig, log1m), up, valid in zip(sc, upto_l, valids):
                    wts = jnp.exp(logsig + ((total - a_pre) - up))
                    wts_l.append(wts if valid is None else jnp.where(valid, wts, 0.0))
                    a_pre = a_pre + jnp.sum(log1m, axis=-1, keepdims=True)
                ee_l = [dw * wts for dw, wts in zip(dws, wts_l)]
                before_l = _dot_x2_l(ee_l, before_m)
                for (logsig, _), ks, kb, wts, ee, bef, valid in zip(sc, kss, kbs, wts_l, ee_l, before_l, valids):
                    beta = jnp.exp(logsig)
                    dz = ee * (1.0 - beta) - beta * (r_pre + bef)
                    if valid is not None:
                        dz = jnp.where(valid, dz, 0.0)
                    dzb = _bf(dz)
                    dq_acc = dq_acc + jnp.dot(dzb, kb, preferred_element_type=F32)
                    dp_ref[1, ks, :] += lax.dot_general(dzb, qm, _TN, preferred_element_type=F32)
                    dp_ref[2, ks, :] += lax.dot_general(_bf(wts), dom, _TN, preferred_element_type=F32)
                    r_pre = r_pre + jnp.sum(ee, axis=-1, keepdims=True)
                return dq_acc, a_pre, r_pre

            cr = (jnp.zeros((2 * bq, LANES), F32), jnp.zeros((2 * bq, 1), F32), jnp.zeros((2 * bq, 1), F32))
            n_before = jnp.maximum(n_live - 2, 0)
            cr = lax.fori_loop(0, n_before % 2, lambda t, cr: tiles([blk0(k_first)], cr, [None]), cr)
            k_pairs = k_first + n_before % 2
            cr = lax.fori_loop(0, n_before // 2,
                               lambda t, cr: tiles([blk0(k_pairs + 2 * t), blk0(k_pairs + 2 * t + 1)], cr,
                                                   [None, None]), cr)
            dq2, _, _ = tiles([blk0(jnp.maximum(nkb - 2, 0)), k_last], cr, [n_live >= 2, rel < q0 - k_last])
            dp_ref[0, pl.ds(q0, bq), :] = jnp.where(lo, dq2[:bq], dq2[bq:]) * scale
            return c

        lax.fori_loop(0, T // bq, qblock, 0)

        dgq_ref[...] = jnp.zeros_like(dgq_ref)
        dgk_ref[...] = jnp.zeros_like(dgk_ref)

        def epilogue(i, c):
            r0 = pl.multiple_of(i * rows, rows)
            sl = pl.ds(r0, rows)
            for part, g_ref, dg_ref in ((0, gq_ref, dgq_ref), (1, gk_ref, dgk_ref)):
                _, xh, r = _sb_headnorm(p_ref[part, sl, :], g_ref[...], lo)
                dn = dp_ref[part, sl, :]
                dxh = dn * g_ref[...]
                prod = dxh * xh
                m_lo = jnp.sum(jnp.where(lo, prod, 0.0), axis=-1, keepdims=True)
                m_hi = jnp.sum(jnp.where(lo, 0.0, prod), axis=-1, keepdims=True)
                m = jnp.where(lo, m_lo, m_hi) * (1.0 / SB_DH)
                dp_ref[part, sl, :] = r * (dxh - xh * m)
                dg_ref[...] += jnp.sum(dn * xh, axis=0, keepdims=True)
            return c

        lax.fori_loop(0, T // rows, epilogue, 0)

    colblk = pl.BlockSpec((T, LANES), lambda j: (0, j))
    vec = pl.BlockSpec((1, LANES), lambda j: (0, 0))
    part = pl.BlockSpec((4, T, LANES), lambda j: (0, 0, j))
    gvec = pl.BlockSpec((None, 1, LANES), lambda j: (j, 0, 0))
    npair = W // LANES
    return pl.pallas_call(
        body, name=name, grid=(npair,),
        in_specs=[part, vec, vec, colblk, colblk, pl.BlockSpec((None, T // bq, 8, LANES), lambda j: (j, 0, 0, 0)),
                  colblk],
        out_specs=[part, gvec, gvec],
        out_shape=[jax.ShapeDtypeStruct((4, T, W), F32), jax.ShapeDtypeStruct((npair, 1, LANES), F32),
                   jax.ShapeDtypeStruct((npair, 1, LANES), F32)],
        scratch_shapes=[pltpu.VMEM((T, LANES), BF16)] * 4,
        compiler_params=pltpu.CompilerParams(dimension_semantics=("parallel",), vmem_limit_bytes=VMEM_BIG),
    )(p3, gq2, gk2, o, lsum, live, dog)


_NN = (((1,), (0,)), ((), ()))
_NT = (((1,), (1,)), ((), ()))
_TN = (((0,), (0,)), ((), ()))
DN_TB = 512
DN_HEADS_FWD = 4
DN_HEADS_BWD = 2
DN_INV_EXACT_LEVELS = 2
DN_AB_COL = (DN_CONV_W + DN_V_W) // LANES


def _dn_conv(x, w_ref):
    k = w_ref.shape[0]
    return sum(w_ref[i:i + 1, :] * _shift_down(x, k - 1 - i) for i in range(k))


def _dn_prep_fwd(p, conv_w, *, name):
    T = p.shape[0]
    cw = conv_w.shape[1]
    n_qk = 2 * DN_QK_W // LANES

    def body(p_ref, w_ref, o_ref):
        s = _silu(_dn_conv(p_ref[...], w_ref))
        r = lax.rsqrt(jnp.sum(s * s, axis=-1, keepdims=True) + L2_EPS)
        o_ref[...] = jnp.where(pl.program_id(0) < n_qk, s * r, s)

    colblk = pl.BlockSpec((T, LANES), lambda j: (0, j))
    return pl.pallas_call(
        body, name=name, grid=(cw // LANES,),
        in_specs=[colblk, pl.BlockSpec((DN_CONV, LANES), lambda j: (0, j))],
        out_specs=colblk, out_shape=jax.ShapeDtypeStruct((T, cw), F32),
        compiler_params=pltpu.CompilerParams(dimension_semantics=("parallel",), vmem_limit_bytes=VMEM_BIG),
    )(p, conv_w)


def _dn_chunk_tri(rows, upper):
    r = lax.broadcasted_iota(jnp.int32, (rows, rows), 0)
    c = lax.broadcasted_iota(jnp.int32, (rows, rows), 1)
    same = (r // DN_CHUNK) == (c // DN_CHUNK)
    return jnp.logical_and(same, (c >= r) if upper else (c <= r)).astype(BF16)


def _dn_lane_rows(a_log, dt_bias):
    pad = lambda v: jnp.zeros((1, LANES), F32).at[0, :DN_HEADS].set(v)
    return pad(a_log), pad(dt_bias)


def _dn_ab_parts(blk, alog_row, dtb_row):
    lane = lax.broadcasted_iota(jnp.int32, (1, LANES), 1)
    is_a = lane < DN_HEADS
    is_b = jnp.logical_and(lane >= DN_HEADS, lane < 2 * DN_HEADS)
    a_arg = jnp.where(is_a, blk + dtb_row, 0.0)
    neg_exp = jnp.where(is_a, -jnp.exp(alog_row), 0.0)
    log_a = neg_exp * _softplus(a_arg)
    beta = jnp.where(is_b, _sigmoid(blk), 0.0)
    return is_a, is_b, a_arg, neg_exp, log_a, beta


def _dn_ab_fwd(p, alog_row, dtb_row, *, name):
    T = p.shape[0]
    rows = min(DN_TB, T)

    def body(p_ref, al_ref, dt_ref, o_ref):
        _, _, _, _, log_a, beta = _dn_ab_parts(p_ref[...], al_ref[...], dt_ref[...])
        hi, mid, lo_ = _split3(log_a)
        tri = _dn_chunk_tri(rows, upper=False)
        f = lambda q: jnp.dot(tri, q, preferred_element_type=F32)
        o_ref[...] = (f(hi) + f(mid) + f(lo_)) + beta

    blk = pl.BlockSpec((rows, LANES), lambda i: (i, DN_AB_COL))
    vec = pl.BlockSpec((1, LANES), lambda i: (0, 0))
    return pl.pallas_call(
        body, name=name, grid=(T // rows,), in_specs=[blk, vec, vec],
        out_specs=pl.BlockSpec((rows, LANES), lambda i: (i, 0)),
        out_shape=jax.ShapeDtypeStruct((T, LANES), F32),
        compiler_params=pltpu.CompilerParams(dimension_semantics=("parallel",)),
    )(p, alog_row, dtb_row)


def _hp_l(a_l, b_l, dims=_NN):
    sa = [_split3(a)[:2] for a in a_l]
    sb = [_split3(b)[:2] for b in b_l]
    f = lambda p, q: lax.dot_general(p, q, dims, preferred_element_type=F32)
    hh = [f(x[0], y[0]) for x, y in zip(sa, sb)]
    hm = [f(x[0], y[1]) for x, y in zip(sa, sb)]
    mh = [f(x[1], y[0]) for x, y in zip(sa, sb)]
    return [a + (b + c) for a, b, c in zip(hh, hm, mh)]


def _dn_local(qs, k, v, g, beta, nc, inv_l=None):
    c = DN_CHUNK
    cut = lambda x: [x[i * c:(i + 1) * c] for i in range(nc)]
    row = lax.broadcasted_iota(jnp.int32, (c, c), 0)
    col = lax.broadcasted_iota(jnp.int32, (c, c), 1)
    eye, lower, strict = row == col, row >= col, row > col
    rowid = lax.broadcasted_iota(jnp.int32, (c, 1), 0)
    eg = jnp.exp(g)
    kb = k * beta
    rhs_k = kb * eg
    g_l, k_l, kb_l, qs_l = cut(g), cut(k), cut(kb), cut(qs)
    g_row_l = [jnp.sum(jnp.where(eye, x, 0.0), axis=0, keepdims=True) for x in g_l]
    dec_l = [jnp.where(lower, jnp.exp(jnp.where(lower, x - y, 0.0)), 0.0) for x, y in zip(g_l, g_row_l)]
    kk_l = [_dot_nt(a, b) for a, b in zip(kb_l, k_l)]
    qk_l = [_dot_nt(a, b) for a, b in zip(qs_l, k_l)]
    low_l = [jnp.where(strict, a * d, 0.0) for a, d in zip(kk_l, dec_l)]
    if inv_l is None:
        pw_l = [-x for x in low_l]
        inv_l = [eye.astype(F32) + x for x in pw_l]
        plain = lambda a_l, b_l: [_dot(a, b) for a, b in zip(a_l, b_l)]
        for level in range(int(math.log2(c)) - 1):
            mul = _hp_l if level < DN_INV_EXACT_LEVELS else plain
            pw_l = mul(pw_l, pw_l)
            inv_l = [a + b for a, b in zip(inv_l, mul(inv_l, pw_l))]
    u_l = [_dot(a, b) for a, b in zip(inv_l, cut(v * beta))]
    w_l = [_dot(a, b) for a, b in zip(inv_l, cut(rhs_k))]
    aqk_l = [jnp.where(lower, a * d, 0.0) for a, d in zip(qk_l, dec_l)]
    g_last_l = [jnp.sum(jnp.where(rowid == c - 1, x, 0.0), axis=0, keepdims=True) for x in g_l]
    ekd_l = [jnp.exp(a - b) for a, b in zip(g_last_l, g_l)]
    kd_l = [a * b for a, b in zip(k_l, ekd_l)]
    qd_l = cut(qs * eg)
    kw_l = [_dot_tn(a, b) for a, b in zip(kd_l, w_l)]
    qp_l = [q - _dot(a, w) for q, a, w in zip(qd_l, aqk_l, w_l)]
    return dict(eye=eye, lower=lower, strict=strict, dec=dec_l, k=k_l, kb=kb_l, qs=qs_l, low=low_l, inv=inv_l,
                eg=cut(eg), rhs_k=cut(rhs_k), u=u_l, w=w_l, aqk=aqk_l, g_last=g_last_l, qd=qd_l,
                ekd=ekd_l, kd=kd_l, kw=kw_l, qp=qp_l)


def _dn_head_cols(gb_blk, head):
    lane = lax.broadcasted_iota(jnp.int32, (1, LANES), 1)
    g = jnp.sum(jnp.where(lane == head, gb_blk, 0.0), axis=-1, keepdims=True)
    beta = jnp.sum(jnp.where(lane == head + DN_HEADS, gb_blk, 0.0), axis=-1, keepdims=True)
    return g, beta


def _halves_over_ici(s_refs, o_refs, send_sems, recv_sems, first, last):
    x, y, c = _mesh_pos()
    me = 2 * x + y
    chips = _other_chips(x, y)
    pairs = [(a, k) for a in range(len(s_refs)) for k in range(3)]

    def copy(a, k, slot):
        px, py = chips[k]
        return pltpu.make_async_remote_copy(
            src_ref=s_refs[a].at[c], dst_ref=o_refs[a].at[slot, c], send_sem=send_sems.at[3 * a + k],
            recv_sem=recv_sems.at[3 * a + k], device_id=(px, py, c), device_id_type=MESH)

    @pl.when(first)
    def _():
        for a, k in pairs:
            copy(a, k, me).start()

    @pl.when(last)
    def _():
        for a, k in pairs:
            px, py = chips[k]
            copy(a, k, 2 * px + py).wait_recv()
        for a, k in pairs:
            copy(a, k, me).wait_send()


def _dn_delta_fwd(qkv, gb, p, o_gain, *, name, send=()):
    T = qkv.shape[0]
    tb = min(DN_TB, T)
    nb, nc = T // tb, tb // DN_CHUNK
    H = DN_HEADS
    qscale = DN_DK ** -0.5
    ns = len(send)
    hp = DN_HEADS_FWD

    def body(*refs):
        q_ref, k_ref, v_ref, gb_ref, gate_ref, gain_ref = refs[:6]
        o_ref, og_ref, st_ref, inv_ref = refs[6 + ns:10 + ns]
        s_ref = refs[10 + 2 * ns]
        pair, blk = pl.program_id(0), pl.program_id(1)
        if ns:
            _halves_over_ici(refs[6:6 + ns], refs[10 + ns:10 + 2 * ns], refs[11 + 2 * ns], refs[12 + 2 * ns],
                             jnp.logical_and(pair == 0, blk == 0),
                             jnp.logical_and(pair == H // hp - 1, blk == nb - 1))

        @pl.when(blk == 0)
        def _():
            s_ref[...] = jnp.zeros_like(s_ref)

        gbv = gb_ref[...]
        ts, ku, op = [], [], []
        for e in range(hp):
            qk_e, v_e = slice(e * DN_DK, (e + 1) * DN_DK), slice(e * DN_DV, (e + 1) * DN_DV)
            g, beta = _dn_head_cols(gbv, hp * pair + e)
            t = _dn_local(q_ref[:, qk_e] * qscale, k_ref[:, qk_e], v_ref[:, v_e], g, beta, nc)
            ts.append(t)
            for i in range(nc):
                inv_ref[e, i] = t["inv"][i]
            ku.append([_dot_tn(a, b) for a, b in zip(t["kd"], t["u"])])
            op.append([_dot(a, b) for a, b in zip(t["aqk"], t["u"])])
        s32 = [s_ref[e] for e in range(hp)]
        s_l = [[] for _ in range(hp)]
        for i in range(nc):
            sb = [_bf(x) for x in s32]
            for e in range(hp):
                st_ref[e, i] = sb[e]
                s_l[e].append(sb[e])
            prod = [_dot(ts[e]["kw"][i], sb[e]) for e in range(hp)]
            s32 = [s32[e] * jnp.exp(ts[e]["g_last"][i]) - prod[e] + ku[e][i] for e in range(hp)]
        for e in range(hp):
            s_ref[e] = s32[e]
        o = jnp.concatenate(
            [jnp.concatenate([_dot(qp, sb) + x for qp, sb, x in zip(ts[e]["qp"], s_l[e], op[e])], axis=0)
             for e in range(hp)], axis=1)
        o_ref[...] = o
        gain = gain_ref[...]
        for e in range(hp):
            v_e = slice(e * DN_DV, (e + 1) * DN_DV)
            oe = o[:, v_e]
            r = lax.rsqrt(jnp.mean(oe * oe, axis=-1, keepdims=True) + RMS_EPS)
            og_ref[:, v_e] = (((oe * r) * gain) * _silu(gate_ref[:, v_e])).astype(BF16)

    qk = lambda col0: pl.BlockSpec((tb, hp * DN_DK), lambda h, i: (i, col0 // (hp * DN_DK) + h))
    vblk = lambda col0: pl.BlockSpec((tb, hp * DN_DV), lambda h, i: (i, col0 // (hp * DN_DV) + h))
    return pl.pallas_call(
        body, name=name, grid=(H // hp, nb),
        in_specs=[qk(0), qk(DN_QK_W), vblk(2 * DN_QK_W), pl.BlockSpec((tb, LANES), lambda h, i: (i, 0)),
                  vblk(DN_CONV_W), pl.BlockSpec((1, DN_DV), lambda h, i: (0, 0))] + [HBM] * ns,
        out_specs=[vblk(0), vblk(0), pl.BlockSpec((hp, nc, DN_DK, DN_DV), lambda h, i: (h, i, 0, 0)),
                   pl.BlockSpec((hp, nc, DN_CHUNK, DN_CHUNK), lambda h, i: (h, i, 0, 0))] + [HBM] * ns,
        out_shape=[jax.ShapeDtypeStruct((T, DN_V_W), F32), jax.ShapeDtypeStruct((T, DN_V_W), BF16),
                   jax.ShapeDtypeStruct((H, T // DN_CHUNK, DN_DK, DN_DV), BF16),
                   jax.ShapeDtypeStruct((H, T // DN_CHUNK, DN_CHUNK, DN_CHUNK), F32)]
        + [jax.ShapeDtypeStruct((N_CHIPS,) + a.shape, a.dtype) for a in send],
        scratch_shapes=[pltpu.VMEM((hp, DN_DK, DN_DV), F32)]
        + ([pltpu.SemaphoreType.DMA((3 * ns,)), pltpu.SemaphoreType.DMA((3 * ns,))] if ns else []),
        compiler_params=pltpu.CompilerParams(dimension_semantics=("arbitrary", "arbitrary")),
    )(qkv, qkv, qkv, gb, p, o_gain, *send)


def _blocks_over_ici(p_refs, o_refs, send_sems, recv_sems, first, last):
    x, y, c = _mesh_pos()
    me = 2 * x + y
    chips = _other_chips(x, y)
    pairs = [(a, k) for a in range(len(p_refs)) for k in range(3)]

    def copy(a, k, slot):
        px, py = chips[k]
        return pltpu.make_async_remote_copy(
            src_ref=p_refs[a].at[2 * px + py], dst_ref=o_refs[a].at[slot], send_sem=send_sems.at[3 * a + k],
            recv_sem=recv_sems.at[3 * a + k], device_id=(px, py, c), device_id_type=MESH)

    @pl.when(first)
    def _():
        for a, k in pairs:
            copy(a, k, me).start()

    @pl.when(last)
    def _():
        for a, k in pairs:
            px, py = chips[k]
            copy(a, k, 2 * px + py).wait_recv()
        for a, k in pairs:
            copy(a, k, me).wait_send()


def _dn_delta_bwd(qkv, gb, p, o_gain, o, states, invs, dog, *, name, send=()):
    T = qkv.shape[0]
    tb = min(DN_TB, T)
    nb, nc = T // tb, tb // DN_CHUNK
    H = DN_HEADS
    qscale = DN_DK ** -0.5
    ns = len(send)
    hp = DN_HEADS_BWD

    def body(*refs):
        q_ref, k_ref, v_ref, gb_ref, gate_ref, gain_ref, o_ref, st_ref, inv_ref, dog_ref = refs[:10]
        dq_ref, dk_ref, dv_ref, dgate_ref, dgb_ref, dgain_ref = refs[10 + ns:16 + ns]
        ds_ref = refs[16 + 2 * ns]
        pair, blk = pl.program_id(0), pl.program_id(1)
        first = jnp.logical_and(pair == 0, blk == 0)
        if ns:
            _blocks_over_ici(refs[10:10 + ns], refs[16 + ns:16 + 2 * ns], refs[17 + 2 * ns], refs[18 + 2 * ns],
                             first, jnp.logical_and(pair == H // hp - 1, blk == nb - 1))

        @pl.when(blk == 0)
        def _():
            ds_ref[...] = jnp.zeros_like(ds_ref)

        @pl.when(first)
        def _():
            dgain_ref[...] = jnp.zeros_like(dgain_ref)

        lane = lax.broadcasted_iota(jnp.int32, (1, LANES), 1)
        c = DN_CHUNK
        cut = lambda x: [x[i * c:(i + 1) * c] for i in range(nc)]
        cat = lambda xs: jnp.concatenate(xs, axis=0)
        rsum = lambda x: jnp.sum(x, axis=-1, keepdims=True)
        gbv, gain = gb_ref[...], gain_ref[...]

        def before_chain(e):
            qk_e, v_e = slice(e * DN_DK, (e + 1) * DN_DK), slice(e * DN_DV, (e + 1) * DN_DV)
            g, beta = _dn_head_cols(gbv, hp * pair + e)
            ov, gate, dogv = o_ref[:, v_e], gate_ref[:, v_e], dog_ref[:, v_e]
            r = lax.rsqrt(jnp.mean(ov * ov, axis=-1, keepdims=True) + RMS_EPS)
            oh = ov * r
            dnrm = dogv * _silu(gate)
            dgate_ref[:, v_e] = dogv * (oh * gain) * _dsilu(gate)
            doh = dnrm * gain
            do_l = cut(r * (doh - oh * jnp.mean(doh * oh, axis=-1, keepdims=True)))
            dgain_ref[...] += jnp.sum(dnrm * oh, axis=0, keepdims=True)
            k, v = k_ref[:, qk_e], v_ref[:, v_e]
            t = _dn_local(q_ref[:, qk_e] * qscale, k, v, g, beta, nc, [inv_ref[e, i] for i in range(nc)])
            s_l = [st_ref[e, i] for i in range(nc)]
            vn_l = [u - _dot(w, sb) for u, w, sb in zip(t["u"], t["w"], s_l)]
            return dict(
                t=t, beta=beta, v=v, s=s_l, vn=vn_l, egl=[jnp.exp(x) for x in t["g_last"]],
                dqd=[_dot_nt(a, sb) for a, sb in zip(do_l, s_l)], daqk=[_dot_nt(a, b) for a, b in zip(do_l, vn_l)],
                aqk_do=[_dot_tn(a, b) for a, b in zip(t["aqk"], do_l)],
                qp_do=[_dot_tn(a, b) for a, b in zip(t["qp"], do_l)])

        hs = [before_chain(e) for e in range(hp)]
        ds = [ds_ref[e] for e in range(hp)]
        ds_l = [[None] * nc for _ in range(hp)]
        for i in reversed(range(nc)):
            for e in range(hp):
                ds_l[e][i] = ds[e]
            prod = [_dot_tn(hs[e]["t"]["kw"][i], ds[e]) for e in range(hp)]
            ds = [ds[e] * hs[e]["egl"][i] - prod[e] + hs[e]["qp_do"][i] for e in range(hp)]
        for e in range(hp):
            ds_ref[e] = ds[e]

        def after_chain(e):
            hd, t = hs[e], hs[e]["t"]
            lower, strict, eye = t["lower"], t["strict"], t["eye"]
            s_l, vn_l, dqd_l, daqk_l, egl_l, beta, v = (hd["s"], hd["vn"], hd["dqd"], hd["daqk"], hd["egl"],
                                                         hd["beta"], hd["v"])
            dvn_l = [a + _dot(kd, d) for a, kd, d in zip(hd["aqk_do"], t["kd"], ds_l[e])]
            dkd_l = [_dot_nt(a, d) for a, d in zip(vn_l, ds_l[e])]
            dgl_l = [jnp.sum(rsum(d * sb.astype(F32)), axis=0, keepdims=True) * x
                     for d, sb, x in zip(ds_l[e], s_l, egl_l)]
            dw_l = [-_dot_nt(a, sb) for a, sb in zip(dvn_l, s_l)]
            dbv_l = [_dot_tn(a, b) for a, b in zip(t["inv"], dvn_l)]
            dbk_l = [_dot_tn(a, b) for a, b in zip(t["inv"], dw_l)]
            dlow_l = [-(_dot_nt(a, b) + _dot_nt(x, y)) for a, b, x, y in zip(dbv_l, t["u"], dbk_l, t["w"])]
            m_l = [jnp.where(strict, a * d, 0.0) for a, d in zip(dlow_l, t["dec"])]
            nmat_l = [jnp.where(lower, a * d, 0.0) for a, d in zip(daqk_l, t["dec"])]
            dkb_l = [_dot(m, kk) + b * x for m, kk, b, x in zip(m_l, t["k"], dbk_l, t["eg"])]
            dqs_l = [_dot(n, kk) + a * x for n, kk, a, x in zip(nmat_l, t["k"], dqd_l, t["eg"])]
            dk1_l = [_dot_tn(m, kb) for m, kb in zip(m_l, t["kb"])]
            dk2_l = [_dot_tn(n, q) for n, q in zip(nmat_l, t["qs"])]
            beta_l, v_l = cut(beta), cut(v)
            rowid = lax.broadcasted_iota(jnp.int32, (c, 1), 0)
            dk_l, dg_l, dbeta_l = [], [], []
            for i in range(nc):
                dk_l.append(dk1_l[i] + dk2_l[i] + dkd_l[i] * t["ekd"][i] + dkb_l[i] * beta_l[i])
                gmat = jnp.where(strict, dlow_l[i] * t["low"][i], 0.0) + daqk_l[i] * t["aqk"][i]
                s_kd = rsum(dkd_l[i] * t["kd"][i])
                dg = (rsum(gmat) + rsum(dqd_l[i] * t["qd"][i]) - s_kd + rsum(dbk_l[i] * t["rhs_k"][i]))
                dg_row = -jnp.sum(gmat, axis=0, keepdims=True)
                dg = dg + rsum(jnp.where(eye, dg_row, 0.0))
                dgl = dgl_l[i] + jnp.sum(s_kd, axis=0, keepdims=True)
                dg_l.append(dg + jnp.where(rowid == c - 1, dgl, 0.0))
                dbeta_l.append(rsum(dbv_l[i] * v_l[i]) + rsum(dkb_l[i] * t["k"][i]))
            head = hp * pair + e
            dgb = (jnp.where(lane == head, cat(dg_l), 0.0) + jnp.where(lane == head + DN_HEADS, cat(dbeta_l), 0.0))
            return cat(dqs_l) * qscale, cat(dk_l), cat(dbv_l) * beta, dgb

        for e in range(hp):
            dq, dk, dv, dgb = after_chain(e)
            dq_ref[:, e * DN_DK:(e + 1) * DN_DK] = dq
            dk_ref[:, e * DN_DK:(e + 1) * DN_DK] = dk
            dv_ref[:, e * DN_DV:(e + 1) * DN_DV] = dv
            dgb_ref[e] = dgb

    rev = lambda i: nb - 1 - i
    qk = lambda col0: pl.BlockSpec((tb, hp * DN_DK), lambda h, i: (rev(i), col0 // (hp * DN_DK) + h))
    vblk = lambda col0: pl.BlockSpec((tb, hp * DN_DV), lambda h, i: (rev(i), col0 // (hp * DN_DV) + h))
    gain_spec = pl.BlockSpec((1, DN_DV), lambda h, i: (0, 0))
    return pl.pallas_call(
        body, name=name, grid=(H // hp, nb),
        in_specs=[qk(0), qk(DN_QK_W), vblk(2 * DN_QK_W), pl.BlockSpec((tb, LANES), lambda h, i: (rev(i), 0)),
                  vblk(DN_CONV_W), gain_spec, vblk(0),
                  pl.BlockSpec((hp, nc, DN_DK, DN_DV), lambda h, i: (h, rev(i), 0, 0)),
                  pl.BlockSpec((hp, nc, DN_CHUNK, DN_CHUNK), lambda h, i: (h, rev(i), 0, 0)), vblk(0)] + [HBM] * ns,
        out_specs=[qk(0), qk(0), vblk(0), vblk(DN_CONV_W),
                   pl.BlockSpec((hp, tb, LANES), lambda h, i: (h, rev(i), 0)), gain_spec] + [HBM] * ns,
        out_shape=[jax.ShapeDtypeStruct((T, DN_QK_W), F32), jax.ShapeDtypeStruct((T, DN_QK_W), F32),
                   jax.ShapeDtypeStruct((T, DN_V_W), F32), jax.ShapeDtypeStruct((T, DN_IN_PAD), F32),
                   jax.ShapeDtypeStruct((H, T, LANES), F32), jax.ShapeDtypeStruct((1, DN_DV), F32)]
        + [jax.ShapeDtypeStruct(a.shape, a.dtype) for a in send],
        scratch_shapes=[pltpu.VMEM((hp, DN_DK, DN_DV), F32)]
        + ([pltpu.SemaphoreType.DMA((3 * ns,)), pltpu.SemaphoreType.DMA((3 * ns,))] if ns else []),
        compiler_params=pltpu.CompilerParams(dimension_semantics=("arbitrary", "arbitrary")),
    )(qkv, qkv, qkv, gb, p, o_gain, o, states, invs, dog, *send)


def _dn_conv_bwd(p, conv_w, d, dp, *, first, normed, name):
    T, width = d.shape

    def body(p_ref, w_ref, d_ref, dp_in, dp_ref, dw_ref):
        del dp_in
        x = p_ref[...]
        ksz = w_ref.shape[0]
        xs = [_shift_down(x, ksz - 1 - i) for i in range(ksz)]
        xc = sum(w_ref[i:i + 1, :] * xs[i] for i in range(ksz))
        ds = d_ref[...]
        if normed:
            s = _silu(xc)
            r = lax.rsqrt(jnp.sum(s * s, axis=-1, keepdims=True) + L2_EPS)
            y = s * r
            ds = r * (ds - y * jnp.sum(ds * y, axis=-1, keepdims=True))
        dxc = ds * _dsilu(xc)
        dp_ref[...] = sum(w_ref[i:i + 1, :] * _shift_up(dxc, ksz - 1 - i) for i in range(ksz))
        for i in range(ksz):
            dw_ref[i:i + 1, :] = jnp.sum(dxc * xs[i], axis=0, keepdims=True)

    shifted = pl.BlockSpec((T, LANES), lambda j: (0, first + j))
    return pl.pallas_call(
        body, name=name, grid=(width // LANES,),
        in_specs=[shifted, pl.BlockSpec((DN_CONV, LANES), lambda j: (0, first + j)),
                  pl.BlockSpec((T, LANES), lambda j: (0, j)), pl.BlockSpec(memory_space=pl.ANY)],
        out_specs=[shifted, pl.BlockSpec((DN_CONV, LANES), lambda j: (0, j))],
        out_shape=[jax.ShapeDtypeStruct(dp.shape, F32), jax.ShapeDtypeStruct((DN_CONV, width), F32)],
        input_output_aliases={3: 0},
        compiler_params=pltpu.CompilerParams(dimension_semantics=("parallel",), vmem_limit_bytes=VMEM_BIG),
    )(p, conv_w, d, dp)


def _dn_ab_bwd(p, alog_row, dtb_row, dgb, dp, *, name):
    T = p.shape[0]
    rows = min(DN_TB, T)
    H = DN_HEADS

    def body(p_ref, al_ref, dt_ref, dgb_ref, dp_in, dp_ref, dal_ref, ddt_ref):
        del dp_in

        @pl.when(pl.program_id(0) == 0)
        def _():
            dal_ref[...] = jnp.zeros_like(dal_ref)
            ddt_ref[...] = jnp.zeros_like(ddt_ref)

        blk = p_ref[...]
        is_a, is_b, a_arg, neg_exp, log_a, beta = _dn_ab_parts(blk, al_ref[...], dt_ref[...])
        d = dgb_ref[0]
        for hh in range(1, H):
            d = d + dgb_ref[hh]
        hi, mid, lo_ = _split3(jnp.where(is_a, d, 0.0))
        tri = _dn_chunk_tri(rows, upper=True)
        f = lambda q: jnp.dot(tri, q, preferred_element_type=F32)
        dlog_a = f(hi) + f(mid) + f(lo_)
        da_in = dlog_a * neg_exp * _sigmoid(a_arg)
        db_in = jnp.where(is_b, d, 0.0) * beta * (1.0 - beta)
        dp_ref[...] = jnp.where(is_a, da_in, 0.0) + db_in
        dal_ref[...] += jnp.sum(dlog_a * log_a, axis=0, keepdims=True)
        ddt_ref[...] += jnp.sum(jnp.where(is_a, da_in, 0.0), axis=0, keepdims=True)

    blk = pl.BlockSpec((rows, LANES), lambda i: (i, DN_AB_COL))
    vec = pl.BlockSpec((1, LANES), lambda i: (0, 0))
    return pl.pallas_call(
        body, name=name, grid=(T // rows,),
        in_specs=[blk, vec, vec, pl.BlockSpec((H, rows, LANES), lambda i: (0, i, 0)),
                  pl.BlockSpec(memory_space=pl.ANY)],
        out_specs=[blk, vec, vec],
        out_shape=[jax.ShapeDtypeStruct(dp.shape, F32), jax.ShapeDtypeStruct((1, LANES), F32),
                   jax.ShapeDtypeStruct((1, LANES), F32)],
        input_output_aliases={4: 0},
        compiler_params=pltpu.CompilerParams(dimension_semantics=("arbitrary",)),
    )(p, alog_row, dtb_row, dgb, dp)


def _dn_layer_fwd(x, ng, w_in, conv_w, a_log, dt_bias, o_gain, w_out, tag, send=()):
    alog_row, dtb_row = _dn_lane_rows(a_log, dt_bias)
    gain = o_gain.reshape(1, DN_DV)
    h = _rmsnorm_fwd(x, ng, name=f"{tag}_norm")
    p = _matmul(h, w_in, mode="nn", name=f"{tag}_inproj")
    qkv = _dn_prep_fwd(p, conv_w, name=f"{tag}_prep")
    gb = _dn_ab_fwd(p, alog_row, dtb_row, name=f"{tag}_ab")
    o, og, states, invs, *landed = _dn_delta_fwd(qkv, gb, p, gain, name=f"{tag}_delta", send=send)
    x_new = _matmul(og, w_out, mode="nn", res=x, name=f"{tag}_outproj")
    return x_new, (h, p, qkv, gb, o, og, states, invs), landed


def _dn_layer_bwd(dx, x, ng, w_in, conv_w, a_log, dt_bias, o_gain, w_out, saved, tag, send=(), send_dwin=(),
                  chip_sums=None):
    h, p, qkv, gb, o, og, states, invs = saved
    alog_row, dtb_row = _dn_lane_rows(a_log, dt_bias)
    gain = o_gain.reshape(1, DN_DV)
    d_wout = _matmul(og, dx, mode="tn", out_dtype=BF16, name=f"{tag}_dwout")
    if chip_sums is not None:
        d_wout, = chip_sums([_cut2(_by_rows(d_wout))], f"{tag}wout")
        send = list(send) + [d_wout]
    dog = _matmul(dx, w_out, mode="nt", name=f"{tag}_dog")
    dq, dk, dv, dp, dgb, dgain, *landed = _dn_delta_bwd(qkv, gb, p, gain, o, states, invs, dog,
                                                        name=f"{tag}_deltabwd", send=send)
    n_qk = DN_QK_W // LANES
    dp, dconv_q = _dn_conv_bwd(p, conv_w, dq, dp, first=0, normed=True, name=f"{tag}_convbwd_q")
    dp, dconv_k = _dn_conv_bwd(p, conv_w, dk, dp, first=n_qk, normed=True, name=f"{tag}_convbwd_k")
    dp, dconv_v = _dn_conv_bwd(p, conv_w, dv, dp, first=2 * n_qk, normed=False, name=f"{tag}_convbwd_v")
    dconv = jnp.concatenate([dconv_q, dconv_k, dconv_v], axis=1)
    dp, dal, ddt = _dn_ab_bwd(p, alog_row, dtb_row, dgb, dp, name=f"{tag}_abbwd")
    d_win = _matmul(h, dp, mode="tn", name=f"{tag}_dwin", send=send_dwin)
    if send_dwin:
        d_win, *landed_dwin = d_win
        landed = landed + landed_dwin
    if chip_sums is not None:
        d_win, = chip_sums([_cut2(_by_cols(d_win))], f"{tag}win")
        dh, landed_win = _matmul(dp, w_in, mode="nt", name=f"{tag}_dh", send=[d_win])
        landed = landed + [landed_win]
    else:
        dh = _matmul(dp, w_in, mode="nt", name=f"{tag}_dh")
    dx_prev, dng = _rmsnorm_bwd(x, ng, dh, dx, name=f"{tag}_normbwd")
    return dx_prev, dng, d_win, dconv, dal[0, :DN_HEADS], ddt[0, :DN_HEADS], dgain[0], d_wout, landed


def _by_cols(dw):
    return _split(dw[:, :DN_IN].astype(BF16), 1)


def _by_rows(dw):
    return dw.reshape(N_CHIPS, -1, dw.shape[-1])


def _cut2(g4):
    return g4.reshape(N_CHIPS, 2, -1, g4.shape[-1])


def _sb_gains(g):
    return jnp.concatenate([g, g]).reshape(1, LANES)


def _sb_layer_fwd(x, ng, w_in, gq, gk, w_out, tag, send=()):
    h = _rmsnorm_fwd(x, ng, name=f"{tag}_norm")
    p3 = _matmul(h, w_in, mode="nn", b_parts=4, out_parts=4, name=f"{tag}_inproj")
    og, o, lsum, live, *landed = _sb_attn_fwd(p3, _sb_gains(gq), _sb_gains(gk), name=f"{tag}_attn", send=send)
    x_new = _matmul(og, w_out, mode="nn", res=x, name=f"{tag}_outproj")
    return x_new, (h, p3, og, o, lsum, live), landed


def _sb_layer_bwd(dx, x, ng, w_in, gq, gk, w_out, saved, tag):
    h, p3, og, o, lsum, live = saved
    d_wout = _matmul(og, dx, mode="tn", out_dtype=BF16, name=f"{tag}_dwout")
    dog = _matmul(dx, w_out, mode="nt", name=f"{tag}_dog")
    dp3, dgq, dgk = _sb_attn_bwd(p3, _sb_gains(gq), _sb_gains(gk), o, lsum, live, dog, name=f"{tag}_attnbwd")
    fold = lambda d: jnp.sum(d.reshape(-1, SB_DH), axis=0)
    d_win = _matmul(h, dp3, mode="tn", b_parts=4, out_parts=4, out_dtype=BF16, name=f"{tag}_dwin")
    dh = _matmul(dp3, w_in, mode="nt", a_parts=4, b_parts=4, name=f"{tag}_dh")
    dx_prev, dng = _rmsnorm_bwd(x, ng, dh, dx, name=f"{tag}_normbwd")
    return dx_prev, dng, d_win, fold(dgq), fold(dgk), d_wout


N_CHIPS = 4
HBM = pl.BlockSpec(memory_space=pl.ANY)


def _mesh_pos():
    return lax.axis_index("x"), lax.axis_index("y"), lax.axis_index("c")


def _other_chips(x, y):
    return [(1 - x, y), (x, 1 - y), (1 - x, 1 - y)]


def _chip_exchange(srcs, *, send_slot_is_dest, copy_own, name):
    n = len(srcs)

    def body(*refs):
        src_refs, out_refs = refs[:n], refs[n:2 * n]
        send_sems, recv_sems, local_sems = refs[2 * n:]
        x, y, c = _mesh_pos()
        me = 2 * x + y
        chips = _other_chips(x, y)
        local = []
        for a in range(n):
            if not copy_own[a]:
                continue
            own = src_refs[a].at[me] if send_slot_is_dest else src_refs[a]
            local.append(pltpu.make_async_copy(own, out_refs[a].at[me], local_sems.at[a]))
        for cp in local:
            cp.start()

        def copy(a, k, landing_slot):
            px, py = chips[k]
            src = src_refs[a].at[2 * px + py] if send_slot_is_dest else src_refs[a]
            return pltpu.make_async_remote_copy(
                src_ref=src, dst_ref=out_refs[a].at[landing_slot],
                send_sem=send_sems.at[a * 3 + k], recv_sem=recv_sems.at[a * 3 + k],
                device_id=(px, py, c), device_id_type=MESH)

        sends = [copy(a, k, me) for a in range(n) for k in range(3)]
        for cp in sends:
            cp.start()
        for a in range(n):
            for k in range(3):
                px, py = chips[k]
                copy(a, k, 2 * px + py).wait_recv()
        for cp in sends:
            cp.wait_send()
        for cp in local:
            cp.wait()

    outs = []
    for s in srcs:
        shape = s.shape if send_slot_is_dest else (N_CHIPS,) + s.shape
        outs.append(jax.ShapeDtypeStruct(shape, s.dtype))
    return pl.pallas_call(
        body, name=name, in_specs=[HBM] * n, out_specs=[HBM] * n, out_shape=outs,
        scratch_shapes=[pltpu.SemaphoreType.DMA((3 * n,)), pltpu.SemaphoreType.DMA((3 * n,)),
                        pltpu.SemaphoreType.DMA((n,))],
    )(*srcs)


def _sibling_exchange(srcs, *, name):
    n = len(srcs)

    def body(*refs):
        src_refs, out_refs = refs[:n], refs[n:2 * n]
        send_sems, recv_sems = refs[2 * n:]
        x, y, c = _mesh_pos()
        copies = [pltpu.make_async_remote_copy(
            src_ref=src_refs[a], dst_ref=out_refs[a], send_sem=send_sems.at[a], recv_sem=recv_sems.at[a],
            device_id=(x, y, 1 - c), device_id_type=MESH) for a in range(n)]
        for cp in copies:
            cp.start()
        for cp in copies:
            cp.wait()

    return pl.pallas_call(
        body, name=name, in_specs=[HBM] * n, out_specs=[HBM] * n,
        out_shape=[jax.ShapeDtypeStruct(s.shape, s.dtype) for s in srcs],
        scratch_shapes=[pltpu.SemaphoreType.DMA((n,)), pltpu.SemaphoreType.DMA((n,))],
    )(*srcs)


def _gather_halves(shards, small, *, name):
    n = len(shards)

    def body(*refs):
        s_refs, small_ref = refs[:n], refs[n]
        o_refs, osmall_ref = refs[n + 1:2 * n + 1], refs[2 * n + 1]
        send_sems, recv_sems, local_sems = refs[2 * n + 2:]
        x, y, c = _mesh_pos()
        me = 2 * x + y
        chips = _other_chips(x, y)
        local = [pltpu.make_async_copy(small_ref, osmall_ref.at[me], local_sems.at[0])]
        for cp in local:
            cp.start()

        def over_ici(a, k, slot):
            px, py = chips[k]
            return pltpu.make_async_remote_copy(
                src_ref=s_refs[a].at[c], dst_ref=o_refs[a].at[slot, c], send_sem=send_sems.at[3 * a + k],
                recv_sem=recv_sems.at[3 * a + k], device_id=(px, py, c), device_id_type=MESH)

        def small_copy(k, slot):
            px, py = chips[k]
            return pltpu.make_async_remote_copy(
                src_ref=small_ref, dst_ref=osmall_ref.at[slot], send_sem=send_sems.at[3 * n + k],
                recv_sem=recv_sems.at[3 * n + k], device_id=(px, py, c), device_id_type=MESH)

        def to_sibling(a, k, half):
            px, py = chips[k]
            blk = o_refs[a].at[2 * px + py, half]
            return pltpu.make_async_remote_copy(
                src_ref=blk, dst_ref=blk, send_sem=send_sems.at[3 * n + 3 + 3 * a + k],
                recv_sem=recv_sems.at[3 * n + 3 + 3 * a + k], device_id=(x, y, 1 - c), device_id_type=MESH)

        sends = [over_ici(a, k, me) for a in range(n) for k in range(3)] + [small_copy(k, me) for k in range(3)]
        for cp in sends:
            cp.start()
        passed = []
        for a in range(n):
            for k in range(3):
                px, py = chips[k]
                over_ici(a, k, 2 * px + py).wait_recv()
                passed.append(to_sibling(a, k, c))
                passed[-1].start()
        for k in range(3):
            px, py = chips[k]
            small_copy(k, 2 * px + py).wait_recv()
        for a in range(n):
            for k in range(3):
                to_sibling(a, k, 1 - c).wait_recv()
        for cp in sends + passed:
            cp.wait_send()
        for cp in local:
            cp.wait()

    nsem = 6 * n + 3
    return pl.pallas_call(
        body, name=name, in_specs=[HBM] * (n + 1), out_specs=[HBM] * (n + 1),
        out_shape=[jax.ShapeDtypeStruct((N_CHIPS,) + s.shape, s.dtype) for s in shards + [small]],
        scratch_shapes=[pltpu.SemaphoreType.DMA((nsem,)), pltpu.SemaphoreType.DMA((nsem,)),
                        pltpu.SemaphoreType.DMA((1,))],
    )(*shards, small)


def _forward_halves(landed, *, name):
    n = len(landed)

    def body(*refs):
        o_refs = refs[n:2 * n]
        send_sems, recv_sems = refs[2 * n:]
        x, y, c = _mesh_pos()
        chips = _other_chips(x, y)
        pairs = [(a, k) for a in range(n) for k in range(3)]

        def copy(a, k, half):
            px, py = chips[k]
            blk = o_refs[a].at[2 * px + py, half]
            return pltpu.make_async_remote_copy(
                src_ref=blk, dst_ref=blk, send_sem=send_sems.at[3 * a + k], recv_sem=recv_sems.at[3 * a + k],
                device_id=(x, y, 1 - c), device_id_type=MESH)

        sends = [copy(a, k, c) for a, k in pairs]
        for cp in sends:
            cp.start()
        for a, k in pairs:
            copy(a, k, 1 - c).wait_recv()
        for cp in sends:
            cp.wait_send()

    return pl.pallas_call(
        body, name=name, in_specs=[HBM] * n, out_specs=[HBM] * n,
        out_shape=[jax.ShapeDtypeStruct(a.shape, a.dtype) for a in landed],
        input_output_aliases={a: a for a in range(n)},
        scratch_shapes=[pltpu.SemaphoreType.DMA((3 * n,)), pltpu.SemaphoreType.DMA((3 * n,))],
    )(*landed)


def _swap_other_half(g_list, *, name):
    n = len(g_list)

    def body(*refs):
        g_refs, o_refs = refs[:n], refs[n:2 * n]
        send_sems, recv_sems = refs[2 * n:]
        x, y, c = _mesh_pos()
        copies = [pltpu.make_async_remote_copy(
            src_ref=g_refs[a].at[:, 1 - c], dst_ref=o_refs[a], send_sem=send_sems.at[a], recv_sem=recv_sems.at[a],
            device_id=(x, y, 1 - c), device_id_type=MESH) for a in range(n)]
        for cp in copies:
            cp.start()
        for cp in copies:
            cp.wait()

    return pl.pallas_call(
        body, name=name, in_specs=[HBM] * n, out_specs=[HBM] * n,
        out_shape=[jax.ShapeDtypeStruct((g.shape[0],) + g.shape[2:], g.dtype) for g in g_list],
        scratch_shapes=[pltpu.SemaphoreType.DMA((n,)), pltpu.SemaphoreType.DMA((n,))],
    )(*g_list)


def _row_tile(r):
    return _pick(r, (512, 256, 128, 64, 32, 16, 8))


def _add_my_half(g4, sib4, core, *, name):
    n, _, r, C = g4.shape
    tr = _row_tile(r)

    def body(core_ref, g_ref, s_ref, o_ref):
        del core_ref
        o_ref[...] = (g_ref[...].astype(F32) + s_ref[...].astype(F32)).astype(o_ref.dtype)

    return pl.pallas_call(
        body, name=name,
        grid_spec=pltpu.PrefetchScalarGridSpec(
            num_scalar_prefetch=1, grid=(n, r // tr),
            in_specs=[pl.BlockSpec((None, None, tr, C), lambda j, i, core_ref: (j, core_ref[0], i, 0)),
                      pl.BlockSpec((None, tr, C), lambda j, i, core_ref: (j, i, 0))],
            out_specs=pl.BlockSpec((None, tr, C), lambda j, i, core_ref: (j, i, 0))),
        out_shape=jax.ShapeDtypeStruct((n, r, C), g4.dtype),
        compiler_params=pltpu.CompilerParams(dimension_semantics=("parallel", "parallel")),
    )(core, g4, sib4)


def _sum_chips(landed, part, me, *, name):
    _, r, C = landed.shape
    tr = _row_tile(r)

    def body(me_ref, own_ref, r1_ref, r2_ref, r3_ref, o_ref):
        del me_ref
        f = lambda ref: ref[...].astype(F32)
        o_ref[...] = ((f(own_ref) + f(r1_ref)) + f(r2_ref)) + f(r3_ref)

    slot = lambda d: pl.BlockSpec((None, tr, C), lambda i, me_ref: ((me_ref[0] + d) % N_CHIPS, i, 0))
    return pl.pallas_call(
        body, name=name,
        grid_spec=pltpu.PrefetchScalarGridSpec(
            num_scalar_prefetch=1, grid=(r // tr,), in_specs=[slot(0), slot(1), slot(2), slot(3)],
            out_specs=pl.BlockSpec((tr, C), lambda i, me_ref: (i, 0))),
        out_shape=jax.ShapeDtypeStruct((r, C), F32),
        compiler_params=pltpu.CompilerParams(dimension_semantics=("parallel",)),
    )(me, part, landed, landed, landed)


def _adamw_halves(w, mine, theirs, m, v, core, *, layer, prev, name):
    shape = w.shape
    r, C = mine.shape
    tr = _pick(r, (128, 64, 32, 16, 8))
    per = r // tr
    view = lambda a: a.reshape(-1, C)
    n_prev = 0 if prev is None else 4

    def body(*refs):
        core_ref, w_ref, gm_ref, gt_ref, m_ref, v_ref = refs[:6]
        g_ref, d_ref, nm_ref, nv_ref = refs[6 + n_prev:]
        gv = jnp.where(pl.program_id(0) == core_ref[0], gm_ref[...], gt_ref[...])
        g_ref[...] = gv
        d_ref[...], nm_ref[...], nv_ref[...] = _adamw_math(w_ref[...], gv, m_ref[...], v_ref[...])

    half = pl.BlockSpec((tr, C), lambda h, i, core_ref: ((2 * layer + h) * per + i, 0))
    row = pl.BlockSpec((tr, C), lambda h, i, core_ref: (i, 0))
    out = jax.ShapeDtypeStruct((math.prod(shape) // C, C), F32)
    res = pl.pallas_call(
        body, name=name,
        grid_spec=pltpu.PrefetchScalarGridSpec(
            num_scalar_prefetch=1, grid=(2, per), in_specs=[half, row, row, half, half] + [HBM] * n_prev,
            out_specs=[half] * 4),
        out_shape=[out] * 4,
        input_output_aliases={6 + j: j for j in range(n_prev)},
        compiler_params=pltpu.CompilerParams(dimension_semantics=("parallel", "parallel")),
    )(core, view(w), mine, theirs, view(m), view(v), *([] if prev is None else [view(a) for a in prev]))
    return tuple(a.reshape(shape) for a in res)


def _sum_small(recv4, *, name):
    _, R, C = recv4.shape

    def body(r_ref, o_ref):
        o_ref[...] = ((r_ref[0] + r_ref[1]) + r_ref[2]) + r_ref[3]

    return pl.pallas_call(body, name=name, out_shape=jax.ShapeDtypeStruct((R, C), F32))(recv4)


def _add(a, b, *, name):
    R, C = a.shape
    tr = _pick(R, (512, 256, 128, 64, 32, 16, 8))
    blk = pl.BlockSpec((tr, C), lambda i: (i, 0))

    def body(a_ref, b_ref, o_ref):
        o_ref[...] = a_ref[...] + b_ref[...]

    return pl.pallas_call(body, name=name, grid=(R // tr,), in_specs=[blk, blk], out_specs=blk,
                          out_shape=jax.ShapeDtypeStruct((R, C), F32),
                          compiler_params=pltpu.CompilerParams(dimension_semantics=("parallel",)))(a, b)


def _adamw_math(w, g, m, v):
    nm = ADAM_B1 * m + (1.0 - ADAM_B1) * g
    nv = ADAM_B2 * v + (1.0 - ADAM_B2) * (g * g)
    m_hat = nm / (1.0 - ADAM_B1 ** ADAM_STEP)
    v_hat = nv / (1.0 - ADAM_B2 ** ADAM_STEP)
    return -ADAM_LR * (m_hat / (jnp.sqrt(v_hat) + ADAM_EPS) + ADAM_WD * w), nm, nv


def _adamw(w, g, m, v, *, name):
    shape = w.shape
    C = shape[-1]
    R = w.size // C
    two = lambda a: a.reshape(R, C)
    tr = _pick(R, (256, 128, 64, 32, 16, 8)) if R % 8 == 0 and R > 8 else R
    blk = pl.BlockSpec((tr, C), lambda i: (i, 0))

    def body(w_ref, g_ref, m_ref, v_ref, d_ref, nm_ref, nv_ref):
        d_ref[...], nm_ref[...], nv_ref[...] = _adamw_math(w_ref[...], g_ref[...], m_ref[...], v_ref[...])

    out = jax.ShapeDtypeStruct((R, C), F32)
    d, nm, nv = pl.pallas_call(
        body, name=name, grid=(R // tr,), in_specs=[blk] * 4, out_specs=[blk] * 3, out_shape=[out] * 3,
        compiler_params=pltpu.CompilerParams(dimension_semantics=("parallel",)),
    )(two(w), two(g), two(m), two(v))
    return d.reshape(shape), nm.reshape(shape), nv.reshape(shape)


BIG = (("dn_w_in", (2, 1024, 1540), 2), ("dn_w_out", (2, 512, 1024), 1), ("sb_w_in", (1, 1024, 1024), 2),
       ("sb_w_out", (1, 256, 1024), 1), ("sc_w_in", (1, 1024, 2048), 2), ("sc_w_out", (1, 512, 1024), 1))
SMALL = (("dn_conv_w", (2, 4, 1024), 2), ("dn_o_norm_g", (2, 64), 1), ("sc_conv_w", (1, 3, 512), 2))
REPL = (("norm_g", (4, 1024)), ("dn_a_log", (2, 8)), ("dn_dt_bias", (2, 8)), ("sb_q_norm_g", (1, 64)),
        ("sb_k_norm_g", (1, 64)))


def _halves(shard):
    return shard.reshape(2, -1, shard.shape[-1])


def _pack(arrays, cols, lead=()):
    flat = jnp.concatenate([a.reshape(lead + (-1,)) for a in arrays], axis=-1)
    n = flat.shape[-1]
    rows = -(-n // cols)
    unit = 512 if rows > 512 else 8
    rows = -(-rows // unit) * unit
    flat = jnp.pad(flat, [(0, 0)] * len(lead) + [(0, rows * cols - n)])
    return flat.reshape(lead + (rows, cols))


def _unpack(buf, table, lead=()):
    flat = buf.reshape(lead + (-1,))
    out, off = {}, 0
    for entry in table:
        name, shape = entry[0], entry[1]
        n = math.prod(shape)
        out[name] = flat[..., off:off + n].reshape(lead + shape)
        off += n
    return out


def _join(shards, axis):
    return jnp.concatenate([shards[j] for j in range(N_CHIPS)], axis=axis)


def _split(full, axis):
    return jnp.stack(jnp.split(full, N_CHIPS, axis=axis), axis=0)


def kernel(x, norm_g, dn_w_in, dn_conv_w, dn_a_log, dn_dt_bias, dn_o_norm_g, dn_w_out, sb_w_in, sb_q_norm_g, sb_k_norm_g, sb_w_out, sc_w_in, sc_conv_w, sc_w_out, loss_target, m_norm_g, m_dn_w_in, m_dn_conv_w, m_dn_a_log, m_dn_dt_bias, m_dn_o_norm_g, m_dn_w_out, m_sb_w_in, m_sb_q_norm_g, m_sb_k_norm_g, m_sb_w_out, m_sc_w_in, m_sc_conv_w, m_sc_w_out, v_norm_g, v_dn_w_in, v_dn_conv_w, v_dn_a_log, v_dn_dt_bias, v_dn_o_norm_g, v_dn_w_out, v_sb_w_in, v_sb_q_norm_g, v_sb_k_norm_g, v_sb_w_out, v_sc_w_in, v_sc_conv_w, v_sc_w_out):
    weights = dict(norm_g=norm_g, dn_w_in=dn_w_in, dn_conv_w=dn_conv_w, dn_a_log=dn_a_log, dn_dt_bias=dn_dt_bias,
                   dn_o_norm_g=dn_o_norm_g, dn_w_out=dn_w_out, sb_w_in=sb_w_in, sb_q_norm_g=sb_q_norm_g,
                   sb_k_norm_g=sb_k_norm_g, sb_w_out=sb_w_out, sc_w_in=sc_w_in, sc_conv_w=sc_conv_w, sc_w_out=sc_w_out)
    m_in = dict(norm_g=m_norm_g, dn_w_in=m_dn_w_in, dn_conv_w=m_dn_conv_w, dn_a_log=m_dn_a_log,
                dn_dt_bias=m_dn_dt_bias, dn_o_norm_g=m_dn_o_norm_g, dn_w_out=m_dn_w_out, sb_w_in=m_sb_w_in,
                sb_q_norm_g=m_sb_q_norm_g, sb_k_norm_g=m_sb_k_norm_g, sb_w_out=m_sb_w_out, sc_w_in=m_sc_w_in,
                sc_conv_w=m_sc_conv_w, sc_w_out=m_sc_w_out)
    v_in = dict(norm_g=v_norm_g, dn_w_in=v_dn_w_in, dn_conv_w=v_dn_conv_w, dn_a_log=v_dn_a_log,
                dn_dt_bias=v_dn_dt_bias, dn_o_norm_g=v_dn_o_norm_g, dn_w_out=v_dn_w_out, sb_w_in=v_sb_w_in,
                sb_q_norm_g=v_sb_q_norm_g, sb_k_norm_g=v_sb_k_norm_g, sb_w_out=v_sb_w_out, sc_w_in=v_sc_w_in,
                sc_conv_w=v_sc_conv_w, sc_w_out=v_sc_w_out)
    order = list(weights)
    xi, yi, ci = _mesh_pos()

    small = _pack([weights[n] for n, _, _ in SMALL], LANES)
    later = [("dn_w_in", 1), ("dn_w_out", 1), ("sb_w_in", 0), ("sb_w_out", 0), ("sc_w_in", 0), ("sc_w_out", 0)]
    piece = lambda n, l: _halves(weights[n][l].astype(BF16)[None])
    own_first = [piece("dn_w_in", 0), piece("dn_w_out", 0)]
    own_later = [piece(n, l) for n, l in later]
    own_last, own_mid = own_later[:2], own_later[2:]
    me = 2 * xi + yi
    whole = lambda g4, own: lax.dynamic_update_index_in_dim(g4, own, me, 0)
    flat = lambda g4: g4.reshape(N_CHIPS, -1, g4.shape[-1])
    rows_of = lambda w4: w4.reshape(-1, w4.shape[-1])
    w_in0, w_out0, small4 = _gather_halves(own_first, small, name="gather_first")
    full = {n: _join(a, ax) for (n, _, ax), a in zip(SMALL, _unpack(small4, SMALL, (N_CHIPS,)).values())}

    def dn_in(g4, own):
        cols = [jnp.where(me == j, own, g4[j]).reshape(-1, g4.shape[-1]) for j in range(N_CHIPS)]
        cols.append(jnp.zeros((cols[0].shape[0], DN_IN_PAD - DN_IN), g4.dtype))
        return jnp.concatenate(cols, axis=1)

    def dn_args(j, w_in4, own_in, w_out4):
        return (dn_in(w_in4, own_in), full["dn_conv_w"][j], dn_a_log[j], dn_dt_bias[j], full["dn_o_norm_g"][j],
                rows_of(w_out4))

    x0 = x[0]
    dn0 = dn_args(0, w_in0, own_first[0], whole(w_out0, own_first[1]))
    x1, s0, landed = _dn_layer_fwd(x0, norm_g[0], *dn0, "l0", send=own_mid)
    landed = _forward_halves(landed, name="forward_halves_mid")
    sb_in, sb_out, sc_in, sc_out = [whole(g4, own) for g4, own in zip(landed, own_mid)]
    sb_args = (flat(sb_in), sb_q_norm_g[0], sb_k_norm_g[0], rows_of(sb_out))
    sc_args = (flat(sc_in), full["sc_conv_w"][0], rows_of(sc_out))
    x2, s1, landed = _sb_layer_fwd(x1, norm_g[1], *sb_args, "l1", send=own_last)
    landed = _forward_halves(landed, name="forward_halves_last")
    dn1 = dn_args(1, landed[0], own_last[0], whole(landed[1], own_last[1]))
    x3, s2 = _sc_layer_fwd(x2, norm_g[2], *sc_args, "l2")
    x4, s3, _ = _dn_layer_fwd(x3, norm_g[3], *dn1, "l3")
    dy, loss_local = _loss_head(x4, loss_target[0], name="loss_head")
    loss = lax.psum(loss_local[0, 0], ("x", "y", "c"))

    core = ci.astype(jnp.int32).reshape(1)
    chip = me.astype(jnp.int32).reshape(1)

    def chip_sums(g_list, tag):
        sib = _swap_other_half(g_list, name=f"swap_halves_{tag}")
        return [_add_my_half(g, s, core, name=f"sum_cores_{tag}{i}") for i, (g, s) in enumerate(zip(g_list, sib))]

    dx3, dng3, dwin3, dconv3, dal3, ddt3, dgain3, dwout3, _ = _dn_layer_bwd(dy, x3, norm_g[3], *dn1, s3, "l3")
    dx2, dng2, dwin2, dconv2, dwout2 = _sc_layer_bwd(dx3, x2, norm_g[2], *sc_args, s2, "l2")
    dx1, dng1, dwin1, dgq, dgk, dwout1 = _sb_layer_bwd(dx2, x1, norm_g[1], *sb_args, s1, "l1")
    part_later = chip_sums([_cut2(_by_cols(dwin3)), _cut2(_by_rows(dwout3)), _cut2(dwin1), _cut2(_by_rows(dwout1)),
                            _cut2(dwin2), _cut2(_by_rows(dwout2))], "later")
    dx0, dng0, part_win0, dconv0, dal0, ddt0, dgain0, part_wout0, landed = _dn_layer_bwd(
        dx1, x0, norm_g[0], *dn0, s0, "l0", send=part_later[2:], send_dwin=part_later[:2], chip_sums=chip_sums)
    pieces = later[2:] + [("dn_w_out", 0)] + later[:2] + [("dn_w_in", 0)]
    mine = {(n, l): _sum_chips(r, p, chip, name=f"sum_chips_{n}{l}")
            for (n, l), r, p in zip(pieces, landed, part_later[2:] + [part_wout0] + part_later[:2] + [part_win0])}
    pieces = sorted(pieces, key=lambda nl: nl[1])
    mine = [mine[nl] for nl in pieces]
    theirs = _sibling_exchange(mine, name="swap_results")
    upd = {}
    for (n, l), a, b in zip(pieces, mine, theirs):
        upd[n] = _adamw_halves(weights[n], a, b, m_in[n], v_in[n], core, layer=l, prev=upd.get(n),
                               name=f"adamw_{n}{l}")
    g_out = {n: upd[n][0] for n, _, _ in BIG}

    grads = dict(
        norm_g=jnp.concatenate([dng0, dng1, dng2, dng3], axis=0), dn_conv_w=jnp.stack([dconv0, dconv3]),
        dn_a_log=jnp.stack([dal0, dal3]), dn_dt_bias=jnp.stack([ddt0, ddt3]),
        dn_o_norm_g=jnp.stack([dgain0, dgain3]), sb_q_norm_g=dgq[None], sb_k_norm_g=dgk[None],
        sc_conv_w=dconv2[None])
    repl = [jnp.broadcast_to(grads[n][None], (N_CHIPS,) + s) for n, s in REPL]
    gsmall = _pack([_split(grads[n], ax) for n, _, ax in SMALL] + repl, LANES, (N_CHIPS,))
    rsmall, = _chip_exchange([gsmall], send_slot_is_dest=True, copy_own=(True,), name="scatter_small")
    psmall = _sum_small(rsmall, name="sum_chips_small")
    qsmall, = _sibling_exchange([psmall], name="swap_cores_small")
    tsmall = _add(psmall, qsmall, name="sum_cores_small")
    g_out.update(_unpack(tsmall, SMALL + REPL))

    for n in order:
        if n not in upd:
            upd[n] = (g_out[n],) + _adamw(weights[n], g_out[n], m_in[n], v_in[n], name=f"adamw_{n}")
    return (loss, dx0[None], *[upd[n][0] for n in order], *[upd[n][1] for n in order],
            *[upd[n][2] for n in order], *[upd[n][3] for n in order])
```

```python
import functools
import math

import jax
import jax.numpy as jnp
from jax import lax
from jax.experimental import pallas as pl
from jax.experimental.pallas import tpu as pltpu

F32 = jnp.float32
BF16 = jnp.bfloat16
MESH = pl.DeviceIdType.MESH

RMS_EPS = 1e-6
L2_EPS = 1e-6
LANES = 128
VMEM_BIG = 60 * 1024 * 1024
MM_VMEM = 44 * 1024 * 1024

DN_HEADS, DN_DK, DN_DV, DN_CHUNK, DN_CONV = 8, 128, 256, 64, 4
DN_QK_W = DN_HEADS * DN_DK
DN_V_W = DN_HEADS * DN_DV
DN_CONV_W = 2 * DN_QK_W + DN_V_W
DN_IN = DN_CONV_W + DN_V_W + 2 * DN_HEADS
DN_IN_PAD = DN_CONV_W + DN_V_W + LANES
SB_DH = 64
SC_CONV = 3

ADAM_LR, ADAM_B1, ADAM_B2, ADAM_EPS, ADAM_WD, ADAM_STEP = 0.001, 0.9, 0.999, 1e-08, 0.01, 10


def _pick(n, cands):
    for c in cands:
        if n % c == 0:
            return c
    raise ValueError(f"no tile for {n} in {cands}")


def _bf(x):
    return x.astype(BF16)


def _dot(a, b):
    return jnp.dot(_bf(a), _bf(b), preferred_element_type=F32)


def _dot_nt(a, b):
    return lax.dot_general(_bf(a), _bf(b), (((1,), (1,)), ((), ())), preferred_element_type=F32)


def _dot_tn(a, b):
    return lax.dot_general(_bf(a), _bf(b), (((0,), (0,)), ((), ())), preferred_element_type=F32)


def _split3(a):
    hi = _bf(a)
    r = a - hi.astype(F32)
    mid = _bf(r)
    lo = _bf(r - mid.astype(F32))
    return hi, mid, lo


def _sigmoid(x):
    return 1.0 / (1.0 + jnp.exp(-x))


def _silu(x):
    return x * _sigmoid(x)


def _dsilu(x):
    s = _sigmoid(x)
    return s * (1.0 + x * (1.0 - s))


def _softplus(x):
    return jnp.maximum(x, 0.0) + jnp.log(1.0 + jnp.exp(-jnp.abs(x)))


def _shift_down(z, k):
    if k == 0:
        return z
    row = lax.broadcasted_iota(jnp.int32, z.shape, 0)
    return jnp.where(row >= k, pltpu.roll(z, k, 0), 0.0)


def _shift_up(z, k):
    if k == 0:
        return z
    n = z.shape[0]
    row = lax.broadcasted_iota(jnp.int32, z.shape, 0)
    return jnp.where(row < n - k, pltpu.roll(z, n - k, 0), 0.0)


def _matmul(a, b, *, mode, name, res=None, a_parts=1, b_parts=1, out_parts=1, out_dtype=F32, send=()):
    def dims2(x, parts):
        if parts == 1:
            return x.shape
        assert x.shape[0] == parts
        return (x.shape[1], x.shape[2] * parts)

    ash, bsh = dims2(a, a_parts), dims2(b, b_parts)
    if mode == "nn":
        (M, K), (K2, N) = ash, bsh
        dn = (((1,), (0,)), ((), ()))
    elif mode == "nt":
        (M, K), (N, K2) = ash, bsh
        dn = (((1,), (1,)), ((), ()))
    else:
        (K, M), (K2, N) = ash, bsh
        dn = (((0,), (0,)), ((), ()))
    assert K == K2, (ash, bsh, mode)
    tm_max = _pick(M, (512, 256, 128, 64, 32, 16, 8))
    n_unit = N // max(out_parts, b_parts if mode != "nt" else 1)
    k_unit = K // max(a_parts if mode != "tn" else 1, b_parts if mode == "nt" else 1)
    tm, tn, tk = min(
        ((m, n, k) for m in {tm_max, max(tm_max // 2, 8)}
         for n in (2048, 1792, 1024, 896, 768, 512, 384, 256, 128) if n_unit % n == 0
         for k in (k_unit, 2048, 1792, 1024, 896, 512, 256, 128) if k_unit % k == 0
         if 2 * (m * k * a.dtype.itemsize + k * n * b.dtype.itemsize + 2 * m * n * 4) + m * n * 4 <= MM_VMEM),
        key=lambda t: (-t[0] * t[1] * t[2], -t[0], -t[2]))
    nk = K // tk
    grid = (M // tm, N // tn, nk)

    def spec(parts, rows_are, cols_are, tr, tc, width):
        per = width // parts // tc
        if parts == 1:
            return pl.BlockSpec((tr, tc), lambda i, j, k: ((i, j, k)[rows_are], (i, j, k)[cols_are]))
        return pl.BlockSpec((None, tr, tc), lambda i, j, k: ((i, j, k)[cols_are] // per, (i, j, k)[rows_are],
                                                             (i, j, k)[cols_are] % per))

    if mode == "nn":
        a_spec = spec(a_parts, 0, 2, tm, tk, K)
        b_spec = spec(b_parts, 2, 1, tk, tn, N)
    elif mode == "nt":
        a_spec = spec(a_parts, 0, 2, tm, tk, K)
        b_spec = spec(b_parts, 1, 2, tn, tk, K)
    else:
        a_spec = spec(a_parts, 2, 0, tk, tm, M)
        b_spec = spec(b_parts, 2, 1, tk, tn, N)
    o_spec = spec(out_parts, 0, 1, tm, tn, N)
    in_specs = [a_spec, b_spec]
    operands = [a, b]
    if res is not None:
        in_specs.append(pl.BlockSpec((tm, tn), lambda i, j, k: (i, j)))
        operands.append(res)

    n_in = len(operands)
    ns = len(send)

    def finish(refs, r):
        if res is not None:
            r = refs[2][...] + r
        refs[n_in + ns][...] = r.astype(out_dtype)

    def body(*refs):
        if ns:
            at = lambda step: functools.reduce(jnp.logical_and, [pl.program_id(d) == step[d] for d in range(3)])
            _blocks_over_ici(refs[n_in:n_in + ns], refs[n_in + ns + 1:n_in + 2 * ns + 1], refs[-2], refs[-1],
                             at((0, 0, 0)), at(tuple(g - 1 for g in grid)))
        part = lax.dot_general(_bf(refs[0][...]), _bf(refs[1][...]), dn, preferred_element_type=F32)
        if nk == 1:
            finish(refs, part)
            return
        acc_ref = refs[n_in + 2 * ns + 1]
        k = pl.program_id(2)

        @pl.when(k == 0)
        def _():
            acc_ref[...] = part

        @pl.when(jnp.logical_and(k > 0, k < nk - 1))
        def _():
            acc_ref[...] += part

        @pl.when(k == nk - 1)
        def _():
            finish(refs, acc_ref[...] + part)

    out_shape = (M, N) if out_parts == 1 else (out_parts, M, N // out_parts)
    out = pl.pallas_call(
        body, name=name, grid=grid, in_specs=in_specs + [HBM] * ns, out_specs=[o_spec] + [HBM] * ns,
        out_shape=[jax.ShapeDtypeStruct(out_shape, out_dtype)] + [jax.ShapeDtypeStruct(x.shape, x.dtype) for x in send],
        scratch_shapes=([pltpu.VMEM((tm, tn), F32)] if nk > 1 else [])
        + ([pltpu.SemaphoreType.DMA((3 * ns,)), pltpu.SemaphoreType.DMA((3 * ns,))] if ns else []),
        compiler_params=pltpu.CompilerParams(
            dimension_semantics=("arbitrary",) * 3 if ns else ("parallel", "parallel", "arbitrary"),
            vmem_limit_bytes=VMEM_BIG),
    )(*operands, *send)
    return out if ns else out[0]


def _rmsnorm_fwd(x, g, *, name):
    T, D = x.shape
    tm = _pick(T, (512, 256, 128, 64, 32, 16))

    def body(x_ref, g_ref, h_ref):
        xv = x_ref[...]
        r = lax.rsqrt(jnp.mean(xv * xv, axis=-1, keepdims=True) + RMS_EPS)
        h_ref[...] = ((xv * r) * g_ref[...]).astype(BF16)

    return pl.pallas_call(
        body, name=name, grid=(T // tm,),
        in_specs=[pl.BlockSpec((tm, D), lambda i: (i, 0)), pl.BlockSpec((1, D), lambda i: (0, 0))],
        out_specs=pl.BlockSpec((tm, D), lambda i: (i, 0)),
        out_shape=jax.ShapeDtypeStruct((T, D), BF16),
    )(x, g.reshape(1, D))


def _rmsnorm_bwd(x, g, dh, dx_in, *, name):
    T, D = x.shape
    tm = _pick(T, (512, 256, 128, 64, 32, 16))

    def body(x_ref, g_ref, dh_ref, dxin_ref, dx_ref, dg_ref):
        @pl.when(pl.program_id(0) == 0)
        def _():
            dg_ref[...] = jnp.zeros_like(dg_ref)

        xv = x_ref[...]
        r = lax.rsqrt(jnp.mean(xv * xv, axis=-1, keepdims=True) + RMS_EPS)
        xh = xv * r
        dh_v = dh_ref[...]
        dxh = dh_v * g_ref[...]
        dx_ref[...] = dxin_ref[...] + r * (dxh - xh * jnp.mean(dxh * xh, axis=-1, keepdims=True))
        dg_ref[...] += jnp.sum(dh_v * xh, axis=0, keepdims=True)

    row = pl.BlockSpec((tm, D), lambda i: (i, 0))
    vec = pl.BlockSpec((1, D), lambda i: (0, 0))
    return pl.pallas_call(
        body, name=name, grid=(T // tm,),
        in_specs=[row, vec, row, row], out_specs=[row, vec],
        out_shape=[jax.ShapeDtypeStruct((T, D), F32), jax.ShapeDtypeStruct((1, D), F32)],
        compiler_params=pltpu.CompilerParams(dimension_semantics=("arbitrary",)),
    )(x, g.reshape(1, D), dh, dx_in)


def _loss_head(y, target, *, name):
    T, D = y.shape
    tm = _pick(T, (512, 256, 128, 64, 32, 16))

    def body(y_ref, t_ref, dy_ref, l_ref):
        @pl.when(pl.program_id(0) == 0)
        def _():
            l_ref[...] = jnp.zeros_like(l_ref)

        err = y_ref[...] - t_ref[...]
        dy_ref[...] = err * (1.0 / D)
        l_ref[...] += 0.5 * jnp.sum(jnp.mean(err * err, axis=-1, keepdims=True), axis=0, keepdims=True)

    row = pl.BlockSpec((tm, D), lambda i: (i, 0))
    return pl.pallas_call(
        body, name=name, grid=(T // tm,),
        in_specs=[row, row], out_specs=[row, pl.BlockSpec((1, 1), lambda i: (0, 0))],
        out_shape=[jax.ShapeDtypeStruct((T, D), F32), jax.ShapeDtypeStruct((1, 1), F32)],
        compiler_params=pltpu.CompilerParams(dimension_semantics=("arbitrary",)),
    )(y, target)


def _sc_mid_fwd(p3, conv_w, *, name):
    _, T, W = p3.shape
    K = conv_w.shape[0]
    cw = LANES

    def body(p_ref, w_ref, o_ref):
        z = p_ref[1] * p_ref[2]
        cv = sum(w_ref[i:i + 1, :] * _shift_down(z, K - 1 - i) for i in range(K))
        o_ref[...] = ((p_ref[0] * cv) * _silu(p_ref[3])).astype(BF16)

    return pl.pallas_call(
        body, name=name, grid=(W // cw,),
        in_specs=[pl.BlockSpec((4, T, cw), lambda j: (0, 0, j)), pl.BlockSpec((K, cw), lambda j: (0, j))],
        out_specs=pl.BlockSpec((T, cw), lambda j: (0, j)),
        out_shape=jax.ShapeDtypeStruct((T, W), BF16),
        compiler_params=pltpu.CompilerParams(dimension_semantics=("parallel",), vmem_limit_bytes=VMEM_BIG),
    )(p3, conv_w)


def _sc_mid_bwd(p3, conv_w, do, *, name):
    _, T, W = p3.shape
    K = conv_w.shape[0]
    cw = LANES

    def body(p_ref, w_ref, do_ref, dp_ref, dw_ref):
        b, c, u, gate = p_ref[0], p_ref[1], p_ref[2], p_ref[3]
        z = c * u
        zs = [_shift_down(z, K - 1 - i) for i in range(K)]
        cv = sum(w_ref[i:i + 1, :] * zs[i] for i in range(K))
        y = b * cv
        dov = do_ref[...]
        dy = dov * _silu(gate)
        dp_ref[3] = dov * y * _dsilu(gate)
        dp_ref[0] = dy * cv
        dcv = dy * b
        dz = sum(w_ref[i:i + 1, :] * _shift_up(dcv, K - 1 - i) for i in range(K))
        dp_ref[1] = dz * u
        dp_ref[2] = dz * c
        for i in range(K):
            dw_ref[i:i + 1, :] = jnp.sum(dcv * zs[i], axis=0, keepdims=True)

    return pl.pallas_call(
        body, name=name, grid=(W // cw,),
        in_specs=[pl.BlockSpec((4, T, cw), lambda j: (0, 0, j)), pl.BlockSpec((K, cw), lambda j: (0, j)),
                  pl.BlockSpec((T, cw), lambda j: (0, j))],
        out_specs=[pl.BlockSpec((4, T, cw), lambda j: (0, 0, j)), pl.BlockSpec((K, cw), lambda j: (0, j))],
        out_shape=[jax.ShapeDtypeStruct((4, T, W), F32), jax.ShapeDtypeStruct((K, W), F32)],
        compiler_params=pltpu.CompilerParams(dimension_semantics=("parallel",), vmem_limit_bytes=VMEM_BIG),
    )(p3, conv_w, do)


def _sc_layer_fwd(x, ng, w_in, conv_w, w_out, tag):
    h = _rmsnorm_fwd(x, ng, name=f"{tag}_norm")
    p3 = _matmul(h, w_in, mode="nn", b_parts=4, out_parts=4, name=f"{tag}_inproj")
    og = _sc_mid_fwd(p3, conv_w, name=f"{tag}_mid")
    x_new = _matmul(og, w_out, mode="nn", res=x, name=f"{tag}_outproj")
    return x_new, (h, p3, og)


def _sc_layer_bwd(dx, x, ng, w_in, conv_w, w_out, saved, tag):
    h, p3, og = saved
    d_wout = _matmul(og, dx, mode="tn", out_dtype=BF16, name=f"{tag}_dwout")
    dog = _matmul(dx, w_out, mode="nt", name=f"{tag}_dog")
    dp3, dconv = _sc_mid_bwd(p3, conv_w, dog, name=f"{tag}_midbwd")
    d_win = _matmul(h, dp3, mode="tn", b_parts=4, out_parts=4, out_dtype=BF16, name=f"{tag}_dwin")
    dh = _matmul(dp3, w_in, mode="nt", a_parts=4, b_parts=4, name=f"{tag}_dh")
    dx_prev, dng = _rmsnorm_bwd(x, ng, dh, dx, name=f"{tag}_normbwd")
    return dx_prev, dng, d_win, dconv, d_wout


SB_BQ = 256
SB_BK = 256
SB_ROWS = 512
SB_DEAD = -110.0


def _sb_half_mask():
    return lax.broadcasted_iota(jnp.int32, (1, LANES), 1) < SB_DH


def _sb_headnorm(x, g, lo):
    x2 = x * x
    s_lo = jnp.sum(jnp.where(lo, x2, 0.0), axis=-1, keepdims=True)
    s_hi = jnp.sum(jnp.where(lo, 0.0, x2), axis=-1, keepdims=True)
    r = lax.rsqrt(jnp.where(lo, s_lo, s_hi) * (1.0 / SB_DH) + RMS_EPS)
    xh = x * r
    return xh * g, xh, r


def _dot_x2_l(a_l, b_exact_bf16):
    his = [_bf(a) for a in a_l]
    mids = [_bf(a - h.astype(F32)) for a, h in zip(a_l, his)]
    f = lambda p: jnp.dot(p, b_exact_bf16, preferred_element_type=F32)
    return [x + y for x, y in zip([f(h) for h in his], [f(m) for m in mids])]


def _sb_stack(xb, lo):
    zero = jnp.zeros_like(xb)
    return jnp.concatenate([jnp.where(lo, xb, zero), jnp.where(lo, zero, xb)], axis=0)


def _sb_rel(bq, bk):
    row = lax.broadcasted_iota(jnp.int32, (2 * bq, bk), 0)
    col = lax.broadcasted_iota(jnp.int32, (2 * bq, bk), 1)
    return col - jnp.where(row >= bq, row - bq, row)


def _sb_tile(qm, kb, valid):
    z = lax.dot_general(qm, kb, (((1,), (1,)), ((), ())), preferred_element_type=F32)
    sp = _softplus(z)
    return z - sp, (-sp if valid is None else jnp.where(valid, -sp, 0.0))


def _sb_attn_fwd(p3, gq2, gk2, *, name, send=()):
    _, T, W = p3.shape
    bq, bk = min(SB_BQ, T), min(SB_BK, T)
    rows = min(SB_ROWS, T)
    scale = SB_DH ** -0.5
    ns = len(send)
    npair = W // LANES

    def body(*refs):
        p_ref, gq_ref, gk_ref = refs[:3]
        og_ref, o_ref, ls_ref, cnt_ref = refs[3 + ns:7 + ns]
        qn_ref, kn_ref, v_ref = refs[7 + 2 * ns:10 + 2 * ns]
        if ns:
            _halves_over_ici(refs[3:3 + ns], refs[7 + ns:7 + 2 * ns], refs[10 + 2 * ns], refs[11 + 2 * ns],
                             pl.program_id(0) == 0, pl.program_id(0) == npair - 1)
        lo = _sb_half_mask()

        def prologue(i, c):
            r0 = pl.multiple_of(i * rows, rows)
            sl = pl.ds(r0, rows)
            qn_ref[sl, :] = (_sb_headnorm(p_ref[0, sl, :], gq_ref[...], lo)[0] * scale).astype(BF16)
            kn_ref[sl, :] = _sb_headnorm(p_ref[1, sl, :], gk_ref[...], lo)[0].astype(BF16)
            v_ref[sl, :] = p_ref[2, sl, :].astype(BF16)
            return c

        lax.fori_loop(0, T // rows, prologue, 0)

        rel = _sb_rel(bq, bk)
        tri = (lax.broadcasted_iota(jnp.int32, (bk, bk), 0)
               > lax.broadcasted_iota(jnp.int32, (bk, bk), 1)).astype(BF16)

        def qblock(qi, c):
            q0 = pl.multiple_of(qi * bq, bq)
            qm = _sb_stack(qn_ref[pl.ds(q0, bq), :], lo)
            nkb = (q0 + bq - 1) // bk + 1

            def tiles(k0s, carry, valids):
                o_acc, a_carry = carry
                sc = [_sb_tile(qm, kn_ref[pl.ds(k0, bk), :], valid) for k0, valid in zip(k0s, valids)]
                later = _dot_x2_l([log1m for _, log1m in sc], tri)
                for (logsig, log1m), lat, k0, valid in zip(sc, later, k0s, valids):
                    wts = jnp.exp(logsig + (lat + a_carry))
                    if valid is not None:
                        wts = jnp.where(valid, wts, 0.0)
                    o_acc = o_acc + jnp.dot(_bf(wts), v_ref[pl.ds(k0, bk), :], preferred_element_type=F32)
                    a_carry = a_carry + jnp.sum(log1m, axis=-1, keepdims=True)
                return o_acc, a_carry

            blk0 = lambda j: pl.multiple_of(j * bk, bk)
            k_last = blk0(nkb - 1)
            o2, t2 = tiles([k_last, blk0(jnp.maximum(nkb - 2, 0))],
                           (jnp.zeros((2 * bq, LANES), F32), jnp.zeros((2 * bq, 1), F32)),
                           [rel < q0 - k_last, nkb >= 2])

            def alive(st):
                return jnp.logical_and(st[0] < nkb - 1, jnp.max(st[2]) > SB_DEAD)

            def back_one(st):
                return (st[0] + 1,) + tiles([blk0(nkb - 2 - st[0])], st[1:], [None])

            n_back, o2, t2 = lax.while_loop(alive, back_one, (jnp.int32(1), o2, t2))
            o = jnp.where(lo, o2[:bq], o2[bq:])
            o_ref[pl.ds(q0, bq), :] = o
            ls_ref[pl.ds(q0, bq), :] = jnp.where(lo, t2[:bq], t2[bq:])
            cnt_ref[qi] = jnp.full((8, LANES), jnp.minimum(n_back + 1, nkb).astype(F32))
            og_ref[pl.ds(q0, bq), :] = (o * _silu(p_ref[3, pl.ds(q0, bq), :])).astype(BF16)
            return c

        lax.fori_loop(0, T // bq, qblock, 0)

    colblk = pl.BlockSpec((T, LANES), lambda j: (0, j))
    vec = pl.BlockSpec((1, LANES), lambda j: (0, 0))
    return pl.pallas_call(
        body, name=name, grid=(npair,),
        in_specs=[pl.BlockSpec((4, T, LANES), lambda j: (0, 0, j)), vec, vec] + [HBM] * ns,
        out_specs=[colblk, colblk, colblk, pl.BlockSpec((None, T // bq, 8, LANES), lambda j: (j, 0, 0, 0))]
        + [HBM] * ns,
        out_shape=[jax.ShapeDtypeStruct((T, W), BF16), jax.ShapeDtypeStruct((T, W), F32),
                   jax.ShapeDtypeStruct((T, W), F32), jax.ShapeDtypeStruct((npair, T // bq, 8, LANES), F32)]
        + [jax.ShapeDtypeStruct((N_CHIPS,) + a.shape, a.dtype) for a in send],
        scratch_shapes=[pltpu.VMEM((T, LANES), BF16)] * 3
        + ([pltpu.SemaphoreType.DMA((3 * ns,)), pltpu.SemaphoreType.DMA((3 * ns,))] if ns else []),
        compiler_params=pltpu.CompilerParams(dimension_semantics=("arbitrary",), vmem_limit_bytes=VMEM_BIG),
    )(p3, gq2, gk2, *send)


def _sb_attn_bwd(p3, gq2, gk2, o, lsum, live, dog, *, name):
    _, T, W = p3.shape
    bq, bk = min(SB_BQ, T), min(SB_BK, T)
    rows = min(SB_ROWS, T)
    scale = SB_DH ** -0.5

    def body(p_ref, gq_ref, gk_ref, o_ref, ls_ref, cnt_ref, dog_ref, dp_ref, dgq_ref, dgk_ref,
             qn_ref, kn_ref, v_ref, do_ref):
        lo = _sb_half_mask()

        def prologue(i, c):
            r0 = pl.multiple_of(i * rows, rows)
            sl = pl.ds(r0, rows)
            qn_ref[sl, :] = (_sb_headnorm(p_ref[0, sl, :], gq_ref[...], lo)[0] * scale).astype(BF16)
            kn_ref[sl, :] = _sb_headnorm(p_ref[1, sl, :], gk_ref[...], lo)[0].astype(BF16)
            v_ref[sl, :] = p_ref[2, sl, :].astype(BF16)
            gate = p_ref[3, sl, :]
            dogv = dog_ref[sl, :]
            dp_ref[3, sl, :] = dogv * o_ref[sl, :] * _dsilu(gate)
            do_ref[sl, :] = (dogv * _silu(gate)).astype(BF16)
            zero = jnp.zeros((rows, LANES), F32)
            dp_ref[0, sl, :] = zero
            dp_ref[1, sl, :] = zero
            dp_ref[2, sl, :] = zero
            return c

        lax.fori_loop(0, T // rows, prologue, 0)

        rel = _sb_rel(bq, bk)
        rj = lax.broadcasted_iota(jnp.int32, (bk, bk), 0)
        cj = lax.broadcasted_iota(jnp.int32, (bk, bk), 1)
        upto = (rj <= cj).astype(BF16)
        before_m = (rj < cj).astype(BF16)

        def qblock(qi, c):
            q0 = pl.multiple_of(qi * bq, bq)
            qm = _sb_stack(qn_ref[pl.ds(q0, bq), :], lo)
            dom = _sb_stack(do_ref[pl.ds(q0, bq), :], lo)
            nkb = (q0 + bq - 1) // bk + 1
            blk0 = lambda j: pl.multiple_of(j * bk, bk)
            k_last = blk0(nkb - 1)

            lsb = ls_ref[pl.ds(q0, bq), :]
            total = jnp.concatenate([lsb[:, 0:1], lsb[:, SB_DH:SB_DH + 1]], axis=0)
            n_live = jnp.clip(jnp.max(cnt_ref[qi]).astype(jnp.int32), 1, nkb)
            k_first = nkb - n_live

            def tiles(k0s, carry, valids):
                dq_acc, a_pre, r_pre = carry
                kss = [pl.ds(k0, bk) for k0 in k0s]
                kbs = [kn_ref[ks, :] for ks in kss]
                sc = [_sb_tile(qm, kb, valid) for kb, valid in zip(kbs, valids)]
                dws = [lax.dot_general(dom, v_ref[ks, :], _NT, preferred_element_type=F32) for ks in kss]
                upto_l = _dot_x2_l([log1m for _, log1m in sc], upto)
                wts_l = []
                for (logsig, log1m), up, valid in zip(sc, upto_l, valids):
                    wts = jnp.exp(logsig + ((total - a_pre) - up))
                    wts_l.append(wts if valid is None else jnp.where(valid, wts, 0.0))
                    a_pre = a_pre + jnp.sum(log1m, axis=-1, keepdims=True)
                ee_l = [dw * wts for dw, wts in zip(dws, wts_l)]
                before_l = _dot_x2_l(ee_l, before_m)
                for (logsig, _), ks, kb, wts, ee, bef, valid in zip(sc, kss, kbs, wts_l, ee_l, before_l, valids):
                    beta = jnp.exp(logsig)
                    dz = ee * (1.0 - beta) - beta * (r_pre + bef)
                    if valid is not None:
                        dz = jnp.where(valid, dz, 0.0)
                    dzb = _bf(dz)
                    dq_acc = dq_acc + jnp.dot(dzb, kb, preferred_element_type=F32)
                    dp_ref[1, ks, :] += lax.dot_general(dzb, qm, _TN, preferred_element_type=F32)
                    dp_ref[2, ks, :] += lax.dot_general(_bf(wts), dom, _TN, preferred_element_type=F32)
                    r_pre = r_pre + jnp.sum(ee, axis=-1, keepdims=True)
                return dq_acc, a_pre, r_pre

            cr = (jnp.zeros((2 * bq, LANES), F32), jnp.zeros((2 * bq, 1), F32), jnp.zeros((2 * bq, 1), F32))
            n_before = jnp.maximum(n_live - 2, 0)
            cr = lax.fori_loop(0, n_before % 2, lambda t, cr: tiles([blk0(k_first)], cr, [None]), cr)
            k_pairs = k_first + n_before % 2
            cr = lax.fori_loop(0, n_before // 2,
                               lambda t, cr: tiles([blk0(k_pairs + 2 * t), blk0(k_pairs + 2 * t + 1)], cr,
                                                   [None, None]), cr)
            dq2, _, _ = tiles([blk0(jnp.maximum(nkb - 2, 0)), k_last], cr, [n_live >= 2, rel < q0 - k_last])
            dp_ref[0, pl.ds(q0, bq), :] = jnp.where(lo, dq2[:bq], dq2[bq:]) * scale
            return c

        lax.fori_loop(0, T // bq, qblock, 0)

        dgq_ref[...] = jnp.zeros_like(dgq_ref)
        dgk_ref[...] = jnp.zeros_like(dgk_ref)

        def epilogue(i, c):
            r0 = pl.multiple_of(i * rows, rows)
            sl = pl.ds(r0, rows)
            for part, g_ref, dg_ref in ((0, gq_ref, dgq_ref), (1, gk_ref, dgk_ref)):
                _, xh, r = _sb_headnorm(p_ref[part, sl, :], g_ref[...], lo)
                dn = dp_ref[part, sl, :]
                dxh = dn * g_ref[...]
                prod = dxh * xh
                m_lo = jnp.sum(jnp.where(lo, prod, 0.0), axis=-1, keepdims=True)
                m_hi = jnp.sum(jnp.where(lo, 0.0, prod), axis=-1, keepdims=True)
                m = jnp.where(lo, m_lo, m_hi) * (1.0 / SB_DH)
                dp_ref[part, sl, :] = r * (dxh - xh * m)
                dg_ref[...] += jnp.sum(dn * xh, axis=0, keepdims=True)
            return c

        lax.fori_loop(0, T // rows, epilogue, 0)

    colblk = pl.BlockSpec((T, LANES), lambda j: (0, j))
    vec = pl.BlockSpec((1, LANES), lambda j: (0, 0))
    part = pl.BlockSpec((4, T, LANES), lambda j: (0, 0, j))
    gvec = pl.BlockSpec((None, 1, LANES), lambda j: (j, 0, 0))
    npair = W // LANES
    return pl.pallas_call(
        body, name=name, grid=(npair,),
        in_specs=[part, vec, vec, colblk, colblk, pl.BlockSpec((None, T // bq, 8, LANES), lambda j: (j, 0, 0, 0)),
                  colblk],
        out_specs=[part, gvec, gvec],
        out_shape=[jax.ShapeDtypeStruct((4, T, W), F32), jax.ShapeDtypeStruct((npair, 1, LANES), F32),
                   jax.ShapeDtypeStruct((npair, 1, LANES), F32)],
        scratch_shapes=[pltpu.VMEM((T, LANES), BF16)] * 4,
        compiler_params=pltpu.CompilerParams(dimension_semantics=("parallel",), vmem_limit_bytes=VMEM_BIG),
    )(p3, gq2, gk2, o, lsum, live, dog)


_NN = (((1,), (0,)), ((), ()))
_NT = (((1,), (1,)), ((), ()))
_TN = (((0,), (0,)), ((), ()))
DN_TB = 512
DN_HEADS_FWD = 4
DN_HEADS_BWD = 2
DN_INV_EXACT_LEVELS = 2
DN_AB_COL = (DN_CONV_W + DN_V_W) // LANES


def _dn_conv(x, w_ref):
    k = w_ref.shape[0]
    return sum(w_ref[i:i + 1, :] * _shift_down(x, k - 1 - i) for i in range(k))


def _dn_prep_fwd(p, conv_w, *, name):
    T = p.shape[0]
    cw = conv_w.shape[1]
    n_qk = 2 * DN_QK_W // LANES

    def body(p_ref, w_ref, o_ref):
        s = _silu(_dn_conv(p_ref[...], w_ref))
        r = lax.rsqrt(jnp.sum(s * s, axis=-1, keepdims=True) + L2_EPS)
        o_ref[...] = jnp.where(pl.program_id(0) < n_qk, s * r, s)

    colblk = pl.BlockSpec((T, LANES), lambda j: (0, j))
    return pl.pallas_call(
        body, name=name, grid=(cw // LANES,),
        in_specs=[colblk, pl.BlockSpec((DN_CONV, LANES), lambda j: (0, j))],
        out_specs=colblk, out_shape=jax.ShapeDtypeStruct((T, cw), F32),
        compiler_params=pltpu.CompilerParams(dimension_semantics=("parallel",), vmem_limit_bytes=VMEM_BIG),
    )(p, conv_w)


def _dn_chunk_tri(rows, upper):
    r = lax.broadcasted_iota(jnp.int32, (rows, rows), 0)
    c = lax.broadcasted_iota(jnp.int32, (rows, rows), 1)
    same = (r // DN_CHUNK) == (c // DN_CHUNK)
    return jnp.logical_and(same, (c >= r) if upper else (c <= r)).astype(BF16)


def _dn_lane_rows(a_log, dt_bias):
    pad = lambda v: jnp.zeros((1, LANES), F32).at[0, :DN_HEADS].set(v)
    return pad(a_log), pad(dt_bias)


def _dn_ab_parts(blk, alog_row, dtb_row):
    lane = lax.broadcasted_iota(jnp.int32, (1, LANES), 1)
    is_a = lane < DN_HEADS
    is_b = jnp.logical_and(lane >= DN_HEADS, lane < 2 * DN_HEADS)
    a_arg = jnp.where(is_a, blk + dtb_row, 0.0)
    neg_exp = jnp.where(is_a, -jnp.exp(alog_row), 0.0)
    log_a = neg_exp * _softplus(a_arg)
    beta = jnp.where(is_b, _sigmoid(blk), 0.0)
    return is_a, is_b, a_arg, neg_exp, log_a, beta


def _dn_ab_fwd(p, alog_row, dtb_row, *, name):
    T = p.shape[0]
    rows = min(DN_TB, T)

    def body(p_ref, al_ref, dt_ref, o_ref):
        _, _, _, _, log_a, beta = _dn_ab_parts(p_ref[...], al_ref[...], dt_ref[...])
        hi, mid, lo_ = _split3(log_a)
        tri = _dn_chunk_tri(rows, upper=False)
        f = lambda q: jnp.dot(tri, q, preferred_element_type=F32)
        o_ref[...] = (f(hi) + f(mid) + f(lo_)) + beta

    blk = pl.BlockSpec((rows, LANES), lambda i: (i, DN_AB_COL))
    vec = pl.BlockSpec((1, LANES), lambda i: (0, 0))
    return pl.pallas_call(
        body, name=name, grid=(T // rows,), in_specs=[blk, vec, vec],
        out_specs=pl.BlockSpec((rows, LANES), lambda i: (i, 0)),
        out_shape=jax.ShapeDtypeStruct((T, LANES), F32),
        compiler_params=pltpu.CompilerParams(dimension_semantics=("parallel",)),
    )(p, alog_row, dtb_row)


def _hp_l(a_l, b_l, dims=_NN):
    sa = [_split3(a)[:2] for a in a_l]
    sb = [_split3(b)[:2] for b in b_l]
    f = lambda p, q: lax.dot_general(p, q, dims, preferred_element_type=F32)
    hh = [f(x[0], y[0]) for x, y in zip(sa, sb)]
    hm = [f(x[0], y[1]) for x, y in zip(sa, sb)]
    mh = [f(x[1], y[0]) for x, y in zip(sa, sb)]
    return [a + (b + c) for a, b, c in zip(hh, hm, mh)]


def _dn_local(qs, k, v, g, beta, nc, inv_l=None):
    c = DN_CHUNK
    cut = lambda x: [x[i * c:(i + 1) * c] for i in range(nc)]
    row = lax.broadcasted_iota(jnp.int32, (c, c), 0)
    col = lax.broadcasted_iota(jnp.int32, (c, c), 1)
    eye, lower, strict = row == col, row >= col, row > col
    rowid = lax.broadcasted_iota(jnp.int32, (c, 1), 0)
    eg = jnp.exp(g)
    kb = k * beta
    rhs_k = kb * eg
    g_l, k_l, kb_l, qs_l = cut(g), cut(k), cut(kb), cut(qs)
    g_row_l = [jnp.sum(jnp.where(eye, x, 0.0), axis=0, keepdims=True) for x in g_l]
    dec_l = [jnp.where(lower, jnp.exp(jnp.where(lower, x - y, 0.0)), 0.0) for x, y in zip(g_l, g_row_l)]
    kk_l = [_dot_nt(a, b) for a, b in zip(kb_l, k_l)]
    qk_l = [_dot_nt(a, b) for a, b in zip(qs_l, k_l)]
    low_l = [jnp.where(strict, a * d, 0.0) for a, d in zip(kk_l, dec_l)]
    if inv_l is None:
        pw_l = [-x for x in low_l]
        inv_l = [eye.astype(F32) + x for x in pw_l]
        plain = lambda a_l, b_l: [_dot(a, b) for a, b in zip(a_l, b_l)]
        for level in range(int(math.log2(c)) - 1):
            mul = _hp_l if level < DN_INV_EXACT_LEVELS else plain
            pw_l = mul(pw_l, pw_l)
            inv_l = [a + b for a, b in zip(inv_l, mul(inv_l, pw_l))]
    u_l = [_dot(a, b) for a, b in zip(inv_l, cut(v * beta))]
    w_l = [_dot(a, b) for a, b in zip(inv_l, cut(rhs_k))]
    aqk_l = [jnp.where(lower, a * d, 0.0) for a, d in zip(qk_l, dec_l)]
    g_last_l = [jnp.sum(jnp.where(rowid == c - 1, x, 0.0), axis=0, keepdims=True) for x in g_l]
    ekd_l = [jnp.exp(a - b) for a, b in zip(g_last_l, g_l)]
    kd_l = [a * b for a, b in zip(k_l, ekd_l)]
    qd_l = cut(qs * eg)
    kw_l = [_dot_tn(a, b) for a, b in zip(kd_l, w_l)]
    qp_l = [q - _dot(a, w) for q, a, w in zip(qd_l, aqk_l, w_l)]
    return dict(eye=eye, lower=lower, strict=strict, dec=dec_l, k=k_l, kb=kb_l, qs=qs_l, low=low_l, inv=inv_l,
                eg=cut(eg), rhs_k=cut(rhs_k), u=u_l, w=w_l, aqk=aqk_l, g_last=g_last_l, qd=qd_l,
                ekd=ekd_l, kd=kd_l, kw=kw_l, qp=qp_l)


def _dn_head_cols(gb_blk, head):
    lane = lax.broadcasted_iota(jnp.int32, (1, LANES), 1)
    g = jnp.sum(jnp.where(lane == head, gb_blk, 0.0), axis=-1, keepdims=True)
    beta = jnp.sum(jnp.where(lane == head + DN_HEADS, gb_blk, 0.0), axis=-1, keepdims=True)
    return g, beta


def _halves_over_ici(s_refs, o_refs, send_sems, recv_sems, first, last):
    x, y, c = _mesh_pos()
    me = 2 * x + y
    chips = _other_chips(x, y)
    pairs = [(a, k) for a in range(len(s_refs)) for k in range(3)]

    def copy(a, k, slot):
        px, py = chips[k]
        return pltpu.make_async_remote_copy(
            src_ref=s_refs[a].at[c], dst_ref=o_refs[a].at[slot, c], send_sem=send_sems.at[3 * a + k],
            recv_sem=recv_sems.at[3 * a + k], device_id=(px, py, c), device_id_type=MESH)

    @pl.when(first)
    def _():
        for a, k in pairs:
            copy(a, k, me).start()

    @pl.when(last)
    def _():
        for a, k in pairs:
            px, py = chips[k]
            copy(a, k, 2 * px + py).wait_recv()
        for a, k in pairs:
            copy(a, k, me).wait_send()


def _dn_delta_fwd(qkv, gb, p, o_gain, *, name, send=()):
    T = qkv.shape[0]
    tb = min(DN_TB, T)
    nb, nc = T // tb, tb // DN_CHUNK
    H = DN_HEADS
    qscale = DN_DK ** -0.5
    ns = len(send)
    hp = DN_HEADS_FWD

    def body(*refs):
        q_ref, k_ref, v_ref, gb_ref, gate_ref, gain_ref = refs[:6]
        o_ref, og_ref, st_ref, inv_ref = refs[6 + ns:10 + ns]
        s_ref = refs[10 + 2 * ns]
        pair, blk = pl.program_id(0), pl.program_id(1)
        if ns:
            _halves_over_ici(refs[6:6 + ns], refs[10 + ns:10 + 2 * ns], refs[11 + 2 * ns], refs[12 + 2 * ns],
                             jnp.logical_and(pair == 0, blk == 0),
                             jnp.logical_and(pair == H // hp - 1, blk == nb - 1))

        @pl.when(blk == 0)
        def _():
            s_ref[...] = jnp.zeros_like(s_ref)

        gbv = gb_ref[...]
        ts, ku, op = [], [], []
        for e in range(hp):
            qk_e, v_e = slice(e * DN_DK, (e + 1) * DN_DK), slice(e * DN_DV, (e + 1) * DN_DV)
            g, beta = _dn_head_cols(gbv, hp * pair + e)
            t = _dn_local(q_ref[:, qk_e] * qscale, k_ref[:, qk_e], v_ref[:, v_e], g, beta, nc)
            ts.append(t)
            for i in range(nc):
                inv_ref[e, i] = t["inv"][i]
            ku.append([_dot_tn(a, b) for a, b in zip(t["kd"], t["u"])])
            op.append([_dot(a, b) for a, b in zip(t["aqk"], t["u"])])
        s32 = [s_ref[e] for e in range(hp)]
        s_l = [[] for _ in range(hp)]
        for i in range(nc):
            sb = [_bf(x) for x in s32]
            for e in range(hp):
                st_ref[e, i] = sb[e]
                s_l[e].append(sb[e])
            prod = [_dot(ts[e]["kw"][i], sb[e]) for e in range(hp)]
            s32 = [s32[e] * jnp.exp(ts[e]["g_last"][i]) - prod[e] + ku[e][i] for e in range(hp)]
        for e in range(hp):
            s_ref[e] = s32[e]
        o = jnp.concatenate(
            [jnp.concatenate([_dot(qp, sb) + x for qp, sb, x in zip(ts[e]["qp"], s_l[e], op[e])], axis=0)
             for e in range(hp)], axis=1)
        o_ref[...] = o
        gain = gain_ref[...]
        for e in range(hp):
            v_e = slice(e * DN_DV, (e + 1) * DN_DV)
            oe = o[:, v_e]
            r = lax.rsqrt(jnp.mean(oe * oe, axis=-1, keepdims=True) + RMS_EPS)
            og_ref[:, v_e] = (((oe * r) * gain) * _silu(gate_ref[:, v_e])).astype(BF16)

    qk = lambda col0: pl.BlockSpec((tb, hp * DN_DK), lambda h, i: (i, col0 // (hp * DN_DK) + h))
    vblk = lambda col0: pl.BlockSpec((tb, hp * DN_DV), lambda h, i: (i, col0 // (hp * DN_DV) + h))
    return pl.pallas_call(
        body, name=name, grid=(H // hp, nb),
        in_specs=[qk(0), qk(DN_QK_W), vblk(2 * DN_QK_W), pl.BlockSpec((tb, LANES), lambda h, i: (i, 0)),
                  vblk(DN_CONV_W), pl.BlockSpec((1, DN_DV), lambda h, i: (0, 0))] + [HBM] * ns,
        out_specs=[vblk(0), vblk(0), pl.BlockSpec((hp, nc, DN_DK, DN_DV), lambda h, i: (h, i, 0, 0)),
                   pl.BlockSpec((hp, nc, DN_CHUNK, DN_CHUNK), lambda h, i: (h, i, 0, 0))] + [HBM] * ns,
        out_shape=[jax.ShapeDtypeStruct((T, DN_V_W), F32), jax.ShapeDtypeStruct((T, DN_V_W), BF16),
                   jax.ShapeDtypeStruct((H, T // DN_CHUNK, DN_DK, DN_DV), BF16),
                   jax.ShapeDtypeStruct((H, T // DN_CHUNK, DN_CHUNK, DN_CHUNK), F32)]
        + [jax.ShapeDtypeStruct((N_CHIPS,) + a.shape, a.dtype) for a in send],
        scratch_shapes=[pltpu.VMEM((hp, DN_DK, DN_DV), F32)]
        + ([pltpu.SemaphoreType.DMA((3 * ns,)), pltpu.SemaphoreType.DMA((3 * ns,))] if ns else []),
        compiler_params=pltpu.CompilerParams(dimension_semantics=("arbitrary", "arbitrary")),
    )(qkv, qkv, qkv, gb, p, o_gain, *send)


def _blocks_over_ici(p_refs, o_refs, send_sems, recv_sems, first, last):
    x, y, c = _mesh_pos()
    me = 2 * x + y
    chips = _other_chips(x, y)
    pairs = [(a, k) for a in range(len(p_refs)) for k in range(3)]

    def copy(a, k, slot):
        px, py = chips[k]
        return pltpu.make_async_remote_copy(
            src_ref=p_refs[a].at[2 * px + py], dst_ref=o_refs[a].at[slot], send_sem=send_sems.at[3 * a + k],
            recv_sem=recv_sems.at[3 * a + k], device_id=(px, py, c), device_id_type=MESH)

    @pl.when(first)
    def _():
        for a, k in pairs:
            copy(a, k, me).start()

    @pl.when(last)
    def _():
        for a, k in pairs:
            px, py = chips[k]
            copy(a, k, 2 * px + py).wait_recv()
        for a, k in pairs:
            copy(a, k, me).wait_send()


def _dn_delta_bwd(qkv, gb, p, o_gain, o, states, invs, dog, *, name, send=()):
    T = qkv.shape[0]
    tb = min(DN_TB, T)
    nb, nc = T // tb, tb // DN_CHUNK
    H = DN_HEADS
    qscale = DN_DK ** -0.5
    ns = len(send)
    hp = DN_HEADS_BWD

    def body(*refs):
        q_ref, k_ref, v_ref, gb_ref, gate_ref, gain_ref, o_ref, st_ref, inv_ref, dog_ref = refs[:10]
        dq_ref, dk_ref, dv_ref, dgate_ref, dgb_ref, dgain_ref = refs[10 + ns:16 + ns]
        ds_ref = refs[16 + 2 * ns]
        pair, blk = pl.program_id(0), pl.program_id(1)
        first = jnp.logical_and(pair == 0, blk == 0)
        if ns:
            _blocks_over_ici(refs[10:10 + ns], refs[16 + ns:16 + 2 * ns], refs[17 + 2 * ns], refs[18 + 2 * ns],
                             first, jnp.logical_and(pair == H // hp - 1, blk == nb - 1))

        @pl.when(blk == 0)
        def _():
            ds_ref[...] = jnp.zeros_like(ds_ref)

        @pl.when(first)
        def _():
            dgain_ref[...] = jnp.zeros_like(dgain_ref)

        lane = lax.broadcasted_iota(jnp.int32, (1, LANES), 1)
        c = DN_CHUNK
        cut = lambda x: [x[i * c:(i + 1) * c] for i in range(nc)]
        cat = lambda xs: jnp.concatenate(xs, axis=0)
        rsum = lambda x: jnp.sum(x, axis=-1, keepdims=True)
        gbv, gain = gb_ref[...], gain_ref[...]

        def before_chain(e):
            qk_e, v_e = slice(e * DN_DK, (e + 1) * DN_DK), slice(e * DN_DV, (e + 1) * DN_DV)
            g, beta = _dn_head_cols(gbv, hp * pair + e)
            ov, gate, dogv = o_ref[:, v_e], gate_ref[:, v_e], dog_ref[:, v_e]
            r = lax.rsqrt(jnp.mean(ov * ov, axis=-1, keepdims=True) + RMS_EPS)
            oh = ov * r
            dnrm = dogv * _silu(gate)
            dgate_ref[:, v_e] = dogv * (oh * gain) * _dsilu(gate)
            doh = dnrm * gain
            do_l = cut(r * (doh - oh * jnp.mean(doh * oh, axis=-1, keepdims=True)))
            dgain_ref[...] += jnp.sum(dnrm * oh, axis=0, keepdims=True)
            k, v = k_ref[:, qk_e], v_ref[:, v_e]
            t = _dn_local(q_ref[:, qk_e] * qscale, k, v, g, beta, nc, [inv_ref[e, i] for i in range(nc)])
            s_l = [st_ref[e, i] for i in range(nc)]
            vn_l = [u - _dot(w, sb) for u, w, sb in zip(t["u"], t["w"], s_l)]
            return dict(
                t=t, beta=beta, v=v, s=s_l, vn=vn_l, egl=[jnp.exp(x) for x in t["g_last"]],
                dqd=[_dot_nt(a, sb) for a, sb in zip(do_l, s_l)], daqk=[_dot_nt(a, b) for a, b in zip(do_l, vn_l)],
                aqk_do=[_dot_tn(a, b) for a, b in zip(t["aqk"], do_l)],
                qp_do=[_dot_tn(a, b) for a, b in zip(t["qp"], do_l)])

        hs = [before_chain(e) for e in range(hp)]
        ds = [ds_ref[e] for e in range(hp)]
        ds_l = [[None] * nc for _ in range(hp)]
        for i in reversed(range(nc)):
            for e in range(hp):
                ds_l[e][i] = ds[e]
            prod = [_dot_tn(hs[e]["t"]["kw"][i], ds[e]) for e in range(hp)]
            ds = [ds[e] * hs[e]["egl"][i] - prod[e] + hs[e]["qp_do"][i] for e in range(hp)]
        for e in range(hp):
            ds_ref[e] = ds[e]

        def after_chain(e):
            hd, t = hs[e], hs[e]["t"]
            lower, strict, eye = t["lower"], t["strict"], t["eye"]
            s_l, vn_l, dqd_l, daqk_l, egl_l, beta, v = (hd["s"], hd["vn"], hd["dqd"], hd["daqk"], hd["egl"],
                                                         hd["beta"], hd["v"])
            dvn_l = [a + _dot(kd, d) for a, kd, d in zip(hd["aqk_do"], t["kd"], ds_l[e])]
            dkd_l = [_dot_nt(a, d) for a, d in zip(vn_l, ds_l[e])]
            dgl_l = [jnp.sum(rsum(d * sb.astype(F32)), axis=0, keepdims=True) * x
                     for d, sb, x in zip(ds_l[e], s_l, egl_l)]
            dw_l = [-_dot_nt(a, sb) for a, sb in zip(dvn_l, s_l)]
            dbv_l = [_dot_tn(a, b) for a, b in zip(t["inv"], dvn_l)]
            dbk_l = [_dot_tn(a, b) for a, b in zip(t["inv"], dw_l)]
            dlow_l = [-(_dot_nt(a, b) + _dot_nt(x, y)) for a, b, x, y in zip(dbv_l, t["u"], dbk_l, t["w"])]
            m_l = [jnp.where(strict, a * d, 0.0) for a, d in zip(dlow_l, t["dec"])]
            nmat_l = [jnp.where(lower, a * d, 0.0) for a, d in zip(daqk_l, t["dec"])]
            dkb_l = [_dot(m, kk) + b * x for m, kk, b, x in zip(m_l, t["k"], dbk_l, t["eg"])]
            dqs_l = [_dot(n, kk) + a * x for n, kk, a, x in zip(nmat_l, t["k"], dqd_l, t["eg"])]
            dk1_l = [_dot_tn(m, kb) for m, kb in zip(m_l, t["kb"])]
            dk2_l = [_dot_tn(n, q) for n, q in zip(nmat_l, t["qs"])]
            beta_l, v_l = cut(beta), cut(v)
            rowid = lax.broadcasted_iota(jnp.int32, (c, 1), 0)
            dk_l, dg_l, dbeta_l = [], [], []
            for i in range(nc):
                dk_l.append(dk1_l[i] + dk2_l[i] + dkd_l[i] * t["ekd"][i] + dkb_l[i] * beta_l[i])
                gmat = jnp.where(strict, dlow_l[i] * t["low"][i], 0.0) + daqk_l[i] * t["aqk"][i]
                s_kd = rsum(dkd_l[i] * t["kd"][i])
                dg = (rsum(gmat) + rsum(dqd_l[i] * t["qd"][i]) - s_kd + rsum(dbk_l[i] * t["rhs_k"][i]))
                dg_row = -jnp.sum(gmat, axis=0, keepdims=True)
                dg = dg + rsum(jnp.where(eye, dg_row, 0.0))
                dgl = dgl_l[i] + jnp.sum(s_kd, axis=0, keepdims=True)
                dg_l.append(dg + jnp.where(rowid == c - 1, dgl, 0.0))
                dbeta_l.append(rsum(dbv_l[i] * v_l[i]) + rsum(dkb_l[i] * t["k"][i]))
            head = hp * pair + e
            dgb = (jnp.where(lane == head, cat(dg_l), 0.0) + jnp.where(lane == head + DN_HEADS, cat(dbeta_l), 0.0))
            return cat(dqs_l) * qscale, cat(dk_l), cat(dbv_l) * beta, dgb

        for e in range(hp):
            dq, dk, dv, dgb = after_chain(e)
            dq_ref[:, e * DN_DK:(e + 1) * DN_DK] = dq
            dk_ref[:, e * DN_DK:(e + 1) * DN_DK] = dk
            dv_ref[:, e * DN_DV:(e + 1) * DN_DV] = dv
            dgb_ref[e] = dgb

    rev = lambda i: nb - 1 - i
    qk = lambda col0: pl.BlockSpec((tb, hp * DN_DK), lambda h, i: (rev(i), col0 // (hp * DN_DK) + h))
    vblk = lambda col0: pl.BlockSpec((tb, hp * DN_DV), lambda h, i: (rev(i), col0 // (hp * DN_DV) + h))
    gain_spec = pl.BlockSpec((1, DN_DV), lambda h, i: (0, 0))
    return pl.pallas_call(
        body, name=name, grid=(H // hp, nb),
        in_specs=[qk(0), qk(DN_QK_W), vblk(2 * DN_QK_W), pl.BlockSpec((tb, LANES), lambda h, i: (rev(i), 0)),
                  vblk(DN_CONV_W), gain_spec, vblk(0),
                  pl.BlockSpec((hp, nc, DN_DK, DN_DV), lambda h, i: (h, rev(i), 0, 0)),
                  pl.BlockSpec((hp, nc, DN_CHUNK, DN_CHUNK), lambda h, i: (h, rev(i), 0, 0)), vblk(0)] + [HBM] * ns,
        out_specs=[qk(0), qk(0), vblk(0), vblk(DN_CONV_W),
                   pl.BlockSpec((hp, tb, LANES), lambda h, i: (h, rev(i), 0)), gain_spec] + [HBM] * ns,
        out_shape=[jax.ShapeDtypeStruct((T, DN_QK_W), F32), jax.ShapeDtypeStruct((T, DN_QK_W), F32),
                   jax.ShapeDtypeStruct((T, DN_V_W), F32), jax.ShapeDtypeStruct((T, DN_IN_PAD), F32),
                   jax.ShapeDtypeStruct((H, T, LANES), F32), jax.ShapeDtypeStruct((1, DN_DV), F32)]
        + [jax.ShapeDtypeStruct(a.shape, a.dtype) for a in send],
        scratch_shapes=[pltpu.VMEM((hp, DN_DK, DN_DV), F32)]
        + ([pltpu.SemaphoreType.DMA((3 * ns,)), pltpu.SemaphoreType.DMA((3 * ns,))] if ns else []),
        compiler_params=pltpu.CompilerParams(dimension_semantics=("arbitrary", "arbitrary")),
    )(qkv, qkv, qkv, gb, p, o_gain, o, states, invs, dog, *send)


def _dn_conv_bwd(p, conv_w, d, dp, *, first, normed, name):
    T, width = d.shape

    def body(p_ref, w_ref, d_ref, dp_in, dp_ref, dw_ref):
        del dp_in
        x = p_ref[...]
        ksz = w_ref.shape[0]
        xs = [_shift_down(x, ksz - 1 - i) for i in range(ksz)]
        xc = sum(w_ref[i:i + 1, :] * xs[i] for i in range(ksz))
        ds = d_ref[...]
        if normed:
            s = _silu(xc)
            r = lax.rsqrt(jnp.sum(s * s, axis=-1, keepdims=True) + L2_EPS)
            y = s * r
            ds = r * (ds - y * jnp.sum(ds * y, axis=-1, keepdims=True))
        dxc = ds * _dsilu(xc)
        dp_ref[...] = sum(w_ref[i:i + 1, :] * _shift_up(dxc, ksz - 1 - i) for i in range(ksz))
        for i in range(ksz):
            dw_ref[i:i + 1, :] = jnp.sum(dxc * xs[i], axis=0, keepdims=True)

    shifted = pl.BlockSpec((T, LANES), lambda j: (0, first + j))
    return pl.pallas_call(
        body, name=name, grid=(width // LANES,),
        in_specs=[shifted, pl.BlockSpec((DN_CONV, LANES), lambda j: (0, first + j)),
                  pl.BlockSpec((T, LANES), lambda j: (0, j)), pl.BlockSpec(memory_space=pl.ANY)],
        out_specs=[shifted, pl.BlockSpec((DN_CONV, LANES), lambda j: (0, j))],
        out_shape=[jax.ShapeDtypeStruct(dp.shape, F32), jax.ShapeDtypeStruct((DN_CONV, width), F32)],
        input_output_aliases={3: 0},
        compiler_params=pltpu.CompilerParams(dimension_semantics=("parallel",), vmem_limit_bytes=VMEM_BIG),
    )(p, conv_w, d, dp)


def _dn_ab_bwd(p, alog_row, dtb_row, dgb, dp, *, name):
    T = p.shape[0]
    rows = min(DN_TB, T)
    H = DN_HEADS

    def body(p_ref, al_ref, dt_ref, dgb_ref, dp_in, dp_ref, dal_ref, ddt_ref):
        del dp_in

        @pl.when(pl.program_id(0) == 0)
        def _():
            dal_ref[...] = jnp.zeros_like(dal_ref)
            ddt_ref[...] = jnp.zeros_like(ddt_ref)

        blk = p_ref[...]
        is_a, is_b, a_arg, neg_exp, log_a, beta = _dn_ab_parts(blk, al_ref[...], dt_ref[...])
        d = dgb_ref[0]
        for hh in range(1, H):
            d = d + dgb_ref[hh]
        hi, mid, lo_ = _split3(jnp.where(is_a, d, 0.0))
        tri = _dn_chunk_tri(rows, upper=True)
        f = lambda q: jnp.dot(tri, q, preferred_element_type=F32)
        dlog_a = f(hi) + f(mid) + f(lo_)
        da_in = dlog_a * neg_exp * _sigmoid(a_arg)
        db_in = jnp.where(is_b, d, 0.0) * beta * (1.0 - beta)
        dp_ref[...] = jnp.where(is_a, da_in, 0.0) + db_in
        dal_ref[...] += jnp.sum(dlog_a * log_a, axis=0, keepdims=True)
        ddt_ref[...] += jnp.sum(jnp.where(is_a, da_in, 0.0), axis=0, keepdims=True)

    blk = pl.BlockSpec((rows, LANES), lambda i: (i, DN_AB_COL))
    vec = pl.BlockSpec((1, LANES), lambda i: (0, 0))
    return pl.pallas_call(
        body, name=name, grid=(T // rows,),
        in_specs=[blk, vec, vec, pl.BlockSpec((H, rows, LANES), lambda i: (0, i, 0)),
                  pl.BlockSpec(memory_space=pl.ANY)],
        out_specs=[blk, vec, vec],
        out_shape=[jax.ShapeDtypeStruct(dp.shape, F32), jax.ShapeDtypeStruct((1, LANES), F32),
                   jax.ShapeDtypeStruct((1, LANES), F32)],
        input_output_aliases={4: 0},
        compiler_params=pltpu.CompilerParams(dimension_semantics=("arbitrary",)),
    )(p, alog_row, dtb_row, dgb, dp)


def _dn_layer_fwd(x, ng, w_in, conv_w, a_log, dt_bias, o_gain, w_out, tag, send=()):
    alog_row, dtb_row = _dn_lane_rows(a_log, dt_bias)
    gain = o_gain.reshape(1, DN_DV)
    h = _rmsnorm_fwd(x, ng, name=f"{tag}_norm")
    p = _matmul(h, w_in, mode="nn", name=f"{tag}_inproj")
    qkv = _dn_prep_fwd(p, conv_w, name=f"{tag}_prep")
    gb = _dn_ab_fwd(p, alog_row, dtb_row, name=f"{tag}_ab")
    o, og, states, invs, *landed = _dn_delta_fwd(qkv, gb, p, gain, name=f"{tag}_delta", send=send)
    x_new = _matmul(og, w_out, mode="nn", res=x, name=f"{tag}_outproj")
    return x_new, (h, p, qkv, gb, o, og, states, invs), landed


def _dn_layer_bwd(dx, x, ng, w_in, conv_w, a_log, dt_bias, o_gain, w_out, saved, tag, send=(), send_dwin=(),
                  chip_sums=None):
    h, p, qkv, gb, o, og, states, invs = saved
    alog_row, dtb_row = _dn_lane_rows(a_log, dt_bias)
    gain = o_gain.reshape(1, DN_DV)
    d_wout = _matmul(og, dx, mode="tn", out_dtype=BF16, name=f"{tag}_dwout")
    if chip_sums is not None:
        d_wout, = chip_sums([_cut2(_by_rows(d_wout))], f"{tag}wout")
        send = list(send) + [d_wout]
    dog = _matmul(dx, w_out, mode="nt", name=f"{tag}_dog")
    dq, dk, dv, dp, dgb, dgain, *landed = _dn_delta_bwd(qkv, gb, p, gain, o, states, invs, dog,
                                                        name=f"{tag}_deltabwd", send=send)
    n_qk = DN_QK_W // LANES
    dp, dconv_q = _dn_conv_bwd(p, conv_w, dq, dp, first=0, normed=True, name=f"{tag}_convbwd_q")
    dp, dconv_k = _dn_conv_bwd(p, conv_w, dk, dp, first=n_qk, normed=True, name=f"{tag}_convbwd_k")
    dp, dconv_v = _dn_conv_bwd(p, conv_w, dv, dp, first=2 * n_qk, normed=False, name=f"{tag}_convbwd_v")
    dconv = jnp.concatenate([dconv_q, dconv_k, dconv_v], axis=1)
    dp, dal, ddt = _dn_ab_bwd(p, alog_row, dtb_row, dgb, dp, name=f"{tag}_abbwd")
    d_win = _matmul(h, dp, mode="tn", out_dtype=BF16, name=f"{tag}_dwin", send=send_dwin)
    if send_dwin:
        d_win, *landed_dwin = d_win
        landed = landed + landed_dwin
    if chip_sums is not None:
        d_win, = chip_sums([_cut2(_by_cols(d_win))], f"{tag}win")
        dh, landed_win = _matmul(dp, w_in, mode="nt", name=f"{tag}_dh", send=[d_win])
        landed = landed + [landed_win]
    else:
        dh = _matmul(dp, w_in, mode="nt", name=f"{tag}_dh")
    dx_prev, dng = _rmsnorm_bwd(x, ng, dh, dx, name=f"{tag}_normbwd")
    return dx_prev, dng, d_win, dconv, dal[0, :DN_HEADS], ddt[0, :DN_HEADS], dgain[0], d_wout, landed


def _by_cols(dw):
    return _split(dw[:, :DN_IN], 1)


def _by_rows(dw):
    return dw.reshape(N_CHIPS, -1, dw.shape[-1])


def _cut2(g4):
    return g4.reshape(N_CHIPS, 2, -1, g4.shape[-1])


def _sb_gains(g):
    return jnp.concatenate([g, g]).reshape(1, LANES)


def _sb_layer_fwd(x, ng, w_in, gq, gk, w_out, tag, send=()):
    h = _rmsnorm_fwd(x, ng, name=f"{tag}_norm")
    p3 = _matmul(h, w_in, mode="nn", b_parts=4, out_parts=4, name=f"{tag}_inproj")
    og, o, lsum, live, *landed = _sb_attn_fwd(p3, _sb_gains(gq), _sb_gains(gk), name=f"{tag}_attn", send=send)
    x_new = _matmul(og, w_out, mode="nn", res=x, name=f"{tag}_outproj")
    return x_new, (h, p3, og, o, lsum, live), landed


def _sb_layer_bwd(dx, x, ng, w_in, gq, gk, w_out, saved, tag):
    h, p3, og, o, lsum, live = saved
    d_wout = _matmul(og, dx, mode="tn", out_dtype=BF16, name=f"{tag}_dwout")
    dog = _matmul(dx, w_out, mode="nt", name=f"{tag}_dog")
    dp3, dgq, dgk = _sb_attn_bwd(p3, _sb_gains(gq), _sb_gains(gk), o, lsum, live, dog, name=f"{tag}_attnbwd")
    fold = lambda d: jnp.sum(d.reshape(-1, SB_DH), axis=0)
    d_win = _matmul(h, dp3, mode="tn", b_parts=4, out_parts=4, out_dtype=BF16, name=f"{tag}_dwin")
    dh = _matmul(dp3, w_in, mode="nt", a_parts=4, b_parts=4, name=f"{tag}_dh")
    dx_prev, dng = _rmsnorm_bwd(x, ng, dh, dx, name=f"{tag}_normbwd")
    return dx_prev, dng, d_win, fold(dgq), fold(dgk), d_wout


N_CHIPS = 4
HBM = pl.BlockSpec(memory_space=pl.ANY)


def _mesh_pos():
    return lax.axis_index("x"), lax.axis_index("y"), lax.axis_index("c")


def _other_chips(x, y):
    return [(1 - x, y), (x, 1 - y), (1 - x, 1 - y)]


def _chip_exchange(srcs, *, send_slot_is_dest, copy_own, name):
    n = len(srcs)

    def body(*refs):
        src_refs, out_refs = refs[:n], refs[n:2 * n]
        send_sems, recv_sems, local_sems = refs[2 * n:]
        x, y, c = _mesh_pos()
        me = 2 * x + y
        chips = _other_chips(x, y)
        local = []
        for a in range(n):
            if not copy_own[a]:
                continue
            own = src_refs[a].at[me] if send_slot_is_dest else src_refs[a]
            local.append(pltpu.make_async_copy(own, out_refs[a].at[me], local_sems.at[a]))
        for cp in local:
            cp.start()

        def copy(a, k, landing_slot):
            px, py = chips[k]
            src = src_refs[a].at[2 * px + py] if send_slot_is_dest else src_refs[a]
            return pltpu.make_async_remote_copy(
                src_ref=src, dst_ref=out_refs[a].at[landing_slot],
                send_sem=send_sems.at[a * 3 + k], recv_sem=recv_sems.at[a * 3 + k],
                device_id=(px, py, c), device_id_type=MESH)

        sends = [copy(a, k, me) for a in range(n) for k in range(3)]
        for cp in sends:
            cp.start()
        for a in range(n):
            for k in range(3):
                px, py = chips[k]
                copy(a, k, 2 * px + py).wait_recv()
        for cp in sends:
            cp.wait_send()
        for cp in local:
            cp.wait()

    outs = []
    for s in srcs:
        shape = s.shape if send_slot_is_dest else (N_CHIPS,) + s.shape
        outs.append(jax.ShapeDtypeStruct(shape, s.dtype))
    return pl.pallas_call(
        body, name=name, in_specs=[HBM] * n, out_specs=[HBM] * n, out_shape=outs,
        scratch_shapes=[pltpu.SemaphoreType.DMA((3 * n,)), pltpu.SemaphoreType.DMA((3 * n,)),
                        pltpu.SemaphoreType.DMA((n,))],
    )(*srcs)


def _sibling_exchange(srcs, *, name):
    n = len(srcs)

    def body(*refs):
        src_refs, out_refs = refs[:n], refs[n:2 * n]
        send_sems, recv_sems = refs[2 * n:]
        x, y, c = _mesh_pos()
        copies = [pltpu.make_async_remote_copy(
            src_ref=src_refs[a], dst_ref=out_refs[a], send_sem=send_sems.at[a], recv_sem=recv_sems.at[a],
            device_id=(x, y, 1 - c), device_id_type=MESH) for a in range(n)]
        for cp in copies:
            cp.start()
        for cp in copies:
            cp.wait()

    return pl.pallas_call(
        body, name=name, in_specs=[HBM] * n, out_specs=[HBM] * n,
        out_shape=[jax.ShapeDtypeStruct(s.shape, s.dtype) for s in srcs],
        scratch_shapes=[pltpu.SemaphoreType.DMA((n,)), pltpu.SemaphoreType.DMA((n,))],
    )(*srcs)


def _gather_halves(shards, small, *, name):
    n = len(shards)

    def body(*refs):
        s_refs, small_ref = refs[:n], refs[n]
        o_refs, osmall_ref = refs[n + 1:2 * n + 1], refs[2 * n + 1]
        send_sems, recv_sems, local_sems = refs[2 * n + 2:]
        x, y, c = _mesh_pos()
        me = 2 * x + y
        chips = _other_chips(x, y)
        local = [pltpu.make_async_copy(small_ref, osmall_ref.at[me], local_sems.at[0])]
        for cp in local:
            cp.start()

        def over_ici(a, k, slot):
            px, py = chips[k]
            return pltpu.make_async_remote_copy(
                src_ref=s_refs[a].at[c], dst_ref=o_refs[a].at[slot, c], send_sem=send_sems.at[3 * a + k],
                recv_sem=recv_sems.at[3 * a + k], device_id=(px, py, c), device_id_type=MESH)

        def small_copy(k, slot):
            px, py = chips[k]
            return pltpu.make_async_remote_copy(
                src_ref=small_ref, dst_ref=osmall_ref.at[slot], send_sem=send_sems.at[3 * n + k],
                recv_sem=recv_sems.at[3 * n + k], device_id=(px, py, c), device_id_type=MESH)

        def to_sibling(a, k, half):
            px, py = chips[k]
            blk = o_refs[a].at[2 * px + py, half]
            return pltpu.make_async_remote_copy(
                src_ref=blk, dst_ref=blk, send_sem=send_sems.at[3 * n + 3 + 3 * a + k],
                recv_sem=recv_sems.at[3 * n + 3 + 3 * a + k], device_id=(x, y, 1 - c), device_id_type=MESH)

        sends = [over_ici(a, k, me) for a in range(n) for k in range(3)] + [small_copy(k, me) for k in range(3)]
        for cp in sends:
            cp.start()
        passed = []
        for a in range(n):
            for k in range(3):
                px, py = chips[k]
                over_ici(a, k, 2 * px + py).wait_recv()
                passed.append(to_sibling(a, k, c))
                passed[-1].start()
        for k in range(3):
            px, py = chips[k]
            small_copy(k, 2 * px + py).wait_recv()
        for a in range(n):
            for k in range(3):
                to_sibling(a, k, 1 - c).wait_recv()
        for cp in sends + passed:
            cp.wait_send()
        for cp in local:
            cp.wait()

    nsem = 6 * n + 3
    return pl.pallas_call(
        body, name=name, in_specs=[HBM] * (n + 1), out_specs=[HBM] * (n + 1),
        out_shape=[jax.ShapeDtypeStruct((N_CHIPS,) + s.shape, s.dtype) for s in shards + [small]],
        scratch_shapes=[pltpu.SemaphoreType.DMA((nsem,)), pltpu.SemaphoreType.DMA((nsem,)),
                        pltpu.SemaphoreType.DMA((1,))],
    )(*shards, small)


def _forward_halves(landed, *, name):
    n = len(landed)

    def body(*refs):
        o_refs = refs[n:2 * n]
        send_sems, recv_sems = refs[2 * n:]
        x, y, c = _mesh_pos()
        chips = _other_chips(x, y)
        pairs = [(a, k) for a in range(n) for k in range(3)]

        def copy(a, k, half):
            px, py = chips[k]
            blk = o_refs[a].at[2 * px + py, half]
            return pltpu.make_async_remote_copy(
                src_ref=blk, dst_ref=blk, send_sem=send_sems.at[3 * a + k], recv_sem=recv_sems.at[3 * a + k],
                device_id=(x, y, 1 - c), device_id_type=MESH)

        sends = [copy(a, k, c) for a, k in pairs]
        for cp in sends:
            cp.start()
        for a, k in pairs:
            copy(a, k, 1 - c).wait_recv()
        for cp in sends:
            cp.wait_send()

    return pl.pallas_call(
        body, name=name, in_specs=[HBM] * n, out_specs=[HBM] * n,
        out_shape=[jax.ShapeDtypeStruct(a.shape, a.dtype) for a in landed],
        input_output_aliases={a: a for a in range(n)},
        scratch_shapes=[pltpu.SemaphoreType.DMA((3 * n,)), pltpu.SemaphoreType.DMA((3 * n,))],
    )(*landed)


def _swap_other_half(g_list, *, name):
    n = len(g_list)

    def body(*refs):
        g_refs, o_refs = refs[:n], refs[n:2 * n]
        send_sems, recv_sems = refs[2 * n:]
        x, y, c = _mesh_pos()
        copies = [pltpu.make_async_remote_copy(
            src_ref=g_refs[a].at[:, 1 - c], dst_ref=o_refs[a], send_sem=send_sems.at[a], recv_sem=recv_sems.at[a],
            device_id=(x, y, 1 - c), device_id_type=MESH) for a in range(n)]
        for cp in copies:
            cp.start()
        for cp in copies:
            cp.wait()

    return pl.pallas_call(
        body, name=name, in_specs=[HBM] * n, out_specs=[HBM] * n,
        out_shape=[jax.ShapeDtypeStruct((g.shape[0],) + g.shape[2:], g.dtype) for g in g_list],
        scratch_shapes=[pltpu.SemaphoreType.DMA((n,)), pltpu.SemaphoreType.DMA((n,))],
    )(*g_list)


def _row_tile(r):
    return _pick(r, (512, 256, 128, 64, 32, 16, 8))


def _add_my_half(g4, sib4, core, *, name):
    n, _, r, C = g4.shape
    tr = _row_tile(r)

    def body(core_ref, g_ref, s_ref, o_ref):
        del core_ref
        o_ref[...] = (g_ref[...].astype(F32) + s_ref[...].astype(F32)).astype(o_ref.dtype)

    return pl.pallas_call(
        body, name=name,
        grid_spec=pltpu.PrefetchScalarGridSpec(
            num_scalar_prefetch=1, grid=(n, r // tr),
            in_specs=[pl.BlockSpec((None, None, tr, C), lambda j, i, core_ref: (j, core_ref[0], i, 0)),
                      pl.BlockSpec((None, tr, C), lambda j, i, core_ref: (j, i, 0))],
            out_specs=pl.BlockSpec((None, tr, C), lambda j, i, core_ref: (j, i, 0))),
        out_shape=jax.ShapeDtypeStruct((n, r, C), g4.dtype),
        compiler_params=pltpu.CompilerParams(dimension_semantics=("parallel", "parallel")),
    )(core, g4, sib4)


def _sum_chips(landed, part, me, *, name):
    _, r, C = landed.shape
    tr = _row_tile(r)

    def body(me_ref, own_ref, r1_ref, r2_ref, r3_ref, o_ref):
        del me_ref
        f = lambda ref: ref[...].astype(F32)
        o_ref[...] = ((f(own_ref) + f(r1_ref)) + f(r2_ref)) + f(r3_ref)

    slot = lambda d: pl.BlockSpec((None, tr, C), lambda i, me_ref: ((me_ref[0] + d) % N_CHIPS, i, 0))
    return pl.pallas_call(
        body, name=name,
        grid_spec=pltpu.PrefetchScalarGridSpec(
            num_scalar_prefetch=1, grid=(r // tr,), in_specs=[slot(0), slot(1), slot(2), slot(3)],
            out_specs=pl.BlockSpec((tr, C), lambda i, me_ref: (i, 0))),
        out_shape=jax.ShapeDtypeStruct((r, C), F32),
        compiler_params=pltpu.CompilerParams(dimension_semantics=("parallel",)),
    )(me, part, landed, landed, landed)


def _adamw_halves(w, mine, theirs, m, v, core, *, layer, prev, name):
    shape = w.shape
    r, C = mine.shape
    tr = _pick(r, (128, 64, 32, 16, 8))
    per = r // tr
    view = lambda a: a.reshape(-1, C)
    n_prev = 0 if prev is None else 4

    def body(*refs):
        core_ref, w_ref, gm_ref, gt_ref, m_ref, v_ref = refs[:6]
        g_ref, d_ref, nm_ref, nv_ref = refs[6 + n_prev:]
        gv = jnp.where(pl.program_id(0) == core_ref[0], gm_ref[...], gt_ref[...])
        g_ref[...] = gv
        d_ref[...], nm_ref[...], nv_ref[...] = _adamw_math(w_ref[...], gv, m_ref[...], v_ref[...])

    half = pl.BlockSpec((tr, C), lambda h, i, core_ref: ((2 * layer + h) * per + i, 0))
    row = pl.BlockSpec((tr, C), lambda h, i, core_ref: (i, 0))
    out = jax.ShapeDtypeStruct((math.prod(shape) // C, C), F32)
    res = pl.pallas_call(
        body, name=name,
        grid_spec=pltpu.PrefetchScalarGridSpec(
            num_scalar_prefetch=1, grid=(2, per), in_specs=[half, row, row, half, half] + [HBM] * n_prev,
            out_specs=[half] * 4),
        out_shape=[out] * 4,
        input_output_aliases={6 + j: j for j in range(n_prev)},
        compiler_params=pltpu.CompilerParams(dimension_semantics=("parallel", "parallel")),
    )(core, view(w), mine, theirs, view(m), view(v), *([] if prev is None else [view(a) for a in prev]))
    return tuple(a.reshape(shape) for a in res)


def _sum_small(recv4, *, name):
    _, R, C = recv4.shape

    def body(r_ref, o_ref):
        o_ref[...] = ((r_ref[0] + r_ref[1]) + r_ref[2]) + r_ref[3]

    return pl.pallas_call(body, name=name, out_shape=jax.ShapeDtypeStruct((R, C), F32))(recv4)


def _add(a, b, *, name):
    R, C = a.shape
    tr = _pick(R, (512, 256, 128, 64, 32, 16, 8))
    blk = pl.BlockSpec((tr, C), lambda i: (i, 0))

    def body(a_ref, b_ref, o_ref):
        o_ref[...] = a_ref[...] + b_ref[...]

    return pl.pallas_call(body, name=name, grid=(R // tr,), in_specs=[blk, blk], out_specs=blk,
                          out_shape=jax.ShapeDtypeStruct((R, C), F32),
                          compiler_params=pltpu.CompilerParams(dimension_semantics=("parallel",)))(a, b)


def _adamw_math(w, g, m, v):
    nm = ADAM_B1 * m + (1.0 - ADAM_B1) * g
    nv = ADAM_B2 * v + (1.0 - ADAM_B2) * (g * g)
    m_hat = nm / (1.0 - ADAM_B1 ** ADAM_STEP)
    v_hat = nv / (1.0 - ADAM_B2 ** ADAM_STEP)
    return -ADAM_LR * (m_hat / (jnp.sqrt(v_hat) + ADAM_EPS) + ADAM_WD * w), nm, nv


def _adamw(w, g, m, v, *, name):
    shape = w.shape
    C = shape[-1]
    R = w.size // C
    two = lambda a: a.reshape(R, C)
    tr = _pick(R, (256, 128, 64, 32, 16, 8)) if R % 8 == 0 and R > 8 else R
    blk = pl.BlockSpec((tr, C), lambda i: (i, 0))

    def body(w_ref, g_ref, m_ref, v_ref, d_ref, nm_ref, nv_ref):
        d_ref[...], nm_ref[...], nv_ref[...] = _adamw_math(w_ref[...], g_ref[...], m_ref[...], v_ref[...])

    out = jax.ShapeDtypeStruct((R, C), F32)
    d, nm, nv = pl.pallas_call(
        body, name=name, grid=(R // tr,), in_specs=[blk] * 4, out_specs=[blk] * 3, out_shape=[out] * 3,
        compiler_params=pltpu.CompilerParams(dimension_semantics=("parallel",)),
    )(two(w), two(g), two(m), two(v))
    return d.reshape(shape), nm.reshape(shape), nv.reshape(shape)


BIG = (("dn_w_in", (2, 1024, 1540), 2), ("dn_w_out", (2, 512, 1024), 1), ("sb_w_in", (1, 1024, 1024), 2),
       ("sb_w_out", (1, 256, 1024), 1), ("sc_w_in", (1, 1024, 2048), 2), ("sc_w_out", (1, 512, 1024), 1))
SMALL = (("dn_conv_w", (2, 4, 1024), 2), ("dn_o_norm_g", (2, 64), 1), ("sc_conv_w", (1, 3, 512), 2))
REPL = (("norm_g", (4, 1024)), ("dn_a_log", (2, 8)), ("dn_dt_bias", (2, 8)), ("sb_q_norm_g", (1, 64)),
        ("sb_k_norm_g", (1, 64)))


def _halves(shard):
    return shard.reshape(2, -1, shard.shape[-1])


def _pack(arrays, cols, lead=()):
    flat = jnp.concatenate([a.reshape(lead + (-1,)) for a in arrays], axis=-1)
    n = flat.shape[-1]
    rows = -(-n // cols)
    unit = 512 if rows > 512 else 8
    rows = -(-rows // unit) * unit
    flat = jnp.pad(flat, [(0, 0)] * len(lead) + [(0, rows * cols - n)])
    return flat.reshape(lead + (rows, cols))


def _unpack(buf, table, lead=()):
    flat = buf.reshape(lead + (-1,))
    out, off = {}, 0
    for entry in table:
        name, shape = entry[0], entry[1]
        n = math.prod(shape)
        out[name] = flat[..., off:off + n].reshape(lead + shape)
        off += n
    return out


def _join(shards, axis):
    return jnp.concatenate([shards[j] for j in range(N_CHIPS)], axis=axis)


def _split(full, axis):
    return jnp.stack(jnp.split(full, N_CHIPS, axis=axis), axis=0)


def kernel(x, norm_g, dn_w_in, dn_conv_w, dn_a_log, dn_dt_bias, dn_o_norm_g, dn_w_out, sb_w_in, sb_q_norm_g, sb_k_norm_g, sb_w_out, sc_w_in, sc_conv_w, sc_w_out, loss_target, m_norm_g, m_dn_w_in, m_dn_conv_w, m_dn_a_log, m_dn_dt_bias, m_dn_o_norm_g, m_dn_w_out, m_sb_w_in, m_sb_q_norm_g, m_sb_k_norm_g, m_sb_w_out, m_sc_w_in, m_sc_conv_w, m_sc_w_out, v_norm_g, v_dn_w_in, v_dn_conv_w, v_dn_a_log, v_dn_dt_bias, v_dn_o_norm_g, v_dn_w_out, v_sb_w_in, v_sb_q_norm_g, v_sb_k_norm_g, v_sb_w_out, v_sc_w_in, v_sc_conv_w, v_sc_w_out):
    weights = dict(norm_g=norm_g, dn_w_in=dn_w_in, dn_conv_w=dn_conv_w, dn_a_log=dn_a_log, dn_dt_bias=dn_dt_bias,
                   dn_o_norm_g=dn_o_norm_g, dn_w_out=dn_w_out, sb_w_in=sb_w_in, sb_q_norm_g=sb_q_norm_g,
                   sb_k_norm_g=sb_k_norm_g, sb_w_out=sb_w_out, sc_w_in=sc_w_in, sc_conv_w=sc_conv_w, sc_w_out=sc_w_out)
    m_in = dict(norm_g=m_norm_g, dn_w_in=m_dn_w_in, dn_conv_w=m_dn_conv_w, dn_a_log=m_dn_a_log,
                dn_dt_bias=m_dn_dt_bias, dn_o_norm_g=m_dn_o_norm_g, dn_w_out=m_dn_w_out, sb_w_in=m_sb_w_in,
                sb_q_norm_g=m_sb_q_norm_g, sb_k_norm_g=m_sb_k_norm_g, sb_w_out=m_sb_w_out, sc_w_in=m_sc_w_in,
                sc_conv_w=m_sc_conv_w, sc_w_out=m_sc_w_out)
    v_in = dict(norm_g=v_norm_g, dn_w_in=v_dn_w_in, dn_conv_w=v_dn_conv_w, dn_a_log=v_dn_a_log,
                dn_dt_bias=v_dn_dt_bias, dn_o_norm_g=v_dn_o_norm_g, dn_w_out=v_dn_w_out, sb_w_in=v_sb_w_in,
                sb_q_norm_g=v_sb_q_norm_g, sb_k_norm_g=v_sb_k_norm_g, sb_w_out=v_sb_w_out, sc_w_in=v_sc_w_in,
                sc_conv_w=v_sc_conv_w, sc_w_out=v_sc_w_out)
    order = list(weights)
    xi, yi, ci = _mesh_pos()

    small = _pack([weights[n] for n, _, _ in SMALL], LANES)
    later = [("dn_w_in", 1), ("dn_w_out", 1), ("sb_w_in", 0), ("sb_w_out", 0), ("sc_w_in", 0), ("sc_w_out", 0)]
    piece = lambda n, l: _halves(weights[n][l].astype(BF16)[None])
    own_first = [piece("dn_w_in", 0), piece("dn_w_out", 0)]
    own_later = [piece(n, l) for n, l in later]
    own_last, own_mid = own_later[:2], own_later[2:]
    me = 2 * xi + yi
    whole = lambda g4, own: lax.dynamic_update_index_in_dim(g4, own, me, 0)
    flat = lambda g4: g4.reshape(N_CHIPS, -1, g4.shape[-1])
    rows_of = lambda w4: w4.reshape(-1, w4.shape[-1])
    w_in0, w_out0, small4 = _gather_halves(own_first, small, name="gather_first")
    full = {n: _join(a, ax) for (n, _, ax), a in zip(SMALL, _unpack(small4, SMALL, (N_CHIPS,)).values())}

    def dn_in(g4, own):
        cols = [jnp.where(me == j, own, g4[j]).reshape(-1, g4.shape[-1]) for j in range(N_CHIPS)]
        cols.append(jnp.zeros((cols[0].shape[0], DN_IN_PAD - DN_IN), g4.dtype))
        return jnp.concatenate(cols, axis=1)

    def dn_args(j, w_in4, own_in, w_out4):
        return (dn_in(w_in4, own_in), full["dn_conv_w"][j], dn_a_log[j], dn_dt_bias[j], full["dn_o_norm_g"][j],
                rows_of(w_out4))

    x0 = x[0]
    dn0 = dn_args(0, w_in0, own_first[0], whole(w_out0, own_first[1]))
    x1, s0, landed = _dn_layer_fwd(x0, norm_g[0], *dn0, "l0", send=own_mid)
    landed = _forward_halves(landed, name="forward_halves_mid")
    sb_in, sb_out, sc_in, sc_out = [whole(g4, own) for g4, own in zip(landed, own_mid)]
    sb_args = (flat(sb_in), sb_q_norm_g[0], sb_k_norm_g[0], rows_of(sb_out))
    sc_args = (flat(sc_in), full["sc_conv_w"][0], rows_of(sc_out))
    x2, s1, landed = _sb_layer_fwd(x1, norm_g[1], *sb_args, "l1", send=own_last)
    landed = _forward_halves(landed, name="forward_halves_last")
    dn1 = dn_args(1, landed[0], own_last[0], whole(landed[1], own_last[1]))
    x3, s2 = _sc_layer_fwd(x2, norm_g[2], *sc_args, "l2")
    x4, s3, _ = _dn_layer_fwd(x3, norm_g[3], *dn1, "l3")
    dy, loss_local = _loss_head(x4, loss_target[0], name="loss_head")
    loss = lax.psum(loss_local[0, 0], ("x", "y", "c"))

    core = ci.astype(jnp.int32).reshape(1)
    chip = me.astype(jnp.int32).reshape(1)

    def chip_sums(g_list, tag):
        sib = _swap_other_half(g_list, name=f"swap_halves_{tag}")
        return [_add_my_half(g, s, core, name=f"sum_cores_{tag}{i}") for i, (g, s) in enumerate(zip(g_list, sib))]

    dx3, dng3, dwin3, dconv3, dal3, ddt3, dgain3, dwout3, _ = _dn_layer_bwd(dy, x3, norm_g[3], *dn1, s3, "l3")
    dx2, dng2, dwin2, dconv2, dwout2 = _sc_layer_bwd(dx3, x2, norm_g[2], *sc_args, s2, "l2")
    dx1, dng1, dwin1, dgq, dgk, dwout1 = _sb_layer_bwd(dx2, x1, norm_g[1], *sb_args, s1, "l1")
    part_later = chip_sums([_cut2(_by_cols(dwin3)), _cut2(_by_rows(dwout3)), _cut2(dwin1), _cut2(_by_rows(dwout1)),
                            _cut2(dwin2), _cut2(_by_rows(dwout2))], "later")
    dx0, dng0, part_win0, dconv0, dal0, ddt0, dgain0, part_wout0, landed = _dn_layer_bwd(
        dx1, x0, norm_g[0], *dn0, s0, "l0", send=part_later[2:], send_dwin=part_later[:2], chip_sums=chip_sums)
    pieces = later[2:] + [("dn_w_out", 0)] + later[:2] + [("dn_w_in", 0)]
    mine = {(n, l): _sum_chips(r, p, chip, name=f"sum_chips_{n}{l}")
            for (n, l), r, p in zip(pieces, landed, part_later[2:] + [part_wout0] + part_later[:2] + [part_win0])}
    pieces = sorted(pieces, key=lambda nl: nl[1])
    mine = [mine[nl] for nl in pieces]
    theirs = _sibling_exchange(mine, name="swap_results")
    upd = {}
    for (n, l), a, b in zip(pieces, mine, theirs):
        upd[n] = _adamw_halves(weights[n], a, b, m_in[n], v_in[n], core, layer=l, prev=upd.get(n),
                               name=f"adamw_{n}{l}")
    g_out = {n: upd[n][0] for n, _, _ in BIG}

    grads = dict(
        norm_g=jnp.concatenate([dng0, dng1, dng2, dng3], axis=0), dn_conv_w=jnp.stack([dconv0, dconv3]),
        dn_a_log=jnp.stack([dal0, dal3]), dn_dt_bias=jnp.stack([ddt0, ddt3]),
        dn_o_norm_g=jnp.stack([dgain0, dgain3]), sb_q_norm_g=dgq[None], sb_k_norm_g=dgk[None],
        sc_conv_w=dconv2[None])
    repl = [jnp.broadcast_to(grads[n][None], (N_CHIPS,) + s) for n, s in REPL]
    gsmall = _pack([_split(grads[n], ax) for n, _, ax in SMALL] + repl, LANES, (N_CHIPS,))
    rsmall, = _chip_exchange([gsmall], send_slot_is_dest=True, copy_own=(True,), name="scatter_small")
    psmall = _sum_small(rsmall, name="sum_chips_small")
    qsmall, = _sibling_exchange([psmall], name="swap_cores_small")
    tsmall = _add(psmall, qsmall, name="sum_cores_small")
    g_out.update(_unpack(tsmall, SMALL + REPL))

    for n in order:
        if n not in upd:
            upd[n] = (g_out[n],) + _adamw(weights[n], g_out[n], m_in[n], v_in[n], name=f"adamw_{n}")
    return (loss, dx0[None], *[upd[n][0] for n in order], *[upd[n][1] for n in order],
            *[upd[n][2] for n in order], *[upd[n][3] for n in order])
```

```python
import functools
import math

import jax
import jax.numpy as jnp
from jax import lax
from jax.experimental import pallas as pl
from jax.experimental.pallas import tpu as pltpu

F32 = jnp.float32
BF16 = jnp.bfloat16
MESH = pl.DeviceIdType.MESH

RMS_EPS = 1e-6
L2_EPS = 1e-6
LANES = 128
VMEM_BIG = 60 * 1024 * 1024
MM_VMEM = 44 * 1024 * 1024

DN_HEADS, DN_DK, DN_DV, DN_CHUNK, DN_CONV = 8, 128, 256, 64, 4
DN_QK_W = DN_HEADS * DN_DK
DN_V_W = DN_HEADS * DN_DV
DN_CONV_W = 2 * DN_QK_W + DN_V_W
DN_IN = DN_CONV_W + DN_V_W + 2 * DN_HEADS
DN_IN_PAD = DN_CONV_W + DN_V_W + LANES
SB_DH = 64
SC_CONV = 3

ADAM_LR, ADAM_B1, ADAM_B2, ADAM_EPS, ADAM_WD, ADAM_STEP = 0.001, 0.9, 0.999, 1e-08, 0.01, 10


def _pick(n, cands):
    for c in cands:
        if n % c == 0:
            return c
    raise ValueError(f"no tile for {n} in {cands}")


def _bf(x):
    return x.astype(BF16)


def _dot(a, b):
    return jnp.dot(_bf(a), _bf(b), preferred_element_type=F32)


def _dot_nt(a, b):
    return lax.dot_general(_bf(a), _bf(b), (((1,), (1,)), ((), ())), preferred_element_type=F32)


def _dot_tn(a, b):
    return lax.dot_general(_bf(a), _bf(b), (((0,), (0,)), ((), ())), preferred_element_type=F32)


def _split3(a):
    hi = _bf(a)
    r = a - hi.astype(F32)
    mid = _bf(r)
    lo = _bf(r - mid.astype(F32))
    return hi, mid, lo


def _sigmoid(x):
    return 1.0 / (1.0 + jnp.exp(-x))


def _silu(x):
    return x * _sigmoid(x)


def _dsilu(x):
    s = _sigmoid(x)
    return s * (1.0 + x * (1.0 - s))


def _softplus(x):
    return jnp.maximum(x, 0.0) + jnp.log(1.0 + jnp.exp(-jnp.abs(x)))


def _shift_down(z, k):
    if k == 0:
        return z
    row = lax.broadcasted_iota(jnp.int32, z.shape, 0)
    return jnp.where(row >= k, pltpu.roll(z, k, 0), 0.0)


def _shift_up(z, k):
    if k == 0:
        return z
    n = z.shape[0]
    row = lax.broadcasted_iota(jnp.int32, z.shape, 0)
    return jnp.where(row < n - k, pltpu.roll(z, n - k, 0), 0.0)


def _matmul(a, b, *, mode, name, res=None, a_parts=1, b_parts=1, out_parts=1, out_dtype=F32, send=(), norm_g=None):
    def dims2(x, parts):
        if parts == 1:
            return x.shape
        assert x.shape[0] == parts
        return (x.shape[1], x.shape[2] * parts)

    ash, bsh = dims2(a, a_parts), dims2(b, b_parts)
    if mode == "nn":
        (M, K), (K2, N) = ash, bsh
        dn = (((1,), (0,)), ((), ()))
    elif mode == "nt":
        (M, K), (N, K2) = ash, bsh
        dn = (((1,), (1,)), ((), ()))
    else:
        (K, M), (K2, N) = ash, bsh
        dn = (((0,), (0,)), ((), ()))
    assert K == K2, (ash, bsh, mode)
    tm_max = _pick(M, (512, 256, 128, 64, 32, 16, 8))
    n_unit = N // max(out_parts, b_parts if mode != "nt" else 1)
    k_unit = K // max(a_parts if mode != "tn" else 1, b_parts if mode == "nt" else 1)
    tm, tn, tk = min(
        ((m, n, k) for m in {tm_max, max(tm_max // 2, 8)}
         for n in ((N,) if norm_g is not None else (2048, 1792, 1024, 896, 768, 512, 384, 256, 128)) if n_unit % n == 0
         for k in (k_unit, 2048, 1792, 1024, 896, 512, 256, 128) if k_unit % k == 0
         if 2 * (m * k * a.dtype.itemsize + k * n * b.dtype.itemsize + 2 * m * n * 4) + m * n * 4 <= MM_VMEM),
        key=lambda t: (-t[0] * t[1] * t[2], -t[0], -t[2]))
    nk = K // tk
    grid = (M // tm, N // tn, nk)

    def spec(parts, rows_are, cols_are, tr, tc, width):
        per = width // parts // tc
        if parts == 1:
            return pl.BlockSpec((tr, tc), lambda i, j, k: ((i, j, k)[rows_are], (i, j, k)[cols_are]))
        return pl.BlockSpec((None, tr, tc), lambda i, j, k: ((i, j, k)[cols_are] // per, (i, j, k)[rows_are],
                                                             (i, j, k)[cols_are] % per))

    if mode == "nn":
        a_spec = spec(a_parts, 0, 2, tm, tk, K)
        b_spec = spec(b_parts, 2, 1, tk, tn, N)
    elif mode == "nt":
        a_spec = spec(a_parts, 0, 2, tm, tk, K)
        b_spec = spec(b_parts, 1, 2, tn, tk, K)
    else:
        a_spec = spec(a_parts, 2, 0, tk, tm, M)
        b_spec = spec(b_parts, 2, 1, tk, tn, N)
    o_spec = spec(out_parts, 0, 1, tm, tn, N)
    in_specs = [a_spec, b_spec]
    operands = [a, b]
    if res is not None:
        in_specs.append(pl.BlockSpec((tm, tn), lambda i, j, k: (i, j)))
        operands.append(res)
    nh = 0 if norm_g is None else 1
    if nh:
        assert res is not None and out_parts == 1 and out_dtype == F32 and tn == N
        in_specs.append(pl.BlockSpec((1, N), lambda i, j, k: (0, 0)))
        operands.append(norm_g.reshape(1, N))

    n_in = len(operands)
    ns = len(send)

    def finish(refs, r):
        if res is not None:
            r = refs[2][...] + r
        refs[n_in + ns][...] = r.astype(out_dtype)
        if nh:
            scale = lax.rsqrt(jnp.mean(r * r, axis=-1, keepdims=True) + RMS_EPS)
            refs[n_in + 2 * ns + 1][...] = ((r * scale) * refs[3][...]).astype(BF16)

    def body(*refs):
        if ns:
            at = lambda step: functools.reduce(jnp.logical_and, [pl.program_id(d) == step[d] for d in range(3)])
            _blocks_over_ici(refs[n_in:n_in + ns], refs[n_in + ns + 1:n_in + 2 * ns + 1], refs[-2], refs[-1],
                             at((0, 0, 0)), at(tuple(g - 1 for g in grid)))
        part = lax.dot_general(_bf(refs[0][...]), _bf(refs[1][...]), dn, preferred_element_type=F32)
        if nk == 1:
            finish(refs, part)
            return
        acc_ref = refs[n_in + 2 * ns + 1 + nh]
        k = pl.program_id(2)

        @pl.when(k == 0)
        def _():
            acc_ref[...] = part

        @pl.when(jnp.logical_and(k > 0, k < nk - 1))
        def _():
            acc_ref[...] += part

        @pl.when(k == nk - 1)
        def _():
            finish(refs, acc_ref[...] + part)

    out_shape = (M, N) if out_parts == 1 else (out_parts, M, N // out_parts)
    out = pl.pallas_call(
        body, name=name, grid=grid, in_specs=in_specs + [HBM] * ns, out_specs=[o_spec] + [HBM] * ns + [o_spec] * nh,
        out_shape=[jax.ShapeDtypeStruct(out_shape, out_dtype)] + [jax.ShapeDtypeStruct(x.shape, x.dtype) for x in send]
        + [jax.ShapeDtypeStruct((M, N), BF16)] * nh,
        scratch_shapes=([pltpu.VMEM((tm, tn), F32)] if nk > 1 else [])
        + ([pltpu.SemaphoreType.DMA((3 * ns,)), pltpu.SemaphoreType.DMA((3 * ns,))] if ns else []),
        compiler_params=pltpu.CompilerParams(
            dimension_semantics=("arbitrary",) * 3 if ns else ("parallel", "parallel", "arbitrary"),
            vmem_limit_bytes=VMEM_BIG),
    )(*operands, *send)
    return out if ns or nh else out[0]


def _rmsnorm_fwd(x, g, *, name):
    T, D = x.shape
    tm = _pick(T, (512, 256, 128, 64, 32, 16))

    def body(x_ref, g_ref, h_ref):
        xv = x_ref[...]
        r = lax.rsqrt(jnp.mean(xv * xv, axis=-1, keepdims=True) + RMS_EPS)
        h_ref[...] = ((xv * r) * g_ref[...]).astype(BF16)

    return pl.pallas_call(
        body, name=name, grid=(T // tm,),
        in_specs=[pl.BlockSpec((tm, D), lambda i: (i, 0)), pl.BlockSpec((1, D), lambda i: (0, 0))],
        out_specs=pl.BlockSpec((tm, D), lambda i: (i, 0)),
        out_shape=jax.ShapeDtypeStruct((T, D), BF16),
    )(x, g.reshape(1, D))


def _rmsnorm_bwd(x, g, dh, dx_in, *, name):
    T, D = x.shape
    tm = _pick(T, (512, 256, 128, 64, 32, 16))

    def body(x_ref, g_ref, dh_ref, dxin_ref, dx_ref, dg_ref):
        @pl.when(pl.program_id(0) == 0)
        def _():
            dg_ref[...] = jnp.zeros_like(dg_ref)

        xv = x_ref[...]
        r = lax.rsqrt(jnp.mean(xv * xv, axis=-1, keepdims=True) + RMS_EPS)
        xh = xv * r
        dh_v = dh_ref[...]
        dxh = dh_v * g_ref[...]
        dx_ref[...] = dxin_ref[...] + r * (dxh - xh * jnp.mean(dxh * xh, axis=-1, keepdims=True))
        dg_ref[...] += jnp.sum(dh_v * xh, axis=0, keepdims=True)

    row = pl.BlockSpec((tm, D), lambda i: (i, 0))
    vec = pl.BlockSpec((1, D), lambda i: (0, 0))
    return pl.pallas_call(
        body, name=name, grid=(T // tm,),
        in_specs=[row, vec, row, row], out_specs=[row, vec],
        out_shape=[jax.ShapeDtypeStruct((T, D), F32), jax.ShapeDtypeStruct((1, D), F32)],
        compiler_params=pltpu.CompilerParams(dimension_semantics=("arbitrary",)),
    )(x, g.reshape(1, D), dh, dx_in)


def _loss_head(y, target, *, name):
    T, D = y.shape
    tm = _pick(T, (512, 256, 128, 64, 32, 16))

    def body(y_ref, t_ref, dy_ref, l_ref):
        @pl.when(pl.program_id(0) == 0)
        def _():
            l_ref[...] = jnp.zeros_like(l_ref)

        err = y_ref[...] - t_ref[...]
        dy_ref[...] = err * (1.0 / D)
        l_ref[...] += 0.5 * jnp.sum(jnp.mean(err * err, axis=-1, keepdims=True), axis=0, keepdims=True)

    row = pl.BlockSpec((tm, D), lambda i: (i, 0))
    return pl.pallas_call(
        body, name=name, grid=(T // tm,),
        in_specs=[row, row], out_specs=[row, pl.BlockSpec((1, 1), lambda i: (0, 0))],
        out_shape=[jax.ShapeDtypeStruct((T, D), F32), jax.ShapeDtypeStruct((1, 1), F32)],
        compiler_params=pltpu.CompilerParams(dimension_semantics=("arbitrary",)),
    )(y, target)


def _sc_mid_fwd(p3, conv_w, *, name):
    _, T, W = p3.shape
    K = conv_w.shape[0]
    cw = LANES

    def body(p_ref, w_ref, o_ref):
        z = p_ref[1] * p_ref[2]
        cv = sum(w_ref[i:i + 1, :] * _shift_down(z, K - 1 - i) for i in range(K))
        o_ref[...] = ((p_ref[0] * cv) * _silu(p_ref[3])).astype(BF16)

    return pl.pallas_call(
        body, name=name, grid=(W // cw,),
        in_specs=[pl.BlockSpec((4, T, cw), lambda j: (0, 0, j)), pl.BlockSpec((K, cw), lambda j: (0, j))],
        out_specs=pl.BlockSpec((T, cw), lambda j: (0, j)),
        out_shape=jax.ShapeDtypeStruct((T, W), BF16),
        compiler_params=pltpu.CompilerParams(dimension_semantics=("parallel",), vmem_limit_bytes=VMEM_BIG),
    )(p3, conv_w)


def _sc_mid_bwd(p3, conv_w, do, *, name):
    _, T, W = p3.shape
    K = conv_w.shape[0]
    cw = LANES

    def body(p_ref, w_ref, do_ref, dp_ref, dw_ref):
        b, c, u, gate = p_ref[0], p_ref[1], p_ref[2], p_ref[3]
        z = c * u
        zs = [_shift_down(z, K - 1 - i) for i in range(K)]
        cv = sum(w_ref[i:i + 1, :] * zs[i] for i in range(K))
        y = b * cv
        dov = do_ref[...]
        dy = dov * _silu(gate)
        dp_ref[3] = dov * y * _dsilu(gate)
        dp_ref[0] = dy * cv
        dcv = dy * b
        dz = sum(w_ref[i:i + 1, :] * _shift_up(dcv, K - 1 - i) for i in range(K))
        dp_ref[1] = dz * u
        dp_ref[2] = dz * c
        for i in range(K):
            dw_ref[i:i + 1, :] = jnp.sum(dcv * zs[i], axis=0, keepdims=True)

    return pl.pallas_call(
        body, name=name, grid=(W // cw,),
        in_specs=[pl.BlockSpec((4, T, cw), lambda j: (0, 0, j)), pl.BlockSpec((K, cw), lambda j: (0, j)),
                  pl.BlockSpec((T, cw), lambda j: (0, j))],
        out_specs=[pl.BlockSpec((4, T, cw), lambda j: (0, 0, j)), pl.BlockSpec((K, cw), lambda j: (0, j))],
        out_shape=[jax.ShapeDtypeStruct((4, T, W), F32), jax.ShapeDtypeStruct((K, W), F32)],
        compiler_params=pltpu.CompilerParams(dimension_semantics=("parallel",), vmem_limit_bytes=VMEM_BIG),
    )(p3, conv_w, do)


def _sc_layer_fwd(x, ng, w_in, conv_w, w_out, tag, h=None, next_g=None):
    if h is None:
        h = _rmsnorm_fwd(x, ng, name=f"{tag}_norm")
    p3 = _matmul(h, w_in, mode="nn", b_parts=4, out_parts=4, name=f"{tag}_inproj")
    og = _sc_mid_fwd(p3, conv_w, name=f"{tag}_mid")
    x_new = _matmul(og, w_out, mode="nn", res=x, name=f"{tag}_outproj", norm_g=next_g)
    return x_new, (h, p3, og)


def _sc_layer_bwd(dx, x, ng, w_in, conv_w, w_out, saved, tag):
    h, p3, og = saved
    d_wout = _matmul(og, dx, mode="tn", out_dtype=BF16, name=f"{tag}_dwout")
    dog = _matmul(dx, w_out, mode="nt", name=f"{tag}_dog")
    dp3, dconv = _sc_mid_bwd(p3, conv_w, dog, name=f"{tag}_midbwd")
    d_win = _matmul(h, dp3, mode="tn", b_parts=4, out_parts=4, out_dtype=BF16, name=f"{tag}_dwin")
    dh = _matmul(dp3, w_in, mode="nt", a_parts=4, b_parts=4, name=f"{tag}_dh")
    dx_prev, dng = _rmsnorm_bwd(x, ng, dh, dx, name=f"{tag}_normbwd")
    return dx_prev, dng, d_win, dconv, d_wout


SB_BQ = 256
SB_BK = 256
SB_ROWS = 512
SB_DEAD = -110.0


def _sb_half_mask():
    return lax.broadcasted_iota(jnp.int32, (1, LANES), 1) < SB_DH


def _sb_headnorm(x, g, lo):
    x2 = x * x
    s_lo = jnp.sum(jnp.where(lo, x2, 0.0), axis=-1, keepdims=True)
    s_hi = jnp.sum(jnp.where(lo, 0.0, x2), axis=-1, keepdims=True)
    r = lax.rsqrt(jnp.where(lo, s_lo, s_hi) * (1.0 / SB_DH) + RMS_EPS)
    xh = x * r
    return xh * g, xh, r


def _dot_x2_l(a_l, b_exact_bf16):
    his = [_bf(a) for a in a_l]
    mids = [_bf(a - h.astype(F32)) for a, h in zip(a_l, his)]
    f = lambda p: jnp.dot(p, b_exact_bf16, preferred_element_type=F32)
    return [x + y for x, y in zip([f(h) for h in his], [f(m) for m in mids])]


def _sb_stack(xb, lo):
    zero = jnp.zeros_like(xb)
    return jnp.concatenate([jnp.where(lo, xb, zero), jnp.where(lo, zero, xb)], axis=0)


def _sb_rel(bq, bk):
    row = lax.broadcasted_iota(jnp.int32, (2 * bq, bk), 0)
    col = lax.broadcasted_iota(jnp.int32, (2 * bq, bk), 1)
    return col - jnp.where(row >= bq, row - bq, row)


def _sb_tile(qm, kb, valid):
    z = lax.dot_general(qm, kb, (((1,), (1,)), ((), ())), preferred_element_type=F32)
    sp = _softplus(z)
    return z - sp, (-sp if valid is None else jnp.where(valid, -sp, 0.0))


def _sb_attn_fwd(p3, gq2, gk2, *, name, send=()):
    _, T, W = p3.shape
    bq, bk = min(SB_BQ, T), min(SB_BK, T)
    rows = min(SB_ROWS, T)
    scale = SB_DH ** -0.5
    ns = len(send)
    npair = W // LANES

    def body(*refs):
        p_ref, gq_ref, gk_ref = refs[:3]
        og_ref, o_ref, ls_ref, cnt_ref = refs[3 + ns:7 + ns]
        qn_ref, kn_ref, v_ref = refs[7 + 2 * ns:10 + 2 * ns]
        if ns:
            _halves_over_ici(refs[3:3 + ns], refs[7 + ns:7 + 2 * ns], refs[10 + 2 * ns], refs[11 + 2 * ns],
                             pl.program_id(0) == 0, pl.program_id(0) == npair - 1)
        lo = _sb_half_mask()

        def prologue(i, c):
            r0 = pl.multiple_of(i * rows, rows)
            sl = pl.ds(r0, rows)
            qn_ref[sl, :] = (_sb_headnorm(p_ref[0, sl, :], gq_ref[...], lo)[0] * scale).astype(BF16)
            kn_ref[sl, :] = _sb_headnorm(p_ref[1, sl, :], gk_ref[...], lo)[0].astype(BF16)
            v_ref[sl, :] = p_ref[2, sl, :].astype(BF16)
            return c

        lax.fori_loop(0, T // rows, prologue, 0)

        rel = _sb_rel(bq, bk)
        tri = (lax.broadcasted_iota(jnp.int32, (bk, bk), 0)
               > lax.broadcasted_iota(jnp.int32, (bk, bk), 1)).astype(BF16)

        def qblock(qi, c):
            q0 = pl.multiple_of(qi * bq, bq)
            qm = _sb_stack(qn_ref[pl.ds(q0, bq), :], lo)
            nkb = (q0 + bq - 1) // bk + 1

            def tiles(k0s, carry, valids):
                o_acc, a_carry = carry
                sc = [_sb_tile(qm, kn_ref[pl.ds(k0, bk), :], valid) for k0, valid in zip(k0s, valids)]
                later = _dot_x2_l([log1m for _, log1m in sc], tri)
                for (logsig, log1m), lat, k0, valid in zip(sc, later, k0s, valids):
                    wts = jnp.exp(logsig + (lat + a_carry))
                    if valid is not None:
                        wts = jnp.where(valid, wts, 0.0)
                    o_acc = o_acc + jnp.dot(_bf(wts), v_ref[pl.ds(k0, bk), :], preferred_element_type=F32)
                    a_carry = a_carry + jnp.sum(log1m, axis=-1, keepdims=True)
                return o_acc, a_carry

            blk0 = lambda j: pl.multiple_of(j * bk, bk)
            k_last = blk0(nkb - 1)
            o2, t2 = tiles([k_last, blk0(jnp.maximum(nkb - 2, 0))],
                           (jnp.zeros((2 * bq, LANES), F32), jnp.zeros((2 * bq, 1), F32)),
                           [rel < q0 - k_last, nkb >= 2])

            def alive(st):
                return jnp.logical_and(st[0] < nkb - 1, jnp.max(st[2]) > SB_DEAD)

            def back_one(st):
                return (st[0] + 1,) + tiles([blk0(nkb - 2 - st[0])], st[1:], [None])

            n_back, o2, t2 = lax.while_loop(alive, back_one, (jnp.int32(1), o2, t2))
            o = jnp.where(lo, o2[:bq], o2[bq:])
            o_ref[pl.ds(q0, bq), :] = o
            ls_ref[pl.ds(q0, bq), :] = jnp.where(lo, t2[:bq], t2[bq:])
            cnt_ref[qi] = jnp.full((8, LANES), jnp.minimum(n_back + 1, nkb).astype(F32))
            og_ref[pl.ds(q0, bq), :] = (o * _silu(p_ref[3, pl.ds(q0, bq), :])).astype(BF16)
            return c

        lax.fori_loop(0, T // bq, qblock, 0)

    colblk = pl.BlockSpec((T, LANES), lambda j: (0, j))
    vec = pl.BlockSpec((1, LANES), lambda j: (0, 0))
    return pl.pallas_call(
        body, name=name, grid=(npair,),
        in_specs=[pl.BlockSpec((4, T, LANES), lambda j: (0, 0, j)), vec, vec] + [HBM] * ns,
        out_specs=[colblk, colblk, colblk, pl.BlockSpec((None, T // bq, 8, LANES), lambda j: (j, 0, 0, 0))]
        + [HBM] * ns,
        out_shape=[jax.ShapeDtypeStruct((T, W), BF16), jax.ShapeDtypeStruct((T, W), F32),
                   jax.ShapeDtypeStruct((T, W), F32), jax.ShapeDtypeStruct((npair, T // bq, 8, LANES), F32)]
        + [jax.ShapeDtypeStruct((N_CHIPS,) + a.shape, a.dtype) for a in send],
        scratch_shapes=[pltpu.VMEM((T, LANES), BF16)] * 3
        + ([pltpu.SemaphoreType.DMA((3 * ns,)), pltpu.SemaphoreType.DMA((3 * ns,))] if ns else []),
        compiler_params=pltpu.CompilerParams(dimension_semantics=("arbitrary",), vmem_limit_bytes=VMEM_BIG),
    )(p3, gq2, gk2, *send)


def _sb_attn_bwd(p3, gq2, gk2, o, lsum, live, dog, *, name):
    _, T, W = p3.shape
    bq, bk = min(SB_BQ, T), min(SB_BK, T)
    rows = min(SB_ROWS, T)
    scale = SB_DH ** -0.5

    def body(p_ref, gq_ref, gk_ref, o_ref, ls_ref, cnt_ref, dog_ref, dp_ref, dgq_ref, dgk_ref,
             qn_ref, kn_ref, v_ref, do_ref):
        lo = _sb_half_mask()

        def prologue(i, c):
            r0 = pl.multiple_of(i * rows, rows)
            sl = pl.ds(r0, rows)
            qn_ref[sl, :] = (_sb_headnorm(p_ref[0, sl, :], gq_ref[...], lo)[0] * scale).astype(BF16)
            kn_ref[sl, :] = _sb_headnorm(p_ref[1, sl, :], gk_ref[...], lo)[0].astype(BF16)
            v_ref[sl, :] = p_ref[2, sl, :].astype(BF16)
            gate = p_ref[3, sl, :]
            dogv = dog_ref[sl, :]
            dp_ref[3, sl, :] = dogv * o_ref[sl, :] * _dsilu(gate)
            do_ref[sl, :] = (dogv * _silu(gate)).astype(BF16)
            zero = jnp.zeros((rows, LANES), F32)
            dp_ref[0, sl, :] = zero
            dp_ref[1, sl, :] = zero
            dp_ref[2, sl, :] = zero
            return c

        lax.fori_loop(0, T // rows, prologue, 0)

        rel = _sb_rel(bq, bk)
        rj = lax.broadcasted_iota(jnp.int32, (bk, bk), 0)
        cj = lax.broadcasted_iota(jnp.int32, (bk, bk), 1)
        upto = (rj <= cj).astype(BF16)
        before_m = (rj < cj).astype(BF16)

        def qblock(qi, c):
            q0 = pl.multiple_of(qi * bq, bq)
            qm = _sb_stack(qn_ref[pl.ds(q0, bq), :], lo)
            dom = _sb_stack(do_ref[pl.ds(q0, bq), :], lo)
            nkb = (q0 + bq - 1) // bk + 1
            blk0 = lambda j: pl.multiple_of(j * bk, bk)
            k_last = blk0(nkb - 1)

            lsb = ls_ref[pl.ds(q0, bq), :]
            total = jnp.concatenate([lsb[:, 0:1], lsb[:, SB_DH:SB_DH + 1]], axis=0)
            n_live = jnp.clip(jnp.max(cnt_ref[qi]).astype(jnp.int32), 1, nkb)
            k_first = nkb - n_live

            def tiles(k0s, carry, valids):
                dq_acc, a_pre, r_pre = carry
                kss = [pl.ds(k0, bk) for k0 in k0s]
                kbs = [kn_ref[ks, :] for ks in kss]
                sc = [_sb_tile(qm, kb, valid) for kb, valid in zip(kbs, valids)]
                dws = [lax.dot_general(dom, v_ref[ks, :], _NT, preferred_element_type=F32) for ks in kss]
                upto_l = _dot_x2_l([log1m for _, log1m in sc], upto)
                wts_l = []
                for (logsig, log1m), up, valid in zip(sc, upto_l, valids):
                    wts = jnp.exp(logsig + ((total - a_pre) - up))
                    wts_l.append(wts if valid is None else jnp.where(valid, wts, 0.0))
                    a_pre = a_pre + jnp.sum(log1m, axis=-1, keepdims=True)
                ee_l = [dw * wts for dw, wts in zip(dws, wts_l)]
                before_l = _dot_x2_l(ee_l, before_m)
                for (logsig, _), ks, kb, wts, ee, bef, valid in zip(sc, kss, kbs, wts_l, ee_l, before_l, valids):
                    beta = jnp.exp(logsig)
                    dz = ee * (1.0 - beta) - beta * (r_pre + bef)
                    if valid is not None:
                        dz = jnp.where(valid, dz, 0.0)
                    dzb = _bf(dz)
                    dq_acc = dq_acc + jnp.dot(dzb, kb, preferred_element_type=F32)
                    dp_ref[1, ks, :] += lax.dot_general(dzb, qm, _TN, preferred_element_type=F32)
                    dp_ref[2, ks, :] += lax.dot_general(_bf(wts), dom, _TN, preferred_element_type=F32)
                    r_pre = r_pre + jnp.sum(ee, axis=-1, keepdims=True)
                return dq_acc, a_pre, r_pre

            cr = (jnp.zeros((2 * bq, LANES), F32), jnp.zeros((2 * bq, 1), F32), jnp.zeros((2 * bq, 1), F32))
            n_before = jnp.maximum(n_live - 2, 0)
            cr = lax.fori_loop(0, n_before % 2, lambda t, cr: tiles([blk0(k_first)], cr, [None]), cr)
            k_pairs = k_first + n_before % 2
            cr = lax.fori_loop(0, n_before // 2,
                               lambda t, cr: tiles([blk0(k_pairs + 2 * t), blk0(k_pairs + 2 * t + 1)], cr,
                                                   [None, None]), cr)
            dq2, _, _ = tiles([blk0(jnp.maximum(nkb - 2, 0)), k_last], cr, [n_live >= 2, rel < q0 - k_last])
            dp_ref[0, pl.ds(q0, bq), :] = jnp.where(lo, dq2[:bq], dq2[bq:]) * scale
            return c

        lax.fori_loop(0, T // bq, qblock, 0)

        dgq_ref[...] = jnp.zeros_like(dgq_ref)
        dgk_ref[...] = jnp.zeros_like(dgk_ref)

        def epilogue(i, c):
            r0 = pl.multiple_of(i * rows, rows)
            sl = pl.ds(r0, rows)
            for part, g_ref, dg_ref in ((0, gq_ref, dgq_ref), (1, gk_ref, dgk_ref)):
                _, xh, r = _sb_headnorm(p_ref[part, sl, :], g_ref[...], lo)
                dn = dp_ref[part, sl, :]
                dxh = dn * g_ref[...]
                prod = dxh * xh
                m_lo = jnp.sum(jnp.where(lo, prod, 0.0), axis=-1, keepdims=True)
                m_hi = jnp.sum(jnp.where(lo, 0.0, prod), axis=-1, keepdims=True)
                m = jnp.where(lo, m_lo, m_hi) * (1.0 / SB_DH)
                dp_ref[part, sl, :] = r * (dxh - xh * m)
                dg_ref[...] += jnp.sum(dn * xh, axis=0, keepdims=True)
            return c

        lax.fori_loop(0, T // rows, epilogue, 0)

    colblk = pl.BlockSpec((T, LANES), lambda j: (0, j))
    vec = pl.BlockSpec((1, LANES), lambda j: (0, 0))
    part = pl.BlockSpec((4, T, LANES), lambda j: (0, 0, j))
    gvec = pl.BlockSpec((None, 1, LANES), lambda j: (j, 0, 0))
    npair = W // LANES
    return pl.pallas_call(
        body, name=name, grid=(npair,),
        in_specs=[part, vec, vec, colblk, colblk, pl.BlockSpec((None, T // bq, 8, LANES), lambda j: (j, 0, 0, 0)),
                  colblk],
        out_specs=[part, gvec, gvec],
        out_shape=[jax.ShapeDtypeStruct((4, T, W), F32), jax.ShapeDtypeStruct((npair, 1, LANES), F32),
                   jax.ShapeDtypeStruct((npair, 1, LANES), F32)],
        scratch_shapes=[pltpu.VMEM((T, LANES), BF16)] * 4,
        compiler_params=pltpu.CompilerParams(dimension_semantics=("parallel",), vmem_limit_bytes=VMEM_BIG),
    )(p3, gq2, gk2, o, lsum, live, dog)


_NN = (((1,), (0,)), ((), ()))
_NT = (((1,), (1,)), ((), ()))
_TN = (((0,), (0,)), ((), ()))
DN_TB = 512
DN_HEADS_FWD = 4
DN_HEADS_BWD = 2
DN_INV_EXACT_LEVELS = 2
DN_AB_COL = (DN_CONV_W + DN_V_W) // LANES


def _dn_conv(x, w_ref):
    k = w_ref.shape[0]
    return sum(w_ref[i:i + 1, :] * _shift_down(x, k - 1 - i) for i in range(k))


def _dn_prep_fwd(p, conv_w, *, name):
    T = p.shape[0]
    cw = conv_w.shape[1]
    n_qk = 2 * DN_QK_W // LANES

    def body(p_ref, w_ref, o_ref):
        s = _silu(_dn_conv(p_ref[...], w_ref))
        r = lax.rsqrt(jnp.sum(s * s, axis=-1, keepdims=True) + L2_EPS)
        o_ref[...] = jnp.where(pl.program_id(0) < n_qk, s * r, s)

    colblk = pl.BlockSpec((T, LANES), lambda j: (0, j))
    return pl.pallas_call(
        body, name=name, grid=(cw // LANES,),
        in_specs=[colblk, pl.BlockSpec((DN_CONV, LANES), lambda j: (0, j))],
        out_specs=colblk, out_shape=jax.ShapeDtypeStruct((T, cw), F32),
        compiler_params=pltpu.CompilerParams(dimension_semantics=("parallel",), vmem_limit_bytes=VMEM_BIG),
    )(p, conv_w)


def _dn_chunk_tri(rows, upper):
    r = lax.broadcasted_iota(jnp.int32, (rows, rows), 0)
    c = lax.broadcasted_iota(jnp.int32, (rows, rows), 1)
    same = (r // DN_CHUNK) == (c // DN_CHUNK)
    return jnp.logical_and(same, (c >= r) if upper else (c <= r)).astype(BF16)


def _dn_lane_rows(a_log, dt_bias):
    pad = lambda v: jnp.zeros((1, LANES), F32).at[0, :DN_HEADS].set(v)
    return pad(a_log), pad(dt_bias)


def _dn_ab_parts(blk, alog_row, dtb_row):
    lane = lax.broadcasted_iota(jnp.int32, (1, LANES), 1)
    is_a = lane < DN_HEADS
    is_b = jnp.logical_and(lane >= DN_HEADS, lane < 2 * DN_HEADS)
    a_arg = jnp.where(is_a, blk + dtb_row, 0.0)
    neg_exp = jnp.where(is_a, -jnp.exp(alog_row), 0.0)
    log_a = neg_exp * _softplus(a_arg)
    beta = jnp.where(is_b, _sigmoid(blk), 0.0)
    return is_a, is_b, a_arg, neg_exp, log_a, beta


def _dn_ab_fwd(p, alog_row, dtb_row, *, name):
    T = p.shape[0]
    rows = min(DN_TB, T)

    def body(p_ref, al_ref, dt_ref, o_ref):
        _, _, _, _, log_a, beta = _dn_ab_parts(p_ref[...], al_ref[...], dt_ref[...])
        hi, mid, lo_ = _split3(log_a)
        tri = _dn_chunk_tri(rows, upper=False)
        f = lambda q: jnp.dot(tri, q, preferred_element_type=F32)
        o_ref[...] = (f(hi) + f(mid) + f(lo_)) + beta

    blk = pl.BlockSpec((rows, LANES), lambda i: (i, DN_AB_COL))
    vec = pl.BlockSpec((1, LANES), lambda i: (0, 0))
    return pl.pallas_call(
        body, name=name, grid=(T // rows,), in_specs=[blk, vec, vec],
        out_specs=pl.BlockSpec((rows, LANES), lambda i: (i, 0)),
        out_shape=jax.ShapeDtypeStruct((T, LANES), F32),
        compiler_params=pltpu.CompilerParams(dimension_semantics=("parallel",)),
    )(p, alog_row, dtb_row)


def _hp_l(a_l, b_l, dims=_NN):
    sa = [_split3(a)[:2] for a in a_l]
    sb = [_split3(b)[:2] for b in b_l]
    f = lambda p, q: lax.dot_general(p, q, dims, preferred_element_type=F32)
    hh = [f(x[0], y[0]) for x, y in zip(sa, sb)]
    hm = [f(x[0], y[1]) for x, y in zip(sa, sb)]
    mh = [f(x[1], y[0]) for x, y in zip(sa, sb)]
    return [a + (b + c) for a, b, c in zip(hh, hm, mh)]


def _dn_local(qs, k, v, g, beta, nc, inv_l=None):
    c = DN_CHUNK
    cut = lambda x: [x[i * c:(i + 1) * c] for i in range(nc)]
    row = lax.broadcasted_iota(jnp.int32, (c, c), 0)
    col = lax.broadcasted_iota(jnp.int32, (c, c), 1)
    eye, lower, strict = row == col, row >= col, row > col
    rowid = lax.broadcasted_iota(jnp.int32, (c, 1), 0)
    eg = jnp.exp(g)
    kb = k * beta
    rhs_k = kb * eg
    g_l, k_l, kb_l, qs_l = cut(g), cut(k), cut(kb), cut(qs)
    g_row_l = [jnp.sum(jnp.where(eye, x, 0.0), axis=0, keepdims=True) for x in g_l]
    dec_l = [jnp.where(lower, jnp.exp(jnp.where(lower, x - y, 0.0)), 0.0) for x, y in zip(g_l, g_row_l)]
    kk_l = [_dot_nt(a, b) for a, b in zip(kb_l, k_l)]
    qk_l = [_dot_nt(a, b) for a, b in zip(qs_l, k_l)]
    low_l = [jnp.where(strict, a * d, 0.0) for a, d in zip(kk_l, dec_l)]
    if inv_l is None:
        pw_l = [-x for x in low_l]
        inv_l = [eye.astype(F32) + x for x in pw_l]
        plain = lambda a_l, b_l: [_dot(a, b) for a, b in zip(a_l, b_l)]
        for level in range(int(math.log2(c)) - 1):
            mul = _hp_l if level < DN_INV_EXACT_LEVELS else plain
            pw_l = mul(pw_l, pw_l)
            inv_l = [a + b for a, b in zip(inv_l, mul(inv_l, pw_l))]
    u_l = [_dot(a, b) for a, b in zip(inv_l, cut(v * beta))]
    w_l = [_dot(a, b) for a, b in zip(inv_l, cut(rhs_k))]
    aqk_l = [jnp.where(lower, a * d, 0.0) for a, d in zip(qk_l, dec_l)]
    g_last_l = [jnp.sum(jnp.where(rowid == c - 1, x, 0.0), axis=0, keepdims=True) for x in g_l]
    ekd_l = [jnp.exp(a - b) for a, b in zip(g_last_l, g_l)]
    kd_l = [a * b for a, b in zip(k_l, ekd_l)]
    qd_l = cut(qs * eg)
    kw_l = [_dot_tn(a, b) for a, b in zip(kd_l, w_l)]
    qp_l = [q - _dot(a, w) for q, a, w in zip(qd_l, aqk_l, w_l)]
    return dict(eye=eye, lower=lower, strict=strict, dec=dec_l, k=k_l, kb=kb_l, qs=qs_l, low=low_l, inv=inv_l,
                eg=cut(eg), rhs_k=cut(rhs_k), u=u_l, w=w_l, aqk=aqk_l, g_last=g_last_l, qd=qd_l,
                ekd=ekd_l, kd=kd_l, kw=kw_l, qp=qp_l)


def _dn_head_cols(gb_blk, head):
    lane = lax.broadcasted_iota(jnp.int32, (1, LANES), 1)
    g = jnp.sum(jnp.where(lane == head, gb_blk, 0.0), axis=-1, keepdims=True)
    beta = jnp.sum(jnp.where(lane == head + DN_HEADS, gb_blk, 0.0), axis=-1, keepdims=True)
    return g, beta


def _halves_over_ici(s_refs, o_refs, send_sems, recv_sems, first, last):
    x, y, c = _mesh_pos()
    me = 2 * x + y
    chips = _other_chips(x, y)
    pairs = [(a, k) for a in range(len(s_refs)) for k in range(3)]

    def copy(a, k, slot):
        px, py = chips[k]
        return pltpu.make_async_remote_copy(
            src_ref=s_refs[a].at[c], dst_ref=o_refs[a].at[slot, c], send_sem=send_sems.at[3 * a + k],
            recv_sem=recv_sems.at[3 * a + k], device_id=(px, py, c), device_id_type=MESH)

    @pl.when(first)
    def _():
        for a, k in pairs:
            copy(a, k, me).start()

    @pl.when(last)
    def _():
        for a, k in pairs:
            px, py = chips[k]
            copy(a, k, 2 * px + py).wait_recv()
        for a, k in pairs:
            copy(a, k, me).wait_send()


def _dn_delta_fwd(qkv, gb, p, o_gain, *, name, send=()):
    T = qkv.shape[0]
    tb = min(DN_TB, T)
    nb, nc = T // tb, tb // DN_CHUNK
    H = DN_HEADS
    qscale = DN_DK ** -0.5
    ns = len(send)
    hp = DN_HEADS_FWD

    def body(*refs):
        q_ref, k_ref, v_ref, gb_ref, gate_ref, gain_ref = refs[:6]
        o_ref, og_ref, st_ref, inv_ref = refs[6 + ns:10 + ns]
        s_ref = refs[10 + 2 * ns]
        pair, blk = pl.program_id(0), pl.program_id(1)
        if ns:
            _halves_over_ici(refs[6:6 + ns], refs[10 + ns:10 + 2 * ns], refs[11 + 2 * ns], refs[12 + 2 * ns],
                             jnp.logical_and(pair == 0, blk == 0),
                             jnp.logical_and(pair == H // hp - 1, blk == nb - 1))

        @pl.when(blk == 0)
        def _():
            s_ref[...] = jnp.zeros_like(s_ref)

        gbv = gb_ref[...]
        ts, ku, op = [], [], []
        for e in range(hp):
            qk_e, v_e = slice(e * DN_DK, (e + 1) * DN_DK), slice(e * DN_DV, (e + 1) * DN_DV)
            g, beta = _dn_head_cols(gbv, hp * pair + e)
            t = _dn_local(q_ref[:, qk_e] * qscale, k_ref[:, qk_e], v_ref[:, v_e], g, beta, nc)
            ts.append(t)
            for i in range(nc):
                inv_ref[e, i] = t["inv"][i]
            ku.append([_dot_tn(a, b) for a, b in zip(t["kd"], t["u"])])
            op.append([_dot(a, b) for a, b in zip(t["aqk"], t["u"])])
        s32 = [s_ref[e] for e in range(hp)]
        s_l = [[] for _ in range(hp)]
        for i in range(nc):
            sb = [_bf(x) for x in s32]
            for e in range(hp):
                st_ref[e, i] = sb[e]
                s_l[e].append(sb[e])
            prod = [_dot(ts[e]["kw"][i], sb[e]) for e in range(hp)]
            s32 = [s32[e] * jnp.exp(ts[e]["g_last"][i]) - prod[e] + ku[e][i] for e in range(hp)]
        for e in range(hp):
            s_ref[e] = s32[e]
        o = jnp.concatenate(
            [jnp.concatenate([_dot(qp, sb) + x for qp, sb, x in zip(ts[e]["qp"], s_l[e], op[e])], axis=0)
             for e in range(hp)], axis=1)
        o_ref[...] = o
        gain = gain_ref[...]
        for e in range(hp):
            v_e = slice(e * DN_DV, (e + 1) * DN_DV)
            oe = o[:, v_e]
            r = lax.rsqrt(jnp.mean(oe * oe, axis=-1, keepdims=True) + RMS_EPS)
            og_ref[:, v_e] = (((oe * r) * gain) * _silu(gate_ref[:, v_e])).astype(BF16)

    qk = lambda col0: pl.BlockSpec((tb, hp * DN_DK), lambda h, i: (i, col0 // (hp * DN_DK) + h))
    vblk = lambda col0: pl.BlockSpec((tb, hp * DN_DV), lambda h, i: (i, col0 // (hp * DN_DV) + h))
    return pl.pallas_call(
        body, name=name, grid=(H // hp, nb),
        in_specs=[qk(0), qk(DN_QK_W), vblk(2 * DN_QK_W), pl.BlockSpec((tb, LANES), lambda h, i: (i, 0)),
                  vblk(DN_CONV_W), pl.BlockSpec((1, DN_DV), lambda h, i: (0, 0))] + [HBM] * ns,
        out_specs=[vblk(0), vblk(0), pl.BlockSpec((hp, nc, DN_DK, DN_DV), lambda h, i: (h, i, 0, 0)),
                   pl.BlockSpec((hp, nc, DN_CHUNK, DN_CHUNK), lambda h, i: (h, i, 0, 0))] + [HBM] * ns,
        out_shape=[jax.ShapeDtypeStruct((T, DN_V_W), F32), jax.ShapeDtypeStruct((T, DN_V_W), BF16),
                   jax.ShapeDtypeStruct((H, T // DN_CHUNK, DN_DK, DN_DV), BF16),
                   jax.ShapeDtypeStruct((H, T // DN_CHUNK, DN_CHUNK, DN_CHUNK), F32)]
        + [jax.ShapeDtypeStruct((N_CHIPS,) + a.shape, a.dtype) for a in send],
        scratch_shapes=[pltpu.VMEM((hp, DN_DK, DN_DV), F32)]
        + ([pltpu.SemaphoreType.DMA((3 * ns,)), pltpu.SemaphoreType.DMA((3 * ns,))] if ns else []),
        compiler_params=pltpu.CompilerParams(dimension_semantics=("arbitrary", "arbitrary")),
    )(qkv, qkv, qkv, gb, p, o_gain, *send)


def _blocks_over_ici(p_refs, o_refs, send_sems, recv_sems, first, last):
    x, y, c = _mesh_pos()
    me = 2 * x + y
    chips = _other_chips(x, y)
    pairs = [(a, k) for a in range(len(p_refs)) for k in range(3)]

    def copy(a, k, slot):
        px, py = chips[k]
        return pltpu.make_async_remote_copy(
            src_ref=p_refs[a].at[2 * px + py], dst_ref=o_refs[a].at[slot], send_sem=send_sems.at[3 * a + k],
            recv_sem=recv_sems.at[3 * a + k], device_id=(px, py, c), device_id_type=MESH)

    @pl.when(first)
    def _():
        for a, k in pairs:
            copy(a, k, me).start()

    @pl.when(last)
    def _():
        for a, k in pairs:
            px, py = chips[k]
            copy(a, k, 2 * px + py).wait_recv()
        for a, k in pairs:
            copy(a, k, me).wait_send()


def _dn_delta_bwd(qkv, gb, p, o_gain, o, states, invs, dog, *, name, send=()):
    T = qkv.shape[0]
    tb = min(DN_TB, T)
    nb, nc = T // tb, tb // DN_CHUNK
    H = DN_HEADS
    qscale = DN_DK ** -0.5
    ns = len(send)
    hp = DN_HEADS_BWD

    def body(*refs):
        q_ref, k_ref, v_ref, gb_ref, gate_ref, gain_ref, o_ref, st_ref, inv_ref, dog_ref = refs[:10]
        dq_ref, dk_ref, dv_ref, dgate_ref, dgb_ref, dgain_ref = refs[10 + ns:16 + ns]
        ds_ref = refs[16 + 2 * ns]
        pair, blk = pl.program_id(0), pl.program_id(1)
        first = jnp.logical_and(pair == 0, blk == 0)
        if ns:
            _blocks_over_ici(refs[10:10 + ns], refs[16 + ns:16 + 2 * ns], refs[17 + 2 * ns], refs[18 + 2 * ns],
                             first, jnp.logical_and(pair == H // hp - 1, blk == nb - 1))

        @pl.when(blk == 0)
        def _():
            ds_ref[...] = jnp.zeros_like(ds_ref)

        @pl.when(first)
        def _():
            dgain_ref[...] = jnp.zeros_like(dgain_ref)

        lane = lax.broadcasted_iota(jnp.int32, (1, LANES), 1)
        c = DN_CHUNK
        cut = lambda x: [x[i * c:(i + 1) * c] for i in range(nc)]
        cat = lambda xs: jnp.concatenate(xs, axis=0)
        rsum = lambda x: jnp.sum(x, axis=-1, keepdims=True)
        gbv, gain = gb_ref[...], gain_ref[...]

        def before_chain(e):
            qk_e, v_e = slice(e * DN_DK, (e + 1) * DN_DK), slice(e * DN_DV, (e + 1) * DN_DV)
            g, beta = _dn_head_cols(gbv, hp * pair + e)
            ov, gate, dogv = o_ref[:, v_e], gate_ref[:, v_e], dog_ref[:, v_e]
            r = lax.rsqrt(jnp.mean(ov * ov, axis=-1, keepdims=True) + RMS_EPS)
            oh = ov * r
            dnrm = dogv * _silu(gate)
            dgate_ref[:, v_e] = dogv * (oh * gain) * _dsilu(gate)
            doh = dnrm * gain
            do_l = cut(r * (doh - oh * jnp.mean(doh * oh, axis=-1, keepdims=True)))
            dgain_ref[...] += jnp.sum(dnrm * oh, axis=0, keepdims=True)
            k, v = k_ref[:, qk_e], v_ref[:, v_e]
            t = _dn_local(q_ref[:, qk_e] * qscale, k, v, g, beta, nc, [inv_ref[e, i] for i in range(nc)])
            s_l = [st_ref[e, i] for i in range(nc)]
            vn_l = [u - _dot(w, sb) for u, w, sb in zip(t["u"], t["w"], s_l)]
            return dict(
                t=t, beta=beta, v=v, s=s_l, vn=vn_l, egl=[jnp.exp(x) for x in t["g_last"]],
                dqd=[_dot_nt(a, sb) for a, sb in zip(do_l, s_l)], daqk=[_dot_nt(a, b) for a, b in zip(do_l, vn_l)],
                aqk_do=[_dot_tn(a, b) for a, b in zip(t["aqk"], do_l)],
                qp_do=[_dot_tn(a, b) for a, b in zip(t["qp"], do_l)])

        hs = [before_chain(e) for e in range(hp)]
        ds = [ds_ref[e] for e in range(hp)]
        ds_l = [[None] * nc for _ in range(hp)]
        for i in reversed(range(nc)):
            for e in range(hp):
                ds_l[e][i] = ds[e]
            prod = [_dot_tn(hs[e]["t"]["kw"][i], ds[e]) for e in range(hp)]
            ds = [ds[e] * hs[e]["egl"][i] - prod[e] + hs[e]["qp_do"][i] for e in range(hp)]
        for e in range(hp):
            ds_ref[e] = ds[e]

        def after_chain(e):
            hd, t = hs[e], hs[e]["t"]
            lower, strict, eye = t["lower"], t["strict"], t["eye"]
            s_l, vn_l, dqd_l, daqk_l, egl_l, beta, v = (hd["s"], hd["vn"], hd["dqd"], hd["daqk"], hd["egl"],
                                                         hd["beta"], hd["v"])
            dvn_l = [a + _dot(kd, d) for a, kd, d in zip(hd["aqk_do"], t["kd"], ds_l[e])]
            dkd_l = [_dot_nt(a, d) for a, d in zip(vn_l, ds_l[e])]
            dgl_l = [jnp.sum(rsum(d * sb.astype(F32)), axis=0, keepdims=True) * x
                     for d, sb, x in zip(ds_l[e], s_l, egl_l)]
            dw_l = [-_dot_nt(a, sb) for a, sb in zip(dvn_l, s_l)]
            dbv_l = [_dot_tn(a, b) for a, b in zip(t["inv"], dvn_l)]
            dbk_l = [_dot_tn(a, b) for a, b in zip(t["inv"], dw_l)]
            dlow_l = [-(_dot_nt(a, b) + _dot_nt(x, y)) for a, b, x, y in zip(dbv_l, t["u"], dbk_l, t["w"])]
            m_l = [jnp.where(strict, a * d, 0.0) for a, d in zip(dlow_l, t["dec"])]
            nmat_l = [jnp.where(lower, a * d, 0.0) for a, d in zip(daqk_l, t["dec"])]
            dkb_l = [_dot(m, kk) + b * x for m, kk, b, x in zip(m_l, t["k"], dbk_l, t["eg"])]
            dqs_l = [_dot(n, kk) + a * x for n, kk, a, x in zip(nmat_l, t["k"], dqd_l, t["eg"])]
            dk1_l = [_dot_tn(m, kb) for m, kb in zip(m_l, t["kb"])]
            dk2_l = [_dot_tn(n, q) for n, q in zip(nmat_l, t["qs"])]
            beta_l, v_l = cut(beta), cut(v)
            rowid = lax.broadcasted_iota(jnp.int32, (c, 1), 0)
            dk_l, dg_l, dbeta_l = [], [], []
            for i in range(nc):
                dk_l.append(dk1_l[i] + dk2_l[i] + dkd_l[i] * t["ekd"][i] + dkb_l[i] * beta_l[i])
                gmat = jnp.where(strict, dlow_l[i] * t["low"][i], 0.0) + daqk_l[i] * t["aqk"][i]
                s_kd = rsum(dkd_l[i] * t["kd"][i])
                dg = (rsum(gmat) + rsum(dqd_l[i] * t["qd"][i]) - s_kd + rsum(dbk_l[i] * t["rhs_k"][i]))
                dg_row = -jnp.sum(gmat, axis=0, keepdims=True)
                dg = dg + rsum(jnp.where(eye, dg_row, 0.0))
                dgl = dgl_l[i] + jnp.sum(s_kd, axis=0, keepdims=True)
                dg_l.append(dg + jnp.where(rowid == c - 1, dgl, 0.0))
                dbeta_l.append(rsum(dbv_l[i] * v_l[i]) + rsum(dkb_l[i] * t["k"][i]))
            head = hp * pair + e
            dgb = (jnp.where(lane == head, cat(dg_l), 0.0) + jnp.where(lane == head + DN_HEADS, cat(dbeta_l), 0.0))
            return cat(dqs_l) * qscale, cat(dk_l), cat(dbv_l) * beta, dgb

        for e in range(hp):
            dq, dk, dv, dgb = after_chain(e)
            dq_ref[:, e * DN_DK:(e + 1) * DN_DK] = dq
            dk_ref[:, e * DN_DK:(e + 1) * DN_DK] = dk
            dv_ref[:, e * DN_DV:(e + 1) * DN_DV] = dv
            dgb_ref[e] = dgb

    rev = lambda i: nb - 1 - i
    qk = lambda col0: pl.BlockSpec((tb, hp * DN_DK), lambda h, i: (rev(i), col0 // (hp * DN_DK) + h))
    vblk = lambda col0: pl.BlockSpec((tb, hp * DN_DV), lambda h, i: (rev(i), col0 // (hp * DN_DV) + h))
    gain_spec = pl.BlockSpec((1, DN_DV), lambda h, i: (0, 0))
    return pl.pallas_call(
        body, name=name, grid=(H // hp, nb),
        in_specs=[qk(0), qk(DN_QK_W), vblk(2 * DN_QK_W), pl.BlockSpec((tb, LANES), lambda h, i: (rev(i), 0)),
                  vblk(DN_CONV_W), gain_spec, vblk(0),
                  pl.BlockSpec((hp, nc, DN_DK, DN_DV), lambda h, i: (h, rev(i), 0, 0)),
                  pl.BlockSpec((hp, nc, DN_CHUNK, DN_CHUNK), lambda h, i: (h, rev(i), 0, 0)), vblk(0)] + [HBM] * ns,
        out_specs=[qk(0), qk(0), vblk(0), vblk(DN_CONV_W),
                   pl.BlockSpec((hp, tb, LANES), lambda h, i: (h, rev(i), 0)), gain_spec] + [HBM] * ns,
        out_shape=[jax.ShapeDtypeStruct((T, DN_QK_W), F32), jax.ShapeDtypeStruct((T, DN_QK_W), F32),
                   jax.ShapeDtypeStruct((T, DN_V_W), F32), jax.ShapeDtypeStruct((T, DN_IN_PAD), F32),
                   jax.ShapeDtypeStruct((H, T, LANES), F32), jax.ShapeDtypeStruct((1, DN_DV), F32)]
        + [jax.ShapeDtypeStruct(a.shape, a.dtype) for a in send],
        scratch_shapes=[pltpu.VMEM((hp, DN_DK, DN_DV), F32)]
        + ([pltpu.SemaphoreType.DMA((3 * ns,)), pltpu.SemaphoreType.DMA((3 * ns,))] if ns else []),
        compiler_params=pltpu.CompilerParams(dimension_semantics=("arbitrary", "arbitrary")),
    )(qkv, qkv, qkv, gb, p, o_gain, o, states, invs, dog, *send)


def _dn_conv_bwd(p, conv_w, d, dp, *, first, normed, name):
    T, width = d.shape

    def body(p_ref, w_ref, d_ref, dp_in, dp_ref, dw_ref):
        del dp_in
        x = p_ref[...]
        ksz = w_ref.shape[0]
        xs = [_shift_down(x, ksz - 1 - i) for i in range(ksz)]
        xc = sum(w_ref[i:i + 1, :] * xs[i] for i in range(ksz))
        ds = d_ref[...]
        if normed:
            s = _silu(xc)
            r = lax.rsqrt(jnp.sum(s * s, axis=-1, keepdims=True) + L2_EPS)
            y = s * r
            ds = r * (ds - y * jnp.sum(ds * y, axis=-1, keepdims=True))
        dxc = ds * _dsilu(xc)
        dp_ref[...] = sum(w_ref[i:i + 1, :] * _shift_up(dxc, ksz - 1 - i) for i in range(ksz))
        for i in range(ksz):
            dw_ref[i:i + 1, :] = jnp.sum(dxc * xs[i], axis=0, keepdims=True)

    shifted = pl.BlockSpec((T, LANES), lambda j: (0, first + j))
    return pl.pallas_call(
        body, name=name, grid=(width // LANES,),
        in_specs=[shifted, pl.BlockSpec((DN_CONV, LANES), lambda j: (0, first + j)),
                  pl.BlockSpec((T, LANES), lambda j: (0, j)), pl.BlockSpec(memory_space=pl.ANY)],
        out_specs=[shifted, pl.BlockSpec((DN_CONV, LANES), lambda j: (0, j))],
        out_shape=[jax.ShapeDtypeStruct(dp.shape, F32), jax.ShapeDtypeStruct((DN_CONV, width), F32)],
        input_output_aliases={3: 0},
        compiler_params=pltpu.CompilerParams(dimension_semantics=("parallel",), vmem_limit_bytes=VMEM_BIG),
    )(p, conv_w, d, dp)


def _dn_ab_bwd(p, alog_row, dtb_row, dgb, dp, *, name):
    T = p.shape[0]
    rows = min(DN_TB, T)
    H = DN_HEADS

    def body(p_ref, al_ref, dt_ref, dgb_ref, dp_in, dp_ref, dal_ref, ddt_ref):
        del dp_in

        @pl.when(pl.program_id(0) == 0)
        def _():
            dal_ref[...] = jnp.zeros_like(dal_ref)
            ddt_ref[...] = jnp.zeros_like(ddt_ref)

        blk = p_ref[...]
        is_a, is_b, a_arg, neg_exp, log_a, beta = _dn_ab_parts(blk, al_ref[...], dt_ref[...])
        d = dgb_ref[0]
        for hh in range(1, H):
            d = d + dgb_ref[hh]
        hi, mid, lo_ = _split3(jnp.where(is_a, d, 0.0))
        tri = _dn_chunk_tri(rows, upper=True)
        f = lambda q: jnp.dot(tri, q, preferred_element_type=F32)
        dlog_a = f(hi) + f(mid) + f(lo_)
        da_in = dlog_a * neg_exp * _sigmoid(a_arg)
        db_in = jnp.where(is_b, d, 0.0) * beta * (1.0 - beta)
        dp_ref[...] = jnp.where(is_a, da_in, 0.0) + db_in
        dal_ref[...] += jnp.sum(dlog_a * log_a, axis=0, keepdims=True)
        ddt_ref[...] += jnp.sum(jnp.where(is_a, da_in, 0.0), axis=0, keepdims=True)

    blk = pl.BlockSpec((rows, LANES), lambda i: (i, DN_AB_COL))
    vec = pl.BlockSpec((1, LANES), lambda i: (0, 0))
    return pl.pallas_call(
        body, name=name, grid=(T // rows,),
        in_specs=[blk, vec, vec, pl.BlockSpec((H, rows, LANES), lambda i: (0, i, 0)),
                  pl.BlockSpec(memory_space=pl.ANY)],
        out_specs=[blk, vec, vec],
        out_shape=[jax.ShapeDtypeStruct(dp.shape, F32), jax.ShapeDtypeStruct((1, LANES), F32),
                   jax.ShapeDtypeStruct((1, LANES), F32)],
        input_output_aliases={4: 0},
        compiler_params=pltpu.CompilerParams(dimension_semantics=("arbitrary",)),
    )(p, alog_row, dtb_row, dgb, dp)


def _dn_layer_fwd(x, ng, w_in, conv_w, a_log, dt_bias, o_gain, w_out, tag, send=(), h=None, next_g=None):
    alog_row, dtb_row = _dn_lane_rows(a_log, dt_bias)
    gain = o_gain.reshape(1, DN_DV)
    if h is None:
        h = _rmsnorm_fwd(x, ng, name=f"{tag}_norm")
    p = _matmul(h, w_in, mode="nn", name=f"{tag}_inproj")
    qkv = _dn_prep_fwd(p, conv_w, name=f"{tag}_prep")
    gb = _dn_ab_fwd(p, alog_row, dtb_row, name=f"{tag}_ab")
    o, og, states, invs, *landed = _dn_delta_fwd(qkv, gb, p, gain, name=f"{tag}_delta", send=send)
    x_new = _matmul(og, w_out, mode="nn", res=x, name=f"{tag}_outproj", norm_g=next_g)
    return x_new, (h, p, qkv, gb, o, og, states, invs), landed


def _dn_layer_bwd(dx, x, ng, w_in, conv_w, a_log, dt_bias, o_gain, w_out, saved, tag, send=(), send_dwin=(),
                  chip_sums=None):
    h, p, qkv, gb, o, og, states, invs = saved
    alog_row, dtb_row = _dn_lane_rows(a_log, dt_bias)
    gain = o_gain.reshape(1, DN_DV)
    d_wout = _matmul(og, dx, mode="tn", out_dtype=BF16, name=f"{tag}_dwout")
    if chip_sums is not None:
        d_wout, = chip_sums([_cut2(_by_rows(d_wout))], f"{tag}wout")
        send = list(send) + [d_wout]
    dog = _matmul(dx, w_out, mode="nt", name=f"{tag}_dog")
    dq, dk, dv, dp, dgb, dgain, *landed = _dn_delta_bwd(qkv, gb, p, gain, o, states, invs, dog,
                                                        name=f"{tag}_deltabwd", send=send)
    n_qk = DN_QK_W // LANES
    dp, dconv_q = _dn_conv_bwd(p, conv_w, dq, dp, first=0, normed=True, name=f"{tag}_convbwd_q")
    dp, dconv_k = _dn_conv_bwd(p, conv_w, dk, dp, first=n_qk, normed=True, name=f"{tag}_convbwd_k")
    dp, dconv_v = _dn_conv_bwd(p, conv_w, dv, dp, first=2 * n_qk, normed=False, name=f"{tag}_convbwd_v")
    dconv = jnp.concatenate([dconv_q, dconv_k, dconv_v], axis=1)
    dp, dal, ddt = _dn_ab_bwd(p, alog_row, dtb_row, dgb, dp, name=f"{tag}_abbwd")
    d_win = _matmul(h, dp, mode="tn", out_dtype=BF16, name=f"{tag}_dwin", send=send_dwin)
    if send_dwin:
        d_win, *landed_dwin = d_win
        landed = landed + landed_dwin
    if chip_sums is not None:
        d_win, = chip_sums([_cut2(_by_cols(d_win))], f"{tag}win")
        dh, landed_win = _matmul(dp, w_in, mode="nt", name=f"{tag}_dh", send=[d_win])
        landed = landed + [landed_win]
    else:
        dh = _matmul(dp, w_in, mode="nt", name=f"{tag}_dh")
    dx_prev, dng = _rmsnorm_bwd(x, ng, dh, dx, name=f"{tag}_normbwd")
    return dx_prev, dng, d_win, dconv, dal[0, :DN_HEADS], ddt[0, :DN_HEADS], dgain[0], d_wout, landed


def _by_cols(dw):
    return _split(dw[:, :DN_IN], 1)


def _by_rows(dw):
    return dw.reshape(N_CHIPS, -1, dw.shape[-1])


def _cut2(g4):
    return g4.reshape(N_CHIPS, 2, -1, g4.shape[-1])


def _sb_gains(g):
    return jnp.concatenate([g, g]).reshape(1, LANES)


def _sb_layer_fwd(x, ng, w_in, gq, gk, w_out, tag, send=(), h=None, next_g=None):
    if h is None:
        h = _rmsnorm_fwd(x, ng, name=f"{tag}_norm")
    p3 = _matmul(h, w_in, mode="nn", b_parts=4, out_parts=4, name=f"{tag}_inproj")
    og, o, lsum, live, *landed = _sb_attn_fwd(p3, _sb_gains(gq), _sb_gains(gk), name=f"{tag}_attn", send=send)
    x_new = _matmul(og, w_out, mode="nn", res=x, name=f"{tag}_outproj", norm_g=next_g)
    return x_new, (h, p3, og, o, lsum, live), landed


def _sb_layer_bwd(dx, x, ng, w_in, gq, gk, w_out, saved, tag):
    h, p3, og, o, lsum, live = saved
    d_wout = _matmul(og, dx, mode="tn", out_dtype=BF16, name=f"{tag}_dwout")
    dog = _matmul(dx, w_out, mode="nt", name=f"{tag}_dog")
    dp3, dgq, dgk = _sb_attn_bwd(p3, _sb_gains(gq), _sb_gains(gk), o, lsum, live, dog, name=f"{tag}_attnbwd")
    fold = lambda d: jnp.sum(d.reshape(-1, SB_DH), axis=0)
    d_win = _matmul(h, dp3, mode="tn", b_parts=4, out_parts=4, out_dtype=BF16, name=f"{tag}_dwin")
    dh = _matmul(dp3, w_in, mode="nt", a_parts=4, b_parts=4, name=f"{tag}_dh")
    dx_prev, dng = _rmsnorm_bwd(x, ng, dh, dx, name=f"{tag}_normbwd")
    return dx_prev, dng, d_win, fold(dgq), fold(dgk), d_wout


N_CHIPS = 4
HBM = pl.BlockSpec(memory_space=pl.ANY)


def _mesh_pos():
    return lax.axis_index("x"), lax.axis_index("y"), lax.axis_index("c")


def _other_chips(x, y):
    return [(1 - x, y), (x, 1 - y), (1 - x, 1 - y)]


def _chip_exchange(srcs, *, send_slot_is_dest, copy_own, name):
    n = len(srcs)

    def body(*refs):
        src_refs, out_refs = refs[:n], refs[n:2 * n]
        send_sems, recv_sems, local_sems = refs[2 * n:]
        x, y, c = _mesh_pos()
        me = 2 * x + y
        chips = _other_chips(x, y)
        local = []
        for a in range(n):
            if not copy_own[a]:
                continue
            own = src_refs[a].at[me] if send_slot_is_dest else src_refs[a]
            local.append(pltpu.make_async_copy(own, out_refs[a].at[me], local_sems.at[a]))
        for cp in local:
            cp.start()

        def copy(a, k, landing_slot):
            px, py = chips[k]
            src = src_refs[a].at[2 * px + py] if send_slot_is_dest else src_refs[a]
            return pltpu.make_async_remote_copy(
                src_ref=src, dst_ref=out_refs[a].at[landing_slot],
                send_sem=send_sems.at[a * 3 + k], recv_sem=recv_sems.at[a * 3 + k],
                device_id=(px, py, c), device_id_type=MESH)

        sends = [copy(a, k, me) for a in range(n) for k in range(3)]
        for cp in sends:
            cp.start()
        for a in range(n):
            for k in range(3):
                px, py = chips[k]
                copy(a, k, 2 * px + py).wait_recv()
        for cp in sends:
            cp.wait_send()
        for cp in local:
            cp.wait()

    outs = []
    for s in srcs:
        shape = s.shape if send_slot_is_dest else (N_CHIPS,) + s.shape
        outs.append(jax.ShapeDtypeStruct(shape, s.dtype))
    return pl.pallas_call(
        body, name=name, in_specs=[HBM] * n, out_specs=[HBM] * n, out_shape=outs,
        scratch_shapes=[pltpu.SemaphoreType.DMA((3 * n,)), pltpu.SemaphoreType.DMA((3 * n,)),
                        pltpu.SemaphoreType.DMA((n,))],
    )(*srcs)


def _sibling_exchange(srcs, *, name):
    n = len(srcs)

    def body(*refs):
        src_refs, out_refs = refs[:n], refs[n:2 * n]
        send_sems, recv_sems = refs[2 * n:]
        x, y, c = _mesh_pos()
        copies = [pltpu.make_async_remote_copy(
            src_ref=src_refs[a], dst_ref=out_refs[a], send_sem=send_sems.at[a], recv_sem=recv_sems.at[a],
            device_id=(x, y, 1 - c), device_id_type=MESH) for a in range(n)]
        for cp in copies:
            cp.start()
        for cp in copies:
            cp.wait()

    return pl.pallas_call(
        body, name=name, in_specs=[HBM] * n, out_specs=[HBM] * n,
        out_shape=[jax.ShapeDtypeStruct(s.shape, s.dtype) for s in srcs],
        scratch_shapes=[pltpu.SemaphoreType.DMA((n,)), pltpu.SemaphoreType.DMA((n,))],
    )(*srcs)


def _gather_halves(shards, small, *, name):
    n = len(shards)

    def body(*refs):
        s_refs, small_ref = refs[:n], refs[n]
        o_refs, osmall_ref = refs[n + 1:2 * n + 1], refs[2 * n + 1]
        send_sems, recv_sems, local_sems = refs[2 * n + 2:]
        x, y, c = _mesh_pos()
        me = 2 * x + y
        chips = _other_chips(x, y)
        local = [pltpu.make_async_copy(small_ref, osmall_ref.at[me], local_sems.at[0])]
        for cp in local:
            cp.start()

        def over_ici(a, k, slot):
            px, py = chips[k]
            return pltpu.make_async_remote_copy(
                src_ref=s_refs[a].at[c], dst_ref=o_refs[a].at[slot, c], send_sem=send_sems.at[3 * a + k],
                recv_sem=recv_sems.at[3 * a + k], device_id=(px, py, c), device_id_type=MESH)

        def small_copy(k, slot):
            px, py = chips[k]
            return pltpu.make_async_remote_copy(
                src_ref=small_ref, dst_ref=osmall_ref.at[slot], send_sem=send_sems.at[3 * n + k],
                recv_sem=recv_sems.at[3 * n + k], device_id=(px, py, c), device_id_type=MESH)

        def to_sibling(a, k, half):
            px, py = chips[k]
            blk = o_refs[a].at[2 * px + py, half]
            return pltpu.make_async_remote_copy(
                src_ref=blk, dst_ref=blk, send_sem=send_sems.at[3 * n + 3 + 3 * a + k],
                recv_sem=recv_sems.at[3 * n + 3 + 3 * a + k], device_id=(x, y, 1 - c), device_id_type=MESH)

        sends = [over_ici(a, k, me) for a in range(n) for k in range(3)] + [small_copy(k, me) for k in range(3)]
        for cp in sends:
            cp.start()
        passed = []
        for a in range(n):
            for k in range(3):
                px, py = chips[k]
                over_ici(a, k, 2 * px + py).wait_recv()
                passed.append(to_sibling(a, k, c))
                passed[-1].start()
        for k in range(3):
            px, py = chips[k]
            small_copy(k, 2 * px + py).wait_recv()
        for a in range(n):
            for k in range(3):
                to_sibling(a, k, 1 - c).wait_recv()
        for cp in sends + passed:
            cp.wait_send()
        for cp in local:
            cp.wait()

    nsem = 6 * n + 3
    return pl.pallas_call(
        body, name=name, in_specs=[HBM] * (n + 1), out_specs=[HBM] * (n + 1),
        out_shape=[jax.ShapeDtypeStruct((N_CHIPS,) + s.shape, s.dtype) for s in shards + [small]],
        scratch_shapes=[pltpu.SemaphoreType.DMA((nsem,)), pltpu.SemaphoreType.DMA((nsem,)),
                        pltpu.SemaphoreType.DMA((1,))],
    )(*shards, small)


def _forward_halves(landed, *, name):
    n = len(landed)

    def body(*refs):
        o_refs = refs[n:2 * n]
        send_sems, recv_sems = refs[2 * n:]
        x, y, c = _mesh_pos()
        chips = _other_chips(x, y)
        pairs = [(a, k) for a in range(n) for k in range(3)]

        def copy(a, k, half):
            px, py = chips[k]
            blk = o_refs[a].at[2 * px + py, half]
            return pltpu.make_async_remote_copy(
                src_ref=blk, dst_ref=blk, send_sem=send_sems.at[3 * a + k], recv_sem=recv_sems.at[3 * a + k],
                device_id=(x, y, 1 - c), device_id_type=MESH)

        sends = [copy(a, k, c) for a, k in pairs]
        for cp in sends:
            cp.start()
        for a, k in pairs:
            copy(a, k, 1 - c).wait_recv()
        for cp in sends:
            cp.wait_send()

    return pl.pallas_call(
        body, name=name, in_specs=[HBM] * n, out_specs=[HBM] * n,
        out_shape=[jax.ShapeDtypeStruct(a.shape, a.dtype) for a in landed],
        input_output_aliases={a: a for a in range(n)},
        scratch_shapes=[pltpu.SemaphoreType.DMA((3 * n,)), pltpu.SemaphoreType.DMA((3 * n,))],
    )(*landed)


def _swap_other_half(g_list, *, name):
    n = len(g_list)

    def body(*refs):
        g_refs, o_refs = refs[:n], refs[n:2 * n]
        send_sems, recv_sems = refs[2 * n:]
        x, y, c = _mesh_pos()
        copies = [pltpu.make_async_remote_copy(
            src_ref=g_refs[a].at[:, 1 - c], dst_ref=o_refs[a], send_sem=send_sems.at[a], recv_sem=recv_sems.at[a],
            device_id=(x, y, 1 - c), device_id_type=MESH) for a in range(n)]
        for cp in copies:
            cp.start()
        for cp in copies:
            cp.wait()

    return pl.pallas_call(
        body, name=name, in_specs=[HBM] * n, out_specs=[HBM] * n,
        out_shape=[jax.ShapeDtypeStruct((g.shape[0],) + g.shape[2:], g.dtype) for g in g_list],
        scratch_shapes=[pltpu.SemaphoreType.DMA((n,)), pltpu.SemaphoreType.DMA((n,))],
    )(*g_list)


def _row_tile(r):
    return _pick(r, (512, 256, 128, 64, 32, 16, 8))


def _add_my_half(g4, sib4, core, *, name):
    n, _, r, C = g4.shape
    tr = _row_tile(r)

    def body(core_ref, g_ref, s_ref, o_ref):
        del core_ref
        o_ref[...] = (g_ref[...].astype(F32) + s_ref[...].astype(F32)).astype(o_ref.dtype)

    return pl.pallas_call(
        body, name=name,
        grid_spec=pltpu.PrefetchScalarGridSpec(
            num_scalar_prefetch=1, grid=(n, r // tr),
            in_specs=[pl.BlockSpec((None, None, tr, C), lambda j, i, core_ref: (j, core_ref[0], i, 0)),
                      pl.BlockSpec((None, tr, C), lambda j, i, core_ref: (j, i, 0))],
            out_specs=pl.BlockSpec((None, tr, C), lambda j, i, core_ref: (j, i, 0))),
        out_shape=jax.ShapeDtypeStruct((n, r, C), g4.dtype),
        compiler_params=pltpu.CompilerParams(dimension_semantics=("parallel", "parallel")),
    )(core, g4, sib4)


def _sum_chips(landed, part, me, *, name):
    _, r, C = landed.shape
    tr = _row_tile(r)

    def body(me_ref, own_ref, r1_ref, r2_ref, r3_ref, o_ref):
        del me_ref
        f = lambda ref: ref[...].astype(F32)
        o_ref[...] = ((f(own_ref) + f(r1_ref)) + f(r2_ref)) + f(r3_ref)

    slot = lambda d: pl.BlockSpec((None, tr, C), lambda i, me_ref: ((me_ref[0] + d) % N_CHIPS, i, 0))
    return pl.pallas_call(
        body, name=name,
        grid_spec=pltpu.PrefetchScalarGridSpec(
            num_scalar_prefetch=1, grid=(r // tr,), in_specs=[slot(0), slot(1), slot(2), slot(3)],
            out_specs=pl.BlockSpec((tr, C), lambda i, me_ref: (i, 0))),
        out_shape=jax.ShapeDtypeStruct((r, C), F32),
        compiler_params=pltpu.CompilerParams(dimension_semantics=("parallel",)),
    )(me, part, landed, landed, landed)


def _adamw_halves(w, mine, theirs, m, v, core, *, layer, prev, name):
    shape = w.shape
    r, C = mine.shape
    tr = _pick(r, (128, 64, 32, 16, 8))
    per = r // tr
    view = lambda a: a.reshape(-1, C)
    n_prev = 0 if prev is None else 4

    def body(*refs):
        core_ref, w_ref, gm_ref, gt_ref, m_ref, v_ref = refs[:6]
        g_ref, d_ref, nm_ref, nv_ref = refs[6 + n_prev:]
        gv = jnp.where(pl.program_id(0) == core_ref[0], gm_ref[...], gt_ref[...])
        g_ref[...] = gv
        d_ref[...], nm_ref[...], nv_ref[...] = _adamw_math(w_ref[...], gv, m_ref[...], v_ref[...])

    half = pl.BlockSpec((tr, C), lambda h, i, core_ref: ((2 * layer + h) * per + i, 0))
    row = pl.BlockSpec((tr, C), lambda h, i, core_ref: (i, 0))
    out = jax.ShapeDtypeStruct((math.prod(shape) // C, C), F32)
    res = pl.pallas_call(
        body, name=name,
        grid_spec=pltpu.PrefetchScalarGridSpec(
            num_scalar_prefetch=1, grid=(2, per), in_specs=[half, row, row, half, half] + [HBM] * n_prev,
            out_specs=[half] * 4),
        out_shape=[out] * 4,
        input_output_aliases={6 + j: j for j in range(n_prev)},
        compiler_params=pltpu.CompilerParams(dimension_semantics=("parallel", "parallel")),
    )(core, view(w), mine, theirs, view(m), view(v), *([] if prev is None else [view(a) for a in prev]))
    return tuple(a.reshape(shape) for a in res)


def _sum_small(recv4, *, name):
    _, R, C = recv4.shape

    def body(r_ref, o_ref):
        o_ref[...] = ((r_ref[0] + r_ref[1]) + r_ref[2]) + r_ref[3]

    return pl.pallas_call(body, name=name, out_shape=jax.ShapeDtypeStruct((R, C), F32))(recv4)


def _add(a, b, *, name):
    R, C = a.shape
    tr = _pick(R, (512, 256, 128, 64, 32, 16, 8))
    blk = pl.BlockSpec((tr, C), lambda i: (i, 0))

    def body(a_ref, b_ref, o_ref):
        o_ref[...] = a_ref[...] + b_ref[...]

    return pl.pallas_call(body, name=name, grid=(R // tr,), in_specs=[blk, blk], out_specs=blk,
                          out_shape=jax.ShapeDtypeStruct((R, C), F32),
                          compiler_params=pltpu.CompilerParams(dimension_semantics=("parallel",)))(a, b)


def _adamw_math(w, g, m, v):
    nm = ADAM_B1 * m + (1.0 - ADAM_B1) * g
    nv = ADAM_B2 * v + (1.0 - ADAM_B2) * (g * g)
    m_hat = nm / (1.0 - ADAM_B1 ** ADAM_STEP)
    v_hat = nv / (1.0 - ADAM_B2 ** ADAM_STEP)
    return -ADAM_LR * (m_hat / (jnp.sqrt(v_hat) + ADAM_EPS) + ADAM_WD * w), nm, nv


def _adamw(w, g, m, v, *, name):
    shape = w.shape
    C = shape[-1]
    R = w.size // C
    two = lambda a: a.reshape(R, C)
    tr = _pick(R, (256, 128, 64, 32, 16, 8)) if R % 8 == 0 and R > 8 else R
    blk = pl.BlockSpec((tr, C), lambda i: (i, 0))

    def body(w_ref, g_ref, m_ref, v_ref, d_ref, nm_ref, nv_ref):
        d_ref[...], nm_ref[...], nv_ref[...] = _adamw_math(w_ref[...], g_ref[...], m_ref[...], v_ref[...])

    out = jax.ShapeDtypeStruct((R, C), F32)
    d, nm, nv = pl.pallas_call(
        body, name=name, grid=(R // tr,), in_specs=[blk] * 4, out_specs=[blk] * 3, out_shape=[out] * 3,
        compiler_params=pltpu.CompilerParams(dimension_semantics=("parallel",)),
    )(two(w), two(g), two(m), two(v))
    return d.reshape(shape), nm.reshape(shape), nv.reshape(shape)


BIG = (("dn_w_in", (2, 1024, 1540), 2), ("dn_w_out", (2, 512, 1024), 1), ("sb_w_in", (1, 1024, 1024), 2),
       ("sb_w_out", (1, 256, 1024), 1), ("sc_w_in", (1, 1024, 2048), 2), ("sc_w_out", (1, 512, 1024), 1))
SMALL = (("dn_conv_w", (2, 4, 1024), 2), ("dn_o_norm_g", (2, 64), 1), ("sc_conv_w", (1, 3, 512), 2))
REPL = (("norm_g", (4, 1024)), ("dn_a_log", (2, 8)), ("dn_dt_bias", (2, 8)), ("sb_q_norm_g", (1, 64)),
        ("sb_k_norm_g", (1, 64)))


def _halves(shard):
    return shard.reshape(2, -1, shard.shape[-1])


def _pack(arrays, cols, lead=()):
    flat = jnp.concatenate([a.reshape(lead + (-1,)) for a in arrays], axis=-1)
    n = flat.shape[-1]
    rows = -(-n // cols)
    unit = 512 if rows > 512 else 8
    rows = -(-rows // unit) * unit
    flat = jnp.pad(flat, [(0, 0)] * len(lead) + [(0, rows * cols - n)])
    return flat.reshape(lead + (rows, cols))


def _unpack(buf, table, lead=()):
    flat = buf.reshape(lead + (-1,))
    out, off = {}, 0
    for entry in table:
        name, shape = entry[0], entry[1]
        n = math.prod(shape)
        out[name] = flat[..., off:off + n].reshape(lead + shape)
        off += n
    return out


def _join(shards, axis):
    return jnp.concatenate([shards[j] for j in range(N_CHIPS)], axis=axis)


def _split(full, axis):
    return jnp.stack(jnp.split(full, N_CHIPS, axis=axis), axis=0)


def kernel(x, norm_g, dn_w_in, dn_conv_w, dn_a_log, dn_dt_bias, dn_o_norm_g, dn_w_out, sb_w_in, sb_q_norm_g, sb_k_norm_g, sb_w_out, sc_w_in, sc_conv_w, sc_w_out, loss_target, m_norm_g, m_dn_w_in, m_dn_conv_w, m_dn_a_log, m_dn_dt_bias, m_dn_o_norm_g, m_dn_w_out, m_sb_w_in, m_sb_q_norm_g, m_sb_k_norm_g, m_sb_w_out, m_sc_w_in, m_sc_conv_w, m_sc_w_out, v_norm_g, v_dn_w_in, v_dn_conv_w, v_dn_a_log, v_dn_dt_bias, v_dn_o_norm_g, v_dn_w_out, v_sb_w_in, v_sb_q_norm_g, v_sb_k_norm_g, v_sb_w_out, v_sc_w_in, v_sc_conv_w, v_sc_w_out):
    weights = dict(norm_g=norm_g, dn_w_in=dn_w_in, dn_conv_w=dn_conv_w, dn_a_log=dn_a_log, dn_dt_bias=dn_dt_bias,
                   dn_o_norm_g=dn_o_norm_g, dn_w_out=dn_w_out, sb_w_in=sb_w_in, sb_q_norm_g=sb_q_norm_g,
                   sb_k_norm_g=sb_k_norm_g, sb_w_out=sb_w_out, sc_w_in=sc_w_in, sc_conv_w=sc_conv_w, sc_w_out=sc_w_out)
    m_in = dict(norm_g=m_norm_g, dn_w_in=m_dn_w_in, dn_conv_w=m_dn_conv_w, dn_a_log=m_dn_a_log,
                dn_dt_bias=m_dn_dt_bias, dn_o_norm_g=m_dn_o_norm_g, dn_w_out=m_dn_w_out, sb_w_in=m_sb_w_in,
                sb_q_norm_g=m_sb_q_norm_g, sb_k_norm_g=m_sb_k_norm_g, sb_w_out=m_sb_w_out, sc_w_in=m_sc_w_in,
                sc_conv_w=m_sc_conv_w, sc_w_out=m_sc_w_out)
    v_in = dict(norm_g=v_norm_g, dn_w_in=v_dn_w_in, dn_conv_w=v_dn_conv_w, dn_a_log=v_dn_a_log,
                dn_dt_bias=v_dn_dt_bias, dn_o_norm_g=v_dn_o_norm_g, dn_w_out=v_dn_w_out, sb_w_in=v_sb_w_in,
                sb_q_norm_g=v_sb_q_norm_g, sb_k_norm_g=v_sb_k_norm_g, sb_w_out=v_sb_w_out, sc_w_in=v_sc_w_in,
                sc_conv_w=v_sc_conv_w, sc_w_out=v_sc_w_out)
    order = list(weights)
    xi, yi, ci = _mesh_pos()

    small = _pack([weights[n] for n, _, _ in SMALL], LANES)
    later = [("dn_w_in", 1), ("dn_w_out", 1), ("sb_w_in", 0), ("sb_w_out", 0), ("sc_w_in", 0), ("sc_w_out", 0)]
    piece = lambda n, l: _halves(weights[n][l].astype(BF16)[None])
    own_first = [piece("dn_w_in", 0), piece("dn_w_out", 0)]
    own_later = [piece(n, l) for n, l in later]
    own_last, own_mid = own_later[:2], own_later[2:]
    me = 2 * xi + yi
    whole = lambda g4, own: lax.dynamic_update_index_in_dim(g4, own, me, 0)
    flat = lambda g4: g4.reshape(N_CHIPS, -1, g4.shape[-1])
    rows_of = lambda w4: w4.reshape(-1, w4.shape[-1])
    w_in0, w_out0, small4 = _gather_halves(own_first, small, name="gather_first")
    full = {n: _join(a, ax) for (n, _, ax), a in zip(SMALL, _unpack(small4, SMALL, (N_CHIPS,)).values())}

    def dn_in(g4, own):
        cols = [jnp.where(me == j, own, g4[j]).reshape(-1, g4.shape[-1]) for j in range(N_CHIPS)]
        cols.append(jnp.zeros((cols[0].shape[0], DN_IN_PAD - DN_IN), g4.dtype))
        return jnp.concatenate(cols, axis=1)

    def dn_args(j, w_in4, own_in, w_out4):
        return (dn_in(w_in4, own_in), full["dn_conv_w"][j], dn_a_log[j], dn_dt_bias[j], full["dn_o_norm_g"][j],
                rows_of(w_out4))

    x0 = x[0]
    dn0 = dn_args(0, w_in0, own_first[0], whole(w_out0, own_first[1]))
    (x1, h1), s0, landed = _dn_layer_fwd(x0, norm_g[0], *dn0, "l0", send=own_mid, next_g=norm_g[1])
    landed = _forward_halves(landed, name="forward_halves_mid")
    sb_in, sb_out, sc_in, sc_out = [whole(g4, own) for g4, own in zip(landed, own_mid)]
    sb_args = (flat(sb_in), sb_q_norm_g[0], sb_k_norm_g[0], rows_of(sb_out))
    sc_args = (flat(sc_in), full["sc_conv_w"][0], rows_of(sc_out))
    (x2, h2), s1, landed = _sb_layer_fwd(x1, norm_g[1], *sb_args, "l1", send=own_last, h=h1, next_g=norm_g[2])
    landed = _forward_halves(landed, name="forward_halves_last")
    dn1 = dn_args(1, landed[0], own_last[0], whole(landed[1], own_last[1]))
    (x3, h3), s2 = _sc_layer_fwd(x2, norm_g[2], *sc_args, "l2", h=h2, next_g=norm_g[3])
    x4, s3, _ = _dn_layer_fwd(x3, norm_g[3], *dn1, "l3", h=h3)
    dy, loss_local = _loss_head(x4, loss_target[0], name="loss_head")
    loss = lax.psum(loss_local[0, 0], ("x", "y", "c"))

    core = ci.astype(jnp.int32).reshape(1)
    chip = me.astype(jnp.int32).reshape(1)

    def chip_sums(g_list, tag):
        sib = _swap_other_half(g_list, name=f"swap_halves_{tag}")
        return [_add_my_half(g, s, core, name=f"sum_cores_{tag}{i}") for i, (g, s) in enumerate(zip(g_list, sib))]

    dx3, dng3, dwin3, dconv3, dal3, ddt3, dgain3, dwout3, _ = _dn_layer_bwd(dy, x3, norm_g[3], *dn1, s3, "l3")
    dx2, dng2, dwin2, dconv2, dwout2 = _sc_layer_bwd(dx3, x2, norm_g[2], *sc_args, s2, "l2")
    dx1, dng1, dwin1, dgq, dgk, dwout1 = _sb_layer_bwd(dx2, x1, norm_g[1], *sb_args, s1, "l1")
    part_later = chip_sums([_cut2(_by_cols(dwin3)), _cut2(_by_rows(dwout3)), _cut2(dwin1), _cut2(_by_rows(dwout1)),
                            _cut2(dwin2), _cut2(_by_rows(dwout2))], "later")
    dx0, dng0, part_win0, dconv0, dal0, ddt0, dgain0, part_wout0, landed = _dn_layer_bwd(
        dx1, x0, norm_g[0], *dn0, s0, "l0", send=part_later[2:], send_dwin=part_later[:2], chip_sums=chip_sums)
    pieces = later[2:] + [("dn_w_out", 0)] + later[:2] + [("dn_w_in", 0)]
    mine = {(n, l): _sum_chips(r, p, chip, name=f"sum_chips_{n}{l}")
            for (n, l), r, p in zip(pieces, landed, part_later[2:] + [part_wout0] + part_later[:2] + [part_win0])}
    pieces = sorted(pieces, key=lambda nl: nl[1])
    mine = [mine[nl] for nl in pieces]
    theirs = _sibling_exchange(mine, name="swap_results")
    upd = {}
    for (n, l), a, b in zip(pieces, mine, theirs):
        upd[n] = _adamw_halves(weights[n], a, b, m_in[n], v_in[n], core, layer=l, prev=upd.get(n),
                               name=f"adamw_{n}{l}")
    g_out = {n: upd[n][0] for n, _, _ in BIG}

    grads = dict(
        norm_g=jnp.concatenate([dng0, dng1, dng2, dng3], axis=0), dn_conv_w=jnp.stack([dconv0, dconv3]),
        dn_a_log=jnp.stack([dal0, dal3]), dn_dt_bias=jnp.stack([ddt0, ddt3]),
        dn_o_norm_g=jnp.stack([dgain0, dgain3]), sb_q_norm_g=dgq[None], sb_k_norm_g=dgk[None],
        sc_conv_w=dconv2[None])
    repl = [jnp.broadcast_to(grads[n][None], (N_CHIPS,) + s) for n, s in REPL]
    gsmall = _pack([_split(grads[n], ax) for n, _, ax in SMALL] + repl, LANES, (N_CHIPS,))
    rsmall, = _chip_exchange([gsmall], send_slot_is_dest=True, copy_own=(True,), name="scatter_small")
    psmall = _sum_small(rsmall, name="sum_chips_small")
    qsmall, = _sibling_exchange([psmall], name="swap_cores_small")
    tsmall = _add(psmall, qsmall, name="sum_cores_small")
    g_out.update(_unpack(tsmall, SMALL + REPL))

    for n in order:
        if n not in upd:
            upd[n] = (g_out[n],) + _adamw(weights[n], g_out[n], m_in[n], v_in[n], name=f"adamw_{n}")
    return (loss, dx0[None], *[upd[n][0] for n in order], *[upd[n][1] for n in order],
            *[upd[n][2] for n in order], *[upd[n][3] for n in order])
```
